```python
import jax, jax.numpy as jnp
from jax import lax
import numpy as np

D_MODEL = 1024
BATCH = 16
SEQ = 2048
DEPTH = 1

MEM_LEN = 256
MIX_WIDTH = D_MODEL
FOX_HEADS = 8
FOX_HEAD_DIM = (MIX_WIDTH // 2) // FOX_HEADS
FOX_WIDTH = FOX_HEADS * FOX_HEAD_DIM
GDN_HEADS = 4
GDN_HEAD_DIM = (MIX_WIDTH // 2) // GDN_HEADS
GDN_WIDTH = GDN_HEADS * GDN_HEAD_DIM
CONV_WIDTH = 4
GDN_CHUNK = 64
Q_BLOCK = 128
XATTN_HEADS = 4
XATTN_HEAD_DIM = 128
XATTN_WIDTH = XATTN_HEADS * XATTN_HEAD_DIM
D_FF = 4 * D_MODEL
EPS = 1e-6
NEG_INF = -1e30

IN_SIZES = [3 * FOX_WIDTH, FOX_HEADS, 3 * GDN_WIDTH, GDN_HEADS, GDN_HEADS, GDN_WIDTH]
IN_DIM = sum(IN_SIZES)
IN_OFFSETS = [int(o) for o in np.cumsum(IN_SIZES)[:-1]]

kernel_name = "hymba_fox_gdn_memxattn_block"


def rms_norm(x, g):
    xf = x.astype(jnp.float32)
    y = xf * lax.rsqrt(jnp.mean(xf * xf, axis=-1, keepdims=True) + EPS)
    return (y * g.astype(jnp.float32)).astype(x.dtype)


def l2_norm(x):
    xf = x.astype(jnp.float32)
    return (xf * lax.rsqrt(jnp.sum(xf * xf, axis=-1, keepdims=True) + EPS)).astype(x.dtype)


def causal_depthwise_conv(x, w):
    C = x.shape[-1]
    return lax.conv_general_dilated(
        x, w[:, None, :].astype(x.dtype), window_strides=(1,),
        padding=((CONV_WIDTH - 1, 0),),
        dimension_numbers=("NWC", "WIO", "NWC"),
        feature_group_count=C)


def forgetting_attention(q, k, v, f_logit):
    B, S, H, Dh = q.shape
    nb = S // Q_BLOCK
    scale = Dh ** -0.5
    c = jnp.cumsum(jax.nn.log_sigmoid(f_logit.astype(jnp.float32)), axis=1)
    c = c.transpose(0, 2, 1)
    q_blocks = q.reshape(B, nb, Q_BLOCK, H, Dh).transpose(1, 0, 3, 2, 4)
    c_blocks = c.reshape(B, H, nb, Q_BLOCK).transpose(2, 0, 1, 3)
    kpos = jnp.arange(S)

    def block(args):
        i, q_i, c_i = args
        qpos = i * Q_BLOCK + jnp.arange(Q_BLOCK)
        s = jnp.einsum("bhqd,bshd->bhqs", q_i, k,
                       preferred_element_type=jnp.float32) * scale
        s = s + c_i[..., :, None] - c[:, :, None, :]
        s = jnp.where(kpos[None, :] <= qpos[:, None], s, NEG_INF)
        p = jax.nn.softmax(s, axis=-1)
        return jnp.einsum("bhqs,bshd->bqhd", p.astype(v.dtype), v)

    out = lax.map(block, (jnp.arange(nb), q_blocks, c_blocks))
    return out.transpose(1, 0, 2, 3, 4).reshape(B, S, H, Dh)


def gated_delta_rule(q, k, v, g, beta):
    out_dtype = v.dtype
    B, S, H, Dk = q.shape
    Dv = v.shape[-1]
    C = GDN_CHUNK
    N = S // C
    f32 = jnp.float32

    def chunk(t):
        t = t.astype(f32).reshape((B, N, C, H) + t.shape[3:])
        return jnp.moveaxis(t, 3, 1)

    q = chunk(q) * (Dk ** -0.5)
    k = chunk(k)
    v = chunk(v)
    beta = chunk(beta)
    gc = jnp.cumsum(chunk(g), axis=-1)

    idx = jnp.arange(C)
    lower_incl = idx[:, None] >= idx[None, :]
    strict = idx[:, None] > idx[None, :]
    decay = jnp.exp(jnp.where(lower_incl, gc[..., :, None] - gc[..., None, :], NEG_INF))

    k_beta = k * beta[..., None]
    v_beta = v * beta[..., None]
    L = jnp.where(strict, jnp.einsum("bhncd,bhnsd->bhncs", k_beta, k) * decay, 0.0)
    eye = jnp.broadcast_to(jnp.eye(C, dtype=f32), L.shape)
    T = lax.linalg.triangular_solve(eye + L, eye, left_side=True, lower=True)
    u = jnp.einsum("bhncs,bhnse->bhnce", T, v_beta)
    w = jnp.einsum("bhncs,bhnsd->bhncd", T, k_beta * jnp.exp(gc)[..., None])
    intra = jnp.einsum("bhncd,bhnsd->bhncs", q, k) * decay

    def to_scan(t):
        return jnp.moveaxis(t, 2, 0)

    def step(state, xs):
        q_n, k_n, u_n, w_n, gc_n, intra_n = xs
        v_new = u_n - jnp.einsum("bhcd,bhde->bhce", w_n, state)
        o = (jnp.einsum("bhcd,bhde->bhce", q_n * jnp.exp(gc_n)[..., None], state)
             + jnp.einsum("bhcs,bhse->bhce", intra_n, v_new))
        g_last = gc_n[..., -1]
        state = (state * jnp.exp(g_last)[..., None, None]
                 + jnp.einsum("bhcd,bhce->bhde",
                              k_n * jnp.exp(g_last[..., None] - gc_n)[..., None], v_new))
        return state, o

    state0 = jnp.zeros((B, H, Dk, Dv), f32)
    _, o = lax.scan(step, state0, (to_scan(q), to_scan(k), to_scan(u), to_scan(w),
                                   to_scan(gc), to_scan(intra)))
    return o.transpose(1, 0, 3, 2, 4).reshape(B, S, H, Dv).astype(out_dtype)


def _fwd_setup_inputs(seed: int = 0) -> dict:
    key = jax.random.key(seed)
    ks = jax.random.split(key, 26)
    L = DEPTH
    f32 = jnp.float32

    def nrm(k, shape, scale):
        return jax.random.normal(k, shape, f32) * scale

    def gain(k, dim):
        return 1.0 + 0.02 * jax.random.normal(k, (L, dim), f32)

    dt = jnp.exp(jax.random.uniform(ks[10], (L, GDN_HEADS), f32,
                                    minval=np.log(0.001), maxval=np.log(0.1)))
    return {
        "x": nrm(ks[0], (BATCH, SEQ, D_MODEL), 1.0),
        "mem": nrm(ks[1], (BATCH, MEM_LEN, D_MODEL), 1.0),
        "norm_mix_g": gain(ks[2], D_MODEL),
        "w_in": nrm(ks[3], (L, D_MODEL, IN_DIM), D_MODEL ** -0.5),
        "fox_qnorm_g": gain(ks[4], FOX_HEAD_DIM),
        "fox_knorm_g": gain(ks[5], FOX_HEAD_DIM),
        "fox_f_bias": 2.0 + 0.5 * jax.random.normal(ks[6], (L, FOX_HEADS), f32),
        "fox_onorm_g": gain(ks[7], FOX_HEAD_DIM),
        "gdn_conv_w": nrm(ks[8], (L, CONV_WIDTH, 3 * GDN_WIDTH), CONV_WIDTH ** -0.5),
        "gdn_A_log": jnp.log(jax.random.uniform(ks[9], (L, GDN_HEADS), f32, minval=1.0, maxval=16.0)),
        "gdn_dt_bias": dt + jnp.log(-jnp.expm1(-dt)),
        "gdn_onorm_g": gain(ks[11], GDN_HEAD_DIM),
        "w_out": nrm(ks[12], (L, MIX_WIDTH, D_MODEL), MIX_WIDTH ** -0.5),
        "norm_xattn_g": gain(ks[13], D_MODEL),
        "mem_norm_g": gain(ks[14], D_MODEL),
        "w_cq": nrm(ks[15], (L, D_MODEL, XATTN_WIDTH), D_MODEL ** -0.5),
        "w_ckv": nrm(ks[16], (L, D_MODEL, 2 * XATTN_WIDTH), D_MODEL ** -0.5),
        "xattn_qnorm_g": gain(ks[17], XATTN_HEAD_DIM),
        "xattn_knorm_g": gain(ks[18], XATTN_HEAD_DIM),
        "w_co": nrm(ks[19], (L, XATTN_WIDTH, D_MODEL), XATTN_WIDTH ** -0.5),
        "norm_mlp_g": gain(ks[20], D_MODEL),
        "w_mlp1": nrm(ks[21], (L, D_MODEL, D_FF), D_MODEL ** -0.5),
        "w_mlp2": nrm(ks[22], (L, D_FF, D_MODEL), D_FF ** -0.5),
    }


def _fwd_reference(x, mem, norm_mix_g, w_in, fox_qnorm_g, fox_knorm_g, fox_f_bias, fox_onorm_g,
              gdn_conv_w, gdn_A_log, gdn_dt_bias, gdn_onorm_g, w_out,
              norm_xattn_g, mem_norm_g, w_cq, w_ckv, xattn_qnorm_g, xattn_knorm_g, w_co,
              norm_mlp_g, w_mlp1, w_mlp2):
    B, S, _ = x.shape
    M = mem.shape[1]
    for l in range(DEPTH):
        h = rms_norm(x, norm_mix_g[l])
        proj = h @ w_in[l]
        fox_qkv, fox_f, gdn_qkv, gdn_b, gdn_a, gdn_z = jnp.split(proj, IN_OFFSETS, axis=-1)

        fq, fk, fv = jnp.split(fox_qkv.reshape(B, S, 3 * FOX_HEADS, FOX_HEAD_DIM), 3, axis=2)
        fq = rms_norm(fq, fox_qnorm_g[l])
        fk = rms_norm(fk, fox_knorm_g[l])
        o_a = forgetting_attention(fq, fk, fv, fox_f + fox_f_bias[l])
        o_a = rms_norm(o_a, fox_onorm_g[l]).reshape(B, S, FOX_WIDTH)

        gqkv = jax.nn.silu(causal_depthwise_conv(gdn_qkv, gdn_conv_w[l]))
        gq, gk, gv = jnp.split(gqkv.reshape(B, S, 3 * GDN_HEADS, GDN_HEAD_DIM), 3, axis=2)
        gq = l2_norm(gq)
        gk = l2_norm(gk)
        beta = jax.nn.sigmoid(gdn_b.astype(jnp.float32))
        g = -jnp.exp(gdn_A_log[l].astype(jnp.float32)) * jax.nn.softplus(
            (gdn_a + gdn_dt_bias[l]).astype(jnp.float32))
        o_b = gated_delta_rule(gq, gk, gv, g, beta)
        o_b = rms_norm(o_b, gdn_onorm_g[l]) * jax.nn.silu(gdn_z.reshape(B, S, GDN_HEADS, GDN_HEAD_DIM))
        o_b = o_b.reshape(B, S, GDN_WIDTH)

        x = x + jnp.concatenate([o_a, o_b], axis=-1) @ w_out[l]

        hq = rms_norm(x, norm_xattn_g[l])
        hm = rms_norm(mem, mem_norm_g[l])
        cq = rms_norm((hq @ w_cq[l]).reshape(B, S, XATTN_HEADS, XATTN_HEAD_DIM), xattn_qnorm_g[l])
        ck, cv = jnp.split((hm @ w_ckv[l]).reshape(B, M, 2 * XATTN_HEADS, XATTN_HEAD_DIM), 2, axis=2)
        ck = rms_norm(ck, xattn_knorm_g[l])
        s = jnp.einsum("bqhd,bmhd->bhqm", cq, ck,
                       preferred_element_type=jnp.float32) * (XATTN_HEAD_DIM ** -0.5)
        p = jax.nn.softmax(s, axis=-1)
        co = jnp.einsum("bhqm,bmhd->bqhd", p.astype(cv.dtype), cv).reshape(B, S, XATTN_WIDTH)
        x = x + co @ w_co[l]

        hf = rms_norm(x, norm_mlp_g[l])
        x = x + jnp.square(jax.nn.relu(hf @ w_mlp1[l])) @ w_mlp2[l]
    return x


import jax as _jax
import jax.numpy as _jnp

TWIN_FORMAT = 'train_step'
FWD_PARAMS = ['x', 'mem', 'norm_mix_g', 'w_in', 'fox_qnorm_g', 'fox_knorm_g', 'fox_f_bias', 'fox_onorm_g', 'gdn_conv_w', 'gdn_A_log', 'gdn_dt_bias', 'gdn_onorm_g', 'w_out', 'norm_xattn_g', 'mem_norm_g', 'w_cq', 'w_ckv', 'xattn_qnorm_g', 'xattn_knorm_g', 'w_co', 'norm_mlp_g', 'w_mlp1', 'w_mlp2']
TWIN_WEIGHTS = ['norm_mix_g', 'w_in', 'fox_qnorm_g', 'fox_knorm_g', 'fox_f_bias', 'fox_onorm_g', 'gdn_conv_w', 'gdn_A_log', 'gdn_dt_bias', 'gdn_onorm_g', 'w_out', 'norm_xattn_g', 'mem_norm_g', 'w_cq', 'w_ckv', 'xattn_qnorm_g', 'xattn_knorm_g', 'w_co', 'norm_mlp_g', 'w_mlp1', 'w_mlp2']
TWIN_DIFF_INPUT = 'x'
TWIN_INPUTS = ['x', 'mem', 'norm_mix_g', 'w_in', 'fox_qnorm_g', 'fox_knorm_g', 'fox_f_bias', 'fox_onorm_g', 'gdn_conv_w', 'gdn_A_log', 'gdn_dt_bias', 'gdn_onorm_g', 'w_out', 'norm_xattn_g', 'mem_norm_g', 'w_cq', 'w_ckv', 'xattn_qnorm_g', 'xattn_knorm_g', 'w_co', 'norm_mlp_g', 'w_mlp1', 'w_mlp2', 'loss_target', 'm_norm_mix_g', 'm_w_in', 'm_fox_qnorm_g', 'm_fox_knorm_g', 'm_fox_f_bias', 'm_fox_onorm_g', 'm_gdn_conv_w', 'm_gdn_A_log', 'm_gdn_dt_bias', 'm_gdn_onorm_g', 'm_w_out', 'm_norm_xattn_g', 'm_mem_norm_g', 'm_w_cq', 'm_w_ckv', 'm_xattn_qnorm_g', 'm_xattn_knorm_g', 'm_w_co', 'm_norm_mlp_g', 'm_w_mlp1', 'm_w_mlp2', 'v_norm_mix_g', 'v_w_in', 'v_fox_qnorm_g', 'v_fox_knorm_g', 'v_fox_f_bias', 'v_fox_onorm_g', 'v_gdn_conv_w', 'v_gdn_A_log', 'v_gdn_dt_bias', 'v_gdn_onorm_g', 'v_w_out', 'v_norm_xattn_g', 'v_mem_norm_g', 'v_w_cq', 'v_w_ckv', 'v_xattn_qnorm_g', 'v_xattn_knorm_g', 'v_w_co', 'v_norm_mlp_g', 'v_w_mlp1', 'v_w_mlp2']
TWIN_OUTPUTS = ['loss', 'grad_x', 'grad_norm_mix_g', 'grad_w_in', 'grad_fox_qnorm_g', 'grad_fox_knorm_g', 'grad_fox_f_bias', 'grad_fox_onorm_g', 'grad_gdn_conv_w', 'grad_gdn_A_log', 'grad_gdn_dt_bias', 'grad_gdn_onorm_g', 'grad_w_out', 'grad_norm_xattn_g', 'grad_mem_norm_g', 'grad_w_cq', 'grad_w_ckv', 'grad_xattn_qnorm_g', 'grad_xattn_knorm_g', 'grad_w_co', 'grad_norm_mlp_g', 'grad_w_mlp1', 'grad_w_mlp2', 'delta_norm_mix_g', 'delta_w_in', 'delta_fox_qnorm_g', 'delta_fox_knorm_g', 'delta_fox_f_bias', 'delta_fox_onorm_g', 'delta_gdn_conv_w', 'delta_gdn_A_log', 'delta_gdn_dt_bias', 'delta_gdn_onorm_g', 'delta_w_out', 'delta_norm_xattn_g', 'delta_mem_norm_g', 'delta_w_cq', 'delta_w_ckv', 'delta_xattn_qnorm_g', 'delta_xattn_knorm_g', 'delta_w_co', 'delta_norm_mlp_g', 'delta_w_mlp1', 'delta_w_mlp2', 'new_m_norm_mix_g', 'new_m_w_in', 'new_m_fox_qnorm_g', 'new_m_fox_knorm_g', 'new_m_fox_f_bias', 'new_m_fox_onorm_g', 'new_m_gdn_conv_w', 'new_m_gdn_A_log', 'new_m_gdn_dt_bias', 'new_m_gdn_onorm_g', 'new_m_w_out', 'new_m_norm_xattn_g', 'new_m_mem_norm_g', 'new_m_w_cq', 'new_m_w_ckv', 'new_m_xattn_qnorm_g', 'new_m_xattn_knorm_g', 'new_m_w_co', 'new_m_norm_mlp_g', 'new_m_w_mlp1', 'new_m_w_mlp2', 'new_v_norm_mix_g', 'new_v_w_in', 'new_v_fox_qnorm_g', 'new_v_fox_knorm_g', 'new_v_fox_f_bias', 'new_v_fox_onorm_g', 'new_v_gdn_conv_w', 'new_v_gdn_A_log', 'new_v_gdn_dt_bias', 'new_v_gdn_onorm_g', 'new_v_w_out', 'new_v_norm_xattn_g', 'new_v_mem_norm_g', 'new_v_w_cq', 'new_v_w_ckv', 'new_v_xattn_qnorm_g', 'new_v_xattn_knorm_g', 'new_v_w_co', 'new_v_norm_mlp_g', 'new_v_w_mlp1', 'new_v_w_mlp2']
TWIN_LEAF_KINDS = {'loss': 'loss', 'grad_x': 'grad_x', 'grad_norm_mix_g': 'grad_w', 'grad_w_in': 'grad_w', 'grad_fox_qnorm_g': 'grad_w', 'grad_fox_knorm_g': 'grad_w', 'grad_fox_f_bias': 'grad_w', 'grad_fox_onorm_g': 'grad_w', 'grad_gdn_conv_w': 'grad_w', 'grad_gdn_A_log': 'grad_w', 'grad_gdn_dt_bias': 'grad_w', 'grad_gdn_onorm_g': 'grad_w', 'grad_w_out': 'grad_w', 'grad_norm_xattn_g': 'grad_w', 'grad_mem_norm_g': 'grad_w', 'grad_w_cq': 'grad_w', 'grad_w_ckv': 'grad_w', 'grad_xattn_qnorm_g': 'grad_w', 'grad_xattn_knorm_g': 'grad_w', 'grad_w_co': 'grad_w', 'grad_norm_mlp_g': 'grad_w', 'grad_w_mlp1': 'grad_w', 'grad_w_mlp2': 'grad_w', 'delta_norm_mix_g': 'delta_w', 'delta_w_in': 'delta_w', 'delta_fox_qnorm_g': 'delta_w', 'delta_fox_knorm_g': 'delta_w', 'delta_fox_f_bias': 'delta_w', 'delta_fox_onorm_g': 'delta_w', 'delta_gdn_conv_w': 'delta_w', 'delta_gdn_A_log': 'delta_w', 'delta_gdn_dt_bias': 'delta_w', 'delta_gdn_onorm_g': 'delta_w', 'delta_w_out': 'delta_w', 'delta_norm_xattn_g': 'delta_w', 'delta_mem_norm_g': 'delta_w', 'delta_w_cq': 'delta_w', 'delta_w_ckv': 'delta_w', 'delta_xattn_qnorm_g': 'delta_w', 'delta_xattn_knorm_g': 'delta_w', 'delta_w_co': 'delta_w', 'delta_norm_mlp_g': 'delta_w', 'delta_w_mlp1': 'delta_w', 'delta_w_mlp2': 'delta_w', 'new_m_norm_mix_g': 'new_m', 'new_m_w_in': 'new_m', 'new_m_fox_qnorm_g': 'new_m', 'new_m_fox_knorm_g': 'new_m', 'new_m_fox_f_bias': 'new_m', 'new_m_fox_onorm_g': 'new_m', 'new_m_gdn_conv_w': 'new_m', 'new_m_gdn_A_log': 'new_m', 'new_m_gdn_dt_bias': 'new_m', 'new_m_gdn_onorm_g': 'new_m', 'new_m_w_out': 'new_m', 'new_m_norm_xattn_g': 'new_m', 'new_m_mem_norm_g': 'new_m', 'new_m_w_cq': 'new_m', 'new_m_w_ckv': 'new_m', 'new_m_xattn_qnorm_g': 'new_m', 'new_m_xattn_knorm_g': 'new_m', 'new_m_w_co': 'new_m', 'new_m_norm_mlp_g': 'new_m', 'new_m_w_mlp1': 'new_m', 'new_m_w_mlp2': 'new_m', 'new_v_norm_mix_g': 'new_v', 'new_v_w_in': 'new_v', 'new_v_fox_qnorm_g': 'new_v', 'new_v_fox_knorm_g': 'new_v', 'new_v_fox_f_bias': 'new_v', 'new_v_fox_onorm_g': 'new_v', 'new_v_gdn_conv_w': 'new_v', 'new_v_gdn_A_log': 'new_v', 'new_v_gdn_dt_bias': 'new_v', 'new_v_gdn_onorm_g': 'new_v', 'new_v_w_out': 'new_v', 'new_v_norm_xattn_g': 'new_v', 'new_v_mem_norm_g': 'new_v', 'new_v_w_cq': 'new_v', 'new_v_w_ckv': 'new_v', 'new_v_xattn_qnorm_g': 'new_v', 'new_v_xattn_knorm_g': 'new_v', 'new_v_w_co': 'new_v', 'new_v_norm_mlp_g': 'new_v', 'new_v_w_mlp1': 'new_v', 'new_v_w_mlp2': 'new_v'}


def _forward(args):
    return _fwd_reference(*[args[k] for k in FWD_PARAMS])


def _output_shape():
    out = _jax.eval_shape(lambda: _forward(_fwd_setup_inputs(0)))
    return out.shape, out.dtype

N_MICROBATCH = 1
ADAM_LR = 0.001
ADAM_B1 = 0.9
ADAM_B2 = 0.999
ADAM_EPS = 1e-08
ADAM_WD = 0.01
ADAM_STEP = 10
PER_EXAMPLE_BATCH_AXIS = {'x': 0, 'mem': 0, 'loss_target': 0}
SHARED_INPUTS = []
_WEIGHT_DTYPES = {'norm_mix_g': _jnp.float32, 'w_in': _jnp.float32, 'fox_qnorm_g': _jnp.float32, 'fox_knorm_g': _jnp.float32, 'fox_f_bias': _jnp.float32, 'fox_onorm_g': _jnp.float32, 'gdn_conv_w': _jnp.float32, 'gdn_A_log': _jnp.float32, 'gdn_dt_bias': _jnp.float32, 'gdn_onorm_g': _jnp.float32, 'w_out': _jnp.float32, 'norm_xattn_g': _jnp.float32, 'mem_norm_g': _jnp.float32, 'w_cq': _jnp.float32, 'w_ckv': _jnp.float32, 'xattn_qnorm_g': _jnp.float32, 'xattn_knorm_g': _jnp.float32, 'w_co': _jnp.float32, 'norm_mlp_g': _jnp.float32, 'w_mlp1': _jnp.float32, 'w_mlp2': _jnp.float32}
MOMENT_SCALE = {'norm_mix_g': 5.611154e+00, 'w_in': 5.792432e-01, 'fox_qnorm_g': 1.832798e+00, 'fox_knorm_g': 1.866309e+00, 'fox_f_bias': 3.532881e+00, 'fox_onorm_g': 2.560421e+02, 'gdn_conv_w': 1.131931e+00, 'gdn_A_log': 3.048153e+01, 'gdn_dt_bias': 2.888040e+01, 'gdn_onorm_g': 4.919146e+01, 'w_out': 2.204495e+00, 'norm_xattn_g': 1.093774e-01, 'mem_norm_g': 7.874139e-01, 'w_cq': 1.541827e-01, 'w_ckv': 6.827957e-01, 'xattn_qnorm_g': 2.487988e+00, 'xattn_knorm_g': 2.472713e+00, 'w_co': 7.156660e-01, 'norm_mlp_g': 9.600799e+01, 'w_mlp1': 1.209041e+00, 'w_mlp2': 8.162674e+00}


def _to_microbatches(a, axis):
    t = _jnp.moveaxis(a, axis, 0)
    t = t.reshape((N_MICROBATCH, t.shape[0] // N_MICROBATCH) + t.shape[1:])
    return _jnp.moveaxis(t, 1, axis + 1)


def setup_inputs(seed: int = 0) -> dict:
    inp = _fwd_setup_inputs(seed)
    key = _jax.random.fold_in(_jax.random.key(seed), 7919)
    shape, _ = _output_shape()
    out = dict(inp)
    out["loss_target"] = _jax.random.normal(_jax.random.fold_in(key, 0), shape, _jnp.float32)
    for i, name in enumerate(TWIN_WEIGHTS):
        w = inp[name].astype(_jnp.float32)
        if MOMENT_SCALE is None:
            s = _jnp.sqrt(_jnp.mean(_jnp.square(w)) + 1e-30)
        else:
            s = MOMENT_SCALE[name]
        km, kv = _jax.random.split(_jax.random.fold_in(key, i + 1))
        out[name] = w
        out["m_" + name] = s * _jax.random.normal(km, w.shape, _jnp.float32)
        out["v_" + name] = (s * s) * _jax.random.uniform(kv, w.shape, _jnp.float32, 0.5, 1.5)
    if N_MICROBATCH > 1:
        for name, axis in PER_EXAMPLE_BATCH_AXIS.items():
            out[name] = _to_microbatches(out[name], axis)
    return {'x': out['x'], 'mem': out['mem'], 'norm_mix_g': out['norm_mix_g'], 'w_in': out['w_in'], 'fox_qnorm_g': out['fox_qnorm_g'], 'fox_knorm_g': out['fox_knorm_g'], 'fox_f_bias': out['fox_f_bias'], 'fox_onorm_g': out['fox_onorm_g'], 'gdn_conv_w': out['gdn_conv_w'], 'gdn_A_log': out['gdn_A_log'], 'gdn_dt_bias': out['gdn_dt_bias'], 'gdn_onorm_g': out['gdn_onorm_g'], 'w_out': out['w_out'], 'norm_xattn_g': out['norm_xattn_g'], 'mem_norm_g': out['mem_norm_g'], 'w_cq': out['w_cq'], 'w_ckv': out['w_ckv'], 'xattn_qnorm_g': out['xattn_qnorm_g'], 'xattn_knorm_g': out['xattn_knorm_g'], 'w_co': out['w_co'], 'norm_mlp_g': out['norm_mlp_g'], 'w_mlp1': out['w_mlp1'], 'w_mlp2': out['w_mlp2'], 'loss_target': out['loss_target'], 'm_norm_mix_g': out['m_norm_mix_g'], 'm_w_in': out['m_w_in'], 'm_fox_qnorm_g': out['m_fox_qnorm_g'], 'm_fox_knorm_g': out['m_fox_knorm_g'], 'm_fox_f_bias': out['m_fox_f_bias'], 'm_fox_onorm_g': out['m_fox_onorm_g'], 'm_gdn_conv_w': out['m_gdn_conv_w'], 'm_gdn_A_log': out['m_gdn_A_log'], 'm_gdn_dt_bias': out['m_gdn_dt_bias'], 'm_gdn_onorm_g': out['m_gdn_onorm_g'], 'm_w_out': out['m_w_out'], 'm_norm_xattn_g': out['m_norm_xattn_g'], 'm_mem_norm_g': out['m_mem_norm_g'], 'm_w_cq': out['m_w_cq'], 'm_w_ckv': out['m_w_ckv'], 'm_xattn_qnorm_g': out['m_xattn_qnorm_g'], 'm_xattn_knorm_g': out['m_xattn_knorm_g'], 'm_w_co': out['m_w_co'], 'm_norm_mlp_g': out['m_norm_mlp_g'], 'm_w_mlp1': out['m_w_mlp1'], 'm_w_mlp2': out['m_w_mlp2'], 'v_norm_mix_g': out['v_norm_mix_g'], 'v_w_in': out['v_w_in'], 'v_fox_qnorm_g': out['v_fox_qnorm_g'], 'v_fox_knorm_g': out['v_fox_knorm_g'], 'v_fox_f_bias': out['v_fox_f_bias'], 'v_fox_onorm_g': out['v_fox_onorm_g'], 'v_gdn_conv_w': out['v_gdn_conv_w'], 'v_gdn_A_log': out['v_gdn_A_log'], 'v_gdn_dt_bias': out['v_gdn_dt_bias'], 'v_gdn_onorm_g': out['v_gdn_onorm_g'], 'v_w_out': out['v_w_out'], 'v_norm_xattn_g': out['v_norm_xattn_g'], 'v_mem_norm_g': out['v_mem_norm_g'], 'v_w_cq': out['v_w_cq'], 'v_w_ckv': out['v_w_ckv'], 'v_xattn_qnorm_g': out['v_xattn_qnorm_g'], 'v_xattn_knorm_g': out['v_xattn_knorm_g'], 'v_w_co': out['v_w_co'], 'v_norm_mlp_g': out['v_norm_mlp_g'], 'v_w_mlp1': out['v_w_mlp1'], 'v_w_mlp2': out['v_w_mlp2']}


def _loss(weights, diff, rest, loss_target):
    with _jax.named_scope("forward"):
        args = {**rest, TWIN_DIFF_INPUT: diff, **{k: w.astype(_WEIGHT_DTYPES[k]) for k, w in weights.items()}}
        y = _forward(args)
    with _jax.named_scope("loss_head"):
        err = _jnp.square(y.astype(_jnp.float32) - loss_target)
        return 0.5 * _jnp.sum(_jnp.mean(err, axis=-1)) if err.ndim else 0.5 * err


def _adamw(w, g, m, v):
    m = ADAM_B1 * m + (1.0 - ADAM_B1) * g
    v = ADAM_B2 * v + (1.0 - ADAM_B2) * _jnp.square(g)
    m_hat = m / (1.0 - ADAM_B1 ** ADAM_STEP)
    v_hat = v / (1.0 - ADAM_B2 ** ADAM_STEP)
    delta = -ADAM_LR * (m_hat / (_jnp.sqrt(v_hat) + ADAM_EPS) + ADAM_WD * w)
    return delta, m, v


def reference(x, mem, norm_mix_g, w_in, fox_qnorm_g, fox_knorm_g, fox_f_bias, fox_onorm_g, gdn_conv_w, gdn_A_log, gdn_dt_bias, gdn_onorm_g, w_out, norm_xattn_g, mem_norm_g, w_cq, w_ckv, xattn_qnorm_g, xattn_knorm_g, w_co, norm_mlp_g, w_mlp1, w_mlp2, loss_target, m_norm_mix_g, m_w_in, m_fox_qnorm_g, m_fox_knorm_g, m_fox_f_bias, m_fox_onorm_g, m_gdn_conv_w, m_gdn_A_log, m_gdn_dt_bias, m_gdn_onorm_g, m_w_out, m_norm_xattn_g, m_mem_norm_g, m_w_cq, m_w_ckv, m_xattn_qnorm_g, m_xattn_knorm_g, m_w_co, m_norm_mlp_g, m_w_mlp1, m_w_mlp2, v_norm_mix_g, v_w_in, v_fox_qnorm_g, v_fox_knorm_g, v_fox_f_bias, v_fox_onorm_g, v_gdn_conv_w, v_gdn_A_log, v_gdn_dt_bias, v_gdn_onorm_g, v_w_out, v_norm_xattn_g, v_mem_norm_g, v_w_cq, v_w_ckv, v_xattn_qnorm_g, v_xattn_knorm_g, v_w_co, v_norm_mlp_g, v_w_mlp1, v_w_mlp2):
    given = dict(x=x, mem=mem, norm_mix_g=norm_mix_g, w_in=w_in, fox_qnorm_g=fox_qnorm_g, fox_knorm_g=fox_knorm_g, fox_f_bias=fox_f_bias, fox_onorm_g=fox_onorm_g, gdn_conv_w=gdn_conv_w, gdn_A_log=gdn_A_log, gdn_dt_bias=gdn_dt_bias, gdn_onorm_g=gdn_onorm_g, w_out=w_out, norm_xattn_g=norm_xattn_g, mem_norm_g=mem_norm_g, w_cq=w_cq, w_ckv=w_ckv, xattn_qnorm_g=xattn_qnorm_g, xattn_knorm_g=xattn_knorm_g, w_co=w_co, norm_mlp_g=norm_mlp_g, w_mlp1=w_mlp1, w_mlp2=w_mlp2, loss_target=loss_target, m_norm_mix_g=m_norm_mix_g, m_w_in=m_w_in, m_fox_qnorm_g=m_fox_qnorm_g, m_fox_knorm_g=m_fox_knorm_g, m_fox_f_bias=m_fox_f_bias, m_fox_onorm_g=m_fox_onorm_g, m_gdn_conv_w=m_gdn_conv_w, m_gdn_A_log=m_gdn_A_log, m_gdn_dt_bias=m_gdn_dt_bias, m_gdn_onorm_g=m_gdn_onorm_g, m_w_out=m_w_out, m_norm_xattn_g=m_norm_xattn_g, m_mem_norm_g=m_mem_norm_g, m_w_cq=m_w_cq, m_w_ckv=m_w_ckv, m_xattn_qnorm_g=m_xattn_qnorm_g, m_xattn_knorm_g=m_xattn_knorm_g, m_w_co=m_w_co, m_norm_mlp_g=m_norm_mlp_g, m_w_mlp1=m_w_mlp1, m_w_mlp2=m_w_mlp2, v_norm_mix_g=v_norm_mix_g, v_w_in=v_w_in, v_fox_qnorm_g=v_fox_qnorm_g, v_fox_knorm_g=v_fox_knorm_g, v_fox_f_bias=v_fox_f_bias, v_fox_onorm_g=v_fox_onorm_g, v_gdn_conv_w=v_gdn_conv_w, v_gdn_A_log=v_gdn_A_log, v_gdn_dt_bias=v_gdn_dt_bias, v_gdn_onorm_g=v_gdn_onorm_g, v_w_out=v_w_out, v_norm_xattn_g=v_norm_xattn_g, v_mem_norm_g=v_mem_norm_g, v_w_cq=v_w_cq, v_w_ckv=v_w_ckv, v_xattn_qnorm_g=v_xattn_qnorm_g, v_xattn_knorm_g=v_xattn_knorm_g, v_w_co=v_w_co, v_norm_mlp_g=v_norm_mlp_g, v_w_mlp1=v_w_mlp1, v_w_mlp2=v_w_mlp2)
    weights = {n: given[n] for n in TWIN_WEIGHTS}
    shared = {n: given[n] for n in SHARED_INPUTS}
    per_example = {n: given[n] for n in ['x', 'mem']}
    grad_fn = _jax.value_and_grad(_loss, argnums=(0, 1))

    def one_microbatch(ex, loss_target):
        ex = dict(ex)
        diff = ex.pop(TWIN_DIFF_INPUT)
        return grad_fn(weights, diff, {**shared, **ex}, loss_target)

    if N_MICROBATCH == 1:
        loss, (grad_w, grad_x) = one_microbatch(per_example, given["loss_target"])
    else:
        def body(carry, xs):
            loss_sum, grad_sum = carry
            l_k, (gw_k, gx_k) = one_microbatch(xs[0], xs[1])
            with _jax.named_scope("update"):
                return (loss_sum + l_k, _jax.tree.map(_jnp.add, grad_sum, gw_k)), gx_k

        init = (_jnp.zeros((), _jnp.float32), _jax.tree.map(_jnp.zeros_like, weights))
        (loss, grad_w), grad_x = _jax.lax.scan(body, init, (per_example, given["loss_target"]))
    with _jax.named_scope("update"):
        delta_w, new_m, new_v = {}, {}, {}
        for n in TWIN_WEIGHTS:
            delta_w[n], new_m[n], new_v[n] = _adamw(weights[n], grad_w[n], given["m_" + n], given["v_" + n])
    return (loss, grad_x, *[grad_w[n] for n in TWIN_WEIGHTS], *[delta_w[n] for n in TWIN_WEIGHTS],
            *[new_m[n] for n in TWIN_WEIGHTS], *[new_v[n] for n in TWIN_WEIGHTS])
```

```python
import functools

import jax
import jax.numpy as jnp
import numpy as np
from jax import lax
from jax.experimental import pallas as pl
from jax.experimental.pallas import tpu as pltpu

f32 = jnp.float32
bf16 = jnp.bfloat16
MXU_DTYPE = jnp.bfloat16
HI = lax.Precision.HIGHEST

D_MODEL = 1024
FOX_HEADS = 8
FOX_HEAD_DIM = 64
FOX_WIDTH = 512
GDN_HEADS = 4
GDN_HEAD_DIM = 128
GDN_WIDTH = 512
CONV_WIDTH = 4
GDN_CHUNK = 64
XATTN_HEADS = 4
XATTN_HEAD_DIM = 128
XATTN_WIDTH = 512
D_FF = 4096
IN_DIM = 3600
EPS = 1e-6
NEG_INF = -1e30
LANES = 128
ADAM_LR = 0.001
ADAM_B1 = 0.9
ADAM_B2 = 0.999
ADAM_EPS = 1e-08
ADAM_WD = 0.01
ADAM_STEP = 10
VMEM_LIMIT = 48 * 1024 * 1024

COL_FOX = 0
COL_GDN = 1536
COL_Z = 3072
COL_SMALL = 3584
IN_ALIGNED = 3840
IN_TILE = 768
SM_F = 0
SM_B = 8
SM_A = 12


def _cparams(*sem):
    return pltpu.CompilerParams(dimension_semantics=sem, vmem_limit_bytes=VMEM_LIMIT)


def _mx(v):
    return v.astype(MXU_DTYPE)


def _dot(a, b, dims, precision=None):
    return lax.dot_general(a, b, (dims, ((), ())), preferred_element_type=f32, precision=precision)


NN = ((1,), (0,))
NT = ((1,), (1,))
TN = ((0,), (0,))


def matmul(a, b, *, name, ta=False, tb=False, residual=None, relu2_out=False, relu2_bwd_aux=None,
           out_dtype=f32, tm=512, tn=512, tk=1024):
    M, K = (a.shape[1], a.shape[0]) if ta else a.shape
    N = b.shape[0] if tb else b.shape[1]
    tm, tn, tk = min(tm, M), min(tn, N), min(tk, K)
    assert M % tm == 0 and N % tn == 0 and K % tk == 0, (name, M, N, K)
    nk = K // tk
    has_res = residual is not None
    has_aux = relu2_bwd_aux is not None

    def body(*refs):
        a_ref, b_ref = refs[0], refs[1]
        pos = 2
        res_ref = aux_ref = None
        if has_res:
            res_ref = refs[pos]
            pos += 1
        if has_aux:
            aux_ref = refs[pos]
            pos += 1
        o_ref = refs[pos]
        pos += 1
        act_ref = None
        if relu2_out:
            act_ref = refs[pos]
            pos += 1
        acc_ref = refs[pos]
        k = pl.program_id(2)

        @pl.when(k == 0)
        def _():
            acc_ref[...] = jnp.zeros_like(acc_ref)

        dims = ((0,) if ta else (1,), (1,) if tb else (0,))
        acc_ref[...] += _dot(_mx(a_ref[...]), _mx(b_ref[...]), dims)

        @pl.when(k == nk - 1)
        def _():
            r = acc_ref[...]
            if has_res:
                r = r + res_ref[...]
            if has_aux:
                r = r * (2.0 * jnp.maximum(aux_ref[...], 0.0))
            o_ref[...] = r.astype(o_ref.dtype)
            if relu2_out:
                act_ref[...] = jnp.square(jnp.maximum(r, 0.0)).astype(act_ref.dtype)

    a_spec = pl.BlockSpec((tk, tm), lambda i, j, k: (k, i)) if ta else pl.BlockSpec((tm, tk), lambda i, j, k: (i, k))
    b_spec = pl.BlockSpec((tn, tk), lambda i, j, k: (j, k)) if tb else pl.BlockSpec((tk, tn), lambda i, j, k: (k, j))
    o_spec = pl.BlockSpec((tm, tn), lambda i, j, k: (i, j))
    in_specs, args = [a_spec, b_spec], [a, b]
    if has_res:
        in_specs.append(o_spec)
        args.append(residual)
    if has_aux:
        in_specs.append(o_spec)
        args.append(relu2_bwd_aux)
    out_shape = [jax.ShapeDtypeStruct((M, N), out_dtype)]
    out_specs = [o_spec]
    if relu2_out:
        out_shape.append(jax.ShapeDtypeStruct((M, N), MXU_DTYPE))
        out_specs.append(o_spec)
    res = pl.pallas_call(
        body, name=name, grid=(M // tm, N // tn, nk), in_specs=in_specs, out_specs=out_specs, out_shape=out_shape,
        scratch_shapes=[pltpu.VMEM((tm, tn), f32)],
        compiler_params=_cparams("parallel", "parallel", "arbitrary"),
    )(*args)
    return res if relu2_out else res[0]


def rms_fwd(x, g, *, name, tr=512):
    R, D = x.shape
    tr = min(tr, R)

    def body(x_ref, g_ref, o_ref):
        xv = x_ref[...]
        y = xv * lax.rsqrt(jnp.mean(xv * xv, axis=-1, keepdims=True) + EPS)
        o_ref[...] = (y * g_ref[...]).astype(o_ref.dtype)

    return pl.pallas_call(
        body, name=name, grid=(R // tr,),
        in_specs=[pl.BlockSpec((tr, D), lambda i: (i, 0)), pl.BlockSpec((1, D), lambda i: (0, 0))],
        out_specs=pl.BlockSpec((tr, D), lambda i: (i, 0)),
        out_shape=jax.ShapeDtypeStruct((R, D), MXU_DTYPE),
        compiler_params=_cparams("parallel"),
    )(x, g)


def rms_bwd(x, g, dh, residual, *, name, tr=512):
    R, D = x.shape
    tr = min(tr, R)
    has_res = residual is not None

    def body(*refs):
        if has_res:
            x_ref, g_ref, dh_ref, res_ref, dx_ref, dg_ref = refs
        else:
            x_ref, g_ref, dh_ref, dx_ref, dg_ref = refs
        xv = x_ref[...]
        rstd = lax.rsqrt(jnp.mean(xv * xv, axis=-1, keepdims=True) + EPS)
        xhat = xv * rstd
        dh = dh_ref[...].astype(f32)
        gd = dh * g_ref[...]
        dx = rstd * (gd - xhat * jnp.mean(gd * xhat, axis=-1, keepdims=True))
        if has_res:
            dx = dx + res_ref[...]
        dx_ref[...] = dx

        @pl.when(pl.program_id(0) == 0)
        def _():
            dg_ref[...] = jnp.zeros_like(dg_ref)

        dg_ref[...] += jnp.sum(dh * xhat, axis=0, keepdims=True)

    row = pl.BlockSpec((tr, D), lambda i: (i, 0))
    vec = pl.BlockSpec((1, D), lambda i: (0, 0))
    in_specs = [row, vec, row] + ([row] if has_res else [])
    args = [x, g, dh] + ([residual] if has_res else [])
    return pl.pallas_call(
        body, name=name, grid=(R // tr,), in_specs=in_specs, out_specs=[row, vec],
        out_shape=[jax.ShapeDtypeStruct((R, D), f32), jax.ShapeDtypeStruct((1, D), f32)],
        compiler_params=_cparams("arbitrary"),
    )(*args)


def loss_head(y, target, *, tr=512):
    R, D = y.shape
    tr = min(tr, R)

    def body(y_ref, t_ref, dy_ref, loss_ref):
        e = y_ref[...] - t_ref[...]
        dy_ref[...] = e * (1.0 / D)

        @pl.when(pl.program_id(0) == 0)
        def _():
            loss_ref[...] = jnp.zeros_like(loss_ref)

        part = 0.5 * jnp.sum(jnp.mean(e * e, axis=-1, keepdims=True), axis=0, keepdims=True)
        loss_ref[...] += jnp.broadcast_to(part, loss_ref.shape)

    row = pl.BlockSpec((tr, D), lambda i: (i, 0))
    return pl.pallas_call(
        body, name="loss_head", grid=(R // tr,), in_specs=[row, row],
        out_specs=[row, pl.BlockSpec((1, LANES), lambda i: (0, 0))],
        out_shape=[jax.ShapeDtypeStruct((R, D), f32), jax.ShapeDtypeStruct((1, LANES), f32)],
        compiler_params=_cparams("arbitrary"),
    )(y, target)


def _head_rms(v, g):
    r = lax.rsqrt(jnp.mean(v * v, axis=-1, keepdims=True) + EPS)
    return v * r * g, r


def _head_rms_bwd(v, r, g, dn):
    vhat = v * r
    gd = dn * g
    dv = r * (gd - vhat * jnp.mean(gd * vhat, axis=-1, keepdims=True))
    return dv, jnp.sum(dn * vhat, axis=0, keepdims=True)


def _softmax_rows(s):
    m = jnp.max(s, axis=-1, keepdims=True)
    e = jnp.exp(s - m)
    return e / jnp.sum(e, axis=-1, keepdims=True)


def xattn_fwd(cq, ckv, gq, gk, *, B, tq=512):
    T = cq.shape[0]
    S = T // B
    M = ckv.shape[0] // B
    tq = min(tq, S)
    nq = S // tq
    scale = XATTN_HEAD_DIM ** -0.5

    def body(q_ref, k_ref, v_ref, gq_ref, gk_ref, o_ref):
        qn, _ = _head_rms(q_ref[...], gq_ref[...])
        kn, _ = _head_rms(k_ref[...], gk_ref[...])
        p = _softmax_rows(_dot(_mx(qn), _mx(kn), NT) * scale)
        o_ref[...] = _dot(_mx(p), _mx(v_ref[...]), NN).astype(o_ref.dtype)

    hd = XATTN_HEAD_DIM
    vec = pl.BlockSpec((1, hd), lambda b, h, i: (0, 0))
    return pl.pallas_call(
        body, name="xattn_fwd", grid=(B, XATTN_HEADS, nq),
        in_specs=[pl.BlockSpec((tq, hd), lambda b, h, i: (b * nq + i, h)),
                  pl.BlockSpec((M, hd), lambda b, h, i: (b, h)),
                  pl.BlockSpec((M, hd), lambda b, h, i: (b, XATTN_HEADS + h)), vec, vec],
        out_specs=pl.BlockSpec((tq, hd), lambda b, h, i: (b * nq + i, h)),
        out_shape=jax.ShapeDtypeStruct((T, XATTN_WIDTH), MXU_DTYPE),
        compiler_params=_cparams("parallel", "parallel", "parallel"),
    )(cq, ckv, ckv, gq, gk)


def xattn_bwd(cq, ckv, gq, gk, dco, *, B, tq=512):
    T = cq.shape[0]
    S = T // B
    M = ckv.shape[0] // B
    tq = min(tq, S)
    nq = S // tq
    scale = XATTN_HEAD_DIM ** -0.5
    hd = XATTN_HEAD_DIM

    def body(q_ref, k_ref, v_ref, gq_ref, gk_ref, do_ref, dq_ref, dk_ref, dv_ref, dgq_ref, dgk_ref, dkn_acc, dv_acc):
        b, h, i = pl.program_id(0), pl.program_id(1), pl.program_id(2)

        @pl.when((b == 0) & (h == 0) & (i == 0))
        def _():
            dgq_ref[...] = jnp.zeros_like(dgq_ref)
            dgk_ref[...] = jnp.zeros_like(dgk_ref)

        @pl.when(i == 0)
        def _():
            dkn_acc[...] = jnp.zeros_like(dkn_acc)
            dv_acc[...] = jnp.zeros_like(dv_acc)

        q, k, v = q_ref[...], k_ref[...], v_ref[...]
        gqv, gkv = gq_ref[...], gk_ref[...]
        qn, rq = _head_rms(q, gqv)
        kn, rk = _head_rms(k, gkv)
        p = _softmax_rows(_dot(_mx(qn), _mx(kn), NT) * scale)
        do = do_ref[...]
        dv_acc[...] += _dot(_mx(p), _mx(do), TN)
        dp = _dot(_mx(do), _mx(v), NT)
        ds = p * (dp - jnp.sum(dp * p, axis=-1, keepdims=True)) * scale
        dqn = _dot(_mx(ds), _mx(kn), NN)
        dkn_acc[...] += _dot(_mx(ds), _mx(qn), TN)
        dq, dgq = _head_rms_bwd(q, rq, gqv, dqn)
        dq_ref[...] = dq.astype(dq_ref.dtype)
        dgq_ref[...] += dgq

        @pl.when(i == nq - 1)
        def _():
            dk, dgk = _head_rms_bwd(k, rk, gkv, dkn_acc[...])
            dk_ref[...] = dk.astype(dk_ref.dtype)
            dv_ref[...] = dv_acc[...].astype(dv_ref.dtype)
            dgk_ref[...] += dgk

    vec = pl.BlockSpec((1, hd), lambda b, h, i: (0, 0))
    qspec = pl.BlockSpec((tq, hd), lambda b, h, i: (b * nq + i, h))
    kspec = pl.BlockSpec((M, hd), lambda b, h, i: (b, h))
    vspec = pl.BlockSpec((M, hd), lambda b, h, i: (b, XATTN_HEADS + h))
    dq, dk, dv, dgq, dgk = pl.pallas_call(
        body, name="xattn_bwd", grid=(B, XATTN_HEADS, nq),
        in_specs=[qspec, kspec, vspec, vec, vec, qspec],
        out_specs=[qspec, kspec, kspec, vec, vec],
        out_shape=[jax.ShapeDtypeStruct((T, XATTN_WIDTH), MXU_DTYPE),
                   jax.ShapeDtypeStruct((B * M, XATTN_WIDTH), MXU_DTYPE),
                   jax.ShapeDtypeStruct((B * M, XATTN_WIDTH), MXU_DTYPE),
                   jax.ShapeDtypeStruct((1, hd), f32), jax.ShapeDtypeStruct((1, hd), f32)],
        scratch_shapes=[pltpu.VMEM((M, hd), f32), pltpu.VMEM((M, hd), f32)],
        compiler_params=_cparams("arbitrary", "arbitrary", "arbitrary"),
    )(cq, ckv, ckv, gq, gk, dco)
    return dq, jnp.concatenate([dk, dv], axis=1), dgq, dgk


FOX_PAIRS = FOX_HEADS // 2


def _fox_scores(qn, kn, ccol, crow, q0, tq, S, scale):
    s = _dot(_mx(qn), _mx(kn), NT) * scale + ccol - crow
    qpos = q0 + lax.broadcasted_iota(jnp.int32, (tq, S), 0)
    kpos = lax.broadcasted_iota(jnp.int32, (tq, S), 1)
    return jnp.where(kpos <= qpos, s, NEG_INF)


def fox_fwd(P, ccol, crow, gq, gk, go, *, B, tq=256):
    T = P.shape[0]
    S = T // B
    tq = min(tq, S)
    nq = S // tq
    hd = FOX_HEAD_DIM
    scale = hd ** -0.5

    def body(q_ref, k_ref, v_ref, ccol_ref, crow_ref, gq_ref, gk_ref, go_ref, o_ref, oa_ref):
        q0 = pl.program_id(2) * tq
        for e in range(2):
            sl = slice(e * hd, (e + 1) * hd)
            qn, _ = _head_rms(q_ref[:, sl], gq_ref[:, sl])
            kn, _ = _head_rms(k_ref[:, sl], gk_ref[:, sl])
            p = _softmax_rows(_fox_scores(qn, kn, ccol_ref[0, e], crow_ref[0, e], q0, tq, S, scale))
            o = _dot(_mx(p), _mx(v_ref[:, sl]), NN)
            o_ref[:, sl] = o
            oa_ref[:, sl] = _head_rms(o, go_ref[:, sl])[0].astype(oa_ref.dtype)

    W = 2 * hd
    vec = pl.BlockSpec((1, W), lambda b, h, i: (0, 0))
    ospec = pl.BlockSpec((tq, W), lambda b, h, i: (b * nq + i, h))
    return pl.pallas_call(
        body, name="fox_fwd", grid=(B, FOX_PAIRS, nq),
        in_specs=[pl.BlockSpec((tq, W), lambda b, h, i: (b * nq + i, h)),
                  pl.BlockSpec((S, W), lambda b, h, i: (b, FOX_PAIRS + h)),
                  pl.BlockSpec((S, W), lambda b, h, i: (b, 2 * FOX_PAIRS + h)),
                  pl.BlockSpec((1, 2, tq, 1), lambda b, h, i: (b, h, i, 0)),
                  pl.BlockSpec((1, 2, 1, S), lambda b, h, i: (b, h, 0, 0)), vec, vec, vec],
        out_specs=[ospec, ospec],
        out_shape=[jax.ShapeDtypeStruct((T, FOX_WIDTH), f32), jax.ShapeDtypeStruct((T, FOX_WIDTH), MXU_DTYPE)],
        compiler_params=_cparams("parallel", "parallel", "parallel"),
    )(P, P, P, ccol, crow, gq, gk, go)


def fox_bwd(P, ccol, crow, gq, gk, go, o_raw, d_oab, *, B, tq=256):
    T = P.shape[0]
    S = T // B
    tq = min(tq, S)
    nq = S // tq
    hd = FOX_HEAD_DIM
    scale = hd ** -0.5

    def body(q_ref, k_ref, v_ref, ccol_ref, crow_ref, gq_ref, gk_ref, go_ref, o_ref, doa_ref,
             dq_ref, dk_ref, dv_ref, dccol_ref, dcrow_ref, dgq_ref, dgk_ref, dgo_ref, dkn_acc, dv_acc, dcrow_acc):
        b, h, i = pl.program_id(0), pl.program_id(1), pl.program_id(2)
        q0 = i * tq

        @pl.when((b == 0) & (h == 0) & (i == 0))
        def _():
            dgq_ref[...] = jnp.zeros_like(dgq_ref)
            dgk_ref[...] = jnp.zeros_like(dgk_ref)
            dgo_ref[...] = jnp.zeros_like(dgo_ref)

        @pl.when(i == 0)
        def _():
            dkn_acc[...] = jnp.zeros_like(dkn_acc)
            dv_acc[...] = jnp.zeros_like(dv_acc)
            dcrow_acc[...] = jnp.zeros_like(dcrow_acc)

        for e in range(2):
            sl = slice(e * hd, (e + 1) * hd)
            q, k, v = q_ref[:, sl], k_ref[:, sl], v_ref[:, sl]
            gqv, gkv, gov = gq_ref[:, sl], gk_ref[:, sl], go_ref[:, sl]
            qn, rq = _head_rms(q, gqv)
            kn, rk = _head_rms(k, gkv)
            p = _softmax_rows(_fox_scores(qn, kn, ccol_ref[0, e], crow_ref[0, e], q0, tq, S, scale))
            o = o_ref[:, sl]
            ro = lax.rsqrt(jnp.mean(o * o, axis=-1, keepdims=True) + EPS)
            do, dgo = _head_rms_bwd(o, ro, gov, doa_ref[:, sl])
            dgo_ref[:, sl] += dgo
            dv_acc[e] += _dot(_mx(p), _mx(do), TN)
            dp = _dot(_mx(do), _mx(v), NT)
            ds = p * (dp - jnp.sum(do * o, axis=-1, keepdims=True))
            dccol_ref[0, e] = jnp.sum(ds, axis=1, keepdims=True)
            dcrow_acc[e] -= jnp.sum(ds, axis=0, keepdims=True)
            dqn = _dot(_mx(ds), _mx(kn), NN) * scale
            dkn_acc[e] += _dot(_mx(ds), _mx(qn), TN) * scale
            dq, dgq = _head_rms_bwd(q, rq, gqv, dqn)
            dq_ref[:, sl] = dq.astype(dq_ref.dtype)
            dgq_ref[:, sl] += dgq

        @pl.when(i == nq - 1)
        def _():
            for e in range(2):
                sl = slice(e * hd, (e + 1) * hd)
                k = k_ref[:, sl]
                gkv = gk_ref[:, sl]
                rk = lax.rsqrt(jnp.mean(k * k, axis=-1, keepdims=True) + EPS)
                dk, dgk = _head_rms_bwd(k, rk, gkv, dkn_acc[e])
                dk_ref[:, sl] = dk.astype(dk_ref.dtype)
                dv_ref[:, sl] = dv_acc[e].astype(dv_ref.dtype)
                dgk_ref[:, sl] += dgk
                dcrow_ref[0, e] = dcrow_acc[e]

    W = 2 * hd
    vec = pl.BlockSpec((1, W), lambda b, h, i: (0, 0))
    qspec = pl.BlockSpec((tq, W), lambda b, h, i: (b * nq + i, h))
    kvout = pl.BlockSpec((S, W), lambda b, h, i: (b, h))
    colspec = pl.BlockSpec((1, 2, tq, 1), lambda b, h, i: (b, h, i, 0))
    rowspec = pl.BlockSpec((1, 2, 1, S), lambda b, h, i: (b, h, 0, 0))
    return pl.pallas_call(
        body, name="fox_bwd", grid=(B, FOX_PAIRS, nq),
        in_specs=[qspec,
                  pl.BlockSpec((S, W), lambda b, h, i: (b, FOX_PAIRS + h)),
                  pl.BlockSpec((S, W), lambda b, h, i: (b, 2 * FOX_PAIRS + h)),
                  colspec, rowspec, vec, vec, vec, qspec, qspec],
        out_specs=[qspec, kvout, kvout, colspec, rowspec, vec, vec, vec],
        out_shape=[jax.ShapeDtypeStruct((T, FOX_WIDTH), MXU_DTYPE), jax.ShapeDtypeStruct((T, FOX_WIDTH), MXU_DTYPE),
                   jax.ShapeDtypeStruct((T, FOX_WIDTH), MXU_DTYPE),
                   jax.ShapeDtypeStruct((B, FOX_HEADS, S, 1), f32), jax.ShapeDtypeStruct((B, FOX_HEADS, 1, S), f32),
                   jax.ShapeDtypeStruct((1, W), f32), jax.ShapeDtypeStruct((1, W), f32), jax.ShapeDtypeStruct((1, W), f32)],
        scratch_shapes=[pltpu.VMEM((2, S, hd), f32), pltpu.VMEM((2, S, hd), f32), pltpu.VMEM((2, 1, S), f32)],
        compiler_params=_cparams("arbitrary", "arbitrary", "arbitrary"),
    )(P, P, P, ccol, crow, gq, gk, go, o_raw, d_oab)


def _lane_mask(lo, hi, shape):
    lane = lax.broadcasted_iota(jnp.int32, shape, 1)
    return (lane >= lo) & (lane < hi)


def _cumsum_rows(v, period, reverse=False):
    n = v.shape[0]
    pos = lax.broadcasted_iota(jnp.int32, v.shape, 0) % period
    sh = 1
    while sh < period:
        if reverse:
            v = v + jnp.where(pos + sh < period, pltpu.roll(v, n - sh, 0), 0.0)
        else:
            v = v + jnp.where(pos >= sh, pltpu.roll(v, sh, 0), 0.0)
        sh *= 2
    return v


def _gate_values(z, bias, alog):
    zb = z + bias
    ls = jax.nn.log_sigmoid(zb)
    beta = jax.nn.sigmoid(z)
    g = -jnp.exp(alog) * jax.nn.softplus(zb)
    return zb, ls, beta, g


def gates_fwd(P, bias, alog, *, B):
    T = P.shape[0]
    S = T // B

    def body(z_ref, bias_ref, alog_ref, o_ref):
        z = z_ref[...]
        _, ls, beta, g = _gate_values(z, bias_ref[...], alog_ref[...])
        c = _cumsum_rows(ls, S)
        gc = _cumsum_rows(g, GDN_CHUNK)
        o = jnp.where(_lane_mask(SM_F, SM_F + FOX_HEADS, z.shape), c, 0.0)
        o = jnp.where(_lane_mask(SM_B, SM_B + GDN_HEADS, z.shape), beta, o)
        o = jnp.where(_lane_mask(SM_A, SM_A + GDN_HEADS, z.shape), gc, o)
        o_ref[...] = o

    vec = pl.BlockSpec((1, LANES), lambda b: (0, 0))
    return pl.pallas_call(
        body, name="gates_fwd", grid=(B,),
        in_specs=[pl.BlockSpec((S, LANES), lambda b: (b, COL_SMALL // LANES)), vec, vec],
        out_specs=pl.BlockSpec((S, LANES), lambda b: (b, 0)),
        out_shape=jax.ShapeDtypeStruct((T, LANES), f32),
        compiler_params=_cparams("parallel"),
    )(P, bias, alog)


def gates_bwd(P, bias, alog, dgates, *, B):
    T = P.shape[0]
    S = T // B

    def body(z_ref, bias_ref, alog_ref, dg_ref, dz_ref, par_ref):
        z = z_ref[...]
        zb, ls, beta, g = _gate_values(z, bias_ref[...], alog_ref[...])
        d = dg_ref[...]
        dls = _cumsum_rows(d, S, reverse=True)
        dgr = _cumsum_rows(d, GDN_CHUNK, reverse=True)
        sig = jax.nn.sigmoid(zb)
        dz_f = dls * (1.0 - sig)
        dz_b = d * beta * (1.0 - beta)
        dz_a = dgr * (-jnp.exp(alog_ref[...])) * sig
        dz = jnp.where(_lane_mask(SM_F, SM_F + FOX_HEADS, z.shape), dz_f, 0.0)
        dz = jnp.where(_lane_mask(SM_B, SM_B + GDN_HEADS, z.shape), dz_b, dz)
        dz = jnp.where(_lane_mask(SM_A, SM_A + GDN_HEADS, z.shape), dz_a, dz)
        dz_ref[...] = dz.astype(dz_ref.dtype)

        @pl.when(pl.program_id(0) == 0)
        def _():
            par_ref[...] = jnp.zeros_like(par_ref)

        dalog = jnp.where(_lane_mask(SM_A, SM_A + GDN_HEADS, z.shape), dgr * g, 0.0)
        par_ref[0:1, :] += jnp.sum(dz, axis=0, keepdims=True)
        par_ref[1:2, :] += jnp.sum(dalog, axis=0, keepdims=True)

    vec = pl.BlockSpec((1, LANES), lambda b: (0, 0))
    return pl.pallas_call(
        body, name="gates_bwd", grid=(B,),
        in_specs=[pl.BlockSpec((S, LANES), lambda b: (b, COL_SMALL // LANES)), vec, vec,
                  pl.BlockSpec((S, LANES), lambda b: (b, 0))],
        out_specs=[pl.BlockSpec((S, LANES), lambda b: (b, 0)), pl.BlockSpec((8, LANES), lambda b: (0, 0))],
        out_shape=[jax.ShapeDtypeStruct((T, LANES), MXU_DTYPE), jax.ShapeDtypeStruct((8, LANES), f32)],
        compiler_params=_cparams("arbitrary"),
    )(P, bias, alog, dgates)


GDN_BLOCKS = 3 * GDN_HEADS


def _shift_rows(v, d, reverse=False):
    if d == 0:
        return v
    n = v.shape[0]
    row = lax.broadcasted_iota(jnp.int32, v.shape, 0)
    if reverse:
        return jnp.where(row + d < n, pltpu.roll(v, n - d, 0), 0.0)
    return jnp.where(row >= d, pltpu.roll(v, d, 0), 0.0)


def _conv_silu(x, w):
    pre = sum(w[j:j + 1, :] * _shift_rows(x, CONV_WIDTH - 1 - j) for j in range(CONV_WIDTH))
    return pre, pre * jax.nn.sigmoid(pre)


def gdn_prep_fwd(P, conv_w, *, B):
    T = P.shape[0]
    S = T // B

    def body(x_ref, w_ref, o_ref):
        _, y = _conv_silu(x_ref[...], w_ref[...])
        yn = y * lax.rsqrt(jnp.sum(y * y, axis=-1, keepdims=True) + EPS)
        o_ref[...] = jnp.where(pl.program_id(1) < 2 * GDN_HEADS, yn, y)

    return pl.pallas_call(
        body, name="gdn_prep_fwd", grid=(B, GDN_BLOCKS),
        in_specs=[pl.BlockSpec((S, LANES), lambda b, j: (b, COL_GDN // LANES + j)),
                  pl.BlockSpec((CONV_WIDTH, LANES), lambda b, j: (0, j))],
        out_specs=pl.BlockSpec((S, LANES), lambda b, j: (b, j)),
        out_shape=jax.ShapeDtypeStruct((T, 3 * GDN_WIDTH), f32),
        compiler_params=_cparams("parallel", "parallel"),
    )(P, conv_w)


def gdn_prep_bwd(P, conv_w, dG, *, B):
    T = P.shape[0]
    S = T // B

    def body(x_ref, w_ref, dg_ref, dx_ref, dw_ref):
        x, w = x_ref[...], w_ref[...]
        pre, y = _conv_silu(x, w)
        dn = dg_ref[...]
        r = lax.rsqrt(jnp.sum(y * y, axis=-1, keepdims=True) + EPS)
        n = y * r
        dy_norm = r * (dn - n * jnp.sum(dn * n, axis=-1, keepdims=True))
        dy = jnp.where(pl.program_id(0) < 2 * GDN_HEADS, dy_norm, dn)
        sg = jax.nn.sigmoid(pre)
        dpre = dy * (sg * (1.0 + pre * (1.0 - sg)))
        dx = sum(w[j:j + 1, :] * _shift_rows(dpre, CONV_WIDTH - 1 - j, reverse=True) for j in range(CONV_WIDTH))
        dx_ref[...] = dx.astype(dx_ref.dtype)

        @pl.when(pl.program_id(1) == 0)
        def _():
            dw_ref[...] = jnp.zeros_like(dw_ref)

        for j in range(CONV_WIDTH):
            dw_ref[j:j + 1, :] += jnp.sum(dpre * _shift_rows(x, CONV_WIDTH - 1 - j), axis=0, keepdims=True)

    return pl.pallas_call(
        body, name="gdn_prep_bwd", grid=(GDN_BLOCKS, B),
        in_specs=[pl.BlockSpec((S, LANES), lambda j, b: (b, COL_GDN // LANES + j)),
                  pl.BlockSpec((CONV_WIDTH, LANES), lambda j, b: (0, j)),
                  pl.BlockSpec((S, LANES), lambda j, b: (b, j))],
        out_specs=[pl.BlockSpec((S, LANES), lambda j, b: (b, j)),
                   pl.BlockSpec((CONV_WIDTH, LANES), lambda j, b: (0, j))],
        out_shape=[jax.ShapeDtypeStruct((T, 3 * GDN_WIDTH), MXU_DTYPE),
                   jax.ShapeDtypeStruct((CONV_WIDTH, 3 * GDN_WIDTH), f32)],
        compiler_params=_cparams("arbitrary", "arbitrary"),
    )(P, conv_w, dG)


GDN_GROUP = 4
B_NN = (((2,), (1,)), ((0,), (0,)))
B_NT = (((2,), (2,)), ((0,), (0,)))
B_TN = (((1,), (1,)), ((0,), (0,)))


def _bmm(a, b, dims):
    return lax.dot_general(a, b, dims, preferred_element_type=f32, precision=HI)


def _tri_inverse(A):
    C = A.shape[-1]
    row = lax.broadcasted_iota(jnp.int32, A.shape, 1)
    col = lax.broadcasted_iota(jnp.int32, A.shape, 2)
    eye = (row == col).astype(f32)
    X = jnp.where((row // 4) == (col // 4), -A, 0.0)
    X2 = _bmm(X, X, B_NN)
    Tm = eye + X + X2 + _bmm(X, X2, B_NN)
    b = 4
    while b < C:
        off = ((row // (2 * b)) == (col // (2 * b))) & ((row // b) != (col // b))
        Tm = Tm - _bmm(_bmm(Tm, jnp.where(off, A, 0.0), B_NN), Tm, B_NN)
        b *= 2
    return Tm


def _pick_lane(block, lane_idx):
    lane = lax.broadcasted_iota(jnp.int32, block.shape, 1)
    return jnp.sum(jnp.where(lane == lane_idx, block, 0.0), axis=1, keepdims=True)


def _gdn_local(q, k, v, beta, gc):
    C = GDN_CHUNK
    n = q.shape[0] // C
    q = q.reshape(n, C, -1) * (GDN_HEAD_DIM ** -0.5)
    k = k.reshape(n, C, -1)
    v = v.reshape(n, C, -1)
    beta = beta.reshape(n, C, 1)
    gc = gc.reshape(n, C, 1)
    row = lax.broadcasted_iota(jnp.int32, (n, C, C), 1)
    col = lax.broadcasted_iota(jnp.int32, (n, C, C), 2)
    gcT = jnp.swapaxes(jnp.broadcast_to(gc, (n, C, C)), 1, 2)
    D = jnp.exp(jnp.where(row >= col, gc - gcT, NEG_INF))
    kb = k * beta
    vb = v * beta
    A = jnp.where(row > col, _bmm(kb, k, B_NT) * D, 0.0)
    Tm = _tri_inverse(A)
    Gam = jnp.exp(gc)
    kg = kb * Gam
    gl = gc[:, C - 1:C, :]
    kdec = jnp.exp(gl - gc)
    return dict(q=q, k=k, v=v, beta=beta, gc=gc, D=D, kb=kb, vb=vb, A=A, Tm=Tm, Gam=Gam, kg=kg,
                u=_bmm(Tm, vb, B_NN), w=_bmm(Tm, kg, B_NN), M=_bmm(q, k, B_NT) * D,
                kdec=kdec, kd=k * kdec, qg=q * Gam, gam=jnp.exp(gl), row=row, col=col)


def _gdn_store_local(loc, r0, u_s, w_s, qg_s, kd_s, M_s, gam_s, c0):
    n = loc["u"].shape[0]
    R = n * GDN_CHUNK
    u_s[pl.ds(r0, R), :] = loc["u"].reshape(R, -1)
    w_s[pl.ds(r0, R), :] = loc["w"].reshape(R, -1)
    qg_s[pl.ds(r0, R), :] = loc["qg"].reshape(R, -1)
    kd_s[pl.ds(r0, R), :] = loc["kd"].reshape(R, -1)
    M_s[pl.ds(r0, R), :] = loc["M"].reshape(R, -1)
    gam_s[pl.ds(c0, n)] = jnp.broadcast_to(loc["gam"], (n, 1, LANES))


def _gdn_specs(S):
    blk = lambda off: pl.BlockSpec((S, LANES), lambda b, h: (b, off + h))
    return blk


def gdn_fwd(G, gates, P, g_on, *, B):
    T = G.shape[0]
    S = T // B
    C = GDN_CHUNK
    N = S // C
    R = GDN_GROUP * C
    hd = GDN_HEAD_DIM

    def body(q_ref, k_ref, v_ref, gt_ref, z_ref, gon_ref, o_ref, ob_ref, st_ref, u_s, w_s, qg_s, kd_s, M_s, gam_s):
        h = pl.program_id(1)

        def local(gi, carry):
            r0 = pl.multiple_of(gi * R, R)
            gt = gt_ref[pl.ds(r0, R), :]
            loc = _gdn_local(q_ref[pl.ds(r0, R), :], k_ref[pl.ds(r0, R), :], v_ref[pl.ds(r0, R), :],
                             _pick_lane(gt, SM_B + h), _pick_lane(gt, SM_A + h))
            _gdn_store_local(loc, r0, u_s, w_s, qg_s, kd_s, M_s, gam_s, gi * GDN_GROUP)
            return carry

        lax.fori_loop(0, N // GDN_GROUP, local, 0)

        def step(n, state):
            r0 = pl.multiple_of(n * C, C)
            st_ref[0, 0, n] = state
            v_new = u_s[pl.ds(r0, C), :] - _dot(w_s[pl.ds(r0, C), :], state, NN, HI)
            o_ref[pl.ds(r0, C), :] = (_dot(qg_s[pl.ds(r0, C), :], state, NN, HI)
                                      + _dot(M_s[pl.ds(r0, C), :], v_new, NN, HI))
            return state * gam_s[n] + _dot(kd_s[pl.ds(r0, C), :], v_new, TN, HI)

        lax.fori_loop(0, N, step, jnp.zeros((hd, hd), f32))
        o = o_ref[...]
        z = z_ref[...]
        ob_ref[...] = (_head_rms(o, gon_ref[...])[0] * (z * jax.nn.sigmoid(z))).astype(ob_ref.dtype)

    blk = lambda off: pl.BlockSpec((S, LANES), lambda b, h: (b, off + h))
    rows = lambda: pltpu.VMEM((S, hd), f32)
    return pl.pallas_call(
        body, name="gdn_fwd", grid=(B, GDN_HEADS),
        in_specs=[blk(0), blk(GDN_HEADS), blk(2 * GDN_HEADS), pl.BlockSpec((S, LANES), lambda b, h: (b, 0)),
                  blk(COL_Z // LANES), pl.BlockSpec((1, hd), lambda b, h: (0, 0))],
        out_specs=[blk(0), blk(0), pl.BlockSpec((1, 1, N, hd, hd), lambda b, h: (b, h, 0, 0, 0))],
        out_shape=[jax.ShapeDtypeStruct((T, GDN_WIDTH), f32), jax.ShapeDtypeStruct((T, GDN_WIDTH), MXU_DTYPE),
                   jax.ShapeDtypeStruct((B, GDN_HEADS, N, hd, hd), f32)],
        scratch_shapes=[rows(), rows(), rows(), rows(), pltpu.VMEM((S, C), f32), pltpu.VMEM((N, 1, LANES), f32)],
        compiler_params=_cparams("parallel", "parallel"),
    )(G, G, G, gates, P, g_on)


def gdn_bwd(G, gates, P, g_on, o_raw, states, d_oab, *, B):
    T = G.shape[0]
    S = T // B
    C = GDN_CHUNK
    N = S // C
    R = GDN_GROUP * C
    hd = GDN_HEAD_DIM

    def body(q_ref, k_ref, v_ref, gt_ref, z_ref, gon_ref, o_ref, st_ref, dob_ref,
             dq_ref, dk_ref, dv_ref, dgt_ref, dz_ref, dgon_ref,
             u_s, w_s, qg_s, kd_s, M_s, gam_s, do_s, du_s, dw_s, dqg_s, dkd_s, dM_s, dgl_s):
        b, h = pl.program_id(0), pl.program_id(1)

        @pl.when((b == 0) & (h == 0))
        def _():
            dgon_ref[...] = jnp.zeros_like(dgon_ref)

        @pl.when(h == 0)
        def _():
            dgt_ref[...] = jnp.zeros_like(dgt_ref)

        def group_inputs(gi):
            r0 = pl.multiple_of(gi * R, R)
            gt = gt_ref[pl.ds(r0, R), :]
            return r0, _gdn_local(q_ref[pl.ds(r0, R), :], k_ref[pl.ds(r0, R), :], v_ref[pl.ds(r0, R), :],
                                  _pick_lane(gt, SM_B + h), _pick_lane(gt, SM_A + h))

        def local(gi, carry):
            r0, loc = group_inputs(gi)
            _gdn_store_local(loc, r0, u_s, w_s, qg_s, kd_s, M_s, gam_s, gi * GDN_GROUP)
            o, z, gon = o_ref[pl.ds(r0, R), :], z_ref[pl.ds(r0, R), :], gon_ref[...]
            dob = dob_ref[pl.ds(r0, R), :]
            on, ro = _head_rms(o, gon)
            sz = jax.nn.sigmoid(z)
            dz_ref[pl.ds(r0, R), :] = (dob * on * (sz * (1.0 + z * (1.0 - sz)))).astype(dz_ref.dtype)
            do, dgon = _head_rms_bwd(o, ro, gon, dob * (z * sz))
            do_s[pl.ds(r0, R), :] = do
            dgon_ref[...] += dgon
            return carry

        lax.fori_loop(0, N // GDN_GROUP, local, 0)

        def step(t, dS):
            n = N - 1 - t
            r0 = pl.multiple_of(n * C, C)
            rows = pl.ds(r0, C)
            state = st_ref[0, 0, n]
            w_n, M_n, kd_n, do_n = w_s[rows, :], M_s[rows, :], kd_s[rows, :], do_s[rows, :]
            v_new = u_s[rows, :] - _dot(w_n, state, NN, HI)
            dv_new = _dot(M_n, do_n, TN, HI) + _dot(kd_n, dS, NN, HI)
            du_s[rows, :] = dv_new
            dw_s[rows, :] = -_dot(dv_new, state, NT, HI)
            dqg_s[rows, :] = _dot(do_n, state, NT, HI)
            dM_s[rows, :] = _dot(do_n, v_new, NT, HI)
            dkd_s[rows, :] = _dot(v_new, dS, NT, HI)
            gam = gam_s[n]
            dgl_s[n] = jnp.broadcast_to(jnp.sum(jnp.sum(dS * state, axis=1, keepdims=True), axis=0, keepdims=True), (1, LANES)) * gam
            return dS * gam + _dot(qg_s[rows, :], do_n, TN, HI) - _dot(w_n, dv_new, TN, HI)

        lax.fori_loop(0, N, step, jnp.zeros((hd, hd), f32))

        def finish(gi, carry):
            r0, L = group_inputs(gi)
            n = GDN_GROUP
            rows = pl.ds(r0, R)
            g3 = lambda ref: ref[rows, :].reshape(n, C, -1)
            du, dw, dqg, dkd, dM = g3(du_s), g3(dw_s), g3(dqg_s), g3(dkd_s), g3(dM_s)
            TmT = jnp.swapaxes(L["Tm"], 1, 2)
            dTm = _bmm(du, L["vb"], B_NT) + _bmm(dw, L["kg"], B_NT)
            dvb = _bmm(TmT, du, B_NN)
            dkg = _bmm(TmT, dw, B_NN)
            dA = jnp.where(L["row"] > L["col"], -_bmm(_bmm(TmT, dTm, B_NN), TmT, B_NN), 0.0)
            dKK = dA * L["D"]
            dQK = dM * L["D"]
            dkb = _bmm(dKK, L["k"], B_NN) + dkg * L["Gam"]
            dk = (_bmm(dKK, L["kb"], B_TN) + _bmm(dQK, L["q"], B_TN) + dkd * L["kdec"] + L["beta"] * dkb)
            dq = (_bmm(dQK, L["k"], B_NN) + dqg * L["Gam"]) * (GDN_HEAD_DIM ** -0.5)
            E = dA * L["A"] + dM * L["M"]
            r = jnp.sum(dkd * L["kd"], axis=-1, keepdims=True)
            dgc = (jnp.sum(E, axis=2, keepdims=True) - jnp.sum(jnp.swapaxes(E, 1, 2), axis=2, keepdims=True)
                   + jnp.sum(dkg * L["kg"], axis=-1, keepdims=True) + jnp.sum(dqg * L["qg"], axis=-1, keepdims=True) - r)
            dgl = jnp.sum(r, axis=1, keepdims=True) + dgl_s[pl.ds(gi * n, n)][:, :, 0:1]
            rowc = lax.broadcasted_iota(jnp.int32, (n, C, 1), 1)
            dgc = dgc + jnp.where(rowc == C - 1, dgl, 0.0)
            dbeta = jnp.sum(dkb * L["k"], axis=-1, keepdims=True) + jnp.sum(dvb * L["v"], axis=-1, keepdims=True)
            dq_ref[rows, :] = dq.reshape(R, hd)
            dk_ref[rows, :] = dk.reshape(R, hd)
            dv_ref[rows, :] = (L["beta"] * dvb).reshape(R, hd)
            lane = lax.broadcasted_iota(jnp.int32, (R, LANES), 1)
            dgt_ref[rows, :] += (jnp.where(lane == SM_B + h, dbeta.reshape(R, 1), 0.0)
                                 + jnp.where(lane == SM_A + h, dgc.reshape(R, 1), 0.0))
            return carry

        lax.fori_loop(0, N // GDN_GROUP, finish, 0)

    blk = lambda off: pl.BlockSpec((S, LANES), lambda b, h: (b, off + h))
    rows = lambda: pltpu.VMEM((S, hd), f32)
    return pl.pallas_call(
        body, name="gdn_bwd", grid=(B, GDN_HEADS),
        in_specs=[blk(0), blk(GDN_HEADS), blk(2 * GDN_HEADS), pl.BlockSpec((S, LANES), lambda b, h: (b, 0)),
                  blk(COL_Z // LANES), pl.BlockSpec((1, hd), lambda b, h: (0, 0)), blk(0),
                  pl.BlockSpec((1, 1, N, hd, hd), lambda b, h: (b, h, 0, 0, 0)), blk(GDN_HEADS)],
        out_specs=[blk(0), blk(0), blk(0), pl.BlockSpec((S, LANES), lambda b, h: (b, 0)), blk(0),
                   pl.BlockSpec((1, hd), lambda b, h: (0, 0))],
        out_shape=[jax.ShapeDtypeStruct((T, GDN_WIDTH), f32), jax.ShapeDtypeStruct((T, GDN_WIDTH), f32),
                   jax.ShapeDtypeStruct((T, GDN_WIDTH), f32), jax.ShapeDtypeStruct((T, LANES), f32),
                   jax.ShapeDtypeStruct((T, GDN_WIDTH), MXU_DTYPE), jax.ShapeDtypeStruct((1, hd), f32)],
        scratch_shapes=[rows(), rows(), rows(), rows(), pltpu.VMEM((S, C), f32), pltpu.VMEM((N, 1, LANES), f32),
                        rows(), rows(), rows(), rows(), rows(), pltpu.VMEM((S, C), f32), pltpu.VMEM((N, 1, LANES), f32)],
        compiler_params=_cparams("arbitrary", "arbitrary"),
    )(G, G, G, gates, P, g_on, o_raw, states, d_oab)


IN_SPLIT = (0, 1536, 1544, 3080, 3088, 3600)


def align_w_in(w):
    s = IN_SPLIT
    pad = jnp.zeros((w.shape[0], IN_ALIGNED - IN_DIM), w.dtype)
    return jnp.concatenate([w[:, s[0]:s[1]], w[:, s[2]:s[3]], w[:, s[4]:s[5]], w[:, s[1]:s[2]], w[:, s[3]:s[4]], pad], axis=1)


def unalign_w_in(wa):
    return jnp.concatenate([wa[:, 0:1536], wa[:, COL_SMALL:COL_SMALL + 8], wa[:, 1536:3072],
                            wa[:, COL_SMALL + 8:COL_SMALL + 16], wa[:, 3072:3584]], axis=1)


def _lanes_vec(pieces):
    v = jnp.zeros((1, LANES), f32)
    for off, a in pieces:
        v = lax.dynamic_update_slice(v, a.astype(f32), (0, off))
    return v


def local_step(x, mem, target, w, sp, *, B):
    T = x.shape[0]
    S = T // B
    tile2 = lambda g: jnp.concatenate([g, g], axis=1)
    gq2, gk2, go2 = tile2(sp["fox_qnorm_g"]), tile2(sp["fox_knorm_g"]), tile2(sp["fox_onorm_g"])
    bias = _lanes_vec([(SM_F, sp["fox_f_bias"]), (SM_A, sp["gdn_dt_bias"])])
    alog = _lanes_vec([(SM_A, sp["gdn_A_log"])])

    h1 = rms_fwd(x, sp["norm_mix_g"], name="rms_mix")
    P = matmul(h1, w["wa"], name="mm_in", tn=IN_TILE)
    gates = gates_fwd(P, bias, alog, B=B)
    c = gates[:, SM_F:SM_F + FOX_HEADS].reshape(B, S, FOX_HEADS).transpose(0, 2, 1)
    ccol, crow = c[..., None], c[:, :, None, :]
    o_raw, o_a = fox_fwd(P, ccol, crow, gq2, gk2, go2, B=B)
    G = gdn_prep_fwd(P, w["conv_w"], B=B)
    ob_raw, o_b, states = gdn_fwd(G, gates, P, sp["gdn_onorm_g"], B=B)
    oab = jnp.concatenate([o_a, o_b], axis=1)
    x2 = matmul(oab, w["w_out"], residual=x, name="mm_out")
    hq = rms_fwd(x2, sp["norm_xattn_g"], name="rms_xattn")
    hm = rms_fwd(mem, sp["mem_norm_g"], name="rms_mem")
    cq = matmul(hq, w["w_cq"], name="mm_cq")
    ckv = matmul(hm, w["w_ckv"], name="mm_ckv")
    co = xattn_fwd(cq, ckv, sp["xattn_qnorm_g"], sp["xattn_knorm_g"], B=B)
    x3 = matmul(co, w["w_co"], residual=x2, name="mm_co")
    hf = rms_fwd(x3, sp["norm_mlp_g"], name="rms_mlp")
    a, act = matmul(hf, w["w_mlp1"], relu2_out=True, name="mm_mlp1")
    x4 = matmul(act, w["w_mlp2"], residual=x3, name="mm_mlp2")
    dy, loss = loss_head(x4, target)

    da = matmul(dy, w["w_mlp2"], tb=True, relu2_bwd_aux=a, out_dtype=MXU_DTYPE, name="mm_d_act")
    g_mlp2 = matmul(act, dy, ta=True, name="mm_g_mlp2")
    g_mlp1 = matmul(hf, da, ta=True, name="mm_g_mlp1")
    dhf = matmul(da, w["w_mlp1"], tb=True, name="mm_d_hf")
    dx3, g_norm_mlp = rms_bwd(x3, sp["norm_mlp_g"], dhf, dy, name="rms_mlp_bwd")
    dco = matmul(dx3, w["w_co"], tb=True, name="mm_d_co")
    g_co = matmul(co, dx3, ta=True, name="mm_g_co")
    dcq, dckv, g_xq, g_xk = xattn_bwd(cq, ckv, sp["xattn_qnorm_g"], sp["xattn_knorm_g"], dco, B=B)
    g_cq = matmul(hq, dcq, ta=True, name="mm_g_cq")
    dhq = matmul(dcq, w["w_cq"], tb=True, name="mm_d_hq")
    g_ckv = matmul(hm, dckv, ta=True, name="mm_g_ckv")
    dhm = matmul(dckv, w["w_ckv"], tb=True, name="mm_d_hm")
    _, g_mem_norm = rms_bwd(mem, sp["mem_norm_g"], dhm, None, name="rms_mem_bwd")
    dx2, g_norm_xattn = rms_bwd(x2, sp["norm_xattn_g"], dhq, dx3, name="rms_xattn_bwd")
    doab = matmul(dx2, w["w_out"], tb=True, name="mm_d_oab")
    g_out = matmul(oab, dx2, ta=True, name="mm_g_out")
    dq_f, dk_f, dv_f, dccol, dcrow, dgq2, dgk2, dgo2 = fox_bwd(P, ccol, crow, gq2, gk2, go2, o_raw, doab, B=B)
    dGq, dGk, dGv, dgt, dz, g_gdn_on = gdn_bwd(G, gates, P, sp["gdn_onorm_g"], ob_raw, states, doab, B=B)
    dPg, g_conv = gdn_prep_bwd(P, w["conv_w"], jnp.concatenate([dGq, dGk, dGv], axis=1), B=B)
    dc = (dccol[..., 0] + dcrow[:, :, 0, :]).transpose(0, 2, 1).reshape(T, FOX_HEADS)
    dgates = dgt + jnp.pad(dc, ((0, 0), (SM_F, LANES - SM_F - FOX_HEADS)))
    dsmall, par = gates_bwd(P, bias, alog, dgates, B=B)
    dP = jnp.concatenate([dq_f, dk_f, dv_f, dPg, dz, dsmall, jnp.zeros((T, IN_ALIGNED - COL_SMALL - LANES), MXU_DTYPE)], axis=1)
    g_wa = matmul(h1, dP, ta=True, name="mm_g_in", tn=IN_TILE)
    dh1 = matmul(dP, w["wa"], tb=True, name="mm_d_h1", tk=IN_TILE)
    dx, g_norm_mix = rms_bwd(x, sp["norm_mix_g"], dh1, dx2, name="rms_mix_bwd")

    fold = lambda g: g[:, :FOX_HEAD_DIM] + g[:, FOX_HEAD_DIM:]
    big = dict(w_in=unalign_w_in(g_wa), w_out=g_out, w_cq=g_cq, w_ckv=g_ckv, w_co=g_co, w_mlp1=g_mlp1, w_mlp2=g_mlp2)
    small = dict(norm_mix_g=g_norm_mix, fox_qnorm_g=fold(dgq2), fox_knorm_g=fold(dgk2),
                 fox_f_bias=par[0:1, SM_F:SM_F + FOX_HEADS], fox_onorm_g=fold(dgo2), gdn_conv_w=g_conv,
                 gdn_A_log=par[1:2, SM_A:SM_A + GDN_HEADS], gdn_dt_bias=par[0:1, SM_A:SM_A + GDN_HEADS],
                 gdn_onorm_g=g_gdn_on, norm_xattn_g=g_norm_xattn, mem_norm_g=g_mem_norm,
                 xattn_qnorm_g=g_xq, xattn_knorm_g=g_xk, norm_mlp_g=g_norm_mlp)
    return loss, dx, big, small


MESH_IDS = pl.DeviceIdType.MESH
N_CHIPS = 4
HBM_SPEC = pl.BlockSpec(memory_space=pltpu.HBM)
PACK_ROWS = 30720
PACK_HALF = PACK_ROWS // 2
PACK_BLOCK = 3072


def _place():
    return lax.axis_index("x"), lax.axis_index("y"), lax.axis_index("c")


def _other_chips(x, y):
    return [(1 - x, y), (x, 1 - y), (1 - x, 1 - y)]


def _remote(src, dst, send_sem, recv_sem, to):
    return pltpu.make_async_remote_copy(src_ref=src, dst_ref=dst, send_sem=send_sem, recv_sem=recv_sem,
                                        device_id=to, device_id_type=MESH_IDS)


def all_gather_shards(packed):
    half = PACK_HALF

    def body(src_ref, out_ref, send_sems, recv_sems, local_sem):
        x, y, c = _place()
        me_chip = 2 * x + y
        sibling = (x, y, 1 - c)
        chips = _other_chips(x, y)
        mine = pltpu.make_async_copy(src_ref, out_ref.at[me_chip], local_sem)
        mine.start()

        def rows(chip, core):
            return out_ref.at[chip, pl.ds(core * half, half), :]

        sends = [_remote(src_ref.at[pl.ds(c * half, half), :], rows(me_chip, c), send_sems.at[j], recv_sems.at[j], (px, py, c))
                 for j, (px, py) in enumerate(chips)]
        for cp in sends:
            cp.start()
        passed = []
        for j, (px, py) in enumerate(chips):
            theirs = rows(2 * px + py, c)
            _remote(theirs, theirs, send_sems.at[j], recv_sems.at[j], (px, py, c)).wait_recv()
            cp = _remote(theirs, theirs, send_sems.at[3 + j], recv_sems.at[3 + j], sibling)
            cp.start()
            passed.append(cp)
        for j, (px, py) in enumerate(chips):
            theirs = rows(2 * px + py, 1 - c)
            _remote(theirs, theirs, send_sems.at[3 + j], recv_sems.at[3 + j], sibling).wait_recv()
        for cp in sends + passed:
            cp.wait_send()
        mine.wait()

    return pl.pallas_call(
        body, name="all_gather_shards", in_specs=[HBM_SPEC], out_specs=HBM_SPEC,
        out_shape=jax.ShapeDtypeStruct((N_CHIPS,) + packed.shape, packed.dtype),
        scratch_shapes=[pltpu.SemaphoreType.DMA((6,)), pltpu.SemaphoreType.DMA((6,)), pltpu.SemaphoreType.DMA(())],
    )(packed)


def exchange_core_halves(G):
    half = PACK_HALF

    def body(g_ref, land_ref, send_sem, recv_sem):
        x, y, c = _place()
        cp = _remote(g_ref.at[:, pl.ds((1 - c) * half, half), :], land_ref, send_sem, recv_sem, (x, y, 1 - c))
        cp.start()
        cp.wait()

    return pl.pallas_call(
        body, name="exchange_core_halves", in_specs=[HBM_SPEC], out_specs=HBM_SPEC,
        out_shape=jax.ShapeDtypeStruct((N_CHIPS, half, LANES), G.dtype),
        scratch_shapes=[pltpu.SemaphoreType.DMA(()), pltpu.SemaphoreType.DMA(())],
    )(G)


def add_core_halves(G, land, core):
    nb = PACK_HALF // PACK_BLOCK

    def body(c_ref, g_ref, l_ref, o_ref):
        o_ref[...] = g_ref[...] + l_ref[...]

    blk = (1, PACK_BLOCK, LANES)
    return pl.pallas_call(
        body, name="add_core_halves",
        grid_spec=pltpu.PrefetchScalarGridSpec(
            num_scalar_prefetch=1, grid=(N_CHIPS, nb),
            in_specs=[pl.BlockSpec(blk, lambda k, i, c_ref: (k, c_ref[0] * nb + i, 0)),
                      pl.BlockSpec(blk, lambda k, i, c_ref: (k, i, 0))],
            out_specs=pl.BlockSpec(blk, lambda k, i, c_ref: (k, i, 0))),
        out_shape=jax.ShapeDtypeStruct(land.shape, land.dtype),
        compiler_params=_cparams("parallel", "parallel"),
    )(core, G, land)


def scatter_to_chips(part):
    def body(p_ref, land_ref, send_sems, recv_sems, local_sem):
        x, y, c = _place()
        me_chip = 2 * x + y
        chips = _other_chips(x, y)
        mine = pltpu.make_async_copy(p_ref.at[me_chip], land_ref.at[me_chip], local_sem)
        mine.start()
        sends = [_remote(p_ref.at[2 * px + py], land_ref.at[me_chip], send_sems.at[j], recv_sems.at[j], (px, py, c))
                 for j, (px, py) in enumerate(chips)]
        for cp in sends:
            cp.start()
        for j, (px, py) in enumerate(chips):
            slot = land_ref.at[2 * px + py]
            _remote(slot, slot, send_sems.at[j], recv_sems.at[j], (px, py, c)).wait_recv()
        for cp in sends:
            cp.wait_send()
        mine.wait()

    return pl.pallas_call(
        body, name="scatter_to_chips", in_specs=[HBM_SPEC], out_specs=HBM_SPEC,
        out_shape=jax.ShapeDtypeStruct(part.shape, part.dtype),
        scratch_shapes=[pltpu.SemaphoreType.DMA((3,)), pltpu.SemaphoreType.DMA((3,)), pltpu.SemaphoreType.DMA(())],
    )(part)


def sum_chips(land):
    nb = PACK_HALF // PACK_BLOCK

    def body(l_ref, o_ref):
        o_ref[...] = ((l_ref[0] + l_ref[1]) + l_ref[2]) + l_ref[3]

    return pl.pallas_call(
        body, name="sum_chips", grid=(nb,),
        in_specs=[pl.BlockSpec((N_CHIPS, PACK_BLOCK, LANES), lambda i: (0, i, 0))],
        out_specs=pl.BlockSpec((PACK_BLOCK, LANES), lambda i: (i, 0)),
        out_shape=jax.ShapeDtypeStruct((PACK_HALF, LANES), land.dtype),
        compiler_params=_cparams("parallel"),
    )(land)


def join_core_halves(red):
    half = PACK_HALF

    def body(r_ref, out_ref, send_sem, recv_sem, local_sem):
        x, y, c = _place()
        mine = pltpu.make_async_copy(r_ref, out_ref.at[pl.ds(c * half, half), :], local_sem)
        mine.start()
        cp = _remote(r_ref, out_ref.at[pl.ds(c * half, half), :], send_sem, recv_sem, (x, y, 1 - c))
        cp.start()
        other = out_ref.at[pl.ds((1 - c) * half, half), :]
        _remote(other, other, send_sem, recv_sem, (x, y, 1 - c)).wait_recv()
        cp.wait_send()
        mine.wait()

    return pl.pallas_call(
        body, name="join_core_halves", in_specs=[HBM_SPEC], out_specs=HBM_SPEC,
        out_shape=jax.ShapeDtypeStruct((PACK_ROWS, LANES), red.dtype),
        scratch_shapes=[pltpu.SemaphoreType.DMA(()), pltpu.SemaphoreType.DMA(()), pltpu.SemaphoreType.DMA(())],
    )(red)


N_DEV = 8


def all_reduce_small(v):
    def body(src_ref, out_ref, land_ref, send_sems, recv_sems):
        x, y, c = _place()
        me = 4 * x + 2 * y + c
        copies = []
        for r in range(1, N_DEV):
            peer = ((1 - x) if r & 4 else x, (1 - y) if r & 2 else y, (1 - c) if r & 1 else c)
            copies.append(_remote(src_ref, land_ref.at[r], send_sems.at[r - 1], recv_sems.at[r - 1], peer))
        for cp in copies:
            cp.start()
        land_ref[0] = src_ref[...]
        for cp in copies:
            cp.wait()
        acc = land_ref[me]
        for d in range(1, N_DEV):
            acc = acc + land_ref[jnp.bitwise_xor(me, d)]
        out_ref[...] = acc

    vm = pl.BlockSpec(memory_space=pltpu.VMEM)
    return pl.pallas_call(
        body, name="all_reduce_small", in_specs=[vm], out_specs=vm,
        out_shape=jax.ShapeDtypeStruct(v.shape, v.dtype),
        scratch_shapes=[pltpu.VMEM((N_DEV,) + v.shape, v.dtype),
                        pltpu.SemaphoreType.DMA((N_DEV - 1,)), pltpu.SemaphoreType.DMA((N_DEV - 1,))],
    )(v)


def adamw(w, g, m, v, *, name, tr):
    R, C = w.shape
    tr = min(tr, R)

    def body(w_ref, g_ref, m_ref, v_ref, d_ref, nm_ref, nv_ref):
        gv = g_ref[...]
        nm = ADAM_B1 * m_ref[...] + (1.0 - ADAM_B1) * gv
        nv = ADAM_B2 * v_ref[...] + (1.0 - ADAM_B2) * jnp.square(gv)
        m_hat = nm / (1.0 - ADAM_B1 ** ADAM_STEP)
        v_hat = nv / (1.0 - ADAM_B2 ** ADAM_STEP)
        d_ref[...] = -ADAM_LR * (m_hat / (jnp.sqrt(v_hat) + ADAM_EPS) + ADAM_WD * w_ref[...])
        nm_ref[...] = nm
        nv_ref[...] = nv

    blk = pl.BlockSpec((tr, C), lambda i: (i, 0))
    out = jax.ShapeDtypeStruct((R, C), f32)
    return pl.pallas_call(
        body, name=name, grid=(R // tr,), in_specs=[blk] * 4, out_specs=[blk] * 3, out_shape=[out] * 3,
        compiler_params=_cparams("parallel"),
    )(w, g, m, v)


BIG_SHARDS = (("w_in", (1024, 900), True), ("w_out", (256, 1024), False), ("w_cq", (256, 512), False),
              ("w_ckv", (256, 1024), False), ("w_co", (512, 256), True), ("w_mlp1", (1024, 1024), True),
              ("w_mlp2", (1024, 1024), False))
CONV_SHARD = (CONV_WIDTH, 3 * GDN_WIDTH // N_CHIPS)
SMALL_DIMS = (("norm_mix_g", 1024), ("fox_qnorm_g", 64), ("fox_knorm_g", 64), ("fox_f_bias", 8), ("fox_onorm_g", 64),
              ("gdn_A_log", 4), ("gdn_dt_bias", 4), ("gdn_onorm_g", 128), ("norm_xattn_g", 1024), ("mem_norm_g", 1024),
              ("xattn_qnorm_g", 128), ("xattn_knorm_g", 128), ("norm_mlp_g", 1024))
WEIGHT_ORDER = ("norm_mix_g", "w_in", "fox_qnorm_g", "fox_knorm_g", "fox_f_bias", "fox_onorm_g", "gdn_conv_w", "gdn_A_log",
                "gdn_dt_bias", "gdn_onorm_g", "w_out", "norm_xattn_g", "mem_norm_g", "w_cq", "w_ckv", "xattn_qnorm_g",
                "xattn_knorm_g", "w_co", "norm_mlp_g", "w_mlp1", "w_mlp2")


def _pack_rows(pieces, rows, lead=()):
    flat = []
    for p in pieces:
        p = p.reshape(lead + (-1,))
        pad = (-p.shape[-1]) % LANES
        flat.append(jnp.pad(p, [(0, 0)] * len(lead) + [(0, pad)]) if pad else p)
    cat = jnp.concatenate(flat, axis=-1)
    cat = jnp.pad(cat, [(0, 0)] * len(lead) + [(0, rows * LANES - cat.shape[-1])])
    return cat.reshape(lead + (rows, LANES))


def _unpack_rows(buf, sizes, lead=()):
    flat = buf.reshape(lead + (-1,))
    out, off = [], 0
    for n in sizes:
        out.append(flat[..., off:off + n])
        off += n + (-n) % LANES
    return out


def _conv_to_wire(conv):
    return lax.bitcast_convert_type(conv, bf16)


def _conv_from_wire(wire):
    return lax.bitcast_convert_type(wire, f32)


SMALL_ROWS = 96
SMALL_ADAM_ROWS = 56


def kernel(x, mem, norm_mix_g, w_in, fox_qnorm_g, fox_knorm_g, fox_f_bias, fox_onorm_g, gdn_conv_w, gdn_A_log, gdn_dt_bias, gdn_onorm_g, w_out, norm_xattn_g, mem_norm_g, w_cq, w_ckv, xattn_qnorm_g, xattn_knorm_g, w_co, norm_mlp_g, w_mlp1, w_mlp2, loss_target, m_norm_mix_g, m_w_in, m_fox_qnorm_g, m_fox_knorm_g, m_fox_f_bias, m_fox_onorm_g, m_gdn_conv_w, m_gdn_A_log, m_gdn_dt_bias, m_gdn_onorm_g, m_w_out, m_norm_xattn_g, m_mem_norm_g, m_w_cq, m_w_ckv, m_xattn_qnorm_g, m_xattn_knorm_g, m_w_co, m_norm_mlp_g, m_w_mlp1, m_w_mlp2, v_norm_mix_g, v_w_in, v_fox_qnorm_g, v_fox_knorm_g, v_fox_f_bias, v_fox_onorm_g, v_gdn_conv_w, v_gdn_A_log, v_gdn_dt_bias, v_gdn_onorm_g, v_w_out, v_norm_xattn_g, v_mem_norm_g, v_w_cq, v_w_ckv, v_xattn_qnorm_g, v_xattn_knorm_g, v_w_co, v_norm_mlp_g, v_w_mlp1, v_w_mlp2):
    wts = dict(norm_mix_g=norm_mix_g, w_in=w_in, fox_qnorm_g=fox_qnorm_g, fox_knorm_g=fox_knorm_g, fox_f_bias=fox_f_bias,
               fox_onorm_g=fox_onorm_g, gdn_conv_w=gdn_conv_w, gdn_A_log=gdn_A_log, gdn_dt_bias=gdn_dt_bias,
               gdn_onorm_g=gdn_onorm_g, w_out=w_out, norm_xattn_g=norm_xattn_g, mem_norm_g=mem_norm_g, w_cq=w_cq, w_ckv=w_ckv,
               xattn_qnorm_g=xattn_qnorm_g, xattn_knorm_g=xattn_knorm_g, w_co=w_co, norm_mlp_g=norm_mlp_g, w_mlp1=w_mlp1,
               w_mlp2=w_mlp2)
    mom = dict(norm_mix_g=m_norm_mix_g, w_in=m_w_in, fox_qnorm_g=m_fox_qnorm_g, fox_knorm_g=m_fox_knorm_g,
               fox_f_bias=m_fox_f_bias, fox_onorm_g=m_fox_onorm_g, gdn_conv_w=m_gdn_conv_w, gdn_A_log=m_gdn_A_log,
               gdn_dt_bias=m_gdn_dt_bias, gdn_onorm_g=m_gdn_onorm_g, w_out=m_w_out, norm_xattn_g=m_norm_xattn_g,
               mem_norm_g=m_mem_norm_g, w_cq=m_w_cq, w_ckv=m_w_ckv, xattn_qnorm_g=m_xattn_qnorm_g,
               xattn_knorm_g=m_xattn_knorm_g, w_co=m_w_co, norm_mlp_g=m_norm_mlp_g, w_mlp1=m_w_mlp1, w_mlp2=m_w_mlp2)
    var = dict(norm_mix_g=v_norm_mix_g, w_in=v_w_in, fox_qnorm_g=v_fox_qnorm_g, fox_knorm_g=v_fox_knorm_g,
               fox_f_bias=v_fox_f_bias, fox_onorm_g=v_fox_onorm_g, gdn_conv_w=v_gdn_conv_w, gdn_A_log=v_gdn_A_log,
               gdn_dt_bias=v_gdn_dt_bias, gdn_onorm_g=v_gdn_onorm_g, w_out=v_w_out, norm_xattn_g=v_norm_xattn_g,
               mem_norm_g=v_mem_norm_g, w_cq=v_w_cq, w_ckv=v_w_ckv, xattn_qnorm_g=v_xattn_qnorm_g,
               xattn_knorm_g=v_xattn_knorm_g, w_co=v_w_co, norm_mlp_g=v_norm_mlp_g, w_mlp1=v_w_mlp1, w_mlp2=v_w_mlp2)
    B, S, D = x.shape
    T = B * S
    big_sizes = [r * c for _, (r, c), _ in BIG_SHARDS]
    conv_wire = 2 * CONV_SHARD[0] * CONV_SHARD[1]

    shards = [wts[n][0].astype(MXU_DTYPE) for n, _, _ in BIG_SHARDS] + [_conv_to_wire(gdn_conv_w[0])]
    gathered = all_gather_shards(_pack_rows(shards, PACK_ROWS))
    pieces = _unpack_rows(gathered, big_sizes + [conv_wire], lead=(N_CHIPS,))
    full = {}
    for (n, (r, c), by_cols), p in zip(BIG_SHARDS, pieces):
        p = p.reshape(N_CHIPS, r, c)
        full[n] = p.transpose(1, 0, 2).reshape(r, N_CHIPS * c) if by_cols else p.reshape(N_CHIPS * r, c)
    conv_full = _conv_from_wire(pieces[-1].reshape((N_CHIPS,) + CONV_SHARD + (2,)))
    conv_full = conv_full.transpose(1, 0, 2).reshape(CONV_WIDTH, 3 * GDN_WIDTH)
    w = dict(wa=align_w_in(full["w_in"]), w_out=full["w_out"], w_cq=full["w_cq"], w_ckv=full["w_ckv"], w_co=full["w_co"],
             w_mlp1=full["w_mlp1"], w_mlp2=full["w_mlp2"], conv_w=conv_full)
    sp = {n: wts[n] for n, _ in SMALL_DIMS}

    loss_part, grad_x, g_big, g_small = local_step(x.reshape(T, D), mem.reshape(-1, D), loss_target.reshape(T, D), w, sp, B=B)

    small_pieces = [g_small[n] for n, _ in SMALL_DIMS] + [g_small["gdn_conv_w"], loss_part]
    small_sizes = [d for _, d in SMALL_DIMS] + [CONV_WIDTH * 3 * GDN_WIDTH, LANES]
    red_small = _unpack_rows(all_reduce_small(_pack_rows(small_pieces, SMALL_ROWS)), small_sizes)
    grads = {n: p.reshape(1, d) for (n, d), p in zip(SMALL_DIMS, red_small)}
    chip = 2 * lax.axis_index("x") + lax.axis_index("y")
    conv_grad = lax.dynamic_slice(red_small[-2].reshape(CONV_WIDTH, 3 * GDN_WIDTH), (0, chip * CONV_SHARD[1]), CONV_SHARD)
    grads["gdn_conv_w"] = conv_grad.reshape((1,) + CONV_SHARD)
    loss = red_small[-1][0]

    by_chip = []
    for n, (r, c), by_cols in BIG_SHARDS:
        g = g_big[n]
        g = g.reshape(r, N_CHIPS, c).transpose(1, 0, 2) if by_cols else g.reshape(N_CHIPS, r, c)
        by_chip.append(g)
    G = _pack_rows(by_chip, PACK_ROWS, lead=(N_CHIPS,))
    core = lax.axis_index("c").astype(jnp.int32).reshape(1)
    chip_part = add_core_halves(G, exchange_core_halves(G), core)
    reduced = join_core_halves(sum_chips(scatter_to_chips(chip_part)))
    for (n, (r, c), _), p in zip(BIG_SHARDS, _unpack_rows(reduced, big_sizes)):
        grads[n] = p.reshape(1, r, c)

    delta, new_m, new_v = {}, {}, {}
    for n, (r, c), _ in BIG_SHARDS:
        d, nm, nv = adamw(wts[n][0], grads[n][0], mom[n][0], var[n][0], name="adamw_" + n, tr=256)
        delta[n], new_m[n], new_v[n] = d[None], nm[None], nv[None]
    small_names = [n for n, _ in SMALL_DIMS] + ["gdn_conv_w"]
    small_sz = [d for _, d in SMALL_DIMS] + [CONV_SHARD[0] * CONV_SHARD[1]]
    packed4 = [_pack_rows([src[n] for n in small_names], SMALL_ADAM_ROWS) for src in (wts, grads, mom, var)]
    outs = adamw(*packed4, name="adamw_small", tr=SMALL_ADAM_ROWS)
    for dst, buf in zip((delta, new_m, new_v), outs):
        for n, p in zip(small_names, _unpack_rows(buf, small_sz)):
            dst[n] = p.reshape(wts[n].shape)

    return (loss, grad_x.reshape(B, S, D), *[grads[n] for n in WEIGHT_ORDER], *[delta[n] for n in WEIGHT_ORDER],
            *[new_m[n] for n in WEIGHT_ORDER], *[new_v[n] for n in WEIGHT_ORDER])
```

```python
import functools

import jax
import jax.numpy as jnp
import numpy as np
from jax import lax
from jax.experimental import pallas as pl
from jax.experimental.pallas import tpu as pltpu

f32 = jnp.float32
bf16 = jnp.bfloat16
MXU_DTYPE = jnp.bfloat16
WIRE_DTYPE = jnp.bfloat16
INV_PRECISION = lax.Precision.HIGH

D_MODEL = 1024
FOX_HEADS = 8
FOX_HEAD_DIM = 64
FOX_WIDTH = 512
GDN_HEADS = 4
GDN_HEAD_DIM = 128
GDN_WIDTH = 512
CONV_WIDTH = 4
GDN_CHUNK = 64
XATTN_HEADS = 4
XATTN_HEAD_DIM = 128
XATTN_WIDTH = 512
D_FF = 4096
IN_DIM = 3600
EPS = 1e-6
NEG_INF = -1e30
LANES = 128
ADAM_LR = 0.001
ADAM_B1 = 0.9
ADAM_B2 = 0.999
ADAM_EPS = 1e-08
ADAM_WD = 0.01
ADAM_STEP = 10
VMEM_LIMIT = 48 * 1024 * 1024

COL_FOX = 0
COL_GDN = 1536
COL_Z = 3072
COL_SMALL = 3584
IN_ALIGNED = 3840
IN_TILE = 768
SM_F = 0
SM_B = 8
SM_A = 12


def _cparams(*sem):
    return pltpu.CompilerParams(dimension_semantics=sem, vmem_limit_bytes=VMEM_LIMIT)


def _mx(v):
    return v.astype(MXU_DTYPE)


def _dot(a, b, dims, precision=None):
    return lax.dot_general(a, b, (dims, ((), ())), preferred_element_type=f32, precision=precision)


def _dotm(a, b, dims):
    return _dot(_mx(a), _mx(b), dims)


NN = ((1,), (0,))
NT = ((1,), (1,))
TN = ((0,), (0,))


def matmul(a, b, *, name, ta=False, tb=False, residual=None, relu2_out=False, relu2_bwd_aux=None,
           out_dtype=f32, tm=1024, tn=1024, tk=1024):
    M, K = (a.shape[1], a.shape[0]) if ta else a.shape
    N = b.shape[0] if tb else b.shape[1]
    tm, tn, tk = min(tm, M), min(tn, N), min(tk, K)
    assert M % tm == 0 and N % tn == 0 and K % tk == 0, (name, M, N, K)
    nk = K // tk
    has_res = residual is not None
    has_aux = relu2_bwd_aux is not None

    def body(*refs):
        a_ref, b_ref = refs[0], refs[1]
        pos = 2
        res_ref = aux_ref = None
        if has_res:
            res_ref = refs[pos]
            pos += 1
        if has_aux:
            aux_ref = refs[pos]
            pos += 1
        o_ref = refs[pos]
        pos += 1
        act_ref = None
        if relu2_out:
            act_ref = refs[pos]
            pos += 1
        acc_ref = refs[pos]
        k = pl.program_id(2)

        @pl.when(k == 0)
        def _():
            acc_ref[...] = jnp.zeros_like(acc_ref)

        dims = ((0,) if ta else (1,), (1,) if tb else (0,))
        acc_ref[...] += _dot(_mx(a_ref[...]), _mx(b_ref[...]), dims)

        @pl.when(k == nk - 1)
        def _():
            r = acc_ref[...]
            if has_res:
                r = r + res_ref[...]
            if has_aux:
                r = r * (2.0 * jnp.maximum(aux_ref[...], 0.0))
            o_ref[...] = r.astype(o_ref.dtype)
            if relu2_out:
                act_ref[...] = jnp.square(jnp.maximum(r, 0.0)).astype(act_ref.dtype)

    a_spec = pl.BlockSpec((tk, tm), lambda i, j, k: (k, i)) if ta else pl.BlockSpec((tm, tk), lambda i, j, k: (i, k))
    b_spec = pl.BlockSpec((tn, tk), lambda i, j, k: (j, k)) if tb else pl.BlockSpec((tk, tn), lambda i, j, k: (k, j))
    o_spec = pl.BlockSpec((tm, tn), lambda i, j, k: (i, j))
    in_specs, args = [a_spec, b_spec], [a, b]
    if has_res:
        in_specs.append(o_spec)
        args.append(residual)
    if has_aux:
        in_specs.append(o_spec)
        args.append(relu2_bwd_aux)
    out_shape = [jax.ShapeDtypeStruct((M, N), out_dtype)]
    out_specs = [o_spec]
    if relu2_out:
        out_shape.append(jax.ShapeDtypeStruct((M, N), MXU_DTYPE))
        out_specs.append(o_spec)
    res = pl.pallas_call(
        body, name=name, grid=(M // tm, N // tn, nk), in_specs=in_specs, out_specs=out_specs, out_shape=out_shape,
        scratch_shapes=[pltpu.VMEM((tm, tn), f32)],
        compiler_params=_cparams("parallel", "parallel", "arbitrary"),
    )(*args)
    return res if relu2_out else res[0]


def rms_fwd(x, g, *, name, tr=512):
    R, D = x.shape
    tr = min(tr, R)

    def body(x_ref, g_ref, o_ref):
        xv = x_ref[...]
        y = xv * lax.rsqrt(jnp.mean(xv * xv, axis=-1, keepdims=True) + EPS)
        o_ref[...] = (y * g_ref[...]).astype(o_ref.dtype)

    return pl.pallas_call(
        body, name=name, grid=(R // tr,),
        in_specs=[pl.BlockSpec((tr, D), lambda i: (i, 0)), pl.BlockSpec((1, D), lambda i: (0, 0))],
        out_specs=pl.BlockSpec((tr, D), lambda i: (i, 0)),
        out_shape=jax.ShapeDtypeStruct((R, D), MXU_DTYPE),
        compiler_params=_cparams("parallel"),
    )(x, g)


def rms_bwd(x, g, dh, residual, *, name, tr=512):
    R, D = x.shape
    tr = min(tr, R)
    has_res = residual is not None

    def body(*refs):
        if has_res:
            x_ref, g_ref, dh_ref, res_ref, dx_ref, dg_ref = refs
        else:
            x_ref, g_ref, dh_ref, dx_ref, dg_ref = refs
        xv = x_ref[...]
        rstd = lax.rsqrt(jnp.mean(xv * xv, axis=-1, keepdims=True) + EPS)
        xhat = xv * rstd
        dh = dh_ref[...].astype(f32)
        gd = dh * g_ref[...]
        dx = rstd * (gd - xhat * jnp.mean(gd * xhat, axis=-1, keepdims=True))
        if has_res:
            dx = dx + res_ref[...]
        dx_ref[...] = dx

        @pl.when(pl.program_id(0) == 0)
        def _():
            dg_ref[...] = jnp.zeros_like(dg_ref)

        dg_ref[...] += jnp.sum(dh * xhat, axis=0, keepdims=True)

    row = pl.BlockSpec((tr, D), lambda i: (i, 0))
    vec = pl.BlockSpec((1, D), lambda i: (0, 0))
    in_specs = [row, vec, row] + ([row] if has_res else [])
    args = [x, g, dh] + ([residual] if has_res else [])
    return pl.pallas_call(
        body, name=name, grid=(R // tr,), in_specs=in_specs, out_specs=[row, vec],
        out_shape=[jax.ShapeDtypeStruct((R, D), f32), jax.ShapeDtypeStruct((1, D), f32)],
        compiler_params=_cparams("arbitrary"),
    )(*args)


def loss_head(y, target, *, tr=512):
    R, D = y.shape
    tr = min(tr, R)

    def body(y_ref, t_ref, dy_ref, loss_ref):
        e = y_ref[...] - t_ref[...]
        dy_ref[...] = e * (1.0 / D)

        @pl.when(pl.program_id(0) == 0)
        def _():
            loss_ref[...] = jnp.zeros_like(loss_ref)

        part = 0.5 * jnp.sum(jnp.mean(e * e, axis=-1, keepdims=True), axis=0, keepdims=True)
        loss_ref[...] += jnp.broadcast_to(part, loss_ref.shape)

    row = pl.BlockSpec((tr, D), lambda i: (i, 0))
    return pl.pallas_call(
        body, name="loss_head", grid=(R // tr,), in_specs=[row, row],
        out_specs=[row, pl.BlockSpec((1, LANES), lambda i: (0, 0))],
        out_shape=[jax.ShapeDtypeStruct((R, D), f32), jax.ShapeDtypeStruct((1, LANES), f32)],
        compiler_params=_cparams("arbitrary"),
    )(y, target)


def _head_rms(v, g):
    r = lax.rsqrt(jnp.mean(v * v, axis=-1, keepdims=True) + EPS)
    return v * r * g, r


def _head_rms_bwd(v, r, g, dn):
    vhat = v * r
    gd = dn * g
    dv = r * (gd - vhat * jnp.mean(gd * vhat, axis=-1, keepdims=True))
    return dv, jnp.sum(dn * vhat, axis=0, keepdims=True)


def _softmax_rows(s):
    m = jnp.max(s, axis=-1, keepdims=True)
    e = jnp.exp(s - m)
    return e / jnp.sum(e, axis=-1, keepdims=True)


def xattn_fwd(cq, ckv, gq, gk, *, B, tq=512):
    T = cq.shape[0]
    S = T // B
    M = ckv.shape[0] // B
    tq = min(tq, S)
    nq = S // tq
    scale = XATTN_HEAD_DIM ** -0.5

    def body(q_ref, k_ref, v_ref, gq_ref, gk_ref, o_ref):
        qn, _ = _head_rms(q_ref[...], gq_ref[...])
        kn, _ = _head_rms(k_ref[...], gk_ref[...])
        p = _softmax_rows(_dot(_mx(qn), _mx(kn), NT) * scale)
        o_ref[...] = _dot(_mx(p), _mx(v_ref[...]), NN).astype(o_ref.dtype)

    hd = XATTN_HEAD_DIM
    vec = pl.BlockSpec((1, hd), lambda b, h, i: (0, 0))
    return pl.pallas_call(
        body, name="xattn_fwd", grid=(B, XATTN_HEADS, nq),
        in_specs=[pl.BlockSpec((tq, hd), lambda b, h, i: (b * nq + i, h)),
                  pl.BlockSpec((M, hd), lambda b, h, i: (b, h)),
                  pl.BlockSpec((M, hd), lambda b, h, i: (b, XATTN_HEADS + h)), vec, vec],
        out_specs=pl.BlockSpec((tq, hd), lambda b, h, i: (b * nq + i, h)),
        out_shape=jax.ShapeDtypeStruct((T, XATTN_WIDTH), MXU_DTYPE),
        compiler_params=_cparams("parallel", "parallel", "parallel"),
    )(cq, ckv, ckv, gq, gk)


def xattn_bwd(cq, ckv, gq, gk, dco, *, B, tq=512):
    T = cq.shape[0]
    S = T // B
    M = ckv.shape[0] // B
    tq = min(tq, S)
    nq = S // tq
    scale = XATTN_HEAD_DIM ** -0.5
    hd = XATTN_HEAD_DIM

    def body(q_ref, k_ref, v_ref, gq_ref, gk_ref, do_ref, dq_ref, dk_ref, dv_ref, dgq_ref, dgk_ref, dkn_acc, dv_acc):
        b, h, i = pl.program_id(0), pl.program_id(1), pl.program_id(2)

        @pl.when((b == 0) & (h == 0) & (i == 0))
        def _():
            dgq_ref[...] = jnp.zeros_like(dgq_ref)
            dgk_ref[...] = jnp.zeros_like(dgk_ref)

        @pl.when(i == 0)
        def _():
            dkn_acc[...] = jnp.zeros_like(dkn_acc)
            dv_acc[...] = jnp.zeros_like(dv_acc)

        q, k, v = q_ref[...], k_ref[...], v_ref[...]
        gqv, gkv = gq_ref[...], gk_ref[...]
        qn, rq = _head_rms(q, gqv)
        kn, rk = _head_rms(k, gkv)
        p = _softmax_rows(_dot(_mx(qn), _mx(kn), NT) * scale)
        do = do_ref[...]
        dv_acc[...] += _dot(_mx(p), _mx(do), TN)
        dp = _dot(_mx(do), _mx(v), NT)
        ds = p * (dp - jnp.sum(dp * p, axis=-1, keepdims=True)) * scale
        dqn = _dot(_mx(ds), _mx(kn), NN)
        dkn_acc[...] += _dot(_mx(ds), _mx(qn), TN)
        dq, dgq = _head_rms_bwd(q, rq, gqv, dqn)
        dq_ref[...] = dq.astype(dq_ref.dtype)
        dgq_ref[...] += dgq

        @pl.when(i == nq - 1)
        def _():
            dk, dgk = _head_rms_bwd(k, rk, gkv, dkn_acc[...])
            dk_ref[...] = dk.astype(dk_ref.dtype)
            dv_ref[...] = dv_acc[...].astype(dv_ref.dtype)
            dgk_ref[...] += dgk

    vec = pl.BlockSpec((1, hd), lambda b, h, i: (0, 0))
    qspec = pl.BlockSpec((tq, hd), lambda b, h, i: (b * nq + i, h))
    kspec = pl.BlockSpec((M, hd), lambda b, h, i: (b, h))
    vspec = pl.BlockSpec((M, hd), lambda b, h, i: (b, XATTN_HEADS + h))
    dq, dk, dv, dgq, dgk = pl.pallas_call(
        body, name="xattn_bwd", grid=(B, XATTN_HEADS, nq),
        in_specs=[qspec, kspec, vspec, vec, vec, qspec],
        out_specs=[qspec, kspec, kspec, vec, vec],
        out_shape=[jax.ShapeDtypeStruct((T, XATTN_WIDTH), MXU_DTYPE),
                   jax.ShapeDtypeStruct((B * M, XATTN_WIDTH), MXU_DTYPE),
                   jax.ShapeDtypeStruct((B * M, XATTN_WIDTH), MXU_DTYPE),
                   jax.ShapeDtypeStruct((1, hd), f32), jax.ShapeDtypeStruct((1, hd), f32)],
        scratch_shapes=[pltpu.VMEM((M, hd), f32), pltpu.VMEM((M, hd), f32)],
        compiler_params=_cparams("arbitrary", "arbitrary", "arbitrary"),
    )(cq, ckv, ckv, gq, gk, dco)
    return dq, jnp.concatenate([dk, dv], axis=1), dgq, dgk


FOX_PAIRS = FOX_HEADS // 2


def _fox_scores(qn, kn, ccol, crow, q0, tq, S, scale):
    s = _dot(_mx(qn), _mx(kn), NT) * scale + ccol - crow
    qpos = q0 + lax.broadcasted_iota(jnp.int32, (tq, S), 0)
    kpos = lax.broadcasted_iota(jnp.int32, (tq, S), 1)
    return jnp.where(kpos <= qpos, s, NEG_INF)


def fox_fwd(P, ccol, crow, gq, gk, go, *, B, tq=256):
    T = P.shape[0]
    S = T // B
    tq = min(tq, S)
    nq = S // tq
    hd = FOX_HEAD_DIM
    scale = hd ** -0.5

    def body(q_ref, k_ref, v_ref, ccol_ref, crow_ref, gq_ref, gk_ref, go_ref, o_ref, oa_ref):
        q0 = pl.program_id(2) * tq
        for e in range(2):
            sl = slice(e * hd, (e + 1) * hd)
            qn, _ = _head_rms(q_ref[:, sl], gq_ref[:, sl])
            kn, _ = _head_rms(k_ref[:, sl], gk_ref[:, sl])
            p = _softmax_rows(_fox_scores(qn, kn, ccol_ref[0, e], crow_ref[0, e], q0, tq, S, scale))
            o = _dot(_mx(p), _mx(v_ref[:, sl]), NN)
            o_ref[:, sl] = o
            oa_ref[:, sl] = _head_rms(o, go_ref[:, sl])[0].astype(oa_ref.dtype)

    W = 2 * hd
    vec = pl.BlockSpec((1, W), lambda b, h, i: (0, 0))
    ospec = pl.BlockSpec((tq, W), lambda b, h, i: (b * nq + i, h))
    return pl.pallas_call(
        body, name="fox_fwd", grid=(B, FOX_PAIRS, nq),
        in_specs=[pl.BlockSpec((tq, W), lambda b, h, i: (b * nq + i, h)),
                  pl.BlockSpec((S, W), lambda b, h, i: (b, FOX_PAIRS + h)),
                  pl.BlockSpec((S, W), lambda b, h, i: (b, 2 * FOX_PAIRS + h)),
                  pl.BlockSpec((1, 2, tq, 1), lambda b, h, i: (b, h, i, 0)),
                  pl.BlockSpec((1, 2, 1, S), lambda b, h, i: (b, h, 0, 0)), vec, vec, vec],
        out_specs=[ospec, ospec],
        out_shape=[jax.ShapeDtypeStruct((T, FOX_WIDTH), f32), jax.ShapeDtypeStruct((T, FOX_WIDTH), MXU_DTYPE)],
        compiler_params=_cparams("parallel", "parallel", "parallel"),
    )(P, P, P, ccol, crow, gq, gk, go)


def fox_bwd(P, ccol, crow, gq, gk, go, o_raw, d_oab, *, B, tq=256):
    T = P.shape[0]
    S = T // B
    tq = min(tq, S)
    nq = S // tq
    hd = FOX_HEAD_DIM
    scale = hd ** -0.5

    def body(q_ref, k_ref, v_ref, ccol_ref, crow_ref, gq_ref, gk_ref, go_ref, o_ref, doa_ref,
             dq_ref, dk_ref, dv_ref, dccol_ref, dcrow_ref, dgq_ref, dgk_ref, dgo_ref, dkn_acc, dv_acc, dcrow_acc):
        b, h, i = pl.program_id(0), pl.program_id(1), pl.program_id(2)
        q0 = i * tq

        @pl.when((b == 0) & (h == 0) & (i == 0))
        def _():
            dgq_ref[...] = jnp.zeros_like(dgq_ref)
            dgk_ref[...] = jnp.zeros_like(dgk_ref)
            dgo_ref[...] = jnp.zeros_like(dgo_ref)

        @pl.when(i == 0)
        def _():
            dkn_acc[...] = jnp.zeros_like(dkn_acc)
            dv_acc[...] = jnp.zeros_like(dv_acc)
            dcrow_acc[...] = jnp.zeros_like(dcrow_acc)

        for e in range(2):
            sl = slice(e * hd, (e + 1) * hd)
            q, k, v = q_ref[:, sl], k_ref[:, sl], v_ref[:, sl]
            gqv, gkv, gov = gq_ref[:, sl], gk_ref[:, sl], go_ref[:, sl]
            qn, rq = _head_rms(q, gqv)
            kn, rk = _head_rms(k, gkv)
            p = _softmax_rows(_fox_scores(qn, kn, ccol_ref[0, e], crow_ref[0, e], q0, tq, S, scale))
            o = o_ref[:, sl]
            ro = lax.rsqrt(jnp.mean(o * o, axis=-1, keepdims=True) + EPS)
            do, dgo = _head_rms_bwd(o, ro, gov, doa_ref[:, sl])
            dgo_ref[:, sl] += dgo
            dv_acc[e] += _dot(_mx(p), _mx(do), TN)
            dp = _dot(_mx(do), _mx(v), NT)
            ds = p * (dp - jnp.sum(do * o, axis=-1, keepdims=True))
            dccol_ref[0, e] = jnp.sum(ds, axis=1, keepdims=True)
            dcrow_acc[e] -= jnp.sum(ds, axis=0, keepdims=True)
            dqn = _dot(_mx(ds), _mx(kn), NN) * scale
            dkn_acc[e] += _dot(_mx(ds), _mx(qn), TN) * scale
            dq, dgq = _head_rms_bwd(q, rq, gqv, dqn)
            dq_ref[:, sl] = dq.astype(dq_ref.dtype)
            dgq_ref[:, sl] += dgq

        @pl.when(i == nq - 1)
        def _():
            for e in range(2):
                sl = slice(e * hd, (e + 1) * hd)
                k = k_ref[:, sl]
                gkv = gk_ref[:, sl]
                rk = lax.rsqrt(jnp.mean(k * k, axis=-1, keepdims=True) + EPS)
                dk, dgk = _head_rms_bwd(k, rk, gkv, dkn_acc[e])
                dk_ref[:, sl] = dk.astype(dk_ref.dtype)
                dv_ref[:, sl] = dv_acc[e].astype(dv_ref.dtype)
                dgk_ref[:, sl] += dgk
                dcrow_ref[0, e] = dcrow_acc[e]

    W = 2 * hd
    vec = pl.BlockSpec((1, W), lambda b, h, i: (0, 0))
    qspec = pl.BlockSpec((tq, W), lambda b, h, i: (b * nq + i, h))
    kvout = pl.BlockSpec((S, W), lambda b, h, i: (b, h))
    colspec = pl.BlockSpec((1, 2, tq, 1), lambda b, h, i: (b, h, i, 0))
    rowspec = pl.BlockSpec((1, 2, 1, S), lambda b, h, i: (b, h, 0, 0))
    return pl.pallas_call(
        body, name="fox_bwd", grid=(B, FOX_PAIRS, nq),
        in_specs=[qspec,
                  pl.BlockSpec((S, W), lambda b, h, i: (b, FOX_PAIRS + h)),
                  pl.BlockSpec((S, W), lambda b, h, i: (b, 2 * FOX_PAIRS + h)),
                  colspec, rowspec, vec, vec, vec, qspec, qspec],
        out_specs=[qspec, kvout, kvout, colspec, rowspec, vec, vec, vec],
        out_shape=[jax.ShapeDtypeStruct((T, FOX_WIDTH), MXU_DTYPE), jax.ShapeDtypeStruct((T, FOX_WIDTH), MXU_DTYPE),
                   jax.ShapeDtypeStruct((T, FOX_WIDTH), MXU_DTYPE),
                   jax.ShapeDtypeStruct((B, FOX_HEADS, S, 1), f32), jax.ShapeDtypeStruct((B, FOX_HEADS, 1, S), f32),
                   jax.ShapeDtypeStruct((1, W), f32), jax.ShapeDtypeStruct((1, W), f32), jax.ShapeDtypeStruct((1, W), f32)],
        scratch_shapes=[pltpu.VMEM((2, S, hd), f32), pltpu.VMEM((2, S, hd), f32), pltpu.VMEM((2, 1, S), f32)],
        compiler_params=_cparams("arbitrary", "arbitrary", "arbitrary"),
    )(P, P, P, ccol, crow, gq, gk, go, o_raw, d_oab)


def _lane_mask(lo, hi, shape):
    lane = lax.broadcasted_iota(jnp.int32, shape, 1)
    return (lane >= lo) & (lane < hi)


def _cumsum_rows(v, period, reverse=False):
    n = v.shape[0]
    pos = lax.broadcasted_iota(jnp.int32, v.shape, 0) % period
    sh = 1
    while sh < period:
        if reverse:
            v = v + jnp.where(pos + sh < period, pltpu.roll(v, n - sh, 0), 0.0)
        else:
            v = v + jnp.where(pos >= sh, pltpu.roll(v, sh, 0), 0.0)
        sh *= 2
    return v


def _gate_values(z, bias, alog):
    zb = z + bias
    ls = jax.nn.log_sigmoid(zb)
    beta = jax.nn.sigmoid(z)
    g = -jnp.exp(alog) * jax.nn.softplus(zb)
    return zb, ls, beta, g


def gates_fwd(P, bias, alog, *, B):
    T = P.shape[0]
    S = T // B

    def body(z_ref, bias_ref, alog_ref, o_ref):
        z = z_ref[...]
        _, ls, beta, g = _gate_values(z, bias_ref[...], alog_ref[...])
        c = _cumsum_rows(ls, S)
        gc = _cumsum_rows(g, GDN_CHUNK)
        o = jnp.where(_lane_mask(SM_F, SM_F + FOX_HEADS, z.shape), c, 0.0)
        o = jnp.where(_lane_mask(SM_B, SM_B + GDN_HEADS, z.shape), beta, o)
        o = jnp.where(_lane_mask(SM_A, SM_A + GDN_HEADS, z.shape), gc, o)
        o_ref[...] = o

    vec = pl.BlockSpec((1, LANES), lambda b: (0, 0))
    return pl.pallas_call(
        body, name="gates_fwd", grid=(B,),
        in_specs=[pl.BlockSpec((S, LANES), lambda b: (b, COL_SMALL // LANES)), vec, vec],
        out_specs=pl.BlockSpec((S, LANES), lambda b: (b, 0)),
        out_shape=jax.ShapeDtypeStruct((T, LANES), f32),
        compiler_params=_cparams("parallel"),
    )(P, bias, alog)


def gates_bwd(P, bias, alog, dgates, *, B):
    T = P.shape[0]
    S = T // B

    def body(z_ref, bias_ref, alog_ref, dg_ref, dz_ref, par_ref):
        z = z_ref[...]
        zb, ls, beta, g = _gate_values(z, bias_ref[...], alog_ref[...])
        d = dg_ref[...]
        dls = _cumsum_rows(d, S, reverse=True)
        dgr = _cumsum_rows(d, GDN_CHUNK, reverse=True)
        sig = jax.nn.sigmoid(zb)
        dz_f = dls * (1.0 - sig)
        dz_b = d * beta * (1.0 - beta)
        dz_a = dgr * (-jnp.exp(alog_ref[...])) * sig
        dz = jnp.where(_lane_mask(SM_F, SM_F + FOX_HEADS, z.shape), dz_f, 0.0)
        dz = jnp.where(_lane_mask(SM_B, SM_B + GDN_HEADS, z.shape), dz_b, dz)
        dz = jnp.where(_lane_mask(SM_A, SM_A + GDN_HEADS, z.shape), dz_a, dz)
        dz_ref[...] = dz.astype(dz_ref.dtype)

        @pl.when(pl.program_id(0) == 0)
        def _():
            par_ref[...] = jnp.zeros_like(par_ref)

        dalog = jnp.where(_lane_mask(SM_A, SM_A + GDN_HEADS, z.shape), dgr * g, 0.0)
        par_ref[0:1, :] += jnp.sum(dz, axis=0, keepdims=True)
        par_ref[1:2, :] += jnp.sum(dalog, axis=0, keepdims=True)

    vec = pl.BlockSpec((1, LANES), lambda b: (0, 0))
    return pl.pallas_call(
        body, name="gates_bwd", grid=(B,),
        in_specs=[pl.BlockSpec((S, LANES), lambda b: (b, COL_SMALL // LANES)), vec, vec,
                  pl.BlockSpec((S, LANES), lambda b: (b, 0))],
        out_specs=[pl.BlockSpec((S, LANES), lambda b: (b, 0)), pl.BlockSpec((8, LANES), lambda b: (0, 0))],
        out_shape=[jax.ShapeDtypeStruct((T, LANES), MXU_DTYPE), jax.ShapeDtypeStruct((8, LANES), f32)],
        compiler_params=_cparams("arbitrary"),
    )(P, bias, alog, dgates)


GDN_BLOCKS = 3 * GDN_HEADS


def _shift_rows(v, d, reverse=False):
    if d == 0:
        return v
    n = v.shape[0]
    row = lax.broadcasted_iota(jnp.int32, v.shape, 0)
    if reverse:
        return jnp.where(row + d < n, pltpu.roll(v, n - d, 0), 0.0)
    return jnp.where(row >= d, pltpu.roll(v, d, 0), 0.0)


def _conv_silu(x, w):
    pre = sum(w[j:j + 1, :] * _shift_rows(x, CONV_WIDTH - 1 - j) for j in range(CONV_WIDTH))
    return pre, pre * jax.nn.sigmoid(pre)


def gdn_prep_fwd(P, conv_w, *, B):
    T = P.shape[0]
    S = T // B

    def body(x_ref, w_ref, o_ref):
        _, y = _conv_silu(x_ref[...], w_ref[...])
        yn = y * lax.rsqrt(jnp.sum(y * y, axis=-1, keepdims=True) + EPS)
        o_ref[...] = jnp.where(pl.program_id(1) < 2 * GDN_HEADS, yn, y)

    return pl.pallas_call(
        body, name="gdn_prep_fwd", grid=(B, GDN_BLOCKS),
        in_specs=[pl.BlockSpec((S, LANES), lambda b, j: (b, COL_GDN // LANES + j)),
                  pl.BlockSpec((CONV_WIDTH, LANES), lambda b, j: (0, j))],
        out_specs=pl.BlockSpec((S, LANES), lambda b, j: (b, j)),
        out_shape=jax.ShapeDtypeStruct((T, 3 * GDN_WIDTH), f32),
        compiler_params=_cparams("parallel", "parallel"),
    )(P, conv_w)


def gdn_prep_bwd(P, conv_w, dG, *, B):
    T = P.shape[0]
    S = T // B

    def body(x_ref, w_ref, dg_ref, dx_ref, dw_ref):
        x, w = x_ref[...], w_ref[...]
        pre, y = _conv_silu(x, w)
        dn = dg_ref[...]
        r = lax.rsqrt(jnp.sum(y * y, axis=-1, keepdims=True) + EPS)
        n = y * r
        dy_norm = r * (dn - n * jnp.sum(dn * n, axis=-1, keepdims=True))
        dy = jnp.where(pl.program_id(0) < 2 * GDN_HEADS, dy_norm, dn)
        sg = jax.nn.sigmoid(pre)
        dpre = dy * (sg * (1.0 + pre * (1.0 - sg)))
        dx = sum(w[j:j + 1, :] * _shift_rows(dpre, CONV_WIDTH - 1 - j, reverse=True) for j in range(CONV_WIDTH))
        dx_ref[...] = dx.astype(dx_ref.dtype)

        @pl.when(pl.program_id(1) == 0)
        def _():
            dw_ref[...] = jnp.zeros_like(dw_ref)

        for j in range(CONV_WIDTH):
            dw_ref[j:j + 1, :] += jnp.sum(dpre * _shift_rows(x, CONV_WIDTH - 1 - j), axis=0, keepdims=True)

    return pl.pallas_call(
        body, name="gdn_prep_bwd", grid=(GDN_BLOCKS, B),
        in_specs=[pl.BlockSpec((S, LANES), lambda j, b: (b, COL_GDN // LANES + j)),
                  pl.BlockSpec((CONV_WIDTH, LANES), lambda j, b: (0, j)),
                  pl.BlockSpec((S, LANES), lambda j, b: (b, j))],
        out_specs=[pl.BlockSpec((S, LANES), lambda j, b: (b, j)),
                   pl.BlockSpec((CONV_WIDTH, LANES), lambda j, b: (0, j))],
        out_shape=[jax.ShapeDtypeStruct((T, 3 * GDN_WIDTH), MXU_DTYPE),
                   jax.ShapeDtypeStruct((CONV_WIDTH, 3 * GDN_WIDTH), f32)],
        compiler_params=_cparams("arbitrary", "arbitrary"),
    )(P, conv_w, dG)


GDN_GROUP = 4
B_NN = (((2,), (1,)), ((0,), (0,)))
B_NT = (((2,), (2,)), ((0,), (0,)))
B_TN = (((1,), (1,)), ((0,), (0,)))


def _bmm(a, b, dims, precision=None):
    if precision is None:
        a, b = _mx(a), _mx(b)
    return lax.dot_general(a, b, dims, preferred_element_type=f32, precision=precision)


def _tri_inverse(A):
    C = A.shape[-1]
    row = lax.broadcasted_iota(jnp.int32, A.shape, 1)
    col = lax.broadcasted_iota(jnp.int32, A.shape, 2)
    eye = (row == col).astype(f32)
    X = jnp.where((row // 4) == (col // 4), -A, 0.0)
    X2 = _bmm(X, X, B_NN, INV_PRECISION)
    Tm = eye + X + X2 + _bmm(X, X2, B_NN, INV_PRECISION)
    b = 4
    while b < C:
        off = ((row // (2 * b)) == (col // (2 * b))) & ((row // b) != (col // b))
        Tm = Tm - _bmm(_bmm(Tm, jnp.where(off, A, 0.0), B_NN, INV_PRECISION), Tm, B_NN, INV_PRECISION)
        b *= 2
    return Tm


def _pick_lane(block, lane_idx):
    lane = lax.broadcasted_iota(jnp.int32, block.shape, 1)
    return jnp.sum(jnp.where(lane == lane_idx, block, 0.0), axis=1, keepdims=True)


def _gdn_local(q, k, v, beta, gc, Tm=None):
    C = GDN_CHUNK
    n = q.shape[0] // C
    q = q.reshape(n, C, -1) * (GDN_HEAD_DIM ** -0.5)
    k = k.reshape(n, C, -1)
    v = v.reshape(n, C, -1)
    beta = beta.reshape(n, C, 1)
    gc = gc.reshape(n, C, 1)
    row = lax.broadcasted_iota(jnp.int32, (n, C, C), 1)
    col = lax.broadcasted_iota(jnp.int32, (n, C, C), 2)
    gcT = jnp.swapaxes(jnp.broadcast_to(gc, (n, C, C)), 1, 2)
    D = jnp.exp(jnp.where(row >= col, gc - gcT, NEG_INF))
    kb = k * beta
    vb = v * beta
    A = jnp.where(row > col, _bmm(kb, k, B_NT) * D, 0.0)
    Gam = jnp.exp(gc)
    kg = kb * Gam
    gl = gc[:, C - 1:C, :]
    kdec = jnp.exp(gl - gc)
    loc = dict(q=q, k=k, v=v, beta=beta, gc=gc, D=D, kb=kb, vb=vb, A=A, Gam=Gam, kg=kg,
               kdec=kdec, kd=k * kdec, qg=q * Gam, gam=jnp.exp(gl), row=row, col=col)
    if Tm is None:
        Tm = _tri_inverse(A)
        loc.update(u=_bmm(Tm, vb, B_NN), w=_bmm(Tm, kg, B_NN), M=_bmm(q, k, B_NT) * D)
    else:
        Tm = Tm.reshape(n, C, C)
    loc["Tm"] = Tm
    return loc


def _gdn_store_local(loc, r0, u_s, w_s, qg_s, kd_s, M_s, gam_s, c0):
    n = loc["u"].shape[0]
    R = n * GDN_CHUNK
    u_s[pl.ds(r0, R), :] = loc["u"].reshape(R, -1)
    w_s[pl.ds(r0, R), :] = loc["w"].reshape(R, -1)
    qg_s[pl.ds(r0, R), :] = loc["qg"].reshape(R, -1)
    kd_s[pl.ds(r0, R), :] = loc["kd"].reshape(R, -1)
    M_s[pl.ds(r0, R), :] = loc["M"].reshape(R, -1)
    gam_s[pl.ds(c0, n)] = jnp.broadcast_to(loc["gam"], (n, 1, LANES))


def _gdn_specs(S):
    blk = lambda off: pl.BlockSpec((S, LANES), lambda b, h: (b, off + h))
    return blk


def gdn_fwd(G, gates, P, g_on, *, B):
    T = G.shape[0]
    S = T // B
    C = GDN_CHUNK
    N = S // C
    R = GDN_GROUP * C
    hd = GDN_HEAD_DIM

    def body(q_ref, k_ref, v_ref, gt_ref, z_ref, gon_ref, o_ref, ob_ref, st_ref, u_s, w_s, qg_s, kd_s, M_s, gam_s):
        h = pl.program_id(1)

        def local(gi, carry):
            r0 = pl.multiple_of(gi * R, R)
            gt = gt_ref[pl.ds(r0, R), :]
            loc = _gdn_local(q_ref[pl.ds(r0, R), :], k_ref[pl.ds(r0, R), :], v_ref[pl.ds(r0, R), :],
                             _pick_lane(gt, SM_B + h), _pick_lane(gt, SM_A + h))
            _gdn_store_local(loc, r0, u_s, w_s, qg_s, kd_s, M_s, gam_s, gi * GDN_GROUP)
            return carry

        lax.fori_loop(0, N // GDN_GROUP, local, 0)

        def step(n, state):
            r0 = pl.multiple_of(n * C, C)
            st_ref[0, 0, n] = state
            v_new = u_s[pl.ds(r0, C), :] - _dotm(w_s[pl.ds(r0, C), :], state, NN)
            o_ref[pl.ds(r0, C), :] = (_dotm(qg_s[pl.ds(r0, C), :], state, NN)
                                      + _dotm(M_s[pl.ds(r0, C), :], v_new, NN))
            return state * gam_s[n] + _dotm(kd_s[pl.ds(r0, C), :], v_new, TN)

        lax.fori_loop(0, N, step, jnp.zeros((hd, hd), f32))
        o = o_ref[...]
        z = z_ref[...]
        ob_ref[...] = (_head_rms(o, gon_ref[...])[0] * (z * jax.nn.sigmoid(z))).astype(ob_ref.dtype)

    blk = lambda off: pl.BlockSpec((S, LANES), lambda b, h: (b, off + h))
    rows = lambda: pltpu.VMEM((S, hd), f32)
    return pl.pallas_call(
        body, name="gdn_fwd", grid=(B, GDN_HEADS),
        in_specs=[blk(0), blk(GDN_HEADS), blk(2 * GDN_HEADS), pl.BlockSpec((S, LANES), lambda b, h: (b, 0)),
                  blk(COL_Z // LANES), pl.BlockSpec((1, hd), lambda b, h: (0, 0))],
        out_specs=[blk(0), blk(0), pl.BlockSpec((1, 1, N, hd, hd), lambda b, h: (b, h, 0, 0, 0))],
        out_shape=[jax.ShapeDtypeStruct((T, GDN_WIDTH), f32), jax.ShapeDtypeStruct((T, GDN_WIDTH), MXU_DTYPE),
                   jax.ShapeDtypeStruct((B, GDN_HEADS, N, hd, hd), f32)],
        scratch_shapes=[rows(), rows(), rows(), rows(), pltpu.VMEM((S, C), f32), pltpu.VMEM((N, 1, LANES), f32)],
        compiler_params=_cparams("parallel", "parallel"),
    )(G, G, G, gates, P, g_on)


def gdn_bwd(G, gates, P, g_on, o_raw, states, d_oab, *, B):
    T = G.shape[0]
    S = T // B
    C = GDN_CHUNK
    N = S // C
    R = GDN_GROUP * C
    hd = GDN_HEAD_DIM

    def body(q_ref, k_ref, v_ref, gt_ref, z_ref, gon_ref, o_ref, st_ref, dob_ref,
             dq_ref, dk_ref, dv_ref, dgt_ref, dz_ref, dgon_ref,
             u_s, w_s, qg_s, kd_s, M_s, gam_s, do_s, du_s, dw_s, dqg_s, dkd_s, dM_s, dgl_s, Tm_s):
        b, h = pl.program_id(0), pl.program_id(1)

        @pl.when((b == 0) & (h == 0))
        def _():
            dgon_ref[...] = jnp.zeros_like(dgon_ref)

        @pl.when(h == 0)
        def _():
            dgt_ref[...] = jnp.zeros_like(dgt_ref)

        def group_inputs(gi, Tm_of=None):
            r0 = pl.multiple_of(gi * R, R)
            gt = gt_ref[pl.ds(r0, R), :]
            Tm = None if Tm_of is None else Tm_of[pl.ds(r0, R), :]
            return r0, _gdn_local(q_ref[pl.ds(r0, R), :], k_ref[pl.ds(r0, R), :], v_ref[pl.ds(r0, R), :],
                                  _pick_lane(gt, SM_B + h), _pick_lane(gt, SM_A + h), Tm)

        def local(gi, carry):
            r0, loc = group_inputs(gi)
            _gdn_store_local(loc, r0, u_s, w_s, qg_s, kd_s, M_s, gam_s, gi * GDN_GROUP)
            Tm_s[pl.ds(r0, R), :] = loc["Tm"].reshape(R, C)
            o, z, gon = o_ref[pl.ds(r0, R), :], z_ref[pl.ds(r0, R), :], gon_ref[...]
            dob = dob_ref[pl.ds(r0, R), :]
            on, ro = _head_rms(o, gon)
            sz = jax.nn.sigmoid(z)
            dz_ref[pl.ds(r0, R), :] = (dob * on * (sz * (1.0 + z * (1.0 - sz)))).astype(dz_ref.dtype)
            do, dgon = _head_rms_bwd(o, ro, gon, dob * (z * sz))
            do_s[pl.ds(r0, R), :] = do
            dgon_ref[...] += dgon
            return carry

        lax.fori_loop(0, N // GDN_GROUP, local, 0)

        def step(t, dS):
            n = N - 1 - t
            r0 = pl.multiple_of(n * C, C)
            rows = pl.ds(r0, C)
            state = st_ref[0, 0, n]
            w_n, M_n, kd_n, do_n = w_s[rows, :], M_s[rows, :], kd_s[rows, :], do_s[rows, :]
            v_new = u_s[rows, :] - _dotm(w_n, state, NN)
            dv_new = _dotm(M_n, do_n, TN) + _dotm(kd_n, dS, NN)
            du_s[rows, :] = dv_new
            dw_s[rows, :] = -_dotm(dv_new, state, NT)
            dqg_s[rows, :] = _dotm(do_n, state, NT)
            dM_s[rows, :] = _dotm(do_n, v_new, NT)
            dkd_s[rows, :] = _dotm(v_new, dS, NT)
            gam = gam_s[n]
            dgl_s[n] = jnp.broadcast_to(jnp.sum(jnp.sum(dS * state, axis=1, keepdims=True), axis=0, keepdims=True), (1, LANES)) * gam
            return dS * gam + _dotm(qg_s[rows, :], do_n, TN) - _dotm(w_n, dv_new, TN)

        lax.fori_loop(0, N, step, jnp.zeros((hd, hd), f32))

        def finish(gi, carry):
            r0, L = group_inputs(gi, Tm_s)
            n = GDN_GROUP
            rows = pl.ds(r0, R)
            g3 = lambda ref: ref[rows, :].reshape(n, C, -1)
            du, dw, dqg, dkd, dM = g3(du_s), g3(dw_s), g3(dqg_s), g3(dkd_s), g3(dM_s)
            L["M"] = g3(M_s)
            TmT = jnp.swapaxes(L["Tm"], 1, 2)
            dTm = _bmm(du, L["vb"], B_NT) + _bmm(dw, L["kg"], B_NT)
            dvb = _bmm(TmT, du, B_NN)
            dkg = _bmm(TmT, dw, B_NN)
            dA = jnp.where(L["row"] > L["col"], -_bmm(_bmm(TmT, dTm, B_NN), TmT, B_NN), 0.0)
            dKK = dA * L["D"]
            dQK = dM * L["D"]
            dkb = _bmm(dKK, L["k"], B_NN) + dkg * L["Gam"]
            dk = (_bmm(dKK, L["kb"], B_TN) + _bmm(dQK, L["q"], B_TN) + dkd * L["kdec"] + L["beta"] * dkb)
            dq = (_bmm(dQK, L["k"], B_NN) + dqg * L["Gam"]) * (GDN_HEAD_DIM ** -0.5)
            E = dA * L["A"] + dM * L["M"]
            r = jnp.sum(dkd * L["kd"], axis=-1, keepdims=True)
            dgc = (jnp.sum(E, axis=2, keepdims=True) - jnp.sum(jnp.swapaxes(E, 1, 2), axis=2, keepdims=True)
                   + jnp.sum(dkg * L["kg"], axis=-1, keepdims=True) + jnp.sum(dqg * L["qg"], axis=-1, keepdims=True) - r)
            dgl = jnp.sum(r, axis=1, keepdims=True) + dgl_s[pl.ds(gi * n, n)][:, :, 0:1]
            rowc = lax.broadcasted_iota(jnp.int32, (n, C, 1), 1)
            dgc = dgc + jnp.where(rowc == C - 1, dgl, 0.0)
            dbeta = jnp.sum(dkb * L["k"], axis=-1, keepdims=True) + jnp.sum(dvb * L["v"], axis=-1, keepdims=True)
            dq_ref[rows, :] = dq.reshape(R, hd)
            dk_ref[rows, :] = dk.reshape(R, hd)
            dv_ref[rows, :] = (L["beta"] * dvb).reshape(R, hd)
            lane = lax.broadcasted_iota(jnp.int32, (R, LANES), 1)
            dgt_ref[rows, :] += (jnp.where(lane == SM_B + h, dbeta.reshape(R, 1), 0.0)
                                 + jnp.where(lane == SM_A + h, dgc.reshape(R, 1), 0.0))
            return carry

        lax.fori_loop(0, N // GDN_GROUP, finish, 0)

    blk = lambda off: pl.BlockSpec((S, LANES), lambda b, h: (b, off + h))
    rows = lambda: pltpu.VMEM((S, hd), f32)
    return pl.pallas_call(
        body, name="gdn_bwd", grid=(B, GDN_HEADS),
        in_specs=[blk(0), blk(GDN_HEADS), blk(2 * GDN_HEADS), pl.BlockSpec((S, LANES), lambda b, h: (b, 0)),
                  blk(COL_Z // LANES), pl.BlockSpec((1, hd), lambda b, h: (0, 0)), blk(0),
                  pl.BlockSpec((1, 1, N, hd, hd), lambda b, h: (b, h, 0, 0, 0)), blk(GDN_HEADS)],
        out_specs=[blk(0), blk(0), blk(0), pl.BlockSpec((S, LANES), lambda b, h: (b, 0)), blk(0),
                   pl.BlockSpec((1, hd), lambda b, h: (0, 0))],
        out_shape=[jax.ShapeDtypeStruct((T, GDN_WIDTH), f32), jax.ShapeDtypeStruct((T, GDN_WIDTH), f32),
                   jax.ShapeDtypeStruct((T, GDN_WIDTH), f32), jax.ShapeDtypeStruct((T, LANES), f32),
                   jax.ShapeDtypeStruct((T, GDN_WIDTH), MXU_DTYPE), jax.ShapeDtypeStruct((1, hd), f32)],
        scratch_shapes=[rows(), rows(), rows(), rows(), pltpu.VMEM((S, C), f32), pltpu.VMEM((N, 1, LANES), f32),
                        rows(), rows(), rows(), rows(), rows(), pltpu.VMEM((S, C), f32), pltpu.VMEM((N, 1, LANES), f32),
                        pltpu.VMEM((S, C), f32)],
        compiler_params=_cparams("arbitrary", "arbitrary"),
    )(G, G, G, gates, P, g_on, o_raw, states, d_oab)


IN_SPLIT = (0, 1536, 1544, 3080, 3088, 3600)


def align_w_in(w):
    s = IN_SPLIT
    pad = jnp.zeros((w.shape[0], IN_ALIGNED - IN_DIM), w.dtype)
    return jnp.concatenate([w[:, s[0]:s[1]], w[:, s[2]:s[3]], w[:, s[4]:s[5]], w[:, s[1]:s[2]], w[:, s[3]:s[4]], pad], axis=1)


def unalign_w_in(wa):
    return jnp.concatenate([wa[:, 0:1536], wa[:, COL_SMALL:COL_SMALL + 8], wa[:, 1536:3072],
                            wa[:, COL_SMALL + 8:COL_SMALL + 16], wa[:, 3072:3584]], axis=1)


def _lanes_vec(pieces):
    v = jnp.zeros((1, LANES), f32)
    for off, a in pieces:
        v = lax.dynamic_update_slice(v, a.astype(f32), (0, off))
    return v


def local_step(x, mem, target, w, sp, *, B):
    T = x.shape[0]
    S = T // B
    tile2 = lambda g: jnp.concatenate([g, g], axis=1)
    gq2, gk2, go2 = tile2(sp["fox_qnorm_g"]), tile2(sp["fox_knorm_g"]), tile2(sp["fox_onorm_g"])
    bias = _lanes_vec([(SM_F, sp["fox_f_bias"]), (SM_A, sp["gdn_dt_bias"])])
    alog = _lanes_vec([(SM_A, sp["gdn_A_log"])])

    h1 = rms_fwd(x, sp["norm_mix_g"], name="rms_mix")
    P = matmul(h1, w["wa"], name="mm_in", tn=IN_TILE)
    gates = gates_fwd(P, bias, alog, B=B)
    c = gates[:, SM_F:SM_F + FOX_HEADS].reshape(B, S, FOX_HEADS).transpose(0, 2, 1)
    ccol, crow = c[..., None], c[:, :, None, :]
    o_raw, o_a = fox_fwd(P, ccol, crow, gq2, gk2, go2, B=B)
    G = gdn_prep_fwd(P, w["conv_w"], B=B)
    ob_raw, o_b, states = gdn_fwd(G, gates, P, sp["gdn_onorm_g"], B=B)
    oab = jnp.concatenate([o_a, o_b], axis=1)
    x2 = matmul(oab, w["w_out"], residual=x, name="mm_out")
    hq = rms_fwd(x2, sp["norm_xattn_g"], name="rms_xattn")
    hm = rms_fwd(mem, sp["mem_norm_g"], name="rms_mem")
    cq = matmul(hq, w["w_cq"], name="mm_cq")
    ckv = matmul(hm, w["w_ckv"], name="mm_ckv")
    co = xattn_fwd(cq, ckv, sp["xattn_qnorm_g"], sp["xattn_knorm_g"], B=B)
    x3 = matmul(co, w["w_co"], residual=x2, name="mm_co")
    hf = rms_fwd(x3, sp["norm_mlp_g"], name="rms_mlp")
    a, act = matmul(hf, w["w_mlp1"], relu2_out=True, name="mm_mlp1")
    x4 = matmul(act, w["w_mlp2"], residual=x3, name="mm_mlp2")
    dy, loss = loss_head(x4, target)

    da = matmul(dy, w["w_mlp2"], tb=True, relu2_bwd_aux=a, out_dtype=MXU_DTYPE, name="mm_d_act")
    g_mlp2 = matmul(act, dy, ta=True, name="mm_g_mlp2")
    g_mlp1 = matmul(hf, da, ta=True, name="mm_g_mlp1")
    dhf = matmul(da, w["w_mlp1"], tb=True, name="mm_d_hf")
    dx3, g_norm_mlp = rms_bwd(x3, sp["norm_mlp_g"], dhf, dy, name="rms_mlp_bwd")
    dco = matmul(dx3, w["w_co"], tb=True, name="mm_d_co")
    g_co = matmul(co, dx3, ta=True, name="mm_g_co")
    dcq, dckv, g_xq, g_xk = xattn_bwd(cq, ckv, sp["xattn_qnorm_g"], sp["xattn_knorm_g"], dco, B=B)
    g_cq = matmul(hq, dcq, ta=True, name="mm_g_cq")
    dhq = matmul(dcq, w["w_cq"], tb=True, name="mm_d_hq")
    g_ckv = matmul(hm, dckv, ta=True, name="mm_g_ckv")
    dhm = matmul(dckv, w["w_ckv"], tb=True, name="mm_d_hm")
    _, g_mem_norm = rms_bwd(mem, sp["mem_norm_g"], dhm, None, name="rms_mem_bwd")
    dx2, g_norm_xattn = rms_bwd(x2, sp["norm_xattn_g"], dhq, dx3, name="rms_xattn_bwd")
    doab = matmul(dx2, w["w_out"], tb=True, name="mm_d_oab")
    g_out = matmul(oab, dx2, ta=True, name="mm_g_out")
    dq_f, dk_f, dv_f, dccol, dcrow, dgq2, dgk2, dgo2 = fox_bwd(P, ccol, crow, gq2, gk2, go2, o_raw, doab, B=B)
    dGq, dGk, dGv, dgt, dz, g_gdn_on = gdn_bwd(G, gates, P, sp["gdn_onorm_g"], ob_raw, states, doab, B=B)
    dPg, g_conv = gdn_prep_bwd(P, w["conv_w"], jnp.concatenate([dGq, dGk, dGv], axis=1), B=B)
    dc = (dccol[..., 0] + dcrow[:, :, 0, :]).transpose(0, 2, 1).reshape(T, FOX_HEADS)
    dgates = dgt + jnp.pad(dc, ((0, 0), (SM_F, LANES - SM_F - FOX_HEADS)))
    dsmall, par = gates_bwd(P, bias, alog, dgates, B=B)
    dP = jnp.concatenate([dq_f, dk_f, dv_f, dPg, dz, dsmall, jnp.zeros((T, IN_ALIGNED - COL_SMALL - LANES), MXU_DTYPE)], axis=1)
    g_wa = matmul(h1, dP, ta=True, name="mm_g_in", tn=IN_TILE)
    dh1 = matmul(dP, w["wa"], tb=True, name="mm_d_h1", tk=IN_TILE)
    dx, g_norm_mix = rms_bwd(x, sp["norm_mix_g"], dh1, dx2, name="rms_mix_bwd")

    fold = lambda g: g[:, :FOX_HEAD_DIM] + g[:, FOX_HEAD_DIM:]
    big = dict(w_in=unalign_w_in(g_wa), w_out=g_out, w_cq=g_cq, w_ckv=g_ckv, w_co=g_co, w_mlp1=g_mlp1, w_mlp2=g_mlp2)
    small = dict(norm_mix_g=g_norm_mix, fox_qnorm_g=fold(dgq2), fox_knorm_g=fold(dgk2),
                 fox_f_bias=par[0:1, SM_F:SM_F + FOX_HEADS], fox_onorm_g=fold(dgo2), gdn_conv_w=g_conv,
                 gdn_A_log=par[1:2, SM_A:SM_A + GDN_HEADS], gdn_dt_bias=par[0:1, SM_A:SM_A + GDN_HEADS],
                 gdn_onorm_g=g_gdn_on, norm_xattn_g=g_norm_xattn, mem_norm_g=g_mem_norm,
                 xattn_qnorm_g=g_xq, xattn_knorm_g=g_xk, norm_mlp_g=g_norm_mlp)
    return loss, dx, big, small


MESH_IDS = pl.DeviceIdType.MESH
N_CHIPS = 4
HBM_SPEC = pl.BlockSpec(memory_space=pltpu.HBM)
PACK_ROWS = 30720
PACK_HALF = PACK_ROWS // 2
PACK_BLOCK = 3072


def _place():
    return lax.axis_index("x"), lax.axis_index("y"), lax.axis_index("c")


def _other_chips(x, y):
    return [(1 - x, y), (x, 1 - y), (1 - x, 1 - y)]


def _remote(src, dst, send_sem, recv_sem, to):
    return pltpu.make_async_remote_copy(src_ref=src, dst_ref=dst, send_sem=send_sem, recv_sem=recv_sem,
                                        device_id=to, device_id_type=MESH_IDS)


def all_gather_shards(packed):
    half = PACK_HALF

    def body(src_ref, out_ref, send_sems, recv_sems, local_sem):
        x, y, c = _place()
        me_chip = 2 * x + y
        sibling = (x, y, 1 - c)
        chips = _other_chips(x, y)
        mine = pltpu.make_async_copy(src_ref, out_ref.at[me_chip], local_sem)
        mine.start()

        def rows(chip, core):
            return out_ref.at[chip, pl.ds(core * half, half), :]

        sends = [_remote(src_ref.at[pl.ds(c * half, half), :], rows(me_chip, c), send_sems.at[j], recv_sems.at[j], (px, py, c))
                 for j, (px, py) in enumerate(chips)]
        for cp in sends:
            cp.start()
        passed = []
        for j, (px, py) in enumerate(chips):
            theirs = rows(2 * px + py, c)
            _remote(theirs, theirs, send_sems.at[j], recv_sems.at[j], (px, py, c)).wait_recv()
            cp = _remote(theirs, theirs, send_sems.at[3 + j], recv_sems.at[3 + j], sibling)
            cp.start()
            passed.append(cp)
        for j, (px, py) in enumerate(chips):
            theirs = rows(2 * px + py, 1 - c)
            _remote(theirs, theirs, send_sems.at[3 + j], recv_sems.at[3 + j], sibling).wait_recv()
        for cp in sends + passed:
            cp.wait_send()
        mine.wait()

    return pl.pallas_call(
        body, name="all_gather_shards", in_specs=[HBM_SPEC], out_specs=HBM_SPEC,
        out_shape=jax.ShapeDtypeStruct((N_CHIPS,) + packed.shape, packed.dtype),
        scratch_shapes=[pltpu.SemaphoreType.DMA((6,)), pltpu.SemaphoreType.DMA((6,)), pltpu.SemaphoreType.DMA(())],
    )(packed)


def exchange_core_halves(G):
    half = PACK_HALF

    def body(g_ref, land_ref, send_sem, recv_sem):
        x, y, c = _place()
        cp = _remote(g_ref.at[:, pl.ds((1 - c) * half, half), :], land_ref, send_sem, recv_sem, (x, y, 1 - c))
        cp.start()
        cp.wait()

    return pl.pallas_call(
        body, name="exchange_core_halves", in_specs=[HBM_SPEC], out_specs=HBM_SPEC,
        out_shape=jax.ShapeDtypeStruct((N_CHIPS, half, LANES), G.dtype),
        scratch_shapes=[pltpu.SemaphoreType.DMA(()), pltpu.SemaphoreType.DMA(())],
    )(G)


def add_core_halves(G, land, core):
    nb = PACK_HALF // PACK_BLOCK

    def body(c_ref, g_ref, l_ref, o_ref):
        o_ref[...] = (g_ref[...].astype(f32) + l_ref[...].astype(f32)).astype(o_ref.dtype)

    blk = (1, PACK_BLOCK, LANES)
    return pl.pallas_call(
        body, name="add_core_halves",
        grid_spec=pltpu.PrefetchScalarGridSpec(
            num_scalar_prefetch=1, grid=(N_CHIPS, nb),
            in_specs=[pl.BlockSpec(blk, lambda k, i, c_ref: (k, c_ref[0] * nb + i, 0)),
                      pl.BlockSpec(blk, lambda k, i, c_ref: (k, i, 0))],
            out_specs=pl.BlockSpec(blk, lambda k, i, c_ref: (k, i, 0))),
        out_shape=jax.ShapeDtypeStruct(land.shape, land.dtype),
        compiler_params=_cparams("parallel", "parallel"),
    )(core, G, land)


def scatter_to_chips(part):
    def body(p_ref, land_ref, send_sems, recv_sems, local_sem):
        x, y, c = _place()
        me_chip = 2 * x + y
        chips = _other_chips(x, y)
        mine = pltpu.make_async_copy(p_ref.at[me_chip], land_ref.at[me_chip], local_sem)
        mine.start()
        sends = [_remote(p_ref.at[2 * px + py], land_ref.at[me_chip], send_sems.at[j], recv_sems.at[j], (px, py, c))
                 for j, (px, py) in enumerate(chips)]
        for cp in sends:
            cp.start()
        for j, (px, py) in enumerate(chips):
            slot = land_ref.at[2 * px + py]
            _remote(slot, slot, send_sems.at[j], recv_sems.at[j], (px, py, c)).wait_recv()
        for cp in sends:
            cp.wait_send()
        mine.wait()

    return pl.pallas_call(
        body, name="scatter_to_chips", in_specs=[HBM_SPEC], out_specs=HBM_SPEC,
        out_shape=jax.ShapeDtypeStruct(part.shape, part.dtype),
        scratch_shapes=[pltpu.SemaphoreType.DMA((3,)), pltpu.SemaphoreType.DMA((3,)), pltpu.SemaphoreType.DMA(())],
    )(part)


def sum_chips(land):
    nb = PACK_HALF // PACK_BLOCK

    def body(l_ref, o_ref):
        o_ref[...] = ((l_ref[0].astype(f32) + l_ref[1].astype(f32)) + l_ref[2].astype(f32)) + l_ref[3].astype(f32)

    return pl.pallas_call(
        body, name="sum_chips", grid=(nb,),
        in_specs=[pl.BlockSpec((N_CHIPS, PACK_BLOCK, LANES), lambda i: (0, i, 0))],
        out_specs=pl.BlockSpec((PACK_BLOCK, LANES), lambda i: (i, 0)),
        out_shape=jax.ShapeDtypeStruct((PACK_HALF, LANES), f32),
        compiler_params=_cparams("parallel"),
    )(land)


def join_core_halves(red):
    half = PACK_HALF

    def body(r_ref, out_ref, send_sem, recv_sem, local_sem):
        x, y, c = _place()
        mine = pltpu.make_async_copy(r_ref, out_ref.at[pl.ds(c * half, half), :], local_sem)
        mine.start()
        cp = _remote(r_ref, out_ref.at[pl.ds(c * half, half), :], send_sem, recv_sem, (x, y, 1 - c))
        cp.start()
        other = out_ref.at[pl.ds((1 - c) * half, half), :]
        _remote(other, other, send_sem, recv_sem, (x, y, 1 - c)).wait_recv()
        cp.wait_send()
        mine.wait()

    return pl.pallas_call(
        body, name="join_core_halves", in_specs=[HBM_SPEC], out_specs=HBM_SPEC,
        out_shape=jax.ShapeDtypeStruct((PACK_ROWS, LANES), red.dtype),
        scratch_shapes=[pltpu.SemaphoreType.DMA(()), pltpu.SemaphoreType.DMA(()), pltpu.SemaphoreType.DMA(())],
    )(red)


N_DEV = 8


def all_reduce_small(v):
    def body(src_ref, out_ref, land_ref, send_sems, recv_sems):
        x, y, c = _place()
        me = 4 * x + 2 * y + c
        copies = []
        for r in range(1, N_DEV):
            peer = ((1 - x) if r & 4 else x, (1 - y) if r & 2 else y, (1 - c) if r & 1 else c)
            copies.append(_remote(src_ref, land_ref.at[r], send_sems.at[r - 1], recv_sems.at[r - 1], peer))
        for cp in copies:
            cp.start()
        land_ref[0] = src_ref[...]
        for cp in copies:
            cp.wait()
        acc = land_ref[me]
        for d in range(1, N_DEV):
            acc = acc + land_ref[jnp.bitwise_xor(me, d)]
        out_ref[...] = acc

    vm = pl.BlockSpec(memory_space=pltpu.VMEM)
    return pl.pallas_call(
        body, name="all_reduce_small", in_specs=[vm], out_specs=vm,
        out_shape=jax.ShapeDtypeStruct(v.shape, v.dtype),
        scratch_shapes=[pltpu.VMEM((N_DEV,) + v.shape, v.dtype),
                        pltpu.SemaphoreType.DMA((N_DEV - 1,)), pltpu.SemaphoreType.DMA((N_DEV - 1,))],
    )(v)


def adamw(w, g, m, v, *, name, tr):
    R, C = w.shape
    tr = min(tr, R)

    def body(w_ref, g_ref, m_ref, v_ref, d_ref, nm_ref, nv_ref):
        gv = g_ref[...]
        nm = ADAM_B1 * m_ref[...] + (1.0 - ADAM_B1) * gv
        nv = ADAM_B2 * v_ref[...] + (1.0 - ADAM_B2) * jnp.square(gv)
        m_hat = nm / (1.0 - ADAM_B1 ** ADAM_STEP)
        v_hat = nv / (1.0 - ADAM_B2 ** ADAM_STEP)
        d_ref[...] = -ADAM_LR * (m_hat / (jnp.sqrt(v_hat) + ADAM_EPS) + ADAM_WD * w_ref[...])
        nm_ref[...] = nm
        nv_ref[...] = nv

    blk = pl.BlockSpec((tr, C), lambda i: (i, 0))
    out = jax.ShapeDtypeStruct((R, C), f32)
    return pl.pallas_call(
        body, name=name, grid=(R // tr,), in_specs=[blk] * 4, out_specs=[blk] * 3, out_shape=[out] * 3,
        compiler_params=_cparams("parallel"),
    )(w, g, m, v)


BIG_SHARDS = (("w_in", (1024, 900), True), ("w_out", (256, 1024), False), ("w_cq", (256, 512), False),
              ("w_ckv", (256, 1024), False), ("w_co", (512, 256), True), ("w_mlp1", (1024, 1024), True),
              ("w_mlp2", (1024, 1024), False))
CONV_SHARD = (CONV_WIDTH, 3 * GDN_WIDTH // N_CHIPS)
SMALL_DIMS = (("norm_mix_g", 1024), ("fox_qnorm_g", 64), ("fox_knorm_g", 64), ("fox_f_bias", 8), ("fox_onorm_g", 64),
              ("gdn_A_log", 4), ("gdn_dt_bias", 4), ("gdn_onorm_g", 128), ("norm_xattn_g", 1024), ("mem_norm_g", 1024),
              ("xattn_qnorm_g", 128), ("xattn_knorm_g", 128), ("norm_mlp_g", 1024))
WEIGHT_ORDER = ("norm_mix_g", "w_in", "fox_qnorm_g", "fox_knorm_g", "fox_f_bias", "fox_onorm_g", "gdn_conv_w", "gdn_A_log",
                "gdn_dt_bias", "gdn_onorm_g", "w_out", "norm_xattn_g", "mem_norm_g", "w_cq", "w_ckv", "xattn_qnorm_g",
                "xattn_knorm_g", "w_co", "norm_mlp_g", "w_mlp1", "w_mlp2")


def _pack_rows(pieces, rows, lead=()):
    flat = []
    for p in pieces:
        p = p.reshape(lead + (-1,))
        pad = (-p.shape[-1]) % LANES
        flat.append(jnp.pad(p, [(0, 0)] * len(lead) + [(0, pad)]) if pad else p)
    cat = jnp.concatenate(flat, axis=-1)
    cat = jnp.pad(cat, [(0, 0)] * len(lead) + [(0, rows * LANES - cat.shape[-1])])
    return cat.reshape(lead + (rows, LANES))


def _unpack_rows(buf, sizes, lead=()):
    flat = buf.reshape(lead + (-1,))
    out, off = [], 0
    for n in sizes:
        out.append(flat[..., off:off + n])
        off += n + (-n) % LANES
    return out


def _conv_to_wire(conv):
    return lax.bitcast_convert_type(conv, bf16)


def _conv_from_wire(wire):
    return lax.bitcast_convert_type(wire, f32)


SMALL_ROWS = 96
SMALL_ADAM_ROWS = 56


def kernel(x, mem, norm_mix_g, w_in, fox_qnorm_g, fox_knorm_g, fox_f_bias, fox_onorm_g, gdn_conv_w, gdn_A_log, gdn_dt_bias, gdn_onorm_g, w_out, norm_xattn_g, mem_norm_g, w_cq, w_ckv, xattn_qnorm_g, xattn_knorm_g, w_co, norm_mlp_g, w_mlp1, w_mlp2, loss_target, m_norm_mix_g, m_w_in, m_fox_qnorm_g, m_fox_knorm_g, m_fox_f_bias, m_fox_onorm_g, m_gdn_conv_w, m_gdn_A_log, m_gdn_dt_bias, m_gdn_onorm_g, m_w_out, m_norm_xattn_g, m_mem_norm_g, m_w_cq, m_w_ckv, m_xattn_qnorm_g, m_xattn_knorm_g, m_w_co, m_norm_mlp_g, m_w_mlp1, m_w_mlp2, v_norm_mix_g, v_w_in, v_fox_qnorm_g, v_fox_knorm_g, v_fox_f_bias, v_fox_onorm_g, v_gdn_conv_w, v_gdn_A_log, v_gdn_dt_bias, v_gdn_onorm_g, v_w_out, v_norm_xattn_g, v_mem_norm_g, v_w_cq, v_w_ckv, v_xattn_qnorm_g, v_xattn_knorm_g, v_w_co, v_norm_mlp_g, v_w_mlp1, v_w_mlp2):
    wts = dict(norm_mix_g=norm_mix_g, w_in=w_in, fox_qnorm_g=fox_qnorm_g, fox_knorm_g=fox_knorm_g, fox_f_bias=fox_f_bias,
               fox_onorm_g=fox_onorm_g, gdn_conv_w=gdn_conv_w, gdn_A_log=gdn_A_log, gdn_dt_bias=gdn_dt_bias,
               gdn_onorm_g=gdn_onorm_g, w_out=w_out, norm_xattn_g=norm_xattn_g, mem_norm_g=mem_norm_g, w_cq=w_cq, w_ckv=w_ckv,
               xattn_qnorm_g=xattn_qnorm_g, xattn_knorm_g=xattn_knorm_g, w_co=w_co, norm_mlp_g=norm_mlp_g, w_mlp1=w_mlp1,
               w_mlp2=w_mlp2)
    mom = dict(norm_mix_g=m_norm_mix_g, w_in=m_w_in, fox_qnorm_g=m_fox_qnorm_g, fox_knorm_g=m_fox_knorm_g,
               fox_f_bias=m_fox_f_bias, fox_onorm_g=m_fox_onorm_g, gdn_conv_w=m_gdn_conv_w, gdn_A_log=m_gdn_A_log,
               gdn_dt_bias=m_gdn_dt_bias, gdn_onorm_g=m_gdn_onorm_g, w_out=m_w_out, norm_xattn_g=m_norm_xattn_g,
               mem_norm_g=m_mem_norm_g, w_cq=m_w_cq, w_ckv=m_w_ckv, xattn_qnorm_g=m_xattn_qnorm_g,
               xattn_knorm_g=m_xattn_knorm_g, w_co=m_w_co, norm_mlp_g=m_norm_mlp_g, w_mlp1=m_w_mlp1, w_mlp2=m_w_mlp2)
    var = dict(norm_mix_g=v_norm_mix_g, w_in=v_w_in, fox_qnorm_g=v_fox_qnorm_g, fox_knorm_g=v_fox_knorm_g,
               fox_f_bias=v_fox_f_bias, fox_onorm_g=v_fox_onorm_g, gdn_conv_w=v_gdn_conv_w, gdn_A_log=v_gdn_A_log,
               gdn_dt_bias=v_gdn_dt_bias, gdn_onorm_g=v_gdn_onorm_g, w_out=v_w_out, norm_xattn_g=v_norm_xattn_g,
               mem_norm_g=v_mem_norm_g, w_cq=v_w_cq, w_ckv=v_w_ckv, xattn_qnorm_g=v_xattn_qnorm_g,
               xattn_knorm_g=v_xattn_knorm_g, w_co=v_w_co, norm_mlp_g=v_norm_mlp_g, w_mlp1=v_w_mlp1, w_mlp2=v_w_mlp2)
    B, S, D = x.shape
    T = B * S
    big_sizes = [r * c for _, (r, c), _ in BIG_SHARDS]
    conv_wire = 2 * CONV_SHARD[0] * CONV_SHARD[1]

    shards = [wts[n][0].astype(MXU_DTYPE) for n, _, _ in BIG_SHARDS] + [_conv_to_wire(gdn_conv_w[0])]
    gathered = all_gather_shards(_pack_rows(shards, PACK_ROWS))
    pieces = _unpack_rows(gathered, big_sizes + [conv_wire], lead=(N_CHIPS,))
    full = {}
    for (n, (r, c), by_cols), p in zip(BIG_SHARDS, pieces):
        p = p.reshape(N_CHIPS, r, c)
        full[n] = p.transpose(1, 0, 2).reshape(r, N_CHIPS * c) if by_cols else p.reshape(N_CHIPS * r, c)
    conv_full = _conv_from_wire(pieces[-1].reshape((N_CHIPS,) + CONV_SHARD + (2,)))
    conv_full = conv_full.transpose(1, 0, 2).reshape(CONV_WIDTH, 3 * GDN_WIDTH)
    w = dict(wa=align_w_in(full["w_in"]), w_out=full["w_out"], w_cq=full["w_cq"], w_ckv=full["w_ckv"], w_co=full["w_co"],
             w_mlp1=full["w_mlp1"], w_mlp2=full["w_mlp2"], conv_w=conv_full)
    sp = {n: wts[n] for n, _ in SMALL_DIMS}

    loss_part, grad_x, g_big, g_small = local_step(x.reshape(T, D), mem.reshape(-1, D), loss_target.reshape(T, D), w, sp, B=B)

    small_pieces = [g_small[n] for n, _ in SMALL_DIMS] + [g_small["gdn_conv_w"], loss_part]
    small_sizes = [d for _, d in SMALL_DIMS] + [CONV_WIDTH * 3 * GDN_WIDTH, LANES]
    red_small = _unpack_rows(all_reduce_small(_pack_rows(small_pieces, SMALL_ROWS)), small_sizes)
    grads = {n: p.reshape(1, d) for (n, d), p in zip(SMALL_DIMS, red_small)}
    chip = 2 * lax.axis_index("x") + lax.axis_index("y")
    conv_grad = lax.dynamic_slice(red_small[-2].reshape(CONV_WIDTH, 3 * GDN_WIDTH), (0, chip * CONV_SHARD[1]), CONV_SHARD)
    grads["gdn_conv_w"] = conv_grad.reshape((1,) + CONV_SHARD)
    loss = red_small[-1][0]

    by_chip = []
    for n, (r, c), by_cols in BIG_SHARDS:
        g = g_big[n]
        g = g.reshape(r, N_CHIPS, c).transpose(1, 0, 2) if by_cols else g.reshape(N_CHIPS, r, c)
        by_chip.append(g.astype(WIRE_DTYPE))
    G = _pack_rows(by_chip, PACK_ROWS, lead=(N_CHIPS,))
    core = lax.axis_index("c").astype(jnp.int32).reshape(1)
    chip_part = add_core_halves(G, exchange_core_halves(G), core)
    reduced = join_core_halves(sum_chips(scatter_to_chips(chip_part)))
    for (n, (r, c), _), p in zip(BIG_SHARDS, _unpack_rows(reduced, big_sizes)):
        grads[n] = p.reshape(1, r, c)

    delta, new_m, new_v = {}, {}, {}
    for n, (r, c), _ in BIG_SHARDS:
        d, nm, nv = adamw(wts[n][0], grads[n][0], mom[n][0], var[n][0], name="adamw_" + n, tr=256)
        delta[n], new_m[n], new_v[n] = d[None], nm[None], nv[None]
    small_names = [n for n, _ in SMALL_DIMS] + ["gdn_conv_w"]
    small_sz = [d for _, d in SMALL_DIMS] + [CONV_SHARD[0] * CONV_SHARD[1]]
    packed4 = [_pack_rows([src[n] for n in small_names], SMALL_ADAM_ROWS) for src in (wts, grads, mom, var)]
    outs = adamw(*packed4, name="adamw_small", tr=SMALL_ADAM_ROWS)
    for dst, buf in zip((delta, new_m, new_v), outs):
        for n, p in zip(small_names, _unpack_rows(buf, small_sz)):
            dst[n] = p.reshape(wts[n].shape)

    return (loss, grad_x.reshape(B, S, D), *[grads[n] for n in WEIGHT_ORDER], *[delta[n] for n in WEIGHT_ORDER],
            *[new_m[n] for n in WEIGHT_ORDER], *[new_v[n] for n in WEIGHT_ORDER])
```

```python
import functools

import jax
import jax.numpy as jnp
import numpy as np
from jax import lax
from jax.experimental import pallas as pl
from jax.experimental.pallas import tpu as pltpu

f32 = jnp.float32
bf16 = jnp.bfloat16
MXU_DTYPE = jnp.bfloat16
WIRE_DTYPE = jnp.bfloat16
INV_PRECISION = lax.Precision.HIGH

D_MODEL = 1024
FOX_HEADS = 8
FOX_HEAD_DIM = 64
FOX_WIDTH = 512
GDN_HEADS = 4
GDN_HEAD_DIM = 128
GDN_WIDTH = 512
CONV_WIDTH = 4
GDN_CHUNK = 64
XATTN_HEADS = 4
XATTN_HEAD_DIM = 128
XATTN_WIDTH = 512
D_FF = 4096
IN_DIM = 3600
EPS = 1e-6
NEG_INF = -1e30
LANES = 128
ADAM_LR = 0.001
ADAM_B1 = 0.9
ADAM_B2 = 0.999
ADAM_EPS = 1e-08
ADAM_WD = 0.01
ADAM_STEP = 10
VMEM_LIMIT = 48 * 1024 * 1024

COL_FOX = 0
COL_GDN = 1536
COL_Z = 3072
COL_SMALL = 3584
IN_ALIGNED = 3840
IN_TILE = 768
SM_F = 0
SM_B = 8
SM_A = 12


def _cparams(*sem):
    return pltpu.CompilerParams(dimension_semantics=sem, vmem_limit_bytes=VMEM_LIMIT)


def _mx(v):
    return v.astype(MXU_DTYPE)


def _dot(a, b, dims, precision=None):
    return lax.dot_general(a, b, (dims, ((), ())), preferred_element_type=f32, precision=precision)


def _dotm(a, b, dims):
    return _dot(_mx(a), _mx(b), dims)


NN = ((1,), (0,))
NT = ((1,), (1,))
TN = ((0,), (0,))


def matmul(a, b, *, name, ta=False, tb=False, residual=None, relu2_out=False, relu2_bwd_aux=None,
           out_dtype=f32, tm=1024, tn=1024, tk=1024):
    M, K = (a.shape[1], a.shape[0]) if ta else a.shape
    N = b.shape[0] if tb else b.shape[1]
    tm, tn, tk = min(tm, M), min(tn, N), min(tk, K)
    assert M % tm == 0 and N % tn == 0 and K % tk == 0, (name, M, N, K)
    nk = K // tk
    has_res = residual is not None
    has_aux = relu2_bwd_aux is not None

    def body(*refs):
        a_ref, b_ref = refs[0], refs[1]
        pos = 2
        res_ref = aux_ref = None
        if has_res:
            res_ref = refs[pos]
            pos += 1
        if has_aux:
            aux_ref = refs[pos]
            pos += 1
        o_ref = refs[pos]
        pos += 1
        act_ref = None
        if relu2_out:
            act_ref = refs[pos]
            pos += 1
        acc_ref = refs[pos]
        k = pl.program_id(2)

        @pl.when(k == 0)
        def _():
            acc_ref[...] = jnp.zeros_like(acc_ref)

        dims = ((0,) if ta else (1,), (1,) if tb else (0,))
        acc_ref[...] += _dot(_mx(a_ref[...]), _mx(b_ref[...]), dims)

        @pl.when(k == nk - 1)
        def _():
            r = acc_ref[...]
            if has_res:
                r = r + res_ref[...]
            if has_aux:
                r = r * (2.0 * jnp.maximum(aux_ref[...], 0.0))
            o_ref[...] = r.astype(o_ref.dtype)
            if relu2_out:
                act_ref[...] = jnp.square(jnp.maximum(r, 0.0)).astype(act_ref.dtype)

    a_spec = pl.BlockSpec((tk, tm), lambda i, j, k: (k, i)) if ta else pl.BlockSpec((tm, tk), lambda i, j, k: (i, k))
    b_spec = pl.BlockSpec((tn, tk), lambda i, j, k: (j, k)) if tb else pl.BlockSpec((tk, tn), lambda i, j, k: (k, j))
    o_spec = pl.BlockSpec((tm, tn), lambda i, j, k: (i, j))
    in_specs, args = [a_spec, b_spec], [a, b]
    if has_res:
        in_specs.append(o_spec)
        args.append(residual)
    if has_aux:
        in_specs.append(o_spec)
        args.append(relu2_bwd_aux)
    out_shape = [jax.ShapeDtypeStruct((M, N), out_dtype)]
    out_specs = [o_spec]
    if relu2_out:
        out_shape.append(jax.ShapeDtypeStruct((M, N), MXU_DTYPE))
        out_specs.append(o_spec)
    res = pl.pallas_call(
        body, name=name, grid=(M // tm, N // tn, nk), in_specs=in_specs, out_specs=out_specs, out_shape=out_shape,
        scratch_shapes=[pltpu.VMEM((tm, tn), f32)],
        compiler_params=_cparams("parallel", "parallel", "arbitrary"),
    )(*args)
    return res if relu2_out else res[0]


def rms_fwd(x, g, *, name, tr=512):
    R, D = x.shape
    tr = min(tr, R)

    def body(x_ref, g_ref, o_ref):
        xv = x_ref[...]
        y = xv * lax.rsqrt(jnp.mean(xv * xv, axis=-1, keepdims=True) + EPS)
        o_ref[...] = (y * g_ref[...]).astype(o_ref.dtype)

    return pl.pallas_call(
        body, name=name, grid=(R // tr,),
        in_specs=[pl.BlockSpec((tr, D), lambda i: (i, 0)), pl.BlockSpec((1, D), lambda i: (0, 0))],
        out_specs=pl.BlockSpec((tr, D), lambda i: (i, 0)),
        out_shape=jax.ShapeDtypeStruct((R, D), MXU_DTYPE),
        compiler_params=_cparams("parallel"),
    )(x, g)


def rms_bwd(x, g, dh, residual, *, name, tr=512):
    R, D = x.shape
    tr = min(tr, R)
    has_res = residual is not None

    def body(*refs):
        if has_res:
            x_ref, g_ref, dh_ref, res_ref, dx_ref, dg_ref = refs
        else:
            x_ref, g_ref, dh_ref, dx_ref, dg_ref = refs
        xv = x_ref[...]
        rstd = lax.rsqrt(jnp.mean(xv * xv, axis=-1, keepdims=True) + EPS)
        xhat = xv * rstd
        dh = dh_ref[...].astype(f32)
        gd = dh * g_ref[...]
        dx = rstd * (gd - xhat * jnp.mean(gd * xhat, axis=-1, keepdims=True))
        if has_res:
            dx = dx + res_ref[...]
        dx_ref[...] = dx

        @pl.when(pl.program_id(0) == 0)
        def _():
            dg_ref[...] = jnp.zeros_like(dg_ref)

        dg_ref[...] += jnp.sum(dh * xhat, axis=0, keepdims=True)

    row = pl.BlockSpec((tr, D), lambda i: (i, 0))
    vec = pl.BlockSpec((1, D), lambda i: (0, 0))
    in_specs = [row, vec, row] + ([row] if has_res else [])
    args = [x, g, dh] + ([residual] if has_res else [])
    return pl.pallas_call(
        body, name=name, grid=(R // tr,), in_specs=in_specs, out_specs=[row, vec],
        out_shape=[jax.ShapeDtypeStruct((R, D), f32), jax.ShapeDtypeStruct((1, D), f32)],
        compiler_params=_cparams("arbitrary"),
    )(*args)


def loss_head(y, target, *, tr=512):
    R, D = y.shape
    tr = min(tr, R)

    def body(y_ref, t_ref, dy_ref, loss_ref):
        e = y_ref[...] - t_ref[...]
        dy_ref[...] = e * (1.0 / D)

        @pl.when(pl.program_id(0) == 0)
        def _():
            loss_ref[...] = jnp.zeros_like(loss_ref)

        part = 0.5 * jnp.sum(jnp.mean(e * e, axis=-1, keepdims=True), axis=0, keepdims=True)
        loss_ref[...] += jnp.broadcast_to(part, loss_ref.shape)

    row = pl.BlockSpec((tr, D), lambda i: (i, 0))
    return pl.pallas_call(
        body, name="loss_head", grid=(R // tr,), in_specs=[row, row],
        out_specs=[row, pl.BlockSpec((1, LANES), lambda i: (0, 0))],
        out_shape=[jax.ShapeDtypeStruct((R, D), f32), jax.ShapeDtypeStruct((1, LANES), f32)],
        compiler_params=_cparams("arbitrary"),
    )(y, target)


def _head_rms(v, g):
    r = lax.rsqrt(jnp.mean(v * v, axis=-1, keepdims=True) + EPS)
    return v * r * g, r


def _head_rms_bwd(v, r, g, dn):
    vhat = v * r
    gd = dn * g
    dv = r * (gd - vhat * jnp.mean(gd * vhat, axis=-1, keepdims=True))
    return dv, jnp.sum(dn * vhat, axis=0, keepdims=True)


def _softmax_rows(s):
    m = jnp.max(s, axis=-1, keepdims=True)
    e = jnp.exp(s - m)
    return e / jnp.sum(e, axis=-1, keepdims=True)


def xattn_fwd(cq, ckv, gq, gk, *, B, tq=512):
    T = cq.shape[0]
    S = T // B
    M = ckv.shape[0] // B
    tq = min(tq, S)
    nq = S // tq
    scale = XATTN_HEAD_DIM ** -0.5

    def body(q_ref, k_ref, v_ref, gq_ref, gk_ref, o_ref):
        qn, _ = _head_rms(q_ref[...], gq_ref[...])
        kn, _ = _head_rms(k_ref[...], gk_ref[...])
        p = _softmax_rows(_dot(_mx(qn), _mx(kn), NT) * scale)
        o_ref[...] = _dot(_mx(p), _mx(v_ref[...]), NN).astype(o_ref.dtype)

    hd = XATTN_HEAD_DIM
    vec = pl.BlockSpec((1, hd), lambda b, h, i: (0, 0))
    return pl.pallas_call(
        body, name="xattn_fwd", grid=(B, XATTN_HEADS, nq),
        in_specs=[pl.BlockSpec((tq, hd), lambda b, h, i: (b * nq + i, h)),
                  pl.BlockSpec((M, hd), lambda b, h, i: (b, h)),
                  pl.BlockSpec((M, hd), lambda b, h, i: (b, XATTN_HEADS + h)), vec, vec],
        out_specs=pl.BlockSpec((tq, hd), lambda b, h, i: (b * nq + i, h)),
        out_shape=jax.ShapeDtypeStruct((T, XATTN_WIDTH), MXU_DTYPE),
        compiler_params=_cparams("parallel", "parallel", "parallel"),
    )(cq, ckv, ckv, gq, gk)


def xattn_bwd(cq, ckv, gq, gk, dco, *, B, tq=512):
    T = cq.shape[0]
    S = T // B
    M = ckv.shape[0] // B
    tq = min(tq, S)
    nq = S // tq
    scale = XATTN_HEAD_DIM ** -0.5
    hd = XATTN_HEAD_DIM

    def body(q_ref, k_ref, v_ref, gq_ref, gk_ref, do_ref, dq_ref, dk_ref, dv_ref, dgq_ref, dgk_ref, dkn_acc, dv_acc):
        b, h, i = pl.program_id(0), pl.program_id(1), pl.program_id(2)

        @pl.when((b == 0) & (h == 0) & (i == 0))
        def _():
            dgq_ref[...] = jnp.zeros_like(dgq_ref)
            dgk_ref[...] = jnp.zeros_like(dgk_ref)

        @pl.when(i == 0)
        def _():
            dkn_acc[...] = jnp.zeros_like(dkn_acc)
            dv_acc[...] = jnp.zeros_like(dv_acc)

        q, k, v = q_ref[...], k_ref[...], v_ref[...]
        gqv, gkv = gq_ref[...], gk_ref[...]
        qn, rq = _head_rms(q, gqv)
        kn, rk = _head_rms(k, gkv)
        p = _softmax_rows(_dot(_mx(qn), _mx(kn), NT) * scale)
        do = do_ref[...]
        dv_acc[...] += _dot(_mx(p), _mx(do), TN)
        dp = _dot(_mx(do), _mx(v), NT)
        ds = p * (dp - jnp.sum(dp * p, axis=-1, keepdims=True)) * scale
        dqn = _dot(_mx(ds), _mx(kn), NN)
        dkn_acc[...] += _dot(_mx(ds), _mx(qn), TN)
        dq, dgq = _head_rms_bwd(q, rq, gqv, dqn)
        dq_ref[...] = dq.astype(dq_ref.dtype)
        dgq_ref[...] += dgq

        @pl.when(i == nq - 1)
        def _():
            dk, dgk = _head_rms_bwd(k, rk, gkv, dkn_acc[...])
            dk_ref[...] = dk.astype(dk_ref.dtype)
            dv_ref[...] = dv_acc[...].astype(dv_ref.dtype)
            dgk_ref[...] += dgk

    vec = pl.BlockSpec((1, hd), lambda b, h, i: (0, 0))
    qspec = pl.BlockSpec((tq, hd), lambda b, h, i: (b * nq + i, h))
    kspec = pl.BlockSpec((M, hd), lambda b, h, i: (b, h))
    vspec = pl.BlockSpec((M, hd), lambda b, h, i: (b, XATTN_HEADS + h))
    dq, dk, dv, dgq, dgk = pl.pallas_call(
        body, name="xattn_bwd", grid=(B, XATTN_HEADS, nq),
        in_specs=[qspec, kspec, vspec, vec, vec, qspec],
        out_specs=[qspec, kspec, kspec, vec, vec],
        out_shape=[jax.ShapeDtypeStruct((T, XATTN_WIDTH), MXU_DTYPE),
                   jax.ShapeDtypeStruct((B * M, XATTN_WIDTH), MXU_DTYPE),
                   jax.ShapeDtypeStruct((B * M, XATTN_WIDTH), MXU_DTYPE),
                   jax.ShapeDtypeStruct((1, hd), f32), jax.ShapeDtypeStruct((1, hd), f32)],
        scratch_shapes=[pltpu.VMEM((M, hd), f32), pltpu.VMEM((M, hd), f32)],
        compiler_params=_cparams("arbitrary", "arbitrary", "arbitrary"),
    )(cq, ckv, ckv, gq, gk, dco)
    return dq, jnp.concatenate([dk, dv], axis=1), dgq, dgk


FOX_PAIRS = FOX_HEADS // 2


def _fox_scores(qn, kn, ccol, crow, q0, tq, S, scale):
    s = _dot(_mx(qn), _mx(kn), NT) * scale + ccol - crow
    qpos = q0 + lax.broadcasted_iota(jnp.int32, (tq, S), 0)
    kpos = lax.broadcasted_iota(jnp.int32, (tq, S), 1)
    return jnp.where(kpos <= qpos, s, NEG_INF)


def fox_fwd(P, ccol, crow, gq, gk, go, *, B, tq=256):
    T = P.shape[0]
    S = T // B
    tq = min(tq, S)
    nq = S // tq
    hd = FOX_HEAD_DIM
    scale = hd ** -0.5

    def body(q_ref, k_ref, v_ref, ccol_ref, crow_ref, gq_ref, gk_ref, go_ref, o_ref, oa_ref):
        q0 = pl.program_id(2) * tq
        for e in range(2):
            sl = slice(e * hd, (e + 1) * hd)
            qn, _ = _head_rms(q_ref[:, sl], gq_ref[:, sl])
            kn, _ = _head_rms(k_ref[:, sl], gk_ref[:, sl])
            p = _softmax_rows(_fox_scores(qn, kn, ccol_ref[0, e], crow_ref[0, e], q0, tq, S, scale))
            o = _dot(_mx(p), _mx(v_ref[:, sl]), NN)
            o_ref[:, sl] = o
            oa_ref[:, sl] = _head_rms(o, go_ref[:, sl])[0].astype(oa_ref.dtype)

    W = 2 * hd
    vec = pl.BlockSpec((1, W), lambda b, h, i: (0, 0))
    ospec = pl.BlockSpec((tq, W), lambda b, h, i: (b * nq + i, h))
    return pl.pallas_call(
        body, name="fox_fwd", grid=(B, FOX_PAIRS, nq),
        in_specs=[pl.BlockSpec((tq, W), lambda b, h, i: (b * nq + i, h)),
                  pl.BlockSpec((S, W), lambda b, h, i: (b, FOX_PAIRS + h)),
                  pl.BlockSpec((S, W), lambda b, h, i: (b, 2 * FOX_PAIRS + h)),
                  pl.BlockSpec((1, 2, tq, 1), lambda b, h, i: (b, h, i, 0)),
                  pl.BlockSpec((1, 2, 1, S), lambda b, h, i: (b, h, 0, 0)), vec, vec, vec],
        out_specs=[ospec, ospec],
        out_shape=[jax.ShapeDtypeStruct((T, FOX_WIDTH), f32), jax.ShapeDtypeStruct((T, FOX_WIDTH), MXU_DTYPE)],
        compiler_params=_cparams("parallel", "parallel", "parallel"),
    )(P, P, P, ccol, crow, gq, gk, go)


def fox_bwd(P, ccol, crow, gq, gk, go, o_raw, d_oab, *, B, tq=256):
    T = P.shape[0]
    S = T // B
    tq = min(tq, S)
    nq = S // tq
    hd = FOX_HEAD_DIM
    scale = hd ** -0.5

    def body(q_ref, k_ref, v_ref, ccol_ref, crow_ref, gq_ref, gk_ref, go_ref, o_ref, doa_ref,
             dq_ref, dk_ref, dv_ref, dccol_ref, dcrow_ref, dgq_ref, dgk_ref, dgo_ref, dkn_acc, dv_acc, dcrow_acc):
        b, h, i = pl.program_id(0), pl.program_id(1), pl.program_id(2)
        q0 = i * tq

        @pl.when((b == 0) & (h == 0) & (i == 0))
        def _():
            dgq_ref[...] = jnp.zeros_like(dgq_ref)
            dgk_ref[...] = jnp.zeros_like(dgk_ref)
            dgo_ref[...] = jnp.zeros_like(dgo_ref)

        @pl.when(i == 0)
        def _():
            dkn_acc[...] = jnp.zeros_like(dkn_acc)
            dv_acc[...] = jnp.zeros_like(dv_acc)
            dcrow_acc[...] = jnp.zeros_like(dcrow_acc)

        for e in range(2):
            sl = slice(e * hd, (e + 1) * hd)
            q, k, v = q_ref[:, sl], k_ref[:, sl], v_ref[:, sl]
            gqv, gkv, gov = gq_ref[:, sl], gk_ref[:, sl], go_ref[:, sl]
            qn, rq = _head_rms(q, gqv)
            kn, rk = _head_rms(k, gkv)
            p = _softmax_rows(_fox_scores(qn, kn, ccol_ref[0, e], crow_ref[0, e], q0, tq, S, scale))
            o = o_ref[:, sl]
            ro = lax.rsqrt(jnp.mean(o * o, axis=-1, keepdims=True) + EPS)
            do, dgo = _head_rms_bwd(o, ro, gov, doa_ref[:, sl])
            dgo_ref[:, sl] += dgo
            dv_acc[e] += _dot(_mx(p), _mx(do), TN)
            dp = _dot(_mx(do), _mx(v), NT)
            ds = p * (dp - jnp.sum(do * o, axis=-1, keepdims=True))
            dccol_ref[0, e] = jnp.sum(ds, axis=1, keepdims=True)
            dcrow_acc[e] -= jnp.sum(ds, axis=0, keepdims=True)
            dqn = _dot(_mx(ds), _mx(kn), NN) * scale
            dkn_acc[e] += _dot(_mx(ds), _mx(qn), TN) * scale
            dq, dgq = _head_rms_bwd(q, rq, gqv, dqn)
            dq_ref[:, sl] = dq.astype(dq_ref.dtype)
            dgq_ref[:, sl] += dgq

        @pl.when(i == nq - 1)
        def _():
            for e in range(2):
                sl = slice(e * hd, (e + 1) * hd)
                k = k_ref[:, sl]
                gkv = gk_ref[:, sl]
                rk = lax.rsqrt(jnp.mean(k * k, axis=-1, keepdims=True) + EPS)
                dk, dgk = _head_rms_bwd(k, rk, gkv, dkn_acc[e])
                dk_ref[:, sl] = dk.astype(dk_ref.dtype)
                dv_ref[:, sl] = dv_acc[e].astype(dv_ref.dtype)
                dgk_ref[:, sl] += dgk
                dcrow_ref[0, e] = dcrow_acc[e]

    W = 2 * hd
    vec = pl.BlockSpec((1, W), lambda b, h, i: (0, 0))
    qspec = pl.BlockSpec((tq, W), lambda b, h, i: (b * nq + i, h))
    kvout = pl.BlockSpec((S, W), lambda b, h, i: (b, h))
    colspec = pl.BlockSpec((1, 2, tq, 1), lambda b, h, i: (b, h, i, 0))
    rowspec = pl.BlockSpec((1, 2, 1, S), lambda b, h, i: (b, h, 0, 0))
    return pl.pallas_call(
        body, name="fox_bwd", grid=(B, FOX_PAIRS, nq),
        in_specs=[qspec,
                  pl.BlockSpec((S, W), lambda b, h, i: (b, FOX_PAIRS + h)),
                  pl.BlockSpec((S, W), lambda b, h, i: (b, 2 * FOX_PAIRS + h)),
                  colspec, rowspec, vec, vec, vec, qspec, qspec],
        out_specs=[qspec, kvout, kvout, colspec, rowspec, vec, vec, vec],
        out_shape=[jax.ShapeDtypeStruct((T, FOX_WIDTH), MXU_DTYPE), jax.ShapeDtypeStruct((T, FOX_WIDTH), MXU_DTYPE),
                   jax.ShapeDtypeStruct((T, FOX_WIDTH), MXU_DTYPE),
                   jax.ShapeDtypeStruct((B, FOX_HEADS, S, 1), f32), jax.ShapeDtypeStruct((B, FOX_HEADS, 1, S), f32),
                   jax.ShapeDtypeStruct((1, W), f32), jax.ShapeDtypeStruct((1, W), f32), jax.ShapeDtypeStruct((1, W), f32)],
        scratch_shapes=[pltpu.VMEM((2, S, hd), f32), pltpu.VMEM((2, S, hd), f32), pltpu.VMEM((2, 1, S), f32)],
        compiler_params=_cparams("arbitrary", "arbitrary", "arbitrary"),
    )(P, P, P, ccol, crow, gq, gk, go, o_raw, d_oab)


FOX_TILE = 256
GROUP_PRECISION = lax.Precision.HIGH


def _head_mean(v):
    n = v.shape[1]
    r = lax.broadcasted_iota(jnp.int32, (n, n), 0) // FOX_HEAD_DIM
    c = lax.broadcasted_iota(jnp.int32, (n, n), 1) // FOX_HEAD_DIM
    return _dot(v, (r == c).astype(f32), NN, GROUP_PRECISION) * (1.0 / FOX_HEAD_DIM)


def fox_prep_fwd(P, gq, gk, *, tr=512):
    T = P.shape[0]
    tr = min(tr, T)
    scale = FOX_HEAD_DIM ** -0.5

    def body(q_ref, k_ref, v_ref, gq_ref, gk_ref, qn_ref, kn_ref, vb_ref):
        q, k = q_ref[...], k_ref[...]
        qn_ref[...] = (q * lax.rsqrt(_head_mean(q * q) + EPS) * (gq_ref[...] * scale)).astype(qn_ref.dtype)
        kn_ref[...] = (k * lax.rsqrt(_head_mean(k * k) + EPS) * gk_ref[...]).astype(kn_ref.dtype)
        vb_ref[...] = v_ref[...].astype(vb_ref.dtype)

    W = FOX_WIDTH
    col = lambda j: pl.BlockSpec((tr, W), lambda i: (i, j))
    vec = pl.BlockSpec((1, W), lambda i: (0, 0))
    out = jax.ShapeDtypeStruct((T, W), MXU_DTYPE)
    return pl.pallas_call(
        body, name="fox_prep_fwd", grid=(T // tr,), in_specs=[col(0), col(1), col(2), vec, vec],
        out_specs=[col(0)] * 3, out_shape=[out] * 3, compiler_params=_cparams("parallel"),
    )(P, P, P, gq, gk)


def fox_prep_bwd(P, gq, gk, dqn, dkn, *, tr=512):
    T = P.shape[0]
    tr = min(tr, T)
    scale = FOX_HEAD_DIM ** -0.5

    def body(q_ref, k_ref, gq_ref, gk_ref, dqn_ref, dkn_ref, dq_ref, dk_ref, dgq_ref, dgk_ref):
        @pl.when(pl.program_id(0) == 0)
        def _():
            dgq_ref[...] = jnp.zeros_like(dgq_ref)
            dgk_ref[...] = jnp.zeros_like(dgk_ref)

        def one(x, g, dn, dx_ref, dg_ref):
            r = lax.rsqrt(_head_mean(x * x) + EPS)
            xhat = x * r
            gd = dn * g
            dx_ref[...] = (r * (gd - xhat * _head_mean(gd * xhat))).astype(dx_ref.dtype)
            return jnp.sum(dn * xhat, axis=0, keepdims=True)

        dgq_ref[...] += scale * one(q_ref[...], gq_ref[...] * scale, dqn_ref[...], dq_ref, dgq_ref)
        dgk_ref[...] += one(k_ref[...], gk_ref[...], dkn_ref[...], dk_ref, dgk_ref)

    W = FOX_WIDTH
    col = lambda j: pl.BlockSpec((tr, W), lambda i: (i, j))
    vec = pl.BlockSpec((1, W), lambda i: (0, 0))
    return pl.pallas_call(
        body, name="fox_prep_bwd", grid=(T // tr,), in_specs=[col(0), col(1), vec, vec, col(0), col(0)],
        out_specs=[col(0), col(0), vec, vec],
        out_shape=[jax.ShapeDtypeStruct((T, W), MXU_DTYPE), jax.ShapeDtypeStruct((T, W), MXU_DTYPE),
                   jax.ShapeDtypeStruct((1, W), f32), jax.ShapeDtypeStruct((1, W), f32)],
        compiler_params=_cparams("arbitrary"),
    )(P, P, gq, gk, dqn, dkn)


def _fox_tile_scores(q, k_ref, crow_ref, cc, e, j, sl, masked):
    t = FOX_TILE
    k = k_ref[pl.ds(pl.multiple_of(j * t, t), t), sl]
    s = _dot(q, k, NT) + cc - crow_ref[0, e, j]
    if masked:
        row = lax.broadcasted_iota(jnp.int32, (t, t), 0)
        col = lax.broadcasted_iota(jnp.int32, (t, t), 1)
        s = jnp.where(col <= row, s, NEG_INF)
    return s, k


def fox_core_fwd(qn, kn, vb, ccol, crow, go, *, B):
    T = qn.shape[0]
    S = T // B
    t = FOX_TILE
    nq = S // t
    hd = FOX_HEAD_DIM

    def body(q_ref, k_ref, v_ref, ccol_ref, crow_ref, go_ref, o_ref, oa_ref, lse_ref):
        i = pl.program_id(2)
        for e in range(2):
            sl = slice(e * hd, (e + 1) * hd)
            q = q_ref[:, sl]
            cc = ccol_ref[0, e]

            def update(carry, j, masked):
                m, l, acc = carry
                s, _ = _fox_tile_scores(q, k_ref, crow_ref, cc, e, j, sl, masked)
                m2 = jnp.maximum(m, jnp.max(s, axis=-1, keepdims=True))
                a = jnp.exp(m - m2)
                p = jnp.exp(s - m2)
                v = v_ref[pl.ds(pl.multiple_of(j * t, t), t), sl]
                return m2, a * l + jnp.sum(p, axis=-1, keepdims=True), a * acc + _dot(_mx(p), v, NN)

            carry = (jnp.full((t, 1), NEG_INF, f32), jnp.zeros((t, 1), f32), jnp.zeros((t, hd), f32))
            carry = lax.fori_loop(0, i, lambda j, cr: update(cr, j, False), carry)
            m, l, acc = update(carry, i, True)
            o = acc / l
            o_ref[:, sl] = o
            oa_ref[:, sl] = _head_rms(o, go_ref[:, sl])[0].astype(oa_ref.dtype)
            lse_ref[0, e] = m + jnp.log(l)

    W = 2 * hd
    qspec = pl.BlockSpec((t, W), lambda b, h, i: (b * nq + i, h))
    kspec = pl.BlockSpec((S, W), lambda b, h, i: (b, h))
    colspec = pl.BlockSpec((1, 2, t, 1), lambda b, h, i: (b, h, i, 0))
    return pl.pallas_call(
        body, name="fox_core_fwd", grid=(B, FOX_PAIRS, nq),
        in_specs=[qspec, kspec, kspec, colspec, pl.BlockSpec((1, 2, nq, 1, t), lambda b, h, i: (b, h, 0, 0, 0)),
                  pl.BlockSpec((1, W), lambda b, h, i: (0, 0))],
        out_specs=[qspec, qspec, colspec],
        out_shape=[jax.ShapeDtypeStruct((T, FOX_WIDTH), f32), jax.ShapeDtypeStruct((T, FOX_WIDTH), MXU_DTYPE),
                   jax.ShapeDtypeStruct((B, FOX_HEADS, S, 1), f32)],
        compiler_params=_cparams("parallel", "parallel", "parallel"),
    )(qn, kn, vb, ccol, crow, go)


def fox_core_bwd(qn, kn, vb, ccol, crow, go, o_raw, lse, d_oab, *, B):
    T = qn.shape[0]
    S = T // B
    t = FOX_TILE
    nq = S // t
    hd = FOX_HEAD_DIM

    def body(q_ref, k_ref, v_ref, ccol_ref, crow_ref, go_ref, o_ref, lse_ref, doa_ref,
             dq_ref, dk_ref, dv_ref, dccol_ref, dcrow_ref, dgo_ref, dk_acc, dv_acc, dcrow_acc):
        b, h, i = pl.program_id(0), pl.program_id(1), pl.program_id(2)

        @pl.when((b == 0) & (h == 0) & (i == 0))
        def _():
            dgo_ref[...] = jnp.zeros_like(dgo_ref)

        @pl.when(i == 0)
        def _():
            dk_acc[...] = jnp.zeros_like(dk_acc)
            dv_acc[...] = jnp.zeros_like(dv_acc)
            dcrow_acc[...] = jnp.zeros_like(dcrow_acc)

        for e in range(2):
            sl = slice(e * hd, (e + 1) * hd)
            q = q_ref[:, sl]
            cc = ccol_ref[0, e]
            lse_e = lse_ref[0, e]
            o = o_ref[:, sl]
            ro = lax.rsqrt(jnp.mean(o * o, axis=-1, keepdims=True) + EPS)
            do, dgo = _head_rms_bwd(o, ro, go_ref[:, sl], doa_ref[:, sl])
            dgo_ref[:, sl] += dgo
            delta = jnp.sum(do * o, axis=-1, keepdims=True)
            do_b = _mx(do)

            def update(carry, j, masked):
                dq, dcc = carry
                s, k = _fox_tile_scores(q, k_ref, crow_ref, cc, e, j, sl, masked)
                rows = pl.ds(pl.multiple_of(j * t, t), t)
                p = jnp.exp(s - lse_e)
                dv_acc[e, rows, :] += _dot(_mx(p), do_b, TN)
                ds = p * (_dot(do_b, v_ref[rows, sl], NT) - delta)
                dcrow_acc[e, j] -= jnp.sum(ds, axis=0, keepdims=True)
                ds_b = _mx(ds)
                dk_acc[e, rows, :] += _dot(ds_b, q, TN)
                return dq + _dot(ds_b, k, NN), dcc + jnp.sum(ds, axis=1, keepdims=True)

            carry = (jnp.zeros((t, hd), f32), jnp.zeros((t, 1), f32))
            carry = lax.fori_loop(0, i, lambda j, cr: update(cr, j, False), carry)
            dq, dcc = update(carry, i, True)
            dq_ref[:, sl] = dq
            dccol_ref[0, e] = dcc

        @pl.when(i == nq - 1)
        def _():
            for e in range(2):
                sl = slice(e * hd, (e + 1) * hd)
                dk_ref[:, sl] = dk_acc[e]
                dv_ref[:, sl] = dv_acc[e].astype(dv_ref.dtype)
            dcrow_ref[0] = dcrow_acc[...]

    W = 2 * hd
    qspec = pl.BlockSpec((t, W), lambda b, h, i: (b * nq + i, h))
    kspec = pl.BlockSpec((S, W), lambda b, h, i: (b, h))
    colspec = pl.BlockSpec((1, 2, t, 1), lambda b, h, i: (b, h, i, 0))
    rowspec = pl.BlockSpec((1, 2, nq, 1, t), lambda b, h, i: (b, h, 0, 0, 0))
    vec = pl.BlockSpec((1, W), lambda b, h, i: (0, 0))
    return pl.pallas_call(
        body, name="fox_core_bwd", grid=(B, FOX_PAIRS, nq),
        in_specs=[qspec, kspec, kspec, colspec, rowspec, vec, qspec, colspec, qspec],
        out_specs=[qspec, kspec, kspec, colspec, rowspec, vec],
        out_shape=[jax.ShapeDtypeStruct((T, FOX_WIDTH), f32), jax.ShapeDtypeStruct((T, FOX_WIDTH), f32),
                   jax.ShapeDtypeStruct((T, FOX_WIDTH), MXU_DTYPE),
                   jax.ShapeDtypeStruct((B, FOX_HEADS, S, 1), f32), jax.ShapeDtypeStruct((B, FOX_HEADS, nq, 1, t), f32),
                   jax.ShapeDtypeStruct((1, W), f32)],
        scratch_shapes=[pltpu.VMEM((2, S, hd), f32), pltpu.VMEM((2, S, hd), f32), pltpu.VMEM((2, nq, 1, t), f32)],
        compiler_params=_cparams("arbitrary", "arbitrary", "arbitrary"),
    )(qn, kn, vb, ccol, crow, go, o_raw, lse, d_oab)


def _lane_mask(lo, hi, shape):
    lane = lax.broadcasted_iota(jnp.int32, shape, 1)
    return (lane >= lo) & (lane < hi)


def _cumsum_rows(v, period, reverse=False):
    n = v.shape[0]
    pos = lax.broadcasted_iota(jnp.int32, v.shape, 0) % period
    sh = 1
    while sh < period:
        if reverse:
            v = v + jnp.where(pos + sh < period, pltpu.roll(v, n - sh, 0), 0.0)
        else:
            v = v + jnp.where(pos >= sh, pltpu.roll(v, sh, 0), 0.0)
        sh *= 2
    return v


def _gate_values(z, bias, alog):
    zb = z + bias
    ls = jax.nn.log_sigmoid(zb)
    beta = jax.nn.sigmoid(z)
    g = -jnp.exp(alog) * jax.nn.softplus(zb)
    return zb, ls, beta, g


def gates_fwd(P, bias, alog, *, B):
    T = P.shape[0]
    S = T // B

    def body(z_ref, bias_ref, alog_ref, o_ref):
        z = z_ref[...]
        _, ls, beta, g = _gate_values(z, bias_ref[...], alog_ref[...])
        c = _cumsum_rows(ls, S)
        gc = _cumsum_rows(g, GDN_CHUNK)
        o = jnp.where(_lane_mask(SM_F, SM_F + FOX_HEADS, z.shape), c, 0.0)
        o = jnp.where(_lane_mask(SM_B, SM_B + GDN_HEADS, z.shape), beta, o)
        o = jnp.where(_lane_mask(SM_A, SM_A + GDN_HEADS, z.shape), gc, o)
        o_ref[...] = o

    vec = pl.BlockSpec((1, LANES), lambda b: (0, 0))
    return pl.pallas_call(
        body, name="gates_fwd", grid=(B,),
        in_specs=[pl.BlockSpec((S, LANES), lambda b: (b, COL_SMALL // LANES)), vec, vec],
        out_specs=pl.BlockSpec((S, LANES), lambda b: (b, 0)),
        out_shape=jax.ShapeDtypeStruct((T, LANES), f32),
        compiler_params=_cparams("parallel"),
    )(P, bias, alog)


def gates_bwd(P, bias, alog, dgates, *, B):
    T = P.shape[0]
    S = T // B

    def body(z_ref, bias_ref, alog_ref, dg_ref, dz_ref, par_ref):
        z = z_ref[...]
        zb, ls, beta, g = _gate_values(z, bias_ref[...], alog_ref[...])
        d = dg_ref[...]
        dls = _cumsum_rows(d, S, reverse=True)
        dgr = _cumsum_rows(d, GDN_CHUNK, reverse=True)
        sig = jax.nn.sigmoid(zb)
        dz_f = dls * (1.0 - sig)
        dz_b = d * beta * (1.0 - beta)
        dz_a = dgr * (-jnp.exp(alog_ref[...])) * sig
        dz = jnp.where(_lane_mask(SM_F, SM_F + FOX_HEADS, z.shape), dz_f, 0.0)
        dz = jnp.where(_lane_mask(SM_B, SM_B + GDN_HEADS, z.shape), dz_b, dz)
        dz = jnp.where(_lane_mask(SM_A, SM_A + GDN_HEADS, z.shape), dz_a, dz)
        dz_ref[...] = dz.astype(dz_ref.dtype)

        @pl.when(pl.program_id(0) == 0)
        def _():
            par_ref[...] = jnp.zeros_like(par_ref)

        dalog = jnp.where(_lane_mask(SM_A, SM_A + GDN_HEADS, z.shape), dgr * g, 0.0)
        par_ref[0:1, :] += jnp.sum(dz, axis=0, keepdims=True)
        par_ref[1:2, :] += jnp.sum(dalog, axis=0, keepdims=True)

    vec = pl.BlockSpec((1, LANES), lambda b: (0, 0))
    return pl.pallas_call(
        body, name="gates_bwd", grid=(B,),
        in_specs=[pl.BlockSpec((S, LANES), lambda b: (b, COL_SMALL // LANES)), vec, vec,
                  pl.BlockSpec((S, LANES), lambda b: (b, 0))],
        out_specs=[pl.BlockSpec((S, LANES), lambda b: (b, 0)), pl.BlockSpec((8, LANES), lambda b: (0, 0))],
        out_shape=[jax.ShapeDtypeStruct((T, LANES), MXU_DTYPE), jax.ShapeDtypeStruct((8, LANES), f32)],
        compiler_params=_cparams("arbitrary"),
    )(P, bias, alog, dgates)


GDN_BLOCKS = 3 * GDN_HEADS


def _shift_rows(v, d, reverse=False):
    if d == 0:
        return v
    n = v.shape[0]
    row = lax.broadcasted_iota(jnp.int32, v.shape, 0)
    if reverse:
        return jnp.where(row + d < n, pltpu.roll(v, n - d, 0), 0.0)
    return jnp.where(row >= d, pltpu.roll(v, d, 0), 0.0)


def _conv_silu(x, w):
    pre = sum(w[j:j + 1, :] * _shift_rows(x, CONV_WIDTH - 1 - j) for j in range(CONV_WIDTH))
    return pre, pre * jax.nn.sigmoid(pre)


def gdn_prep_fwd(P, conv_w, *, B):
    T = P.shape[0]
    S = T // B

    def body(x_ref, w_ref, o_ref):
        _, y = _conv_silu(x_ref[...], w_ref[...])
        yn = y * lax.rsqrt(jnp.sum(y * y, axis=-1, keepdims=True) + EPS)
        o_ref[...] = jnp.where(pl.program_id(1) < 2 * GDN_HEADS, yn, y)

    return pl.pallas_call(
        body, name="gdn_prep_fwd", grid=(B, GDN_BLOCKS),
        in_specs=[pl.BlockSpec((S, LANES), lambda b, j: (b, COL_GDN // LANES + j)),
                  pl.BlockSpec((CONV_WIDTH, LANES), lambda b, j: (0, j))],
        out_specs=pl.BlockSpec((S, LANES), lambda b, j: (b, j)),
        out_shape=jax.ShapeDtypeStruct((T, 3 * GDN_WIDTH), f32),
        compiler_params=_cparams("parallel", "parallel"),
    )(P, conv_w)


def gdn_prep_bwd(P, conv_w, dG, *, B):
    T = P.shape[0]
    S = T // B

    def body(x_ref, w_ref, dg_ref, dx_ref, dw_ref):
        x, w = x_ref[...], w_ref[...]
        pre, y = _conv_silu(x, w)
        dn = dg_ref[...]
        r = lax.rsqrt(jnp.sum(y * y, axis=-1, keepdims=True) + EPS)
        n = y * r
        dy_norm = r * (dn - n * jnp.sum(dn * n, axis=-1, keepdims=True))
        dy = jnp.where(pl.program_id(0) < 2 * GDN_HEADS, dy_norm, dn)
        sg = jax.nn.sigmoid(pre)
        dpre = dy * (sg * (1.0 + pre * (1.0 - sg)))
        dx = sum(w[j:j + 1, :] * _shift_rows(dpre, CONV_WIDTH - 1 - j, reverse=True) for j in range(CONV_WIDTH))
        dx_ref[...] = dx.astype(dx_ref.dtype)

        @pl.when(pl.program_id(1) == 0)
        def _():
            dw_ref[...] = jnp.zeros_like(dw_ref)

        for j in range(CONV_WIDTH):
            dw_ref[j:j + 1, :] += jnp.sum(dpre * _shift_rows(x, CONV_WIDTH - 1 - j), axis=0, keepdims=True)

    return pl.pallas_call(
        body, name="gdn_prep_bwd", grid=(GDN_BLOCKS, B),
        in_specs=[pl.BlockSpec((S, LANES), lambda j, b: (b, COL_GDN // LANES + j)),
                  pl.BlockSpec((CONV_WIDTH, LANES), lambda j, b: (0, j)),
                  pl.BlockSpec((S, LANES), lambda j, b: (b, j))],
        out_specs=[pl.BlockSpec((S, LANES), lambda j, b: (b, j)),
                   pl.BlockSpec((CONV_WIDTH, LANES), lambda j, b: (0, j))],
        out_shape=[jax.ShapeDtypeStruct((T, 3 * GDN_WIDTH), MXU_DTYPE),
                   jax.ShapeDtypeStruct((CONV_WIDTH, 3 * GDN_WIDTH), f32)],
        compiler_params=_cparams("arbitrary", "arbitrary"),
    )(P, conv_w, dG)


GDN_GROUP = 4
B_NN = (((2,), (1,)), ((0,), (0,)))
B_NT = (((2,), (2,)), ((0,), (0,)))
B_TN = (((1,), (1,)), ((0,), (0,)))


def _bmm(a, b, dims, precision=None):
    if precision is None:
        a, b = _mx(a), _mx(b)
    return lax.dot_general(a, b, dims, preferred_element_type=f32, precision=precision)


def _tri_inverse(A):
    C = A.shape[-1]
    row = lax.broadcasted_iota(jnp.int32, A.shape, 1)
    col = lax.broadcasted_iota(jnp.int32, A.shape, 2)
    eye = (row == col).astype(f32)
    X = jnp.where((row // 4) == (col // 4), -A, 0.0)
    X2 = _bmm(X, X, B_NN, INV_PRECISION)
    Tm = eye + X + X2 + _bmm(X, X2, B_NN, INV_PRECISION)
    b = 4
    while b < C:
        off = ((row // (2 * b)) == (col // (2 * b))) & ((row // b) != (col // b))
        Tm = Tm - _bmm(_bmm(Tm, jnp.where(off, A, 0.0), B_NN, INV_PRECISION), Tm, B_NN, INV_PRECISION)
        b *= 2
    return Tm


def _pick_lane(block, lane_idx):
    lane = lax.broadcasted_iota(jnp.int32, block.shape, 1)
    return jnp.sum(jnp.where(lane == lane_idx, block, 0.0), axis=1, keepdims=True)


def _gdn_local(q, k, v, beta, gc, Tm=None):
    C = GDN_CHUNK
    n = q.shape[0] // C
    q = q.reshape(n, C, -1) * (GDN_HEAD_DIM ** -0.5)
    k = k.reshape(n, C, -1)
    v = v.reshape(n, C, -1)
    beta = beta.reshape(n, C, 1)
    gc = gc.reshape(n, C, 1)
    row = lax.broadcasted_iota(jnp.int32, (n, C, C), 1)
    col = lax.broadcasted_iota(jnp.int32, (n, C, C), 2)
    gcT = jnp.swapaxes(jnp.broadcast_to(gc, (n, C, C)), 1, 2)
    D = jnp.exp(jnp.where(row >= col, gc - gcT, NEG_INF))
    kb = k * beta
    vb = v * beta
    A = jnp.where(row > col, _bmm(kb, k, B_NT) * D, 0.0)
    Gam = jnp.exp(gc)
    kg = kb * Gam
    gl = gc[:, C - 1:C, :]
    kdec = jnp.exp(gl - gc)
    loc = dict(q=q, k=k, v=v, beta=beta, gc=gc, D=D, kb=kb, vb=vb, A=A, Gam=Gam, kg=kg,
               kdec=kdec, kd=k * kdec, qg=q * Gam, gam=jnp.exp(gl), row=row, col=col)
    if Tm is None:
        Tm = _tri_inverse(A)
        loc.update(u=_bmm(Tm, vb, B_NN), w=_bmm(Tm, kg, B_NN), M=_bmm(q, k, B_NT) * D)
    else:
        Tm = Tm.reshape(n, C, C)
    loc["Tm"] = Tm
    return loc


def _gdn_store_local(loc, r0, u_s, w_s, qg_s, kd_s, M_s, gam_s, c0):
    n = loc["u"].shape[0]
    R = n * GDN_CHUNK
    u_s[pl.ds(r0, R), :] = loc["u"].reshape(R, -1)
    w_s[pl.ds(r0, R), :] = loc["w"].reshape(R, -1)
    qg_s[pl.ds(r0, R), :] = loc["qg"].reshape(R, -1)
    kd_s[pl.ds(r0, R), :] = loc["kd"].reshape(R, -1)
    M_s[pl.ds(r0, R), :] = loc["M"].reshape(R, -1)
    gam_s[pl.ds(c0, n)] = jnp.broadcast_to(loc["gam"], (n, 1, LANES))


def _gdn_specs(S):
    blk = lambda off: pl.BlockSpec((S, LANES), lambda b, h: (b, off + h))
    return blk


def gdn_fwd(G, gates, P, g_on, *, B):
    T = G.shape[0]
    S = T // B
    C = GDN_CHUNK
    N = S // C
    R = GDN_GROUP * C
    hd = GDN_HEAD_DIM

    def body(q_ref, k_ref, v_ref, gt_ref, z_ref, gon_ref, o_ref, ob_ref, st_ref, u_s, w_s, qg_s, kd_s, M_s, gam_s):
        h = pl.program_id(1)

        def local(gi, carry):
            r0 = pl.multiple_of(gi * R, R)
            gt = gt_ref[pl.ds(r0, R), :]
            loc = _gdn_local(q_ref[pl.ds(r0, R), :], k_ref[pl.ds(r0, R), :], v_ref[pl.ds(r0, R), :],
                             _pick_lane(gt, SM_B + h), _pick_lane(gt, SM_A + h))
            _gdn_store_local(loc, r0, u_s, w_s, qg_s, kd_s, M_s, gam_s, gi * GDN_GROUP)
            return carry

        lax.fori_loop(0, N // GDN_GROUP, local, 0)

        def step(n, state):
            r0 = pl.multiple_of(n * C, C)
            st_ref[0, 0, n] = state
            v_new = u_s[pl.ds(r0, C), :] - _dotm(w_s[pl.ds(r0, C), :], state, NN)
            o_ref[pl.ds(r0, C), :] = (_dotm(qg_s[pl.ds(r0, C), :], state, NN)
                                      + _dotm(M_s[pl.ds(r0, C), :], v_new, NN))
            return state * gam_s[n] + _dotm(kd_s[pl.ds(r0, C), :], v_new, TN)

        lax.fori_loop(0, N, step, jnp.zeros((hd, hd), f32))
        o = o_ref[...]
        z = z_ref[...]
        ob_ref[...] = (_head_rms(o, gon_ref[...])[0] * (z * jax.nn.sigmoid(z))).astype(ob_ref.dtype)

    blk = lambda off: pl.BlockSpec((S, LANES), lambda b, h: (b, off + h))
    rows = lambda: pltpu.VMEM((S, hd), f32)
    return pl.pallas_call(
        body, name="gdn_fwd", grid=(B, GDN_HEADS),
        in_specs=[blk(0), blk(GDN_HEADS), blk(2 * GDN_HEADS), pl.BlockSpec((S, LANES), lambda b, h: (b, 0)),
                  blk(COL_Z // LANES), pl.BlockSpec((1, hd), lambda b, h: (0, 0))],
        out_specs=[blk(0), blk(0), pl.BlockSpec((1, 1, N, hd, hd), lambda b, h: (b, h, 0, 0, 0))],
        out_shape=[jax.ShapeDtypeStruct((T, GDN_WIDTH), f32), jax.ShapeDtypeStruct((T, GDN_WIDTH), MXU_DTYPE),
                   jax.ShapeDtypeStruct((B, GDN_HEADS, N, hd, hd), f32)],
        scratch_shapes=[rows(), rows(), rows(), rows(), pltpu.VMEM((S, C), f32), pltpu.VMEM((N, 1, LANES), f32)],
        compiler_params=_cparams("parallel", "parallel"),
    )(G, G, G, gates, P, g_on)


def gdn_bwd(G, gates, P, g_on, o_raw, states, d_oab, *, B):
    T = G.shape[0]
    S = T // B
    C = GDN_CHUNK
    N = S // C
    R = GDN_GROUP * C
    hd = GDN_HEAD_DIM

    def body(q_ref, k_ref, v_ref, gt_ref, z_ref, gon_ref, o_ref, st_ref, dob_ref,
             dq_ref, dk_ref, dv_ref, dgt_ref, dz_ref, dgon_ref,
             u_s, w_s, qg_s, kd_s, M_s, gam_s, do_s, du_s, dw_s, dqg_s, dkd_s, dM_s, dgl_s, Tm_s):
        b, h = pl.program_id(0), pl.program_id(1)

        @pl.when((b == 0) & (h == 0))
        def _():
            dgon_ref[...] = jnp.zeros_like(dgon_ref)

        @pl.when(h == 0)
        def _():
            dgt_ref[...] = jnp.zeros_like(dgt_ref)

        def group_inputs(gi, Tm_of=None):
            r0 = pl.multiple_of(gi * R, R)
            gt = gt_ref[pl.ds(r0, R), :]
            Tm = None if Tm_of is None else Tm_of[pl.ds(r0, R), :]
            return r0, _gdn_local(q_ref[pl.ds(r0, R), :], k_ref[pl.ds(r0, R), :], v_ref[pl.ds(r0, R), :],
                                  _pick_lane(gt, SM_B + h), _pick_lane(gt, SM_A + h), Tm)

        def local(gi, carry):
            r0, loc = group_inputs(gi)
            _gdn_store_local(loc, r0, u_s, w_s, qg_s, kd_s, M_s, gam_s, gi * GDN_GROUP)
            Tm_s[pl.ds(r0, R), :] = loc["Tm"].reshape(R, C)
            o, z, gon = o_ref[pl.ds(r0, R), :], z_ref[pl.ds(r0, R), :], gon_ref[...]
            dob = dob_ref[pl.ds(r0, R), :]
            on, ro = _head_rms(o, gon)
            sz = jax.nn.sigmoid(z)
            dz_ref[pl.ds(r0, R), :] = (dob * on * (sz * (1.0 + z * (1.0 - sz)))).astype(dz_ref.dtype)
            do, dgon = _head_rms_bwd(o, ro, gon, dob * (z * sz))
            do_s[pl.ds(r0, R), :] = do
            dgon_ref[...] += dgon
            return carry

        lax.fori_loop(0, N // GDN_GROUP, local, 0)

        def step(t, dS):
            n = N - 1 - t
            r0 = pl.multiple_of(n * C, C)
            rows = pl.ds(r0, C)
            state = st_ref[0, 0, n]
            w_n, M_n, kd_n, do_n = w_s[rows, :], M_s[rows, :], kd_s[rows, :], do_s[rows, :]
            v_new = u_s[rows, :] - _dotm(w_n, state, NN)
            dv_new = _dotm(M_n, do_n, TN) + _dotm(kd_n, dS, NN)
            du_s[rows, :] = dv_new
            dw_s[rows, :] = -_dotm(dv_new, state, NT)
            dqg_s[rows, :] = _dotm(do_n, state, NT)
            dM_s[rows, :] = _dotm(do_n, v_new, NT)
            dkd_s[rows, :] = _dotm(v_new, dS, NT)
            gam = gam_s[n]
            dgl_s[n] = jnp.broadcast_to(jnp.sum(jnp.sum(dS * state, axis=1, keepdims=True), axis=0, keepdims=True), (1, LANES)) * gam
            return dS * gam + _dotm(qg_s[rows, :], do_n, TN) - _dotm(w_n, dv_new, TN)

        lax.fori_loop(0, N, step, jnp.zeros((hd, hd), f32))

        def finish(gi, carry):
            r0, L = group_inputs(gi, Tm_s)
            n = GDN_GROUP
            rows = pl.ds(r0, R)
            g3 = lambda ref: ref[rows, :].reshape(n, C, -1)
            du, dw, dqg, dkd, dM = g3(du_s), g3(dw_s), g3(dqg_s), g3(dkd_s), g3(dM_s)
            L["M"] = g3(M_s)
            TmT = jnp.swapaxes(L["Tm"], 1, 2)
            dTm = _bmm(du, L["vb"], B_NT) + _bmm(dw, L["kg"], B_NT)
            dvb = _bmm(TmT, du, B_NN)
            dkg = _bmm(TmT, dw, B_NN)
            dA = jnp.where(L["row"] > L["col"], -_bmm(_bmm(TmT, dTm, B_NN), TmT, B_NN), 0.0)
            dKK = dA * L["D"]
            dQK = dM * L["D"]
            dkb = _bmm(dKK, L["k"], B_NN) + dkg * L["Gam"]
            dk = (_bmm(dKK, L["kb"], B_TN) + _bmm(dQK, L["q"], B_TN) + dkd * L["kdec"] + L["beta"] * dkb)
            dq = (_bmm(dQK, L["k"], B_NN) + dqg * L["Gam"]) * (GDN_HEAD_DIM ** -0.5)
            E = dA * L["A"] + dM * L["M"]
            r = jnp.sum(dkd * L["kd"], axis=-1, keepdims=True)
            dgc = (jnp.sum(E, axis=2, keepdims=True) - jnp.sum(jnp.swapaxes(E, 1, 2), axis=2, keepdims=True)
                   + jnp.sum(dkg * L["kg"], axis=-1, keepdims=True) + jnp.sum(dqg * L["qg"], axis=-1, keepdims=True) - r)
            dgl = jnp.sum(r, axis=1, keepdims=True) + dgl_s[pl.ds(gi * n, n)][:, :, 0:1]
            rowc = lax.broadcasted_iota(jnp.int32, (n, C, 1), 1)
            dgc = dgc + jnp.where(rowc == C - 1, dgl, 0.0)
            dbeta = jnp.sum(dkb * L["k"], axis=-1, keepdims=True) + jnp.sum(dvb * L["v"], axis=-1, keepdims=True)
            dq_ref[rows, :] = dq.reshape(R, hd)
            dk_ref[rows, :] = dk.reshape(R, hd)
            dv_ref[rows, :] = (L["beta"] * dvb).reshape(R, hd)
            lane = lax.broadcasted_iota(jnp.int32, (R, LANES), 1)
            dgt_ref[rows, :] += (jnp.where(lane == SM_B + h, dbeta.reshape(R, 1), 0.0)
                                 + jnp.where(lane == SM_A + h, dgc.reshape(R, 1), 0.0))
            return carry

        lax.fori_loop(0, N // GDN_GROUP, finish, 0)

    blk = lambda off: pl.BlockSpec((S, LANES), lambda b, h: (b, off + h))
    rows = lambda: pltpu.VMEM((S, hd), f32)
    return pl.pallas_call(
        body, name="gdn_bwd", grid=(B, GDN_HEADS),
        in_specs=[blk(0), blk(GDN_HEADS), blk(2 * GDN_HEADS), pl.BlockSpec((S, LANES), lambda b, h: (b, 0)),
                  blk(COL_Z // LANES), pl.BlockSpec((1, hd), lambda b, h: (0, 0)), blk(0),
                  pl.BlockSpec((1, 1, N, hd, hd), lambda b, h: (b, h, 0, 0, 0)), blk(GDN_HEADS)],
        out_specs=[blk(0), blk(0), blk(0), pl.BlockSpec((S, LANES), lambda b, h: (b, 0)), blk(0),
                   pl.BlockSpec((1, hd), lambda b, h: (0, 0))],
        out_shape=[jax.ShapeDtypeStruct((T, GDN_WIDTH), f32), jax.ShapeDtypeStruct((T, GDN_WIDTH), f32),
                   jax.ShapeDtypeStruct((T, GDN_WIDTH), f32), jax.ShapeDtypeStruct((T, LANES), f32),
                   jax.ShapeDtypeStruct((T, GDN_WIDTH), MXU_DTYPE), jax.ShapeDtypeStruct((1, hd), f32)],
        scratch_shapes=[rows(), rows(), rows(), rows(), pltpu.VMEM((S, C), f32), pltpu.VMEM((N, 1, LANES), f32),
                        rows(), rows(), rows(), rows(), rows(), pltpu.VMEM((S, C), f32), pltpu.VMEM((N, 1, LANES), f32),
                        pltpu.VMEM((S, C), f32)],
        compiler_params=_cparams("arbitrary", "arbitrary"),
    )(G, G, G, gates, P, g_on, o_raw, states, d_oab)


IN_SPLIT = (0, 1536, 1544, 3080, 3088, 3600)


def align_w_in(w):
    s = IN_SPLIT
    pad = jnp.zeros((w.shape[0], IN_ALIGNED - IN_DIM), w.dtype)
    return jnp.concatenate([w[:, s[0]:s[1]], w[:, s[2]:s[3]], w[:, s[4]:s[5]], w[:, s[1]:s[2]], w[:, s[3]:s[4]], pad], axis=1)


def unalign_w_in(wa):
    return jnp.concatenate([wa[:, 0:1536], wa[:, COL_SMALL:COL_SMALL + 8], wa[:, 1536:3072],
                            wa[:, COL_SMALL + 8:COL_SMALL + 16], wa[:, 3072:3584]], axis=1)


def _lanes_vec(pieces):
    v = jnp.zeros((1, LANES), f32)
    for off, a in pieces:
        v = lax.dynamic_update_slice(v, a.astype(f32), (0, off))
    return v


def local_step(x, mem, target, w, sp, *, B):
    T = x.shape[0]
    S = T // B
    gq8, gk8 = jnp.tile(sp["fox_qnorm_g"], (1, FOX_HEADS)), jnp.tile(sp["fox_knorm_g"], (1, FOX_HEADS))
    go2 = jnp.tile(sp["fox_onorm_g"], (1, 2))
    bias = _lanes_vec([(SM_F, sp["fox_f_bias"]), (SM_A, sp["gdn_dt_bias"])])
    alog = _lanes_vec([(SM_A, sp["gdn_A_log"])])

    h1 = rms_fwd(x, sp["norm_mix_g"], name="rms_mix")
    P = matmul(h1, w["wa"], name="mm_in", tn=IN_TILE)
    gates = gates_fwd(P, bias, alog, B=B)
    c = gates[:, SM_F:SM_F + FOX_HEADS].reshape(B, S, FOX_HEADS).transpose(0, 2, 1)
    ccol, crow = c[..., None], c.reshape(B, FOX_HEADS, S // FOX_TILE, 1, FOX_TILE)
    qn, kn, vb = fox_prep_fwd(P, gq8, gk8)
    o_raw, o_a, lse = fox_core_fwd(qn, kn, vb, ccol, crow, go2, B=B)
    G = gdn_prep_fwd(P, w["conv_w"], B=B)
    ob_raw, o_b, states = gdn_fwd(G, gates, P, sp["gdn_onorm_g"], B=B)
    oab = jnp.concatenate([o_a, o_b], axis=1)
    x2 = matmul(oab, w["w_out"], residual=x, name="mm_out")
    hq = rms_fwd(x2, sp["norm_xattn_g"], name="rms_xattn")
    hm = rms_fwd(mem, sp["mem_norm_g"], name="rms_mem")
    cq = matmul(hq, w["w_cq"], name="mm_cq")
    ckv = matmul(hm, w["w_ckv"], name="mm_ckv")
    co = xattn_fwd(cq, ckv, sp["xattn_qnorm_g"], sp["xattn_knorm_g"], B=B)
    x3 = matmul(co, w["w_co"], residual=x2, name="mm_co")
    hf = rms_fwd(x3, sp["norm_mlp_g"], name="rms_mlp")
    a, act = matmul(hf, w["w_mlp1"], relu2_out=True, name="mm_mlp1")
    x4 = matmul(act, w["w_mlp2"], residual=x3, name="mm_mlp2")
    dy, loss = loss_head(x4, target)

    da = matmul(dy, w["w_mlp2"], tb=True, relu2_bwd_aux=a, out_dtype=MXU_DTYPE, name="mm_d_act")
    g_mlp2 = matmul(act, dy, ta=True, name="mm_g_mlp2")
    g_mlp1 = matmul(hf, da, ta=True, name="mm_g_mlp1")
    dhf = matmul(da, w["w_mlp1"], tb=True, name="mm_d_hf")
    dx3, g_norm_mlp = rms_bwd(x3, sp["norm_mlp_g"], dhf, dy, name="rms_mlp_bwd")
    dco = matmul(dx3, w["w_co"], tb=True, name="mm_d_co")
    g_co = matmul(co, dx3, ta=True, name="mm_g_co")
    dcq, dckv, g_xq, g_xk = xattn_bwd(cq, ckv, sp["xattn_qnorm_g"], sp["xattn_knorm_g"], dco, B=B)
    g_cq = matmul(hq, dcq, ta=True, name="mm_g_cq")
    dhq = matmul(dcq, w["w_cq"], tb=True, name="mm_d_hq")
    g_ckv = matmul(hm, dckv, ta=True, name="mm_g_ckv")
    dhm = matmul(dckv, w["w_ckv"], tb=True, name="mm_d_hm")
    _, g_mem_norm = rms_bwd(mem, sp["mem_norm_g"], dhm, None, name="rms_mem_bwd")
    dx2, g_norm_xattn = rms_bwd(x2, sp["norm_xattn_g"], dhq, dx3, name="rms_xattn_bwd")
    doab = matmul(dx2, w["w_out"], tb=True, name="mm_d_oab")
    g_out = matmul(oab, dx2, ta=True, name="mm_g_out")
    dqn, dkn, dv_f, dccol, dcrow, dgo2 = fox_core_bwd(qn, kn, vb, ccol, crow, go2, o_raw, lse, doab, B=B)
    dq_f, dk_f, dgq8, dgk8 = fox_prep_bwd(P, gq8, gk8, dqn, dkn)
    dGq, dGk, dGv, dgt, dz, g_gdn_on = gdn_bwd(G, gates, P, sp["gdn_onorm_g"], ob_raw, states, doab, B=B)
    dPg, g_conv = gdn_prep_bwd(P, w["conv_w"], jnp.concatenate([dGq, dGk, dGv], axis=1), B=B)
    dc = (dccol[..., 0] + dcrow.reshape(B, FOX_HEADS, S)).transpose(0, 2, 1).reshape(T, FOX_HEADS)
    dgates = dgt + jnp.pad(dc, ((0, 0), (SM_F, LANES - SM_F - FOX_HEADS)))
    dsmall, par = gates_bwd(P, bias, alog, dgates, B=B)
    dP = jnp.concatenate([dq_f, dk_f, dv_f, dPg, dz, dsmall, jnp.zeros((T, IN_ALIGNED - COL_SMALL - LANES), MXU_DTYPE)], axis=1)
    g_wa = matmul(h1, dP, ta=True, name="mm_g_in", tn=IN_TILE)
    dh1 = matmul(dP, w["wa"], tb=True, name="mm_d_h1", tk=IN_TILE)
    dx, g_norm_mix = rms_bwd(x, sp["norm_mix_g"], dh1, dx2, name="rms_mix_bwd")

    fold = lambda g: jnp.sum(g.reshape(-1, FOX_HEAD_DIM), axis=0, keepdims=True)
    big = dict(w_in=unalign_w_in(g_wa), w_out=g_out, w_cq=g_cq, w_ckv=g_ckv, w_co=g_co, w_mlp1=g_mlp1, w_mlp2=g_mlp2)
    small = dict(norm_mix_g=g_norm_mix, fox_qnorm_g=fold(dgq8), fox_knorm_g=fold(dgk8),
                 fox_f_bias=par[0:1, SM_F:SM_F + FOX_HEADS], fox_onorm_g=fold(dgo2), gdn_conv_w=g_conv,
                 gdn_A_log=par[1:2, SM_A:SM_A + GDN_HEADS], gdn_dt_bias=par[0:1, SM_A:SM_A + GDN_HEADS],
                 gdn_onorm_g=g_gdn_on, norm_xattn_g=g_norm_xattn, mem_norm_g=g_mem_norm,
                 xattn_qnorm_g=g_xq, xattn_knorm_g=g_xk, norm_mlp_g=g_norm_mlp)
    return loss, dx, big, small


MESH_IDS = pl.DeviceIdType.MESH
N_CHIPS = 4
HBM_SPEC = pl.BlockSpec(memory_space=pltpu.HBM)
PACK_ROWS = 30720
PACK_HALF = PACK_ROWS // 2
PACK_BLOCK = 3072


def _place():
    return lax.axis_index("x"), lax.axis_index("y"), lax.axis_index("c")


def _other_chips(x, y):
    return [(1 - x, y), (x, 1 - y), (1 - x, 1 - y)]


def _remote(src, dst, send_sem, recv_sem, to):
    return pltpu.make_async_remote_copy(src_ref=src, dst_ref=dst, send_sem=send_sem, recv_sem=recv_sem,
                                        device_id=to, device_id_type=MESH_IDS)


def all_gather_shards(packed):
    half = PACK_HALF

    def body(src_ref, out_ref, send_sems, recv_sems):
        x, y, c = _place()
        me_chip = 2 * x + y
        sibling = (x, y, 1 - c)
        chips = _other_chips(x, y)

        def rows(chip, core):
            return out_ref.at[chip, pl.ds(core * half, half), :]

        sends = [_remote(src_ref.at[pl.ds(c * half, half), :], rows(me_chip, c), send_sems.at[j], recv_sems.at[j], (px, py, c))
                 for j, (px, py) in enumerate(chips)]
        for cp in sends:
            cp.start()
        passed = []
        for j, (px, py) in enumerate(chips):
            theirs = rows(2 * px + py, c)
            _remote(theirs, theirs, send_sems.at[j], recv_sems.at[j], (px, py, c)).wait_recv()
            cp = _remote(theirs, theirs, send_sems.at[3 + j], recv_sems.at[3 + j], sibling)
            cp.start()
            passed.append(cp)
        for j, (px, py) in enumerate(chips):
            theirs = rows(2 * px + py, 1 - c)
            _remote(theirs, theirs, send_sems.at[3 + j], recv_sems.at[3 + j], sibling).wait_recv()
        for cp in sends + passed:
            cp.wait_send()

    return pl.pallas_call(
        body, name="all_gather_shards", in_specs=[HBM_SPEC], out_specs=HBM_SPEC,
        out_shape=jax.ShapeDtypeStruct((N_CHIPS,) + packed.shape, packed.dtype),
        scratch_shapes=[pltpu.SemaphoreType.DMA((6,)), pltpu.SemaphoreType.DMA((6,))],
    )(packed)


def exchange_core_halves(G):
    half = PACK_HALF

    def body(g_ref, land_ref, send_sem, recv_sem):
        x, y, c = _place()
        cp = _remote(g_ref.at[:, pl.ds((1 - c) * half, half), :], land_ref, send_sem, recv_sem, (x, y, 1 - c))
        cp.start()
        cp.wait()

    return pl.pallas_call(
        body, name="exchange_core_halves", in_specs=[HBM_SPEC], out_specs=HBM_SPEC,
        out_shape=jax.ShapeDtypeStruct((N_CHIPS, half, LANES), G.dtype),
        scratch_shapes=[pltpu.SemaphoreType.DMA(()), pltpu.SemaphoreType.DMA(())],
    )(G)


def add_core_halves(G, land, core):
    nb = PACK_HALF // PACK_BLOCK

    def body(c_ref, g_ref, l_ref, o_ref):
        o_ref[...] = (g_ref[...].astype(f32) + l_ref[...].astype(f32)).astype(o_ref.dtype)

    blk = (1, PACK_BLOCK, LANES)
    return pl.pallas_call(
        body, name="add_core_halves",
        grid_spec=pltpu.PrefetchScalarGridSpec(
            num_scalar_prefetch=1, grid=(N_CHIPS, nb),
            in_specs=[pl.BlockSpec(blk, lambda k, i, c_ref: (k, c_ref[0] * nb + i, 0)),
                      pl.BlockSpec(blk, lambda k, i, c_ref: (k, i, 0))],
            out_specs=pl.BlockSpec(blk, lambda k, i, c_ref: (k, i, 0))),
        out_shape=jax.ShapeDtypeStruct(land.shape, land.dtype),
        compiler_params=_cparams("parallel", "parallel"),
    )(core, G, land)


def scatter_to_chips(part):
    def body(p_ref, land_ref, send_sems, recv_sems):
        x, y, c = _place()
        me_chip = 2 * x + y
        chips = _other_chips(x, y)
        sends = [_remote(p_ref.at[2 * px + py], land_ref.at[me_chip], send_sems.at[j], recv_sems.at[j], (px, py, c))
                 for j, (px, py) in enumerate(chips)]
        for cp in sends:
            cp.start()
        for j, (px, py) in enumerate(chips):
            slot = land_ref.at[2 * px + py]
            _remote(slot, slot, send_sems.at[j], recv_sems.at[j], (px, py, c)).wait_recv()
        for cp in sends:
            cp.wait_send()

    return pl.pallas_call(
        body, name="scatter_to_chips", in_specs=[HBM_SPEC], out_specs=HBM_SPEC,
        out_shape=jax.ShapeDtypeStruct(part.shape, part.dtype),
        scratch_shapes=[pltpu.SemaphoreType.DMA((3,)), pltpu.SemaphoreType.DMA((3,))],
    )(part)


def sum_chips(part, land, order):
    nb = PACK_HALF // PACK_BLOCK

    def body(order_ref, p_ref, l1_ref, l2_ref, l3_ref, o_ref):
        o_ref[...] = ((p_ref[0].astype(f32) + l1_ref[0].astype(f32)) + l2_ref[0].astype(f32)) + l3_ref[0].astype(f32)

    slot = lambda j: pl.BlockSpec((1, PACK_BLOCK, LANES), lambda i, order_ref: (order_ref[j], i, 0))
    return pl.pallas_call(
        body, name="sum_chips",
        grid_spec=pltpu.PrefetchScalarGridSpec(
            num_scalar_prefetch=1, grid=(nb,), in_specs=[slot(0), slot(1), slot(2), slot(3)],
            out_specs=pl.BlockSpec((PACK_BLOCK, LANES), lambda i, order_ref: (i, 0))),
        out_shape=jax.ShapeDtypeStruct((PACK_HALF, LANES), f32),
        compiler_params=_cparams("parallel"),
    )(order, part, land, land, land)


def swap_core_halves(red):
    def body(r_ref, out_ref, send_sem, recv_sem):
        x, y, c = _place()
        cp = _remote(r_ref, out_ref, send_sem, recv_sem, (x, y, 1 - c))
        cp.start()
        cp.wait()

    return pl.pallas_call(
        body, name="swap_core_halves", in_specs=[HBM_SPEC], out_specs=HBM_SPEC,
        out_shape=jax.ShapeDtypeStruct(red.shape, red.dtype),
        scratch_shapes=[pltpu.SemaphoreType.DMA(()), pltpu.SemaphoreType.DMA(())],
    )(red)


N_DEV = 8


def all_reduce_small(v):
    def body(src_ref, out_ref, land_ref, send_sems, recv_sems):
        x, y, c = _place()
        me = 4 * x + 2 * y + c
        copies = []
        for r in range(1, N_DEV):
            peer = ((1 - x) if r & 4 else x, (1 - y) if r & 2 else y, (1 - c) if r & 1 else c)
            copies.append(_remote(src_ref, land_ref.at[r], send_sems.at[r - 1], recv_sems.at[r - 1], peer))
        for cp in copies:
            cp.start()
        land_ref[0] = src_ref[...]
        for cp in copies:
            cp.wait()
        acc = land_ref[me]
        for d in range(1, N_DEV):
            acc = acc + land_ref[jnp.bitwise_xor(me, d)]
        out_ref[...] = acc

    vm = pl.BlockSpec(memory_space=pltpu.VMEM)
    return pl.pallas_call(
        body, name="all_reduce_small", in_specs=[vm], out_specs=vm,
        out_shape=jax.ShapeDtypeStruct(v.shape, v.dtype),
        scratch_shapes=[pltpu.VMEM((N_DEV,) + v.shape, v.dtype),
                        pltpu.SemaphoreType.DMA((N_DEV - 1,)), pltpu.SemaphoreType.DMA((N_DEV - 1,))],
    )(v)


def adamw(w, g, m, v, *, name, tr):
    R, C = w.shape
    tr = min(tr, R)

    def body(w_ref, g_ref, m_ref, v_ref, d_ref, nm_ref, nv_ref):
        gv = g_ref[...]
        nm = ADAM_B1 * m_ref[...] + (1.0 - ADAM_B1) * gv
        nv = ADAM_B2 * v_ref[...] + (1.0 - ADAM_B2) * jnp.square(gv)
        m_hat = nm / (1.0 - ADAM_B1 ** ADAM_STEP)
        v_hat = nv / (1.0 - ADAM_B2 ** ADAM_STEP)
        d_ref[...] = -ADAM_LR * (m_hat / (jnp.sqrt(v_hat) + ADAM_EPS) + ADAM_WD * w_ref[...])
        nm_ref[...] = nm
        nv_ref[...] = nv

    blk = pl.BlockSpec((tr, C), lambda i: (i, 0))
    out = jax.ShapeDtypeStruct((R, C), f32)
    return pl.pallas_call(
        body, name=name, grid=(R // tr,), in_specs=[blk] * 4, out_specs=[blk] * 3, out_shape=[out] * 3,
        compiler_params=_cparams("parallel"),
    )(w, g, m, v)


BIG_SHARDS = (("w_in", (1024, 900), True), ("w_out", (256, 1024), False), ("w_cq", (256, 512), False),
              ("w_ckv", (256, 1024), False), ("w_co", (512, 256), True), ("w_mlp1", (1024, 1024), True),
              ("w_mlp2", (1024, 1024), False))
CONV_SHARD = (CONV_WIDTH, 3 * GDN_WIDTH // N_CHIPS)
SMALL_DIMS = (("norm_mix_g", 1024), ("fox_qnorm_g", 64), ("fox_knorm_g", 64), ("fox_f_bias", 8), ("fox_onorm_g", 64),
              ("gdn_A_log", 4), ("gdn_dt_bias", 4), ("gdn_onorm_g", 128), ("norm_xattn_g", 1024), ("mem_norm_g", 1024),
              ("xattn_qnorm_g", 128), ("xattn_knorm_g", 128), ("norm_mlp_g", 1024))
WEIGHT_ORDER = ("norm_mix_g", "w_in", "fox_qnorm_g", "fox_knorm_g", "fox_f_bias", "fox_onorm_g", "gdn_conv_w", "gdn_A_log",
                "gdn_dt_bias", "gdn_onorm_g", "w_out", "norm_xattn_g", "mem_norm_g", "w_cq", "w_ckv", "xattn_qnorm_g",
                "xattn_knorm_g", "w_co", "norm_mlp_g", "w_mlp1", "w_mlp2")


def _pack_rows(pieces, rows, lead=()):
    flat = []
    for p in pieces:
        p = p.reshape(lead + (-1,))
        pad = (-p.shape[-1]) % LANES
        flat.append(jnp.pad(p, [(0, 0)] * len(lead) + [(0, pad)]) if pad else p)
    cat = jnp.concatenate(flat, axis=-1)
    cat = jnp.pad(cat, [(0, 0)] * len(lead) + [(0, rows * LANES - cat.shape[-1])])
    return cat.reshape(lead + (rows, LANES))


def _unpack_rows(buf, sizes, lead=()):
    flat = buf.reshape(lead + (-1,))
    out, off = [], 0
    for n in sizes:
        out.append(flat[..., off:off + n])
        off += n + (-n) % LANES
    return out


def _conv_to_wire(conv):
    return lax.bitcast_convert_type(conv, bf16)


def _conv_from_wire(wire):
    return lax.bitcast_convert_type(wire, f32)


SMALL_ROWS = 96
SMALL_ADAM_ROWS = 56


def kernel(x, mem, norm_mix_g, w_in, fox_qnorm_g, fox_knorm_g, fox_f_bias, fox_onorm_g, gdn_conv_w, gdn_A_log, gdn_dt_bias, gdn_onorm_g, w_out, norm_xattn_g, mem_norm_g, w_cq, w_ckv, xattn_qnorm_g, xattn_knorm_g, w_co, norm_mlp_g, w_mlp1, w_mlp2, loss_target, m_norm_mix_g, m_w_in, m_fox_qnorm_g, m_fox_knorm_g, m_fox_f_bias, m_fox_onorm_g, m_gdn_conv_w, m_gdn_A_log, m_gdn_dt_bias, m_gdn_onorm_g, m_w_out, m_norm_xattn_g, m_mem_norm_g, m_w_cq, m_w_ckv, m_xattn_qnorm_g, m_xattn_knorm_g, m_w_co, m_norm_mlp_g, m_w_mlp1, m_w_mlp2, v_norm_mix_g, v_w_in, v_fox_qnorm_g, v_fox_knorm_g, v_fox_f_bias, v_fox_onorm_g, v_gdn_conv_w, v_gdn_A_log, v_gdn_dt_bias, v_gdn_onorm_g, v_w_out, v_norm_xattn_g, v_mem_norm_g, v_w_cq, v_w_ckv, v_xattn_qnorm_g, v_xattn_knorm_g, v_w_co, v_norm_mlp_g, v_w_mlp1, v_w_mlp2):
    wts = dict(norm_mix_g=norm_mix_g, w_in=w_in, fox_qnorm_g=fox_qnorm_g, fox_knorm_g=fox_knorm_g, fox_f_bias=fox_f_bias,
               fox_onorm_g=fox_onorm_g, gdn_conv_w=gdn_conv_w, gdn_A_log=gdn_A_log, gdn_dt_bias=gdn_dt_bias,
               gdn_onorm_g=gdn_onorm_g, w_out=w_out, norm_xattn_g=norm_xattn_g, mem_norm_g=mem_norm_g, w_cq=w_cq, w_ckv=w_ckv,
               xattn_qnorm_g=xattn_qnorm_g, xattn_knorm_g=xattn_knorm_g, w_co=w_co, norm_mlp_g=norm_mlp_g, w_mlp1=w_mlp1,
               w_mlp2=w_mlp2)
    mom = dict(norm_mix_g=m_norm_mix_g, w_in=m_w_in, fox_qnorm_g=m_fox_qnorm_g, fox_knorm_g=m_fox_knorm_g,
               fox_f_bias=m_fox_f_bias, fox_onorm_g=m_fox_onorm_g, gdn_conv_w=m_gdn_conv_w, gdn_A_log=m_gdn_A_log,
               gdn_dt_bias=m_gdn_dt_bias, gdn_onorm_g=m_gdn_onorm_g, w_out=m_w_out, norm_xattn_g=m_norm_xattn_g,
               mem_norm_g=m_mem_norm_g, w_cq=m_w_cq, w_ckv=m_w_ckv, xattn_qnorm_g=m_xattn_qnorm_g,
               xattn_knorm_g=m_xattn_knorm_g, w_co=m_w_co, norm_mlp_g=m_norm_mlp_g, w_mlp1=m_w_mlp1, w_mlp2=m_w_mlp2)
    var = dict(norm_mix_g=v_norm_mix_g, w_in=v_w_in, fox_qnorm_g=v_fox_qnorm_g, fox_knorm_g=v_fox_knorm_g,
               fox_f_bias=v_fox_f_bias, fox_onorm_g=v_fox_onorm_g, gdn_conv_w=v_gdn_conv_w, gdn_A_log=v_gdn_A_log,
               gdn_dt_bias=v_gdn_dt_bias, gdn_onorm_g=v_gdn_onorm_g, w_out=v_w_out, norm_xattn_g=v_norm_xattn_g,
               mem_norm_g=v_mem_norm_g, w_cq=v_w_cq, w_ckv=v_w_ckv, xattn_qnorm_g=v_xattn_qnorm_g,
               xattn_knorm_g=v_xattn_knorm_g, w_co=v_w_co, norm_mlp_g=v_norm_mlp_g, w_mlp1=v_w_mlp1, w_mlp2=v_w_mlp2)
    B, S, D = x.shape
    T = B * S
    big_sizes = [r * c for _, (r, c), _ in BIG_SHARDS]
    conv_wire = 2 * CONV_SHARD[0] * CONV_SHARD[1]

    shards = [wts[n][0].astype(MXU_DTYPE) for n, _, _ in BIG_SHARDS] + [_conv_to_wire(gdn_conv_w[0])]
    chip = 2 * lax.axis_index("x") + lax.axis_index("y")
    packed = _pack_rows(shards, PACK_ROWS)
    gathered = lax.dynamic_update_slice(all_gather_shards(packed), packed[None], (chip, 0, 0))
    pieces =_unpack_rows(gathered, big_sizes + [conv_wire], lead=(N_CHIPS,))
    full = {}
    for (n, (r, c), by_cols), p in zip(BIG_SHARDS, pieces):
        p = p.reshape(N_CHIPS, r, c)
        full[n] = p.transpose(1, 0, 2).reshape(r, N_CHIPS * c) if by_cols else p.reshape(N_CHIPS * r, c)
    conv_full = _conv_from_wire(pieces[-1].reshape((N_CHIPS,) + CONV_SHARD + (2,)))
    conv_full = conv_full.transpose(1, 0, 2).reshape(CONV_WIDTH, 3 * GDN_WIDTH)
    w = dict(wa=align_w_in(full["w_in"]), w_out=full["w_out"], w_cq=full["w_cq"], w_ckv=full["w_ckv"], w_co=full["w_co"],
             w_mlp1=full["w_mlp1"], w_mlp2=full["w_mlp2"], conv_w=conv_full)
    sp = {n: wts[n] for n, _ in SMALL_DIMS}

    loss_part, grad_x, g_big, g_small = local_step(x.reshape(T, D), mem.reshape(-1, D), loss_target.reshape(T, D), w, sp, B=B)

    small_pieces = [g_small[n] for n, _ in SMALL_DIMS] + [g_small["gdn_conv_w"], loss_part]
    small_sizes = [d for _, d in SMALL_DIMS] + [CONV_WIDTH * 3 * GDN_WIDTH, LANES]
    red_small = _unpack_rows(all_reduce_small(_pack_rows(small_pieces, SMALL_ROWS)), small_sizes)
    grads = {n: p.reshape(1, d) for (n, d), p in zip(SMALL_DIMS, red_small)}
    chip = 2 * lax.axis_index("x") + lax.axis_index("y")
    conv_grad = lax.dynamic_slice(red_small[-2].reshape(CONV_WIDTH, 3 * GDN_WIDTH), (0, chip * CONV_SHARD[1]), CONV_SHARD)
    grads["gdn_conv_w"] = conv_grad.reshape((1,) + CONV_SHARD)
    loss = red_small[-1][0]

    by_chip = []
    for n, (r, c), by_cols in BIG_SHARDS:
        g = g_big[n]
        g = g.reshape(r, N_CHIPS, c).transpose(1, 0, 2) if by_cols else g.reshape(N_CHIPS, r, c)
        by_chip.append(g.astype(WIRE_DTYPE))
    G = _pack_rows(by_chip, PACK_ROWS, lead=(N_CHIPS,))
    core = lax.axis_index("c").astype(jnp.int32).reshape(1)
    chip_part = add_core_halves(G, exchange_core_halves(G), core)
    order = jnp.stack([chip, chip ^ 2, chip ^ 1, chip ^ 3]).astype(jnp.int32)
    mine = sum_chips(chip_part, scatter_to_chips(chip_part), order)
    theirs = swap_core_halves(mine)
    south = core[0] == 0
    reduced = jnp.concatenate([jnp.where(south, mine, theirs), jnp.where(south, theirs, mine)], axis=0)
    for (n, (r, c), _), p in zip(BIG_SHARDS, _unpack_rows(reduced, big_sizes)):
        grads[n] = p.reshape(1, r, c)

    delta, new_m, new_v = {}, {}, {}
    for n, (r, c), _ in BIG_SHARDS:
        d, nm, nv = adamw(wts[n][0], grads[n][0], mom[n][0], var[n][0], name="adamw_" + n, tr=256)
        delta[n], new_m[n], new_v[n] = d[None], nm[None], nv[None]
    small_names = [n for n, _ in SMALL_DIMS] + ["gdn_conv_w"]
    small_sz = [d for _, d in SMALL_DIMS] + [CONV_SHARD[0] * CONV_SHARD[1]]
    packed4 = [_pack_rows([src[n] for n in small_names], SMALL_ADAM_ROWS) for src in (wts, grads, mom, var)]
    outs = adamw(*packed4, name="adamw_small", tr=SMALL_ADAM_ROWS)
    for dst, buf in zip((delta, new_m, new_v), outs):
        for n, p in zip(small_names, _unpack_rows(buf, small_sz)):
            dst[n] = p.reshape(wts[n].shape)

    return (loss, grad_x.reshape(B, S, D), *[grads[n] for n in WEIGHT_ORDER], *[delta[n] for n in WEIGHT_ORDER],
            *[new_m[n] for n in WEIGHT_ORDER], *[new_v[n] for n in WEIGHT_ORDER])
```

```python
import functools

import jax
import jax.numpy as jnp
import numpy as np
from jax import lax
from jax.experimental import pallas as pl
from jax.experimental.pallas import tpu as pltpu

f32 = jnp.float32
bf16 = jnp.bfloat16
MXU_DTYPE = jnp.bfloat16
WIRE_DTYPE = jnp.bfloat16
INV_PRECISION = lax.Precision.HIGH

D_MODEL = 1024
FOX_HEADS = 8
FOX_HEAD_DIM = 64
FOX_WIDTH = 512
GDN_HEADS = 4
GDN_HEAD_DIM = 128
GDN_WIDTH = 512
CONV_WIDTH = 4
GDN_CHUNK = 64
XATTN_HEADS = 4
XATTN_HEAD_DIM = 128
XATTN_WIDTH = 512
D_FF = 4096
IN_DIM = 3600
EPS = 1e-6
NEG_INF = -1e30
LANES = 128
ADAM_LR = 0.001
ADAM_B1 = 0.9
ADAM_B2 = 0.999
ADAM_EPS = 1e-08
ADAM_WD = 0.01
ADAM_STEP = 10
VMEM_LIMIT = 48 * 1024 * 1024

COL_FOX = 0
COL_GDN = 1536
COL_Z = 3072
COL_SMALL = 3584
IN_ALIGNED = 3840
IN_TILE = 768
SM_F = 0
SM_B = 8
SM_A = 12


def _cparams(*sem):
    return pltpu.CompilerParams(dimension_semantics=sem, vmem_limit_bytes=VMEM_LIMIT)


def _mx(v):
    return v.astype(MXU_DTYPE)


def _dot(a, b, dims, precision=None):
    return lax.dot_general(a, b, (dims, ((), ())), preferred_element_type=f32, precision=precision)


def _dotm(a, b, dims):
    return _dot(_mx(a), _mx(b), dims)


NN = ((1,), (0,))
NT = ((1,), (1,))
TN = ((0,), (0,))


def matmul(a, b, *, name, ta=False, tb=False, residual=None, relu2_out=False, relu2_bwd_aux=None,
           out_dtype=f32, tm=1024, tn=1024, tk=1024):
    M, K = (a.shape[1], a.shape[0]) if ta else a.shape
    N = b.shape[0] if tb else b.shape[1]
    tm, tn, tk = min(tm, M), min(tn, N), min(tk, K)
    assert M % tm == 0 and N % tn == 0 and K % tk == 0, (name, M, N, K)
    nk = K // tk
    has_res = residual is not None
    has_aux = relu2_bwd_aux is not None

    def body(*refs):
        a_ref, b_ref = refs[0], refs[1]
        pos = 2
        res_ref = aux_ref = None
        if has_res:
            res_ref = refs[pos]
            pos += 1
        if has_aux:
            aux_ref = refs[pos]
            pos += 1
        o_ref = refs[pos]
        pos += 1
        act_ref = None
        if relu2_out:
            act_ref = refs[pos]
            pos += 1
        acc_ref = refs[pos]
        k = pl.program_id(2)

        @pl.when(k == 0)
        def _():
            acc_ref[...] = jnp.zeros_like(acc_ref)

        dims = ((0,) if ta else (1,), (1,) if tb else (0,))
        acc_ref[...] += _dot(_mx(a_ref[...]), _mx(b_ref[...]), dims)

        @pl.when(k == nk - 1)
        def _():
            r = acc_ref[...]
            if has_res:
                r = r + res_ref[...]
            if has_aux:
                r = r * (2.0 * jnp.maximum(aux_ref[...], 0.0))
            o_ref[...] = r.astype(o_ref.dtype)
            if relu2_out:
                act_ref[...] = jnp.square(jnp.maximum(r, 0.0)).astype(act_ref.dtype)

    a_spec = pl.BlockSpec((tk, tm), lambda i, j, k: (k, i)) if ta else pl.BlockSpec((tm, tk), lambda i, j, k: (i, k))
    b_spec = pl.BlockSpec((tn, tk), lambda i, j, k: (j, k)) if tb else pl.BlockSpec((tk, tn), lambda i, j, k: (k, j))
    o_spec = pl.BlockSpec((tm, tn), lambda i, j, k: (i, j))
    in_specs, args = [a_spec, b_spec], [a, b]
    if has_res:
        in_specs.append(o_spec)
        args.append(residual)
    if has_aux:
        in_specs.append(o_spec)
        args.append(relu2_bwd_aux)
    out_shape = [jax.ShapeDtypeStruct((M, N), out_dtype)]
    out_specs = [o_spec]
    if relu2_out:
        out_shape.append(jax.ShapeDtypeStruct((M, N), MXU_DTYPE))
        out_specs.append(o_spec)
    res = pl.pallas_call(
        body, name=name, grid=(M // tm, N // tn, nk), in_specs=in_specs, out_specs=out_specs, out_shape=out_shape,
        scratch_shapes=[pltpu.VMEM((tm, tn), f32)],
        compiler_params=_cparams("parallel", "parallel", "arbitrary"),
    )(*args)
    return res if relu2_out else res[0]


def rms_fwd(x, g, *, name, tr=512):
    R, D = x.shape
    tr = min(tr, R)

    def body(x_ref, g_ref, o_ref):
        xv = x_ref[...]
        y = xv * lax.rsqrt(jnp.mean(xv * xv, axis=-1, keepdims=True) + EPS)
        o_ref[...] = (y * g_ref[...]).astype(o_ref.dtype)

    return pl.pallas_call(
        body, name=name, grid=(R // tr,),
        in_specs=[pl.BlockSpec((tr, D), lambda i: (i, 0)), pl.BlockSpec((1, D), lambda i: (0, 0))],
        out_specs=pl.BlockSpec((tr, D), lambda i: (i, 0)),
        out_shape=jax.ShapeDtypeStruct((R, D), MXU_DTYPE),
        compiler_params=_cparams("parallel"),
    )(x, g)


def rms_bwd(x, g, dh, residual, *, name, tr=512):
    R, D = x.shape
    tr = min(tr, R)
    has_res = residual is not None

    def body(*refs):
        if has_res:
            x_ref, g_ref, dh_ref, res_ref, dx_ref, dg_ref = refs
        else:
            x_ref, g_ref, dh_ref, dx_ref, dg_ref = refs
        xv = x_ref[...]
        rstd = lax.rsqrt(jnp.mean(xv * xv, axis=-1, keepdims=True) + EPS)
        xhat = xv * rstd
        dh = dh_ref[...].astype(f32)
        gd = dh * g_ref[...]
        dx = rstd * (gd - xhat * jnp.mean(gd * xhat, axis=-1, keepdims=True))
        if has_res:
            dx = dx + res_ref[...]
        dx_ref[...] = dx

        @pl.when(pl.program_id(0) == 0)
        def _():
            dg_ref[...] = jnp.zeros_like(dg_ref)

        dg_ref[...] += jnp.sum(dh * xhat, axis=0, keepdims=True)

    row = pl.BlockSpec((tr, D), lambda i: (i, 0))
    vec = pl.BlockSpec((1, D), lambda i: (0, 0))
    in_specs = [row, vec, row] + ([row] if has_res else [])
    args = [x, g, dh] + ([residual] if has_res else [])
    return pl.pallas_call(
        body, name=name, grid=(R // tr,), in_specs=in_specs, out_specs=[row, vec],
        out_shape=[jax.ShapeDtypeStruct((R, D), f32), jax.ShapeDtypeStruct((1, D), f32)],
        compiler_params=_cparams("arbitrary"),
    )(*args)


def loss_head(y, target, *, tr=512):
    R, D = y.shape
    tr = min(tr, R)

    def body(y_ref, t_ref, dy_ref, loss_ref):
        e = y_ref[...] - t_ref[...]
        dy_ref[...] = e * (1.0 / D)

        @pl.when(pl.program_id(0) == 0)
        def _():
            loss_ref[...] = jnp.zeros_like(loss_ref)

        part = 0.5 * jnp.sum(jnp.mean(e * e, axis=-1, keepdims=True), axis=0, keepdims=True)
        loss_ref[...] += jnp.broadcast_to(part, loss_ref.shape)

    row = pl.BlockSpec((tr, D), lambda i: (i, 0))
    return pl.pallas_call(
        body, name="loss_head", grid=(R // tr,), in_specs=[row, row],
        out_specs=[row, pl.BlockSpec((1, LANES), lambda i: (0, 0))],
        out_shape=[jax.ShapeDtypeStruct((R, D), f32), jax.ShapeDtypeStruct((1, LANES), f32)],
        compiler_params=_cparams("arbitrary"),
    )(y, target)


def _head_rms(v, g):
    r = lax.rsqrt(jnp.mean(v * v, axis=-1, keepdims=True) + EPS)
    return v * r * g, r


def _head_rms_bwd(v, r, g, dn):
    vhat = v * r
    gd = dn * g
    dv = r * (gd - vhat * jnp.mean(gd * vhat, axis=-1, keepdims=True))
    return dv, jnp.sum(dn * vhat, axis=0, keepdims=True)


def _softmax_rows(s):
    m = jnp.max(s, axis=-1, keepdims=True)
    e = jnp.exp(s - m)
    return e / jnp.sum(e, axis=-1, keepdims=True)


def xattn_fwd(cq, ckv, gq, gk, *, B, tq=512):
    T = cq.shape[0]
    S = T // B
    M = ckv.shape[0] // B
    tq = min(tq, S)
    nq = S // tq
    scale = XATTN_HEAD_DIM ** -0.5

    def body(q_ref, k_ref, v_ref, gq_ref, gk_ref, o_ref):
        qn, _ = _head_rms(q_ref[...], gq_ref[...])
        kn, _ = _head_rms(k_ref[...], gk_ref[...])
        p = _softmax_rows(_dot(_mx(qn), _mx(kn), NT) * scale)
        o_ref[...] = _dot(_mx(p), _mx(v_ref[...]), NN).astype(o_ref.dtype)

    hd = XATTN_HEAD_DIM
    vec = pl.BlockSpec((1, hd), lambda b, h, i: (0, 0))
    return pl.pallas_call(
        body, name="xattn_fwd", grid=(B, XATTN_HEADS, nq),
        in_specs=[pl.BlockSpec((tq, hd), lambda b, h, i: (b * nq + i, h)),
                  pl.BlockSpec((M, hd), lambda b, h, i: (b, h)),
                  pl.BlockSpec((M, hd), lambda b, h, i: (b, XATTN_HEADS + h)), vec, vec],
        out_specs=pl.BlockSpec((tq, hd), lambda b, h, i: (b * nq + i, h)),
        out_shape=jax.ShapeDtypeStruct((T, XATTN_WIDTH), MXU_DTYPE),
        compiler_params=_cparams("parallel", "parallel", "parallel"),
    )(cq, ckv, ckv, gq, gk)


def xattn_bwd(cq, ckv, gq, gk, dco, *, B, tq=512):
    T = cq.shape[0]
    S = T // B
    M = ckv.shape[0] // B
    tq = min(tq, S)
    nq = S // tq
    scale = XATTN_HEAD_DIM ** -0.5
    hd = XATTN_HEAD_DIM

    def body(q_ref, k_ref, v_ref, gq_ref, gk_ref, do_ref, dq_ref, dk_ref, dv_ref, dgq_ref, dgk_ref, dkn_acc, dv_acc):
        b, h, i = pl.program_id(0), pl.program_id(1), pl.program_id(2)

        @pl.when((b == 0) & (h == 0) & (i == 0))
        def _():
            dgq_ref[...] = jnp.zeros_like(dgq_ref)
            dgk_ref[...] = jnp.zeros_like(dgk_ref)

        @pl.when(i == 0)
        def _():
            dkn_acc[...] = jnp.zeros_like(dkn_acc)
            dv_acc[...] = jnp.zeros_like(dv_acc)

        q, k, v = q_ref[...], k_ref[...], v_ref[...]
        gqv, gkv = gq_ref[...], gk_ref[...]
        qn, rq = _head_rms(q, gqv)
        kn, rk = _head_rms(k, gkv)
        p = _softmax_rows(_dot(_mx(qn), _mx(kn), NT) * scale)
        do = do_ref[...]
        dv_acc[...] += _dot(_mx(p), _mx(do), TN)
        dp = _dot(_mx(do), _mx(v), NT)
        ds = p * (dp - jnp.sum(dp * p, axis=-1, keepdims=True)) * scale
        dqn = _dot(_mx(ds), _mx(kn), NN)
        dkn_acc[...] += _dot(_mx(ds), _mx(qn), TN)
        dq, dgq = _head_rms_bwd(q, rq, gqv, dqn)
        dq_ref[...] = dq.astype(dq_ref.dtype)
        dgq_ref[...] += dgq

        @pl.when(i == nq - 1)
        def _():
            dk, dgk = _head_rms_bwd(k, rk, gkv, dkn_acc[...])
            dk_ref[...] = dk.astype(dk_ref.dtype)
            dv_ref[...] = dv_acc[...].astype(dv_ref.dtype)
            dgk_ref[...] += dgk

    vec = pl.BlockSpec((1, hd), lambda b, h, i: (0, 0))
    qspec = pl.BlockSpec((tq, hd), lambda b, h, i: (b * nq + i, h))
    kspec = pl.BlockSpec((M, hd), lambda b, h, i: (b, h))
    vspec = pl.BlockSpec((M, hd), lambda b, h, i: (b, XATTN_HEADS + h))
    dq, dk, dv, dgq, dgk = pl.pallas_call(
        body, name="xattn_bwd", grid=(B, XATTN_HEADS, nq),
        in_specs=[qspec, kspec, vspec, vec, vec, qspec],
        out_specs=[qspec, kspec, kspec, vec, vec],
        out_shape=[jax.ShapeDtypeStruct((T, XATTN_WIDTH), MXU_DTYPE),
                   jax.ShapeDtypeStruct((B * M, XATTN_WIDTH), MXU_DTYPE),
                   jax.ShapeDtypeStruct((B * M, XATTN_WIDTH), MXU_DTYPE),
                   jax.ShapeDtypeStruct((1, hd), f32), jax.ShapeDtypeStruct((1, hd), f32)],
        scratch_shapes=[pltpu.VMEM((M, hd), f32), pltpu.VMEM((M, hd), f32)],
        compiler_params=_cparams("arbitrary", "arbitrary", "arbitrary"),
    )(cq, ckv, ckv, gq, gk, dco)
    return dq, jnp.concatenate([dk, dv], axis=1), dgq, dgk


FOX_PAIRS = FOX_HEADS // 2


def _fox_scores(qn, kn, ccol, crow, q0, tq, S, scale):
    s = _dot(_mx(qn), _mx(kn), NT) * scale + ccol - crow
    qpos = q0 + lax.broadcasted_iota(jnp.int32, (tq, S), 0)
    kpos = lax.broadcasted_iota(jnp.int32, (tq, S), 1)
    return jnp.where(kpos <= qpos, s, NEG_INF)


def fox_fwd(P, ccol, crow, gq, gk, go, *, B, tq=256):
    T = P.shape[0]
    S = T // B
    tq = min(tq, S)
    nq = S // tq
    hd = FOX_HEAD_DIM
    scale = hd ** -0.5

    def body(q_ref, k_ref, v_ref, ccol_ref, crow_ref, gq_ref, gk_ref, go_ref, o_ref, oa_ref):
        q0 = pl.program_id(2) * tq
        for e in range(2):
            sl = slice(e * hd, (e + 1) * hd)
            qn, _ = _head_rms(q_ref[:, sl], gq_ref[:, sl])
            kn, _ = _head_rms(k_ref[:, sl], gk_ref[:, sl])
            p = _softmax_rows(_fox_scores(qn, kn, ccol_ref[0, e], crow_ref[0, e], q0, tq, S, scale))
            o = _dot(_mx(p), _mx(v_ref[:, sl]), NN)
            o_ref[:, sl] = o
            oa_ref[:, sl] = _head_rms(o, go_ref[:, sl])[0].astype(oa_ref.dtype)

    W = 2 * hd
    vec = pl.BlockSpec((1, W), lambda b, h, i: (0, 0))
    ospec = pl.BlockSpec((tq, W), lambda b, h, i: (b * nq + i, h))
    return pl.pallas_call(
        body, name="fox_fwd", grid=(B, FOX_PAIRS, nq),
        in_specs=[pl.BlockSpec((tq, W), lambda b, h, i: (b * nq + i, h)),
                  pl.BlockSpec((S, W), lambda b, h, i: (b, FOX_PAIRS + h)),
                  pl.BlockSpec((S, W), lambda b, h, i: (b, 2 * FOX_PAIRS + h)),
                  pl.BlockSpec((1, 2, tq, 1), lambda b, h, i: (b, h, i, 0)),
                  pl.BlockSpec((1, 2, 1, S), lambda b, h, i: (b, h, 0, 0)), vec, vec, vec],
        out_specs=[ospec, ospec],
        out_shape=[jax.ShapeDtypeStruct((T, FOX_WIDTH), f32), jax.ShapeDtypeStruct((T, FOX_WIDTH), MXU_DTYPE)],
        compiler_params=_cparams("parallel", "parallel", "parallel"),
    )(P, P, P, ccol, crow, gq, gk, go)


def fox_bwd(P, ccol, crow, gq, gk, go, o_raw, d_oab, *, B, tq=256):
    T = P.shape[0]
    S = T // B
    tq = min(tq, S)
    nq = S // tq
    hd = FOX_HEAD_DIM
    scale = hd ** -0.5

    def body(q_ref, k_ref, v_ref, ccol_ref, crow_ref, gq_ref, gk_ref, go_ref, o_ref, doa_ref,
             dq_ref, dk_ref, dv_ref, dccol_ref, dcrow_ref, dgq_ref, dgk_ref, dgo_ref, dkn_acc, dv_acc, dcrow_acc):
        b, h, i = pl.program_id(0), pl.program_id(1), pl.program_id(2)
        q0 = i * tq

        @pl.when((b == 0) & (h == 0) & (i == 0))
        def _():
            dgq_ref[...] = jnp.zeros_like(dgq_ref)
            dgk_ref[...] = jnp.zeros_like(dgk_ref)
            dgo_ref[...] = jnp.zeros_like(dgo_ref)

        @pl.when(i == 0)
        def _():
            dkn_acc[...] = jnp.zeros_like(dkn_acc)
            dv_acc[...] = jnp.zeros_like(dv_acc)
            dcrow_acc[...] = jnp.zeros_like(dcrow_acc)

        for e in range(2):
            sl = slice(e * hd, (e + 1) * hd)
            q, k, v = q_ref[:, sl], k_ref[:, sl], v_ref[:, sl]
            gqv, gkv, gov = gq_ref[:, sl], gk_ref[:, sl], go_ref[:, sl]
            qn, rq = _head_rms(q, gqv)
            kn, rk = _head_rms(k, gkv)
            p = _softmax_rows(_fox_scores(qn, kn, ccol_ref[0, e], crow_ref[0, e], q0, tq, S, scale))
            o = o_ref[:, sl]
            ro = lax.rsqrt(jnp.mean(o * o, axis=-1, keepdims=True) + EPS)
            do, dgo = _head_rms_bwd(o, ro, gov, doa_ref[:, sl])
            dgo_ref[:, sl] += dgo
            dv_acc[e] += _dot(_mx(p), _mx(do), TN)
            dp = _dot(_mx(do), _mx(v), NT)
            ds = p * (dp - jnp.sum(do * o, axis=-1, keepdims=True))
            dccol_ref[0, e] = jnp.sum(ds, axis=1, keepdims=True)
            dcrow_acc[e] -= jnp.sum(ds, axis=0, keepdims=True)
            dqn = _dot(_mx(ds), _mx(kn), NN) * scale
            dkn_acc[e] += _dot(_mx(ds), _mx(qn), TN) * scale
            dq, dgq = _head_rms_bwd(q, rq, gqv, dqn)
            dq_ref[:, sl] = dq.astype(dq_ref.dtype)
            dgq_ref[:, sl] += dgq

        @pl.when(i == nq - 1)
        def _():
            for e in range(2):
                sl = slice(e * hd, (e + 1) * hd)
                k = k_ref[:, sl]
                gkv = gk_ref[:, sl]
                rk = lax.rsqrt(jnp.mean(k * k, axis=-1, keepdims=True) + EPS)
                dk, dgk = _head_rms_bwd(k, rk, gkv, dkn_acc[e])
                dk_ref[:, sl] = dk.astype(dk_ref.dtype)
                dv_ref[:, sl] = dv_acc[e].astype(dv_ref.dtype)
                dgk_ref[:, sl] += dgk
                dcrow_ref[0, e] = dcrow_acc[e]

    W = 2 * hd
    vec = pl.BlockSpec((1, W), lambda b, h, i: (0, 0))
    qspec = pl.BlockSpec((tq, W), lambda b, h, i: (b * nq + i, h))
    kvout = pl.BlockSpec((S, W), lambda b, h, i: (b, h))
    colspec = pl.BlockSpec((1, 2, tq, 1), lambda b, h, i: (b, h, i, 0))
    rowspec = pl.BlockSpec((1, 2, 1, S), lambda b, h, i: (b, h, 0, 0))
    return pl.pallas_call(
        body, name="fox_bwd", grid=(B, FOX_PAIRS, nq),
        in_specs=[qspec,
                  pl.BlockSpec((S, W), lambda b, h, i: (b, FOX_PAIRS + h)),
                  pl.BlockSpec((S, W), lambda b, h, i: (b, 2 * FOX_PAIRS + h)),
                  colspec, rowspec, vec, vec, vec, qspec, qspec],
        out_specs=[qspec, kvout, kvout, colspec, rowspec, vec, vec, vec],
        out_shape=[jax.ShapeDtypeStruct((T, FOX_WIDTH), MXU_DTYPE), jax.ShapeDtypeStruct((T, FOX_WIDTH), MXU_DTYPE),
                   jax.ShapeDtypeStruct((T, FOX_WIDTH), MXU_DTYPE),
                   jax.ShapeDtypeStruct((B, FOX_HEADS, S, 1), f32), jax.ShapeDtypeStruct((B, FOX_HEADS, 1, S), f32),
                   jax.ShapeDtypeStruct((1, W), f32), jax.ShapeDtypeStruct((1, W), f32), jax.ShapeDtypeStruct((1, W), f32)],
        scratch_shapes=[pltpu.VMEM((2, S, hd), f32), pltpu.VMEM((2, S, hd), f32), pltpu.VMEM((2, 1, S), f32)],
        compiler_params=_cparams("arbitrary", "arbitrary", "arbitrary"),
    )(P, P, P, ccol, crow, gq, gk, go, o_raw, d_oab)


FOX_TQ = 512
FOX_TK = 512
GROUP_PRECISION = lax.Precision.HIGH


def _head_mean(v):
    n = v.shape[1]
    r = lax.broadcasted_iota(jnp.int32, (n, n), 0) // FOX_HEAD_DIM
    c = lax.broadcasted_iota(jnp.int32, (n, n), 1) // FOX_HEAD_DIM
    return _dot(v, (r == c).astype(f32), NN, GROUP_PRECISION) * (1.0 / FOX_HEAD_DIM)


def fox_prep_fwd(P, gq, gk, *, tr=512):
    T = P.shape[0]
    tr = min(tr, T)
    scale = FOX_HEAD_DIM ** -0.5

    def body(q_ref, k_ref, v_ref, gq_ref, gk_ref, qn_ref, kn_ref, vb_ref):
        q, k = q_ref[...], k_ref[...]
        qn_ref[...] = (q * lax.rsqrt(_head_mean(q * q) + EPS) * (gq_ref[...] * scale)).astype(qn_ref.dtype)
        kn_ref[...] = (k * lax.rsqrt(_head_mean(k * k) + EPS) * gk_ref[...]).astype(kn_ref.dtype)
        vb_ref[...] = v_ref[...].astype(vb_ref.dtype)

    W = FOX_WIDTH
    col = lambda j: pl.BlockSpec((tr, W), lambda i: (i, j))
    vec = pl.BlockSpec((1, W), lambda i: (0, 0))
    out = jax.ShapeDtypeStruct((T, W), MXU_DTYPE)
    return pl.pallas_call(
        body, name="fox_prep_fwd", grid=(T // tr,), in_specs=[col(0), col(1), col(2), vec, vec],
        out_specs=[col(0)] * 3, out_shape=[out] * 3, compiler_params=_cparams("parallel"),
    )(P, P, P, gq, gk)


def fox_prep_bwd(P, gq, gk, dqn, dkn, *, tr=512):
    T = P.shape[0]
    tr = min(tr, T)
    scale = FOX_HEAD_DIM ** -0.5

    def body(q_ref, k_ref, gq_ref, gk_ref, dqn_ref, dkn_ref, dq_ref, dk_ref, dgq_ref, dgk_ref):
        @pl.when(pl.program_id(0) == 0)
        def _():
            dgq_ref[...] = jnp.zeros_like(dgq_ref)
            dgk_ref[...] = jnp.zeros_like(dgk_ref)

        def one(x, g, dn, dx_ref, dg_ref):
            r = lax.rsqrt(_head_mean(x * x) + EPS)
            xhat = x * r
            gd = dn * g
            dx_ref[...] = (r * (gd - xhat * _head_mean(gd * xhat))).astype(dx_ref.dtype)
            return jnp.sum(dn * xhat, axis=0, keepdims=True)

        dgq_ref[...] += scale * one(q_ref[...], gq_ref[...] * scale, dqn_ref[...], dq_ref, dgq_ref)
        dgk_ref[...] += one(k_ref[...], gk_ref[...], dkn_ref[...], dk_ref, dgk_ref)

    W = FOX_WIDTH
    col = lambda j: pl.BlockSpec((tr, W), lambda i: (i, j))
    vec = pl.BlockSpec((1, W), lambda i: (0, 0))
    return pl.pallas_call(
        body, name="fox_prep_bwd", grid=(T // tr,), in_specs=[col(0), col(1), vec, vec, col(0), col(0)],
        out_specs=[col(0), col(0), vec, vec],
        out_shape=[jax.ShapeDtypeStruct((T, W), MXU_DTYPE), jax.ShapeDtypeStruct((T, W), MXU_DTYPE),
                   jax.ShapeDtypeStruct((1, W), f32), jax.ShapeDtypeStruct((1, W), f32)],
        compiler_params=_cparams("arbitrary"),
    )(P, P, gq, gk, dqn, dkn)


def _fox_tile_scores(q, k_ref, ccol_ref, cq, e, j, sl, mask_off):
    tq, tk = FOX_TQ, FOX_TK
    rows = pl.ds(pl.multiple_of(j * tk, tk), tk)
    k = k_ref[rows, sl]
    s = _dot(k, q, NT) + cq - ccol_ref[0, e, rows, :]
    if mask_off is not None:
        key = lax.broadcasted_iota(jnp.int32, (tk, tq), 0) + mask_off
        query = lax.broadcasted_iota(jnp.int32, (tk, tq), 1)
        s = jnp.where(key <= query, s, NEG_INF)
    return s, k, rows


def _fox_sweep(i, update, carry):
    nd = FOX_TQ // FOX_TK
    carry = lax.fori_loop(0, i * nd, lambda j, cr: update(cr, j, None), carry)
    for d in range(nd):
        carry = update(carry, i * nd + d, d * FOX_TK)
    return carry


def fox_core_fwd(qn, kn, vb, ccol, crow, go, *, B):
    T = qn.shape[0]
    S = T // B
    tq = FOX_TQ
    nq = S // tq
    hd = FOX_HEAD_DIM

    def body(q_ref, k_ref, v_ref, ccol_ref, crow_ref, go_ref, o_ref, oa_ref, lse_ref):
        i = pl.program_id(2)
        for e in range(2):
            sl = slice(e * hd, (e + 1) * hd)
            q = q_ref[:, sl]
            cq = crow_ref[0, e, i]

            def update(carry, j, mask_off):
                m, l, acc = carry
                s, _, rows = _fox_tile_scores(q, k_ref, ccol_ref, cq, e, j, sl, mask_off)
                m2 = jnp.maximum(m, jnp.max(s, axis=0, keepdims=True))
                a = jnp.exp(m - m2)
                p = jnp.exp(s - m2)
                return m2, a * l + jnp.sum(p, axis=0, keepdims=True), a * acc + _dot(v_ref[rows, sl], _mx(p), TN)

            carry = (jnp.full((1, tq), NEG_INF, f32), jnp.zeros((1, tq), f32), jnp.zeros((hd, tq), f32))
            m, l, acc = _fox_sweep(i, update, carry)
            o = (acc / l).T
            o_ref[:, sl] = o
            oa_ref[:, sl] = _head_rms(o, go_ref[:, sl])[0].astype(oa_ref.dtype)
            lse_ref[0, e, 0] = m + jnp.log(l)

    W = 2 * hd
    qspec = pl.BlockSpec((tq, W), lambda b, h, i: (b * nq + i, h))
    kspec = pl.BlockSpec((S, W), lambda b, h, i: (b, h))
    return pl.pallas_call(
        body, name="fox_core_fwd", grid=(B, FOX_PAIRS, nq),
        in_specs=[qspec, kspec, kspec, pl.BlockSpec((1, 2, S, 1), lambda b, h, i: (b, h, 0, 0)),
                  pl.BlockSpec((1, 2, nq, 1, tq), lambda b, h, i: (b, h, 0, 0, 0)),
                  pl.BlockSpec((1, W), lambda b, h, i: (0, 0))],
        out_specs=[qspec, qspec, pl.BlockSpec((1, 2, 1, 1, tq), lambda b, h, i: (b, h, i, 0, 0))],
        out_shape=[jax.ShapeDtypeStruct((T, FOX_WIDTH), f32), jax.ShapeDtypeStruct((T, FOX_WIDTH), MXU_DTYPE),
                   jax.ShapeDtypeStruct((B, FOX_HEADS, nq, 1, tq), f32)],
        compiler_params=_cparams("parallel", "parallel", "parallel"),
    )(qn, kn, vb, ccol, crow, go)


def fox_core_bwd(qn, kn, vb, ccol, crow, go, o_raw, lse, d_oab, *, B):
    T = qn.shape[0]
    S = T // B
    tq = FOX_TQ
    nq = S // tq
    hd = FOX_HEAD_DIM

    def body(q_ref, k_ref, v_ref, ccol_ref, crow_ref, go_ref, o_ref, lse_ref, doa_ref,
             dq_ref, dk_ref, dv_ref, dccol_ref, dcrow_ref, dgo_ref, dk_acc, dv_acc, dck_acc):
        b, h, i = pl.program_id(0), pl.program_id(1), pl.program_id(2)

        @pl.when((b == 0) & (h == 0) & (i == 0))
        def _():
            dgo_ref[...] = jnp.zeros_like(dgo_ref)

        @pl.when(i == 0)
        def _():
            dk_acc[...] = jnp.zeros_like(dk_acc)
            dv_acc[...] = jnp.zeros_like(dv_acc)
            dck_acc[...] = jnp.zeros_like(dck_acc)

        for e in range(2):
            sl = slice(e * hd, (e + 1) * hd)
            q = q_ref[:, sl]
            cq = crow_ref[0, e, i]
            lse_e = lse_ref[0, e, 0]
            o = o_ref[:, sl]
            ro = lax.rsqrt(jnp.mean(o * o, axis=-1, keepdims=True) + EPS)
            do, dgo = _head_rms_bwd(o, ro, go_ref[:, sl], doa_ref[:, sl])
            dgo_ref[:, sl] += dgo
            delta = jnp.sum((do * o).T, axis=0, keepdims=True)
            do_b = _mx(do)

            def update(carry, j, mask_off):
                dq, dcq = carry
                s, k, rows = _fox_tile_scores(q, k_ref, ccol_ref, cq, e, j, sl, mask_off)
                p = jnp.exp(s - lse_e)
                dv_acc[e, rows, :] += _dot(_mx(p), do_b, NN)
                ds = p * (_dot(v_ref[rows, sl], do_b, NT) - delta)
                dck_acc[e, rows, :] -= jnp.sum(ds, axis=1, keepdims=True)
                ds_b = _mx(ds)
                dk_acc[e, rows, :] += _dot(ds_b, q, NN)
                return dq + _dot(ds_b, k, TN), dcq + jnp.sum(ds, axis=0, keepdims=True)

            dq, dcq = _fox_sweep(i, update, (jnp.zeros((tq, hd), f32), jnp.zeros((1, tq), f32)))
            dq_ref[:, sl] = dq
            dcrow_ref[0, e, 0] = dcq

        @pl.when(i == nq - 1)
        def _():
            for e in range(2):
                sl = slice(e * hd, (e + 1) * hd)
                dk_ref[:, sl] = dk_acc[e]
                dv_ref[:, sl] = dv_acc[e].astype(dv_ref.dtype)
            dccol_ref[0] = dck_acc[...]

    W = 2 * hd
    qspec = pl.BlockSpec((tq, W), lambda b, h, i: (b * nq + i, h))
    kspec = pl.BlockSpec((S, W), lambda b, h, i: (b, h))
    colspec = pl.BlockSpec((1, 2, S, 1), lambda b, h, i: (b, h, 0, 0))
    rowspec = pl.BlockSpec((1, 2, nq, 1, tq), lambda b, h, i: (b, h, 0, 0, 0))
    tilespec = pl.BlockSpec((1, 2, 1, 1, tq), lambda b, h, i: (b, h, i, 0, 0))
    vec = pl.BlockSpec((1, W), lambda b, h, i: (0, 0))
    return pl.pallas_call(
        body, name="fox_core_bwd", grid=(B, FOX_PAIRS, nq),
        in_specs=[qspec, kspec, kspec, colspec, rowspec, vec, qspec, tilespec, qspec],
        out_specs=[qspec, kspec, kspec, colspec, tilespec, vec],
        out_shape=[jax.ShapeDtypeStruct((T, FOX_WIDTH), f32), jax.ShapeDtypeStruct((T, FOX_WIDTH), f32),
                   jax.ShapeDtypeStruct((T, FOX_WIDTH), MXU_DTYPE),
                   jax.ShapeDtypeStruct((B, FOX_HEADS, S, 1), f32), jax.ShapeDtypeStruct((B, FOX_HEADS, nq, 1, tq), f32),
                   jax.ShapeDtypeStruct((1, W), f32)],
        scratch_shapes=[pltpu.VMEM((2, S, hd), f32), pltpu.VMEM((2, S, hd), f32), pltpu.VMEM((2, S, 1), f32)],
        compiler_params=_cparams("arbitrary", "arbitrary", "arbitrary"),
    )(qn, kn, vb, ccol, crow, go, o_raw, lse, d_oab)


def _lane_mask(lo, hi, shape):
    lane = lax.broadcasted_iota(jnp.int32, shape, 1)
    return (lane >= lo) & (lane < hi)


def _cumsum_rows(v, period, reverse=False):
    n = v.shape[0]
    pos = lax.broadcasted_iota(jnp.int32, v.shape, 0) % period
    sh = 1
    while sh < period:
        if reverse:
            v = v + jnp.where(pos + sh < period, pltpu.roll(v, n - sh, 0), 0.0)
        else:
            v = v + jnp.where(pos >= sh, pltpu.roll(v, sh, 0), 0.0)
        sh *= 2
    return v


def _gate_values(z, bias, alog):
    zb = z + bias
    ls = jax.nn.log_sigmoid(zb)
    beta = jax.nn.sigmoid(z)
    g = -jnp.exp(alog) * jax.nn.softplus(zb)
    return zb, ls, beta, g


def gates_fwd(P, bias, alog, *, B):
    T = P.shape[0]
    S = T // B

    def body(z_ref, bias_ref, alog_ref, o_ref):
        z = z_ref[...]
        _, ls, beta, g = _gate_values(z, bias_ref[...], alog_ref[...])
        c = _cumsum_rows(ls, S)
        gc = _cumsum_rows(g, GDN_CHUNK)
        o = jnp.where(_lane_mask(SM_F, SM_F + FOX_HEADS, z.shape), c, 0.0)
        o = jnp.where(_lane_mask(SM_B, SM_B + GDN_HEADS, z.shape), beta, o)
        o = jnp.where(_lane_mask(SM_A, SM_A + GDN_HEADS, z.shape), gc, o)
        o_ref[...] = o

    vec = pl.BlockSpec((1, LANES), lambda b: (0, 0))
    return pl.pallas_call(
        body, name="gates_fwd", grid=(B,),
        in_specs=[pl.BlockSpec((S, LANES), lambda b: (b, COL_SMALL // LANES)), vec, vec],
        out_specs=pl.BlockSpec((S, LANES), lambda b: (b, 0)),
        out_shape=jax.ShapeDtypeStruct((T, LANES), f32),
        compiler_params=_cparams("parallel"),
    )(P, bias, alog)


def gates_bwd(P, bias, alog, dgates, *, B):
    T = P.shape[0]
    S = T // B

    def body(z_ref, bias_ref, alog_ref, dg_ref, dz_ref, par_ref):
        z = z_ref[...]
        zb, ls, beta, g = _gate_values(z, bias_ref[...], alog_ref[...])
        d = dg_ref[...]
        dls = _cumsum_rows(d, S, reverse=True)
        dgr = _cumsum_rows(d, GDN_CHUNK, reverse=True)
        sig = jax.nn.sigmoid(zb)
        dz_f = dls * (1.0 - sig)
        dz_b = d * beta * (1.0 - beta)
        dz_a = dgr * (-jnp.exp(alog_ref[...])) * sig
        dz = jnp.where(_lane_mask(SM_F, SM_F + FOX_HEADS, z.shape), dz_f, 0.0)
        dz = jnp.where(_lane_mask(SM_B, SM_B + GDN_HEADS, z.shape), dz_b, dz)
        dz = jnp.where(_lane_mask(SM_A, SM_A + GDN_HEADS, z.shape), dz_a, dz)
        dz_ref[...] = dz.astype(dz_ref.dtype)

        @pl.when(pl.program_id(0) == 0)
        def _():
            par_ref[...] = jnp.zeros_like(par_ref)

        dalog = jnp.where(_lane_mask(SM_A, SM_A + GDN_HEADS, z.shape), dgr * g, 0.0)
        par_ref[0:1, :] += jnp.sum(dz, axis=0, keepdims=True)
        par_ref[1:2, :] += jnp.sum(dalog, axis=0, keepdims=True)

    vec = pl.BlockSpec((1, LANES), lambda b: (0, 0))
    return pl.pallas_call(
        body, name="gates_bwd", grid=(B,),
        in_specs=[pl.BlockSpec((S, LANES), lambda b: (b, COL_SMALL // LANES)), vec, vec,
                  pl.BlockSpec((S, LANES), lambda b: (b, 0))],
        out_specs=[pl.BlockSpec((S, LANES), lambda b: (b, 0)), pl.BlockSpec((8, LANES), lambda b: (0, 0))],
        out_shape=[jax.ShapeDtypeStruct((T, LANES), MXU_DTYPE), jax.ShapeDtypeStruct((8, LANES), f32)],
        compiler_params=_cparams("arbitrary"),
    )(P, bias, alog, dgates)


GDN_BLOCKS = 3 * GDN_HEADS


def _shift_rows(v, d, reverse=False):
    if d == 0:
        return v
    n = v.shape[0]
    row = lax.broadcasted_iota(jnp.int32, v.shape, 0)
    if reverse:
        return jnp.where(row + d < n, pltpu.roll(v, n - d, 0), 0.0)
    return jnp.where(row >= d, pltpu.roll(v, d, 0), 0.0)


def _conv_silu(x, w):
    pre = sum(w[j:j + 1, :] * _shift_rows(x, CONV_WIDTH - 1 - j) for j in range(CONV_WIDTH))
    return pre, pre * jax.nn.sigmoid(pre)


def gdn_prep_fwd(P, conv_w, *, B):
    T = P.shape[0]
    S = T // B

    def body(x_ref, w_ref, o_ref):
        _, y = _conv_silu(x_ref[...], w_ref[...])
        yn = y * lax.rsqrt(jnp.sum(y * y, axis=-1, keepdims=True) + EPS)
        o_ref[...] = jnp.where(pl.program_id(1) < 2 * GDN_HEADS, yn, y)

    return pl.pallas_call(
        body, name="gdn_prep_fwd", grid=(B, GDN_BLOCKS),
        in_specs=[pl.BlockSpec((S, LANES), lambda b, j: (b, COL_GDN // LANES + j)),
                  pl.BlockSpec((CONV_WIDTH, LANES), lambda b, j: (0, j))],
        out_specs=pl.BlockSpec((S, LANES), lambda b, j: (b, j)),
        out_shape=jax.ShapeDtypeStruct((T, 3 * GDN_WIDTH), f32),
        compiler_params=_cparams("parallel", "parallel"),
    )(P, conv_w)


def gdn_prep_bwd(P, conv_w, dG, *, B):
    T = P.shape[0]
    S = T // B

    def body(x_ref, w_ref, dg_ref, dx_ref, dw_ref):
        x, w = x_ref[...], w_ref[...]
        pre, y = _conv_silu(x, w)
        dn = dg_ref[...]
        r = lax.rsqrt(jnp.sum(y * y, axis=-1, keepdims=True) + EPS)
        n = y * r
        dy_norm = r * (dn - n * jnp.sum(dn * n, axis=-1, keepdims=True))
        dy = jnp.where(pl.program_id(0) < 2 * GDN_HEADS, dy_norm, dn)
        sg = jax.nn.sigmoid(pre)
        dpre = dy * (sg * (1.0 + pre * (1.0 - sg)))
        dx = sum(w[j:j + 1, :] * _shift_rows(dpre, CONV_WIDTH - 1 - j, reverse=True) for j in range(CONV_WIDTH))
        dx_ref[...] = dx.astype(dx_ref.dtype)

        @pl.when(pl.program_id(1) == 0)
        def _():
            dw_ref[...] = jnp.zeros_like(dw_ref)

        for j in range(CONV_WIDTH):
            dw_ref[j:j + 1, :] += jnp.sum(dpre * _shift_rows(x, CONV_WIDTH - 1 - j), axis=0, keepdims=True)

    return pl.pallas_call(
        body, name="gdn_prep_bwd", grid=(GDN_BLOCKS, B),
        in_specs=[pl.BlockSpec((S, LANES), lambda j, b: (b, COL_GDN // LANES + j)),
                  pl.BlockSpec((CONV_WIDTH, LANES), lambda j, b: (0, j)),
                  pl.BlockSpec((S, LANES), lambda j, b: (b, j))],
        out_specs=[pl.BlockSpec((S, LANES), lambda j, b: (b, j)),
                   pl.BlockSpec((CONV_WIDTH, LANES), lambda j, b: (0, j))],
        out_shape=[jax.ShapeDtypeStruct((T, 3 * GDN_WIDTH), MXU_DTYPE),
                   jax.ShapeDtypeStruct((CONV_WIDTH, 3 * GDN_WIDTH), f32)],
        compiler_params=_cparams("arbitrary", "arbitrary"),
    )(P, conv_w, dG)


GDN_GROUP = 4
B_NN = (((2,), (1,)), ((0,), (0,)))
B_NT = (((2,), (2,)), ((0,), (0,)))
B_TN = (((1,), (1,)), ((0,), (0,)))


def _bmm(a, b, dims, precision=None):
    if precision is None:
        a, b = _mx(a), _mx(b)
    return lax.dot_general(a, b, dims, preferred_element_type=f32, precision=precision)


def _tri_inverse(A):
    C = A.shape[-1]
    row = lax.broadcasted_iota(jnp.int32, A.shape, 1)
    col = lax.broadcasted_iota(jnp.int32, A.shape, 2)
    eye = (row == col).astype(f32)
    X = jnp.where((row // 4) == (col // 4), -A, 0.0)
    X2 = _bmm(X, X, B_NN, INV_PRECISION)
    Tm = eye + X + X2 + _bmm(X, X2, B_NN, INV_PRECISION)
    b = 4
    while b < C:
        off = ((row // (2 * b)) == (col // (2 * b))) & ((row // b) != (col // b))
        Tm = Tm - _bmm(_bmm(Tm, jnp.where(off, A, 0.0), B_NN, INV_PRECISION), Tm, B_NN, INV_PRECISION)
        b *= 2
    return Tm


def _pick_lane(block, lane_idx):
    lane = lax.broadcasted_iota(jnp.int32, block.shape, 1)
    return jnp.sum(jnp.where(lane == lane_idx, block, 0.0), axis=1, keepdims=True)


def _gdn_local(q, k, v, beta, gc, Tm=None):
    C = GDN_CHUNK
    n = q.shape[0] // C
    q = q.reshape(n, C, -1) * (GDN_HEAD_DIM ** -0.5)
    k = k.reshape(n, C, -1)
    v = v.reshape(n, C, -1)
    beta = beta.reshape(n, C, 1)
    gc = gc.reshape(n, C, 1)
    row = lax.broadcasted_iota(jnp.int32, (n, C, C), 1)
    col = lax.broadcasted_iota(jnp.int32, (n, C, C), 2)
    gcT = jnp.swapaxes(jnp.broadcast_to(gc, (n, C, C)), 1, 2)
    D = jnp.exp(jnp.where(row >= col, gc - gcT, NEG_INF))
    kb = k * beta
    vb = v * beta
    A = jnp.where(row > col, _bmm(kb, k, B_NT) * D, 0.0)
    Gam = jnp.exp(gc)
    kg = kb * Gam
    gl = gc[:, C - 1:C, :]
    kdec = jnp.exp(gl - gc)
    loc = dict(q=q, k=k, v=v, beta=beta, gc=gc, D=D, kb=kb, vb=vb, A=A, Gam=Gam, kg=kg,
               kdec=kdec, kd=k * kdec, qg=q * Gam, gam=jnp.exp(gl), row=row, col=col)
    if Tm is None:
        Tm = _tri_inverse(A)
        loc.update(u=_bmm(Tm, vb, B_NN), w=_bmm(Tm, kg, B_NN), M=_bmm(q, k, B_NT) * D)
    else:
        Tm = Tm.reshape(n, C, C)
    loc["Tm"] = Tm
    return loc


def _gdn_store_local(loc, r0, u_s, w_s, qg_s, kd_s, M_s, gam_s, c0):
    n = loc["u"].shape[0]
    R = n * GDN_CHUNK
    u_s[pl.ds(r0, R), :] = loc["u"].reshape(R, -1)
    w_s[pl.ds(r0, R), :] = loc["w"].reshape(R, -1)
    qg_s[pl.ds(r0, R), :] = loc["qg"].reshape(R, -1)
    kd_s[pl.ds(r0, R), :] = loc["kd"].reshape(R, -1)
    M_s[pl.ds(r0, R), :] = loc["M"].reshape(R, -1)
    gam_s[pl.ds(c0, n)] = jnp.broadcast_to(loc["gam"], (n, 1, LANES))


def _gdn_specs(S):
    blk = lambda off: pl.BlockSpec((S, LANES), lambda b, h: (b, off + h))
    return blk


def gdn_fwd(G, gates, P, g_on, *, B):
    T = G.shape[0]
    S = T // B
    C = GDN_CHUNK
    N = S // C
    R = GDN_GROUP * C
    hd = GDN_HEAD_DIM

    def body(q_ref, k_ref, v_ref, gt_ref, z_ref, gon_ref, o_ref, ob_ref, st_ref, u_s, w_s, qg_s, kd_s, M_s, gam_s):
        h = pl.program_id(1)

        def local(gi, carry):
            r0 = pl.multiple_of(gi * R, R)
            gt = gt_ref[pl.ds(r0, R), :]
            loc = _gdn_local(q_ref[pl.ds(r0, R), :], k_ref[pl.ds(r0, R), :], v_ref[pl.ds(r0, R), :],
                             _pick_lane(gt, SM_B + h), _pick_lane(gt, SM_A + h))
            _gdn_store_local(loc, r0, u_s, w_s, qg_s, kd_s, M_s, gam_s, gi * GDN_GROUP)
            return carry

        lax.fori_loop(0, N // GDN_GROUP, local, 0)

        def step(n, state):
            r0 = pl.multiple_of(n * C, C)
            st_ref[0, 0, n] = state
            v_new = u_s[pl.ds(r0, C), :] - _dotm(w_s[pl.ds(r0, C), :], state, NN)
            o_ref[pl.ds(r0, C), :] = (_dotm(qg_s[pl.ds(r0, C), :], state, NN)
                                      + _dotm(M_s[pl.ds(r0, C), :], v_new, NN))
            return state * gam_s[n] + _dotm(kd_s[pl.ds(r0, C), :], v_new, TN)

        lax.fori_loop(0, N, step, jnp.zeros((hd, hd), f32))
        o = o_ref[...]
        z = z_ref[...]
        ob_ref[...] = (_head_rms(o, gon_ref[...])[0] * (z * jax.nn.sigmoid(z))).astype(ob_ref.dtype)

    blk = lambda off: pl.BlockSpec((S, LANES), lambda b, h: (b, off + h))
    rows = lambda: pltpu.VMEM((S, hd), f32)
    return pl.pallas_call(
        body, name="gdn_fwd", grid=(B, GDN_HEADS),
        in_specs=[blk(0), blk(GDN_HEADS), blk(2 * GDN_HEADS), pl.BlockSpec((S, LANES), lambda b, h: (b, 0)),
                  blk(COL_Z // LANES), pl.BlockSpec((1, hd), lambda b, h: (0, 0))],
        out_specs=[blk(0), blk(0), pl.BlockSpec((1, 1, N, hd, hd), lambda b, h: (b, h, 0, 0, 0))],
        out_shape=[jax.ShapeDtypeStruct((T, GDN_WIDTH), f32), jax.ShapeDtypeStruct((T, GDN_WIDTH), MXU_DTYPE),
                   jax.ShapeDtypeStruct((B, GDN_HEADS, N, hd, hd), f32)],
        scratch_shapes=[rows(), rows(), rows(), rows(), pltpu.VMEM((S, C), f32), pltpu.VMEM((N, 1, LANES), f32)],
        compiler_params=_cparams("parallel", "parallel"),
    )(G, G, G, gates, P, g_on)


def gdn_bwd(G, gates, P, g_on, o_raw, states, d_oab, *, B):
    T = G.shape[0]
    S = T // B
    C = GDN_CHUNK
    N = S // C
    R = GDN_GROUP * C
    hd = GDN_HEAD_DIM

    def body(q_ref, k_ref, v_ref, gt_ref, z_ref, gon_ref, o_ref, st_ref, dob_ref,
             dq_ref, dk_ref, dv_ref, dgt_ref, dz_ref, dgon_ref,
             u_s, w_s, qg_s, kd_s, M_s, gam_s, do_s, du_s, dw_s, dqg_s, dkd_s, dM_s, dgl_s, Tm_s):
        b, h = pl.program_id(0), pl.program_id(1)

        @pl.when((b == 0) & (h == 0))
        def _():
            dgon_ref[...] = jnp.zeros_like(dgon_ref)

        @pl.when(h == 0)
        def _():
            dgt_ref[...] = jnp.zeros_like(dgt_ref)

        def group_inputs(gi, Tm_of=None):
            r0 = pl.multiple_of(gi * R, R)
            gt = gt_ref[pl.ds(r0, R), :]
            Tm = None if Tm_of is None else Tm_of[pl.ds(r0, R), :]
            return r0, _gdn_local(q_ref[pl.ds(r0, R), :], k_ref[pl.ds(r0, R), :], v_ref[pl.ds(r0, R), :],
                                  _pick_lane(gt, SM_B + h), _pick_lane(gt, SM_A + h), Tm)

        def local(gi, carry):
            r0, loc = group_inputs(gi)
            _gdn_store_local(loc, r0, u_s, w_s, qg_s, kd_s, M_s, gam_s, gi * GDN_GROUP)
            Tm_s[pl.ds(r0, R), :] = loc["Tm"].reshape(R, C)
            o, z, gon = o_ref[pl.ds(r0, R), :], z_ref[pl.ds(r0, R), :], gon_ref[...]
            dob = dob_ref[pl.ds(r0, R), :]
            on, ro = _head_rms(o, gon)
            sz = jax.nn.sigmoid(z)
            dz_ref[pl.ds(r0, R), :] = (dob * on * (sz * (1.0 + z * (1.0 - sz)))).astype(dz_ref.dtype)
            do, dgon = _head_rms_bwd(o, ro, gon, dob * (z * sz))
            do_s[pl.ds(r0, R), :] = do
            dgon_ref[...] += dgon
            return carry

        lax.fori_loop(0, N // GDN_GROUP, local, 0)

        def step(t, dS):
            n = N - 1 - t
            r0 = pl.multiple_of(n * C, C)
            rows = pl.ds(r0, C)
            state = st_ref[0, 0, n]
            w_n, M_n, kd_n, do_n = w_s[rows, :], M_s[rows, :], kd_s[rows, :], do_s[rows, :]
            v_new = u_s[rows, :] - _dotm(w_n, state, NN)
            dv_new = _dotm(M_n, do_n, TN) + _dotm(kd_n, dS, NN)
            du_s[rows, :] = dv_new
            dw_s[rows, :] = -_dotm(dv_new, state, NT)
            dqg_s[rows, :] = _dotm(do_n, state, NT)
            dM_s[rows, :] = _dotm(do_n, v_new, NT)
            dkd_s[rows, :] = _dotm(v_new, dS, NT)
            gam = gam_s[n]
            dgl_s[n] = jnp.broadcast_to(jnp.sum(jnp.sum(dS * state, axis=1, keepdims=True), axis=0, keepdims=True), (1, LANES)) * gam
            return dS * gam + _dotm(qg_s[rows, :], do_n, TN) - _dotm(w_n, dv_new, TN)

        lax.fori_loop(0, N, step, jnp.zeros((hd, hd), f32))

        def finish(gi, carry):
            r0, L = group_inputs(gi, Tm_s)
            n = GDN_GROUP
            rows = pl.ds(r0, R)
            g3 = lambda ref: ref[rows, :].reshape(n, C, -1)
            du, dw, dqg, dkd, dM = g3(du_s), g3(dw_s), g3(dqg_s), g3(dkd_s), g3(dM_s)
            L["M"] = g3(M_s)
            TmT = jnp.swapaxes(L["Tm"], 1, 2)
            dTm = _bmm(du, L["vb"], B_NT) + _bmm(dw, L["kg"], B_NT)
            dvb = _bmm(TmT, du, B_NN)
            dkg = _bmm(TmT, dw, B_NN)
            dA = jnp.where(L["row"] > L["col"], -_bmm(_bmm(TmT, dTm, B_NN), TmT, B_NN), 0.0)
            dKK = dA * L["D"]
            dQK = dM * L["D"]
            dkb = _bmm(dKK, L["k"], B_NN) + dkg * L["Gam"]
            dk = (_bmm(dKK, L["kb"], B_TN) + _bmm(dQK, L["q"], B_TN) + dkd * L["kdec"] + L["beta"] * dkb)
            dq = (_bmm(dQK, L["k"], B_NN) + dqg * L["Gam"]) * (GDN_HEAD_DIM ** -0.5)
            E = dA * L["A"] + dM * L["M"]
            r = jnp.sum(dkd * L["kd"], axis=-1, keepdims=True)
            dgc = (jnp.sum(E, axis=2, keepdims=True) - jnp.sum(jnp.swapaxes(E, 1, 2), axis=2, keepdims=True)
                   + jnp.sum(dkg * L["kg"], axis=-1, keepdims=True) + jnp.sum(dqg * L["qg"], axis=-1, keepdims=True) - r)
            dgl = jnp.sum(r, axis=1, keepdims=True) + dgl_s[pl.ds(gi * n, n)][:, :, 0:1]
            rowc = lax.broadcasted_iota(jnp.int32, (n, C, 1), 1)
            dgc = dgc + jnp.where(rowc == C - 1, dgl, 0.0)
            dbeta = jnp.sum(dkb * L["k"], axis=-1, keepdims=True) + jnp.sum(dvb * L["v"], axis=-1, keepdims=True)
            dq_ref[rows, :] = dq.reshape(R, hd)
            dk_ref[rows, :] = dk.reshape(R, hd)
            dv_ref[rows, :] = (L["beta"] * dvb).reshape(R, hd)
            lane = lax.broadcasted_iota(jnp.int32, (R, LANES), 1)
            dgt_ref[rows, :] += (jnp.where(lane == SM_B + h, dbeta.reshape(R, 1), 0.0)
                                 + jnp.where(lane == SM_A + h, dgc.reshape(R, 1), 0.0))
            return carry

        lax.fori_loop(0, N // GDN_GROUP, finish, 0)

    blk = lambda off: pl.BlockSpec((S, LANES), lambda b, h: (b, off + h))
    rows = lambda: pltpu.VMEM((S, hd), f32)
    return pl.pallas_call(
        body, name="gdn_bwd", grid=(B, GDN_HEADS),
        in_specs=[blk(0), blk(GDN_HEADS), blk(2 * GDN_HEADS), pl.BlockSpec((S, LANES), lambda b, h: (b, 0)),
                  blk(COL_Z // LANES), pl.BlockSpec((1, hd), lambda b, h: (0, 0)), blk(0),
                  pl.BlockSpec((1, 1, N, hd, hd), lambda b, h: (b, h, 0, 0, 0)), blk(GDN_HEADS)],
        out_specs=[blk(0), blk(0), blk(0), pl.BlockSpec((S, LANES), lambda b, h: (b, 0)), blk(0),
                   pl.BlockSpec((1, hd), lambda b, h: (0, 0))],
        out_shape=[jax.ShapeDtypeStruct((T, GDN_WIDTH), f32), jax.ShapeDtypeStruct((T, GDN_WIDTH), f32),
                   jax.ShapeDtypeStruct((T, GDN_WIDTH), f32), jax.ShapeDtypeStruct((T, LANES), f32),
                   jax.ShapeDtypeStruct((T, GDN_WIDTH), MXU_DTYPE), jax.ShapeDtypeStruct((1, hd), f32)],
        scratch_shapes=[rows(), rows(), rows(), rows(), pltpu.VMEM((S, C), f32), pltpu.VMEM((N, 1, LANES), f32),
                        rows(), rows(), rows(), rows(), rows(), pltpu.VMEM((S, C), f32), pltpu.VMEM((N, 1, LANES), f32),
                        pltpu.VMEM((S, C), f32)],
        compiler_params=_cparams("arbitrary", "arbitrary"),
    )(G, G, G, gates, P, g_on, o_raw, states, d_oab)


IN_SPLIT = (0, 1536, 1544, 3080, 3088, 3600)


def align_w_in(w):
    s = IN_SPLIT
    pad = jnp.zeros((w.shape[0], IN_ALIGNED - IN_DIM), w.dtype)
    return jnp.concatenate([w[:, s[0]:s[1]], w[:, s[2]:s[3]], w[:, s[4]:s[5]], w[:, s[1]:s[2]], w[:, s[3]:s[4]], pad], axis=1)


def unalign_w_in(wa):
    return jnp.concatenate([wa[:, 0:1536], wa[:, COL_SMALL:COL_SMALL + 8], wa[:, 1536:3072],
                            wa[:, COL_SMALL + 8:COL_SMALL + 16], wa[:, 3072:3584]], axis=1)


def _lanes_vec(pieces):
    v = jnp.zeros((1, LANES), f32)
    for off, a in pieces:
        v = lax.dynamic_update_slice(v, a.astype(f32), (0, off))
    return v


def local_step(x, mem, target, w, sp, *, B):
    T = x.shape[0]
    S = T // B
    gq8, gk8 = jnp.tile(sp["fox_qnorm_g"], (1, FOX_HEADS)), jnp.tile(sp["fox_knorm_g"], (1, FOX_HEADS))
    go2 = jnp.tile(sp["fox_onorm_g"], (1, 2))
    bias = _lanes_vec([(SM_F, sp["fox_f_bias"]), (SM_A, sp["gdn_dt_bias"])])
    alog = _lanes_vec([(SM_A, sp["gdn_A_log"])])

    h1 = rms_fwd(x, sp["norm_mix_g"], name="rms_mix")
    P = matmul(h1, w["wa"], name="mm_in", tn=IN_TILE)
    gates = gates_fwd(P, bias, alog, B=B)
    c = gates[:, SM_F:SM_F + FOX_HEADS].reshape(B, S, FOX_HEADS).transpose(0, 2, 1)
    ccol, crow = c[..., None], c.reshape(B, FOX_HEADS, S // FOX_TQ, 1, FOX_TQ)
    qn, kn, vb = fox_prep_fwd(P, gq8, gk8)
    o_raw, o_a, lse = fox_core_fwd(qn, kn, vb, ccol, crow, go2, B=B)
    G = gdn_prep_fwd(P, w["conv_w"], B=B)
    ob_raw, o_b, states = gdn_fwd(G, gates, P, sp["gdn_onorm_g"], B=B)
    oab = jnp.concatenate([o_a, o_b], axis=1)
    x2 = matmul(oab, w["w_out"], residual=x, name="mm_out")
    hq = rms_fwd(x2, sp["norm_xattn_g"], name="rms_xattn")
    hm = rms_fwd(mem, sp["mem_norm_g"], name="rms_mem")
    cq = matmul(hq, w["w_cq"], name="mm_cq")
    ckv = matmul(hm, w["w_ckv"], name="mm_ckv")
    co = xattn_fwd(cq, ckv, sp["xattn_qnorm_g"], sp["xattn_knorm_g"], B=B)
    x3 = matmul(co, w["w_co"], residual=x2, name="mm_co")
    hf = rms_fwd(x3, sp["norm_mlp_g"], name="rms_mlp")
    a, act = matmul(hf, w["w_mlp1"], relu2_out=True, name="mm_mlp1")
    x4 = matmul(act, w["w_mlp2"], residual=x3, name="mm_mlp2")
    dy, loss = loss_head(x4, target)

    da = matmul(dy, w["w_mlp2"], tb=True, relu2_bwd_aux=a, out_dtype=MXU_DTYPE, name="mm_d_act")
    g_mlp2 = matmul(act, dy, ta=True, name="mm_g_mlp2")
    g_mlp1 = matmul(hf, da, ta=True, name="mm_g_mlp1")
    dhf = matmul(da, w["w_mlp1"], tb=True, name="mm_d_hf")
    dx3, g_norm_mlp = rms_bwd(x3, sp["norm_mlp_g"], dhf, dy, name="rms_mlp_bwd")
    dco = matmul(dx3, w["w_co"], tb=True, name="mm_d_co")
    g_co = matmul(co, dx3, ta=True, name="mm_g_co")
    dcq, dckv, g_xq, g_xk = xattn_bwd(cq, ckv, sp["xattn_qnorm_g"], sp["xattn_knorm_g"], dco, B=B)
    g_cq = matmul(hq, dcq, ta=True, name="mm_g_cq")
    dhq = matmul(dcq, w["w_cq"], tb=True, name="mm_d_hq")
    g_ckv = matmul(hm, dckv, ta=True, name="mm_g_ckv")
    dhm = matmul(dckv, w["w_ckv"], tb=True, name="mm_d_hm")
    _, g_mem_norm = rms_bwd(mem, sp["mem_norm_g"], dhm, None, name="rms_mem_bwd")
    dx2, g_norm_xattn = rms_bwd(x2, sp["norm_xattn_g"], dhq, dx3, name="rms_xattn_bwd")
    doab = matmul(dx2, w["w_out"], tb=True, name="mm_d_oab")
    g_out = matmul(oab, dx2, ta=True, name="mm_g_out")
    dqn, dkn, dv_f, dccol, dcrow, dgo2 = fox_core_bwd(qn, kn, vb, ccol, crow, go2, o_raw, lse, doab, B=B)
    dq_f, dk_f, dgq8, dgk8 = fox_prep_bwd(P, gq8, gk8, dqn, dkn)
    dGq, dGk, dGv, dgt, dz, g_gdn_on = gdn_bwd(G, gates, P, sp["gdn_onorm_g"], ob_raw, states, doab, B=B)
    dPg, g_conv = gdn_prep_bwd(P, w["conv_w"], jnp.concatenate([dGq, dGk, dGv], axis=1), B=B)
    dc = (dccol[..., 0] + dcrow.reshape(B, FOX_HEADS, S)).transpose(0, 2, 1).reshape(T, FOX_HEADS)
    dgates = dgt + jnp.pad(dc, ((0, 0), (SM_F, LANES - SM_F - FOX_HEADS)))
    dsmall, par = gates_bwd(P, bias, alog, dgates, B=B)
    dP = jnp.concatenate([dq_f, dk_f, dv_f, dPg, dz, dsmall, jnp.zeros((T, IN_ALIGNED - COL_SMALL - LANES), MXU_DTYPE)], axis=1)
    g_wa = matmul(h1, dP, ta=True, name="mm_g_in", tn=IN_TILE)
    dh1 = matmul(dP, w["wa"], tb=True, name="mm_d_h1", tk=IN_TILE)
    dx, g_norm_mix = rms_bwd(x, sp["norm_mix_g"], dh1, dx2, name="rms_mix_bwd")

    fold = lambda g: jnp.sum(g.reshape(-1, FOX_HEAD_DIM), axis=0, keepdims=True)
    big = dict(w_in=unalign_w_in(g_wa), w_out=g_out, w_cq=g_cq, w_ckv=g_ckv, w_co=g_co, w_mlp1=g_mlp1, w_mlp2=g_mlp2)
    small = dict(norm_mix_g=g_norm_mix, fox_qnorm_g=fold(dgq8), fox_knorm_g=fold(dgk8),
                 fox_f_bias=par[0:1, SM_F:SM_F + FOX_HEADS], fox_onorm_g=fold(dgo2), gdn_conv_w=g_conv,
                 gdn_A_log=par[1:2, SM_A:SM_A + GDN_HEADS], gdn_dt_bias=par[0:1, SM_A:SM_A + GDN_HEADS],
                 gdn_onorm_g=g_gdn_on, norm_xattn_g=g_norm_xattn, mem_norm_g=g_mem_norm,
                 xattn_qnorm_g=g_xq, xattn_knorm_g=g_xk, norm_mlp_g=g_norm_mlp)
    return loss, dx, big, small


MESH_IDS = pl.DeviceIdType.MESH
N_CHIPS = 4
HBM_SPEC = pl.BlockSpec(memory_space=pltpu.HBM)
PACK_ROWS = 30720
PACK_HALF = PACK_ROWS // 2
PACK_BLOCK = 3072


def _place():
    return lax.axis_index("x"), lax.axis_index("y"), lax.axis_index("c")


def _other_chips(x, y):
    return [(1 - x, y), (x, 1 - y), (1 - x, 1 - y)]


def _remote(src, dst, send_sem, recv_sem, to):
    return pltpu.make_async_remote_copy(src_ref=src, dst_ref=dst, send_sem=send_sem, recv_sem=recv_sem,
                                        device_id=to, device_id_type=MESH_IDS)


def all_gather_shards(packed):
    half = PACK_HALF

    def body(src_ref, out_ref, send_sems, recv_sems):
        x, y, c = _place()
        me_chip = 2 * x + y
        sibling = (x, y, 1 - c)
        chips = _other_chips(x, y)

        def rows(chip, core):
            return out_ref.at[chip, pl.ds(core * half, half), :]

        sends = [_remote(src_ref.at[pl.ds(c * half, half), :], rows(me_chip, c), send_sems.at[j], recv_sems.at[j], (px, py, c))
                 for j, (px, py) in enumerate(chips)]
        for cp in sends:
            cp.start()
        passed = []
        for j, (px, py) in enumerate(chips):
            theirs = rows(2 * px + py, c)
            _remote(theirs, theirs, send_sems.at[j], recv_sems.at[j], (px, py, c)).wait_recv()
            cp = _remote(theirs, theirs, send_sems.at[3 + j], recv_sems.at[3 + j], sibling)
            cp.start()
            passed.append(cp)
        for j, (px, py) in enumerate(chips):
            theirs = rows(2 * px + py, 1 - c)
            _remote(theirs, theirs, send_sems.at[3 + j], recv_sems.at[3 + j], sibling).wait_recv()
        for cp in sends + passed:
            cp.wait_send()

    return pl.pallas_call(
        body, name="all_gather_shards", in_specs=[HBM_SPEC], out_specs=HBM_SPEC,
        out_shape=jax.ShapeDtypeStruct((N_CHIPS,) + packed.shape, packed.dtype),
        scratch_shapes=[pltpu.SemaphoreType.DMA((6,)), pltpu.SemaphoreType.DMA((6,))],
    )(packed)


def exchange_core_halves(G):
    half = PACK_HALF

    def body(g_ref, land_ref, send_sem, recv_sem):
        x, y, c = _place()
        cp = _remote(g_ref.at[:, pl.ds((1 - c) * half, half), :], land_ref, send_sem, recv_sem, (x, y, 1 - c))
        cp.start()
        cp.wait()

    return pl.pallas_call(
        body, name="exchange_core_halves", in_specs=[HBM_SPEC], out_specs=HBM_SPEC,
        out_shape=jax.ShapeDtypeStruct((N_CHIPS, half, LANES), G.dtype),
        scratch_shapes=[pltpu.SemaphoreType.DMA(()), pltpu.SemaphoreType.DMA(())],
    )(G)


def add_core_halves(G, land, core):
    nb = PACK_HALF // PACK_BLOCK

    def body(c_ref, g_ref, l_ref, o_ref):
        o_ref[...] = (g_ref[...].astype(f32) + l_ref[...].astype(f32)).astype(o_ref.dtype)

    blk = (1, PACK_BLOCK, LANES)
    return pl.pallas_call(
        body, name="add_core_halves",
        grid_spec=pltpu.PrefetchScalarGridSpec(
            num_scalar_prefetch=1, grid=(N_CHIPS, nb),
            in_specs=[pl.BlockSpec(blk, lambda k, i, c_ref: (k, c_ref[0] * nb + i, 0)),
                      pl.BlockSpec(blk, lambda k, i, c_ref: (k, i, 0))],
            out_specs=pl.BlockSpec(blk, lambda k, i, c_ref: (k, i, 0))),
        out_shape=jax.ShapeDtypeStruct(land.shape, land.dtype),
        compiler_params=_cparams("parallel", "parallel"),
    )(core, G, land)


def scatter_to_chips(part):
    def body(p_ref, land_ref, send_sems, recv_sems):
        x, y, c = _place()
        me_chip = 2 * x + y
        chips = _other_chips(x, y)
        sends = [_remote(p_ref.at[2 * px + py], land_ref.at[me_chip], send_sems.at[j], recv_sems.at[j], (px, py, c))
                 for j, (px, py) in enumerate(chips)]
        for cp in sends:
            cp.start()
        for j, (px, py) in enumerate(chips):
            slot = land_ref.at[2 * px + py]
            _remote(slot, slot, send_sems.at[j], recv_sems.at[j], (px, py, c)).wait_recv()
        for cp in sends:
            cp.wait_send()

    return pl.pallas_call(
        body, name="scatter_to_chips", in_specs=[HBM_SPEC], out_specs=HBM_SPEC,
        out_shape=jax.ShapeDtypeStruct(part.shape, part.dtype),
        scratch_shapes=[pltpu.SemaphoreType.DMA((3,)), pltpu.SemaphoreType.DMA((3,))],
    )(part)


def sum_chips(part, land, order):
    nb = PACK_HALF // PACK_BLOCK

    def body(order_ref, p_ref, l1_ref, l2_ref, l3_ref, o_ref):
        o_ref[...] = ((p_ref[0].astype(f32) + l1_ref[0].astype(f32)) + l2_ref[0].astype(f32)) + l3_ref[0].astype(f32)

    slot = lambda j: pl.BlockSpec((1, PACK_BLOCK, LANES), lambda i, order_ref: (order_ref[j], i, 0))
    return pl.pallas_call(
        body, name="sum_chips",
        grid_spec=pltpu.PrefetchScalarGridSpec(
            num_scalar_prefetch=1, grid=(nb,), in_specs=[slot(0), slot(1), slot(2), slot(3)],
            out_specs=pl.BlockSpec((PACK_BLOCK, LANES), lambda i, order_ref: (i, 0))),
        out_shape=jax.ShapeDtypeStruct((PACK_HALF, LANES), f32),
        compiler_params=_cparams("parallel"),
    )(order, part, land, land, land)


def swap_core_halves(red):
    def body(r_ref, out_ref, send_sem, recv_sem):
        x, y, c = _place()
        cp = _remote(r_ref, out_ref, send_sem, recv_sem, (x, y, 1 - c))
        cp.start()
        cp.wait()

    return pl.pallas_call(
        body, name="swap_core_halves", in_specs=[HBM_SPEC], out_specs=HBM_SPEC,
        out_shape=jax.ShapeDtypeStruct(red.shape, red.dtype),
        scratch_shapes=[pltpu.SemaphoreType.DMA(()), pltpu.SemaphoreType.DMA(())],
    )(red)


N_DEV = 8


def all_reduce_small(v):
    def body(src_ref, out_ref, land_ref, send_sems, recv_sems):
        x, y, c = _place()
        me = 4 * x + 2 * y + c
        copies = []
        for r in range(1, N_DEV):
            peer = ((1 - x) if r & 4 else x, (1 - y) if r & 2 else y, (1 - c) if r & 1 else c)
            copies.append(_remote(src_ref, land_ref.at[r], send_sems.at[r - 1], recv_sems.at[r - 1], peer))
        for cp in copies:
            cp.start()
        land_ref[0] = src_ref[...]
        for cp in copies:
            cp.wait()
        acc = land_ref[me]
        for d in range(1, N_DEV):
            acc = acc + land_ref[jnp.bitwise_xor(me, d)]
        out_ref[...] = acc

    vm = pl.BlockSpec(memory_space=pltpu.VMEM)
    return pl.pallas_call(
        body, name="all_reduce_small", in_specs=[vm], out_specs=vm,
        out_shape=jax.ShapeDtypeStruct(v.shape, v.dtype),
        scratch_shapes=[pltpu.VMEM((N_DEV,) + v.shape, v.dtype),
                        pltpu.SemaphoreType.DMA((N_DEV - 1,)), pltpu.SemaphoreType.DMA((N_DEV - 1,))],
    )(v)


def adamw(w, g, m, v, *, name, tr):
    R, C = w.shape
    tr = min(tr, R)

    def body(w_ref, g_ref, m_ref, v_ref, d_ref, nm_ref, nv_ref):
        gv = g_ref[...]
        nm = ADAM_B1 * m_ref[...] + (1.0 - ADAM_B1) * gv
        nv = ADAM_B2 * v_ref[...] + (1.0 - ADAM_B2) * jnp.square(gv)
        m_hat = nm / (1.0 - ADAM_B1 ** ADAM_STEP)
        v_hat = nv / (1.0 - ADAM_B2 ** ADAM_STEP)
        d_ref[...] = -ADAM_LR * (m_hat / (jnp.sqrt(v_hat) + ADAM_EPS) + ADAM_WD * w_ref[...])
        nm_ref[...] = nm
        nv_ref[...] = nv

    blk = pl.BlockSpec((tr, C), lambda i: (i, 0))
    out = jax.ShapeDtypeStruct((R, C), f32)
    return pl.pallas_call(
        body, name=name, grid=(R // tr,), in_specs=[blk] * 4, out_specs=[blk] * 3, out_shape=[out] * 3,
        compiler_params=_cparams("parallel"),
    )(w, g, m, v)


BIG_SHARDS = (("w_in", (1024, 900), True), ("w_out", (256, 1024), False), ("w_cq", (256, 512), False),
              ("w_ckv", (256, 1024), False), ("w_co", (512, 256), True), ("w_mlp1", (1024, 1024), True),
              ("w_mlp2", (1024, 1024), False))
CONV_SHARD = (CONV_WIDTH, 3 * GDN_WIDTH // N_CHIPS)
SMALL_DIMS = (("norm_mix_g", 1024), ("fox_qnorm_g", 64), ("fox_knorm_g", 64), ("fox_f_bias", 8), ("fox_onorm_g", 64),
              ("gdn_A_log", 4), ("gdn_dt_bias", 4), ("gdn_onorm_g", 128), ("norm_xattn_g", 1024), ("mem_norm_g", 1024),
              ("xattn_qnorm_g", 128), ("xattn_knorm_g", 128), ("norm_mlp_g", 1024))
WEIGHT_ORDER = ("norm_mix_g", "w_in", "fox_qnorm_g", "fox_knorm_g", "fox_f_bias", "fox_onorm_g", "gdn_conv_w", "gdn_A_log",
                "gdn_dt_bias", "gdn_onorm_g", "w_out", "norm_xattn_g", "mem_norm_g", "w_cq", "w_ckv", "xattn_qnorm_g",
                "xattn_knorm_g", "w_co", "norm_mlp_g", "w_mlp1", "w_mlp2")


def _pack_rows(pieces, rows, lead=()):
    flat = []
    for p in pieces:
        p = p.reshape(lead + (-1,))
        pad = (-p.shape[-1]) % LANES
        flat.append(jnp.pad(p, [(0, 0)] * len(lead) + [(0, pad)]) if pad else p)
    cat = jnp.concatenate(flat, axis=-1)
    cat = jnp.pad(cat, [(0, 0)] * len(lead) + [(0, rows * LANES - cat.shape[-1])])
    return cat.reshape(lead + (rows, LANES))


def _unpack_rows(buf, sizes, lead=()):
    flat = buf.reshape(lead + (-1,))
    out, off = [], 0
    for n in sizes:
        out.append(flat[..., off:off + n])
        off += n + (-n) % LANES
    return out


def _conv_to_wire(conv):
    return lax.bitcast_convert_type(conv, bf16)


def _conv_from_wire(wire):
    return lax.bitcast_convert_type(wire, f32)


SMALL_ROWS = 96
SMALL_ADAM_ROWS = 56


def kernel(x, mem, norm_mix_g, w_in, fox_qnorm_g, fox_knorm_g, fox_f_bias, fox_onorm_g, gdn_conv_w, gdn_A_log, gdn_dt_bias, gdn_onorm_g, w_out, norm_xattn_g, mem_norm_g, w_cq, w_ckv, xattn_qnorm_g, xattn_knorm_g, w_co, norm_mlp_g, w_mlp1, w_mlp2, loss_target, m_norm_mix_g, m_w_in, m_fox_qnorm_g, m_fox_knorm_g, m_fox_f_bias, m_fox_onorm_g, m_gdn_conv_w, m_gdn_A_log, m_gdn_dt_bias, m_gdn_onorm_g, m_w_out, m_norm_xattn_g, m_mem_norm_g, m_w_cq, m_w_ckv, m_xattn_qnorm_g, m_xattn_knorm_g, m_w_co, m_norm_mlp_g, m_w_mlp1, m_w_mlp2, v_norm_mix_g, v_w_in, v_fox_qnorm_g, v_fox_knorm_g, v_fox_f_bias, v_fox_onorm_g, v_gdn_conv_w, v_gdn_A_log, v_gdn_dt_bias, v_gdn_onorm_g, v_w_out, v_norm_xattn_g, v_mem_norm_g, v_w_cq, v_w_ckv, v_xattn_qnorm_g, v_xattn_knorm_g, v_w_co, v_norm_mlp_g, v_w_mlp1, v_w_mlp2):
    wts = dict(norm_mix_g=norm_mix_g, w_in=w_in, fox_qnorm_g=fox_qnorm_g, fox_knorm_g=fox_knorm_g, fox_f_bias=fox_f_bias,
               fox_onorm_g=fox_onorm_g, gdn_conv_w=gdn_conv_w, gdn_A_log=gdn_A_log, gdn_dt_bias=gdn_dt_bias,
               gdn_onorm_g=gdn_onorm_g, w_out=w_out, norm_xattn_g=norm_xattn_g, mem_norm_g=mem_norm_g, w_cq=w_cq, w_ckv=w_ckv,
               xattn_qnorm_g=xattn_qnorm_g, xattn_knorm_g=xattn_knorm_g, w_co=w_co, norm_mlp_g=norm_mlp_g, w_mlp1=w_mlp1,
               w_mlp2=w_mlp2)
    mom = dict(norm_mix_g=m_norm_mix_g, w_in=m_w_in, fox_qnorm_g=m_fox_qnorm_g, fox_knorm_g=m_fox_knorm_g,
               fox_f_bias=m_fox_f_bias, fox_onorm_g=m_fox_onorm_g, gdn_conv_w=m_gdn_conv_w, gdn_A_log=m_gdn_A_log,
               gdn_dt_bias=m_gdn_dt_bias, gdn_onorm_g=m_gdn_onorm_g, w_out=m_w_out, norm_xattn_g=m_norm_xattn_g,
               mem_norm_g=m_mem_norm_g, w_cq=m_w_cq, w_ckv=m_w_ckv, xattn_qnorm_g=m_xattn_qnorm_g,
               xattn_knorm_g=m_xattn_knorm_g, w_co=m_w_co, norm_mlp_g=m_norm_mlp_g, w_mlp1=m_w_mlp1, w_mlp2=m_w_mlp2)
    var = dict(norm_mix_g=v_norm_mix_g, w_in=v_w_in, fox_qnorm_g=v_fox_qnorm_g, fox_knorm_g=v_fox_knorm_g,
               fox_f_bias=v_fox_f_bias, fox_onorm_g=v_fox_onorm_g, gdn_conv_w=v_gdn_conv_w, gdn_A_log=v_gdn_A_log,
               gdn_dt_bias=v_gdn_dt_bias, gdn_onorm_g=v_gdn_onorm_g, w_out=v_w_out, norm_xattn_g=v_norm_xattn_g,
               mem_norm_g=v_mem_norm_g, w_cq=v_w_cq, w_ckv=v_w_ckv, xattn_qnorm_g=v_xattn_qnorm_g,
               xattn_knorm_g=v_xattn_knorm_g, w_co=v_w_co, norm_mlp_g=v_norm_mlp_g, w_mlp1=v_w_mlp1, w_mlp2=v_w_mlp2)
    B, S, D = x.shape
    T = B * S
    big_sizes = [r * c for _, (r, c), _ in BIG_SHARDS]
    conv_wire = 2 * CONV_SHARD[0] * CONV_SHARD[1]

    shards = [wts[n][0].astype(MXU_DTYPE) for n, _, _ in BIG_SHARDS] + [_conv_to_wire(gdn_conv_w[0])]
    chip = 2 * lax.axis_index("x") + lax.axis_index("y")
    packed = _pack_rows(shards, PACK_ROWS)
    gathered = lax.dynamic_update_slice(all_gather_shards(packed), packed[None], (chip, 0, 0))
    pieces =_unpack_rows(gathered, big_sizes + [conv_wire], lead=(N_CHIPS,))
    full = {}
    for (n, (r, c), by_cols), p in zip(BIG_SHARDS, pieces):
        p = p.reshape(N_CHIPS, r, c)
        full[n] = p.transpose(1, 0, 2).reshape(r, N_CHIPS * c) if by_cols else p.reshape(N_CHIPS * r, c)
    conv_full = _conv_from_wire(pieces[-1].reshape((N_CHIPS,) + CONV_SHARD + (2,)))
    conv_full = conv_full.transpose(1, 0, 2).reshape(CONV_WIDTH, 3 * GDN_WIDTH)
    w = dict(wa=align_w_in(full["w_in"]), w_out=full["w_out"], w_cq=full["w_cq"], w_ckv=full["w_ckv"], w_co=full["w_co"],
             w_mlp1=full["w_mlp1"], w_mlp2=full["w_mlp2"], conv_w=conv_full)
    sp = {n: wts[n] for n, _ in SMALL_DIMS}

    loss_part, grad_x, g_big, g_small = local_step(x.reshape(T, D), mem.reshape(-1, D), loss_target.reshape(T, D), w, sp, B=B)

    small_pieces = [g_small[n] for n, _ in SMALL_DIMS] + [g_small["gdn_conv_w"], loss_part]
    small_sizes = [d for _, d in SMALL_DIMS] + [CONV_WIDTH * 3 * GDN_WIDTH, LANES]
    red_small = _unpack_rows(all_reduce_small(_pack_rows(small_pieces, SMALL_ROWS)), small_sizes)
    grads = {n: p.reshape(1, d) for (n, d), p in zip(SMALL_DIMS, red_small)}
    chip = 2 * lax.axis_index("x") + lax.axis_index("y")
    conv_grad = lax.dynamic_slice(red_small[-2].reshape(CONV_WIDTH, 3 * GDN_WIDTH), (0, chip * CONV_SHARD[1]), CONV_SHARD)
    grads["gdn_conv_w"] = conv_grad.reshape((1,) + CONV_SHARD)
    loss = red_small[-1][0]

    by_chip = []
    for n, (r, c), by_cols in BIG_SHARDS:
        g = g_big[n]
        g = g.reshape(r, N_CHIPS, c).transpose(1, 0, 2) if by_cols else g.reshape(N_CHIPS, r, c)
        by_chip.append(g.astype(WIRE_DTYPE))
    G = _pack_rows(by_chip, PACK_ROWS, lead=(N_CHIPS,))
    core = lax.axis_index("c").astype(jnp.int32).reshape(1)
    chip_part = add_core_halves(G, exchange_core_halves(G), core)
    order = jnp.stack([chip, chip ^ 2, chip ^ 1, chip ^ 3]).astype(jnp.int32)
    mine = sum_chips(chip_part, scatter_to_chips(chip_part), order)
    theirs = swap_core_halves(mine)
    south = core[0] == 0
    reduced = jnp.concatenate([jnp.where(south, mine, theirs), jnp.where(south, theirs, mine)], axis=0)
    for (n, (r, c), _), p in zip(BIG_SHARDS, _unpack_rows(reduced, big_sizes)):
        grads[n] = p.reshape(1, r, c)

    delta, new_m, new_v = {}, {}, {}
    for n, (r, c), _ in BIG_SHARDS:
        d, nm, nv = adamw(wts[n][0], grads[n][0], mom[n][0], var[n][0], name="adamw_" + n, tr=256)
        delta[n], new_m[n], new_v[n] = d[None], nm[None], nv[None]
    small_names = [n for n, _ in SMALL_DIMS] + ["gdn_conv_w"]
    small_sz = [d for _, d in SMALL_DIMS] + [CONV_SHARD[0] * CONV_SHARD[1]]
    packed4 = [_pack_rows([src[n] for n in small_names], SMALL_ADAM_ROWS) for src in (wts, grads, mom, var)]
    outs = adamw(*packed4, name="adamw_small", tr=SMALL_ADAM_ROWS)
    for dst, buf in zip((delta, new_m, new_v), outs):
        for n, p in zip(small_names, _unpack_rows(buf, small_sz)):
            dst[n] = p.reshape(wts[n].shape)

    return (loss, grad_x.reshape(B, S, D), *[grads[n] for n in WEIGHT_ORDER], *[delta[n] for n in WEIGHT_ORDER],
            *[new_m[n] for n in WEIGHT_ORDER], *[new_v[n] for n in WEIGHT_ORDER])
```

```python
import functools

import jax
import jax.numpy as jnp
import numpy as np
from jax import lax
from jax.experimental import pallas as pl
from jax.experimental.pallas import tpu as pltpu

f32 = jnp.float32
bf16 = jnp.bfloat16
MXU_DTYPE = jnp.bfloat16
WIRE_DTYPE = jnp.bfloat16
INV_PRECISION = lax.Precision.HIGH

D_MODEL = 1024
FOX_HEADS = 8
FOX_HEAD_DIM = 64
FOX_WIDTH = 512
GDN_HEADS = 4
GDN_HEAD_DIM = 128
GDN_WIDTH = 512
CONV_WIDTH = 4
GDN_CHUNK = 64
XATTN_HEADS = 4
XATTN_HEAD_DIM = 128
XATTN_WIDTH = 512
D_FF = 4096
IN_DIM = 3600
EPS = 1e-6
NEG_INF = -1e30
LANES = 128
ADAM_LR = 0.001
ADAM_B1 = 0.9
ADAM_B2 = 0.999
ADAM_EPS = 1e-08
ADAM_WD = 0.01
ADAM_STEP = 10
VMEM_LIMIT = 48 * 1024 * 1024

COL_FOX = 0
COL_GDN = 1536
COL_Z = 3072
COL_SMALL = 3584
IN_ALIGNED = 3840
IN_TILE = 768
SM_F = 0
SM_B = 8
SM_A = 12


def _cparams(*sem):
    return pltpu.CompilerParams(dimension_semantics=sem, vmem_limit_bytes=VMEM_LIMIT)


def _mx(v):
    return v.astype(MXU_DTYPE)


def _dot(a, b, dims, precision=None):
    return lax.dot_general(a, b, (dims, ((), ())), preferred_element_type=f32, precision=precision)


def _dotm(a, b, dims):
    return _dot(_mx(a), _mx(b), dims)


NN = ((1,), (0,))
NT = ((1,), (1,))
TN = ((0,), (0,))


def matmul(a, b, *, name, ta=False, tb=False, b_stacked=False, out_stacked=False, residual=None, relu2_out=False,
           relu2_bwd_aux=None, out_dtype=f32, tm=1024, tn=1024, tk=1024):
    M, K = (a.shape[1], a.shape[0]) if ta else a.shape
    if b_stacked:
        b_cols = b.shape[2]
        N, tk = (b.shape[1], min(tk, b_cols)) if tb else (N_CHIPS * b_cols, tk)
        tn = tn if tb else min(tn, b_cols)
        assert K == (N_CHIPS * b_cols if tb else b.shape[1]), (name, a.shape, b.shape)
    else:
        N = b.shape[0] if tb else b.shape[1]
    if out_stacked:
        tn = min(tn, N // N_CHIPS)
    tm, tn, tk = min(tm, M), min(tn, N), min(tk, K)
    assert M % tm == 0 and N % tn == 0 and K % tk == 0, (name, M, N, K)
    nk = K // tk
    has_res = residual is not None
    has_aux = relu2_bwd_aux is not None

    def body(*refs):
        a_ref, b_ref = refs[0], refs[1]
        pos = 2
        res_ref = aux_ref = None
        if has_res:
            res_ref = refs[pos]
            pos += 1
        if has_aux:
            aux_ref = refs[pos]
            pos += 1
        o_ref = refs[pos]
        pos += 1
        act_ref = None
        if relu2_out:
            act_ref = refs[pos]
            pos += 1
        acc_ref = refs[pos]
        k = pl.program_id(2)

        @pl.when(k == 0)
        def _():
            acc_ref[...] = jnp.zeros_like(acc_ref)

        dims = ((0,) if ta else (1,), (1,) if tb else (0,))
        acc_ref[...] += _dot(_mx(a_ref[...]), _mx(b_ref[...]), dims)

        @pl.when(k == nk - 1)
        def _():
            r = acc_ref[...]
            if has_res:
                r = r + res_ref[...]
            if has_aux:
                r = r * (2.0 * jnp.maximum(aux_ref[...], 0.0))
            o_ref[...] = r.astype(o_ref.dtype)
            if relu2_out:
                act_ref[...] = jnp.square(jnp.maximum(r, 0.0)).astype(act_ref.dtype)

    a_spec = pl.BlockSpec((tk, tm), lambda i, j, k: (k, i)) if ta else pl.BlockSpec((tm, tk), lambda i, j, k: (i, k))
    if b_stacked and tb:
        per = b_cols // tk
        b_spec = pl.BlockSpec((None, tn, tk), lambda i, j, k: (k // per, j, k % per))
    elif b_stacked:
        per = b_cols // tn
        b_spec = pl.BlockSpec((None, tk, tn), lambda i, j, k: (j // per, k, j % per))
    else:
        b_spec = pl.BlockSpec((tn, tk), lambda i, j, k: (j, k)) if tb else pl.BlockSpec((tk, tn), lambda i, j, k: (k, j))
    if out_stacked:
        assert not (has_res or has_aux or relu2_out), name
        per_o = N // N_CHIPS // tn
        o_spec = pl.BlockSpec((None, tm, tn), lambda i, j, k: (j // per_o, i, j % per_o))
        out_full = (N_CHIPS, M, N // N_CHIPS)
    else:
        o_spec = pl.BlockSpec((tm, tn), lambda i, j, k: (i, j))
        out_full = (M, N)
    in_specs, args = [a_spec, b_spec], [a, b]
    if has_res:
        in_specs.append(o_spec)
        args.append(residual)
    if has_aux:
        in_specs.append(o_spec)
        args.append(relu2_bwd_aux)
    out_shape = [jax.ShapeDtypeStruct(out_full, out_dtype)]
    out_specs = [o_spec]
    if relu2_out:
        out_shape.append(jax.ShapeDtypeStruct((M, N), MXU_DTYPE))
        out_specs.append(o_spec)
    res = pl.pallas_call(
        body, name=name, grid=(M // tm, N // tn, nk), in_specs=in_specs, out_specs=out_specs, out_shape=out_shape,
        scratch_shapes=[pltpu.VMEM((tm, tn), f32)],
        compiler_params=_cparams("parallel", "parallel", "arbitrary"),
    )(*args)
    return res if relu2_out else res[0]


def rms_fwd(x, g, *, name, tr=512):
    R, D = x.shape
    tr = min(tr, R)

    def body(x_ref, g_ref, o_ref):
        xv = x_ref[...]
        y = xv * lax.rsqrt(jnp.mean(xv * xv, axis=-1, keepdims=True) + EPS)
        o_ref[...] = (y * g_ref[...]).astype(o_ref.dtype)

    return pl.pallas_call(
        body, name=name, grid=(R // tr,),
        in_specs=[pl.BlockSpec((tr, D), lambda i: (i, 0)), pl.BlockSpec((1, D), lambda i: (0, 0))],
        out_specs=pl.BlockSpec((tr, D), lambda i: (i, 0)),
        out_shape=jax.ShapeDtypeStruct((R, D), MXU_DTYPE),
        compiler_params=_cparams("parallel"),
    )(x, g)


def rms_bwd(x, g, dh, residual, *, name, tr=512):
    R, D = x.shape
    tr = min(tr, R)
    has_res = residual is not None

    def body(*refs):
        if has_res:
            x_ref, g_ref, dh_ref, res_ref, dx_ref, dg_ref = refs
        else:
            x_ref, g_ref, dh_ref, dx_ref, dg_ref = refs
        xv = x_ref[...]
        rstd = lax.rsqrt(jnp.mean(xv * xv, axis=-1, keepdims=True) + EPS)
        xhat = xv * rstd
        dh = dh_ref[...].astype(f32)
        gd = dh * g_ref[...]
        dx = rstd * (gd - xhat * jnp.mean(gd * xhat, axis=-1, keepdims=True))
        if has_res:
            dx = dx + res_ref[...]
        dx_ref[...] = dx

        @pl.when(pl.program_id(0) == 0)
        def _():
            dg_ref[...] = jnp.zeros_like(dg_ref)

        dg_ref[...] += jnp.sum(dh * xhat, axis=0, keepdims=True)

    row = pl.BlockSpec((tr, D), lambda i: (i, 0))
    vec = pl.BlockSpec((1, D), lambda i: (0, 0))
    in_specs = [row, vec, row] + ([row] if has_res else [])
    args = [x, g, dh] + ([residual] if has_res else [])
    return pl.pallas_call(
        body, name=name, grid=(R // tr,), in_specs=in_specs, out_specs=[row, vec],
        out_shape=[jax.ShapeDtypeStruct((R, D), f32), jax.ShapeDtypeStruct((1, D), f32)],
        compiler_params=_cparams("arbitrary"),
    )(*args)


def loss_head(y, target, *, tr=512):
    R, D = y.shape
    tr = min(tr, R)

    def body(y_ref, t_ref, dy_ref, loss_ref):
        e = y_ref[...] - t_ref[...]
        dy_ref[...] = e * (1.0 / D)

        @pl.when(pl.program_id(0) == 0)
        def _():
            loss_ref[...] = jnp.zeros_like(loss_ref)

        part = 0.5 * jnp.sum(jnp.mean(e * e, axis=-1, keepdims=True), axis=0, keepdims=True)
        loss_ref[...] += jnp.broadcast_to(part, loss_ref.shape)

    row = pl.BlockSpec((tr, D), lambda i: (i, 0))
    return pl.pallas_call(
        body, name="loss_head", grid=(R // tr,), in_specs=[row, row],
        out_specs=[row, pl.BlockSpec((1, LANES), lambda i: (0, 0))],
        out_shape=[jax.ShapeDtypeStruct((R, D), f32), jax.ShapeDtypeStruct((1, LANES), f32)],
        compiler_params=_cparams("arbitrary"),
    )(y, target)


def _head_rms(v, g):
    r = lax.rsqrt(jnp.mean(v * v, axis=-1, keepdims=True) + EPS)
    return v * r * g, r


def _head_rms_bwd(v, r, g, dn):
    vhat = v * r
    gd = dn * g
    dv = r * (gd - vhat * jnp.mean(gd * vhat, axis=-1, keepdims=True))
    return dv, jnp.sum(dn * vhat, axis=0, keepdims=True)


def _softmax_rows(s):
    m = jnp.max(s, axis=-1, keepdims=True)
    e = jnp.exp(s - m)
    return e / jnp.sum(e, axis=-1, keepdims=True)


def xattn_fwd(cq, ckv, gq, gk, *, B, tq=512):
    T = cq.shape[0]
    S = T // B
    M = ckv.shape[0] // B
    tq = min(tq, S)
    nq = S // tq
    scale = XATTN_HEAD_DIM ** -0.5

    def body(q_ref, k_ref, v_ref, gq_ref, gk_ref, o_ref):
        qn, _ = _head_rms(q_ref[...], gq_ref[...])
        kn, _ = _head_rms(k_ref[...], gk_ref[...])
        p = _softmax_rows(_dot(_mx(qn), _mx(kn), NT) * scale)
        o_ref[...] = _dot(_mx(p), _mx(v_ref[...]), NN).astype(o_ref.dtype)

    hd = XATTN_HEAD_DIM
    vec = pl.BlockSpec((1, hd), lambda b, h, i: (0, 0))
    return pl.pallas_call(
        body, name="xattn_fwd", grid=(B, XATTN_HEADS, nq),
        in_specs=[pl.BlockSpec((tq, hd), lambda b, h, i: (b * nq + i, h)),
                  pl.BlockSpec((M, hd), lambda b, h, i: (b, h)),
                  pl.BlockSpec((M, hd), lambda b, h, i: (b, XATTN_HEADS + h)), vec, vec],
        out_specs=pl.BlockSpec((tq, hd), lambda b, h, i: (b * nq + i, h)),
        out_shape=jax.ShapeDtypeStruct((T, XATTN_WIDTH), MXU_DTYPE),
        compiler_params=_cparams("parallel", "parallel", "parallel"),
    )(cq, ckv, ckv, gq, gk)


def xattn_bwd(cq, ckv, gq, gk, dco, *, B, tq=512):
    T = cq.shape[0]
    S = T // B
    M = ckv.shape[0] // B
    tq = min(tq, S)
    nq = S // tq
    scale = XATTN_HEAD_DIM ** -0.5
    hd = XATTN_HEAD_DIM

    def body(q_ref, k_ref, v_ref, gq_ref, gk_ref, do_ref, dq_ref, dk_ref, dv_ref, dgq_ref, dgk_ref, dkn_acc, dv_acc):
        b, h, i = pl.program_id(0), pl.program_id(1), pl.program_id(2)

        @pl.when((b == 0) & (h == 0) & (i == 0))
        def _():
            dgq_ref[...] = jnp.zeros_like(dgq_ref)
            dgk_ref[...] = jnp.zeros_like(dgk_ref)

        @pl.when(i == 0)
        def _():
            dkn_acc[...] = jnp.zeros_like(dkn_acc)
            dv_acc[...] = jnp.zeros_like(dv_acc)

        q, k, v = q_ref[...], k_ref[...], v_ref[...]
        gqv, gkv = gq_ref[...], gk_ref[...]
        qn, rq = _head_rms(q, gqv)
        kn, rk = _head_rms(k, gkv)
        p = _softmax_rows(_dot(_mx(qn), _mx(kn), NT) * scale)
        do = do_ref[...]
        dv_acc[...] += _dot(_mx(p), _mx(do), TN)
        dp = _dot(_mx(do), _mx(v), NT)
        ds = p * (dp - jnp.sum(dp * p, axis=-1, keepdims=True)) * scale
        dqn = _dot(_mx(ds), _mx(kn), NN)
        dkn_acc[...] += _dot(_mx(ds), _mx(qn), TN)
        dq, dgq = _head_rms_bwd(q, rq, gqv, dqn)
        dq_ref[...] = dq.astype(dq_ref.dtype)
        dgq_ref[...] += dgq

        @pl.when(i == nq - 1)
        def _():
            dk, dgk = _head_rms_bwd(k, rk, gkv, dkn_acc[...])
            dk_ref[...] = dk.astype(dk_ref.dtype)
            dv_ref[...] = dv_acc[...].astype(dv_ref.dtype)
            dgk_ref[...] += dgk

    vec = pl.BlockSpec((1, hd), lambda b, h, i: (0, 0))
    qspec = pl.BlockSpec((tq, hd), lambda b, h, i: (b * nq + i, h))
    kspec = pl.BlockSpec((M, hd), lambda b, h, i: (b, h))
    vspec = pl.BlockSpec((M, hd), lambda b, h, i: (b, XATTN_HEADS + h))
    dq, dk, dv, dgq, dgk = pl.pallas_call(
        body, name="xattn_bwd", grid=(B, XATTN_HEADS, nq),
        in_specs=[qspec, kspec, vspec, vec, vec, qspec],
        out_specs=[qspec, kspec, kspec, vec, vec],
        out_shape=[jax.ShapeDtypeStruct((T, XATTN_WIDTH), MXU_DTYPE),
                   jax.ShapeDtypeStruct((B * M, XATTN_WIDTH), MXU_DTYPE),
                   jax.ShapeDtypeStruct((B * M, XATTN_WIDTH), MXU_DTYPE),
                   jax.ShapeDtypeStruct((1, hd), f32), jax.ShapeDtypeStruct((1, hd), f32)],
        scratch_shapes=[pltpu.VMEM((M, hd), f32), pltpu.VMEM((M, hd), f32)],
        compiler_params=_cparams("arbitrary", "arbitrary", "arbitrary"),
    )(cq, ckv, ckv, gq, gk, dco)
    return dq, jnp.concatenate([dk, dv], axis=1), dgq, dgk


FOX_PAIRS = FOX_HEADS // 2


def _fox_scores(qn, kn, ccol, crow, q0, tq, S, scale):
    s = _dot(_mx(qn), _mx(kn), NT) * scale + ccol - crow
    qpos = q0 + lax.broadcasted_iota(jnp.int32, (tq, S), 0)
    kpos = lax.broadcasted_iota(jnp.int32, (tq, S), 1)
    return jnp.where(kpos <= qpos, s, NEG_INF)


def fox_fwd(P, ccol, crow, gq, gk, go, *, B, tq=256):
    T = P.shape[0]
    S = T // B
    tq = min(tq, S)
    nq = S // tq
    hd = FOX_HEAD_DIM
    scale = hd ** -0.5

    def body(q_ref, k_ref, v_ref, ccol_ref, crow_ref, gq_ref, gk_ref, go_ref, o_ref, oa_ref):
        q0 = pl.program_id(2) * tq
        for e in range(2):
            sl = slice(e * hd, (e + 1) * hd)
            qn, _ = _head_rms(q_ref[:, sl], gq_ref[:, sl])
            kn, _ = _head_rms(k_ref[:, sl], gk_ref[:, sl])
            p = _softmax_rows(_fox_scores(qn, kn, ccol_ref[0, e], crow_ref[0, e], q0, tq, S, scale))
            o = _dot(_mx(p), _mx(v_ref[:, sl]), NN)
            o_ref[:, sl] = o
            oa_ref[:, sl] = _head_rms(o, go_ref[:, sl])[0].astype(oa_ref.dtype)

    W = 2 * hd
    vec = pl.BlockSpec((1, W), lambda b, h, i: (0, 0))
    ospec = pl.BlockSpec((tq, W), lambda b, h, i: (b * nq + i, h))
    return pl.pallas_call(
        body, name="fox_fwd", grid=(B, FOX_PAIRS, nq),
        in_specs=[pl.BlockSpec((tq, W), lambda b, h, i: (b * nq + i, h)),
                  pl.BlockSpec((S, W), lambda b, h, i: (b, FOX_PAIRS + h)),
                  pl.BlockSpec((S, W), lambda b, h, i: (b, 2 * FOX_PAIRS + h)),
                  pl.BlockSpec((1, 2, tq, 1), lambda b, h, i: (b, h, i, 0)),
                  pl.BlockSpec((1, 2, 1, S), lambda b, h, i: (b, h, 0, 0)), vec, vec, vec],
        out_specs=[ospec, ospec],
        out_shape=[jax.ShapeDtypeStruct((T, FOX_WIDTH), f32), jax.ShapeDtypeStruct((T, FOX_WIDTH), MXU_DTYPE)],
        compiler_params=_cparams("parallel", "parallel", "parallel"),
    )(P, P, P, ccol, crow, gq, gk, go)


def fox_bwd(P, ccol, crow, gq, gk, go, o_raw, d_oab, *, B, tq=256):
    T = P.shape[0]
    S = T // B
    tq = min(tq, S)
    nq = S // tq
    hd = FOX_HEAD_DIM
    scale = hd ** -0.5

    def body(q_ref, k_ref, v_ref, ccol_ref, crow_ref, gq_ref, gk_ref, go_ref, o_ref, doa_ref,
             dq_ref, dk_ref, dv_ref, dccol_ref, dcrow_ref, dgq_ref, dgk_ref, dgo_ref, dkn_acc, dv_acc, dcrow_acc):
        b, h, i = pl.program_id(0), pl.program_id(1), pl.program_id(2)
        q0 = i * tq

        @pl.when((b == 0) & (h == 0) & (i == 0))
        def _():
            dgq_ref[...] = jnp.zeros_like(dgq_ref)
            dgk_ref[...] = jnp.zeros_like(dgk_ref)
            dgo_ref[...] = jnp.zeros_like(dgo_ref)

        @pl.when(i == 0)
        def _():
            dkn_acc[...] = jnp.zeros_like(dkn_acc)
            dv_acc[...] = jnp.zeros_like(dv_acc)
            dcrow_acc[...] = jnp.zeros_like(dcrow_acc)

        for e in range(2):
            sl = slice(e * hd, (e + 1) * hd)
            q, k, v = q_ref[:, sl], k_ref[:, sl], v_ref[:, sl]
            gqv, gkv, gov = gq_ref[:, sl], gk_ref[:, sl], go_ref[:, sl]
            qn, rq = _head_rms(q, gqv)
            kn, rk = _head_rms(k, gkv)
            p = _softmax_rows(_fox_scores(qn, kn, ccol_ref[0, e], crow_ref[0, e], q0, tq, S, scale))
            o = o_ref[:, sl]
            ro = lax.rsqrt(jnp.mean(o * o, axis=-1, keepdims=True) + EPS)
            do, dgo = _head_rms_bwd(o, ro, gov, doa_ref[:, sl])
            dgo_ref[:, sl] += dgo
            dv_acc[e] += _dot(_mx(p), _mx(do), TN)
            dp = _dot(_mx(do), _mx(v), NT)
            ds = p * (dp - jnp.sum(do * o, axis=-1, keepdims=True))
            dccol_ref[0, e] = jnp.sum(ds, axis=1, keepdims=True)
            dcrow_acc[e] -= jnp.sum(ds, axis=0, keepdims=True)
            dqn = _dot(_mx(ds), _mx(kn), NN) * scale
            dkn_acc[e] += _dot(_mx(ds), _mx(qn), TN) * scale
            dq, dgq = _head_rms_bwd(q, rq, gqv, dqn)
            dq_ref[:, sl] = dq.astype(dq_ref.dtype)
            dgq_ref[:, sl] += dgq

        @pl.when(i == nq - 1)
        def _():
            for e in range(2):
                sl = slice(e * hd, (e + 1) * hd)
                k = k_ref[:, sl]
                gkv = gk_ref[:, sl]
                rk = lax.rsqrt(jnp.mean(k * k, axis=-1, keepdims=True) + EPS)
                dk, dgk = _head_rms_bwd(k, rk, gkv, dkn_acc[e])
                dk_ref[:, sl] = dk.astype(dk_ref.dtype)
                dv_ref[:, sl] = dv_acc[e].astype(dv_ref.dtype)
                dgk_ref[:, sl] += dgk
                dcrow_ref[0, e] = dcrow_acc[e]

    W = 2 * hd
    vec = pl.BlockSpec((1, W), lambda b, h, i: (0, 0))
    qspec = pl.BlockSpec((tq, W), lambda b, h, i: (b * nq + i, h))
    kvout = pl.BlockSpec((S, W), lambda b, h, i: (b, h))
    colspec = pl.BlockSpec((1, 2, tq, 1), lambda b, h, i: (b, h, i, 0))
    rowspec = pl.BlockSpec((1, 2, 1, S), lambda b, h, i: (b, h, 0, 0))
    return pl.pallas_call(
        body, name="fox_bwd", grid=(B, FOX_PAIRS, nq),
        in_specs=[qspec,
                  pl.BlockSpec((S, W), lambda b, h, i: (b, FOX_PAIRS + h)),
                  pl.BlockSpec((S, W), lambda b, h, i: (b, 2 * FOX_PAIRS + h)),
                  colspec, rowspec, vec, vec, vec, qspec, qspec],
        out_specs=[qspec, kvout, kvout, colspec, rowspec, vec, vec, vec],
        out_shape=[jax.ShapeDtypeStruct((T, FOX_WIDTH), MXU_DTYPE), jax.ShapeDtypeStruct((T, FOX_WIDTH), MXU_DTYPE),
                   jax.ShapeDtypeStruct((T, FOX_WIDTH), MXU_DTYPE),
                   jax.ShapeDtypeStruct((B, FOX_HEADS, S, 1), f32), jax.ShapeDtypeStruct((B, FOX_HEADS, 1, S), f32),
                   jax.ShapeDtypeStruct((1, W), f32), jax.ShapeDtypeStruct((1, W), f32), jax.ShapeDtypeStruct((1, W), f32)],
        scratch_shapes=[pltpu.VMEM((2, S, hd), f32), pltpu.VMEM((2, S, hd), f32), pltpu.VMEM((2, 1, S), f32)],
        compiler_params=_cparams("arbitrary", "arbitrary", "arbitrary"),
    )(P, P, P, ccol, crow, gq, gk, go, o_raw, d_oab)


FOX_TQ = 512
FOX_TK = 512
GROUP_PRECISION = lax.Precision.HIGH


def _head_mean(v):
    n = v.shape[1]
    r = lax.broadcasted_iota(jnp.int32, (n, n), 0) // FOX_HEAD_DIM
    c = lax.broadcasted_iota(jnp.int32, (n, n), 1) // FOX_HEAD_DIM
    return _dot(v, (r == c).astype(f32), NN, GROUP_PRECISION) * (1.0 / FOX_HEAD_DIM)


def fox_prep_fwd(P, gq, gk, *, tr=512):
    T = P.shape[0]
    tr = min(tr, T)
    scale = FOX_HEAD_DIM ** -0.5

    def body(q_ref, k_ref, v_ref, gq_ref, gk_ref, qn_ref, kn_ref, vb_ref):
        q, k = q_ref[...], k_ref[...]
        qn_ref[...] = (q * lax.rsqrt(_head_mean(q * q) + EPS) * (gq_ref[...] * scale)).astype(qn_ref.dtype)
        kn_ref[...] = (k * lax.rsqrt(_head_mean(k * k) + EPS) * gk_ref[...]).astype(kn_ref.dtype)
        vb_ref[...] = v_ref[...].astype(vb_ref.dtype)

    W = FOX_WIDTH
    col = lambda j: pl.BlockSpec((tr, W), lambda i: (i, j))
    vec = pl.BlockSpec((1, W), lambda i: (0, 0))
    out = jax.ShapeDtypeStruct((T, W), MXU_DTYPE)
    return pl.pallas_call(
        body, name="fox_prep_fwd", grid=(T // tr,), in_specs=[col(0), col(1), col(2), vec, vec],
        out_specs=[col(0)] * 3, out_shape=[out] * 3, compiler_params=_cparams("parallel"),
    )(P, P, P, gq, gk)


def fox_prep_bwd(P, gq, gk, dqn, dkn, *, tr=512):
    T = P.shape[0]
    tr = min(tr, T)
    scale = FOX_HEAD_DIM ** -0.5

    def body(q_ref, k_ref, gq_ref, gk_ref, dqn_ref, dkn_ref, dq_ref, dk_ref, dgq_ref, dgk_ref):
        @pl.when(pl.program_id(0) == 0)
        def _():
            dgq_ref[...] = jnp.zeros_like(dgq_ref)
            dgk_ref[...] = jnp.zeros_like(dgk_ref)

        def one(x, g, dn, dx_ref, dg_ref):
            r = lax.rsqrt(_head_mean(x * x) + EPS)
            xhat = x * r
            gd = dn * g
            dx_ref[...] = (r * (gd - xhat * _head_mean(gd * xhat))).astype(dx_ref.dtype)
            return jnp.sum(dn * xhat, axis=0, keepdims=True)

        dgq_ref[...] += scale * one(q_ref[...], gq_ref[...] * scale, dqn_ref[...], dq_ref, dgq_ref)
        dgk_ref[...] += one(k_ref[...], gk_ref[...], dkn_ref[...], dk_ref, dgk_ref)

    W = FOX_WIDTH
    col = lambda j: pl.BlockSpec((tr, W), lambda i: (i, j))
    vec = pl.BlockSpec((1, W), lambda i: (0, 0))
    return pl.pallas_call(
        body, name="fox_prep_bwd", grid=(T // tr,), in_specs=[col(0), col(1), vec, vec, col(0), col(0)],
        out_specs=[col(0), col(0), vec, vec],
        out_shape=[jax.ShapeDtypeStruct((T, W), MXU_DTYPE), jax.ShapeDtypeStruct((T, W), MXU_DTYPE),
                   jax.ShapeDtypeStruct((1, W), f32), jax.ShapeDtypeStruct((1, W), f32)],
        compiler_params=_cparams("arbitrary"),
    )(P, P, gq, gk, dqn, dkn)


def _fox_tile_scores(q, k_ref, ccol_ref, cq, e, j, sl, mask_off):
    tq, tk = FOX_TQ, FOX_TK
    rows = pl.ds(pl.multiple_of(j * tk, tk), tk)
    k = k_ref[rows, sl]
    s = _dot(k, q, NT) + cq - ccol_ref[0, e, rows, :]
    if mask_off is not None:
        key = lax.broadcasted_iota(jnp.int32, (tk, tq), 0) + mask_off
        query = lax.broadcasted_iota(jnp.int32, (tk, tq), 1)
        s = jnp.where(key <= query, s, NEG_INF)
    return s, k, rows


def _fox_sweep(i, update, carry):
    nd = FOX_TQ // FOX_TK
    carry = lax.fori_loop(0, i * nd, lambda j, cr: update(cr, j, None), carry)
    for d in range(nd):
        carry = update(carry, i * nd + d, d * FOX_TK)
    return carry


def fox_core_fwd(qn, kn, vb, ccol, crow, go, *, B):
    T = qn.shape[0]
    S = T // B
    tq = FOX_TQ
    nq = S // tq
    hd = FOX_HEAD_DIM

    def body(q_ref, k_ref, v_ref, ccol_ref, crow_ref, go_ref, o_ref, oa_ref, lse_ref):
        i = pl.program_id(2)
        for e in range(2):
            sl = slice(e * hd, (e + 1) * hd)
            q = q_ref[:, sl]
            cq = crow_ref[0, e, i]

            def update(carry, j, mask_off):
                m, l, acc = carry
                s, _, rows = _fox_tile_scores(q, k_ref, ccol_ref, cq, e, j, sl, mask_off)
                m2 = jnp.maximum(m, jnp.max(s, axis=0, keepdims=True))
                a = jnp.exp(m - m2)
                p = jnp.exp(s - m2)
                return m2, a * l + jnp.sum(p, axis=0, keepdims=True), a * acc + _dot(v_ref[rows, sl], _mx(p), TN)

            carry = (jnp.full((1, tq), NEG_INF, f32), jnp.zeros((1, tq), f32), jnp.zeros((hd, tq), f32))
            m, l, acc = _fox_sweep(i, update, carry)
            o = (acc / l).T
            o_ref[:, sl] = o
            oa_ref[:, sl] = _head_rms(o, go_ref[:, sl])[0].astype(oa_ref.dtype)
            lse_ref[0, e, 0] = m + jnp.log(l)

    W = 2 * hd
    qspec = pl.BlockSpec((tq, W), lambda b, h, i: (b * nq + i, h))
    kspec = pl.BlockSpec((S, W), lambda b, h, i: (b, h))
    return pl.pallas_call(
        body, name="fox_core_fwd", grid=(B, FOX_PAIRS, nq),
        in_specs=[qspec, kspec, kspec, pl.BlockSpec((1, 2, S, 1), lambda b, h, i: (b, h, 0, 0)),
                  pl.BlockSpec((1, 2, nq, 1, tq), lambda b, h, i: (b, h, 0, 0, 0)),
                  pl.BlockSpec((1, W), lambda b, h, i: (0, 0))],
        out_specs=[qspec, qspec, pl.BlockSpec((1, 2, 1, 1, tq), lambda b, h, i: (b, h, i, 0, 0))],
        out_shape=[jax.ShapeDtypeStruct((T, FOX_WIDTH), f32), jax.ShapeDtypeStruct((T, FOX_WIDTH), MXU_DTYPE),
                   jax.ShapeDtypeStruct((B, FOX_HEADS, nq, 1, tq), f32)],
        compiler_params=_cparams("parallel", "parallel", "parallel"),
    )(qn, kn, vb, ccol, crow, go)


def fox_core_bwd(qn, kn, vb, ccol, crow, go, o_raw, lse, d_oab, *, B):
    T = qn.shape[0]
    S = T // B
    tq = FOX_TQ
    nq = S // tq
    hd = FOX_HEAD_DIM

    def body(q_ref, k_ref, v_ref, ccol_ref, crow_ref, go_ref, o_ref, lse_ref, doa_ref,
             dq_ref, dk_ref, dv_ref, dccol_ref, dcrow_ref, dgo_ref, dk_acc, dv_acc, dck_acc):
        b, h, i = pl.program_id(0), pl.program_id(1), pl.program_id(2)

        @pl.when((b == 0) & (h == 0) & (i == 0))
        def _():
            dgo_ref[...] = jnp.zeros_like(dgo_ref)

        @pl.when(i == 0)
        def _():
            dk_acc[...] = jnp.zeros_like(dk_acc)
            dv_acc[...] = jnp.zeros_like(dv_acc)
            dck_acc[...] = jnp.zeros_like(dck_acc)

        for e in range(2):
            sl = slice(e * hd, (e + 1) * hd)
            q = q_ref[:, sl]
            cq = crow_ref[0, e, i]
            lse_e = lse_ref[0, e, 0]
            o = o_ref[:, sl]
            ro = lax.rsqrt(jnp.mean(o * o, axis=-1, keepdims=True) + EPS)
            do, dgo = _head_rms_bwd(o, ro, go_ref[:, sl], doa_ref[:, sl])
            dgo_ref[:, sl] += dgo
            delta = jnp.sum((do * o).T, axis=0, keepdims=True)
            do_b = _mx(do)

            def update(carry, j, mask_off):
                dq, dcq = carry
                s, k, rows = _fox_tile_scores(q, k_ref, ccol_ref, cq, e, j, sl, mask_off)
                p = jnp.exp(s - lse_e)
                dv_acc[e, rows, :] += _dot(_mx(p), do_b, NN)
                ds = p * (_dot(v_ref[rows, sl], do_b, NT) - delta)
                dck_acc[e, rows, :] -= jnp.sum(ds, axis=1, keepdims=True)
                ds_b = _mx(ds)
                dk_acc[e, rows, :] += _dot(ds_b, q, NN)
                return dq + _dot(ds_b, k, TN), dcq + jnp.sum(ds, axis=0, keepdims=True)

            dq, dcq = _fox_sweep(i, update, (jnp.zeros((tq, hd), f32), jnp.zeros((1, tq), f32)))
            dq_ref[:, sl] = dq
            dcrow_ref[0, e, 0] = dcq

        @pl.when(i == nq - 1)
        def _():
            for e in range(2):
                sl = slice(e * hd, (e + 1) * hd)
                dk_ref[:, sl] = dk_acc[e]
                dv_ref[:, sl] = dv_acc[e].astype(dv_ref.dtype)
            dccol_ref[0] = dck_acc[...]

    W = 2 * hd
    qspec = pl.BlockSpec((tq, W), lambda b, h, i: (b * nq + i, h))
    kspec = pl.BlockSpec((S, W), lambda b, h, i: (b, h))
    colspec = pl.BlockSpec((1, 2, S, 1), lambda b, h, i: (b, h, 0, 0))
    rowspec = pl.BlockSpec((1, 2, nq, 1, tq), lambda b, h, i: (b, h, 0, 0, 0))
    tilespec = pl.BlockSpec((1, 2, 1, 1, tq), lambda b, h, i: (b, h, i, 0, 0))
    vec = pl.BlockSpec((1, W), lambda b, h, i: (0, 0))
    return pl.pallas_call(
        body, name="fox_core_bwd", grid=(B, FOX_PAIRS, nq),
        in_specs=[qspec, kspec, kspec, colspec, rowspec, vec, qspec, tilespec, qspec],
        out_specs=[qspec, kspec, kspec, colspec, tilespec, vec],
        out_shape=[jax.ShapeDtypeStruct((T, FOX_WIDTH), f32), jax.ShapeDtypeStruct((T, FOX_WIDTH), f32),
                   jax.ShapeDtypeStruct((T, FOX_WIDTH), MXU_DTYPE),
                   jax.ShapeDtypeStruct((B, FOX_HEADS, S, 1), f32), jax.ShapeDtypeStruct((B, FOX_HEADS, nq, 1, tq), f32),
                   jax.ShapeDtypeStruct((1, W), f32)],
        scratch_shapes=[pltpu.VMEM((2, S, hd), f32), pltpu.VMEM((2, S, hd), f32), pltpu.VMEM((2, S, 1), f32)],
        compiler_params=_cparams("arbitrary", "arbitrary", "arbitrary"),
    )(qn, kn, vb, ccol, crow, go, o_raw, lse, d_oab)


def _lane_mask(lo, hi, shape):
    lane = lax.broadcasted_iota(jnp.int32, shape, 1)
    return (lane >= lo) & (lane < hi)


def _cumsum_rows(v, period, reverse=False):
    n = v.shape[0]
    pos = lax.broadcasted_iota(jnp.int32, v.shape, 0) % period
    sh = 1
    while sh < period:
        if reverse:
            v = v + jnp.where(pos + sh < period, pltpu.roll(v, n - sh, 0), 0.0)
        else:
            v = v + jnp.where(pos >= sh, pltpu.roll(v, sh, 0), 0.0)
        sh *= 2
    return v


def _gate_values(z, bias, alog):
    zb = z + bias
    ls = jax.nn.log_sigmoid(zb)
    beta = jax.nn.sigmoid(z)
    g = -jnp.exp(alog) * jax.nn.softplus(zb)
    return zb, ls, beta, g


def gates_fwd(P, bias, alog, *, B):
    T = P.shape[0]
    S = T // B

    def body(z_ref, bias_ref, alog_ref, o_ref):
        z = z_ref[...]
        _, ls, beta, g = _gate_values(z, bias_ref[...], alog_ref[...])
        c = _cumsum_rows(ls, S)
        gc = _cumsum_rows(g, GDN_CHUNK)
        o = jnp.where(_lane_mask(SM_F, SM_F + FOX_HEADS, z.shape), c, 0.0)
        o = jnp.where(_lane_mask(SM_B, SM_B + GDN_HEADS, z.shape), beta, o)
        o = jnp.where(_lane_mask(SM_A, SM_A + GDN_HEADS, z.shape), gc, o)
        o_ref[...] = o

    vec = pl.BlockSpec((1, LANES), lambda b: (0, 0))
    return pl.pallas_call(
        body, name="gates_fwd", grid=(B,),
        in_specs=[pl.BlockSpec((S, LANES), lambda b: (b, COL_SMALL // LANES)), vec, vec],
        out_specs=pl.BlockSpec((S, LANES), lambda b: (b, 0)),
        out_shape=jax.ShapeDtypeStruct((T, LANES), f32),
        compiler_params=_cparams("parallel"),
    )(P, bias, alog)


def gates_bwd(P, bias, alog, dgates, *, B):
    T = P.shape[0]
    S = T // B

    def body(z_ref, bias_ref, alog_ref, dg_ref, dz_ref, par_ref):
        z = z_ref[...]
        zb, ls, beta, g = _gate_values(z, bias_ref[...], alog_ref[...])
        d = dg_ref[...]
        dls = _cumsum_rows(d, S, reverse=True)
        dgr = _cumsum_rows(d, GDN_CHUNK, reverse=True)
        sig = jax.nn.sigmoid(zb)
        dz_f = dls * (1.0 - sig)
        dz_b = d * beta * (1.0 - beta)
        dz_a = dgr * (-jnp.exp(alog_ref[...])) * sig
        dz = jnp.where(_lane_mask(SM_F, SM_F + FOX_HEADS, z.shape), dz_f, 0.0)
        dz = jnp.where(_lane_mask(SM_B, SM_B + GDN_HEADS, z.shape), dz_b, dz)
        dz = jnp.where(_lane_mask(SM_A, SM_A + GDN_HEADS, z.shape), dz_a, dz)
        dz_ref[...] = dz.astype(dz_ref.dtype)

        @pl.when(pl.program_id(0) == 0)
        def _():
            par_ref[...] = jnp.zeros_like(par_ref)

        dalog = jnp.where(_lane_mask(SM_A, SM_A + GDN_HEADS, z.shape), dgr * g, 0.0)
        par_ref[0:1, :] += jnp.sum(dz, axis=0, keepdims=True)
        par_ref[1:2, :] += jnp.sum(dalog, axis=0, keepdims=True)

    vec = pl.BlockSpec((1, LANES), lambda b: (0, 0))
    return pl.pallas_call(
        body, name="gates_bwd", grid=(B,),
        in_specs=[pl.BlockSpec((S, LANES), lambda b: (b, COL_SMALL // LANES)), vec, vec,
                  pl.BlockSpec((S, LANES), lambda b: (b, 0))],
        out_specs=[pl.BlockSpec((S, LANES), lambda b: (b, 0)), pl.BlockSpec((8, LANES), lambda b: (0, 0))],
        out_shape=[jax.ShapeDtypeStruct((T, LANES), MXU_DTYPE), jax.ShapeDtypeStruct((8, LANES), f32)],
        compiler_params=_cparams("arbitrary"),
    )(P, bias, alog, dgates)


GDN_BLOCKS = 3 * GDN_HEADS


def _shift_rows(v, d, reverse=False):
    if d == 0:
        return v
    n = v.shape[0]
    row = lax.broadcasted_iota(jnp.int32, v.shape, 0)
    if reverse:
        return jnp.where(row + d < n, pltpu.roll(v, n - d, 0), 0.0)
    return jnp.where(row >= d, pltpu.roll(v, d, 0), 0.0)


def _conv_silu(x, w):
    pre = sum(w[j:j + 1, :] * _shift_rows(x, CONV_WIDTH - 1 - j) for j in range(CONV_WIDTH))
    return pre, pre * jax.nn.sigmoid(pre)


def gdn_prep_fwd(P, conv_w, *, B):
    T = P.shape[0]
    S = T // B

    def body(x_ref, w_ref, o_ref):
        _, y = _conv_silu(x_ref[...], w_ref[...])
        yn = y * lax.rsqrt(jnp.sum(y * y, axis=-1, keepdims=True) + EPS)
        o_ref[...] = jnp.where(pl.program_id(1) < 2 * GDN_HEADS, yn, y)

    return pl.pallas_call(
        body, name="gdn_prep_fwd", grid=(B, GDN_BLOCKS),
        in_specs=[pl.BlockSpec((S, LANES), lambda b, j: (b, COL_GDN // LANES + j)),
                  pl.BlockSpec((CONV_WIDTH, LANES), lambda b, j: (0, j))],
        out_specs=pl.BlockSpec((S, LANES), lambda b, j: (b, j)),
        out_shape=jax.ShapeDtypeStruct((T, 3 * GDN_WIDTH), f32),
        compiler_params=_cparams("parallel", "parallel"),
    )(P, conv_w)


def gdn_prep_bwd(P, conv_w, dG, *, B):
    T = P.shape[0]
    S = T // B

    def body(x_ref, w_ref, dg_ref, dx_ref, dw_ref):
        x, w = x_ref[...], w_ref[...]
        pre, y = _conv_silu(x, w)
        dn = dg_ref[...]
        r = lax.rsqrt(jnp.sum(y * y, axis=-1, keepdims=True) + EPS)
        n = y * r
        dy_norm = r * (dn - n * jnp.sum(dn * n, axis=-1, keepdims=True))
        dy = jnp.where(pl.program_id(0) < 2 * GDN_HEADS, dy_norm, dn)
        sg = jax.nn.sigmoid(pre)
        dpre = dy * (sg * (1.0 + pre * (1.0 - sg)))
        dx = sum(w[j:j + 1, :] * _shift_rows(dpre, CONV_WIDTH - 1 - j, reverse=True) for j in range(CONV_WIDTH))
        dx_ref[...] = dx.astype(dx_ref.dtype)

        @pl.when(pl.program_id(1) == 0)
        def _():
            dw_ref[...] = jnp.zeros_like(dw_ref)

        for j in range(CONV_WIDTH):
            dw_ref[j:j + 1, :] += jnp.sum(dpre * _shift_rows(x, CONV_WIDTH - 1 - j), axis=0, keepdims=True)

    return pl.pallas_call(
        body, name="gdn_prep_bwd", grid=(GDN_BLOCKS, B),
        in_specs=[pl.BlockSpec((S, LANES), lambda j, b: (b, COL_GDN // LANES + j)),
                  pl.BlockSpec((CONV_WIDTH, LANES), lambda j, b: (0, j)),
                  pl.BlockSpec((S, LANES), lambda j, b: (b, j))],
        out_specs=[pl.BlockSpec((S, LANES), lambda j, b: (b, j)),
                   pl.BlockSpec((CONV_WIDTH, LANES), lambda j, b: (0, j))],
        out_shape=[jax.ShapeDtypeStruct((T, 3 * GDN_WIDTH), MXU_DTYPE),
                   jax.ShapeDtypeStruct((CONV_WIDTH, 3 * GDN_WIDTH), f32)],
        compiler_params=_cparams("arbitrary", "arbitrary"),
    )(P, conv_w, dG)


GDN_GROUP = 16
B_NN = (((2,), (1,)), ((0,), (0,)))
B_NT = (((2,), (2,)), ((0,), (0,)))
B_TN = (((1,), (1,)), ((0,), (0,)))


def _bmm(a, b, dims, precision=None):
    if precision is None:
        a, b = _mx(a), _mx(b)
    return lax.dot_general(a, b, dims, preferred_element_type=f32, precision=precision)


def _tri_inverse(A):
    C = A.shape[-1]
    row = lax.broadcasted_iota(jnp.int32, A.shape, 1)
    col = lax.broadcasted_iota(jnp.int32, A.shape, 2)
    eye = (row == col).astype(f32)
    X = jnp.where((row // 4) == (col // 4), -A, 0.0)
    X2 = _bmm(X, X, B_NN, INV_PRECISION)
    Tm = eye + X + X2 + _bmm(X, X2, B_NN, INV_PRECISION)
    b = 4
    while b < C:
        off = ((row // (2 * b)) == (col // (2 * b))) & ((row // b) != (col // b))
        Tm = Tm - _bmm(_bmm(Tm, jnp.where(off, A, 0.0), B_NN, INV_PRECISION), Tm, B_NN, INV_PRECISION)
        b *= 2
    return Tm


def _pick_lane(block, lane_idx):
    lane = lax.broadcasted_iota(jnp.int32, block.shape, 1)
    return jnp.sum(jnp.where(lane == lane_idx, block, 0.0), axis=1, keepdims=True)


def _gdn_local(q, k, v, beta, gc, Tm=None):
    C = GDN_CHUNK
    n = q.shape[0] // C
    q = q.reshape(n, C, -1) * (GDN_HEAD_DIM ** -0.5)
    k = k.reshape(n, C, -1)
    v = v.reshape(n, C, -1)
    beta = beta.reshape(n, C, 1)
    gc = gc.reshape(n, C, 1)
    row = lax.broadcasted_iota(jnp.int32, (n, C, C), 1)
    col = lax.broadcasted_iota(jnp.int32, (n, C, C), 2)
    gcT = jnp.swapaxes(jnp.broadcast_to(gc, (n, C, C)), 1, 2)
    D = jnp.exp(jnp.where(row >= col, gc - gcT, NEG_INF))
    kb = k * beta
    vb = v * beta
    A = jnp.where(row > col, _bmm(kb, k, B_NT) * D, 0.0)
    Gam = jnp.exp(gc)
    kg = kb * Gam
    gl = gc[:, C - 1:C, :]
    kdec = jnp.exp(gl - gc)
    loc = dict(q=q, k=k, v=v, beta=beta, gc=gc, D=D, kb=kb, vb=vb, A=A, Gam=Gam, kg=kg,
               kdec=kdec, kd=k * kdec, qg=q * Gam, gam=jnp.exp(gl), row=row, col=col)
    if Tm is None:
        Tm = _tri_inverse(A)
        loc.update(u=_bmm(Tm, vb, B_NN), w=_bmm(Tm, kg, B_NN), M=_bmm(q, k, B_NT) * D)
    else:
        Tm = Tm.reshape(n, C, C)
    loc["Tm"] = Tm
    return loc


def _gdn_store_local(loc, r0, u_s, w_s, qg_s, kd_s, M_s, gam_s, c0):
    n = loc["u"].shape[0]
    R = n * GDN_CHUNK
    u_s[pl.ds(r0, R), :] = loc["u"].reshape(R, -1)
    w_s[pl.ds(r0, R), :] = loc["w"].reshape(R, -1)
    qg_s[pl.ds(r0, R), :] = loc["qg"].reshape(R, -1)
    kd_s[pl.ds(r0, R), :] = loc["kd"].reshape(R, -1)
    M_s[pl.ds(r0, R), :] = loc["M"].reshape(R, -1)
    gam_s[pl.ds(c0, n)] = jnp.broadcast_to(loc["gam"], (n, 1, LANES))


def _gdn_specs(S):
    blk = lambda off: pl.BlockSpec((S, LANES), lambda b, h: (b, off + h))
    return blk


def gdn_fwd(G, gates, P, g_on, *, B):
    T = G.shape[0]
    S = T // B
    C = GDN_CHUNK
    N = S // C
    grp = min(GDN_GROUP, N)
    R = grp * C
    hd = GDN_HEAD_DIM

    def body(q_ref, k_ref, v_ref, gt_ref, z_ref, gon_ref, o_ref, ob_ref, st_ref, u_s, w_s, qg_s, kd_s, M_s, gam_s):
        h = pl.program_id(1)

        def local(gi, carry):
            r0 = pl.multiple_of(gi * R, R)
            gt = gt_ref[pl.ds(r0, R), :]
            loc = _gdn_local(q_ref[pl.ds(r0, R), :], k_ref[pl.ds(r0, R), :], v_ref[pl.ds(r0, R), :],
                             _pick_lane(gt, SM_B + h), _pick_lane(gt, SM_A + h))
            _gdn_store_local(loc, r0, u_s, w_s, qg_s, kd_s, M_s, gam_s, gi * grp)
            return carry

        lax.fori_loop(0, N // grp, local, 0)

        def step(n, state):
            r0 = pl.multiple_of(n * C, C)
            st_ref[0, 0, n] = state
            v_new = u_s[pl.ds(r0, C), :] - _dotm(w_s[pl.ds(r0, C), :], state, NN)
            o_ref[pl.ds(r0, C), :] = (_dotm(qg_s[pl.ds(r0, C), :], state, NN)
                                      + _dotm(M_s[pl.ds(r0, C), :], v_new, NN))
            return state * gam_s[n] + _dotm(kd_s[pl.ds(r0, C), :], v_new, TN)

        lax.fori_loop(0, N, step, jnp.zeros((hd, hd), f32))
        o = o_ref[...]
        z = z_ref[...]
        ob_ref[...] = (_head_rms(o, gon_ref[...])[0] * (z * jax.nn.sigmoid(z))).astype(ob_ref.dtype)

    blk = lambda off: pl.BlockSpec((S, LANES), lambda b, h: (b, off + h))
    rows = lambda: pltpu.VMEM((S, hd), f32)
    return pl.pallas_call(
        body, name="gdn_fwd", grid=(B, GDN_HEADS),
        in_specs=[blk(0), blk(GDN_HEADS), blk(2 * GDN_HEADS), pl.BlockSpec((S, LANES), lambda b, h: (b, 0)),
                  blk(COL_Z // LANES), pl.BlockSpec((1, hd), lambda b, h: (0, 0))],
        out_specs=[blk(0), blk(0), pl.BlockSpec((1, 1, N, hd, hd), lambda b, h: (b, h, 0, 0, 0))],
        out_shape=[jax.ShapeDtypeStruct((T, GDN_WIDTH), f32), jax.ShapeDtypeStruct((T, GDN_WIDTH), MXU_DTYPE),
                   jax.ShapeDtypeStruct((B, GDN_HEADS, N, hd, hd), f32)],
        scratch_shapes=[rows(), rows(), rows(), rows(), pltpu.VMEM((S, C), f32), pltpu.VMEM((N, 1, LANES), f32)],
        compiler_params=_cparams("parallel", "parallel"),
    )(G, G, G, gates, P, g_on)


def gdn_bwd(G, gates, P, g_on, o_raw, states, d_oab, *, B):
    T = G.shape[0]
    S = T // B
    C = GDN_CHUNK
    N = S // C
    grp = min(GDN_GROUP, N)
    R = grp * C
    hd = GDN_HEAD_DIM

    def body(q_ref, k_ref, v_ref, gt_ref, z_ref, gon_ref, o_ref, st_ref, dob_ref,
             dq_ref, dk_ref, dv_ref, dgt_ref, dz_ref, dgon_ref,
             u_s, w_s, qg_s, kd_s, M_s, gam_s, do_s, du_s, dw_s, dqg_s, dkd_s, dM_s, dgl_s, Tm_s):
        b, h = pl.program_id(0), pl.program_id(1)

        @pl.when((b == 0) & (h == 0))
        def _():
            dgon_ref[...] = jnp.zeros_like(dgon_ref)

        @pl.when(h == 0)
        def _():
            dgt_ref[...] = jnp.zeros_like(dgt_ref)

        def group_inputs(gi, Tm_of=None):
            r0 = pl.multiple_of(gi * R, R)
            gt = gt_ref[pl.ds(r0, R), :]
            Tm = None if Tm_of is None else Tm_of[pl.ds(r0, R), :]
            return r0, _gdn_local(q_ref[pl.ds(r0, R), :], k_ref[pl.ds(r0, R), :], v_ref[pl.ds(r0, R), :],
                                  _pick_lane(gt, SM_B + h), _pick_lane(gt, SM_A + h), Tm)

        def local(gi, carry):
            r0, loc = group_inputs(gi)
            _gdn_store_local(loc, r0, u_s, w_s, qg_s, kd_s, M_s, gam_s, gi * grp)
            Tm_s[pl.ds(r0, R), :] = loc["Tm"].reshape(R, C)
            o, z, gon = o_ref[pl.ds(r0, R), :], z_ref[pl.ds(r0, R), :], gon_ref[...]
            dob = dob_ref[pl.ds(r0, R), :]
            on, ro = _head_rms(o, gon)
            sz = jax.nn.sigmoid(z)
            dz_ref[pl.ds(r0, R), :] = (dob * on * (sz * (1.0 + z * (1.0 - sz)))).astype(dz_ref.dtype)
            do, dgon = _head_rms_bwd(o, ro, gon, dob * (z * sz))
            do_s[pl.ds(r0, R), :] = do
            dgon_ref[...] += dgon
            return carry

        lax.fori_loop(0, N // grp, local, 0)

        def step(t, dS):
            n = N - 1 - t
            r0 = pl.multiple_of(n * C, C)
            rows = pl.ds(r0, C)
            state = st_ref[0, 0, n]
            w_n, M_n, kd_n, do_n = w_s[rows, :], M_s[rows, :], kd_s[rows, :], do_s[rows, :]
            v_new = u_s[rows, :] - _dotm(w_n, state, NN)
            dv_new = _dotm(M_n, do_n, TN) + _dotm(kd_n, dS, NN)
            du_s[rows, :] = dv_new
            dw_s[rows, :] = -_dotm(dv_new, state, NT)
            dqg_s[rows, :] = _dotm(do_n, state, NT)
            dM_s[rows, :] = _dotm(do_n, v_new, NT)
            dkd_s[rows, :] = _dotm(v_new, dS, NT)
            gam = gam_s[n]
            dgl_s[n] = jnp.broadcast_to(jnp.sum(jnp.sum(dS * state, axis=1, keepdims=True), axis=0, keepdims=True), (1, LANES)) * gam
            return dS * gam + _dotm(qg_s[rows, :], do_n, TN) - _dotm(w_n, dv_new, TN)

        lax.fori_loop(0, N, step, jnp.zeros((hd, hd), f32))

        def finish(gi, carry):
            r0, L = group_inputs(gi, Tm_s)
            n = grp
            rows = pl.ds(r0, R)
            g3 = lambda ref: ref[rows, :].reshape(n, C, -1)
            du, dw, dqg, dkd, dM = g3(du_s), g3(dw_s), g3(dqg_s), g3(dkd_s), g3(dM_s)
            L["M"] = g3(M_s)
            TmT = jnp.swapaxes(L["Tm"], 1, 2)
            dTm = _bmm(du, L["vb"], B_NT) + _bmm(dw, L["kg"], B_NT)
            dvb = _bmm(TmT, du, B_NN)
            dkg = _bmm(TmT, dw, B_NN)
            dA = jnp.where(L["row"] > L["col"], -_bmm(_bmm(TmT, dTm, B_NN), TmT, B_NN), 0.0)
            dKK = dA * L["D"]
            dQK = dM * L["D"]
            dkb = _bmm(dKK, L["k"], B_NN) + dkg * L["Gam"]
            dk = (_bmm(dKK, L["kb"], B_TN) + _bmm(dQK, L["q"], B_TN) + dkd * L["kdec"] + L["beta"] * dkb)
            dq = (_bmm(dQK, L["k"], B_NN) + dqg * L["Gam"]) * (GDN_HEAD_DIM ** -0.5)
            E = dA * L["A"] + dM * L["M"]
            r = jnp.sum(dkd * L["kd"], axis=-1, keepdims=True)
            dgc = (jnp.sum(E, axis=2, keepdims=True) - jnp.sum(jnp.swapaxes(E, 1, 2), axis=2, keepdims=True)
                   + jnp.sum(dkg * L["kg"], axis=-1, keepdims=True) + jnp.sum(dqg * L["qg"], axis=-1, keepdims=True) - r)
            dgl = jnp.sum(r, axis=1, keepdims=True) + dgl_s[pl.ds(gi * n, n)][:, :, 0:1]
            rowc = lax.broadcasted_iota(jnp.int32, (n, C, 1), 1)
            dgc = dgc + jnp.where(rowc == C - 1, dgl, 0.0)
            dbeta = jnp.sum(dkb * L["k"], axis=-1, keepdims=True) + jnp.sum(dvb * L["v"], axis=-1, keepdims=True)
            dq_ref[rows, :] = dq.reshape(R, hd)
            dk_ref[rows, :] = dk.reshape(R, hd)
            dv_ref[rows, :] = (L["beta"] * dvb).reshape(R, hd)
            lane = lax.broadcasted_iota(jnp.int32, (R, LANES), 1)
            dgt_ref[rows, :] += (jnp.where(lane == SM_B + h, dbeta.reshape(R, 1), 0.0)
                                 + jnp.where(lane == SM_A + h, dgc.reshape(R, 1), 0.0))
            return carry

        lax.fori_loop(0, N // grp, finish, 0)

    blk = lambda off: pl.BlockSpec((S, LANES), lambda b, h: (b, off + h))
    rows = lambda: pltpu.VMEM((S, hd), f32)
    return pl.pallas_call(
        body, name="gdn_bwd", grid=(B, GDN_HEADS),
        in_specs=[blk(0), blk(GDN_HEADS), blk(2 * GDN_HEADS), pl.BlockSpec((S, LANES), lambda b, h: (b, 0)),
                  blk(COL_Z // LANES), pl.BlockSpec((1, hd), lambda b, h: (0, 0)), blk(0),
                  pl.BlockSpec((1, 1, N, hd, hd), lambda b, h: (b, h, 0, 0, 0)), blk(GDN_HEADS)],
        out_specs=[blk(0), blk(0), blk(0), pl.BlockSpec((S, LANES), lambda b, h: (b, 0)), blk(0),
                   pl.BlockSpec((1, hd), lambda b, h: (0, 0))],
        out_shape=[jax.ShapeDtypeStruct((T, GDN_WIDTH), f32), jax.ShapeDtypeStruct((T, GDN_WIDTH), f32),
                   jax.ShapeDtypeStruct((T, GDN_WIDTH), f32), jax.ShapeDtypeStruct((T, LANES), f32),
                   jax.ShapeDtypeStruct((T, GDN_WIDTH), MXU_DTYPE), jax.ShapeDtypeStruct((1, hd), f32)],
        scratch_shapes=[rows(), rows(), rows(), rows(), pltpu.VMEM((S, C), f32), pltpu.VMEM((N, 1, LANES), f32),
                        rows(), rows(), rows(), rows(), rows(), pltpu.VMEM((S, C), f32), pltpu.VMEM((N, 1, LANES), f32),
                        pltpu.VMEM((S, C), f32)],
        compiler_params=_cparams("arbitrary", "arbitrary"),
    )(G, G, G, gates, P, g_on, o_raw, states, d_oab)


IN_SPLIT = (0, 1536, 1544, 3080, 3088, 3600)


def align_w_in(w):
    s = IN_SPLIT
    pad = jnp.zeros((w.shape[0], IN_ALIGNED - IN_DIM), w.dtype)
    return jnp.concatenate([w[:, s[0]:s[1]], w[:, s[2]:s[3]], w[:, s[4]:s[5]], w[:, s[1]:s[2]], w[:, s[3]:s[4]], pad], axis=1)


def unalign_w_in(wa):
    return jnp.concatenate([wa[:, 0:1536], wa[:, COL_SMALL:COL_SMALL + 8], wa[:, 1536:3072],
                            wa[:, COL_SMALL + 8:COL_SMALL + 16], wa[:, 3072:3584]], axis=1)


def _lanes_vec(pieces):
    v = jnp.zeros((1, LANES), f32)
    for off, a in pieces:
        v = lax.dynamic_update_slice(v, a.astype(f32), (0, off))
    return v


def local_step(x, mem, target, w, sp, *, B):
    T = x.shape[0]
    S = T // B
    gq8, gk8 = jnp.tile(sp["fox_qnorm_g"], (1, FOX_HEADS)), jnp.tile(sp["fox_knorm_g"], (1, FOX_HEADS))
    go2 = jnp.tile(sp["fox_onorm_g"], (1, 2))
    bias = _lanes_vec([(SM_F, sp["fox_f_bias"]), (SM_A, sp["gdn_dt_bias"])])
    alog = _lanes_vec([(SM_A, sp["gdn_A_log"])])

    h1 = rms_fwd(x, sp["norm_mix_g"], name="rms_mix")
    P = matmul(h1, w["wa"], name="mm_in", tn=IN_TILE)
    gates = gates_fwd(P, bias, alog, B=B)
    c = gates[:, SM_F:SM_F + FOX_HEADS].reshape(B, S, FOX_HEADS).transpose(0, 2, 1)
    ccol, crow = c[..., None], c.reshape(B, FOX_HEADS, S // FOX_TQ, 1, FOX_TQ)
    qn, kn, vb = fox_prep_fwd(P, gq8, gk8)
    o_raw, o_a, lse = fox_core_fwd(qn, kn, vb, ccol, crow, go2, B=B)
    G = gdn_prep_fwd(P, w["conv_w"], B=B)
    ob_raw, o_b, states = gdn_fwd(G, gates, P, sp["gdn_onorm_g"], B=B)
    oab = jnp.concatenate([o_a, o_b], axis=1)
    x2 = matmul(oab, w["w_out"], residual=x, name="mm_out")
    hq = rms_fwd(x2, sp["norm_xattn_g"], name="rms_xattn")
    hm = rms_fwd(mem, sp["mem_norm_g"], name="rms_mem")
    cq = matmul(hq, w["w_cq"], name="mm_cq")
    ckv = matmul(hm, w["w_ckv"], name="mm_ckv")
    co = xattn_fwd(cq, ckv, sp["xattn_qnorm_g"], sp["xattn_knorm_g"], B=B)
    x3 = matmul(co, w["w_co"], b_stacked=True, residual=x2, name="mm_co")
    hf = rms_fwd(x3, sp["norm_mlp_g"], name="rms_mlp")
    a, act = matmul(hf, w["w_mlp1"], b_stacked=True, relu2_out=True, name="mm_mlp1")
    x4 = matmul(act, w["w_mlp2"], residual=x3, name="mm_mlp2")
    dy, loss = loss_head(x4, target)

    da = matmul(dy, w["w_mlp2"], tb=True, relu2_bwd_aux=a, out_dtype=MXU_DTYPE, name="mm_d_act")
    g_mlp2 = matmul(act, dy, ta=True, out_dtype=WIRE_DTYPE, name="mm_g_mlp2")
    g_mlp1 = matmul(hf, da, ta=True, out_stacked=True, out_dtype=WIRE_DTYPE, name="mm_g_mlp1")
    dhf = matmul(da, w["w_mlp1"], tb=True, b_stacked=True, name="mm_d_hf")
    dx3, g_norm_mlp = rms_bwd(x3, sp["norm_mlp_g"], dhf, dy, name="rms_mlp_bwd")
    dco = matmul(dx3, w["w_co"], tb=True, b_stacked=True, name="mm_d_co")
    g_co = matmul(co, dx3, ta=True, out_stacked=True, out_dtype=WIRE_DTYPE, name="mm_g_co")
    dcq, dckv, g_xq, g_xk = xattn_bwd(cq, ckv, sp["xattn_qnorm_g"], sp["xattn_knorm_g"], dco, B=B)
    g_cq = matmul(hq, dcq, ta=True, out_dtype=WIRE_DTYPE, name="mm_g_cq")
    dhq = matmul(dcq, w["w_cq"], tb=True, name="mm_d_hq")
    g_ckv = matmul(hm, dckv, ta=True, out_dtype=WIRE_DTYPE, name="mm_g_ckv")
    dhm = matmul(dckv, w["w_ckv"], tb=True, name="mm_d_hm")
    _, g_mem_norm = rms_bwd(mem, sp["mem_norm_g"], dhm, None, name="rms_mem_bwd")
    dx2, g_norm_xattn = rms_bwd(x2, sp["norm_xattn_g"], dhq, dx3, name="rms_xattn_bwd")
    doab = matmul(dx2, w["w_out"], tb=True, name="mm_d_oab")
    g_out = matmul(oab, dx2, ta=True, out_dtype=WIRE_DTYPE, name="mm_g_out")
    dqn, dkn, dv_f, dccol, dcrow, dgo2 = fox_core_bwd(qn, kn, vb, ccol, crow, go2, o_raw, lse, doab, B=B)
    dq_f, dk_f, dgq8, dgk8 = fox_prep_bwd(P, gq8, gk8, dqn, dkn)
    dGq, dGk, dGv, dgt, dz, g_gdn_on = gdn_bwd(G, gates, P, sp["gdn_onorm_g"], ob_raw, states, doab, B=B)
    dPg, g_conv = gdn_prep_bwd(P, w["conv_w"], jnp.concatenate([dGq, dGk, dGv], axis=1), B=B)
    dc = (dccol[..., 0] + dcrow.reshape(B, FOX_HEADS, S)).transpose(0, 2, 1).reshape(T, FOX_HEADS)
    dgates = dgt + jnp.pad(dc, ((0, 0), (SM_F, LANES - SM_F - FOX_HEADS)))
    dsmall, par = gates_bwd(P, bias, alog, dgates, B=B)
    dP = jnp.concatenate([dq_f, dk_f, dv_f, dPg, dz, dsmall, jnp.zeros((T, IN_ALIGNED - COL_SMALL - LANES), MXU_DTYPE)], axis=1)
    g_wa = matmul(h1, dP, ta=True, out_dtype=WIRE_DTYPE, name="mm_g_in", tn=IN_TILE)
    dh1 = matmul(dP, w["wa"], tb=True, name="mm_d_h1", tk=IN_TILE)
    dx, g_norm_mix = rms_bwd(x, sp["norm_mix_g"], dh1, dx2, name="rms_mix_bwd")

    fold = lambda g: jnp.sum(g.reshape(-1, FOX_HEAD_DIM), axis=0, keepdims=True)
    by_rows = lambda g: g.reshape(N_CHIPS, g.shape[0] // N_CHIPS, g.shape[1])
    g_in = unalign_w_in(g_wa).reshape(D_MODEL, N_CHIPS, IN_DIM // N_CHIPS).transpose(1, 0, 2)
    big = dict(w_in=g_in, w_out=by_rows(g_out), w_cq=by_rows(g_cq), w_ckv=by_rows(g_ckv), w_co=g_co, w_mlp1=g_mlp1,
               w_mlp2=by_rows(g_mlp2))
    small = dict(norm_mix_g=g_norm_mix, fox_qnorm_g=fold(dgq8), fox_knorm_g=fold(dgk8),
                 fox_f_bias=par[0:1, SM_F:SM_F + FOX_HEADS], fox_onorm_g=fold(dgo2), gdn_conv_w=g_conv,
                 gdn_A_log=par[1:2, SM_A:SM_A + GDN_HEADS], gdn_dt_bias=par[0:1, SM_A:SM_A + GDN_HEADS],
                 gdn_onorm_g=g_gdn_on, norm_xattn_g=g_norm_xattn, mem_norm_g=g_mem_norm,
                 xattn_qnorm_g=g_xq, xattn_knorm_g=g_xk, norm_mlp_g=g_norm_mlp)
    return loss, dx, big, small


MESH_IDS = pl.DeviceIdType.MESH
N_CHIPS = 4
HBM_SPEC = pl.BlockSpec(memory_space=pltpu.HBM)
PACK_ROWS = 30720
PACK_HALF = PACK_ROWS // 2
PACK_BLOCK = 3072


def _place():
    return lax.axis_index("x"), lax.axis_index("y"), lax.axis_index("c")


def _other_chips(x, y):
    return [(1 - x, y), (x, 1 - y), (1 - x, 1 - y)]


def _remote(src, dst, send_sem, recv_sem, to):
    return pltpu.make_async_remote_copy(src_ref=src, dst_ref=dst, send_sem=send_sem, recv_sem=recv_sem,
                                        device_id=to, device_id_type=MESH_IDS)


def all_gather_shards(packed):
    half = PACK_HALF

    def body(src_ref, out_ref, send_sems, recv_sems):
        x, y, c = _place()
        me_chip = 2 * x + y
        sibling = (x, y, 1 - c)
        chips = _other_chips(x, y)

        def rows(chip, core):
            return out_ref.at[chip, pl.ds(core * half, half), :]

        sends = [_remote(src_ref.at[pl.ds(c * half, half), :], rows(me_chip, c), send_sems.at[j], recv_sems.at[j], (px, py, c))
                 for j, (px, py) in enumerate(chips)]
        for cp in sends:
            cp.start()
        passed = []
        for j, (px, py) in enumerate(chips):
            theirs = rows(2 * px + py, c)
            _remote(theirs, theirs, send_sems.at[j], recv_sems.at[j], (px, py, c)).wait_recv()
            cp = _remote(theirs, theirs, send_sems.at[3 + j], recv_sems.at[3 + j], sibling)
            cp.start()
            passed.append(cp)
        for j, (px, py) in enumerate(chips):
            theirs = rows(2 * px + py, 1 - c)
            _remote(theirs, theirs, send_sems.at[3 + j], recv_sems.at[3 + j], sibling).wait_recv()
        for cp in sends + passed:
            cp.wait_send()

    return pl.pallas_call(
        body, name="all_gather_shards", in_specs=[HBM_SPEC], out_specs=HBM_SPEC,
        out_shape=jax.ShapeDtypeStruct((N_CHIPS,) + packed.shape, packed.dtype),
        scratch_shapes=[pltpu.SemaphoreType.DMA((6,)), pltpu.SemaphoreType.DMA((6,))],
    )(packed)


def exchange_core_halves(G):
    half = PACK_HALF

    def body(g_ref, land_ref, send_sem, recv_sem):
        x, y, c = _place()
        cp = _remote(g_ref.at[:, pl.ds((1 - c) * half, half), :], land_ref, send_sem, recv_sem, (x, y, 1 - c))
        cp.start()
        cp.wait()

    return pl.pallas_call(
        body, name="exchange_core_halves", in_specs=[HBM_SPEC], out_specs=HBM_SPEC,
        out_shape=jax.ShapeDtypeStruct((N_CHIPS, half, LANES), G.dtype),
        scratch_shapes=[pltpu.SemaphoreType.DMA(()), pltpu.SemaphoreType.DMA(())],
    )(G)


def add_core_halves(G, land, core):
    nb = PACK_HALF // PACK_BLOCK

    def body(c_ref, g_ref, l_ref, o_ref):
        o_ref[...] = (g_ref[...].astype(f32) + l_ref[...].astype(f32)).astype(o_ref.dtype)

    blk = (1, PACK_BLOCK, LANES)
    return pl.pallas_call(
        body, name="add_core_halves",
        grid_spec=pltpu.PrefetchScalarGridSpec(
            num_scalar_prefetch=1, grid=(N_CHIPS, nb),
            in_specs=[pl.BlockSpec(blk, lambda k, i, c_ref: (k, c_ref[0] * nb + i, 0)),
                      pl.BlockSpec(blk, lambda k, i, c_ref: (k, i, 0))],
            out_specs=pl.BlockSpec(blk, lambda k, i, c_ref: (k, i, 0))),
        out_shape=jax.ShapeDtypeStruct(land.shape, land.dtype),
        compiler_params=_cparams("parallel", "parallel"),
    )(core, G, land)


def scatter_to_chips(part):
    def body(p_ref, land_ref, send_sems, recv_sems):
        x, y, c = _place()
        me_chip = 2 * x + y
        chips = _other_chips(x, y)
        sends = [_remote(p_ref.at[2 * px + py], land_ref.at[me_chip], send_sems.at[j], recv_sems.at[j], (px, py, c))
                 for j, (px, py) in enumerate(chips)]
        for cp in sends:
            cp.start()
        for j, (px, py) in enumerate(chips):
            slot = land_ref.at[2 * px + py]
            _remote(slot, slot, send_sems.at[j], recv_sems.at[j], (px, py, c)).wait_recv()
        for cp in sends:
            cp.wait_send()

    return pl.pallas_call(
        body, name="scatter_to_chips", in_specs=[HBM_SPEC], out_specs=HBM_SPEC,
        out_shape=jax.ShapeDtypeStruct(part.shape, part.dtype),
        scratch_shapes=[pltpu.SemaphoreType.DMA((3,)), pltpu.SemaphoreType.DMA((3,))],
    )(part)


def sum_chips(part, land, order):
    nb = PACK_HALF // PACK_BLOCK

    def body(order_ref, p_ref, l1_ref, l2_ref, l3_ref, o_ref):
        o_ref[...] = ((p_ref[0].astype(f32) + l1_ref[0].astype(f32)) + l2_ref[0].astype(f32)) + l3_ref[0].astype(f32)

    slot = lambda j: pl.BlockSpec((1, PACK_BLOCK, LANES), lambda i, order_ref: (order_ref[j], i, 0))
    return pl.pallas_call(
        body, name="sum_chips",
        grid_spec=pltpu.PrefetchScalarGridSpec(
            num_scalar_prefetch=1, grid=(nb,), in_specs=[slot(0), slot(1), slot(2), slot(3)],
            out_specs=pl.BlockSpec((PACK_BLOCK, LANES), lambda i, order_ref: (i, 0))),
        out_shape=jax.ShapeDtypeStruct((PACK_HALF, LANES), f32),
        compiler_params=_cparams("parallel"),
    )(order, part, land, land, land)


def swap_core_halves(red):
    def body(r_ref, out_ref, send_sem, recv_sem):
        x, y, c = _place()
        cp = _remote(r_ref, out_ref, send_sem, recv_sem, (x, y, 1 - c))
        cp.start()
        cp.wait()

    return pl.pallas_call(
        body, name="swap_core_halves", in_specs=[HBM_SPEC], out_specs=HBM_SPEC,
        out_shape=jax.ShapeDtypeStruct(red.shape, red.dtype),
        scratch_shapes=[pltpu.SemaphoreType.DMA(()), pltpu.SemaphoreType.DMA(())],
    )(red)


def _half(ref, core):
    rows = ref.shape[-2] // 2
    return ref.at[(slice(None),) * (len(ref.shape) - 2) + (pl.ds(core * rows, rows), slice(None))]


def gather_weights(shards, conv):
    n = len(shards)

    def body(*refs):
        src, conv_src = refs[:n], refs[n]
        out, conv_out = refs[n + 1:2 * n + 1], refs[2 * n + 1]
        send_sems, recv_sems = refs[2 * n + 2], refs[2 * n + 3]
        x, y, c = _place()
        me_chip = 2 * x + y
        sibling = (x, y, 1 - c)
        chips = _other_chips(x, y)
        sends = []
        for a in range(n):
            for j, (px, py) in enumerate(chips):
                sends.append(_remote(_half(src[a], c), _half(out[a].at[me_chip], c),
                                     send_sems.at[6 * a + j], recv_sems.at[6 * a + j], (px, py, c)))
        for j, (px, py) in enumerate(chips):
            sends.append(_remote(conv_src, conv_out.at[me_chip], send_sems.at[6 * n + j], recv_sems.at[6 * n + j], (px, py, c)))
        for cp in sends:
            cp.start()
        passed = []
        for a in range(n):
            for j, (px, py) in enumerate(chips):
                theirs = _half(out[a].at[2 * px + py], c)
                _remote(theirs, theirs, send_sems.at[6 * a + j], recv_sems.at[6 * a + j], (px, py, c)).wait_recv()
                cp = _remote(theirs, theirs, send_sems.at[6 * a + 3 + j], recv_sems.at[6 * a + 3 + j], sibling)
                cp.start()
                passed.append(cp)
        for j, (px, py) in enumerate(chips):
            theirs = conv_out.at[2 * px + py]
            _remote(theirs, theirs, send_sems.at[6 * n + j], recv_sems.at[6 * n + j], (px, py, c)).wait_recv()
        for a in range(n):
            for j, (px, py) in enumerate(chips):
                theirs = _half(out[a].at[2 * px + py], 1 - c)
                _remote(theirs, theirs, send_sems.at[6 * a + 3 + j], recv_sems.at[6 * a + 3 + j], sibling).wait_recv()
        for cp in sends + passed:
            cp.wait_send()

    return pl.pallas_call(
        body, name="gather_weights", in_specs=[HBM_SPEC] * (n + 1), out_specs=[HBM_SPEC] * (n + 1),
        out_shape=[jax.ShapeDtypeStruct((N_CHIPS,) + s.shape, s.dtype) for s in list(shards) + [conv]],
        scratch_shapes=[pltpu.SemaphoreType.DMA((6 * n + 3,)), pltpu.SemaphoreType.DMA((6 * n + 3,))],
    )(*shards, conv)


def swap_grad_halves(grads):
    n = len(grads)

    def body(*refs):
        g, land, send_sems, recv_sems = refs[:n], refs[n:2 * n], refs[2 * n], refs[2 * n + 1]
        x, y, c = _place()
        copies = [_remote(_half(g[a], 1 - c), land[a], send_sems.at[a], recv_sems.at[a], (x, y, 1 - c)) for a in range(n)]
        for cp in copies:
            cp.start()
        for cp in copies:
            cp.wait()

    return pl.pallas_call(
        body, name="swap_grad_halves", in_specs=[HBM_SPEC] * n, out_specs=[HBM_SPEC] * n,
        out_shape=[jax.ShapeDtypeStruct((N_CHIPS, g.shape[1] // 2, g.shape[2]), g.dtype) for g in grads],
        scratch_shapes=[pltpu.SemaphoreType.DMA((n,)), pltpu.SemaphoreType.DMA((n,))],
    )(*grads)


GRAD_ROWS = 256


def add_grad_halves(g, land, core, *, name):
    _, half, cols = land.shape
    tr = min(GRAD_ROWS, half)
    nb = half // tr

    def body(c_ref, g_ref, l_ref, o_ref):
        o_ref[...] = (g_ref[...].astype(f32) + l_ref[...].astype(f32)).astype(o_ref.dtype)

    blk = (1, tr, cols)
    return pl.pallas_call(
        body, name=name,
        grid_spec=pltpu.PrefetchScalarGridSpec(
            num_scalar_prefetch=1, grid=(N_CHIPS, nb),
            in_specs=[pl.BlockSpec(blk, lambda k, i, c_ref: (k, c_ref[0] * nb + i, 0)),
                      pl.BlockSpec(blk, lambda k, i, c_ref: (k, i, 0))],
            out_specs=pl.BlockSpec(blk, lambda k, i, c_ref: (k, i, 0))),
        out_shape=jax.ShapeDtypeStruct(land.shape, land.dtype),
        compiler_params=_cparams("parallel", "parallel"),
    )(core, g, land)


def scatter_grads(parts):
    n = len(parts)

    def body(*refs):
        p, land, send_sems, recv_sems = refs[:n], refs[n:2 * n], refs[2 * n], refs[2 * n + 1]
        x, y, c = _place()
        me_chip = 2 * x + y
        chips = _other_chips(x, y)
        sends = [_remote(p[a].at[2 * px + py], land[a].at[me_chip], send_sems.at[3 * a + j], recv_sems.at[3 * a + j], (px, py, c))
                 for a in range(n) for j, (px, py) in enumerate(chips)]
        for cp in sends:
            cp.start()
        for a in range(n):
            for j, (px, py) in enumerate(chips):
                slot = land[a].at[2 * px + py]
                _remote(slot, slot, send_sems.at[3 * a + j], recv_sems.at[3 * a + j], (px, py, c)).wait_recv()
        for cp in sends:
            cp.wait_send()

    return pl.pallas_call(
        body, name="scatter_grads", in_specs=[HBM_SPEC] * n, out_specs=[HBM_SPEC] * n,
        out_shape=[jax.ShapeDtypeStruct(p.shape, p.dtype) for p in parts],
        scratch_shapes=[pltpu.SemaphoreType.DMA((3 * n,)), pltpu.SemaphoreType.DMA((3 * n,))],
    )(*parts)


def sum_grads(part, land, order, *, name):
    _, half, cols = part.shape
    tr = min(GRAD_ROWS, half)

    def body(order_ref, p_ref, l1_ref, l2_ref, l3_ref, o_ref):
        o_ref[...] = ((p_ref[0].astype(f32) + l1_ref[0].astype(f32)) + l2_ref[0].astype(f32)) + l3_ref[0].astype(f32)

    slot = lambda j: pl.BlockSpec((1, tr, cols), lambda i, order_ref: (order_ref[j], i, 0))
    return pl.pallas_call(
        body, name=name,
        grid_spec=pltpu.PrefetchScalarGridSpec(
            num_scalar_prefetch=1, grid=(half // tr,), in_specs=[slot(0), slot(1), slot(2), slot(3)],
            out_specs=pl.BlockSpec((tr, cols), lambda i, order_ref: (i, 0))),
        out_shape=jax.ShapeDtypeStruct((half, cols), f32),
        compiler_params=_cparams("parallel"),
    )(order, part, land, land, land)


def swap_reduced_halves(mine):
    n = len(mine)

    def body(*refs):
        r, out, send_sems, recv_sems = refs[:n], refs[n:2 * n], refs[2 * n], refs[2 * n + 1]
        x, y, c = _place()
        copies = [_remote(r[a], out[a], send_sems.at[a], recv_sems.at[a], (x, y, 1 - c)) for a in range(n)]
        for cp in copies:
            cp.start()
        for cp in copies:
            cp.wait()

    return pl.pallas_call(
        body, name="swap_reduced_halves", in_specs=[HBM_SPEC] * n, out_specs=[HBM_SPEC] * n,
        out_shape=[jax.ShapeDtypeStruct(r.shape, r.dtype) for r in mine],
        scratch_shapes=[pltpu.SemaphoreType.DMA((n,)), pltpu.SemaphoreType.DMA((n,))],
    )(*mine)


def adamw_halves(w, mine, theirs, m, v, core, *, name):
    R, C = w.shape
    tr = min(GRAD_ROWS, R // 2)
    half_nb = R // 2 // tr

    def body(c_ref, w_ref, a_ref, b_ref, m_ref, v_ref, g_ref, d_ref, nm_ref, nv_ref):
        low = pl.program_id(0) < half_nb
        gv = jnp.where(low == (c_ref[0] == 0), a_ref[...], b_ref[...])
        nm = ADAM_B1 * m_ref[...] + (1.0 - ADAM_B1) * gv
        nv = ADAM_B2 * v_ref[...] + (1.0 - ADAM_B2) * jnp.square(gv)
        m_hat = nm / (1.0 - ADAM_B1 ** ADAM_STEP)
        v_hat = nv / (1.0 - ADAM_B2 ** ADAM_STEP)
        g_ref[...] = gv
        d_ref[...] = -ADAM_LR * (m_hat / (jnp.sqrt(v_hat) + ADAM_EPS) + ADAM_WD * w_ref[...])
        nm_ref[...] = nm
        nv_ref[...] = nv

    full = pl.BlockSpec((tr, C), lambda i, c_ref: (i, 0))
    part = pl.BlockSpec((tr, C), lambda i, c_ref: (i % half_nb, 0))
    out = jax.ShapeDtypeStruct((R, C), f32)
    return pl.pallas_call(
        body, name=name,
        grid_spec=pltpu.PrefetchScalarGridSpec(
            num_scalar_prefetch=1, grid=(2 * half_nb,), in_specs=[full, part, part, full, full], out_specs=[full] * 4),
        out_shape=[out] * 4, compiler_params=_cparams("parallel"),
    )(core, w, mine, theirs, m, v)


N_DEV = 8


def all_reduce_small(v):
    def body(src_ref, out_ref, land_ref, send_sems, recv_sems):
        x, y, c = _place()
        me = 4 * x + 2 * y + c
        copies = []
        for r in range(1, N_DEV):
            peer = ((1 - x) if r & 4 else x, (1 - y) if r & 2 else y, (1 - c) if r & 1 else c)
            copies.append(_remote(src_ref, land_ref.at[r], send_sems.at[r - 1], recv_sems.at[r - 1], peer))
        for cp in copies:
            cp.start()
        land_ref[0] = src_ref[...]
        for cp in copies:
            cp.wait()
        acc = land_ref[me]
        for d in range(1, N_DEV):
            acc = acc + land_ref[jnp.bitwise_xor(me, d)]
        out_ref[...] = acc

    vm = pl.BlockSpec(memory_space=pltpu.VMEM)
    return pl.pallas_call(
        body, name="all_reduce_small", in_specs=[vm], out_specs=vm,
        out_shape=jax.ShapeDtypeStruct(v.shape, v.dtype),
        scratch_shapes=[pltpu.VMEM((N_DEV,) + v.shape, v.dtype),
                        pltpu.SemaphoreType.DMA((N_DEV - 1,)), pltpu.SemaphoreType.DMA((N_DEV - 1,))],
    )(v)


def adamw(w, g, m, v, *, name, tr):
    R, C = w.shape
    tr = min(tr, R)

    def body(w_ref, g_ref, m_ref, v_ref, d_ref, nm_ref, nv_ref):
        gv = g_ref[...]
        nm = ADAM_B1 * m_ref[...] + (1.0 - ADAM_B1) * gv
        nv = ADAM_B2 * v_ref[...] + (1.0 - ADAM_B2) * jnp.square(gv)
        m_hat = nm / (1.0 - ADAM_B1 ** ADAM_STEP)
        v_hat = nv / (1.0 - ADAM_B2 ** ADAM_STEP)
        d_ref[...] = -ADAM_LR * (m_hat / (jnp.sqrt(v_hat) + ADAM_EPS) + ADAM_WD * w_ref[...])
        nm_ref[...] = nm
        nv_ref[...] = nv

    blk = pl.BlockSpec((tr, C), lambda i: (i, 0))
    out = jax.ShapeDtypeStruct((R, C), f32)
    return pl.pallas_call(
        body, name=name, grid=(R // tr,), in_specs=[blk] * 4, out_specs=[blk] * 3, out_shape=[out] * 3,
        compiler_params=_cparams("parallel"),
    )(w, g, m, v)


BIG_SHARDS = (("w_in", (1024, 900), True), ("w_out", (256, 1024), False), ("w_cq", (256, 512), False),
              ("w_ckv", (256, 1024), False), ("w_co", (512, 256), True), ("w_mlp1", (1024, 1024), True),
              ("w_mlp2", (1024, 1024), False))
CONV_SHARD = (CONV_WIDTH, 3 * GDN_WIDTH // N_CHIPS)
SMALL_DIMS = (("norm_mix_g", 1024), ("fox_qnorm_g", 64), ("fox_knorm_g", 64), ("fox_f_bias", 8), ("fox_onorm_g", 64),
              ("gdn_A_log", 4), ("gdn_dt_bias", 4), ("gdn_onorm_g", 128), ("norm_xattn_g", 1024), ("mem_norm_g", 1024),
              ("xattn_qnorm_g", 128), ("xattn_knorm_g", 128), ("norm_mlp_g", 1024))
WEIGHT_ORDER = ("norm_mix_g", "w_in", "fox_qnorm_g", "fox_knorm_g", "fox_f_bias", "fox_onorm_g", "gdn_conv_w", "gdn_A_log",
                "gdn_dt_bias", "gdn_onorm_g", "w_out", "norm_xattn_g", "mem_norm_g", "w_cq", "w_ckv", "xattn_qnorm_g",
                "xattn_knorm_g", "w_co", "norm_mlp_g", "w_mlp1", "w_mlp2")


def _pack_rows(pieces, rows, lead=()):
    flat = []
    for p in pieces:
        p = p.reshape(lead + (-1,))
        pad = (-p.shape[-1]) % LANES
        flat.append(jnp.pad(p, [(0, 0)] * len(lead) + [(0, pad)]) if pad else p)
    cat = jnp.concatenate(flat, axis=-1)
    cat = jnp.pad(cat, [(0, 0)] * len(lead) + [(0, rows * LANES - cat.shape[-1])])
    return cat.reshape(lead + (rows, LANES))


def _unpack_rows(buf, sizes, lead=()):
    flat = buf.reshape(lead + (-1,))
    out, off = [], 0
    for n in sizes:
        out.append(flat[..., off:off + n])
        off += n + (-n) % LANES
    return out


def _conv_to_wire(conv):
    return lax.bitcast_convert_type(conv, bf16)


def _conv_from_wire(wire):
    return lax.bitcast_convert_type(wire, f32)


SMALL_ROWS = 96
SMALL_ADAM_ROWS = 56


def kernel(x, mem, norm_mix_g, w_in, fox_qnorm_g, fox_knorm_g, fox_f_bias, fox_onorm_g, gdn_conv_w, gdn_A_log, gdn_dt_bias, gdn_onorm_g, w_out, norm_xattn_g, mem_norm_g, w_cq, w_ckv, xattn_qnorm_g, xattn_knorm_g, w_co, norm_mlp_g, w_mlp1, w_mlp2, loss_target, m_norm_mix_g, m_w_in, m_fox_qnorm_g, m_fox_knorm_g, m_fox_f_bias, m_fox_onorm_g, m_gdn_conv_w, m_gdn_A_log, m_gdn_dt_bias, m_gdn_onorm_g, m_w_out, m_norm_xattn_g, m_mem_norm_g, m_w_cq, m_w_ckv, m_xattn_qnorm_g, m_xattn_knorm_g, m_w_co, m_norm_mlp_g, m_w_mlp1, m_w_mlp2, v_norm_mix_g, v_w_in, v_fox_qnorm_g, v_fox_knorm_g, v_fox_f_bias, v_fox_onorm_g, v_gdn_conv_w, v_gdn_A_log, v_gdn_dt_bias, v_gdn_onorm_g, v_w_out, v_norm_xattn_g, v_mem_norm_g, v_w_cq, v_w_ckv, v_xattn_qnorm_g, v_xattn_knorm_g, v_w_co, v_norm_mlp_g, v_w_mlp1, v_w_mlp2):
    wts = dict(norm_mix_g=norm_mix_g, w_in=w_in, fox_qnorm_g=fox_qnorm_g, fox_knorm_g=fox_knorm_g, fox_f_bias=fox_f_bias,
               fox_onorm_g=fox_onorm_g, gdn_conv_w=gdn_conv_w, gdn_A_log=gdn_A_log, gdn_dt_bias=gdn_dt_bias,
               gdn_onorm_g=gdn_onorm_g, w_out=w_out, norm_xattn_g=norm_xattn_g, mem_norm_g=mem_norm_g, w_cq=w_cq, w_ckv=w_ckv,
               xattn_qnorm_g=xattn_qnorm_g, xattn_knorm_g=xattn_knorm_g, w_co=w_co, norm_mlp_g=norm_mlp_g, w_mlp1=w_mlp1,
               w_mlp2=w_mlp2)
    mom = dict(norm_mix_g=m_norm_mix_g, w_in=m_w_in, fox_qnorm_g=m_fox_qnorm_g, fox_knorm_g=m_fox_knorm_g,
               fox_f_bias=m_fox_f_bias, fox_onorm_g=m_fox_onorm_g, gdn_conv_w=m_gdn_conv_w, gdn_A_log=m_gdn_A_log,
               gdn_dt_bias=m_gdn_dt_bias, gdn_onorm_g=m_gdn_onorm_g, w_out=m_w_out, norm_xattn_g=m_norm_xattn_g,
               mem_norm_g=m_mem_norm_g, w_cq=m_w_cq, w_ckv=m_w_ckv, xattn_qnorm_g=m_xattn_qnorm_g,
               xattn_knorm_g=m_xattn_knorm_g, w_co=m_w_co, norm_mlp_g=m_norm_mlp_g, w_mlp1=m_w_mlp1, w_mlp2=m_w_mlp2)
    var = dict(norm_mix_g=v_norm_mix_g, w_in=v_w_in, fox_qnorm_g=v_fox_qnorm_g, fox_knorm_g=v_fox_knorm_g,
               fox_f_bias=v_fox_f_bias, fox_onorm_g=v_fox_onorm_g, gdn_conv_w=v_gdn_conv_w, gdn_A_log=v_gdn_A_log,
               gdn_dt_bias=v_gdn_dt_bias, gdn_onorm_g=v_gdn_onorm_g, w_out=v_w_out, norm_xattn_g=v_norm_xattn_g,
               mem_norm_g=v_mem_norm_g, w_cq=v_w_cq, w_ckv=v_w_ckv, xattn_qnorm_g=v_xattn_qnorm_g,
               xattn_knorm_g=v_xattn_knorm_g, w_co=v_w_co, norm_mlp_g=v_norm_mlp_g, w_mlp1=v_w_mlp1, w_mlp2=v_w_mlp2)
    B, S, D = x.shape
    T = B * S
    big_names = [n for n, _, _ in BIG_SHARDS]
    chip = 2 * lax.axis_index("x") + lax.axis_index("y")
    core = lax.axis_index("c").astype(jnp.int32).reshape(1)

    shards = [wts[n][0].astype(MXU_DTYPE) for n in big_names]
    *gathered, conv_all = gather_weights(shards, gdn_conv_w[0])
    own = lambda g, s: lax.dynamic_update_slice(g, s[None], (chip,) + (0,) * s.ndim)
    full = {n: own(g, s) for n, g, s in zip(big_names, gathered, shards)}
    conv_full = own(conv_all, gdn_conv_w[0]).transpose(1, 0, 2).reshape(CONV_WIDTH, 3 * GDN_WIDTH)
    rows = lambda g: g.reshape(N_CHIPS * g.shape[1], g.shape[2])
    w_in_full = full["w_in"].transpose(1, 0, 2).reshape(D_MODEL, IN_DIM)
    w = dict(wa=align_w_in(w_in_full), w_out=rows(full["w_out"]), w_cq=rows(full["w_cq"]), w_ckv=rows(full["w_ckv"]),
             w_co=full["w_co"], w_mlp1=full["w_mlp1"], w_mlp2=rows(full["w_mlp2"]), conv_w=conv_full)
    sp = {n: wts[n] for n, _ in SMALL_DIMS}

    loss_part, grad_x, g_big, g_small = local_step(x.reshape(T, D), mem.reshape(-1, D), loss_target.reshape(T, D), w, sp, B=B)

    small_pieces = [g_small[n] for n, _ in SMALL_DIMS] + [g_small["gdn_conv_w"], loss_part]
    small_sizes = [d for _, d in SMALL_DIMS] + [CONV_WIDTH * 3 * GDN_WIDTH, LANES]
    red_small = _unpack_rows(all_reduce_small(_pack_rows(small_pieces, SMALL_ROWS)), small_sizes)
    grads = {n: p.reshape(1, d) for (n, d), p in zip(SMALL_DIMS, red_small)}
    conv_grad = lax.dynamic_slice(red_small[-2].reshape(CONV_WIDTH, 3 * GDN_WIDTH), (0, chip * CONV_SHARD[1]), CONV_SHARD)
    grads["gdn_conv_w"] = conv_grad.reshape((1,) + CONV_SHARD)
    loss = red_small[-1][0]

    by_chip = [g_big[n] for n in big_names]
    landed = swap_grad_halves(by_chip)
    chip_part = [add_grad_halves(g, l, core, name="add_halves_" + n) for n, g, l in zip(big_names, by_chip, landed)]
    order = jnp.stack([chip, chip ^ 2, chip ^ 1, chip ^ 3]).astype(jnp.int32)
    mine = [sum_grads(p, l, order, name="sum_chips_" + n) for n, p, l in zip(big_names, chip_part, scatter_grads(chip_part))]
    theirs = swap_reduced_halves(mine)

    delta, new_m, new_v = {}, {}, {}
    for n, a, b in zip(big_names, mine, theirs):
        g, d, nm, nv = adamw_halves(wts[n][0], a, b, mom[n][0], var[n][0], core, name="adamw_" + n)
        grads[n], delta[n], new_m[n], new_v[n] = g[None], d[None], nm[None], nv[None]
    small_names = [n for n, _ in SMALL_DIMS] + ["gdn_conv_w"]
    small_sz = [d for _, d in SMALL_DIMS] + [CONV_SHARD[0] * CONV_SHARD[1]]
    packed4 = [_pack_rows([src[n] for n in small_names], SMALL_ADAM_ROWS) for src in (wts, grads, mom, var)]
    outs = adamw(*packed4, name="adamw_small", tr=SMALL_ADAM_ROWS)
    for dst, buf in zip((delta, new_m, new_v), outs):
        for n, p in zip(small_names, _unpack_rows(buf, small_sz)):
            dst[n] = p.reshape(wts[n].shape)

    return (loss, grad_x.reshape(B, S, D), *[grads[n] for n in WEIGHT_ORDER], *[delta[n] for n in WEIGHT_ORDER],
            *[new_m[n] for n in WEIGHT_ORDER], *[new_v[n] for n in WEIGHT_ORDER])
```

```python
import functools

import jax
import jax.numpy as jnp
import numpy as np
from jax import lax
from jax.experimental import pallas as pl
from jax.experimental.pallas import tpu as pltpu

f32 = jnp.float32
bf16 = jnp.bfloat16
MXU_DTYPE = jnp.bfloat16
WIRE_DTYPE = jnp.bfloat16
INV_PRECISION = lax.Precision.HIGH

D_MODEL = 1024
FOX_HEADS = 8
FOX_HEAD_DIM = 64
FOX_WIDTH = 512
GDN_HEADS = 4
GDN_HEAD_DIM = 128
GDN_WIDTH = 512
CONV_WIDTH = 4
GDN_CHUNK = 64
XATTN_HEADS = 4
XATTN_HEAD_DIM = 128
XATTN_WIDTH = 512
D_FF = 4096
IN_DIM = 3600
EPS = 1e-6
NEG_INF = -1e30
LANES = 128
ADAM_LR = 0.001
ADAM_B1 = 0.9
ADAM_B2 = 0.999
ADAM_EPS = 1e-08
ADAM_WD = 0.01
ADAM_STEP = 10
VMEM_LIMIT = 48 * 1024 * 1024

COL_FOX = 0
COL_GDN = 1536
COL_Z = 3072
COL_SMALL = 3584
IN_ALIGNED = 3840
IN_TILE = 768
SM_F = 0
SM_B = 8
SM_A = 12


def _cparams(*sem):
    return pltpu.CompilerParams(dimension_semantics=sem, vmem_limit_bytes=VMEM_LIMIT)


def _mx(v):
    return v.astype(MXU_DTYPE)


def _dot(a, b, dims, precision=None):
    return lax.dot_general(a, b, (dims, ((), ())), preferred_element_type=f32, precision=precision)


def _dotm(a, b, dims):
    return _dot(_mx(a), _mx(b), dims)


NN = ((1,), (0,))
NT = ((1,), (1,))
TN = ((0,), (0,))


def matmul(a, b, *, name, ta=False, tb=False, b_stacked=False, out_stacked=False, residual=None, relu2_out=False,
           relu2_bwd_aux=None, out_dtype=f32, tm=1024, tn=1024, tk=1024):
    M, K = (a.shape[1], a.shape[0]) if ta else a.shape
    if b_stacked:
        b_cols = b.shape[2]
        N, tk = (b.shape[1], min(tk, b_cols)) if tb else (N_CHIPS * b_cols, tk)
        tn = tn if tb else min(tn, b_cols)
        assert K == (N_CHIPS * b_cols if tb else b.shape[1]), (name, a.shape, b.shape)
    else:
        N = b.shape[0] if tb else b.shape[1]
    if out_stacked:
        tn = min(tn, N // N_CHIPS)
    tm, tn, tk = min(tm, M), min(tn, N), min(tk, K)
    assert M % tm == 0 and N % tn == 0 and K % tk == 0, (name, M, N, K)
    nk = K // tk
    has_res = residual is not None
    has_aux = relu2_bwd_aux is not None

    def body(*refs):
        a_ref, b_ref = refs[0], refs[1]
        pos = 2
        res_ref = aux_ref = None
        if has_res:
            res_ref = refs[pos]
            pos += 1
        if has_aux:
            aux_ref = refs[pos]
            pos += 1
        o_ref = refs[pos]
        pos += 1
        act_ref = None
        if relu2_out:
            act_ref = refs[pos]
            pos += 1
        acc_ref = refs[pos]
        k = pl.program_id(2)

        @pl.when(k == 0)
        def _():
            acc_ref[...] = jnp.zeros_like(acc_ref)

        dims = ((0,) if ta else (1,), (1,) if tb else (0,))
        acc_ref[...] += _dot(_mx(a_ref[...]), _mx(b_ref[...]), dims)

        @pl.when(k == nk - 1)
        def _():
            r = acc_ref[...]
            if has_res:
                r = r + res_ref[...]
            if has_aux:
                r = r * (2.0 * jnp.maximum(aux_ref[...], 0.0))
            o_ref[...] = r.astype(o_ref.dtype)
            if relu2_out:
                act_ref[...] = jnp.square(jnp.maximum(r, 0.0)).astype(act_ref.dtype)

    a_spec = pl.BlockSpec((tk, tm), lambda i, j, k: (k, i)) if ta else pl.BlockSpec((tm, tk), lambda i, j, k: (i, k))
    if b_stacked and tb:
        per = b_cols // tk
        b_spec = pl.BlockSpec((None, tn, tk), lambda i, j, k: (k // per, j, k % per))
    elif b_stacked:
        per = b_cols // tn
        b_spec = pl.BlockSpec((None, tk, tn), lambda i, j, k: (j // per, k, j % per))
    else:
        b_spec = pl.BlockSpec((tn, tk), lambda i, j, k: (j, k)) if tb else pl.BlockSpec((tk, tn), lambda i, j, k: (k, j))
    if out_stacked:
        assert not (has_res or has_aux or relu2_out), name
        per_o = N // N_CHIPS // tn
        o_spec = pl.BlockSpec((None, tm, tn), lambda i, j, k: (j // per_o, i, j % per_o))
        out_full = (N_CHIPS, M, N // N_CHIPS)
    else:
        o_spec = pl.BlockSpec((tm, tn), lambda i, j, k: (i, j))
        out_full = (M, N)
    in_specs, args = [a_spec, b_spec], [a, b]
    if has_res:
        in_specs.append(o_spec)
        args.append(residual)
    if has_aux:
        in_specs.append(o_spec)
        args.append(relu2_bwd_aux)
    out_shape = [jax.ShapeDtypeStruct(out_full, out_dtype)]
    out_specs = [o_spec]
    if relu2_out:
        out_shape.append(jax.ShapeDtypeStruct((M, N), MXU_DTYPE))
        out_specs.append(o_spec)
    res = pl.pallas_call(
        body, name=name, grid=(M // tm, N // tn, nk), in_specs=in_specs, out_specs=out_specs, out_shape=out_shape,
        scratch_shapes=[pltpu.VMEM((tm, tn), f32)],
        compiler_params=_cparams("parallel", "parallel", "arbitrary"),
    )(*args)
    return res if relu2_out else res[0]


def rms_fwd(x, g, *, name, tr=512):
    R, D = x.shape
    tr = min(tr, R)

    def body(x_ref, g_ref, o_ref):
        xv = x_ref[...]
        y = xv * lax.rsqrt(jnp.mean(xv * xv, axis=-1, keepdims=True) + EPS)
        o_ref[...] = (y * g_ref[...]).astype(o_ref.dtype)

    return pl.pallas_call(
        body, name=name, grid=(R // tr,),
        in_specs=[pl.BlockSpec((tr, D), lambda i: (i, 0)), pl.BlockSpec((1, D), lambda i: (0, 0))],
        out_specs=pl.BlockSpec((tr, D), lambda i: (i, 0)),
        out_shape=jax.ShapeDtypeStruct((R, D), MXU_DTYPE),
        compiler_params=_cparams("parallel"),
    )(x, g)


def rms_bwd(x, g, dh, residual, *, name, tr=512):
    R, D = x.shape
    tr = min(tr, R)
    has_res = residual is not None

    def body(*refs):
        if has_res:
            x_ref, g_ref, dh_ref, res_ref, dx_ref, dg_ref = refs
        else:
            x_ref, g_ref, dh_ref, dx_ref, dg_ref = refs
        xv = x_ref[...]
        rstd = lax.rsqrt(jnp.mean(xv * xv, axis=-1, keepdims=True) + EPS)
        xhat = xv * rstd
        dh = dh_ref[...].astype(f32)
        gd = dh * g_ref[...]
        dx = rstd * (gd - xhat * jnp.mean(gd * xhat, axis=-1, keepdims=True))
        if has_res:
            dx = dx + res_ref[...]
        dx_ref[...] = dx

        @pl.when(pl.program_id(0) == 0)
        def _():
            dg_ref[...] = jnp.zeros_like(dg_ref)

        dg_ref[...] += jnp.sum(dh * xhat, axis=0, keepdims=True)

    row = pl.BlockSpec((tr, D), lambda i: (i, 0))
    vec = pl.BlockSpec((1, D), lambda i: (0, 0))
    in_specs = [row, vec, row] + ([row] if has_res else [])
    args = [x, g, dh] + ([residual] if has_res else [])
    return pl.pallas_call(
        body, name=name, grid=(R // tr,), in_specs=in_specs, out_specs=[row, vec],
        out_shape=[jax.ShapeDtypeStruct((R, D), f32), jax.ShapeDtypeStruct((1, D), f32)],
        compiler_params=_cparams("arbitrary"),
    )(*args)


def loss_head(y, target, *, tr=512):
    R, D = y.shape
    tr = min(tr, R)

    def body(y_ref, t_ref, dy_ref, loss_ref):
        e = y_ref[...] - t_ref[...]
        dy_ref[...] = e * (1.0 / D)

        @pl.when(pl.program_id(0) == 0)
        def _():
            loss_ref[...] = jnp.zeros_like(loss_ref)

        part = 0.5 * jnp.sum(jnp.mean(e * e, axis=-1, keepdims=True), axis=0, keepdims=True)
        loss_ref[...] += jnp.broadcast_to(part, loss_ref.shape)

    row = pl.BlockSpec((tr, D), lambda i: (i, 0))
    return pl.pallas_call(
        body, name="loss_head", grid=(R // tr,), in_specs=[row, row],
        out_specs=[row, pl.BlockSpec((1, LANES), lambda i: (0, 0))],
        out_shape=[jax.ShapeDtypeStruct((R, D), f32), jax.ShapeDtypeStruct((1, LANES), f32)],
        compiler_params=_cparams("arbitrary"),
    )(y, target)


def _head_rms(v, g):
    r = lax.rsqrt(jnp.mean(v * v, axis=-1, keepdims=True) + EPS)
    return v * r * g, r


def _head_rms_bwd(v, r, g, dn):
    vhat = v * r
    gd = dn * g
    dv = r * (gd - vhat * jnp.mean(gd * vhat, axis=-1, keepdims=True))
    return dv, jnp.sum(dn * vhat, axis=0, keepdims=True)


def _softmax_rows(s):
    m = jnp.max(s, axis=-1, keepdims=True)
    e = jnp.exp(s - m)
    return e / jnp.sum(e, axis=-1, keepdims=True)


def xattn_fwd(cq, ckv, gq, gk, *, B, tq=512):
    T = cq.shape[0]
    S = T // B
    M = ckv.shape[0] // B
    tq = min(tq, S)
    nq = S // tq
    scale = XATTN_HEAD_DIM ** -0.5

    def body(q_ref, k_ref, v_ref, gq_ref, gk_ref, o_ref):
        qn, _ = _head_rms(q_ref[...], gq_ref[...])
        kn, _ = _head_rms(k_ref[...], gk_ref[...])
        p = _softmax_rows(_dot(_mx(qn), _mx(kn), NT) * scale)
        o_ref[...] = _dot(_mx(p), _mx(v_ref[...]), NN).astype(o_ref.dtype)

    hd = XATTN_HEAD_DIM
    vec = pl.BlockSpec((1, hd), lambda b, h, i: (0, 0))
    return pl.pallas_call(
        body, name="xattn_fwd", grid=(B, XATTN_HEADS, nq),
        in_specs=[pl.BlockSpec((tq, hd), lambda b, h, i: (b * nq + i, h)),
                  pl.BlockSpec((M, hd), lambda b, h, i: (b, h)),
                  pl.BlockSpec((M, hd), lambda b, h, i: (b, XATTN_HEADS + h)), vec, vec],
        out_specs=pl.BlockSpec((tq, hd), lambda b, h, i: (b * nq + i, h)),
        out_shape=jax.ShapeDtypeStruct((T, XATTN_WIDTH), MXU_DTYPE),
        compiler_params=_cparams("parallel", "parallel", "parallel"),
    )(cq, ckv, ckv, gq, gk)


def xattn_bwd(cq, ckv, gq, gk, dco, *, B, tq=512):
    T = cq.shape[0]
    S = T // B
    M = ckv.shape[0] // B
    tq = min(tq, S)
    nq = S // tq
    scale = XATTN_HEAD_DIM ** -0.5
    hd = XATTN_HEAD_DIM

    def body(q_ref, k_ref, v_ref, gq_ref, gk_ref, do_ref, dq_ref, dk_ref, dv_ref, dgq_ref, dgk_ref, dkn_acc, dv_acc):
        b, h, i = pl.program_id(0), pl.program_id(1), pl.program_id(2)

        @pl.when((b == 0) & (h == 0) & (i == 0))
        def _():
            dgq_ref[...] = jnp.zeros_like(dgq_ref)
            dgk_ref[...] = jnp.zeros_like(dgk_ref)

        @pl.when(i == 0)
        def _():
            dkn_acc[...] = jnp.zeros_like(dkn_acc)
            dv_acc[...] = jnp.zeros_like(dv_acc)

        q, k, v = q_ref[...], k_ref[...], v_ref[...]
        gqv, gkv = gq_ref[...], gk_ref[...]
        qn, rq = _head_rms(q, gqv)
        kn, rk = _head_rms(k, gkv)
        p = _softmax_rows(_dot(_mx(qn), _mx(kn), NT) * scale)
        do = do_ref[...]
        dv_acc[...] += _dot(_mx(p), _mx(do), TN)
        dp = _dot(_mx(do), _mx(v), NT)
        ds = p * (dp - jnp.sum(dp * p, axis=-1, keepdims=True)) * scale
        dqn = _dot(_mx(ds), _mx(kn), NN)
        dkn_acc[...] += _dot(_mx(ds), _mx(qn), TN)
        dq, dgq = _head_rms_bwd(q, rq, gqv, dqn)
        dq_ref[...] = dq.astype(dq_ref.dtype)
        dgq_ref[...] += dgq

        @pl.when(i == nq - 1)
        def _():
            dk, dgk = _head_rms_bwd(k, rk, gkv, dkn_acc[...])
            dk_ref[...] = dk.astype(dk_ref.dtype)
            dv_ref[...] = dv_acc[...].astype(dv_ref.dtype)
            dgk_ref[...] += dgk

    vec = pl.BlockSpec((1, hd), lambda b, h, i: (0, 0))
    qspec = pl.BlockSpec((tq, hd), lambda b, h, i: (b * nq + i, h))
    kspec = pl.BlockSpec((M, hd), lambda b, h, i: (b, h))
    vspec = pl.BlockSpec((M, hd), lambda b, h, i: (b, XATTN_HEADS + h))
    dq, dk, dv, dgq, dgk = pl.pallas_call(
        body, name="xattn_bwd", grid=(B, XATTN_HEADS, nq),
        in_specs=[qspec, kspec, vspec, vec, vec, qspec],
        out_specs=[qspec, kspec, kspec, vec, vec],
        out_shape=[jax.ShapeDtypeStruct((T, XATTN_WIDTH), MXU_DTYPE),
                   jax.ShapeDtypeStruct((B * M, XATTN_WIDTH), MXU_DTYPE),
                   jax.ShapeDtypeStruct((B * M, XATTN_WIDTH), MXU_DTYPE),
                   jax.ShapeDtypeStruct((1, hd), f32), jax.ShapeDtypeStruct((1, hd), f32)],
        scratch_shapes=[pltpu.VMEM((M, hd), f32), pltpu.VMEM((M, hd), f32)],
        compiler_params=_cparams("arbitrary", "arbitrary", "arbitrary"),
    )(cq, ckv, ckv, gq, gk, dco)
    return dq, jnp.concatenate([dk, dv], axis=1), dgq, dgk


FOX_PAIRS = FOX_HEADS // 2


def _fox_scores(qn, kn, ccol, crow, q0, tq, S, scale):
    s = _dot(_mx(qn), _mx(kn), NT) * scale + ccol - crow
    qpos = q0 + lax.broadcasted_iota(jnp.int32, (tq, S), 0)
    kpos = lax.broadcasted_iota(jnp.int32, (tq, S), 1)
    return jnp.where(kpos <= qpos, s, NEG_INF)


def fox_fwd(P, ccol, crow, gq, gk, go, *, B, tq=256):
    T = P.shape[0]
    S = T // B
    tq = min(tq, S)
    nq = S // tq
    hd = FOX_HEAD_DIM
    scale = hd ** -0.5

    def body(q_ref, k_ref, v_ref, ccol_ref, crow_ref, gq_ref, gk_ref, go_ref, o_ref, oa_ref):
        q0 = pl.program_id(2) * tq
        for e in range(2):
            sl = slice(e * hd, (e + 1) * hd)
            qn, _ = _head_rms(q_ref[:, sl], gq_ref[:, sl])
            kn, _ = _head_rms(k_ref[:, sl], gk_ref[:, sl])
            p = _softmax_rows(_fox_scores(qn, kn, ccol_ref[0, e], crow_ref[0, e], q0, tq, S, scale))
            o = _dot(_mx(p), _mx(v_ref[:, sl]), NN)
            o_ref[:, sl] = o
            oa_ref[:, sl] = _head_rms(o, go_ref[:, sl])[0].astype(oa_ref.dtype)

    W = 2 * hd
    vec = pl.BlockSpec((1, W), lambda b, h, i: (0, 0))
    ospec = pl.BlockSpec((tq, W), lambda b, h, i: (b * nq + i, h))
    return pl.pallas_call(
        body, name="fox_fwd", grid=(B, FOX_PAIRS, nq),
        in_specs=[pl.BlockSpec((tq, W), lambda b, h, i: (b * nq + i, h)),
                  pl.BlockSpec((S, W), lambda b, h, i: (b, FOX_PAIRS + h)),
                  pl.BlockSpec((S, W), lambda b, h, i: (b, 2 * FOX_PAIRS + h)),
                  pl.BlockSpec((1, 2, tq, 1), lambda b, h, i: (b, h, i, 0)),
                  pl.BlockSpec((1, 2, 1, S), lambda b, h, i: (b, h, 0, 0)), vec, vec, vec],
        out_specs=[ospec, ospec],
        out_shape=[jax.ShapeDtypeStruct((T, FOX_WIDTH), f32), jax.ShapeDtypeStruct((T, FOX_WIDTH), MXU_DTYPE)],
        compiler_params=_cparams("parallel", "parallel", "parallel"),
    )(P, P, P, ccol, crow, gq, gk, go)


def fox_bwd(P, ccol, crow, gq, gk, go, o_raw, d_oab, *, B, tq=256):
    T = P.shape[0]
    S = T // B
    tq = min(tq, S)
    nq = S // tq
    hd = FOX_HEAD_DIM
    scale = hd ** -0.5

    def body(q_ref, k_ref, v_ref, ccol_ref, crow_ref, gq_ref, gk_ref, go_ref, o_ref, doa_ref,
             dq_ref, dk_ref, dv_ref, dccol_ref, dcrow_ref, dgq_ref, dgk_ref, dgo_ref, dkn_acc, dv_acc, dcrow_acc):
        b, h, i = pl.program_id(0), pl.program_id(1), pl.program_id(2)
        q0 = i * tq

        @pl.when((b == 0) & (h == 0) & (i == 0))
        def _():
            dgq_ref[...] = jnp.zeros_like(dgq_ref)
            dgk_ref[...] = jnp.zeros_like(dgk_ref)
            dgo_ref[...] = jnp.zeros_like(dgo_ref)

        @pl.when(i == 0)
        def _():
            dkn_acc[...] = jnp.zeros_like(dkn_acc)
            dv_acc[...] = jnp.zeros_like(dv_acc)
            dcrow_acc[...] = jnp.zeros_like(dcrow_acc)

        for e in range(2):
            sl = slice(e * hd, (e + 1) * hd)
            q, k, v = q_ref[:, sl], k_ref[:, sl], v_ref[:, sl]
            gqv, gkv, gov = gq_ref[:, sl], gk_ref[:, sl], go_ref[:, sl]
            qn, rq = _head_rms(q, gqv)
            kn, rk = _head_rms(k, gkv)
            p = _softmax_rows(_fox_scores(qn, kn, ccol_ref[0, e], crow_ref[0, e], q0, tq, S, scale))
            o = o_ref[:, sl]
            ro = lax.rsqrt(jnp.mean(o * o, axis=-1, keepdims=True) + EPS)
            do, dgo = _head_rms_bwd(o, ro, gov, doa_ref[:, sl])
            dgo_ref[:, sl] += dgo
            dv_acc[e] += _dot(_mx(p), _mx(do), TN)
            dp = _dot(_mx(do), _mx(v), NT)
            ds = p * (dp - jnp.sum(do * o, axis=-1, keepdims=True))
            dccol_ref[0, e] = jnp.sum(ds, axis=1, keepdims=True)
            dcrow_acc[e] -= jnp.sum(ds, axis=0, keepdims=True)
            dqn = _dot(_mx(ds), _mx(kn), NN) * scale
            dkn_acc[e] += _dot(_mx(ds), _mx(qn), TN) * scale
            dq, dgq = _head_rms_bwd(q, rq, gqv, dqn)
            dq_ref[:, sl] = dq.astype(dq_ref.dtype)
            dgq_ref[:, sl] += dgq

        @pl.when(i == nq - 1)
        def _():
            for e in range(2):
                sl = slice(e * hd, (e + 1) * hd)
                k = k_ref[:, sl]
                gkv = gk_ref[:, sl]
                rk = lax.rsqrt(jnp.mean(k * k, axis=-1, keepdims=True) + EPS)
                dk, dgk = _head_rms_bwd(k, rk, gkv, dkn_acc[e])
                dk_ref[:, sl] = dk.astype(dk_ref.dtype)
                dv_ref[:, sl] = dv_acc[e].astype(dv_ref.dtype)
                dgk_ref[:, sl] += dgk
                dcrow_ref[0, e] = dcrow_acc[e]

    W = 2 * hd
    vec = pl.BlockSpec((1, W), lambda b, h, i: (0, 0))
    qspec = pl.BlockSpec((tq, W), lambda b, h, i: (b * nq + i, h))
    kvout = pl.BlockSpec((S, W), lambda b, h, i: (b, h))
    colspec = pl.BlockSpec((1, 2, tq, 1), lambda b, h, i: (b, h, i, 0))
    rowspec = pl.BlockSpec((1, 2, 1, S), lambda b, h, i: (b, h, 0, 0))
    return pl.pallas_call(
        body, name="fox_bwd", grid=(B, FOX_PAIRS, nq),
        in_specs=[qspec,
                  pl.BlockSpec((S, W), lambda b, h, i: (b, FOX_PAIRS + h)),
                  pl.BlockSpec((S, W), lambda b, h, i: (b, 2 * FOX_PAIRS + h)),
                  colspec, rowspec, vec, vec, vec, qspec, qspec],
        out_specs=[qspec, kvout, kvout, colspec, rowspec, vec, vec, vec],
        out_shape=[jax.ShapeDtypeStruct((T, FOX_WIDTH), MXU_DTYPE), jax.ShapeDtypeStruct((T, FOX_WIDTH), MXU_DTYPE),
                   jax.ShapeDtypeStruct((T, FOX_WIDTH), MXU_DTYPE),
                   jax.ShapeDtypeStruct((B, FOX_HEADS, S, 1), f32), jax.ShapeDtypeStruct((B, FOX_HEADS, 1, S), f32),
                   jax.ShapeDtypeStruct((1, W), f32), jax.ShapeDtypeStruct((1, W), f32), jax.ShapeDtypeStruct((1, W), f32)],
        scratch_shapes=[pltpu.VMEM((2, S, hd), f32), pltpu.VMEM((2, S, hd), f32), pltpu.VMEM((2, 1, S), f32)],
        compiler_params=_cparams("arbitrary", "arbitrary", "arbitrary"),
    )(P, P, P, ccol, crow, gq, gk, go, o_raw, d_oab)


FOX_TQ = 512
FOX_TK = 512
GROUP_PRECISION = lax.Precision.HIGH


def _head_mean(v):
    n = v.shape[1]
    r = lax.broadcasted_iota(jnp.int32, (n, n), 0) // FOX_HEAD_DIM
    c = lax.broadcasted_iota(jnp.int32, (n, n), 1) // FOX_HEAD_DIM
    return _dot(v, (r == c).astype(f32), NN, GROUP_PRECISION) * (1.0 / FOX_HEAD_DIM)


def fox_prep_fwd(P, gq, gk, *, tr=512):
    T = P.shape[0]
    tr = min(tr, T)
    scale = FOX_HEAD_DIM ** -0.5

    def body(q_ref, k_ref, v_ref, gq_ref, gk_ref, qn_ref, kn_ref, vb_ref):
        q, k = q_ref[...], k_ref[...]
        qn_ref[...] = (q * lax.rsqrt(_head_mean(q * q) + EPS) * (gq_ref[...] * scale)).astype(qn_ref.dtype)
        kn_ref[...] = (k * lax.rsqrt(_head_mean(k * k) + EPS) * gk_ref[...]).astype(kn_ref.dtype)
        vb_ref[...] = v_ref[...].astype(vb_ref.dtype)

    W = FOX_WIDTH
    col = lambda j: pl.BlockSpec((tr, W), lambda i: (i, j))
    vec = pl.BlockSpec((1, W), lambda i: (0, 0))
    out = jax.ShapeDtypeStruct((T, W), MXU_DTYPE)
    return pl.pallas_call(
        body, name="fox_prep_fwd", grid=(T // tr,), in_specs=[col(0), col(1), col(2), vec, vec],
        out_specs=[col(0)] * 3, out_shape=[out] * 3, compiler_params=_cparams("parallel"),
    )(P, P, P, gq, gk)


def fox_prep_bwd(P, gq, gk, dqn, dkn, *, tr=512):
    T = P.shape[0]
    tr = min(tr, T)
    scale = FOX_HEAD_DIM ** -0.5

    def body(q_ref, k_ref, gq_ref, gk_ref, dqn_ref, dkn_ref, dq_ref, dk_ref, dgq_ref, dgk_ref):
        @pl.when(pl.program_id(0) == 0)
        def _():
            dgq_ref[...] = jnp.zeros_like(dgq_ref)
            dgk_ref[...] = jnp.zeros_like(dgk_ref)

        def one(x, g, dn, dx_ref, dg_ref):
            r = lax.rsqrt(_head_mean(x * x) + EPS)
            xhat = x * r
            gd = dn * g
            dx_ref[...] = (r * (gd - xhat * _head_mean(gd * xhat))).astype(dx_ref.dtype)
            return jnp.sum(dn * xhat, axis=0, keepdims=True)

        dgq_ref[...] += scale * one(q_ref[...], gq_ref[...] * scale, dqn_ref[...], dq_ref, dgq_ref)
        dgk_ref[...] += one(k_ref[...], gk_ref[...], dkn_ref[...], dk_ref, dgk_ref)

    W = FOX_WIDTH
    col = lambda j: pl.BlockSpec((tr, W), lambda i: (i, j))
    vec = pl.BlockSpec((1, W), lambda i: (0, 0))
    return pl.pallas_call(
        body, name="fox_prep_bwd", grid=(T // tr,), in_specs=[col(0), col(1), vec, vec, col(0), col(0)],
        out_specs=[col(0), col(0), vec, vec],
        out_shape=[jax.ShapeDtypeStruct((T, W), MXU_DTYPE), jax.ShapeDtypeStruct((T, W), MXU_DTYPE),
                   jax.ShapeDtypeStruct((1, W), f32), jax.ShapeDtypeStruct((1, W), f32)],
        compiler_params=_cparams("arbitrary"),
    )(P, P, gq, gk, dqn, dkn)


def _fox_tile_scores(q, k_ref, ccol_ref, cq, e, j, sl, mask_off):
    tq, tk = FOX_TQ, FOX_TK
    rows = pl.ds(pl.multiple_of(j * tk, tk), tk)
    k = k_ref[rows, sl]
    s = _dot(k, q, NT) + cq - ccol_ref[0, e, rows, :]
    if mask_off is not None:
        key = lax.broadcasted_iota(jnp.int32, (tk, tq), 0) + mask_off
        query = lax.broadcasted_iota(jnp.int32, (tk, tq), 1)
        s = jnp.where(key <= query, s, NEG_INF)
    return s, k, rows


def _fox_sweep(i, update, carry):
    nd = FOX_TQ // FOX_TK
    carry = lax.fori_loop(0, i * nd, lambda j, cr: update(cr, j, None), carry)
    for d in range(nd):
        carry = update(carry, i * nd + d, d * FOX_TK)
    return carry


def fox_core_fwd(qn, kn, vb, ccol, crow, go, *, B):
    T = qn.shape[0]
    S = T // B
    tq = FOX_TQ
    nq = S // tq
    hd = FOX_HEAD_DIM

    def body(q_ref, k_ref, v_ref, ccol_ref, crow_ref, go_ref, o_ref, oa_ref, lse_ref):
        i = pl.program_id(2)
        for e in range(2):
            sl = slice(e * hd, (e + 1) * hd)
            q = q_ref[:, sl]
            cq = crow_ref[0, e, i]

            def update(carry, j, mask_off):
                m, l, acc = carry
                s, _, rows = _fox_tile_scores(q, k_ref, ccol_ref, cq, e, j, sl, mask_off)
                m2 = jnp.maximum(m, jnp.max(s, axis=0, keepdims=True))
                a = jnp.exp(m - m2)
                p = jnp.exp(s - m2)
                return m2, a * l + jnp.sum(p, axis=0, keepdims=True), a * acc + _dot(v_ref[rows, sl], _mx(p), TN)

            carry = (jnp.full((1, tq), NEG_INF, f32), jnp.zeros((1, tq), f32), jnp.zeros((hd, tq), f32))
            m, l, acc = _fox_sweep(i, update, carry)
            o = (acc / l).T
            o_ref[:, sl] = o
            oa_ref[:, sl] = _head_rms(o, go_ref[:, sl])[0].astype(oa_ref.dtype)
            lse_ref[0, e, 0] = m + jnp.log(l)

    W = 2 * hd
    qspec = pl.BlockSpec((tq, W), lambda b, h, i: (b * nq + i, h))
    kspec = pl.BlockSpec((S, W), lambda b, h, i: (b, h))
    return pl.pallas_call(
        body, name="fox_core_fwd", grid=(B, FOX_PAIRS, nq),
        in_specs=[qspec, kspec, kspec, pl.BlockSpec((1, 2, S, 1), lambda b, h, i: (b, h, 0, 0)),
                  pl.BlockSpec((1, 2, nq, 1, tq), lambda b, h, i: (b, h, 0, 0, 0)),
                  pl.BlockSpec((1, W), lambda b, h, i: (0, 0))],
        out_specs=[qspec, qspec, pl.BlockSpec((1, 2, 1, 1, tq), lambda b, h, i: (b, h, i, 0, 0))],
        out_shape=[jax.ShapeDtypeStruct((T, FOX_WIDTH), f32), jax.ShapeDtypeStruct((T, FOX_WIDTH), MXU_DTYPE),
                   jax.ShapeDtypeStruct((B, FOX_HEADS, nq, 1, tq), f32)],
        compiler_params=_cparams("parallel", "parallel", "parallel"),
    )(qn, kn, vb, ccol, crow, go)


def fox_core_bwd(qn, kn, vb, ccol, crow, go, o_raw, lse, d_oab, *, B):
    T = qn.shape[0]
    S = T // B
    tq = FOX_TQ
    nq = S // tq
    hd = FOX_HEAD_DIM

    def body(q_ref, k_ref, v_ref, ccol_ref, crow_ref, go_ref, o_ref, lse_ref, doa_ref,
             dq_ref, dk_ref, dv_ref, dccol_ref, dcrow_ref, dgo_ref, dk_acc, dv_acc, dck_acc):
        b, h, i = pl.program_id(0), pl.program_id(1), pl.program_id(2)

        @pl.when((b == 0) & (h == 0) & (i == 0))
        def _():
            dgo_ref[...] = jnp.zeros_like(dgo_ref)

        @pl.when(i == 0)
        def _():
            dk_acc[...] = jnp.zeros_like(dk_acc)
            dv_acc[...] = jnp.zeros_like(dv_acc)
            dck_acc[...] = jnp.zeros_like(dck_acc)

        for e in range(2):
            sl = slice(e * hd, (e + 1) * hd)
            q = q_ref[:, sl]
            cq = crow_ref[0, e, i]
            lse_e = lse_ref[0, e, 0]
            o = o_ref[:, sl]
            ro = lax.rsqrt(jnp.mean(o * o, axis=-1, keepdims=True) + EPS)
            do, dgo = _head_rms_bwd(o, ro, go_ref[:, sl], doa_ref[:, sl])
            dgo_ref[:, sl] += dgo
            delta = jnp.sum((do * o).T, axis=0, keepdims=True)
            do_b = _mx(do)

            def update(carry, j, mask_off):
                dq, dcq = carry
                s, k, rows = _fox_tile_scores(q, k_ref, ccol_ref, cq, e, j, sl, mask_off)
                p = jnp.exp(s - lse_e)
                dv_acc[e, rows, :] += _dot(_mx(p), do_b, NN)
                ds = p * (_dot(v_ref[rows, sl], do_b, NT) - delta)
                dck_acc[e, rows, :] -= jnp.sum(ds, axis=1, keepdims=True)
                ds_b = _mx(ds)
                dk_acc[e, rows, :] += _dot(ds_b, q, NN)
                return dq + _dot(ds_b, k, TN), dcq + jnp.sum(ds, axis=0, keepdims=True)

            dq, dcq = _fox_sweep(i, update, (jnp.zeros((tq, hd), f32), jnp.zeros((1, tq), f32)))
            dq_ref[:, sl] = dq
            dcrow_ref[0, e, 0] = dcq

        @pl.when(i == nq - 1)
        def _():
            for e in range(2):
                sl = slice(e * hd, (e + 1) * hd)
                dk_ref[:, sl] = dk_acc[e]
                dv_ref[:, sl] = dv_acc[e].astype(dv_ref.dtype)
            dccol_ref[0] = dck_acc[...]

    W = 2 * hd
    qspec = pl.BlockSpec((tq, W), lambda b, h, i: (b * nq + i, h))
    kspec = pl.BlockSpec((S, W), lambda b, h, i: (b, h))
    colspec = pl.BlockSpec((1, 2, S, 1), lambda b, h, i: (b, h, 0, 0))
    rowspec = pl.BlockSpec((1, 2, nq, 1, tq), lambda b, h, i: (b, h, 0, 0, 0))
    tilespec = pl.BlockSpec((1, 2, 1, 1, tq), lambda b, h, i: (b, h, i, 0, 0))
    vec = pl.BlockSpec((1, W), lambda b, h, i: (0, 0))
    return pl.pallas_call(
        body, name="fox_core_bwd", grid=(B, FOX_PAIRS, nq),
        in_specs=[qspec, kspec, kspec, colspec, rowspec, vec, qspec, tilespec, qspec],
        out_specs=[qspec, kspec, kspec, colspec, tilespec, vec],
        out_shape=[jax.ShapeDtypeStruct((T, FOX_WIDTH), f32), jax.ShapeDtypeStruct((T, FOX_WIDTH), f32),
                   jax.ShapeDtypeStruct((T, FOX_WIDTH), MXU_DTYPE),
                   jax.ShapeDtypeStruct((B, FOX_HEADS, S, 1), f32), jax.ShapeDtypeStruct((B, FOX_HEADS, nq, 1, tq), f32),
                   jax.ShapeDtypeStruct((1, W), f32)],
        scratch_shapes=[pltpu.VMEM((2, S, hd), f32), pltpu.VMEM((2, S, hd), f32), pltpu.VMEM((2, S, 1), f32)],
        compiler_params=_cparams("arbitrary", "arbitrary", "arbitrary"),
    )(qn, kn, vb, ccol, crow, go, o_raw, lse, d_oab)


def _lane_mask(lo, hi, shape):
    lane = lax.broadcasted_iota(jnp.int32, shape, 1)
    return (lane >= lo) & (lane < hi)


def _cumsum_rows(v, period, reverse=False):
    n = v.shape[0]
    pos = lax.broadcasted_iota(jnp.int32, v.shape, 0) % period
    sh = 1
    while sh < period:
        if reverse:
            v = v + jnp.where(pos + sh < period, pltpu.roll(v, n - sh, 0), 0.0)
        else:
            v = v + jnp.where(pos >= sh, pltpu.roll(v, sh, 0), 0.0)
        sh *= 2
    return v


def _gate_values(z, bias, alog):
    zb = z + bias
    ls = jax.nn.log_sigmoid(zb)
    beta = jax.nn.sigmoid(z)
    g = -jnp.exp(alog) * jax.nn.softplus(zb)
    return zb, ls, beta, g


def gates_fwd(P, bias, alog, *, B):
    T = P.shape[0]
    S = T // B

    def body(z_ref, bias_ref, alog_ref, o_ref):
        z = z_ref[...]
        _, ls, beta, g = _gate_values(z, bias_ref[...], alog_ref[...])
        c = _cumsum_rows(ls, S)
        gc = _cumsum_rows(g, GDN_CHUNK)
        o = jnp.where(_lane_mask(SM_F, SM_F + FOX_HEADS, z.shape), c, 0.0)
        o = jnp.where(_lane_mask(SM_B, SM_B + GDN_HEADS, z.shape), beta, o)
        o = jnp.where(_lane_mask(SM_A, SM_A + GDN_HEADS, z.shape), gc, o)
        o_ref[...] = o

    vec = pl.BlockSpec((1, LANES), lambda b: (0, 0))
    return pl.pallas_call(
        body, name="gates_fwd", grid=(B,),
        in_specs=[pl.BlockSpec((S, LANES), lambda b: (b, COL_SMALL // LANES)), vec, vec],
        out_specs=pl.BlockSpec((S, LANES), lambda b: (b, 0)),
        out_shape=jax.ShapeDtypeStruct((T, LANES), f32),
        compiler_params=_cparams("parallel"),
    )(P, bias, alog)


def gates_bwd(P, bias, alog, dgates, *, B):
    T = P.shape[0]
    S = T // B

    def body(z_ref, bias_ref, alog_ref, dg_ref, dz_ref, par_ref):
        z = z_ref[...]
        zb, ls, beta, g = _gate_values(z, bias_ref[...], alog_ref[...])
        d = dg_ref[...]
        dls = _cumsum_rows(d, S, reverse=True)
        dgr = _cumsum_rows(d, GDN_CHUNK, reverse=True)
        sig = jax.nn.sigmoid(zb)
        dz_f = dls * (1.0 - sig)
        dz_b = d * beta * (1.0 - beta)
        dz_a = dgr * (-jnp.exp(alog_ref[...])) * sig
        dz = jnp.where(_lane_mask(SM_F, SM_F + FOX_HEADS, z.shape), dz_f, 0.0)
        dz = jnp.where(_lane_mask(SM_B, SM_B + GDN_HEADS, z.shape), dz_b, dz)
        dz = jnp.where(_lane_mask(SM_A, SM_A + GDN_HEADS, z.shape), dz_a, dz)
        dz_ref[...] = dz.astype(dz_ref.dtype)

        @pl.when(pl.program_id(0) == 0)
        def _():
            par_ref[...] = jnp.zeros_like(par_ref)

        dalog = jnp.where(_lane_mask(SM_A, SM_A + GDN_HEADS, z.shape), dgr * g, 0.0)
        par_ref[0:1, :] += jnp.sum(dz, axis=0, keepdims=True)
        par_ref[1:2, :] += jnp.sum(dalog, axis=0, keepdims=True)

    vec = pl.BlockSpec((1, LANES), lambda b: (0, 0))
    return pl.pallas_call(
        body, name="gates_bwd", grid=(B,),
        in_specs=[pl.BlockSpec((S, LANES), lambda b: (b, COL_SMALL // LANES)), vec, vec,
                  pl.BlockSpec((S, LANES), lambda b: (b, 0))],
        out_specs=[pl.BlockSpec((S, LANES), lambda b: (b, 0)), pl.BlockSpec((8, LANES), lambda b: (0, 0))],
        out_shape=[jax.ShapeDtypeStruct((T, LANES), MXU_DTYPE), jax.ShapeDtypeStruct((8, LANES), f32)],
        compiler_params=_cparams("arbitrary"),
    )(P, bias, alog, dgates)


GDN_BLOCKS = 3 * GDN_HEADS


def _shift_rows(v, d, reverse=False):
    if d == 0:
        return v
    n = v.shape[0]
    row = lax.broadcasted_iota(jnp.int32, v.shape, 0)
    if reverse:
        return jnp.where(row + d < n, pltpu.roll(v, n - d, 0), 0.0)
    return jnp.where(row >= d, pltpu.roll(v, d, 0), 0.0)


def _conv_silu(x, w):
    pre = sum(w[j:j + 1, :] * _shift_rows(x, CONV_WIDTH - 1 - j) for j in range(CONV_WIDTH))
    return pre, pre * jax.nn.sigmoid(pre)


def gdn_prep_fwd(P, conv_w, *, B):
    T = P.shape[0]
    S = T // B

    def body(x_ref, w_ref, o_ref):
        _, y = _conv_silu(x_ref[...], w_ref[...])
        yn = y * lax.rsqrt(jnp.sum(y * y, axis=-1, keepdims=True) + EPS)
        o_ref[...] = jnp.where(pl.program_id(1) < 2 * GDN_HEADS, yn, y)

    return pl.pallas_call(
        body, name="gdn_prep_fwd", grid=(B, GDN_BLOCKS),
        in_specs=[pl.BlockSpec((S, LANES), lambda b, j: (b, COL_GDN // LANES + j)),
                  pl.BlockSpec((CONV_WIDTH, LANES), lambda b, j: (0, j))],
        out_specs=pl.BlockSpec((S, LANES), lambda b, j: (b, j)),
        out_shape=jax.ShapeDtypeStruct((T, 3 * GDN_WIDTH), f32),
        compiler_params=_cparams("parallel", "parallel"),
    )(P, conv_w)


def gdn_prep_bwd(P, conv_w, dG, *, B):
    T = P.shape[0]
    S = T // B

    def body(x_ref, w_ref, dg_ref, dx_ref, dw_ref):
        x, w = x_ref[...], w_ref[...]
        pre, y = _conv_silu(x, w)
        dn = dg_ref[...]
        r = lax.rsqrt(jnp.sum(y * y, axis=-1, keepdims=True) + EPS)
        n = y * r
        dy_norm = r * (dn - n * jnp.sum(dn * n, axis=-1, keepdims=True))
        dy = jnp.where(pl.program_id(0) < 2 * GDN_HEADS, dy_norm, dn)
        sg = jax.nn.sigmoid(pre)
        dpre = dy * (sg * (1.0 + pre * (1.0 - sg)))
        dx = sum(w[j:j + 1, :] * _shift_rows(dpre, CONV_WIDTH - 1 - j, reverse=True) for j in range(CONV_WIDTH))
        dx_ref[...] = dx.astype(dx_ref.dtype)

        @pl.when(pl.program_id(1) == 0)
        def _():
            dw_ref[...] = jnp.zeros_like(dw_ref)

        for j in range(CONV_WIDTH):
            dw_ref[j:j + 1, :] += jnp.sum(dpre * _shift_rows(x, CONV_WIDTH - 1 - j), axis=0, keepdims=True)

    return pl.pallas_call(
        body, name="gdn_prep_bwd", grid=(GDN_BLOCKS, B),
        in_specs=[pl.BlockSpec((S, LANES), lambda j, b: (b, COL_GDN // LANES + j)),
                  pl.BlockSpec((CONV_WIDTH, LANES), lambda j, b: (0, j)),
                  pl.BlockSpec((S, LANES), lambda j, b: (b, j))],
        out_specs=[pl.BlockSpec((S, LANES), lambda j, b: (b, j)),
                   pl.BlockSpec((CONV_WIDTH, LANES), lambda j, b: (0, j))],
        out_shape=[jax.ShapeDtypeStruct((T, 3 * GDN_WIDTH), MXU_DTYPE),
                   jax.ShapeDtypeStruct((CONV_WIDTH, 3 * GDN_WIDTH), f32)],
        compiler_params=_cparams("arbitrary", "arbitrary"),
    )(P, conv_w, dG)


GDN_GROUP = 16
B_NN = (((2,), (1,)), ((0,), (0,)))
B_NT = (((2,), (2,)), ((0,), (0,)))
B_TN = (((1,), (1,)), ((0,), (0,)))


def _bmm(a, b, dims, precision=None):
    if precision is None:
        a, b = _mx(a), _mx(b)
    return lax.dot_general(a, b, dims, preferred_element_type=f32, precision=precision)


def _tri_inverse(A):
    C = A.shape[-1]
    row = lax.broadcasted_iota(jnp.int32, A.shape, 1)
    col = lax.broadcasted_iota(jnp.int32, A.shape, 2)
    eye = (row == col).astype(f32)
    X = jnp.where((row // 4) == (col // 4), -A, 0.0)
    X2 = _bmm(X, X, B_NN, INV_PRECISION)
    Tm = eye + X + X2 + _bmm(X, X2, B_NN, INV_PRECISION)
    b = 4
    while b < C:
        off = ((row // (2 * b)) == (col // (2 * b))) & ((row // b) != (col // b))
        Tm = Tm - _bmm(_bmm(Tm, jnp.where(off, A, 0.0), B_NN, INV_PRECISION), Tm, B_NN, INV_PRECISION)
        b *= 2
    return Tm


def _pick_lane(block, lane_idx):
    lane = lax.broadcasted_iota(jnp.int32, block.shape, 1)
    return jnp.sum(jnp.where(lane == lane_idx, block, 0.0), axis=1, keepdims=True)


def _gdn_local(q, k, v, beta, gc, Tm=None):
    C = GDN_CHUNK
    n = q.shape[0] // C
    q = q.reshape(n, C, -1) * (GDN_HEAD_DIM ** -0.5)
    k = k.reshape(n, C, -1)
    v = v.reshape(n, C, -1)
    beta = beta.reshape(n, C, 1)
    gc = gc.reshape(n, C, 1)
    row = lax.broadcasted_iota(jnp.int32, (n, C, C), 1)
    col = lax.broadcasted_iota(jnp.int32, (n, C, C), 2)
    gcT = jnp.swapaxes(jnp.broadcast_to(gc, (n, C, C)), 1, 2)
    D = jnp.exp(jnp.where(row >= col, gc - gcT, NEG_INF))
    kb = k * beta
    vb = v * beta
    A = jnp.where(row > col, _bmm(kb, k, B_NT) * D, 0.0)
    Gam = jnp.exp(gc)
    kg = kb * Gam
    gl = gc[:, C - 1:C, :]
    kdec = jnp.exp(gl - gc)
    loc = dict(q=q, k=k, v=v, beta=beta, gc=gc, D=D, kb=kb, vb=vb, A=A, Gam=Gam, kg=kg,
               kdec=kdec, kd=k * kdec, qg=q * Gam, gam=jnp.exp(gl), row=row, col=col)
    if Tm is None:
        Tm = _tri_inverse(A)
        loc.update(u=_bmm(Tm, vb, B_NN), w=_bmm(Tm, kg, B_NN), M=_bmm(q, k, B_NT) * D)
    else:
        Tm = Tm.reshape(n, C, C)
    loc["Tm"] = Tm
    return loc


def _gdn_store_local(loc, r0, u_s, w_s, qg_s, kd_s, M_s, gam_s, c0):
    n = loc["u"].shape[0]
    R = n * GDN_CHUNK
    u_s[pl.ds(r0, R), :] = loc["u"].reshape(R, -1)
    w_s[pl.ds(r0, R), :] = loc["w"].reshape(R, -1)
    qg_s[pl.ds(r0, R), :] = loc["qg"].reshape(R, -1)
    kd_s[pl.ds(r0, R), :] = loc["kd"].reshape(R, -1)
    M_s[pl.ds(r0, R), :] = loc["M"].reshape(R, -1)
    gam_s[pl.ds(c0, n)] = jnp.broadcast_to(loc["gam"], (n, 1, LANES))


def _gdn_specs(S):
    blk = lambda off: pl.BlockSpec((S, LANES), lambda b, h: (b, off + h))
    return blk


def gdn_fwd(G, gates, P, g_on, *, B):
    T = G.shape[0]
    S = T // B
    C = GDN_CHUNK
    N = S // C
    grp = min(GDN_GROUP, N)
    R = grp * C
    hd = GDN_HEAD_DIM

    def body(q_ref, k_ref, v_ref, gt_ref, z_ref, gon_ref, o_ref, ob_ref, st_ref, u_s, w_s, qg_s, kd_s, M_s, gam_s):
        h = pl.program_id(1)

        def local(gi, carry):
            r0 = pl.multiple_of(gi * R, R)
            gt = gt_ref[pl.ds(r0, R), :]
            loc = _gdn_local(q_ref[pl.ds(r0, R), :], k_ref[pl.ds(r0, R), :], v_ref[pl.ds(r0, R), :],
                             _pick_lane(gt, SM_B + h), _pick_lane(gt, SM_A + h))
            _gdn_store_local(loc, r0, u_s, w_s, qg_s, kd_s, M_s, gam_s, gi * grp)
            return carry

        lax.fori_loop(0, N // grp, local, 0)

        def step(n, state):
            r0 = pl.multiple_of(n * C, C)
            st_ref[0, 0, n] = state
            v_new = u_s[pl.ds(r0, C), :] - _dotm(w_s[pl.ds(r0, C), :], state, NN)
            o_ref[pl.ds(r0, C), :] = (_dotm(qg_s[pl.ds(r0, C), :], state, NN)
                                      + _dotm(M_s[pl.ds(r0, C), :], v_new, NN))
            return state * gam_s[n] + _dotm(kd_s[pl.ds(r0, C), :], v_new, TN)

        lax.fori_loop(0, N, step, jnp.zeros((hd, hd), f32))
        o = o_ref[...]
        z = z_ref[...]
        ob_ref[...] = (_head_rms(o, gon_ref[...])[0] * (z * jax.nn.sigmoid(z))).astype(ob_ref.dtype)

    blk = lambda off: pl.BlockSpec((S, LANES), lambda b, h: (b, off + h))
    rows = lambda: pltpu.VMEM((S, hd), f32)
    return pl.pallas_call(
        body, name="gdn_fwd", grid=(B, GDN_HEADS),
        in_specs=[blk(0), blk(GDN_HEADS), blk(2 * GDN_HEADS), pl.BlockSpec((S, LANES), lambda b, h: (b, 0)),
                  blk(COL_Z // LANES), pl.BlockSpec((1, hd), lambda b, h: (0, 0))],
        out_specs=[blk(0), blk(0), pl.BlockSpec((1, 1, N, hd, hd), lambda b, h: (b, h, 0, 0, 0))],
        out_shape=[jax.ShapeDtypeStruct((T, GDN_WIDTH), f32), jax.ShapeDtypeStruct((T, GDN_WIDTH), MXU_DTYPE),
                   jax.ShapeDtypeStruct((B, GDN_HEADS, N, hd, hd), f32)],
        scratch_shapes=[rows(), rows(), rows(), rows(), pltpu.VMEM((S, C), f32), pltpu.VMEM((N, 1, LANES), f32)],
        compiler_params=_cparams("parallel", "parallel"),
    )(G, G, G, gates, P, g_on)


def gdn_bwd(G, gates, P, g_on, o_raw, states, d_oab, *, B):
    T = G.shape[0]
    S = T // B
    C = GDN_CHUNK
    N = S // C
    grp = min(GDN_GROUP, N)
    R = grp * C
    hd = GDN_HEAD_DIM

    def body(q_ref, k_ref, v_ref, gt_ref, z_ref, gon_ref, o_ref, st_ref, dob_ref,
             dq_ref, dk_ref, dv_ref, dgt_ref, dz_ref, dgon_ref,
             u_s, w_s, qg_s, kd_s, M_s, gam_s, do_s, du_s, dw_s, dqg_s, dkd_s, dM_s, dgl_s, Tm_s):
        b, h = pl.program_id(0), pl.program_id(1)

        @pl.when((b == 0) & (h == 0))
        def _():
            dgon_ref[...] = jnp.zeros_like(dgon_ref)

        @pl.when(h == 0)
        def _():
            dgt_ref[...] = jnp.zeros_like(dgt_ref)

        def group_inputs(gi, Tm_of=None):
            r0 = pl.multiple_of(gi * R, R)
            gt = gt_ref[pl.ds(r0, R), :]
            Tm = None if Tm_of is None else Tm_of[pl.ds(r0, R), :]
            return r0, _gdn_local(q_ref[pl.ds(r0, R), :], k_ref[pl.ds(r0, R), :], v_ref[pl.ds(r0, R), :],
                                  _pick_lane(gt, SM_B + h), _pick_lane(gt, SM_A + h), Tm)

        def local(gi, carry):
            r0, loc = group_inputs(gi)
            _gdn_store_local(loc, r0, u_s, w_s, qg_s, kd_s, M_s, gam_s, gi * grp)
            Tm_s[pl.ds(r0, R), :] = loc["Tm"].reshape(R, C)
            o, z, gon = o_ref[pl.ds(r0, R), :], z_ref[pl.ds(r0, R), :], gon_ref[...]
            dob = dob_ref[pl.ds(r0, R), :]
            on, ro = _head_rms(o, gon)
            sz = jax.nn.sigmoid(z)
            dz_ref[pl.ds(r0, R), :] = (dob * on * (sz * (1.0 + z * (1.0 - sz)))).astype(dz_ref.dtype)
            do, dgon = _head_rms_bwd(o, ro, gon, dob * (z * sz))
            do_s[pl.ds(r0, R), :] = do
            dgon_ref[...] += dgon
            return carry

        lax.fori_loop(0, N // grp, local, 0)

        def step(t, dS):
            n = N - 1 - t
            r0 = pl.multiple_of(n * C, C)
            rows = pl.ds(r0, C)
            state = st_ref[0, 0, n]
            w_n, M_n, kd_n, do_n = w_s[rows, :], M_s[rows, :], kd_s[rows, :], do_s[rows, :]
            v_new = u_s[rows, :] - _dotm(w_n, state, NN)
            dv_new = _dotm(M_n, do_n, TN) + _dotm(kd_n, dS, NN)
            du_s[rows, :] = dv_new
            dw_s[rows, :] = -_dotm(dv_new, state, NT)
            dqg_s[rows, :] = _dotm(do_n, state, NT)
            dM_s[rows, :] = _dotm(do_n, v_new, NT)
            dkd_s[rows, :] = _dotm(v_new, dS, NT)
            gam = gam_s[n]
            dgl_s[n] = jnp.broadcast_to(jnp.sum(jnp.sum(dS * state, axis=1, keepdims=True), axis=0, keepdims=True), (1, LANES)) * gam
            return dS * gam + _dotm(qg_s[rows, :], do_n, TN) - _dotm(w_n, dv_new, TN)

        lax.fori_loop(0, N, step, jnp.zeros((hd, hd), f32))

        def finish(gi, carry):
            r0, L = group_inputs(gi, Tm_s)
            n = grp
            rows = pl.ds(r0, R)
            g3 = lambda ref: ref[rows, :].reshape(n, C, -1)
            du, dw, dqg, dkd, dM = g3(du_s), g3(dw_s), g3(dqg_s), g3(dkd_s), g3(dM_s)
            L["M"] = g3(M_s)
            TmT = jnp.swapaxes(L["Tm"], 1, 2)
            dTm = _bmm(du, L["vb"], B_NT) + _bmm(dw, L["kg"], B_NT)
            dvb = _bmm(TmT, du, B_NN)
            dkg = _bmm(TmT, dw, B_NN)
            dA = jnp.where(L["row"] > L["col"], -_bmm(_bmm(TmT, dTm, B_NN), TmT, B_NN), 0.0)
            dKK = dA * L["D"]
            dQK = dM * L["D"]
            dkb = _bmm(dKK, L["k"], B_NN) + dkg * L["Gam"]
            dk = (_bmm(dKK, L["kb"], B_TN) + _bmm(dQK, L["q"], B_TN) + dkd * L["kdec"] + L["beta"] * dkb)
            dq = (_bmm(dQK, L["k"], B_NN) + dqg * L["Gam"]) * (GDN_HEAD_DIM ** -0.5)
            E = dA * L["A"] + dM * L["M"]
            r = jnp.sum(dkd * L["kd"], axis=-1, keepdims=True)
            dgc = (jnp.sum(E, axis=2, keepdims=True) - jnp.sum(jnp.swapaxes(E, 1, 2), axis=2, keepdims=True)
                   + jnp.sum(dkg * L["kg"], axis=-1, keepdims=True) + jnp.sum(dqg * L["qg"], axis=-1, keepdims=True) - r)
            dgl = jnp.sum(r, axis=1, keepdims=True) + dgl_s[pl.ds(gi * n, n)][:, :, 0:1]
            rowc = lax.broadcasted_iota(jnp.int32, (n, C, 1), 1)
            dgc = dgc + jnp.where(rowc == C - 1, dgl, 0.0)
            dbeta = jnp.sum(dkb * L["k"], axis=-1, keepdims=True) + jnp.sum(dvb * L["v"], axis=-1, keepdims=True)
            dq_ref[rows, :] = dq.reshape(R, hd)
            dk_ref[rows, :] = dk.reshape(R, hd)
            dv_ref[rows, :] = (L["beta"] * dvb).reshape(R, hd)
            lane = lax.broadcasted_iota(jnp.int32, (R, LANES), 1)
            dgt_ref[rows, :] += (jnp.where(lane == SM_B + h, dbeta.reshape(R, 1), 0.0)
                                 + jnp.where(lane == SM_A + h, dgc.reshape(R, 1), 0.0))
            return carry

        lax.fori_loop(0, N // grp, finish, 0)

    blk = lambda off: pl.BlockSpec((S, LANES), lambda b, h: (b, off + h))
    rows = lambda: pltpu.VMEM((S, hd), f32)
    return pl.pallas_call(
        body, name="gdn_bwd", grid=(B, GDN_HEADS),
        in_specs=[blk(0), blk(GDN_HEADS), blk(2 * GDN_HEADS), pl.BlockSpec((S, LANES), lambda b, h: (b, 0)),
                  blk(COL_Z // LANES), pl.BlockSpec((1, hd), lambda b, h: (0, 0)), blk(0),
                  pl.BlockSpec((1, 1, N, hd, hd), lambda b, h: (b, h, 0, 0, 0)), blk(GDN_HEADS)],
        out_specs=[blk(0), blk(0), blk(0), pl.BlockSpec((S, LANES), lambda b, h: (b, 0)), blk(0),
                   pl.BlockSpec((1, hd), lambda b, h: (0, 0))],
        out_shape=[jax.ShapeDtypeStruct((T, GDN_WIDTH), f32), jax.ShapeDtypeStruct((T, GDN_WIDTH), f32),
                   jax.ShapeDtypeStruct((T, GDN_WIDTH), f32), jax.ShapeDtypeStruct((T, LANES), f32),
                   jax.ShapeDtypeStruct((T, GDN_WIDTH), MXU_DTYPE), jax.ShapeDtypeStruct((1, hd), f32)],
        scratch_shapes=[rows(), rows(), rows(), rows(), pltpu.VMEM((S, C), f32), pltpu.VMEM((N, 1, LANES), f32),
                        rows(), rows(), rows(), rows(), rows(), pltpu.VMEM((S, C), f32), pltpu.VMEM((N, 1, LANES), f32),
                        pltpu.VMEM((S, C), f32)],
        compiler_params=_cparams("arbitrary", "arbitrary"),
    )(G, G, G, gates, P, g_on, o_raw, states, d_oab)


IN_SPLIT = (0, 1536, 1544, 3080, 3088, 3600)


def align_w_in(w):
    s = IN_SPLIT
    pad = jnp.zeros((w.shape[0], IN_ALIGNED - IN_DIM), w.dtype)
    return jnp.concatenate([w[:, s[0]:s[1]], w[:, s[2]:s[3]], w[:, s[4]:s[5]], w[:, s[1]:s[2]], w[:, s[3]:s[4]], pad], axis=1)


def unalign_w_in(wa):
    return jnp.concatenate([wa[:, 0:1536], wa[:, COL_SMALL:COL_SMALL + 8], wa[:, 1536:3072],
                            wa[:, COL_SMALL + 8:COL_SMALL + 16], wa[:, 3072:3584]], axis=1)


def _lanes_vec(pieces):
    v = jnp.zeros((1, LANES), f32)
    for off, a in pieces:
        v = lax.dynamic_update_slice(v, a.astype(f32), (0, off))
    return v


def local_step(x, mem, target, w, sp, *, B):
    T = x.shape[0]
    S = T // B
    gq8, gk8 = jnp.tile(sp["fox_qnorm_g"], (1, FOX_HEADS)), jnp.tile(sp["fox_knorm_g"], (1, FOX_HEADS))
    go2 = jnp.tile(sp["fox_onorm_g"], (1, 2))
    bias = _lanes_vec([(SM_F, sp["fox_f_bias"]), (SM_A, sp["gdn_dt_bias"])])
    alog = _lanes_vec([(SM_A, sp["gdn_A_log"])])

    h1 = rms_fwd(x, sp["norm_mix_g"], name="rms_mix")
    P = matmul(h1, w["wa"], name="mm_in", tn=IN_TILE)
    gates = gates_fwd(P, bias, alog, B=B)
    c = gates[:, SM_F:SM_F + FOX_HEADS].reshape(B, S, FOX_HEADS).transpose(0, 2, 1)
    ccol, crow = c[..., None], c.reshape(B, FOX_HEADS, S // FOX_TQ, 1, FOX_TQ)
    qn, kn, vb = fox_prep_fwd(P, gq8, gk8)
    o_raw, o_a, lse = fox_core_fwd(qn, kn, vb, ccol, crow, go2, B=B)
    G = gdn_prep_fwd(P, w["conv_w"], B=B)
    ob_raw, o_b, states = gdn_fwd(G, gates, P, sp["gdn_onorm_g"], B=B)
    oab = jnp.concatenate([o_a, o_b], axis=1)
    if "late" in w:
        w = {**w, **w["late"](oab)}
    x2 = matmul(oab, w["w_out"], residual=x, name="mm_out")
    hq = rms_fwd(x2, sp["norm_xattn_g"], name="rms_xattn")
    hm = rms_fwd(mem, sp["mem_norm_g"], name="rms_mem")
    cq = matmul(hq, w["w_cq"], name="mm_cq")
    ckv = matmul(hm, w["w_ckv"], name="mm_ckv")
    co = xattn_fwd(cq, ckv, sp["xattn_qnorm_g"], sp["xattn_knorm_g"], B=B)
    x3 = matmul(co, w["w_co"], b_stacked=True, residual=x2, name="mm_co")
    hf = rms_fwd(x3, sp["norm_mlp_g"], name="rms_mlp")
    a, act = matmul(hf, w["w_mlp1"], b_stacked=True, relu2_out=True, name="mm_mlp1")
    x4 = matmul(act, w["w_mlp2"], residual=x3, name="mm_mlp2")
    dy, loss = loss_head(x4, target)

    da = matmul(dy, w["w_mlp2"], tb=True, relu2_bwd_aux=a, out_dtype=MXU_DTYPE, name="mm_d_act")
    g_mlp2 = matmul(act, dy, ta=True, out_dtype=WIRE_DTYPE, name="mm_g_mlp2")
    g_mlp1 = matmul(hf, da, ta=True, out_stacked=True, out_dtype=WIRE_DTYPE, name="mm_g_mlp1")
    dhf = matmul(da, w["w_mlp1"], tb=True, b_stacked=True, name="mm_d_hf")
    dx3, g_norm_mlp = rms_bwd(x3, sp["norm_mlp_g"], dhf, dy, name="rms_mlp_bwd")
    dco = matmul(dx3, w["w_co"], tb=True, b_stacked=True, name="mm_d_co")
    g_co = matmul(co, dx3, ta=True, out_stacked=True, out_dtype=WIRE_DTYPE, name="mm_g_co")
    dcq, dckv, g_xq, g_xk = xattn_bwd(cq, ckv, sp["xattn_qnorm_g"], sp["xattn_knorm_g"], dco, B=B)
    g_cq = matmul(hq, dcq, ta=True, out_dtype=WIRE_DTYPE, name="mm_g_cq")
    dhq = matmul(dcq, w["w_cq"], tb=True, name="mm_d_hq")
    g_ckv = matmul(hm, dckv, ta=True, out_dtype=WIRE_DTYPE, name="mm_g_ckv")
    dhm = matmul(dckv, w["w_ckv"], tb=True, name="mm_d_hm")
    _, g_mem_norm = rms_bwd(mem, sp["mem_norm_g"], dhm, None, name="rms_mem_bwd")
    dx2, g_norm_xattn = rms_bwd(x2, sp["norm_xattn_g"], dhq, dx3, name="rms_xattn_bwd")
    doab = matmul(dx2, w["w_out"], tb=True, name="mm_d_oab")
    g_out = matmul(oab, dx2, ta=True, out_dtype=WIRE_DTYPE, name="mm_g_out")
    dqn, dkn, dv_f, dccol, dcrow, dgo2 = fox_core_bwd(qn, kn, vb, ccol, crow, go2, o_raw, lse, doab, B=B)
    dq_f, dk_f, dgq8, dgk8 = fox_prep_bwd(P, gq8, gk8, dqn, dkn)
    dGq, dGk, dGv, dgt, dz, g_gdn_on = gdn_bwd(G, gates, P, sp["gdn_onorm_g"], ob_raw, states, doab, B=B)
    dPg, g_conv = gdn_prep_bwd(P, w["conv_w"], jnp.concatenate([dGq, dGk, dGv], axis=1), B=B)
    dc = (dccol[..., 0] + dcrow.reshape(B, FOX_HEADS, S)).transpose(0, 2, 1).reshape(T, FOX_HEADS)
    dgates = dgt + jnp.pad(dc, ((0, 0), (SM_F, LANES - SM_F - FOX_HEADS)))
    dsmall, par = gates_bwd(P, bias, alog, dgates, B=B)
    dP = jnp.concatenate([dq_f, dk_f, dv_f, dPg, dz, dsmall, jnp.zeros((T, IN_ALIGNED - COL_SMALL - LANES), MXU_DTYPE)], axis=1)
    g_wa = matmul(h1, dP, ta=True, out_dtype=WIRE_DTYPE, name="mm_g_in", tn=IN_TILE)
    dh1 = matmul(dP, w["wa"], tb=True, name="mm_d_h1", tk=IN_TILE)
    dx, g_norm_mix = rms_bwd(x, sp["norm_mix_g"], dh1, dx2, name="rms_mix_bwd")

    fold = lambda g: jnp.sum(g.reshape(-1, FOX_HEAD_DIM), axis=0, keepdims=True)
    by_rows = lambda g: g.reshape(N_CHIPS, g.shape[0] // N_CHIPS, g.shape[1])
    g_in = unalign_w_in(g_wa).reshape(D_MODEL, N_CHIPS, IN_DIM // N_CHIPS).transpose(1, 0, 2)
    big = dict(w_in=g_in, w_out=by_rows(g_out), w_cq=by_rows(g_cq), w_ckv=by_rows(g_ckv), w_co=g_co, w_mlp1=g_mlp1,
               w_mlp2=by_rows(g_mlp2))
    small = dict(norm_mix_g=g_norm_mix, fox_qnorm_g=fold(dgq8), fox_knorm_g=fold(dgk8),
                 fox_f_bias=par[0:1, SM_F:SM_F + FOX_HEADS], fox_onorm_g=fold(dgo2), gdn_conv_w=g_conv,
                 gdn_A_log=par[1:2, SM_A:SM_A + GDN_HEADS], gdn_dt_bias=par[0:1, SM_A:SM_A + GDN_HEADS],
                 gdn_onorm_g=g_gdn_on, norm_xattn_g=g_norm_xattn, mem_norm_g=g_mem_norm,
                 xattn_qnorm_g=g_xq, xattn_knorm_g=g_xk, norm_mlp_g=g_norm_mlp)
    return loss, dx, big, small


MESH_IDS = pl.DeviceIdType.MESH
N_CHIPS = 4
HBM_SPEC = pl.BlockSpec(memory_space=pltpu.HBM)
PACK_ROWS = 30720
PACK_HALF = PACK_ROWS // 2
PACK_BLOCK = 3072


def _place():
    return lax.axis_index("x"), lax.axis_index("y"), lax.axis_index("c")


def _other_chips(x, y):
    return [(1 - x, y), (x, 1 - y), (1 - x, 1 - y)]


def _remote(src, dst, send_sem, recv_sem, to):
    return pltpu.make_async_remote_copy(src_ref=src, dst_ref=dst, send_sem=send_sem, recv_sem=recv_sem,
                                        device_id=to, device_id_type=MESH_IDS)


def all_gather_shards(packed):
    half = PACK_HALF

    def body(src_ref, out_ref, send_sems, recv_sems):
        x, y, c = _place()
        me_chip = 2 * x + y
        sibling = (x, y, 1 - c)
        chips = _other_chips(x, y)

        def rows(chip, core):
            return out_ref.at[chip, pl.ds(core * half, half), :]

        sends = [_remote(src_ref.at[pl.ds(c * half, half), :], rows(me_chip, c), send_sems.at[j], recv_sems.at[j], (px, py, c))
                 for j, (px, py) in enumerate(chips)]
        for cp in sends:
            cp.start()
        passed = []
        for j, (px, py) in enumerate(chips):
            theirs = rows(2 * px + py, c)
            _remote(theirs, theirs, send_sems.at[j], recv_sems.at[j], (px, py, c)).wait_recv()
            cp = _remote(theirs, theirs, send_sems.at[3 + j], recv_sems.at[3 + j], sibling)
            cp.start()
            passed.append(cp)
        for j, (px, py) in enumerate(chips):
            theirs = rows(2 * px + py, 1 - c)
            _remote(theirs, theirs, send_sems.at[3 + j], recv_sems.at[3 + j], sibling).wait_recv()
        for cp in sends + passed:
            cp.wait_send()

    return pl.pallas_call(
        body, name="all_gather_shards", in_specs=[HBM_SPEC], out_specs=HBM_SPEC,
        out_shape=jax.ShapeDtypeStruct((N_CHIPS,) + packed.shape, packed.dtype),
        scratch_shapes=[pltpu.SemaphoreType.DMA((6,)), pltpu.SemaphoreType.DMA((6,))],
    )(packed)


def exchange_core_halves(G):
    half = PACK_HALF

    def body(g_ref, land_ref, send_sem, recv_sem):
        x, y, c = _place()
        cp = _remote(g_ref.at[:, pl.ds((1 - c) * half, half), :], land_ref, send_sem, recv_sem, (x, y, 1 - c))
        cp.start()
        cp.wait()

    return pl.pallas_call(
        body, name="exchange_core_halves", in_specs=[HBM_SPEC], out_specs=HBM_SPEC,
        out_shape=jax.ShapeDtypeStruct((N_CHIPS, half, LANES), G.dtype),
        scratch_shapes=[pltpu.SemaphoreType.DMA(()), pltpu.SemaphoreType.DMA(())],
    )(G)


def add_core_halves(G, land, core):
    nb = PACK_HALF // PACK_BLOCK

    def body(c_ref, g_ref, l_ref, o_ref):
        o_ref[...] = (g_ref[...].astype(f32) + l_ref[...].astype(f32)).astype(o_ref.dtype)

    blk = (1, PACK_BLOCK, LANES)
    return pl.pallas_call(
        body, name="add_core_halves",
        grid_spec=pltpu.PrefetchScalarGridSpec(
            num_scalar_prefetch=1, grid=(N_CHIPS, nb),
            in_specs=[pl.BlockSpec(blk, lambda k, i, c_ref: (k, c_ref[0] * nb + i, 0)),
                      pl.BlockSpec(blk, lambda k, i, c_ref: (k, i, 0))],
            out_specs=pl.BlockSpec(blk, lambda k, i, c_ref: (k, i, 0))),
        out_shape=jax.ShapeDtypeStruct(land.shape, land.dtype),
        compiler_params=_cparams("parallel", "parallel"),
    )(core, G, land)


def scatter_to_chips(part):
    def body(p_ref, land_ref, send_sems, recv_sems):
        x, y, c = _place()
        me_chip = 2 * x + y
        chips = _other_chips(x, y)
        sends = [_remote(p_ref.at[2 * px + py], land_ref.at[me_chip], send_sems.at[j], recv_sems.at[j], (px, py, c))
                 for j, (px, py) in enumerate(chips)]
        for cp in sends:
            cp.start()
        for j, (px, py) in enumerate(chips):
            slot = land_ref.at[2 * px + py]
            _remote(slot, slot, send_sems.at[j], recv_sems.at[j], (px, py, c)).wait_recv()
        for cp in sends:
            cp.wait_send()

    return pl.pallas_call(
        body, name="scatter_to_chips", in_specs=[HBM_SPEC], out_specs=HBM_SPEC,
        out_shape=jax.ShapeDtypeStruct(part.shape, part.dtype),
        scratch_shapes=[pltpu.SemaphoreType.DMA((3,)), pltpu.SemaphoreType.DMA((3,))],
    )(part)


def sum_chips(part, land, order):
    nb = PACK_HALF // PACK_BLOCK

    def body(order_ref, p_ref, l1_ref, l2_ref, l3_ref, o_ref):
        o_ref[...] = ((p_ref[0].astype(f32) + l1_ref[0].astype(f32)) + l2_ref[0].astype(f32)) + l3_ref[0].astype(f32)

    slot = lambda j: pl.BlockSpec((1, PACK_BLOCK, LANES), lambda i, order_ref: (order_ref[j], i, 0))
    return pl.pallas_call(
        body, name="sum_chips",
        grid_spec=pltpu.PrefetchScalarGridSpec(
            num_scalar_prefetch=1, grid=(nb,), in_specs=[slot(0), slot(1), slot(2), slot(3)],
            out_specs=pl.BlockSpec((PACK_BLOCK, LANES), lambda i, order_ref: (i, 0))),
        out_shape=jax.ShapeDtypeStruct((PACK_HALF, LANES), f32),
        compiler_params=_cparams("parallel"),
    )(order, part, land, land, land)


def swap_core_halves(red):
    def body(r_ref, out_ref, send_sem, recv_sem):
        x, y, c = _place()
        cp = _remote(r_ref, out_ref, send_sem, recv_sem, (x, y, 1 - c))
        cp.start()
        cp.wait()

    return pl.pallas_call(
        body, name="swap_core_halves", in_specs=[HBM_SPEC], out_specs=HBM_SPEC,
        out_shape=jax.ShapeDtypeStruct(red.shape, red.dtype),
        scratch_shapes=[pltpu.SemaphoreType.DMA(()), pltpu.SemaphoreType.DMA(())],
    )(red)


def _half(ref, core):
    rows = ref.shape[-2] // 2
    return ref.at[(slice(None),) * (len(ref.shape) - 2) + (pl.ds(core * rows, rows), slice(None))]


def gather_weights(shards, conv):
    n = len(shards)

    def body(*refs):
        src, conv_src = refs[:n], refs[n]
        out, conv_out = refs[n + 1:2 * n + 1], refs[2 * n + 1]
        send_sems, recv_sems = refs[2 * n + 2], refs[2 * n + 3]
        x, y, c = _place()
        me_chip = 2 * x + y
        sibling = (x, y, 1 - c)
        chips = _other_chips(x, y)
        sends = []
        for a in range(n):
            for j, (px, py) in enumerate(chips):
                sends.append(_remote(_half(src[a], c), _half(out[a].at[me_chip], c),
                                     send_sems.at[6 * a + j], recv_sems.at[6 * a + j], (px, py, c)))
        for j, (px, py) in enumerate(chips):
            sends.append(_remote(conv_src, conv_out.at[me_chip], send_sems.at[6 * n + j], recv_sems.at[6 * n + j], (px, py, c)))
        for cp in sends:
            cp.start()
        passed = []
        for a in range(n):
            for j, (px, py) in enumerate(chips):
                theirs = _half(out[a].at[2 * px + py], c)
                _remote(theirs, theirs, send_sems.at[6 * a + j], recv_sems.at[6 * a + j], (px, py, c)).wait_recv()
                cp = _remote(theirs, theirs, send_sems.at[6 * a + 3 + j], recv_sems.at[6 * a + 3 + j], sibling)
                cp.start()
                passed.append(cp)
        for j, (px, py) in enumerate(chips):
            theirs = conv_out.at[2 * px + py]
            _remote(theirs, theirs, send_sems.at[6 * n + j], recv_sems.at[6 * n + j], (px, py, c)).wait_recv()
        for a in range(n):
            for j, (px, py) in enumerate(chips):
                theirs = _half(out[a].at[2 * px + py], 1 - c)
                _remote(theirs, theirs, send_sems.at[6 * a + 3 + j], recv_sems.at[6 * a + 3 + j], sibling).wait_recv()
        for cp in sends + passed:
            cp.wait_send()

    return pl.pallas_call(
        body, name="gather_weights", in_specs=[HBM_SPEC] * (n + 1), out_specs=[HBM_SPEC] * (n + 1),
        out_shape=[jax.ShapeDtypeStruct((N_CHIPS,) + s.shape, s.dtype) for s in list(shards) + [conv]],
        scratch_shapes=[pltpu.SemaphoreType.DMA((6 * n + 3,)), pltpu.SemaphoreType.DMA((6 * n + 3,))],
    )(*shards, conv)


SEM_SPEC = pl.BlockSpec(memory_space=pltpu.SEMAPHORE)
SPLIT_EFFECT = pltpu.SideEffectType.DATAFLOW_SIDE_EFFECTING


def _gather_async_copies(src, land, send_sems, recv_sems, x, y, c):
    me_chip = 2 * x + y
    sends, arrivals = [], []
    for a in range(len(src)):
        for j, (px, py) in enumerate(_other_chips(x, y)):
            for core in range(2):
                sends.append(_remote(_half(src[a], c), _half(land[a].at[me_chip], c), send_sems.at[6 * a + 2 * j + core],
                                     recv_sems.at[6 * a + 2 * j + c], (px, py, core)))
                theirs = _half(land[a].at[2 * px + py], core)
                arrivals.append(_remote(theirs, theirs, send_sems.at[6 * a + 2 * j + core],
                                        recv_sems.at[6 * a + 2 * j + core], (px, py, core)))
    return sends, arrivals


def gather_weights_start(shards):
    n = len(shards)

    def body(*refs):
        src, land = refs[:n], refs[n:2 * n]
        send_sems, recv_sems, token = refs[2 * n], refs[2 * n + 1], refs[4 * n + 2]
        x, y, c = _place()
        for cp in _gather_async_copies(src, land, send_sems, recv_sems, x, y, c)[0]:
            cp.start()
        token[...] = jnp.zeros_like(token)

    zones = [pltpu.with_memory_space_constraint(lax.empty((N_CHIPS,) + s.shape, s.dtype), pltpu.HBM) for s in shards]
    srcs = [pltpu.with_memory_space_constraint(s, pltpu.HBM) for s in shards]
    out = pl.pallas_call(
        body, name="gather_weights_start",
        out_shape=[pltpu.SemaphoreType.DMA((6 * n,)), pltpu.SemaphoreType.DMA((6 * n,))]
        + [pltpu.HBM(s.shape, s.dtype) for s in shards] + [pltpu.HBM(z.shape, z.dtype) for z in zones]
        + [jax.ShapeDtypeStruct((8, LANES), f32)],
        in_specs=[HBM_SPEC] * (2 * n),
        out_specs=[SEM_SPEC, SEM_SPEC] + [HBM_SPEC] * (2 * n) + [pl.BlockSpec(memory_space=pltpu.VMEM)],
        input_output_aliases={i: 2 + i for i in range(2 * n)},
        compiler_params=pltpu.CompilerParams(has_side_effects=SPLIT_EFFECT),
    )(*srcs, *zones)
    return out[0], out[1], out[2:2 + n], out[2 + n:2 + 2 * n], out[-1]


def gather_weights_wait(send_sems, recv_sems, shards, zones, after):
    n = len(shards)

    def body(*refs):
        src, land = refs[:n], refs[n:2 * n]
        send_sems, recv_sems = refs[2 * n], refs[2 * n + 1]
        x, y, c = _place()
        sends, arrivals = _gather_async_copies(src, land, send_sems, recv_sems, x, y, c)
        for cp in sends:
            cp.wait_send()
        for cp in arrivals:
            cp.wait_recv()

    out = pl.pallas_call(
        body, name="gather_weights_wait",
        out_shape=[pltpu.HBM(s.shape, s.dtype) for s in shards] + [pltpu.HBM(z.shape, z.dtype) for z in zones],
        in_specs=[HBM_SPEC] * (2 * n) + [SEM_SPEC, SEM_SPEC, pl.BlockSpec(memory_space=pl.ANY)],
        out_specs=[HBM_SPEC] * (2 * n),
        input_output_aliases={i: i for i in range(2 * n)},
        compiler_params=pltpu.CompilerParams(has_side_effects=SPLIT_EFFECT),
    )(*shards, *zones, send_sems, recv_sems, after)
    return out[n:]


def swap_grad_halves(grads):
    n = len(grads)

    def body(*refs):
        g, land, send_sems, recv_sems = refs[:n], refs[n:2 * n], refs[2 * n], refs[2 * n + 1]
        x, y, c = _place()
        copies = [_remote(_half(g[a], 1 - c), land[a], send_sems.at[a], recv_sems.at[a], (x, y, 1 - c)) for a in range(n)]
        for cp in copies:
            cp.start()
        for cp in copies:
            cp.wait()

    return pl.pallas_call(
        body, name="swap_grad_halves", in_specs=[HBM_SPEC] * n, out_specs=[HBM_SPEC] * n,
        out_shape=[jax.ShapeDtypeStruct((N_CHIPS, g.shape[1] // 2, g.shape[2]), g.dtype) for g in grads],
        scratch_shapes=[pltpu.SemaphoreType.DMA((n,)), pltpu.SemaphoreType.DMA((n,))],
    )(*grads)


GRAD_ROWS = 256


def add_grad_halves(g, land, core, *, name):
    _, half, cols = land.shape
    tr = min(GRAD_ROWS, half)
    nb = half // tr

    def body(c_ref, g_ref, l_ref, o_ref):
        o_ref[...] = (g_ref[...].astype(f32) + l_ref[...].astype(f32)).astype(o_ref.dtype)

    blk = (1, tr, cols)
    return pl.pallas_call(
        body, name=name,
        grid_spec=pltpu.PrefetchScalarGridSpec(
            num_scalar_prefetch=1, grid=(N_CHIPS, nb),
            in_specs=[pl.BlockSpec(blk, lambda k, i, c_ref: (k, c_ref[0] * nb + i, 0)),
                      pl.BlockSpec(blk, lambda k, i, c_ref: (k, i, 0))],
            out_specs=pl.BlockSpec(blk, lambda k, i, c_ref: (k, i, 0))),
        out_shape=jax.ShapeDtypeStruct(land.shape, land.dtype),
        compiler_params=_cparams("parallel", "parallel"),
    )(core, g, land)


def scatter_grads(parts):
    n = len(parts)

    def body(*refs):
        p, land, send_sems, recv_sems = refs[:n], refs[n:2 * n], refs[2 * n], refs[2 * n + 1]
        x, y, c = _place()
        me_chip = 2 * x + y
        chips = _other_chips(x, y)
        sends = [_remote(p[a].at[2 * px + py], land[a].at[me_chip], send_sems.at[3 * a + j], recv_sems.at[3 * a + j], (px, py, c))
                 for a in range(n) for j, (px, py) in enumerate(chips)]
        for cp in sends:
            cp.start()
        for a in range(n):
            for j, (px, py) in enumerate(chips):
                slot = land[a].at[2 * px + py]
                _remote(slot, slot, send_sems.at[3 * a + j], recv_sems.at[3 * a + j], (px, py, c)).wait_recv()
        for cp in sends:
            cp.wait_send()

    return pl.pallas_call(
        body, name="scatter_grads", in_specs=[HBM_SPEC] * n, out_specs=[HBM_SPEC] * n,
        out_shape=[jax.ShapeDtypeStruct(p.shape, p.dtype) for p in parts],
        scratch_shapes=[pltpu.SemaphoreType.DMA((3 * n,)), pltpu.SemaphoreType.DMA((3 * n,))],
    )(*parts)


def sum_grads(part, land, order, *, name):
    _, half, cols = part.shape
    tr = min(GRAD_ROWS, half)

    def body(order_ref, p_ref, l1_ref, l2_ref, l3_ref, o_ref):
        o_ref[...] = ((p_ref[0].astype(f32) + l1_ref[0].astype(f32)) + l2_ref[0].astype(f32)) + l3_ref[0].astype(f32)

    slot = lambda j: pl.BlockSpec((1, tr, cols), lambda i, order_ref: (order_ref[j], i, 0))
    return pl.pallas_call(
        body, name=name,
        grid_spec=pltpu.PrefetchScalarGridSpec(
            num_scalar_prefetch=1, grid=(half // tr,), in_specs=[slot(0), slot(1), slot(2), slot(3)],
            out_specs=pl.BlockSpec((tr, cols), lambda i, order_ref: (i, 0))),
        out_shape=jax.ShapeDtypeStruct((half, cols), f32),
        compiler_params=_cparams("parallel"),
    )(order, part, land, land, land)


def swap_reduced_halves(mine):
    n = len(mine)

    def body(*refs):
        r, out, send_sems, recv_sems = refs[:n], refs[n:2 * n], refs[2 * n], refs[2 * n + 1]
        x, y, c = _place()
        copies = [_remote(r[a], out[a], send_sems.at[a], recv_sems.at[a], (x, y, 1 - c)) for a in range(n)]
        for cp in copies:
            cp.start()
        for cp in copies:
            cp.wait()

    return pl.pallas_call(
        body, name="swap_reduced_halves", in_specs=[HBM_SPEC] * n, out_specs=[HBM_SPEC] * n,
        out_shape=[jax.ShapeDtypeStruct(r.shape, r.dtype) for r in mine],
        scratch_shapes=[pltpu.SemaphoreType.DMA((n,)), pltpu.SemaphoreType.DMA((n,))],
    )(*mine)


def adamw_halves(w, mine, theirs, m, v, core, *, name):
    R, C = w.shape
    tr = min(GRAD_ROWS, R // 2)
    half_nb = R // 2 // tr

    def body(c_ref, w_ref, a_ref, b_ref, m_ref, v_ref, g_ref, d_ref, nm_ref, nv_ref):
        low = pl.program_id(0) < half_nb
        gv = jnp.where(low == (c_ref[0] == 0), a_ref[...], b_ref[...])
        nm = ADAM_B1 * m_ref[...] + (1.0 - ADAM_B1) * gv
        nv = ADAM_B2 * v_ref[...] + (1.0 - ADAM_B2) * jnp.square(gv)
        m_hat = nm / (1.0 - ADAM_B1 ** ADAM_STEP)
        v_hat = nv / (1.0 - ADAM_B2 ** ADAM_STEP)
        g_ref[...] = gv
        d_ref[...] = -ADAM_LR * (m_hat / (jnp.sqrt(v_hat) + ADAM_EPS) + ADAM_WD * w_ref[...])
        nm_ref[...] = nm
        nv_ref[...] = nv

    full = pl.BlockSpec((tr, C), lambda i, c_ref: (i, 0))
    part = pl.BlockSpec((tr, C), lambda i, c_ref: (i % half_nb, 0))
    out = jax.ShapeDtypeStruct((R, C), f32)
    return pl.pallas_call(
        body, name=name,
        grid_spec=pltpu.PrefetchScalarGridSpec(
            num_scalar_prefetch=1, grid=(2 * half_nb,), in_specs=[full, part, part, full, full], out_specs=[full] * 4),
        out_shape=[out] * 4, compiler_params=_cparams("parallel"),
    )(core, w, mine, theirs, m, v)


N_DEV = 8


def all_reduce_small(v):
    def body(src_ref, out_ref, land_ref, send_sems, recv_sems):
        x, y, c = _place()
        me = 4 * x + 2 * y + c
        copies = []
        for r in range(1, N_DEV):
            peer = ((1 - x) if r & 4 else x, (1 - y) if r & 2 else y, (1 - c) if r & 1 else c)
            copies.append(_remote(src_ref, land_ref.at[r], send_sems.at[r - 1], recv_sems.at[r - 1], peer))
        for cp in copies:
            cp.start()
        land_ref[0] = src_ref[...]
        for cp in copies:
            cp.wait()
        acc = land_ref[me]
        for d in range(1, N_DEV):
            acc = acc + land_ref[jnp.bitwise_xor(me, d)]
        out_ref[...] = acc

    vm = pl.BlockSpec(memory_space=pltpu.VMEM)
    return pl.pallas_call(
        body, name="all_reduce_small", in_specs=[vm], out_specs=vm,
        out_shape=jax.ShapeDtypeStruct(v.shape, v.dtype),
        scratch_shapes=[pltpu.VMEM((N_DEV,) + v.shape, v.dtype),
                        pltpu.SemaphoreType.DMA((N_DEV - 1,)), pltpu.SemaphoreType.DMA((N_DEV - 1,))],
    )(v)


def adamw(w, g, m, v, *, name, tr):
    R, C = w.shape
    tr = min(tr, R)

    def body(w_ref, g_ref, m_ref, v_ref, d_ref, nm_ref, nv_ref):
        gv = g_ref[...]
        nm = ADAM_B1 * m_ref[...] + (1.0 - ADAM_B1) * gv
        nv = ADAM_B2 * v_ref[...] + (1.0 - ADAM_B2) * jnp.square(gv)
        m_hat = nm / (1.0 - ADAM_B1 ** ADAM_STEP)
        v_hat = nv / (1.0 - ADAM_B2 ** ADAM_STEP)
        d_ref[...] = -ADAM_LR * (m_hat / (jnp.sqrt(v_hat) + ADAM_EPS) + ADAM_WD * w_ref[...])
        nm_ref[...] = nm
        nv_ref[...] = nv

    blk = pl.BlockSpec((tr, C), lambda i: (i, 0))
    out = jax.ShapeDtypeStruct((R, C), f32)
    return pl.pallas_call(
        body, name=name, grid=(R // tr,), in_specs=[blk] * 4, out_specs=[blk] * 3, out_shape=[out] * 3,
        compiler_params=_cparams("parallel"),
    )(w, g, m, v)


BIG_SHARDS = (("w_in", (1024, 900), True), ("w_out", (256, 1024), False), ("w_cq", (256, 512), False),
              ("w_ckv", (256, 1024), False), ("w_co", (512, 256), True), ("w_mlp1", (1024, 1024), True),
              ("w_mlp2", (1024, 1024), False))
CONV_SHARD = (CONV_WIDTH, 3 * GDN_WIDTH // N_CHIPS)
SMALL_DIMS = (("norm_mix_g", 1024), ("fox_qnorm_g", 64), ("fox_knorm_g", 64), ("fox_f_bias", 8), ("fox_onorm_g", 64),
              ("gdn_A_log", 4), ("gdn_dt_bias", 4), ("gdn_onorm_g", 128), ("norm_xattn_g", 1024), ("mem_norm_g", 1024),
              ("xattn_qnorm_g", 128), ("xattn_knorm_g", 128), ("norm_mlp_g", 1024))
WEIGHT_ORDER = ("norm_mix_g", "w_in", "fox_qnorm_g", "fox_knorm_g", "fox_f_bias", "fox_onorm_g", "gdn_conv_w", "gdn_A_log",
                "gdn_dt_bias", "gdn_onorm_g", "w_out", "norm_xattn_g", "mem_norm_g", "w_cq", "w_ckv", "xattn_qnorm_g",
                "xattn_knorm_g", "w_co", "norm_mlp_g", "w_mlp1", "w_mlp2")


def _pack_rows(pieces, rows, lead=()):
    flat = []
    for p in pieces:
        p = p.reshape(lead + (-1,))
        pad = (-p.shape[-1]) % LANES
        flat.append(jnp.pad(p, [(0, 0)] * len(lead) + [(0, pad)]) if pad else p)
    cat = jnp.concatenate(flat, axis=-1)
    cat = jnp.pad(cat, [(0, 0)] * len(lead) + [(0, rows * LANES - cat.shape[-1])])
    return cat.reshape(lead + (rows, LANES))


def _unpack_rows(buf, sizes, lead=()):
    flat = buf.reshape(lead + (-1,))
    out, off = [], 0
    for n in sizes:
        out.append(flat[..., off:off + n])
        off += n + (-n) % LANES
    return out


def _conv_to_wire(conv):
    return lax.bitcast_convert_type(conv, bf16)


def _conv_from_wire(wire):
    return lax.bitcast_convert_type(wire, f32)


SMALL_ROWS = 96
SMALL_ADAM_ROWS = 56


def kernel(x, mem, norm_mix_g, w_in, fox_qnorm_g, fox_knorm_g, fox_f_bias, fox_onorm_g, gdn_conv_w, gdn_A_log, gdn_dt_bias, gdn_onorm_g, w_out, norm_xattn_g, mem_norm_g, w_cq, w_ckv, xattn_qnorm_g, xattn_knorm_g, w_co, norm_mlp_g, w_mlp1, w_mlp2, loss_target, m_norm_mix_g, m_w_in, m_fox_qnorm_g, m_fox_knorm_g, m_fox_f_bias, m_fox_onorm_g, m_gdn_conv_w, m_gdn_A_log, m_gdn_dt_bias, m_gdn_onorm_g, m_w_out, m_norm_xattn_g, m_mem_norm_g, m_w_cq, m_w_ckv, m_xattn_qnorm_g, m_xattn_knorm_g, m_w_co, m_norm_mlp_g, m_w_mlp1, m_w_mlp2, v_norm_mix_g, v_w_in, v_fox_qnorm_g, v_fox_knorm_g, v_fox_f_bias, v_fox_onorm_g, v_gdn_conv_w, v_gdn_A_log, v_gdn_dt_bias, v_gdn_onorm_g, v_w_out, v_norm_xattn_g, v_mem_norm_g, v_w_cq, v_w_ckv, v_xattn_qnorm_g, v_xattn_knorm_g, v_w_co, v_norm_mlp_g, v_w_mlp1, v_w_mlp2):
    wts = dict(norm_mix_g=norm_mix_g, w_in=w_in, fox_qnorm_g=fox_qnorm_g, fox_knorm_g=fox_knorm_g, fox_f_bias=fox_f_bias,
               fox_onorm_g=fox_onorm_g, gdn_conv_w=gdn_conv_w, gdn_A_log=gdn_A_log, gdn_dt_bias=gdn_dt_bias,
               gdn_onorm_g=gdn_onorm_g, w_out=w_out, norm_xattn_g=norm_xattn_g, mem_norm_g=mem_norm_g, w_cq=w_cq, w_ckv=w_ckv,
               xattn_qnorm_g=xattn_qnorm_g, xattn_knorm_g=xattn_knorm_g, w_co=w_co, norm_mlp_g=norm_mlp_g, w_mlp1=w_mlp1,
               w_mlp2=w_mlp2)
    mom = dict(norm_mix_g=m_norm_mix_g, w_in=m_w_in, fox_qnorm_g=m_fox_qnorm_g, fox_knorm_g=m_fox_knorm_g,
               fox_f_bias=m_fox_f_bias, fox_onorm_g=m_fox_onorm_g, gdn_conv_w=m_gdn_conv_w, gdn_A_log=m_gdn_A_log,
               gdn_dt_bias=m_gdn_dt_bias, gdn_onorm_g=m_gdn_onorm_g, w_out=m_w_out, norm_xattn_g=m_norm_xattn_g,
               mem_norm_g=m_mem_norm_g, w_cq=m_w_cq, w_ckv=m_w_ckv, xattn_qnorm_g=m_xattn_qnorm_g,
               xattn_knorm_g=m_xattn_knorm_g, w_co=m_w_co, norm_mlp_g=m_norm_mlp_g, w_mlp1=m_w_mlp1, w_mlp2=m_w_mlp2)
    var = dict(norm_mix_g=v_norm_mix_g, w_in=v_w_in, fox_qnorm_g=v_fox_qnorm_g, fox_knorm_g=v_fox_knorm_g,
               fox_f_bias=v_fox_f_bias, fox_onorm_g=v_fox_onorm_g, gdn_conv_w=v_gdn_conv_w, gdn_A_log=v_gdn_A_log,
               gdn_dt_bias=v_gdn_dt_bias, gdn_onorm_g=v_gdn_onorm_g, w_out=v_w_out, norm_xattn_g=v_norm_xattn_g,
               mem_norm_g=v_mem_norm_g, w_cq=v_w_cq, w_ckv=v_w_ckv, xattn_qnorm_g=v_xattn_qnorm_g,
               xattn_knorm_g=v_xattn_knorm_g, w_co=v_w_co, norm_mlp_g=v_norm_mlp_g, w_mlp1=v_w_mlp1, w_mlp2=v_w_mlp2)
    B, S, D = x.shape
    T = B * S
    big_names = [n for n, _, _ in BIG_SHARDS]
    chip = 2 * lax.axis_index("x") + lax.axis_index("y")
    core = lax.axis_index("c").astype(jnp.int32).reshape(1)

    shards = {n: wts[n][0].astype(MXU_DTYPE) for n in big_names}
    w_in_all, conv_all = gather_weights([shards["w_in"]], gdn_conv_w[0])
    late = big_names[1:]
    send_sems, recv_sems, late_src, late_zones, token = gather_weights_start([shards[n] for n in late])
    own = lambda g, s: lax.dynamic_update_slice(g, s[None], (chip,) + (0,) * s.ndim)
    full = {"w_in": own(w_in_all, shards["w_in"])}
    conv_full = own(conv_all, gdn_conv_w[0]).transpose(1, 0, 2).reshape(CONV_WIDTH, 3 * GDN_WIDTH)
    rows = lambda g: g.reshape(N_CHIPS * g.shape[1], g.shape[2])
    w_in_full = full["w_in"].transpose(1, 0, 2).reshape(D_MODEL, IN_DIM)

    def late_weights(after):
        zones = gather_weights_wait(send_sems, recv_sems, late_src, late_zones, after)
        got = {n: own(z, shards[n]) for n, z in zip(late, zones)}
        return dict(w_out=rows(got["w_out"]), w_cq=rows(got["w_cq"]), w_ckv=rows(got["w_ckv"]), w_co=got["w_co"],
                    w_mlp1=got["w_mlp1"], w_mlp2=rows(got["w_mlp2"]))

    w = dict(wa=align_w_in(w_in_full), conv_w=conv_full, late=late_weights)
    sp = {n: wts[n] for n, _ in SMALL_DIMS}
    sp["norm_mix_g"] = sp["norm_mix_g"] + token[0, 0]

    loss_part, grad_x, g_big, g_small = local_step(x.reshape(T, D), mem.reshape(-1, D), loss_target.reshape(T, D), w, sp, B=B)

    small_pieces = [g_small[n] for n, _ in SMALL_DIMS] + [g_small["gdn_conv_w"], loss_part]
    small_sizes = [d for _, d in SMALL_DIMS] + [CONV_WIDTH * 3 * GDN_WIDTH, LANES]
    red_small = _unpack_rows(all_reduce_small(_pack_rows(small_pieces, SMALL_ROWS)), small_sizes)
    grads = {n: p.reshape(1, d) for (n, d), p in zip(SMALL_DIMS, red_small)}
    conv_grad = lax.dynamic_slice(red_small[-2].reshape(CONV_WIDTH, 3 * GDN_WIDTH), (0, chip * CONV_SHARD[1]), CONV_SHARD)
    grads["gdn_conv_w"] = conv_grad.reshape((1,) + CONV_SHARD)
    loss = red_small[-1][0]

    by_chip = [g_big[n] for n in big_names]
    landed = swap_grad_halves(by_chip)
    chip_part = [add_grad_halves(g, l, core, name="add_halves_" + n) for n, g, l in zip(big_names, by_chip, landed)]
    order = jnp.stack([chip, chip ^ 2, chip ^ 1, chip ^ 3]).astype(jnp.int32)
    mine = [sum_grads(p, l, order, name="sum_chips_" + n) for n, p, l in zip(big_names, chip_part, scatter_grads(chip_part))]
    theirs = swap_reduced_halves(mine)

    delta, new_m, new_v = {}, {}, {}
    for n, a, b in zip(big_names, mine, theirs):
        g, d, nm, nv = adamw_halves(wts[n][0], a, b, mom[n][0], var[n][0], core, name="adamw_" + n)
        grads[n], delta[n], new_m[n], new_v[n] = g[None], d[None], nm[None], nv[None]
    small_names = [n for n, _ in SMALL_DIMS] + ["gdn_conv_w"]
    small_sz = [d for _, d in SMALL_DIMS] + [CONV_SHARD[0] * CONV_SHARD[1]]
    packed4 = [_pack_rows([src[n] for n in small_names], SMALL_ADAM_ROWS) for src in (wts, grads, mom, var)]
    outs = adamw(*packed4, name="adamw_small", tr=SMALL_ADAM_ROWS)
    for dst, buf in zip((delta, new_m, new_v), outs):
        for n, p in zip(small_names, _unpack_rows(buf, small_sz)):
            dst[n] = p.reshape(wts[n].shape)

    return (loss, grad_x.reshape(B, S, D), *[grads[n] for n in WEIGHT_ORDER], *[delta[n] for n in WEIGHT_ORDER],
            *[new_m[n] for n in WEIGHT_ORDER], *[new_v[n] for n in WEIGHT_ORDER])
```

```python
import functools

import jax
import jax.numpy as jnp
import numpy as np
from jax import lax
from jax.experimental import pallas as pl
from jax.experimental.pallas import tpu as pltpu

f32 = jnp.float32
bf16 = jnp.bfloat16
MXU_DTYPE = jnp.bfloat16
WIRE_DTYPE = jnp.bfloat16
INV_PRECISION = lax.Precision.HIGH

D_MODEL = 1024
FOX_HEADS = 8
FOX_HEAD_DIM = 64
FOX_WIDTH = 512
GDN_HEADS = 4
GDN_HEAD_DIM = 128
GDN_WIDTH = 512
CONV_WIDTH = 4
GDN_CHUNK = 64
XATTN_HEADS = 4
XATTN_HEAD_DIM = 128
XATTN_WIDTH = 512
D_FF = 4096
IN_DIM = 3600
EPS = 1e-6
NEG_INF = -1e30
LANES = 128
ADAM_LR = 0.001
ADAM_B1 = 0.9
ADAM_B2 = 0.999
ADAM_EPS = 1e-08
ADAM_WD = 0.01
ADAM_STEP = 10
VMEM_LIMIT = 48 * 1024 * 1024

COL_FOX = 0
COL_GDN = 1536
COL_Z = 3072
COL_SMALL = 3584
IN_ALIGNED = 3840
IN_TILE = 768
SM_F = 0
SM_B = 8
SM_A = 12


def _cparams(*sem):
    return pltpu.CompilerParams(dimension_semantics=sem, vmem_limit_bytes=VMEM_LIMIT)


def _mx(v):
    return v.astype(MXU_DTYPE)


def _dot(a, b, dims, precision=None):
    return lax.dot_general(a, b, (dims, ((), ())), preferred_element_type=f32, precision=precision)


def _dotm(a, b, dims):
    return _dot(_mx(a), _mx(b), dims)


NN = ((1,), (0,))
NT = ((1,), (1,))
TN = ((0,), (0,))


def matmul(a, b, *, name, ta=False, tb=False, b_stacked=False, out_stacked=False, residual=None, relu2_out=False,
           relu2_bwd_aux=None, out_dtype=f32, tm=1024, tn=1024, tk=1024):
    M, K = (a.shape[1], a.shape[0]) if ta else a.shape
    if b_stacked:
        b_cols = b.shape[2]
        N, tk = (b.shape[1], min(tk, b_cols)) if tb else (N_CHIPS * b_cols, tk)
        tn = tn if tb else min(tn, b_cols)
        assert K == (N_CHIPS * b_cols if tb else b.shape[1]), (name, a.shape, b.shape)
    else:
        N = b.shape[0] if tb else b.shape[1]
    if out_stacked:
        tn = min(tn, N // N_CHIPS)
    tm, tn, tk = min(tm, M), min(tn, N), min(tk, K)
    assert M % tm == 0 and N % tn == 0 and K % tk == 0, (name, M, N, K)
    nk = K // tk
    has_res = residual is not None
    has_aux = relu2_bwd_aux is not None

    def body(*refs):
        a_ref, b_ref = refs[0], refs[1]
        pos = 2
        res_ref = aux_ref = None
        if has_res:
            res_ref = refs[pos]
            pos += 1
        if has_aux:
            aux_ref = refs[pos]
            pos += 1
        o_ref = refs[pos]
        pos += 1
        act_ref = None
        if relu2_out:
            act_ref = refs[pos]
            pos += 1
        acc_ref = refs[pos]
        k = pl.program_id(2)

        @pl.when(k == 0)
        def _():
            acc_ref[...] = jnp.zeros_like(acc_ref)

        dims = ((0,) if ta else (1,), (1,) if tb else (0,))
        acc_ref[...] += _dot(_mx(a_ref[...]), _mx(b_ref[...]), dims)

        @pl.when(k == nk - 1)
        def _():
            r = acc_ref[...]
            if has_res:
                r = r + res_ref[...]
            if has_aux:
                r = r * (2.0 * jnp.maximum(aux_ref[...], 0.0))
            o_ref[...] = r.astype(o_ref.dtype)
            if relu2_out:
                act_ref[...] = jnp.square(jnp.maximum(r, 0.0)).astype(act_ref.dtype)

    a_spec = pl.BlockSpec((tk, tm), lambda i, j, k: (k, i)) if ta else pl.BlockSpec((tm, tk), lambda i, j, k: (i, k))
    if b_stacked and tb:
        per = b_cols // tk
        b_spec = pl.BlockSpec((None, tn, tk), lambda i, j, k: (k // per, j, k % per))
    elif b_stacked:
        per = b_cols // tn
        b_spec = pl.BlockSpec((None, tk, tn), lambda i, j, k: (j // per, k, j % per))
    else:
        b_spec = pl.BlockSpec((tn, tk), lambda i, j, k: (j, k)) if tb else pl.BlockSpec((tk, tn), lambda i, j, k: (k, j))
    if out_stacked:
        assert not (has_res or has_aux or relu2_out), name
        per_o = N // N_CHIPS // tn
        o_spec = pl.BlockSpec((None, tm, tn), lambda i, j, k: (j // per_o, i, j % per_o))
        out_full = (N_CHIPS, M, N // N_CHIPS)
    else:
        o_spec = pl.BlockSpec((tm, tn), lambda i, j, k: (i, j))
        out_full = (M, N)
    in_specs, args = [a_spec, b_spec], [a, b]
    if has_res:
        in_specs.append(o_spec)
        args.append(residual)
    if has_aux:
        in_specs.append(o_spec)
        args.append(relu2_bwd_aux)
    out_shape = [jax.ShapeDtypeStruct(out_full, out_dtype)]
    out_specs = [o_spec]
    if relu2_out:
        out_shape.append(jax.ShapeDtypeStruct((M, N), MXU_DTYPE))
        out_specs.append(o_spec)
    res = pl.pallas_call(
        body, name=name, grid=(M // tm, N // tn, nk), in_specs=in_specs, out_specs=out_specs, out_shape=out_shape,
        scratch_shapes=[pltpu.VMEM((tm, tn), f32)],
        compiler_params=_cparams("parallel", "parallel", "arbitrary"),
    )(*args)
    return res if relu2_out else res[0]


def rms_fwd(x, g, *, name, tr=512):
    R, D = x.shape
    tr = min(tr, R)

    def body(x_ref, g_ref, o_ref):
        xv = x_ref[...]
        y = xv * lax.rsqrt(jnp.mean(xv * xv, axis=-1, keepdims=True) + EPS)
        o_ref[...] = (y * g_ref[...]).astype(o_ref.dtype)

    return pl.pallas_call(
        body, name=name, grid=(R // tr,),
        in_specs=[pl.BlockSpec((tr, D), lambda i: (i, 0)), pl.BlockSpec((1, D), lambda i: (0, 0))],
        out_specs=pl.BlockSpec((tr, D), lambda i: (i, 0)),
        out_shape=jax.ShapeDtypeStruct((R, D), MXU_DTYPE),
        compiler_params=_cparams("parallel"),
    )(x, g)


def rms_bwd(x, g, dh, residual, *, name, tr=512):
    R, D = x.shape
    tr = min(tr, R)
    has_res = residual is not None

    def body(*refs):
        if has_res:
            x_ref, g_ref, dh_ref, res_ref, dx_ref, dg_ref = refs
        else:
            x_ref, g_ref, dh_ref, dx_ref, dg_ref = refs
        xv = x_ref[...]
        rstd = lax.rsqrt(jnp.mean(xv * xv, axis=-1, keepdims=True) + EPS)
        xhat = xv * rstd
        dh = dh_ref[...].astype(f32)
        gd = dh * g_ref[...]
        dx = rstd * (gd - xhat * jnp.mean(gd * xhat, axis=-1, keepdims=True))
        if has_res:
            dx = dx + res_ref[...]
        dx_ref[...] = dx

        @pl.when(pl.program_id(0) == 0)
        def _():
            dg_ref[...] = jnp.zeros_like(dg_ref)

        dg_ref[...] += jnp.sum(dh * xhat, axis=0, keepdims=True)

    row = pl.BlockSpec((tr, D), lambda i: (i, 0))
    vec = pl.BlockSpec((1, D), lambda i: (0, 0))
    in_specs = [row, vec, row] + ([row] if has_res else [])
    args = [x, g, dh] + ([residual] if has_res else [])
    return pl.pallas_call(
        body, name=name, grid=(R // tr,), in_specs=in_specs, out_specs=[row, vec],
        out_shape=[jax.ShapeDtypeStruct((R, D), f32), jax.ShapeDtypeStruct((1, D), f32)],
        compiler_params=_cparams("arbitrary"),
    )(*args)


def loss_head(y, target, *, tr=512):
    R, D = y.shape
    tr = min(tr, R)

    def body(y_ref, t_ref, dy_ref, loss_ref):
        e = y_ref[...] - t_ref[...]
        dy_ref[...] = e * (1.0 / D)

        @pl.when(pl.program_id(0) == 0)
        def _():
            loss_ref[...] = jnp.zeros_like(loss_ref)

        part = 0.5 * jnp.sum(jnp.mean(e * e, axis=-1, keepdims=True), axis=0, keepdims=True)
        loss_ref[...] += jnp.broadcast_to(part, loss_ref.shape)

    row = pl.BlockSpec((tr, D), lambda i: (i, 0))
    return pl.pallas_call(
        body, name="loss_head", grid=(R // tr,), in_specs=[row, row],
        out_specs=[row, pl.BlockSpec((1, LANES), lambda i: (0, 0))],
        out_shape=[jax.ShapeDtypeStruct((R, D), f32), jax.ShapeDtypeStruct((1, LANES), f32)],
        compiler_params=_cparams("arbitrary"),
    )(y, target)


def _head_rms(v, g):
    r = lax.rsqrt(jnp.mean(v * v, axis=-1, keepdims=True) + EPS)
    return v * r * g, r


def _head_rms_bwd(v, r, g, dn):
    vhat = v * r
    gd = dn * g
    dv = r * (gd - vhat * jnp.mean(gd * vhat, axis=-1, keepdims=True))
    return dv, jnp.sum(dn * vhat, axis=0, keepdims=True)


def _softmax_rows(s):
    m = jnp.max(s, axis=-1, keepdims=True)
    e = jnp.exp(s - m)
    return e / jnp.sum(e, axis=-1, keepdims=True)


def xattn_fwd(cq, ckv, gq, gk, *, B, tq=512):
    T = cq.shape[0]
    S = T // B
    M = ckv.shape[0] // B
    tq = min(tq, S)
    nq = S // tq
    scale = XATTN_HEAD_DIM ** -0.5

    def body(q_ref, k_ref, v_ref, gq_ref, gk_ref, o_ref):
        qn, _ = _head_rms(q_ref[...], gq_ref[...])
        kn, _ = _head_rms(k_ref[...], gk_ref[...])
        p = _softmax_rows(_dot(_mx(qn), _mx(kn), NT) * scale)
        o_ref[...] = _dot(_mx(p), _mx(v_ref[...]), NN).astype(o_ref.dtype)

    hd = XATTN_HEAD_DIM
    vec = pl.BlockSpec((1, hd), lambda b, h, i: (0, 0))
    return pl.pallas_call(
        body, name="xattn_fwd", grid=(B, XATTN_HEADS, nq),
        in_specs=[pl.BlockSpec((tq, hd), lambda b, h, i: (b * nq + i, h)),
                  pl.BlockSpec((M, hd), lambda b, h, i: (b, h)),
                  pl.BlockSpec((M, hd), lambda b, h, i: (b, XATTN_HEADS + h)), vec, vec],
        out_specs=pl.BlockSpec((tq, hd), lambda b, h, i: (b * nq + i, h)),
        out_shape=jax.ShapeDtypeStruct((T, XATTN_WIDTH), MXU_DTYPE),
        compiler_params=_cparams("parallel", "parallel", "parallel"),
    )(cq, ckv, ckv, gq, gk)


def xattn_bwd(cq, ckv, gq, gk, dco, *, B, tq=512):
    T = cq.shape[0]
    S = T // B
    M = ckv.shape[0] // B
    tq = min(tq, S)
    nq = S // tq
    scale = XATTN_HEAD_DIM ** -0.5
    hd = XATTN_HEAD_DIM

    def body(q_ref, k_ref, v_ref, gq_ref, gk_ref, do_ref, dq_ref, dk_ref, dv_ref, dgq_ref, dgk_ref, dkn_acc, dv_acc):
        b, h, i = pl.program_id(0), pl.program_id(1), pl.program_id(2)

        @pl.when((b == 0) & (h == 0) & (i == 0))
        def _():
            dgq_ref[...] = jnp.zeros_like(dgq_ref)
            dgk_ref[...] = jnp.zeros_like(dgk_ref)

        @pl.when(i == 0)
        def _():
            dkn_acc[...] = jnp.zeros_like(dkn_acc)
            dv_acc[...] = jnp.zeros_like(dv_acc)

        q, k, v = q_ref[...], k_ref[...], v_ref[...]
        gqv, gkv = gq_ref[...], gk_ref[...]
        qn, rq = _head_rms(q, gqv)
        kn, rk = _head_rms(k, gkv)
        p = _softmax_rows(_dot(_mx(qn), _mx(kn), NT) * scale)
        do = do_ref[...]
        dv_acc[...] += _dot(_mx(p), _mx(do), TN)
        dp = _dot(_mx(do), _mx(v), NT)
        ds = p * (dp - jnp.sum(dp * p, axis=-1, keepdims=True)) * scale
        dqn = _dot(_mx(ds), _mx(kn), NN)
        dkn_acc[...] += _dot(_mx(ds), _mx(qn), TN)
        dq, dgq = _head_rms_bwd(q, rq, gqv, dqn)
        dq_ref[...] = dq.astype(dq_ref.dtype)
        dgq_ref[...] += dgq

        @pl.when(i == nq - 1)
        def _():
            dk, dgk = _head_rms_bwd(k, rk, gkv, dkn_acc[...])
            dk_ref[...] = dk.astype(dk_ref.dtype)
            dv_ref[...] = dv_acc[...].astype(dv_ref.dtype)
            dgk_ref[...] += dgk

    vec = pl.BlockSpec((1, hd), lambda b, h, i: (0, 0))
    qspec = pl.BlockSpec((tq, hd), lambda b, h, i: (b * nq + i, h))
    kspec = pl.BlockSpec((M, hd), lambda b, h, i: (b, h))
    vspec = pl.BlockSpec((M, hd), lambda b, h, i: (b, XATTN_HEADS + h))
    dq, dk, dv, dgq, dgk = pl.pallas_call(
        body, name="xattn_bwd", grid=(B, XATTN_HEADS, nq),
        in_specs=[qspec, kspec, vspec, vec, vec, qspec],
        out_specs=[qspec, kspec, kspec, vec, vec],
        out_shape=[jax.ShapeDtypeStruct((T, XATTN_WIDTH), MXU_DTYPE),
                   jax.ShapeDtypeStruct((B * M, XATTN_WIDTH), MXU_DTYPE),
                   jax.ShapeDtypeStruct((B * M, XATTN_WIDTH), MXU_DTYPE),
                   jax.ShapeDtypeStruct((1, hd), f32), jax.ShapeDtypeStruct((1, hd), f32)],
        scratch_shapes=[pltpu.VMEM((M, hd), f32), pltpu.VMEM((M, hd), f32)],
        compiler_params=_cparams("arbitrary", "arbitrary", "arbitrary"),
    )(cq, ckv, ckv, gq, gk, dco)
    return dq, jnp.concatenate([dk, dv], axis=1), dgq, dgk


FOX_PAIRS = FOX_HEADS // 2


def _fox_scores(qn, kn, ccol, crow, q0, tq, S, scale):
    s = _dot(_mx(qn), _mx(kn), NT) * scale + ccol - crow
    qpos = q0 + lax.broadcasted_iota(jnp.int32, (tq, S), 0)
    kpos = lax.broadcasted_iota(jnp.int32, (tq, S), 1)
    return jnp.where(kpos <= qpos, s, NEG_INF)


def fox_fwd(P, ccol, crow, gq, gk, go, *, B, tq=256):
    T = P.shape[0]
    S = T // B
    tq = min(tq, S)
    nq = S // tq
    hd = FOX_HEAD_DIM
    scale = hd ** -0.5

    def body(q_ref, k_ref, v_ref, ccol_ref, crow_ref, gq_ref, gk_ref, go_ref, o_ref, oa_ref):
        q0 = pl.program_id(2) * tq
        for e in range(2):
            sl = slice(e * hd, (e + 1) * hd)
            qn, _ = _head_rms(q_ref[:, sl], gq_ref[:, sl])
            kn, _ = _head_rms(k_ref[:, sl], gk_ref[:, sl])
            p = _softmax_rows(_fox_scores(qn, kn, ccol_ref[0, e], crow_ref[0, e], q0, tq, S, scale))
            o = _dot(_mx(p), _mx(v_ref[:, sl]), NN)
            o_ref[:, sl] = o
            oa_ref[:, sl] = _head_rms(o, go_ref[:, sl])[0].astype(oa_ref.dtype)

    W = 2 * hd
    vec = pl.BlockSpec((1, W), lambda b, h, i: (0, 0))
    ospec = pl.BlockSpec((tq, W), lambda b, h, i: (b * nq + i, h))
    return pl.pallas_call(
        body, name="fox_fwd", grid=(B, FOX_PAIRS, nq),
        in_specs=[pl.BlockSpec((tq, W), lambda b, h, i: (b * nq + i, h)),
                  pl.BlockSpec((S, W), lambda b, h, i: (b, FOX_PAIRS + h)),
                  pl.BlockSpec((S, W), lambda b, h, i: (b, 2 * FOX_PAIRS + h)),
                  pl.BlockSpec((1, 2, tq, 1), lambda b, h, i: (b, h, i, 0)),
                  pl.BlockSpec((1, 2, 1, S), lambda b, h, i: (b, h, 0, 0)), vec, vec, vec],
        out_specs=[ospec, ospec],
        out_shape=[jax.ShapeDtypeStruct((T, FOX_WIDTH), f32), jax.ShapeDtypeStruct((T, FOX_WIDTH), MXU_DTYPE)],
        compiler_params=_cparams("parallel", "parallel", "parallel"),
    )(P, P, P, ccol, crow, gq, gk, go)


def fox_bwd(P, ccol, crow, gq, gk, go, o_raw, d_oab, *, B, tq=256):
    T = P.shape[0]
    S = T // B
    tq = min(tq, S)
    nq = S // tq
    hd = FOX_HEAD_DIM
    scale = hd ** -0.5

    def body(q_ref, k_ref, v_ref, ccol_ref, crow_ref, gq_ref, gk_ref, go_ref, o_ref, doa_ref,
             dq_ref, dk_ref, dv_ref, dccol_ref, dcrow_ref, dgq_ref, dgk_ref, dgo_ref, dkn_acc, dv_acc, dcrow_acc):
        b, h, i = pl.program_id(0), pl.program_id(1), pl.program_id(2)
        q0 = i * tq

        @pl.when((b == 0) & (h == 0) & (i == 0))
        def _():
            dgq_ref[...] = jnp.zeros_like(dgq_ref)
            dgk_ref[...] = jnp.zeros_like(dgk_ref)
            dgo_ref[...] = jnp.zeros_like(dgo_ref)

        @pl.when(i == 0)
        def _():
            dkn_acc[...] = jnp.zeros_like(dkn_acc)
            dv_acc[...] = jnp.zeros_like(dv_acc)
            dcrow_acc[...] = jnp.zeros_like(dcrow_acc)

        for e in range(2):
            sl = slice(e * hd, (e + 1) * hd)
            q, k, v = q_ref[:, sl], k_ref[:, sl], v_ref[:, sl]
            gqv, gkv, gov = gq_ref[:, sl], gk_ref[:, sl], go_ref[:, sl]
            qn, rq = _head_rms(q, gqv)
            kn, rk = _head_rms(k, gkv)
            p = _softmax_rows(_fox_scores(qn, kn, ccol_ref[0, e], crow_ref[0, e], q0, tq, S, scale))
            o = o_ref[:, sl]
            ro = lax.rsqrt(jnp.mean(o * o, axis=-1, keepdims=True) + EPS)
            do, dgo = _head_rms_bwd(o, ro, gov, doa_ref[:, sl])
            dgo_ref[:, sl] += dgo
            dv_acc[e] += _dot(_mx(p), _mx(do), TN)
            dp = _dot(_mx(do), _mx(v), NT)
            ds = p * (dp - jnp.sum(do * o, axis=-1, keepdims=True))
            dccol_ref[0, e] = jnp.sum(ds, axis=1, keepdims=True)
            dcrow_acc[e] -= jnp.sum(ds, axis=0, keepdims=True)
            dqn = _dot(_mx(ds), _mx(kn), NN) * scale
            dkn_acc[e] += _dot(_mx(ds), _mx(qn), TN) * scale
            dq, dgq = _head_rms_bwd(q, rq, gqv, dqn)
            dq_ref[:, sl] = dq.astype(dq_ref.dtype)
            dgq_ref[:, sl] += dgq

        @pl.when(i == nq - 1)
        def _():
            for e in range(2):
                sl = slice(e * hd, (e + 1) * hd)
                k = k_ref[:, sl]
                gkv = gk_ref[:, sl]
                rk = lax.rsqrt(jnp.mean(k * k, axis=-1, keepdims=True) + EPS)
                dk, dgk = _head_rms_bwd(k, rk, gkv, dkn_acc[e])
                dk_ref[:, sl] = dk.astype(dk_ref.dtype)
                dv_ref[:, sl] = dv_acc[e].astype(dv_ref.dtype)
                dgk_ref[:, sl] += dgk
                dcrow_ref[0, e] = dcrow_acc[e]

    W = 2 * hd
    vec = pl.BlockSpec((1, W), lambda b, h, i: (0, 0))
    qspec = pl.BlockSpec((tq, W), lambda b, h, i: (b * nq + i, h))
    kvout = pl.BlockSpec((S, W), lambda b, h, i: (b, h))
    colspec = pl.BlockSpec((1, 2, tq, 1), lambda b, h, i: (b, h, i, 0))
    rowspec = pl.BlockSpec((1, 2, 1, S), lambda b, h, i: (b, h, 0, 0))
    return pl.pallas_call(
        body, name="fox_bwd", grid=(B, FOX_PAIRS, nq),
        in_specs=[qspec,
                  pl.BlockSpec((S, W), lambda b, h, i: (b, FOX_PAIRS + h)),
                  pl.BlockSpec((S, W), lambda b, h, i: (b, 2 * FOX_PAIRS + h)),
                  colspec, rowspec, vec, vec, vec, qspec, qspec],
        out_specs=[qspec, kvout, kvout, colspec, rowspec, vec, vec, vec],
        out_shape=[jax.ShapeDtypeStruct((T, FOX_WIDTH), MXU_DTYPE), jax.ShapeDtypeStruct((T, FOX_WIDTH), MXU_DTYPE),
                   jax.ShapeDtypeStruct((T, FOX_WIDTH), MXU_DTYPE),
                   jax.ShapeDtypeStruct((B, FOX_HEADS, S, 1), f32), jax.ShapeDtypeStruct((B, FOX_HEADS, 1, S), f32),
                   jax.ShapeDtypeStruct((1, W), f32), jax.ShapeDtypeStruct((1, W), f32), jax.ShapeDtypeStruct((1, W), f32)],
        scratch_shapes=[pltpu.VMEM((2, S, hd), f32), pltpu.VMEM((2, S, hd), f32), pltpu.VMEM((2, 1, S), f32)],
        compiler_params=_cparams("arbitrary", "arbitrary", "arbitrary"),
    )(P, P, P, ccol, crow, gq, gk, go, o_raw, d_oab)


FOX_TQ = 512
FOX_TK = 512
GROUP_PRECISION = lax.Precision.HIGH


def _head_mean(v):
    n = v.shape[1]
    r = lax.broadcasted_iota(jnp.int32, (n, n), 0) // FOX_HEAD_DIM
    c = lax.broadcasted_iota(jnp.int32, (n, n), 1) // FOX_HEAD_DIM
    return _dot(v, (r == c).astype(f32), NN, GROUP_PRECISION) * (1.0 / FOX_HEAD_DIM)


def fox_prep_fwd(P, gq, gk, *, tr=512):
    T = P.shape[0]
    tr = min(tr, T)
    scale = FOX_HEAD_DIM ** -0.5

    def body(q_ref, k_ref, v_ref, gq_ref, gk_ref, qn_ref, kn_ref, vb_ref):
        q, k = q_ref[...], k_ref[...]
        qn_ref[...] = (q * lax.rsqrt(_head_mean(q * q) + EPS) * (gq_ref[...] * scale)).astype(qn_ref.dtype)
        kn_ref[...] = (k * lax.rsqrt(_head_mean(k * k) + EPS) * gk_ref[...]).astype(kn_ref.dtype)
        vb_ref[...] = v_ref[...].astype(vb_ref.dtype)

    W = FOX_WIDTH
    col = lambda j: pl.BlockSpec((tr, W), lambda i: (i, j))
    vec = pl.BlockSpec((1, W), lambda i: (0, 0))
    out = jax.ShapeDtypeStruct((T, W), MXU_DTYPE)
    return pl.pallas_call(
        body, name="fox_prep_fwd", grid=(T // tr,), in_specs=[col(0), col(1), col(2), vec, vec],
        out_specs=[col(0)] * 3, out_shape=[out] * 3, compiler_params=_cparams("parallel"),
    )(P, P, P, gq, gk)


def fox_prep_bwd(P, gq, gk, dqn, dkn, *, tr=512):
    T = P.shape[0]
    tr = min(tr, T)
    scale = FOX_HEAD_DIM ** -0.5

    def body(q_ref, k_ref, gq_ref, gk_ref, dqn_ref, dkn_ref, dq_ref, dk_ref, dgq_ref, dgk_ref):
        @pl.when(pl.program_id(0) == 0)
        def _():
            dgq_ref[...] = jnp.zeros_like(dgq_ref)
            dgk_ref[...] = jnp.zeros_like(dgk_ref)

        def one(x, g, dn, dx_ref, dg_ref):
            r = lax.rsqrt(_head_mean(x * x) + EPS)
            xhat = x * r
            gd = dn * g
            dx_ref[...] = (r * (gd - xhat * _head_mean(gd * xhat))).astype(dx_ref.dtype)
            return jnp.sum(dn * xhat, axis=0, keepdims=True)

        dgq_ref[...] += scale * one(q_ref[...], gq_ref[...] * scale, dqn_ref[...], dq_ref, dgq_ref)
        dgk_ref[...] += one(k_ref[...], gk_ref[...], dkn_ref[...], dk_ref, dgk_ref)

    W = FOX_WIDTH
    col = lambda j: pl.BlockSpec((tr, W), lambda i: (i, j))
    vec = pl.BlockSpec((1, W), lambda i: (0, 0))
    return pl.pallas_call(
        body, name="fox_prep_bwd", grid=(T // tr,), in_specs=[col(0), col(1), vec, vec, col(0), col(0)],
        out_specs=[col(0), col(0), vec, vec],
        out_shape=[jax.ShapeDtypeStruct((T, W), MXU_DTYPE), jax.ShapeDtypeStruct((T, W), MXU_DTYPE),
                   jax.ShapeDtypeStruct((1, W), f32), jax.ShapeDtypeStruct((1, W), f32)],
        compiler_params=_cparams("arbitrary"),
    )(P, P, gq, gk, dqn, dkn)


def _fox_tile_scores(q, k_ref, ccol_ref, cq, e, j, sl, mask_off):
    tq, tk = FOX_TQ, FOX_TK
    rows = pl.ds(pl.multiple_of(j * tk, tk), tk)
    k = k_ref[rows, sl]
    s = _dot(k, q, NT) + cq - ccol_ref[0, e, rows, :]
    if mask_off is not None:
        key = lax.broadcasted_iota(jnp.int32, (tk, tq), 0) + mask_off
        query = lax.broadcasted_iota(jnp.int32, (tk, tq), 1)
        s = jnp.where(key <= query, s, NEG_INF)
    return s, k, rows


def _fox_sweep(i, update, carry):
    nd = FOX_TQ // FOX_TK
    carry = lax.fori_loop(0, i * nd, lambda j, cr: update(cr, j, None), carry)
    for d in range(nd):
        carry = update(carry, i * nd + d, d * FOX_TK)
    return carry


def fox_core_fwd(qn, kn, vb, ccol, crow, go, *, B):
    T = qn.shape[0]
    S = T // B
    tq = FOX_TQ
    nq = S // tq
    hd = FOX_HEAD_DIM

    def body(q_ref, k_ref, v_ref, ccol_ref, crow_ref, go_ref, o_ref, oa_ref, lse_ref):
        i = pl.program_id(2)
        for e in range(2):
            sl = slice(e * hd, (e + 1) * hd)
            q = q_ref[:, sl]
            cq = crow_ref[0, e, i]

            def update(carry, j, mask_off):
                m, l, acc = carry
                s, _, rows = _fox_tile_scores(q, k_ref, ccol_ref, cq, e, j, sl, mask_off)
                m2 = jnp.maximum(m, jnp.max(s, axis=0, keepdims=True))
                a = jnp.exp(m - m2)
                p = jnp.exp(s - m2)
                return m2, a * l + jnp.sum(p, axis=0, keepdims=True), a * acc + _dot(v_ref[rows, sl], _mx(p), TN)

            carry = (jnp.full((1, tq), NEG_INF, f32), jnp.zeros((1, tq), f32), jnp.zeros((hd, tq), f32))
            m, l, acc = _fox_sweep(i, update, carry)
            o = (acc / l).T
            o_ref[:, sl] = o
            oa_ref[:, sl] = _head_rms(o, go_ref[:, sl])[0].astype(oa_ref.dtype)
            lse_ref[0, e, 0] = m + jnp.log(l)

    W = 2 * hd
    qspec = pl.BlockSpec((tq, W), lambda b, h, i: (b * nq + i, h))
    kspec = pl.BlockSpec((S, W), lambda b, h, i: (b, h))
    return pl.pallas_call(
        body, name="fox_core_fwd", grid=(B, FOX_PAIRS, nq),
        in_specs=[qspec, kspec, kspec, pl.BlockSpec((1, 2, S, 1), lambda b, h, i: (b, h, 0, 0)),
                  pl.BlockSpec((1, 2, nq, 1, tq), lambda b, h, i: (b, h, 0, 0, 0)),
                  pl.BlockSpec((1, W), lambda b, h, i: (0, 0))],
        out_specs=[qspec, qspec, pl.BlockSpec((1, 2, 1, 1, tq), lambda b, h, i: (b, h, i, 0, 0))],
        out_shape=[jax.ShapeDtypeStruct((T, FOX_WIDTH), f32), jax.ShapeDtypeStruct((T, FOX_WIDTH), MXU_DTYPE),
                   jax.ShapeDtypeStruct((B, FOX_HEADS, nq, 1, tq), f32)],
        compiler_params=_cparams("parallel", "parallel", "parallel"),
    )(qn, kn, vb, ccol, crow, go)


def fox_core_bwd(qn, kn, vb, ccol, crow, go, o_raw, lse, d_oab, *, B):
    T = qn.shape[0]
    S = T // B
    tq = FOX_TQ
    nq = S // tq
    hd = FOX_HEAD_DIM

    def body(q_ref, k_ref, v_ref, ccol_ref, crow_ref, go_ref, o_ref, lse_ref, doa_ref,
             dq_ref, dk_ref, dv_ref, dccol_ref, dcrow_ref, dgo_ref, dk_acc, dv_acc, dck_acc):
        b, h, i = pl.program_id(0), pl.program_id(1), pl.program_id(2)

        @pl.when((b == 0) & (h == 0) & (i == 0))
        def _():
            dgo_ref[...] = jnp.zeros_like(dgo_ref)

        @pl.when(i == 0)
        def _():
            dk_acc[...] = jnp.zeros_like(dk_acc)
            dv_acc[...] = jnp.zeros_like(dv_acc)
            dck_acc[...] = jnp.zeros_like(dck_acc)

        for e in range(2):
            sl = slice(e * hd, (e + 1) * hd)
            q = q_ref[:, sl]
            cq = crow_ref[0, e, i]
            lse_e = lse_ref[0, e, 0]
            o = o_ref[:, sl]
            ro = lax.rsqrt(jnp.mean(o * o, axis=-1, keepdims=True) + EPS)
            do, dgo = _head_rms_bwd(o, ro, go_ref[:, sl], doa_ref[:, sl])
            dgo_ref[:, sl] += dgo
            delta = jnp.sum((do * o).T, axis=0, keepdims=True)
            do_b = _mx(do)

            def update(carry, j, mask_off):
                dq, dcq = carry
                s, k, rows = _fox_tile_scores(q, k_ref, ccol_ref, cq, e, j, sl, mask_off)
                p = jnp.exp(s - lse_e)
                dv_acc[e, rows, :] += _dot(_mx(p), do_b, NN)
                ds = p * (_dot(v_ref[rows, sl], do_b, NT) - delta)
                dck_acc[e, rows, :] -= jnp.sum(ds, axis=1, keepdims=True)
                ds_b = _mx(ds)
                dk_acc[e, rows, :] += _dot(ds_b, q, NN)
                return dq + _dot(ds_b, k, TN), dcq + jnp.sum(ds, axis=0, keepdims=True)

            dq, dcq = _fox_sweep(i, update, (jnp.zeros((tq, hd), f32), jnp.zeros((1, tq), f32)))
            dq_ref[:, sl] = dq
            dcrow_ref[0, e, 0] = dcq

        @pl.when(i == nq - 1)
        def _():
            for e in range(2):
                sl = slice(e * hd, (e + 1) * hd)
                dk_ref[:, sl] = dk_acc[e]
                dv_ref[:, sl] = dv_acc[e].astype(dv_ref.dtype)
            dccol_ref[0] = dck_acc[...]

    W = 2 * hd
    qspec = pl.BlockSpec((tq, W), lambda b, h, i: (b * nq + i, h))
    kspec = pl.BlockSpec((S, W), lambda b, h, i: (b, h))
    colspec = pl.BlockSpec((1, 2, S, 1), lambda b, h, i: (b, h, 0, 0))
    rowspec = pl.BlockSpec((1, 2, nq, 1, tq), lambda b, h, i: (b, h, 0, 0, 0))
    tilespec = pl.BlockSpec((1, 2, 1, 1, tq), lambda b, h, i: (b, h, i, 0, 0))
    vec = pl.BlockSpec((1, W), lambda b, h, i: (0, 0))
    return pl.pallas_call(
        body, name="fox_core_bwd", grid=(B, FOX_PAIRS, nq),
        in_specs=[qspec, kspec, kspec, colspec, rowspec, vec, qspec, tilespec, qspec],
        out_specs=[qspec, kspec, kspec, colspec, tilespec, vec],
        out_shape=[jax.ShapeDtypeStruct((T, FOX_WIDTH), f32), jax.ShapeDtypeStruct((T, FOX_WIDTH), f32),
                   jax.ShapeDtypeStruct((T, FOX_WIDTH), MXU_DTYPE),
                   jax.ShapeDtypeStruct((B, FOX_HEADS, S, 1), f32), jax.ShapeDtypeStruct((B, FOX_HEADS, nq, 1, tq), f32),
                   jax.ShapeDtypeStruct((1, W), f32)],
        scratch_shapes=[pltpu.VMEM((2, S, hd), f32), pltpu.VMEM((2, S, hd), f32), pltpu.VMEM((2, S, 1), f32)],
        compiler_params=_cparams("arbitrary", "arbitrary", "arbitrary"),
    )(qn, kn, vb, ccol, crow, go, o_raw, lse, d_oab)


def _lane_mask(lo, hi, shape):
    lane = lax.broadcasted_iota(jnp.int32, shape, 1)
    return (lane >= lo) & (lane < hi)


def _cumsum_rows(v, period, reverse=False):
    n = v.shape[0]
    pos = lax.broadcasted_iota(jnp.int32, v.shape, 0) % period
    sh = 1
    while sh < period:
        if reverse:
            v = v + jnp.where(pos + sh < period, pltpu.roll(v, n - sh, 0), 0.0)
        else:
            v = v + jnp.where(pos >= sh, pltpu.roll(v, sh, 0), 0.0)
        sh *= 2
    return v


def _gate_values(z, bias, alog):
    zb = z + bias
    ls = jax.nn.log_sigmoid(zb)
    beta = jax.nn.sigmoid(z)
    g = -jnp.exp(alog) * jax.nn.softplus(zb)
    return zb, ls, beta, g


def gates_fwd(P, bias, alog, *, B):
    T = P.shape[0]
    S = T // B

    def body(z_ref, bias_ref, alog_ref, o_ref):
        z = z_ref[...]
        _, ls, beta, g = _gate_values(z, bias_ref[...], alog_ref[...])
        c = _cumsum_rows(ls, S)
        gc = _cumsum_rows(g, GDN_CHUNK)
        o = jnp.where(_lane_mask(SM_F, SM_F + FOX_HEADS, z.shape), c, 0.0)
        o = jnp.where(_lane_mask(SM_B, SM_B + GDN_HEADS, z.shape), beta, o)
        o = jnp.where(_lane_mask(SM_A, SM_A + GDN_HEADS, z.shape), gc, o)
        o_ref[...] = o

    vec = pl.BlockSpec((1, LANES), lambda b: (0, 0))
    return pl.pallas_call(
        body, name="gates_fwd", grid=(B,),
        in_specs=[pl.BlockSpec((S, LANES), lambda b: (b, COL_SMALL // LANES)), vec, vec],
        out_specs=pl.BlockSpec((S, LANES), lambda b: (b, 0)),
        out_shape=jax.ShapeDtypeStruct((T, LANES), f32),
        compiler_params=_cparams("parallel"),
    )(P, bias, alog)


def gates_bwd(P, bias, alog, dgates, *, B):
    T = P.shape[0]
    S = T // B

    def body(z_ref, bias_ref, alog_ref, dg_ref, dz_ref, par_ref):
        z = z_ref[...]
        zb, ls, beta, g = _gate_values(z, bias_ref[...], alog_ref[...])
        d = dg_ref[...]
        dls = _cumsum_rows(d, S, reverse=True)
        dgr = _cumsum_rows(d, GDN_CHUNK, reverse=True)
        sig = jax.nn.sigmoid(zb)
        dz_f = dls * (1.0 - sig)
        dz_b = d * beta * (1.0 - beta)
        dz_a = dgr * (-jnp.exp(alog_ref[...])) * sig
        dz = jnp.where(_lane_mask(SM_F, SM_F + FOX_HEADS, z.shape), dz_f, 0.0)
        dz = jnp.where(_lane_mask(SM_B, SM_B + GDN_HEADS, z.shape), dz_b, dz)
        dz = jnp.where(_lane_mask(SM_A, SM_A + GDN_HEADS, z.shape), dz_a, dz)
        dz_ref[...] = dz.astype(dz_ref.dtype)

        @pl.when(pl.program_id(0) == 0)
        def _():
            par_ref[...] = jnp.zeros_like(par_ref)

        dalog = jnp.where(_lane_mask(SM_A, SM_A + GDN_HEADS, z.shape), dgr * g, 0.0)
        par_ref[0:1, :] += jnp.sum(dz, axis=0, keepdims=True)
        par_ref[1:2, :] += jnp.sum(dalog, axis=0, keepdims=True)

    vec = pl.BlockSpec((1, LANES), lambda b: (0, 0))
    return pl.pallas_call(
        body, name="gates_bwd", grid=(B,),
        in_specs=[pl.BlockSpec((S, LANES), lambda b: (b, COL_SMALL // LANES)), vec, vec,
                  pl.BlockSpec((S, LANES), lambda b: (b, 0))],
        out_specs=[pl.BlockSpec((S, LANES), lambda b: (b, 0)), pl.BlockSpec((8, LANES), lambda b: (0, 0))],
        out_shape=[jax.ShapeDtypeStruct((T, LANES), MXU_DTYPE), jax.ShapeDtypeStruct((8, LANES), f32)],
        compiler_params=_cparams("arbitrary"),
    )(P, bias, alog, dgates)


GDN_BLOCKS = 3 * GDN_HEADS


def _shift_rows(v, d, reverse=False):
    if d == 0:
        return v
    n = v.shape[0]
    row = lax.broadcasted_iota(jnp.int32, v.shape, 0)
    if reverse:
        return jnp.where(row + d < n, pltpu.roll(v, n - d, 0), 0.0)
    return jnp.where(row >= d, pltpu.roll(v, d, 0), 0.0)


def _conv_silu(x, w):
    pre = sum(w[j:j + 1, :] * _shift_rows(x, CONV_WIDTH - 1 - j) for j in range(CONV_WIDTH))
    return pre, pre * jax.nn.sigmoid(pre)


def gdn_prep_fwd(P, conv_w, *, B):
    T = P.shape[0]
    S = T // B

    def body(x_ref, w_ref, o_ref):
        _, y = _conv_silu(x_ref[...], w_ref[...])
        yn = y * lax.rsqrt(jnp.sum(y * y, axis=-1, keepdims=True) + EPS)
        o_ref[...] = jnp.where(pl.program_id(1) < 2 * GDN_HEADS, yn, y)

    return pl.pallas_call(
        body, name="gdn_prep_fwd", grid=(B, GDN_BLOCKS),
        in_specs=[pl.BlockSpec((S, LANES), lambda b, j: (b, COL_GDN // LANES + j)),
                  pl.BlockSpec((CONV_WIDTH, LANES), lambda b, j: (0, j))],
        out_specs=pl.BlockSpec((S, LANES), lambda b, j: (b, j)),
        out_shape=jax.ShapeDtypeStruct((T, 3 * GDN_WIDTH), f32),
        compiler_params=_cparams("parallel", "parallel"),
    )(P, conv_w)


def gdn_prep_bwd(P, conv_w, dG, *, B):
    T = P.shape[0]
    S = T // B

    def body(x_ref, w_ref, dg_ref, dx_ref, dw_ref):
        x, w = x_ref[...], w_ref[...]
        pre, y = _conv_silu(x, w)
        dn = dg_ref[...]
        r = lax.rsqrt(jnp.sum(y * y, axis=-1, keepdims=True) + EPS)
        n = y * r
        dy_norm = r * (dn - n * jnp.sum(dn * n, axis=-1, keepdims=True))
        dy = jnp.where(pl.program_id(0) < 2 * GDN_HEADS, dy_norm, dn)
        sg = jax.nn.sigmoid(pre)
        dpre = dy * (sg * (1.0 + pre * (1.0 - sg)))
        dx = sum(w[j:j + 1, :] * _shift_rows(dpre, CONV_WIDTH - 1 - j, reverse=True) for j in range(CONV_WIDTH))
        dx_ref[...] = dx.astype(dx_ref.dtype)

        @pl.when(pl.program_id(1) == 0)
        def _():
            dw_ref[...] = jnp.zeros_like(dw_ref)

        for j in range(CONV_WIDTH):
            dw_ref[j:j + 1, :] += jnp.sum(dpre * _shift_rows(x, CONV_WIDTH - 1 - j), axis=0, keepdims=True)

    return pl.pallas_call(
        body, name="gdn_prep_bwd", grid=(GDN_BLOCKS, B),
        in_specs=[pl.BlockSpec((S, LANES), lambda j, b: (b, COL_GDN // LANES + j)),
                  pl.BlockSpec((CONV_WIDTH, LANES), lambda j, b: (0, j)),
                  pl.BlockSpec((S, LANES), lambda j, b: (b, j))],
        out_specs=[pl.BlockSpec((S, LANES), lambda j, b: (b, j)),
                   pl.BlockSpec((CONV_WIDTH, LANES), lambda j, b: (0, j))],
        out_shape=[jax.ShapeDtypeStruct((T, 3 * GDN_WIDTH), MXU_DTYPE),
                   jax.ShapeDtypeStruct((CONV_WIDTH, 3 * GDN_WIDTH), f32)],
        compiler_params=_cparams("arbitrary", "arbitrary"),
    )(P, conv_w, dG)


GDN_GROUP = 16
B_NN = (((2,), (1,)), ((0,), (0,)))
B_NT = (((2,), (2,)), ((0,), (0,)))
B_TN = (((1,), (1,)), ((0,), (0,)))


def _bmm(a, b, dims, precision=None):
    if precision is None:
        a, b = _mx(a), _mx(b)
    return lax.dot_general(a, b, dims, preferred_element_type=f32, precision=precision)


def _tri_inverse(A):
    C = A.shape[-1]
    row = lax.broadcasted_iota(jnp.int32, A.shape, 1)
    col = lax.broadcasted_iota(jnp.int32, A.shape, 2)
    eye = (row == col).astype(f32)
    X = jnp.where((row // 4) == (col // 4), -A, 0.0)
    X2 = _bmm(X, X, B_NN, INV_PRECISION)
    Tm = eye + X + X2 + _bmm(X, X2, B_NN, INV_PRECISION)
    b = 4
    while b < C:
        off = ((row // (2 * b)) == (col // (2 * b))) & ((row // b) != (col // b))
        Tm = Tm - _bmm(_bmm(Tm, jnp.where(off, A, 0.0), B_NN, INV_PRECISION), Tm, B_NN, INV_PRECISION)
        b *= 2
    return Tm


def _pick_lane(block, lane_idx):
    lane = lax.broadcasted_iota(jnp.int32, block.shape, 1)
    return jnp.sum(jnp.where(lane == lane_idx, block, 0.0), axis=1, keepdims=True)


def _gdn_local(q, k, v, beta, gc, Tm=None):
    C = GDN_CHUNK
    n = q.shape[0] // C
    q = q.reshape(n, C, -1) * (GDN_HEAD_DIM ** -0.5)
    k = k.reshape(n, C, -1)
    v = v.reshape(n, C, -1)
    beta = beta.reshape(n, C, 1)
    gc = gc.reshape(n, C, 1)
    row = lax.broadcasted_iota(jnp.int32, (n, C, C), 1)
    col = lax.broadcasted_iota(jnp.int32, (n, C, C), 2)
    gcT = jnp.swapaxes(jnp.broadcast_to(gc, (n, C, C)), 1, 2)
    D = jnp.exp(jnp.where(row >= col, gc - gcT, NEG_INF))
    kb = k * beta
    vb = v * beta
    A = jnp.where(row > col, _bmm(kb, k, B_NT) * D, 0.0)
    Gam = jnp.exp(gc)
    kg = kb * Gam
    gl = gc[:, C - 1:C, :]
    kdec = jnp.exp(gl - gc)
    loc = dict(q=q, k=k, v=v, beta=beta, gc=gc, D=D, kb=kb, vb=vb, A=A, Gam=Gam, kg=kg,
               kdec=kdec, kd=k * kdec, qg=q * Gam, gam=jnp.exp(gl), row=row, col=col)
    if Tm is None:
        Tm = _tri_inverse(A)
        loc.update(u=_bmm(Tm, vb, B_NN), w=_bmm(Tm, kg, B_NN), M=_bmm(q, k, B_NT) * D)
    else:
        Tm = Tm.reshape(n, C, C)
    loc["Tm"] = Tm
    return loc


def _gdn_store_local(loc, r0, u_s, w_s, qg_s, kd_s, M_s, gam_s, c0):
    n = loc["u"].shape[0]
    R = n * GDN_CHUNK
    u_s[pl.ds(r0, R), :] = loc["u"].reshape(R, -1)
    w_s[pl.ds(r0, R), :] = loc["w"].reshape(R, -1)
    qg_s[pl.ds(r0, R), :] = loc["qg"].reshape(R, -1)
    kd_s[pl.ds(r0, R), :] = loc["kd"].reshape(R, -1)
    M_s[pl.ds(r0, R), :] = loc["M"].reshape(R, -1)
    gam_s[pl.ds(c0, n)] = jnp.broadcast_to(loc["gam"], (n, 1, LANES))


def _gdn_specs(S):
    blk = lambda off: pl.BlockSpec((S, LANES), lambda b, h: (b, off + h))
    return blk


def gdn_fwd(G, gates, P, g_on, *, B):
    T = G.shape[0]
    S = T // B
    C = GDN_CHUNK
    N = S // C
    grp = min(GDN_GROUP, N)
    R = grp * C
    hd = GDN_HEAD_DIM

    def body(q_ref, k_ref, v_ref, gt_ref, z_ref, gon_ref, o_ref, ob_ref, st_ref, u_s, w_s, qg_s, kd_s, M_s, gam_s):
        h = pl.program_id(1)

        def local(gi, carry):
            r0 = pl.multiple_of(gi * R, R)
            gt = gt_ref[pl.ds(r0, R), :]
            loc = _gdn_local(q_ref[pl.ds(r0, R), :], k_ref[pl.ds(r0, R), :], v_ref[pl.ds(r0, R), :],
                             _pick_lane(gt, SM_B + h), _pick_lane(gt, SM_A + h))
            _gdn_store_local(loc, r0, u_s, w_s, qg_s, kd_s, M_s, gam_s, gi * grp)
            return carry

        lax.fori_loop(0, N // grp, local, 0)

        def step(n, state):
            r0 = pl.multiple_of(n * C, C)
            st_ref[0, 0, n] = state
            v_new = u_s[pl.ds(r0, C), :] - _dotm(w_s[pl.ds(r0, C), :], state, NN)
            o_ref[pl.ds(r0, C), :] = (_dotm(qg_s[pl.ds(r0, C), :], state, NN)
                                      + _dotm(M_s[pl.ds(r0, C), :], v_new, NN))
            return state * gam_s[n] + _dotm(kd_s[pl.ds(r0, C), :], v_new, TN)

        lax.fori_loop(0, N, step, jnp.zeros((hd, hd), f32))
        o = o_ref[...]
        z = z_ref[...]
        ob_ref[...] = (_head_rms(o, gon_ref[...])[0] * (z * jax.nn.sigmoid(z))).astype(ob_ref.dtype)

    blk = lambda off: pl.BlockSpec((S, LANES), lambda b, h: (b, off + h))
    rows = lambda: pltpu.VMEM((S, hd), f32)
    return pl.pallas_call(
        body, name="gdn_fwd", grid=(B, GDN_HEADS),
        in_specs=[blk(0), blk(GDN_HEADS), blk(2 * GDN_HEADS), pl.BlockSpec((S, LANES), lambda b, h: (b, 0)),
                  blk(COL_Z // LANES), pl.BlockSpec((1, hd), lambda b, h: (0, 0))],
        out_specs=[blk(0), blk(0), pl.BlockSpec((1, 1, N, hd, hd), lambda b, h: (b, h, 0, 0, 0))],
        out_shape=[jax.ShapeDtypeStruct((T, GDN_WIDTH), f32), jax.ShapeDtypeStruct((T, GDN_WIDTH), MXU_DTYPE),
                   jax.ShapeDtypeStruct((B, GDN_HEADS, N, hd, hd), f32)],
        scratch_shapes=[rows(), rows(), rows(), rows(), pltpu.VMEM((S, C), f32), pltpu.VMEM((N, 1, LANES), f32)],
        compiler_params=_cparams("parallel", "parallel"),
    )(G, G, G, gates, P, g_on)


def gdn_bwd(G, gates, P, g_on, o_raw, states, d_oab, *, B):
    T = G.shape[0]
    S = T // B
    C = GDN_CHUNK
    N = S // C
    grp = min(GDN_GROUP, N)
    R = grp * C
    hd = GDN_HEAD_DIM

    def body(q_ref, k_ref, v_ref, gt_ref, z_ref, gon_ref, o_ref, st_ref, dob_ref,
             dq_ref, dk_ref, dv_ref, dgt_ref, dz_ref, dgon_ref,
             u_s, w_s, qg_s, kd_s, M_s, gam_s, do_s, du_s, dw_s, dqg_s, dkd_s, dM_s, dgl_s, Tm_s):
        b, h = pl.program_id(0), pl.program_id(1)

        @pl.when((b == 0) & (h == 0))
        def _():
            dgon_ref[...] = jnp.zeros_like(dgon_ref)

        @pl.when(h == 0)
        def _():
            dgt_ref[...] = jnp.zeros_like(dgt_ref)

        def group_inputs(gi, Tm_of=None):
            r0 = pl.multiple_of(gi * R, R)
            gt = gt_ref[pl.ds(r0, R), :]
            Tm = None if Tm_of is None else Tm_of[pl.ds(r0, R), :]
            return r0, _gdn_local(q_ref[pl.ds(r0, R), :], k_ref[pl.ds(r0, R), :], v_ref[pl.ds(r0, R), :],
                                  _pick_lane(gt, SM_B + h), _pick_lane(gt, SM_A + h), Tm)

        def local(gi, carry):
            r0, loc = group_inputs(gi)
            _gdn_store_local(loc, r0, u_s, w_s, qg_s, kd_s, M_s, gam_s, gi * grp)
            Tm_s[pl.ds(r0, R), :] = loc["Tm"].reshape(R, C)
            o, z, gon = o_ref[pl.ds(r0, R), :], z_ref[pl.ds(r0, R), :], gon_ref[...]
            dob = dob_ref[pl.ds(r0, R), :]
            on, ro = _head_rms(o, gon)
            sz = jax.nn.sigmoid(z)
            dz_ref[pl.ds(r0, R), :] = (dob * on * (sz * (1.0 + z * (1.0 - sz)))).astype(dz_ref.dtype)
            do, dgon = _head_rms_bwd(o, ro, gon, dob * (z * sz))
            do_s[pl.ds(r0, R), :] = do
            dgon_ref[...] += dgon
            return carry

        lax.fori_loop(0, N // grp, local, 0)

        def step(t, dS):
            n = N - 1 - t
            r0 = pl.multiple_of(n * C, C)
            rows = pl.ds(r0, C)
            state = st_ref[0, 0, n]
            w_n, M_n, kd_n, do_n = w_s[rows, :], M_s[rows, :], kd_s[rows, :], do_s[rows, :]
            v_new = u_s[rows, :] - _dotm(w_n, state, NN)
            dv_new = _dotm(M_n, do_n, TN) + _dotm(kd_n, dS, NN)
            du_s[rows, :] = dv_new
            dw_s[rows, :] = -_dotm(dv_new, state, NT)
            dqg_s[rows, :] = _dotm(do_n, state, NT)
            dM_s[rows, :] = _dotm(do_n, v_new, NT)
            dkd_s[rows, :] = _dotm(v_new, dS, NT)
            gam = gam_s[n]
            dgl_s[n] = jnp.broadcast_to(jnp.sum(jnp.sum(dS * state, axis=1, keepdims=True), axis=0, keepdims=True), (1, LANES)) * gam
            return dS * gam + _dotm(qg_s[rows, :], do_n, TN) - _dotm(w_n, dv_new, TN)

        lax.fori_loop(0, N, step, jnp.zeros((hd, hd), f32))

        def finish(gi, carry):
            r0, L = group_inputs(gi, Tm_s)
            n = grp
            rows = pl.ds(r0, R)
            g3 = lambda ref: ref[rows, :].reshape(n, C, -1)
            du, dw, dqg, dkd, dM = g3(du_s), g3(dw_s), g3(dqg_s), g3(dkd_s), g3(dM_s)
            L["M"] = g3(M_s)
            TmT = jnp.swapaxes(L["Tm"], 1, 2)
            dTm = _bmm(du, L["vb"], B_NT) + _bmm(dw, L["kg"], B_NT)
            dvb = _bmm(TmT, du, B_NN)
            dkg = _bmm(TmT, dw, B_NN)
            dA = jnp.where(L["row"] > L["col"], -_bmm(_bmm(TmT, dTm, B_NN), TmT, B_NN), 0.0)
            dKK = dA * L["D"]
            dQK = dM * L["D"]
            dkb = _bmm(dKK, L["k"], B_NN) + dkg * L["Gam"]
            dk = (_bmm(dKK, L["kb"], B_TN) + _bmm(dQK, L["q"], B_TN) + dkd * L["kdec"] + L["beta"] * dkb)
            dq = (_bmm(dQK, L["k"], B_NN) + dqg * L["Gam"]) * (GDN_HEAD_DIM ** -0.5)
            E = dA * L["A"] + dM * L["M"]
            r = jnp.sum(dkd * L["kd"], axis=-1, keepdims=True)
            dgc = (jnp.sum(E, axis=2, keepdims=True) - jnp.sum(jnp.swapaxes(E, 1, 2), axis=2, keepdims=True)
                   + jnp.sum(dkg * L["kg"], axis=-1, keepdims=True) + jnp.sum(dqg * L["qg"], axis=-1, keepdims=True) - r)
            dgl = jnp.sum(r, axis=1, keepdims=True) + dgl_s[pl.ds(gi * n, n)][:, :, 0:1]
            rowc = lax.broadcasted_iota(jnp.int32, (n, C, 1), 1)
            dgc = dgc + jnp.where(rowc == C - 1, dgl, 0.0)
            dbeta = jnp.sum(dkb * L["k"], axis=-1, keepdims=True) + jnp.sum(dvb * L["v"], axis=-1, keepdims=True)
            dq_ref[rows, :] = dq.reshape(R, hd)
            dk_ref[rows, :] = dk.reshape(R, hd)
            dv_ref[rows, :] = (L["beta"] * dvb).reshape(R, hd)
            lane = lax.broadcasted_iota(jnp.int32, (R, LANES), 1)
            dgt_ref[rows, :] += (jnp.where(lane == SM_B + h, dbeta.reshape(R, 1), 0.0)
                                 + jnp.where(lane == SM_A + h, dgc.reshape(R, 1), 0.0))
            return carry

        lax.fori_loop(0, N // grp, finish, 0)

    blk = lambda off: pl.BlockSpec((S, LANES), lambda b, h: (b, off + h))
    rows = lambda: pltpu.VMEM((S, hd), f32)
    return pl.pallas_call(
        body, name="gdn_bwd", grid=(B, GDN_HEADS),
        in_specs=[blk(0), blk(GDN_HEADS), blk(2 * GDN_HEADS), pl.BlockSpec((S, LANES), lambda b, h: (b, 0)),
                  blk(COL_Z // LANES), pl.BlockSpec((1, hd), lambda b, h: (0, 0)), blk(0),
                  pl.BlockSpec((1, 1, N, hd, hd), lambda b, h: (b, h, 0, 0, 0)), blk(GDN_HEADS)],
        out_specs=[blk(0), blk(0), blk(0), pl.BlockSpec((S, LANES), lambda b, h: (b, 0)), blk(0),
                   pl.BlockSpec((1, hd), lambda b, h: (0, 0))],
        out_shape=[jax.ShapeDtypeStruct((T, GDN_WIDTH), f32), jax.ShapeDtypeStruct((T, GDN_WIDTH), f32),
                   jax.ShapeDtypeStruct((T, GDN_WIDTH), f32), jax.ShapeDtypeStruct((T, LANES), f32),
                   jax.ShapeDtypeStruct((T, GDN_WIDTH), MXU_DTYPE), jax.ShapeDtypeStruct((1, hd), f32)],
        scratch_shapes=[rows(), rows(), rows(), rows(), pltpu.VMEM((S, C), f32), pltpu.VMEM((N, 1, LANES), f32),
                        rows(), rows(), rows(), rows(), rows(), pltpu.VMEM((S, C), f32), pltpu.VMEM((N, 1, LANES), f32),
                        pltpu.VMEM((S, C), f32)],
        compiler_params=_cparams("arbitrary", "arbitrary"),
    )(G, G, G, gates, P, g_on, o_raw, states, d_oab)


IN_SPLIT = (0, 1536, 1544, 3080, 3088, 3600)


def align_w_in(w):
    s = IN_SPLIT
    pad = jnp.zeros((w.shape[0], IN_ALIGNED - IN_DIM), w.dtype)
    return jnp.concatenate([w[:, s[0]:s[1]], w[:, s[2]:s[3]], w[:, s[4]:s[5]], w[:, s[1]:s[2]], w[:, s[3]:s[4]], pad], axis=1)


def unalign_w_in(wa):
    return jnp.concatenate([wa[:, 0:1536], wa[:, COL_SMALL:COL_SMALL + 8], wa[:, 1536:3072],
                            wa[:, COL_SMALL + 8:COL_SMALL + 16], wa[:, 3072:3584]], axis=1)


def _lanes_vec(pieces):
    v = jnp.zeros((1, LANES), f32)
    for off, a in pieces:
        v = lax.dynamic_update_slice(v, a.astype(f32), (0, off))
    return v


def local_step(x, mem, target, w, sp, *, B):
    T = x.shape[0]
    S = T // B
    gq8, gk8 = jnp.tile(sp["fox_qnorm_g"], (1, FOX_HEADS)), jnp.tile(sp["fox_knorm_g"], (1, FOX_HEADS))
    go2 = jnp.tile(sp["fox_onorm_g"], (1, 2))
    bias = _lanes_vec([(SM_F, sp["fox_f_bias"]), (SM_A, sp["gdn_dt_bias"])])
    alog = _lanes_vec([(SM_A, sp["gdn_A_log"])])

    h1 = rms_fwd(x, sp["norm_mix_g"], name="rms_mix")
    P = matmul(h1, w["wa"], name="mm_in", tn=IN_TILE)
    gates = gates_fwd(P, bias, alog, B=B)
    c = gates[:, SM_F:SM_F + FOX_HEADS].reshape(B, S, FOX_HEADS).transpose(0, 2, 1)
    ccol, crow = c[..., None], c.reshape(B, FOX_HEADS, S // FOX_TQ, 1, FOX_TQ)
    qn, kn, vb = fox_prep_fwd(P, gq8, gk8)
    o_raw, o_a, lse = fox_core_fwd(qn, kn, vb, ccol, crow, go2, B=B)
    G = gdn_prep_fwd(P, w["conv_w"], B=B)
    ob_raw, o_b, states = gdn_fwd(G, gates, P, sp["gdn_onorm_g"], B=B)
    oab = jnp.concatenate([o_a, o_b], axis=1)
    if "late" in w:
        w = {**w, **w["late"](oab)}
    x2 = matmul(oab, w["w_out"], residual=x, name="mm_out")
    hq = rms_fwd(x2, sp["norm_xattn_g"], name="rms_xattn")
    hm = rms_fwd(mem, sp["mem_norm_g"], name="rms_mem")
    cq = matmul(hq, w["w_cq"], name="mm_cq")
    ckv = matmul(hm, w["w_ckv"], name="mm_ckv")
    co = xattn_fwd(cq, ckv, sp["xattn_qnorm_g"], sp["xattn_knorm_g"], B=B)
    x3 = matmul(co, w["w_co"], b_stacked=True, residual=x2, name="mm_co")
    hf = rms_fwd(x3, sp["norm_mlp_g"], name="rms_mlp")
    a, act = matmul(hf, w["w_mlp1"], b_stacked=True, relu2_out=True, name="mm_mlp1")
    x4 = matmul(act, w["w_mlp2"], residual=x3, name="mm_mlp2")
    dy, loss = loss_head(x4, target)

    da = matmul(dy, w["w_mlp2"], tb=True, relu2_bwd_aux=a, out_dtype=MXU_DTYPE, name="mm_d_act")
    g_mlp2 = matmul(act, dy, ta=True, out_dtype=WIRE_DTYPE, name="mm_g_mlp2")
    g_mlp1 = matmul(hf, da, ta=True, out_stacked=True, out_dtype=WIRE_DTYPE, name="mm_g_mlp1")
    dhf = matmul(da, w["w_mlp1"], tb=True, b_stacked=True, name="mm_d_hf")
    by_rows = lambda g: g.reshape(N_CHIPS, g.shape[0] // N_CHIPS, g.shape[1])
    early = w.get("grads_ready", lambda grads: jnp.zeros((1, 1), f32))
    tok = early(dict(w_mlp1=g_mlp1, w_mlp2=by_rows(g_mlp2)))[0, 0]
    dx3, g_norm_mlp = rms_bwd(x3, sp["norm_mlp_g"] + tok, dhf, dy, name="rms_mlp_bwd")
    dco = matmul(dx3, w["w_co"], tb=True, b_stacked=True, name="mm_d_co")
    g_co = matmul(co, dx3, ta=True, out_stacked=True, out_dtype=WIRE_DTYPE, name="mm_g_co")
    dcq, dckv, g_xq, g_xk = xattn_bwd(cq, ckv, sp["xattn_qnorm_g"], sp["xattn_knorm_g"], dco, B=B)
    g_cq = matmul(hq, dcq, ta=True, out_dtype=WIRE_DTYPE, name="mm_g_cq")
    dhq = matmul(dcq, w["w_cq"], tb=True, name="mm_d_hq")
    g_ckv = matmul(hm, dckv, ta=True, out_dtype=WIRE_DTYPE, name="mm_g_ckv")
    dhm = matmul(dckv, w["w_ckv"], tb=True, name="mm_d_hm")
    _, g_mem_norm = rms_bwd(mem, sp["mem_norm_g"], dhm, None, name="rms_mem_bwd")
    dx2, g_norm_xattn = rms_bwd(x2, sp["norm_xattn_g"], dhq, dx3, name="rms_xattn_bwd")
    doab = matmul(dx2, w["w_out"], tb=True, name="mm_d_oab")
    g_out = matmul(oab, dx2, ta=True, out_dtype=WIRE_DTYPE, name="mm_g_out")
    tok = early(dict(w_co=g_co, w_cq=by_rows(g_cq), w_ckv=by_rows(g_ckv), w_out=by_rows(g_out)))[0, 0]
    dqn, dkn, dv_f, dccol, dcrow, dgo2 = fox_core_bwd(qn, kn, vb, ccol, crow, go2 + tok, o_raw, lse, doab, B=B)
    dq_f, dk_f, dgq8, dgk8 = fox_prep_bwd(P, gq8, gk8, dqn, dkn)
    dGq, dGk, dGv, dgt, dz, g_gdn_on = gdn_bwd(G, gates, P, sp["gdn_onorm_g"], ob_raw, states, doab, B=B)
    dPg, g_conv = gdn_prep_bwd(P, w["conv_w"], jnp.concatenate([dGq, dGk, dGv], axis=1), B=B)
    dc = (dccol[..., 0] + dcrow.reshape(B, FOX_HEADS, S)).transpose(0, 2, 1).reshape(T, FOX_HEADS)
    dgates = dgt + jnp.pad(dc, ((0, 0), (SM_F, LANES - SM_F - FOX_HEADS)))
    dsmall, par = gates_bwd(P, bias, alog, dgates, B=B)
    dP = jnp.concatenate([dq_f, dk_f, dv_f, dPg, dz, dsmall, jnp.zeros((T, IN_ALIGNED - COL_SMALL - LANES), MXU_DTYPE)], axis=1)
    g_wa = matmul(h1, dP, ta=True, out_dtype=WIRE_DTYPE, name="mm_g_in", tn=IN_TILE)
    dh1 = matmul(dP, w["wa"], tb=True, name="mm_d_h1", tk=IN_TILE)
    dx, g_norm_mix = rms_bwd(x, sp["norm_mix_g"], dh1, dx2, name="rms_mix_bwd")

    fold = lambda g: jnp.sum(g.reshape(-1, FOX_HEAD_DIM), axis=0, keepdims=True)
    g_in = unalign_w_in(g_wa).reshape(D_MODEL, N_CHIPS, IN_DIM // N_CHIPS).transpose(1, 0, 2)
    big = dict(w_in=g_in, w_out=by_rows(g_out), w_cq=by_rows(g_cq), w_ckv=by_rows(g_ckv), w_co=g_co, w_mlp1=g_mlp1,
               w_mlp2=by_rows(g_mlp2))
    small = dict(norm_mix_g=g_norm_mix, fox_qnorm_g=fold(dgq8), fox_knorm_g=fold(dgk8),
                 fox_f_bias=par[0:1, SM_F:SM_F + FOX_HEADS], fox_onorm_g=fold(dgo2), gdn_conv_w=g_conv,
                 gdn_A_log=par[1:2, SM_A:SM_A + GDN_HEADS], gdn_dt_bias=par[0:1, SM_A:SM_A + GDN_HEADS],
                 gdn_onorm_g=g_gdn_on, norm_xattn_g=g_norm_xattn, mem_norm_g=g_mem_norm,
                 xattn_qnorm_g=g_xq, xattn_knorm_g=g_xk, norm_mlp_g=g_norm_mlp)
    return loss, dx, big, small


MESH_IDS = pl.DeviceIdType.MESH
N_CHIPS = 4
HBM_SPEC = pl.BlockSpec(memory_space=pltpu.HBM)
PACK_ROWS = 30720
PACK_HALF = PACK_ROWS // 2
PACK_BLOCK = 3072


def _place():
    return lax.axis_index("x"), lax.axis_index("y"), lax.axis_index("c")


def _other_chips(x, y):
    return [(1 - x, y), (x, 1 - y), (1 - x, 1 - y)]


def _remote(src, dst, send_sem, recv_sem, to):
    return pltpu.make_async_remote_copy(src_ref=src, dst_ref=dst, send_sem=send_sem, recv_sem=recv_sem,
                                        device_id=to, device_id_type=MESH_IDS)


def all_gather_shards(packed):
    half = PACK_HALF

    def body(src_ref, out_ref, send_sems, recv_sems):
        x, y, c = _place()
        me_chip = 2 * x + y
        sibling = (x, y, 1 - c)
        chips = _other_chips(x, y)

        def rows(chip, core):
            return out_ref.at[chip, pl.ds(core * half, half), :]

        sends = [_remote(src_ref.at[pl.ds(c * half, half), :], rows(me_chip, c), send_sems.at[j], recv_sems.at[j], (px, py, c))
                 for j, (px, py) in enumerate(chips)]
        for cp in sends:
            cp.start()
        passed = []
        for j, (px, py) in enumerate(chips):
            theirs = rows(2 * px + py, c)
            _remote(theirs, theirs, send_sems.at[j], recv_sems.at[j], (px, py, c)).wait_recv()
            cp = _remote(theirs, theirs, send_sems.at[3 + j], recv_sems.at[3 + j], sibling)
            cp.start()
            passed.append(cp)
        for j, (px, py) in enumerate(chips):
            theirs = rows(2 * px + py, 1 - c)
            _remote(theirs, theirs, send_sems.at[3 + j], recv_sems.at[3 + j], sibling).wait_recv()
        for cp in sends + passed:
            cp.wait_send()

    return pl.pallas_call(
        body, name="all_gather_shards", in_specs=[HBM_SPEC], out_specs=HBM_SPEC,
        out_shape=jax.ShapeDtypeStruct((N_CHIPS,) + packed.shape, packed.dtype),
        scratch_shapes=[pltpu.SemaphoreType.DMA((6,)), pltpu.SemaphoreType.DMA((6,))],
    )(packed)


def exchange_core_halves(G):
    half = PACK_HALF

    def body(g_ref, land_ref, send_sem, recv_sem):
        x, y, c = _place()
        cp = _remote(g_ref.at[:, pl.ds((1 - c) * half, half), :], land_ref, send_sem, recv_sem, (x, y, 1 - c))
        cp.start()
        cp.wait()

    return pl.pallas_call(
        body, name="exchange_core_halves", in_specs=[HBM_SPEC], out_specs=HBM_SPEC,
        out_shape=jax.ShapeDtypeStruct((N_CHIPS, half, LANES), G.dtype),
        scratch_shapes=[pltpu.SemaphoreType.DMA(()), pltpu.SemaphoreType.DMA(())],
    )(G)


def add_core_halves(G, land, core):
    nb = PACK_HALF // PACK_BLOCK

    def body(c_ref, g_ref, l_ref, o_ref):
        o_ref[...] = (g_ref[...].astype(f32) + l_ref[...].astype(f32)).astype(o_ref.dtype)

    blk = (1, PACK_BLOCK, LANES)
    return pl.pallas_call(
        body, name="add_core_halves",
        grid_spec=pltpu.PrefetchScalarGridSpec(
            num_scalar_prefetch=1, grid=(N_CHIPS, nb),
            in_specs=[pl.BlockSpec(blk, lambda k, i, c_ref: (k, c_ref[0] * nb + i, 0)),
                      pl.BlockSpec(blk, lambda k, i, c_ref: (k, i, 0))],
            out_specs=pl.BlockSpec(blk, lambda k, i, c_ref: (k, i, 0))),
        out_shape=jax.ShapeDtypeStruct(land.shape, land.dtype),
        compiler_params=_cparams("parallel", "parallel"),
    )(core, G, land)


def scatter_to_chips(part):
    def body(p_ref, land_ref, send_sems, recv_sems):
        x, y, c = _place()
        me_chip = 2 * x + y
        chips = _other_chips(x, y)
        sends = [_remote(p_ref.at[2 * px + py], land_ref.at[me_chip], send_sems.at[j], recv_sems.at[j], (px, py, c))
                 for j, (px, py) in enumerate(chips)]
        for cp in sends:
            cp.start()
        for j, (px, py) in enumerate(chips):
            slot = land_ref.at[2 * px + py]
            _remote(slot, slot, send_sems.at[j], recv_sems.at[j], (px, py, c)).wait_recv()
        for cp in sends:
            cp.wait_send()

    return pl.pallas_call(
        body, name="scatter_to_chips", in_specs=[HBM_SPEC], out_specs=HBM_SPEC,
        out_shape=jax.ShapeDtypeStruct(part.shape, part.dtype),
        scratch_shapes=[pltpu.SemaphoreType.DMA((3,)), pltpu.SemaphoreType.DMA((3,))],
    )(part)


def sum_chips(part, land, order):
    nb = PACK_HALF // PACK_BLOCK

    def body(order_ref, p_ref, l1_ref, l2_ref, l3_ref, o_ref):
        o_ref[...] = ((p_ref[0].astype(f32) + l1_ref[0].astype(f32)) + l2_ref[0].astype(f32)) + l3_ref[0].astype(f32)

    slot = lambda j: pl.BlockSpec((1, PACK_BLOCK, LANES), lambda i, order_ref: (order_ref[j], i, 0))
    return pl.pallas_call(
        body, name="sum_chips",
        grid_spec=pltpu.PrefetchScalarGridSpec(
            num_scalar_prefetch=1, grid=(nb,), in_specs=[slot(0), slot(1), slot(2), slot(3)],
            out_specs=pl.BlockSpec((PACK_BLOCK, LANES), lambda i, order_ref: (i, 0))),
        out_shape=jax.ShapeDtypeStruct((PACK_HALF, LANES), f32),
        compiler_params=_cparams("parallel"),
    )(order, part, land, land, land)


def swap_core_halves(red):
    def body(r_ref, out_ref, send_sem, recv_sem):
        x, y, c = _place()
        cp = _remote(r_ref, out_ref, send_sem, recv_sem, (x, y, 1 - c))
        cp.start()
        cp.wait()

    return pl.pallas_call(
        body, name="swap_core_halves", in_specs=[HBM_SPEC], out_specs=HBM_SPEC,
        out_shape=jax.ShapeDtypeStruct(red.shape, red.dtype),
        scratch_shapes=[pltpu.SemaphoreType.DMA(()), pltpu.SemaphoreType.DMA(())],
    )(red)


def _half(ref, core):
    rows = ref.shape[-2] // 2
    return ref.at[(slice(None),) * (len(ref.shape) - 2) + (pl.ds(core * rows, rows), slice(None))]


def gather_weights(shards, conv):
    n = len(shards)

    def body(*refs):
        src, conv_src = refs[:n], refs[n]
        out, conv_out = refs[n + 1:2 * n + 1], refs[2 * n + 1]
        send_sems, recv_sems = refs[2 * n + 2], refs[2 * n + 3]
        x, y, c = _place()
        me_chip = 2 * x + y
        sibling = (x, y, 1 - c)
        chips = _other_chips(x, y)
        sends = []
        for a in range(n):
            for j, (px, py) in enumerate(chips):
                sends.append(_remote(_half(src[a], c), _half(out[a].at[me_chip], c),
                                     send_sems.at[6 * a + j], recv_sems.at[6 * a + j], (px, py, c)))
        for j, (px, py) in enumerate(chips):
            sends.append(_remote(conv_src, conv_out.at[me_chip], send_sems.at[6 * n + j], recv_sems.at[6 * n + j], (px, py, c)))
        for cp in sends:
            cp.start()
        passed = []
        for a in range(n):
            for j, (px, py) in enumerate(chips):
                theirs = _half(out[a].at[2 * px + py], c)
                _remote(theirs, theirs, send_sems.at[6 * a + j], recv_sems.at[6 * a + j], (px, py, c)).wait_recv()
                cp = _remote(theirs, theirs, send_sems.at[6 * a + 3 + j], recv_sems.at[6 * a + 3 + j], sibling)
                cp.start()
                passed.append(cp)
        for j, (px, py) in enumerate(chips):
            theirs = conv_out.at[2 * px + py]
            _remote(theirs, theirs, send_sems.at[6 * n + j], recv_sems.at[6 * n + j], (px, py, c)).wait_recv()
        for a in range(n):
            for j, (px, py) in enumerate(chips):
                theirs = _half(out[a].at[2 * px + py], 1 - c)
                _remote(theirs, theirs, send_sems.at[6 * a + 3 + j], recv_sems.at[6 * a + 3 + j], sibling).wait_recv()
        for cp in sends + passed:
            cp.wait_send()

    return pl.pallas_call(
        body, name="gather_weights", in_specs=[HBM_SPEC] * (n + 1), out_specs=[HBM_SPEC] * (n + 1),
        out_shape=[jax.ShapeDtypeStruct((N_CHIPS,) + s.shape, s.dtype) for s in list(shards) + [conv]],
        scratch_shapes=[pltpu.SemaphoreType.DMA((6 * n + 3,)), pltpu.SemaphoreType.DMA((6 * n + 3,))],
    )(*shards, conv)


SEM_SPEC = pl.BlockSpec(memory_space=pltpu.SEMAPHORE)
SPLIT_EFFECT = pltpu.SideEffectType.DATAFLOW_SIDE_EFFECTING


def _gather_async_copies(src, land, send_sems, recv_sems, x, y, c):
    me_chip = 2 * x + y
    sends, arrivals = [], []
    for a in range(len(src)):
        for j, (px, py) in enumerate(_other_chips(x, y)):
            for core in range(2):
                sends.append(_remote(_half(src[a], c), _half(land[a].at[me_chip], c), send_sems.at[6 * a + 2 * j + core],
                                     recv_sems.at[6 * a + 2 * j + c], (px, py, core)))
                theirs = _half(land[a].at[2 * px + py], core)
                arrivals.append(_remote(theirs, theirs, send_sems.at[6 * a + 2 * j + core],
                                        recv_sems.at[6 * a + 2 * j + core], (px, py, core)))
    return sends, arrivals


def gather_weights_start(shards):
    n = len(shards)

    def body(*refs):
        src, land = refs[:n], refs[n:2 * n]
        send_sems, recv_sems, token = refs[2 * n], refs[2 * n + 1], refs[4 * n + 2]
        x, y, c = _place()
        for cp in _gather_async_copies(src, land, send_sems, recv_sems, x, y, c)[0]:
            cp.start()
        token[...] = jnp.zeros_like(token)

    zones = [pltpu.with_memory_space_constraint(lax.empty((N_CHIPS,) + s.shape, s.dtype), pltpu.HBM) for s in shards]
    srcs = [pltpu.with_memory_space_constraint(s, pltpu.HBM) for s in shards]
    out = pl.pallas_call(
        body, name="gather_weights_start",
        out_shape=[pltpu.SemaphoreType.DMA((6 * n,)), pltpu.SemaphoreType.DMA((6 * n,))]
        + [pltpu.HBM(s.shape, s.dtype) for s in shards] + [pltpu.HBM(z.shape, z.dtype) for z in zones]
        + [jax.ShapeDtypeStruct((8, LANES), f32)],
        in_specs=[HBM_SPEC] * (2 * n),
        out_specs=[SEM_SPEC, SEM_SPEC] + [HBM_SPEC] * (2 * n) + [pl.BlockSpec(memory_space=pltpu.VMEM)],
        input_output_aliases={i: 2 + i for i in range(2 * n)},
        compiler_params=pltpu.CompilerParams(has_side_effects=SPLIT_EFFECT),
    )(*srcs, *zones)
    return out[0], out[1], out[2:2 + n], out[2 + n:2 + 2 * n], out[-1]


def gather_weights_wait(send_sems, recv_sems, shards, zones, after):
    n = len(shards)

    def body(*refs):
        src, land = refs[:n], refs[n:2 * n]
        send_sems, recv_sems = refs[2 * n], refs[2 * n + 1]
        x, y, c = _place()
        sends, arrivals = _gather_async_copies(src, land, send_sems, recv_sems, x, y, c)
        for cp in sends:
            cp.wait_send()
        for cp in arrivals:
            cp.wait_recv()

    out = pl.pallas_call(
        body, name="gather_weights_wait",
        out_shape=[pltpu.HBM(s.shape, s.dtype) for s in shards] + [pltpu.HBM(z.shape, z.dtype) for z in zones],
        in_specs=[HBM_SPEC] * (2 * n) + [SEM_SPEC, SEM_SPEC, pl.BlockSpec(memory_space=pl.ANY)],
        out_specs=[HBM_SPEC] * (2 * n),
        input_output_aliases={i: i for i in range(2 * n)},
        compiler_params=pltpu.CompilerParams(has_side_effects=SPLIT_EFFECT),
    )(*shards, *zones, send_sems, recv_sems, after)
    return out[n:]


def swap_grad_halves(grads, *, name):
    n = len(grads)

    def body(*refs):
        g, land, send_sems, recv_sems = refs[:n], refs[n:2 * n], refs[2 * n], refs[2 * n + 1]
        x, y, c = _place()
        copies = [_remote(_half(g[a], 1 - c), land[a], send_sems.at[a], recv_sems.at[a], (x, y, 1 - c)) for a in range(n)]
        for cp in copies:
            cp.start()
        for cp in copies:
            cp.wait()

    return pl.pallas_call(
        body, name=name, in_specs=[HBM_SPEC] * n, out_specs=[HBM_SPEC] * n,
        out_shape=[jax.ShapeDtypeStruct((N_CHIPS, g.shape[1] // 2, g.shape[2]), g.dtype) for g in grads],
        scratch_shapes=[pltpu.SemaphoreType.DMA((n,)), pltpu.SemaphoreType.DMA((n,))],
    )(*grads)


GRAD_ROWS = 256


def add_grad_halves(g, land, core, *, name):
    _, half, cols = land.shape
    tr = min(GRAD_ROWS, half)
    nb = half // tr

    def body(c_ref, g_ref, l_ref, o_ref):
        o_ref[...] = (g_ref[...].astype(f32) + l_ref[...].astype(f32)).astype(o_ref.dtype)

    blk = (1, tr, cols)
    return pl.pallas_call(
        body, name=name,
        grid_spec=pltpu.PrefetchScalarGridSpec(
            num_scalar_prefetch=1, grid=(N_CHIPS, nb),
            in_specs=[pl.BlockSpec(blk, lambda k, i, c_ref: (k, c_ref[0] * nb + i, 0)),
                      pl.BlockSpec(blk, lambda k, i, c_ref: (k, i, 0))],
            out_specs=pl.BlockSpec(blk, lambda k, i, c_ref: (k, i, 0))),
        out_shape=jax.ShapeDtypeStruct(land.shape, land.dtype),
        compiler_params=_cparams("parallel", "parallel"),
    )(core, g, land)


def scatter_grads(parts):
    n = len(parts)

    def body(*refs):
        p, land, send_sems, recv_sems = refs[:n], refs[n:2 * n], refs[2 * n], refs[2 * n + 1]
        x, y, c = _place()
        me_chip = 2 * x + y
        chips = _other_chips(x, y)
        sends = [_remote(p[a].at[2 * px + py], land[a].at[me_chip], send_sems.at[3 * a + j], recv_sems.at[3 * a + j], (px, py, c))
                 for a in range(n) for j, (px, py) in enumerate(chips)]
        for cp in sends:
            cp.start()
        for a in range(n):
            for j, (px, py) in enumerate(chips):
                slot = land[a].at[2 * px + py]
                _remote(slot, slot, send_sems.at[3 * a + j], recv_sems.at[3 * a + j], (px, py, c)).wait_recv()
        for cp in sends:
            cp.wait_send()

    return pl.pallas_call(
        body, name="scatter_grads", in_specs=[HBM_SPEC] * n, out_specs=[HBM_SPEC] * n,
        out_shape=[jax.ShapeDtypeStruct(p.shape, p.dtype) for p in parts],
        scratch_shapes=[pltpu.SemaphoreType.DMA((3 * n,)), pltpu.SemaphoreType.DMA((3 * n,))],
    )(*parts)


def _scatter_async_copies(parts, land, send_sems, recv_sems, x, y, c):
    me_chip = 2 * x + y
    sends, arrivals = [], []
    for a in range(len(parts)):
        for j, (px, py) in enumerate(_other_chips(x, y)):
            sems = (send_sems.at[3 * a + j], recv_sems.at[3 * a + j], (px, py, c))
            sends.append(_remote(parts[a].at[2 * px + py], land[a].at[me_chip], *sems))
            slot = land[a].at[2 * px + py]
            arrivals.append(_remote(slot, slot, *sems))
    return sends, arrivals


def scatter_grads_start(parts, *, name):
    n = len(parts)

    def body(*refs):
        p, land = refs[:n], refs[n:2 * n]
        send_sems, recv_sems, token = refs[2 * n], refs[2 * n + 1], refs[4 * n + 2]
        x, y, c = _place()
        for cp in _scatter_async_copies(p, land, send_sems, recv_sems, x, y, c)[0]:
            cp.start()
        token[...] = jnp.zeros_like(token)

    zones = [pltpu.with_memory_space_constraint(lax.empty(p.shape, p.dtype), pltpu.HBM) for p in parts]
    srcs = [pltpu.with_memory_space_constraint(p, pltpu.HBM) for p in parts]
    hbm = [pltpu.HBM(p.shape, p.dtype) for p in parts]
    out = pl.pallas_call(
        body, name=name,
        out_shape=[pltpu.SemaphoreType.DMA((3 * n,)), pltpu.SemaphoreType.DMA((3 * n,))] + hbm + hbm
        + [jax.ShapeDtypeStruct((8, LANES), f32)],
        in_specs=[HBM_SPEC] * (2 * n),
        out_specs=[SEM_SPEC, SEM_SPEC] + [HBM_SPEC] * (2 * n) + [pl.BlockSpec(memory_space=pltpu.VMEM)],
        input_output_aliases={i: 2 + i for i in range(2 * n)},
        compiler_params=pltpu.CompilerParams(has_side_effects=SPLIT_EFFECT),
    )(*srcs, *zones)
    return out[0], out[1], out[2:2 + n], out[2 + n:2 + 2 * n], out[-1]


def scatter_grads_wait(send_sems, recv_sems, parts, zones, after, *, name):
    n = len(parts)

    def body(*refs):
        p, land = refs[:n], refs[n:2 * n]
        x, y, c = _place()
        sends, arrivals = _scatter_async_copies(p, land, refs[2 * n], refs[2 * n + 1], x, y, c)
        for cp in sends:
            cp.wait_send()
        for cp in arrivals:
            cp.wait_recv()

    hbm = [pltpu.HBM(p.shape, p.dtype) for p in parts]
    out = pl.pallas_call(
        body, name=name, out_shape=hbm + hbm,
        in_specs=[HBM_SPEC] * (2 * n) + [SEM_SPEC, SEM_SPEC, pl.BlockSpec(memory_space=pl.ANY)],
        out_specs=[HBM_SPEC] * (2 * n),
        input_output_aliases={i: i for i in range(2 * n)},
        compiler_params=pltpu.CompilerParams(has_side_effects=SPLIT_EFFECT),
    )(*parts, *zones, send_sems, recv_sems, after)
    return out[:n], out[n:]


def sum_grads(part, land, order, *, name):
    _, half, cols = part.shape
    tr = min(GRAD_ROWS, half)

    def body(order_ref, p_ref, l1_ref, l2_ref, l3_ref, o_ref):
        o_ref[...] = ((p_ref[0].astype(f32) + l1_ref[0].astype(f32)) + l2_ref[0].astype(f32)) + l3_ref[0].astype(f32)

    slot = lambda j: pl.BlockSpec((1, tr, cols), lambda i, order_ref: (order_ref[j], i, 0))
    return pl.pallas_call(
        body, name=name,
        grid_spec=pltpu.PrefetchScalarGridSpec(
            num_scalar_prefetch=1, grid=(half // tr,), in_specs=[slot(0), slot(1), slot(2), slot(3)],
            out_specs=pl.BlockSpec((tr, cols), lambda i, order_ref: (i, 0))),
        out_shape=jax.ShapeDtypeStruct((half, cols), f32),
        compiler_params=_cparams("parallel"),
    )(order, part, land, land, land)


def swap_reduced_halves(mine):
    n = len(mine)

    def body(*refs):
        r, out, send_sems, recv_sems = refs[:n], refs[n:2 * n], refs[2 * n], refs[2 * n + 1]
        x, y, c = _place()
        copies = [_remote(r[a], out[a], send_sems.at[a], recv_sems.at[a], (x, y, 1 - c)) for a in range(n)]
        for cp in copies:
            cp.start()
        for cp in copies:
            cp.wait()

    return pl.pallas_call(
        body, name="swap_reduced_halves", in_specs=[HBM_SPEC] * n, out_specs=[HBM_SPEC] * n,
        out_shape=[jax.ShapeDtypeStruct(r.shape, r.dtype) for r in mine],
        scratch_shapes=[pltpu.SemaphoreType.DMA((n,)), pltpu.SemaphoreType.DMA((n,))],
    )(*mine)


def adamw_halves(w, mine, theirs, m, v, core, *, name):
    R, C = w.shape
    tr = min(GRAD_ROWS, R // 2)
    half_nb = R // 2 // tr

    def body(c_ref, w_ref, a_ref, b_ref, m_ref, v_ref, g_ref, d_ref, nm_ref, nv_ref):
        low = pl.program_id(0) < half_nb
        gv = jnp.where(low == (c_ref[0] == 0), a_ref[...], b_ref[...])
        nm = ADAM_B1 * m_ref[...] + (1.0 - ADAM_B1) * gv
        nv = ADAM_B2 * v_ref[...] + (1.0 - ADAM_B2) * jnp.square(gv)
        m_hat = nm / (1.0 - ADAM_B1 ** ADAM_STEP)
        v_hat = nv / (1.0 - ADAM_B2 ** ADAM_STEP)
        g_ref[...] = gv
        d_ref[...] = -ADAM_LR * (m_hat / (jnp.sqrt(v_hat) + ADAM_EPS) + ADAM_WD * w_ref[...])
        nm_ref[...] = nm
        nv_ref[...] = nv

    full = pl.BlockSpec((tr, C), lambda i, c_ref: (i, 0))
    part = pl.BlockSpec((tr, C), lambda i, c_ref: (i % half_nb, 0))
    out = jax.ShapeDtypeStruct((R, C), f32)
    return pl.pallas_call(
        body, name=name,
        grid_spec=pltpu.PrefetchScalarGridSpec(
            num_scalar_prefetch=1, grid=(2 * half_nb,), in_specs=[full, part, part, full, full], out_specs=[full] * 4),
        out_shape=[out] * 4, compiler_params=_cparams("parallel"),
    )(core, w, mine, theirs, m, v)


N_DEV = 8


def all_reduce_small(v):
    def body(src_ref, out_ref, land_ref, send_sems, recv_sems):
        x, y, c = _place()
        me = 4 * x + 2 * y + c
        copies = []
        for r in range(1, N_DEV):
            peer = ((1 - x) if r & 4 else x, (1 - y) if r & 2 else y, (1 - c) if r & 1 else c)
            copies.append(_remote(src_ref, land_ref.at[r], send_sems.at[r - 1], recv_sems.at[r - 1], peer))
        for cp in copies:
            cp.start()
        land_ref[0] = src_ref[...]
        for cp in copies:
            cp.wait()
        acc = land_ref[me]
        for d in range(1, N_DEV):
            acc = acc + land_ref[jnp.bitwise_xor(me, d)]
        out_ref[...] = acc

    vm = pl.BlockSpec(memory_space=pltpu.VMEM)
    return pl.pallas_call(
        body, name="all_reduce_small", in_specs=[vm], out_specs=vm,
        out_shape=jax.ShapeDtypeStruct(v.shape, v.dtype),
        scratch_shapes=[pltpu.VMEM((N_DEV,) + v.shape, v.dtype),
                        pltpu.SemaphoreType.DMA((N_DEV - 1,)), pltpu.SemaphoreType.DMA((N_DEV - 1,))],
    )(v)


def adamw(w, g, m, v, *, name, tr):
    R, C = w.shape
    tr = min(tr, R)

    def body(w_ref, g_ref, m_ref, v_ref, d_ref, nm_ref, nv_ref):
        gv = g_ref[...]
        nm = ADAM_B1 * m_ref[...] + (1.0 - ADAM_B1) * gv
        nv = ADAM_B2 * v_ref[...] + (1.0 - ADAM_B2) * jnp.square(gv)
        m_hat = nm / (1.0 - ADAM_B1 ** ADAM_STEP)
        v_hat = nv / (1.0 - ADAM_B2 ** ADAM_STEP)
        d_ref[...] = -ADAM_LR * (m_hat / (jnp.sqrt(v_hat) + ADAM_EPS) + ADAM_WD * w_ref[...])
        nm_ref[...] = nm
        nv_ref[...] = nv

    blk = pl.BlockSpec((tr, C), lambda i: (i, 0))
    out = jax.ShapeDtypeStruct((R, C), f32)
    return pl.pallas_call(
        body, name=name, grid=(R // tr,), in_specs=[blk] * 4, out_specs=[blk] * 3, out_shape=[out] * 3,
        compiler_params=_cparams("parallel"),
    )(w, g, m, v)


BIG_SHARDS = (("w_in", (1024, 900), True), ("w_out", (256, 1024), False), ("w_cq", (256, 512), False),
              ("w_ckv", (256, 1024), False), ("w_co", (512, 256), True), ("w_mlp1", (1024, 1024), True),
              ("w_mlp2", (1024, 1024), False))
CONV_SHARD = (CONV_WIDTH, 3 * GDN_WIDTH // N_CHIPS)
SMALL_DIMS = (("norm_mix_g", 1024), ("fox_qnorm_g", 64), ("fox_knorm_g", 64), ("fox_f_bias", 8), ("fox_onorm_g", 64),
              ("gdn_A_log", 4), ("gdn_dt_bias", 4), ("gdn_onorm_g", 128), ("norm_xattn_g", 1024), ("mem_norm_g", 1024),
              ("xattn_qnorm_g", 128), ("xattn_knorm_g", 128), ("norm_mlp_g", 1024))
WEIGHT_ORDER = ("norm_mix_g", "w_in", "fox_qnorm_g", "fox_knorm_g", "fox_f_bias", "fox_onorm_g", "gdn_conv_w", "gdn_A_log",
                "gdn_dt_bias", "gdn_onorm_g", "w_out", "norm_xattn_g", "mem_norm_g", "w_cq", "w_ckv", "xattn_qnorm_g",
                "xattn_knorm_g", "w_co", "norm_mlp_g", "w_mlp1", "w_mlp2")


def _pack_rows(pieces, rows, lead=()):
    flat = []
    for p in pieces:
        p = p.reshape(lead + (-1,))
        pad = (-p.shape[-1]) % LANES
        flat.append(jnp.pad(p, [(0, 0)] * len(lead) + [(0, pad)]) if pad else p)
    cat = jnp.concatenate(flat, axis=-1)
    cat = jnp.pad(cat, [(0, 0)] * len(lead) + [(0, rows * LANES - cat.shape[-1])])
    return cat.reshape(lead + (rows, LANES))


def _unpack_rows(buf, sizes, lead=()):
    flat = buf.reshape(lead + (-1,))
    out, off = [], 0
    for n in sizes:
        out.append(flat[..., off:off + n])
        off += n + (-n) % LANES
    return out


def _conv_to_wire(conv):
    return lax.bitcast_convert_type(conv, bf16)


def _conv_from_wire(wire):
    return lax.bitcast_convert_type(wire, f32)


SMALL_ROWS = 96
SMALL_ADAM_ROWS = 56


def kernel(x, mem, norm_mix_g, w_in, fox_qnorm_g, fox_knorm_g, fox_f_bias, fox_onorm_g, gdn_conv_w, gdn_A_log, gdn_dt_bias, gdn_onorm_g, w_out, norm_xattn_g, mem_norm_g, w_cq, w_ckv, xattn_qnorm_g, xattn_knorm_g, w_co, norm_mlp_g, w_mlp1, w_mlp2, loss_target, m_norm_mix_g, m_w_in, m_fox_qnorm_g, m_fox_knorm_g, m_fox_f_bias, m_fox_onorm_g, m_gdn_conv_w, m_gdn_A_log, m_gdn_dt_bias, m_gdn_onorm_g, m_w_out, m_norm_xattn_g, m_mem_norm_g, m_w_cq, m_w_ckv, m_xattn_qnorm_g, m_xattn_knorm_g, m_w_co, m_norm_mlp_g, m_w_mlp1, m_w_mlp2, v_norm_mix_g, v_w_in, v_fox_qnorm_g, v_fox_knorm_g, v_fox_f_bias, v_fox_onorm_g, v_gdn_conv_w, v_gdn_A_log, v_gdn_dt_bias, v_gdn_onorm_g, v_w_out, v_norm_xattn_g, v_mem_norm_g, v_w_cq, v_w_ckv, v_xattn_qnorm_g, v_xattn_knorm_g, v_w_co, v_norm_mlp_g, v_w_mlp1, v_w_mlp2):
    wts = dict(norm_mix_g=norm_mix_g, w_in=w_in, fox_qnorm_g=fox_qnorm_g, fox_knorm_g=fox_knorm_g, fox_f_bias=fox_f_bias,
               fox_onorm_g=fox_onorm_g, gdn_conv_w=gdn_conv_w, gdn_A_log=gdn_A_log, gdn_dt_bias=gdn_dt_bias,
               gdn_onorm_g=gdn_onorm_g, w_out=w_out, norm_xattn_g=norm_xattn_g, mem_norm_g=mem_norm_g, w_cq=w_cq, w_ckv=w_ckv,
               xattn_qnorm_g=xattn_qnorm_g, xattn_knorm_g=xattn_knorm_g, w_co=w_co, norm_mlp_g=norm_mlp_g, w_mlp1=w_mlp1,
               w_mlp2=w_mlp2)
    mom = dict(norm_mix_g=m_norm_mix_g, w_in=m_w_in, fox_qnorm_g=m_fox_qnorm_g, fox_knorm_g=m_fox_knorm_g,
               fox_f_bias=m_fox_f_bias, fox_onorm_g=m_fox_onorm_g, gdn_conv_w=m_gdn_conv_w, gdn_A_log=m_gdn_A_log,
               gdn_dt_bias=m_gdn_dt_bias, gdn_onorm_g=m_gdn_onorm_g, w_out=m_w_out, norm_xattn_g=m_norm_xattn_g,
               mem_norm_g=m_mem_norm_g, w_cq=m_w_cq, w_ckv=m_w_ckv, xattn_qnorm_g=m_xattn_qnorm_g,
               xattn_knorm_g=m_xattn_knorm_g, w_co=m_w_co, norm_mlp_g=m_norm_mlp_g, w_mlp1=m_w_mlp1, w_mlp2=m_w_mlp2)
    var = dict(norm_mix_g=v_norm_mix_g, w_in=v_w_in, fox_qnorm_g=v_fox_qnorm_g, fox_knorm_g=v_fox_knorm_g,
               fox_f_bias=v_fox_f_bias, fox_onorm_g=v_fox_onorm_g, gdn_conv_w=v_gdn_conv_w, gdn_A_log=v_gdn_A_log,
               gdn_dt_bias=v_gdn_dt_bias, gdn_onorm_g=v_gdn_onorm_g, w_out=v_w_out, norm_xattn_g=v_norm_xattn_g,
               mem_norm_g=v_mem_norm_g, w_cq=v_w_cq, w_ckv=v_w_ckv, xattn_qnorm_g=v_xattn_qnorm_g,
               xattn_knorm_g=v_xattn_knorm_g, w_co=v_w_co, norm_mlp_g=v_norm_mlp_g, w_mlp1=v_w_mlp1, w_mlp2=v_w_mlp2)
    B, S, D = x.shape
    T = B * S
    big_names = [n for n, _, _ in BIG_SHARDS]
    chip = 2 * lax.axis_index("x") + lax.axis_index("y")
    core = lax.axis_index("c").astype(jnp.int32).reshape(1)

    shards = {n: wts[n][0].astype(MXU_DTYPE) for n in big_names}
    w_in_all, conv_all = gather_weights([shards["w_in"]], gdn_conv_w[0])
    late = big_names[1:]
    send_sems, recv_sems, late_src, late_zones, token = gather_weights_start([shards[n] for n in late])
    own = lambda g, s: lax.dynamic_update_slice(g, s[None], (chip,) + (0,) * s.ndim)
    full = {"w_in": own(w_in_all, shards["w_in"])}
    conv_full = own(conv_all, gdn_conv_w[0]).transpose(1, 0, 2).reshape(CONV_WIDTH, 3 * GDN_WIDTH)
    rows = lambda g: g.reshape(N_CHIPS * g.shape[1], g.shape[2])
    w_in_full = full["w_in"].transpose(1, 0, 2).reshape(D_MODEL, IN_DIM)

    def late_weights(after):
        zones = gather_weights_wait(send_sems, recv_sems, late_src, late_zones, after)
        got = {n: own(z, shards[n]) for n, z in zip(late, zones)}
        return dict(w_out=rows(got["w_out"]), w_cq=rows(got["w_cq"]), w_ckv=rows(got["w_ckv"]), w_co=got["w_co"],
                    w_mlp1=got["w_mlp1"], w_mlp2=rows(got["w_mlp2"]))

    def chip_partials(names, by_chip):
        landed = swap_grad_halves(by_chip, name="swap_grad_halves_" + names[0])
        return [add_grad_halves(g, l, core, name="add_halves_" + n) for n, g, l in zip(names, by_chip, landed)]

    in_flight = []

    def grads_ready(ready):
        names = list(ready)
        *started, tok = scatter_grads_start(chip_partials(names, [ready[n] for n in names]),
                                            name="scatter_grads_start_%d" % len(in_flight))
        in_flight.append((names, *started))
        return tok

    w = dict(wa=align_w_in(w_in_full), conv_w=conv_full, late=late_weights, grads_ready=grads_ready)
    sp = {n: wts[n] for n, _ in SMALL_DIMS}
    sp["norm_mix_g"] = sp["norm_mix_g"] + token[0, 0]

    loss_part, grad_x, g_big, g_small = local_step(x.reshape(T, D), mem.reshape(-1, D), loss_target.reshape(T, D), w, sp, B=B)

    small_pieces = [g_small[n] for n, _ in SMALL_DIMS] + [g_small["gdn_conv_w"], loss_part]
    small_sizes = [d for _, d in SMALL_DIMS] + [CONV_WIDTH * 3 * GDN_WIDTH, LANES]
    red_small = _unpack_rows(all_reduce_small(_pack_rows(small_pieces, SMALL_ROWS)), small_sizes)
    grads = {n: p.reshape(1, d) for (n, d), p in zip(SMALL_DIMS, red_small)}
    conv_grad = lax.dynamic_slice(red_small[-2].reshape(CONV_WIDTH, 3 * GDN_WIDTH), (0, chip * CONV_SHARD[1]), CONV_SHARD)
    grads["gdn_conv_w"] = conv_grad.reshape((1,) + CONV_SHARD)
    loss = red_small[-1][0]

    names = ["w_in"]
    chip_part = chip_partials(names, [g_big[n] for n in names])
    parts, zones = dict(zip(names, chip_part)), dict(zip(names, scatter_grads(chip_part)))
    for k, (names, send_sems, recv_sems, thru, land) in enumerate(in_flight):
        thru, land = scatter_grads_wait(send_sems, recv_sems, thru, land, grad_x, name="scatter_grads_wait_%d" % k)
        parts.update(zip(names, thru))
        zones.update(zip(names, land))
    order = jnp.stack([chip, chip ^ 2, chip ^ 1, chip ^ 3]).astype(jnp.int32)
    mine = [sum_grads(parts[n], zones[n], order, name="sum_chips_" + n) for n in big_names]
    theirs = swap_reduced_halves(mine)

    delta, new_m, new_v = {}, {}, {}
    for n, a, b in zip(big_names, mine, theirs):
        g, d, nm, nv = adamw_halves(wts[n][0], a, b, mom[n][0], var[n][0], core, name="adamw_" + n)
        grads[n], delta[n], new_m[n], new_v[n] = g[None], d[None], nm[None], nv[None]
    small_names = [n for n, _ in SMALL_DIMS] + ["gdn_conv_w"]
    small_sz = [d for _, d in SMALL_DIMS] + [CONV_SHARD[0] * CONV_SHARD[1]]
    packed4 = [_pack_rows([src[n] for n in small_names], SMALL_ADAM_ROWS) for src in (wts, grads, mom, var)]
    outs = adamw(*packed4, name="adamw_small", tr=SMALL_ADAM_ROWS)
    for dst, buf in zip((delta, new_m, new_v), outs):
        for n, p in zip(small_names, _unpack_rows(buf, small_sz)):
            dst[n] = p.reshape(wts[n].shape)

    return (loss, grad_x.reshape(B, S, D), *[grads[n] for n in WEIGHT_ORDER], *[delta[n] for n in WEIGHT_ORDER],
            *[new_m[n] for n in WEIGHT_ORDER], *[new_v[n] for n in WEIGHT_ORDER])
```

```python
import functools

import jax
import jax.numpy as jnp
import numpy as np
from jax import lax
from jax.experimental import pallas as pl
from jax.experimental.pallas import tpu as pltpu

f32 = jnp.float32
bf16 = jnp.bfloat16
MXU_DTYPE = jnp.bfloat16
WIRE_DTYPE = jnp.bfloat16
INV_PRECISION = lax.Precision.HIGH

D_MODEL = 1024
FOX_HEADS = 8
FOX_HEAD_DIM = 64
FOX_WIDTH = 512
GDN_HEADS = 4
GDN_HEAD_DIM = 128
GDN_WIDTH = 512
CONV_WIDTH = 4
GDN_CHUNK = 64
XATTN_HEADS = 4
XATTN_HEAD_DIM = 128
XATTN_WIDTH = 512
D_FF = 4096
IN_DIM = 3600
EPS = 1e-6
NEG_INF = -1e30
LANES = 128
ADAM_LR = 0.001
ADAM_B1 = 0.9
ADAM_B2 = 0.999
ADAM_EPS = 1e-08
ADAM_WD = 0.01
ADAM_STEP = 10
VMEM_LIMIT = 48 * 1024 * 1024

COL_FOX = 0
COL_GDN = 1536
COL_Z = 3072
COL_SMALL = 3584
IN_ALIGNED = 3840
IN_TILE = 768
SM_F = 0
SM_B = 8
SM_A = 12


def _cparams(*sem):
    return pltpu.CompilerParams(dimension_semantics=sem, vmem_limit_bytes=VMEM_LIMIT)


def _mx(v):
    return v.astype(MXU_DTYPE)


def _dot(a, b, dims, precision=None):
    return lax.dot_general(a, b, (dims, ((), ())), preferred_element_type=f32, precision=precision)


def _dotm(a, b, dims):
    return _dot(_mx(a), _mx(b), dims)


NN = ((1,), (0,))
NT = ((1,), (1,))
TN = ((0,), (0,))


def matmul(a, b, *, name, ta=False, tb=False, b_stacked=False, out_stacked=False, residual=None, relu2_out=False,
           relu2_bwd_aux=None, out_dtype=f32, tm=1024, tn=1024, tk=1024):
    M, K = (a.shape[1], a.shape[0]) if ta else a.shape
    if b_stacked:
        b_cols = b.shape[2]
        N, tk = (b.shape[1], min(tk, b_cols)) if tb else (N_CHIPS * b_cols, tk)
        tn = tn if tb else min(tn, b_cols)
        assert K == (N_CHIPS * b_cols if tb else b.shape[1]), (name, a.shape, b.shape)
    else:
        N = b.shape[0] if tb else b.shape[1]
    if out_stacked:
        tn = min(tn, N // N_CHIPS)
    tm, tn, tk = min(tm, M), min(tn, N), min(tk, K)
    assert M % tm == 0 and N % tn == 0 and K % tk == 0, (name, M, N, K)
    nk = K // tk
    has_res = residual is not None
    has_aux = relu2_bwd_aux is not None

    def body(*refs):
        a_ref, b_ref = refs[0], refs[1]
        pos = 2
        res_ref = aux_ref = None
        if has_res:
            res_ref = refs[pos]
            pos += 1
        if has_aux:
            aux_ref = refs[pos]
            pos += 1
        o_ref = refs[pos]
        acc_ref = refs[pos + 1]
        k = pl.program_id(2)

        @pl.when(k == 0)
        def _():
            acc_ref[...] = jnp.zeros_like(acc_ref)

        dims = ((0,) if ta else (1,), (1,) if tb else (0,))
        acc_ref[...] += _dot(_mx(a_ref[...]), _mx(b_ref[...]), dims)

        @pl.when(k == nk - 1)
        def _():
            r = acc_ref[...]
            if has_res:
                r = r + res_ref[...]
            if has_aux:
                r = r * (2.0 * jnp.sqrt(aux_ref[...].astype(f32)))
            if relu2_out:
                o_ref[...] = jnp.square(jnp.maximum(r, 0.0)).astype(o_ref.dtype)
            else:
                o_ref[...] = r.astype(o_ref.dtype)

    a_spec = pl.BlockSpec((tk, tm), lambda i, j, k: (k, i)) if ta else pl.BlockSpec((tm, tk), lambda i, j, k: (i, k))
    if b_stacked and tb:
        per = b_cols // tk
        b_spec = pl.BlockSpec((None, tn, tk), lambda i, j, k: (k // per, j, k % per))
    elif b_stacked:
        per = b_cols // tn
        b_spec = pl.BlockSpec((None, tk, tn), lambda i, j, k: (j // per, k, j % per))
    else:
        b_spec = pl.BlockSpec((tn, tk), lambda i, j, k: (j, k)) if tb else pl.BlockSpec((tk, tn), lambda i, j, k: (k, j))
    if out_stacked:
        assert not (has_res or has_aux or relu2_out), name
        per_o = N // N_CHIPS // tn
        o_spec = pl.BlockSpec((None, tm, tn), lambda i, j, k: (j // per_o, i, j % per_o))
        out_full = (N_CHIPS, M, N // N_CHIPS)
    else:
        o_spec = pl.BlockSpec((tm, tn), lambda i, j, k: (i, j))
        out_full = (M, N)
    in_specs, args = [a_spec, b_spec], [a, b]
    if has_res:
        in_specs.append(o_spec)
        args.append(residual)
    if has_aux:
        in_specs.append(o_spec)
        args.append(relu2_bwd_aux)
    out_shape = [jax.ShapeDtypeStruct(out_full, out_dtype)]
    out_specs = [o_spec]
    res = pl.pallas_call(
        body, name=name, grid=(M // tm, N // tn, nk), in_specs=in_specs, out_specs=out_specs, out_shape=out_shape,
        scratch_shapes=[pltpu.VMEM((tm, tn), f32)],
        compiler_params=_cparams("parallel", "parallel", "arbitrary"),
    )(*args)
    return res[0]


def rms_fwd(x, g, *, name, tr=512):
    R, D = x.shape
    tr = min(tr, R)

    def body(x_ref, g_ref, o_ref):
        xv = x_ref[...]
        y = xv * lax.rsqrt(jnp.mean(xv * xv, axis=-1, keepdims=True) + EPS)
        o_ref[...] = (y * g_ref[...]).astype(o_ref.dtype)

    return pl.pallas_call(
        body, name=name, grid=(R // tr,),
        in_specs=[pl.BlockSpec((tr, D), lambda i: (i, 0)), pl.BlockSpec((1, D), lambda i: (0, 0))],
        out_specs=pl.BlockSpec((tr, D), lambda i: (i, 0)),
        out_shape=jax.ShapeDtypeStruct((R, D), MXU_DTYPE),
        compiler_params=_cparams("parallel"),
    )(x, g)


def rms_bwd(x, g, dh, residual, *, name, tr=512):
    R, D = x.shape
    tr = min(tr, R)
    has_res = residual is not None

    def body(*refs):
        if has_res:
            x_ref, g_ref, dh_ref, res_ref, dx_ref, dg_ref = refs
        else:
            x_ref, g_ref, dh_ref, dx_ref, dg_ref = refs
        xv = x_ref[...]
        rstd = lax.rsqrt(jnp.mean(xv * xv, axis=-1, keepdims=True) + EPS)
        xhat = xv * rstd
        dh = dh_ref[...].astype(f32)
        gd = dh * g_ref[...]
        dx = rstd * (gd - xhat * jnp.mean(gd * xhat, axis=-1, keepdims=True))
        if has_res:
            dx = dx + res_ref[...]
        dx_ref[...] = dx

        @pl.when(pl.program_id(0) == 0)
        def _():
            dg_ref[...] = jnp.zeros_like(dg_ref)

        dg_ref[...] += jnp.sum(dh * xhat, axis=0, keepdims=True)

    row = pl.BlockSpec((tr, D), lambda i: (i, 0))
    vec = pl.BlockSpec((1, D), lambda i: (0, 0))
    in_specs = [row, vec, row] + ([row] if has_res else [])
    args = [x, g, dh] + ([residual] if has_res else [])
    return pl.pallas_call(
        body, name=name, grid=(R // tr,), in_specs=in_specs, out_specs=[row, vec],
        out_shape=[jax.ShapeDtypeStruct((R, D), f32), jax.ShapeDtypeStruct((1, D), f32)],
        compiler_params=_cparams("arbitrary"),
    )(*args)


def loss_head(y, target, *, tr=512):
    R, D = y.shape
    tr = min(tr, R)

    def body(y_ref, t_ref, dy_ref, loss_ref):
        e = y_ref[...] - t_ref[...]
        dy_ref[...] = e * (1.0 / D)

        @pl.when(pl.program_id(0) == 0)
        def _():
            loss_ref[...] = jnp.zeros_like(loss_ref)

        part = 0.5 * jnp.sum(jnp.mean(e * e, axis=-1, keepdims=True), axis=0, keepdims=True)
        loss_ref[...] += jnp.broadcast_to(part, loss_ref.shape)

    row = pl.BlockSpec((tr, D), lambda i: (i, 0))
    return pl.pallas_call(
        body, name="loss_head", grid=(R // tr,), in_specs=[row, row],
        out_specs=[row, pl.BlockSpec((1, LANES), lambda i: (0, 0))],
        out_shape=[jax.ShapeDtypeStruct((R, D), f32), jax.ShapeDtypeStruct((1, LANES), f32)],
        compiler_params=_cparams("arbitrary"),
    )(y, target)


def _head_rms(v, g):
    r = lax.rsqrt(jnp.mean(v * v, axis=-1, keepdims=True) + EPS)
    return v * r * g, r


def _head_rms_bwd(v, r, g, dn):
    vhat = v * r
    gd = dn * g
    dv = r * (gd - vhat * jnp.mean(gd * vhat, axis=-1, keepdims=True))
    return dv, jnp.sum(dn * vhat, axis=0, keepdims=True)


def _softmax_rows(s):
    m = jnp.max(s, axis=-1, keepdims=True)
    e = jnp.exp(s - m)
    return e / jnp.sum(e, axis=-1, keepdims=True)


def xattn_fwd(cq, ckv, gq, gk, *, B, tq=512):
    T = cq.shape[0]
    S = T // B
    M = ckv.shape[0] // B
    tq = min(tq, S)
    nq = S // tq
    scale = XATTN_HEAD_DIM ** -0.5

    def body(q_ref, k_ref, v_ref, gq_ref, gk_ref, o_ref):
        qn, _ = _head_rms(q_ref[...], gq_ref[...])
        kn, _ = _head_rms(k_ref[...], gk_ref[...])
        p = _softmax_rows(_dot(_mx(qn), _mx(kn), NT) * scale)
        o_ref[...] = _dot(_mx(p), _mx(v_ref[...]), NN).astype(o_ref.dtype)

    hd = XATTN_HEAD_DIM
    vec = pl.BlockSpec((1, hd), lambda b, h, i: (0, 0))
    return pl.pallas_call(
        body, name="xattn_fwd", grid=(B, XATTN_HEADS, nq),
        in_specs=[pl.BlockSpec((tq, hd), lambda b, h, i: (b * nq + i, h)),
                  pl.BlockSpec((M, hd), lambda b, h, i: (b, h)),
                  pl.BlockSpec((M, hd), lambda b, h, i: (b, XATTN_HEADS + h)), vec, vec],
        out_specs=pl.BlockSpec((tq, hd), lambda b, h, i: (b * nq + i, h)),
        out_shape=jax.ShapeDtypeStruct((T, XATTN_WIDTH), MXU_DTYPE),
        compiler_params=_cparams("parallel", "parallel", "parallel"),
    )(cq, ckv, ckv, gq, gk)


def xattn_bwd(cq, ckv, gq, gk, dco, *, B, tq=512):
    T = cq.shape[0]
    S = T // B
    M = ckv.shape[0] // B
    tq = min(tq, S)
    nq = S // tq
    scale = XATTN_HEAD_DIM ** -0.5
    hd = XATTN_HEAD_DIM

    def body(q_ref, k_ref, v_ref, gq_ref, gk_ref, do_ref, dq_ref, dk_ref, dv_ref, dgq_ref, dgk_ref, dkn_acc, dv_acc):
        b, h, i = pl.program_id(0), pl.program_id(1), pl.program_id(2)

        @pl.when((b == 0) & (h == 0) & (i == 0))
        def _():
            dgq_ref[...] = jnp.zeros_like(dgq_ref)
            dgk_ref[...] = jnp.zeros_like(dgk_ref)

        @pl.when(i == 0)
        def _():
            dkn_acc[...] = jnp.zeros_like(dkn_acc)
            dv_acc[...] = jnp.zeros_like(dv_acc)

        q, k, v = q_ref[...], k_ref[...], v_ref[...]
        gqv, gkv = gq_ref[...], gk_ref[...]
        qn, rq = _head_rms(q, gqv)
        kn, rk = _head_rms(k, gkv)
        p = _softmax_rows(_dot(_mx(qn), _mx(kn), NT) * scale)
        do = do_ref[...]
        dv_acc[...] += _dot(_mx(p), _mx(do), TN)
        dp = _dot(_mx(do), _mx(v), NT)
        ds = p * (dp - jnp.sum(dp * p, axis=-1, keepdims=True)) * scale
        dqn = _dot(_mx(ds), _mx(kn), NN)
        dkn_acc[...] += _dot(_mx(ds), _mx(qn), TN)
        dq, dgq = _head_rms_bwd(q, rq, gqv, dqn)
        dq_ref[...] = dq.astype(dq_ref.dtype)
        dgq_ref[...] += dgq

        @pl.when(i == nq - 1)
        def _():
            dk, dgk = _head_rms_bwd(k, rk, gkv, dkn_acc[...])
            dk_ref[...] = dk.astype(dk_ref.dtype)
            dv_ref[...] = dv_acc[...].astype(dv_ref.dtype)
            dgk_ref[...] += dgk

    vec = pl.BlockSpec((1, hd), lambda b, h, i: (0, 0))
    qspec = pl.BlockSpec((tq, hd), lambda b, h, i: (b * nq + i, h))
    kspec = pl.BlockSpec((M, hd), lambda b, h, i: (b, h))
    vspec = pl.BlockSpec((M, hd), lambda b, h, i: (b, XATTN_HEADS + h))
    dq, dk, dv, dgq, dgk = pl.pallas_call(
        body, name="xattn_bwd", grid=(B, XATTN_HEADS, nq),
        in_specs=[qspec, kspec, vspec, vec, vec, qspec],
        out_specs=[qspec, kspec, kspec, vec, vec],
        out_shape=[jax.ShapeDtypeStruct((T, XATTN_WIDTH), MXU_DTYPE),
                   jax.ShapeDtypeStruct((B * M, XATTN_WIDTH), MXU_DTYPE),
                   jax.ShapeDtypeStruct((B * M, XATTN_WIDTH), MXU_DTYPE),
                   jax.ShapeDtypeStruct((1, hd), f32), jax.ShapeDtypeStruct((1, hd), f32)],
        scratch_shapes=[pltpu.VMEM((M, hd), f32), pltpu.VMEM((M, hd), f32)],
        compiler_params=_cparams("arbitrary", "arbitrary", "arbitrary"),
    )(cq, ckv, ckv, gq, gk, dco)
    return dq, jnp.concatenate([dk, dv], axis=1), dgq, dgk


FOX_PAIRS = FOX_HEADS // 2


def _fox_scores(qn, kn, ccol, crow, q0, tq, S, scale):
    s = _dot(_mx(qn), _mx(kn), NT) * scale + ccol - crow
    qpos = q0 + lax.broadcasted_iota(jnp.int32, (tq, S), 0)
    kpos = lax.broadcasted_iota(jnp.int32, (tq, S), 1)
    return jnp.where(kpos <= qpos, s, NEG_INF)


def fox_fwd(P, ccol, crow, gq, gk, go, *, B, tq=256):
    T = P.shape[0]
    S = T // B
    tq = min(tq, S)
    nq = S // tq
    hd = FOX_HEAD_DIM
    scale = hd ** -0.5

    def body(q_ref, k_ref, v_ref, ccol_ref, crow_ref, gq_ref, gk_ref, go_ref, o_ref, oa_ref):
        q0 = pl.program_id(2) * tq
        for e in range(2):
            sl = slice(e * hd, (e + 1) * hd)
            qn, _ = _head_rms(q_ref[:, sl], gq_ref[:, sl])
            kn, _ = _head_rms(k_ref[:, sl], gk_ref[:, sl])
            p = _softmax_rows(_fox_scores(qn, kn, ccol_ref[0, e], crow_ref[0, e], q0, tq, S, scale))
            o = _dot(_mx(p), _mx(v_ref[:, sl]), NN)
            o_ref[:, sl] = o
            oa_ref[:, sl] = _head_rms(o, go_ref[:, sl])[0].astype(oa_ref.dtype)

    W = 2 * hd
    vec = pl.BlockSpec((1, W), lambda b, h, i: (0, 0))
    ospec = pl.BlockSpec((tq, W), lambda b, h, i: (b * nq + i, h))
    return pl.pallas_call(
        body, name="fox_fwd", grid=(B, FOX_PAIRS, nq),
        in_specs=[pl.BlockSpec((tq, W), lambda b, h, i: (b * nq + i, h)),
                  pl.BlockSpec((S, W), lambda b, h, i: (b, FOX_PAIRS + h)),
                  pl.BlockSpec((S, W), lambda b, h, i: (b, 2 * FOX_PAIRS + h)),
                  pl.BlockSpec((1, 2, tq, 1), lambda b, h, i: (b, h, i, 0)),
                  pl.BlockSpec((1, 2, 1, S), lambda b, h, i: (b, h, 0, 0)), vec, vec, vec],
        out_specs=[ospec, ospec],
        out_shape=[jax.ShapeDtypeStruct((T, FOX_WIDTH), f32), jax.ShapeDtypeStruct((T, FOX_WIDTH), MXU_DTYPE)],
        compiler_params=_cparams("parallel", "parallel", "parallel"),
    )(P, P, P, ccol, crow, gq, gk, go)


def fox_bwd(P, ccol, crow, gq, gk, go, o_raw, d_oab, *, B, tq=256):
    T = P.shape[0]
    S = T // B
    tq = min(tq, S)
    nq = S // tq
    hd = FOX_HEAD_DIM
    scale = hd ** -0.5

    def body(q_ref, k_ref, v_ref, ccol_ref, crow_ref, gq_ref, gk_ref, go_ref, o_ref, doa_ref,
             dq_ref, dk_ref, dv_ref, dccol_ref, dcrow_ref, dgq_ref, dgk_ref, dgo_ref, dkn_acc, dv_acc, dcrow_acc):
        b, h, i = pl.program_id(0), pl.program_id(1), pl.program_id(2)
        q0 = i * tq

        @pl.when((b == 0) & (h == 0) & (i == 0))
        def _():
            dgq_ref[...] = jnp.zeros_like(dgq_ref)
            dgk_ref[...] = jnp.zeros_like(dgk_ref)
            dgo_ref[...] = jnp.zeros_like(dgo_ref)

        @pl.when(i == 0)
        def _():
            dkn_acc[...] = jnp.zeros_like(dkn_acc)
            dv_acc[...] = jnp.zeros_like(dv_acc)
            dcrow_acc[...] = jnp.zeros_like(dcrow_acc)

        for e in range(2):
            sl = slice(e * hd, (e + 1) * hd)
            q, k, v = q_ref[:, sl], k_ref[:, sl], v_ref[:, sl]
            gqv, gkv, gov = gq_ref[:, sl], gk_ref[:, sl], go_ref[:, sl]
            qn, rq = _head_rms(q, gqv)
            kn, rk = _head_rms(k, gkv)
            p = _softmax_rows(_fox_scores(qn, kn, ccol_ref[0, e], crow_ref[0, e], q0, tq, S, scale))
            o = o_ref[:, sl]
            ro = lax.rsqrt(jnp.mean(o * o, axis=-1, keepdims=True) + EPS)
            do, dgo = _head_rms_bwd(o, ro, gov, doa_ref[:, sl])
            dgo_ref[:, sl] += dgo
            dv_acc[e] += _dot(_mx(p), _mx(do), TN)
            dp = _dot(_mx(do), _mx(v), NT)
            ds = p * (dp - jnp.sum(do * o, axis=-1, keepdims=True))
            dccol_ref[0, e] = jnp.sum(ds, axis=1, keepdims=True)
            dcrow_acc[e] -= jnp.sum(ds, axis=0, keepdims=True)
            dqn = _dot(_mx(ds), _mx(kn), NN) * scale
            dkn_acc[e] += _dot(_mx(ds), _mx(qn), TN) * scale
            dq, dgq = _head_rms_bwd(q, rq, gqv, dqn)
            dq_ref[:, sl] = dq.astype(dq_ref.dtype)
            dgq_ref[:, sl] += dgq

        @pl.when(i == nq - 1)
        def _():
            for e in range(2):
                sl = slice(e * hd, (e + 1) * hd)
                k = k_ref[:, sl]
                gkv = gk_ref[:, sl]
                rk = lax.rsqrt(jnp.mean(k * k, axis=-1, keepdims=True) + EPS)
                dk, dgk = _head_rms_bwd(k, rk, gkv, dkn_acc[e])
                dk_ref[:, sl] = dk.astype(dk_ref.dtype)
                dv_ref[:, sl] = dv_acc[e].astype(dv_ref.dtype)
                dgk_ref[:, sl] += dgk
                dcrow_ref[0, e] = dcrow_acc[e]

    W = 2 * hd
    vec = pl.BlockSpec((1, W), lambda b, h, i: (0, 0))
    qspec = pl.BlockSpec((tq, W), lambda b, h, i: (b * nq + i, h))
    kvout = pl.BlockSpec((S, W), lambda b, h, i: (b, h))
    colspec = pl.BlockSpec((1, 2, tq, 1), lambda b, h, i: (b, h, i, 0))
    rowspec = pl.BlockSpec((1, 2, 1, S), lambda b, h, i: (b, h, 0, 0))
    return pl.pallas_call(
        body, name="fox_bwd", grid=(B, FOX_PAIRS, nq),
        in_specs=[qspec,
                  pl.BlockSpec((S, W), lambda b, h, i: (b, FOX_PAIRS + h)),
                  pl.BlockSpec((S, W), lambda b, h, i: (b, 2 * FOX_PAIRS + h)),
                  colspec, rowspec, vec, vec, vec, qspec, qspec],
        out_specs=[qspec, kvout, kvout, colspec, rowspec, vec, vec, vec],
        out_shape=[jax.ShapeDtypeStruct((T, FOX_WIDTH), MXU_DTYPE), jax.ShapeDtypeStruct((T, FOX_WIDTH), MXU_DTYPE),
                   jax.ShapeDtypeStruct((T, FOX_WIDTH), MXU_DTYPE),
                   jax.ShapeDtypeStruct((B, FOX_HEADS, S, 1), f32), jax.ShapeDtypeStruct((B, FOX_HEADS, 1, S), f32),
                   jax.ShapeDtypeStruct((1, W), f32), jax.ShapeDtypeStruct((1, W), f32), jax.ShapeDtypeStruct((1, W), f32)],
        scratch_shapes=[pltpu.VMEM((2, S, hd), f32), pltpu.VMEM((2, S, hd), f32), pltpu.VMEM((2, 1, S), f32)],
        compiler_params=_cparams("arbitrary", "arbitrary", "arbitrary"),
    )(P, P, P, ccol, crow, gq, gk, go, o_raw, d_oab)


FOX_TQ = 512
FOX_TK = 512
GROUP_PRECISION = lax.Precision.HIGH


def _head_mean(v):
    n = v.shape[1]
    r = lax.broadcasted_iota(jnp.int32, (n, n), 0) // FOX_HEAD_DIM
    c = lax.broadcasted_iota(jnp.int32, (n, n), 1) // FOX_HEAD_DIM
    return _dot(v, (r == c).astype(f32), NN, GROUP_PRECISION) * (1.0 / FOX_HEAD_DIM)


def fox_prep_fwd(P, gq, gk, *, tr=512):
    T = P.shape[0]
    tr = min(tr, T)
    scale = FOX_HEAD_DIM ** -0.5

    def body(q_ref, k_ref, v_ref, gq_ref, gk_ref, qn_ref, kn_ref, vb_ref):
        q, k = q_ref[...], k_ref[...]
        qn_ref[...] = (q * lax.rsqrt(_head_mean(q * q) + EPS) * (gq_ref[...] * scale)).astype(qn_ref.dtype)
        kn_ref[...] = (k * lax.rsqrt(_head_mean(k * k) + EPS) * gk_ref[...]).astype(kn_ref.dtype)
        vb_ref[...] = v_ref[...].astype(vb_ref.dtype)

    W = FOX_WIDTH
    col = lambda j: pl.BlockSpec((tr, W), lambda i: (i, j))
    vec = pl.BlockSpec((1, W), lambda i: (0, 0))
    out = jax.ShapeDtypeStruct((T, W), MXU_DTYPE)
    return pl.pallas_call(
        body, name="fox_prep_fwd", grid=(T // tr,), in_specs=[col(0), col(1), col(2), vec, vec],
        out_specs=[col(0)] * 3, out_shape=[out] * 3, compiler_params=_cparams("parallel"),
    )(P, P, P, gq, gk)


def fox_prep_bwd(P, gq, gk, dqn, dkn, *, tr=512):
    T = P.shape[0]
    tr = min(tr, T)
    scale = FOX_HEAD_DIM ** -0.5

    def body(q_ref, k_ref, gq_ref, gk_ref, dqn_ref, dkn_ref, dq_ref, dk_ref, dgq_ref, dgk_ref):
        @pl.when(pl.program_id(0) == 0)
        def _():
            dgq_ref[...] = jnp.zeros_like(dgq_ref)
            dgk_ref[...] = jnp.zeros_like(dgk_ref)

        def one(x, g, dn, dx_ref, dg_ref):
            r = lax.rsqrt(_head_mean(x * x) + EPS)
            xhat = x * r
            gd = dn * g
            dx_ref[...] = (r * (gd - xhat * _head_mean(gd * xhat))).astype(dx_ref.dtype)
            return jnp.sum(dn * xhat, axis=0, keepdims=True)

        dgq_ref[...] += scale * one(q_ref[...], gq_ref[...] * scale, dqn_ref[...], dq_ref, dgq_ref)
        dgk_ref[...] += one(k_ref[...], gk_ref[...], dkn_ref[...], dk_ref, dgk_ref)

    W = FOX_WIDTH
    col = lambda j: pl.BlockSpec((tr, W), lambda i: (i, j))
    vec = pl.BlockSpec((1, W), lambda i: (0, 0))
    return pl.pallas_call(
        body, name="fox_prep_bwd", grid=(T // tr,), in_specs=[col(0), col(1), vec, vec, col(0), col(0)],
        out_specs=[col(0), col(0), vec, vec],
        out_shape=[jax.ShapeDtypeStruct((T, W), MXU_DTYPE), jax.ShapeDtypeStruct((T, W), MXU_DTYPE),
                   jax.ShapeDtypeStruct((1, W), f32), jax.ShapeDtypeStruct((1, W), f32)],
        compiler_params=_cparams("arbitrary"),
    )(P, P, gq, gk, dqn, dkn)


def _fox_tile_scores(q, k_ref, ccol_ref, cq, e, j, sl, mask_off):
    tq, tk = FOX_TQ, FOX_TK
    rows = pl.ds(pl.multiple_of(j * tk, tk), tk)
    k = k_ref[rows, sl]
    s = _dot(k, q, NT) + cq - ccol_ref[0, e, rows, :]
    if mask_off is not None:
        key = lax.broadcasted_iota(jnp.int32, (tk, tq), 0) + mask_off
        query = lax.broadcasted_iota(jnp.int32, (tk, tq), 1)
        s = jnp.where(key <= query, s, NEG_INF)
    return s, k, rows


def _fox_sweep(i, update, carry):
    nd = FOX_TQ // FOX_TK
    carry = lax.fori_loop(0, i * nd, lambda j, cr: update(cr, j, None), carry)
    for d in range(nd):
        carry = update(carry, i * nd + d, d * FOX_TK)
    return carry


def fox_core_fwd(qn, kn, vb, ccol, crow, go, *, B):
    T = qn.shape[0]
    S = T // B
    tq = FOX_TQ
    nq = S // tq
    hd = FOX_HEAD_DIM

    def body(q_ref, k_ref, v_ref, ccol_ref, crow_ref, go_ref, o_ref, oa_ref, lse_ref):
        i = pl.program_id(2)
        for e in range(2):
            sl = slice(e * hd, (e + 1) * hd)
            q = q_ref[:, sl]
            cq = crow_ref[0, e, i]

            def update(carry, j, mask_off):
                m, l, acc = carry
                s, _, rows = _fox_tile_scores(q, k_ref, ccol_ref, cq, e, j, sl, mask_off)
                m2 = jnp.maximum(m, jnp.max(s, axis=0, keepdims=True))
                a = jnp.exp(m - m2)
                p = jnp.exp(s - m2)
                return m2, a * l + jnp.sum(p, axis=0, keepdims=True), a * acc + _dot(v_ref[rows, sl], _mx(p), TN)

            carry = (jnp.full((1, tq), NEG_INF, f32), jnp.zeros((1, tq), f32), jnp.zeros((hd, tq), f32))
            m, l, acc = _fox_sweep(i, update, carry)
            o = (acc / l).T
            o_ref[:, sl] = o
            oa_ref[:, sl] = _head_rms(o, go_ref[:, sl])[0].astype(oa_ref.dtype)
            lse_ref[0, e, 0] = m + jnp.log(l)

    W = 2 * hd
    qspec = pl.BlockSpec((tq, W), lambda b, h, i: (b * nq + i, h))
    kspec = pl.BlockSpec((S, W), lambda b, h, i: (b, h))
    return pl.pallas_call(
        body, name="fox_core_fwd", grid=(B, FOX_PAIRS, nq),
        in_specs=[qspec, kspec, kspec, pl.BlockSpec((1, 2, S, 1), lambda b, h, i: (b, h, 0, 0)),
                  pl.BlockSpec((1, 2, nq, 1, tq), lambda b, h, i: (b, h, 0, 0, 0)),
                  pl.BlockSpec((1, W), lambda b, h, i: (0, 0))],
        out_specs=[qspec, qspec, pl.BlockSpec((1, 2, 1, 1, tq), lambda b, h, i: (b, h, i, 0, 0))],
        out_shape=[jax.ShapeDtypeStruct((T, FOX_WIDTH), f32), jax.ShapeDtypeStruct((T, FOX_WIDTH), MXU_DTYPE),
                   jax.ShapeDtypeStruct((B, FOX_HEADS, nq, 1, tq), f32)],
        compiler_params=_cparams("parallel", "parallel", "parallel"),
    )(qn, kn, vb, ccol, crow, go)


def fox_core_bwd(qn, kn, vb, ccol, crow, go, o_raw, lse, d_oab, *, B):
    T = qn.shape[0]
    S = T // B
    tq = FOX_TQ
    nq = S // tq
    hd = FOX_HEAD_DIM

    def body(q_ref, k_ref, v_ref, ccol_ref, crow_ref, go_ref, o_ref, lse_ref, doa_ref,
             dq_ref, dk_ref, dv_ref, dccol_ref, dcrow_ref, dgo_ref, dk_acc, dv_acc, dck_acc):
        b, h, i = pl.program_id(0), pl.program_id(1), pl.program_id(2)

        @pl.when((b == 0) & (h == 0) & (i == 0))
        def _():
            dgo_ref[...] = jnp.zeros_like(dgo_ref)

        @pl.when(i == 0)
        def _():
            dk_acc[...] = jnp.zeros_like(dk_acc)
            dv_acc[...] = jnp.zeros_like(dv_acc)
            dck_acc[...] = jnp.zeros_like(dck_acc)

        for e in range(2):
            sl = slice(e * hd, (e + 1) * hd)
            q = q_ref[:, sl]
            cq = crow_ref[0, e, i]
            lse_e = lse_ref[0, e, 0]
            o = o_ref[:, sl]
            ro = lax.rsqrt(jnp.mean(o * o, axis=-1, keepdims=True) + EPS)
            do, dgo = _head_rms_bwd(o, ro, go_ref[:, sl], doa_ref[:, sl])
            dgo_ref[:, sl] += dgo
            delta = jnp.sum((do * o).T, axis=0, keepdims=True)
            do_b = _mx(do)

            def update(carry, j, mask_off):
                dq, dcq = carry
                s, k, rows = _fox_tile_scores(q, k_ref, ccol_ref, cq, e, j, sl, mask_off)
                p = jnp.exp(s - lse_e)
                dv_acc[e, rows, :] += _dot(_mx(p), do_b, NN)
                ds = p * (_dot(v_ref[rows, sl], do_b, NT) - delta)
                dck_acc[e, rows, :] -= jnp.sum(ds, axis=1, keepdims=True)
                ds_b = _mx(ds)
                dk_acc[e, rows, :] += _dot(ds_b, q, NN)
                return dq + _dot(ds_b, k, TN), dcq + jnp.sum(ds, axis=0, keepdims=True)

            dq, dcq = _fox_sweep(i, update, (jnp.zeros((tq, hd), f32), jnp.zeros((1, tq), f32)))
            dq_ref[:, sl] = dq
            dcrow_ref[0, e, 0] = dcq

        @pl.when(i == nq - 1)
        def _():
            for e in range(2):
                sl = slice(e * hd, (e + 1) * hd)
                dk_ref[:, sl] = dk_acc[e]
                dv_ref[:, sl] = dv_acc[e].astype(dv_ref.dtype)
            dccol_ref[0] = dck_acc[...]

    W = 2 * hd
    qspec = pl.BlockSpec((tq, W), lambda b, h, i: (b * nq + i, h))
    kspec = pl.BlockSpec((S, W), lambda b, h, i: (b, h))
    colspec = pl.BlockSpec((1, 2, S, 1), lambda b, h, i: (b, h, 0, 0))
    rowspec = pl.BlockSpec((1, 2, nq, 1, tq), lambda b, h, i: (b, h, 0, 0, 0))
    tilespec = pl.BlockSpec((1, 2, 1, 1, tq), lambda b, h, i: (b, h, i, 0, 0))
    vec = pl.BlockSpec((1, W), lambda b, h, i: (0, 0))
    return pl.pallas_call(
        body, name="fox_core_bwd", grid=(B, FOX_PAIRS, nq),
        in_specs=[qspec, kspec, kspec, colspec, rowspec, vec, qspec, tilespec, qspec],
        out_specs=[qspec, kspec, kspec, colspec, tilespec, vec],
        out_shape=[jax.ShapeDtypeStruct((T, FOX_WIDTH), f32), jax.ShapeDtypeStruct((T, FOX_WIDTH), f32),
                   jax.ShapeDtypeStruct((T, FOX_WIDTH), MXU_DTYPE),
                   jax.ShapeDtypeStruct((B, FOX_HEADS, S, 1), f32), jax.ShapeDtypeStruct((B, FOX_HEADS, nq, 1, tq), f32),
                   jax.ShapeDtypeStruct((1, W), f32)],
        scratch_shapes=[pltpu.VMEM((2, S, hd), f32), pltpu.VMEM((2, S, hd), f32), pltpu.VMEM((2, S, 1), f32)],
        compiler_params=_cparams("arbitrary", "arbitrary", "arbitrary"),
    )(qn, kn, vb, ccol, crow, go, o_raw, lse, d_oab)


def _lane_mask(lo, hi, shape):
    lane = lax.broadcasted_iota(jnp.int32, shape, 1)
    return (lane >= lo) & (lane < hi)


def _cumsum_rows(v, period, reverse=False):
    n = v.shape[0]
    pos = lax.broadcasted_iota(jnp.int32, v.shape, 0) % period
    sh = 1
    while sh < period:
        if reverse:
            v = v + jnp.where(pos + sh < period, pltpu.roll(v, n - sh, 0), 0.0)
        else:
            v = v + jnp.where(pos >= sh, pltpu.roll(v, sh, 0), 0.0)
        sh *= 2
    return v


def _gate_values(z, bias, alog):
    zb = z + bias
    ls = jax.nn.log_sigmoid(zb)
    beta = jax.nn.sigmoid(z)
    g = -jnp.exp(alog) * jax.nn.softplus(zb)
    return zb, ls, beta, g


def gates_fwd(P, bias, alog, *, B):
    T = P.shape[0]
    S = T // B

    def body(z_ref, bias_ref, alog_ref, o_ref):
        z = z_ref[...]
        _, ls, beta, g = _gate_values(z, bias_ref[...], alog_ref[...])
        c = _cumsum_rows(ls, S)
        gc = _cumsum_rows(g, GDN_CHUNK)
        o = jnp.where(_lane_mask(SM_F, SM_F + FOX_HEADS, z.shape), c, 0.0)
        o = jnp.where(_lane_mask(SM_B, SM_B + GDN_HEADS, z.shape), beta, o)
        o = jnp.where(_lane_mask(SM_A, SM_A + GDN_HEADS, z.shape), gc, o)
        o_ref[...] = o

    vec = pl.BlockSpec((1, LANES), lambda b: (0, 0))
    return pl.pallas_call(
        body, name="gates_fwd", grid=(B,),
        in_specs=[pl.BlockSpec((S, LANES), lambda b: (b, COL_SMALL // LANES)), vec, vec],
        out_specs=pl.BlockSpec((S, LANES), lambda b: (b, 0)),
        out_shape=jax.ShapeDtypeStruct((T, LANES), f32),
        compiler_params=_cparams("parallel"),
    )(P, bias, alog)


def gates_bwd(P, bias, alog, dgates, *, B):
    T = P.shape[0]
    S = T // B

    def body(z_ref, bias_ref, alog_ref, dg_ref, dz_ref, par_ref):
        z = z_ref[...]
        zb, ls, beta, g = _gate_values(z, bias_ref[...], alog_ref[...])
        d = dg_ref[...]
        dls = _cumsum_rows(d, S, reverse=True)
        dgr = _cumsum_rows(d, GDN_CHUNK, reverse=True)
        sig = jax.nn.sigmoid(zb)
        dz_f = dls * (1.0 - sig)
        dz_b = d * beta * (1.0 - beta)
        dz_a = dgr * (-jnp.exp(alog_ref[...])) * sig
        dz = jnp.where(_lane_mask(SM_F, SM_F + FOX_HEADS, z.shape), dz_f, 0.0)
        dz = jnp.where(_lane_mask(SM_B, SM_B + GDN_HEADS, z.shape), dz_b, dz)
        dz = jnp.where(_lane_mask(SM_A, SM_A + GDN_HEADS, z.shape), dz_a, dz)
        dz_ref[...] = dz.astype(dz_ref.dtype)

        @pl.when(pl.program_id(0) == 0)
        def _():
            par_ref[...] = jnp.zeros_like(par_ref)

        dalog = jnp.where(_lane_mask(SM_A, SM_A + GDN_HEADS, z.shape), dgr * g, 0.0)
        par_ref[0:1, :] += jnp.sum(dz, axis=0, keepdims=True)
        par_ref[1:2, :] += jnp.sum(dalog, axis=0, keepdims=True)

    vec = pl.BlockSpec((1, LANES), lambda b: (0, 0))
    return pl.pallas_call(
        body, name="gates_bwd", grid=(B,),
        in_specs=[pl.BlockSpec((S, LANES), lambda b: (b, COL_SMALL // LANES)), vec, vec,
                  pl.BlockSpec((S, LANES), lambda b: (b, 0))],
        out_specs=[pl.BlockSpec((S, LANES), lambda b: (b, 0)), pl.BlockSpec((8, LANES), lambda b: (0, 0))],
        out_shape=[jax.ShapeDtypeStruct((T, LANES), MXU_DTYPE), jax.ShapeDtypeStruct((8, LANES), f32)],
        compiler_params=_cparams("arbitrary"),
    )(P, bias, alog, dgates)


GDN_BLOCKS = 3 * GDN_HEADS


def _shift_rows(v, d, reverse=False):
    if d == 0:
        return v
    n = v.shape[0]
    row = lax.broadcasted_iota(jnp.int32, v.shape, 0)
    if reverse:
        return jnp.where(row + d < n, pltpu.roll(v, n - d, 0), 0.0)
    return jnp.where(row >= d, pltpu.roll(v, d, 0), 0.0)


def _conv_silu(x, w):
    pre = sum(w[j:j + 1, :] * _shift_rows(x, CONV_WIDTH - 1 - j) for j in range(CONV_WIDTH))
    return pre, pre * jax.nn.sigmoid(pre)


def gdn_prep_fwd(P, conv_w, *, B):
    T = P.shape[0]
    S = T // B

    def body(x_ref, w_ref, o_ref):
        _, y = _conv_silu(x_ref[...], w_ref[...])
        yn = y * lax.rsqrt(jnp.sum(y * y, axis=-1, keepdims=True) + EPS)
        o_ref[...] = jnp.where(pl.program_id(1) < 2 * GDN_HEADS, yn, y)

    return pl.pallas_call(
        body, name="gdn_prep_fwd", grid=(B, GDN_BLOCKS),
        in_specs=[pl.BlockSpec((S, LANES), lambda b, j: (b, COL_GDN // LANES + j)),
                  pl.BlockSpec((CONV_WIDTH, LANES), lambda b, j: (0, j))],
        out_specs=pl.BlockSpec((S, LANES), lambda b, j: (b, j)),
        out_shape=jax.ShapeDtypeStruct((T, 3 * GDN_WIDTH), f32),
        compiler_params=_cparams("parallel", "parallel"),
    )(P, conv_w)


def gdn_prep_bwd(P, conv_w, dG, *, B):
    T = P.shape[0]
    S = T // B

    def body(x_ref, w_ref, dg_ref, dx_ref, dw_ref):
        x, w = x_ref[...], w_ref[...]
        pre, y = _conv_silu(x, w)
        dn = dg_ref[...]
        r = lax.rsqrt(jnp.sum(y * y, axis=-1, keepdims=True) + EPS)
        n = y * r
        dy_norm = r * (dn - n * jnp.sum(dn * n, axis=-1, keepdims=True))
        dy = jnp.where(pl.program_id(0) < 2 * GDN_HEADS, dy_norm, dn)
        sg = jax.nn.sigmoid(pre)
        dpre = dy * (sg * (1.0 + pre * (1.0 - sg)))
        dx = sum(w[j:j + 1, :] * _shift_rows(dpre, CONV_WIDTH - 1 - j, reverse=True) for j in range(CONV_WIDTH))
        dx_ref[...] = dx.astype(dx_ref.dtype)

        @pl.when(pl.program_id(1) == 0)
        def _():
            dw_ref[...] = jnp.zeros_like(dw_ref)

        for j in range(CONV_WIDTH):
            dw_ref[j:j + 1, :] += jnp.sum(dpre * _shift_rows(x, CONV_WIDTH - 1 - j), axis=0, keepdims=True)

    return pl.pallas_call(
        body, name="gdn_prep_bwd", grid=(GDN_BLOCKS, B),
        in_specs=[pl.BlockSpec((S, LANES), lambda j, b: (b, COL_GDN // LANES + j)),
                  pl.BlockSpec((CONV_WIDTH, LANES), lambda j, b: (0, j)),
                  pl.BlockSpec((S, LANES), lambda j, b: (b, j))],
        out_specs=[pl.BlockSpec((S, LANES), lambda j, b: (b, j)),
                   pl.BlockSpec((CONV_WIDTH, LANES), lambda j, b: (0, j))],
        out_shape=[jax.ShapeDtypeStruct((T, 3 * GDN_WIDTH), MXU_DTYPE),
                   jax.ShapeDtypeStruct((CONV_WIDTH, 3 * GDN_WIDTH), f32)],
        compiler_params=_cparams("arbitrary", "arbitrary"),
    )(P, conv_w, dG)


GDN_GROUP = 16
B_NN = (((2,), (1,)), ((0,), (0,)))
B_NT = (((2,), (2,)), ((0,), (0,)))
B_TN = (((1,), (1,)), ((0,), (0,)))


def _bmm(a, b, dims, precision=None):
    if precision is None:
        a, b = _mx(a), _mx(b)
    return lax.dot_general(a, b, dims, preferred_element_type=f32, precision=precision)


def _tri_inverse(A):
    C = A.shape[-1]
    row = lax.broadcasted_iota(jnp.int32, A.shape, 1)
    col = lax.broadcasted_iota(jnp.int32, A.shape, 2)
    eye = (row == col).astype(f32)
    X = jnp.where((row // 4) == (col // 4), -A, 0.0)
    X2 = _bmm(X, X, B_NN, INV_PRECISION)
    Tm = eye + X + X2 + _bmm(X, X2, B_NN, INV_PRECISION)
    b = 4
    while b < C:
        off = ((row // (2 * b)) == (col // (2 * b))) & ((row // b) != (col // b))
        Tm = Tm - _bmm(_bmm(Tm, jnp.where(off, A, 0.0), B_NN, INV_PRECISION), Tm, B_NN, INV_PRECISION)
        b *= 2
    return Tm


def _pick_lane(block, lane_idx):
    lane = lax.broadcasted_iota(jnp.int32, block.shape, 1)
    return jnp.sum(jnp.where(lane == lane_idx, block, 0.0), axis=1, keepdims=True)


def _gdn_local(q, k, v, beta, gc, Tm=None):
    C = GDN_CHUNK
    n = q.shape[0] // C
    q = q.reshape(n, C, -1) * (GDN_HEAD_DIM ** -0.5)
    k = k.reshape(n, C, -1)
    v = v.reshape(n, C, -1)
    beta = beta.reshape(n, C, 1)
    gc = gc.reshape(n, C, 1)
    row = lax.broadcasted_iota(jnp.int32, (n, C, C), 1)
    col = lax.broadcasted_iota(jnp.int32, (n, C, C), 2)
    gcT = jnp.swapaxes(jnp.broadcast_to(gc, (n, C, C)), 1, 2)
    D = jnp.exp(jnp.where(row >= col, gc - gcT, NEG_INF))
    kb = k * beta
    vb = v * beta
    A = jnp.where(row > col, _bmm(kb, k, B_NT) * D, 0.0)
    Gam = jnp.exp(gc)
    kg = kb * Gam
    gl = gc[:, C - 1:C, :]
    kdec = jnp.exp(gl - gc)
    loc = dict(q=q, k=k, v=v, beta=beta, gc=gc, D=D, kb=kb, vb=vb, A=A, Gam=Gam, kg=kg,
               kdec=kdec, kd=k * kdec, qg=q * Gam, gam=jnp.exp(gl), row=row, col=col)
    if Tm is None:
        Tm = _tri_inverse(A)
        loc.update(u=_bmm(Tm, vb, B_NN), w=_bmm(Tm, kg, B_NN), M=_bmm(q, k, B_NT) * D)
    else:
        Tm = Tm.reshape(n, C, C)
    loc["Tm"] = Tm
    return loc


def _gdn_store_local(loc, r0, u_s, w_s, qg_s, kd_s, M_s, gam_s, c0):
    n = loc["u"].shape[0]
    R = n * GDN_CHUNK
    u_s[pl.ds(r0, R), :] = loc["u"].reshape(R, -1)
    w_s[pl.ds(r0, R), :] = loc["w"].reshape(R, -1)
    qg_s[pl.ds(r0, R), :] = loc["qg"].reshape(R, -1)
    kd_s[pl.ds(r0, R), :] = loc["kd"].reshape(R, -1)
    M_s[pl.ds(r0, R), :] = loc["M"].reshape(R, -1)
    gam_s[pl.ds(c0, n)] = jnp.broadcast_to(loc["gam"], (n, 1, LANES))


def _gdn_specs(S):
    blk = lambda off: pl.BlockSpec((S, LANES), lambda b, h: (b, off + h))
    return blk


def gdn_fwd(G, gates, P, g_on, *, B):
    T = G.shape[0]
    S = T // B
    C = GDN_CHUNK
    N = S // C
    grp = min(GDN_GROUP, N)
    R = grp * C
    hd = GDN_HEAD_DIM

    def body(q_ref, k_ref, v_ref, gt_ref, z_ref, gon_ref, o_ref, ob_ref, st_ref, u_s, w_s, qg_s, kd_s, M_s, gam_s):
        h = pl.program_id(1)

        def local(gi, carry):
            r0 = pl.multiple_of(gi * R, R)
            gt = gt_ref[pl.ds(r0, R), :]
            loc = _gdn_local(q_ref[pl.ds(r0, R), :], k_ref[pl.ds(r0, R), :], v_ref[pl.ds(r0, R), :],
                             _pick_lane(gt, SM_B + h), _pick_lane(gt, SM_A + h))
            _gdn_store_local(loc, r0, u_s, w_s, qg_s, kd_s, M_s, gam_s, gi * grp)
            return carry

        lax.fori_loop(0, N // grp, local, 0)

        def step(n, state):
            r0 = pl.multiple_of(n * C, C)
            st_ref[0, 0, n] = state
            v_new = u_s[pl.ds(r0, C), :] - _dotm(w_s[pl.ds(r0, C), :], state, NN)
            o_ref[pl.ds(r0, C), :] = (_dotm(qg_s[pl.ds(r0, C), :], state, NN)
                                      + _dotm(M_s[pl.ds(r0, C), :], v_new, NN))
            return state * gam_s[n] + _dotm(kd_s[pl.ds(r0, C), :], v_new, TN)

        lax.fori_loop(0, N, step, jnp.zeros((hd, hd), f32))
        o = o_ref[...]
        z = z_ref[...]
        ob_ref[...] = (_head_rms(o, gon_ref[...])[0] * (z * jax.nn.sigmoid(z))).astype(ob_ref.dtype)

    blk = lambda off: pl.BlockSpec((S, LANES), lambda b, h: (b, off + h))
    rows = lambda: pltpu.VMEM((S, hd), f32)
    return pl.pallas_call(
        body, name="gdn_fwd", grid=(B, GDN_HEADS),
        in_specs=[blk(0), blk(GDN_HEADS), blk(2 * GDN_HEADS), pl.BlockSpec((S, LANES), lambda b, h: (b, 0)),
                  blk(COL_Z // LANES), pl.BlockSpec((1, hd), lambda b, h: (0, 0))],
        out_specs=[blk(0), blk(0), pl.BlockSpec((1, 1, N, hd, hd), lambda b, h: (b, h, 0, 0, 0))],
        out_shape=[jax.ShapeDtypeStruct((T, GDN_WIDTH), f32), jax.ShapeDtypeStruct((T, GDN_WIDTH), MXU_DTYPE),
                   jax.ShapeDtypeStruct((B, GDN_HEADS, N, hd, hd), f32)],
        scratch_shapes=[rows(), rows(), rows(), rows(), pltpu.VMEM((S, C), f32), pltpu.VMEM((N, 1, LANES), f32)],
        compiler_params=_cparams("parallel", "parallel"),
    )(G, G, G, gates, P, g_on)


def gdn_bwd(G, gates, P, g_on, o_raw, states, d_oab, *, B):
    T = G.shape[0]
    S = T // B
    C = GDN_CHUNK
    N = S // C
    grp = min(GDN_GROUP, N)
    R = grp * C
    hd = GDN_HEAD_DIM

    def body(q_ref, k_ref, v_ref, gt_ref, z_ref, gon_ref, o_ref, st_ref, dob_ref,
             dq_ref, dk_ref, dv_ref, dgt_ref, dz_ref, dgon_ref,
             u_s, w_s, qg_s, kd_s, M_s, gam_s, do_s, du_s, dw_s, dqg_s, dkd_s, dM_s, dgl_s, Tm_s):
        b, h = pl.program_id(0), pl.program_id(1)

        @pl.when((b == 0) & (h == 0))
        def _():
            dgon_ref[...] = jnp.zeros_like(dgon_ref)

        @pl.when(h == 0)
        def _():
            dgt_ref[...] = jnp.zeros_like(dgt_ref)

        def group_inputs(gi, Tm_of=None):
            r0 = pl.multiple_of(gi * R, R)
            gt = gt_ref[pl.ds(r0, R), :]
            Tm = None if Tm_of is None else Tm_of[pl.ds(r0, R), :]
            return r0, _gdn_local(q_ref[pl.ds(r0, R), :], k_ref[pl.ds(r0, R), :], v_ref[pl.ds(r0, R), :],
                                  _pick_lane(gt, SM_B + h), _pick_lane(gt, SM_A + h), Tm)

        def local(gi, carry):
            r0, loc = group_inputs(gi)
            _gdn_store_local(loc, r0, u_s, w_s, qg_s, kd_s, M_s, gam_s, gi * grp)
            Tm_s[pl.ds(r0, R), :] = loc["Tm"].reshape(R, C)
            o, z, gon = o_ref[pl.ds(r0, R), :], z_ref[pl.ds(r0, R), :], gon_ref[...]
            dob = dob_ref[pl.ds(r0, R), :]
            on, ro = _head_rms(o, gon)
            sz = jax.nn.sigmoid(z)
            dz_ref[pl.ds(r0, R), :] = (dob * on * (sz * (1.0 + z * (1.0 - sz)))).astype(dz_ref.dtype)
            do, dgon = _head_rms_bwd(o, ro, gon, dob * (z * sz))
            do_s[pl.ds(r0, R), :] = do
            dgon_ref[...] += dgon
            return carry

        lax.fori_loop(0, N // grp, local, 0)

        def step(t, dS):
            n = N - 1 - t
            r0 = pl.multiple_of(n * C, C)
            rows = pl.ds(r0, C)
            state = st_ref[0, 0, n]
            w_n, M_n, kd_n, do_n = w_s[rows, :], M_s[rows, :], kd_s[rows, :], do_s[rows, :]
            v_new = u_s[rows, :] - _dotm(w_n, state, NN)
            dv_new = _dotm(M_n, do_n, TN) + _dotm(kd_n, dS, NN)
            du_s[rows, :] = dv_new
            dw_s[rows, :] = -_dotm(dv_new, state, NT)
            dqg_s[rows, :] = _dotm(do_n, state, NT)
            dM_s[rows, :] = _dotm(do_n, v_new, NT)
            dkd_s[rows, :] = _dotm(v_new, dS, NT)
            gam = gam_s[n]
            dgl_s[n] = jnp.broadcast_to(jnp.sum(jnp.sum(dS * state, axis=1, keepdims=True), axis=0, keepdims=True), (1, LANES)) * gam
            return dS * gam + _dotm(qg_s[rows, :], do_n, TN) - _dotm(w_n, dv_new, TN)

        lax.fori_loop(0, N, step, jnp.zeros((hd, hd), f32))

        def finish(gi, carry):
            r0, L = group_inputs(gi, Tm_s)
            n = grp
            rows = pl.ds(r0, R)
            g3 = lambda ref: ref[rows, :].reshape(n, C, -1)
            du, dw, dqg, dkd, dM = g3(du_s), g3(dw_s), g3(dqg_s), g3(dkd_s), g3(dM_s)
            L["M"] = g3(M_s)
            TmT = jnp.swapaxes(L["Tm"], 1, 2)
            dTm = _bmm(du, L["vb"], B_NT) + _bmm(dw, L["kg"], B_NT)
            dvb = _bmm(TmT, du, B_NN)
            dkg = _bmm(TmT, dw, B_NN)
            dA = jnp.where(L["row"] > L["col"], -_bmm(_bmm(TmT, dTm, B_NN), TmT, B_NN), 0.0)
            dKK = dA * L["D"]
            dQK = dM * L["D"]
            dkb = _bmm(dKK, L["k"], B_NN) + dkg * L["Gam"]
            dk = (_bmm(dKK, L["kb"], B_TN) + _bmm(dQK, L["q"], B_TN) + dkd * L["kdec"] + L["beta"] * dkb)
            dq = (_bmm(dQK, L["k"], B_NN) + dqg * L["Gam"]) * (GDN_HEAD_DIM ** -0.5)
            E = dA * L["A"] + dM * L["M"]
            r = jnp.sum(dkd * L["kd"], axis=-1, keepdims=True)
            dgc = (jnp.sum(E, axis=2, keepdims=True) - jnp.sum(jnp.swapaxes(E, 1, 2), axis=2, keepdims=True)
                   + jnp.sum(dkg * L["kg"], axis=-1, keepdims=True) + jnp.sum(dqg * L["qg"], axis=-1, keepdims=True) - r)
            dgl = jnp.sum(r, axis=1, keepdims=True) + dgl_s[pl.ds(gi * n, n)][:, :, 0:1]
            rowc = lax.broadcasted_iota(jnp.int32, (n, C, 1), 1)
            dgc = dgc + jnp.where(rowc == C - 1, dgl, 0.0)
            dbeta = jnp.sum(dkb * L["k"], axis=-1, keepdims=True) + jnp.sum(dvb * L["v"], axis=-1, keepdims=True)
            dq_ref[rows, :] = dq.reshape(R, hd)
            dk_ref[rows, :] = dk.reshape(R, hd)
            dv_ref[rows, :] = (L["beta"] * dvb).reshape(R, hd)
            lane = lax.broadcasted_iota(jnp.int32, (R, LANES), 1)
            dgt_ref[rows, :] += (jnp.where(lane == SM_B + h, dbeta.reshape(R, 1), 0.0)
                                 + jnp.where(lane == SM_A + h, dgc.reshape(R, 1), 0.0))
            return carry

        lax.fori_loop(0, N // grp, finish, 0)

    blk = lambda off: pl.BlockSpec((S, LANES), lambda b, h: (b, off + h))
    rows = lambda: pltpu.VMEM((S, hd), f32)
    return pl.pallas_call(
        body, name="gdn_bwd", grid=(B, GDN_HEADS),
        in_specs=[blk(0), blk(GDN_HEADS), blk(2 * GDN_HEADS), pl.BlockSpec((S, LANES), lambda b, h: (b, 0)),
                  blk(COL_Z // LANES), pl.BlockSpec((1, hd), lambda b, h: (0, 0)), blk(0),
                  pl.BlockSpec((1, 1, N, hd, hd), lambda b, h: (b, h, 0, 0, 0)), blk(GDN_HEADS)],
        out_specs=[blk(0), blk(0), blk(0), pl.BlockSpec((S, LANES), lambda b, h: (b, 0)), blk(0),
                   pl.BlockSpec((1, hd), lambda b, h: (0, 0))],
        out_shape=[jax.ShapeDtypeStruct((T, GDN_WIDTH), f32), jax.ShapeDtypeStruct((T, GDN_WIDTH), f32),
                   jax.ShapeDtypeStruct((T, GDN_WIDTH), f32), jax.ShapeDtypeStruct((T, LANES), f32),
                   jax.ShapeDtypeStruct((T, GDN_WIDTH), MXU_DTYPE), jax.ShapeDtypeStruct((1, hd), f32)],
        scratch_shapes=[rows(), rows(), rows(), rows(), pltpu.VMEM((S, C), f32), pltpu.VMEM((N, 1, LANES), f32),
                        rows(), rows(), rows(), rows(), rows(), pltpu.VMEM((S, C), f32), pltpu.VMEM((N, 1, LANES), f32),
                        pltpu.VMEM((S, C), f32)],
        compiler_params=_cparams("arbitrary", "arbitrary"),
    )(G, G, G, gates, P, g_on, o_raw, states, d_oab)


IN_SPLIT = (0, 1536, 1544, 3080, 3088, 3600)


IN_SHARD = IN_DIM // 4
IN_SHARD_PAD = 928


def align_w_in_t(wt):
    s = IN_SPLIT
    pad = jnp.zeros((IN_ALIGNED - IN_DIM, wt.shape[1]), wt.dtype)
    return jnp.concatenate([wt[s[0]:s[1]], wt[s[2]:s[3]], wt[s[4]:s[5]], wt[s[1]:s[2]], wt[s[3]:s[4]], pad], axis=0)


def unalign_w_in_t(wa):
    return jnp.concatenate([wa[0:1536], wa[COL_SMALL:COL_SMALL + 8], wa[1536:3072],
                            wa[COL_SMALL + 8:COL_SMALL + 16], wa[3072:3584]], axis=0)


def _lanes_vec(pieces):
    v = jnp.zeros((1, LANES), f32)
    for off, a in pieces:
        v = lax.dynamic_update_slice(v, a.astype(f32), (0, off))
    return v


def local_step(x, mem, target, w, sp, *, B):
    T = x.shape[0]
    S = T // B
    gq8, gk8 = jnp.tile(sp["fox_qnorm_g"], (1, FOX_HEADS)), jnp.tile(sp["fox_knorm_g"], (1, FOX_HEADS))
    go2 = jnp.tile(sp["fox_onorm_g"], (1, 2))
    bias = _lanes_vec([(SM_F, sp["fox_f_bias"]), (SM_A, sp["gdn_dt_bias"])])
    alog = _lanes_vec([(SM_A, sp["gdn_A_log"])])

    h1 = rms_fwd(x, sp["norm_mix_g"], name="rms_mix")
    P = matmul(h1, w["wa_t"], tb=True, name="mm_in", tn=IN_TILE)
    gates = gates_fwd(P, bias, alog, B=B)
    c = gates[:, SM_F:SM_F + FOX_HEADS].reshape(B, S, FOX_HEADS).transpose(0, 2, 1)
    ccol, crow = c[..., None], c.reshape(B, FOX_HEADS, S // FOX_TQ, 1, FOX_TQ)
    qn, kn, vb = fox_prep_fwd(P, gq8, gk8)
    o_raw, o_a, lse = fox_core_fwd(qn, kn, vb, ccol, crow, go2, B=B)
    G = gdn_prep_fwd(P, w["conv_w"], B=B)
    ob_raw, o_b, states = gdn_fwd(G, gates, P, sp["gdn_onorm_g"], B=B)
    oab = jnp.concatenate([o_a, o_b], axis=1)
    if "late" in w:
        w = {**w, **w["late"](oab)}
    x2 = matmul(oab, w["w_out"], residual=x, name="mm_out")
    hq = rms_fwd(x2, sp["norm_xattn_g"], name="rms_xattn")
    hm = rms_fwd(mem, sp["mem_norm_g"], name="rms_mem")
    cq = matmul(hq, w["w_cq"], name="mm_cq")
    ckv = matmul(hm, w["w_ckv"], name="mm_ckv")
    co = xattn_fwd(cq, ckv, sp["xattn_qnorm_g"], sp["xattn_knorm_g"], B=B)
    x3 = matmul(co, w["w_co"], b_stacked=True, residual=x2, name="mm_co")
    hf = rms_fwd(x3, sp["norm_mlp_g"], name="rms_mlp")
    act = matmul(hf, w["w_mlp1"], b_stacked=True, relu2_out=True, out_dtype=MXU_DTYPE, name="mm_mlp1")
    x4 = matmul(act, w["w_mlp2"], residual=x3, name="mm_mlp2")
    dy, loss = loss_head(x4, target)

    da = matmul(dy, w["w_mlp2"], tb=True, relu2_bwd_aux=act, out_dtype=MXU_DTYPE, name="mm_d_act")
    g_mlp2 = matmul(act, dy, ta=True, out_dtype=WIRE_DTYPE, name="mm_g_mlp2")
    g_mlp1 = matmul(hf, da, ta=True, out_stacked=True, out_dtype=WIRE_DTYPE, name="mm_g_mlp1")
    dhf = matmul(da, w["w_mlp1"], tb=True, b_stacked=True, name="mm_d_hf")
    by_rows = lambda g: g.reshape(N_CHIPS, g.shape[0] // N_CHIPS, g.shape[1])
    early = w.get("grads_ready", lambda grads: jnp.zeros((1, 1), f32))
    tok = early(dict(w_mlp1=g_mlp1, w_mlp2=by_rows(g_mlp2)))[0, 0]
    dx3, g_norm_mlp = rms_bwd(x3, sp["norm_mlp_g"] + tok, dhf, dy, name="rms_mlp_bwd")
    dco = matmul(dx3, w["w_co"], tb=True, b_stacked=True, name="mm_d_co")
    g_co = matmul(co, dx3, ta=True, out_stacked=True, out_dtype=WIRE_DTYPE, name="mm_g_co")
    dcq, dckv, g_xq, g_xk = xattn_bwd(cq, ckv, sp["xattn_qnorm_g"], sp["xattn_knorm_g"], dco, B=B)
    g_cq = matmul(hq, dcq, ta=True, out_dtype=WIRE_DTYPE, name="mm_g_cq")
    dhq = matmul(dcq, w["w_cq"], tb=True, name="mm_d_hq")
    g_ckv = matmul(hm, dckv, ta=True, out_dtype=WIRE_DTYPE, name="mm_g_ckv")
    dhm = matmul(dckv, w["w_ckv"], tb=True, name="mm_d_hm")
    _, g_mem_norm = rms_bwd(mem, sp["mem_norm_g"], dhm, None, name="rms_mem_bwd")
    dx2, g_norm_xattn = rms_bwd(x2, sp["norm_xattn_g"], dhq, dx3, name="rms_xattn_bwd")
    doab = matmul(dx2, w["w_out"], tb=True, name="mm_d_oab")
    g_out = matmul(oab, dx2, ta=True, out_dtype=WIRE_DTYPE, name="mm_g_out")
    tok = early(dict(w_co=g_co, w_cq=by_rows(g_cq), w_ckv=by_rows(g_ckv), w_out=by_rows(g_out)))[0, 0]
    dqn, dkn, dv_f, dccol, dcrow, dgo2 = fox_core_bwd(qn, kn, vb, ccol, crow, go2 + tok, o_raw, lse, doab, B=B)
    dq_f, dk_f, dgq8, dgk8 = fox_prep_bwd(P, gq8, gk8, dqn, dkn)
    dGq, dGk, dGv, dgt, dz, g_gdn_on = gdn_bwd(G, gates, P, sp["gdn_onorm_g"], ob_raw, states, doab, B=B)
    dPg, g_conv = gdn_prep_bwd(P, w["conv_w"], jnp.concatenate([dGq, dGk, dGv], axis=1), B=B)
    dc = (dccol[..., 0] + dcrow.reshape(B, FOX_HEADS, S)).transpose(0, 2, 1).reshape(T, FOX_HEADS)
    dgates = dgt + jnp.pad(dc, ((0, 0), (SM_F, LANES - SM_F - FOX_HEADS)))
    dsmall, par = gates_bwd(P, bias, alog, dgates, B=B)
    dP = jnp.concatenate([dq_f, dk_f, dv_f, dPg, dz, dsmall, jnp.zeros((T, IN_ALIGNED - COL_SMALL - LANES), MXU_DTYPE)], axis=1)
    g_wa = matmul(dP, h1, ta=True, out_dtype=WIRE_DTYPE, name="mm_g_in", tm=IN_TILE)
    dh1 = matmul(dP, w["wa_t"], name="mm_d_h1", tk=IN_TILE)
    dx, g_norm_mix = rms_bwd(x, sp["norm_mix_g"], dh1, dx2, name="rms_mix_bwd")

    fold = lambda g: jnp.sum(g.reshape(-1, FOX_HEAD_DIM), axis=0, keepdims=True)
    g_in = jnp.pad(unalign_w_in_t(g_wa).reshape(N_CHIPS, IN_SHARD, D_MODEL), ((0, 0), (0, IN_SHARD_PAD - IN_SHARD), (0, 0)))
    big = dict(w_in=g_in, w_out=by_rows(g_out), w_cq=by_rows(g_cq), w_ckv=by_rows(g_ckv), w_co=g_co, w_mlp1=g_mlp1,
               w_mlp2=by_rows(g_mlp2))
    small = dict(norm_mix_g=g_norm_mix, fox_qnorm_g=fold(dgq8), fox_knorm_g=fold(dgk8),
                 fox_f_bias=par[0:1, SM_F:SM_F + FOX_HEADS], fox_onorm_g=fold(dgo2), gdn_conv_w=g_conv,
                 gdn_A_log=par[1:2, SM_A:SM_A + GDN_HEADS], gdn_dt_bias=par[0:1, SM_A:SM_A + GDN_HEADS],
                 gdn_onorm_g=g_gdn_on, norm_xattn_g=g_norm_xattn, mem_norm_g=g_mem_norm,
                 xattn_qnorm_g=g_xq, xattn_knorm_g=g_xk, norm_mlp_g=g_norm_mlp)
    return loss, dx, big, small


MESH_IDS = pl.DeviceIdType.MESH
N_CHIPS = 4
HBM_SPEC = pl.BlockSpec(memory_space=pltpu.HBM)
PACK_ROWS = 30720
PACK_HALF = PACK_ROWS // 2
PACK_BLOCK = 3072


def _place():
    return lax.axis_index("x"), lax.axis_index("y"), lax.axis_index("c")


def _other_chips(x, y):
    return [(1 - x, y), (x, 1 - y), (1 - x, 1 - y)]


def _remote(src, dst, send_sem, recv_sem, to):
    return pltpu.make_async_remote_copy(src_ref=src, dst_ref=dst, send_sem=send_sem, recv_sem=recv_sem,
                                        device_id=to, device_id_type=MESH_IDS)


def all_gather_shards(packed):
    half = PACK_HALF

    def body(src_ref, out_ref, send_sems, recv_sems):
        x, y, c = _place()
        me_chip = 2 * x + y
        sibling = (x, y, 1 - c)
        chips = _other_chips(x, y)

        def rows(chip, core):
            return out_ref.at[chip, pl.ds(core * half, half), :]

        sends = [_remote(src_ref.at[pl.ds(c * half, half), :], rows(me_chip, c), send_sems.at[j], recv_sems.at[j], (px, py, c))
                 for j, (px, py) in enumerate(chips)]
        for cp in sends:
            cp.start()
        passed = []
        for j, (px, py) in enumerate(chips):
            theirs = rows(2 * px + py, c)
            _remote(theirs, theirs, send_sems.at[j], recv_sems.at[j], (px, py, c)).wait_recv()
            cp = _remote(theirs, theirs, send_sems.at[3 + j], recv_sems.at[3 + j], sibling)
            cp.start()
            passed.append(cp)
        for j, (px, py) in enumerate(chips):
            theirs = rows(2 * px + py, 1 - c)
            _remote(theirs, theirs, send_sems.at[3 + j], recv_sems.at[3 + j], sibling).wait_recv()
        for cp in sends + passed:
            cp.wait_send()

    return pl.pallas_call(
        body, name="all_gather_shards", in_specs=[HBM_SPEC], out_specs=HBM_SPEC,
        out_shape=jax.ShapeDtypeStruct((N_CHIPS,) + packed.shape, packed.dtype),
        scratch_shapes=[pltpu.SemaphoreType.DMA((6,)), pltpu.SemaphoreType.DMA((6,))],
    )(packed)


def exchange_core_halves(G):
    half = PACK_HALF

    def body(g_ref, land_ref, send_sem, recv_sem):
        x, y, c = _place()
        cp = _remote(g_ref.at[:, pl.ds((1 - c) * half, half), :], land_ref, send_sem, recv_sem, (x, y, 1 - c))
        cp.start()
        cp.wait()

    return pl.pallas_call(
        body, name="exchange_core_halves", in_specs=[HBM_SPEC], out_specs=HBM_SPEC,
        out_shape=jax.ShapeDtypeStruct((N_CHIPS, half, LANES), G.dtype),
        scratch_shapes=[pltpu.SemaphoreType.DMA(()), pltpu.SemaphoreType.DMA(())],
    )(G)


def add_core_halves(G, land, core):
    nb = PACK_HALF // PACK_BLOCK

    def body(c_ref, g_ref, l_ref, o_ref):
        o_ref[...] = (g_ref[...].astype(f32) + l_ref[...].astype(f32)).astype(o_ref.dtype)

    blk = (1, PACK_BLOCK, LANES)
    return pl.pallas_call(
        body, name="add_core_halves",
        grid_spec=pltpu.PrefetchScalarGridSpec(
            num_scalar_prefetch=1, grid=(N_CHIPS, nb),
            in_specs=[pl.BlockSpec(blk, lambda k, i, c_ref: (k, c_ref[0] * nb + i, 0)),
                      pl.BlockSpec(blk, lambda k, i, c_ref: (k, i, 0))],
            out_specs=pl.BlockSpec(blk, lambda k, i, c_ref: (k, i, 0))),
        out_shape=jax.ShapeDtypeStruct(land.shape, land.dtype),
        compiler_params=_cparams("parallel", "parallel"),
    )(core, G, land)


def scatter_to_chips(part):
    def body(p_ref, land_ref, send_sems, recv_sems):
        x, y, c = _place()
        me_chip = 2 * x + y
        chips = _other_chips(x, y)
        sends = [_remote(p_ref.at[2 * px + py], land_ref.at[me_chip], send_sems.at[j], recv_sems.at[j], (px, py, c))
                 for j, (px, py) in enumerate(chips)]
        for cp in sends:
            cp.start()
        for j, (px, py) in enumerate(chips):
            slot = land_ref.at[2 * px + py]
            _remote(slot, slot, send_sems.at[j], recv_sems.at[j], (px, py, c)).wait_recv()
        for cp in sends:
            cp.wait_send()

    return pl.pallas_call(
        body, name="scatter_to_chips", in_specs=[HBM_SPEC], out_specs=HBM_SPEC,
        out_shape=jax.ShapeDtypeStruct(part.shape, part.dtype),
        scratch_shapes=[pltpu.SemaphoreType.DMA((3,)), pltpu.SemaphoreType.DMA((3,))],
    )(part)


def sum_chips(part, land, order):
    nb = PACK_HALF // PACK_BLOCK

    def body(order_ref, p_ref, l1_ref, l2_ref, l3_ref, o_ref):
        o_ref[...] = ((p_ref[0].astype(f32) + l1_ref[0].astype(f32)) + l2_ref[0].astype(f32)) + l3_ref[0].astype(f32)

    slot = lambda j: pl.BlockSpec((1, PACK_BLOCK, LANES), lambda i, order_ref: (order_ref[j], i, 0))
    return pl.pallas_call(
        body, name="sum_chips",
        grid_spec=pltpu.PrefetchScalarGridSpec(
            num_scalar_prefetch=1, grid=(nb,), in_specs=[slot(0), slot(1), slot(2), slot(3)],
            out_specs=pl.BlockSpec((PACK_BLOCK, LANES), lambda i, order_ref: (i, 0))),
        out_shape=jax.ShapeDtypeStruct((PACK_HALF, LANES), f32),
        compiler_params=_cparams("parallel"),
    )(order, part, land, land, land)


def swap_core_halves(red):
    def body(r_ref, out_ref, send_sem, recv_sem):
        x, y, c = _place()
        cp = _remote(r_ref, out_ref, send_sem, recv_sem, (x, y, 1 - c))
        cp.start()
        cp.wait()

    return pl.pallas_call(
        body, name="swap_core_halves", in_specs=[HBM_SPEC], out_specs=HBM_SPEC,
        out_shape=jax.ShapeDtypeStruct(red.shape, red.dtype),
        scratch_shapes=[pltpu.SemaphoreType.DMA(()), pltpu.SemaphoreType.DMA(())],
    )(red)


def _half(ref, core):
    rows = ref.shape[-2] // 2
    return ref.at[(slice(None),) * (len(ref.shape) - 2) + (pl.ds(core * rows, rows), slice(None))]


def gather_weights(shards, conv):
    n = len(shards)

    def body(*refs):
        src, conv_src = refs[:n], refs[n]
        out, conv_out = refs[n + 1:2 * n + 1], refs[2 * n + 1]
        send_sems, recv_sems = refs[2 * n + 2], refs[2 * n + 3]
        x, y, c = _place()
        me_chip = 2 * x + y
        sibling = (x, y, 1 - c)
        chips = _other_chips(x, y)
        sends = []
        for a in range(n):
            for j, (px, py) in enumerate(chips):
                sends.append(_remote(_half(src[a], c), _half(out[a].at[me_chip], c),
                                     send_sems.at[6 * a + j], recv_sems.at[6 * a + j], (px, py, c)))
        for j, (px, py) in enumerate(chips):
            sends.append(_remote(conv_src, conv_out.at[me_chip], send_sems.at[6 * n + j], recv_sems.at[6 * n + j], (px, py, c)))
        for cp in sends:
            cp.start()
        passed = []
        for a in range(n):
            for j, (px, py) in enumerate(chips):
                theirs = _half(out[a].at[2 * px + py], c)
                _remote(theirs, theirs, send_sems.at[6 * a + j], recv_sems.at[6 * a + j], (px, py, c)).wait_recv()
                cp = _remote(theirs, theirs, send_sems.at[6 * a + 3 + j], recv_sems.at[6 * a + 3 + j], sibling)
                cp.start()
                passed.append(cp)
        for j, (px, py) in enumerate(chips):
            theirs = conv_out.at[2 * px + py]
            _remote(theirs, theirs, send_sems.at[6 * n + j], recv_sems.at[6 * n + j], (px, py, c)).wait_recv()
        for a in range(n):
            for j, (px, py) in enumerate(chips):
                theirs = _half(out[a].at[2 * px + py], 1 - c)
                _remote(theirs, theirs, send_sems.at[6 * a + 3 + j], recv_sems.at[6 * a + 3 + j], sibling).wait_recv()
        for cp in sends + passed:
            cp.wait_send()

    return pl.pallas_call(
        body, name="gather_weights", in_specs=[HBM_SPEC] * (n + 1), out_specs=[HBM_SPEC] * (n + 1),
        out_shape=[jax.ShapeDtypeStruct((N_CHIPS,) + s.shape, s.dtype) for s in list(shards) + [conv]],
        scratch_shapes=[pltpu.SemaphoreType.DMA((6 * n + 3,)), pltpu.SemaphoreType.DMA((6 * n + 3,))],
    )(*shards, conv)


SEM_SPEC = pl.BlockSpec(memory_space=pltpu.SEMAPHORE)
SPLIT_EFFECT = pltpu.SideEffectType.DATAFLOW_SIDE_EFFECTING


def _gather_async_copies(src, land, send_sems, recv_sems, x, y, c):
    me_chip = 2 * x + y
    sends, arrivals = [], []
    for a in range(len(src)):
        for j, (px, py) in enumerate(_other_chips(x, y)):
            for core in range(2):
                sends.append(_remote(_half(src[a], c), _half(land[a].at[me_chip], c), send_sems.at[6 * a + 2 * j + core],
                                     recv_sems.at[6 * a + 2 * j + c], (px, py, core)))
                theirs = _half(land[a].at[2 * px + py], core)
                arrivals.append(_remote(theirs, theirs, send_sems.at[6 * a + 2 * j + core],
                                        recv_sems.at[6 * a + 2 * j + core], (px, py, core)))
    return sends, arrivals


def gather_weights_start(shards):
    n = len(shards)

    def body(*refs):
        src, land = refs[:n], refs[n:2 * n]
        send_sems, recv_sems, token = refs[2 * n], refs[2 * n + 1], refs[4 * n + 2]
        x, y, c = _place()
        for cp in _gather_async_copies(src, land, send_sems, recv_sems, x, y, c)[0]:
            cp.start()
        token[...] = jnp.zeros_like(token)

    zones = [pltpu.with_memory_space_constraint(lax.empty((N_CHIPS,) + s.shape, s.dtype), pltpu.HBM) for s in shards]
    srcs = [pltpu.with_memory_space_constraint(s, pltpu.HBM) for s in shards]
    out = pl.pallas_call(
        body, name="gather_weights_start",
        out_shape=[pltpu.SemaphoreType.DMA((6 * n,)), pltpu.SemaphoreType.DMA((6 * n,))]
        + [pltpu.HBM(s.shape, s.dtype) for s in shards] + [pltpu.HBM(z.shape, z.dtype) for z in zones]
        + [jax.ShapeDtypeStruct((8, LANES), f32)],
        in_specs=[HBM_SPEC] * (2 * n),
        out_specs=[SEM_SPEC, SEM_SPEC] + [HBM_SPEC] * (2 * n) + [pl.BlockSpec(memory_space=pltpu.VMEM)],
        input_output_aliases={i: 2 + i for i in range(2 * n)},
        compiler_params=pltpu.CompilerParams(has_side_effects=SPLIT_EFFECT),
    )(*srcs, *zones)
    return out[0], out[1], out[2:2 + n], out[2 + n:2 + 2 * n], out[-1]


def gather_weights_wait(send_sems, recv_sems, shards, zones, after):
    n = len(shards)

    def body(*refs):
        src, land = refs[:n], refs[n:2 * n]
        send_sems, recv_sems = refs[2 * n], refs[2 * n + 1]
        x, y, c = _place()
        sends, arrivals = _gather_async_copies(src, land, send_sems, recv_sems, x, y, c)
        for cp in sends:
            cp.wait_send()
        for cp in arrivals:
            cp.wait_recv()

    out = pl.pallas_call(
        body, name="gather_weights_wait",
        out_shape=[pltpu.HBM(s.shape, s.dtype) for s in shards] + [pltpu.HBM(z.shape, z.dtype) for z in zones],
        in_specs=[HBM_SPEC] * (2 * n) + [SEM_SPEC, SEM_SPEC, pl.BlockSpec(memory_space=pl.ANY)],
        out_specs=[HBM_SPEC] * (2 * n),
        input_output_aliases={i: i for i in range(2 * n)},
        compiler_params=pltpu.CompilerParams(has_side_effects=SPLIT_EFFECT),
    )(*shards, *zones, send_sems, recv_sems, after)
    return out[n:]


def swap_grad_halves(grads, *, name):
    n = len(grads)

    def body(*refs):
        g, land, send_sems, recv_sems = refs[:n], refs[n:2 * n], refs[2 * n], refs[2 * n + 1]
        x, y, c = _place()
        copies = [_remote(_half(g[a], 1 - c), land[a], send_sems.at[a], recv_sems.at[a], (x, y, 1 - c)) for a in range(n)]
        for cp in copies:
            cp.start()
        for cp in copies:
            cp.wait()

    return pl.pallas_call(
        body, name=name, in_specs=[HBM_SPEC] * n, out_specs=[HBM_SPEC] * n,
        out_shape=[jax.ShapeDtypeStruct((N_CHIPS, g.shape[1] // 2, g.shape[2]), g.dtype) for g in grads],
        scratch_shapes=[pltpu.SemaphoreType.DMA((n,)), pltpu.SemaphoreType.DMA((n,))],
    )(*grads)


GRAD_ROWS = 256


def add_grad_halves(g, land, core, *, name):
    _, half, cols = land.shape
    tr = GRAD_ROWS if half % GRAD_ROWS == 0 else half
    nb = half // tr

    def body(c_ref, g_ref, l_ref, o_ref):
        o_ref[...] = (g_ref[...].astype(f32) + l_ref[...].astype(f32)).astype(o_ref.dtype)

    blk = (1, tr, cols)
    return pl.pallas_call(
        body, name=name,
        grid_spec=pltpu.PrefetchScalarGridSpec(
            num_scalar_prefetch=1, grid=(N_CHIPS, nb),
            in_specs=[pl.BlockSpec(blk, lambda k, i, c_ref: (k, c_ref[0] * nb + i, 0)),
                      pl.BlockSpec(blk, lambda k, i, c_ref: (k, i, 0))],
            out_specs=pl.BlockSpec(blk, lambda k, i, c_ref: (k, i, 0))),
        out_shape=jax.ShapeDtypeStruct(land.shape, land.dtype),
        compiler_params=_cparams("parallel", "parallel"),
    )(core, g, land)


def scatter_grads(parts):
    n = len(parts)

    def body(*refs):
        p, land, send_sems, recv_sems = refs[:n], refs[n:2 * n], refs[2 * n], refs[2 * n + 1]
        x, y, c = _place()
        me_chip = 2 * x + y
        chips = _other_chips(x, y)
        sends = [_remote(p[a].at[2 * px + py], land[a].at[me_chip], send_sems.at[3 * a + j], recv_sems.at[3 * a + j], (px, py, c))
                 for a in range(n) for j, (px, py) in enumerate(chips)]
        for cp in sends:
            cp.start()
        for a in range(n):
            for j, (px, py) in enumerate(chips):
                slot = land[a].at[2 * px + py]
                _remote(slot, slot, send_sems.at[3 * a + j], recv_sems.at[3 * a + j], (px, py, c)).wait_recv()
        for cp in sends:
            cp.wait_send()

    return pl.pallas_call(
        body, name="scatter_grads", in_specs=[HBM_SPEC] * n, out_specs=[HBM_SPEC] * n,
        out_shape=[jax.ShapeDtypeStruct(p.shape, p.dtype) for p in parts],
        scratch_shapes=[pltpu.SemaphoreType.DMA((3 * n,)), pltpu.SemaphoreType.DMA((3 * n,))],
    )(*parts)


def _scatter_async_copies(parts, land, send_sems, recv_sems, x, y, c):
    me_chip = 2 * x + y
    sends, arrivals = [], []
    for a in range(len(parts)):
        for j, (px, py) in enumerate(_other_chips(x, y)):
            sems = (send_sems.at[3 * a + j], recv_sems.at[3 * a + j], (px, py, c))
            sends.append(_remote(parts[a].at[2 * px + py], land[a].at[me_chip], *sems))
            slot = land[a].at[2 * px + py]
            arrivals.append(_remote(slot, slot, *sems))
    return sends, arrivals


def scatter_grads_start(parts, *, name):
    n = len(parts)

    def body(*refs):
        p, land = refs[:n], refs[n:2 * n]
        send_sems, recv_sems, token = refs[2 * n], refs[2 * n + 1], refs[4 * n + 2]
        x, y, c = _place()
        for cp in _scatter_async_copies(p, land, send_sems, recv_sems, x, y, c)[0]:
            cp.start()
        token[...] = jnp.zeros_like(token)

    zones = [pltpu.with_memory_space_constraint(lax.empty(p.shape, p.dtype), pltpu.HBM) for p in parts]
    srcs = [pltpu.with_memory_space_constraint(p, pltpu.HBM) for p in parts]
    hbm = [pltpu.HBM(p.shape, p.dtype) for p in parts]
    out = pl.pallas_call(
        body, name=name,
        out_shape=[pltpu.SemaphoreType.DMA((3 * n,)), pltpu.SemaphoreType.DMA((3 * n,))] + hbm + hbm
        + [jax.ShapeDtypeStruct((8, LANES), f32)],
        in_specs=[HBM_SPEC] * (2 * n),
        out_specs=[SEM_SPEC, SEM_SPEC] + [HBM_SPEC] * (2 * n) + [pl.BlockSpec(memory_space=pltpu.VMEM)],
        input_output_aliases={i: 2 + i for i in range(2 * n)},
        compiler_params=pltpu.CompilerParams(has_side_effects=SPLIT_EFFECT),
    )(*srcs, *zones)
    return out[0], out[1], out[2:2 + n], out[2 + n:2 + 2 * n], out[-1]


def scatter_grads_wait(send_sems, recv_sems, parts, zones, after, *, name):
    n = len(parts)

    def body(*refs):
        p, land = refs[:n], refs[n:2 * n]
        x, y, c = _place()
        sends, arrivals = _scatter_async_copies(p, land, refs[2 * n], refs[2 * n + 1], x, y, c)
        for cp in sends:
            cp.wait_send()
        for cp in arrivals:
            cp.wait_recv()

    hbm = [pltpu.HBM(p.shape, p.dtype) for p in parts]
    out = pl.pallas_call(
        body, name=name, out_shape=hbm + hbm,
        in_specs=[HBM_SPEC] * (2 * n) + [SEM_SPEC, SEM_SPEC, pl.BlockSpec(memory_space=pl.ANY)],
        out_specs=[HBM_SPEC] * (2 * n),
        input_output_aliases={i: i for i in range(2 * n)},
        compiler_params=pltpu.CompilerParams(has_side_effects=SPLIT_EFFECT),
    )(*parts, *zones, send_sems, recv_sems, after)
    return out[:n], out[n:]


def sum_grads(part, land, order, *, name):
    _, half, cols = part.shape
    tr = GRAD_ROWS if half % GRAD_ROWS == 0 else half

    def body(order_ref, p_ref, l1_ref, l2_ref, l3_ref, o_ref):
        o_ref[...] = ((p_ref[0].astype(f32) + l1_ref[0].astype(f32)) + l2_ref[0].astype(f32)) + l3_ref[0].astype(f32)

    slot = lambda j: pl.BlockSpec((1, tr, cols), lambda i, order_ref: (order_ref[j], i, 0))
    return pl.pallas_call(
        body, name=name,
        grid_spec=pltpu.PrefetchScalarGridSpec(
            num_scalar_prefetch=1, grid=(half // tr,), in_specs=[slot(0), slot(1), slot(2), slot(3)],
            out_specs=pl.BlockSpec((tr, cols), lambda i, order_ref: (i, 0))),
        out_shape=jax.ShapeDtypeStruct((half, cols), f32),
        compiler_params=_cparams("parallel"),
    )(order, part, land, land, land)


def swap_reduced_halves(mine):
    n = len(mine)

    def body(*refs):
        r, out, send_sems, recv_sems = refs[:n], refs[n:2 * n], refs[2 * n], refs[2 * n + 1]
        x, y, c = _place()
        copies = [_remote(r[a], out[a], send_sems.at[a], recv_sems.at[a], (x, y, 1 - c)) for a in range(n)]
        for cp in copies:
            cp.start()
        for cp in copies:
            cp.wait()

    return pl.pallas_call(
        body, name="swap_reduced_halves", in_specs=[HBM_SPEC] * n, out_specs=[HBM_SPEC] * n,
        out_shape=[jax.ShapeDtypeStruct(r.shape, r.dtype) for r in mine],
        scratch_shapes=[pltpu.SemaphoreType.DMA((n,)), pltpu.SemaphoreType.DMA((n,))],
    )(*mine)


def adamw_halves(w, mine, theirs, m, v, core, *, name):
    R, C = w.shape
    tr = min(GRAD_ROWS, R // 2)
    half_nb = R // 2 // tr

    def body(c_ref, w_ref, a_ref, b_ref, m_ref, v_ref, g_ref, d_ref, nm_ref, nv_ref):
        low = pl.program_id(0) < half_nb
        gv = jnp.where(low == (c_ref[0] == 0), a_ref[...], b_ref[...])
        nm = ADAM_B1 * m_ref[...] + (1.0 - ADAM_B1) * gv
        nv = ADAM_B2 * v_ref[...] + (1.0 - ADAM_B2) * jnp.square(gv)
        m_hat = nm / (1.0 - ADAM_B1 ** ADAM_STEP)
        v_hat = nv / (1.0 - ADAM_B2 ** ADAM_STEP)
        g_ref[...] = gv
        d_ref[...] = -ADAM_LR * (m_hat / (jnp.sqrt(v_hat) + ADAM_EPS) + ADAM_WD * w_ref[...])
        nm_ref[...] = nm
        nv_ref[...] = nv

    full = pl.BlockSpec((tr, C), lambda i, c_ref: (i, 0))
    part = pl.BlockSpec((tr, C), lambda i, c_ref: (i % half_nb, 0))
    out = jax.ShapeDtypeStruct((R, C), f32)
    return pl.pallas_call(
        body, name=name,
        grid_spec=pltpu.PrefetchScalarGridSpec(
            num_scalar_prefetch=1, grid=(2 * half_nb,), in_specs=[full, part, part, full, full], out_specs=[full] * 4),
        out_shape=[out] * 4, compiler_params=_cparams("parallel"),
    )(core, w, mine, theirs, m, v)


N_DEV = 8


def all_reduce_small(v):
    def body(src_ref, out_ref, land_ref, send_sems, recv_sems):
        x, y, c = _place()
        me = 4 * x + 2 * y + c
        copies = []
        for r in range(1, N_DEV):
            peer = ((1 - x) if r & 4 else x, (1 - y) if r & 2 else y, (1 - c) if r & 1 else c)
            copies.append(_remote(src_ref, land_ref.at[r], send_sems.at[r - 1], recv_sems.at[r - 1], peer))
        for cp in copies:
            cp.start()
        land_ref[0] = src_ref[...]
        for cp in copies:
            cp.wait()
        acc = land_ref[me]
        for d in range(1, N_DEV):
            acc = acc + land_ref[jnp.bitwise_xor(me, d)]
        out_ref[...] = acc

    vm = pl.BlockSpec(memory_space=pltpu.VMEM)
    return pl.pallas_call(
        body, name="all_reduce_small", in_specs=[vm], out_specs=vm,
        out_shape=jax.ShapeDtypeStruct(v.shape, v.dtype),
        scratch_shapes=[pltpu.VMEM((N_DEV,) + v.shape, v.dtype),
                        pltpu.SemaphoreType.DMA((N_DEV - 1,)), pltpu.SemaphoreType.DMA((N_DEV - 1,))],
    )(v)


def adamw(w, g, m, v, *, name, tr=None, tc=None):
    R, C = w.shape
    if tc is None:
        tr, tc = min(tr, R), C
        blk = pl.BlockSpec((tr, C), lambda i: (i, 0))
    else:
        tr = R
        blk = pl.BlockSpec((R, tc), lambda i: (0, i))

    def body(w_ref, g_ref, m_ref, v_ref, d_ref, nm_ref, nv_ref):
        gv = g_ref[...]
        nm = ADAM_B1 * m_ref[...] + (1.0 - ADAM_B1) * gv
        nv = ADAM_B2 * v_ref[...] + (1.0 - ADAM_B2) * jnp.square(gv)
        m_hat = nm / (1.0 - ADAM_B1 ** ADAM_STEP)
        v_hat = nv / (1.0 - ADAM_B2 ** ADAM_STEP)
        d_ref[...] = -ADAM_LR * (m_hat / (jnp.sqrt(v_hat) + ADAM_EPS) + ADAM_WD * w_ref[...])
        nm_ref[...] = nm
        nv_ref[...] = nv

    out = jax.ShapeDtypeStruct((R, C), f32)
    return pl.pallas_call(
        body, name=name, grid=((R // tr) * (C // tc),), in_specs=[blk] * 4, out_specs=[blk] * 3, out_shape=[out] * 3,
        compiler_params=_cparams("parallel"),
    )(w, g, m, v)


BIG_SHARDS = (("w_in", (1024, 900), True), ("w_out", (256, 1024), False), ("w_cq", (256, 512), False),
              ("w_ckv", (256, 1024), False), ("w_co", (512, 256), True), ("w_mlp1", (1024, 1024), True),
              ("w_mlp2", (1024, 1024), False))
CONV_SHARD = (CONV_WIDTH, 3 * GDN_WIDTH // N_CHIPS)
SMALL_DIMS = (("norm_mix_g", 1024), ("fox_qnorm_g", 64), ("fox_knorm_g", 64), ("fox_f_bias", 8), ("fox_onorm_g", 64),
              ("gdn_A_log", 4), ("gdn_dt_bias", 4), ("gdn_onorm_g", 128), ("norm_xattn_g", 1024), ("mem_norm_g", 1024),
              ("xattn_qnorm_g", 128), ("xattn_knorm_g", 128), ("norm_mlp_g", 1024))
WEIGHT_ORDER = ("norm_mix_g", "w_in", "fox_qnorm_g", "fox_knorm_g", "fox_f_bias", "fox_onorm_g", "gdn_conv_w", "gdn_A_log",
                "gdn_dt_bias", "gdn_onorm_g", "w_out", "norm_xattn_g", "mem_norm_g", "w_cq", "w_ckv", "xattn_qnorm_g",
                "xattn_knorm_g", "w_co", "norm_mlp_g", "w_mlp1", "w_mlp2")


def _pack_rows(pieces, rows, lead=()):
    flat = []
    for p in pieces:
        p = p.reshape(lead + (-1,))
        pad = (-p.shape[-1]) % LANES
        flat.append(jnp.pad(p, [(0, 0)] * len(lead) + [(0, pad)]) if pad else p)
    cat = jnp.concatenate(flat, axis=-1)
    cat = jnp.pad(cat, [(0, 0)] * len(lead) + [(0, rows * LANES - cat.shape[-1])])
    return cat.reshape(lead + (rows, LANES))


def _unpack_rows(buf, sizes, lead=()):
    flat = buf.reshape(lead + (-1,))
    out, off = [], 0
    for n in sizes:
        out.append(flat[..., off:off + n])
        off += n + (-n) % LANES
    return out


def _conv_to_wire(conv):
    return lax.bitcast_convert_type(conv, bf16)


def _conv_from_wire(wire):
    return lax.bitcast_convert_type(wire, f32)


SMALL_ROWS = 96
SMALL_ADAM_ROWS = 56


def kernel(x, mem, norm_mix_g, w_in, fox_qnorm_g, fox_knorm_g, fox_f_bias, fox_onorm_g, gdn_conv_w, gdn_A_log, gdn_dt_bias, gdn_onorm_g, w_out, norm_xattn_g, mem_norm_g, w_cq, w_ckv, xattn_qnorm_g, xattn_knorm_g, w_co, norm_mlp_g, w_mlp1, w_mlp2, loss_target, m_norm_mix_g, m_w_in, m_fox_qnorm_g, m_fox_knorm_g, m_fox_f_bias, m_fox_onorm_g, m_gdn_conv_w, m_gdn_A_log, m_gdn_dt_bias, m_gdn_onorm_g, m_w_out, m_norm_xattn_g, m_mem_norm_g, m_w_cq, m_w_ckv, m_xattn_qnorm_g, m_xattn_knorm_g, m_w_co, m_norm_mlp_g, m_w_mlp1, m_w_mlp2, v_norm_mix_g, v_w_in, v_fox_qnorm_g, v_fox_knorm_g, v_fox_f_bias, v_fox_onorm_g, v_gdn_conv_w, v_gdn_A_log, v_gdn_dt_bias, v_gdn_onorm_g, v_w_out, v_norm_xattn_g, v_mem_norm_g, v_w_cq, v_w_ckv, v_xattn_qnorm_g, v_xattn_knorm_g, v_w_co, v_norm_mlp_g, v_w_mlp1, v_w_mlp2):
    wts = dict(norm_mix_g=norm_mix_g, w_in=w_in, fox_qnorm_g=fox_qnorm_g, fox_knorm_g=fox_knorm_g, fox_f_bias=fox_f_bias,
               fox_onorm_g=fox_onorm_g, gdn_conv_w=gdn_conv_w, gdn_A_log=gdn_A_log, gdn_dt_bias=gdn_dt_bias,
               gdn_onorm_g=gdn_onorm_g, w_out=w_out, norm_xattn_g=norm_xattn_g, mem_norm_g=mem_norm_g, w_cq=w_cq, w_ckv=w_ckv,
               xattn_qnorm_g=xattn_qnorm_g, xattn_knorm_g=xattn_knorm_g, w_co=w_co, norm_mlp_g=norm_mlp_g, w_mlp1=w_mlp1,
               w_mlp2=w_mlp2)
    mom = dict(norm_mix_g=m_norm_mix_g, w_in=m_w_in, fox_qnorm_g=m_fox_qnorm_g, fox_knorm_g=m_fox_knorm_g,
               fox_f_bias=m_fox_f_bias, fox_onorm_g=m_fox_onorm_g, gdn_conv_w=m_gdn_conv_w, gdn_A_log=m_gdn_A_log,
               gdn_dt_bias=m_gdn_dt_bias, gdn_onorm_g=m_gdn_onorm_g, w_out=m_w_out, norm_xattn_g=m_norm_xattn_g,
               mem_norm_g=m_mem_norm_g, w_cq=m_w_cq, w_ckv=m_w_ckv, xattn_qnorm_g=m_xattn_qnorm_g,
               xattn_knorm_g=m_xattn_knorm_g, w_co=m_w_co, norm_mlp_g=m_norm_mlp_g, w_mlp1=m_w_mlp1, w_mlp2=m_w_mlp2)
    var = dict(norm_mix_g=v_norm_mix_g, w_in=v_w_in, fox_qnorm_g=v_fox_qnorm_g, fox_knorm_g=v_fox_knorm_g,
               fox_f_bias=v_fox_f_bias, fox_onorm_g=v_fox_onorm_g, gdn_conv_w=v_gdn_conv_w, gdn_A_log=v_gdn_A_log,
               gdn_dt_bias=v_gdn_dt_bias, gdn_onorm_g=v_gdn_onorm_g, w_out=v_w_out, norm_xattn_g=v_norm_xattn_g,
               mem_norm_g=v_mem_norm_g, w_cq=v_w_cq, w_ckv=v_w_ckv, xattn_qnorm_g=v_xattn_qnorm_g,
               xattn_knorm_g=v_xattn_knorm_g, w_co=v_w_co, norm_mlp_g=v_norm_mlp_g, w_mlp1=v_w_mlp1, w_mlp2=v_w_mlp2)
    B, S, D = x.shape
    T = B * S
    big_names = [n for n, _, _ in BIG_SHARDS]
    chip = 2 * lax.axis_index("x") + lax.axis_index("y")
    core = lax.axis_index("c").astype(jnp.int32).reshape(1)

    shards = {n: wts[n][0].astype(MXU_DTYPE) for n in big_names[1:]}
    in_t = lambda p: jnp.swapaxes(p[0], 0, 1)
    shards["w_in"] = jnp.pad(in_t(w_in).astype(MXU_DTYPE), ((0, IN_SHARD_PAD - IN_SHARD), (0, 0)))
    w_in_all, conv_all = gather_weights([shards["w_in"]], gdn_conv_w[0])
    late = big_names[1:]
    send_sems, recv_sems, late_src, late_zones, token = gather_weights_start([shards[n] for n in late])
    own = lambda g, s: lax.dynamic_update_slice(g, s[None], (chip,) + (0,) * s.ndim)
    full = {"w_in": own(w_in_all, shards["w_in"])}
    conv_full = own(conv_all, gdn_conv_w[0]).transpose(1, 0, 2).reshape(CONV_WIDTH, 3 * GDN_WIDTH)
    rows = lambda g: g.reshape(N_CHIPS * g.shape[1], g.shape[2])
    w_in_t = full["w_in"][:, :IN_SHARD].reshape(IN_DIM, D_MODEL)

    def late_weights(after):
        zones = gather_weights_wait(send_sems, recv_sems, late_src, late_zones, after)
        got = {n: own(z, shards[n]) for n, z in zip(late, zones)}
        return dict(w_out=rows(got["w_out"]), w_cq=rows(got["w_cq"]), w_ckv=rows(got["w_ckv"]), w_co=got["w_co"],
                    w_mlp1=got["w_mlp1"], w_mlp2=rows(got["w_mlp2"]))

    def chip_partials(names, by_chip):
        landed = swap_grad_halves(by_chip, name="swap_grad_halves_" + names[0])
        return [add_grad_halves(g, l, core, name="add_halves_" + n) for n, g, l in zip(names, by_chip, landed)]

    in_flight = []

    def grads_ready(ready):
        names = list(ready)
        *started, tok = scatter_grads_start(chip_partials(names, [ready[n] for n in names]),
                                            name="scatter_grads_start_%d" % len(in_flight))
        in_flight.append((names, *started))
        return tok

    w = dict(wa_t=align_w_in_t(w_in_t), conv_w=conv_full, late=late_weights, grads_ready=grads_ready)
    sp = {n: wts[n] for n, _ in SMALL_DIMS}
    sp["norm_mix_g"] = sp["norm_mix_g"] + token[0, 0]

    loss_part, grad_x, g_big, g_small = local_step(x.reshape(T, D), mem.reshape(-1, D), loss_target.reshape(T, D), w, sp, B=B)

    small_pieces = [g_small[n] for n, _ in SMALL_DIMS] + [g_small["gdn_conv_w"], loss_part]
    small_sizes = [d for _, d in SMALL_DIMS] + [CONV_WIDTH * 3 * GDN_WIDTH, LANES]
    red_small = _unpack_rows(all_reduce_small(_pack_rows(small_pieces, SMALL_ROWS)), small_sizes)
    grads = {n: p.reshape(1, d) for (n, d), p in zip(SMALL_DIMS, red_small)}
    conv_grad = lax.dynamic_slice(red_small[-2].reshape(CONV_WIDTH, 3 * GDN_WIDTH), (0, chip * CONV_SHARD[1]), CONV_SHARD)
    grads["gdn_conv_w"] = conv_grad.reshape((1,) + CONV_SHARD)
    loss = red_small[-1][0]

    names = ["w_in"]
    chip_part = chip_partials(names, [g_big[n] for n in names])
    parts, zones = dict(zip(names, chip_part)), dict(zip(names, scatter_grads(chip_part)))
    for k, (names, send_sems, recv_sems, thru, land) in enumerate(in_flight):
        thru, land = scatter_grads_wait(send_sems, recv_sems, thru, land, grad_x, name="scatter_grads_wait_%d" % k)
        parts.update(zip(names, thru))
        zones.update(zip(names, land))
    order = jnp.stack([chip, chip ^ 2, chip ^ 1, chip ^ 3]).astype(jnp.int32)
    mine = [sum_grads(parts[n], zones[n], order, name="sum_chips_" + n) for n in big_names]
    theirs = swap_reduced_halves(mine)

    delta, new_m, new_v = {}, {}, {}
    for n, a, b in zip(big_names[1:], mine[1:], theirs[1:]):
        g, d, nm, nv = adamw_halves(wts[n][0], a, b, mom[n][0], var[n][0], core, name="adamw_" + n)
        grads[n], delta[n], new_m[n], new_v[n] = g[None], d[None], nm[None], nv[None]
    south = core[0] == 0
    g_in_t = jnp.concatenate([jnp.where(south, mine[0], theirs[0]), jnp.where(south, theirs[0], mine[0])])[:IN_SHARD]
    back = lambda t: jnp.swapaxes(t, 0, 1)[None]
    d, nm, nv = adamw(in_t(w_in), g_in_t, in_t(m_w_in), in_t(v_w_in), name="adamw_w_in", tc=256)
    grads["w_in"], delta["w_in"], new_m["w_in"], new_v["w_in"] = back(g_in_t), back(d), back(nm), back(nv)
    small_names = [n for n, _ in SMALL_DIMS] + ["gdn_conv_w"]
    small_sz = [d for _, d in SMALL_DIMS] + [CONV_SHARD[0] * CONV_SHARD[1]]
    packed4 = [_pack_rows([src[n] for n in small_names], SMALL_ADAM_ROWS) for src in (wts, grads, mom, var)]
    outs = adamw(*packed4, name="adamw_small", tr=SMALL_ADAM_ROWS)
    for dst, buf in zip((delta, new_m, new_v), outs):
        for n, p in zip(small_names, _unpack_rows(buf, small_sz)):
            dst[n] = p.reshape(wts[n].shape)

    return (loss, grad_x.reshape(B, S, D), *[grads[n] for n in WEIGHT_ORDER], *[delta[n] for n in WEIGHT_ORDER],
            *[new_m[n] for n in WEIGHT_ORDER], *[new_v[n] for n in WEIGHT_ORDER])
```

```python
import functools

import jax
import jax.numpy as jnp
import numpy as np
from jax import lax
from jax.experimental import pallas as pl
from jax.experimental.pallas import tpu as pltpu

f32 = jnp.float32
bf16 = jnp.bfloat16
MXU_DTYPE = jnp.bfloat16
WIRE_DTYPE = jnp.bfloat16
INV_PRECISION = lax.Precision.HIGH

D_MODEL = 1024
FOX_HEADS = 8
FOX_HEAD_DIM = 64
FOX_WIDTH = 512
GDN_HEADS = 4
GDN_HEAD_DIM = 128
GDN_WIDTH = 512
CONV_WIDTH = 4
GDN_CHUNK = 64
XATTN_HEADS = 4
XATTN_HEAD_DIM = 128
XATTN_WIDTH = 512
D_FF = 4096
IN_DIM = 3600
EPS = 1e-6
NEG_INF = -1e30
LANES = 128
ADAM_LR = 0.001
ADAM_B1 = 0.9
ADAM_B2 = 0.999
ADAM_EPS = 1e-08
ADAM_WD = 0.01
ADAM_STEP = 10
VMEM_LIMIT = 48 * 1024 * 1024

COL_FOX = 0
COL_GDN = 1536
COL_Z = 3072
COL_SMALL = 3584
IN_ALIGNED = 3840
IN_TILE = 768
SM_F = 0
SM_B = 8
SM_A = 12


def _cparams(*sem):
    return pltpu.CompilerParams(dimension_semantics=sem, vmem_limit_bytes=VMEM_LIMIT)


def _mx(v):
    return v.astype(MXU_DTYPE)


def _dot(a, b, dims, precision=None):
    return lax.dot_general(a, b, (dims, ((), ())), preferred_element_type=f32, precision=precision)


def _dotm(a, b, dims):
    return _dot(_mx(a), _mx(b), dims)


NN = ((1,), (0,))
NT = ((1,), (1,))
TN = ((0,), (0,))


def matmul(a, b, *, name, ta=False, tb=False, b_stacked=False, out_stacked=False, residual=None, relu2_out=False,
           relu2_bwd_aux=None, out_dtype=f32, tm=1024, tn=1024, tk=1024):
    M, K = (a.shape[1], a.shape[0]) if ta else a.shape
    if b_stacked:
        b_cols = b.shape[2]
        N, tk = (b.shape[1], min(tk, b_cols)) if tb else (N_CHIPS * b_cols, tk)
        tn = tn if tb else min(tn, b_cols)
        assert K == (N_CHIPS * b_cols if tb else b.shape[1]), (name, a.shape, b.shape)
    else:
        N = b.shape[0] if tb else b.shape[1]
    if out_stacked:
        tn = min(tn, N // N_CHIPS)
    tm, tn, tk = min(tm, M), min(tn, N), min(tk, K)
    assert M % tm == 0 and N % tn == 0 and K % tk == 0, (name, M, N, K)
    nk = K // tk
    has_res = residual is not None
    has_aux = relu2_bwd_aux is not None

    def body(*refs):
        a_ref, b_ref = refs[0], refs[1]
        pos = 2
        res_ref = aux_ref = None
        if has_res:
            res_ref = refs[pos]
            pos += 1
        if has_aux:
            aux_ref = refs[pos]
            pos += 1
        o_ref = refs[pos]
        acc_ref = refs[pos + 1]
        k = pl.program_id(2)

        @pl.when(k == 0)
        def _():
            acc_ref[...] = jnp.zeros_like(acc_ref)

        dims = ((0,) if ta else (1,), (1,) if tb else (0,))
        acc_ref[...] += _dot(_mx(a_ref[...]), _mx(b_ref[...]), dims)

        @pl.when(k == nk - 1)
        def _():
            r = acc_ref[...]
            if has_res:
                r = r + res_ref[...]
            if has_aux:
                r = r * (2.0 * jnp.sqrt(aux_ref[...].astype(f32)))
            if relu2_out:
                o_ref[...] = jnp.square(jnp.maximum(r, 0.0)).astype(o_ref.dtype)
            else:
                o_ref[...] = r.astype(o_ref.dtype)

    a_spec = pl.BlockSpec((tk, tm), lambda i, j, k: (k, i)) if ta else pl.BlockSpec((tm, tk), lambda i, j, k: (i, k))
    if b_stacked and tb:
        per = b_cols // tk
        b_spec = pl.BlockSpec((None, tn, tk), lambda i, j, k: (k // per, j, k % per))
    elif b_stacked:
        per = b_cols // tn
        b_spec = pl.BlockSpec((None, tk, tn), lambda i, j, k: (j // per, k, j % per))
    else:
        b_spec = pl.BlockSpec((tn, tk), lambda i, j, k: (j, k)) if tb else pl.BlockSpec((tk, tn), lambda i, j, k: (k, j))
    if out_stacked:
        assert not (has_res or has_aux or relu2_out), name
        per_o = N // N_CHIPS // tn
        o_spec = pl.BlockSpec((None, tm, tn), lambda i, j, k: (j // per_o, i, j % per_o))
        out_full = (N_CHIPS, M, N // N_CHIPS)
    else:
        o_spec = pl.BlockSpec((tm, tn), lambda i, j, k: (i, j))
        out_full = (M, N)
    in_specs, args = [a_spec, b_spec], [a, b]
    if has_res:
        in_specs.append(o_spec)
        args.append(residual)
    if has_aux:
        in_specs.append(o_spec)
        args.append(relu2_bwd_aux)
    out_shape = [jax.ShapeDtypeStruct(out_full, out_dtype)]
    out_specs = [o_spec]
    res = pl.pallas_call(
        body, name=name, grid=(M // tm, N // tn, nk), in_specs=in_specs, out_specs=out_specs, out_shape=out_shape,
        scratch_shapes=[pltpu.VMEM((tm, tn), f32)],
        compiler_params=_cparams("parallel", "parallel", "arbitrary"),
    )(*args)
    return res[0]


def rms_fwd(x, g, *, name, tr=512):
    R, D = x.shape
    tr = min(tr, R)

    def body(x_ref, g_ref, o_ref):
        xv = x_ref[...]
        y = xv * lax.rsqrt(jnp.mean(xv * xv, axis=-1, keepdims=True) + EPS)
        o_ref[...] = (y * g_ref[...]).astype(o_ref.dtype)

    return pl.pallas_call(
        body, name=name, grid=(R // tr,),
        in_specs=[pl.BlockSpec((tr, D), lambda i: (i, 0)), pl.BlockSpec((1, D), lambda i: (0, 0))],
        out_specs=pl.BlockSpec((tr, D), lambda i: (i, 0)),
        out_shape=jax.ShapeDtypeStruct((R, D), MXU_DTYPE),
        compiler_params=_cparams("parallel"),
    )(x, g)


def rms_bwd(x, g, dh, residual, *, name, tr=512):
    R, D = x.shape
    tr = min(tr, R)
    has_res = residual is not None

    def body(*refs):
        if has_res:
            x_ref, g_ref, dh_ref, res_ref, dx_ref, dg_ref = refs
        else:
            x_ref, g_ref, dh_ref, dx_ref, dg_ref = refs
        xv = x_ref[...]
        rstd = lax.rsqrt(jnp.mean(xv * xv, axis=-1, keepdims=True) + EPS)
        xhat = xv * rstd
        dh = dh_ref[...].astype(f32)
        gd = dh * g_ref[...]
        dx = rstd * (gd - xhat * jnp.mean(gd * xhat, axis=-1, keepdims=True))
        if has_res:
            dx = dx + res_ref[...]
        dx_ref[...] = dx

        @pl.when(pl.program_id(0) == 0)
        def _():
            dg_ref[...] = jnp.zeros_like(dg_ref)

        dg_ref[...] += jnp.sum(dh * xhat, axis=0, keepdims=True)

    row = pl.BlockSpec((tr, D), lambda i: (i, 0))
    vec = pl.BlockSpec((1, D), lambda i: (0, 0))
    in_specs = [row, vec, row] + ([row] if has_res else [])
    args = [x, g, dh] + ([residual] if has_res else [])
    return pl.pallas_call(
        body, name=name, grid=(R // tr,), in_specs=in_specs, out_specs=[row, vec],
        out_shape=[jax.ShapeDtypeStruct((R, D), f32), jax.ShapeDtypeStruct((1, D), f32)],
        compiler_params=_cparams("arbitrary"),
    )(*args)


def loss_head(y, target, *, tr=512):
    R, D = y.shape
    tr = min(tr, R)

    def body(y_ref, t_ref, dy_ref, loss_ref):
        e = y_ref[...] - t_ref[...]
        dy_ref[...] = e * (1.0 / D)

        @pl.when(pl.program_id(0) == 0)
        def _():
            loss_ref[...] = jnp.zeros_like(loss_ref)

        part = 0.5 * jnp.sum(jnp.mean(e * e, axis=-1, keepdims=True), axis=0, keepdims=True)
        loss_ref[...] += jnp.broadcast_to(part, loss_ref.shape)

    row = pl.BlockSpec((tr, D), lambda i: (i, 0))
    return pl.pallas_call(
        body, name="loss_head", grid=(R // tr,), in_specs=[row, row],
        out_specs=[row, pl.BlockSpec((1, LANES), lambda i: (0, 0))],
        out_shape=[jax.ShapeDtypeStruct((R, D), f32), jax.ShapeDtypeStruct((1, LANES), f32)],
        compiler_params=_cparams("arbitrary"),
    )(y, target)


def _head_rms(v, g):
    r = lax.rsqrt(jnp.mean(v * v, axis=-1, keepdims=True) + EPS)
    return v * r * g, r


def _head_rms_bwd(v, r, g, dn):
    vhat = v * r
    gd = dn * g
    dv = r * (gd - vhat * jnp.mean(gd * vhat, axis=-1, keepdims=True))
    return dv, jnp.sum(dn * vhat, axis=0, keepdims=True)


def _softmax_rows(s):
    m = jnp.max(s, axis=-1, keepdims=True)
    e = jnp.exp(s - m)
    return e / jnp.sum(e, axis=-1, keepdims=True)


def xattn_fwd(cq, ckv, gq, gk, *, B, tq=512):
    T = cq.shape[0]
    S = T // B
    M = ckv.shape[0] // B
    tq = min(tq, S)
    nq = S // tq
    scale = XATTN_HEAD_DIM ** -0.5

    def body(q_ref, k_ref, v_ref, gq_ref, gk_ref, o_ref):
        qn, _ = _head_rms(q_ref[...], gq_ref[...])
        kn, _ = _head_rms(k_ref[...], gk_ref[...])
        p = _softmax_rows(_dot(_mx(qn), _mx(kn), NT) * scale)
        o_ref[...] = _dot(_mx(p), _mx(v_ref[...]), NN).astype(o_ref.dtype)

    hd = XATTN_HEAD_DIM
    vec = pl.BlockSpec((1, hd), lambda b, h, i: (0, 0))
    return pl.pallas_call(
        body, name="xattn_fwd", grid=(B, XATTN_HEADS, nq),
        in_specs=[pl.BlockSpec((tq, hd), lambda b, h, i: (b * nq + i, h)),
                  pl.BlockSpec((M, hd), lambda b, h, i: (b, h)),
                  pl.BlockSpec((M, hd), lambda b, h, i: (b, XATTN_HEADS + h)), vec, vec],
        out_specs=pl.BlockSpec((tq, hd), lambda b, h, i: (b * nq + i, h)),
        out_shape=jax.ShapeDtypeStruct((T, XATTN_WIDTH), MXU_DTYPE),
        compiler_params=_cparams("parallel", "parallel", "parallel"),
    )(cq, ckv, ckv, gq, gk)


def xattn_bwd(cq, ckv, gq, gk, dco, *, B, tq=512):
    T = cq.shape[0]
    S = T // B
    M = ckv.shape[0] // B
    tq = min(tq, S)
    nq = S // tq
    scale = XATTN_HEAD_DIM ** -0.5
    hd = XATTN_HEAD_DIM

    def body(q_ref, k_ref, v_ref, gq_ref, gk_ref, do_ref, dq_ref, dk_ref, dv_ref, dgq_ref, dgk_ref, dkn_acc, dv_acc):
        b, h, i = pl.program_id(0), pl.program_id(1), pl.program_id(2)

        @pl.when((b == 0) & (h == 0) & (i == 0))
        def _():
            dgq_ref[...] = jnp.zeros_like(dgq_ref)
            dgk_ref[...] = jnp.zeros_like(dgk_ref)

        @pl.when(i == 0)
        def _():
            dkn_acc[...] = jnp.zeros_like(dkn_acc)
            dv_acc[...] = jnp.zeros_like(dv_acc)

        q, k, v = q_ref[...], k_ref[...], v_ref[...]
        gqv, gkv = gq_ref[...], gk_ref[...]
        qn, rq = _head_rms(q, gqv)
        kn, rk = _head_rms(k, gkv)
        p = _softmax_rows(_dot(_mx(qn), _mx(kn), NT) * scale)
        do = do_ref[...]
        dv_acc[...] += _dot(_mx(p), _mx(do), TN)
        dp = _dot(_mx(do), _mx(v), NT)
        ds = p * (dp - jnp.sum(dp * p, axis=-1, keepdims=True)) * scale
        dqn = _dot(_mx(ds), _mx(kn), NN)
        dkn_acc[...] += _dot(_mx(ds), _mx(qn), TN)
        dq, dgq = _head_rms_bwd(q, rq, gqv, dqn)
        dq_ref[...] = dq.astype(dq_ref.dtype)
        dgq_ref[...] += dgq

        @pl.when(i == nq - 1)
        def _():
            dk, dgk = _head_rms_bwd(k, rk, gkv, dkn_acc[...])
            dk_ref[...] = dk.astype(dk_ref.dtype)
            dv_ref[...] = dv_acc[...].astype(dv_ref.dtype)
            dgk_ref[...] += dgk

    vec = pl.BlockSpec((1, hd), lambda b, h, i: (0, 0))
    qspec = pl.BlockSpec((tq, hd), lambda b, h, i: (b * nq + i, h))
    kspec = pl.BlockSpec((M, hd), lambda b, h, i: (b, h))
    vspec = pl.BlockSpec((M, hd), lambda b, h, i: (b, XATTN_HEADS + h))
    dq, dk, dv, dgq, dgk = pl.pallas_call(
        body, name="xattn_bwd", grid=(B, XATTN_HEADS, nq),
        in_specs=[qspec, kspec, vspec, vec, vec, qspec],
        out_specs=[qspec, kspec, kspec, vec, vec],
        out_shape=[jax.ShapeDtypeStruct((T, XATTN_WIDTH), MXU_DTYPE),
                   jax.ShapeDtypeStruct((B * M, XATTN_WIDTH), MXU_DTYPE),
                   jax.ShapeDtypeStruct((B * M, XATTN_WIDTH), MXU_DTYPE),
                   jax.ShapeDtypeStruct((1, hd), f32), jax.ShapeDtypeStruct((1, hd), f32)],
        scratch_shapes=[pltpu.VMEM((M, hd), f32), pltpu.VMEM((M, hd), f32)],
        compiler_params=_cparams("arbitrary", "arbitrary", "arbitrary"),
    )(cq, ckv, ckv, gq, gk, dco)
    return dq, jnp.concatenate([dk, dv], axis=1), dgq, dgk


FOX_PAIRS = FOX_HEADS // 2


def _fox_scores(qn, kn, ccol, crow, q0, tq, S, scale):
    s = _dot(_mx(qn), _mx(kn), NT) * scale + ccol - crow
    qpos = q0 + lax.broadcasted_iota(jnp.int32, (tq, S), 0)
    kpos = lax.broadcasted_iota(jnp.int32, (tq, S), 1)
    return jnp.where(kpos <= qpos, s, NEG_INF)


def fox_fwd(P, ccol, crow, gq, gk, go, *, B, tq=256):
    T = P.shape[0]
    S = T // B
    tq = min(tq, S)
    nq = S // tq
    hd = FOX_HEAD_DIM
    scale = hd ** -0.5

    def body(q_ref, k_ref, v_ref, ccol_ref, crow_ref, gq_ref, gk_ref, go_ref, o_ref, oa_ref):
        q0 = pl.program_id(2) * tq
        for e in range(2):
            sl = slice(e * hd, (e + 1) * hd)
            qn, _ = _head_rms(q_ref[:, sl], gq_ref[:, sl])
            kn, _ = _head_rms(k_ref[:, sl], gk_ref[:, sl])
            p = _softmax_rows(_fox_scores(qn, kn, ccol_ref[0, e], crow_ref[0, e], q0, tq, S, scale))
            o = _dot(_mx(p), _mx(v_ref[:, sl]), NN)
            o_ref[:, sl] = o
            oa_ref[:, sl] = _head_rms(o, go_ref[:, sl])[0].astype(oa_ref.dtype)

    W = 2 * hd
    vec = pl.BlockSpec((1, W), lambda b, h, i: (0, 0))
    ospec = pl.BlockSpec((tq, W), lambda b, h, i: (b * nq + i, h))
    return pl.pallas_call(
        body, name="fox_fwd", grid=(B, FOX_PAIRS, nq),
        in_specs=[pl.BlockSpec((tq, W), lambda b, h, i: (b * nq + i, h)),
                  pl.BlockSpec((S, W), lambda b, h, i: (b, FOX_PAIRS + h)),
                  pl.BlockSpec((S, W), lambda b, h, i: (b, 2 * FOX_PAIRS + h)),
                  pl.BlockSpec((1, 2, tq, 1), lambda b, h, i: (b, h, i, 0)),
                  pl.BlockSpec((1, 2, 1, S), lambda b, h, i: (b, h, 0, 0)), vec, vec, vec],
        out_specs=[ospec, ospec],
        out_shape=[jax.ShapeDtypeStruct((T, FOX_WIDTH), f32), jax.ShapeDtypeStruct((T, FOX_WIDTH), MXU_DTYPE)],
        compiler_params=_cparams("parallel", "parallel", "parallel"),
    )(P, P, P, ccol, crow, gq, gk, go)


def fox_bwd(P, ccol, crow, gq, gk, go, o_raw, d_oab, *, B, tq=256):
    T = P.shape[0]
    S = T // B
    tq = min(tq, S)
    nq = S // tq
    hd = FOX_HEAD_DIM
    scale = hd ** -0.5

    def body(q_ref, k_ref, v_ref, ccol_ref, crow_ref, gq_ref, gk_ref, go_ref, o_ref, doa_ref,
             dq_ref, dk_ref, dv_ref, dccol_ref, dcrow_ref, dgq_ref, dgk_ref, dgo_ref, dkn_acc, dv_acc, dcrow_acc):
        b, h, i = pl.program_id(0), pl.program_id(1), pl.program_id(2)
        q0 = i * tq

        @pl.when((b == 0) & (h == 0) & (i == 0))
        def _():
            dgq_ref[...] = jnp.zeros_like(dgq_ref)
            dgk_ref[...] = jnp.zeros_like(dgk_ref)
            dgo_ref[...] = jnp.zeros_like(dgo_ref)

        @pl.when(i == 0)
        def _():
            dkn_acc[...] = jnp.zeros_like(dkn_acc)
            dv_acc[...] = jnp.zeros_like(dv_acc)
            dcrow_acc[...] = jnp.zeros_like(dcrow_acc)

        for e in range(2):
            sl = slice(e * hd, (e + 1) * hd)
            q, k, v = q_ref[:, sl], k_ref[:, sl], v_ref[:, sl]
            gqv, gkv, gov = gq_ref[:, sl], gk_ref[:, sl], go_ref[:, sl]
            qn, rq = _head_rms(q, gqv)
            kn, rk = _head_rms(k, gkv)
            p = _softmax_rows(_fox_scores(qn, kn, ccol_ref[0, e], crow_ref[0, e], q0, tq, S, scale))
            o = o_ref[:, sl]
            ro = lax.rsqrt(jnp.mean(o * o, axis=-1, keepdims=True) + EPS)
            do, dgo = _head_rms_bwd(o, ro, gov, doa_ref[:, sl])
            dgo_ref[:, sl] += dgo
            dv_acc[e] += _dot(_mx(p), _mx(do), TN)
            dp = _dot(_mx(do), _mx(v), NT)
            ds = p * (dp - jnp.sum(do * o, axis=-1, keepdims=True))
            dccol_ref[0, e] = jnp.sum(ds, axis=1, keepdims=True)
            dcrow_acc[e] -= jnp.sum(ds, axis=0, keepdims=True)
            dqn = _dot(_mx(ds), _mx(kn), NN) * scale
            dkn_acc[e] += _dot(_mx(ds), _mx(qn), TN) * scale
            dq, dgq = _head_rms_bwd(q, rq, gqv, dqn)
            dq_ref[:, sl] = dq.astype(dq_ref.dtype)
            dgq_ref[:, sl] += dgq

        @pl.when(i == nq - 1)
        def _():
            for e in range(2):
                sl = slice(e * hd, (e + 1) * hd)
                k = k_ref[:, sl]
                gkv = gk_ref[:, sl]
                rk = lax.rsqrt(jnp.mean(k * k, axis=-1, keepdims=True) + EPS)
                dk, dgk = _head_rms_bwd(k, rk, gkv, dkn_acc[e])
                dk_ref[:, sl] = dk.astype(dk_ref.dtype)
                dv_ref[:, sl] = dv_acc[e].astype(dv_ref.dtype)
                dgk_ref[:, sl] += dgk
                dcrow_ref[0, e] = dcrow_acc[e]

    W = 2 * hd
    vec = pl.BlockSpec((1, W), lambda b, h, i: (0, 0))
    qspec = pl.BlockSpec((tq, W), lambda b, h, i: (b * nq + i, h))
    kvout = pl.BlockSpec((S, W), lambda b, h, i: (b, h))
    colspec = pl.BlockSpec((1, 2, tq, 1), lambda b, h, i: (b, h, i, 0))
    rowspec = pl.BlockSpec((1, 2, 1, S), lambda b, h, i: (b, h, 0, 0))
    return pl.pallas_call(
        body, name="fox_bwd", grid=(B, FOX_PAIRS, nq),
        in_specs=[qspec,
                  pl.BlockSpec((S, W), lambda b, h, i: (b, FOX_PAIRS + h)),
                  pl.BlockSpec((S, W), lambda b, h, i: (b, 2 * FOX_PAIRS + h)),
                  colspec, rowspec, vec, vec, vec, qspec, qspec],
        out_specs=[qspec, kvout, kvout, colspec, rowspec, vec, vec, vec],
        out_shape=[jax.ShapeDtypeStruct((T, FOX_WIDTH), MXU_DTYPE), jax.ShapeDtypeStruct((T, FOX_WIDTH), MXU_DTYPE),
                   jax.ShapeDtypeStruct((T, FOX_WIDTH), MXU_DTYPE),
                   jax.ShapeDtypeStruct((B, FOX_HEADS, S, 1), f32), jax.ShapeDtypeStruct((B, FOX_HEADS, 1, S), f32),
                   jax.ShapeDtypeStruct((1, W), f32), jax.ShapeDtypeStruct((1, W), f32), jax.ShapeDtypeStruct((1, W), f32)],
        scratch_shapes=[pltpu.VMEM((2, S, hd), f32), pltpu.VMEM((2, S, hd), f32), pltpu.VMEM((2, 1, S), f32)],
        compiler_params=_cparams("arbitrary", "arbitrary", "arbitrary"),
    )(P, P, P, ccol, crow, gq, gk, go, o_raw, d_oab)


FOX_TQ = 512
FOX_TK = 512
GROUP_PRECISION = lax.Precision.HIGH


def _head_mean(v):
    n = v.shape[1]
    r = lax.broadcasted_iota(jnp.int32, (n, n), 0) // FOX_HEAD_DIM
    c = lax.broadcasted_iota(jnp.int32, (n, n), 1) // FOX_HEAD_DIM
    return _dot(v, (r == c).astype(f32), NN, GROUP_PRECISION) * (1.0 / FOX_HEAD_DIM)


def fox_prep_fwd(P, gq, gk, *, tr=512):
    T = P.shape[0]
    tr = min(tr, T)
    scale = FOX_HEAD_DIM ** -0.5

    def body(q_ref, k_ref, v_ref, gq_ref, gk_ref, qn_ref, kn_ref, vb_ref):
        q, k = q_ref[...], k_ref[...]
        qn_ref[...] = (q * lax.rsqrt(_head_mean(q * q) + EPS) * (gq_ref[...] * scale)).astype(qn_ref.dtype)
        kn_ref[...] = (k * lax.rsqrt(_head_mean(k * k) + EPS) * gk_ref[...]).astype(kn_ref.dtype)
        vb_ref[...] = v_ref[...].astype(vb_ref.dtype)

    W = FOX_WIDTH
    col = lambda j: pl.BlockSpec((tr, W), lambda i: (i, j))
    vec = pl.BlockSpec((1, W), lambda i: (0, 0))
    out = jax.ShapeDtypeStruct((T, W), MXU_DTYPE)
    return pl.pallas_call(
        body, name="fox_prep_fwd", grid=(T // tr,), in_specs=[col(0), col(1), col(2), vec, vec],
        out_specs=[col(0)] * 3, out_shape=[out] * 3, compiler_params=_cparams("parallel"),
    )(P, P, P, gq, gk)


def fox_prep_bwd(P, gq, gk, dqn, dkn, *, tr=512):
    T = P.shape[0]
    tr = min(tr, T)
    scale = FOX_HEAD_DIM ** -0.5

    def body(q_ref, k_ref, gq_ref, gk_ref, dqn_ref, dkn_ref, dq_ref, dk_ref, dgq_ref, dgk_ref):
        @pl.when(pl.program_id(0) == 0)
        def _():
            dgq_ref[...] = jnp.zeros_like(dgq_ref)
            dgk_ref[...] = jnp.zeros_like(dgk_ref)

        def one(x, g, dn, dx_ref, dg_ref):
            r = lax.rsqrt(_head_mean(x * x) + EPS)
            xhat = x * r
            gd = dn * g
            dx_ref[...] = (r * (gd - xhat * _head_mean(gd * xhat))).astype(dx_ref.dtype)
            return jnp.sum(dn * xhat, axis=0, keepdims=True)

        dgq_ref[...] += scale * one(q_ref[...], gq_ref[...] * scale, dqn_ref[...], dq_ref, dgq_ref)
        dgk_ref[...] += one(k_ref[...], gk_ref[...], dkn_ref[...], dk_ref, dgk_ref)

    W = FOX_WIDTH
    col = lambda j: pl.BlockSpec((tr, W), lambda i: (i, j))
    vec = pl.BlockSpec((1, W), lambda i: (0, 0))
    return pl.pallas_call(
        body, name="fox_prep_bwd", grid=(T // tr,), in_specs=[col(0), col(1), vec, vec, col(0), col(0)],
        out_specs=[col(0), col(0), vec, vec],
        out_shape=[jax.ShapeDtypeStruct((T, W), MXU_DTYPE), jax.ShapeDtypeStruct((T, W), MXU_DTYPE),
                   jax.ShapeDtypeStruct((1, W), f32), jax.ShapeDtypeStruct((1, W), f32)],
        compiler_params=_cparams("arbitrary"),
    )(P, P, gq, gk, dqn, dkn)


def _fox_tile_scores(q, k_ref, ccol_ref, cq, e, j, sl, mask_off):
    tq, tk = FOX_TQ, FOX_TK
    rows = pl.ds(pl.multiple_of(j * tk, tk), tk)
    k = k_ref[rows, sl]
    s = _dot(k, q, NT) + cq - ccol_ref[0, e, rows, :]
    if mask_off is not None:
        key = lax.broadcasted_iota(jnp.int32, (tk, tq), 0) + mask_off
        query = lax.broadcasted_iota(jnp.int32, (tk, tq), 1)
        s = jnp.where(key <= query, s, NEG_INF)
    return s, k, rows


def _fox_sweep(i, update, carry):
    nd = FOX_TQ // FOX_TK
    carry = lax.fori_loop(0, i * nd, lambda j, cr: update(cr, j, None), carry)
    for d in range(nd):
        carry = update(carry, i * nd + d, d * FOX_TK)
    return carry


def fox_core_fwd(qn, kn, vb, ccol, crow, go, *, B):
    T = qn.shape[0]
    S = T // B
    tq = FOX_TQ
    nq = S // tq
    hd = FOX_HEAD_DIM

    def body(q_ref, k_ref, v_ref, ccol_ref, crow_ref, go_ref, o_ref, oa_ref, lse_ref):
        i = pl.program_id(2)
        for e in range(2):
            sl = slice(e * hd, (e + 1) * hd)
            q = q_ref[:, sl]
            cq = crow_ref[0, e, i]

            def update(carry, j, mask_off):
                m, l, acc = carry
                s, _, rows = _fox_tile_scores(q, k_ref, ccol_ref, cq, e, j, sl, mask_off)
                m2 = jnp.maximum(m, jnp.max(s, axis=0, keepdims=True))
                a = jnp.exp(m - m2)
                p = jnp.exp(s - m2)
                return m2, a * l + jnp.sum(p, axis=0, keepdims=True), a * acc + _dot(v_ref[rows, sl], _mx(p), TN)

            carry = (jnp.full((1, tq), NEG_INF, f32), jnp.zeros((1, tq), f32), jnp.zeros((hd, tq), f32))
            m, l, acc = _fox_sweep(i, update, carry)
            o = (acc / l).T
            o_ref[:, sl] = o
            oa_ref[:, sl] = _head_rms(o, go_ref[:, sl])[0].astype(oa_ref.dtype)
            lse_ref[0, e, 0] = m + jnp.log(l)

    W = 2 * hd
    qspec = pl.BlockSpec((tq, W), lambda b, h, i: (b * nq + i, h))
    kspec = pl.BlockSpec((S, W), lambda b, h, i: (b, h))
    return pl.pallas_call(
        body, name="fox_core_fwd", grid=(B, FOX_PAIRS, nq),
        in_specs=[qspec, kspec, kspec, pl.BlockSpec((1, 2, S, 1), lambda b, h, i: (b, h, 0, 0)),
                  pl.BlockSpec((1, 2, nq, 1, tq), lambda b, h, i: (b, h, 0, 0, 0)),
                  pl.BlockSpec((1, W), lambda b, h, i: (0, 0))],
        out_specs=[qspec, qspec, pl.BlockSpec((1, 2, 1, 1, tq), lambda b, h, i: (b, h, i, 0, 0))],
        out_shape=[jax.ShapeDtypeStruct((T, FOX_WIDTH), f32), jax.ShapeDtypeStruct((T, FOX_WIDTH), MXU_DTYPE),
                   jax.ShapeDtypeStruct((B, FOX_HEADS, nq, 1, tq), f32)],
        compiler_params=_cparams("parallel", "parallel", "parallel"),
    )(qn, kn, vb, ccol, crow, go)


def fox_core_bwd(qn, kn, vb, ccol, crow, go, o_raw, lse, d_oab, *, B):
    T = qn.shape[0]
    S = T // B
    tq = FOX_TQ
    nq = S // tq
    hd = FOX_HEAD_DIM

    def body(q_ref, k_ref, v_ref, ccol_ref, crow_ref, go_ref, o_ref, lse_ref, doa_ref,
             dq_ref, dk_ref, dv_ref, dccol_ref, dcrow_ref, dgo_ref, dk_acc, dv_acc, dck_acc):
        b, h, i = pl.program_id(0), pl.program_id(1), pl.program_id(2)

        @pl.when((b == 0) & (h == 0) & (i == 0))
        def _():
            dgo_ref[...] = jnp.zeros_like(dgo_ref)

        @pl.when(i == 0)
        def _():
            dk_acc[...] = jnp.zeros_like(dk_acc)
            dv_acc[...] = jnp.zeros_like(dv_acc)
            dck_acc[...] = jnp.zeros_like(dck_acc)

        for e in range(2):
            sl = slice(e * hd, (e + 1) * hd)
            q = q_ref[:, sl]
            cq = crow_ref[0, e, i]
            lse_e = lse_ref[0, e, 0]
            o = o_ref[:, sl]
            ro = lax.rsqrt(jnp.mean(o * o, axis=-1, keepdims=True) + EPS)
            do, dgo = _head_rms_bwd(o, ro, go_ref[:, sl], doa_ref[:, sl])
            dgo_ref[:, sl] += dgo
            delta = jnp.sum((do * o).T, axis=0, keepdims=True)
            do_b = _mx(do)

            def update(carry, j, mask_off):
                dq, dcq = carry
                s, k, rows = _fox_tile_scores(q, k_ref, ccol_ref, cq, e, j, sl, mask_off)
                p = jnp.exp(s - lse_e)
                dv_acc[e, rows, :] += _dot(_mx(p), do_b, NN)
                ds = p * (_dot(v_ref[rows, sl], do_b, NT) - delta)
                dck_acc[e, rows, :] -= jnp.sum(ds, axis=1, keepdims=True)
                ds_b = _mx(ds)
                dk_acc[e, rows, :] += _dot(ds_b, q, NN)
                return dq + _dot(ds_b, k, TN), dcq + jnp.sum(ds, axis=0, keepdims=True)

            dq, dcq = _fox_sweep(i, update, (jnp.zeros((tq, hd), f32), jnp.zeros((1, tq), f32)))
            dq_ref[:, sl] = dq
            dcrow_ref[0, e, 0] = dcq

        @pl.when(i == nq - 1)
        def _():
            for e in range(2):
                sl = slice(e * hd, (e + 1) * hd)
                dk_ref[:, sl] = dk_acc[e]
                dv_ref[:, sl] = dv_acc[e].astype(dv_ref.dtype)
            dccol_ref[0] = dck_acc[...]

    W = 2 * hd
    qspec = pl.BlockSpec((tq, W), lambda b, h, i: (b * nq + i, h))
    kspec = pl.BlockSpec((S, W), lambda b, h, i: (b, h))
    colspec = pl.BlockSpec((1, 2, S, 1), lambda b, h, i: (b, h, 0, 0))
    rowspec = pl.BlockSpec((1, 2, nq, 1, tq), lambda b, h, i: (b, h, 0, 0, 0))
    tilespec = pl.BlockSpec((1, 2, 1, 1, tq), lambda b, h, i: (b, h, i, 0, 0))
    vec = pl.BlockSpec((1, W), lambda b, h, i: (0, 0))
    return pl.pallas_call(
        body, name="fox_core_bwd", grid=(B, FOX_PAIRS, nq),
        in_specs=[qspec, kspec, kspec, colspec, rowspec, vec, qspec, tilespec, qspec],
        out_specs=[qspec, kspec, kspec, colspec, tilespec, vec],
        out_shape=[jax.ShapeDtypeStruct((T, FOX_WIDTH), f32), jax.ShapeDtypeStruct((T, FOX_WIDTH), f32),
                   jax.ShapeDtypeStruct((T, FOX_WIDTH), MXU_DTYPE),
                   jax.ShapeDtypeStruct((B, FOX_HEADS, S, 1), f32), jax.ShapeDtypeStruct((B, FOX_HEADS, nq, 1, tq), f32),
                   jax.ShapeDtypeStruct((1, W), f32)],
        scratch_shapes=[pltpu.VMEM((2, S, hd), f32), pltpu.VMEM((2, S, hd), f32), pltpu.VMEM((2, S, 1), f32)],
        compiler_params=_cparams("arbitrary", "arbitrary", "arbitrary"),
    )(qn, kn, vb, ccol, crow, go, o_raw, lse, d_oab)


def _lane_mask(lo, hi, shape):
    lane = lax.broadcasted_iota(jnp.int32, shape, 1)
    return (lane >= lo) & (lane < hi)


def _cumsum_rows(v, period, reverse=False):
    n = v.shape[0]
    pos = lax.broadcasted_iota(jnp.int32, v.shape, 0) % period
    sh = 1
    while sh < period:
        if reverse:
            v = v + jnp.where(pos + sh < period, pltpu.roll(v, n - sh, 0), 0.0)
        else:
            v = v + jnp.where(pos >= sh, pltpu.roll(v, sh, 0), 0.0)
        sh *= 2
    return v


def _gate_values(z, bias, alog):
    zb = z + bias
    ls = jax.nn.log_sigmoid(zb)
    beta = jax.nn.sigmoid(z)
    g = -jnp.exp(alog) * jax.nn.softplus(zb)
    return zb, ls, beta, g


def gates_fwd(P, bias, alog, *, B):
    T = P.shape[0]
    S = T // B

    def body(z_ref, bias_ref, alog_ref, o_ref):
        z = z_ref[...]
        _, ls, beta, g = _gate_values(z, bias_ref[...], alog_ref[...])
        c = _cumsum_rows(ls, S)
        gc = _cumsum_rows(g, GDN_CHUNK)
        o = jnp.where(_lane_mask(SM_F, SM_F + FOX_HEADS, z.shape), c, 0.0)
        o = jnp.where(_lane_mask(SM_B, SM_B + GDN_HEADS, z.shape), beta, o)
        o = jnp.where(_lane_mask(SM_A, SM_A + GDN_HEADS, z.shape), gc, o)
        o_ref[...] = o

    vec = pl.BlockSpec((1, LANES), lambda b: (0, 0))
    return pl.pallas_call(
        body, name="gates_fwd", grid=(B,),
        in_specs=[pl.BlockSpec((S, LANES), lambda b: (b, COL_SMALL // LANES)), vec, vec],
        out_specs=pl.BlockSpec((S, LANES), lambda b: (b, 0)),
        out_shape=jax.ShapeDtypeStruct((T, LANES), f32),
        compiler_params=_cparams("parallel"),
    )(P, bias, alog)


def gates_bwd(P, bias, alog, dgates, *, B):
    T = P.shape[0]
    S = T // B

    def body(z_ref, bias_ref, alog_ref, dg_ref, dz_ref, par_ref):
        z = z_ref[...]
        zb, ls, beta, g = _gate_values(z, bias_ref[...], alog_ref[...])
        d = dg_ref[...]
        dls = _cumsum_rows(d, S, reverse=True)
        dgr = _cumsum_rows(d, GDN_CHUNK, reverse=True)
        sig = jax.nn.sigmoid(zb)
        dz_f = dls * (1.0 - sig)
        dz_b = d * beta * (1.0 - beta)
        dz_a = dgr * (-jnp.exp(alog_ref[...])) * sig
        dz = jnp.where(_lane_mask(SM_F, SM_F + FOX_HEADS, z.shape), dz_f, 0.0)
        dz = jnp.where(_lane_mask(SM_B, SM_B + GDN_HEADS, z.shape), dz_b, dz)
        dz = jnp.where(_lane_mask(SM_A, SM_A + GDN_HEADS, z.shape), dz_a, dz)
        dz_ref[...] = dz.astype(dz_ref.dtype)

        @pl.when(pl.program_id(0) == 0)
        def _():
            par_ref[...] = jnp.zeros_like(par_ref)

        dalog = jnp.where(_lane_mask(SM_A, SM_A + GDN_HEADS, z.shape), dgr * g, 0.0)
        par_ref[0:1, :] += jnp.sum(dz, axis=0, keepdims=True)
        par_ref[1:2, :] += jnp.sum(dalog, axis=0, keepdims=True)

    vec = pl.BlockSpec((1, LANES), lambda b: (0, 0))
    return pl.pallas_call(
        body, name="gates_bwd", grid=(B,),
        in_specs=[pl.BlockSpec((S, LANES), lambda b: (b, COL_SMALL // LANES)), vec, vec,
                  pl.BlockSpec((S, LANES), lambda b: (b, 0))],
        out_specs=[pl.BlockSpec((S, LANES), lambda b: (b, 0)), pl.BlockSpec((8, LANES), lambda b: (0, 0))],
        out_shape=[jax.ShapeDtypeStruct((T, LANES), MXU_DTYPE), jax.ShapeDtypeStruct((8, LANES), f32)],
        compiler_params=_cparams("arbitrary"),
    )(P, bias, alog, dgates)


GDN_BLOCKS = 3 * GDN_HEADS


def _shift_rows(v, d, reverse=False):
    if d == 0:
        return v
    n = v.shape[0]
    row = lax.broadcasted_iota(jnp.int32, v.shape, 0)
    if reverse:
        return jnp.where(row + d < n, pltpu.roll(v, n - d, 0), 0.0)
    return jnp.where(row >= d, pltpu.roll(v, d, 0), 0.0)


def _conv_silu(x, w):
    pre = sum(w[j:j + 1, :] * _shift_rows(x, CONV_WIDTH - 1 - j) for j in range(CONV_WIDTH))
    return pre, pre * jax.nn.sigmoid(pre)


def gdn_prep_fwd(P, conv_w, *, B):
    T = P.shape[0]
    S = T // B

    def body(x_ref, w_ref, o_ref):
        _, y = _conv_silu(x_ref[...], w_ref[...])
        yn = y * lax.rsqrt(jnp.sum(y * y, axis=-1, keepdims=True) + EPS)
        o_ref[...] = jnp.where(pl.program_id(1) < 2 * GDN_HEADS, yn, y)

    return pl.pallas_call(
        body, name="gdn_prep_fwd", grid=(B, GDN_BLOCKS),
        in_specs=[pl.BlockSpec((S, LANES), lambda b, j: (b, COL_GDN // LANES + j)),
                  pl.BlockSpec((CONV_WIDTH, LANES), lambda b, j: (0, j))],
        out_specs=pl.BlockSpec((S, LANES), lambda b, j: (b, j)),
        out_shape=jax.ShapeDtypeStruct((T, 3 * GDN_WIDTH), f32),
        compiler_params=_cparams("parallel", "parallel"),
    )(P, conv_w)


def gdn_prep_bwd(P, conv_w, dG, *, B):
    T = P.shape[0]
    S = T // B

    def body(x_ref, w_ref, dg_ref, dx_ref, dw_ref):
        x, w = x_ref[...], w_ref[...]
        pre, y = _conv_silu(x, w)
        dn = dg_ref[...]
        r = lax.rsqrt(jnp.sum(y * y, axis=-1, keepdims=True) + EPS)
        n = y * r
        dy_norm = r * (dn - n * jnp.sum(dn * n, axis=-1, keepdims=True))
        dy = jnp.where(pl.program_id(0) < 2 * GDN_HEADS, dy_norm, dn)
        sg = jax.nn.sigmoid(pre)
        dpre = dy * (sg * (1.0 + pre * (1.0 - sg)))
        dx = sum(w[j:j + 1, :] * _shift_rows(dpre, CONV_WIDTH - 1 - j, reverse=True) for j in range(CONV_WIDTH))
        dx_ref[...] = dx.astype(dx_ref.dtype)

        @pl.when(pl.program_id(1) == 0)
        def _():
            dw_ref[...] = jnp.zeros_like(dw_ref)

        for j in range(CONV_WIDTH):
            dw_ref[j:j + 1, :] += jnp.sum(dpre * _shift_rows(x, CONV_WIDTH - 1 - j), axis=0, keepdims=True)

    return pl.pallas_call(
        body, name="gdn_prep_bwd", grid=(GDN_BLOCKS, B),
        in_specs=[pl.BlockSpec((S, LANES), lambda j, b: (b, COL_GDN // LANES + j)),
                  pl.BlockSpec((CONV_WIDTH, LANES), lambda j, b: (0, j)),
                  pl.BlockSpec((S, LANES), lambda j, b: (b, j))],
        out_specs=[pl.BlockSpec((S, LANES), lambda j, b: (b, j)),
                   pl.BlockSpec((CONV_WIDTH, LANES), lambda j, b: (0, j))],
        out_shape=[jax.ShapeDtypeStruct((T, 3 * GDN_WIDTH), MXU_DTYPE),
                   jax.ShapeDtypeStruct((CONV_WIDTH, 3 * GDN_WIDTH), f32)],
        compiler_params=_cparams("arbitrary", "arbitrary"),
    )(P, conv_w, dG)


GDN_GROUP = 16
B_NN = (((2,), (1,)), ((0,), (0,)))
B_NT = (((2,), (2,)), ((0,), (0,)))
B_TN = (((1,), (1,)), ((0,), (0,)))


def _bmm(a, b, dims, precision=None):
    if precision is None:
        a, b = _mx(a), _mx(b)
    return lax.dot_general(a, b, dims, preferred_element_type=f32, precision=precision)


def _tri_inverse(A):
    C = A.shape[-1]
    row = lax.broadcasted_iota(jnp.int32, A.shape, 1)
    col = lax.broadcasted_iota(jnp.int32, A.shape, 2)
    eye = (row == col).astype(f32)
    X = jnp.where((row // 4) == (col // 4), -A, 0.0)
    X2 = _bmm(X, X, B_NN, INV_PRECISION)
    Tm = eye + X + X2 + _bmm(X, X2, B_NN, INV_PRECISION)
    b = 4
    while b < C:
        off = ((row // (2 * b)) == (col // (2 * b))) & ((row // b) != (col // b))
        Tm = Tm - _bmm(_bmm(Tm, jnp.where(off, A, 0.0), B_NN, INV_PRECISION), Tm, B_NN, INV_PRECISION)
        b *= 2
    return Tm


def _pick_lane(block, lane_idx):
    lane = lax.broadcasted_iota(jnp.int32, block.shape, 1)
    return jnp.sum(jnp.where(lane == lane_idx, block, 0.0), axis=1, keepdims=True)


def _gdn_local(q, k, v, beta, gc, Tm=None):
    C = GDN_CHUNK
    n = q.shape[0] // C
    q = q.reshape(n, C, -1) * (GDN_HEAD_DIM ** -0.5)
    k = k.reshape(n, C, -1)
    v = v.reshape(n, C, -1)
    beta = beta.reshape(n, C, 1)
    gc = gc.reshape(n, C, 1)
    row = lax.broadcasted_iota(jnp.int32, (n, C, C), 1)
    col = lax.broadcasted_iota(jnp.int32, (n, C, C), 2)
    gcT = jnp.swapaxes(jnp.broadcast_to(gc, (n, C, C)), 1, 2)
    D = jnp.exp(jnp.where(row >= col, gc - gcT, NEG_INF))
    kb = k * beta
    vb = v * beta
    A = jnp.where(row > col, _bmm(kb, k, B_NT) * D, 0.0)
    Gam = jnp.exp(gc)
    kg = kb * Gam
    gl = gc[:, C - 1:C, :]
    kdec = jnp.exp(gl - gc)
    loc = dict(q=q, k=k, v=v, beta=beta, gc=gc, D=D, kb=kb, vb=vb, A=A, Gam=Gam, kg=kg,
               kdec=kdec, kd=k * kdec, qg=q * Gam, gam=jnp.exp(gl), row=row, col=col)
    if Tm is None:
        Tm = _tri_inverse(A)
        loc.update(u=_bmm(Tm, vb, B_NN), w=_bmm(Tm, kg, B_NN), M=_bmm(q, k, B_NT) * D)
    else:
        Tm = Tm.reshape(n, C, C)
    loc["Tm"] = Tm
    return loc


def _gdn_store_local(loc, r0, u_s, w_s, qg_s, kd_s, M_s, gam_s, c0):
    n = loc["u"].shape[0]
    R = n * GDN_CHUNK
    u_s[pl.ds(r0, R), :] = loc["u"].reshape(R, -1)
    w_s[pl.ds(r0, R), :] = loc["w"].reshape(R, -1)
    qg_s[pl.ds(r0, R), :] = loc["qg"].reshape(R, -1)
    kd_s[pl.ds(r0, R), :] = loc["kd"].reshape(R, -1)
    M_s[pl.ds(r0, R), :] = loc["M"].reshape(R, -1)
    gam_s[pl.ds(c0, n)] = jnp.broadcast_to(loc["gam"], (n, 1, LANES))


def _gdn_specs(S):
    blk = lambda off: pl.BlockSpec((S, LANES), lambda b, h: (b, off + h))
    return blk


def gdn_fwd(G, gates, P, g_on, *, B):
    T = G.shape[0]
    S = T // B
    C = GDN_CHUNK
    N = S // C
    grp = min(GDN_GROUP, N)
    R = grp * C
    hd = GDN_HEAD_DIM

    def body(q_ref, k_ref, v_ref, gt_ref, z_ref, gon_ref, o_ref, ob_ref, st_ref, u_s, w_s, qg_s, kd_s, M_s, gam_s):
        h = pl.program_id(1)

        def local(gi, carry):
            r0 = pl.multiple_of(gi * R, R)
            gt = gt_ref[pl.ds(r0, R), :]
            loc = _gdn_local(q_ref[pl.ds(r0, R), :], k_ref[pl.ds(r0, R), :], v_ref[pl.ds(r0, R), :],
                             _pick_lane(gt, SM_B + h), _pick_lane(gt, SM_A + h))
            _gdn_store_local(loc, r0, u_s, w_s, qg_s, kd_s, M_s, gam_s, gi * grp)
            return carry

        lax.fori_loop(0, N // grp, local, 0)

        def step(n, state):
            r0 = pl.multiple_of(n * C, C)
            st_ref[0, 0, n] = state
            v_new = u_s[pl.ds(r0, C), :] - _dotm(w_s[pl.ds(r0, C), :], state, NN)
            o_ref[pl.ds(r0, C), :] = (_dotm(qg_s[pl.ds(r0, C), :], state, NN)
                                      + _dotm(M_s[pl.ds(r0, C), :], v_new, NN))
            return state * gam_s[n] + _dotm(kd_s[pl.ds(r0, C), :], v_new, TN)

        lax.fori_loop(0, N, step, jnp.zeros((hd, hd), f32))
        o = o_ref[...]
        z = z_ref[...]
        ob_ref[...] = (_head_rms(o, gon_ref[...])[0] * (z * jax.nn.sigmoid(z))).astype(ob_ref.dtype)

    blk = lambda off: pl.BlockSpec((S, LANES), lambda b, h: (b, off + h))
    rows = lambda: pltpu.VMEM((S, hd), f32)
    return pl.pallas_call(
        body, name="gdn_fwd", grid=(B, GDN_HEADS),
        in_specs=[blk(0), blk(GDN_HEADS), blk(2 * GDN_HEADS), pl.BlockSpec((S, LANES), lambda b, h: (b, 0)),
                  blk(COL_Z // LANES), pl.BlockSpec((1, hd), lambda b, h: (0, 0))],
        out_specs=[blk(0), blk(0), pl.BlockSpec((1, 1, N, hd, hd), lambda b, h: (b, h, 0, 0, 0))],
        out_shape=[jax.ShapeDtypeStruct((T, GDN_WIDTH), f32), jax.ShapeDtypeStruct((T, GDN_WIDTH), MXU_DTYPE),
                   jax.ShapeDtypeStruct((B, GDN_HEADS, N, hd, hd), f32)],
        scratch_shapes=[rows(), rows(), rows(), rows(), pltpu.VMEM((S, C), f32), pltpu.VMEM((N, 1, LANES), f32)],
        compiler_params=_cparams("parallel", "parallel"),
    )(G, G, G, gates, P, g_on)


def gdn_bwd(G, gates, P, g_on, o_raw, states, d_oab, *, B):
    T = G.shape[0]
    S = T // B
    C = GDN_CHUNK
    N = S // C
    grp = min(GDN_GROUP, N)
    R = grp * C
    hd = GDN_HEAD_DIM

    def body(q_ref, k_ref, v_ref, gt_ref, z_ref, gon_ref, o_ref, st_ref, dob_ref,
             dq_ref, dk_ref, dv_ref, dgt_ref, dz_ref, dgon_ref,
             u_s, w_s, qg_s, kd_s, M_s, gam_s, do_s, du_s, dw_s, dqg_s, dkd_s, dM_s, dgl_s, Tm_s):
        b, h = pl.program_id(0), pl.program_id(1)

        @pl.when((b == 0) & (h == 0))
        def _():
            dgon_ref[...] = jnp.zeros_like(dgon_ref)

        @pl.when(h == 0)
        def _():
            dgt_ref[...] = jnp.zeros_like(dgt_ref)

        def group_inputs(gi, Tm_of=None):
            r0 = pl.multiple_of(gi * R, R)
            gt = gt_ref[pl.ds(r0, R), :]
            Tm = None if Tm_of is None else Tm_of[pl.ds(r0, R), :]
            return r0, _gdn_local(q_ref[pl.ds(r0, R), :], k_ref[pl.ds(r0, R), :], v_ref[pl.ds(r0, R), :],
                                  _pick_lane(gt, SM_B + h), _pick_lane(gt, SM_A + h), Tm)

        def local(gi, carry):
            r0, loc = group_inputs(gi)
            _gdn_store_local(loc, r0, u_s, w_s, qg_s, kd_s, M_s, gam_s, gi * grp)
            Tm_s[pl.ds(r0, R), :] = loc["Tm"].reshape(R, C)
            o, z, gon = o_ref[pl.ds(r0, R), :], z_ref[pl.ds(r0, R), :], gon_ref[...]
            dob = dob_ref[pl.ds(r0, R), :]
            on, ro = _head_rms(o, gon)
            sz = jax.nn.sigmoid(z)
            dz_ref[pl.ds(r0, R), :] = (dob * on * (sz * (1.0 + z * (1.0 - sz)))).astype(dz_ref.dtype)
            do, dgon = _head_rms_bwd(o, ro, gon, dob * (z * sz))
            do_s[pl.ds(r0, R), :] = do
            dgon_ref[...] += dgon
            return carry

        lax.fori_loop(0, N // grp, local, 0)

        def step(t, dS):
            n = N - 1 - t
            r0 = pl.multiple_of(n * C, C)
            rows = pl.ds(r0, C)
            state = st_ref[0, 0, n]
            w_n, M_n, kd_n, do_n = w_s[rows, :], M_s[rows, :], kd_s[rows, :], do_s[rows, :]
            v_new = u_s[rows, :] - _dotm(w_n, state, NN)
            dv_new = _dotm(M_n, do_n, TN) + _dotm(kd_n, dS, NN)
            du_s[rows, :] = dv_new
            dw_s[rows, :] = -_dotm(dv_new, state, NT)
            dqg_s[rows, :] = _dotm(do_n, state, NT)
            dM_s[rows, :] = _dotm(do_n, v_new, NT)
            dkd_s[rows, :] = _dotm(v_new, dS, NT)
            gam = gam_s[n]
            dgl_s[n] = jnp.broadcast_to(jnp.sum(jnp.sum(dS * state, axis=1, keepdims=True), axis=0, keepdims=True), (1, LANES)) * gam
            return dS * gam + _dotm(qg_s[rows, :], do_n, TN) - _dotm(w_n, dv_new, TN)

        lax.fori_loop(0, N, step, jnp.zeros((hd, hd), f32))

        def finish(gi, carry):
            r0, L = group_inputs(gi, Tm_s)
            n = grp
            rows = pl.ds(r0, R)
            g3 = lambda ref: ref[rows, :].reshape(n, C, -1)
            du, dw, dqg, dkd, dM = g3(du_s), g3(dw_s), g3(dqg_s), g3(dkd_s), g3(dM_s)
            L["M"] = g3(M_s)
            TmT = jnp.swapaxes(L["Tm"], 1, 2)
            dTm = _bmm(du, L["vb"], B_NT) + _bmm(dw, L["kg"], B_NT)
            dvb = _bmm(TmT, du, B_NN)
            dkg = _bmm(TmT, dw, B_NN)
            dA = jnp.where(L["row"] > L["col"], -_bmm(_bmm(TmT, dTm, B_NN), TmT, B_NN), 0.0)
            dKK = dA * L["D"]
            dQK = dM * L["D"]
            dkb = _bmm(dKK, L["k"], B_NN) + dkg * L["Gam"]
            dk = (_bmm(dKK, L["kb"], B_TN) + _bmm(dQK, L["q"], B_TN) + dkd * L["kdec"] + L["beta"] * dkb)
            dq = (_bmm(dQK, L["k"], B_NN) + dqg * L["Gam"]) * (GDN_HEAD_DIM ** -0.5)
            E = dA * L["A"] + dM * L["M"]
            r = jnp.sum(dkd * L["kd"], axis=-1, keepdims=True)
            dgc = (jnp.sum(E, axis=2, keepdims=True) - jnp.sum(jnp.swapaxes(E, 1, 2), axis=2, keepdims=True)
                   + jnp.sum(dkg * L["kg"], axis=-1, keepdims=True) + jnp.sum(dqg * L["qg"], axis=-1, keepdims=True) - r)
            dgl = jnp.sum(r, axis=1, keepdims=True) + dgl_s[pl.ds(gi * n, n)][:, :, 0:1]
            rowc = lax.broadcasted_iota(jnp.int32, (n, C, 1), 1)
            dgc = dgc + jnp.where(rowc == C - 1, dgl, 0.0)
            dbeta = jnp.sum(dkb * L["k"], axis=-1, keepdims=True) + jnp.sum(dvb * L["v"], axis=-1, keepdims=True)
            dq_ref[rows, :] = dq.reshape(R, hd)
            dk_ref[rows, :] = dk.reshape(R, hd)
            dv_ref[rows, :] = (L["beta"] * dvb).reshape(R, hd)
            lane = lax.broadcasted_iota(jnp.int32, (R, LANES), 1)
            dgt_ref[rows, :] += (jnp.where(lane == SM_B + h, dbeta.reshape(R, 1), 0.0)
                                 + jnp.where(lane == SM_A + h, dgc.reshape(R, 1), 0.0))
            return carry

        lax.fori_loop(0, N // grp, finish, 0)

    blk = lambda off: pl.BlockSpec((S, LANES), lambda b, h: (b, off + h))
    rows = lambda: pltpu.VMEM((S, hd), f32)
    return pl.pallas_call(
        body, name="gdn_bwd", grid=(B, GDN_HEADS),
        in_specs=[blk(0), blk(GDN_HEADS), blk(2 * GDN_HEADS), pl.BlockSpec((S, LANES), lambda b, h: (b, 0)),
                  blk(COL_Z // LANES), pl.BlockSpec((1, hd), lambda b, h: (0, 0)), blk(0),
                  pl.BlockSpec((1, 1, N, hd, hd), lambda b, h: (b, h, 0, 0, 0)), blk(GDN_HEADS)],
        out_specs=[blk(0), blk(0), blk(0), pl.BlockSpec((S, LANES), lambda b, h: (b, 0)), blk(0),
                   pl.BlockSpec((1, hd), lambda b, h: (0, 0))],
        out_shape=[jax.ShapeDtypeStruct((T, GDN_WIDTH), f32), jax.ShapeDtypeStruct((T, GDN_WIDTH), f32),
                   jax.ShapeDtypeStruct((T, GDN_WIDTH), f32), jax.ShapeDtypeStruct((T, LANES), f32),
                   jax.ShapeDtypeStruct((T, GDN_WIDTH), MXU_DTYPE), jax.ShapeDtypeStruct((1, hd), f32)],
        scratch_shapes=[rows(), rows(), rows(), rows(), pltpu.VMEM((S, C), f32), pltpu.VMEM((N, 1, LANES), f32),
                        rows(), rows(), rows(), rows(), rows(), pltpu.VMEM((S, C), f32), pltpu.VMEM((N, 1, LANES), f32),
                        pltpu.VMEM((S, C), f32)],
        compiler_params=_cparams("arbitrary", "arbitrary"),
    )(G, G, G, gates, P, g_on, o_raw, states, d_oab)


IN_SPLIT = (0, 1536, 1544, 3080, 3088, 3600)


IN_SHARD = IN_DIM // 4
IN_SHARD_PAD = 928


def align_w_in_t(wt):
    s = IN_SPLIT
    pad = jnp.zeros((IN_ALIGNED - IN_DIM, wt.shape[1]), wt.dtype)
    return jnp.concatenate([wt[s[0]:s[1]], wt[s[2]:s[3]], wt[s[4]:s[5]], wt[s[1]:s[2]], wt[s[3]:s[4]], pad], axis=0)


def unalign_w_in_t(wa):
    return jnp.concatenate([wa[0:1536], wa[COL_SMALL:COL_SMALL + 8], wa[1536:3072],
                            wa[COL_SMALL + 8:COL_SMALL + 16], wa[3072:3584]], axis=0)


def _lanes_vec(pieces):
    v = jnp.zeros((1, LANES), f32)
    for off, a in pieces:
        v = lax.dynamic_update_slice(v, a.astype(f32), (0, off))
    return v


def local_step(x, mem, target, w, sp, *, B):
    T = x.shape[0]
    S = T // B
    gq8, gk8 = jnp.tile(sp["fox_qnorm_g"], (1, FOX_HEADS)), jnp.tile(sp["fox_knorm_g"], (1, FOX_HEADS))
    go2 = jnp.tile(sp["fox_onorm_g"], (1, 2))
    bias = _lanes_vec([(SM_F, sp["fox_f_bias"]), (SM_A, sp["gdn_dt_bias"])])
    alog = _lanes_vec([(SM_A, sp["gdn_A_log"])])

    h1 = rms_fwd(x, sp["norm_mix_g"], name="rms_mix")
    P = matmul(h1, w["wa_t"], tb=True, name="mm_in", tn=IN_TILE)
    gates = gates_fwd(P, bias, alog, B=B)
    c = gates[:, SM_F:SM_F + FOX_HEADS].reshape(B, S, FOX_HEADS).transpose(0, 2, 1)
    ccol, crow = c[..., None], c.reshape(B, FOX_HEADS, S // FOX_TQ, 1, FOX_TQ)
    qn, kn, vb = fox_prep_fwd(P, gq8, gk8)
    o_raw, o_a, lse = fox_core_fwd(qn, kn, vb, ccol, crow, go2, B=B)
    G = gdn_prep_fwd(P, w["conv_w"], B=B)
    ob_raw, o_b, states = gdn_fwd(G, gates, P, sp["gdn_onorm_g"], B=B)
    oab = jnp.concatenate([o_a, o_b], axis=1)
    if "late" in w:
        w = {**w, **w["late"](oab)}
    x2 = matmul(oab, w["w_out"], residual=x, name="mm_out")
    hq = rms_fwd(x2, sp["norm_xattn_g"], name="rms_xattn")
    hm = rms_fwd(mem, sp["mem_norm_g"], name="rms_mem")
    cq = matmul(hq, w["w_cq"], name="mm_cq")
    ckv = matmul(hm, w["w_ckv"], name="mm_ckv")
    co = xattn_fwd(cq, ckv, sp["xattn_qnorm_g"], sp["xattn_knorm_g"], B=B)
    x3 = matmul(co, w["w_co"], b_stacked=True, residual=x2, name="mm_co")
    hf = rms_fwd(x3, sp["norm_mlp_g"], name="rms_mlp")
    act = matmul(hf, w["w_mlp1"], b_stacked=True, relu2_out=True, out_dtype=MXU_DTYPE, name="mm_mlp1")
    x4 = matmul(act, w["w_mlp2"], residual=x3, name="mm_mlp2")
    dy, loss = loss_head(x4, target)

    da = matmul(dy, w["w_mlp2"], tb=True, relu2_bwd_aux=act, out_dtype=MXU_DTYPE, name="mm_d_act")
    g_mlp2 = matmul(act, dy, ta=True, out_dtype=WIRE_DTYPE, name="mm_g_mlp2")
    g_mlp1 = matmul(hf, da, ta=True, out_stacked=True, out_dtype=WIRE_DTYPE, name="mm_g_mlp1")
    dhf = matmul(da, w["w_mlp1"], tb=True, b_stacked=True, name="mm_d_hf")
    by_rows = lambda g: g.reshape(N_CHIPS, g.shape[0] // N_CHIPS, g.shape[1])
    early = w.get("grads_ready", lambda grads: jnp.zeros((1, 1), f32))
    tok = early(dict(w_mlp1=g_mlp1, w_mlp2=by_rows(g_mlp2)))[0, 0]
    dx3, g_norm_mlp = rms_bwd(x3, sp["norm_mlp_g"] + tok, dhf, dy, name="rms_mlp_bwd")
    dco = matmul(dx3, w["w_co"], tb=True, b_stacked=True, name="mm_d_co")
    g_co = matmul(co, dx3, ta=True, out_stacked=True, out_dtype=WIRE_DTYPE, name="mm_g_co")
    dcq, dckv, g_xq, g_xk = xattn_bwd(cq, ckv, sp["xattn_qnorm_g"], sp["xattn_knorm_g"], dco, B=B)
    g_cq = matmul(hq, dcq, ta=True, out_dtype=WIRE_DTYPE, name="mm_g_cq")
    dhq = matmul(dcq, w["w_cq"], tb=True, name="mm_d_hq")
    g_ckv = matmul(hm, dckv, ta=True, out_dtype=WIRE_DTYPE, name="mm_g_ckv")
    dhm = matmul(dckv, w["w_ckv"], tb=True, name="mm_d_hm")
    _, g_mem_norm = rms_bwd(mem, sp["mem_norm_g"], dhm, None, name="rms_mem_bwd")
    dx2, g_norm_xattn = rms_bwd(x2, sp["norm_xattn_g"], dhq, dx3, name="rms_xattn_bwd")
    doab = matmul(dx2, w["w_out"], tb=True, name="mm_d_oab")
    g_out = matmul(oab, dx2, ta=True, out_dtype=WIRE_DTYPE, name="mm_g_out")
    tok = early(dict(w_co=g_co, w_cq=by_rows(g_cq), w_ckv=by_rows(g_ckv), w_out=by_rows(g_out)))[0, 0]
    dqn, dkn, dv_f, dccol, dcrow, dgo2 = fox_core_bwd(qn, kn, vb, ccol, crow, go2 + tok, o_raw, lse, doab, B=B)
    dq_f, dk_f, dgq8, dgk8 = fox_prep_bwd(P, gq8, gk8, dqn, dkn)
    dGq, dGk, dGv, dgt, dz, g_gdn_on = gdn_bwd(G, gates, P, sp["gdn_onorm_g"], ob_raw, states, doab, B=B)
    dPg, g_conv = gdn_prep_bwd(P, w["conv_w"], jnp.concatenate([dGq, dGk, dGv], axis=1), B=B)
    dc = (dccol[..., 0] + dcrow.reshape(B, FOX_HEADS, S)).transpose(0, 2, 1).reshape(T, FOX_HEADS)
    dgates = dgt + jnp.pad(dc, ((0, 0), (SM_F, LANES - SM_F - FOX_HEADS)))
    dsmall, par = gates_bwd(P, bias, alog, dgates, B=B)
    dP = jnp.concatenate([dq_f, dk_f, dv_f, dPg, dz, dsmall, jnp.zeros((T, IN_ALIGNED - COL_SMALL - LANES), MXU_DTYPE)], axis=1)
    g_wa = matmul(dP, h1, ta=True, out_dtype=WIRE_DTYPE, name="mm_g_in", tm=IN_TILE)
    dh1 = matmul(dP, w["wa_t"], name="mm_d_h1", tk=IN_TILE)
    dx, g_norm_mix = rms_bwd(x, sp["norm_mix_g"], dh1, dx2, name="rms_mix_bwd")

    fold = lambda g: jnp.sum(g.reshape(-1, FOX_HEAD_DIM), axis=0, keepdims=True)
    g_in = jnp.pad(unalign_w_in_t(g_wa).reshape(N_CHIPS, IN_SHARD, D_MODEL), ((0, 0), (0, IN_SHARD_PAD - IN_SHARD), (0, 0)))
    big = dict(w_in=g_in, w_out=by_rows(g_out), w_cq=by_rows(g_cq), w_ckv=by_rows(g_ckv), w_co=g_co, w_mlp1=g_mlp1,
               w_mlp2=by_rows(g_mlp2))
    small = dict(norm_mix_g=g_norm_mix, fox_qnorm_g=fold(dgq8), fox_knorm_g=fold(dgk8),
                 fox_f_bias=par[0:1, SM_F:SM_F + FOX_HEADS], fox_onorm_g=fold(dgo2), gdn_conv_w=g_conv,
                 gdn_A_log=par[1:2, SM_A:SM_A + GDN_HEADS], gdn_dt_bias=par[0:1, SM_A:SM_A + GDN_HEADS],
                 gdn_onorm_g=g_gdn_on, norm_xattn_g=g_norm_xattn, mem_norm_g=g_mem_norm,
                 xattn_qnorm_g=g_xq, xattn_knorm_g=g_xk, norm_mlp_g=g_norm_mlp)
    return loss, dx, big, small


MESH_IDS = pl.DeviceIdType.MESH
N_CHIPS = 4
HBM_SPEC = pl.BlockSpec(memory_space=pltpu.HBM)
PACK_ROWS = 30720
PACK_HALF = PACK_ROWS // 2
PACK_BLOCK = 3072


def _place():
    return lax.axis_index("x"), lax.axis_index("y"), lax.axis_index("c")


def _other_chips(x, y):
    return [(1 - x, y), (x, 1 - y), (1 - x, 1 - y)]


def _remote(src, dst, send_sem, recv_sem, to):
    return pltpu.make_async_remote_copy(src_ref=src, dst_ref=dst, send_sem=send_sem, recv_sem=recv_sem,
                                        device_id=to, device_id_type=MESH_IDS)


def all_gather_shards(packed):
    half = PACK_HALF

    def body(src_ref, out_ref, send_sems, recv_sems):
        x, y, c = _place()
        me_chip = 2 * x + y
        sibling = (x, y, 1 - c)
        chips = _other_chips(x, y)

        def rows(chip, core):
            return out_ref.at[chip, pl.ds(core * half, half), :]

        sends = [_remote(src_ref.at[pl.ds(c * half, half), :], rows(me_chip, c), send_sems.at[j], recv_sems.at[j], (px, py, c))
                 for j, (px, py) in enumerate(chips)]
        for cp in sends:
            cp.start()
        passed = []
        for j, (px, py) in enumerate(chips):
            theirs = rows(2 * px + py, c)
            _remote(theirs, theirs, send_sems.at[j], recv_sems.at[j], (px, py, c)).wait_recv()
            cp = _remote(theirs, theirs, send_sems.at[3 + j], recv_sems.at[3 + j], sibling)
            cp.start()
            passed.append(cp)
        for j, (px, py) in enumerate(chips):
            theirs = rows(2 * px + py, 1 - c)
            _remote(theirs, theirs, send_sems.at[3 + j], recv_sems.at[3 + j], sibling).wait_recv()
        for cp in sends + passed:
            cp.wait_send()

    return pl.pallas_call(
        body, name="all_gather_shards", in_specs=[HBM_SPEC], out_specs=HBM_SPEC,
        out_shape=jax.ShapeDtypeStruct((N_CHIPS,) + packed.shape, packed.dtype),
        scratch_shapes=[pltpu.SemaphoreType.DMA((6,)), pltpu.SemaphoreType.DMA((6,))],
    )(packed)


def exchange_core_halves(G):
    half = PACK_HALF

    def body(g_ref, land_ref, send_sem, recv_sem):
        x, y, c = _place()
        cp = _remote(g_ref.at[:, pl.ds((1 - c) * half, half), :], land_ref, send_sem, recv_sem, (x, y, 1 - c))
        cp.start()
        cp.wait()

    return pl.pallas_call(
        body, name="exchange_core_halves", in_specs=[HBM_SPEC], out_specs=HBM_SPEC,
        out_shape=jax.ShapeDtypeStruct((N_CHIPS, half, LANES), G.dtype),
        scratch_shapes=[pltpu.SemaphoreType.DMA(()), pltpu.SemaphoreType.DMA(())],
    )(G)


def add_core_halves(G, land, core):
    nb = PACK_HALF // PACK_BLOCK

    def body(c_ref, g_ref, l_ref, o_ref):
        o_ref[...] = (g_ref[...].astype(f32) + l_ref[...].astype(f32)).astype(o_ref.dtype)

    blk = (1, PACK_BLOCK, LANES)
    return pl.pallas_call(
        body, name="add_core_halves",
        grid_spec=pltpu.PrefetchScalarGridSpec(
            num_scalar_prefetch=1, grid=(N_CHIPS, nb),
            in_specs=[pl.BlockSpec(blk, lambda k, i, c_ref: (k, c_ref[0] * nb + i, 0)),
                      pl.BlockSpec(blk, lambda k, i, c_ref: (k, i, 0))],
            out_specs=pl.BlockSpec(blk, lambda k, i, c_ref: (k, i, 0))),
        out_shape=jax.ShapeDtypeStruct(land.shape, land.dtype),
        compiler_params=_cparams("parallel", "parallel"),
    )(core, G, land)


def scatter_to_chips(part):
    def body(p_ref, land_ref, send_sems, recv_sems):
        x, y, c = _place()
        me_chip = 2 * x + y
        chips = _other_chips(x, y)
        sends = [_remote(p_ref.at[2 * px + py], land_ref.at[me_chip], send_sems.at[j], recv_sems.at[j], (px, py, c))
                 for j, (px, py) in enumerate(chips)]
        for cp in sends:
            cp.start()
        for j, (px, py) in enumerate(chips):
            slot = land_ref.at[2 * px + py]
            _remote(slot, slot, send_sems.at[j], recv_sems.at[j], (px, py, c)).wait_recv()
        for cp in sends:
            cp.wait_send()

    return pl.pallas_call(
        body, name="scatter_to_chips", in_specs=[HBM_SPEC], out_specs=HBM_SPEC,
        out_shape=jax.ShapeDtypeStruct(part.shape, part.dtype),
        scratch_shapes=[pltpu.SemaphoreType.DMA((3,)), pltpu.SemaphoreType.DMA((3,))],
    )(part)


def sum_chips(part, land, order):
    nb = PACK_HALF // PACK_BLOCK

    def body(order_ref, p_ref, l1_ref, l2_ref, l3_ref, o_ref):
        o_ref[...] = ((p_ref[0].astype(f32) + l1_ref[0].astype(f32)) + l2_ref[0].astype(f32)) + l3_ref[0].astype(f32)

    slot = lambda j: pl.BlockSpec((1, PACK_BLOCK, LANES), lambda i, order_ref: (order_ref[j], i, 0))
    return pl.pallas_call(
        body, name="sum_chips",
        grid_spec=pltpu.PrefetchScalarGridSpec(
            num_scalar_prefetch=1, grid=(nb,), in_specs=[slot(0), slot(1), slot(2), slot(3)],
            out_specs=pl.BlockSpec((PACK_BLOCK, LANES), lambda i, order_ref: (i, 0))),
        out_shape=jax.ShapeDtypeStruct((PACK_HALF, LANES), f32),
        compiler_params=_cparams("parallel"),
    )(order, part, land, land, land)


def swap_core_halves(red):
    def body(r_ref, out_ref, send_sem, recv_sem):
        x, y, c = _place()
        cp = _remote(r_ref, out_ref, send_sem, recv_sem, (x, y, 1 - c))
        cp.start()
        cp.wait()

    return pl.pallas_call(
        body, name="swap_core_halves", in_specs=[HBM_SPEC], out_specs=HBM_SPEC,
        out_shape=jax.ShapeDtypeStruct(red.shape, red.dtype),
        scratch_shapes=[pltpu.SemaphoreType.DMA(()), pltpu.SemaphoreType.DMA(())],
    )(red)


def _half(ref, core):
    rows = ref.shape[-2] // 2
    return ref.at[(slice(None),) * (len(ref.shape) - 2) + (pl.ds(core * rows, rows), slice(None))]


def gather_weights(shards, conv):
    n = len(shards)

    def body(*refs):
        src, conv_src = refs[:n], refs[n]
        out, conv_out = refs[n + 1:2 * n + 1], refs[2 * n + 1]
        send_sems, recv_sems = refs[2 * n + 2], refs[2 * n + 3]
        x, y, c = _place()
        me_chip = 2 * x + y
        sibling = (x, y, 1 - c)
        chips = _other_chips(x, y)
        sends = []
        for a in range(n):
            for j, (px, py) in enumerate(chips):
                sends.append(_remote(_half(src[a], c), _half(out[a].at[me_chip], c),
                                     send_sems.at[6 * a + j], recv_sems.at[6 * a + j], (px, py, c)))
        for j, (px, py) in enumerate(chips):
            sends.append(_remote(conv_src, conv_out.at[me_chip], send_sems.at[6 * n + j], recv_sems.at[6 * n + j], (px, py, c)))
        for cp in sends:
            cp.start()
        passed = []
        for a in range(n):
            for j, (px, py) in enumerate(chips):
                theirs = _half(out[a].at[2 * px + py], c)
                _remote(theirs, theirs, send_sems.at[6 * a + j], recv_sems.at[6 * a + j], (px, py, c)).wait_recv()
                cp = _remote(theirs, theirs, send_sems.at[6 * a + 3 + j], recv_sems.at[6 * a + 3 + j], sibling)
                cp.start()
                passed.append(cp)
        for j, (px, py) in enumerate(chips):
            theirs = conv_out.at[2 * px + py]
            _remote(theirs, theirs, send_sems.at[6 * n + j], recv_sems.at[6 * n + j], (px, py, c)).wait_recv()
        for a in range(n):
            for j, (px, py) in enumerate(chips):
                theirs = _half(out[a].at[2 * px + py], 1 - c)
                _remote(theirs, theirs, send_sems.at[6 * a + 3 + j], recv_sems.at[6 * a + 3 + j], sibling).wait_recv()
        for cp in sends + passed:
            cp.wait_send()

    return pl.pallas_call(
        body, name="gather_weights", in_specs=[HBM_SPEC] * (n + 1), out_specs=[HBM_SPEC] * (n + 1),
        out_shape=[jax.ShapeDtypeStruct((N_CHIPS,) + s.shape, s.dtype) for s in list(shards) + [conv]],
        scratch_shapes=[pltpu.SemaphoreType.DMA((6 * n + 3,)), pltpu.SemaphoreType.DMA((6 * n + 3,))],
    )(*shards, conv)


SEM_SPEC = pl.BlockSpec(memory_space=pltpu.SEMAPHORE)
SPLIT_EFFECT = pltpu.SideEffectType.DATAFLOW_SIDE_EFFECTING


def _gather_async_copies(src, land, send_sems, recv_sems, x, y, c):
    me_chip = 2 * x + y
    sends, arrivals = [], []
    for a in range(len(src)):
        for j, (px, py) in enumerate(_other_chips(x, y)):
            for core in range(2):
                sends.append(_remote(_half(src[a], c), _half(land[a].at[me_chip], c), send_sems.at[6 * a + 2 * j + core],
                                     recv_sems.at[6 * a + 2 * j + c], (px, py, core)))
                theirs = _half(land[a].at[2 * px + py], core)
                arrivals.append(_remote(theirs, theirs, send_sems.at[6 * a + 2 * j + core],
                                        recv_sems.at[6 * a + 2 * j + core], (px, py, core)))
    return sends, arrivals


def gather_weights_start(shards, after):
    n = len(shards)

    def body(*refs):
        src, land = refs[:n], refs[n:2 * n]
        send_sems, recv_sems, token = refs[2 * n + 1], refs[2 * n + 2], refs[4 * n + 3]
        x, y, c = _place()
        for cp in _gather_async_copies(src, land, send_sems, recv_sems, x, y, c)[0]:
            cp.start()
        token[...] = jnp.zeros_like(token)

    zones = [pltpu.with_memory_space_constraint(lax.empty((N_CHIPS,) + s.shape, s.dtype), pltpu.HBM) for s in shards]
    srcs = [pltpu.with_memory_space_constraint(s, pltpu.HBM) for s in shards]
    out = pl.pallas_call(
        body, name="gather_weights_start",
        out_shape=[pltpu.SemaphoreType.DMA((6 * n,)), pltpu.SemaphoreType.DMA((6 * n,))]
        + [pltpu.HBM(s.shape, s.dtype) for s in shards] + [pltpu.HBM(z.shape, z.dtype) for z in zones]
        + [jax.ShapeDtypeStruct((8, LANES), f32)],
        in_specs=[HBM_SPEC] * (2 * n) + [pl.BlockSpec(memory_space=pl.ANY)],
        out_specs=[SEM_SPEC, SEM_SPEC] + [HBM_SPEC] * (2 * n) + [pl.BlockSpec(memory_space=pltpu.VMEM)],
        input_output_aliases={i: 2 + i for i in range(2 * n)},
        compiler_params=pltpu.CompilerParams(has_side_effects=SPLIT_EFFECT),
    )(*srcs, *zones, after)
    return out[0], out[1], out[2:2 + n], out[2 + n:2 + 2 * n], out[-1]


def gather_weights_wait(send_sems, recv_sems, shards, zones, after):
    n = len(shards)

    def body(*refs):
        src, land = refs[:n], refs[n:2 * n]
        send_sems, recv_sems = refs[2 * n], refs[2 * n + 1]
        x, y, c = _place()
        sends, arrivals = _gather_async_copies(src, land, send_sems, recv_sems, x, y, c)
        for cp in sends:
            cp.wait_send()
        for cp in arrivals:
            cp.wait_recv()

    out = pl.pallas_call(
        body, name="gather_weights_wait",
        out_shape=[pltpu.HBM(s.shape, s.dtype) for s in shards] + [pltpu.HBM(z.shape, z.dtype) for z in zones],
        in_specs=[HBM_SPEC] * (2 * n) + [SEM_SPEC, SEM_SPEC, pl.BlockSpec(memory_space=pl.ANY)],
        out_specs=[HBM_SPEC] * (2 * n),
        input_output_aliases={i: i for i in range(2 * n)},
        compiler_params=pltpu.CompilerParams(has_side_effects=SPLIT_EFFECT),
    )(*shards, *zones, send_sems, recv_sems, after)
    return out[n:]


def swap_grad_halves(grads, *, name):
    n = len(grads)

    def body(*refs):
        g, land, send_sems, recv_sems = refs[:n], refs[n:2 * n], refs[2 * n], refs[2 * n + 1]
        x, y, c = _place()
        copies = [_remote(_half(g[a], 1 - c), land[a], send_sems.at[a], recv_sems.at[a], (x, y, 1 - c)) for a in range(n)]
        for cp in copies:
            cp.start()
        for cp in copies:
            cp.wait()

    return pl.pallas_call(
        body, name=name, in_specs=[HBM_SPEC] * n, out_specs=[HBM_SPEC] * n,
        out_shape=[jax.ShapeDtypeStruct((N_CHIPS, g.shape[1] // 2, g.shape[2]), g.dtype) for g in grads],
        scratch_shapes=[pltpu.SemaphoreType.DMA((n,)), pltpu.SemaphoreType.DMA((n,))],
    )(*grads)


GRAD_ROWS = 256


def add_grad_halves(g, land, core, *, name):
    _, half, cols = land.shape
    tr = GRAD_ROWS if half % GRAD_ROWS == 0 else half
    nb = half // tr

    def body(c_ref, g_ref, l_ref, o_ref):
        o_ref[...] = (g_ref[...].astype(f32) + l_ref[...].astype(f32)).astype(o_ref.dtype)

    blk = (1, tr, cols)
    return pl.pallas_call(
        body, name=name,
        grid_spec=pltpu.PrefetchScalarGridSpec(
            num_scalar_prefetch=1, grid=(N_CHIPS, nb),
            in_specs=[pl.BlockSpec(blk, lambda k, i, c_ref: (k, c_ref[0] * nb + i, 0)),
                      pl.BlockSpec(blk, lambda k, i, c_ref: (k, i, 0))],
            out_specs=pl.BlockSpec(blk, lambda k, i, c_ref: (k, i, 0))),
        out_shape=jax.ShapeDtypeStruct(land.shape, land.dtype),
        compiler_params=_cparams("parallel", "parallel"),
    )(core, g, land)


def scatter_grads(parts):
    n = len(parts)

    def body(*refs):
        p, land, send_sems, recv_sems = refs[:n], refs[n:2 * n], refs[2 * n], refs[2 * n + 1]
        x, y, c = _place()
        me_chip = 2 * x + y
        chips = _other_chips(x, y)
        sends = [_remote(p[a].at[2 * px + py], land[a].at[me_chip], send_sems.at[3 * a + j], recv_sems.at[3 * a + j], (px, py, c))
                 for a in range(n) for j, (px, py) in enumerate(chips)]
        for cp in sends:
            cp.start()
        for a in range(n):
            for j, (px, py) in enumerate(chips):
                slot = land[a].at[2 * px + py]
                _remote(slot, slot, send_sems.at[3 * a + j], recv_sems.at[3 * a + j], (px, py, c)).wait_recv()
        for cp in sends:
            cp.wait_send()

    return pl.pallas_call(
        body, name="scatter_grads", in_specs=[HBM_SPEC] * n, out_specs=[HBM_SPEC] * n,
        out_shape=[jax.ShapeDtypeStruct(p.shape, p.dtype) for p in parts],
        scratch_shapes=[pltpu.SemaphoreType.DMA((3 * n,)), pltpu.SemaphoreType.DMA((3 * n,))],
    )(*parts)


def _scatter_async_copies(parts, land, send_sems, recv_sems, x, y, c):
    me_chip = 2 * x + y
    sends, arrivals = [], []
    for a in range(len(parts)):
        for j, (px, py) in enumerate(_other_chips(x, y)):
            sems = (send_sems.at[3 * a + j], recv_sems.at[3 * a + j], (px, py, c))
            sends.append(_remote(parts[a].at[2 * px + py], land[a].at[me_chip], *sems))
            slot = land[a].at[2 * px + py]
            arrivals.append(_remote(slot, slot, *sems))
    return sends, arrivals


def scatter_grads_start(parts, *, name):
    n = len(parts)

    def body(*refs):
        p, land = refs[:n], refs[n:2 * n]
        send_sems, recv_sems, token = refs[2 * n], refs[2 * n + 1], refs[4 * n + 2]
        x, y, c = _place()
        for cp in _scatter_async_copies(p, land, send_sems, recv_sems, x, y, c)[0]:
            cp.start()
        token[...] = jnp.zeros_like(token)

    zones = [pltpu.with_memory_space_constraint(lax.empty(p.shape, p.dtype), pltpu.HBM) for p in parts]
    srcs = [pltpu.with_memory_space_constraint(p, pltpu.HBM) for p in parts]
    hbm = [pltpu.HBM(p.shape, p.dtype) for p in parts]
    out = pl.pallas_call(
        body, name=name,
        out_shape=[pltpu.SemaphoreType.DMA((3 * n,)), pltpu.SemaphoreType.DMA((3 * n,))] + hbm + hbm
        + [jax.ShapeDtypeStruct((8, LANES), f32)],
        in_specs=[HBM_SPEC] * (2 * n),
        out_specs=[SEM_SPEC, SEM_SPEC] + [HBM_SPEC] * (2 * n) + [pl.BlockSpec(memory_space=pltpu.VMEM)],
        input_output_aliases={i: 2 + i for i in range(2 * n)},
        compiler_params=pltpu.CompilerParams(has_side_effects=SPLIT_EFFECT),
    )(*srcs, *zones)
    return out[0], out[1], out[2:2 + n], out[2 + n:2 + 2 * n], out[-1]


def scatter_grads_wait(send_sems, recv_sems, parts, zones, after, *, name):
    n = len(parts)

    def body(*refs):
        p, land = refs[:n], refs[n:2 * n]
        x, y, c = _place()
        sends, arrivals = _scatter_async_copies(p, land, refs[2 * n], refs[2 * n + 1], x, y, c)
        for cp in sends:
            cp.wait_send()
        for cp in arrivals:
            cp.wait_recv()

    hbm = [pltpu.HBM(p.shape, p.dtype) for p in parts]
    out = pl.pallas_call(
        body, name=name, out_shape=hbm + hbm,
        in_specs=[HBM_SPEC] * (2 * n) + [SEM_SPEC, SEM_SPEC, pl.BlockSpec(memory_space=pl.ANY)],
        out_specs=[HBM_SPEC] * (2 * n),
        input_output_aliases={i: i for i in range(2 * n)},
        compiler_params=pltpu.CompilerParams(has_side_effects=SPLIT_EFFECT),
    )(*parts, *zones, send_sems, recv_sems, after)
    return out[:n], out[n:]


def sum_grads(part, land, order, *, name):
    _, half, cols = part.shape
    tr = GRAD_ROWS if half % GRAD_ROWS == 0 else half

    def body(order_ref, p_ref, l1_ref, l2_ref, l3_ref, o_ref):
        o_ref[...] = ((p_ref[0].astype(f32) + l1_ref[0].astype(f32)) + l2_ref[0].astype(f32)) + l3_ref[0].astype(f32)

    slot = lambda j: pl.BlockSpec((1, tr, cols), lambda i, order_ref: (order_ref[j], i, 0))
    return pl.pallas_call(
        body, name=name,
        grid_spec=pltpu.PrefetchScalarGridSpec(
            num_scalar_prefetch=1, grid=(half // tr,), in_specs=[slot(0), slot(1), slot(2), slot(3)],
            out_specs=pl.BlockSpec((tr, cols), lambda i, order_ref: (i, 0))),
        out_shape=jax.ShapeDtypeStruct((half, cols), f32),
        compiler_params=_cparams("parallel"),
    )(order, part, land, land, land)


def swap_reduced_halves(mine):
    n = len(mine)

    def body(*refs):
        r, out, send_sems, recv_sems = refs[:n], refs[n:2 * n], refs[2 * n], refs[2 * n + 1]
        x, y, c = _place()
        copies = [_remote(r[a], out[a], send_sems.at[a], recv_sems.at[a], (x, y, 1 - c)) for a in range(n)]
        for cp in copies:
            cp.start()
        for cp in copies:
            cp.wait()

    return pl.pallas_call(
        body, name="swap_reduced_halves", in_specs=[HBM_SPEC] * n, out_specs=[HBM_SPEC] * n,
        out_shape=[jax.ShapeDtypeStruct(r.shape, r.dtype) for r in mine],
        scratch_shapes=[pltpu.SemaphoreType.DMA((n,)), pltpu.SemaphoreType.DMA((n,))],
    )(*mine)


def adamw_halves(w, mine, theirs, m, v, core, *, name):
    R, C = w.shape
    tr = min(GRAD_ROWS, R // 2)
    half_nb = R // 2 // tr

    def body(c_ref, w_ref, a_ref, b_ref, m_ref, v_ref, g_ref, d_ref, nm_ref, nv_ref):
        low = pl.program_id(0) < half_nb
        gv = jnp.where(low == (c_ref[0] == 0), a_ref[...], b_ref[...])
        nm = ADAM_B1 * m_ref[...] + (1.0 - ADAM_B1) * gv
        nv = ADAM_B2 * v_ref[...] + (1.0 - ADAM_B2) * jnp.square(gv)
        m_hat = nm / (1.0 - ADAM_B1 ** ADAM_STEP)
        v_hat = nv / (1.0 - ADAM_B2 ** ADAM_STEP)
        g_ref[...] = gv
        d_ref[...] = -ADAM_LR * (m_hat / (jnp.sqrt(v_hat) + ADAM_EPS) + ADAM_WD * w_ref[...])
        nm_ref[...] = nm
        nv_ref[...] = nv

    full = pl.BlockSpec((tr, C), lambda i, c_ref: (i, 0))
    part = pl.BlockSpec((tr, C), lambda i, c_ref: (i % half_nb, 0))
    out = jax.ShapeDtypeStruct((R, C), f32)
    return pl.pallas_call(
        body, name=name,
        grid_spec=pltpu.PrefetchScalarGridSpec(
            num_scalar_prefetch=1, grid=(2 * half_nb,), in_specs=[full, part, part, full, full], out_specs=[full] * 4),
        out_shape=[out] * 4, compiler_params=_cparams("parallel"),
    )(core, w, mine, theirs, m, v)


N_DEV = 8


def all_reduce_small(v):
    def body(src_ref, out_ref, land_ref, send_sems, recv_sems):
        x, y, c = _place()
        me = 4 * x + 2 * y + c
        copies = []
        for r in range(1, N_DEV):
            peer = ((1 - x) if r & 4 else x, (1 - y) if r & 2 else y, (1 - c) if r & 1 else c)
            copies.append(_remote(src_ref, land_ref.at[r], send_sems.at[r - 1], recv_sems.at[r - 1], peer))
        for cp in copies:
            cp.start()
        land_ref[0] = src_ref[...]
        for cp in copies:
            cp.wait()
        acc = land_ref[me]
        for d in range(1, N_DEV):
            acc = acc + land_ref[jnp.bitwise_xor(me, d)]
        out_ref[...] = acc

    vm = pl.BlockSpec(memory_space=pltpu.VMEM)
    return pl.pallas_call(
        body, name="all_reduce_small", in_specs=[vm], out_specs=vm,
        out_shape=jax.ShapeDtypeStruct(v.shape, v.dtype),
        scratch_shapes=[pltpu.VMEM((N_DEV,) + v.shape, v.dtype),
                        pltpu.SemaphoreType.DMA((N_DEV - 1,)), pltpu.SemaphoreType.DMA((N_DEV - 1,))],
    )(v)


def adamw(w, g, m, v, *, name, tr=None, tc=None):
    R, C = w.shape
    if tc is None:
        tr, tc = min(tr, R), C
        blk = pl.BlockSpec((tr, C), lambda i: (i, 0))
    else:
        tr = R
        blk = pl.BlockSpec((R, tc), lambda i: (0, i))

    def body(w_ref, g_ref, m_ref, v_ref, d_ref, nm_ref, nv_ref):
        gv = g_ref[...]
        nm = ADAM_B1 * m_ref[...] + (1.0 - ADAM_B1) * gv
        nv = ADAM_B2 * v_ref[...] + (1.0 - ADAM_B2) * jnp.square(gv)
        m_hat = nm / (1.0 - ADAM_B1 ** ADAM_STEP)
        v_hat = nv / (1.0 - ADAM_B2 ** ADAM_STEP)
        d_ref[...] = -ADAM_LR * (m_hat / (jnp.sqrt(v_hat) + ADAM_EPS) + ADAM_WD * w_ref[...])
        nm_ref[...] = nm
        nv_ref[...] = nv

    out = jax.ShapeDtypeStruct((R, C), f32)
    return pl.pallas_call(
        body, name=name, grid=((R // tr) * (C // tc),), in_specs=[blk] * 4, out_specs=[blk] * 3, out_shape=[out] * 3,
        compiler_params=_cparams("parallel"),
    )(w, g, m, v)


BIG_SHARDS = (("w_in", (1024, 900), True), ("w_out", (256, 1024), False), ("w_cq", (256, 512), False),
              ("w_ckv", (256, 1024), False), ("w_co", (512, 256), True), ("w_mlp1", (1024, 1024), True),
              ("w_mlp2", (1024, 1024), False))
CONV_SHARD = (CONV_WIDTH, 3 * GDN_WIDTH // N_CHIPS)
SMALL_DIMS = (("norm_mix_g", 1024), ("fox_qnorm_g", 64), ("fox_knorm_g", 64), ("fox_f_bias", 8), ("fox_onorm_g", 64),
              ("gdn_A_log", 4), ("gdn_dt_bias", 4), ("gdn_onorm_g", 128), ("norm_xattn_g", 1024), ("mem_norm_g", 1024),
              ("xattn_qnorm_g", 128), ("xattn_knorm_g", 128), ("norm_mlp_g", 1024))
WEIGHT_ORDER = ("norm_mix_g", "w_in", "fox_qnorm_g", "fox_knorm_g", "fox_f_bias", "fox_onorm_g", "gdn_conv_w", "gdn_A_log",
                "gdn_dt_bias", "gdn_onorm_g", "w_out", "norm_xattn_g", "mem_norm_g", "w_cq", "w_ckv", "xattn_qnorm_g",
                "xattn_knorm_g", "w_co", "norm_mlp_g", "w_mlp1", "w_mlp2")


def _pack_rows(pieces, rows, lead=()):
    flat = []
    for p in pieces:
        p = p.reshape(lead + (-1,))
        pad = (-p.shape[-1]) % LANES
        flat.append(jnp.pad(p, [(0, 0)] * len(lead) + [(0, pad)]) if pad else p)
    cat = jnp.concatenate(flat, axis=-1)
    cat = jnp.pad(cat, [(0, 0)] * len(lead) + [(0, rows * LANES - cat.shape[-1])])
    return cat.reshape(lead + (rows, LANES))


def _unpack_rows(buf, sizes, lead=()):
    flat = buf.reshape(lead + (-1,))
    out, off = [], 0
    for n in sizes:
        out.append(flat[..., off:off + n])
        off += n + (-n) % LANES
    return out


def _conv_to_wire(conv):
    return lax.bitcast_convert_type(conv, bf16)


def _conv_from_wire(wire):
    return lax.bitcast_convert_type(wire, f32)


SMALL_ROWS = 96
SMALL_ADAM_ROWS = 56


def kernel(x, mem, norm_mix_g, w_in, fox_qnorm_g, fox_knorm_g, fox_f_bias, fox_onorm_g, gdn_conv_w, gdn_A_log, gdn_dt_bias, gdn_onorm_g, w_out, norm_xattn_g, mem_norm_g, w_cq, w_ckv, xattn_qnorm_g, xattn_knorm_g, w_co, norm_mlp_g, w_mlp1, w_mlp2, loss_target, m_norm_mix_g, m_w_in, m_fox_qnorm_g, m_fox_knorm_g, m_fox_f_bias, m_fox_onorm_g, m_gdn_conv_w, m_gdn_A_log, m_gdn_dt_bias, m_gdn_onorm_g, m_w_out, m_norm_xattn_g, m_mem_norm_g, m_w_cq, m_w_ckv, m_xattn_qnorm_g, m_xattn_knorm_g, m_w_co, m_norm_mlp_g, m_w_mlp1, m_w_mlp2, v_norm_mix_g, v_w_in, v_fox_qnorm_g, v_fox_knorm_g, v_fox_f_bias, v_fox_onorm_g, v_gdn_conv_w, v_gdn_A_log, v_gdn_dt_bias, v_gdn_onorm_g, v_w_out, v_norm_xattn_g, v_mem_norm_g, v_w_cq, v_w_ckv, v_xattn_qnorm_g, v_xattn_knorm_g, v_w_co, v_norm_mlp_g, v_w_mlp1, v_w_mlp2):
    wts = dict(norm_mix_g=norm_mix_g, w_in=w_in, fox_qnorm_g=fox_qnorm_g, fox_knorm_g=fox_knorm_g, fox_f_bias=fox_f_bias,
               fox_onorm_g=fox_onorm_g, gdn_conv_w=gdn_conv_w, gdn_A_log=gdn_A_log, gdn_dt_bias=gdn_dt_bias,
               gdn_onorm_g=gdn_onorm_g, w_out=w_out, norm_xattn_g=norm_xattn_g, mem_norm_g=mem_norm_g, w_cq=w_cq, w_ckv=w_ckv,
               xattn_qnorm_g=xattn_qnorm_g, xattn_knorm_g=xattn_knorm_g, w_co=w_co, norm_mlp_g=norm_mlp_g, w_mlp1=w_mlp1,
               w_mlp2=w_mlp2)
    mom = dict(norm_mix_g=m_norm_mix_g, w_in=m_w_in, fox_qnorm_g=m_fox_qnorm_g, fox_knorm_g=m_fox_knorm_g,
               fox_f_bias=m_fox_f_bias, fox_onorm_g=m_fox_onorm_g, gdn_conv_w=m_gdn_conv_w, gdn_A_log=m_gdn_A_log,
               gdn_dt_bias=m_gdn_dt_bias, gdn_onorm_g=m_gdn_onorm_g, w_out=m_w_out, norm_xattn_g=m_norm_xattn_g,
               mem_norm_g=m_mem_norm_g, w_cq=m_w_cq, w_ckv=m_w_ckv, xattn_qnorm_g=m_xattn_qnorm_g,
               xattn_knorm_g=m_xattn_knorm_g, w_co=m_w_co, norm_mlp_g=m_norm_mlp_g, w_mlp1=m_w_mlp1, w_mlp2=m_w_mlp2)
    var = dict(norm_mix_g=v_norm_mix_g, w_in=v_w_in, fox_qnorm_g=v_fox_qnorm_g, fox_knorm_g=v_fox_knorm_g,
               fox_f_bias=v_fox_f_bias, fox_onorm_g=v_fox_onorm_g, gdn_conv_w=v_gdn_conv_w, gdn_A_log=v_gdn_A_log,
               gdn_dt_bias=v_gdn_dt_bias, gdn_onorm_g=v_gdn_onorm_g, w_out=v_w_out, norm_xattn_g=v_norm_xattn_g,
               mem_norm_g=v_mem_norm_g, w_cq=v_w_cq, w_ckv=v_w_ckv, xattn_qnorm_g=v_xattn_qnorm_g,
               xattn_knorm_g=v_xattn_knorm_g, w_co=v_w_co, norm_mlp_g=v_norm_mlp_g, w_mlp1=v_w_mlp1, w_mlp2=v_w_mlp2)
    B, S, D = x.shape
    T = B * S
    big_names = [n for n, _, _ in BIG_SHARDS]
    chip = 2 * lax.axis_index("x") + lax.axis_index("y")
    core = lax.axis_index("c").astype(jnp.int32).reshape(1)

    shards = {n: wts[n][0].astype(MXU_DTYPE) for n in big_names[1:]}
    in_t = lambda p: jnp.swapaxes(p[0], 0, 1)
    shards["w_in"] = jnp.pad(in_t(w_in).astype(MXU_DTYPE), ((0, IN_SHARD_PAD - IN_SHARD), (0, 0)))
    w_in_all, conv_all = gather_weights([shards["w_in"]], gdn_conv_w[0])
    late = big_names[1:]
    send_sems, recv_sems, late_src, late_zones, token = gather_weights_start([shards[n] for n in late], conv_all)
    own = lambda g, s: lax.dynamic_update_slice(g, s[None], (chip,) + (0,) * s.ndim)
    full = {"w_in": own(w_in_all, shards["w_in"])}
    conv_full = own(conv_all, gdn_conv_w[0]).transpose(1, 0, 2).reshape(CONV_WIDTH, 3 * GDN_WIDTH)
    rows = lambda g: g.reshape(N_CHIPS * g.shape[1], g.shape[2])
    w_in_t = full["w_in"][:, :IN_SHARD].reshape(IN_DIM, D_MODEL)

    def late_weights(after):
        zones = gather_weights_wait(send_sems, recv_sems, late_src, late_zones, after)
        got = {n: own(z, shards[n]) for n, z in zip(late, zones)}
        return dict(w_out=rows(got["w_out"]), w_cq=rows(got["w_cq"]), w_ckv=rows(got["w_ckv"]), w_co=got["w_co"],
                    w_mlp1=got["w_mlp1"], w_mlp2=rows(got["w_mlp2"]))

    def chip_partials(names, by_chip):
        landed = swap_grad_halves(by_chip, name="swap_grad_halves_" + names[0])
        return [add_grad_halves(g, l, core, name="add_halves_" + n) for n, g, l in zip(names, by_chip, landed)]

    in_flight = []

    def grads_ready(ready):
        names = list(ready)
        *started, tok = scatter_grads_start(chip_partials(names, [ready[n] for n in names]),
                                            name="scatter_grads_start_%d" % len(in_flight))
        in_flight.append((names, *started))
        return tok

    w = dict(wa_t=align_w_in_t(w_in_t), conv_w=conv_full, late=late_weights, grads_ready=grads_ready)
    sp = {n: wts[n] for n, _ in SMALL_DIMS}
    sp["norm_mix_g"] = sp["norm_mix_g"] + token[0, 0]

    loss_part, grad_x, g_big, g_small = local_step(x.reshape(T, D), mem.reshape(-1, D), loss_target.reshape(T, D), w, sp, B=B)

    small_pieces = [g_small[n] for n, _ in SMALL_DIMS] + [g_small["gdn_conv_w"], loss_part]
    small_sizes = [d for _, d in SMALL_DIMS] + [CONV_WIDTH * 3 * GDN_WIDTH, LANES]
    red_small = _unpack_rows(all_reduce_small(_pack_rows(small_pieces, SMALL_ROWS)), small_sizes)
    grads = {n: p.reshape(1, d) for (n, d), p in zip(SMALL_DIMS, red_small)}
    conv_grad = lax.dynamic_slice(red_small[-2].reshape(CONV_WIDTH, 3 * GDN_WIDTH), (0, chip * CONV_SHARD[1]), CONV_SHARD)
    grads["gdn_conv_w"] = conv_grad.reshape((1,) + CONV_SHARD)
    loss = red_small[-1][0]

    names = ["w_in"]
    chip_part = chip_partials(names, [g_big[n] for n in names])
    parts, zones = dict(zip(names, chip_part)), dict(zip(names, scatter_grads(chip_part)))
    for k, (names, send_sems, recv_sems, thru, land) in enumerate(in_flight):
        thru, land = scatter_grads_wait(send_sems, recv_sems, thru, land, grad_x, name="scatter_grads_wait_%d" % k)
        parts.update(zip(names, thru))
        zones.update(zip(names, land))
    order = jnp.stack([chip, chip ^ 2, chip ^ 1, chip ^ 3]).astype(jnp.int32)
    mine = [sum_grads(parts[n], zones[n], order, name="sum_chips_" + n) for n in big_names]
    theirs = swap_reduced_halves(mine)

    delta, new_m, new_v = {}, {}, {}
    for n, a, b in zip(big_names[1:], mine[1:], theirs[1:]):
        g, d, nm, nv = adamw_halves(wts[n][0], a, b, mom[n][0], var[n][0], core, name="adamw_" + n)
        grads[n], delta[n], new_m[n], new_v[n] = g[None], d[None], nm[None], nv[None]
    south = core[0] == 0
    g_in_t = jnp.concatenate([jnp.where(south, mine[0], theirs[0]), jnp.where(south, theirs[0], mine[0])])[:IN_SHARD]
    back = lambda t: jnp.swapaxes(t, 0, 1)[None]
    d, nm, nv = adamw(in_t(w_in), g_in_t, in_t(m_w_in), in_t(v_w_in), name="adamw_w_in", tc=256)
    grads["w_in"], delta["w_in"], new_m["w_in"], new_v["w_in"] = back(g_in_t), back(d), back(nm), back(nv)
    small_names = [n for n, _ in SMALL_DIMS] + ["gdn_conv_w"]
    small_sz = [d for _, d in SMALL_DIMS] + [CONV_SHARD[0] * CONV_SHARD[1]]
    packed4 = [_pack_rows([src[n] for n in small_names], SMALL_ADAM_ROWS) for src in (wts, grads, mom, var)]
    outs = adamw(*packed4, name="adamw_small", tr=SMALL_ADAM_ROWS)
    for dst, buf in zip((delta, new_m, new_v), outs):
        for n, p in zip(small_names, _unpack_rows(buf, small_sz)):
            dst[n] = p.reshape(wts[n].shape)

    return (loss, grad_x.reshape(B, S, D), *[grads[n] for n in WEIGHT_ORDER], *[delta[n] for n in WEIGHT_ORDER],
            *[new_m[n] for n in WEIGHT_ORDER], *[new_v[n] for n in WEIGHT_ORDER])
```

```python
import functools

import jax
import jax.numpy as jnp
import numpy as np
from jax import lax
from jax.experimental import pallas as pl
from jax.experimental.pallas import tpu as pltpu

f32 = jnp.float32
bf16 = jnp.bfloat16
MXU_DTYPE = jnp.bfloat16
WIRE_DTYPE = jnp.bfloat16
INV_PRECISION = lax.Precision.HIGH

D_MODEL = 1024
FOX_HEADS = 8
FOX_HEAD_DIM = 64
FOX_WIDTH = 512
GDN_HEADS = 4
GDN_HEAD_DIM = 128
GDN_WIDTH = 512
CONV_WIDTH = 4
GDN_CHUNK = 64
XATTN_HEADS = 4
XATTN_HEAD_DIM = 128
XATTN_WIDTH = 512
D_FF = 4096
IN_DIM = 3600
EPS = 1e-6
NEG_INF = -1e30
LANES = 128
ADAM_LR = 0.001
ADAM_B1 = 0.9
ADAM_B2 = 0.999
ADAM_EPS = 1e-08
ADAM_WD = 0.01
ADAM_STEP = 10
VMEM_LIMIT = 48 * 1024 * 1024

COL_FOX = 0
COL_GDN = 1536
COL_Z = 3072
COL_SMALL = 3584
IN_ALIGNED = 3840
IN_TILE = 768
SM_F = 0
SM_B = 8
SM_A = 12


def _cparams(*sem):
    return pltpu.CompilerParams(dimension_semantics=sem, vmem_limit_bytes=VMEM_LIMIT)


def _mx(v):
    return v.astype(MXU_DTYPE)


def _dot(a, b, dims, precision=None):
    return lax.dot_general(a, b, (dims, ((), ())), preferred_element_type=f32, precision=precision)


def _dotm(a, b, dims):
    return _dot(_mx(a), _mx(b), dims)


NN = ((1,), (0,))
NT = ((1,), (1,))
TN = ((0,), (0,))


def matmul(a, b, *, name, ta=False, tb=False, b_stacked=False, out_stacked=False, residual=None, relu2_out=False,
           relu2_bwd_aux=None, out_dtype=f32, tm=1024, tn=1024, tk=1024):
    M, K = (a.shape[1], a.shape[0]) if ta else a.shape
    if b_stacked:
        b_cols = b.shape[2]
        N, tk = (b.shape[1], min(tk, b_cols)) if tb else (N_CHIPS * b_cols, tk)
        tn = tn if tb else min(tn, b_cols)
        assert K == (N_CHIPS * b_cols if tb else b.shape[1]), (name, a.shape, b.shape)
    else:
        N = b.shape[0] if tb else b.shape[1]
    if out_stacked:
        tn = min(tn, N // N_CHIPS)
    tm, tn, tk = min(tm, M), min(tn, N), min(tk, K)
    assert M % tm == 0 and N % tn == 0 and K % tk == 0, (name, M, N, K)
    nk = K // tk
    has_res = residual is not None
    has_aux = relu2_bwd_aux is not None

    def body(*refs):
        a_ref, b_ref = refs[0], refs[1]
        pos = 2
        res_ref = aux_ref = None
        if has_res:
            res_ref = refs[pos]
            pos += 1
        if has_aux:
            aux_ref = refs[pos]
            pos += 1
        o_ref = refs[pos]
        k = pl.program_id(2)
        dims = ((0,) if ta else (1,), (1,) if tb else (0,))
        part = _dot(_mx(a_ref[...]), _mx(b_ref[...]), dims)

        def finish(r):
            if has_res:
                r = r + res_ref[...]
            if has_aux:
                r = r * (2.0 * jnp.sqrt(aux_ref[...].astype(f32)))
            if relu2_out:
                o_ref[...] = jnp.square(jnp.maximum(r, 0.0)).astype(o_ref.dtype)
            else:
                o_ref[...] = r.astype(o_ref.dtype)

        if nk == 1:
            finish(part)
            return
        acc_ref = refs[pos + 1]

        @pl.when(k == 0)
        def _():
            acc_ref[...] = part

        @pl.when((k > 0) & (k < nk - 1))
        def _():
            acc_ref[...] += part

        @pl.when(k == nk - 1)
        def _():
            finish(acc_ref[...] + part)

    a_spec = pl.BlockSpec((tk, tm), lambda i, j, k: (k, i)) if ta else pl.BlockSpec((tm, tk), lambda i, j, k: (i, k))
    if b_stacked and tb:
        per = b_cols // tk
        b_spec = pl.BlockSpec((None, tn, tk), lambda i, j, k: (k // per, j, k % per))
    elif b_stacked:
        per = b_cols // tn
        b_spec = pl.BlockSpec((None, tk, tn), lambda i, j, k: (j // per, k, j % per))
    else:
        b_spec = pl.BlockSpec((tn, tk), lambda i, j, k: (j, k)) if tb else pl.BlockSpec((tk, tn), lambda i, j, k: (k, j))
    if out_stacked:
        assert not (has_res or has_aux or relu2_out), name
        per_o = N // N_CHIPS // tn
        o_spec = pl.BlockSpec((None, tm, tn), lambda i, j, k: (j // per_o, i, j % per_o))
        out_full = (N_CHIPS, M, N // N_CHIPS)
    else:
        o_spec = pl.BlockSpec((tm, tn), lambda i, j, k: (i, j))
        out_full = (M, N)
    in_specs, args = [a_spec, b_spec], [a, b]
    if has_res:
        in_specs.append(o_spec)
        args.append(residual)
    if has_aux:
        in_specs.append(o_spec)
        args.append(relu2_bwd_aux)
    out_shape = [jax.ShapeDtypeStruct(out_full, out_dtype)]
    out_specs = [o_spec]
    res = pl.pallas_call(
        body, name=name, grid=(M // tm, N // tn, nk), in_specs=in_specs, out_specs=out_specs, out_shape=out_shape,
        scratch_shapes=[pltpu.VMEM((tm, tn), f32)] if nk > 1 else [],
        compiler_params=_cparams("parallel", "parallel", "arbitrary"),
    )(*args)
    return res[0]


def matmul_rows(a, b, extras, *, name, mode, tb=False, b_stacked=False, tm=512, tk=1024):
    M, K = a.shape
    N = D_MODEL
    if b_stacked:
        assert tb, name
        tk = min(tk, b.shape[2])
        per = b.shape[2] // tk
        b_spec = pl.BlockSpec((None, N, tk), lambda i, k: (k // per, 0, k % per))
    elif tb:
        tk = min(tk, K)
        b_spec = pl.BlockSpec((N, tk), lambda i, k: (0, k))
    else:
        tk = min(tk, K)
        b_spec = pl.BlockSpec((tk, N), lambda i, k: (k, 0))
    tm = min(tm, M)
    assert M % tm == 0 and K % tk == 0, (name, M, K)
    nk = K // tk
    extras = [e for e in extras if e is not None]
    n_ex = len(extras)

    def body(*refs):
        a_ref, b_ref = refs[0], refs[1]
        ex = refs[2:2 + n_ex]
        o_ref, s_ref = refs[2 + n_ex], refs[3 + n_ex]
        i, k = pl.program_id(0), pl.program_id(1)
        part = _dot(_mx(a_ref[...]), _mx(b_ref[...]), ((1,), (1,) if tb else (0,)))

        def finish(y):
            @pl.when(i == 0)
            def _():
                s_ref[...] = jnp.zeros_like(s_ref)

            if mode == "rms_bwd":
                xv, gv = ex[0][...], ex[1][...]
                rstd = lax.rsqrt(jnp.mean(xv * xv, axis=-1, keepdims=True) + EPS)
                xhat = xv * rstd
                gd = y * gv
                dx = rstd * (gd - xhat * jnp.mean(gd * xhat, axis=-1, keepdims=True))
                o_ref[...] = dx + ex[2][...] if n_ex == 3 else dx
                s_ref[...] += jnp.sum(y * xhat, axis=0, keepdims=True)
            else:
                e = y + ex[0][...] - ex[1][...]
                o_ref[...] = e * (1.0 / N)
                tot = 0.5 * jnp.sum(jnp.mean(e * e, axis=-1, keepdims=True), axis=0, keepdims=True)
                s_ref[...] += jnp.broadcast_to(tot, s_ref.shape)

        if nk == 1:
            finish(part)
            return
        acc_ref = refs[4 + n_ex]

        @pl.when(k == 0)
        def _():
            acc_ref[...] = part

        @pl.when((k > 0) & (k < nk - 1))
        def _():
            acc_ref[...] += part

        @pl.when(k == nk - 1)
        def _():
            finish(acc_ref[...] + part)

    row = pl.BlockSpec((tm, N), lambda i, k: (i, 0))
    vec = pl.BlockSpec((1, N), lambda i, k: (0, 0))
    if mode == "rms_bwd":
        ex_specs = [row, vec] + ([row] if n_ex == 3 else [])
        s_shape, s_spec = jax.ShapeDtypeStruct((1, N), f32), vec
    else:
        ex_specs = [row, row]
        s_shape, s_spec = jax.ShapeDtypeStruct((1, LANES), f32), pl.BlockSpec((1, LANES), lambda i, k: (0, 0))
    return pl.pallas_call(
        body, name=name, grid=(M // tm, nk),
        in_specs=[pl.BlockSpec((tm, tk), lambda i, k: (i, k)), b_spec] + ex_specs,
        out_specs=[row, s_spec], out_shape=[jax.ShapeDtypeStruct((M, N), f32), s_shape],
        scratch_shapes=[pltpu.VMEM((tm, N), f32)] if nk > 1 else [],
        compiler_params=_cparams("arbitrary", "arbitrary"),
    )(a, b, *extras)


def rms_fwd(x, g, *, name, tr=512):
    R, D = x.shape
    tr = min(tr, R)

    def body(x_ref, g_ref, o_ref):
        xv = x_ref[...]
        y = xv * lax.rsqrt(jnp.mean(xv * xv, axis=-1, keepdims=True) + EPS)
        o_ref[...] = (y * g_ref[...]).astype(o_ref.dtype)

    return pl.pallas_call(
        body, name=name, grid=(R // tr,),
        in_specs=[pl.BlockSpec((tr, D), lambda i: (i, 0)), pl.BlockSpec((1, D), lambda i: (0, 0))],
        out_specs=pl.BlockSpec((tr, D), lambda i: (i, 0)),
        out_shape=jax.ShapeDtypeStruct((R, D), MXU_DTYPE),
        compiler_params=_cparams("parallel"),
    )(x, g)


def rms_bwd(x, g, dh, residual, *, name, tr=512):
    R, D = x.shape
    tr = min(tr, R)
    has_res = residual is not None

    def body(*refs):
        if has_res:
            x_ref, g_ref, dh_ref, res_ref, dx_ref, dg_ref = refs
        else:
            x_ref, g_ref, dh_ref, dx_ref, dg_ref = refs
        xv = x_ref[...]
        rstd = lax.rsqrt(jnp.mean(xv * xv, axis=-1, keepdims=True) + EPS)
        xhat = xv * rstd
        dh = dh_ref[...].astype(f32)
        gd = dh * g_ref[...]
        dx = rstd * (gd - xhat * jnp.mean(gd * xhat, axis=-1, keepdims=True))
        if has_res:
            dx = dx + res_ref[...]
        dx_ref[...] = dx

        @pl.when(pl.program_id(0) == 0)
        def _():
            dg_ref[...] = jnp.zeros_like(dg_ref)

        dg_ref[...] += jnp.sum(dh * xhat, axis=0, keepdims=True)

    row = pl.BlockSpec((tr, D), lambda i: (i, 0))
    vec = pl.BlockSpec((1, D), lambda i: (0, 0))
    in_specs = [row, vec, row] + ([row] if has_res else [])
    args = [x, g, dh] + ([residual] if has_res else [])
    return pl.pallas_call(
        body, name=name, grid=(R // tr,), in_specs=in_specs, out_specs=[row, vec],
        out_shape=[jax.ShapeDtypeStruct((R, D), f32), jax.ShapeDtypeStruct((1, D), f32)],
        compiler_params=_cparams("arbitrary"),
    )(*args)


def loss_head(y, target, *, tr=512):
    R, D = y.shape
    tr = min(tr, R)

    def body(y_ref, t_ref, dy_ref, loss_ref):
        e = y_ref[...] - t_ref[...]
        dy_ref[...] = e * (1.0 / D)

        @pl.when(pl.program_id(0) == 0)
        def _():
            loss_ref[...] = jnp.zeros_like(loss_ref)

        part = 0.5 * jnp.sum(jnp.mean(e * e, axis=-1, keepdims=True), axis=0, keepdims=True)
        loss_ref[...] += jnp.broadcast_to(part, loss_ref.shape)

    row = pl.BlockSpec((tr, D), lambda i: (i, 0))
    return pl.pallas_call(
        body, name="loss_head", grid=(R // tr,), in_specs=[row, row],
        out_specs=[row, pl.BlockSpec((1, LANES), lambda i: (0, 0))],
        out_shape=[jax.ShapeDtypeStruct((R, D), f32), jax.ShapeDtypeStruct((1, LANES), f32)],
        compiler_params=_cparams("arbitrary"),
    )(y, target)


def _head_rms(v, g):
    r = lax.rsqrt(jnp.mean(v * v, axis=-1, keepdims=True) + EPS)
    return v * r * g, r


def _head_rms_bwd(v, r, g, dn):
    vhat = v * r
    gd = dn * g
    dv = r * (gd - vhat * jnp.mean(gd * vhat, axis=-1, keepdims=True))
    return dv, jnp.sum(dn * vhat, axis=0, keepdims=True)


def _softmax_rows(s):
    m = jnp.max(s, axis=-1, keepdims=True)
    e = jnp.exp(s - m)
    return e / jnp.sum(e, axis=-1, keepdims=True)


def xattn_fwd(cq, ckv, gq, gk, *, B, tq=512):
    T = cq.shape[0]
    S = T // B
    M = ckv.shape[0] // B
    tq = min(tq, S)
    nq = S // tq
    scale = XATTN_HEAD_DIM ** -0.5

    def body(q_ref, k_ref, v_ref, gq_ref, gk_ref, o_ref):
        qn, _ = _head_rms(q_ref[...], gq_ref[...])
        kn, _ = _head_rms(k_ref[...], gk_ref[...])
        p = _softmax_rows(_dot(_mx(qn), _mx(kn), NT) * scale)
        o_ref[...] = _dot(_mx(p), _mx(v_ref[...]), NN).astype(o_ref.dtype)

    hd = XATTN_HEAD_DIM
    vec = pl.BlockSpec((1, hd), lambda b, h, i: (0, 0))
    return pl.pallas_call(
        body, name="xattn_fwd", grid=(B, XATTN_HEADS, nq),
        in_specs=[pl.BlockSpec((tq, hd), lambda b, h, i: (b * nq + i, h)),
                  pl.BlockSpec((M, hd), lambda b, h, i: (b, h)),
                  pl.BlockSpec((M, hd), lambda b, h, i: (b, XATTN_HEADS + h)), vec, vec],
        out_specs=pl.BlockSpec((tq, hd), lambda b, h, i: (b * nq + i, h)),
        out_shape=jax.ShapeDtypeStruct((T, XATTN_WIDTH), MXU_DTYPE),
        compiler_params=_cparams("parallel", "parallel", "parallel"),
    )(cq, ckv, ckv, gq, gk)


def xattn_bwd(cq, ckv, gq, gk, dco, *, B, tq=512):
    T = cq.shape[0]
    S = T // B
    M = ckv.shape[0] // B
    tq = min(tq, S)
    nq = S // tq
    scale = XATTN_HEAD_DIM ** -0.5
    hd = XATTN_HEAD_DIM

    def body(q_ref, k_ref, v_ref, gq_ref, gk_ref, do_ref, dq_ref, dk_ref, dv_ref, dgq_ref, dgk_ref, dkn_acc, dv_acc):
        b, h, i = pl.program_id(0), pl.program_id(1), pl.program_id(2)

        @pl.when((b == 0) & (h == 0) & (i == 0))
        def _():
            dgq_ref[...] = jnp.zeros_like(dgq_ref)
            dgk_ref[...] = jnp.zeros_like(dgk_ref)

        @pl.when(i == 0)
        def _():
            dkn_acc[...] = jnp.zeros_like(dkn_acc)
            dv_acc[...] = jnp.zeros_like(dv_acc)

        q, k, v = q_ref[...], k_ref[...], v_ref[...]
        gqv, gkv = gq_ref[...], gk_ref[...]
        qn, rq = _head_rms(q, gqv)
        kn, rk = _head_rms(k, gkv)
        p = _softmax_rows(_dot(_mx(qn), _mx(kn), NT) * scale)
        do = do_ref[...]
        dv_acc[...] += _dot(_mx(p), _mx(do), TN)
        dp = _dot(_mx(do), _mx(v), NT)
        ds = p * (dp - jnp.sum(dp * p, axis=-1, keepdims=True)) * scale
        dqn = _dot(_mx(ds), _mx(kn), NN)
        dkn_acc[...] += _dot(_mx(ds), _mx(qn), TN)
        dq, dgq = _head_rms_bwd(q, rq, gqv, dqn)
        dq_ref[...] = dq.astype(dq_ref.dtype)
        dgq_ref[...] += dgq

        @pl.when(i == nq - 1)
        def _():
            dk, dgk = _head_rms_bwd(k, rk, gkv, dkn_acc[...])
            dk_ref[...] = dk.astype(dk_ref.dtype)
            dv_ref[...] = dv_acc[...].astype(dv_ref.dtype)
            dgk_ref[...] += dgk

    vec = pl.BlockSpec((1, hd), lambda b, h, i: (0, 0))
    qspec = pl.BlockSpec((tq, hd), lambda b, h, i: (b * nq + i, h))
    kspec = pl.BlockSpec((M, hd), lambda b, h, i: (b, h))
    vspec = pl.BlockSpec((M, hd), lambda b, h, i: (b, XATTN_HEADS + h))
    dq, dk, dv, dgq, dgk = pl.pallas_call(
        body, name="xattn_bwd", grid=(B, XATTN_HEADS, nq),
        in_specs=[qspec, kspec, vspec, vec, vec, qspec],
        out_specs=[qspec, kspec, kspec, vec, vec],
        out_shape=[jax.ShapeDtypeStruct((T, XATTN_WIDTH), MXU_DTYPE),
                   jax.ShapeDtypeStruct((B * M, XATTN_WIDTH), MXU_DTYPE),
                   jax.ShapeDtypeStruct((B * M, XATTN_WIDTH), MXU_DTYPE),
                   jax.ShapeDtypeStruct((1, hd), f32), jax.ShapeDtypeStruct((1, hd), f32)],
        scratch_shapes=[pltpu.VMEM((M, hd), f32), pltpu.VMEM((M, hd), f32)],
        compiler_params=_cparams("arbitrary", "arbitrary", "arbitrary"),
    )(cq, ckv, ckv, gq, gk, dco)
    return dq, jnp.concatenate([dk, dv], axis=1), dgq, dgk


FOX_PAIRS = FOX_HEADS // 2


def _fox_scores(qn, kn, ccol, crow, q0, tq, S, scale):
    s = _dot(_mx(qn), _mx(kn), NT) * scale + ccol - crow
    qpos = q0 + lax.broadcasted_iota(jnp.int32, (tq, S), 0)
    kpos = lax.broadcasted_iota(jnp.int32, (tq, S), 1)
    return jnp.where(kpos <= qpos, s, NEG_INF)


def fox_fwd(P, ccol, crow, gq, gk, go, *, B, tq=256):
    T = P.shape[0]
    S = T // B
    tq = min(tq, S)
    nq = S // tq
    hd = FOX_HEAD_DIM
    scale = hd ** -0.5

    def body(q_ref, k_ref, v_ref, ccol_ref, crow_ref, gq_ref, gk_ref, go_ref, o_ref, oa_ref):
        q0 = pl.program_id(2) * tq
        for e in range(2):
            sl = slice(e * hd, (e + 1) * hd)
            qn, _ = _head_rms(q_ref[:, sl], gq_ref[:, sl])
            kn, _ = _head_rms(k_ref[:, sl], gk_ref[:, sl])
            p = _softmax_rows(_fox_scores(qn, kn, ccol_ref[0, e], crow_ref[0, e], q0, tq, S, scale))
            o = _dot(_mx(p), _mx(v_ref[:, sl]), NN)
            o_ref[:, sl] = o
            oa_ref[:, sl] = _head_rms(o, go_ref[:, sl])[0].astype(oa_ref.dtype)

    W = 2 * hd
    vec = pl.BlockSpec((1, W), lambda b, h, i: (0, 0))
    ospec = pl.BlockSpec((tq, W), lambda b, h, i: (b * nq + i, h))
    return pl.pallas_call(
        body, name="fox_fwd", grid=(B, FOX_PAIRS, nq),
        in_specs=[pl.BlockSpec((tq, W), lambda b, h, i: (b * nq + i, h)),
                  pl.BlockSpec((S, W), lambda b, h, i: (b, FOX_PAIRS + h)),
                  pl.BlockSpec((S, W), lambda b, h, i: (b, 2 * FOX_PAIRS + h)),
                  pl.BlockSpec((1, 2, tq, 1), lambda b, h, i: (b, h, i, 0)),
                  pl.BlockSpec((1, 2, 1, S), lambda b, h, i: (b, h, 0, 0)), vec, vec, vec],
        out_specs=[ospec, ospec],
        out_shape=[jax.ShapeDtypeStruct((T, FOX_WIDTH), f32), jax.ShapeDtypeStruct((T, FOX_WIDTH), MXU_DTYPE)],
        compiler_params=_cparams("parallel", "parallel", "parallel"),
    )(P, P, P, ccol, crow, gq, gk, go)


def fox_bwd(P, ccol, crow, gq, gk, go, o_raw, d_oab, *, B, tq=256):
    T = P.shape[0]
    S = T // B
    tq = min(tq, S)
    nq = S // tq
    hd = FOX_HEAD_DIM
    scale = hd ** -0.5

    def body(q_ref, k_ref, v_ref, ccol_ref, crow_ref, gq_ref, gk_ref, go_ref, o_ref, doa_ref,
             dq_ref, dk_ref, dv_ref, dccol_ref, dcrow_ref, dgq_ref, dgk_ref, dgo_ref, dkn_acc, dv_acc, dcrow_acc):
        b, h, i = pl.program_id(0), pl.program_id(1), pl.program_id(2)
        q0 = i * tq

        @pl.when((b == 0) & (h == 0) & (i == 0))
        def _():
            dgq_ref[...] = jnp.zeros_like(dgq_ref)
            dgk_ref[...] = jnp.zeros_like(dgk_ref)
            dgo_ref[...] = jnp.zeros_like(dgo_ref)

        @pl.when(i == 0)
        def _():
            dkn_acc[...] = jnp.zeros_like(dkn_acc)
            dv_acc[...] = jnp.zeros_like(dv_acc)
            dcrow_acc[...] = jnp.zeros_like(dcrow_acc)

        for e in range(2):
            sl = slice(e * hd, (e + 1) * hd)
            q, k, v = q_ref[:, sl], k_ref[:, sl], v_ref[:, sl]
            gqv, gkv, gov = gq_ref[:, sl], gk_ref[:, sl], go_ref[:, sl]
            qn, rq = _head_rms(q, gqv)
            kn, rk = _head_rms(k, gkv)
            p = _softmax_rows(_fox_scores(qn, kn, ccol_ref[0, e], crow_ref[0, e], q0, tq, S, scale))
            o = o_ref[:, sl]
            ro = lax.rsqrt(jnp.mean(o * o, axis=-1, keepdims=True) + EPS)
            do, dgo = _head_rms_bwd(o, ro, gov, doa_ref[:, sl])
            dgo_ref[:, sl] += dgo
            dv_acc[e] += _dot(_mx(p), _mx(do), TN)
            dp = _dot(_mx(do), _mx(v), NT)
            ds = p * (dp - jnp.sum(do * o, axis=-1, keepdims=True))
            dccol_ref[0, e] = jnp.sum(ds, axis=1, keepdims=True)
            dcrow_acc[e] -= jnp.sum(ds, axis=0, keepdims=True)
            dqn = _dot(_mx(ds), _mx(kn), NN) * scale
            dkn_acc[e] += _dot(_mx(ds), _mx(qn), TN) * scale
            dq, dgq = _head_rms_bwd(q, rq, gqv, dqn)
            dq_ref[:, sl] = dq.astype(dq_ref.dtype)
            dgq_ref[:, sl] += dgq

        @pl.when(i == nq - 1)
        def _():
            for e in range(2):
                sl = slice(e * hd, (e + 1) * hd)
                k = k_ref[:, sl]
                gkv = gk_ref[:, sl]
                rk = lax.rsqrt(jnp.mean(k * k, axis=-1, keepdims=True) + EPS)
                dk, dgk = _head_rms_bwd(k, rk, gkv, dkn_acc[e])
                dk_ref[:, sl] = dk.astype(dk_ref.dtype)
                dv_ref[:, sl] = dv_acc[e].astype(dv_ref.dtype)
                dgk_ref[:, sl] += dgk
                dcrow_ref[0, e] = dcrow_acc[e]

    W = 2 * hd
    vec = pl.BlockSpec((1, W), lambda b, h, i: (0, 0))
    qspec = pl.BlockSpec((tq, W), lambda b, h, i: (b * nq + i, h))
    kvout = pl.BlockSpec((S, W), lambda b, h, i: (b, h))
    colspec = pl.BlockSpec((1, 2, tq, 1), lambda b, h, i: (b, h, i, 0))
    rowspec = pl.BlockSpec((1, 2, 1, S), lambda b, h, i: (b, h, 0, 0))
    return pl.pallas_call(
        body, name="fox_bwd", grid=(B, FOX_PAIRS, nq),
        in_specs=[qspec,
                  pl.BlockSpec((S, W), lambda b, h, i: (b, FOX_PAIRS + h)),
                  pl.BlockSpec((S, W), lambda b, h, i: (b, 2 * FOX_PAIRS + h)),
                  colspec, rowspec, vec, vec, vec, qspec, qspec],
        out_specs=[qspec, kvout, kvout, colspec, rowspec, vec, vec, vec],
        out_shape=[jax.ShapeDtypeStruct((T, FOX_WIDTH), MXU_DTYPE), jax.ShapeDtypeStruct((T, FOX_WIDTH), MXU_DTYPE),
                   jax.ShapeDtypeStruct((T, FOX_WIDTH), MXU_DTYPE),
                   jax.ShapeDtypeStruct((B, FOX_HEADS, S, 1), f32), jax.ShapeDtypeStruct((B, FOX_HEADS, 1, S), f32),
                   jax.ShapeDtypeStruct((1, W), f32), jax.ShapeDtypeStruct((1, W), f32), jax.ShapeDtypeStruct((1, W), f32)],
        scratch_shapes=[pltpu.VMEM((2, S, hd), f32), pltpu.VMEM((2, S, hd), f32), pltpu.VMEM((2, 1, S), f32)],
        compiler_params=_cparams("arbitrary", "arbitrary", "arbitrary"),
    )(P, P, P, ccol, crow, gq, gk, go, o_raw, d_oab)


FOX_TQ = 512
FOX_TK = 512
GROUP_PRECISION = lax.Precision.HIGH


def _head_mean(v):
    n = v.shape[1]
    r = lax.broadcasted_iota(jnp.int32, (n, n), 0) // FOX_HEAD_DIM
    c = lax.broadcasted_iota(jnp.int32, (n, n), 1) // FOX_HEAD_DIM
    return _dot(v, (r == c).astype(f32), NN, GROUP_PRECISION) * (1.0 / FOX_HEAD_DIM)


def fox_prep_fwd(P, gq, gk, *, tr=512):
    T = P.shape[0]
    tr = min(tr, T)
    scale = FOX_HEAD_DIM ** -0.5

    def body(q_ref, k_ref, v_ref, gq_ref, gk_ref, qn_ref, kn_ref, vb_ref):
        q, k = q_ref[...], k_ref[...]
        qn_ref[...] = (q * lax.rsqrt(_head_mean(q * q) + EPS) * (gq_ref[...] * scale)).astype(qn_ref.dtype)
        kn_ref[...] = (k * lax.rsqrt(_head_mean(k * k) + EPS) * gk_ref[...]).astype(kn_ref.dtype)
        vb_ref[...] = v_ref[...].astype(vb_ref.dtype)

    W = FOX_WIDTH
    col = lambda j: pl.BlockSpec((tr, W), lambda i: (i, j))
    vec = pl.BlockSpec((1, W), lambda i: (0, 0))
    out = jax.ShapeDtypeStruct((T, W), MXU_DTYPE)
    return pl.pallas_call(
        body, name="fox_prep_fwd", grid=(T // tr,), in_specs=[col(0), col(1), col(2), vec, vec],
        out_specs=[col(0)] * 3, out_shape=[out] * 3, compiler_params=_cparams("parallel"),
    )(P, P, P, gq, gk)


def fox_prep_bwd(P, gq, gk, dqn, dkn, *, tr=512):
    T = P.shape[0]
    tr = min(tr, T)
    scale = FOX_HEAD_DIM ** -0.5

    def body(q_ref, k_ref, gq_ref, gk_ref, dqn_ref, dkn_ref, dq_ref, dk_ref, dgq_ref, dgk_ref):
        @pl.when(pl.program_id(0) == 0)
        def _():
            dgq_ref[...] = jnp.zeros_like(dgq_ref)
            dgk_ref[...] = jnp.zeros_like(dgk_ref)

        def one(x, g, dn, dx_ref, dg_ref):
            r = lax.rsqrt(_head_mean(x * x) + EPS)
            xhat = x * r
            gd = dn * g
            dx_ref[...] = (r * (gd - xhat * _head_mean(gd * xhat))).astype(dx_ref.dtype)
            return jnp.sum(dn * xhat, axis=0, keepdims=True)

        dgq_ref[...] += scale * one(q_ref[...], gq_ref[...] * scale, dqn_ref[...], dq_ref, dgq_ref)
        dgk_ref[...] += one(k_ref[...], gk_ref[...], dkn_ref[...], dk_ref, dgk_ref)

    W = FOX_WIDTH
    col = lambda j: pl.BlockSpec((tr, W), lambda i: (i, j))
    vec = pl.BlockSpec((1, W), lambda i: (0, 0))
    return pl.pallas_call(
        body, name="fox_prep_bwd", grid=(T // tr,), in_specs=[col(0), col(1), vec, vec, col(0), col(0)],
        out_specs=[col(0), col(0), vec, vec],
        out_shape=[jax.ShapeDtypeStruct((T, W), MXU_DTYPE), jax.ShapeDtypeStruct((T, W), MXU_DTYPE),
                   jax.ShapeDtypeStruct((1, W), f32), jax.ShapeDtypeStruct((1, W), f32)],
        compiler_params=_cparams("arbitrary"),
    )(P, P, gq, gk, dqn, dkn)


def _fox_tile_scores(q, k_ref, ccol_ref, cq, e, j, sl, mask_off):
    tq, tk = FOX_TQ, FOX_TK
    rows = pl.ds(pl.multiple_of(j * tk, tk), tk)
    k = k_ref[rows, sl]
    s = _dot(k, q, NT) + cq - ccol_ref[0, e, rows, :]
    if mask_off is not None:
        key = lax.broadcasted_iota(jnp.int32, (tk, tq), 0) + mask_off
        query = lax.broadcasted_iota(jnp.int32, (tk, tq), 1)
        s = jnp.where(key <= query, s, NEG_INF)
    return s, k, rows


def _fox_sweep(i, update, carry):
    nd = FOX_TQ // FOX_TK
    carry = lax.fori_loop(0, i * nd, lambda j, cr: update(cr, j, None), carry)
    for d in range(nd):
        carry = update(carry, i * nd + d, d * FOX_TK)
    return carry


def fox_core_fwd(qn, kn, vb, ccol, crow, go, *, B):
    T = qn.shape[0]
    S = T // B
    tq = FOX_TQ
    nq = S // tq
    hd = FOX_HEAD_DIM

    def body(q_ref, k_ref, v_ref, ccol_ref, crow_ref, go_ref, o_ref, oa_ref, lse_ref):
        i = pl.program_id(2)
        for e in range(2):
            sl = slice(e * hd, (e + 1) * hd)
            q = q_ref[:, sl]
            cq = crow_ref[0, e, i]

            def update(carry, j, mask_off):
                m, l, acc = carry
                s, _, rows = _fox_tile_scores(q, k_ref, ccol_ref, cq, e, j, sl, mask_off)
                m2 = jnp.maximum(m, jnp.max(s, axis=0, keepdims=True))
                a = jnp.exp(m - m2)
                p = jnp.exp(s - m2)
                return m2, a * l + jnp.sum(p, axis=0, keepdims=True), a * acc + _dot(v_ref[rows, sl], _mx(p), TN)

            carry = (jnp.full((1, tq), NEG_INF, f32), jnp.zeros((1, tq), f32), jnp.zeros((hd, tq), f32))
            m, l, acc = _fox_sweep(i, update, carry)
            o = (acc / l).T
            o_ref[:, sl] = o
            oa_ref[:, sl] = _head_rms(o, go_ref[:, sl])[0].astype(oa_ref.dtype)
            lse_ref[0, e, 0] = m + jnp.log(l)

    W = 2 * hd
    qspec = pl.BlockSpec((tq, W), lambda b, h, i: (b * nq + i, h))
    kspec = pl.BlockSpec((S, W), lambda b, h, i: (b, h))
    return pl.pallas_call(
        body, name="fox_core_fwd", grid=(B, FOX_PAIRS, nq),
        in_specs=[qspec, kspec, kspec, pl.BlockSpec((1, 2, S, 1), lambda b, h, i: (b, h, 0, 0)),
                  pl.BlockSpec((1, 2, nq, 1, tq), lambda b, h, i: (b, h, 0, 0, 0)),
                  pl.BlockSpec((1, W), lambda b, h, i: (0, 0))],
        out_specs=[qspec, qspec, pl.BlockSpec((1, 2, 1, 1, tq), lambda b, h, i: (b, h, i, 0, 0))],
        out_shape=[jax.ShapeDtypeStruct((T, FOX_WIDTH), f32), jax.ShapeDtypeStruct((T, FOX_WIDTH), MXU_DTYPE),
                   jax.ShapeDtypeStruct((B, FOX_HEADS, nq, 1, tq), f32)],
        compiler_params=_cparams("parallel", "parallel", "parallel"),
    )(qn, kn, vb, ccol, crow, go)


def fox_core_bwd(qn, kn, vb, ccol, crow, go, o_raw, lse, d_oab, *, B):
    T = qn.shape[0]
    S = T // B
    tq = FOX_TQ
    nq = S // tq
    hd = FOX_HEAD_DIM

    def body(q_ref, k_ref, v_ref, ccol_ref, crow_ref, go_ref, o_ref, lse_ref, doa_ref,
             dq_ref, dk_ref, dv_ref, dccol_ref, dcrow_ref, dgo_ref, dk_acc, dv_acc, dck_acc):
        b, h, i = pl.program_id(0), pl.program_id(1), pl.program_id(2)

        @pl.when((b == 0) & (h == 0) & (i == 0))
        def _():
            dgo_ref[...] = jnp.zeros_like(dgo_ref)

        @pl.when(i == 0)
        def _():
            dk_acc[...] = jnp.zeros_like(dk_acc)
            dv_acc[...] = jnp.zeros_like(dv_acc)
            dck_acc[...] = jnp.zeros_like(dck_acc)

        for e in range(2):
            sl = slice(e * hd, (e + 1) * hd)
            q = q_ref[:, sl]
            cq = crow_ref[0, e, i]
            lse_e = lse_ref[0, e, 0]
            o = o_ref[:, sl]
            ro = lax.rsqrt(jnp.mean(o * o, axis=-1, keepdims=True) + EPS)
            do, dgo = _head_rms_bwd(o, ro, go_ref[:, sl], doa_ref[:, sl])
            dgo_ref[:, sl] += dgo
            delta = jnp.sum((do * o).T, axis=0, keepdims=True)
            do_b = _mx(do)

            def update(carry, j, mask_off):
                dq, dcq = carry
                s, k, rows = _fox_tile_scores(q, k_ref, ccol_ref, cq, e, j, sl, mask_off)
                p = jnp.exp(s - lse_e)
                dv_acc[e, rows, :] += _dot(_mx(p), do_b, NN)
                ds = p * (_dot(v_ref[rows, sl], do_b, NT) - delta)
                dck_acc[e, rows, :] -= jnp.sum(ds, axis=1, keepdims=True)
                ds_b = _mx(ds)
                dk_acc[e, rows, :] += _dot(ds_b, q, NN)
                return dq + _dot(ds_b, k, TN), dcq + jnp.sum(ds, axis=0, keepdims=True)

            dq, dcq = _fox_sweep(i, update, (jnp.zeros((tq, hd), f32), jnp.zeros((1, tq), f32)))
            dq_ref[:, sl] = dq
            dcrow_ref[0, e, 0] = dcq

        @pl.when(i == nq - 1)
        def _():
            for e in range(2):
                sl = slice(e * hd, (e + 1) * hd)
                dk_ref[:, sl] = dk_acc[e]
                dv_ref[:, sl] = dv_acc[e].astype(dv_ref.dtype)
            dccol_ref[0] = dck_acc[...]

    W = 2 * hd
    qspec = pl.BlockSpec((tq, W), lambda b, h, i: (b * nq + i, h))
    kspec = pl.BlockSpec((S, W), lambda b, h, i: (b, h))
    colspec = pl.BlockSpec((1, 2, S, 1), lambda b, h, i: (b, h, 0, 0))
    rowspec = pl.BlockSpec((1, 2, nq, 1, tq), lambda b, h, i: (b, h, 0, 0, 0))
    tilespec = pl.BlockSpec((1, 2, 1, 1, tq), lambda b, h, i: (b, h, i, 0, 0))
    vec = pl.BlockSpec((1, W), lambda b, h, i: (0, 0))
    return pl.pallas_call(
        body, name="fox_core_bwd", grid=(B, FOX_PAIRS, nq),
        in_specs=[qspec, kspec, kspec, colspec, rowspec, vec, qspec, tilespec, qspec],
        out_specs=[qspec, kspec, kspec, colspec, tilespec, vec],
        out_shape=[jax.ShapeDtypeStruct((T, FOX_WIDTH), f32), jax.ShapeDtypeStruct((T, FOX_WIDTH), f32),
                   jax.ShapeDtypeStruct((T, FOX_WIDTH), MXU_DTYPE),
                   jax.ShapeDtypeStruct((B, FOX_HEADS, S, 1), f32), jax.ShapeDtypeStruct((B, FOX_HEADS, nq, 1, tq), f32),
                   jax.ShapeDtypeStruct((1, W), f32)],
        scratch_shapes=[pltpu.VMEM((2, S, hd), f32), pltpu.VMEM((2, S, hd), f32), pltpu.VMEM((2, S, 1), f32)],
        compiler_params=_cparams("arbitrary", "arbitrary", "arbitrary"),
    )(qn, kn, vb, ccol, crow, go, o_raw, lse, d_oab)


def _lane_mask(lo, hi, shape):
    lane = lax.broadcasted_iota(jnp.int32, shape, 1)
    return (lane >= lo) & (lane < hi)


def _cumsum_rows(v, period, reverse=False):
    n = v.shape[0]
    pos = lax.broadcasted_iota(jnp.int32, v.shape, 0) % period
    sh = 1
    while sh < period:
        if reverse:
            v = v + jnp.where(pos + sh < period, pltpu.roll(v, n - sh, 0), 0.0)
        else:
            v = v + jnp.where(pos >= sh, pltpu.roll(v, sh, 0), 0.0)
        sh *= 2
    return v


def _gate_values(z, bias, alog):
    zb = z + bias
    ls = jax.nn.log_sigmoid(zb)
    beta = jax.nn.sigmoid(z)
    g = -jnp.exp(alog) * jax.nn.softplus(zb)
    return zb, ls, beta, g


def gates_fwd(P, bias, alog, *, B):
    T = P.shape[0]
    S = T // B

    def body(z_ref, bias_ref, alog_ref, o_ref):
        z = z_ref[...]
        _, ls, beta, g = _gate_values(z, bias_ref[...], alog_ref[...])
        c = _cumsum_rows(ls, S)
        gc = _cumsum_rows(g, GDN_CHUNK)
        o = jnp.where(_lane_mask(SM_F, SM_F + FOX_HEADS, z.shape), c, 0.0)
        o = jnp.where(_lane_mask(SM_B, SM_B + GDN_HEADS, z.shape), beta, o)
        o = jnp.where(_lane_mask(SM_A, SM_A + GDN_HEADS, z.shape), gc, o)
        o_ref[...] = o

    vec = pl.BlockSpec((1, LANES), lambda b: (0, 0))
    return pl.pallas_call(
        body, name="gates_fwd", grid=(B,),
        in_specs=[pl.BlockSpec((S, LANES), lambda b: (b, COL_SMALL // LANES)), vec, vec],
        out_specs=pl.BlockSpec((S, LANES), lambda b: (b, 0)),
        out_shape=jax.ShapeDtypeStruct((T, LANES), f32),
        compiler_params=_cparams("parallel"),
    )(P, bias, alog)


def gates_bwd(P, bias, alog, dgates, *, B):
    T = P.shape[0]
    S = T // B

    def body(z_ref, bias_ref, alog_ref, dg_ref, dz_ref, par_ref):
        z = z_ref[...]
        zb, ls, beta, g = _gate_values(z, bias_ref[...], alog_ref[...])
        d = dg_ref[...]
        dls = _cumsum_rows(d, S, reverse=True)
        dgr = _cumsum_rows(d, GDN_CHUNK, reverse=True)
        sig = jax.nn.sigmoid(zb)
        dz_f = dls * (1.0 - sig)
        dz_b = d * beta * (1.0 - beta)
        dz_a = dgr * (-jnp.exp(alog_ref[...])) * sig
        dz = jnp.where(_lane_mask(SM_F, SM_F + FOX_HEADS, z.shape), dz_f, 0.0)
        dz = jnp.where(_lane_mask(SM_B, SM_B + GDN_HEADS, z.shape), dz_b, dz)
        dz = jnp.where(_lane_mask(SM_A, SM_A + GDN_HEADS, z.shape), dz_a, dz)
        dz_ref[...] = dz.astype(dz_ref.dtype)

        @pl.when(pl.program_id(0) == 0)
        def _():
            par_ref[...] = jnp.zeros_like(par_ref)

        dalog = jnp.where(_lane_mask(SM_A, SM_A + GDN_HEADS, z.shape), dgr * g, 0.0)
        par_ref[0:1, :] += jnp.sum(dz, axis=0, keepdims=True)
        par_ref[1:2, :] += jnp.sum(dalog, axis=0, keepdims=True)

    vec = pl.BlockSpec((1, LANES), lambda b: (0, 0))
    return pl.pallas_call(
        body, name="gates_bwd", grid=(B,),
        in_specs=[pl.BlockSpec((S, LANES), lambda b: (b, COL_SMALL // LANES)), vec, vec,
                  pl.BlockSpec((S, LANES), lambda b: (b, 0))],
        out_specs=[pl.BlockSpec((S, LANES), lambda b: (b, 0)), pl.BlockSpec((8, LANES), lambda b: (0, 0))],
        out_shape=[jax.ShapeDtypeStruct((T, LANES), MXU_DTYPE), jax.ShapeDtypeStruct((8, LANES), f32)],
        compiler_params=_cparams("arbitrary"),
    )(P, bias, alog, dgates)


GDN_BLOCKS = 3 * GDN_HEADS


def _shift_rows(v, d, reverse=False):
    if d == 0:
        return v
    n = v.shape[0]
    row = lax.broadcasted_iota(jnp.int32, v.shape, 0)
    if reverse:
        return jnp.where(row + d < n, pltpu.roll(v, n - d, 0), 0.0)
    return jnp.where(row >= d, pltpu.roll(v, d, 0), 0.0)


def _conv_silu(x, w):
    pre = sum(w[j:j + 1, :] * _shift_rows(x, CONV_WIDTH - 1 - j) for j in range(CONV_WIDTH))
    return pre, pre * jax.nn.sigmoid(pre)


def gdn_prep_fwd(P, conv_w, *, B):
    T = P.shape[0]
    S = T // B

    def body(x_ref, w_ref, o_ref):
        _, y = _conv_silu(x_ref[...], w_ref[...])
        yn = y * lax.rsqrt(jnp.sum(y * y, axis=-1, keepdims=True) + EPS)
        o_ref[...] = jnp.where(pl.program_id(1) < 2 * GDN_HEADS, yn, y)

    return pl.pallas_call(
        body, name="gdn_prep_fwd", grid=(B, GDN_BLOCKS),
        in_specs=[pl.BlockSpec((S, LANES), lambda b, j: (b, COL_GDN // LANES + j)),
                  pl.BlockSpec((CONV_WIDTH, LANES), lambda b, j: (0, j))],
        out_specs=pl.BlockSpec((S, LANES), lambda b, j: (b, j)),
        out_shape=jax.ShapeDtypeStruct((T, 3 * GDN_WIDTH), f32),
        compiler_params=_cparams("parallel", "parallel"),
    )(P, conv_w)


def gdn_prep_bwd(P, conv_w, dG, *, B):
    T = P.shape[0]
    S = T // B

    def body(x_ref, w_ref, dg_ref, dx_ref, dw_ref):
        x, w = x_ref[...], w_ref[...]
        pre, y = _conv_silu(x, w)
        dn = dg_ref[...]
        r = lax.rsqrt(jnp.sum(y * y, axis=-1, keepdims=True) + EPS)
        n = y * r
        dy_norm = r * (dn - n * jnp.sum(dn * n, axis=-1, keepdims=True))
        dy = jnp.where(pl.program_id(0) < 2 * GDN_HEADS, dy_norm, dn)
        sg = jax.nn.sigmoid(pre)
        dpre = dy * (sg * (1.0 + pre * (1.0 - sg)))
        dx = sum(w[j:j + 1, :] * _shift_rows(dpre, CONV_WIDTH - 1 - j, reverse=True) for j in range(CONV_WIDTH))
        dx_ref[...] = dx.astype(dx_ref.dtype)

        @pl.when(pl.program_id(1) == 0)
        def _():
            dw_ref[...] = jnp.zeros_like(dw_ref)

        for j in range(CONV_WIDTH):
            dw_ref[j:j + 1, :] += jnp.sum(dpre * _shift_rows(x, CONV_WIDTH - 1 - j), axis=0, keepdims=True)

    return pl.pallas_call(
        body, name="gdn_prep_bwd", grid=(GDN_BLOCKS, B),
        in_specs=[pl.BlockSpec((S, LANES), lambda j, b: (b, COL_GDN // LANES + j)),
                  pl.BlockSpec((CONV_WIDTH, LANES), lambda j, b: (0, j)),
                  pl.BlockSpec((S, LANES), lambda j, b: (b, j))],
        out_specs=[pl.BlockSpec((S, LANES), lambda j, b: (b, j)),
                   pl.BlockSpec((CONV_WIDTH, LANES), lambda j, b: (0, j))],
        out_shape=[jax.ShapeDtypeStruct((T, 3 * GDN_WIDTH), MXU_DTYPE),
                   jax.ShapeDtypeStruct((CONV_WIDTH, 3 * GDN_WIDTH), f32)],
        compiler_params=_cparams("arbitrary", "arbitrary"),
    )(P, conv_w, dG)


GDN_GROUP = 16
B_NN = (((2,), (1,)), ((0,), (0,)))
B_NT = (((2,), (2,)), ((0,), (0,)))
B_TN = (((1,), (1,)), ((0,), (0,)))


def _bmm(a, b, dims, precision=None):
    if precision is None:
        a, b = _mx(a), _mx(b)
    return lax.dot_general(a, b, dims, preferred_element_type=f32, precision=precision)


def _tri_inverse(A):
    C = A.shape[-1]
    row = lax.broadcasted_iota(jnp.int32, A.shape, 1)
    col = lax.broadcasted_iota(jnp.int32, A.shape, 2)
    eye = (row == col).astype(f32)
    X = jnp.where((row // 4) == (col // 4), -A, 0.0)
    X2 = _bmm(X, X, B_NN, INV_PRECISION)
    Tm = eye + X + X2 + _bmm(X, X2, B_NN, INV_PRECISION)
    b = 4
    while b < C:
        off = ((row // (2 * b)) == (col // (2 * b))) & ((row // b) != (col // b))
        Tm = Tm - _bmm(_bmm(Tm, jnp.where(off, A, 0.0), B_NN, INV_PRECISION), Tm, B_NN, INV_PRECISION)
        b *= 2
    return Tm


def _pick_lane(block, lane_idx):
    lane = lax.broadcasted_iota(jnp.int32, block.shape, 1)
    return jnp.sum(jnp.where(lane == lane_idx, block, 0.0), axis=1, keepdims=True)


def _gdn_local(q, k, v, beta, gc, Tm=None):
    C = GDN_CHUNK
    n = q.shape[0] // C
    q = q.reshape(n, C, -1) * (GDN_HEAD_DIM ** -0.5)
    k = k.reshape(n, C, -1)
    v = v.reshape(n, C, -1)
    beta = beta.reshape(n, C, 1)
    gc = gc.reshape(n, C, 1)
    row = lax.broadcasted_iota(jnp.int32, (n, C, C), 1)
    col = lax.broadcasted_iota(jnp.int32, (n, C, C), 2)
    gcT = jnp.swapaxes(jnp.broadcast_to(gc, (n, C, C)), 1, 2)
    D = jnp.exp(jnp.where(row >= col, gc - gcT, NEG_INF))
    kb = k * beta
    vb = v * beta
    A = jnp.where(row > col, _bmm(kb, k, B_NT) * D, 0.0)
    Gam = jnp.exp(gc)
    kg = kb * Gam
    gl = gc[:, C - 1:C, :]
    kdec = jnp.exp(gl - gc)
    loc = dict(q=q, k=k, v=v, beta=beta, gc=gc, D=D, kb=kb, vb=vb, A=A, Gam=Gam, kg=kg,
               kdec=kdec, kd=k * kdec, qg=q * Gam, gam=jnp.exp(gl), row=row, col=col)
    if Tm is None:
        Tm = _tri_inverse(A)
        loc.update(u=_bmm(Tm, vb, B_NN), w=_bmm(Tm, kg, B_NN), M=_bmm(q, k, B_NT) * D)
    else:
        Tm = Tm.reshape(n, C, C)
    loc["Tm"] = Tm
    return loc


def _gdn_store_local(loc, r0, u_s, w_s, qg_s, kd_s, M_s, gam_s, c0):
    n = loc["u"].shape[0]
    R = n * GDN_CHUNK
    u_s[pl.ds(r0, R), :] = loc["u"].reshape(R, -1)
    w_s[pl.ds(r0, R), :] = loc["w"].reshape(R, -1)
    qg_s[pl.ds(r0, R), :] = loc["qg"].reshape(R, -1)
    kd_s[pl.ds(r0, R), :] = loc["kd"].reshape(R, -1)
    M_s[pl.ds(r0, R), :] = loc["M"].reshape(R, -1)
    gam_s[pl.ds(c0, n)] = jnp.broadcast_to(loc["gam"], (n, 1, LANES))


def _gdn_specs(S):
    blk = lambda off: pl.BlockSpec((S, LANES), lambda b, h: (b, off + h))
    return blk


def gdn_fwd(G, gates, P, g_on, *, B):
    T = G.shape[0]
    S = T // B
    C = GDN_CHUNK
    N = S // C
    grp = min(GDN_GROUP, N)
    R = grp * C
    hd = GDN_HEAD_DIM

    def body(q_ref, k_ref, v_ref, gt_ref, z_ref, gon_ref, o_ref, ob_ref, st_ref, u_s, w_s, qg_s, kd_s, M_s, gam_s):
        h = pl.program_id(1)

        def local(gi, carry):
            r0 = pl.multiple_of(gi * R, R)
            gt = gt_ref[pl.ds(r0, R), :]
            loc = _gdn_local(q_ref[pl.ds(r0, R), :], k_ref[pl.ds(r0, R), :], v_ref[pl.ds(r0, R), :],
                             _pick_lane(gt, SM_B + h), _pick_lane(gt, SM_A + h))
            _gdn_store_local(loc, r0, u_s, w_s, qg_s, kd_s, M_s, gam_s, gi * grp)
            return carry

        lax.fori_loop(0, N // grp, local, 0)

        def step(n, state):
            r0 = pl.multiple_of(n * C, C)
            st_ref[0, 0, n] = state
            v_new = u_s[pl.ds(r0, C), :] - _dotm(w_s[pl.ds(r0, C), :], state, NN)
            o_ref[pl.ds(r0, C), :] = (_dotm(qg_s[pl.ds(r0, C), :], state, NN)
                                      + _dotm(M_s[pl.ds(r0, C), :], v_new, NN))
            return state * gam_s[n] + _dotm(kd_s[pl.ds(r0, C), :], v_new, TN)

        lax.fori_loop(0, N, step, jnp.zeros((hd, hd), f32))
        o = o_ref[...]
        z = z_ref[...]
        ob_ref[...] = (_head_rms(o, gon_ref[...])[0] * (z * jax.nn.sigmoid(z))).astype(ob_ref.dtype)

    blk = lambda off: pl.BlockSpec((S, LANES), lambda b, h: (b, off + h))
    rows = lambda: pltpu.VMEM((S, hd), f32)
    return pl.pallas_call(
        body, name="gdn_fwd", grid=(B, GDN_HEADS),
        in_specs=[blk(0), blk(GDN_HEADS), blk(2 * GDN_HEADS), pl.BlockSpec((S, LANES), lambda b, h: (b, 0)),
                  blk(COL_Z // LANES), pl.BlockSpec((1, hd), lambda b, h: (0, 0))],
        out_specs=[blk(0), blk(0), pl.BlockSpec((1, 1, N, hd, hd), lambda b, h: (b, h, 0, 0, 0))],
        out_shape=[jax.ShapeDtypeStruct((T, GDN_WIDTH), f32), jax.ShapeDtypeStruct((T, GDN_WIDTH), MXU_DTYPE),
                   jax.ShapeDtypeStruct((B, GDN_HEADS, N, hd, hd), f32)],
        scratch_shapes=[rows(), rows(), rows(), rows(), pltpu.VMEM((S, C), f32), pltpu.VMEM((N, 1, LANES), f32)],
        compiler_params=_cparams("parallel", "parallel"),
    )(G, G, G, gates, P, g_on)


def gdn_bwd(G, gates, P, g_on, o_raw, states, d_oab, *, B):
    T = G.shape[0]
    S = T // B
    C = GDN_CHUNK
    N = S // C
    grp = min(GDN_GROUP, N)
    R = grp * C
    hd = GDN_HEAD_DIM

    def body(q_ref, k_ref, v_ref, gt_ref, z_ref, gon_ref, o_ref, st_ref, dob_ref,
             dq_ref, dk_ref, dv_ref, dgt_ref, dz_ref, dgon_ref,
             u_s, w_s, qg_s, kd_s, M_s, gam_s, do_s, du_s, dw_s, dqg_s, dkd_s, dM_s, dgl_s, Tm_s):
        b, h = pl.program_id(0), pl.program_id(1)

        @pl.when((b == 0) & (h == 0))
        def _():
            dgon_ref[...] = jnp.zeros_like(dgon_ref)

        @pl.when(h == 0)
        def _():
            dgt_ref[...] = jnp.zeros_like(dgt_ref)

        def group_inputs(gi, Tm_of=None):
            r0 = pl.multiple_of(gi * R, R)
            gt = gt_ref[pl.ds(r0, R), :]
            Tm = None if Tm_of is None else Tm_of[pl.ds(r0, R), :]
            return r0, _gdn_local(q_ref[pl.ds(r0, R), :], k_ref[pl.ds(r0, R), :], v_ref[pl.ds(r0, R), :],
                                  _pick_lane(gt, SM_B + h), _pick_lane(gt, SM_A + h), Tm)

        def local(gi, carry):
            r0, loc = group_inputs(gi)
            _gdn_store_local(loc, r0, u_s, w_s, qg_s, kd_s, M_s, gam_s, gi * grp)
            Tm_s[pl.ds(r0, R), :] = loc["Tm"].reshape(R, C)
            o, z, gon = o_ref[pl.ds(r0, R), :], z_ref[pl.ds(r0, R), :], gon_ref[...]
            dob = dob_ref[pl.ds(r0, R), :]
            on, ro = _head_rms(o, gon)
            sz = jax.nn.sigmoid(z)
            dz_ref[pl.ds(r0, R), :] = (dob * on * (sz * (1.0 + z * (1.0 - sz)))).astype(dz_ref.dtype)
            do, dgon = _head_rms_bwd(o, ro, gon, dob * (z * sz))
            do_s[pl.ds(r0, R), :] = do
            dgon_ref[...] += dgon
            return carry

        lax.fori_loop(0, N // grp, local, 0)

        def step(t, dS):
            n = N - 1 - t
            r0 = pl.multiple_of(n * C, C)
            rows = pl.ds(r0, C)
            state = st_ref[0, 0, n]
            w_n, M_n, kd_n, do_n = w_s[rows, :], M_s[rows, :], kd_s[rows, :], do_s[rows, :]
            v_new = u_s[rows, :] - _dotm(w_n, state, NN)
            dv_new = _dotm(M_n, do_n, TN) + _dotm(kd_n, dS, NN)
            du_s[rows, :] = dv_new
            dw_s[rows, :] = -_dotm(dv_new, state, NT)
            dqg_s[rows, :] = _dotm(do_n, state, NT)
            dM_s[rows, :] = _dotm(do_n, v_new, NT)
            dkd_s[rows, :] = _dotm(v_new, dS, NT)
            gam = gam_s[n]
            dgl_s[n] = jnp.broadcast_to(jnp.sum(jnp.sum(dS * state, axis=1, keepdims=True), axis=0, keepdims=True), (1, LANES)) * gam
            return dS * gam + _dotm(qg_s[rows, :], do_n, TN) - _dotm(w_n, dv_new, TN)

        lax.fori_loop(0, N, step, jnp.zeros((hd, hd), f32))

        def finish(gi, carry):
            r0, L = group_inputs(gi, Tm_s)
            n = grp
            rows = pl.ds(r0, R)
            g3 = lambda ref: ref[rows, :].reshape(n, C, -1)
            du, dw, dqg, dkd, dM = g3(du_s), g3(dw_s), g3(dqg_s), g3(dkd_s), g3(dM_s)
            L["M"] = g3(M_s)
            TmT = jnp.swapaxes(L["Tm"], 1, 2)
            dTm = _bmm(du, L["vb"], B_NT) + _bmm(dw, L["kg"], B_NT)
            dvb = _bmm(TmT, du, B_NN)
            dkg = _bmm(TmT, dw, B_NN)
            dA = jnp.where(L["row"] > L["col"], -_bmm(_bmm(TmT, dTm, B_NN), TmT, B_NN), 0.0)
            dKK = dA * L["D"]
            dQK = dM * L["D"]
            dkb = _bmm(dKK, L["k"], B_NN) + dkg * L["Gam"]
            dk = (_bmm(dKK, L["kb"], B_TN) + _bmm(dQK, L["q"], B_TN) + dkd * L["kdec"] + L["beta"] * dkb)
            dq = (_bmm(dQK, L["k"], B_NN) + dqg * L["Gam"]) * (GDN_HEAD_DIM ** -0.5)
            E = dA * L["A"] + dM * L["M"]
            r = jnp.sum(dkd * L["kd"], axis=-1, keepdims=True)
            dgc = (jnp.sum(E, axis=2, keepdims=True) - jnp.sum(jnp.swapaxes(E, 1, 2), axis=2, keepdims=True)
                   + jnp.sum(dkg * L["kg"], axis=-1, keepdims=True) + jnp.sum(dqg * L["qg"], axis=-1, keepdims=True) - r)
            dgl = jnp.sum(r, axis=1, keepdims=True) + dgl_s[pl.ds(gi * n, n)][:, :, 0:1]
            rowc = lax.broadcasted_iota(jnp.int32, (n, C, 1), 1)
            dgc = dgc + jnp.where(rowc == C - 1, dgl, 0.0)
            dbeta = jnp.sum(dkb * L["k"], axis=-1, keepdims=True) + jnp.sum(dvb * L["v"], axis=-1, keepdims=True)
            dq_ref[rows, :] = dq.reshape(R, hd)
            dk_ref[rows, :] = dk.reshape(R, hd)
            dv_ref[rows, :] = (L["beta"] * dvb).reshape(R, hd)
            lane = lax.broadcasted_iota(jnp.int32, (R, LANES), 1)
            dgt_ref[rows, :] += (jnp.where(lane == SM_B + h, dbeta.reshape(R, 1), 0.0)
                                 + jnp.where(lane == SM_A + h, dgc.reshape(R, 1), 0.0))
            return carry

        lax.fori_loop(0, N // grp, finish, 0)

    blk = lambda off: pl.BlockSpec((S, LANES), lambda b, h: (b, off + h))
    rows = lambda: pltpu.VMEM((S, hd), f32)
    return pl.pallas_call(
        body, name="gdn_bwd", grid=(B, GDN_HEADS),
        in_specs=[blk(0), blk(GDN_HEADS), blk(2 * GDN_HEADS), pl.BlockSpec((S, LANES), lambda b, h: (b, 0)),
                  blk(COL_Z // LANES), pl.BlockSpec((1, hd), lambda b, h: (0, 0)), blk(0),
                  pl.BlockSpec((1, 1, N, hd, hd), lambda b, h: (b, h, 0, 0, 0)), blk(GDN_HEADS)],
        out_specs=[blk(0), blk(0), blk(0), pl.BlockSpec((S, LANES), lambda b, h: (b, 0)), blk(0),
                   pl.BlockSpec((1, hd), lambda b, h: (0, 0))],
        out_shape=[jax.ShapeDtypeStruct((T, GDN_WIDTH), f32), jax.ShapeDtypeStruct((T, GDN_WIDTH), f32),
                   jax.ShapeDtypeStruct((T, GDN_WIDTH), f32), jax.ShapeDtypeStruct((T, LANES), f32),
                   jax.ShapeDtypeStruct((T, GDN_WIDTH), MXU_DTYPE), jax.ShapeDtypeStruct((1, hd), f32)],
        scratch_shapes=[rows(), rows(), rows(), rows(), pltpu.VMEM((S, C), f32), pltpu.VMEM((N, 1, LANES), f32),
                        rows(), rows(), rows(), rows(), rows(), pltpu.VMEM((S, C), f32), pltpu.VMEM((N, 1, LANES), f32),
                        pltpu.VMEM((S, C), f32)],
        compiler_params=_cparams("arbitrary", "arbitrary"),
    )(G, G, G, gates, P, g_on, o_raw, states, d_oab)


IN_SPLIT = (0, 1536, 1544, 3080, 3088, 3600)


IN_SHARD = IN_DIM // 4
IN_SHARD_PAD = 928


def align_w_in_t(wt):
    s = IN_SPLIT
    pad = jnp.zeros((IN_ALIGNED - IN_DIM, wt.shape[1]), wt.dtype)
    return jnp.concatenate([wt[s[0]:s[1]], wt[s[2]:s[3]], wt[s[4]:s[5]], wt[s[1]:s[2]], wt[s[3]:s[4]], pad], axis=0)


def unalign_w_in_t(wa):
    return jnp.concatenate([wa[0:1536], wa[COL_SMALL:COL_SMALL + 8], wa[1536:3072],
                            wa[COL_SMALL + 8:COL_SMALL + 16], wa[3072:3584]], axis=0)


def _lanes_vec(pieces):
    v = jnp.zeros((1, LANES), f32)
    for off, a in pieces:
        v = lax.dynamic_update_slice(v, a.astype(f32), (0, off))
    return v


def local_step(x, mem, target, w, sp, *, B):
    T = x.shape[0]
    S = T // B
    gq8, gk8 = jnp.tile(sp["fox_qnorm_g"], (1, FOX_HEADS)), jnp.tile(sp["fox_knorm_g"], (1, FOX_HEADS))
    go2 = jnp.tile(sp["fox_onorm_g"], (1, 2))
    bias = _lanes_vec([(SM_F, sp["fox_f_bias"]), (SM_A, sp["gdn_dt_bias"])])
    alog = _lanes_vec([(SM_A, sp["gdn_A_log"])])

    h1 = rms_fwd(x, sp["norm_mix_g"], name="rms_mix")
    P = matmul(h1, w["wa_t"], tb=True, name="mm_in", tn=IN_TILE)
    gates = gates_fwd(P, bias, alog, B=B)
    c = gates[:, SM_F:SM_F + FOX_HEADS].reshape(B, S, FOX_HEADS).transpose(0, 2, 1)
    ccol, crow = c[..., None], c.reshape(B, FOX_HEADS, S // FOX_TQ, 1, FOX_TQ)
    qn, kn, vb = fox_prep_fwd(P, gq8, gk8)
    o_raw, o_a, lse = fox_core_fwd(qn, kn, vb, ccol, crow, go2, B=B)
    G = gdn_prep_fwd(P, w["conv_w"], B=B)
    ob_raw, o_b, states = gdn_fwd(G, gates, P, sp["gdn_onorm_g"], B=B)
    oab = jnp.concatenate([o_a, o_b], axis=1)
    if "late" in w:
        w = {**w, **w["late"](oab)}
    x2 = matmul(oab, w["w_out"], residual=x, name="mm_out")
    hq = rms_fwd(x2, sp["norm_xattn_g"], name="rms_xattn")
    hm = rms_fwd(mem, sp["mem_norm_g"], name="rms_mem")
    cq = matmul(hq, w["w_cq"], name="mm_cq")
    ckv = matmul(hm, w["w_ckv"], name="mm_ckv")
    co = xattn_fwd(cq, ckv, sp["xattn_qnorm_g"], sp["xattn_knorm_g"], B=B)
    x3 = matmul(co, w["w_co"], b_stacked=True, residual=x2, name="mm_co")
    hf = rms_fwd(x3, sp["norm_mlp_g"], name="rms_mlp")
    act = matmul(hf, w["w_mlp1"], b_stacked=True, relu2_out=True, out_dtype=MXU_DTYPE, name="mm_mlp1")
    dy, loss = matmul_rows(act, w["w_mlp2"], (x3, target), mode="loss", name="mm_mlp2_loss")

    da = matmul(dy, w["w_mlp2"], tb=True, relu2_bwd_aux=act, out_dtype=MXU_DTYPE, name="mm_d_act")
    g_mlp2 = matmul(act, dy, ta=True, out_dtype=WIRE_DTYPE, name="mm_g_mlp2")
    g_mlp1 = matmul(hf, da, ta=True, out_stacked=True, out_dtype=WIRE_DTYPE, name="mm_g_mlp1")
    by_rows = lambda g: g.reshape(N_CHIPS, g.shape[0] // N_CHIPS, g.shape[1])
    early = w.get("grads_ready", lambda grads: jnp.zeros((1, 1), f32))
    tok = early(dict(w_mlp1=g_mlp1, w_mlp2=by_rows(g_mlp2)))[0, 0]
    dx3, g_norm_mlp = matmul_rows(da, w["w_mlp1"], (x3, sp["norm_mlp_g"] + tok, dy), mode="rms_bwd", tb=True,
                                  b_stacked=True, name="mm_d_hf_rms")
    dco = matmul(dx3, w["w_co"], tb=True, b_stacked=True, name="mm_d_co")
    g_co = matmul(co, dx3, ta=True, out_stacked=True, out_dtype=WIRE_DTYPE, name="mm_g_co")
    dcq, dckv, g_xq, g_xk = xattn_bwd(cq, ckv, sp["xattn_qnorm_g"], sp["xattn_knorm_g"], dco, B=B)
    g_cq = matmul(hq, dcq, ta=True, out_dtype=WIRE_DTYPE, name="mm_g_cq")
    g_ckv = matmul(hm, dckv, ta=True, out_dtype=WIRE_DTYPE, name="mm_g_ckv")
    _, g_mem_norm = matmul_rows(dckv, w["w_ckv"], (mem, sp["mem_norm_g"], None), mode="rms_bwd", tb=True, name="mm_d_hm_rms")
    dx2, g_norm_xattn = matmul_rows(dcq, w["w_cq"], (x2, sp["norm_xattn_g"], dx3), mode="rms_bwd", tb=True, name="mm_d_hq_rms")
    doab = matmul(dx2, w["w_out"], tb=True, name="mm_d_oab")
    g_out = matmul(oab, dx2, ta=True, out_dtype=WIRE_DTYPE, name="mm_g_out")
    tok = early(dict(w_co=g_co, w_cq=by_rows(g_cq), w_ckv=by_rows(g_ckv), w_out=by_rows(g_out)))[0, 0]
    dqn, dkn, dv_f, dccol, dcrow, dgo2 = fox_core_bwd(qn, kn, vb, ccol, crow, go2 + tok, o_raw, lse, doab, B=B)
    dq_f, dk_f, dgq8, dgk8 = fox_prep_bwd(P, gq8, gk8, dqn, dkn)
    dGq, dGk, dGv, dgt, dz, g_gdn_on = gdn_bwd(G, gates, P, sp["gdn_onorm_g"], ob_raw, states, doab, B=B)
    dPg, g_conv = gdn_prep_bwd(P, w["conv_w"], jnp.concatenate([dGq, dGk, dGv], axis=1), B=B)
    dc = (dccol[..., 0] + dcrow.reshape(B, FOX_HEADS, S)).transpose(0, 2, 1).reshape(T, FOX_HEADS)
    dgates = dgt + jnp.pad(dc, ((0, 0), (SM_F, LANES - SM_F - FOX_HEADS)))
    dsmall, par = gates_bwd(P, bias, alog, dgates, B=B)
    dP = jnp.concatenate([dq_f, dk_f, dv_f, dPg, dz, dsmall, jnp.zeros((T, IN_ALIGNED - COL_SMALL - LANES), MXU_DTYPE)], axis=1)
    g_wa = matmul(dP, h1, ta=True, out_dtype=WIRE_DTYPE, name="mm_g_in", tm=IN_TILE)
    dx, g_norm_mix = matmul_rows(dP, w["wa_t"], (x, sp["norm_mix_g"], dx2), mode="rms_bwd", tk=IN_TILE, name="mm_d_h1_rms")

    fold = lambda g: jnp.sum(g.reshape(-1, FOX_HEAD_DIM), axis=0, keepdims=True)
    g_in = jnp.pad(unalign_w_in_t(g_wa).reshape(N_CHIPS, IN_SHARD, D_MODEL), ((0, 0), (0, IN_SHARD_PAD - IN_SHARD), (0, 0)))
    big = dict(w_in=g_in, w_out=by_rows(g_out), w_cq=by_rows(g_cq), w_ckv=by_rows(g_ckv), w_co=g_co, w_mlp1=g_mlp1,
               w_mlp2=by_rows(g_mlp2))
    small = dict(norm_mix_g=g_norm_mix, fox_qnorm_g=fold(dgq8), fox_knorm_g=fold(dgk8),
                 fox_f_bias=par[0:1, SM_F:SM_F + FOX_HEADS], fox_onorm_g=fold(dgo2), gdn_conv_w=g_conv,
                 gdn_A_log=par[1:2, SM_A:SM_A + GDN_HEADS], gdn_dt_bias=par[0:1, SM_A:SM_A + GDN_HEADS],
                 gdn_onorm_g=g_gdn_on, norm_xattn_g=g_norm_xattn, mem_norm_g=g_mem_norm,
                 xattn_qnorm_g=g_xq, xattn_knorm_g=g_xk, norm_mlp_g=g_norm_mlp)
    return loss, dx, big, small


MESH_IDS = pl.DeviceIdType.MESH
N_CHIPS = 4
HBM_SPEC = pl.BlockSpec(memory_space=pltpu.HBM)
PACK_ROWS = 30720
PACK_HALF = PACK_ROWS // 2
PACK_BLOCK = 3072


def _place():
    return lax.axis_index("x"), lax.axis_index("y"), lax.axis_index("c")


def _other_chips(x, y):
    return [(1 - x, y), (x, 1 - y), (1 - x, 1 - y)]


def _remote(src, dst, send_sem, recv_sem, to):
    return pltpu.make_async_remote_copy(src_ref=src, dst_ref=dst, send_sem=send_sem, recv_sem=recv_sem,
                                        device_id=to, device_id_type=MESH_IDS)


def all_gather_shards(packed):
    half = PACK_HALF

    def body(src_ref, out_ref, send_sems, recv_sems):
        x, y, c = _place()
        me_chip = 2 * x + y
        sibling = (x, y, 1 - c)
        chips = _other_chips(x, y)

        def rows(chip, core):
            return out_ref.at[chip, pl.ds(core * half, half), :]

        sends = [_remote(src_ref.at[pl.ds(c * half, half), :], rows(me_chip, c), send_sems.at[j], recv_sems.at[j], (px, py, c))
                 for j, (px, py) in enumerate(chips)]
        for cp in sends:
            cp.start()
        passed = []
        for j, (px, py) in enumerate(chips):
            theirs = rows(2 * px + py, c)
            _remote(theirs, theirs, send_sems.at[j], recv_sems.at[j], (px, py, c)).wait_recv()
            cp = _remote(theirs, theirs, send_sems.at[3 + j], recv_sems.at[3 + j], sibling)
            cp.start()
            passed.append(cp)
        for j, (px, py) in enumerate(chips):
            theirs = rows(2 * px + py, 1 - c)
            _remote(theirs, theirs, send_sems.at[3 + j], recv_sems.at[3 + j], sibling).wait_recv()
        for cp in sends + passed:
            cp.wait_send()

    return pl.pallas_call(
        body, name="all_gather_shards", in_specs=[HBM_SPEC], out_specs=HBM_SPEC,
        out_shape=jax.ShapeDtypeStruct((N_CHIPS,) + packed.shape, packed.dtype),
        scratch_shapes=[pltpu.SemaphoreType.DMA((6,)), pltpu.SemaphoreType.DMA((6,))],
    )(packed)


def exchange_core_halves(G):
    half = PACK_HALF

    def body(g_ref, land_ref, send_sem, recv_sem):
        x, y, c = _place()
        cp = _remote(g_ref.at[:, pl.ds((1 - c) * half, half), :], land_ref, send_sem, recv_sem, (x, y, 1 - c))
        cp.start()
        cp.wait()

    return pl.pallas_call(
        body, name="exchange_core_halves", in_specs=[HBM_SPEC], out_specs=HBM_SPEC,
        out_shape=jax.ShapeDtypeStruct((N_CHIPS, half, LANES), G.dtype),
        scratch_shapes=[pltpu.SemaphoreType.DMA(()), pltpu.SemaphoreType.DMA(())],
    )(G)


def add_core_halves(G, land, core):
    nb = PACK_HALF // PACK_BLOCK

    def body(c_ref, g_ref, l_ref, o_ref):
        o_ref[...] = (g_ref[...].astype(f32) + l_ref[...].astype(f32)).astype(o_ref.dtype)

    blk = (1, PACK_BLOCK, LANES)
    return pl.pallas_call(
        body, name="add_core_halves",
        grid_spec=pltpu.PrefetchScalarGridSpec(
            num_scalar_prefetch=1, grid=(N_CHIPS, nb),
            in_specs=[pl.BlockSpec(blk, lambda k, i, c_ref: (k, c_ref[0] * nb + i, 0)),
                      pl.BlockSpec(blk, lambda k, i, c_ref: (k, i, 0))],
            out_specs=pl.BlockSpec(blk, lambda k, i, c_ref: (k, i, 0))),
        out_shape=jax.ShapeDtypeStruct(land.shape, land.dtype),
        compiler_params=_cparams("parallel", "parallel"),
    )(core, G, land)


def scatter_to_chips(part):
    def body(p_ref, land_ref, send_sems, recv_sems):
        x, y, c = _place()
        me_chip = 2 * x + y
        chips = _other_chips(x, y)
        sends = [_remote(p_ref.at[2 * px + py], land_ref.at[me_chip], send_sems.at[j], recv_sems.at[j], (px, py, c))
                 for j, (px, py) in enumerate(chips)]
        for cp in sends:
            cp.start()
        for j, (px, py) in enumerate(chips):
            slot = land_ref.at[2 * px + py]
            _remote(slot, slot, send_sems.at[j], recv_sems.at[j], (px, py, c)).wait_recv()
        for cp in sends:
            cp.wait_send()

    return pl.pallas_call(
        body, name="scatter_to_chips", in_specs=[HBM_SPEC], out_specs=HBM_SPEC,
        out_shape=jax.ShapeDtypeStruct(part.shape, part.dtype),
        scratch_shapes=[pltpu.SemaphoreType.DMA((3,)), pltpu.SemaphoreType.DMA((3,))],
    )(part)


def sum_chips(part, land, order):
    nb = PACK_HALF // PACK_BLOCK

    def body(order_ref, p_ref, l1_ref, l2_ref, l3_ref, o_ref):
        o_ref[...] = ((p_ref[0].astype(f32) + l1_ref[0].astype(f32)) + l2_ref[0].astype(f32)) + l3_ref[0].astype(f32)

    slot = lambda j: pl.BlockSpec((1, PACK_BLOCK, LANES), lambda i, order_ref: (order_ref[j], i, 0))
    return pl.pallas_call(
        body, name="sum_chips",
        grid_spec=pltpu.PrefetchScalarGridSpec(
            num_scalar_prefetch=1, grid=(nb,), in_specs=[slot(0), slot(1), slot(2), slot(3)],
            out_specs=pl.BlockSpec((PACK_BLOCK, LANES), lambda i, order_ref: (i, 0))),
        out_shape=jax.ShapeDtypeStruct((PACK_HALF, LANES), f32),
        compiler_params=_cparams("parallel"),
    )(order, part, land, land, land)


def swap_core_halves(red):
    def body(r_ref, out_ref, send_sem, recv_sem):
        x, y, c = _place()
        cp = _remote(r_ref, out_ref, send_sem, recv_sem, (x, y, 1 - c))
        cp.start()
        cp.wait()

    return pl.pallas_call(
        body, name="swap_core_halves", in_specs=[HBM_SPEC], out_specs=HBM_SPEC,
        out_shape=jax.ShapeDtypeStruct(red.shape, red.dtype),
        scratch_shapes=[pltpu.SemaphoreType.DMA(()), pltpu.SemaphoreType.DMA(())],
    )(red)


def _half(ref, core):
    rows = ref.shape[-2] // 2
    return ref.at[(slice(None),) * (len(ref.shape) - 2) + (pl.ds(core * rows, rows), slice(None))]


def gather_weights(shards, conv):
    n = len(shards)

    def body(*refs):
        src, conv_src = refs[:n], refs[n]
        out, conv_out = refs[n + 1:2 * n + 1], refs[2 * n + 1]
        send_sems, recv_sems = refs[2 * n + 2], refs[2 * n + 3]
        x, y, c = _place()
        me_chip = 2 * x + y
        sibling = (x, y, 1 - c)
        chips = _other_chips(x, y)
        sends = []
        for a in range(n):
            for j, (px, py) in enumerate(chips):
                sends.append(_remote(_half(src[a], c), _half(out[a].at[me_chip], c),
                                     send_sems.at[6 * a + j], recv_sems.at[6 * a + j], (px, py, c)))
        for j, (px, py) in enumerate(chips):
            sends.append(_remote(conv_src, conv_out.at[me_chip], send_sems.at[6 * n + j], recv_sems.at[6 * n + j], (px, py, c)))
        for cp in sends:
            cp.start()
        passed = []
        for a in range(n):
            for j, (px, py) in enumerate(chips):
                theirs = _half(out[a].at[2 * px + py], c)
                _remote(theirs, theirs, send_sems.at[6 * a + j], recv_sems.at[6 * a + j], (px, py, c)).wait_recv()
                cp = _remote(theirs, theirs, send_sems.at[6 * a + 3 + j], recv_sems.at[6 * a + 3 + j], sibling)
                cp.start()
                passed.append(cp)
        for j, (px, py) in enumerate(chips):
            theirs = conv_out.at[2 * px + py]
            _remote(theirs, theirs, send_sems.at[6 * n + j], recv_sems.at[6 * n + j], (px, py, c)).wait_recv()
        for a in range(n):
            for j, (px, py) in enumerate(chips):
                theirs = _half(out[a].at[2 * px + py], 1 - c)
                _remote(theirs, theirs, send_sems.at[6 * a + 3 + j], recv_sems.at[6 * a + 3 + j], sibling).wait_recv()
        for cp in sends + passed:
            cp.wait_send()

    return pl.pallas_call(
        body, name="gather_weights", in_specs=[HBM_SPEC] * (n + 1), out_specs=[HBM_SPEC] * (n + 1),
        out_shape=[jax.ShapeDtypeStruct((N_CHIPS,) + s.shape, s.dtype) for s in list(shards) + [conv]],
        scratch_shapes=[pltpu.SemaphoreType.DMA((6 * n + 3,)), pltpu.SemaphoreType.DMA((6 * n + 3,))],
    )(*shards, conv)


SEM_SPEC = pl.BlockSpec(memory_space=pltpu.SEMAPHORE)
SPLIT_EFFECT = pltpu.SideEffectType.DATAFLOW_SIDE_EFFECTING


def _gather_async_copies(src, land, send_sems, recv_sems, x, y, c):
    me_chip = 2 * x + y
    sends, arrivals = [], []
    for a in range(len(src)):
        for j, (px, py) in enumerate(_other_chips(x, y)):
            for core in range(2):
                sends.append(_remote(_half(src[a], c), _half(land[a].at[me_chip], c), send_sems.at[6 * a + 2 * j + core],
                                     recv_sems.at[6 * a + 2 * j + c], (px, py, core)))
                theirs = _half(land[a].at[2 * px + py], core)
                arrivals.append(_remote(theirs, theirs, send_sems.at[6 * a + 2 * j + core],
                                        recv_sems.at[6 * a + 2 * j + core], (px, py, core)))
    return sends, arrivals


def gather_weights_start(shards, after):
    n = len(shards)

    def body(*refs):
        src, land = refs[:n], refs[n:2 * n]
        send_sems, recv_sems, token = refs[2 * n + 1], refs[2 * n + 2], refs[4 * n + 3]
        x, y, c = _place()
        for cp in _gather_async_copies(src, land, send_sems, recv_sems, x, y, c)[0]:
            cp.start()
        token[...] = jnp.zeros_like(token)

    zones = [pltpu.with_memory_space_constraint(lax.empty((N_CHIPS,) + s.shape, s.dtype), pltpu.HBM) for s in shards]
    srcs = [pltpu.with_memory_space_constraint(s, pltpu.HBM) for s in shards]
    out = pl.pallas_call(
        body, name="gather_weights_start",
        out_shape=[pltpu.SemaphoreType.DMA((6 * n,)), pltpu.SemaphoreType.DMA((6 * n,))]
        + [pltpu.HBM(s.shape, s.dtype) for s in shards] + [pltpu.HBM(z.shape, z.dtype) for z in zones]
        + [jax.ShapeDtypeStruct((8, LANES), f32)],
        in_specs=[HBM_SPEC] * (2 * n) + [pl.BlockSpec(memory_space=pl.ANY)],
        out_specs=[SEM_SPEC, SEM_SPEC] + [HBM_SPEC] * (2 * n) + [pl.BlockSpec(memory_space=pltpu.VMEM)],
        input_output_aliases={i: 2 + i for i in range(2 * n)},
        compiler_params=pltpu.CompilerParams(has_side_effects=SPLIT_EFFECT),
    )(*srcs, *zones, after)
    return out[0], out[1], out[2:2 + n], out[2 + n:2 + 2 * n], out[-1]


def gather_weights_wait(send_sems, recv_sems, shards, zones, after):
    n = len(shards)

    def body(*refs):
        src, land = refs[:n], refs[n:2 * n]
        send_sems, recv_sems = refs[2 * n], refs[2 * n + 1]
        x, y, c = _place()
        sends, arrivals = _gather_async_copies(src, land, send_sems, recv_sems, x, y, c)
        for cp in sends:
            cp.wait_send()
        for cp in arrivals:
            cp.wait_recv()

    out = pl.pallas_call(
        body, name="gather_weights_wait",
        out_shape=[pltpu.HBM(s.shape, s.dtype) for s in shards] + [pltpu.HBM(z.shape, z.dtype) for z in zones],
        in_specs=[HBM_SPEC] * (2 * n) + [SEM_SPEC, SEM_SPEC, pl.BlockSpec(memory_space=pl.ANY)],
        out_specs=[HBM_SPEC] * (2 * n),
        input_output_aliases={i: i for i in range(2 * n)},
        compiler_params=pltpu.CompilerParams(has_side_effects=SPLIT_EFFECT),
    )(*shards, *zones, send_sems, recv_sems, after)
    return out[n:]


def swap_grad_halves(grads, *, name):
    n = len(grads)

    def body(*refs):
        g, land, send_sems, recv_sems = refs[:n], refs[n:2 * n], refs[2 * n], refs[2 * n + 1]
        x, y, c = _place()
        copies = [_remote(_half(g[a], 1 - c), land[a], send_sems.at[a], recv_sems.at[a], (x, y, 1 - c)) for a in range(n)]
        for cp in copies:
            cp.start()
        for cp in copies:
            cp.wait()

    return pl.pallas_call(
        body, name=name, in_specs=[HBM_SPEC] * n, out_specs=[HBM_SPEC] * n,
        out_shape=[jax.ShapeDtypeStruct((N_CHIPS, g.shape[1] // 2, g.shape[2]), g.dtype) for g in grads],
        scratch_shapes=[pltpu.SemaphoreType.DMA((n,)), pltpu.SemaphoreType.DMA((n,))],
    )(*grads)


GRAD_ROWS = 256


def add_grad_halves(g, land, core, *, name):
    _, half, cols = land.shape
    tr = GRAD_ROWS if half % GRAD_ROWS == 0 else half
    nb = half // tr

    def body(c_ref, g_ref, l_ref, o_ref):
        o_ref[...] = (g_ref[...].astype(f32) + l_ref[...].astype(f32)).astype(o_ref.dtype)

    blk = (1, tr, cols)
    return pl.pallas_call(
        body, name=name,
        grid_spec=pltpu.PrefetchScalarGridSpec(
            num_scalar_prefetch=1, grid=(N_CHIPS, nb),
            in_specs=[pl.BlockSpec(blk, lambda k, i, c_ref: (k, c_ref[0] * nb + i, 0)),
                      pl.BlockSpec(blk, lambda k, i, c_ref: (k, i, 0))],
            out_specs=pl.BlockSpec(blk, lambda k, i, c_ref: (k, i, 0))),
        out_shape=jax.ShapeDtypeStruct(land.shape, land.dtype),
        compiler_params=_cparams("parallel", "parallel"),
    )(core, g, land)


def scatter_grads(parts):
    n = len(parts)

    def body(*refs):
        p, land, send_sems, recv_sems = refs[:n], refs[n:2 * n], refs[2 * n], refs[2 * n + 1]
        x, y, c = _place()
        me_chip = 2 * x + y
        chips = _other_chips(x, y)
        sends = [_remote(p[a].at[2 * px + py], land[a].at[me_chip], send_sems.at[3 * a + j], recv_sems.at[3 * a + j], (px, py, c))
                 for a in range(n) for j, (px, py) in enumerate(chips)]
        for cp in sends:
            cp.start()
        for a in range(n):
            for j, (px, py) in enumerate(chips):
                slot = land[a].at[2 * px + py]
                _remote(slot, slot, send_sems.at[3 * a + j], recv_sems.at[3 * a + j], (px, py, c)).wait_recv()
        for cp in sends:
            cp.wait_send()

    return pl.pallas_call(
        body, name="scatter_grads", in_specs=[HBM_SPEC] * n, out_specs=[HBM_SPEC] * n,
        out_shape=[jax.ShapeDtypeStruct(p.shape, p.dtype) for p in parts],
        scratch_shapes=[pltpu.SemaphoreType.DMA((3 * n,)), pltpu.SemaphoreType.DMA((3 * n,))],
    )(*parts)


def _scatter_async_copies(parts, land, send_sems, recv_sems, x, y, c):
    me_chip = 2 * x + y
    sends, arrivals = [], []
    for a in range(len(parts)):
        for j, (px, py) in enumerate(_other_chips(x, y)):
            sems = (send_sems.at[3 * a + j], recv_sems.at[3 * a + j], (px, py, c))
            sends.append(_remote(parts[a].at[2 * px + py], land[a].at[me_chip], *sems))
            slot = land[a].at[2 * px + py]
            arrivals.append(_remote(slot, slot, *sems))
    return sends, arrivals


def scatter_grads_start(parts, *, name):
    n = len(parts)

    def body(*refs):
        p, land = refs[:n], refs[n:2 * n]
        send_sems, recv_sems, token = refs[2 * n], refs[2 * n + 1], refs[4 * n + 2]
        x, y, c = _place()
        for cp in _scatter_async_copies(p, land, send_sems, recv_sems, x, y, c)[0]:
            cp.start()
        token[...] = jnp.zeros_like(token)

    zones = [pltpu.with_memory_space_constraint(lax.empty(p.shape, p.dtype), pltpu.HBM) for p in parts]
    srcs = [pltpu.with_memory_space_constraint(p, pltpu.HBM) for p in parts]
    hbm = [pltpu.HBM(p.shape, p.dtype) for p in parts]
    out = pl.pallas_call(
        body, name=name,
        out_shape=[pltpu.SemaphoreType.DMA((3 * n,)), pltpu.SemaphoreType.DMA((3 * n,))] + hbm + hbm
        + [jax.ShapeDtypeStruct((8, LANES), f32)],
        in_specs=[HBM_SPEC] * (2 * n),
        out_specs=[SEM_SPEC, SEM_SPEC] + [HBM_SPEC] * (2 * n) + [pl.BlockSpec(memory_space=pltpu.VMEM)],
        input_output_aliases={i: 2 + i for i in range(2 * n)},
        compiler_params=pltpu.CompilerParams(has_side_effects=SPLIT_EFFECT),
    )(*srcs, *zones)
    return out[0], out[1], out[2:2 + n], out[2 + n:2 + 2 * n], out[-1]


def scatter_grads_wait(send_sems, recv_sems, parts, zones, after, *, name):
    n = len(parts)

    def body(*refs):
        p, land = refs[:n], refs[n:2 * n]
        x, y, c = _place()
        sends, arrivals = _scatter_async_copies(p, land, refs[2 * n], refs[2 * n + 1], x, y, c)
        for cp in sends:
            cp.wait_send()
        for cp in arrivals:
            cp.wait_recv()

    hbm = [pltpu.HBM(p.shape, p.dtype) for p in parts]
    out = pl.pallas_call(
        body, name=name, out_shape=hbm + hbm,
        in_specs=[HBM_SPEC] * (2 * n) + [SEM_SPEC, SEM_SPEC, pl.BlockSpec(memory_space=pl.ANY)],
        out_specs=[HBM_SPEC] * (2 * n),
        input_output_aliases={i: i for i in range(2 * n)},
        compiler_params=pltpu.CompilerParams(has_side_effects=SPLIT_EFFECT),
    )(*parts, *zones, send_sems, recv_sems, after)
    return out[:n], out[n:]


def sum_grads(part, land, order, *, name):
    _, half, cols = part.shape
    tr = GRAD_ROWS if half % GRAD_ROWS == 0 else half

    def body(order_ref, p_ref, l1_ref, l2_ref, l3_ref, o_ref):
        o_ref[...] = ((p_ref[0].astype(f32) + l1_ref[0].astype(f32)) + l2_ref[0].astype(f32)) + l3_ref[0].astype(f32)

    slot = lambda j: pl.BlockSpec((1, tr, cols), lambda i, order_ref: (order_ref[j], i, 0))
    return pl.pallas_call(
        body, name=name,
        grid_spec=pltpu.PrefetchScalarGridSpec(
            num_scalar_prefetch=1, grid=(half // tr,), in_specs=[slot(0), slot(1), slot(2), slot(3)],
            out_specs=pl.BlockSpec((tr, cols), lambda i, order_ref: (i, 0))),
        out_shape=jax.ShapeDtypeStruct((half, cols), f32),
        compiler_params=_cparams("parallel"),
    )(order, part, land, land, land)


def swap_reduced_halves(mine):
    n = len(mine)

    def body(*refs):
        r, out, send_sems, recv_sems = refs[:n], refs[n:2 * n], refs[2 * n], refs[2 * n + 1]
        x, y, c = _place()
        copies = [_remote(r[a], out[a], send_sems.at[a], recv_sems.at[a], (x, y, 1 - c)) for a in range(n)]
        for cp in copies:
            cp.start()
        for cp in copies:
            cp.wait()

    return pl.pallas_call(
        body, name="swap_reduced_halves", in_specs=[HBM_SPEC] * n, out_specs=[HBM_SPEC] * n,
        out_shape=[jax.ShapeDtypeStruct(r.shape, r.dtype) for r in mine],
        scratch_shapes=[pltpu.SemaphoreType.DMA((n,)), pltpu.SemaphoreType.DMA((n,))],
    )(*mine)


def adamw_halves(w, mine, theirs, m, v, core, *, name):
    R, C = w.shape
    tr = min(GRAD_ROWS, R // 2)
    half_nb = R // 2 // tr

    def body(c_ref, w_ref, a_ref, b_ref, m_ref, v_ref, g_ref, d_ref, nm_ref, nv_ref):
        low = pl.program_id(0) < half_nb
        gv = jnp.where(low == (c_ref[0] == 0), a_ref[...], b_ref[...])
        nm = ADAM_B1 * m_ref[...] + (1.0 - ADAM_B1) * gv
        nv = ADAM_B2 * v_ref[...] + (1.0 - ADAM_B2) * jnp.square(gv)
        m_hat = nm / (1.0 - ADAM_B1 ** ADAM_STEP)
        v_hat = nv / (1.0 - ADAM_B2 ** ADAM_STEP)
        g_ref[...] = gv
        d_ref[...] = -ADAM_LR * (m_hat / (jnp.sqrt(v_hat) + ADAM_EPS) + ADAM_WD * w_ref[...])
        nm_ref[...] = nm
        nv_ref[...] = nv

    full = pl.BlockSpec((tr, C), lambda i, c_ref: (i, 0))
    part = pl.BlockSpec((tr, C), lambda i, c_ref: (i % half_nb, 0))
    out = jax.ShapeDtypeStruct((R, C), f32)
    return pl.pallas_call(
        body, name=name,
        grid_spec=pltpu.PrefetchScalarGridSpec(
            num_scalar_prefetch=1, grid=(2 * half_nb,), in_specs=[full, part, part, full, full], out_specs=[full] * 4),
        out_shape=[out] * 4, compiler_params=_cparams("parallel"),
    )(core, w, mine, theirs, m, v)


N_DEV = 8


def all_reduce_small(v):
    def body(src_ref, out_ref, land_ref, send_sems, recv_sems):
        x, y, c = _place()
        me = 4 * x + 2 * y + c
        copies = []
        for r in range(1, N_DEV):
            peer = ((1 - x) if r & 4 else x, (1 - y) if r & 2 else y, (1 - c) if r & 1 else c)
            copies.append(_remote(src_ref, land_ref.at[r], send_sems.at[r - 1], recv_sems.at[r - 1], peer))
        for cp in copies:
            cp.start()
        land_ref[0] = src_ref[...]
        for cp in copies:
            cp.wait()
        acc = land_ref[me]
        for d in range(1, N_DEV):
            acc = acc + land_ref[jnp.bitwise_xor(me, d)]
        out_ref[...] = acc

    vm = pl.BlockSpec(memory_space=pltpu.VMEM)
    return pl.pallas_call(
        body, name="all_reduce_small", in_specs=[vm], out_specs=vm,
        out_shape=jax.ShapeDtypeStruct(v.shape, v.dtype),
        scratch_shapes=[pltpu.VMEM((N_DEV,) + v.shape, v.dtype),
                        pltpu.SemaphoreType.DMA((N_DEV - 1,)), pltpu.SemaphoreType.DMA((N_DEV - 1,))],
    )(v)


def adamw(w, g, m, v, *, name, tr=None, tc=None):
    R, C = w.shape
    if tc is None:
        tr, tc = min(tr, R), C
        blk = pl.BlockSpec((tr, C), lambda i: (i, 0))
    else:
        tr = R
        blk = pl.BlockSpec((R, tc), lambda i: (0, i))

    def body(w_ref, g_ref, m_ref, v_ref, d_ref, nm_ref, nv_ref):
        gv = g_ref[...]
        nm = ADAM_B1 * m_ref[...] + (1.0 - ADAM_B1) * gv
        nv = ADAM_B2 * v_ref[...] + (1.0 - ADAM_B2) * jnp.square(gv)
        m_hat = nm / (1.0 - ADAM_B1 ** ADAM_STEP)
        v_hat = nv / (1.0 - ADAM_B2 ** ADAM_STEP)
        d_ref[...] = -ADAM_LR * (m_hat / (jnp.sqrt(v_hat) + ADAM_EPS) + ADAM_WD * w_ref[...])
        nm_ref[...] = nm
        nv_ref[...] = nv

    out = jax.ShapeDtypeStruct((R, C), f32)
    return pl.pallas_call(
        body, name=name, grid=((R // tr) * (C // tc),), in_specs=[blk] * 4, out_specs=[blk] * 3, out_shape=[out] * 3,
        compiler_params=_cparams("parallel"),
    )(w, g, m, v)


BIG_SHARDS = (("w_in", (1024, 900), True), ("w_out", (256, 1024), False), ("w_cq", (256, 512), False),
              ("w_ckv", (256, 1024), False), ("w_co", (512, 256), True), ("w_mlp1", (1024, 1024), True),
              ("w_mlp2", (1024, 1024), False))
CONV_SHARD = (CONV_WIDTH, 3 * GDN_WIDTH // N_CHIPS)
SMALL_DIMS = (("norm_mix_g", 1024), ("fox_qnorm_g", 64), ("fox_knorm_g", 64), ("fox_f_bias", 8), ("fox_onorm_g", 64),
              ("gdn_A_log", 4), ("gdn_dt_bias", 4), ("gdn_onorm_g", 128), ("norm_xattn_g", 1024), ("mem_norm_g", 1024),
              ("xattn_qnorm_g", 128), ("xattn_knorm_g", 128), ("norm_mlp_g", 1024))
WEIGHT_ORDER = ("norm_mix_g", "w_in", "fox_qnorm_g", "fox_knorm_g", "fox_f_bias", "fox_onorm_g", "gdn_conv_w", "gdn_A_log",
                "gdn_dt_bias", "gdn_onorm_g", "w_out", "norm_xattn_g", "mem_norm_g", "w_cq", "w_ckv", "xattn_qnorm_g",
                "xattn_knorm_g", "w_co", "norm_mlp_g", "w_mlp1", "w_mlp2")


def _pack_rows(pieces, rows, lead=()):
    cat = jnp.concatenate([p.reshape(lead + (-1,)) for p in pieces], axis=-1)
    cat = jnp.pad(cat, [(0, 0)] * len(lead) + [(0, rows * LANES - cat.shape[-1])])
    return cat.reshape(lead + (rows, LANES))


def _unpack_rows(buf, sizes, lead=()):
    flat = buf.reshape(lead + (-1,))
    out, off = [], 0
    for n in sizes:
        out.append(flat[..., off:off + n])
        off += n
    return out


def _conv_to_wire(conv):
    return lax.bitcast_convert_type(conv, bf16)


def _conv_from_wire(wire):
    return lax.bitcast_convert_type(wire, f32)


SMALL_ROWS = 96
SMALL_ADAM_ROWS = 56


def kernel(x, mem, norm_mix_g, w_in, fox_qnorm_g, fox_knorm_g, fox_f_bias, fox_onorm_g, gdn_conv_w, gdn_A_log, gdn_dt_bias, gdn_onorm_g, w_out, norm_xattn_g, mem_norm_g, w_cq, w_ckv, xattn_qnorm_g, xattn_knorm_g, w_co, norm_mlp_g, w_mlp1, w_mlp2, loss_target, m_norm_mix_g, m_w_in, m_fox_qnorm_g, m_fox_knorm_g, m_fox_f_bias, m_fox_onorm_g, m_gdn_conv_w, m_gdn_A_log, m_gdn_dt_bias, m_gdn_onorm_g, m_w_out, m_norm_xattn_g, m_mem_norm_g, m_w_cq, m_w_ckv, m_xattn_qnorm_g, m_xattn_knorm_g, m_w_co, m_norm_mlp_g, m_w_mlp1, m_w_mlp2, v_norm_mix_g, v_w_in, v_fox_qnorm_g, v_fox_knorm_g, v_fox_f_bias, v_fox_onorm_g, v_gdn_conv_w, v_gdn_A_log, v_gdn_dt_bias, v_gdn_onorm_g, v_w_out, v_norm_xattn_g, v_mem_norm_g, v_w_cq, v_w_ckv, v_xattn_qnorm_g, v_xattn_knorm_g, v_w_co, v_norm_mlp_g, v_w_mlp1, v_w_mlp2):
    wts = dict(norm_mix_g=norm_mix_g, w_in=w_in, fox_qnorm_g=fox_qnorm_g, fox_knorm_g=fox_knorm_g, fox_f_bias=fox_f_bias,
               fox_onorm_g=fox_onorm_g, gdn_conv_w=gdn_conv_w, gdn_A_log=gdn_A_log, gdn_dt_bias=gdn_dt_bias,
               gdn_onorm_g=gdn_onorm_g, w_out=w_out, norm_xattn_g=norm_xattn_g, mem_norm_g=mem_norm_g, w_cq=w_cq, w_ckv=w_ckv,
               xattn_qnorm_g=xattn_qnorm_g, xattn_knorm_g=xattn_knorm_g, w_co=w_co, norm_mlp_g=norm_mlp_g, w_mlp1=w_mlp1,
               w_mlp2=w_mlp2)
    mom = dict(norm_mix_g=m_norm_mix_g, w_in=m_w_in, fox_qnorm_g=m_fox_qnorm_g, fox_knorm_g=m_fox_knorm_g,
               fox_f_bias=m_fox_f_bias, fox_onorm_g=m_fox_onorm_g, gdn_conv_w=m_gdn_conv_w, gdn_A_log=m_gdn_A_log,
               gdn_dt_bias=m_gdn_dt_bias, gdn_onorm_g=m_gdn_onorm_g, w_out=m_w_out, norm_xattn_g=m_norm_xattn_g,
               mem_norm_g=m_mem_norm_g, w_cq=m_w_cq, w_ckv=m_w_ckv, xattn_qnorm_g=m_xattn_qnorm_g,
               xattn_knorm_g=m_xattn_knorm_g, w_co=m_w_co, norm_mlp_g=m_norm_mlp_g, w_mlp1=m_w_mlp1, w_mlp2=m_w_mlp2)
    var = dict(norm_mix_g=v_norm_mix_g, w_in=v_w_in, fox_qnorm_g=v_fox_qnorm_g, fox_knorm_g=v_fox_knorm_g,
               fox_f_bias=v_fox_f_bias, fox_onorm_g=v_fox_onorm_g, gdn_conv_w=v_gdn_conv_w, gdn_A_log=v_gdn_A_log,
               gdn_dt_bias=v_gdn_dt_bias, gdn_onorm_g=v_gdn_onorm_g, w_out=v_w_out, norm_xattn_g=v_norm_xattn_g,
               mem_norm_g=v_mem_norm_g, w_cq=v_w_cq, w_ckv=v_w_ckv, xattn_qnorm_g=v_xattn_qnorm_g,
               xattn_knorm_g=v_xattn_knorm_g, w_co=v_w_co, norm_mlp_g=v_norm_mlp_g, w_mlp1=v_w_mlp1, w_mlp2=v_w_mlp2)
    B, S, D = x.shape
    T = B * S
    big_names = [n for n, _, _ in BIG_SHARDS]
    chip = 2 * lax.axis_index("x") + lax.axis_index("y")
    core = lax.axis_index("c").astype(jnp.int32).reshape(1)

    shards = {n: wts[n][0].astype(MXU_DTYPE) for n in big_names[1:]}
    in_t = lambda p: jnp.swapaxes(p[0], 0, 1)
    shards["w_in"] = jnp.pad(in_t(w_in).astype(MXU_DTYPE), ((0, IN_SHARD_PAD - IN_SHARD), (0, 0)))
    w_in_all, conv_all = gather_weights([shards["w_in"]], gdn_conv_w[0])
    late = big_names[1:]
    send_sems, recv_sems, late_src, late_zones, token = gather_weights_start([shards[n] for n in late], conv_all)
    own = lambda g, s: lax.dynamic_update_slice(g, s[None], (chip,) + (0,) * s.ndim)
    full = {"w_in": own(w_in_all, shards["w_in"])}
    conv_full = own(conv_all, gdn_conv_w[0]).transpose(1, 0, 2).reshape(CONV_WIDTH, 3 * GDN_WIDTH)
    rows = lambda g: g.reshape(N_CHIPS * g.shape[1], g.shape[2])
    w_in_t = full["w_in"][:, :IN_SHARD].reshape(IN_DIM, D_MODEL)

    def late_weights(after):
        zones = gather_weights_wait(send_sems, recv_sems, late_src, late_zones, after)
        got = {n: own(z, shards[n]) for n, z in zip(late, zones)}
        return dict(w_out=rows(got["w_out"]), w_cq=rows(got["w_cq"]), w_ckv=rows(got["w_ckv"]), w_co=got["w_co"],
                    w_mlp1=got["w_mlp1"], w_mlp2=rows(got["w_mlp2"]))

    def chip_partials(names, by_chip):
        landed = swap_grad_halves(by_chip, name="swap_grad_halves_" + names[0])
        return [add_grad_halves(g, l, core, name="add_halves_" + n) for n, g, l in zip(names, by_chip, landed)]

    in_flight = []

    def grads_ready(ready):
        names = list(ready)
        *started, tok = scatter_grads_start(chip_partials(names, [ready[n] for n in names]),
                                            name="scatter_grads_start_%d" % len(in_flight))
        in_flight.append((names, *started))
        return tok

    w = dict(wa_t=align_w_in_t(w_in_t), conv_w=conv_full, late=late_weights, grads_ready=grads_ready)
    sp = {n: wts[n] for n, _ in SMALL_DIMS}
    sp["norm_mix_g"] = sp["norm_mix_g"] + token[0, 0]

    loss_part, grad_x, g_big, g_small = local_step(x.reshape(T, D), mem.reshape(-1, D), loss_target.reshape(T, D), w, sp, B=B)

    small_pieces = [g_small[n] for n, _ in SMALL_DIMS] + [g_small["gdn_conv_w"], loss_part]
    small_sizes = [d for _, d in SMALL_DIMS] + [CONV_WIDTH * 3 * GDN_WIDTH, LANES]
    red_small = _unpack_rows(all_reduce_small(_pack_rows(small_pieces, SMALL_ROWS)), small_sizes)
    grads = {n: p.reshape(1, d) for (n, d), p in zip(SMALL_DIMS, red_small)}
    conv_grad = lax.dynamic_slice(red_small[-2].reshape(CONV_WIDTH, 3 * GDN_WIDTH), (0, chip * CONV_SHARD[1]), CONV_SHARD)
    grads["gdn_conv_w"] = conv_grad.reshape((1,) + CONV_SHARD)
    loss = red_small[-1][0]

    names = ["w_in"]
    chip_part = chip_partials(names, [g_big[n] for n in names])
    parts, zones = dict(zip(names, chip_part)), dict(zip(names, scatter_grads(chip_part)))
    for k, (names, send_sems, recv_sems, thru, land) in enumerate(in_flight):
        thru, land = scatter_grads_wait(send_sems, recv_sems, thru, land, grad_x, name="scatter_grads_wait_%d" % k)
        parts.update(zip(names, thru))
        zones.update(zip(names, land))
    order = jnp.stack([chip, chip ^ 2, chip ^ 1, chip ^ 3]).astype(jnp.int32)
    mine = [sum_grads(parts[n], zones[n], order, name="sum_chips_" + n) for n in big_names]
    theirs = swap_reduced_halves(mine)

    delta, new_m, new_v = {}, {}, {}
    for n, a, b in zip(big_names[1:], mine[1:], theirs[1:]):
        g, d, nm, nv = adamw_halves(wts[n][0], a, b, mom[n][0], var[n][0], core, name="adamw_" + n)
        grads[n], delta[n], new_m[n], new_v[n] = g[None], d[None], nm[None], nv[None]
    south = core[0] == 0
    g_in_t = jnp.concatenate([jnp.where(south, mine[0], theirs[0]), jnp.where(south, theirs[0], mine[0])])[:IN_SHARD]
    back = lambda t: jnp.swapaxes(t, 0, 1)[None]
    d, nm, nv = adamw(in_t(w_in), g_in_t, in_t(m_w_in), in_t(v_w_in), name="adamw_w_in", tc=256)
    grads["w_in"], delta["w_in"], new_m["w_in"], new_v["w_in"] = back(g_in_t), back(d), back(nm), back(nv)
    small_names = [n for n, _ in SMALL_DIMS] + ["gdn_conv_w"]
    small_sz = [d for _, d in SMALL_DIMS] + [CONV_SHARD[0] * CONV_SHARD[1]]
    packed4 = [_pack_rows([src[n] for n in small_names], SMALL_ADAM_ROWS) for src in (wts, grads, mom, var)]
    outs = adamw(*packed4, name="adamw_small", tr=SMALL_ADAM_ROWS)
    for dst, buf in zip((delta, new_m, new_v), outs):
        for n, p in zip(small_names, _unpack_rows(buf, small_sz)):
            dst[n] = p.reshape(wts[n].shape)

    return (loss, grad_x.reshape(B, S, D), *[grads[n] for n in WEIGHT_ORDER], *[delta[n] for n in WEIGHT_ORDER],
            *[new_m[n] for n in WEIGHT_ORDER], *[new_v[n] for n in WEIGHT_ORDER])
```

```python
import functools

import jax
import jax.numpy as jnp
import numpy as np
from jax import lax
from jax.experimental import pallas as pl
from jax.experimental.pallas import tpu as pltpu

f32 = jnp.float32
bf16 = jnp.bfloat16
MXU_DTYPE = jnp.bfloat16
WIRE_DTYPE = jnp.bfloat16
INV_PRECISION = lax.Precision.HIGH

D_MODEL = 1024
FOX_HEADS = 8
FOX_HEAD_DIM = 64
FOX_WIDTH = 512
GDN_HEADS = 4
GDN_HEAD_DIM = 128
GDN_WIDTH = 512
CONV_WIDTH = 4
GDN_CHUNK = 64
XATTN_HEADS = 4
XATTN_HEAD_DIM = 128
XATTN_WIDTH = 512
D_FF = 4096
IN_DIM = 3600
EPS = 1e-6
NEG_INF = -1e30
LANES = 128
ADAM_LR = 0.001
ADAM_B1 = 0.9
ADAM_B2 = 0.999
ADAM_EPS = 1e-08
ADAM_WD = 0.01
ADAM_STEP = 10
VMEM_LIMIT = 48 * 1024 * 1024

COL_FOX = 0
COL_GDN = 1536
COL_Z = 3072
COL_SMALL = 3584
IN_ALIGNED = 3840
IN_TILE = 768
SM_F = 0
SM_B = 8
SM_A = 12


def _cparams(*sem):
    return pltpu.CompilerParams(dimension_semantics=sem, vmem_limit_bytes=VMEM_LIMIT)


def _mx(v):
    return v.astype(MXU_DTYPE)


def _dot(a, b, dims, precision=None):
    return lax.dot_general(a, b, (dims, ((), ())), preferred_element_type=f32, precision=precision)


def _dotm(a, b, dims):
    return _dot(_mx(a), _mx(b), dims)


NN = ((1,), (0,))
NT = ((1,), (1,))
TN = ((0,), (0,))


def matmul(a, b, *, name, ta=False, tb=False, b_stacked=False, out_stacked=False, residual=None, relu2_out=False,
           relu2_bwd_aux=None, out_dtype=f32, tm=1024, tn=1024, tk=1024):
    M, K = (a.shape[1], a.shape[0]) if ta else a.shape
    if b_stacked:
        b_cols = b.shape[2]
        N, tk = (b.shape[1], min(tk, b_cols)) if tb else (N_CHIPS * b_cols, tk)
        tn = tn if tb else min(tn, b_cols)
        assert K == (N_CHIPS * b_cols if tb else b.shape[1]), (name, a.shape, b.shape)
    else:
        N = b.shape[0] if tb else b.shape[1]
    if out_stacked:
        tn = min(tn, N // N_CHIPS)
    tm, tn, tk = min(tm, M), min(tn, N), min(tk, K)
    assert M % tm == 0 and N % tn == 0 and K % tk == 0, (name, M, N, K)
    nk = K // tk
    has_res = residual is not None
    has_aux = relu2_bwd_aux is not None

    def body(*refs):
        a_ref, b_ref = refs[0], refs[1]
        pos = 2
        res_ref = aux_ref = None
        if has_res:
            res_ref = refs[pos]
            pos += 1
        if has_aux:
            aux_ref = refs[pos]
            pos += 1
        o_ref = refs[pos]
        k = pl.program_id(2)
        dims = ((0,) if ta else (1,), (1,) if tb else (0,))
        part = _dot(_mx(a_ref[...]), _mx(b_ref[...]), dims)

        def finish(r):
            if has_res:
                r = r + res_ref[...]
            if has_aux:
                r = r * (2.0 * jnp.sqrt(aux_ref[...].astype(f32)))
            if relu2_out:
                o_ref[...] = jnp.square(jnp.maximum(r, 0.0)).astype(o_ref.dtype)
            else:
                o_ref[...] = r.astype(o_ref.dtype)

        if nk == 1:
            finish(part)
            return
        acc_ref = refs[pos + 1]

        @pl.when(k == 0)
        def _():
            acc_ref[...] = part

        @pl.when((k > 0) & (k < nk - 1))
        def _():
            acc_ref[...] += part

        @pl.when(k == nk - 1)
        def _():
            finish(acc_ref[...] + part)

    a_spec = pl.BlockSpec((tk, tm), lambda i, j, k: (k, i)) if ta else pl.BlockSpec((tm, tk), lambda i, j, k: (i, k))
    if b_stacked and tb:
        per = b_cols // tk
        b_spec = pl.BlockSpec((None, tn, tk), lambda i, j, k: (k // per, j, k % per))
    elif b_stacked:
        per = b_cols // tn
        b_spec = pl.BlockSpec((None, tk, tn), lambda i, j, k: (j // per, k, j % per))
    else:
        b_spec = pl.BlockSpec((tn, tk), lambda i, j, k: (j, k)) if tb else pl.BlockSpec((tk, tn), lambda i, j, k: (k, j))
    if out_stacked:
        assert not (has_res or has_aux or relu2_out), name
        per_o = N // N_CHIPS // tn
        o_spec = pl.BlockSpec((None, tm, tn), lambda i, j, k: (j // per_o, i, j % per_o))
        out_full = (N_CHIPS, M, N // N_CHIPS)
    else:
        o_spec = pl.BlockSpec((tm, tn), lambda i, j, k: (i, j))
        out_full = (M, N)
    in_specs, args = [a_spec, b_spec], [a, b]
    if has_res:
        in_specs.append(o_spec)
        args.append(residual)
    if has_aux:
        in_specs.append(o_spec)
        args.append(relu2_bwd_aux)
    out_shape = [jax.ShapeDtypeStruct(out_full, out_dtype)]
    out_specs = [o_spec]
    res = pl.pallas_call(
        body, name=name, grid=(M // tm, N // tn, nk), in_specs=in_specs, out_specs=out_specs, out_shape=out_shape,
        scratch_shapes=[pltpu.VMEM((tm, tn), f32)] if nk > 1 else [],
        compiler_params=_cparams("parallel", "parallel", "arbitrary"),
    )(*args)
    return res[0]


def matmul_rows(a, b, extras, *, name, mode, tb=False, b_stacked=False, tm=1024, tk=1024):
    M, K = a.shape
    N = D_MODEL
    if b_stacked:
        assert tb, name
        tk = min(tk, b.shape[2])
        per = b.shape[2] // tk
        b_spec = pl.BlockSpec((None, N, tk), lambda i, k: (k // per, 0, k % per))
    elif tb:
        tk = min(tk, K)
        b_spec = pl.BlockSpec((N, tk), lambda i, k: (0, k))
    else:
        tk = min(tk, K)
        b_spec = pl.BlockSpec((tk, N), lambda i, k: (k, 0))
    tm = min(tm, M)
    assert M % tm == 0 and K % tk == 0, (name, M, K)
    nk = K // tk
    extras = [e for e in extras if e is not None]
    n_ex = len(extras)

    def body(*refs):
        a_ref, b_ref = refs[0], refs[1]
        ex = refs[2:2 + n_ex]
        o_ref, s_ref = refs[2 + n_ex], refs[3 + n_ex]
        i, k = pl.program_id(0), pl.program_id(1)
        part = _dot(_mx(a_ref[...]), _mx(b_ref[...]), ((1,), (1,) if tb else (0,)))

        def finish(y):
            @pl.when(i == 0)
            def _():
                s_ref[...] = jnp.zeros_like(s_ref)

            if mode == "rms_bwd":
                xv, gv = ex[0][...], ex[1][...]
                rstd = lax.rsqrt(jnp.mean(xv * xv, axis=-1, keepdims=True) + EPS)
                xhat = xv * rstd
                gd = y * gv
                dx = rstd * (gd - xhat * jnp.mean(gd * xhat, axis=-1, keepdims=True))
                o_ref[...] = dx + ex[2][...] if n_ex == 3 else dx
                s_ref[...] += jnp.sum(y * xhat, axis=0, keepdims=True)
            else:
                e = y + ex[0][...] - ex[1][...]
                o_ref[...] = e * (1.0 / N)
                tot = 0.5 * jnp.sum(jnp.mean(e * e, axis=-1, keepdims=True), axis=0, keepdims=True)
                s_ref[...] += jnp.broadcast_to(tot, s_ref.shape)

        if nk == 1:
            finish(part)
            return
        acc_ref = refs[4 + n_ex]

        @pl.when(k == 0)
        def _():
            acc_ref[...] = part

        @pl.when((k > 0) & (k < nk - 1))
        def _():
            acc_ref[...] += part

        @pl.when(k == nk - 1)
        def _():
            finish(acc_ref[...] + part)

    row = pl.BlockSpec((tm, N), lambda i, k: (i, 0))
    vec = pl.BlockSpec((1, N), lambda i, k: (0, 0))
    if mode == "rms_bwd":
        ex_specs = [row, vec] + ([row] if n_ex == 3 else [])
        s_shape, s_spec = jax.ShapeDtypeStruct((1, N), f32), vec
    else:
        ex_specs = [row, row]
        s_shape, s_spec = jax.ShapeDtypeStruct((1, LANES), f32), pl.BlockSpec((1, LANES), lambda i, k: (0, 0))
    return pl.pallas_call(
        body, name=name, grid=(M // tm, nk),
        in_specs=[pl.BlockSpec((tm, tk), lambda i, k: (i, k)), b_spec] + ex_specs,
        out_specs=[row, s_spec], out_shape=[jax.ShapeDtypeStruct((M, N), f32), s_shape],
        scratch_shapes=[pltpu.VMEM((tm, N), f32)] if nk > 1 else [],
        compiler_params=_cparams("arbitrary", "arbitrary"),
    )(a, b, *extras)


def rms_fwd(x, g, *, name, tr=512):
    R, D = x.shape
    tr = min(tr, R)

    def body(x_ref, g_ref, o_ref):
        xv = x_ref[...]
        y = xv * lax.rsqrt(jnp.mean(xv * xv, axis=-1, keepdims=True) + EPS)
        o_ref[...] = (y * g_ref[...]).astype(o_ref.dtype)

    return pl.pallas_call(
        body, name=name, grid=(R // tr,),
        in_specs=[pl.BlockSpec((tr, D), lambda i: (i, 0)), pl.BlockSpec((1, D), lambda i: (0, 0))],
        out_specs=pl.BlockSpec((tr, D), lambda i: (i, 0)),
        out_shape=jax.ShapeDtypeStruct((R, D), MXU_DTYPE),
        compiler_params=_cparams("parallel"),
    )(x, g)


def rms_bwd(x, g, dh, residual, *, name, tr=512):
    R, D = x.shape
    tr = min(tr, R)
    has_res = residual is not None

    def body(*refs):
        if has_res:
            x_ref, g_ref, dh_ref, res_ref, dx_ref, dg_ref = refs
        else:
            x_ref, g_ref, dh_ref, dx_ref, dg_ref = refs
        xv = x_ref[...]
        rstd = lax.rsqrt(jnp.mean(xv * xv, axis=-1, keepdims=True) + EPS)
        xhat = xv * rstd
        dh = dh_ref[...].astype(f32)
        gd = dh * g_ref[...]
        dx = rstd * (gd - xhat * jnp.mean(gd * xhat, axis=-1, keepdims=True))
        if has_res:
            dx = dx + res_ref[...]
        dx_ref[...] = dx

        @pl.when(pl.program_id(0) == 0)
        def _():
            dg_ref[...] = jnp.zeros_like(dg_ref)

        dg_ref[...] += jnp.sum(dh * xhat, axis=0, keepdims=True)

    row = pl.BlockSpec((tr, D), lambda i: (i, 0))
    vec = pl.BlockSpec((1, D), lambda i: (0, 0))
    in_specs = [row, vec, row] + ([row] if has_res else [])
    args = [x, g, dh] + ([residual] if has_res else [])
    return pl.pallas_call(
        body, name=name, grid=(R // tr,), in_specs=in_specs, out_specs=[row, vec],
        out_shape=[jax.ShapeDtypeStruct((R, D), f32), jax.ShapeDtypeStruct((1, D), f32)],
        compiler_params=_cparams("arbitrary"),
    )(*args)


def loss_head(y, target, *, tr=512):
    R, D = y.shape
    tr = min(tr, R)

    def body(y_ref, t_ref, dy_ref, loss_ref):
        e = y_ref[...] - t_ref[...]
        dy_ref[...] = e * (1.0 / D)

        @pl.when(pl.program_id(0) == 0)
        def _():
            loss_ref[...] = jnp.zeros_like(loss_ref)

        part = 0.5 * jnp.sum(jnp.mean(e * e, axis=-1, keepdims=True), axis=0, keepdims=True)
        loss_ref[...] += jnp.broadcast_to(part, loss_ref.shape)

    row = pl.BlockSpec((tr, D), lambda i: (i, 0))
    return pl.pallas_call(
        body, name="loss_head", grid=(R // tr,), in_specs=[row, row],
        out_specs=[row, pl.BlockSpec((1, LANES), lambda i: (0, 0))],
        out_shape=[jax.ShapeDtypeStruct((R, D), f32), jax.ShapeDtypeStruct((1, LANES), f32)],
        compiler_params=_cparams("arbitrary"),
    )(y, target)


def _head_rms(v, g):
    r = lax.rsqrt(jnp.mean(v * v, axis=-1, keepdims=True) + EPS)
    return v * r * g, r


def _head_rms_bwd(v, r, g, dn):
    vhat = v * r
    gd = dn * g
    dv = r * (gd - vhat * jnp.mean(gd * vhat, axis=-1, keepdims=True))
    return dv, jnp.sum(dn * vhat, axis=0, keepdims=True)


def _softmax_rows(s):
    m = jnp.max(s, axis=-1, keepdims=True)
    e = jnp.exp(s - m)
    return e / jnp.sum(e, axis=-1, keepdims=True)


def xattn_fwd(cq, ckv, gq, gk, *, B, tq=512):
    T = cq.shape[0]
    S = T // B
    M = ckv.shape[0] // B
    tq = min(tq, S)
    nq = S // tq
    scale = XATTN_HEAD_DIM ** -0.5

    def body(q_ref, k_ref, v_ref, gq_ref, gk_ref, o_ref):
        qn, _ = _head_rms(q_ref[...], gq_ref[...])
        kn, _ = _head_rms(k_ref[...], gk_ref[...])
        p = _softmax_rows(_dot(_mx(qn), _mx(kn), NT) * scale)
        o_ref[...] = _dot(_mx(p), _mx(v_ref[...]), NN).astype(o_ref.dtype)

    hd = XATTN_HEAD_DIM
    vec = pl.BlockSpec((1, hd), lambda b, h, i: (0, 0))
    return pl.pallas_call(
        body, name="xattn_fwd", grid=(B, XATTN_HEADS, nq),
        in_specs=[pl.BlockSpec((tq, hd), lambda b, h, i: (b * nq + i, h)),
                  pl.BlockSpec((M, hd), lambda b, h, i: (b, h)),
                  pl.BlockSpec((M, hd), lambda b, h, i: (b, XATTN_HEADS + h)), vec, vec],
        out_specs=pl.BlockSpec((tq, hd), lambda b, h, i: (b * nq + i, h)),
        out_shape=jax.ShapeDtypeStruct((T, XATTN_WIDTH), MXU_DTYPE),
        compiler_params=_cparams("parallel", "parallel", "parallel"),
    )(cq, ckv, ckv, gq, gk)


def xattn_bwd(cq, ckv, gq, gk, dco, *, B, tq=512):
    T = cq.shape[0]
    S = T // B
    M = ckv.shape[0] // B
    tq = min(tq, S)
    nq = S // tq
    scale = XATTN_HEAD_DIM ** -0.5
    hd = XATTN_HEAD_DIM

    def body(q_ref, k_ref, v_ref, gq_ref, gk_ref, do_ref, dq_ref, dk_ref, dv_ref, dgq_ref, dgk_ref, dkn_acc, dv_acc):
        b, h, i = pl.program_id(0), pl.program_id(1), pl.program_id(2)

        @pl.when((b == 0) & (h == 0) & (i == 0))
        def _():
            dgq_ref[...] = jnp.zeros_like(dgq_ref)
            dgk_ref[...] = jnp.zeros_like(dgk_ref)

        @pl.when(i == 0)
        def _():
            dkn_acc[...] = jnp.zeros_like(dkn_acc)
            dv_acc[...] = jnp.zeros_like(dv_acc)

        q, k, v = q_ref[...], k_ref[...], v_ref[...]
        gqv, gkv = gq_ref[...], gk_ref[...]
        qn, rq = _head_rms(q, gqv)
        kn, rk = _head_rms(k, gkv)
        p = _softmax_rows(_dot(_mx(qn), _mx(kn), NT) * scale)
        do = do_ref[...]
        dv_acc[...] += _dot(_mx(p), _mx(do), TN)
        dp = _dot(_mx(do), _mx(v), NT)
        ds = p * (dp - jnp.sum(dp * p, axis=-1, keepdims=True)) * scale
        dqn = _dot(_mx(ds), _mx(kn), NN)
        dkn_acc[...] += _dot(_mx(ds), _mx(qn), TN)
        dq, dgq = _head_rms_bwd(q, rq, gqv, dqn)
        dq_ref[...] = dq.astype(dq_ref.dtype)
        dgq_ref[...] += dgq

        @pl.when(i == nq - 1)
        def _():
            dk, dgk = _head_rms_bwd(k, rk, gkv, dkn_acc[...])
            dk_ref[...] = dk.astype(dk_ref.dtype)
            dv_ref[...] = dv_acc[...].astype(dv_ref.dtype)
            dgk_ref[...] += dgk

    vec = pl.BlockSpec((1, hd), lambda b, h, i: (0, 0))
    qspec = pl.BlockSpec((tq, hd), lambda b, h, i: (b * nq + i, h))
    kspec = pl.BlockSpec((M, hd), lambda b, h, i: (b, h))
    vspec = pl.BlockSpec((M, hd), lambda b, h, i: (b, XATTN_HEADS + h))
    dq, dk, dv, dgq, dgk = pl.pallas_call(
        body, name="xattn_bwd", grid=(B, XATTN_HEADS, nq),
        in_specs=[qspec, kspec, vspec, vec, vec, qspec],
        out_specs=[qspec, kspec, kspec, vec, vec],
        out_shape=[jax.ShapeDtypeStruct((T, XATTN_WIDTH), MXU_DTYPE),
                   jax.ShapeDtypeStruct((B * M, XATTN_WIDTH), MXU_DTYPE),
                   jax.ShapeDtypeStruct((B * M, XATTN_WIDTH), MXU_DTYPE),
                   jax.ShapeDtypeStruct((1, hd), f32), jax.ShapeDtypeStruct((1, hd), f32)],
        scratch_shapes=[pltpu.VMEM((M, hd), f32), pltpu.VMEM((M, hd), f32)],
        compiler_params=_cparams("arbitrary", "arbitrary", "arbitrary"),
    )(cq, ckv, ckv, gq, gk, dco)
    return dq, jnp.concatenate([dk, dv], axis=1), dgq, dgk


FOX_PAIRS = FOX_HEADS // 2


def _fox_scores(qn, kn, ccol, crow, q0, tq, S, scale):
    s = _dot(_mx(qn), _mx(kn), NT) * scale + ccol - crow
    qpos = q0 + lax.broadcasted_iota(jnp.int32, (tq, S), 0)
    kpos = lax.broadcasted_iota(jnp.int32, (tq, S), 1)
    return jnp.where(kpos <= qpos, s, NEG_INF)


def fox_fwd(P, ccol, crow, gq, gk, go, *, B, tq=256):
    T = P.shape[0]
    S = T // B
    tq = min(tq, S)
    nq = S // tq
    hd = FOX_HEAD_DIM
    scale = hd ** -0.5

    def body(q_ref, k_ref, v_ref, ccol_ref, crow_ref, gq_ref, gk_ref, go_ref, o_ref, oa_ref):
        q0 = pl.program_id(2) * tq
        for e in range(2):
            sl = slice(e * hd, (e + 1) * hd)
            qn, _ = _head_rms(q_ref[:, sl], gq_ref[:, sl])
            kn, _ = _head_rms(k_ref[:, sl], gk_ref[:, sl])
            p = _softmax_rows(_fox_scores(qn, kn, ccol_ref[0, e], crow_ref[0, e], q0, tq, S, scale))
            o = _dot(_mx(p), _mx(v_ref[:, sl]), NN)
            o_ref[:, sl] = o
            oa_ref[:, sl] = _head_rms(o, go_ref[:, sl])[0].astype(oa_ref.dtype)

    W = 2 * hd
    vec = pl.BlockSpec((1, W), lambda b, h, i: (0, 0))
    ospec = pl.BlockSpec((tq, W), lambda b, h, i: (b * nq + i, h))
    return pl.pallas_call(
        body, name="fox_fwd", grid=(B, FOX_PAIRS, nq),
        in_specs=[pl.BlockSpec((tq, W), lambda b, h, i: (b * nq + i, h)),
                  pl.BlockSpec((S, W), lambda b, h, i: (b, FOX_PAIRS + h)),
                  pl.BlockSpec((S, W), lambda b, h, i: (b, 2 * FOX_PAIRS + h)),
                  pl.BlockSpec((1, 2, tq, 1), lambda b, h, i: (b, h, i, 0)),
                  pl.BlockSpec((1, 2, 1, S), lambda b, h, i: (b, h, 0, 0)), vec, vec, vec],
        out_specs=[ospec, ospec],
        out_shape=[jax.ShapeDtypeStruct((T, FOX_WIDTH), f32), jax.ShapeDtypeStruct((T, FOX_WIDTH), MXU_DTYPE)],
        compiler_params=_cparams("parallel", "parallel", "parallel"),
    )(P, P, P, ccol, crow, gq, gk, go)


def fox_bwd(P, ccol, crow, gq, gk, go, o_raw, d_oab, *, B, tq=256):
    T = P.shape[0]
    S = T // B
    tq = min(tq, S)
    nq = S // tq
    hd = FOX_HEAD_DIM
    scale = hd ** -0.5

    def body(q_ref, k_ref, v_ref, ccol_ref, crow_ref, gq_ref, gk_ref, go_ref, o_ref, doa_ref,
             dq_ref, dk_ref, dv_ref, dccol_ref, dcrow_ref, dgq_ref, dgk_ref, dgo_ref, dkn_acc, dv_acc, dcrow_acc):
        b, h, i = pl.program_id(0), pl.program_id(1), pl.program_id(2)
        q0 = i * tq

        @pl.when((b == 0) & (h == 0) & (i == 0))
        def _():
            dgq_ref[...] = jnp.zeros_like(dgq_ref)
            dgk_ref[...] = jnp.zeros_like(dgk_ref)
            dgo_ref[...] = jnp.zeros_like(dgo_ref)

        @pl.when(i == 0)
        def _():
            dkn_acc[...] = jnp.zeros_like(dkn_acc)
            dv_acc[...] = jnp.zeros_like(dv_acc)
            dcrow_acc[...] = jnp.zeros_like(dcrow_acc)

        for e in range(2):
            sl = slice(e * hd, (e + 1) * hd)
            q, k, v = q_ref[:, sl], k_ref[:, sl], v_ref[:, sl]
            gqv, gkv, gov = gq_ref[:, sl], gk_ref[:, sl], go_ref[:, sl]
            qn, rq = _head_rms(q, gqv)
            kn, rk = _head_rms(k, gkv)
            p = _softmax_rows(_fox_scores(qn, kn, ccol_ref[0, e], crow_ref[0, e], q0, tq, S, scale))
            o = o_ref[:, sl]
            ro = lax.rsqrt(jnp.mean(o * o, axis=-1, keepdims=True) + EPS)
            do, dgo = _head_rms_bwd(o, ro, gov, doa_ref[:, sl])
            dgo_ref[:, sl] += dgo
            dv_acc[e] += _dot(_mx(p), _mx(do), TN)
            dp = _dot(_mx(do), _mx(v), NT)
            ds = p * (dp - jnp.sum(do * o, axis=-1, keepdims=True))
            dccol_ref[0, e] = jnp.sum(ds, axis=1, keepdims=True)
            dcrow_acc[e] -= jnp.sum(ds, axis=0, keepdims=True)
            dqn = _dot(_mx(ds), _mx(kn), NN) * scale
            dkn_acc[e] += _dot(_mx(ds), _mx(qn), TN) * scale
            dq, dgq = _head_rms_bwd(q, rq, gqv, dqn)
            dq_ref[:, sl] = dq.astype(dq_ref.dtype)
            dgq_ref[:, sl] += dgq

        @pl.when(i == nq - 1)
        def _():
            for e in range(2):
                sl = slice(e * hd, (e + 1) * hd)
                k = k_ref[:, sl]
                gkv = gk_ref[:, sl]
                rk = lax.rsqrt(jnp.mean(k * k, axis=-1, keepdims=True) + EPS)
                dk, dgk = _head_rms_bwd(k, rk, gkv, dkn_acc[e])
                dk_ref[:, sl] = dk.astype(dk_ref.dtype)
                dv_ref[:, sl] = dv_acc[e].astype(dv_ref.dtype)
                dgk_ref[:, sl] += dgk
                dcrow_ref[0, e] = dcrow_acc[e]

    W = 2 * hd
    vec = pl.BlockSpec((1, W), lambda b, h, i: (0, 0))
    qspec = pl.BlockSpec((tq, W), lambda b, h, i: (b * nq + i, h))
    kvout = pl.BlockSpec((S, W), lambda b, h, i: (b, h))
    colspec = pl.BlockSpec((1, 2, tq, 1), lambda b, h, i: (b, h, i, 0))
    rowspec = pl.BlockSpec((1, 2, 1, S), lambda b, h, i: (b, h, 0, 0))
    return pl.pallas_call(
        body, name="fox_bwd", grid=(B, FOX_PAIRS, nq),
        in_specs=[qspec,
                  pl.BlockSpec((S, W), lambda b, h, i: (b, FOX_PAIRS + h)),
                  pl.BlockSpec((S, W), lambda b, h, i: (b, 2 * FOX_PAIRS + h)),
                  colspec, rowspec, vec, vec, vec, qspec, qspec],
        out_specs=[qspec, kvout, kvout, colspec, rowspec, vec, vec, vec],
        out_shape=[jax.ShapeDtypeStruct((T, FOX_WIDTH), MXU_DTYPE), jax.ShapeDtypeStruct((T, FOX_WIDTH), MXU_DTYPE),
                   jax.ShapeDtypeStruct((T, FOX_WIDTH), MXU_DTYPE),
                   jax.ShapeDtypeStruct((B, FOX_HEADS, S, 1), f32), jax.ShapeDtypeStruct((B, FOX_HEADS, 1, S), f32),
                   jax.ShapeDtypeStruct((1, W), f32), jax.ShapeDtypeStruct((1, W), f32), jax.ShapeDtypeStruct((1, W), f32)],
        scratch_shapes=[pltpu.VMEM((2, S, hd), f32), pltpu.VMEM((2, S, hd), f32), pltpu.VMEM((2, 1, S), f32)],
        compiler_params=_cparams("arbitrary", "arbitrary", "arbitrary"),
    )(P, P, P, ccol, crow, gq, gk, go, o_raw, d_oab)


FOX_TQ = 512
FOX_TK = 512
GROUP_PRECISION = lax.Precision.HIGH


def _head_mean(v):
    n = v.shape[1]
    r = lax.broadcasted_iota(jnp.int32, (n, n), 0) // FOX_HEAD_DIM
    c = lax.broadcasted_iota(jnp.int32, (n, n), 1) // FOX_HEAD_DIM
    ones = (r == c).astype(bf16)
    hi = v.astype(bf16)
    lo = (v - hi.astype(f32)).astype(bf16)
    return (_dot(hi, ones, NN) + _dot(lo, ones, NN)) * (1.0 / FOX_HEAD_DIM)


def fox_prep_fwd(P, gq, gk, *, tr=512):
    T = P.shape[0]
    tr = min(tr, T)
    scale = FOX_HEAD_DIM ** -0.5

    def body(q_ref, k_ref, v_ref, gq_ref, gk_ref, qn_ref, kn_ref, vb_ref):
        q, k = q_ref[...], k_ref[...]
        qn_ref[...] = (q * lax.rsqrt(_head_mean(q * q) + EPS) * (gq_ref[...] * scale)).astype(qn_ref.dtype)
        kn_ref[...] = (k * lax.rsqrt(_head_mean(k * k) + EPS) * gk_ref[...]).astype(kn_ref.dtype)
        vb_ref[...] = v_ref[...].astype(vb_ref.dtype)

    W = FOX_WIDTH
    col = lambda j: pl.BlockSpec((tr, W), lambda i: (i, j))
    vec = pl.BlockSpec((1, W), lambda i: (0, 0))
    out = jax.ShapeDtypeStruct((T, W), MXU_DTYPE)
    return pl.pallas_call(
        body, name="fox_prep_fwd", grid=(T // tr,), in_specs=[col(0), col(1), col(2), vec, vec],
        out_specs=[col(0)] * 3, out_shape=[out] * 3, compiler_params=_cparams("parallel"),
    )(P, P, P, gq, gk)


def fox_prep_bwd(P, gq, gk, dqn, dkn, *, tr=512):
    T = P.shape[0]
    tr = min(tr, T)
    scale = FOX_HEAD_DIM ** -0.5

    def body(q_ref, k_ref, gq_ref, gk_ref, dqn_ref, dkn_ref, dq_ref, dk_ref, dgq_ref, dgk_ref):
        @pl.when(pl.program_id(0) == 0)
        def _():
            dgq_ref[...] = jnp.zeros_like(dgq_ref)
            dgk_ref[...] = jnp.zeros_like(dgk_ref)

        def one(x, g, dn, dx_ref, dg_ref):
            r = lax.rsqrt(_head_mean(x * x) + EPS)
            xhat = x * r
            gd = dn * g
            dx_ref[...] = (r * (gd - xhat * _head_mean(gd * xhat))).astype(dx_ref.dtype)
            return jnp.sum(dn * xhat, axis=0, keepdims=True)

        dgq_ref[...] += scale * one(q_ref[...], gq_ref[...] * scale, dqn_ref[...], dq_ref, dgq_ref)
        dgk_ref[...] += one(k_ref[...], gk_ref[...], dkn_ref[...], dk_ref, dgk_ref)

    W = FOX_WIDTH
    col = lambda j: pl.BlockSpec((tr, W), lambda i: (i, j))
    vec = pl.BlockSpec((1, W), lambda i: (0, 0))
    return pl.pallas_call(
        body, name="fox_prep_bwd", grid=(T // tr,), in_specs=[col(0), col(1), vec, vec, col(0), col(0)],
        out_specs=[col(0), col(0), vec, vec],
        out_shape=[jax.ShapeDtypeStruct((T, W), MXU_DTYPE), jax.ShapeDtypeStruct((T, W), MXU_DTYPE),
                   jax.ShapeDtypeStruct((1, W), f32), jax.ShapeDtypeStruct((1, W), f32)],
        compiler_params=_cparams("arbitrary"),
    )(P, P, gq, gk, dqn, dkn)


def _fox_tile_scores(q, k_ref, ccol_ref, cq, e, j, sl, mask_off):
    tq, tk = FOX_TQ, FOX_TK
    rows = pl.ds(pl.multiple_of(j * tk, tk), tk)
    k = k_ref[rows, sl]
    s = _dot(k, q, NT) + cq - ccol_ref[0, e, rows, :]
    if mask_off is not None:
        key = lax.broadcasted_iota(jnp.int32, (tk, tq), 0) + mask_off
        query = lax.broadcasted_iota(jnp.int32, (tk, tq), 1)
        s = jnp.where(key <= query, s, NEG_INF)
    return s, k, rows


def _fox_sweep(i, update, carry):
    nd = FOX_TQ // FOX_TK
    carry = lax.fori_loop(0, i * nd, lambda j, cr: update(cr, j, None), carry)
    for d in range(nd):
        carry = update(carry, i * nd + d, d * FOX_TK)
    return carry


def fox_core_fwd(qn, kn, vb, ccol, crow, go, *, B):
    T = qn.shape[0]
    S = T // B
    tq = FOX_TQ
    nq = S // tq
    hd = FOX_HEAD_DIM

    def body(q_ref, k_ref, v_ref, ccol_ref, crow_ref, go_ref, o_ref, oa_ref, lse_ref):
        i = pl.program_id(2)
        for e in range(2):
            sl = slice(e * hd, (e + 1) * hd)
            q = q_ref[:, sl]
            cq = crow_ref[0, e, i]

            def update(carry, j, mask_off):
                m, l, acc = carry
                s, _, rows = _fox_tile_scores(q, k_ref, ccol_ref, cq, e, j, sl, mask_off)
                m2 = jnp.maximum(m, jnp.max(s, axis=0, keepdims=True))
                a = jnp.exp(m - m2)
                p = jnp.exp(s - m2)
                return m2, a * l + jnp.sum(p, axis=0, keepdims=True), a * acc + _dot(v_ref[rows, sl], _mx(p), TN)

            carry = (jnp.full((1, tq), NEG_INF, f32), jnp.zeros((1, tq), f32), jnp.zeros((hd, tq), f32))
            m, l, acc = _fox_sweep(i, update, carry)
            o = (acc / l).T
            o_ref[:, sl] = o
            oa_ref[:, sl] = _head_rms(o, go_ref[:, sl])[0].astype(oa_ref.dtype)
            lse_ref[0, e, 0] = m + jnp.log(l)

    W = 2 * hd
    qspec = pl.BlockSpec((tq, W), lambda b, h, i: (b * nq + i, h))
    kspec = pl.BlockSpec((S, W), lambda b, h, i: (b, h))
    return pl.pallas_call(
        body, name="fox_core_fwd", grid=(B, FOX_PAIRS, nq),
        in_specs=[qspec, kspec, kspec, pl.BlockSpec((1, 2, S, 1), lambda b, h, i: (b, h, 0, 0)),
                  pl.BlockSpec((1, 2, nq, 1, tq), lambda b, h, i: (b, h, 0, 0, 0)),
                  pl.BlockSpec((1, W), lambda b, h, i: (0, 0))],
        out_specs=[qspec, qspec, pl.BlockSpec((1, 2, 1, 1, tq), lambda b, h, i: (b, h, i, 0, 0))],
        out_shape=[jax.ShapeDtypeStruct((T, FOX_WIDTH), f32), jax.ShapeDtypeStruct((T, FOX_WIDTH), MXU_DTYPE),
                   jax.ShapeDtypeStruct((B, FOX_HEADS, nq, 1, tq), f32)],
        compiler_params=_cparams("parallel", "parallel", "parallel"),
    )(qn, kn, vb, ccol, crow, go)


def fox_core_bwd(qn, kn, vb, ccol, crow, go, o_raw, lse, d_oab, *, B):
    T = qn.shape[0]
    S = T // B
    tq = FOX_TQ
    nq = S // tq
    hd = FOX_HEAD_DIM

    def body(q_ref, k_ref, v_ref, ccol_ref, crow_ref, go_ref, o_ref, lse_ref, doa_ref,
             dq_ref, dk_ref, dv_ref, dccol_ref, dcrow_ref, dgo_ref, dk_acc, dv_acc, dck_acc):
        b, h, i = pl.program_id(0), pl.program_id(1), pl.program_id(2)

        @pl.when((b == 0) & (h == 0) & (i == 0))
        def _():
            dgo_ref[...] = jnp.zeros_like(dgo_ref)

        @pl.when(i == 0)
        def _():
            dk_acc[...] = jnp.zeros_like(dk_acc)
            dv_acc[...] = jnp.zeros_like(dv_acc)
            dck_acc[...] = jnp.zeros_like(dck_acc)

        for e in range(2):
            sl = slice(e * hd, (e + 1) * hd)
            q = q_ref[:, sl]
            cq = crow_ref[0, e, i]
            lse_e = lse_ref[0, e, 0]
            o = o_ref[:, sl]
            ro = lax.rsqrt(jnp.mean(o * o, axis=-1, keepdims=True) + EPS)
            do, dgo = _head_rms_bwd(o, ro, go_ref[:, sl], doa_ref[:, sl])
            dgo_ref[:, sl] += dgo
            delta = jnp.sum((do * o).T, axis=0, keepdims=True)
            do_b = _mx(do)

            def update(carry, j, mask_off):
                dq, dcq = carry
                s, k, rows = _fox_tile_scores(q, k_ref, ccol_ref, cq, e, j, sl, mask_off)
                p = jnp.exp(s - lse_e)
                dv_acc[e, rows, :] += _dot(_mx(p), do_b, NN)
                ds = p * (_dot(v_ref[rows, sl], do_b, NT) - delta)
                dck_acc[e, rows, :] -= jnp.sum(ds, axis=1, keepdims=True)
                ds_b = _mx(ds)
                dk_acc[e, rows, :] += _dot(ds_b, q, NN)
                return dq + _dot(ds_b, k, TN), dcq + jnp.sum(ds, axis=0, keepdims=True)

            dq, dcq = _fox_sweep(i, update, (jnp.zeros((tq, hd), f32), jnp.zeros((1, tq), f32)))
            dq_ref[:, sl] = dq
            dcrow_ref[0, e, 0] = dcq

        @pl.when(i == nq - 1)
        def _():
            for e in range(2):
                sl = slice(e * hd, (e + 1) * hd)
                dk_ref[:, sl] = dk_acc[e]
                dv_ref[:, sl] = dv_acc[e].astype(dv_ref.dtype)
            dccol_ref[0] = dck_acc[...]

    W = 2 * hd
    qspec = pl.BlockSpec((tq, W), lambda b, h, i: (b * nq + i, h))
    kspec = pl.BlockSpec((S, W), lambda b, h, i: (b, h))
    colspec = pl.BlockSpec((1, 2, S, 1), lambda b, h, i: (b, h, 0, 0))
    rowspec = pl.BlockSpec((1, 2, nq, 1, tq), lambda b, h, i: (b, h, 0, 0, 0))
    tilespec = pl.BlockSpec((1, 2, 1, 1, tq), lambda b, h, i: (b, h, i, 0, 0))
    vec = pl.BlockSpec((1, W), lambda b, h, i: (0, 0))
    return pl.pallas_call(
        body, name="fox_core_bwd", grid=(B, FOX_PAIRS, nq),
        in_specs=[qspec, kspec, kspec, colspec, rowspec, vec, qspec, tilespec, qspec],
        out_specs=[qspec, kspec, kspec, colspec, tilespec, vec],
        out_shape=[jax.ShapeDtypeStruct((T, FOX_WIDTH), f32), jax.ShapeDtypeStruct((T, FOX_WIDTH), f32),
                   jax.ShapeDtypeStruct((T, FOX_WIDTH), MXU_DTYPE),
                   jax.ShapeDtypeStruct((B, FOX_HEADS, S, 1), f32), jax.ShapeDtypeStruct((B, FOX_HEADS, nq, 1, tq), f32),
                   jax.ShapeDtypeStruct((1, W), f32)],
        scratch_shapes=[pltpu.VMEM((2, S, hd), f32), pltpu.VMEM((2, S, hd), f32), pltpu.VMEM((2, S, 1), f32)],
        compiler_params=_cparams("arbitrary", "arbitrary", "arbitrary"),
    )(qn, kn, vb, ccol, crow, go, o_raw, lse, d_oab)


def _lane_mask(lo, hi, shape):
    lane = lax.broadcasted_iota(jnp.int32, shape, 1)
    return (lane >= lo) & (lane < hi)


def _cumsum_rows(v, period, reverse=False):
    n = v.shape[0]
    pos = lax.broadcasted_iota(jnp.int32, v.shape, 0) % period
    sh = 1
    while sh < period:
        if reverse:
            v = v + jnp.where(pos + sh < period, pltpu.roll(v, n - sh, 0), 0.0)
        else:
            v = v + jnp.where(pos >= sh, pltpu.roll(v, sh, 0), 0.0)
        sh *= 2
    return v


def _gate_values(z, bias, alog):
    zb = z + bias
    ls = jax.nn.log_sigmoid(zb)
    beta = jax.nn.sigmoid(z)
    g = -jnp.exp(alog) * jax.nn.softplus(zb)
    return zb, ls, beta, g


def gates_fwd(P, bias, alog, *, B):
    T = P.shape[0]
    S = T // B

    def body(z_ref, bias_ref, alog_ref, o_ref):
        z = z_ref[...]
        _, ls, beta, g = _gate_values(z, bias_ref[...], alog_ref[...])
        c = _cumsum_rows(ls, S)
        gc = _cumsum_rows(g, GDN_CHUNK)
        o = jnp.where(_lane_mask(SM_F, SM_F + FOX_HEADS, z.shape), c, 0.0)
        o = jnp.where(_lane_mask(SM_B, SM_B + GDN_HEADS, z.shape), beta, o)
        o = jnp.where(_lane_mask(SM_A, SM_A + GDN_HEADS, z.shape), gc, o)
        o_ref[...] = o

    vec = pl.BlockSpec((1, LANES), lambda b: (0, 0))
    return pl.pallas_call(
        body, name="gates_fwd", grid=(B,),
        in_specs=[pl.BlockSpec((S, LANES), lambda b: (b, COL_SMALL // LANES)), vec, vec],
        out_specs=pl.BlockSpec((S, LANES), lambda b: (b, 0)),
        out_shape=jax.ShapeDtypeStruct((T, LANES), f32),
        compiler_params=_cparams("parallel"),
    )(P, bias, alog)


def gates_bwd(P, bias, alog, dgates, *, B):
    T = P.shape[0]
    S = T // B

    def body(z_ref, bias_ref, alog_ref, dg_ref, dz_ref, par_ref):
        z = z_ref[...]
        zb, ls, beta, g = _gate_values(z, bias_ref[...], alog_ref[...])
        d = dg_ref[...]
        dls = _cumsum_rows(d, S, reverse=True)
        dgr = _cumsum_rows(d, GDN_CHUNK, reverse=True)
        sig = jax.nn.sigmoid(zb)
        dz_f = dls * (1.0 - sig)
        dz_b = d * beta * (1.0 - beta)
        dz_a = dgr * (-jnp.exp(alog_ref[...])) * sig
        dz = jnp.where(_lane_mask(SM_F, SM_F + FOX_HEADS, z.shape), dz_f, 0.0)
        dz = jnp.where(_lane_mask(SM_B, SM_B + GDN_HEADS, z.shape), dz_b, dz)
        dz = jnp.where(_lane_mask(SM_A, SM_A + GDN_HEADS, z.shape), dz_a, dz)
        dz_ref[...] = dz.astype(dz_ref.dtype)

        @pl.when(pl.program_id(0) == 0)
        def _():
            par_ref[...] = jnp.zeros_like(par_ref)

        dalog = jnp.where(_lane_mask(SM_A, SM_A + GDN_HEADS, z.shape), dgr * g, 0.0)
        par_ref[0:1, :] += jnp.sum(dz, axis=0, keepdims=True)
        par_ref[1:2, :] += jnp.sum(dalog, axis=0, keepdims=True)

    vec = pl.BlockSpec((1, LANES), lambda b: (0, 0))
    return pl.pallas_call(
        body, name="gates_bwd", grid=(B,),
        in_specs=[pl.BlockSpec((S, LANES), lambda b: (b, COL_SMALL // LANES)), vec, vec,
                  pl.BlockSpec((S, LANES), lambda b: (b, 0))],
        out_specs=[pl.BlockSpec((S, LANES), lambda b: (b, 0)), pl.BlockSpec((8, LANES), lambda b: (0, 0))],
        out_shape=[jax.ShapeDtypeStruct((T, LANES), MXU_DTYPE), jax.ShapeDtypeStruct((8, LANES), f32)],
        compiler_params=_cparams("arbitrary"),
    )(P, bias, alog, dgates)


GDN_BLOCKS = 3 * GDN_HEADS


def _shift_rows(v, d, reverse=False):
    if d == 0:
        return v
    n = v.shape[0]
    row = lax.broadcasted_iota(jnp.int32, v.shape, 0)
    if reverse:
        return jnp.where(row + d < n, pltpu.roll(v, n - d, 0), 0.0)
    return jnp.where(row >= d, pltpu.roll(v, d, 0), 0.0)


def _conv_silu(x, w):
    pre = sum(w[j:j + 1, :] * _shift_rows(x, CONV_WIDTH - 1 - j) for j in range(CONV_WIDTH))
    return pre, pre * jax.nn.sigmoid(pre)


def gdn_prep_fwd(P, conv_w, *, B):
    T = P.shape[0]
    S = T // B

    def body(x_ref, w_ref, o_ref):
        _, y = _conv_silu(x_ref[...], w_ref[...])
        yn = y * lax.rsqrt(jnp.sum(y * y, axis=-1, keepdims=True) + EPS)
        o_ref[...] = jnp.where(pl.program_id(1) < 2 * GDN_HEADS, yn, y)

    return pl.pallas_call(
        body, name="gdn_prep_fwd", grid=(B, GDN_BLOCKS),
        in_specs=[pl.BlockSpec((S, LANES), lambda b, j: (b, COL_GDN // LANES + j)),
                  pl.BlockSpec((CONV_WIDTH, LANES), lambda b, j: (0, j))],
        out_specs=pl.BlockSpec((S, LANES), lambda b, j: (b, j)),
        out_shape=jax.ShapeDtypeStruct((T, 3 * GDN_WIDTH), f32),
        compiler_params=_cparams("parallel", "parallel"),
    )(P, conv_w)


def gdn_prep_bwd(P, conv_w, dG, *, B):
    T = P.shape[0]
    S = T // B

    def body(x_ref, w_ref, dg_ref, dx_ref, dw_ref):
        x, w = x_ref[...], w_ref[...]
        pre, y = _conv_silu(x, w)
        dn = dg_ref[...]
        r = lax.rsqrt(jnp.sum(y * y, axis=-1, keepdims=True) + EPS)
        n = y * r
        dy_norm = r * (dn - n * jnp.sum(dn * n, axis=-1, keepdims=True))
        dy = jnp.where(pl.program_id(0) < 2 * GDN_HEADS, dy_norm, dn)
        sg = jax.nn.sigmoid(pre)
        dpre = dy * (sg * (1.0 + pre * (1.0 - sg)))
        dx = sum(w[j:j + 1, :] * _shift_rows(dpre, CONV_WIDTH - 1 - j, reverse=True) for j in range(CONV_WIDTH))
        dx_ref[...] = dx.astype(dx_ref.dtype)

        @pl.when(pl.program_id(1) == 0)
        def _():
            dw_ref[...] = jnp.zeros_like(dw_ref)

        for j in range(CONV_WIDTH):
            dw_ref[j:j + 1, :] += jnp.sum(dpre * _shift_rows(x, CONV_WIDTH - 1 - j), axis=0, keepdims=True)

    return pl.pallas_call(
        body, name="gdn_prep_bwd", grid=(GDN_BLOCKS, B),
        in_specs=[pl.BlockSpec((S, LANES), lambda j, b: (b, COL_GDN // LANES + j)),
                  pl.BlockSpec((CONV_WIDTH, LANES), lambda j, b: (0, j)),
                  pl.BlockSpec((S, LANES), lambda j, b: (b, j))],
        out_specs=[pl.BlockSpec((S, LANES), lambda j, b: (b, j)),
                   pl.BlockSpec((CONV_WIDTH, LANES), lambda j, b: (0, j))],
        out_shape=[jax.ShapeDtypeStruct((T, 3 * GDN_WIDTH), MXU_DTYPE),
                   jax.ShapeDtypeStruct((CONV_WIDTH, 3 * GDN_WIDTH), f32)],
        compiler_params=_cparams("arbitrary", "arbitrary"),
    )(P, conv_w, dG)


GDN_GROUP = 16
B_NN = (((2,), (1,)), ((0,), (0,)))
B_NT = (((2,), (2,)), ((0,), (0,)))
B_TN = (((1,), (1,)), ((0,), (0,)))


def _bmm(a, b, dims, precision=None):
    if precision is None:
        a, b = _mx(a), _mx(b)
    return lax.dot_general(a, b, dims, preferred_element_type=f32, precision=precision)


def _tri_inverse(A):
    C = A.shape[-1]
    row = lax.broadcasted_iota(jnp.int32, A.shape, 1)
    col = lax.broadcasted_iota(jnp.int32, A.shape, 2)
    eye = (row == col).astype(f32)
    X = jnp.where((row // 4) == (col // 4), -A, 0.0)
    X2 = _bmm(X, X, B_NN, INV_PRECISION)
    Tm = eye + X + X2 + _bmm(X, X2, B_NN, INV_PRECISION)
    b = 4
    while b < C:
        off = ((row // (2 * b)) == (col // (2 * b))) & ((row // b) != (col // b))
        Tm = Tm - _bmm(_bmm(Tm, jnp.where(off, A, 0.0), B_NN, INV_PRECISION), Tm, B_NN, INV_PRECISION)
        b *= 2
    return Tm


def _pick_lane(block, lane_idx):
    lane = lax.broadcasted_iota(jnp.int32, block.shape, 1)
    return jnp.sum(jnp.where(lane == lane_idx, block, 0.0), axis=1, keepdims=True)


def _gdn_local(q, k, v, beta, gc, Tm=None):
    C = GDN_CHUNK
    n = q.shape[0] // C
    q = q.reshape(n, C, -1) * (GDN_HEAD_DIM ** -0.5)
    k = k.reshape(n, C, -1)
    v = v.reshape(n, C, -1)
    beta = beta.reshape(n, C, 1)
    gc = gc.reshape(n, C, 1)
    row = lax.broadcasted_iota(jnp.int32, (n, C, C), 1)
    col = lax.broadcasted_iota(jnp.int32, (n, C, C), 2)
    gcT = jnp.swapaxes(jnp.broadcast_to(gc, (n, C, C)), 1, 2)
    D = jnp.exp(jnp.where(row >= col, gc - gcT, NEG_INF))
    kb = k * beta
    vb = v * beta
    A = jnp.where(row > col, _bmm(kb, k, B_NT) * D, 0.0)
    Gam = jnp.exp(gc)
    kg = kb * Gam
    gl = gc[:, C - 1:C, :]
    kdec = jnp.exp(gl - gc)
    loc = dict(q=q, k=k, v=v, beta=beta, gc=gc, D=D, kb=kb, vb=vb, A=A, Gam=Gam, kg=kg,
               kdec=kdec, kd=k * kdec, qg=q * Gam, gam=jnp.exp(gl), row=row, col=col)
    if Tm is None:
        Tm = _tri_inverse(A)
        loc.update(u=_bmm(Tm, vb, B_NN), w=_bmm(Tm, kg, B_NN), M=_bmm(q, k, B_NT) * D)
    else:
        Tm = Tm.reshape(n, C, C)
    loc["Tm"] = Tm
    return loc


def _gdn_store_local(loc, r0, u_s, w_s, qg_s, kd_s, M_s, gam_s, c0):
    n = loc["u"].shape[0]
    R = n * GDN_CHUNK
    u_s[pl.ds(r0, R), :] = loc["u"].reshape(R, -1)
    w_s[pl.ds(r0, R), :] = loc["w"].reshape(R, -1)
    qg_s[pl.ds(r0, R), :] = loc["qg"].reshape(R, -1)
    kd_s[pl.ds(r0, R), :] = loc["kd"].reshape(R, -1)
    M_s[pl.ds(r0, R), :] = loc["M"].reshape(R, -1)
    gam_s[pl.ds(c0, n)] = jnp.broadcast_to(loc["gam"], (n, 1, LANES))


def _gdn_specs(S):
    blk = lambda off: pl.BlockSpec((S, LANES), lambda b, h: (b, off + h))
    return blk


def gdn_fwd(G, gates, P, g_on, *, B):
    T = G.shape[0]
    S = T // B
    C = GDN_CHUNK
    N = S // C
    grp = min(GDN_GROUP, N)
    R = grp * C
    hd = GDN_HEAD_DIM

    def body(q_ref, k_ref, v_ref, gt_ref, z_ref, gon_ref, o_ref, ob_ref, st_ref, u_s, w_s, qg_s, kd_s, M_s, gam_s):
        h = pl.program_id(1)

        def local(gi, carry):
            r0 = pl.multiple_of(gi * R, R)
            gt = gt_ref[pl.ds(r0, R), :]
            loc = _gdn_local(q_ref[pl.ds(r0, R), :], k_ref[pl.ds(r0, R), :], v_ref[pl.ds(r0, R), :],
                             _pick_lane(gt, SM_B + h), _pick_lane(gt, SM_A + h))
            _gdn_store_local(loc, r0, u_s, w_s, qg_s, kd_s, M_s, gam_s, gi * grp)
            return carry

        lax.fori_loop(0, N // grp, local, 0)

        def step(n, state):
            r0 = pl.multiple_of(n * C, C)
            st_ref[0, 0, n] = state
            v_new = u_s[pl.ds(r0, C), :] - _dotm(w_s[pl.ds(r0, C), :], state, NN)
            o_ref[pl.ds(r0, C), :] = (_dotm(qg_s[pl.ds(r0, C), :], state, NN)
                                      + _dotm(M_s[pl.ds(r0, C), :], v_new, NN))
            return state * gam_s[n] + _dotm(kd_s[pl.ds(r0, C), :], v_new, TN)

        lax.fori_loop(0, N, step, jnp.zeros((hd, hd), f32))
        o = o_ref[...]
        z = z_ref[...]
        ob_ref[...] = (_head_rms(o, gon_ref[...])[0] * (z * jax.nn.sigmoid(z))).astype(ob_ref.dtype)

    blk = lambda off: pl.BlockSpec((S, LANES), lambda b, h: (b, off + h))
    rows = lambda: pltpu.VMEM((S, hd), f32)
    return pl.pallas_call(
        body, name="gdn_fwd", grid=(B, GDN_HEADS),
        in_specs=[blk(0), blk(GDN_HEADS), blk(2 * GDN_HEADS), pl.BlockSpec((S, LANES), lambda b, h: (b, 0)),
                  blk(COL_Z // LANES), pl.BlockSpec((1, hd), lambda b, h: (0, 0))],
        out_specs=[blk(0), blk(0), pl.BlockSpec((1, 1, N, hd, hd), lambda b, h: (b, h, 0, 0, 0))],
        out_shape=[jax.ShapeDtypeStruct((T, GDN_WIDTH), f32), jax.ShapeDtypeStruct((T, GDN_WIDTH), MXU_DTYPE),
                   jax.ShapeDtypeStruct((B, GDN_HEADS, N, hd, hd), f32)],
        scratch_shapes=[rows(), rows(), rows(), rows(), pltpu.VMEM((S, C), f32), pltpu.VMEM((N, 1, LANES), f32)],
        compiler_params=_cparams("parallel", "parallel"),
    )(G, G, G, gates, P, g_on)


def gdn_bwd(G, gates, P, g_on, o_raw, states, d_oab, *, B):
    T = G.shape[0]
    S = T // B
    C = GDN_CHUNK
    N = S // C
    grp = min(GDN_GROUP, N)
    R = grp * C
    hd = GDN_HEAD_DIM

    def body(q_ref, k_ref, v_ref, gt_ref, z_ref, gon_ref, o_ref, st_ref, dob_ref,
             dq_ref, dk_ref, dv_ref, dgt_ref, dz_ref, dgon_ref,
             u_s, w_s, qg_s, kd_s, M_s, gam_s, do_s, du_s, dw_s, dqg_s, dkd_s, dM_s, dgl_s, Tm_s):
        b, h = pl.program_id(0), pl.program_id(1)

        @pl.when((b == 0) & (h == 0))
        def _():
            dgon_ref[...] = jnp.zeros_like(dgon_ref)

        @pl.when(h == 0)
        def _():
            dgt_ref[...] = jnp.zeros_like(dgt_ref)

        def group_inputs(gi, Tm_of=None):
            r0 = pl.multiple_of(gi * R, R)
            gt = gt_ref[pl.ds(r0, R), :]
            Tm = None if Tm_of is None else Tm_of[pl.ds(r0, R), :]
            return r0, _gdn_local(q_ref[pl.ds(r0, R), :], k_ref[pl.ds(r0, R), :], v_ref[pl.ds(r0, R), :],
                                  _pick_lane(gt, SM_B + h), _pick_lane(gt, SM_A + h), Tm)

        def local(gi, carry):
            r0, loc = group_inputs(gi)
            _gdn_store_local(loc, r0, u_s, w_s, qg_s, kd_s, M_s, gam_s, gi * grp)
            Tm_s[pl.ds(r0, R), :] = loc["Tm"].reshape(R, C)
            o, z, gon = o_ref[pl.ds(r0, R), :], z_ref[pl.ds(r0, R), :], gon_ref[...]
            dob = dob_ref[pl.ds(r0, R), :]
            on, ro = _head_rms(o, gon)
            sz = jax.nn.sigmoid(z)
            dz_ref[pl.ds(r0, R), :] = (dob * on * (sz * (1.0 + z * (1.0 - sz)))).astype(dz_ref.dtype)
            do, dgon = _head_rms_bwd(o, ro, gon, dob * (z * sz))
            do_s[pl.ds(r0, R), :] = do
            dgon_ref[...] += dgon
            return carry

        lax.fori_loop(0, N // grp, local, 0)

        def step(t, dS):
            n = N - 1 - t
            r0 = pl.multiple_of(n * C, C)
            rows = pl.ds(r0, C)
            state = st_ref[0, 0, n]
            w_n, M_n, kd_n, do_n = w_s[rows, :], M_s[rows, :], kd_s[rows, :], do_s[rows, :]
            v_new = u_s[rows, :] - _dotm(w_n, state, NN)
            dv_new = _dotm(M_n, do_n, TN) + _dotm(kd_n, dS, NN)
            du_s[rows, :] = dv_new
            dw_s[rows, :] = -_dotm(dv_new, state, NT)
            dqg_s[rows, :] = _dotm(do_n, state, NT)
            dM_s[rows, :] = _dotm(do_n, v_new, NT)
            dkd_s[rows, :] = _dotm(v_new, dS, NT)
            gam = gam_s[n]
            dgl_s[n] = jnp.broadcast_to(jnp.sum(jnp.sum(dS * state, axis=1, keepdims=True), axis=0, keepdims=True), (1, LANES)) * gam
            return dS * gam + _dotm(qg_s[rows, :], do_n, TN) - _dotm(w_n, dv_new, TN)

        lax.fori_loop(0, N, step, jnp.zeros((hd, hd), f32))

        def finish(gi, carry):
            r0, L = group_inputs(gi, Tm_s)
            n = grp
            rows = pl.ds(r0, R)
            g3 = lambda ref: ref[rows, :].reshape(n, C, -1)
            du, dw, dqg, dkd, dM = g3(du_s), g3(dw_s), g3(dqg_s), g3(dkd_s), g3(dM_s)
            L["M"] = g3(M_s)
            TmT = jnp.swapaxes(L["Tm"], 1, 2)
            dTm = _bmm(du, L["vb"], B_NT) + _bmm(dw, L["kg"], B_NT)
            dvb = _bmm(TmT, du, B_NN)
            dkg = _bmm(TmT, dw, B_NN)
            dA = jnp.where(L["row"] > L["col"], -_bmm(_bmm(TmT, dTm, B_NN), TmT, B_NN), 0.0)
            dKK = dA * L["D"]
            dQK = dM * L["D"]
            dkb = _bmm(dKK, L["k"], B_NN) + dkg * L["Gam"]
            dk = (_bmm(dKK, L["kb"], B_TN) + _bmm(dQK, L["q"], B_TN) + dkd * L["kdec"] + L["beta"] * dkb)
            dq = (_bmm(dQK, L["k"], B_NN) + dqg * L["Gam"]) * (GDN_HEAD_DIM ** -0.5)
            E = dA * L["A"] + dM * L["M"]
            r = jnp.sum(dkd * L["kd"], axis=-1, keepdims=True)
            dgc = (jnp.sum(E, axis=2, keepdims=True) - jnp.sum(jnp.swapaxes(E, 1, 2), axis=2, keepdims=True)
                   + jnp.sum(dkg * L["kg"], axis=-1, keepdims=True) + jnp.sum(dqg * L["qg"], axis=-1, keepdims=True) - r)
            dgl = jnp.sum(r, axis=1, keepdims=True) + dgl_s[pl.ds(gi * n, n)][:, :, 0:1]
            rowc = lax.broadcasted_iota(jnp.int32, (n, C, 1), 1)
            dgc = dgc + jnp.where(rowc == C - 1, dgl, 0.0)
            dbeta = jnp.sum(dkb * L["k"], axis=-1, keepdims=True) + jnp.sum(dvb * L["v"], axis=-1, keepdims=True)
            dq_ref[rows, :] = dq.reshape(R, hd)
            dk_ref[rows, :] = dk.reshape(R, hd)
            dv_ref[rows, :] = (L["beta"] * dvb).reshape(R, hd)
            lane = lax.broadcasted_iota(jnp.int32, (R, LANES), 1)
            dgt_ref[rows, :] += (jnp.where(lane == SM_B + h, dbeta.reshape(R, 1), 0.0)
                                 + jnp.where(lane == SM_A + h, dgc.reshape(R, 1), 0.0))
            return carry

        lax.fori_loop(0, N // grp, finish, 0)

    blk = lambda off: pl.BlockSpec((S, LANES), lambda b, h: (b, off + h))
    rows = lambda: pltpu.VMEM((S, hd), f32)
    return pl.pallas_call(
        body, name="gdn_bwd", grid=(B, GDN_HEADS),
        in_specs=[blk(0), blk(GDN_HEADS), blk(2 * GDN_HEADS), pl.BlockSpec((S, LANES), lambda b, h: (b, 0)),
                  blk(COL_Z // LANES), pl.BlockSpec((1, hd), lambda b, h: (0, 0)), blk(0),
                  pl.BlockSpec((1, 1, N, hd, hd), lambda b, h: (b, h, 0, 0, 0)), blk(GDN_HEADS)],
        out_specs=[blk(0), blk(0), blk(0), pl.BlockSpec((S, LANES), lambda b, h: (b, 0)), blk(0),
                   pl.BlockSpec((1, hd), lambda b, h: (0, 0))],
        out_shape=[jax.ShapeDtypeStruct((T, GDN_WIDTH), f32), jax.ShapeDtypeStruct((T, GDN_WIDTH), f32),
                   jax.ShapeDtypeStruct((T, GDN_WIDTH), f32), jax.ShapeDtypeStruct((T, LANES), f32),
                   jax.ShapeDtypeStruct((T, GDN_WIDTH), MXU_DTYPE), jax.ShapeDtypeStruct((1, hd), f32)],
        scratch_shapes=[rows(), rows(), rows(), rows(), pltpu.VMEM((S, C), f32), pltpu.VMEM((N, 1, LANES), f32),
                        rows(), rows(), rows(), rows(), rows(), pltpu.VMEM((S, C), f32), pltpu.VMEM((N, 1, LANES), f32),
                        pltpu.VMEM((S, C), f32)],
        compiler_params=_cparams("arbitrary", "arbitrary"),
    )(G, G, G, gates, P, g_on, o_raw, states, d_oab)


IN_SPLIT = (0, 1536, 1544, 3080, 3088, 3600)


IN_SHARD = IN_DIM // 4
IN_SHARD_PAD = 928


def align_w_in_t(wt):
    s = IN_SPLIT
    pad = jnp.zeros((IN_ALIGNED - IN_DIM, wt.shape[1]), wt.dtype)
    return jnp.concatenate([wt[s[0]:s[1]], wt[s[2]:s[3]], wt[s[4]:s[5]], wt[s[1]:s[2]], wt[s[3]:s[4]], pad], axis=0)


def unalign_w_in_t(wa):
    return jnp.concatenate([wa[0:1536], wa[COL_SMALL:COL_SMALL + 8], wa[1536:3072],
                            wa[COL_SMALL + 8:COL_SMALL + 16], wa[3072:3584]], axis=0)


def _lanes_vec(pieces):
    v = jnp.zeros((1, LANES), f32)
    for off, a in pieces:
        v = lax.dynamic_update_slice(v, a.astype(f32), (0, off))
    return v


def local_step(x, mem, target, w, sp, *, B):
    T = x.shape[0]
    S = T // B
    gq8, gk8 = jnp.tile(sp["fox_qnorm_g"], (1, FOX_HEADS)), jnp.tile(sp["fox_knorm_g"], (1, FOX_HEADS))
    go2 = jnp.tile(sp["fox_onorm_g"], (1, 2))
    bias = _lanes_vec([(SM_F, sp["fox_f_bias"]), (SM_A, sp["gdn_dt_bias"])])
    alog = _lanes_vec([(SM_A, sp["gdn_A_log"])])

    h1 = rms_fwd(x, sp["norm_mix_g"], name="rms_mix")
    P = matmul(h1, w["wa_t"], tb=True, name="mm_in", tn=IN_TILE)
    gates = gates_fwd(P, bias, alog, B=B)
    c = gates[:, SM_F:SM_F + FOX_HEADS].reshape(B, S, FOX_HEADS).transpose(0, 2, 1)
    ccol, crow = c[..., None], c.reshape(B, FOX_HEADS, S // FOX_TQ, 1, FOX_TQ)
    qn, kn, vb = fox_prep_fwd(P, gq8, gk8)
    o_raw, o_a, lse = fox_core_fwd(qn, kn, vb, ccol, crow, go2, B=B)
    G = gdn_prep_fwd(P, w["conv_w"], B=B)
    ob_raw, o_b, states = gdn_fwd(G, gates, P, sp["gdn_onorm_g"], B=B)
    oab = jnp.concatenate([o_a, o_b], axis=1)
    if "late" in w:
        w = {**w, **w["late"](oab)}
    x2 = matmul(oab, w["w_out"], residual=x, name="mm_out")
    hq = rms_fwd(x2, sp["norm_xattn_g"], name="rms_xattn")
    hm = rms_fwd(mem, sp["mem_norm_g"], name="rms_mem")
    cq = matmul(hq, w["w_cq"], name="mm_cq")
    ckv = matmul(hm, w["w_ckv"], name="mm_ckv")
    co = xattn_fwd(cq, ckv, sp["xattn_qnorm_g"], sp["xattn_knorm_g"], B=B)
    x3 = matmul(co, w["w_co"], b_stacked=True, residual=x2, name="mm_co")
    hf = rms_fwd(x3, sp["norm_mlp_g"], name="rms_mlp")
    act = matmul(hf, w["w_mlp1"], b_stacked=True, relu2_out=True, out_dtype=MXU_DTYPE, name="mm_mlp1")
    dy, loss = matmul_rows(act, w["w_mlp2"], (x3, target), mode="loss", name="mm_mlp2_loss")

    da = matmul(dy, w["w_mlp2"], tb=True, relu2_bwd_aux=act, out_dtype=MXU_DTYPE, name="mm_d_act")
    g_mlp2 = matmul(act, dy, ta=True, out_dtype=WIRE_DTYPE, name="mm_g_mlp2")
    g_mlp1 = matmul(hf, da, ta=True, out_stacked=True, out_dtype=WIRE_DTYPE, name="mm_g_mlp1")
    by_rows = lambda g: g.reshape(N_CHIPS, g.shape[0] // N_CHIPS, g.shape[1])
    early = w.get("grads_ready", lambda grads: jnp.zeros((1, 1), f32))
    tok = early(dict(w_mlp1=g_mlp1, w_mlp2=by_rows(g_mlp2)))[0, 0]
    dx3, g_norm_mlp = matmul_rows(da, w["w_mlp1"], (x3, sp["norm_mlp_g"] + tok, dy), mode="rms_bwd", tb=True,
                                  b_stacked=True, name="mm_d_hf_rms")
    dco = matmul(dx3, w["w_co"], tb=True, b_stacked=True, name="mm_d_co")
    g_co = matmul(co, dx3, ta=True, out_stacked=True, out_dtype=WIRE_DTYPE, name="mm_g_co")
    dcq, dckv, g_xq, g_xk = xattn_bwd(cq, ckv, sp["xattn_qnorm_g"], sp["xattn_knorm_g"], dco, B=B)
    g_cq = matmul(hq, dcq, ta=True, out_dtype=WIRE_DTYPE, name="mm_g_cq")
    g_ckv = matmul(hm, dckv, ta=True, out_dtype=WIRE_DTYPE, name="mm_g_ckv")
    _, g_mem_norm = matmul_rows(dckv, w["w_ckv"], (mem, sp["mem_norm_g"], None), mode="rms_bwd", tb=True, name="mm_d_hm_rms")
    dx2, g_norm_xattn = matmul_rows(dcq, w["w_cq"], (x2, sp["norm_xattn_g"], dx3), mode="rms_bwd", tb=True, name="mm_d_hq_rms")
    doab = matmul(dx2, w["w_out"], tb=True, name="mm_d_oab")
    g_out = matmul(oab, dx2, ta=True, out_dtype=WIRE_DTYPE, name="mm_g_out")
    tok = early(dict(w_co=g_co, w_cq=by_rows(g_cq), w_ckv=by_rows(g_ckv), w_out=by_rows(g_out)))[0, 0]
    dqn, dkn, dv_f, dccol, dcrow, dgo2 = fox_core_bwd(qn, kn, vb, ccol, crow, go2 + tok, o_raw, lse, doab, B=B)
    dq_f, dk_f, dgq8, dgk8 = fox_prep_bwd(P, gq8, gk8, dqn, dkn)
    dGq, dGk, dGv, dgt, dz, g_gdn_on = gdn_bwd(G, gates, P, sp["gdn_onorm_g"], ob_raw, states, doab, B=B)
    dPg, g_conv = gdn_prep_bwd(P, w["conv_w"], jnp.concatenate([dGq, dGk, dGv], axis=1), B=B)
    dc = (dccol[..., 0] + dcrow.reshape(B, FOX_HEADS, S)).transpose(0, 2, 1).reshape(T, FOX_HEADS)
    dgates = dgt + jnp.pad(dc, ((0, 0), (SM_F, LANES - SM_F - FOX_HEADS)))
    dsmall, par = gates_bwd(P, bias, alog, dgates, B=B)
    dP = jnp.concatenate([dq_f, dk_f, dv_f, dPg, dz, dsmall, jnp.zeros((T, IN_ALIGNED - COL_SMALL - LANES), MXU_DTYPE)], axis=1)
    g_wa = matmul(dP, h1, ta=True, out_dtype=WIRE_DTYPE, name="mm_g_in", tm=IN_TILE)
    dx, g_norm_mix = matmul_rows(dP, w["wa_t"], (x, sp["norm_mix_g"], dx2), mode="rms_bwd", tk=IN_TILE, name="mm_d_h1_rms")

    fold = lambda g: jnp.sum(g.reshape(-1, FOX_HEAD_DIM), axis=0, keepdims=True)
    g_in = jnp.pad(unalign_w_in_t(g_wa).reshape(N_CHIPS, IN_SHARD, D_MODEL), ((0, 0), (0, IN_SHARD_PAD - IN_SHARD), (0, 0)))
    big = dict(w_in=g_in, w_out=by_rows(g_out), w_cq=by_rows(g_cq), w_ckv=by_rows(g_ckv), w_co=g_co, w_mlp1=g_mlp1,
               w_mlp2=by_rows(g_mlp2))
    small = dict(norm_mix_g=g_norm_mix, fox_qnorm_g=fold(dgq8), fox_knorm_g=fold(dgk8),
                 fox_f_bias=par[0:1, SM_F:SM_F + FOX_HEADS], fox_onorm_g=fold(dgo2), gdn_conv_w=g_conv,
                 gdn_A_log=par[1:2, SM_A:SM_A + GDN_HEADS], gdn_dt_bias=par[0:1, SM_A:SM_A + GDN_HEADS],
                 gdn_onorm_g=g_gdn_on, norm_xattn_g=g_norm_xattn, mem_norm_g=g_mem_norm,
                 xattn_qnorm_g=g_xq, xattn_knorm_g=g_xk, norm_mlp_g=g_norm_mlp)
    return loss, dx, big, small


MESH_IDS = pl.DeviceIdType.MESH
N_CHIPS = 4
HBM_SPEC = pl.BlockSpec(memory_space=pltpu.HBM)
PACK_ROWS = 30720
PACK_HALF = PACK_ROWS // 2
PACK_BLOCK = 3072


def _place():
    return lax.axis_index("x"), lax.axis_index("y"), lax.axis_index("c")


def _other_chips(x, y):
    return [(1 - x, y), (x, 1 - y), (1 - x, 1 - y)]


def _remote(src, dst, send_sem, recv_sem, to):
    return pltpu.make_async_remote_copy(src_ref=src, dst_ref=dst, send_sem=send_sem, recv_sem=recv_sem,
                                        device_id=to, device_id_type=MESH_IDS)


def all_gather_shards(packed):
    half = PACK_HALF

    def body(src_ref, out_ref, send_sems, recv_sems):
        x, y, c = _place()
        me_chip = 2 * x + y
        sibling = (x, y, 1 - c)
        chips = _other_chips(x, y)

        def rows(chip, core):
            return out_ref.at[chip, pl.ds(core * half, half), :]

        sends = [_remote(src_ref.at[pl.ds(c * half, half), :], rows(me_chip, c), send_sems.at[j], recv_sems.at[j], (px, py, c))
                 for j, (px, py) in enumerate(chips)]
        for cp in sends:
            cp.start()
        passed = []
        for j, (px, py) in enumerate(chips):
            theirs = rows(2 * px + py, c)
            _remote(theirs, theirs, send_sems.at[j], recv_sems.at[j], (px, py, c)).wait_recv()
            cp = _remote(theirs, theirs, send_sems.at[3 + j], recv_sems.at[3 + j], sibling)
            cp.start()
            passed.append(cp)
        for j, (px, py) in enumerate(chips):
            theirs = rows(2 * px + py, 1 - c)
            _remote(theirs, theirs, send_sems.at[3 + j], recv_sems.at[3 + j], sibling).wait_recv()
        for cp in sends + passed:
            cp.wait_send()

    return pl.pallas_call(
        body, name="all_gather_shards", in_specs=[HBM_SPEC], out_specs=HBM_SPEC,
        out_shape=jax.ShapeDtypeStruct((N_CHIPS,) + packed.shape, packed.dtype),
        scratch_shapes=[pltpu.SemaphoreType.DMA((6,)), pltpu.SemaphoreType.DMA((6,))],
    )(packed)


def exchange_core_halves(G):
    half = PACK_HALF

    def body(g_ref, land_ref, send_sem, recv_sem):
        x, y, c = _place()
        cp = _remote(g_ref.at[:, pl.ds((1 - c) * half, half), :], land_ref, send_sem, recv_sem, (x, y, 1 - c))
        cp.start()
        cp.wait()

    return pl.pallas_call(
        body, name="exchange_core_halves", in_specs=[HBM_SPEC], out_specs=HBM_SPEC,
        out_shape=jax.ShapeDtypeStruct((N_CHIPS, half, LANES), G.dtype),
        scratch_shapes=[pltpu.SemaphoreType.DMA(()), pltpu.SemaphoreType.DMA(())],
    )(G)


def add_core_halves(G, land, core):
    nb = PACK_HALF // PACK_BLOCK

    def body(c_ref, g_ref, l_ref, o_ref):
        o_ref[...] = (g_ref[...].astype(f32) + l_ref[...].astype(f32)).astype(o_ref.dtype)

    blk = (1, PACK_BLOCK, LANES)
    return pl.pallas_call(
        body, name="add_core_halves",
        grid_spec=pltpu.PrefetchScalarGridSpec(
            num_scalar_prefetch=1, grid=(N_CHIPS, nb),
            in_specs=[pl.BlockSpec(blk, lambda k, i, c_ref: (k, c_ref[0] * nb + i, 0)),
                      pl.BlockSpec(blk, lambda k, i, c_ref: (k, i, 0))],
            out_specs=pl.BlockSpec(blk, lambda k, i, c_ref: (k, i, 0))),
        out_shape=jax.ShapeDtypeStruct(land.shape, land.dtype),
        compiler_params=_cparams("parallel", "parallel"),
    )(core, G, land)


def scatter_to_chips(part):
    def body(p_ref, land_ref, send_sems, recv_sems):
        x, y, c = _place()
        me_chip = 2 * x + y
        chips = _other_chips(x, y)
        sends = [_remote(p_ref.at[2 * px + py], land_ref.at[me_chip], send_sems.at[j], recv_sems.at[j], (px, py, c))
                 for j, (px, py) in enumerate(chips)]
        for cp in sends:
            cp.start()
        for j, (px, py) in enumerate(chips):
            slot = land_ref.at[2 * px + py]
            _remote(slot, slot, send_sems.at[j], recv_sems.at[j], (px, py, c)).wait_recv()
        for cp in sends:
            cp.wait_send()

    return pl.pallas_call(
        body, name="scatter_to_chips", in_specs=[HBM_SPEC], out_specs=HBM_SPEC,
        out_shape=jax.ShapeDtypeStruct(part.shape, part.dtype),
        scratch_shapes=[pltpu.SemaphoreType.DMA((3,)), pltpu.SemaphoreType.DMA((3,))],
    )(part)


def sum_chips(part, land, order):
    nb = PACK_HALF // PACK_BLOCK

    def body(order_ref, p_ref, l1_ref, l2_ref, l3_ref, o_ref):
        o_ref[...] = ((p_ref[0].astype(f32) + l1_ref[0].astype(f32)) + l2_ref[0].astype(f32)) + l3_ref[0].astype(f32)

    slot = lambda j: pl.BlockSpec((1, PACK_BLOCK, LANES), lambda i, order_ref: (order_ref[j], i, 0))
    return pl.pallas_call(
        body, name="sum_chips",
        grid_spec=pltpu.PrefetchScalarGridSpec(
            num_scalar_prefetch=1, grid=(nb,), in_specs=[slot(0), slot(1), slot(2), slot(3)],
            out_specs=pl.BlockSpec((PACK_BLOCK, LANES), lambda i, order_ref: (i, 0))),
        out_shape=jax.ShapeDtypeStruct((PACK_HALF, LANES), f32),
        compiler_params=_cparams("parallel"),
    )(order, part, land, land, land)


def swap_core_halves(red):
    def body(r_ref, out_ref, send_sem, recv_sem):
        x, y, c = _place()
        cp = _remote(r_ref, out_ref, send_sem, recv_sem, (x, y, 1 - c))
        cp.start()
        cp.wait()

    return pl.pallas_call(
        body, name="swap_core_halves", in_specs=[HBM_SPEC], out_specs=HBM_SPEC,
        out_shape=jax.ShapeDtypeStruct(red.shape, red.dtype),
        scratch_shapes=[pltpu.SemaphoreType.DMA(()), pltpu.SemaphoreType.DMA(())],
    )(red)


def _half(ref, core):
    rows = ref.shape[-2] // 2
    return ref.at[(slice(None),) * (len(ref.shape) - 2) + (pl.ds(core * rows, rows), slice(None))]


def gather_weights(shards, conv):
    n = len(shards)

    def body(*refs):
        src, conv_src = refs[:n], refs[n]
        out, conv_out = refs[n + 1:2 * n + 1], refs[2 * n + 1]
        send_sems, recv_sems = refs[2 * n + 2], refs[2 * n + 3]
        x, y, c = _place()
        me_chip = 2 * x + y
        sibling = (x, y, 1 - c)
        chips = _other_chips(x, y)
        sends = []
        for a in range(n):
            for j, (px, py) in enumerate(chips):
                sends.append(_remote(_half(src[a], c), _half(out[a].at[me_chip], c),
                                     send_sems.at[6 * a + j], recv_sems.at[6 * a + j], (px, py, c)))
        for j, (px, py) in enumerate(chips):
            sends.append(_remote(conv_src, conv_out.at[me_chip], send_sems.at[6 * n + j], recv_sems.at[6 * n + j], (px, py, c)))
        for cp in sends:
            cp.start()
        passed = []
        for a in range(n):
            for j, (px, py) in enumerate(chips):
                theirs = _half(out[a].at[2 * px + py], c)
                _remote(theirs, theirs, send_sems.at[6 * a + j], recv_sems.at[6 * a + j], (px, py, c)).wait_recv()
                cp = _remote(theirs, theirs, send_sems.at[6 * a + 3 + j], recv_sems.at[6 * a + 3 + j], sibling)
                cp.start()
                passed.append(cp)
        for j, (px, py) in enumerate(chips):
            theirs = conv_out.at[2 * px + py]
            _remote(theirs, theirs, send_sems.at[6 * n + j], recv_sems.at[6 * n + j], (px, py, c)).wait_recv()
        for a in range(n):
            for j, (px, py) in enumerate(chips):
                theirs = _half(out[a].at[2 * px + py], 1 - c)
                _remote(theirs, theirs, send_sems.at[6 * a + 3 + j], recv_sems.at[6 * a + 3 + j], sibling).wait_recv()
        for cp in sends + passed:
            cp.wait_send()

    return pl.pallas_call(
        body, name="gather_weights", in_specs=[HBM_SPEC] * (n + 1), out_specs=[HBM_SPEC] * (n + 1),
        out_shape=[jax.ShapeDtypeStruct((N_CHIPS,) + s.shape, s.dtype) for s in list(shards) + [conv]],
        scratch_shapes=[pltpu.SemaphoreType.DMA((6 * n + 3,)), pltpu.SemaphoreType.DMA((6 * n + 3,))],
    )(*shards, conv)


SEM_SPEC = pl.BlockSpec(memory_space=pltpu.SEMAPHORE)
SPLIT_EFFECT = pltpu.SideEffectType.DATAFLOW_SIDE_EFFECTING


def _gather_async_copies(src, land, send_sems, recv_sems, x, y, c):
    me_chip = 2 * x + y
    sends, arrivals = [], []
    for a in range(len(src)):
        for j, (px, py) in enumerate(_other_chips(x, y)):
            for core in range(2):
                sends.append(_remote(_half(src[a], c), _half(land[a].at[me_chip], c), send_sems.at[6 * a + 2 * j + core],
                                     recv_sems.at[6 * a + 2 * j + c], (px, py, core)))
                theirs = _half(land[a].at[2 * px + py], core)
                arrivals.append(_remote(theirs, theirs, send_sems.at[6 * a + 2 * j + core],
                                        recv_sems.at[6 * a + 2 * j + core], (px, py, core)))
    return sends, arrivals


def gather_weights_start(shards, after):
    n = len(shards)

    def body(*refs):
        src, land = refs[:n], refs[n:2 * n]
        send_sems, recv_sems, token = refs[2 * n + 1], refs[2 * n + 2], refs[4 * n + 3]
        x, y, c = _place()
        for cp in _gather_async_copies(src, land, send_sems, recv_sems, x, y, c)[0]:
            cp.start()
        token[...] = jnp.zeros_like(token)

    zones = [pltpu.with_memory_space_constraint(lax.empty((N_CHIPS,) + s.shape, s.dtype), pltpu.HBM) for s in shards]
    srcs = [pltpu.with_memory_space_constraint(s, pltpu.HBM) for s in shards]
    out = pl.pallas_call(
        body, name="gather_weights_start",
        out_shape=[pltpu.SemaphoreType.DMA((6 * n,)), pltpu.SemaphoreType.DMA((6 * n,))]
        + [pltpu.HBM(s.shape, s.dtype) for s in shards] + [pltpu.HBM(z.shape, z.dtype) for z in zones]
        + [jax.ShapeDtypeStruct((8, LANES), f32)],
        in_specs=[HBM_SPEC] * (2 * n) + [pl.BlockSpec(memory_space=pl.ANY)],
        out_specs=[SEM_SPEC, SEM_SPEC] + [HBM_SPEC] * (2 * n) + [pl.BlockSpec(memory_space=pltpu.VMEM)],
        input_output_aliases={i: 2 + i for i in range(2 * n)},
        compiler_params=pltpu.CompilerParams(has_side_effects=SPLIT_EFFECT),
    )(*srcs, *zones, after)
    return out[0], out[1], out[2:2 + n], out[2 + n:2 + 2 * n], out[-1]


def gather_weights_wait(send_sems, recv_sems, shards, zones, after):
    n = len(shards)

    def body(*refs):
        src, land = refs[:n], refs[n:2 * n]
        send_sems, recv_sems = refs[2 * n], refs[2 * n + 1]
        x, y, c = _place()
        sends, arrivals = _gather_async_copies(src, land, send_sems, recv_sems, x, y, c)
        for cp in sends:
            cp.wait_send()
        for cp in arrivals:
            cp.wait_recv()

    out = pl.pallas_call(
        body, name="gather_weights_wait",
        out_shape=[pltpu.HBM(s.shape, s.dtype) for s in shards] + [pltpu.HBM(z.shape, z.dtype) for z in zones],
        in_specs=[HBM_SPEC] * (2 * n) + [SEM_SPEC, SEM_SPEC, pl.BlockSpec(memory_space=pl.ANY)],
        out_specs=[HBM_SPEC] * (2 * n),
        input_output_aliases={i: i for i in range(2 * n)},
        compiler_params=pltpu.CompilerParams(has_side_effects=SPLIT_EFFECT),
    )(*shards, *zones, send_sems, recv_sems, after)
    return out[n:]


def swap_grad_halves(grads, *, name):
    n = len(grads)

    def body(*refs):
        g, land, send_sems, recv_sems = refs[:n], refs[n:2 * n], refs[2 * n], refs[2 * n + 1]
        x, y, c = _place()
        copies = [_remote(_half(g[a], 1 - c), land[a], send_sems.at[a], recv_sems.at[a], (x, y, 1 - c)) for a in range(n)]
        for cp in copies:
            cp.start()
        for cp in copies:
            cp.wait()

    return pl.pallas_call(
        body, name=name, in_specs=[HBM_SPEC] * n, out_specs=[HBM_SPEC] * n,
        out_shape=[jax.ShapeDtypeStruct((N_CHIPS, g.shape[1] // 2, g.shape[2]), g.dtype) for g in grads],
        scratch_shapes=[pltpu.SemaphoreType.DMA((n,)), pltpu.SemaphoreType.DMA((n,))],
    )(*grads)


GRAD_ROWS = 256


def add_grad_halves(g, land, core, *, name):
    _, half, cols = land.shape
    tr = GRAD_ROWS if half % GRAD_ROWS == 0 else half
    nb = half // tr

    def body(c_ref, g_ref, l_ref, o_ref):
        o_ref[...] = (g_ref[...].astype(f32) + l_ref[...].astype(f32)).astype(o_ref.dtype)

    blk = (1, tr, cols)
    return pl.pallas_call(
        body, name=name,
        grid_spec=pltpu.PrefetchScalarGridSpec(
            num_scalar_prefetch=1, grid=(N_CHIPS, nb),
            in_specs=[pl.BlockSpec(blk, lambda k, i, c_ref: (k, c_ref[0] * nb + i, 0)),
                      pl.BlockSpec(blk, lambda k, i, c_ref: (k, i, 0))],
            out_specs=pl.BlockSpec(blk, lambda k, i, c_ref: (k, i, 0))),
        out_shape=jax.ShapeDtypeStruct(land.shape, land.dtype),
        compiler_params=_cparams("parallel", "parallel"),
    )(core, g, land)


def scatter_grads(parts):
    n = len(parts)

    def body(*refs):
        p, land, send_sems, recv_sems = refs[:n], refs[n:2 * n], refs[2 * n], refs[2 * n + 1]
        x, y, c = _place()
        me_chip = 2 * x + y
        chips = _other_chips(x, y)
        sends = [_remote(p[a].at[2 * px + py], land[a].at[me_chip], send_sems.at[3 * a + j], recv_sems.at[3 * a + j], (px, py, c))
                 for a in range(n) for j, (px, py) in enumerate(chips)]
        for cp in sends:
            cp.start()
        for a in range(n):
            for j, (px, py) in enumerate(chips):
                slot = land[a].at[2 * px + py]
                _remote(slot, slot, send_sems.at[3 * a + j], recv_sems.at[3 * a + j], (px, py, c)).wait_recv()
        for cp in sends:
            cp.wait_send()

    return pl.pallas_call(
        body, name="scatter_grads", in_specs=[HBM_SPEC] * n, out_specs=[HBM_SPEC] * n,
        out_shape=[jax.ShapeDtypeStruct(p.shape, p.dtype) for p in parts],
        scratch_shapes=[pltpu.SemaphoreType.DMA((3 * n,)), pltpu.SemaphoreType.DMA((3 * n,))],
    )(*parts)


def _scatter_async_copies(parts, land, send_sems, recv_sems, x, y, c):
    me_chip = 2 * x + y
    sends, arrivals = [], []
    for a in range(len(parts)):
        for j, (px, py) in enumerate(_other_chips(x, y)):
            sems = (send_sems.at[3 * a + j], recv_sems.at[3 * a + j], (px, py, c))
            sends.append(_remote(parts[a].at[2 * px + py], land[a].at[me_chip], *sems))
            slot = land[a].at[2 * px + py]
            arrivals.append(_remote(slot, slot, *sems))
    return sends, arrivals


def scatter_grads_start(parts, *, name):
    n = len(parts)

    def body(*refs):
        p, land = refs[:n], refs[n:2 * n]
        send_sems, recv_sems, token = refs[2 * n], refs[2 * n + 1], refs[4 * n + 2]
        x, y, c = _place()
        for cp in _scatter_async_copies(p, land, send_sems, recv_sems, x, y, c)[0]:
            cp.start()
        token[...] = jnp.zeros_like(token)

    zones = [pltpu.with_memory_space_constraint(lax.empty(p.shape, p.dtype), pltpu.HBM) for p in parts]
    srcs = [pltpu.with_memory_space_constraint(p, pltpu.HBM) for p in parts]
    hbm = [pltpu.HBM(p.shape, p.dtype) for p in parts]
    out = pl.pallas_call(
        body, name=name,
        out_shape=[pltpu.SemaphoreType.DMA((3 * n,)), pltpu.SemaphoreType.DMA((3 * n,))] + hbm + hbm
        + [jax.ShapeDtypeStruct((8, LANES), f32)],
        in_specs=[HBM_SPEC] * (2 * n),
        out_specs=[SEM_SPEC, SEM_SPEC] + [HBM_SPEC] * (2 * n) + [pl.BlockSpec(memory_space=pltpu.VMEM)],
        input_output_aliases={i: 2 + i for i in range(2 * n)},
        compiler_params=pltpu.CompilerParams(has_side_effects=SPLIT_EFFECT),
    )(*srcs, *zones)
    return out[0], out[1], out[2:2 + n], out[2 + n:2 + 2 * n], out[-1]


def scatter_grads_wait(send_sems, recv_sems, parts, zones, after, *, name):
    n = len(parts)

    def body(*refs):
        p, land = refs[:n], refs[n:2 * n]
        x, y, c = _place()
        sends, arrivals = _scatter_async_copies(p, land, refs[2 * n], refs[2 * n + 1], x, y, c)
        for cp in sends:
            cp.wait_send()
        for cp in arrivals:
            cp.wait_recv()

    hbm = [pltpu.HBM(p.shape, p.dtype) for p in parts]
    out = pl.pallas_call(
        body, name=name, out_shape=hbm + hbm,
        in_specs=[HBM_SPEC] * (2 * n) + [SEM_SPEC, SEM_SPEC, pl.BlockSpec(memory_space=pl.ANY)],
        out_specs=[HBM_SPEC] * (2 * n),
        input_output_aliases={i: i for i in range(2 * n)},
        compiler_params=pltpu.CompilerParams(has_side_effects=SPLIT_EFFECT),
    )(*parts, *zones, send_sems, recv_sems, after)
    return out[:n], out[n:]


def sum_grads(part, land, order, *, name):
    _, half, cols = part.shape
    tr = GRAD_ROWS if half % GRAD_ROWS == 0 else half

    def body(order_ref, p_ref, l1_ref, l2_ref, l3_ref, o_ref):
        o_ref[...] = ((p_ref[0].astype(f32) + l1_ref[0].astype(f32)) + l2_ref[0].astype(f32)) + l3_ref[0].astype(f32)

    slot = lambda j: pl.BlockSpec((1, tr, cols), lambda i, order_ref: (order_ref[j], i, 0))
    return pl.pallas_call(
        body, name=name,
        grid_spec=pltpu.PrefetchScalarGridSpec(
            num_scalar_prefetch=1, grid=(half // tr,), in_specs=[slot(0), slot(1), slot(2), slot(3)],
            out_specs=pl.BlockSpec((tr, cols), lambda i, order_ref: (i, 0))),
        out_shape=jax.ShapeDtypeStruct((half, cols), f32),
        compiler_params=_cparams("parallel"),
    )(order, part, land, land, land)


def swap_reduced_halves(mine):
    n = len(mine)

    def body(*refs):
        r, out, send_sems, recv_sems = refs[:n], refs[n:2 * n], refs[2 * n], refs[2 * n + 1]
        x, y, c = _place()
        copies = [_remote(r[a], out[a], send_sems.at[a], recv_sems.at[a], (x, y, 1 - c)) for a in range(n)]
        for cp in copies:
            cp.start()
        for cp in copies:
            cp.wait()

    return pl.pallas_call(
        body, name="swap_reduced_halves", in_specs=[HBM_SPEC] * n, out_specs=[HBM_SPEC] * n,
        out_shape=[jax.ShapeDtypeStruct(r.shape, r.dtype) for r in mine],
        scratch_shapes=[pltpu.SemaphoreType.DMA((n,)), pltpu.SemaphoreType.DMA((n,))],
    )(*mine)


def adamw_halves(w, mine, theirs, m, v, core, *, name):
    R, C = w.shape
    tr = min(GRAD_ROWS, R // 2)
    half_nb = R // 2 // tr

    def body(c_ref, w_ref, a_ref, b_ref, m_ref, v_ref, g_ref, d_ref, nm_ref, nv_ref):
        low = pl.program_id(0) < half_nb
        gv = jnp.where(low == (c_ref[0] == 0), a_ref[...], b_ref[...])
        nm = ADAM_B1 * m_ref[...] + (1.0 - ADAM_B1) * gv
        nv = ADAM_B2 * v_ref[...] + (1.0 - ADAM_B2) * jnp.square(gv)
        m_hat = nm / (1.0 - ADAM_B1 ** ADAM_STEP)
        v_hat = nv / (1.0 - ADAM_B2 ** ADAM_STEP)
        g_ref[...] = gv
        d_ref[...] = -ADAM_LR * (m_hat / (jnp.sqrt(v_hat) + ADAM_EPS) + ADAM_WD * w_ref[...])
        nm_ref[...] = nm
        nv_ref[...] = nv

    full = pl.BlockSpec((tr, C), lambda i, c_ref: (i, 0))
    part = pl.BlockSpec((tr, C), lambda i, c_ref: (i % half_nb, 0))
    out = jax.ShapeDtypeStruct((R, C), f32)
    return pl.pallas_call(
        body, name=name,
        grid_spec=pltpu.PrefetchScalarGridSpec(
            num_scalar_prefetch=1, grid=(2 * half_nb,), in_specs=[full, part, part, full, full], out_specs=[full] * 4),
        out_shape=[out] * 4, compiler_params=_cparams("parallel"),
    )(core, w, mine, theirs, m, v)


N_DEV = 8


def all_reduce_small(v):
    def body(src_ref, out_ref, land_ref, send_sems, recv_sems):
        x, y, c = _place()
        me = 4 * x + 2 * y + c
        copies = []
        for r in range(1, N_DEV):
            peer = ((1 - x) if r & 4 else x, (1 - y) if r & 2 else y, (1 - c) if r & 1 else c)
            copies.append(_remote(src_ref, land_ref.at[r], send_sems.at[r - 1], recv_sems.at[r - 1], peer))
        for cp in copies:
            cp.start()
        land_ref[0] = src_ref[...]
        for cp in copies:
            cp.wait()
        acc = land_ref[me]
        for d in range(1, N_DEV):
            acc = acc + land_ref[jnp.bitwise_xor(me, d)]
        out_ref[...] = acc

    vm = pl.BlockSpec(memory_space=pltpu.VMEM)
    return pl.pallas_call(
        body, name="all_reduce_small", in_specs=[vm], out_specs=vm,
        out_shape=jax.ShapeDtypeStruct(v.shape, v.dtype),
        scratch_shapes=[pltpu.VMEM((N_DEV,) + v.shape, v.dtype),
                        pltpu.SemaphoreType.DMA((N_DEV - 1,)), pltpu.SemaphoreType.DMA((N_DEV - 1,))],
    )(v)


def adamw(w, g, m, v, *, name, tr=None, tc=None):
    R, C = w.shape
    if tc is None:
        tr, tc = min(tr, R), C
        blk = pl.BlockSpec((tr, C), lambda i: (i, 0))
    else:
        tr = R
        blk = pl.BlockSpec((R, tc), lambda i: (0, i))

    def body(w_ref, g_ref, m_ref, v_ref, d_ref, nm_ref, nv_ref):
        gv = g_ref[...]
        nm = ADAM_B1 * m_ref[...] + (1.0 - ADAM_B1) * gv
        nv = ADAM_B2 * v_ref[...] + (1.0 - ADAM_B2) * jnp.square(gv)
        m_hat = nm / (1.0 - ADAM_B1 ** ADAM_STEP)
        v_hat = nv / (1.0 - ADAM_B2 ** ADAM_STEP)
        d_ref[...] = -ADAM_LR * (m_hat / (jnp.sqrt(v_hat) + ADAM_EPS) + ADAM_WD * w_ref[...])
        nm_ref[...] = nm
        nv_ref[...] = nv

    out = jax.ShapeDtypeStruct((R, C), f32)
    return pl.pallas_call(
        body, name=name, grid=((R // tr) * (C // tc),), in_specs=[blk] * 4, out_specs=[blk] * 3, out_shape=[out] * 3,
        compiler_params=_cparams("parallel"),
    )(w, g, m, v)


BIG_SHARDS = (("w_in", (1024, 900), True), ("w_out", (256, 1024), False), ("w_cq", (256, 512), False),
              ("w_ckv", (256, 1024), False), ("w_co", (512, 256), True), ("w_mlp1", (1024, 1024), True),
              ("w_mlp2", (1024, 1024), False))
CONV_SHARD = (CONV_WIDTH, 3 * GDN_WIDTH // N_CHIPS)
SMALL_DIMS = (("norm_mix_g", 1024), ("fox_qnorm_g", 64), ("fox_knorm_g", 64), ("fox_f_bias", 8), ("fox_onorm_g", 64),
              ("gdn_A_log", 4), ("gdn_dt_bias", 4), ("gdn_onorm_g", 128), ("norm_xattn_g", 1024), ("mem_norm_g", 1024),
              ("xattn_qnorm_g", 128), ("xattn_knorm_g", 128), ("norm_mlp_g", 1024))
WEIGHT_ORDER = ("norm_mix_g", "w_in", "fox_qnorm_g", "fox_knorm_g", "fox_f_bias", "fox_onorm_g", "gdn_conv_w", "gdn_A_log",
                "gdn_dt_bias", "gdn_onorm_g", "w_out", "norm_xattn_g", "mem_norm_g", "w_cq", "w_ckv", "xattn_qnorm_g",
                "xattn_knorm_g", "w_co", "norm_mlp_g", "w_mlp1", "w_mlp2")


def _pack_rows(pieces, rows, lead=()):
    cat = jnp.concatenate([p.reshape(lead + (-1,)) for p in pieces], axis=-1)
    cat = jnp.pad(cat, [(0, 0)] * len(lead) + [(0, rows * LANES - cat.shape[-1])])
    return cat.reshape(lead + (rows, LANES))


def _unpack_rows(buf, sizes, lead=()):
    flat = buf.reshape(lead + (-1,))
    out, off = [], 0
    for n in sizes:
        out.append(flat[..., off:off + n])
        off += n
    return out


def _conv_to_wire(conv):
    return lax.bitcast_convert_type(conv, bf16)


def _conv_from_wire(wire):
    return lax.bitcast_convert_type(wire, f32)


SMALL_ROWS = 96
SMALL_ADAM_ROWS = 56


def kernel(x, mem, norm_mix_g, w_in, fox_qnorm_g, fox_knorm_g, fox_f_bias, fox_onorm_g, gdn_conv_w, gdn_A_log, gdn_dt_bias, gdn_onorm_g, w_out, norm_xattn_g, mem_norm_g, w_cq, w_ckv, xattn_qnorm_g, xattn_knorm_g, w_co, norm_mlp_g, w_mlp1, w_mlp2, loss_target, m_norm_mix_g, m_w_in, m_fox_qnorm_g, m_fox_knorm_g, m_fox_f_bias, m_fox_onorm_g, m_gdn_conv_w, m_gdn_A_log, m_gdn_dt_bias, m_gdn_onorm_g, m_w_out, m_norm_xattn_g, m_mem_norm_g, m_w_cq, m_w_ckv, m_xattn_qnorm_g, m_xattn_knorm_g, m_w_co, m_norm_mlp_g, m_w_mlp1, m_w_mlp2, v_norm_mix_g, v_w_in, v_fox_qnorm_g, v_fox_knorm_g, v_fox_f_bias, v_fox_onorm_g, v_gdn_conv_w, v_gdn_A_log, v_gdn_dt_bias, v_gdn_onorm_g, v_w_out, v_norm_xattn_g, v_mem_norm_g, v_w_cq, v_w_ckv, v_xattn_qnorm_g, v_xattn_knorm_g, v_w_co, v_norm_mlp_g, v_w_mlp1, v_w_mlp2):
    wts = dict(norm_mix_g=norm_mix_g, w_in=w_in, fox_qnorm_g=fox_qnorm_g, fox_knorm_g=fox_knorm_g, fox_f_bias=fox_f_bias,
               fox_onorm_g=fox_onorm_g, gdn_conv_w=gdn_conv_w, gdn_A_log=gdn_A_log, gdn_dt_bias=gdn_dt_bias,
               gdn_onorm_g=gdn_onorm_g, w_out=w_out, norm_xattn_g=norm_xattn_g, mem_norm_g=mem_norm_g, w_cq=w_cq, w_ckv=w_ckv,
               xattn_qnorm_g=xattn_qnorm_g, xattn_knorm_g=xattn_knorm_g, w_co=w_co, norm_mlp_g=norm_mlp_g, w_mlp1=w_mlp1,
               w_mlp2=w_mlp2)
    mom = dict(norm_mix_g=m_norm_mix_g, w_in=m_w_in, fox_qnorm_g=m_fox_qnorm_g, fox_knorm_g=m_fox_knorm_g,
               fox_f_bias=m_fox_f_bias, fox_onorm_g=m_fox_onorm_g, gdn_conv_w=m_gdn_conv_w, gdn_A_log=m_gdn_A_log,
               gdn_dt_bias=m_gdn_dt_bias, gdn_onorm_g=m_gdn_onorm_g, w_out=m_w_out, norm_xattn_g=m_norm_xattn_g,
               mem_norm_g=m_mem_norm_g, w_cq=m_w_cq, w_ckv=m_w_ckv, xattn_qnorm_g=m_xattn_qnorm_g,
               xattn_knorm_g=m_xattn_knorm_g, w_co=m_w_co, norm_mlp_g=m_norm_mlp_g, w_mlp1=m_w_mlp1, w_mlp2=m_w_mlp2)
    var = dict(norm_mix_g=v_norm_mix_g, w_in=v_w_in, fox_qnorm_g=v_fox_qnorm_g, fox_knorm_g=v_fox_knorm_g,
               fox_f_bias=v_fox_f_bias, fox_onorm_g=v_fox_onorm_g, gdn_conv_w=v_gdn_conv_w, gdn_A_log=v_gdn_A_log,
               gdn_dt_bias=v_gdn_dt_bias, gdn_onorm_g=v_gdn_onorm_g, w_out=v_w_out, norm_xattn_g=v_norm_xattn_g,
               mem_norm_g=v_mem_norm_g, w_cq=v_w_cq, w_ckv=v_w_ckv, xattn_qnorm_g=v_xattn_qnorm_g,
               xattn_knorm_g=v_xattn_knorm_g, w_co=v_w_co, norm_mlp_g=v_norm_mlp_g, w_mlp1=v_w_mlp1, w_mlp2=v_w_mlp2)
    B, S, D = x.shape
    T = B * S
    big_names = [n for n, _, _ in BIG_SHARDS]
    chip = 2 * lax.axis_index("x") + lax.axis_index("y")
    core = lax.axis_index("c").astype(jnp.int32).reshape(1)

    shards = {n: wts[n][0].astype(MXU_DTYPE) for n in big_names[1:]}
    in_t = lambda p: jnp.swapaxes(p[0], 0, 1)
    shards["w_in"] = jnp.pad(in_t(w_in).astype(MXU_DTYPE), ((0, IN_SHARD_PAD - IN_SHARD), (0, 0)))
    w_in_all, conv_all = gather_weights([shards["w_in"]], gdn_conv_w[0])
    late = big_names[1:]
    send_sems, recv_sems, late_src, late_zones, token = gather_weights_start([shards[n] for n in late], conv_all)
    own = lambda g, s: lax.dynamic_update_slice(g, s[None], (chip,) + (0,) * s.ndim)
    full = {"w_in": own(w_in_all, shards["w_in"])}
    conv_full = own(conv_all, gdn_conv_w[0]).transpose(1, 0, 2).reshape(CONV_WIDTH, 3 * GDN_WIDTH)
    rows = lambda g: g.reshape(N_CHIPS * g.shape[1], g.shape[2])
    w_in_t = full["w_in"][:, :IN_SHARD].reshape(IN_DIM, D_MODEL)

    def late_weights(after):
        zones = gather_weights_wait(send_sems, recv_sems, late_src, late_zones, after)
        got = {n: own(z, shards[n]) for n, z in zip(late, zones)}
        return dict(w_out=rows(got["w_out"]), w_cq=rows(got["w_cq"]), w_ckv=rows(got["w_ckv"]), w_co=got["w_co"],
                    w_mlp1=got["w_mlp1"], w_mlp2=rows(got["w_mlp2"]))

    def chip_partials(names, by_chip):
        landed = swap_grad_halves(by_chip, name="swap_grad_halves_" + names[0])
        return [add_grad_halves(g, l, core, name="add_halves_" + n) for n, g, l in zip(names, by_chip, landed)]

    in_flight = []

    def grads_ready(ready):
        names = list(ready)
        *started, tok = scatter_grads_start(chip_partials(names, [ready[n] for n in names]),
                                            name="scatter_grads_start_%d" % len(in_flight))
        in_flight.append((names, *started))
        return tok

    w = dict(wa_t=align_w_in_t(w_in_t), conv_w=conv_full, late=late_weights, grads_ready=grads_ready)
    sp = {n: wts[n] for n, _ in SMALL_DIMS}
    sp["norm_mix_g"] = sp["norm_mix_g"] + token[0, 0]

    loss_part, grad_x, g_big, g_small = local_step(x.reshape(T, D), mem.reshape(-1, D), loss_target.reshape(T, D), w, sp, B=B)

    small_pieces = [g_small[n] for n, _ in SMALL_DIMS] + [g_small["gdn_conv_w"], loss_part]
    small_sizes = [d for _, d in SMALL_DIMS] + [CONV_WIDTH * 3 * GDN_WIDTH, LANES]
    red_small = _unpack_rows(all_reduce_small(_pack_rows(small_pieces, SMALL_ROWS)), small_sizes)
    grads = {n: p.reshape(1, d) for (n, d), p in zip(SMALL_DIMS, red_small)}
    conv_grad = lax.dynamic_slice(red_small[-2].reshape(CONV_WIDTH, 3 * GDN_WIDTH), (0, chip * CONV_SHARD[1]), CONV_SHARD)
    grads["gdn_conv_w"] = conv_grad.reshape((1,) + CONV_SHARD)
    loss = red_small[-1][0]

    names = ["w_in"]
    chip_part = chip_partials(names, [g_big[n] for n in names])
    parts, zones = dict(zip(names, chip_part)), dict(zip(names, scatter_grads(chip_part)))
    for k, (names, send_sems, recv_sems, thru, land) in enumerate(in_flight):
        thru, land = scatter_grads_wait(send_sems, recv_sems, thru, land, grad_x, name="scatter_grads_wait_%d" % k)
        parts.update(zip(names, thru))
        zones.update(zip(names, land))
    order = jnp.stack([chip, chip ^ 2, chip ^ 1, chip ^ 3]).astype(jnp.int32)
    mine = [sum_grads(parts[n], zones[n], order, name="sum_chips_" + n) for n in big_names]
    theirs = swap_reduced_halves(mine)

    delta, new_m, new_v = {}, {}, {}
    for n, a, b in zip(big_names[1:], mine[1:], theirs[1:]):
        g, d, nm, nv = adamw_halves(wts[n][0], a, b, mom[n][0], var[n][0], core, name="adamw_" + n)
        grads[n], delta[n], new_m[n], new_v[n] = g[None], d[None], nm[None], nv[None]
    south = core[0] == 0
    g_in_t = jnp.concatenate([jnp.where(south, mine[0], theirs[0]), jnp.where(south, theirs[0], mine[0])])[:IN_SHARD]
    back = lambda t: jnp.swapaxes(t, 0, 1)[None]
    d, nm, nv = adamw(in_t(w_in), g_in_t, in_t(m_w_in), in_t(v_w_in), name="adamw_w_in", tc=256)
    grads["w_in"], delta["w_in"], new_m["w_in"], new_v["w_in"] = back(g_in_t), back(d), back(nm), back(nv)
    small_names = [n for n, _ in SMALL_DIMS] + ["gdn_conv_w"]
    small_sz = [d for _, d in SMALL_DIMS] + [CONV_SHARD[0] * CONV_SHARD[1]]
    packed4 = [_pack_rows([src[n] for n in small_names], SMALL_ADAM_ROWS) for src in (wts, grads, mom, var)]
    outs = adamw(*packed4, name="adamw_small", tr=SMALL_ADAM_ROWS)
    for dst, buf in zip((delta, new_m, new_v), outs):
        for n, p in zip(small_names, _unpack_rows(buf, small_sz)):
            dst[n] = p.reshape(wts[n].shape)

    return (loss, grad_x.reshape(B, S, D), *[grads[n] for n in WEIGHT_ORDER], *[delta[n] for n in WEIGHT_ORDER],
            *[new_m[n] for n in WEIGHT_ORDER], *[new_v[n] for n in WEIGHT_ORDER])
```

```python
import functools

import jax
import jax.numpy as jnp
import numpy as np
from jax import lax
from jax.experimental import pallas as pl
from jax.experimental.pallas import tpu as pltpu

f32 = jnp.float32
bf16 = jnp.bfloat16
MXU_DTYPE = jnp.bfloat16
WIRE_DTYPE = jnp.bfloat16
INV_PRECISION = lax.Precision.HIGH

D_MODEL = 1024
FOX_HEADS = 8
FOX_HEAD_DIM = 64
FOX_WIDTH = 512
GDN_HEADS = 4
GDN_HEAD_DIM = 128
GDN_WIDTH = 512
CONV_WIDTH = 4
GDN_CHUNK = 64
XATTN_HEADS = 4
XATTN_HEAD_DIM = 128
XATTN_WIDTH = 512
D_FF = 4096
IN_DIM = 3600
EPS = 1e-6
NEG_INF = -1e30
LANES = 128
ADAM_LR = 0.001
ADAM_B1 = 0.9
ADAM_B2 = 0.999
ADAM_EPS = 1e-08
ADAM_WD = 0.01
ADAM_STEP = 10
VMEM_LIMIT = 48 * 1024 * 1024

COL_FOX = 0
COL_GDN = 1536
COL_Z = 3072
COL_SMALL = 3584
IN_ALIGNED = 3840
IN_TILE = 768
SM_F = 0
SM_B = 8
SM_A = 12


def _cparams(*sem):
    return pltpu.CompilerParams(dimension_semantics=sem, vmem_limit_bytes=VMEM_LIMIT)


def _mx(v):
    return v.astype(MXU_DTYPE)


def _dot(a, b, dims, precision=None):
    return lax.dot_general(a, b, (dims, ((), ())), preferred_element_type=f32, precision=precision)


def _dotm(a, b, dims):
    return _dot(_mx(a), _mx(b), dims)


NN = ((1,), (0,))
NT = ((1,), (1,))
TN = ((0,), (0,))


def matmul(a, b, *, name, ta=False, tb=False, b_stacked=False, out_stacked=False, residual=None, relu2_out=False,
           relu2_bwd_aux=None, out_dtype=f32, tm=1024, tn=1024, tk=1024):
    M, K = (a.shape[1], a.shape[0]) if ta else a.shape
    if b_stacked:
        b_cols = b.shape[2]
        N, tk = (b.shape[1], min(tk, b_cols)) if tb else (N_CHIPS * b_cols, tk)
        tn = tn if tb else min(tn, b_cols)
        assert K == (N_CHIPS * b_cols if tb else b.shape[1]), (name, a.shape, b.shape)
    else:
        N = b.shape[0] if tb else b.shape[1]
    if out_stacked:
        tn = min(tn, N // N_CHIPS)
    tm, tn, tk = min(tm, M), min(tn, N), min(tk, K)
    assert M % tm == 0 and N % tn == 0 and K % tk == 0, (name, M, N, K)
    nk = K // tk
    has_res = residual is not None
    has_aux = relu2_bwd_aux is not None

    def body(*refs):
        a_ref, b_ref = refs[0], refs[1]
        pos = 2
        res_ref = aux_ref = None
        if has_res:
            res_ref = refs[pos]
            pos += 1
        if has_aux:
            aux_ref = refs[pos]
            pos += 1
        o_ref = refs[pos]
        k = pl.program_id(2)
        dims = ((0,) if ta else (1,), (1,) if tb else (0,))
        part = _dot(_mx(a_ref[...]), _mx(b_ref[...]), dims)

        def finish(r):
            if has_res:
                r = r + res_ref[...]
            if has_aux:
                r = r * (2.0 * jnp.sqrt(aux_ref[...].astype(f32)))
            if relu2_out:
                o_ref[...] = jnp.square(jnp.maximum(r, 0.0)).astype(o_ref.dtype)
            else:
                o_ref[...] = r.astype(o_ref.dtype)

        if nk == 1:
            finish(part)
            return
        acc_ref = refs[pos + 1]

        @pl.when(k == 0)
        def _():
            acc_ref[...] = part

        @pl.when((k > 0) & (k < nk - 1))
        def _():
            acc_ref[...] += part

        @pl.when(k == nk - 1)
        def _():
            finish(acc_ref[...] + part)

    a_spec = pl.BlockSpec((tk, tm), lambda i, j, k: (k, i)) if ta else pl.BlockSpec((tm, tk), lambda i, j, k: (i, k))
    if b_stacked and tb:
        per = b_cols // tk
        b_spec = pl.BlockSpec((None, tn, tk), lambda i, j, k: (k // per, j, k % per))
    elif b_stacked:
        per = b_cols // tn
        b_spec = pl.BlockSpec((None, tk, tn), lambda i, j, k: (j // per, k, j % per))
    else:
        b_spec = pl.BlockSpec((tn, tk), lambda i, j, k: (j, k)) if tb else pl.BlockSpec((tk, tn), lambda i, j, k: (k, j))
    if out_stacked:
        assert not (has_res or has_aux or relu2_out), name
        per_o = N // N_CHIPS // tn
        o_spec = pl.BlockSpec((None, tm, tn), lambda i, j, k: (j // per_o, i, j % per_o))
        out_full = (N_CHIPS, M, N // N_CHIPS)
    else:
        o_spec = pl.BlockSpec((tm, tn), lambda i, j, k: (i, j))
        out_full = (M, N)
    in_specs, args = [a_spec, b_spec], [a, b]
    if has_res:
        in_specs.append(o_spec)
        args.append(residual)
    if has_aux:
        in_specs.append(o_spec)
        args.append(relu2_bwd_aux)
    out_shape = [jax.ShapeDtypeStruct(out_full, out_dtype)]
    out_specs = [o_spec]
    res = pl.pallas_call(
        body, name=name, grid=(M // tm, N // tn, nk), in_specs=in_specs, out_specs=out_specs, out_shape=out_shape,
        scratch_shapes=[pltpu.VMEM((tm, tn), f32)] if nk > 1 else [],
        compiler_params=_cparams("parallel", "parallel", "arbitrary"),
    )(*args)
    return res[0]


def matmul_rows(a, b, extras, *, name, mode, tb=False, b_stacked=False, tm=1024, tk=1024):
    M, K = a.shape
    N = D_MODEL
    if b_stacked:
        assert tb, name
        tk = min(tk, b.shape[2])
        per = b.shape[2] // tk
        b_spec = pl.BlockSpec((None, N, tk), lambda i, k: (k // per, 0, k % per))
    elif tb:
        tk = min(tk, K)
        b_spec = pl.BlockSpec((N, tk), lambda i, k: (0, k))
    else:
        tk = min(tk, K)
        b_spec = pl.BlockSpec((tk, N), lambda i, k: (k, 0))
    tm = min(tm, M)
    assert M % tm == 0 and K % tk == 0, (name, M, K)
    nk = K // tk
    extras = [e for e in extras if e is not None]
    n_ex = len(extras)

    def body(*refs):
        a_ref, b_ref = refs[0], refs[1]
        ex = refs[2:2 + n_ex]
        o_ref, s_ref = refs[2 + n_ex], refs[3 + n_ex]
        i, k = pl.program_id(0), pl.program_id(1)
        part = _dot(_mx(a_ref[...]), _mx(b_ref[...]), ((1,), (1,) if tb else (0,)))

        def finish(y):
            @pl.when(i == 0)
            def _():
                s_ref[...] = jnp.zeros_like(s_ref)

            if mode == "rms_bwd":
                xv, gv = ex[0][...], ex[1][...]
                rstd = lax.rsqrt(jnp.mean(xv * xv, axis=-1, keepdims=True) + EPS)
                xhat = xv * rstd
                gd = y * gv
                dx = rstd * (gd - xhat * jnp.mean(gd * xhat, axis=-1, keepdims=True))
                o_ref[...] = dx + ex[2][...] if n_ex == 3 else dx
                s_ref[...] += jnp.sum(y * xhat, axis=0, keepdims=True)
            else:
                e = y + ex[0][...] - ex[1][...]
                o_ref[...] = e * (1.0 / N)
                tot = 0.5 * jnp.sum(jnp.mean(e * e, axis=-1, keepdims=True), axis=0, keepdims=True)
                s_ref[...] += jnp.broadcast_to(tot, s_ref.shape)

        if nk == 1:
            finish(part)
            return
        acc_ref = refs[4 + n_ex]

        @pl.when(k == 0)
        def _():
            acc_ref[...] = part

        @pl.when((k > 0) & (k < nk - 1))
        def _():
            acc_ref[...] += part

        @pl.when(k == nk - 1)
        def _():
            finish(acc_ref[...] + part)

    row = pl.BlockSpec((tm, N), lambda i, k: (i, 0))
    vec = pl.BlockSpec((1, N), lambda i, k: (0, 0))
    if mode == "rms_bwd":
        ex_specs = [row, vec] + ([row] if n_ex == 3 else [])
        s_shape, s_spec = jax.ShapeDtypeStruct((1, N), f32), vec
    else:
        ex_specs = [row, row]
        s_shape, s_spec = jax.ShapeDtypeStruct((1, LANES), f32), pl.BlockSpec((1, LANES), lambda i, k: (0, 0))
    return pl.pallas_call(
        body, name=name, grid=(M // tm, nk),
        in_specs=[pl.BlockSpec((tm, tk), lambda i, k: (i, k)), b_spec] + ex_specs,
        out_specs=[row, s_spec], out_shape=[jax.ShapeDtypeStruct((M, N), f32), s_shape],
        scratch_shapes=[pltpu.VMEM((tm, N), f32)] if nk > 1 else [],
        compiler_params=_cparams("arbitrary", "arbitrary"),
    )(a, b, *extras)


def rms_fwd(x, g, *, name, tr=512):
    R, D = x.shape
    tr = min(tr, R)

    def body(x_ref, g_ref, o_ref):
        xv = x_ref[...]
        y = xv * lax.rsqrt(jnp.mean(xv * xv, axis=-1, keepdims=True) + EPS)
        o_ref[...] = (y * g_ref[...]).astype(o_ref.dtype)

    return pl.pallas_call(
        body, name=name, grid=(R // tr,),
        in_specs=[pl.BlockSpec((tr, D), lambda i: (i, 0)), pl.BlockSpec((1, D), lambda i: (0, 0))],
        out_specs=pl.BlockSpec((tr, D), lambda i: (i, 0)),
        out_shape=jax.ShapeDtypeStruct((R, D), MXU_DTYPE),
        compiler_params=_cparams("parallel"),
    )(x, g)


def rms_bwd(x, g, dh, residual, *, name, tr=512):
    R, D = x.shape
    tr = min(tr, R)
    has_res = residual is not None

    def body(*refs):
        if has_res:
            x_ref, g_ref, dh_ref, res_ref, dx_ref, dg_ref = refs
        else:
            x_ref, g_ref, dh_ref, dx_ref, dg_ref = refs
        xv = x_ref[...]
        rstd = lax.rsqrt(jnp.mean(xv * xv, axis=-1, keepdims=True) + EPS)
        xhat = xv * rstd
        dh = dh_ref[...].astype(f32)
        gd = dh * g_ref[...]
        dx = rstd * (gd - xhat * jnp.mean(gd * xhat, axis=-1, keepdims=True))
        if has_res:
            dx = dx + res_ref[...]
        dx_ref[...] = dx

        @pl.when(pl.program_id(0) == 0)
        def _():
            dg_ref[...] = jnp.zeros_like(dg_ref)

        dg_ref[...] += jnp.sum(dh * xhat, axis=0, keepdims=True)

    row = pl.BlockSpec((tr, D), lambda i: (i, 0))
    vec = pl.BlockSpec((1, D), lambda i: (0, 0))
    in_specs = [row, vec, row] + ([row] if has_res else [])
    args = [x, g, dh] + ([residual] if has_res else [])
    return pl.pallas_call(
        body, name=name, grid=(R // tr,), in_specs=in_specs, out_specs=[row, vec],
        out_shape=[jax.ShapeDtypeStruct((R, D), f32), jax.ShapeDtypeStruct((1, D), f32)],
        compiler_params=_cparams("arbitrary"),
    )(*args)


def loss_head(y, target, *, tr=512):
    R, D = y.shape
    tr = min(tr, R)

    def body(y_ref, t_ref, dy_ref, loss_ref):
        e = y_ref[...] - t_ref[...]
        dy_ref[...] = e * (1.0 / D)

        @pl.when(pl.program_id(0) == 0)
        def _():
            loss_ref[...] = jnp.zeros_like(loss_ref)

        part = 0.5 * jnp.sum(jnp.mean(e * e, axis=-1, keepdims=True), axis=0, keepdims=True)
        loss_ref[...] += jnp.broadcast_to(part, loss_ref.shape)

    row = pl.BlockSpec((tr, D), lambda i: (i, 0))
    return pl.pallas_call(
        body, name="loss_head", grid=(R // tr,), in_specs=[row, row],
        out_specs=[row, pl.BlockSpec((1, LANES), lambda i: (0, 0))],
        out_shape=[jax.ShapeDtypeStruct((R, D), f32), jax.ShapeDtypeStruct((1, LANES), f32)],
        compiler_params=_cparams("arbitrary"),
    )(y, target)


def _head_rms(v, g):
    r = lax.rsqrt(jnp.mean(v * v, axis=-1, keepdims=True) + EPS)
    return v * r * g, r


def _head_rms_bwd(v, r, g, dn):
    vhat = v * r
    gd = dn * g
    dv = r * (gd - vhat * jnp.mean(gd * vhat, axis=-1, keepdims=True))
    return dv, jnp.sum(dn * vhat, axis=0, keepdims=True)


def _softmax_rows(s):
    m = jnp.max(s, axis=-1, keepdims=True)
    e = jnp.exp(s - m)
    return e / jnp.sum(e, axis=-1, keepdims=True)


def xattn_fwd(cq, ckv, gq, gk, *, B, tq=512):
    T = cq.shape[0]
    S = T // B
    M = ckv.shape[0] // B
    tq = min(tq, S)
    nq = S // tq
    scale = XATTN_HEAD_DIM ** -0.5

    def body(q_ref, k_ref, v_ref, gq_ref, gk_ref, o_ref):
        qn, _ = _head_rms(q_ref[...], gq_ref[...])
        kn, _ = _head_rms(k_ref[...], gk_ref[...])
        p = _softmax_rows(_dot(_mx(qn), _mx(kn), NT) * scale)
        o_ref[...] = _dot(_mx(p), _mx(v_ref[...]), NN).astype(o_ref.dtype)

    hd = XATTN_HEAD_DIM
    vec = pl.BlockSpec((1, hd), lambda b, h, i: (0, 0))
    return pl.pallas_call(
        body, name="xattn_fwd", grid=(B, XATTN_HEADS, nq),
        in_specs=[pl.BlockSpec((tq, hd), lambda b, h, i: (b * nq + i, h)),
                  pl.BlockSpec((M, hd), lambda b, h, i: (b, h)),
                  pl.BlockSpec((M, hd), lambda b, h, i: (b, XATTN_HEADS + h)), vec, vec],
        out_specs=pl.BlockSpec((tq, hd), lambda b, h, i: (b * nq + i, h)),
        out_shape=jax.ShapeDtypeStruct((T, XATTN_WIDTH), MXU_DTYPE),
        compiler_params=_cparams("parallel", "parallel", "parallel"),
    )(cq, ckv, ckv, gq, gk)


def xattn_bwd(cq, ckv, gq, gk, dco, *, B, tq=512):
    T = cq.shape[0]
    S = T // B
    M = ckv.shape[0] // B
    tq = min(tq, S)
    nq = S // tq
    scale = XATTN_HEAD_DIM ** -0.5
    hd = XATTN_HEAD_DIM

    def body(q_ref, k_ref, v_ref, gq_ref, gk_ref, do_ref, dq_ref, dk_ref, dv_ref, dgq_ref, dgk_ref, dkn_acc, dv_acc):
        b, h, i = pl.program_id(0), pl.program_id(1), pl.program_id(2)

        @pl.when((b == 0) & (h == 0) & (i == 0))
        def _():
            dgq_ref[...] = jnp.zeros_like(dgq_ref)
            dgk_ref[...] = jnp.zeros_like(dgk_ref)

        @pl.when(i == 0)
        def _():
            dkn_acc[...] = jnp.zeros_like(dkn_acc)
            dv_acc[...] = jnp.zeros_like(dv_acc)

        q, k, v = q_ref[...], k_ref[...], v_ref[...]
        gqv, gkv = gq_ref[...], gk_ref[...]
        qn, rq = _head_rms(q, gqv)
        kn, rk = _head_rms(k, gkv)
        p = _softmax_rows(_dot(_mx(qn), _mx(kn), NT) * scale)
        do = do_ref[...]
        dv_acc[...] += _dot(_mx(p), _mx(do), TN)
        dp = _dot(_mx(do), _mx(v), NT)
        ds = p * (dp - jnp.sum(dp * p, axis=-1, keepdims=True)) * scale
        dqn = _dot(_mx(ds), _mx(kn), NN)
        dkn_acc[...] += _dot(_mx(ds), _mx(qn), TN)
        dq, dgq = _head_rms_bwd(q, rq, gqv, dqn)
        dq_ref[...] = dq.astype(dq_ref.dtype)
        dgq_ref[...] += dgq

        @pl.when(i == nq - 1)
        def _():
            dk, dgk = _head_rms_bwd(k, rk, gkv, dkn_acc[...])
            dk_ref[...] = dk.astype(dk_ref.dtype)
            dv_ref[...] = dv_acc[...].astype(dv_ref.dtype)
            dgk_ref[...] += dgk

    vec = pl.BlockSpec((1, hd), lambda b, h, i: (0, 0))
    qspec = pl.BlockSpec((tq, hd), lambda b, h, i: (b * nq + i, h))
    kspec = pl.BlockSpec((M, hd), lambda b, h, i: (b, h))
    vspec = pl.BlockSpec((M, hd), lambda b, h, i: (b, XATTN_HEADS + h))
    dq, dk, dv, dgq, dgk = pl.pallas_call(
        body, name="xattn_bwd", grid=(B, XATTN_HEADS, nq),
        in_specs=[qspec, kspec, vspec, vec, vec, qspec],
        out_specs=[qspec, kspec, kspec, vec, vec],
        out_shape=[jax.ShapeDtypeStruct((T, XATTN_WIDTH), MXU_DTYPE),
                   jax.ShapeDtypeStruct((B * M, XATTN_WIDTH), MXU_DTYPE),
                   jax.ShapeDtypeStruct((B * M, XATTN_WIDTH), MXU_DTYPE),
                   jax.ShapeDtypeStruct((1, hd), f32), jax.ShapeDtypeStruct((1, hd), f32)],
        scratch_shapes=[pltpu.VMEM((M, hd), f32), pltpu.VMEM((M, hd), f32)],
        compiler_params=_cparams("arbitrary", "arbitrary", "arbitrary"),
    )(cq, ckv, ckv, gq, gk, dco)
    return dq, jnp.concatenate([dk, dv], axis=1), dgq, dgk


FOX_PAIRS = FOX_HEADS // 2


def _fox_scores(qn, kn, ccol, crow, q0, tq, S, scale):
    s = _dot(_mx(qn), _mx(kn), NT) * scale + ccol - crow
    qpos = q0 + lax.broadcasted_iota(jnp.int32, (tq, S), 0)
    kpos = lax.broadcasted_iota(jnp.int32, (tq, S), 1)
    return jnp.where(kpos <= qpos, s, NEG_INF)


def fox_fwd(P, ccol, crow, gq, gk, go, *, B, tq=256):
    T = P.shape[0]
    S = T // B
    tq = min(tq, S)
    nq = S // tq
    hd = FOX_HEAD_DIM
    scale = hd ** -0.5

    def body(q_ref, k_ref, v_ref, ccol_ref, crow_ref, gq_ref, gk_ref, go_ref, o_ref, oa_ref):
        q0 = pl.program_id(2) * tq
        for e in range(2):
            sl = slice(e * hd, (e + 1) * hd)
            qn, _ = _head_rms(q_ref[:, sl], gq_ref[:, sl])
            kn, _ = _head_rms(k_ref[:, sl], gk_ref[:, sl])
            p = _softmax_rows(_fox_scores(qn, kn, ccol_ref[0, e], crow_ref[0, e], q0, tq, S, scale))
            o = _dot(_mx(p), _mx(v_ref[:, sl]), NN)
            o_ref[:, sl] = o
            oa_ref[:, sl] = _head_rms(o, go_ref[:, sl])[0].astype(oa_ref.dtype)

    W = 2 * hd
    vec = pl.BlockSpec((1, W), lambda b, h, i: (0, 0))
    ospec = pl.BlockSpec((tq, W), lambda b, h, i: (b * nq + i, h))
    return pl.pallas_call(
        body, name="fox_fwd", grid=(B, FOX_PAIRS, nq),
        in_specs=[pl.BlockSpec((tq, W), lambda b, h, i: (b * nq + i, h)),
                  pl.BlockSpec((S, W), lambda b, h, i: (b, FOX_PAIRS + h)),
                  pl.BlockSpec((S, W), lambda b, h, i: (b, 2 * FOX_PAIRS + h)),
                  pl.BlockSpec((1, 2, tq, 1), lambda b, h, i: (b, h, i, 0)),
                  pl.BlockSpec((1, 2, 1, S), lambda b, h, i: (b, h, 0, 0)), vec, vec, vec],
        out_specs=[ospec, ospec],
        out_shape=[jax.ShapeDtypeStruct((T, FOX_WIDTH), f32), jax.ShapeDtypeStruct((T, FOX_WIDTH), MXU_DTYPE)],
        compiler_params=_cparams("parallel", "parallel", "parallel"),
    )(P, P, P, ccol, crow, gq, gk, go)


def fox_bwd(P, ccol, crow, gq, gk, go, o_raw, d_oab, *, B, tq=256):
    T = P.shape[0]
    S = T // B
    tq = min(tq, S)
    nq = S // tq
    hd = FOX_HEAD_DIM
    scale = hd ** -0.5

    def body(q_ref, k_ref, v_ref, ccol_ref, crow_ref, gq_ref, gk_ref, go_ref, o_ref, doa_ref,
             dq_ref, dk_ref, dv_ref, dccol_ref, dcrow_ref, dgq_ref, dgk_ref, dgo_ref, dkn_acc, dv_acc, dcrow_acc):
        b, h, i = pl.program_id(0), pl.program_id(1), pl.program_id(2)
        q0 = i * tq

        @pl.when((b == 0) & (h == 0) & (i == 0))
        def _():
            dgq_ref[...] = jnp.zeros_like(dgq_ref)
            dgk_ref[...] = jnp.zeros_like(dgk_ref)
            dgo_ref[...] = jnp.zeros_like(dgo_ref)

        @pl.when(i == 0)
        def _():
            dkn_acc[...] = jnp.zeros_like(dkn_acc)
            dv_acc[...] = jnp.zeros_like(dv_acc)
            dcrow_acc[...] = jnp.zeros_like(dcrow_acc)

        for e in range(2):
            sl = slice(e * hd, (e + 1) * hd)
            q, k, v = q_ref[:, sl], k_ref[:, sl], v_ref[:, sl]
            gqv, gkv, gov = gq_ref[:, sl], gk_ref[:, sl], go_ref[:, sl]
            qn, rq = _head_rms(q, gqv)
            kn, rk = _head_rms(k, gkv)
            p = _softmax_rows(_fox_scores(qn, kn, ccol_ref[0, e], crow_ref[0, e], q0, tq, S, scale))
            o = o_ref[:, sl]
            ro = lax.rsqrt(jnp.mean(o * o, axis=-1, keepdims=True) + EPS)
            do, dgo = _head_rms_bwd(o, ro, gov, doa_ref[:, sl])
            dgo_ref[:, sl] += dgo
            dv_acc[e] += _dot(_mx(p), _mx(do), TN)
            dp = _dot(_mx(do), _mx(v), NT)
            ds = p * (dp - jnp.sum(do * o, axis=-1, keepdims=True))
            dccol_ref[0, e] = jnp.sum(ds, axis=1, keepdims=True)
            dcrow_acc[e] -= jnp.sum(ds, axis=0, keepdims=True)
            dqn = _dot(_mx(ds), _mx(kn), NN) * scale
            dkn_acc[e] += _dot(_mx(ds), _mx(qn), TN) * scale
            dq, dgq = _head_rms_bwd(q, rq, gqv, dqn)
            dq_ref[:, sl] = dq.astype(dq_ref.dtype)
            dgq_ref[:, sl] += dgq

        @pl.when(i == nq - 1)
        def _():
            for e in range(2):
                sl = slice(e * hd, (e + 1) * hd)
                k = k_ref[:, sl]
                gkv = gk_ref[:, sl]
                rk = lax.rsqrt(jnp.mean(k * k, axis=-1, keepdims=True) + EPS)
                dk, dgk = _head_rms_bwd(k, rk, gkv, dkn_acc[e])
                dk_ref[:, sl] = dk.astype(dk_ref.dtype)
                dv_ref[:, sl] = dv_acc[e].astype(dv_ref.dtype)
                dgk_ref[:, sl] += dgk
                dcrow_ref[0, e] = dcrow_acc[e]

    W = 2 * hd
    vec = pl.BlockSpec((1, W), lambda b, h, i: (0, 0))
    qspec = pl.BlockSpec((tq, W), lambda b, h, i: (b * nq + i, h))
    kvout = pl.BlockSpec((S, W), lambda b, h, i: (b, h))
    colspec = pl.BlockSpec((1, 2, tq, 1), lambda b, h, i: (b, h, i, 0))
    rowspec = pl.BlockSpec((1, 2, 1, S), lambda b, h, i: (b, h, 0, 0))
    return pl.pallas_call(
        body, name="fox_bwd", grid=(B, FOX_PAIRS, nq),
        in_specs=[qspec,
                  pl.BlockSpec((S, W), lambda b, h, i: (b, FOX_PAIRS + h)),
                  pl.BlockSpec((S, W), lambda b, h, i: (b, 2 * FOX_PAIRS + h)),
                  colspec, rowspec, vec, vec, vec, qspec, qspec],
        out_specs=[qspec, kvout, kvout, colspec, rowspec, vec, vec, vec],
        out_shape=[jax.ShapeDtypeStruct((T, FOX_WIDTH), MXU_DTYPE), jax.ShapeDtypeStruct((T, FOX_WIDTH), MXU_DTYPE),
                   jax.ShapeDtypeStruct((T, FOX_WIDTH), MXU_DTYPE),
                   jax.ShapeDtypeStruct((B, FOX_HEADS, S, 1), f32), jax.ShapeDtypeStruct((B, FOX_HEADS, 1, S), f32),
                   jax.ShapeDtypeStruct((1, W), f32), jax.ShapeDtypeStruct((1, W), f32), jax.ShapeDtypeStruct((1, W), f32)],
        scratch_shapes=[pltpu.VMEM((2, S, hd), f32), pltpu.VMEM((2, S, hd), f32), pltpu.VMEM((2, 1, S), f32)],
        compiler_params=_cparams("arbitrary", "arbitrary", "arbitrary"),
    )(P, P, P, ccol, crow, gq, gk, go, o_raw, d_oab)


FOX_TQ = 512
FOX_TK = 512
GROUP_PRECISION = lax.Precision.HIGH


def _head_mean(v):
    n = v.shape[1]
    r = lax.broadcasted_iota(jnp.int32, (n, n), 0) // FOX_HEAD_DIM
    c = lax.broadcasted_iota(jnp.int32, (n, n), 1) // FOX_HEAD_DIM
    ones = (r == c).astype(bf16)
    hi = v.astype(bf16)
    lo = (v - hi.astype(f32)).astype(bf16)
    return (_dot(hi, ones, NN) + _dot(lo, ones, NN)) * (1.0 / FOX_HEAD_DIM)


def fox_prep_fwd(P, gq, gk, *, tr=512):
    T = P.shape[0]
    tr = min(tr, T)
    scale = FOX_HEAD_DIM ** -0.5

    def body(q_ref, k_ref, v_ref, gq_ref, gk_ref, qn_ref, kn_ref, vb_ref):
        q, k = q_ref[...], k_ref[...]
        qn_ref[...] = (q * lax.rsqrt(_head_mean(q * q) + EPS) * (gq_ref[...] * scale)).astype(qn_ref.dtype)
        kn_ref[...] = (k * lax.rsqrt(_head_mean(k * k) + EPS) * gk_ref[...]).astype(kn_ref.dtype)
        vb_ref[...] = v_ref[...].astype(vb_ref.dtype)

    W = FOX_WIDTH
    col = lambda j: pl.BlockSpec((tr, W), lambda i: (i, j))
    vec = pl.BlockSpec((1, W), lambda i: (0, 0))
    out = jax.ShapeDtypeStruct((T, W), MXU_DTYPE)
    return pl.pallas_call(
        body, name="fox_prep_fwd", grid=(T // tr,), in_specs=[col(0), col(1), col(2), vec, vec],
        out_specs=[col(0)] * 3, out_shape=[out] * 3, compiler_params=_cparams("parallel"),
    )(P, P, P, gq, gk)


def fox_prep_bwd(P, gq, gk, dqn, dkn, *, tr=512):
    T = P.shape[0]
    tr = min(tr, T)
    scale = FOX_HEAD_DIM ** -0.5

    def body(q_ref, k_ref, gq_ref, gk_ref, dqn_ref, dkn_ref, dq_ref, dk_ref, dgq_ref, dgk_ref):
        @pl.when(pl.program_id(0) == 0)
        def _():
            dgq_ref[...] = jnp.zeros_like(dgq_ref)
            dgk_ref[...] = jnp.zeros_like(dgk_ref)

        def one(x, g, dn, dx_ref, dg_ref):
            r = lax.rsqrt(_head_mean(x * x) + EPS)
            xhat = x * r
            gd = dn * g
            dx_ref[...] = (r * (gd - xhat * _head_mean(gd * xhat))).astype(dx_ref.dtype)
            return jnp.sum(dn * xhat, axis=0, keepdims=True)

        dgq_ref[...] += scale * one(q_ref[...], gq_ref[...] * scale, dqn_ref[...], dq_ref, dgq_ref)
        dgk_ref[...] += one(k_ref[...], gk_ref[...], dkn_ref[...], dk_ref, dgk_ref)

    W = FOX_WIDTH
    col = lambda j: pl.BlockSpec((tr, W), lambda i: (i, j))
    vec = pl.BlockSpec((1, W), lambda i: (0, 0))
    return pl.pallas_call(
        body, name="fox_prep_bwd", grid=(T // tr,), in_specs=[col(0), col(1), vec, vec, col(0), col(0)],
        out_specs=[col(0), col(0), vec, vec],
        out_shape=[jax.ShapeDtypeStruct((T, W), MXU_DTYPE), jax.ShapeDtypeStruct((T, W), MXU_DTYPE),
                   jax.ShapeDtypeStruct((1, W), f32), jax.ShapeDtypeStruct((1, W), f32)],
        compiler_params=_cparams("arbitrary"),
    )(P, P, gq, gk, dqn, dkn)


def _fox_tile_scores(q, k_ref, ccol_ref, cq, e, j, sl, mask_off):
    tq, tk = FOX_TQ, FOX_TK
    rows = pl.ds(pl.multiple_of(j * tk, tk), tk)
    k = k_ref[rows, sl]
    s = _dot(k, q, NT) + cq - ccol_ref[0, e, rows, :]
    if mask_off is not None:
        key = lax.broadcasted_iota(jnp.int32, (tk, tq), 0) + mask_off
        query = lax.broadcasted_iota(jnp.int32, (tk, tq), 1)
        s = jnp.where(key <= query, s, NEG_INF)
    return s, k, rows


def _fox_sweep(i, update, carry):
    nd = FOX_TQ // FOX_TK
    carry = lax.fori_loop(0, i * nd, lambda j, cr: update(cr, j, None), carry)
    for d in range(nd):
        carry = update(carry, i * nd + d, d * FOX_TK)
    return carry


def fox_core_fwd(qn, kn, vb, ccol, crow, go, *, B):
    T = qn.shape[0]
    S = T // B
    tq = FOX_TQ
    nq = S // tq
    hd = FOX_HEAD_DIM

    def body(q_ref, k_ref, v_ref, ccol_ref, crow_ref, go_ref, o_ref, oa_ref, lse_ref):
        i = pl.program_id(2)
        for e in range(2):
            sl = slice(e * hd, (e + 1) * hd)
            q = q_ref[:, sl]
            cq = crow_ref[0, e, i]

            def update(carry, j, mask_off):
                m, l, acc = carry
                s, _, rows = _fox_tile_scores(q, k_ref, ccol_ref, cq, e, j, sl, mask_off)
                m2 = jnp.maximum(m, jnp.max(s, axis=0, keepdims=True))
                a = jnp.exp(m - m2)
                p = jnp.exp(s - m2)
                return m2, a * l + jnp.sum(p, axis=0, keepdims=True), a * acc + _dot(v_ref[rows, sl], _mx(p), TN)

            carry = (jnp.full((1, tq), NEG_INF, f32), jnp.zeros((1, tq), f32), jnp.zeros((hd, tq), f32))
            m, l, acc = _fox_sweep(i, update, carry)
            o = (acc / l).T
            o_ref[:, sl] = o
            oa_ref[:, sl] = _head_rms(o, go_ref[:, sl])[0].astype(oa_ref.dtype)
            lse_ref[0, e, 0] = m + jnp.log(l)

    W = 2 * hd
    qspec = pl.BlockSpec((tq, W), lambda b, h, i: (b * nq + i, h))
    kspec = pl.BlockSpec((S, W), lambda b, h, i: (b, h))
    return pl.pallas_call(
        body, name="fox_core_fwd", grid=(B, FOX_PAIRS, nq),
        in_specs=[qspec, kspec, kspec, pl.BlockSpec((1, 2, S, 1), lambda b, h, i: (b, h, 0, 0)),
                  pl.BlockSpec((1, 2, nq, 1, tq), lambda b, h, i: (b, h, 0, 0, 0)),
                  pl.BlockSpec((1, W), lambda b, h, i: (0, 0))],
        out_specs=[qspec, qspec, pl.BlockSpec((1, 2, 1, 1, tq), lambda b, h, i: (b, h, i, 0, 0))],
        out_shape=[jax.ShapeDtypeStruct((T, FOX_WIDTH), f32), jax.ShapeDtypeStruct((T, FOX_WIDTH), MXU_DTYPE),
                   jax.ShapeDtypeStruct((B, FOX_HEADS, nq, 1, tq), f32)],
        compiler_params=_cparams("parallel", "parallel", "parallel"),
    )(qn, kn, vb, ccol, crow, go)


def fox_core_bwd(qn, kn, vb, ccol, crow, go, o_raw, lse, d_oab, *, B):
    T = qn.shape[0]
    S = T // B
    tq = FOX_TQ
    nq = S // tq
    hd = FOX_HEAD_DIM

    def body(q_ref, k_ref, v_ref, ccol_ref, crow_ref, go_ref, o_ref, lse_ref, doa_ref,
             dq_ref, dk_ref, dv_ref, dccol_ref, dcrow_ref, dgo_ref, dk_acc, dv_acc, dck_acc):
        b, h, i = pl.program_id(0), pl.program_id(1), pl.program_id(2)

        @pl.when((b == 0) & (h == 0) & (i == 0))
        def _():
            dgo_ref[...] = jnp.zeros_like(dgo_ref)

        @pl.when(i == 0)
        def _():
            dk_acc[...] = jnp.zeros_like(dk_acc)
            dv_acc[...] = jnp.zeros_like(dv_acc)
            dck_acc[...] = jnp.zeros_like(dck_acc)

        for e in range(2):
            sl = slice(e * hd, (e + 1) * hd)
            q = q_ref[:, sl]
            cq = crow_ref[0, e, i]
            lse_e = lse_ref[0, e, 0]
            o = o_ref[:, sl]
            ro = lax.rsqrt(jnp.mean(o * o, axis=-1, keepdims=True) + EPS)
            do, dgo = _head_rms_bwd(o, ro, go_ref[:, sl], doa_ref[:, sl])
            dgo_ref[:, sl] += dgo
            delta = jnp.sum((do * o).T, axis=0, keepdims=True)
            do_b = _mx(do)

            def update(carry, j, mask_off):
                dq, dcq = carry
                s, k, rows = _fox_tile_scores(q, k_ref, ccol_ref, cq, e, j, sl, mask_off)
                p = jnp.exp(s - lse_e)
                dv_acc[e, rows, :] += _dot(_mx(p), do_b, NN)
                ds = p * (_dot(v_ref[rows, sl], do_b, NT) - delta)
                dck_acc[e, rows, :] -= jnp.sum(ds, axis=1, keepdims=True)
                ds_b = _mx(ds)
                dk_acc[e, rows, :] += _dot(ds_b, q, NN)
                return dq + _dot(ds_b, k, TN), dcq + jnp.sum(ds, axis=0, keepdims=True)

            dq, dcq = _fox_sweep(i, update, (jnp.zeros((tq, hd), f32), jnp.zeros((1, tq), f32)))
            dq_ref[:, sl] = dq
            dcrow_ref[0, e, 0] = dcq

        @pl.when(i == nq - 1)
        def _():
            for e in range(2):
                sl = slice(e * hd, (e + 1) * hd)
                dk_ref[:, sl] = dk_acc[e]
                dv_ref[:, sl] = dv_acc[e].astype(dv_ref.dtype)
            dccol_ref[0] = dck_acc[...]

    W = 2 * hd
    qspec = pl.BlockSpec((tq, W), lambda b, h, i: (b * nq + i, h))
    kspec = pl.BlockSpec((S, W), lambda b, h, i: (b, h))
    colspec = pl.BlockSpec((1, 2, S, 1), lambda b, h, i: (b, h, 0, 0))
    rowspec = pl.BlockSpec((1, 2, nq, 1, tq), lambda b, h, i: (b, h, 0, 0, 0))
    tilespec = pl.BlockSpec((1, 2, 1, 1, tq), lambda b, h, i: (b, h, i, 0, 0))
    vec = pl.BlockSpec((1, W), lambda b, h, i: (0, 0))
    return pl.pallas_call(
        body, name="fox_core_bwd", grid=(B, FOX_PAIRS, nq),
        in_specs=[qspec, kspec, kspec, colspec, rowspec, vec, qspec, tilespec, qspec],
        out_specs=[qspec, kspec, kspec, colspec, tilespec, vec],
        out_shape=[jax.ShapeDtypeStruct((T, FOX_WIDTH), f32), jax.ShapeDtypeStruct((T, FOX_WIDTH), f32),
                   jax.ShapeDtypeStruct((T, FOX_WIDTH), MXU_DTYPE),
                   jax.ShapeDtypeStruct((B, FOX_HEADS, S, 1), f32), jax.ShapeDtypeStruct((B, FOX_HEADS, nq, 1, tq), f32),
                   jax.ShapeDtypeStruct((1, W), f32)],
        scratch_shapes=[pltpu.VMEM((2, S, hd), f32), pltpu.VMEM((2, S, hd), f32), pltpu.VMEM((2, S, 1), f32)],
        compiler_params=_cparams("arbitrary", "arbitrary", "arbitrary"),
    )(qn, kn, vb, ccol, crow, go, o_raw, lse, d_oab)


def _lane_mask(lo, hi, shape):
    lane = lax.broadcasted_iota(jnp.int32, shape, 1)
    return (lane >= lo) & (lane < hi)


def _cumsum_rows(v, period, reverse=False):
    n = v.shape[0]
    pos = lax.broadcasted_iota(jnp.int32, v.shape, 0) % period
    sh = 1
    while sh < period:
        if reverse:
            v = v + jnp.where(pos + sh < period, pltpu.roll(v, n - sh, 0), 0.0)
        else:
            v = v + jnp.where(pos >= sh, pltpu.roll(v, sh, 0), 0.0)
        sh *= 2
    return v


def _gate_values(z, bias, alog):
    zb = z + bias
    ls = jax.nn.log_sigmoid(zb)
    beta = jax.nn.sigmoid(z)
    g = -jnp.exp(alog) * jax.nn.softplus(zb)
    return zb, ls, beta, g


def gates_fwd(P, bias, alog, *, B):
    T = P.shape[0]
    S = T // B

    def body(z_ref, bias_ref, alog_ref, o_ref):
        z = z_ref[...]
        _, ls, beta, g = _gate_values(z, bias_ref[...], alog_ref[...])
        c = _cumsum_rows(ls, S)
        gc = _cumsum_rows(g, GDN_CHUNK)
        o = jnp.where(_lane_mask(SM_F, SM_F + FOX_HEADS, z.shape), c, 0.0)
        o = jnp.where(_lane_mask(SM_B, SM_B + GDN_HEADS, z.shape), beta, o)
        o = jnp.where(_lane_mask(SM_A, SM_A + GDN_HEADS, z.shape), gc, o)
        o_ref[...] = o

    vec = pl.BlockSpec((1, LANES), lambda b: (0, 0))
    return pl.pallas_call(
        body, name="gates_fwd", grid=(B,),
        in_specs=[pl.BlockSpec((S, LANES), lambda b: (b, COL_SMALL // LANES)), vec, vec],
        out_specs=pl.BlockSpec((S, LANES), lambda b: (b, 0)),
        out_shape=jax.ShapeDtypeStruct((T, LANES), f32),
        compiler_params=_cparams("parallel"),
    )(P, bias, alog)


def gates_bwd(P, bias, alog, dgates, *, B):
    T = P.shape[0]
    S = T // B

    def body(z_ref, bias_ref, alog_ref, dg_ref, dz_ref, par_ref):
        z = z_ref[...]
        zb, ls, beta, g = _gate_values(z, bias_ref[...], alog_ref[...])
        d = dg_ref[...]
        dls = _cumsum_rows(d, S, reverse=True)
        dgr = _cumsum_rows(d, GDN_CHUNK, reverse=True)
        sig = jax.nn.sigmoid(zb)
        dz_f = dls * (1.0 - sig)
        dz_b = d * beta * (1.0 - beta)
        dz_a = dgr * (-jnp.exp(alog_ref[...])) * sig
        dz = jnp.where(_lane_mask(SM_F, SM_F + FOX_HEADS, z.shape), dz_f, 0.0)
        dz = jnp.where(_lane_mask(SM_B, SM_B + GDN_HEADS, z.shape), dz_b, dz)
        dz = jnp.where(_lane_mask(SM_A, SM_A + GDN_HEADS, z.shape), dz_a, dz)
        dz_ref[...] = dz.astype(dz_ref.dtype)

        @pl.when(pl.program_id(0) == 0)
        def _():
            par_ref[...] = jnp.zeros_like(par_ref)

        dalog = jnp.where(_lane_mask(SM_A, SM_A + GDN_HEADS, z.shape), dgr * g, 0.0)
        par_ref[0:1, :] += jnp.sum(dz, axis=0, keepdims=True)
        par_ref[1:2, :] += jnp.sum(dalog, axis=0, keepdims=True)

    vec = pl.BlockSpec((1, LANES), lambda b: (0, 0))
    return pl.pallas_call(
        body, name="gates_bwd", grid=(B,),
        in_specs=[pl.BlockSpec((S, LANES), lambda b: (b, COL_SMALL // LANES)), vec, vec,
                  pl.BlockSpec((S, LANES), lambda b: (b, 0))],
        out_specs=[pl.BlockSpec((S, LANES), lambda b: (b, 0)), pl.BlockSpec((8, LANES), lambda b: (0, 0))],
        out_shape=[jax.ShapeDtypeStruct((T, LANES), MXU_DTYPE), jax.ShapeDtypeStruct((8, LANES), f32)],
        compiler_params=_cparams("arbitrary"),
    )(P, bias, alog, dgates)


GDN_BLOCKS = 3 * GDN_HEADS


def _shift_rows(v, d, reverse=False):
    if d == 0:
        return v
    n = v.shape[0]
    row = lax.broadcasted_iota(jnp.int32, v.shape, 0)
    if reverse:
        return jnp.where(row + d < n, pltpu.roll(v, n - d, 0), 0.0)
    return jnp.where(row >= d, pltpu.roll(v, d, 0), 0.0)


def _conv_silu(x, w):
    pre = sum(w[j:j + 1, :] * _shift_rows(x, CONV_WIDTH - 1 - j) for j in range(CONV_WIDTH))
    return pre, pre * jax.nn.sigmoid(pre)


def gdn_prep_fwd(P, conv_w, *, B):
    T = P.shape[0]
    S = T // B

    def body(x_ref, w_ref, o_ref):
        _, y = _conv_silu(x_ref[...], w_ref[...])
        yn = y * lax.rsqrt(jnp.sum(y * y, axis=-1, keepdims=True) + EPS)
        o_ref[...] = jnp.where(pl.program_id(1) < 2 * GDN_HEADS, yn, y)

    return pl.pallas_call(
        body, name="gdn_prep_fwd", grid=(B, GDN_BLOCKS),
        in_specs=[pl.BlockSpec((S, LANES), lambda b, j: (b, COL_GDN // LANES + j)),
                  pl.BlockSpec((CONV_WIDTH, LANES), lambda b, j: (0, j))],
        out_specs=pl.BlockSpec((S, LANES), lambda b, j: (b, j)),
        out_shape=jax.ShapeDtypeStruct((T, 3 * GDN_WIDTH), f32),
        compiler_params=_cparams("parallel", "parallel"),
    )(P, conv_w)


def gdn_prep_bwd(P, conv_w, dG, *, B):
    T = P.shape[0]
    S = T // B

    def body(x_ref, w_ref, dg_ref, dx_ref, dw_ref):
        x, w = x_ref[...], w_ref[...]
        pre, y = _conv_silu(x, w)
        dn = dg_ref[...]
        r = lax.rsqrt(jnp.sum(y * y, axis=-1, keepdims=True) + EPS)
        n = y * r
        dy_norm = r * (dn - n * jnp.sum(dn * n, axis=-1, keepdims=True))
        dy = jnp.where(pl.program_id(0) < 2 * GDN_HEADS, dy_norm, dn)
        sg = jax.nn.sigmoid(pre)
        dpre = dy * (sg * (1.0 + pre * (1.0 - sg)))
        dx = sum(w[j:j + 1, :] * _shift_rows(dpre, CONV_WIDTH - 1 - j, reverse=True) for j in range(CONV_WIDTH))
        dx_ref[...] = dx.astype(dx_ref.dtype)

        @pl.when(pl.program_id(1) == 0)
        def _():
            dw_ref[...] = jnp.zeros_like(dw_ref)

        for j in range(CONV_WIDTH):
            dw_ref[j:j + 1, :] += jnp.sum(dpre * _shift_rows(x, CONV_WIDTH - 1 - j), axis=0, keepdims=True)

    return pl.pallas_call(
        body, name="gdn_prep_bwd", grid=(GDN_BLOCKS, B),
        in_specs=[pl.BlockSpec((S, LANES), lambda j, b: (b, COL_GDN // LANES + j)),
                  pl.BlockSpec((CONV_WIDTH, LANES), lambda j, b: (0, j)),
                  pl.BlockSpec((S, LANES), lambda j, b: (b, j))],
        out_specs=[pl.BlockSpec((S, LANES), lambda j, b: (b, j)),
                   pl.BlockSpec((CONV_WIDTH, LANES), lambda j, b: (0, j))],
        out_shape=[jax.ShapeDtypeStruct((T, 3 * GDN_WIDTH), MXU_DTYPE),
                   jax.ShapeDtypeStruct((CONV_WIDTH, 3 * GDN_WIDTH), f32)],
        compiler_params=_cparams("arbitrary", "arbitrary"),
    )(P, conv_w, dG)


GDN_GROUP = 16
B_NN = (((2,), (1,)), ((0,), (0,)))
B_NT = (((2,), (2,)), ((0,), (0,)))
B_TN = (((1,), (1,)), ((0,), (0,)))


def _bmm(a, b, dims, precision=None):
    if precision is None:
        a, b = _mx(a), _mx(b)
    return lax.dot_general(a, b, dims, preferred_element_type=f32, precision=precision)


def _tri_inverse(A):
    C = A.shape[-1]
    row = lax.broadcasted_iota(jnp.int32, A.shape, 1)
    col = lax.broadcasted_iota(jnp.int32, A.shape, 2)
    eye = (row == col).astype(f32)
    X = jnp.where((row // 4) == (col // 4), -A, 0.0)
    X2 = _bmm(X, X, B_NN, INV_PRECISION)
    Tm = eye + X + X2 + _bmm(X, X2, B_NN, INV_PRECISION)
    b = 4
    while b < C:
        off = ((row // (2 * b)) == (col // (2 * b))) & ((row // b) != (col // b))
        Tm = Tm - _bmm(_bmm(Tm, jnp.where(off, A, 0.0), B_NN, INV_PRECISION), Tm, B_NN, INV_PRECISION)
        b *= 2
    return Tm


def _pick_lane(block, lane_idx):
    lane = lax.broadcasted_iota(jnp.int32, block.shape, 1)
    return jnp.sum(jnp.where(lane == lane_idx, block, 0.0), axis=1, keepdims=True)


def _gdn_local(q, k, v, beta, gc, Tm=None):
    C = GDN_CHUNK
    n = q.shape[0] // C
    q = q.reshape(n, C, -1) * (GDN_HEAD_DIM ** -0.5)
    k = k.reshape(n, C, -1)
    v = v.reshape(n, C, -1)
    beta = beta.reshape(n, C, 1)
    gc = gc.reshape(n, C, 1)
    row = lax.broadcasted_iota(jnp.int32, (n, C, C), 1)
    col = lax.broadcasted_iota(jnp.int32, (n, C, C), 2)
    gcT = jnp.swapaxes(jnp.broadcast_to(gc, (n, C, C)), 1, 2)
    D = jnp.exp(jnp.where(row >= col, gc - gcT, NEG_INF))
    kb = k * beta
    vb = v * beta
    A = jnp.where(row > col, _bmm(kb, k, B_NT) * D, 0.0)
    Gam = jnp.exp(gc)
    kg = kb * Gam
    gl = gc[:, C - 1:C, :]
    kdec = jnp.exp(gl - gc)
    loc = dict(q=q, k=k, v=v, beta=beta, gc=gc, D=D, kb=kb, vb=vb, A=A, Gam=Gam, kg=kg,
               kdec=kdec, kd=k * kdec, qg=q * Gam, gam=jnp.exp(gl), row=row, col=col)
    if Tm is None:
        Tm = _tri_inverse(A)
        loc.update(u=_bmm(Tm, vb, B_NN), w=_bmm(Tm, kg, B_NN), M=_bmm(q, k, B_NT) * D)
    else:
        Tm = Tm.reshape(n, C, C)
    loc["Tm"] = Tm
    return loc


def _gdn_store_local(loc, r0, u_s, w_s, qg_s, kd_s, M_s, gam_s, c0):
    n = loc["u"].shape[0]
    R = n * GDN_CHUNK
    u_s[pl.ds(r0, R), :] = loc["u"].reshape(R, -1)
    w_s[pl.ds(r0, R), :] = loc["w"].reshape(R, -1)
    qg_s[pl.ds(r0, R), :] = loc["qg"].reshape(R, -1)
    kd_s[pl.ds(r0, R), :] = loc["kd"].reshape(R, -1)
    M_s[pl.ds(r0, R), :] = loc["M"].reshape(R, -1)
    gam_s[pl.ds(c0, n)] = jnp.broadcast_to(loc["gam"], (n, 1, LANES))


def _gdn_specs(S):
    blk = lambda off: pl.BlockSpec((S, LANES), lambda b, h: (b, off + h))
    return blk


def gdn_fwd(G, gates, P, g_on, *, B):
    T = G.shape[0]
    S = T // B
    C = GDN_CHUNK
    N = S // C
    grp = min(GDN_GROUP, N)
    R = grp * C
    hd = GDN_HEAD_DIM

    def body(q_ref, k_ref, v_ref, gt_ref, z_ref, gon_ref, o_ref, ob_ref, st_ref, A_s, B_s, Q_s, O_s, gam_s):
        h = pl.program_id(1)

        def local(gi, carry):
            r0 = pl.multiple_of(gi * R, R)
            gt = gt_ref[pl.ds(r0, R), :]
            loc = _gdn_local(q_ref[pl.ds(r0, R), :], k_ref[pl.ds(r0, R), :], v_ref[pl.ds(r0, R), :],
                             _pick_lane(gt, SM_B + h), _pick_lane(gt, SM_A + h))
            chunks = pl.ds(gi * grp, grp)
            A_s[chunks] = -_bmm(loc["kd"], loc["w"], B_TN)
            B_s[chunks] = _bmm(loc["kd"], loc["u"], B_TN)
            Q_s[pl.ds(r0, R), :] = (loc["qg"] - _bmm(loc["M"], loc["w"], B_NN)).reshape(R, hd)
            O_s[pl.ds(r0, R), :] = _bmm(loc["M"], loc["u"], B_NN).reshape(R, hd)
            gam_s[chunks] = jnp.broadcast_to(loc["gam"], (grp, 1, LANES))
            return carry

        lax.fori_loop(0, N // grp, local, 0)

        def step(n, state):
            st_ref[0, 0, n] = state
            return state * gam_s[n] + _dotm(A_s[n], state, NN) + B_s[n]

        lax.fori_loop(0, N, step, jnp.zeros((hd, hd), f32))

        def outputs(gi, carry):
            r0 = pl.multiple_of(gi * R, R)
            Q = Q_s[pl.ds(r0, R), :].reshape(grp, C, hd)
            o = _bmm(Q, st_ref[0, 0, pl.ds(gi * grp, grp)], B_NN).reshape(R, hd) + O_s[pl.ds(r0, R), :]
            o_ref[pl.ds(r0, R), :] = o
            return carry

        lax.fori_loop(0, N // grp, outputs, 0)
        o = o_ref[...]
        z = z_ref[...]
        ob_ref[...] = (_head_rms(o, gon_ref[...])[0] * (z * jax.nn.sigmoid(z))).astype(ob_ref.dtype)

    blk = lambda off: pl.BlockSpec((S, LANES), lambda b, h: (b, off + h))
    rows = lambda: pltpu.VMEM((S, hd), f32)
    return pl.pallas_call(
        body, name="gdn_fwd", grid=(B, GDN_HEADS),
        in_specs=[blk(0), blk(GDN_HEADS), blk(2 * GDN_HEADS), pl.BlockSpec((S, LANES), lambda b, h: (b, 0)),
                  blk(COL_Z // LANES), pl.BlockSpec((1, hd), lambda b, h: (0, 0))],
        out_specs=[blk(0), blk(0), pl.BlockSpec((1, 1, N, hd, hd), lambda b, h: (b, h, 0, 0, 0))],
        out_shape=[jax.ShapeDtypeStruct((T, GDN_WIDTH), f32), jax.ShapeDtypeStruct((T, GDN_WIDTH), MXU_DTYPE),
                   jax.ShapeDtypeStruct((B, GDN_HEADS, N, hd, hd), f32)],
        scratch_shapes=[pltpu.VMEM((N, hd, hd), f32), pltpu.VMEM((N, hd, hd), f32), rows(), rows(),
                        pltpu.VMEM((N, 1, LANES), f32)],
        compiler_params=_cparams("parallel", "parallel"),
    )(G, G, G, gates, P, g_on)


def gdn_bwd(G, gates, P, g_on, o_raw, states, d_oab, *, B):
    T = G.shape[0]
    S = T // B
    C = GDN_CHUNK
    N = S // C
    grp = min(GDN_GROUP, N)
    R = grp * C
    hd = GDN_HEAD_DIM

    def body(q_ref, k_ref, v_ref, gt_ref, z_ref, gon_ref, o_ref, st_ref, dob_ref,
             dq_ref, dk_ref, dv_ref, dgt_ref, dz_ref, dgon_ref,
             u_s, w_s, M_s, gam_s, do_s, Tm_s, A_s, C_s, dst_s):
        b, h = pl.program_id(0), pl.program_id(1)

        @pl.when((b == 0) & (h == 0))
        def _():
            dgon_ref[...] = jnp.zeros_like(dgon_ref)

        @pl.when(h == 0)
        def _():
            dgt_ref[...] = jnp.zeros_like(dgt_ref)

        def group_inputs(gi, Tm_of=None):
            r0 = pl.multiple_of(gi * R, R)
            gt = gt_ref[pl.ds(r0, R), :]
            Tm = None if Tm_of is None else Tm_of[pl.ds(r0, R), :]
            return r0, _gdn_local(q_ref[pl.ds(r0, R), :], k_ref[pl.ds(r0, R), :], v_ref[pl.ds(r0, R), :],
                                  _pick_lane(gt, SM_B + h), _pick_lane(gt, SM_A + h), Tm)

        def local(gi, carry):
            r0, loc = group_inputs(gi)
            rows, chunks = pl.ds(r0, R), pl.ds(gi * grp, grp)
            u_s[rows, :] = loc["u"].reshape(R, hd)
            w_s[rows, :] = loc["w"].reshape(R, hd)
            M_s[rows, :] = loc["M"].reshape(R, C)
            Tm_s[rows, :] = loc["Tm"].reshape(R, C)
            gam_s[chunks] = jnp.broadcast_to(loc["gam"], (grp, 1, LANES))
            o, z, gon = o_ref[rows, :], z_ref[rows, :], gon_ref[...]
            dob = dob_ref[rows, :]
            on, ro = _head_rms(o, gon)
            sz = jax.nn.sigmoid(z)
            dz_ref[rows, :] = (dob * on * (sz * (1.0 + z * (1.0 - sz)))).astype(dz_ref.dtype)
            do, dgon = _head_rms_bwd(o, ro, gon, dob * (z * sz))
            do_s[rows, :] = do
            dgon_ref[...] += dgon
            A_s[chunks] = -_bmm(loc["kd"], loc["w"], B_TN)
            C_s[chunks] = _bmm(loc["qg"] - _bmm(loc["M"], loc["w"], B_NN), do.reshape(grp, C, hd), B_TN)
            return carry

        lax.fori_loop(0, N // grp, local, 0)

        def step(t, dS):
            n = N - 1 - t
            dst_s[n] = dS
            return dS * gam_s[n] + _dotm(A_s[n], dS, TN) + C_s[n]

        lax.fori_loop(0, N, step, jnp.zeros((hd, hd), f32))

        def finish(gi, carry):
            r0, L = group_inputs(gi, Tm_s)
            n = grp
            rows, chunks = pl.ds(r0, R), pl.ds(gi * grp, grp)
            g3 = lambda ref: ref[rows, :].reshape(n, C, -1)
            u, w, do = g3(u_s), g3(w_s), g3(do_s)
            L["M"] = g3(M_s)
            state, dS = st_ref[0, 0, chunks], dst_s[chunks]
            v_new = u - _bmm(w, state, B_NN)
            du = _bmm(L["M"], do, B_TN) + _bmm(L["kd"], dS, B_NN)
            dw = -_bmm(du, state, B_NT)
            dqg = _bmm(do, state, B_NT)
            dM = _bmm(do, v_new, B_NT)
            dkd = _bmm(v_new, dS, B_NT)
            dgl_state = jnp.sum(jnp.sum(dS * state, axis=2, keepdims=True), axis=1, keepdims=True) * L["gam"]
            TmT = jnp.swapaxes(L["Tm"], 1, 2)
            dTm = _bmm(du, L["vb"], B_NT) + _bmm(dw, L["kg"], B_NT)
            dvb = _bmm(TmT, du, B_NN)
            dkg = _bmm(TmT, dw, B_NN)
            dA = jnp.where(L["row"] > L["col"], -_bmm(_bmm(TmT, dTm, B_NN), TmT, B_NN), 0.0)
            dKK = dA * L["D"]
            dQK = dM * L["D"]
            dkb = _bmm(dKK, L["k"], B_NN) + dkg * L["Gam"]
            dk = (_bmm(dKK, L["kb"], B_TN) + _bmm(dQK, L["q"], B_TN) + dkd * L["kdec"] + L["beta"] * dkb)
            dq = (_bmm(dQK, L["k"], B_NN) + dqg * L["Gam"]) * (GDN_HEAD_DIM ** -0.5)
            E = dA * L["A"] + dM * L["M"]
            r = jnp.sum(dkd * L["kd"], axis=-1, keepdims=True)
            dgc = (jnp.sum(E, axis=2, keepdims=True) - jnp.sum(jnp.swapaxes(E, 1, 2), axis=2, keepdims=True)
                   + jnp.sum(dkg * L["kg"], axis=-1, keepdims=True) + jnp.sum(dqg * L["qg"], axis=-1, keepdims=True) - r)
            dgl = jnp.sum(r, axis=1, keepdims=True) + dgl_state
            rowc = lax.broadcasted_iota(jnp.int32, (n, C, 1), 1)
            dgc = dgc + jnp.where(rowc == C - 1, dgl, 0.0)
            dbeta = jnp.sum(dkb * L["k"], axis=-1, keepdims=True) + jnp.sum(dvb * L["v"], axis=-1, keepdims=True)
            dq_ref[rows, :] = dq.reshape(R, hd)
            dk_ref[rows, :] = dk.reshape(R, hd)
            dv_ref[rows, :] = (L["beta"] * dvb).reshape(R, hd)
            lane = lax.broadcasted_iota(jnp.int32, (R, LANES), 1)
            dgt_ref[rows, :] += (jnp.where(lane == SM_B + h, dbeta.reshape(R, 1), 0.0)
                                 + jnp.where(lane == SM_A + h, dgc.reshape(R, 1), 0.0))
            return carry

        lax.fori_loop(0, N // grp, finish, 0)

    blk = lambda off: pl.BlockSpec((S, LANES), lambda b, h: (b, off + h))
    rows = lambda: pltpu.VMEM((S, hd), f32)
    return pl.pallas_call(
        body, name="gdn_bwd", grid=(B, GDN_HEADS),
        in_specs=[blk(0), blk(GDN_HEADS), blk(2 * GDN_HEADS), pl.BlockSpec((S, LANES), lambda b, h: (b, 0)),
                  blk(COL_Z // LANES), pl.BlockSpec((1, hd), lambda b, h: (0, 0)), blk(0),
                  pl.BlockSpec((1, 1, N, hd, hd), lambda b, h: (b, h, 0, 0, 0)), blk(GDN_HEADS)],
        out_specs=[blk(0), blk(0), blk(0), pl.BlockSpec((S, LANES), lambda b, h: (b, 0)), blk(0),
                   pl.BlockSpec((1, hd), lambda b, h: (0, 0))],
        out_shape=[jax.ShapeDtypeStruct((T, GDN_WIDTH), f32), jax.ShapeDtypeStruct((T, GDN_WIDTH), f32),
                   jax.ShapeDtypeStruct((T, GDN_WIDTH), f32), jax.ShapeDtypeStruct((T, LANES), f32),
                   jax.ShapeDtypeStruct((T, GDN_WIDTH), MXU_DTYPE), jax.ShapeDtypeStruct((1, hd), f32)],
        scratch_shapes=[rows(), rows(), pltpu.VMEM((S, C), f32), pltpu.VMEM((N, 1, LANES), f32), rows(),
                        pltpu.VMEM((S, C), f32), pltpu.VMEM((N, hd, hd), f32), pltpu.VMEM((N, hd, hd), f32),
                        pltpu.VMEM((N, hd, hd), f32)],
        compiler_params=_cparams("arbitrary", "arbitrary"),
    )(G, G, G, gates, P, g_on, o_raw, states, d_oab)


IN_SPLIT = (0, 1536, 1544, 3080, 3088, 3600)


IN_SHARD = IN_DIM // 4
IN_SHARD_PAD = 928


def align_w_in_t(wt):
    s = IN_SPLIT
    pad = jnp.zeros((IN_ALIGNED - IN_DIM, wt.shape[1]), wt.dtype)
    return jnp.concatenate([wt[s[0]:s[1]], wt[s[2]:s[3]], wt[s[4]:s[5]], wt[s[1]:s[2]], wt[s[3]:s[4]], pad], axis=0)


def unalign_w_in_t(wa):
    return jnp.concatenate([wa[0:1536], wa[COL_SMALL:COL_SMALL + 8], wa[1536:3072],
                            wa[COL_SMALL + 8:COL_SMALL + 16], wa[3072:3584]], axis=0)


def _lanes_vec(pieces):
    v = jnp.zeros((1, LANES), f32)
    for off, a in pieces:
        v = lax.dynamic_update_slice(v, a.astype(f32), (0, off))
    return v


def local_step(x, mem, target, w, sp, *, B):
    T = x.shape[0]
    S = T // B
    gq8, gk8 = jnp.tile(sp["fox_qnorm_g"], (1, FOX_HEADS)), jnp.tile(sp["fox_knorm_g"], (1, FOX_HEADS))
    go2 = jnp.tile(sp["fox_onorm_g"], (1, 2))
    bias = _lanes_vec([(SM_F, sp["fox_f_bias"]), (SM_A, sp["gdn_dt_bias"])])
    alog = _lanes_vec([(SM_A, sp["gdn_A_log"])])

    h1 = rms_fwd(x, sp["norm_mix_g"], name="rms_mix")
    P = matmul(h1, w["wa_t"], tb=True, name="mm_in", tn=IN_TILE)
    gates = gates_fwd(P, bias, alog, B=B)
    c = gates[:, SM_F:SM_F + FOX_HEADS].reshape(B, S, FOX_HEADS).transpose(0, 2, 1)
    ccol, crow = c[..., None], c.reshape(B, FOX_HEADS, S // FOX_TQ, 1, FOX_TQ)
    qn, kn, vb = fox_prep_fwd(P, gq8, gk8)
    o_raw, o_a, lse = fox_core_fwd(qn, kn, vb, ccol, crow, go2, B=B)
    G = gdn_prep_fwd(P, w["conv_w"], B=B)
    ob_raw, o_b, states = gdn_fwd(G, gates, P, sp["gdn_onorm_g"], B=B)
    oab = jnp.concatenate([o_a, o_b], axis=1)
    if "late" in w:
        w = {**w, **w["late"](oab)}
    x2 = matmul(oab, w["w_out"], residual=x, name="mm_out")
    hq = rms_fwd(x2, sp["norm_xattn_g"], name="rms_xattn")
    hm = rms_fwd(mem, sp["mem_norm_g"], name="rms_mem")
    cq = matmul(hq, w["w_cq"], name="mm_cq")
    ckv = matmul(hm, w["w_ckv"], name="mm_ckv")
    co = xattn_fwd(cq, ckv, sp["xattn_qnorm_g"], sp["xattn_knorm_g"], B=B)
    x3 = matmul(co, w["w_co"], b_stacked=True, residual=x2, name="mm_co")
    hf = rms_fwd(x3, sp["norm_mlp_g"], name="rms_mlp")
    act = matmul(hf, w["w_mlp1"], b_stacked=True, relu2_out=True, out_dtype=MXU_DTYPE, name="mm_mlp1")
    dy, loss = matmul_rows(act, w["w_mlp2"], (x3, target), mode="loss", name="mm_mlp2_loss")

    da = matmul(dy, w["w_mlp2"], tb=True, relu2_bwd_aux=act, out_dtype=MXU_DTYPE, name="mm_d_act")
    g_mlp2 = matmul(act, dy, ta=True, out_dtype=WIRE_DTYPE, name="mm_g_mlp2")
    g_mlp1 = matmul(hf, da, ta=True, out_stacked=True, out_dtype=WIRE_DTYPE, name="mm_g_mlp1")
    by_rows = lambda g: g.reshape(N_CHIPS, g.shape[0] // N_CHIPS, g.shape[1])
    early = w.get("grads_ready", lambda grads: jnp.zeros((1, 1), f32))
    tok = early(dict(w_mlp1=g_mlp1, w_mlp2=by_rows(g_mlp2)))[0, 0]
    dx3, g_norm_mlp = matmul_rows(da, w["w_mlp1"], (x3, sp["norm_mlp_g"] + tok, dy), mode="rms_bwd", tb=True,
                                  b_stacked=True, name="mm_d_hf_rms")
    dco = matmul(dx3, w["w_co"], tb=True, b_stacked=True, name="mm_d_co")
    g_co = matmul(co, dx3, ta=True, out_stacked=True, out_dtype=WIRE_DTYPE, name="mm_g_co")
    dcq, dckv, g_xq, g_xk = xattn_bwd(cq, ckv, sp["xattn_qnorm_g"], sp["xattn_knorm_g"], dco, B=B)
    g_cq = matmul(hq, dcq, ta=True, out_dtype=WIRE_DTYPE, name="mm_g_cq")
    g_ckv = matmul(hm, dckv, ta=True, out_dtype=WIRE_DTYPE, name="mm_g_ckv")
    _, g_mem_norm = matmul_rows(dckv, w["w_ckv"], (mem, sp["mem_norm_g"], None), mode="rms_bwd", tb=True, name="mm_d_hm_rms")
    dx2, g_norm_xattn = matmul_rows(dcq, w["w_cq"], (x2, sp["norm_xattn_g"], dx3), mode="rms_bwd", tb=True, name="mm_d_hq_rms")
    doab = matmul(dx2, w["w_out"], tb=True, name="mm_d_oab")
    g_out = matmul(oab, dx2, ta=True, out_dtype=WIRE_DTYPE, name="mm_g_out")
    tok = early(dict(w_co=g_co, w_cq=by_rows(g_cq), w_ckv=by_rows(g_ckv), w_out=by_rows(g_out)))[0, 0]
    dqn, dkn, dv_f, dccol, dcrow, dgo2 = fox_core_bwd(qn, kn, vb, ccol, crow, go2 + tok, o_raw, lse, doab, B=B)
    dq_f, dk_f, dgq8, dgk8 = fox_prep_bwd(P, gq8, gk8, dqn, dkn)
    dGq, dGk, dGv, dgt, dz, g_gdn_on = gdn_bwd(G, gates, P, sp["gdn_onorm_g"], ob_raw, states, doab, B=B)
    dPg, g_conv = gdn_prep_bwd(P, w["conv_w"], jnp.concatenate([dGq, dGk, dGv], axis=1), B=B)
    dc = (dccol[..., 0] + dcrow.reshape(B, FOX_HEADS, S)).transpose(0, 2, 1).reshape(T, FOX_HEADS)
    dgates = dgt + jnp.pad(dc, ((0, 0), (SM_F, LANES - SM_F - FOX_HEADS)))
    dsmall, par = gates_bwd(P, bias, alog, dgates, B=B)
    dP = jnp.concatenate([dq_f, dk_f, dv_f, dPg, dz, dsmall, jnp.zeros((T, IN_ALIGNED - COL_SMALL - LANES), MXU_DTYPE)], axis=1)
    g_wa = matmul(dP, h1, ta=True, out_dtype=WIRE_DTYPE, name="mm_g_in", tm=IN_TILE)
    dx, g_norm_mix = matmul_rows(dP, w["wa_t"], (x, sp["norm_mix_g"], dx2), mode="rms_bwd", tk=IN_TILE, name="mm_d_h1_rms")

    fold = lambda g: jnp.sum(g.reshape(-1, FOX_HEAD_DIM), axis=0, keepdims=True)
    g_in = jnp.pad(unalign_w_in_t(g_wa).reshape(N_CHIPS, IN_SHARD, D_MODEL), ((0, 0), (0, IN_SHARD_PAD - IN_SHARD), (0, 0)))
    big = dict(w_in=g_in, w_out=by_rows(g_out), w_cq=by_rows(g_cq), w_ckv=by_rows(g_ckv), w_co=g_co, w_mlp1=g_mlp1,
               w_mlp2=by_rows(g_mlp2))
    small = dict(norm_mix_g=g_norm_mix, fox_qnorm_g=fold(dgq8), fox_knorm_g=fold(dgk8),
                 fox_f_bias=par[0:1, SM_F:SM_F + FOX_HEADS], fox_onorm_g=fold(dgo2), gdn_conv_w=g_conv,
                 gdn_A_log=par[1:2, SM_A:SM_A + GDN_HEADS], gdn_dt_bias=par[0:1, SM_A:SM_A + GDN_HEADS],
                 gdn_onorm_g=g_gdn_on, norm_xattn_g=g_norm_xattn, mem_norm_g=g_mem_norm,
                 xattn_qnorm_g=g_xq, xattn_knorm_g=g_xk, norm_mlp_g=g_norm_mlp)
    return loss, dx, big, small


MESH_IDS = pl.DeviceIdType.MESH
N_CHIPS = 4
HBM_SPEC = pl.BlockSpec(memory_space=pltpu.HBM)
PACK_ROWS = 30720
PACK_HALF = PACK_ROWS // 2
PACK_BLOCK = 3072


def _place():
    return lax.axis_index("x"), lax.axis_index("y"), lax.axis_index("c")


def _other_chips(x, y):
    return [(1 - x, y), (x, 1 - y), (1 - x, 1 - y)]


def _remote(src, dst, send_sem, recv_sem, to):
    return pltpu.make_async_remote_copy(src_ref=src, dst_ref=dst, send_sem=send_sem, recv_sem=recv_sem,
                                        device_id=to, device_id_type=MESH_IDS)


def all_gather_shards(packed):
    half = PACK_HALF

    def body(src_ref, out_ref, send_sems, recv_sems):
        x, y, c = _place()
        me_chip = 2 * x + y
        sibling = (x, y, 1 - c)
        chips = _other_chips(x, y)

        def rows(chip, core):
            return out_ref.at[chip, pl.ds(core * half, half), :]

        sends = [_remote(src_ref.at[pl.ds(c * half, half), :], rows(me_chip, c), send_sems.at[j], recv_sems.at[j], (px, py, c))
                 for j, (px, py) in enumerate(chips)]
        for cp in sends:
            cp.start()
        passed = []
        for j, (px, py) in enumerate(chips):
            theirs = rows(2 * px + py, c)
            _remote(theirs, theirs, send_sems.at[j], recv_sems.at[j], (px, py, c)).wait_recv()
            cp = _remote(theirs, theirs, send_sems.at[3 + j], recv_sems.at[3 + j], sibling)
            cp.start()
            passed.append(cp)
        for j, (px, py) in enumerate(chips):
            theirs = rows(2 * px + py, 1 - c)
            _remote(theirs, theirs, send_sems.at[3 + j], recv_sems.at[3 + j], sibling).wait_recv()
        for cp in sends + passed:
            cp.wait_send()

    return pl.pallas_call(
        body, name="all_gather_shards", in_specs=[HBM_SPEC], out_specs=HBM_SPEC,
        out_shape=jax.ShapeDtypeStruct((N_CHIPS,) + packed.shape, packed.dtype),
        scratch_shapes=[pltpu.SemaphoreType.DMA((6,)), pltpu.SemaphoreType.DMA((6,))],
    )(packed)


def exchange_core_halves(G):
    half = PACK_HALF

    def body(g_ref, land_ref, send_sem, recv_sem):
        x, y, c = _place()
        cp = _remote(g_ref.at[:, pl.ds((1 - c) * half, half), :], land_ref, send_sem, recv_sem, (x, y, 1 - c))
        cp.start()
        cp.wait()

    return pl.pallas_call(
        body, name="exchange_core_halves", in_specs=[HBM_SPEC], out_specs=HBM_SPEC,
        out_shape=jax.ShapeDtypeStruct((N_CHIPS, half, LANES), G.dtype),
        scratch_shapes=[pltpu.SemaphoreType.DMA(()), pltpu.SemaphoreType.DMA(())],
    )(G)


def add_core_halves(G, land, core):
    nb = PACK_HALF // PACK_BLOCK

    def body(c_ref, g_ref, l_ref, o_ref):
        o_ref[...] = (g_ref[...].astype(f32) + l_ref[...].astype(f32)).astype(o_ref.dtype)

    blk = (1, PACK_BLOCK, LANES)
    return pl.pallas_call(
        body, name="add_core_halves",
        grid_spec=pltpu.PrefetchScalarGridSpec(
            num_scalar_prefetch=1, grid=(N_CHIPS, nb),
            in_specs=[pl.BlockSpec(blk, lambda k, i, c_ref: (k, c_ref[0] * nb + i, 0)),
                      pl.BlockSpec(blk, lambda k, i, c_ref: (k, i, 0))],
            out_specs=pl.BlockSpec(blk, lambda k, i, c_ref: (k, i, 0))),
        out_shape=jax.ShapeDtypeStruct(land.shape, land.dtype),
        compiler_params=_cparams("parallel", "parallel"),
    )(core, G, land)


def scatter_to_chips(part):
    def body(p_ref, land_ref, send_sems, recv_sems):
        x, y, c = _place()
        me_chip = 2 * x + y
        chips = _other_chips(x, y)
        sends = [_remote(p_ref.at[2 * px + py], land_ref.at[me_chip], send_sems.at[j], recv_sems.at[j], (px, py, c))
                 for j, (px, py) in enumerate(chips)]
        for cp in sends:
            cp.start()
        for j, (px, py) in enumerate(chips):
            slot = land_ref.at[2 * px + py]
            _remote(slot, slot, send_sems.at[j], recv_sems.at[j], (px, py, c)).wait_recv()
        for cp in sends:
            cp.wait_send()

    return pl.pallas_call(
        body, name="scatter_to_chips", in_specs=[HBM_SPEC], out_specs=HBM_SPEC,
        out_shape=jax.ShapeDtypeStruct(part.shape, part.dtype),
        scratch_shapes=[pltpu.SemaphoreType.DMA((3,)), pltpu.SemaphoreType.DMA((3,))],
    )(part)


def sum_chips(part, land, order):
    nb = PACK_HALF // PACK_BLOCK

    def body(order_ref, p_ref, l1_ref, l2_ref, l3_ref, o_ref):
        o_ref[...] = ((p_ref[0].astype(f32) + l1_ref[0].astype(f32)) + l2_ref[0].astype(f32)) + l3_ref[0].astype(f32)

    slot = lambda j: pl.BlockSpec((1, PACK_BLOCK, LANES), lambda i, order_ref: (order_ref[j], i, 0))
    return pl.pallas_call(
        body, name="sum_chips",
        grid_spec=pltpu.PrefetchScalarGridSpec(
            num_scalar_prefetch=1, grid=(nb,), in_specs=[slot(0), slot(1), slot(2), slot(3)],
            out_specs=pl.BlockSpec((PACK_BLOCK, LANES), lambda i, order_ref: (i, 0))),
        out_shape=jax.ShapeDtypeStruct((PACK_HALF, LANES), f32),
        compiler_params=_cparams("parallel"),
    )(order, part, land, land, land)


def swap_core_halves(red):
    def body(r_ref, out_ref, send_sem, recv_sem):
        x, y, c = _place()
        cp = _remote(r_ref, out_ref, send_sem, recv_sem, (x, y, 1 - c))
        cp.start()
        cp.wait()

    return pl.pallas_call(
        body, name="swap_core_halves", in_specs=[HBM_SPEC], out_specs=HBM_SPEC,
        out_shape=jax.ShapeDtypeStruct(red.shape, red.dtype),
        scratch_shapes=[pltpu.SemaphoreType.DMA(()), pltpu.SemaphoreType.DMA(())],
    )(red)


def _half(ref, core):
    rows = ref.shape[-2] // 2
    return ref.at[(slice(None),) * (len(ref.shape) - 2) + (pl.ds(core * rows, rows), slice(None))]


def gather_weights(shards, conv):
    n = len(shards)

    def body(*refs):
        src, conv_src = refs[:n], refs[n]
        out, conv_out = refs[n + 1:2 * n + 1], refs[2 * n + 1]
        send_sems, recv_sems = refs[2 * n + 2], refs[2 * n + 3]
        x, y, c = _place()
        me_chip = 2 * x + y
        sibling = (x, y, 1 - c)
        chips = _other_chips(x, y)
        sends = []
        for a in range(n):
            for j, (px, py) in enumerate(chips):
                sends.append(_remote(_half(src[a], c), _half(out[a].at[me_chip], c),
                                     send_sems.at[6 * a + j], recv_sems.at[6 * a + j], (px, py, c)))
        for j, (px, py) in enumerate(chips):
            sends.append(_remote(conv_src, conv_out.at[me_chip], send_sems.at[6 * n + j], recv_sems.at[6 * n + j], (px, py, c)))
        for cp in sends:
            cp.start()
        passed = []
        for a in range(n):
            for j, (px, py) in enumerate(chips):
                theirs = _half(out[a].at[2 * px + py], c)
                _remote(theirs, theirs, send_sems.at[6 * a + j], recv_sems.at[6 * a + j], (px, py, c)).wait_recv()
                cp = _remote(theirs, theirs, send_sems.at[6 * a + 3 + j], recv_sems.at[6 * a + 3 + j], sibling)
                cp.start()
                passed.append(cp)
        for j, (px, py) in enumerate(chips):
            theirs = conv_out.at[2 * px + py]
            _remote(theirs, theirs, send_sems.at[6 * n + j], recv_sems.at[6 * n + j], (px, py, c)).wait_recv()
        for a in range(n):
            for j, (px, py) in enumerate(chips):
                theirs = _half(out[a].at[2 * px + py], 1 - c)
                _remote(theirs, theirs, send_sems.at[6 * a + 3 + j], recv_sems.at[6 * a + 3 + j], sibling).wait_recv()
        for cp in sends + passed:
            cp.wait_send()

    return pl.pallas_call(
        body, name="gather_weights", in_specs=[HBM_SPEC] * (n + 1), out_specs=[HBM_SPEC] * (n + 1),
        out_shape=[jax.ShapeDtypeStruct((N_CHIPS,) + s.shape, s.dtype) for s in list(shards) + [conv]],
        scratch_shapes=[pltpu.SemaphoreType.DMA((6 * n + 3,)), pltpu.SemaphoreType.DMA((6 * n + 3,))],
    )(*shards, conv)


SEM_SPEC = pl.BlockSpec(memory_space=pltpu.SEMAPHORE)
SPLIT_EFFECT = pltpu.SideEffectType.DATAFLOW_SIDE_EFFECTING


def _gather_async_copies(src, land, send_sems, recv_sems, x, y, c):
    me_chip = 2 * x + y
    sends, arrivals = [], []
    for a in range(len(src)):
        for j, (px, py) in enumerate(_other_chips(x, y)):
            for core in range(2):
                sends.append(_remote(_half(src[a], c), _half(land[a].at[me_chip], c), send_sems.at[6 * a + 2 * j + core],
                                     recv_sems.at[6 * a + 2 * j + c], (px, py, core)))
                theirs = _half(land[a].at[2 * px + py], core)
                arrivals.append(_remote(theirs, theirs, send_sems.at[6 * a + 2 * j + core],
                                        recv_sems.at[6 * a + 2 * j + core], (px, py, core)))
    return sends, arrivals


def gather_weights_start(shards, after):
    n = len(shards)

    def body(*refs):
        src, land = refs[:n], refs[n:2 * n]
        send_sems, recv_sems, token = refs[2 * n + 1], refs[2 * n + 2], refs[4 * n + 3]
        x, y, c = _place()
        for cp in _gather_async_copies(src, land, send_sems, recv_sems, x, y, c)[0]:
            cp.start()
        token[...] = jnp.zeros_like(token)

    zones = [pltpu.with_memory_space_constraint(lax.empty((N_CHIPS,) + s.shape, s.dtype), pltpu.HBM) for s in shards]
    srcs = [pltpu.with_memory_space_constraint(s, pltpu.HBM) for s in shards]
    out = pl.pallas_call(
        body, name="gather_weights_start",
        out_shape=[pltpu.SemaphoreType.DMA((6 * n,)), pltpu.SemaphoreType.DMA((6 * n,))]
        + [pltpu.HBM(s.shape, s.dtype) for s in shards] + [pltpu.HBM(z.shape, z.dtype) for z in zones]
        + [jax.ShapeDtypeStruct((8, LANES), f32)],
        in_specs=[HBM_SPEC] * (2 * n) + [pl.BlockSpec(memory_space=pl.ANY)],
        out_specs=[SEM_SPEC, SEM_SPEC] + [HBM_SPEC] * (2 * n) + [pl.BlockSpec(memory_space=pltpu.VMEM)],
        input_output_aliases={i: 2 + i for i in range(2 * n)},
        compiler_params=pltpu.CompilerParams(has_side_effects=SPLIT_EFFECT),
    )(*srcs, *zones, after)
    return out[0], out[1], out[2:2 + n], out[2 + n:2 + 2 * n], out[-1]


def gather_weights_wait(send_sems, recv_sems, shards, zones, after):
    n = len(shards)

    def body(*refs):
        src, land = refs[:n], refs[n:2 * n]
        send_sems, recv_sems = refs[2 * n], refs[2 * n + 1]
        x, y, c = _place()
        sends, arrivals = _gather_async_copies(src, land, send_sems, recv_sems, x, y, c)
        for cp in sends:
            cp.wait_send()
        for cp in arrivals:
            cp.wait_recv()

    out = pl.pallas_call(
        body, name="gather_weights_wait",
        out_shape=[pltpu.HBM(s.shape, s.dtype) for s in shards] + [pltpu.HBM(z.shape, z.dtype) for z in zones],
        in_specs=[HBM_SPEC] * (2 * n) + [SEM_SPEC, SEM_SPEC, pl.BlockSpec(memory_space=pl.ANY)],
        out_specs=[HBM_SPEC] * (2 * n),
        input_output_aliases={i: i for i in range(2 * n)},
        compiler_params=pltpu.CompilerParams(has_side_effects=SPLIT_EFFECT),
    )(*shards, *zones, send_sems, recv_sems, after)
    return out[n:]


def swap_grad_halves(grads, *, name):
    n = len(grads)

    def body(*refs):
        g, land, send_sems, recv_sems = refs[:n], refs[n:2 * n], refs[2 * n], refs[2 * n + 1]
        x, y, c = _place()
        copies = [_remote(_half(g[a], 1 - c), land[a], send_sems.at[a], recv_sems.at[a], (x, y, 1 - c)) for a in range(n)]
        for cp in copies:
            cp.start()
        for cp in copies:
            cp.wait()

    return pl.pallas_call(
        body, name=name, in_specs=[HBM_SPEC] * n, out_specs=[HBM_SPEC] * n,
        out_shape=[jax.ShapeDtypeStruct((N_CHIPS, g.shape[1] // 2, g.shape[2]), g.dtype) for g in grads],
        scratch_shapes=[pltpu.SemaphoreType.DMA((n,)), pltpu.SemaphoreType.DMA((n,))],
    )(*grads)


GRAD_ROWS = 256


def add_grad_halves(g, land, core, *, name):
    _, half, cols = land.shape
    tr = GRAD_ROWS if half % GRAD_ROWS == 0 else half
    nb = half // tr

    def body(c_ref, g_ref, l_ref, o_ref):
        o_ref[...] = (g_ref[...].astype(f32) + l_ref[...].astype(f32)).astype(o_ref.dtype)

    blk = (1, tr, cols)
    return pl.pallas_call(
        body, name=name,
        grid_spec=pltpu.PrefetchScalarGridSpec(
            num_scalar_prefetch=1, grid=(N_CHIPS, nb),
            in_specs=[pl.BlockSpec(blk, lambda k, i, c_ref: (k, c_ref[0] * nb + i, 0)),
                      pl.BlockSpec(blk, lambda k, i, c_ref: (k, i, 0))],
            out_specs=pl.BlockSpec(blk, lambda k, i, c_ref: (k, i, 0))),
        out_shape=jax.ShapeDtypeStruct(land.shape, land.dtype),
        compiler_params=_cparams("parallel", "parallel"),
    )(core, g, land)


def scatter_grads(parts):
    n = len(parts)

    def body(*refs):
        p, land, send_sems, recv_sems = refs[:n], refs[n:2 * n], refs[2 * n], refs[2 * n + 1]
        x, y, c = _place()
        me_chip = 2 * x + y
        chips = _other_chips(x, y)
        sends = [_remote(p[a].at[2 * px + py], land[a].at[me_chip], send_sems.at[3 * a + j], recv_sems.at[3 * a + j], (px, py, c))
                 for a in range(n) for j, (px, py) in enumerate(chips)]
        for cp in sends:
            cp.start()
        for a in range(n):
            for j, (px, py) in enumerate(chips):
                slot = land[a].at[2 * px + py]
                _remote(slot, slot, send_sems.at[3 * a + j], recv_sems.at[3 * a + j], (px, py, c)).wait_recv()
        for cp in sends:
            cp.wait_send()

    return pl.pallas_call(
        body, name="scatter_grads", in_specs=[HBM_SPEC] * n, out_specs=[HBM_SPEC] * n,
        out_shape=[jax.ShapeDtypeStruct(p.shape, p.dtype) for p in parts],
        scratch_shapes=[pltpu.SemaphoreType.DMA((3 * n,)), pltpu.SemaphoreType.DMA((3 * n,))],
    )(*parts)


def _scatter_async_copies(parts, land, send_sems, recv_sems, x, y, c):
    me_chip = 2 * x + y
    sends, arrivals = [], []
    for a in range(len(parts)):
        for j, (px, py) in enumerate(_other_chips(x, y)):
            sems = (send_sems.at[3 * a + j], recv_sems.at[3 * a + j], (px, py, c))
            sends.append(_remote(parts[a].at[2 * px + py], land[a].at[me_chip], *sems))
            slot = land[a].at[2 * px + py]
            arrivals.append(_remote(slot, slot, *sems))
    return sends, arrivals


def scatter_grads_start(parts, *, name):
    n = len(parts)

    def body(*refs):
        p, land = refs[:n], refs[n:2 * n]
        send_sems, recv_sems, token = refs[2 * n], refs[2 * n + 1], refs[4 * n + 2]
        x, y, c = _place()
        for cp in _scatter_async_copies(p, land, send_sems, recv_sems, x, y, c)[0]:
            cp.start()
        token[...] = jnp.zeros_like(token)

    zones = [pltpu.with_memory_space_constraint(lax.empty(p.shape, p.dtype), pltpu.HBM) for p in parts]
    srcs = [pltpu.with_memory_space_constraint(p, pltpu.HBM) for p in parts]
    hbm = [pltpu.HBM(p.shape, p.dtype) for p in parts]
    out = pl.pallas_call(
        body, name=name,
        out_shape=[pltpu.SemaphoreType.DMA((3 * n,)), pltpu.SemaphoreType.DMA((3 * n,))] + hbm + hbm
        + [jax.ShapeDtypeStruct((8, LANES), f32)],
        in_specs=[HBM_SPEC] * (2 * n),
        out_specs=[SEM_SPEC, SEM_SPEC] + [HBM_SPEC] * (2 * n) + [pl.BlockSpec(memory_space=pltpu.VMEM)],
        input_output_aliases={i: 2 + i for i in range(2 * n)},
        compiler_params=pltpu.CompilerParams(has_side_effects=SPLIT_EFFECT),
    )(*srcs, *zones)
    return out[0], out[1], out[2:2 + n], out[2 + n:2 + 2 * n], out[-1]


def scatter_grads_wait(send_sems, recv_sems, parts, zones, after, *, name):
    n = len(parts)

    def body(*refs):
        p, land = refs[:n], refs[n:2 * n]
        x, y, c = _place()
        sends, arrivals = _scatter_async_copies(p, land, refs[2 * n], refs[2 * n + 1], x, y, c)
        for cp in sends:
            cp.wait_send()
        for cp in arrivals:
            cp.wait_recv()

    hbm = [pltpu.HBM(p.shape, p.dtype) for p in parts]
    out = pl.pallas_call(
        body, name=name, out_shape=hbm + hbm,
        in_specs=[HBM_SPEC] * (2 * n) + [SEM_SPEC, SEM_SPEC, pl.BlockSpec(memory_space=pl.ANY)],
        out_specs=[HBM_SPEC] * (2 * n),
        input_output_aliases={i: i for i in range(2 * n)},
        compiler_params=pltpu.CompilerParams(has_side_effects=SPLIT_EFFECT),
    )(*parts, *zones, send_sems, recv_sems, after)
    return out[:n], out[n:]


def sum_grads(part, land, order, *, name):
    _, half, cols = part.shape
    tr = GRAD_ROWS if half % GRAD_ROWS == 0 else half

    def body(order_ref, p_ref, l1_ref, l2_ref, l3_ref, o_ref):
        o_ref[...] = ((p_ref[0].astype(f32) + l1_ref[0].astype(f32)) + l2_ref[0].astype(f32)) + l3_ref[0].astype(f32)

    slot = lambda j: pl.BlockSpec((1, tr, cols), lambda i, order_ref: (order_ref[j], i, 0))
    return pl.pallas_call(
        body, name=name,
        grid_spec=pltpu.PrefetchScalarGridSpec(
            num_scalar_prefetch=1, grid=(half // tr,), in_specs=[slot(0), slot(1), slot(2), slot(3)],
            out_specs=pl.BlockSpec((tr, cols), lambda i, order_ref: (i, 0))),
        out_shape=jax.ShapeDtypeStruct((half, cols), f32),
        compiler_params=_cparams("parallel"),
    )(order, part, land, land, land)


def swap_reduced_halves(mine):
    n = len(mine)

    def body(*refs):
        r, out, send_sems, recv_sems = refs[:n], refs[n:2 * n], refs[2 * n], refs[2 * n + 1]
        x, y, c = _place()
        copies = [_remote(r[a], out[a], send_sems.at[a], recv_sems.at[a], (x, y, 1 - c)) for a in range(n)]
        for cp in copies:
            cp.start()
        for cp in copies:
            cp.wait()

    return pl.pallas_call(
        body, name="swap_reduced_halves", in_specs=[HBM_SPEC] * n, out_specs=[HBM_SPEC] * n,
        out_shape=[jax.ShapeDtypeStruct(r.shape, r.dtype) for r in mine],
        scratch_shapes=[pltpu.SemaphoreType.DMA((n,)), pltpu.SemaphoreType.DMA((n,))],
    )(*mine)


def adamw_halves(w, mine, theirs, m, v, core, *, name):
    R, C = w.shape
    tr = min(GRAD_ROWS, R // 2)
    half_nb = R // 2 // tr

    def body(c_ref, w_ref, a_ref, b_ref, m_ref, v_ref, g_ref, d_ref, nm_ref, nv_ref):
        low = pl.program_id(0) < half_nb
        gv = jnp.where(low == (c_ref[0] == 0), a_ref[...], b_ref[...])
        nm = ADAM_B1 * m_ref[...] + (1.0 - ADAM_B1) * gv
        nv = ADAM_B2 * v_ref[...] + (1.0 - ADAM_B2) * jnp.square(gv)
        m_hat = nm / (1.0 - ADAM_B1 ** ADAM_STEP)
        v_hat = nv / (1.0 - ADAM_B2 ** ADAM_STEP)
        g_ref[...] = gv
        d_ref[...] = -ADAM_LR * (m_hat / (jnp.sqrt(v_hat) + ADAM_EPS) + ADAM_WD * w_ref[...])
        nm_ref[...] = nm
        nv_ref[...] = nv

    full = pl.BlockSpec((tr, C), lambda i, c_ref: (i, 0))
    part = pl.BlockSpec((tr, C), lambda i, c_ref: (i % half_nb, 0))
    out = jax.ShapeDtypeStruct((R, C), f32)
    return pl.pallas_call(
        body, name=name,
        grid_spec=pltpu.PrefetchScalarGridSpec(
            num_scalar_prefetch=1, grid=(2 * half_nb,), in_specs=[full, part, part, full, full], out_specs=[full] * 4),
        out_shape=[out] * 4, compiler_params=_cparams("parallel"),
    )(core, w, mine, theirs, m, v)


N_DEV = 8


def all_reduce_small(v):
    def body(src_ref, out_ref, land_ref, send_sems, recv_sems):
        x, y, c = _place()
        me = 4 * x + 2 * y + c
        copies = []
        for r in range(1, N_DEV):
            peer = ((1 - x) if r & 4 else x, (1 - y) if r & 2 else y, (1 - c) if r & 1 else c)
            copies.append(_remote(src_ref, land_ref.at[r], send_sems.at[r - 1], recv_sems.at[r - 1], peer))
        for cp in copies:
            cp.start()
        land_ref[0] = src_ref[...]
        for cp in copies:
            cp.wait()
        acc = land_ref[me]
        for d in range(1, N_DEV):
            acc = acc + land_ref[jnp.bitwise_xor(me, d)]
        out_ref[...] = acc

    vm = pl.BlockSpec(memory_space=pltpu.VMEM)
    return pl.pallas_call(
        body, name="all_reduce_small", in_specs=[vm], out_specs=vm,
        out_shape=jax.ShapeDtypeStruct(v.shape, v.dtype),
        scratch_shapes=[pltpu.VMEM((N_DEV,) + v.shape, v.dtype),
                        pltpu.SemaphoreType.DMA((N_DEV - 1,)), pltpu.SemaphoreType.DMA((N_DEV - 1,))],
    )(v)


def adamw(w, g, m, v, *, name, tr=None, tc=None):
    R, C = w.shape
    if tc is None:
        tr, tc = min(tr, R), C
        blk = pl.BlockSpec((tr, C), lambda i: (i, 0))
    else:
        tr = R
        blk = pl.BlockSpec((R, tc), lambda i: (0, i))

    def body(w_ref, g_ref, m_ref, v_ref, d_ref, nm_ref, nv_ref):
        gv = g_ref[...]
        nm = ADAM_B1 * m_ref[...] + (1.0 - ADAM_B1) * gv
        nv = ADAM_B2 * v_ref[...] + (1.0 - ADAM_B2) * jnp.square(gv)
        m_hat = nm / (1.0 - ADAM_B1 ** ADAM_STEP)
        v_hat = nv / (1.0 - ADAM_B2 ** ADAM_STEP)
        d_ref[...] = -ADAM_LR * (m_hat / (jnp.sqrt(v_hat) + ADAM_EPS) + ADAM_WD * w_ref[...])
        nm_ref[...] = nm
        nv_ref[...] = nv

    out = jax.ShapeDtypeStruct((R, C), f32)
    return pl.pallas_call(
        body, name=name, grid=((R // tr) * (C // tc),), in_specs=[blk] * 4, out_specs=[blk] * 3, out_shape=[out] * 3,
        compiler_params=_cparams("parallel"),
    )(w, g, m, v)


BIG_SHARDS = (("w_in", (1024, 900), True), ("w_out", (256, 1024), False), ("w_cq", (256, 512), False),
              ("w_ckv", (256, 1024), False), ("w_co", (512, 256), True), ("w_mlp1", (1024, 1024), True),
              ("w_mlp2", (1024, 1024), False))
CONV_SHARD = (CONV_WIDTH, 3 * GDN_WIDTH // N_CHIPS)
SMALL_DIMS = (("norm_mix_g", 1024), ("fox_qnorm_g", 64), ("fox_knorm_g", 64), ("fox_f_bias", 8), ("fox_onorm_g", 64),
              ("gdn_A_log", 4), ("gdn_dt_bias", 4), ("gdn_onorm_g", 128), ("norm_xattn_g", 1024), ("mem_norm_g", 1024),
              ("xattn_qnorm_g", 128), ("xattn_knorm_g", 128), ("norm_mlp_g", 1024))
WEIGHT_ORDER = ("norm_mix_g", "w_in", "fox_qnorm_g", "fox_knorm_g", "fox_f_bias", "fox_onorm_g", "gdn_conv_w", "gdn_A_log",
                "gdn_dt_bias", "gdn_onorm_g", "w_out", "norm_xattn_g", "mem_norm_g", "w_cq", "w_ckv", "xattn_qnorm_g",
                "xattn_knorm_g", "w_co", "norm_mlp_g", "w_mlp1", "w_mlp2")


def _pack_rows(pieces, rows, lead=()):
    cat = jnp.concatenate([p.reshape(lead + (-1,)) for p in pieces], axis=-1)
    cat = jnp.pad(cat, [(0, 0)] * len(lead) + [(0, rows * LANES - cat.shape[-1])])
    return cat.reshape(lead + (rows, LANES))


def _unpack_rows(buf, sizes, lead=()):
    flat = buf.reshape(lead + (-1,))
    out, off = [], 0
    for n in sizes:
        out.append(flat[..., off:off + n])
        off += n
    return out


def _conv_to_wire(conv):
    return lax.bitcast_convert_type(conv, bf16)


def _conv_from_wire(wire):
    return lax.bitcast_convert_type(wire, f32)


SMALL_ROWS = 96
SMALL_ADAM_ROWS = 56


def kernel(x, mem, norm_mix_g, w_in, fox_qnorm_g, fox_knorm_g, fox_f_bias, fox_onorm_g, gdn_conv_w, gdn_A_log, gdn_dt_bias, gdn_onorm_g, w_out, norm_xattn_g, mem_norm_g, w_cq, w_ckv, xattn_qnorm_g, xattn_knorm_g, w_co, norm_mlp_g, w_mlp1, w_mlp2, loss_target, m_norm_mix_g, m_w_in, m_fox_qnorm_g, m_fox_knorm_g, m_fox_f_bias, m_fox_onorm_g, m_gdn_conv_w, m_gdn_A_log, m_gdn_dt_bias, m_gdn_onorm_g, m_w_out, m_norm_xattn_g, m_mem_norm_g, m_w_cq, m_w_ckv, m_xattn_qnorm_g, m_xattn_knorm_g, m_w_co, m_norm_mlp_g, m_w_mlp1, m_w_mlp2, v_norm_mix_g, v_w_in, v_fox_qnorm_g, v_fox_knorm_g, v_fox_f_bias, v_fox_onorm_g, v_gdn_conv_w, v_gdn_A_log, v_gdn_dt_bias, v_gdn_onorm_g, v_w_out, v_norm_xattn_g, v_mem_norm_g, v_w_cq, v_w_ckv, v_xattn_qnorm_g, v_xattn_knorm_g, v_w_co, v_norm_mlp_g, v_w_mlp1, v_w_mlp2):
    wts = dict(norm_mix_g=norm_mix_g, w_in=w_in, fox_qnorm_g=fox_qnorm_g, fox_knorm_g=fox_knorm_g, fox_f_bias=fox_f_bias,
               fox_onorm_g=fox_onorm_g, gdn_conv_w=gdn_conv_w, gdn_A_log=gdn_A_log, gdn_dt_bias=gdn_dt_bias,
               gdn_onorm_g=gdn_onorm_g, w_out=w_out, norm_xattn_g=norm_xattn_g, mem_norm_g=mem_norm_g, w_cq=w_cq, w_ckv=w_ckv,
               xattn_qnorm_g=xattn_qnorm_g, xattn_knorm_g=xattn_knorm_g, w_co=w_co, norm_mlp_g=norm_mlp_g, w_mlp1=w_mlp1,
               w_mlp2=w_mlp2)
    mom = dict(norm_mix_g=m_norm_mix_g, w_in=m_w_in, fox_qnorm_g=m_fox_qnorm_g, fox_knorm_g=m_fox_knorm_g,
               fox_f_bias=m_fox_f_bias, fox_onorm_g=m_fox_onorm_g, gdn_conv_w=m_gdn_conv_w, gdn_A_log=m_gdn_A_log,
               gdn_dt_bias=m_gdn_dt_bias, gdn_onorm_g=m_gdn_onorm_g, w_out=m_w_out, norm_xattn_g=m_norm_xattn_g,
               mem_norm_g=m_mem_norm_g, w_cq=m_w_cq, w_ckv=m_w_ckv, xattn_qnorm_g=m_xattn_qnorm_g,
               xattn_knorm_g=m_xattn_knorm_g, w_co=m_w_co, norm_mlp_g=m_norm_mlp_g, w_mlp1=m_w_mlp1, w_mlp2=m_w_mlp2)
    var = dict(norm_mix_g=v_norm_mix_g, w_in=v_w_in, fox_qnorm_g=v_fox_qnorm_g, fox_knorm_g=v_fox_knorm_g,
               fox_f_bias=v_fox_f_bias, fox_onorm_g=v_fox_onorm_g, gdn_conv_w=v_gdn_conv_w, gdn_A_log=v_gdn_A_log,
               gdn_dt_bias=v_gdn_dt_bias, gdn_onorm_g=v_gdn_onorm_g, w_out=v_w_out, norm_xattn_g=v_norm_xattn_g,
               mem_norm_g=v_mem_norm_g, w_cq=v_w_cq, w_ckv=v_w_ckv, xattn_qnorm_g=v_xattn_qnorm_g,
               xattn_knorm_g=v_xattn_knorm_g, w_co=v_w_co, norm_mlp_g=v_norm_mlp_g, w_mlp1=v_w_mlp1, w_mlp2=v_w_mlp2)
    B, S, D = x.shape
    T = B * S
    big_names = [n for n, _, _ in BIG_SHARDS]
    chip = 2 * lax.axis_index("x") + lax.axis_index("y")
    core = lax.axis_index("c").astype(jnp.int32).reshape(1)

    shards = {n: wts[n][0].astype(MXU_DTYPE) for n in big_names[1:]}
    in_t = lambda p: jnp.swapaxes(p[0], 0, 1)
    shards["w_in"] = jnp.pad(in_t(w_in).astype(MXU_DTYPE), ((0, IN_SHARD_PAD - IN_SHARD), (0, 0)))
    w_in_all, conv_all = gather_weights([shards["w_in"]], gdn_conv_w[0])
    late = big_names[1:]
    send_sems, recv_sems, late_src, late_zones, token = gather_weights_start([shards[n] for n in late], conv_all)
    own = lambda g, s: lax.dynamic_update_slice(g, s[None], (chip,) + (0,) * s.ndim)
    full = {"w_in": own(w_in_all, shards["w_in"])}
    conv_full = own(conv_all, gdn_conv_w[0]).transpose(1, 0, 2).reshape(CONV_WIDTH, 3 * GDN_WIDTH)
    rows = lambda g: g.reshape(N_CHIPS * g.shape[1], g.shape[2])
    w_in_t = full["w_in"][:, :IN_SHARD].reshape(IN_DIM, D_MODEL)

    def late_weights(after):
        zones = gather_weights_wait(send_sems, recv_sems, late_src, late_zones, after)
        got = {n: own(z, shards[n]) for n, z in zip(late, zones)}
        return dict(w_out=rows(got["w_out"]), w_cq=rows(got["w_cq"]), w_ckv=rows(got["w_ckv"]), w_co=got["w_co"],
                    w_mlp1=got["w_mlp1"], w_mlp2=rows(got["w_mlp2"]))

    def chip_partials(names, by_chip):
        landed = swap_grad_halves(by_chip, name="swap_grad_halves_" + names[0])
        return [add_grad_halves(g, l, core, name="add_halves_" + n) for n, g, l in zip(names, by_chip, landed)]

    in_flight = []

    def grads_ready(ready):
        names = list(ready)
        *started, tok = scatter_grads_start(chip_partials(names, [ready[n] for n in names]),
                                            name="scatter_grads_start_%d" % len(in_flight))
        in_flight.append((names, *started))
        return tok

    w = dict(wa_t=align_w_in_t(w_in_t), conv_w=conv_full, late=late_weights, grads_ready=grads_ready)
    sp = {n: wts[n] for n, _ in SMALL_DIMS}
    sp["norm_mix_g"] = sp["norm_mix_g"] + token[0, 0]

    loss_part, grad_x, g_big, g_small = local_step(x.reshape(T, D), mem.reshape(-1, D), loss_target.reshape(T, D), w, sp, B=B)

    small_pieces = [g_small[n] for n, _ in SMALL_DIMS] + [g_small["gdn_conv_w"], loss_part]
    small_sizes = [d for _, d in SMALL_DIMS] + [CONV_WIDTH * 3 * GDN_WIDTH, LANES]
    red_small = _unpack_rows(all_reduce_small(_pack_rows(small_pieces, SMALL_ROWS)), small_sizes)
    grads = {n: p.reshape(1, d) for (n, d), p in zip(SMALL_DIMS, red_small)}
    conv_grad = lax.dynamic_slice(red_small[-2].reshape(CONV_WIDTH, 3 * GDN_WIDTH), (0, chip * CONV_SHARD[1]), CONV_SHARD)
    grads["gdn_conv_w"] = conv_grad.reshape((1,) + CONV_SHARD)
    loss = red_small[-1][0]

    names = ["w_in"]
    chip_part = chip_partials(names, [g_big[n] for n in names])
    parts, zones = dict(zip(names, chip_part)), dict(zip(names, scatter_grads(chip_part)))
    for k, (names, send_sems, recv_sems, thru, land) in enumerate(in_flight):
        thru, land = scatter_grads_wait(send_sems, recv_sems, thru, land, grad_x, name="scatter_grads_wait_%d" % k)
        parts.update(zip(names, thru))
        zones.update(zip(names, land))
    order = jnp.stack([chip, chip ^ 2, chip ^ 1, chip ^ 3]).astype(jnp.int32)
    mine = [sum_grads(parts[n], zones[n], order, name="sum_chips_" + n) for n in big_names]
    theirs = swap_reduced_halves(mine)

    delta, new_m, new_v = {}, {}, {}
    for n, a, b in zip(big_names[1:], mine[1:], theirs[1:]):
        g, d, nm, nv = adamw_halves(wts[n][0], a, b, mom[n][0], var[n][0], core, name="adamw_" + n)
        grads[n], delta[n], new_m[n], new_v[n] = g[None], d[None], nm[None], nv[None]
    south = core[0] == 0
    g_in_t = jnp.concatenate([jnp.where(south, mine[0], theirs[0]), jnp.where(south, theirs[0], mine[0])])[:IN_SHARD]
    back = lambda t: jnp.swapaxes(t, 0, 1)[None]
    d, nm, nv = adamw(in_t(w_in), g_in_t, in_t(m_w_in), in_t(v_w_in), name="adamw_w_in", tc=256)
    grads["w_in"], delta["w_in"], new_m["w_in"], new_v["w_in"] = back(g_in_t), back(d), back(nm), back(nv)
    small_names = [n for n, _ in SMALL_DIMS] + ["gdn_conv_w"]
    small_sz = [d for _, d in SMALL_DIMS] + [CONV_SHARD[0] * CONV_SHARD[1]]
    packed4 = [_pack_rows([src[n] for n in small_names], SMALL_ADAM_ROWS) for src in (wts, grads, mom, var)]
    outs = adamw(*packed4, name="adamw_small", tr=SMALL_ADAM_ROWS)
    for dst, buf in zip((delta, new_m, new_v), outs):
        for n, p in zip(small_names, _unpack_rows(buf, small_sz)):
            dst[n] = p.reshape(wts[n].shape)

    return (loss, grad_x.reshape(B, S, D), *[grads[n] for n in WEIGHT_ORDER], *[delta[n] for n in WEIGHT_ORDER],
            *[new_m[n] for n in WEIGHT_ORDER], *[new_v[n] for n in WEIGHT_ORDER])
```

```python
import functools

import jax
import jax.numpy as jnp
import numpy as np
from jax import lax
from jax.experimental import pallas as pl
from jax.experimental.pallas import tpu as pltpu

f32 = jnp.float32
bf16 = jnp.bfloat16
MXU_DTYPE = jnp.bfloat16
WIRE_DTYPE = jnp.bfloat16
INV_PRECISION = lax.Precision.HIGH

D_MODEL = 1024
FOX_HEADS = 8
FOX_HEAD_DIM = 64
FOX_WIDTH = 512
GDN_HEADS = 4
GDN_HEAD_DIM = 128
GDN_WIDTH = 512
CONV_WIDTH = 4
GDN_CHUNK = 64
XATTN_HEADS = 4
XATTN_HEAD_DIM = 128
XATTN_WIDTH = 512
D_FF = 4096
IN_DIM = 3600
EPS = 1e-6
NEG_INF = -1e30
LANES = 128
ADAM_LR = 0.001
ADAM_B1 = 0.9
ADAM_B2 = 0.999
ADAM_EPS = 1e-08
ADAM_WD = 0.01
ADAM_STEP = 10
VMEM_LIMIT = 48 * 1024 * 1024

COL_FOX = 0
COL_GDN = 1536
COL_Z = 3072
COL_SMALL = 3584
IN_ALIGNED = 3840
IN_TILE = 768
SM_F = 0
SM_B = 8
SM_A = 12


def _cparams(*sem):
    return pltpu.CompilerParams(dimension_semantics=sem, vmem_limit_bytes=VMEM_LIMIT)


def _mx(v):
    return v.astype(MXU_DTYPE)


def _dot(a, b, dims, precision=None):
    return lax.dot_general(a, b, (dims, ((), ())), preferred_element_type=f32, precision=precision)


def _dotm(a, b, dims):
    return _dot(_mx(a), _mx(b), dims)


NN = ((1,), (0,))
NT = ((1,), (1,))
TN = ((0,), (0,))


def matmul(a, b, *, name, ta=False, tb=False, b_stacked=False, out_stacked=False, residual=None, relu2_out=False,
           relu2_bwd_aux=None, out_dtype=f32, tm=1024, tn=1024, tk=1024):
    M, K = (a.shape[1], a.shape[0]) if ta else a.shape
    if b_stacked:
        b_cols = b.shape[2]
        N, tk = (b.shape[1], min(tk, b_cols)) if tb else (N_CHIPS * b_cols, tk)
        tn = tn if tb else min(tn, b_cols)
        assert K == (N_CHIPS * b_cols if tb else b.shape[1]), (name, a.shape, b.shape)
    else:
        N = b.shape[0] if tb else b.shape[1]
    if out_stacked:
        tn = min(tn, N // N_CHIPS)
    tm, tn, tk = min(tm, M), min(tn, N), min(tk, K)
    assert M % tm == 0 and N % tn == 0 and K % tk == 0, (name, M, N, K)
    nk = K // tk
    has_res = residual is not None
    has_aux = relu2_bwd_aux is not None

    def body(*refs):
        a_ref, b_ref = refs[0], refs[1]
        pos = 2
        res_ref = aux_ref = None
        if has_res:
            res_ref = refs[pos]
            pos += 1
        if has_aux:
            aux_ref = refs[pos]
            pos += 1
        o_ref = refs[pos]
        k = pl.program_id(2)
        dims = ((0,) if ta else (1,), (1,) if tb else (0,))
        part = _dot(_mx(a_ref[...]), _mx(b_ref[...]), dims)

        def finish(r):
            if has_res:
                r = r + res_ref[...]
            if has_aux:
                r = r * (2.0 * jnp.sqrt(aux_ref[...].astype(f32)))
            if relu2_out:
                o_ref[...] = jnp.square(jnp.maximum(r, 0.0)).astype(o_ref.dtype)
            else:
                o_ref[...] = r.astype(o_ref.dtype)

        if nk == 1:
            finish(part)
            return
        acc_ref = refs[pos + 1]

        @pl.when(k == 0)
        def _():
            acc_ref[...] = part

        @pl.when((k > 0) & (k < nk - 1))
        def _():
            acc_ref[...] += part

        @pl.when(k == nk - 1)
        def _():
            finish(acc_ref[...] + part)

    a_spec = pl.BlockSpec((tk, tm), lambda i, j, k: (k, i)) if ta else pl.BlockSpec((tm, tk), lambda i, j, k: (i, k))
    if b_stacked and tb:
        per = b_cols // tk
        b_spec = pl.BlockSpec((None, tn, tk), lambda i, j, k: (k // per, j, k % per))
    elif b_stacked:
        per = b_cols // tn
        b_spec = pl.BlockSpec((None, tk, tn), lambda i, j, k: (j // per, k, j % per))
    else:
        b_spec = pl.BlockSpec((tn, tk), lambda i, j, k: (j, k)) if tb else pl.BlockSpec((tk, tn), lambda i, j, k: (k, j))
    if out_stacked:
        assert not (has_res or has_aux or relu2_out), name
        per_o = N // N_CHIPS // tn
        o_spec = pl.BlockSpec((None, tm, tn), lambda i, j, k: (j // per_o, i, j % per_o))
        out_full = (N_CHIPS, M, N // N_CHIPS)
    else:
        o_spec = pl.BlockSpec((tm, tn), lambda i, j, k: (i, j))
        out_full = (M, N)
    in_specs, args = [a_spec, b_spec], [a, b]
    if has_res:
        in_specs.append(o_spec)
        args.append(residual)
    if has_aux:
        in_specs.append(o_spec)
        args.append(relu2_bwd_aux)
    out_shape = [jax.ShapeDtypeStruct(out_full, out_dtype)]
    out_specs = [o_spec]
    res = pl.pallas_call(
        body, name=name, grid=(M // tm, N // tn, nk), in_specs=in_specs, out_specs=out_specs, out_shape=out_shape,
        scratch_shapes=[pltpu.VMEM((tm, tn), f32)] if nk > 1 else [],
        compiler_params=_cparams("parallel", "parallel", "arbitrary"),
    )(*args)
    return res[0]


def matmul_rows(a, b, extras, *, name, mode, tb=False, b_stacked=False, tm=1024, tk=1024):
    M, K = a.shape
    N = D_MODEL
    if b_stacked:
        assert tb, name
        tk = min(tk, b.shape[2])
        per = b.shape[2] // tk
        b_spec = pl.BlockSpec((None, N, tk), lambda i, k: (k // per, 0, k % per))
    elif tb:
        tk = min(tk, K)
        b_spec = pl.BlockSpec((N, tk), lambda i, k: (0, k))
    else:
        tk = min(tk, K)
        b_spec = pl.BlockSpec((tk, N), lambda i, k: (k, 0))
    tm = min(tm, M)
    assert M % tm == 0 and K % tk == 0, (name, M, K)
    nk = K // tk
    extras = [e for e in extras if e is not None]
    n_ex = len(extras)

    def body(*refs):
        a_ref, b_ref = refs[0], refs[1]
        ex = refs[2:2 + n_ex]
        o_ref, s_ref = refs[2 + n_ex], refs[3 + n_ex]
        i, k = pl.program_id(0), pl.program_id(1)
        part = _dot(_mx(a_ref[...]), _mx(b_ref[...]), ((1,), (1,) if tb else (0,)))

        def finish(y):
            @pl.when(i == 0)
            def _():
                s_ref[...] = jnp.zeros_like(s_ref)

            if mode == "rms_bwd":
                xv, gv = ex[0][...], ex[1][...]
                rstd = lax.rsqrt(jnp.mean(xv * xv, axis=-1, keepdims=True) + EPS)
                xhat = xv * rstd
                gd = y * gv
                dx = rstd * (gd - xhat * jnp.mean(gd * xhat, axis=-1, keepdims=True))
                o_ref[...] = dx + ex[2][...] if n_ex == 3 else dx
                s_ref[...] += jnp.sum(y * xhat, axis=0, keepdims=True)
            else:
                e = y + ex[0][...] - ex[1][...]
                o_ref[...] = e * (1.0 / N)
                tot = 0.5 * jnp.sum(jnp.mean(e * e, axis=-1, keepdims=True), axis=0, keepdims=True)
                s_ref[...] += jnp.broadcast_to(tot, s_ref.shape)

        if nk == 1:
            finish(part)
            return
        acc_ref = refs[4 + n_ex]

        @pl.when(k == 0)
        def _():
            acc_ref[...] = part

        @pl.when((k > 0) & (k < nk - 1))
        def _():
            acc_ref[...] += part

        @pl.when(k == nk - 1)
        def _():
            finish(acc_ref[...] + part)

    row = pl.BlockSpec((tm, N), lambda i, k: (i, 0))
    vec = pl.BlockSpec((1, N), lambda i, k: (0, 0))
    if mode == "rms_bwd":
        ex_specs = [row, vec] + ([row] if n_ex == 3 else [])
        s_shape, s_spec = jax.ShapeDtypeStruct((1, N), f32), vec
    else:
        ex_specs = [row, row]
        s_shape, s_spec = jax.ShapeDtypeStruct((1, LANES), f32), pl.BlockSpec((1, LANES), lambda i, k: (0, 0))
    return pl.pallas_call(
        body, name=name, grid=(M // tm, nk),
        in_specs=[pl.BlockSpec((tm, tk), lambda i, k: (i, k)), b_spec] + ex_specs,
        out_specs=[row, s_spec], out_shape=[jax.ShapeDtypeStruct((M, N), f32), s_shape],
        scratch_shapes=[pltpu.VMEM((tm, N), f32)] if nk > 1 else [],
        compiler_params=_cparams("arbitrary", "arbitrary"),
    )(a, b, *extras)


def rms_fwd(x, g, *, name, tr=512):
    R, D = x.shape
    tr = min(tr, R)

    def body(x_ref, g_ref, o_ref):
        xv = x_ref[...]
        y = xv * lax.rsqrt(jnp.mean(xv * xv, axis=-1, keepdims=True) + EPS)
        o_ref[...] = (y * g_ref[...]).astype(o_ref.dtype)

    return pl.pallas_call(
        body, name=name, grid=(R // tr,),
        in_specs=[pl.BlockSpec((tr, D), lambda i: (i, 0)), pl.BlockSpec((1, D), lambda i: (0, 0))],
        out_specs=pl.BlockSpec((tr, D), lambda i: (i, 0)),
        out_shape=jax.ShapeDtypeStruct((R, D), MXU_DTYPE),
        compiler_params=_cparams("parallel"),
    )(x, g)


def rms_bwd(x, g, dh, residual, *, name, tr=512):
    R, D = x.shape
    tr = min(tr, R)
    has_res = residual is not None

    def body(*refs):
        if has_res:
            x_ref, g_ref, dh_ref, res_ref, dx_ref, dg_ref = refs
        else:
            x_ref, g_ref, dh_ref, dx_ref, dg_ref = refs
        xv = x_ref[...]
        rstd = lax.rsqrt(jnp.mean(xv * xv, axis=-1, keepdims=True) + EPS)
        xhat = xv * rstd
        dh = dh_ref[...].astype(f32)
        gd = dh * g_ref[...]
        dx = rstd * (gd - xhat * jnp.mean(gd * xhat, axis=-1, keepdims=True))
        if has_res:
            dx = dx + res_ref[...]
        dx_ref[...] = dx

        @pl.when(pl.program_id(0) == 0)
        def _():
            dg_ref[...] = jnp.zeros_like(dg_ref)

        dg_ref[...] += jnp.sum(dh * xhat, axis=0, keepdims=True)

    row = pl.BlockSpec((tr, D), lambda i: (i, 0))
    vec = pl.BlockSpec((1, D), lambda i: (0, 0))
    in_specs = [row, vec, row] + ([row] if has_res else [])
    args = [x, g, dh] + ([residual] if has_res else [])
    return pl.pallas_call(
        body, name=name, grid=(R // tr,), in_specs=in_specs, out_specs=[row, vec],
        out_shape=[jax.ShapeDtypeStruct((R, D), f32), jax.ShapeDtypeStruct((1, D), f32)],
        compiler_params=_cparams("arbitrary"),
    )(*args)


def loss_head(y, target, *, tr=512):
    R, D = y.shape
    tr = min(tr, R)

    def body(y_ref, t_ref, dy_ref, loss_ref):
        e = y_ref[...] - t_ref[...]
        dy_ref[...] = e * (1.0 / D)

        @pl.when(pl.program_id(0) == 0)
        def _():
            loss_ref[...] = jnp.zeros_like(loss_ref)

        part = 0.5 * jnp.sum(jnp.mean(e * e, axis=-1, keepdims=True), axis=0, keepdims=True)
        loss_ref[...] += jnp.broadcast_to(part, loss_ref.shape)

    row = pl.BlockSpec((tr, D), lambda i: (i, 0))
    return pl.pallas_call(
        body, name="loss_head", grid=(R // tr,), in_specs=[row, row],
        out_specs=[row, pl.BlockSpec((1, LANES), lambda i: (0, 0))],
        out_shape=[jax.ShapeDtypeStruct((R, D), f32), jax.ShapeDtypeStruct((1, LANES), f32)],
        compiler_params=_cparams("arbitrary"),
    )(y, target)


def _head_rms(v, g):
    r = lax.rsqrt(jnp.mean(v * v, axis=-1, keepdims=True) + EPS)
    return v * r * g, r


def _head_rms_bwd(v, r, g, dn):
    vhat = v * r
    gd = dn * g
    dv = r * (gd - vhat * jnp.mean(gd * vhat, axis=-1, keepdims=True))
    return dv, jnp.sum(dn * vhat, axis=0, keepdims=True)


def _softmax_rows(s):
    m = jnp.max(s, axis=-1, keepdims=True)
    e = jnp.exp(s - m)
    return e / jnp.sum(e, axis=-1, keepdims=True)


def xattn_fwd(cq, ckv, gq, gk, *, B, tq=512):
    T = cq.shape[0]
    S = T // B
    M = ckv.shape[0] // B
    tq = min(tq, S)
    nq = S // tq
    scale = XATTN_HEAD_DIM ** -0.5

    def body(q_ref, k_ref, v_ref, gq_ref, gk_ref, o_ref):
        qn, _ = _head_rms(q_ref[...], gq_ref[...])
        kn, _ = _head_rms(k_ref[...], gk_ref[...])
        p = _softmax_rows(_dot(_mx(qn), _mx(kn), NT) * scale)
        o_ref[...] = _dot(_mx(p), _mx(v_ref[...]), NN).astype(o_ref.dtype)

    hd = XATTN_HEAD_DIM
    vec = pl.BlockSpec((1, hd), lambda b, h, i: (0, 0))
    return pl.pallas_call(
        body, name="xattn_fwd", grid=(B, XATTN_HEADS, nq),
        in_specs=[pl.BlockSpec((tq, hd), lambda b, h, i: (b * nq + i, h)),
                  pl.BlockSpec((M, hd), lambda b, h, i: (b, h)),
                  pl.BlockSpec((M, hd), lambda b, h, i: (b, XATTN_HEADS + h)), vec, vec],
        out_specs=pl.BlockSpec((tq, hd), lambda b, h, i: (b * nq + i, h)),
        out_shape=jax.ShapeDtypeStruct((T, XATTN_WIDTH), MXU_DTYPE),
        compiler_params=_cparams("parallel", "parallel", "parallel"),
    )(cq, ckv, ckv, gq, gk)


def xattn_bwd(cq, ckv, gq, gk, dco, *, B, tq=512):
    T = cq.shape[0]
    S = T // B
    M = ckv.shape[0] // B
    tq = min(tq, S)
    nq = S // tq
    scale = XATTN_HEAD_DIM ** -0.5
    hd = XATTN_HEAD_DIM

    def body(q_ref, k_ref, v_ref, gq_ref, gk_ref, do_ref, dq_ref, dk_ref, dv_ref, dgq_ref, dgk_ref, dkn_acc, dv_acc):
        b, h, i = pl.program_id(0), pl.program_id(1), pl.program_id(2)

        @pl.when((b == 0) & (h == 0) & (i == 0))
        def _():
            dgq_ref[...] = jnp.zeros_like(dgq_ref)
            dgk_ref[...] = jnp.zeros_like(dgk_ref)

        @pl.when(i == 0)
        def _():
            dkn_acc[...] = jnp.zeros_like(dkn_acc)
            dv_acc[...] = jnp.zeros_like(dv_acc)

        q, k, v = q_ref[...], k_ref[...], v_ref[...]
        gqv, gkv = gq_ref[...], gk_ref[...]
        qn, rq = _head_rms(q, gqv)
        kn, rk = _head_rms(k, gkv)
        p = _softmax_rows(_dot(_mx(qn), _mx(kn), NT) * scale)
        do = do_ref[...]
        dv_acc[...] += _dot(_mx(p), _mx(do), TN)
        dp = _dot(_mx(do), _mx(v), NT)
        ds = p * (dp - jnp.sum(dp * p, axis=-1, keepdims=True)) * scale
        dqn = _dot(_mx(ds), _mx(kn), NN)
        dkn_acc[...] += _dot(_mx(ds), _mx(qn), TN)
        dq, dgq = _head_rms_bwd(q, rq, gqv, dqn)
        dq_ref[...] = dq.astype(dq_ref.dtype)
        dgq_ref[...] += dgq

        @pl.when(i == nq - 1)
        def _():
            dk, dgk = _head_rms_bwd(k, rk, gkv, dkn_acc[...])
            dk_ref[...] = dk.astype(dk_ref.dtype)
            dv_ref[...] = dv_acc[...].astype(dv_ref.dtype)
            dgk_ref[...] += dgk

    vec = pl.BlockSpec((1, hd), lambda b, h, i: (0, 0))
    qspec = pl.BlockSpec((tq, hd), lambda b, h, i: (b * nq + i, h))
    kspec = pl.BlockSpec((M, hd), lambda b, h, i: (b, h))
    vspec = pl.BlockSpec((M, hd), lambda b, h, i: (b, XATTN_HEADS + h))
    dq, dk, dv, dgq, dgk = pl.pallas_call(
        body, name="xattn_bwd", grid=(B, XATTN_HEADS, nq),
        in_specs=[qspec, kspec, vspec, vec, vec, qspec],
        out_specs=[qspec, kspec, kspec, vec, vec],
        out_shape=[jax.ShapeDtypeStruct((T, XATTN_WIDTH), MXU_DTYPE),
                   jax.ShapeDtypeStruct((B * M, XATTN_WIDTH), MXU_DTYPE),
                   jax.ShapeDtypeStruct((B * M, XATTN_WIDTH), MXU_DTYPE),
                   jax.ShapeDtypeStruct((1, hd), f32), jax.ShapeDtypeStruct((1, hd), f32)],
        scratch_shapes=[pltpu.VMEM((M, hd), f32), pltpu.VMEM((M, hd), f32)],
        compiler_params=_cparams("arbitrary", "arbitrary", "arbitrary"),
    )(cq, ckv, ckv, gq, gk, dco)
    return dq, jnp.concatenate([dk, dv], axis=1), dgq, dgk


FOX_PAIRS = FOX_HEADS // 2


def _fox_scores(qn, kn, ccol, crow, q0, tq, S, scale):
    s = _dot(_mx(qn), _mx(kn), NT) * scale + ccol - crow
    qpos = q0 + lax.broadcasted_iota(jnp.int32, (tq, S), 0)
    kpos = lax.broadcasted_iota(jnp.int32, (tq, S), 1)
    return jnp.where(kpos <= qpos, s, NEG_INF)


def fox_fwd(P, ccol, crow, gq, gk, go, *, B, tq=256):
    T = P.shape[0]
    S = T // B
    tq = min(tq, S)
    nq = S // tq
    hd = FOX_HEAD_DIM
    scale = hd ** -0.5

    def body(q_ref, k_ref, v_ref, ccol_ref, crow_ref, gq_ref, gk_ref, go_ref, o_ref, oa_ref):
        q0 = pl.program_id(2) * tq
        for e in range(2):
            sl = slice(e * hd, (e + 1) * hd)
            qn, _ = _head_rms(q_ref[:, sl], gq_ref[:, sl])
            kn, _ = _head_rms(k_ref[:, sl], gk_ref[:, sl])
            p = _softmax_rows(_fox_scores(qn, kn, ccol_ref[0, e], crow_ref[0, e], q0, tq, S, scale))
            o = _dot(_mx(p), _mx(v_ref[:, sl]), NN)
            o_ref[:, sl] = o
            oa_ref[:, sl] = _head_rms(o, go_ref[:, sl])[0].astype(oa_ref.dtype)

    W = 2 * hd
    vec = pl.BlockSpec((1, W), lambda b, h, i: (0, 0))
    ospec = pl.BlockSpec((tq, W), lambda b, h, i: (b * nq + i, h))
    return pl.pallas_call(
        body, name="fox_fwd", grid=(B, FOX_PAIRS, nq),
        in_specs=[pl.BlockSpec((tq, W), lambda b, h, i: (b * nq + i, h)),
                  pl.BlockSpec((S, W), lambda b, h, i: (b, FOX_PAIRS + h)),
                  pl.BlockSpec((S, W), lambda b, h, i: (b, 2 * FOX_PAIRS + h)),
                  pl.BlockSpec((1, 2, tq, 1), lambda b, h, i: (b, h, i, 0)),
                  pl.BlockSpec((1, 2, 1, S), lambda b, h, i: (b, h, 0, 0)), vec, vec, vec],
        out_specs=[ospec, ospec],
        out_shape=[jax.ShapeDtypeStruct((T, FOX_WIDTH), f32), jax.ShapeDtypeStruct((T, FOX_WIDTH), MXU_DTYPE)],
        compiler_params=_cparams("parallel", "parallel", "parallel"),
    )(P, P, P, ccol, crow, gq, gk, go)


def fox_bwd(P, ccol, crow, gq, gk, go, o_raw, d_oab, *, B, tq=256):
    T = P.shape[0]
    S = T // B
    tq = min(tq, S)
    nq = S // tq
    hd = FOX_HEAD_DIM
    scale = hd ** -0.5

    def body(q_ref, k_ref, v_ref, ccol_ref, crow_ref, gq_ref, gk_ref, go_ref, o_ref, doa_ref,
             dq_ref, dk_ref, dv_ref, dccol_ref, dcrow_ref, dgq_ref, dgk_ref, dgo_ref, dkn_acc, dv_acc, dcrow_acc):
        b, h, i = pl.program_id(0), pl.program_id(1), pl.program_id(2)
        q0 = i * tq

        @pl.when((b == 0) & (h == 0) & (i == 0))
        def _():
            dgq_ref[...] = jnp.zeros_like(dgq_ref)
            dgk_ref[...] = jnp.zeros_like(dgk_ref)
            dgo_ref[...] = jnp.zeros_like(dgo_ref)

        @pl.when(i == 0)
        def _():
            dkn_acc[...] = jnp.zeros_like(dkn_acc)
            dv_acc[...] = jnp.zeros_like(dv_acc)
            dcrow_acc[...] = jnp.zeros_like(dcrow_acc)

        for e in range(2):
            sl = slice(e * hd, (e + 1) * hd)
            q, k, v = q_ref[:, sl], k_ref[:, sl], v_ref[:, sl]
            gqv, gkv, gov = gq_ref[:, sl], gk_ref[:, sl], go_ref[:, sl]
            qn, rq = _head_rms(q, gqv)
            kn, rk = _head_rms(k, gkv)
            p = _softmax_rows(_fox_scores(qn, kn, ccol_ref[0, e], crow_ref[0, e], q0, tq, S, scale))
            o = o_ref[:, sl]
            ro = lax.rsqrt(jnp.mean(o * o, axis=-1, keepdims=True) + EPS)
            do, dgo = _head_rms_bwd(o, ro, gov, doa_ref[:, sl])
            dgo_ref[:, sl] += dgo
            dv_acc[e] += _dot(_mx(p), _mx(do), TN)
            dp = _dot(_mx(do), _mx(v), NT)
            ds = p * (dp - jnp.sum(do * o, axis=-1, keepdims=True))
            dccol_ref[0, e] = jnp.sum(ds, axis=1, keepdims=True)
            dcrow_acc[e] -= jnp.sum(ds, axis=0, keepdims=True)
            dqn = _dot(_mx(ds), _mx(kn), NN) * scale
            dkn_acc[e] += _dot(_mx(ds), _mx(qn), TN) * scale
            dq, dgq = _head_rms_bwd(q, rq, gqv, dqn)
            dq_ref[:, sl] = dq.astype(dq_ref.dtype)
            dgq_ref[:, sl] += dgq

        @pl.when(i == nq - 1)
        def _():
            for e in range(2):
                sl = slice(e * hd, (e + 1) * hd)
                k = k_ref[:, sl]
                gkv = gk_ref[:, sl]
                rk = lax.rsqrt(jnp.mean(k * k, axis=-1, keepdims=True) + EPS)
                dk, dgk = _head_rms_bwd(k, rk, gkv, dkn_acc[e])
                dk_ref[:, sl] = dk.astype(dk_ref.dtype)
                dv_ref[:, sl] = dv_acc[e].astype(dv_ref.dtype)
                dgk_ref[:, sl] += dgk
                dcrow_ref[0, e] = dcrow_acc[e]

    W = 2 * hd
    vec = pl.BlockSpec((1, W), lambda b, h, i: (0, 0))
    qspec = pl.BlockSpec((tq, W), lambda b, h, i: (b * nq + i, h))
    kvout = pl.BlockSpec((S, W), lambda b, h, i: (b, h))
    colspec = pl.BlockSpec((1, 2, tq, 1), lambda b, h, i: (b, h, i, 0))
    rowspec = pl.BlockSpec((1, 2, 1, S), lambda b, h, i: (b, h, 0, 0))
    return pl.pallas_call(
        body, name="fox_bwd", grid=(B, FOX_PAIRS, nq),
        in_specs=[qspec,
                  pl.BlockSpec((S, W), lambda b, h, i: (b, FOX_PAIRS + h)),
                  pl.BlockSpec((S, W), lambda b, h, i: (b, 2 * FOX_PAIRS + h)),
                  colspec, rowspec, vec, vec, vec, qspec, qspec],
        out_specs=[qspec, kvout, kvout, colspec, rowspec, vec, vec, vec],
        out_shape=[jax.ShapeDtypeStruct((T, FOX_WIDTH), MXU_DTYPE), jax.ShapeDtypeStruct((T, FOX_WIDTH), MXU_DTYPE),
                   jax.ShapeDtypeStruct((T, FOX_WIDTH), MXU_DTYPE),
                   jax.ShapeDtypeStruct((B, FOX_HEADS, S, 1), f32), jax.ShapeDtypeStruct((B, FOX_HEADS, 1, S), f32),
                   jax.ShapeDtypeStruct((1, W), f32), jax.ShapeDtypeStruct((1, W), f32), jax.ShapeDtypeStruct((1, W), f32)],
        scratch_shapes=[pltpu.VMEM((2, S, hd), f32), pltpu.VMEM((2, S, hd), f32), pltpu.VMEM((2, 1, S), f32)],
        compiler_params=_cparams("arbitrary", "arbitrary", "arbitrary"),
    )(P, P, P, ccol, crow, gq, gk, go, o_raw, d_oab)


FOX_TQ = 512
FOX_TK = 512
GROUP_PRECISION = lax.Precision.HIGH


def _head_mean(v):
    n = v.shape[1]
    r = lax.broadcasted_iota(jnp.int32, (n, n), 0) // FOX_HEAD_DIM
    c = lax.broadcasted_iota(jnp.int32, (n, n), 1) // FOX_HEAD_DIM
    ones = (r == c).astype(bf16)
    hi = v.astype(bf16)
    lo = (v - hi.astype(f32)).astype(bf16)
    return (_dot(hi, ones, NN) + _dot(lo, ones, NN)) * (1.0 / FOX_HEAD_DIM)


def fox_prep_fwd(P, gq, gk, *, tr=512):
    T = P.shape[0]
    tr = min(tr, T)
    scale = FOX_HEAD_DIM ** -0.5

    def body(q_ref, k_ref, v_ref, gq_ref, gk_ref, qn_ref, kn_ref, vb_ref):
        q, k = q_ref[...], k_ref[...]
        qn_ref[...] = (q * lax.rsqrt(_head_mean(q * q) + EPS) * (gq_ref[...] * scale)).astype(qn_ref.dtype)
        kn_ref[...] = (k * lax.rsqrt(_head_mean(k * k) + EPS) * gk_ref[...]).astype(kn_ref.dtype)
        vb_ref[...] = v_ref[...].astype(vb_ref.dtype)

    W = FOX_WIDTH
    col = lambda j: pl.BlockSpec((tr, W), lambda i: (i, j))
    vec = pl.BlockSpec((1, W), lambda i: (0, 0))
    out = jax.ShapeDtypeStruct((T, W), MXU_DTYPE)
    return pl.pallas_call(
        body, name="fox_prep_fwd", grid=(T // tr,), in_specs=[col(0), col(1), col(2), vec, vec],
        out_specs=[col(0)] * 3, out_shape=[out] * 3, compiler_params=_cparams("parallel"),
    )(P, P, P, gq, gk)


def fox_prep_bwd(P, gq, gk, dqn, dkn, *, tr=512):
    T = P.shape[0]
    tr = min(tr, T)
    scale = FOX_HEAD_DIM ** -0.5

    def body(q_ref, k_ref, gq_ref, gk_ref, dqn_ref, dkn_ref, dq_ref, dk_ref, dgq_ref, dgk_ref):
        @pl.when(pl.program_id(0) == 0)
        def _():
            dgq_ref[...] = jnp.zeros_like(dgq_ref)
            dgk_ref[...] = jnp.zeros_like(dgk_ref)

        def one(x, g, dn, dx_ref, dg_ref):
            r = lax.rsqrt(_head_mean(x * x) + EPS)
            xhat = x * r
            gd = dn * g
            dx_ref[...] = (r * (gd - xhat * _head_mean(gd * xhat))).astype(dx_ref.dtype)
            return jnp.sum(dn * xhat, axis=0, keepdims=True)

        dgq_ref[...] += scale * one(q_ref[...], gq_ref[...] * scale, dqn_ref[...], dq_ref, dgq_ref)
        dgk_ref[...] += one(k_ref[...], gk_ref[...], dkn_ref[...], dk_ref, dgk_ref)

    W = FOX_WIDTH
    col = lambda j: pl.BlockSpec((tr, W), lambda i: (i, j))
    vec = pl.BlockSpec((1, W), lambda i: (0, 0))
    return pl.pallas_call(
        body, name="fox_prep_bwd", grid=(T // tr,), in_specs=[col(0), col(1), vec, vec, col(0), col(0)],
        out_specs=[col(0), col(0), vec, vec],
        out_shape=[jax.ShapeDtypeStruct((T, W), MXU_DTYPE), jax.ShapeDtypeStruct((T, W), MXU_DTYPE),
                   jax.ShapeDtypeStruct((1, W), f32), jax.ShapeDtypeStruct((1, W), f32)],
        compiler_params=_cparams("arbitrary"),
    )(P, P, gq, gk, dqn, dkn)


def _fox_tile_scores(q, k_ref, ccol_ref, cq, e, j, sl, mask_off):
    tq, tk = FOX_TQ, FOX_TK
    rows = pl.ds(pl.multiple_of(j * tk, tk), tk)
    k = k_ref[rows, sl]
    s = _dot(k, q, NT) + cq - ccol_ref[0, e, rows, :]
    if mask_off is not None:
        key = lax.broadcasted_iota(jnp.int32, (tk, tq), 0) + mask_off
        query = lax.broadcasted_iota(jnp.int32, (tk, tq), 1)
        s = jnp.where(key <= query, s, NEG_INF)
    return s, k, rows


def _fox_sweep(i, update, carry):
    nd = FOX_TQ // FOX_TK
    carry = lax.fori_loop(0, i * nd, lambda j, cr: update(cr, j, None), carry)
    for d in range(nd):
        carry = update(carry, i * nd + d, d * FOX_TK)
    return carry


def fox_core_fwd(qn, kn, vb, ccol, crow, go, *, B):
    T = qn.shape[0]
    S = T // B
    tq = FOX_TQ
    nq = S // tq
    hd = FOX_HEAD_DIM

    def body(q_ref, k_ref, v_ref, ccol_ref, crow_ref, go_ref, o_ref, oa_ref, lse_ref):
        i = pl.program_id(2)
        for e in range(2):
            sl = slice(e * hd, (e + 1) * hd)
            q = q_ref[:, sl]
            cq = crow_ref[0, e, i]

            def update(carry, j, mask_off):
                m, l, acc = carry
                s, _, rows = _fox_tile_scores(q, k_ref, ccol_ref, cq, e, j, sl, mask_off)
                m2 = jnp.maximum(m, jnp.max(s, axis=0, keepdims=True))
                a = jnp.exp(m - m2)
                p = jnp.exp(s - m2)
                return m2, a * l + jnp.sum(p, axis=0, keepdims=True), a * acc + _dot(v_ref[rows, sl], _mx(p), TN)

            carry = (jnp.full((1, tq), NEG_INF, f32), jnp.zeros((1, tq), f32), jnp.zeros((hd, tq), f32))
            m, l, acc = _fox_sweep(i, update, carry)
            o = (acc / l).T
            o_ref[:, sl] = o
            oa_ref[:, sl] = _head_rms(o, go_ref[:, sl])[0].astype(oa_ref.dtype)
            lse_ref[0, e, 0] = m + jnp.log(l)

    W = 2 * hd
    qspec = pl.BlockSpec((tq, W), lambda b, h, i: (b * nq + i, h))
    kspec = pl.BlockSpec((S, W), lambda b, h, i: (b, h))
    return pl.pallas_call(
        body, name="fox_core_fwd", grid=(B, FOX_PAIRS, nq),
        in_specs=[qspec, kspec, kspec, pl.BlockSpec((1, 2, S, 1), lambda b, h, i: (b, h, 0, 0)),
                  pl.BlockSpec((1, 2, nq, 1, tq), lambda b, h, i: (b, h, 0, 0, 0)),
                  pl.BlockSpec((1, W), lambda b, h, i: (0, 0))],
        out_specs=[qspec, qspec, pl.BlockSpec((1, 2, 1, 1, tq), lambda b, h, i: (b, h, i, 0, 0))],
        out_shape=[jax.ShapeDtypeStruct((T, FOX_WIDTH), f32), jax.ShapeDtypeStruct((T, FOX_WIDTH), MXU_DTYPE),
                   jax.ShapeDtypeStruct((B, FOX_HEADS, nq, 1, tq), f32)],
        compiler_params=_cparams("parallel", "parallel", "parallel"),
    )(qn, kn, vb, ccol, crow, go)


def fox_core_bwd(qn, kn, vb, ccol, crow, go, o_raw, lse, d_oab, *, B):
    T = qn.shape[0]
    S = T // B
    tq = FOX_TQ
    nq = S // tq
    hd = FOX_HEAD_DIM

    def body(q_ref, k_ref, v_ref, ccol_ref, crow_ref, go_ref, o_ref, lse_ref, doa_ref,
             dq_ref, dk_ref, dv_ref, dccol_ref, dcrow_ref, dgo_ref, dk_acc, dv_acc, dck_acc):
        b, h, i = pl.program_id(0), pl.program_id(1), pl.program_id(2)

        @pl.when((b == 0) & (h == 0) & (i == 0))
        def _():
            dgo_ref[...] = jnp.zeros_like(dgo_ref)

        @pl.when(i == 0)
        def _():
            dk_acc[...] = jnp.zeros_like(dk_acc)
            dv_acc[...] = jnp.zeros_like(dv_acc)
            dck_acc[...] = jnp.zeros_like(dck_acc)

        for e in range(2):
            sl = slice(e * hd, (e + 1) * hd)
            q = q_ref[:, sl]
            cq = crow_ref[0, e, i]
            lse_e = lse_ref[0, e, 0]
            o = o_ref[:, sl]
            ro = lax.rsqrt(jnp.mean(o * o, axis=-1, keepdims=True) + EPS)
            do, dgo = _head_rms_bwd(o, ro, go_ref[:, sl], doa_ref[:, sl])
            dgo_ref[:, sl] += dgo
            delta = jnp.sum((do * o).T, axis=0, keepdims=True)
            do_b = _mx(do)

            def update(carry, j, mask_off):
                dq, dcq = carry
                s, k, rows = _fox_tile_scores(q, k_ref, ccol_ref, cq, e, j, sl, mask_off)
                p = jnp.exp(s - lse_e)
                dv_acc[e, rows, :] += _dot(_mx(p), do_b, NN)
                ds = p * (_dot(v_ref[rows, sl], do_b, NT) - delta)
                dck_acc[e, rows, :] -= jnp.sum(ds, axis=1, keepdims=True)
                ds_b = _mx(ds)
                dk_acc[e, rows, :] += _dot(ds_b, q, NN)
                return dq + _dot(ds_b, k, TN), dcq + jnp.sum(ds, axis=0, keepdims=True)

            dq, dcq = _fox_sweep(i, update, (jnp.zeros((tq, hd), f32), jnp.zeros((1, tq), f32)))
            dq_ref[:, sl] = dq
            dcrow_ref[0, e, 0] = dcq

        @pl.when(i == nq - 1)
        def _():
            for e in range(2):
                sl = slice(e * hd, (e + 1) * hd)
                dk_ref[:, sl] = dk_acc[e]
                dv_ref[:, sl] = dv_acc[e].astype(dv_ref.dtype)
            dccol_ref[0] = dck_acc[...]

    W = 2 * hd
    qspec = pl.BlockSpec((tq, W), lambda b, h, i: (b * nq + i, h))
    kspec = pl.BlockSpec((S, W), lambda b, h, i: (b, h))
    colspec = pl.BlockSpec((1, 2, S, 1), lambda b, h, i: (b, h, 0, 0))
    rowspec = pl.BlockSpec((1, 2, nq, 1, tq), lambda b, h, i: (b, h, 0, 0, 0))
    tilespec = pl.BlockSpec((1, 2, 1, 1, tq), lambda b, h, i: (b, h, i, 0, 0))
    vec = pl.BlockSpec((1, W), lambda b, h, i: (0, 0))
    return pl.pallas_call(
        body, name="fox_core_bwd", grid=(B, FOX_PAIRS, nq),
        in_specs=[qspec, kspec, kspec, colspec, rowspec, vec, qspec, tilespec, qspec],
        out_specs=[qspec, kspec, kspec, colspec, tilespec, vec],
        out_shape=[jax.ShapeDtypeStruct((T, FOX_WIDTH), f32), jax.ShapeDtypeStruct((T, FOX_WIDTH), f32),
                   jax.ShapeDtypeStruct((T, FOX_WIDTH), MXU_DTYPE),
                   jax.ShapeDtypeStruct((B, FOX_HEADS, S, 1), f32), jax.ShapeDtypeStruct((B, FOX_HEADS, nq, 1, tq), f32),
                   jax.ShapeDtypeStruct((1, W), f32)],
        scratch_shapes=[pltpu.VMEM((2, S, hd), f32), pltpu.VMEM((2, S, hd), f32), pltpu.VMEM((2, S, 1), f32)],
        compiler_params=_cparams("arbitrary", "arbitrary", "arbitrary"),
    )(qn, kn, vb, ccol, crow, go, o_raw, lse, d_oab)


def _lane_mask(lo, hi, shape):
    lane = lax.broadcasted_iota(jnp.int32, shape, 1)
    return (lane >= lo) & (lane < hi)


def _cumsum_rows(v, period, reverse=False):
    n = v.shape[0]
    pos = lax.broadcasted_iota(jnp.int32, v.shape, 0) % period
    sh = 1
    while sh < period:
        if reverse:
            v = v + jnp.where(pos + sh < period, pltpu.roll(v, n - sh, 0), 0.0)
        else:
            v = v + jnp.where(pos >= sh, pltpu.roll(v, sh, 0), 0.0)
        sh *= 2
    return v


def _gate_values(z, bias, alog):
    zb = z + bias
    ls = jax.nn.log_sigmoid(zb)
    beta = jax.nn.sigmoid(z)
    g = -jnp.exp(alog) * jax.nn.softplus(zb)
    return zb, ls, beta, g


def gates_fwd(P, bias, alog, *, B):
    T = P.shape[0]
    S = T // B

    def body(z_ref, bias_ref, alog_ref, o_ref):
        z = z_ref[...]
        _, ls, beta, g = _gate_values(z, bias_ref[...], alog_ref[...])
        c = _cumsum_rows(ls, S)
        gc = _cumsum_rows(g, GDN_CHUNK)
        o = jnp.where(_lane_mask(SM_F, SM_F + FOX_HEADS, z.shape), c, 0.0)
        o = jnp.where(_lane_mask(SM_B, SM_B + GDN_HEADS, z.shape), beta, o)
        o = jnp.where(_lane_mask(SM_A, SM_A + GDN_HEADS, z.shape), gc, o)
        o_ref[...] = o

    vec = pl.BlockSpec((1, LANES), lambda b: (0, 0))
    return pl.pallas_call(
        body, name="gates_fwd", grid=(B,),
        in_specs=[pl.BlockSpec((S, LANES), lambda b: (b, COL_SMALL // LANES)), vec, vec],
        out_specs=pl.BlockSpec((S, LANES), lambda b: (b, 0)),
        out_shape=jax.ShapeDtypeStruct((T, LANES), f32),
        compiler_params=_cparams("parallel"),
    )(P, bias, alog)


def gates_bwd(P, bias, alog, dgates, *, B):
    T = P.shape[0]
    S = T // B

    def body(z_ref, bias_ref, alog_ref, dg_ref, dz_ref, par_ref):
        z = z_ref[...]
        zb, ls, beta, g = _gate_values(z, bias_ref[...], alog_ref[...])
        d = dg_ref[...]
        dls = _cumsum_rows(d, S, reverse=True)
        dgr = _cumsum_rows(d, GDN_CHUNK, reverse=True)
        sig = jax.nn.sigmoid(zb)
        dz_f = dls * (1.0 - sig)
        dz_b = d * beta * (1.0 - beta)
        dz_a = dgr * (-jnp.exp(alog_ref[...])) * sig
        dz = jnp.where(_lane_mask(SM_F, SM_F + FOX_HEADS, z.shape), dz_f, 0.0)
        dz = jnp.where(_lane_mask(SM_B, SM_B + GDN_HEADS, z.shape), dz_b, dz)
        dz = jnp.where(_lane_mask(SM_A, SM_A + GDN_HEADS, z.shape), dz_a, dz)
        dz_ref[...] = dz.astype(dz_ref.dtype)

        @pl.when(pl.program_id(0) == 0)
        def _():
            par_ref[...] = jnp.zeros_like(par_ref)

        dalog = jnp.where(_lane_mask(SM_A, SM_A + GDN_HEADS, z.shape), dgr * g, 0.0)
        par_ref[0:1, :] += jnp.sum(dz, axis=0, keepdims=True)
        par_ref[1:2, :] += jnp.sum(dalog, axis=0, keepdims=True)

    vec = pl.BlockSpec((1, LANES), lambda b: (0, 0))
    return pl.pallas_call(
        body, name="gates_bwd", grid=(B,),
        in_specs=[pl.BlockSpec((S, LANES), lambda b: (b, COL_SMALL // LANES)), vec, vec,
                  pl.BlockSpec((S, LANES), lambda b: (b, 0))],
        out_specs=[pl.BlockSpec((S, LANES), lambda b: (b, 0)), pl.BlockSpec((8, LANES), lambda b: (0, 0))],
        out_shape=[jax.ShapeDtypeStruct((T, LANES), MXU_DTYPE), jax.ShapeDtypeStruct((8, LANES), f32)],
        compiler_params=_cparams("arbitrary"),
    )(P, bias, alog, dgates)


GDN_BLOCKS = 3 * GDN_HEADS


def _shift_rows(v, d, reverse=False):
    if d == 0:
        return v
    n = v.shape[0]
    row = lax.broadcasted_iota(jnp.int32, v.shape, 0)
    if reverse:
        return jnp.where(row + d < n, pltpu.roll(v, n - d, 0), 0.0)
    return jnp.where(row >= d, pltpu.roll(v, d, 0), 0.0)


def _conv_silu(x, w):
    pre = sum(w[j:j + 1, :] * _shift_rows(x, CONV_WIDTH - 1 - j) for j in range(CONV_WIDTH))
    return pre, pre * jax.nn.sigmoid(pre)


def gdn_prep_fwd(P, conv_w, *, B):
    T = P.shape[0]
    S = T // B

    def body(x_ref, w_ref, o_ref):
        _, y = _conv_silu(x_ref[...], w_ref[...])
        yn = y * lax.rsqrt(jnp.sum(y * y, axis=-1, keepdims=True) + EPS)
        o_ref[...] = jnp.where(pl.program_id(1) < 2 * GDN_HEADS, yn, y)

    return pl.pallas_call(
        body, name="gdn_prep_fwd", grid=(B, GDN_BLOCKS),
        in_specs=[pl.BlockSpec((S, LANES), lambda b, j: (b, COL_GDN // LANES + j)),
                  pl.BlockSpec((CONV_WIDTH, LANES), lambda b, j: (0, j))],
        out_specs=pl.BlockSpec((S, LANES), lambda b, j: (b, j)),
        out_shape=jax.ShapeDtypeStruct((T, 3 * GDN_WIDTH), f32),
        compiler_params=_cparams("parallel", "parallel"),
    )(P, conv_w)


def gdn_prep_bwd(P, conv_w, dG, *, B):
    T = P.shape[0]
    S = T // B

    def body(x_ref, w_ref, dg_ref, dx_ref, dw_ref):
        x, w = x_ref[...], w_ref[...]
        pre, y = _conv_silu(x, w)
        dn = dg_ref[...]
        r = lax.rsqrt(jnp.sum(y * y, axis=-1, keepdims=True) + EPS)
        n = y * r
        dy_norm = r * (dn - n * jnp.sum(dn * n, axis=-1, keepdims=True))
        dy = jnp.where(pl.program_id(0) < 2 * GDN_HEADS, dy_norm, dn)
        sg = jax.nn.sigmoid(pre)
        dpre = dy * (sg * (1.0 + pre * (1.0 - sg)))
        dx = sum(w[j:j + 1, :] * _shift_rows(dpre, CONV_WIDTH - 1 - j, reverse=True) for j in range(CONV_WIDTH))
        dx_ref[...] = dx.astype(dx_ref.dtype)

        @pl.when(pl.program_id(1) == 0)
        def _():
            dw_ref[...] = jnp.zeros_like(dw_ref)

        for j in range(CONV_WIDTH):
            dw_ref[j:j + 1, :] += jnp.sum(dpre * _shift_rows(x, CONV_WIDTH - 1 - j), axis=0, keepdims=True)

    return pl.pallas_call(
        body, name="gdn_prep_bwd", grid=(GDN_BLOCKS, B),
        in_specs=[pl.BlockSpec((S, LANES), lambda j, b: (b, COL_GDN // LANES + j)),
                  pl.BlockSpec((CONV_WIDTH, LANES), lambda j, b: (0, j)),
                  pl.BlockSpec((S, LANES), lambda j, b: (b, j))],
        out_specs=[pl.BlockSpec((S, LANES), lambda j, b: (b, j)),
                   pl.BlockSpec((CONV_WIDTH, LANES), lambda j, b: (0, j))],
        out_shape=[jax.ShapeDtypeStruct((T, 3 * GDN_WIDTH), MXU_DTYPE),
                   jax.ShapeDtypeStruct((CONV_WIDTH, 3 * GDN_WIDTH), f32)],
        compiler_params=_cparams("arbitrary", "arbitrary"),
    )(P, conv_w, dG)


GDN_GROUP = 16
B_NN = (((2,), (1,)), ((0,), (0,)))
B_NT = (((2,), (2,)), ((0,), (0,)))
B_TN = (((1,), (1,)), ((0,), (0,)))


def _bmm(a, b, dims, precision=None):
    if precision is None:
        a, b = _mx(a), _mx(b)
    return lax.dot_general(a, b, dims, preferred_element_type=f32, precision=precision)


def _tri_inverse(A):
    C = A.shape[-1]
    row = lax.broadcasted_iota(jnp.int32, A.shape, 1)
    col = lax.broadcasted_iota(jnp.int32, A.shape, 2)
    eye = (row == col).astype(f32)
    X = jnp.where((row // 4) == (col // 4), -A, 0.0)
    X2 = _bmm(X, X, B_NN, INV_PRECISION)
    Tm = eye + X + X2 + _bmm(X, X2, B_NN, INV_PRECISION)
    b = 4
    while b < C:
        off = ((row // (2 * b)) == (col // (2 * b))) & ((row // b) != (col // b))
        Tm = Tm - _bmm(_bmm(Tm, jnp.where(off, A, 0.0), B_NN, INV_PRECISION), Tm, B_NN, INV_PRECISION)
        b *= 2
    return Tm


def _pick_lane(block, lane_idx):
    lane = lax.broadcasted_iota(jnp.int32, block.shape, 1)
    return jnp.sum(jnp.where(lane == lane_idx, block, 0.0), axis=1, keepdims=True)


def _gdn_local(q, k, v, beta, gc, Tm=None, uwm=None):
    C = GDN_CHUNK
    n = q.shape[0] // C
    q = q.reshape(n, C, -1) * (GDN_HEAD_DIM ** -0.5)
    k = k.reshape(n, C, -1)
    v = v.reshape(n, C, -1)
    beta = beta.reshape(n, C, 1)
    gc = gc.reshape(n, C, 1)
    row = lax.broadcasted_iota(jnp.int32, (n, C, C), 1)
    col = lax.broadcasted_iota(jnp.int32, (n, C, C), 2)
    gcT = jnp.swapaxes(jnp.broadcast_to(gc, (n, C, C)), 1, 2)
    D = jnp.exp(jnp.where(row >= col, gc - gcT, NEG_INF))
    kb = k * beta
    vb = v * beta
    A = jnp.where(row > col, _bmm(kb, k, B_NT) * D, 0.0)
    Gam = jnp.exp(gc)
    kg = kb * Gam
    gl = gc[:, C - 1:C, :]
    kdec = jnp.exp(gl - gc)
    loc = dict(q=q, k=k, v=v, beta=beta, gc=gc, D=D, kb=kb, vb=vb, A=A, Gam=Gam, kg=kg,
               kdec=kdec, kd=k * kdec, qg=q * Gam, gam=jnp.exp(gl), row=row, col=col)
    uwm = Tm is None if uwm is None else uwm
    Tm = _tri_inverse(A) if Tm is None else Tm.reshape(n, C, C)
    if uwm:
        loc.update(u=_bmm(Tm, vb, B_NN), w=_bmm(Tm, kg, B_NN), M=_bmm(q, k, B_NT) * D)
    loc["Tm"] = Tm
    return loc


def _gdn_store_local(loc, r0, u_s, w_s, qg_s, kd_s, M_s, gam_s, c0):
    n = loc["u"].shape[0]
    R = n * GDN_CHUNK
    u_s[pl.ds(r0, R), :] = loc["u"].reshape(R, -1)
    w_s[pl.ds(r0, R), :] = loc["w"].reshape(R, -1)
    qg_s[pl.ds(r0, R), :] = loc["qg"].reshape(R, -1)
    kd_s[pl.ds(r0, R), :] = loc["kd"].reshape(R, -1)
    M_s[pl.ds(r0, R), :] = loc["M"].reshape(R, -1)
    gam_s[pl.ds(c0, n)] = jnp.broadcast_to(loc["gam"], (n, 1, LANES))


def _gdn_specs(S):
    blk = lambda off: pl.BlockSpec((S, LANES), lambda b, h: (b, off + h))
    return blk


def gdn_fwd(G, gates, P, g_on, *, B):
    T = G.shape[0]
    S = T // B
    C = GDN_CHUNK
    N = S // C
    grp = min(GDN_GROUP, N)
    R = grp * C
    hd = GDN_HEAD_DIM

    def body(q_ref, k_ref, v_ref, gt_ref, z_ref, gon_ref, o_ref, ob_ref, st_ref, tm_ref, A_s, B_s, Q_s, O_s, gam_s):
        h = pl.program_id(1)

        def local(gi, carry):
            r0 = pl.multiple_of(gi * R, R)
            gt = gt_ref[pl.ds(r0, R), :]
            loc = _gdn_local(q_ref[pl.ds(r0, R), :], k_ref[pl.ds(r0, R), :], v_ref[pl.ds(r0, R), :],
                             _pick_lane(gt, SM_B + h), _pick_lane(gt, SM_A + h))
            chunks = pl.ds(gi * grp, grp)
            tm_ref[0, 0, pl.ds(r0, R), :] = loc["Tm"].reshape(R, C)
            A_s[chunks] = -_bmm(loc["kd"], loc["w"], B_TN)
            B_s[chunks] = _bmm(loc["kd"], loc["u"], B_TN)
            Q_s[pl.ds(r0, R), :] = (loc["qg"] - _bmm(loc["M"], loc["w"], B_NN)).reshape(R, hd)
            O_s[pl.ds(r0, R), :] = _bmm(loc["M"], loc["u"], B_NN).reshape(R, hd)
            gam_s[chunks] = jnp.broadcast_to(loc["gam"], (grp, 1, LANES))
            return carry

        lax.fori_loop(0, N // grp, local, 0)

        def step(n, state):
            st_ref[0, 0, n] = state
            return state * gam_s[n] + _dotm(A_s[n], state, NN) + B_s[n]

        lax.fori_loop(0, N, step, jnp.zeros((hd, hd), f32))

        def outputs(gi, carry):
            r0 = pl.multiple_of(gi * R, R)
            Q = Q_s[pl.ds(r0, R), :].reshape(grp, C, hd)
            o = _bmm(Q, st_ref[0, 0, pl.ds(gi * grp, grp)], B_NN).reshape(R, hd) + O_s[pl.ds(r0, R), :]
            o_ref[pl.ds(r0, R), :] = o
            return carry

        lax.fori_loop(0, N // grp, outputs, 0)
        o = o_ref[...]
        z = z_ref[...]
        ob_ref[...] = (_head_rms(o, gon_ref[...])[0] * (z * jax.nn.sigmoid(z))).astype(ob_ref.dtype)

    blk = lambda off: pl.BlockSpec((S, LANES), lambda b, h: (b, off + h))
    rows = lambda: pltpu.VMEM((S, hd), f32)
    return pl.pallas_call(
        body, name="gdn_fwd", grid=(B, GDN_HEADS),
        in_specs=[blk(0), blk(GDN_HEADS), blk(2 * GDN_HEADS), pl.BlockSpec((S, LANES), lambda b, h: (b, 0)),
                  blk(COL_Z // LANES), pl.BlockSpec((1, hd), lambda b, h: (0, 0))],
        out_specs=[blk(0), blk(0), pl.BlockSpec((1, 1, N, hd, hd), lambda b, h: (b, h, 0, 0, 0)),
                   pl.BlockSpec((1, 1, S, C), lambda b, h: (b, h, 0, 0))],
        out_shape=[jax.ShapeDtypeStruct((T, GDN_WIDTH), f32), jax.ShapeDtypeStruct((T, GDN_WIDTH), MXU_DTYPE),
                   jax.ShapeDtypeStruct((B, GDN_HEADS, N, hd, hd), f32), jax.ShapeDtypeStruct((B, GDN_HEADS, S, C), f32)],
        scratch_shapes=[pltpu.VMEM((N, hd, hd), f32), pltpu.VMEM((N, hd, hd), f32), rows(), rows(),
                        pltpu.VMEM((N, 1, LANES), f32)],
        compiler_params=_cparams("parallel", "parallel"),
    )(G, G, G, gates, P, g_on)


def gdn_bwd(G, gates, P, g_on, o_raw, states, tm, d_oab, *, B):
    T = G.shape[0]
    S = T // B
    C = GDN_CHUNK
    N = S // C
    grp = min(GDN_GROUP, N)
    R = grp * C
    hd = GDN_HEAD_DIM

    def body(q_ref, k_ref, v_ref, gt_ref, z_ref, gon_ref, o_ref, st_ref, tm_ref, dob_ref,
             dq_ref, dk_ref, dv_ref, dgt_ref, dz_ref, dgon_ref,
             u_s, w_s, M_s, gam_s, do_s, A_s, C_s, dst_s):
        b, h = pl.program_id(0), pl.program_id(1)

        @pl.when((b == 0) & (h == 0))
        def _():
            dgon_ref[...] = jnp.zeros_like(dgon_ref)

        @pl.when(h == 0)
        def _():
            dgt_ref[...] = jnp.zeros_like(dgt_ref)

        def group_inputs(gi, uwm):
            r0 = pl.multiple_of(gi * R, R)
            gt = gt_ref[pl.ds(r0, R), :]
            return r0, _gdn_local(q_ref[pl.ds(r0, R), :], k_ref[pl.ds(r0, R), :], v_ref[pl.ds(r0, R), :],
                                  _pick_lane(gt, SM_B + h), _pick_lane(gt, SM_A + h), tm_ref[0, 0, pl.ds(r0, R), :], uwm)

        def local(gi, carry):
            r0, loc = group_inputs(gi, True)
            rows, chunks = pl.ds(r0, R), pl.ds(gi * grp, grp)
            u_s[rows, :] = loc["u"].reshape(R, hd)
            w_s[rows, :] = loc["w"].reshape(R, hd)
            M_s[rows, :] = loc["M"].reshape(R, C)
            gam_s[chunks] = jnp.broadcast_to(loc["gam"], (grp, 1, LANES))
            o, z, gon = o_ref[rows, :], z_ref[rows, :], gon_ref[...]
            dob = dob_ref[rows, :]
            on, ro = _head_rms(o, gon)
            sz = jax.nn.sigmoid(z)
            dz_ref[rows, :] = (dob * on * (sz * (1.0 + z * (1.0 - sz)))).astype(dz_ref.dtype)
            do, dgon = _head_rms_bwd(o, ro, gon, dob * (z * sz))
            do_s[rows, :] = do
            dgon_ref[...] += dgon
            A_s[chunks] = -_bmm(loc["kd"], loc["w"], B_TN)
            C_s[chunks] = _bmm(loc["qg"] - _bmm(loc["M"], loc["w"], B_NN), do.reshape(grp, C, hd), B_TN)
            return carry

        lax.fori_loop(0, N // grp, local, 0)

        def step(t, dS):
            n = N - 1 - t
            dst_s[n] = dS
            return dS * gam_s[n] + _dotm(A_s[n], dS, TN) + C_s[n]

        lax.fori_loop(0, N, step, jnp.zeros((hd, hd), f32))

        def finish(gi, carry):
            r0, L = group_inputs(gi, False)
            n = grp
            rows, chunks = pl.ds(r0, R), pl.ds(gi * grp, grp)
            g3 = lambda ref: ref[rows, :].reshape(n, C, -1)
            u, w, do = g3(u_s), g3(w_s), g3(do_s)
            L["M"] = g3(M_s)
            state, dS = st_ref[0, 0, chunks], dst_s[chunks]
            v_new = u - _bmm(w, state, B_NN)
            du = _bmm(L["M"], do, B_TN) + _bmm(L["kd"], dS, B_NN)
            dw = -_bmm(du, state, B_NT)
            dqg = _bmm(do, state, B_NT)
            dM = _bmm(do, v_new, B_NT)
            dkd = _bmm(v_new, dS, B_NT)
            dgl_state = jnp.sum(jnp.sum(dS * state, axis=2, keepdims=True), axis=1, keepdims=True) * L["gam"]
            TmT = jnp.swapaxes(L["Tm"], 1, 2)
            dTm = _bmm(du, L["vb"], B_NT) + _bmm(dw, L["kg"], B_NT)
            dvb = _bmm(TmT, du, B_NN)
            dkg = _bmm(TmT, dw, B_NN)
            dA = jnp.where(L["row"] > L["col"], -_bmm(_bmm(TmT, dTm, B_NN), TmT, B_NN), 0.0)
            dKK = dA * L["D"]
            dQK = dM * L["D"]
            dkb = _bmm(dKK, L["k"], B_NN) + dkg * L["Gam"]
            dk = (_bmm(dKK, L["kb"], B_TN) + _bmm(dQK, L["q"], B_TN) + dkd * L["kdec"] + L["beta"] * dkb)
            dq = (_bmm(dQK, L["k"], B_NN) + dqg * L["Gam"]) * (GDN_HEAD_DIM ** -0.5)
            E = dA * L["A"] + dM * L["M"]
            r = jnp.sum(dkd * L["kd"], axis=-1, keepdims=True)
            dgc = (jnp.sum(E, axis=2, keepdims=True) - jnp.sum(jnp.swapaxes(E, 1, 2), axis=2, keepdims=True)
                   + jnp.sum(dkg * L["kg"], axis=-1, keepdims=True) + jnp.sum(dqg * L["qg"], axis=-1, keepdims=True) - r)
            dgl = jnp.sum(r, axis=1, keepdims=True) + dgl_state
            rowc = lax.broadcasted_iota(jnp.int32, (n, C, 1), 1)
            dgc = dgc + jnp.where(rowc == C - 1, dgl, 0.0)
            dbeta = jnp.sum(dkb * L["k"], axis=-1, keepdims=True) + jnp.sum(dvb * L["v"], axis=-1, keepdims=True)
            dq_ref[rows, :] = dq.reshape(R, hd)
            dk_ref[rows, :] = dk.reshape(R, hd)
            dv_ref[rows, :] = (L["beta"] * dvb).reshape(R, hd)
            lane = lax.broadcasted_iota(jnp.int32, (R, LANES), 1)
            dgt_ref[rows, :] += (jnp.where(lane == SM_B + h, dbeta.reshape(R, 1), 0.0)
                                 + jnp.where(lane == SM_A + h, dgc.reshape(R, 1), 0.0))
            return carry

        lax.fori_loop(0, N // grp, finish, 0)

    blk = lambda off: pl.BlockSpec((S, LANES), lambda b, h: (b, off + h))
    rows = lambda: pltpu.VMEM((S, hd), f32)
    return pl.pallas_call(
        body, name="gdn_bwd", grid=(B, GDN_HEADS),
        in_specs=[blk(0), blk(GDN_HEADS), blk(2 * GDN_HEADS), pl.BlockSpec((S, LANES), lambda b, h: (b, 0)),
                  blk(COL_Z // LANES), pl.BlockSpec((1, hd), lambda b, h: (0, 0)), blk(0),
                  pl.BlockSpec((1, 1, N, hd, hd), lambda b, h: (b, h, 0, 0, 0)),
                  pl.BlockSpec((1, 1, S, C), lambda b, h: (b, h, 0, 0)), blk(GDN_HEADS)],
        out_specs=[blk(0), blk(0), blk(0), pl.BlockSpec((S, LANES), lambda b, h: (b, 0)), blk(0),
                   pl.BlockSpec((1, hd), lambda b, h: (0, 0))],
        out_shape=[jax.ShapeDtypeStruct((T, GDN_WIDTH), f32), jax.ShapeDtypeStruct((T, GDN_WIDTH), f32),
                   jax.ShapeDtypeStruct((T, GDN_WIDTH), f32), jax.ShapeDtypeStruct((T, LANES), f32),
                   jax.ShapeDtypeStruct((T, GDN_WIDTH), MXU_DTYPE), jax.ShapeDtypeStruct((1, hd), f32)],
        scratch_shapes=[rows(), rows(), pltpu.VMEM((S, C), f32), pltpu.VMEM((N, 1, LANES), f32), rows(),
                        pltpu.VMEM((N, hd, hd), f32), pltpu.VMEM((N, hd, hd), f32), pltpu.VMEM((N, hd, hd), f32)],
        compiler_params=_cparams("arbitrary", "arbitrary"),
    )(G, G, G, gates, P, g_on, o_raw, states, tm, d_oab)


IN_SPLIT = (0, 1536, 1544, 3080, 3088, 3600)


IN_SHARD = IN_DIM // 4
IN_SHARD_PAD = 928


def align_w_in_t(wt):
    s = IN_SPLIT
    pad = jnp.zeros((IN_ALIGNED - IN_DIM, wt.shape[1]), wt.dtype)
    return jnp.concatenate([wt[s[0]:s[1]], wt[s[2]:s[3]], wt[s[4]:s[5]], wt[s[1]:s[2]], wt[s[3]:s[4]], pad], axis=0)


def unalign_w_in_t(wa):
    return jnp.concatenate([wa[0:1536], wa[COL_SMALL:COL_SMALL + 8], wa[1536:3072],
                            wa[COL_SMALL + 8:COL_SMALL + 16], wa[3072:3584]], axis=0)


def _lanes_vec(pieces):
    v = jnp.zeros((1, LANES), f32)
    for off, a in pieces:
        v = lax.dynamic_update_slice(v, a.astype(f32), (0, off))
    return v


def local_step(x, mem, target, w, sp, *, B):
    T = x.shape[0]
    S = T // B
    gq8, gk8 = jnp.tile(sp["fox_qnorm_g"], (1, FOX_HEADS)), jnp.tile(sp["fox_knorm_g"], (1, FOX_HEADS))
    go2 = jnp.tile(sp["fox_onorm_g"], (1, 2))
    bias = _lanes_vec([(SM_F, sp["fox_f_bias"]), (SM_A, sp["gdn_dt_bias"])])
    alog = _lanes_vec([(SM_A, sp["gdn_A_log"])])

    h1 = rms_fwd(x, sp["norm_mix_g"], name="rms_mix")
    P = matmul(h1, w["wa_t"], tb=True, name="mm_in", tn=IN_TILE)
    gates = gates_fwd(P, bias, alog, B=B)
    c = gates[:, SM_F:SM_F + FOX_HEADS].reshape(B, S, FOX_HEADS).transpose(0, 2, 1)
    ccol, crow = c[..., None], c.reshape(B, FOX_HEADS, S // FOX_TQ, 1, FOX_TQ)
    qn, kn, vb = fox_prep_fwd(P, gq8, gk8)
    o_raw, o_a, lse = fox_core_fwd(qn, kn, vb, ccol, crow, go2, B=B)
    G = gdn_prep_fwd(P, w["conv_w"], B=B)
    ob_raw, o_b, states, gdn_tm = gdn_fwd(G, gates, P, sp["gdn_onorm_g"], B=B)
    oab = jnp.concatenate([o_a, o_b], axis=1)
    if "late" in w:
        w = {**w, **w["late"](oab)}
    x2 = matmul(oab, w["w_out"], residual=x, name="mm_out")
    hq = rms_fwd(x2, sp["norm_xattn_g"], name="rms_xattn")
    hm = rms_fwd(mem, sp["mem_norm_g"], name="rms_mem")
    cq = matmul(hq, w["w_cq"], name="mm_cq")
    ckv = matmul(hm, w["w_ckv"], name="mm_ckv")
    co = xattn_fwd(cq, ckv, sp["xattn_qnorm_g"], sp["xattn_knorm_g"], B=B)
    x3 = matmul(co, w["w_co"], b_stacked=True, residual=x2, name="mm_co")
    hf = rms_fwd(x3, sp["norm_mlp_g"], name="rms_mlp")
    act = matmul(hf, w["w_mlp1"], b_stacked=True, relu2_out=True, out_dtype=MXU_DTYPE, name="mm_mlp1")
    dy, loss = matmul_rows(act, w["w_mlp2"], (x3, target), mode="loss", name="mm_mlp2_loss")

    da = matmul(dy, w["w_mlp2"], tb=True, relu2_bwd_aux=act, out_dtype=MXU_DTYPE, name="mm_d_act")
    g_mlp2 = matmul(act, dy, ta=True, out_dtype=WIRE_DTYPE, name="mm_g_mlp2")
    g_mlp1 = matmul(hf, da, ta=True, out_stacked=True, out_dtype=WIRE_DTYPE, name="mm_g_mlp1")
    by_rows = lambda g: g.reshape(N_CHIPS, g.shape[0] // N_CHIPS, g.shape[1])
    early = w.get("grads_ready", lambda grads: jnp.zeros((1, 1), f32))
    tok = early(dict(w_mlp1=g_mlp1, w_mlp2=by_rows(g_mlp2)))[0, 0]
    dx3, g_norm_mlp = matmul_rows(da, w["w_mlp1"], (x3, sp["norm_mlp_g"] + tok, dy), mode="rms_bwd", tb=True,
                                  b_stacked=True, name="mm_d_hf_rms")
    dco = matmul(dx3, w["w_co"], tb=True, b_stacked=True, name="mm_d_co")
    g_co = matmul(co, dx3, ta=True, out_stacked=True, out_dtype=WIRE_DTYPE, name="mm_g_co")
    dcq, dckv, g_xq, g_xk = xattn_bwd(cq, ckv, sp["xattn_qnorm_g"], sp["xattn_knorm_g"], dco, B=B)
    g_cq = matmul(hq, dcq, ta=True, out_dtype=WIRE_DTYPE, name="mm_g_cq")
    g_ckv = matmul(hm, dckv, ta=True, out_dtype=WIRE_DTYPE, name="mm_g_ckv")
    _, g_mem_norm = matmul_rows(dckv, w["w_ckv"], (mem, sp["mem_norm_g"], None), mode="rms_bwd", tb=True, name="mm_d_hm_rms")
    dx2, g_norm_xattn = matmul_rows(dcq, w["w_cq"], (x2, sp["norm_xattn_g"], dx3), mode="rms_bwd", tb=True, name="mm_d_hq_rms")
    doab = matmul(dx2, w["w_out"], tb=True, name="mm_d_oab")
    g_out = matmul(oab, dx2, ta=True, out_dtype=WIRE_DTYPE, name="mm_g_out")
    tok = early(dict(w_co=g_co, w_cq=by_rows(g_cq), w_ckv=by_rows(g_ckv), w_out=by_rows(g_out)))[0, 0]
    dqn, dkn, dv_f, dccol, dcrow, dgo2 = fox_core_bwd(qn, kn, vb, ccol, crow, go2 + tok, o_raw, lse, doab, B=B)
    dq_f, dk_f, dgq8, dgk8 = fox_prep_bwd(P, gq8, gk8, dqn, dkn)
    dGq, dGk, dGv, dgt, dz, g_gdn_on = gdn_bwd(G, gates, P, sp["gdn_onorm_g"], ob_raw, states, gdn_tm, doab, B=B)
    dPg, g_conv = gdn_prep_bwd(P, w["conv_w"], jnp.concatenate([dGq, dGk, dGv], axis=1), B=B)
    dc = (dccol[..., 0] + dcrow.reshape(B, FOX_HEADS, S)).transpose(0, 2, 1).reshape(T, FOX_HEADS)
    dgates = dgt + jnp.pad(dc, ((0, 0), (SM_F, LANES - SM_F - FOX_HEADS)))
    dsmall, par = gates_bwd(P, bias, alog, dgates, B=B)
    dP = jnp.concatenate([dq_f, dk_f, dv_f, dPg, dz, dsmall, jnp.zeros((T, IN_ALIGNED - COL_SMALL - LANES), MXU_DTYPE)], axis=1)
    g_wa = matmul(dP, h1, ta=True, out_dtype=WIRE_DTYPE, name="mm_g_in", tm=IN_TILE)
    dx, g_norm_mix = matmul_rows(dP, w["wa_t"], (x, sp["norm_mix_g"], dx2), mode="rms_bwd", tk=IN_TILE, name="mm_d_h1_rms")

    fold = lambda g: jnp.sum(g.reshape(-1, FOX_HEAD_DIM), axis=0, keepdims=True)
    g_in = jnp.pad(unalign_w_in_t(g_wa).reshape(N_CHIPS, IN_SHARD, D_MODEL), ((0, 0), (0, IN_SHARD_PAD - IN_SHARD), (0, 0)))
    big = dict(w_in=g_in, w_out=by_rows(g_out), w_cq=by_rows(g_cq), w_ckv=by_rows(g_ckv), w_co=g_co, w_mlp1=g_mlp1,
               w_mlp2=by_rows(g_mlp2))
    small = dict(norm_mix_g=g_norm_mix, fox_qnorm_g=fold(dgq8), fox_knorm_g=fold(dgk8),
                 fox_f_bias=par[0:1, SM_F:SM_F + FOX_HEADS], fox_onorm_g=fold(dgo2), gdn_conv_w=g_conv,
                 gdn_A_log=par[1:2, SM_A:SM_A + GDN_HEADS], gdn_dt_bias=par[0:1, SM_A:SM_A + GDN_HEADS],
                 gdn_onorm_g=g_gdn_on, norm_xattn_g=g_norm_xattn, mem_norm_g=g_mem_norm,
                 xattn_qnorm_g=g_xq, xattn_knorm_g=g_xk, norm_mlp_g=g_norm_mlp)
    return loss, dx, big, small


MESH_IDS = pl.DeviceIdType.MESH
N_CHIPS = 4
HBM_SPEC = pl.BlockSpec(memory_space=pltpu.HBM)
PACK_ROWS = 30720
PACK_HALF = PACK_ROWS // 2
PACK_BLOCK = 3072


def _place():
    return lax.axis_index("x"), lax.axis_index("y"), lax.axis_index("c")


def _other_chips(x, y):
    return [(1 - x, y), (x, 1 - y), (1 - x, 1 - y)]


def _remote(src, dst, send_sem, recv_sem, to):
    return pltpu.make_async_remote_copy(src_ref=src, dst_ref=dst, send_sem=send_sem, recv_sem=recv_sem,
                                        device_id=to, device_id_type=MESH_IDS)


def all_gather_shards(packed):
    half = PACK_HALF

    def body(src_ref, out_ref, send_sems, recv_sems):
        x, y, c = _place()
        me_chip = 2 * x + y
        sibling = (x, y, 1 - c)
        chips = _other_chips(x, y)

        def rows(chip, core):
            return out_ref.at[chip, pl.ds(core * half, half), :]

        sends = [_remote(src_ref.at[pl.ds(c * half, half), :], rows(me_chip, c), send_sems.at[j], recv_sems.at[j], (px, py, c))
                 for j, (px, py) in enumerate(chips)]
        for cp in sends:
            cp.start()
        passed = []
        for j, (px, py) in enumerate(chips):
            theirs = rows(2 * px + py, c)
            _remote(theirs, theirs, send_sems.at[j], recv_sems.at[j], (px, py, c)).wait_recv()
            cp = _remote(theirs, theirs, send_sems.at[3 + j], recv_sems.at[3 + j], sibling)
            cp.start()
            passed.append(cp)
        for j, (px, py) in enumerate(chips):
            theirs = rows(2 * px + py, 1 - c)
            _remote(theirs, theirs, send_sems.at[3 + j], recv_sems.at[3 + j], sibling).wait_recv()
        for cp in sends + passed:
            cp.wait_send()

    return pl.pallas_call(
        body, name="all_gather_shards", in_specs=[HBM_SPEC], out_specs=HBM_SPEC,
        out_shape=jax.ShapeDtypeStruct((N_CHIPS,) + packed.shape, packed.dtype),
        scratch_shapes=[pltpu.SemaphoreType.DMA((6,)), pltpu.SemaphoreType.DMA((6,))],
    )(packed)


def exchange_core_halves(G):
    half = PACK_HALF

    def body(g_ref, land_ref, send_sem, recv_sem):
        x, y, c = _place()
        cp = _remote(g_ref.at[:, pl.ds((1 - c) * half, half), :], land_ref, send_sem, recv_sem, (x, y, 1 - c))
        cp.start()
        cp.wait()

    return pl.pallas_call(
        body, name="exchange_core_halves", in_specs=[HBM_SPEC], out_specs=HBM_SPEC,
        out_shape=jax.ShapeDtypeStruct((N_CHIPS, half, LANES), G.dtype),
        scratch_shapes=[pltpu.SemaphoreType.DMA(()), pltpu.SemaphoreType.DMA(())],
    )(G)


def add_core_halves(G, land, core):
    nb = PACK_HALF // PACK_BLOCK

    def body(c_ref, g_ref, l_ref, o_ref):
        o_ref[...] = (g_ref[...].astype(f32) + l_ref[...].astype(f32)).astype(o_ref.dtype)

    blk = (1, PACK_BLOCK, LANES)
    return pl.pallas_call(
        body, name="add_core_halves",
        grid_spec=pltpu.PrefetchScalarGridSpec(
            num_scalar_prefetch=1, grid=(N_CHIPS, nb),
            in_specs=[pl.BlockSpec(blk, lambda k, i, c_ref: (k, c_ref[0] * nb + i, 0)),
                      pl.BlockSpec(blk, lambda k, i, c_ref: (k, i, 0))],
            out_specs=pl.BlockSpec(blk, lambda k, i, c_ref: (k, i, 0))),
        out_shape=jax.ShapeDtypeStruct(land.shape, land.dtype),
        compiler_params=_cparams("parallel", "parallel"),
    )(core, G, land)


def scatter_to_chips(part):
    def body(p_ref, land_ref, send_sems, recv_sems):
        x, y, c = _place()
        me_chip = 2 * x + y
        chips = _other_chips(x, y)
        sends = [_remote(p_ref.at[2 * px + py], land_ref.at[me_chip], send_sems.at[j], recv_sems.at[j], (px, py, c))
                 for j, (px, py) in enumerate(chips)]
        for cp in sends:
            cp.start()
        for j, (px, py) in enumerate(chips):
            slot = land_ref.at[2 * px + py]
            _remote(slot, slot, send_sems.at[j], recv_sems.at[j], (px, py, c)).wait_recv()
        for cp in sends:
            cp.wait_send()

    return pl.pallas_call(
        body, name="scatter_to_chips", in_specs=[HBM_SPEC], out_specs=HBM_SPEC,
        out_shape=jax.ShapeDtypeStruct(part.shape, part.dtype),
        scratch_shapes=[pltpu.SemaphoreType.DMA((3,)), pltpu.SemaphoreType.DMA((3,))],
    )(part)


def sum_chips(part, land, order):
    nb = PACK_HALF // PACK_BLOCK

    def body(order_ref, p_ref, l1_ref, l2_ref, l3_ref, o_ref):
        o_ref[...] = ((p_ref[0].astype(f32) + l1_ref[0].astype(f32)) + l2_ref[0].astype(f32)) + l3_ref[0].astype(f32)

    slot = lambda j: pl.BlockSpec((1, PACK_BLOCK, LANES), lambda i, order_ref: (order_ref[j], i, 0))
    return pl.pallas_call(
        body, name="sum_chips",
        grid_spec=pltpu.PrefetchScalarGridSpec(
            num_scalar_prefetch=1, grid=(nb,), in_specs=[slot(0), slot(1), slot(2), slot(3)],
            out_specs=pl.BlockSpec((PACK_BLOCK, LANES), lambda i, order_ref: (i, 0))),
        out_shape=jax.ShapeDtypeStruct((PACK_HALF, LANES), f32),
        compiler_params=_cparams("parallel"),
    )(order, part, land, land, land)


def swap_core_halves(red):
    def body(r_ref, out_ref, send_sem, recv_sem):
        x, y, c = _place()
        cp = _remote(r_ref, out_ref, send_sem, recv_sem, (x, y, 1 - c))
        cp.start()
        cp.wait()

    return pl.pallas_call(
        body, name="swap_core_halves", in_specs=[HBM_SPEC], out_specs=HBM_SPEC,
        out_shape=jax.ShapeDtypeStruct(red.shape, red.dtype),
        scratch_shapes=[pltpu.SemaphoreType.DMA(()), pltpu.SemaphoreType.DMA(())],
    )(red)


def _half(ref, core):
    rows = ref.shape[-2] // 2
    return ref.at[(slice(None),) * (len(ref.shape) - 2) + (pl.ds(core * rows, rows), slice(None))]


def gather_weights(shards, conv):
    n = len(shards)

    def body(*refs):
        src, conv_src = refs[:n], refs[n]
        out, conv_out = refs[n + 1:2 * n + 1], refs[2 * n + 1]
        send_sems, recv_sems = refs[2 * n + 2], refs[2 * n + 3]
        x, y, c = _place()
        me_chip = 2 * x + y
        sibling = (x, y, 1 - c)
        chips = _other_chips(x, y)
        sends = []
        for a in range(n):
            for j, (px, py) in enumerate(chips):
                sends.append(_remote(_half(src[a], c), _half(out[a].at[me_chip], c),
                                     send_sems.at[6 * a + j], recv_sems.at[6 * a + j], (px, py, c)))
        for j, (px, py) in enumerate(chips):
            sends.append(_remote(conv_src, conv_out.at[me_chip], send_sems.at[6 * n + j], recv_sems.at[6 * n + j], (px, py, c)))
        for cp in sends:
            cp.start()
        passed = []
        for a in range(n):
            for j, (px, py) in enumerate(chips):
                theirs = _half(out[a].at[2 * px + py], c)
                _remote(theirs, theirs, send_sems.at[6 * a + j], recv_sems.at[6 * a + j], (px, py, c)).wait_recv()
                cp = _remote(theirs, theirs, send_sems.at[6 * a + 3 + j], recv_sems.at[6 * a + 3 + j], sibling)
                cp.start()
                passed.append(cp)
        for j, (px, py) in enumerate(chips):
            theirs = conv_out.at[2 * px + py]
            _remote(theirs, theirs, send_sems.at[6 * n + j], recv_sems.at[6 * n + j], (px, py, c)).wait_recv()
        for a in range(n):
            for j, (px, py) in enumerate(chips):
                theirs = _half(out[a].at[2 * px + py], 1 - c)
                _remote(theirs, theirs, send_sems.at[6 * a + 3 + j], recv_sems.at[6 * a + 3 + j], sibling).wait_recv()
        for cp in sends + passed:
            cp.wait_send()

    return pl.pallas_call(
        body, name="gather_weights", in_specs=[HBM_SPEC] * (n + 1), out_specs=[HBM_SPEC] * (n + 1),
        out_shape=[jax.ShapeDtypeStruct((N_CHIPS,) + s.shape, s.dtype) for s in list(shards) + [conv]],
        scratch_shapes=[pltpu.SemaphoreType.DMA((6 * n + 3,)), pltpu.SemaphoreType.DMA((6 * n + 3,))],
    )(*shards, conv)


SEM_SPEC = pl.BlockSpec(memory_space=pltpu.SEMAPHORE)
SPLIT_EFFECT = pltpu.SideEffectType.DATAFLOW_SIDE_EFFECTING


def _gather_async_copies(src, land, send_sems, recv_sems, x, y, c):
    me_chip = 2 * x + y
    sends, arrivals = [], []
    for a in range(len(src)):
        for j, (px, py) in enumerate(_other_chips(x, y)):
            for core in range(2):
                sends.append(_remote(_half(src[a], c), _half(land[a].at[me_chip], c), send_sems.at[6 * a + 2 * j + core],
                                     recv_sems.at[6 * a + 2 * j + c], (px, py, core)))
                theirs = _half(land[a].at[2 * px + py], core)
                arrivals.append(_remote(theirs, theirs, send_sems.at[6 * a + 2 * j + core],
                                        recv_sems.at[6 * a + 2 * j + core], (px, py, core)))
    return sends, arrivals


def gather_weights_start(shards, after):
    n = len(shards)

    def body(*refs):
        src, land = refs[:n], refs[n:2 * n]
        send_sems, recv_sems, token = refs[2 * n + 1], refs[2 * n + 2], refs[4 * n + 3]
        x, y, c = _place()
        for cp in _gather_async_copies(src, land, send_sems, recv_sems, x, y, c)[0]:
            cp.start()
        token[...] = jnp.zeros_like(token)

    zones = [pltpu.with_memory_space_constraint(lax.empty((N_CHIPS,) + s.shape, s.dtype), pltpu.HBM) for s in shards]
    srcs = [pltpu.with_memory_space_constraint(s, pltpu.HBM) for s in shards]
    out = pl.pallas_call(
        body, name="gather_weights_start",
        out_shape=[pltpu.SemaphoreType.DMA((6 * n,)), pltpu.SemaphoreType.DMA((6 * n,))]
        + [pltpu.HBM(s.shape, s.dtype) for s in shards] + [pltpu.HBM(z.shape, z.dtype) for z in zones]
        + [jax.ShapeDtypeStruct((8, LANES), f32)],
        in_specs=[HBM_SPEC] * (2 * n) + [pl.BlockSpec(memory_space=pl.ANY)],
        out_specs=[SEM_SPEC, SEM_SPEC] + [HBM_SPEC] * (2 * n) + [pl.BlockSpec(memory_space=pltpu.VMEM)],
        input_output_aliases={i: 2 + i for i in range(2 * n)},
        compiler_params=pltpu.CompilerParams(has_side_effects=SPLIT_EFFECT),
    )(*srcs, *zones, after)
    return out[0], out[1], out[2:2 + n], out[2 + n:2 + 2 * n], out[-1]


def gather_weights_wait(send_sems, recv_sems, shards, zones, after):
    n = len(shards)

    def body(*refs):
        src, land = refs[:n], refs[n:2 * n]
        send_sems, recv_sems = refs[2 * n], refs[2 * n + 1]
        x, y, c = _place()
        sends, arrivals = _gather_async_copies(src, land, send_sems, recv_sems, x, y, c)
        for cp in sends:
            cp.wait_send()
        for cp in arrivals:
            cp.wait_recv()

    out = pl.pallas_call(
        body, name="gather_weights_wait",
        out_shape=[pltpu.HBM(s.shape, s.dtype) for s in shards] + [pltpu.HBM(z.shape, z.dtype) for z in zones],
        in_specs=[HBM_SPEC] * (2 * n) + [SEM_SPEC, SEM_SPEC, pl.BlockSpec(memory_space=pl.ANY)],
        out_specs=[HBM_SPEC] * (2 * n),
        input_output_aliases={i: i for i in range(2 * n)},
        compiler_params=pltpu.CompilerParams(has_side_effects=SPLIT_EFFECT),
    )(*shards, *zones, send_sems, recv_sems, after)
    return out[n:]


def swap_grad_halves(grads, *, name):
    n = len(grads)

    def body(*refs):
        g, land, send_sems, recv_sems = refs[:n], refs[n:2 * n], refs[2 * n], refs[2 * n + 1]
        x, y, c = _place()
        copies = [_remote(_half(g[a], 1 - c), land[a], send_sems.at[a], recv_sems.at[a], (x, y, 1 - c)) for a in range(n)]
        for cp in copies:
            cp.start()
        for cp in copies:
            cp.wait()

    return pl.pallas_call(
        body, name=name, in_specs=[HBM_SPEC] * n, out_specs=[HBM_SPEC] * n,
        out_shape=[jax.ShapeDtypeStruct((N_CHIPS, g.shape[1] // 2, g.shape[2]), g.dtype) for g in grads],
        scratch_shapes=[pltpu.SemaphoreType.DMA((n,)), pltpu.SemaphoreType.DMA((n,))],
    )(*grads)


GRAD_ROWS = 256


def add_grad_halves(g, land, core, *, name):
    _, half, cols = land.shape
    tr = GRAD_ROWS if half % GRAD_ROWS == 0 else half
    nb = half // tr

    def body(c_ref, g_ref, l_ref, o_ref):
        o_ref[...] = (g_ref[...].astype(f32) + l_ref[...].astype(f32)).astype(o_ref.dtype)

    blk = (1, tr, cols)
    return pl.pallas_call(
        body, name=name,
        grid_spec=pltpu.PrefetchScalarGridSpec(
            num_scalar_prefetch=1, grid=(N_CHIPS, nb),
            in_specs=[pl.BlockSpec(blk, lambda k, i, c_ref: (k, c_ref[0] * nb + i, 0)),
                      pl.BlockSpec(blk, lambda k, i, c_ref: (k, i, 0))],
            out_specs=pl.BlockSpec(blk, lambda k, i, c_ref: (k, i, 0))),
        out_shape=jax.ShapeDtypeStruct(land.shape, land.dtype),
        compiler_params=_cparams("parallel", "parallel"),
    )(core, g, land)


def scatter_grads(parts):
    n = len(parts)

    def body(*refs):
        p, land, send_sems, recv_sems = refs[:n], refs[n:2 * n], refs[2 * n], refs[2 * n + 1]
        x, y, c = _place()
        me_chip = 2 * x + y
        chips = _other_chips(x, y)
        sends = [_remote(p[a].at[2 * px + py], land[a].at[me_chip], send_sems.at[3 * a + j], recv_sems.at[3 * a + j], (px, py, c))
                 for a in range(n) for j, (px, py) in enumerate(chips)]
        for cp in sends:
            cp.start()
        for a in range(n):
            for j, (px, py) in enumerate(chips):
                slot = land[a].at[2 * px + py]
                _remote(slot, slot, send_sems.at[3 * a + j], recv_sems.at[3 * a + j], (px, py, c)).wait_recv()
        for cp in sends:
            cp.wait_send()

    return pl.pallas_call(
        body, name="scatter_grads", in_specs=[HBM_SPEC] * n, out_specs=[HBM_SPEC] * n,
        out_shape=[jax.ShapeDtypeStruct(p.shape, p.dtype) for p in parts],
        scratch_shapes=[pltpu.SemaphoreType.DMA((3 * n,)), pltpu.SemaphoreType.DMA((3 * n,))],
    )(*parts)


def _scatter_async_copies(parts, land, send_sems, recv_sems, x, y, c):
    me_chip = 2 * x + y
    sends, arrivals = [], []
    for a in range(len(parts)):
        for j, (px, py) in enumerate(_other_chips(x, y)):
            sems = (send_sems.at[3 * a + j], recv_sems.at[3 * a + j], (px, py, c))
            sends.append(_remote(parts[a].at[2 * px + py], land[a].at[me_chip], *sems))
            slot = land[a].at[2 * px + py]
            arrivals.append(_remote(slot, slot, *sems))
    return sends, arrivals


def scatter_grads_start(parts, *, name):
    n = len(parts)

    def body(*refs):
        p, land = refs[:n], refs[n:2 * n]
        send_sems, recv_sems, token = refs[2 * n], refs[2 * n + 1], refs[4 * n + 2]
        x, y, c = _place()
        for cp in _scatter_async_copies(p, land, send_sems, recv_sems, x, y, c)[0]:
            cp.start()
        token[...] = jnp.zeros_like(token)

    zones = [pltpu.with_memory_space_constraint(lax.empty(p.shape, p.dtype), pltpu.HBM) for p in parts]
    srcs = [pltpu.with_memory_space_constraint(p, pltpu.HBM) for p in parts]
    hbm = [pltpu.HBM(p.shape, p.dtype) for p in parts]
    out = pl.pallas_call(
        body, name=name,
        out_shape=[pltpu.SemaphoreType.DMA((3 * n,)), pltpu.SemaphoreType.DMA((3 * n,))] + hbm + hbm
        + [jax.ShapeDtypeStruct((8, LANES), f32)],
        in_specs=[HBM_SPEC] * (2 * n),
        out_specs=[SEM_SPEC, SEM_SPEC] + [HBM_SPEC] * (2 * n) + [pl.BlockSpec(memory_space=pltpu.VMEM)],
        input_output_aliases={i: 2 + i for i in range(2 * n)},
        compiler_params=pltpu.CompilerParams(has_side_effects=SPLIT_EFFECT),
    )(*srcs, *zones)
    return out[0], out[1], out[2:2 + n], out[2 + n:2 + 2 * n], out[-1]


def scatter_grads_wait(send_sems, recv_sems, parts, zones, after, *, name):
    n = len(parts)

    def body(*refs):
        p, land = refs[:n], refs[n:2 * n]
        x, y, c = _place()
        sends, arrivals = _scatter_async_copies(p, land, refs[2 * n], refs[2 * n + 1], x, y, c)
        for cp in sends:
            cp.wait_send()
        for cp in arrivals:
            cp.wait_recv()

    hbm = [pltpu.HBM(p.shape, p.dtype) for p in parts]
    out = pl.pallas_call(
        body, name=name, out_shape=hbm + hbm,
        in_specs=[HBM_SPEC] * (2 * n) + [SEM_SPEC, SEM_SPEC, pl.BlockSpec(memory_space=pl.ANY)],
        out_specs=[HBM_SPEC] * (2 * n),
        input_output_aliases={i: i for i in range(2 * n)},
        compiler_params=pltpu.CompilerParams(has_side_effects=SPLIT_EFFECT),
    )(*parts, *zones, send_sems, recv_sems, after)
    return out[:n], out[n:]


def sum_grads(part, land, order, *, name):
    _, half, cols = part.shape
    tr = GRAD_ROWS if half % GRAD_ROWS == 0 else half

    def body(order_ref, p_ref, l1_ref, l2_ref, l3_ref, o_ref):
        o_ref[...] = ((p_ref[0].astype(f32) + l1_ref[0].astype(f32)) + l2_ref[0].astype(f32)) + l3_ref[0].astype(f32)

    slot = lambda j: pl.BlockSpec((1, tr, cols), lambda i, order_ref: (order_ref[j], i, 0))
    return pl.pallas_call(
        body, name=name,
        grid_spec=pltpu.PrefetchScalarGridSpec(
            num_scalar_prefetch=1, grid=(half // tr,), in_specs=[slot(0), slot(1), slot(2), slot(3)],
            out_specs=pl.BlockSpec((tr, cols), lambda i, order_ref: (i, 0))),
        out_shape=jax.ShapeDtypeStruct((half, cols), f32),
        compiler_params=_cparams("parallel"),
    )(order, part, land, land, land)


def swap_reduced_halves(mine, *, name):
    n = len(mine)

    def body(*refs):
        r, out, send_sems, recv_sems = refs[:n], refs[n:2 * n], refs[2 * n], refs[2 * n + 1]
        x, y, c = _place()
        copies = [_remote(r[a], out[a], send_sems.at[a], recv_sems.at[a], (x, y, 1 - c)) for a in range(n)]
        for cp in copies:
            cp.start()
        for cp in copies:
            cp.wait()

    return pl.pallas_call(
        body, name=name, in_specs=[HBM_SPEC] * n, out_specs=[HBM_SPEC] * n,
        out_shape=[jax.ShapeDtypeStruct(r.shape, r.dtype) for r in mine],
        scratch_shapes=[pltpu.SemaphoreType.DMA((n,)), pltpu.SemaphoreType.DMA((n,))],
    )(*mine)


def adamw_halves(w, mine, theirs, m, v, core, *, name):
    R, C = w.shape
    tr = min(GRAD_ROWS, R // 2)
    half_nb = R // 2 // tr

    def body(c_ref, w_ref, a_ref, b_ref, m_ref, v_ref, g_ref, d_ref, nm_ref, nv_ref):
        low = pl.program_id(0) < half_nb
        gv = jnp.where(low == (c_ref[0] == 0), a_ref[...], b_ref[...])
        nm = ADAM_B1 * m_ref[...] + (1.0 - ADAM_B1) * gv
        nv = ADAM_B2 * v_ref[...] + (1.0 - ADAM_B2) * jnp.square(gv)
        m_hat = nm / (1.0 - ADAM_B1 ** ADAM_STEP)
        v_hat = nv / (1.0 - ADAM_B2 ** ADAM_STEP)
        g_ref[...] = gv
        d_ref[...] = -ADAM_LR * (m_hat / (jnp.sqrt(v_hat) + ADAM_EPS) + ADAM_WD * w_ref[...])
        nm_ref[...] = nm
        nv_ref[...] = nv

    full = pl.BlockSpec((tr, C), lambda i, c_ref: (i, 0))
    part = pl.BlockSpec((tr, C), lambda i, c_ref: (i % half_nb, 0))
    out = jax.ShapeDtypeStruct((R, C), f32)
    return pl.pallas_call(
        body, name=name,
        grid_spec=pltpu.PrefetchScalarGridSpec(
            num_scalar_prefetch=1, grid=(2 * half_nb,), in_specs=[full, part, part, full, full], out_specs=[full] * 4),
        out_shape=[out] * 4, compiler_params=_cparams("parallel"),
    )(core, w, mine, theirs, m, v)


N_DEV = 8


def all_reduce_small(v):
    def body(src_ref, out_ref, land_ref, send_sems, recv_sems):
        x, y, c = _place()
        me = 4 * x + 2 * y + c
        copies = []
        for r in range(1, N_DEV):
            peer = ((1 - x) if r & 4 else x, (1 - y) if r & 2 else y, (1 - c) if r & 1 else c)
            copies.append(_remote(src_ref, land_ref.at[r], send_sems.at[r - 1], recv_sems.at[r - 1], peer))
        for cp in copies:
            cp.start()
        land_ref[0] = src_ref[...]
        for cp in copies:
            cp.wait()
        acc = land_ref[me]
        for d in range(1, N_DEV):
            acc = acc + land_ref[jnp.bitwise_xor(me, d)]
        out_ref[...] = acc

    vm = pl.BlockSpec(memory_space=pltpu.VMEM)
    return pl.pallas_call(
        body, name="all_reduce_small", in_specs=[vm], out_specs=vm,
        out_shape=jax.ShapeDtypeStruct(v.shape, v.dtype),
        scratch_shapes=[pltpu.VMEM((N_DEV,) + v.shape, v.dtype),
                        pltpu.SemaphoreType.DMA((N_DEV - 1,)), pltpu.SemaphoreType.DMA((N_DEV - 1,))],
    )(v)


def adamw(w, g, m, v, *, name, tr=None, tc=None):
    R, C = w.shape
    if tc is None:
        tr, tc = min(tr, R), C
        blk = pl.BlockSpec((tr, C), lambda i: (i, 0))
    else:
        tr = R
        blk = pl.BlockSpec((R, tc), lambda i: (0, i))

    def body(w_ref, g_ref, m_ref, v_ref, d_ref, nm_ref, nv_ref):
        gv = g_ref[...]
        nm = ADAM_B1 * m_ref[...] + (1.0 - ADAM_B1) * gv
        nv = ADAM_B2 * v_ref[...] + (1.0 - ADAM_B2) * jnp.square(gv)
        m_hat = nm / (1.0 - ADAM_B1 ** ADAM_STEP)
        v_hat = nv / (1.0 - ADAM_B2 ** ADAM_STEP)
        d_ref[...] = -ADAM_LR * (m_hat / (jnp.sqrt(v_hat) + ADAM_EPS) + ADAM_WD * w_ref[...])
        nm_ref[...] = nm
        nv_ref[...] = nv

    out = jax.ShapeDtypeStruct((R, C), f32)
    return pl.pallas_call(
        body, name=name, grid=((R // tr) * (C // tc),), in_specs=[blk] * 4, out_specs=[blk] * 3, out_shape=[out] * 3,
        compiler_params=_cparams("parallel"),
    )(w, g, m, v)


BIG_SHARDS = (("w_in", (1024, 900), True), ("w_out", (256, 1024), False), ("w_cq", (256, 512), False),
              ("w_ckv", (256, 1024), False), ("w_co", (512, 256), True), ("w_mlp1", (1024, 1024), True),
              ("w_mlp2", (1024, 1024), False))
CONV_SHARD = (CONV_WIDTH, 3 * GDN_WIDTH // N_CHIPS)
SMALL_DIMS = (("norm_mix_g", 1024), ("fox_qnorm_g", 64), ("fox_knorm_g", 64), ("fox_f_bias", 8), ("fox_onorm_g", 64),
              ("gdn_A_log", 4), ("gdn_dt_bias", 4), ("gdn_onorm_g", 128), ("norm_xattn_g", 1024), ("mem_norm_g", 1024),
              ("xattn_qnorm_g", 128), ("xattn_knorm_g", 128), ("norm_mlp_g", 1024))
WEIGHT_ORDER = ("norm_mix_g", "w_in", "fox_qnorm_g", "fox_knorm_g", "fox_f_bias", "fox_onorm_g", "gdn_conv_w", "gdn_A_log",
                "gdn_dt_bias", "gdn_onorm_g", "w_out", "norm_xattn_g", "mem_norm_g", "w_cq", "w_ckv", "xattn_qnorm_g",
                "xattn_knorm_g", "w_co", "norm_mlp_g", "w_mlp1", "w_mlp2")


def _pack_rows(pieces, rows, lead=()):
    cat = jnp.concatenate([p.reshape(lead + (-1,)) for p in pieces], axis=-1)
    cat = jnp.pad(cat, [(0, 0)] * len(lead) + [(0, rows * LANES - cat.shape[-1])])
    return cat.reshape(lead + (rows, LANES))


def _unpack_rows(buf, sizes, lead=()):
    flat = buf.reshape(lead + (-1,))
    out, off = [], 0
    for n in sizes:
        out.append(flat[..., off:off + n])
        off += n
    return out


def _conv_to_wire(conv):
    return lax.bitcast_convert_type(conv, bf16)


def _conv_from_wire(wire):
    return lax.bitcast_convert_type(wire, f32)


SMALL_ROWS = 96
SMALL_ADAM_ROWS = 56


def kernel(x, mem, norm_mix_g, w_in, fox_qnorm_g, fox_knorm_g, fox_f_bias, fox_onorm_g, gdn_conv_w, gdn_A_log, gdn_dt_bias, gdn_onorm_g, w_out, norm_xattn_g, mem_norm_g, w_cq, w_ckv, xattn_qnorm_g, xattn_knorm_g, w_co, norm_mlp_g, w_mlp1, w_mlp2, loss_target, m_norm_mix_g, m_w_in, m_fox_qnorm_g, m_fox_knorm_g, m_fox_f_bias, m_fox_onorm_g, m_gdn_conv_w, m_gdn_A_log, m_gdn_dt_bias, m_gdn_onorm_g, m_w_out, m_norm_xattn_g, m_mem_norm_g, m_w_cq, m_w_ckv, m_xattn_qnorm_g, m_xattn_knorm_g, m_w_co, m_norm_mlp_g, m_w_mlp1, m_w_mlp2, v_norm_mix_g, v_w_in, v_fox_qnorm_g, v_fox_knorm_g, v_fox_f_bias, v_fox_onorm_g, v_gdn_conv_w, v_gdn_A_log, v_gdn_dt_bias, v_gdn_onorm_g, v_w_out, v_norm_xattn_g, v_mem_norm_g, v_w_cq, v_w_ckv, v_xattn_qnorm_g, v_xattn_knorm_g, v_w_co, v_norm_mlp_g, v_w_mlp1, v_w_mlp2):
    wts = dict(norm_mix_g=norm_mix_g, w_in=w_in, fox_qnorm_g=fox_qnorm_g, fox_knorm_g=fox_knorm_g, fox_f_bias=fox_f_bias,
               fox_onorm_g=fox_onorm_g, gdn_conv_w=gdn_conv_w, gdn_A_log=gdn_A_log, gdn_dt_bias=gdn_dt_bias,
               gdn_onorm_g=gdn_onorm_g, w_out=w_out, norm_xattn_g=norm_xattn_g, mem_norm_g=mem_norm_g, w_cq=w_cq, w_ckv=w_ckv,
               xattn_qnorm_g=xattn_qnorm_g, xattn_knorm_g=xattn_knorm_g, w_co=w_co, norm_mlp_g=norm_mlp_g, w_mlp1=w_mlp1,
               w_mlp2=w_mlp2)
    mom = dict(norm_mix_g=m_norm_mix_g, w_in=m_w_in, fox_qnorm_g=m_fox_qnorm_g, fox_knorm_g=m_fox_knorm_g,
               fox_f_bias=m_fox_f_bias, fox_onorm_g=m_fox_onorm_g, gdn_conv_w=m_gdn_conv_w, gdn_A_log=m_gdn_A_log,
               gdn_dt_bias=m_gdn_dt_bias, gdn_onorm_g=m_gdn_onorm_g, w_out=m_w_out, norm_xattn_g=m_norm_xattn_g,
               mem_norm_g=m_mem_norm_g, w_cq=m_w_cq, w_ckv=m_w_ckv, xattn_qnorm_g=m_xattn_qnorm_g,
               xattn_knorm_g=m_xattn_knorm_g, w_co=m_w_co, norm_mlp_g=m_norm_mlp_g, w_mlp1=m_w_mlp1, w_mlp2=m_w_mlp2)
    var = dict(norm_mix_g=v_norm_mix_g, w_in=v_w_in, fox_qnorm_g=v_fox_qnorm_g, fox_knorm_g=v_fox_knorm_g,
               fox_f_bias=v_fox_f_bias, fox_onorm_g=v_fox_onorm_g, gdn_conv_w=v_gdn_conv_w, gdn_A_log=v_gdn_A_log,
               gdn_dt_bias=v_gdn_dt_bias, gdn_onorm_g=v_gdn_onorm_g, w_out=v_w_out, norm_xattn_g=v_norm_xattn_g,
               mem_norm_g=v_mem_norm_g, w_cq=v_w_cq, w_ckv=v_w_ckv, xattn_qnorm_g=v_xattn_qnorm_g,
               xattn_knorm_g=v_xattn_knorm_g, w_co=v_w_co, norm_mlp_g=v_norm_mlp_g, w_mlp1=v_w_mlp1, w_mlp2=v_w_mlp2)
    B, S, D = x.shape
    T = B * S
    big_names = [n for n, _, _ in BIG_SHARDS]
    chip = 2 * lax.axis_index("x") + lax.axis_index("y")
    core = lax.axis_index("c").astype(jnp.int32).reshape(1)

    shards = {n: wts[n][0].astype(MXU_DTYPE) for n in big_names[1:]}
    in_t = lambda p: jnp.swapaxes(p[0], 0, 1)
    shards["w_in"] = jnp.pad(in_t(w_in).astype(MXU_DTYPE), ((0, IN_SHARD_PAD - IN_SHARD), (0, 0)))
    w_in_all, conv_all = gather_weights([shards["w_in"]], gdn_conv_w[0])
    late = big_names[1:]
    send_sems, recv_sems, late_src, late_zones, token = gather_weights_start([shards[n] for n in late], conv_all)
    own = lambda g, s: lax.dynamic_update_slice(g, s[None], (chip,) + (0,) * s.ndim)
    full = {"w_in": own(w_in_all, shards["w_in"])}
    conv_full = own(conv_all, gdn_conv_w[0]).transpose(1, 0, 2).reshape(CONV_WIDTH, 3 * GDN_WIDTH)
    rows = lambda g: g.reshape(N_CHIPS * g.shape[1], g.shape[2])
    w_in_t = full["w_in"][:, :IN_SHARD].reshape(IN_DIM, D_MODEL)

    def late_weights(after):
        zones = gather_weights_wait(send_sems, recv_sems, late_src, late_zones, after)
        got = {n: own(z, shards[n]) for n, z in zip(late, zones)}
        return dict(w_out=rows(got["w_out"]), w_cq=rows(got["w_cq"]), w_ckv=rows(got["w_ckv"]), w_co=got["w_co"],
                    w_mlp1=got["w_mlp1"], w_mlp2=rows(got["w_mlp2"]))

    def chip_partials(names, by_chip):
        landed = swap_grad_halves(by_chip, name="swap_grad_halves_" + names[0])
        return [add_grad_halves(g, l, core, name="add_halves_" + n) for n, g, l in zip(names, by_chip, landed)]

    in_flight = []

    def grads_ready(ready):
        names = list(ready)
        *started, tok = scatter_grads_start(chip_partials(names, [ready[n] for n in names]),
                                            name="scatter_grads_start_%d" % len(in_flight))
        in_flight.append((names, *started))
        return tok

    w = dict(wa_t=align_w_in_t(w_in_t), conv_w=conv_full, late=late_weights, grads_ready=grads_ready)
    sp = {n: wts[n] for n, _ in SMALL_DIMS}
    sp["norm_mix_g"] = sp["norm_mix_g"] + token[0, 0]

    loss_part, grad_x, g_big, g_small = local_step(x.reshape(T, D), mem.reshape(-1, D), loss_target.reshape(T, D), w, sp, B=B)

    small_pieces = [g_small[n] for n, _ in SMALL_DIMS] + [g_small["gdn_conv_w"], loss_part]
    small_sizes = [d for _, d in SMALL_DIMS] + [CONV_WIDTH * 3 * GDN_WIDTH, LANES]
    red_small = _unpack_rows(all_reduce_small(_pack_rows(small_pieces, SMALL_ROWS)), small_sizes)
    grads = {n: p.reshape(1, d) for (n, d), p in zip(SMALL_DIMS, red_small)}
    conv_grad = lax.dynamic_slice(red_small[-2].reshape(CONV_WIDTH, 3 * GDN_WIDTH), (0, chip * CONV_SHARD[1]), CONV_SHARD)
    grads["gdn_conv_w"] = conv_grad.reshape((1,) + CONV_SHARD)
    loss = red_small[-1][0]

    tok_in = grads_ready({"w_in": g_big["w_in"]})
    parts, zones = {}, {}

    def wait_group(k, after):
        names, send_sems, recv_sems, thru, land = in_flight[k]
        thru, land = scatter_grads_wait(send_sems, recv_sems, thru, land, after, name="scatter_grads_wait_%d" % k)
        parts.update(zip(names, thru))
        zones.update(zip(names, land))

    wait_group(0, tok_in)
    wait_group(1, tok_in)
    order = jnp.stack([chip, chip ^ 2, chip ^ 1, chip ^ 3]).astype(jnp.int32)
    mine = [sum_grads(parts[n], zones[n], order, name="sum_chips_" + n) for n in late]
    theirs = swap_reduced_halves(mine, name="swap_reduced_halves")

    delta, new_m, new_v = {}, {}, {}
    for n, a, b in zip(late, mine, theirs):
        g, d, nm, nv = adamw_halves(wts[n][0], a, b, mom[n][0], var[n][0], core, name="adamw_" + n)
        grads[n], delta[n], new_m[n], new_v[n] = g[None], d[None], nm[None], nv[None]
    wait_group(2, new_v[late[-1]])
    mine_in = sum_grads(parts["w_in"], zones["w_in"], order, name="sum_chips_w_in")
    (theirs_in,) = swap_reduced_halves([mine_in], name="swap_reduced_halves_w_in")
    south = core[0] == 0
    g_in_t = jnp.concatenate([jnp.where(south, mine_in, theirs_in), jnp.where(south, theirs_in, mine_in)])[:IN_SHARD]
    back = lambda t: jnp.swapaxes(t, 0, 1)[None]
    d, nm, nv = adamw(in_t(w_in), g_in_t, in_t(m_w_in), in_t(v_w_in), name="adamw_w_in", tc=256)
    grads["w_in"], delta["w_in"], new_m["w_in"], new_v["w_in"] = back(g_in_t), back(d), back(nm), back(nv)
    small_names = [n for n, _ in SMALL_DIMS] + ["gdn_conv_w"]
    small_sz = [d for _, d in SMALL_DIMS] + [CONV_SHARD[0] * CONV_SHARD[1]]
    packed4 = [_pack_rows([src[n] for n in small_names], SMALL_ADAM_ROWS) for src in (wts, grads, mom, var)]
    outs = adamw(*packed4, name="adamw_small", tr=SMALL_ADAM_ROWS)
    for dst, buf in zip((delta, new_m, new_v), outs):
        for n, p in zip(small_names, _unpack_rows(buf, small_sz)):
            dst[n] = p.reshape(wts[n].shape)

    return (loss, grad_x.reshape(B, S, D), *[grads[n] for n in WEIGHT_ORDER], *[delta[n] for n in WEIGHT_ORDER],
            *[new_m[n] for n in WEIGHT_ORDER], *[new_v[n] for n in WEIGHT_ORDER])
```

```python
import functools

import jax
import jax.numpy as jnp
import numpy as np
from jax import lax
from jax.experimental import pallas as pl
from jax.experimental.pallas import tpu as pltpu

f32 = jnp.float32
bf16 = jnp.bfloat16
MXU_DTYPE = jnp.bfloat16
WIRE_DTYPE = jnp.bfloat16
INV_PRECISION = lax.Precision.HIGH

D_MODEL = 1024
FOX_HEADS = 8
FOX_HEAD_DIM = 64
FOX_WIDTH = 512
GDN_HEADS = 4
GDN_HEAD_DIM = 128
GDN_WIDTH = 512
CONV_WIDTH = 4
GDN_CHUNK = 64
XATTN_HEADS = 4
XATTN_HEAD_DIM = 128
XATTN_WIDTH = 512
D_FF = 4096
IN_DIM = 3600
EPS = 1e-6
NEG_INF = -1e30
LANES = 128
ADAM_LR = 0.001
ADAM_B1 = 0.9
ADAM_B2 = 0.999
ADAM_EPS = 1e-08
ADAM_WD = 0.01
ADAM_STEP = 10
VMEM_LIMIT = 48 * 1024 * 1024

COL_FOX = 0
COL_GDN = 1536
COL_Z = 3072
COL_SMALL = 3584
IN_ALIGNED = 3840
IN_TILE = 768
SM_F = 0
SM_B = 8
SM_A = 12


def _cparams(*sem):
    return pltpu.CompilerParams(dimension_semantics=sem, vmem_limit_bytes=VMEM_LIMIT)


def _mx(v):
    return v.astype(MXU_DTYPE)


def _dot(a, b, dims, precision=None):
    return lax.dot_general(a, b, (dims, ((), ())), preferred_element_type=f32, precision=precision)


def _dotm(a, b, dims):
    return _dot(_mx(a), _mx(b), dims)


NN = ((1,), (0,))
NT = ((1,), (1,))
TN = ((0,), (0,))


def matmul(a, b, *, name, ta=False, tb=False, b_stacked=False, out_stacked=False, residual=None, relu2_out=False,
           relu2_bwd_aux=None, out_dtype=f32, tm=1024, tn=1024, tk=1024):
    M, K = (a.shape[1], a.shape[0]) if ta else a.shape
    if b_stacked:
        b_cols = b.shape[2]
        N, tk = (b.shape[1], min(tk, b_cols)) if tb else (N_CHIPS * b_cols, tk)
        tn = tn if tb else min(tn, b_cols)
        assert K == (N_CHIPS * b_cols if tb else b.shape[1]), (name, a.shape, b.shape)
    else:
        N = b.shape[0] if tb else b.shape[1]
    if out_stacked:
        tn = min(tn, N // N_CHIPS)
    tm, tn, tk = min(tm, M), min(tn, N), min(tk, K)
    assert M % tm == 0 and N % tn == 0 and K % tk == 0, (name, M, N, K)
    nk = K // tk
    has_res = residual is not None
    has_aux = relu2_bwd_aux is not None

    def body(*refs):
        a_ref, b_ref = refs[0], refs[1]
        pos = 2
        res_ref = aux_ref = None
        if has_res:
            res_ref = refs[pos]
            pos += 1
        if has_aux:
            aux_ref = refs[pos]
            pos += 1
        o_ref = refs[pos]
        k = pl.program_id(2)
        dims = ((0,) if ta else (1,), (1,) if tb else (0,))
        part = _dot(_mx(a_ref[...]), _mx(b_ref[...]), dims)

        def finish(r):
            if has_res:
                r = r + res_ref[...]
            if has_aux:
                r = r * (2.0 * jnp.sqrt(aux_ref[...].astype(f32)))
            if relu2_out:
                o_ref[...] = jnp.square(jnp.maximum(r, 0.0)).astype(o_ref.dtype)
            else:
                o_ref[...] = r.astype(o_ref.dtype)

        if nk == 1:
            finish(part)
            return
        acc_ref = refs[pos + 1]

        @pl.when(k == 0)
        def _():
            acc_ref[...] = part

        @pl.when((k > 0) & (k < nk - 1))
        def _():
            acc_ref[...] += part

        @pl.when(k == nk - 1)
        def _():
            finish(acc_ref[...] + part)

    a_spec = pl.BlockSpec((tk, tm), lambda i, j, k: (k, i)) if ta else pl.BlockSpec((tm, tk), lambda i, j, k: (i, k))
    if b_stacked and tb:
        per = b_cols // tk
        b_spec = pl.BlockSpec((None, tn, tk), lambda i, j, k: (k // per, j, k % per))
    elif b_stacked:
        per = b_cols // tn
        b_spec = pl.BlockSpec((None, tk, tn), lambda i, j, k: (j // per, k, j % per))
    else:
        b_spec = pl.BlockSpec((tn, tk), lambda i, j, k: (j, k)) if tb else pl.BlockSpec((tk, tn), lambda i, j, k: (k, j))
    if out_stacked:
        assert not (has_res or has_aux or relu2_out), name
        per_o = N // N_CHIPS // tn
        o_spec = pl.BlockSpec((None, tm, tn), lambda i, j, k: (j // per_o, i, j % per_o))
        out_full = (N_CHIPS, M, N // N_CHIPS)
    else:
        o_spec = pl.BlockSpec((tm, tn), lambda i, j, k: (i, j))
        out_full = (M, N)
    in_specs, args = [a_spec, b_spec], [a, b]
    if has_res:
        in_specs.append(o_spec)
        args.append(residual)
    if has_aux:
        in_specs.append(o_spec)
        args.append(relu2_bwd_aux)
    out_shape = [jax.ShapeDtypeStruct(out_full, out_dtype)]
    out_specs = [o_spec]
    res = pl.pallas_call(
        body, name=name, grid=(M // tm, N // tn, nk), in_specs=in_specs, out_specs=out_specs, out_shape=out_shape,
        scratch_shapes=[pltpu.VMEM((tm, tn), f32)] if nk > 1 else [],
        compiler_params=_cparams("parallel", "parallel", "arbitrary"),
    )(*args)
    return res[0]


def matmul_rows(a, b, extras, *, name, mode, tb=False, b_stacked=False, tm=1024, tk=1024):
    M, K = a.shape
    N = D_MODEL
    if b_stacked:
        assert tb, name
        tk = min(tk, b.shape[2])
        per = b.shape[2] // tk
        b_spec = pl.BlockSpec((None, N, tk), lambda i, k: (k // per, 0, k % per))
    elif tb:
        tk = min(tk, K)
        b_spec = pl.BlockSpec((N, tk), lambda i, k: (0, k))
    else:
        tk = min(tk, K)
        b_spec = pl.BlockSpec((tk, N), lambda i, k: (k, 0))
    tm = min(tm, M)
    assert M % tm == 0 and K % tk == 0, (name, M, K)
    nk = K // tk
    extras = [e for e in extras if e is not None]
    n_ex = len(extras)

    def body(*refs):
        a_ref, b_ref = refs[0], refs[1]
        ex = refs[2:2 + n_ex]
        o_ref, s_ref = refs[2 + n_ex], refs[3 + n_ex]
        i, k = pl.program_id(0), pl.program_id(1)
        part = _dot(_mx(a_ref[...]), _mx(b_ref[...]), ((1,), (1,) if tb else (0,)))

        def finish(y):
            @pl.when(i == 0)
            def _():
                s_ref[...] = jnp.zeros_like(s_ref)

            if mode == "rms_bwd":
                xv, gv = ex[0][...], ex[1][...]
                rstd = lax.rsqrt(jnp.mean(xv * xv, axis=-1, keepdims=True) + EPS)
                xhat = xv * rstd
                gd = y * gv
                dx = rstd * (gd - xhat * jnp.mean(gd * xhat, axis=-1, keepdims=True))
                o_ref[...] = dx + ex[2][...] if n_ex == 3 else dx
                s_ref[...] += jnp.sum(y * xhat, axis=0, keepdims=True)
            else:
                e = y + ex[0][...] - ex[1][...]
                o_ref[...] = e * (1.0 / N)
                tot = 0.5 * jnp.sum(jnp.mean(e * e, axis=-1, keepdims=True), axis=0, keepdims=True)
                s_ref[...] += jnp.broadcast_to(tot, s_ref.shape)

        if nk == 1:
            finish(part)
            return
        acc_ref = refs[4 + n_ex]

        @pl.when(k == 0)
        def _():
            acc_ref[...] = part

        @pl.when((k > 0) & (k < nk - 1))
        def _():
            acc_ref[...] += part

        @pl.when(k == nk - 1)
        def _():
            finish(acc_ref[...] + part)

    row = pl.BlockSpec((tm, N), lambda i, k: (i, 0))
    vec = pl.BlockSpec((1, N), lambda i, k: (0, 0))
    if mode == "rms_bwd":
        ex_specs = [row, vec] + ([row] if n_ex == 3 else [])
        s_shape, s_spec = jax.ShapeDtypeStruct((1, N), f32), vec
    else:
        ex_specs = [row, row]
        s_shape, s_spec = jax.ShapeDtypeStruct((1, LANES), f32), pl.BlockSpec((1, LANES), lambda i, k: (0, 0))
    return pl.pallas_call(
        body, name=name, grid=(M // tm, nk),
        in_specs=[pl.BlockSpec((tm, tk), lambda i, k: (i, k)), b_spec] + ex_specs,
        out_specs=[row, s_spec], out_shape=[jax.ShapeDtypeStruct((M, N), f32), s_shape],
        scratch_shapes=[pltpu.VMEM((tm, N), f32)] if nk > 1 else [],
        compiler_params=_cparams("arbitrary", "arbitrary"),
    )(a, b, *extras)


def rms_fwd(x, g, *, name, tr=512):
    R, D = x.shape
    tr = min(tr, R)

    def body(x_ref, g_ref, o_ref):
        xv = x_ref[...]
        y = xv * lax.rsqrt(jnp.mean(xv * xv, axis=-1, keepdims=True) + EPS)
        o_ref[...] = (y * g_ref[...]).astype(o_ref.dtype)

    return pl.pallas_call(
        body, name=name, grid=(R // tr,),
        in_specs=[pl.BlockSpec((tr, D), lambda i: (i, 0)), pl.BlockSpec((1, D), lambda i: (0, 0))],
        out_specs=pl.BlockSpec((tr, D), lambda i: (i, 0)),
        out_shape=jax.ShapeDtypeStruct((R, D), MXU_DTYPE),
        compiler_params=_cparams("parallel"),
    )(x, g)


def rms_bwd(x, g, dh, residual, *, name, tr=512):
    R, D = x.shape
    tr = min(tr, R)
    has_res = residual is not None

    def body(*refs):
        if has_res:
            x_ref, g_ref, dh_ref, res_ref, dx_ref, dg_ref = refs
        else:
            x_ref, g_ref, dh_ref, dx_ref, dg_ref = refs
        xv = x_ref[...]
        rstd = lax.rsqrt(jnp.mean(xv * xv, axis=-1, keepdims=True) + EPS)
        xhat = xv * rstd
        dh = dh_ref[...].astype(f32)
        gd = dh * g_ref[...]
        dx = rstd * (gd - xhat * jnp.mean(gd * xhat, axis=-1, keepdims=True))
        if has_res:
            dx = dx + res_ref[...]
        dx_ref[...] = dx

        @pl.when(pl.program_id(0) == 0)
        def _():
            dg_ref[...] = jnp.zeros_like(dg_ref)

        dg_ref[...] += jnp.sum(dh * xhat, axis=0, keepdims=True)

    row = pl.BlockSpec((tr, D), lambda i: (i, 0))
    vec = pl.BlockSpec((1, D), lambda i: (0, 0))
    in_specs = [row, vec, row] + ([row] if has_res else [])
    args = [x, g, dh] + ([residual] if has_res else [])
    return pl.pallas_call(
        body, name=name, grid=(R // tr,), in_specs=in_specs, out_specs=[row, vec],
        out_shape=[jax.ShapeDtypeStruct((R, D), f32), jax.ShapeDtypeStruct((1, D), f32)],
        compiler_params=_cparams("arbitrary"),
    )(*args)


def loss_head(y, target, *, tr=512):
    R, D = y.shape
    tr = min(tr, R)

    def body(y_ref, t_ref, dy_ref, loss_ref):
        e = y_ref[...] - t_ref[...]
        dy_ref[...] = e * (1.0 / D)

        @pl.when(pl.program_id(0) == 0)
        def _():
            loss_ref[...] = jnp.zeros_like(loss_ref)

        part = 0.5 * jnp.sum(jnp.mean(e * e, axis=-1, keepdims=True), axis=0, keepdims=True)
        loss_ref[...] += jnp.broadcast_to(part, loss_ref.shape)

    row = pl.BlockSpec((tr, D), lambda i: (i, 0))
    return pl.pallas_call(
        body, name="loss_head", grid=(R // tr,), in_specs=[row, row],
        out_specs=[row, pl.BlockSpec((1, LANES), lambda i: (0, 0))],
        out_shape=[jax.ShapeDtypeStruct((R, D), f32), jax.ShapeDtypeStruct((1, LANES), f32)],
        compiler_params=_cparams("arbitrary"),
    )(y, target)


def _head_rms(v, g):
    r = lax.rsqrt(jnp.mean(v * v, axis=-1, keepdims=True) + EPS)
    return v * r * g, r


def _head_rms_bwd(v, r, g, dn):
    vhat = v * r
    gd = dn * g
    dv = r * (gd - vhat * jnp.mean(gd * vhat, axis=-1, keepdims=True))
    return dv, jnp.sum(dn * vhat, axis=0, keepdims=True)


def _softmax_rows(s):
    m = jnp.max(s, axis=-1, keepdims=True)
    e = jnp.exp(s - m)
    return e / jnp.sum(e, axis=-1, keepdims=True)


def xattn_fwd(cq, ckv, gq, gk, *, B, tq=512):
    T = cq.shape[0]
    S = T // B
    M = ckv.shape[0] // B
    tq = min(tq, S)
    nq = S // tq
    scale = XATTN_HEAD_DIM ** -0.5

    def body(q_ref, k_ref, v_ref, gq_ref, gk_ref, o_ref):
        qn, _ = _head_rms(q_ref[...], gq_ref[...])
        kn, _ = _head_rms(k_ref[...], gk_ref[...])
        p = _softmax_rows(_dot(_mx(qn), _mx(kn), NT) * scale)
        o_ref[...] = _dot(_mx(p), _mx(v_ref[...]), NN).astype(o_ref.dtype)

    hd = XATTN_HEAD_DIM
    vec = pl.BlockSpec((1, hd), lambda b, h, i: (0, 0))
    return pl.pallas_call(
        body, name="xattn_fwd", grid=(B, XATTN_HEADS, nq),
        in_specs=[pl.BlockSpec((tq, hd), lambda b, h, i: (b * nq + i, h)),
                  pl.BlockSpec((M, hd), lambda b, h, i: (b, h)),
                  pl.BlockSpec((M, hd), lambda b, h, i: (b, XATTN_HEADS + h)), vec, vec],
        out_specs=pl.BlockSpec((tq, hd), lambda b, h, i: (b * nq + i, h)),
        out_shape=jax.ShapeDtypeStruct((T, XATTN_WIDTH), MXU_DTYPE),
        compiler_params=_cparams("parallel", "parallel", "parallel"),
    )(cq, ckv, ckv, gq, gk)


def xattn_bwd(cq, ckv, gq, gk, dco, *, B, tq=512):
    T = cq.shape[0]
    S = T // B
    M = ckv.shape[0] // B
    tq = min(tq, S)
    nq = S // tq
    scale = XATTN_HEAD_DIM ** -0.5
    hd = XATTN_HEAD_DIM

    def body(q_ref, k_ref, v_ref, gq_ref, gk_ref, do_ref, dq_ref, dk_ref, dv_ref, dgq_ref, dgk_ref, dkn_acc, dv_acc):
        b, h, i = pl.program_id(0), pl.program_id(1), pl.program_id(2)

        @pl.when((b == 0) & (h == 0) & (i == 0))
        def _():
            dgq_ref[...] = jnp.zeros_like(dgq_ref)
            dgk_ref[...] = jnp.zeros_like(dgk_ref)

        @pl.when(i == 0)
        def _():
            dkn_acc[...] = jnp.zeros_like(dkn_acc)
            dv_acc[...] = jnp.zeros_like(dv_acc)

        q, k, v = q_ref[...], k_ref[...], v_ref[...]
        gqv, gkv = gq_ref[...], gk_ref[...]
        qn, rq = _head_rms(q, gqv)
        kn, rk = _head_rms(k, gkv)
        p = _softmax_rows(_dot(_mx(qn), _mx(kn), NT) * scale)
        do = do_ref[...]
        dv_acc[...] += _dot(_mx(p), _mx(do), TN)
        dp = _dot(_mx(do), _mx(v), NT)
        ds = p * (dp - jnp.sum(dp * p, axis=-1, keepdims=True)) * scale
        dqn = _dot(_mx(ds), _mx(kn), NN)
        dkn_acc[...] += _dot(_mx(ds), _mx(qn), TN)
        dq, dgq = _head_rms_bwd(q, rq, gqv, dqn)
        dq_ref[...] = dq.astype(dq_ref.dtype)
        dgq_ref[...] += dgq

        @pl.when(i == nq - 1)
        def _():
            dk, dgk = _head_rms_bwd(k, rk, gkv, dkn_acc[...])
            dk_ref[...] = dk.astype(dk_ref.dtype)
            dv_ref[...] = dv_acc[...].astype(dv_ref.dtype)
            dgk_ref[...] += dgk

    vec = pl.BlockSpec((1, hd), lambda b, h, i: (0, 0))
    qspec = pl.BlockSpec((tq, hd), lambda b, h, i: (b * nq + i, h))
    kspec = pl.BlockSpec((M, hd), lambda b, h, i: (b, h))
    vspec = pl.BlockSpec((M, hd), lambda b, h, i: (b, XATTN_HEADS + h))
    dq, dk, dv, dgq, dgk = pl.pallas_call(
        body, name="xattn_bwd", grid=(B, XATTN_HEADS, nq),
        in_specs=[qspec, kspec, vspec, vec, vec, qspec],
        out_specs=[qspec, kspec, kspec, vec, vec],
        out_shape=[jax.ShapeDtypeStruct((T, XATTN_WIDTH), MXU_DTYPE),
                   jax.ShapeDtypeStruct((B * M, XATTN_WIDTH), MXU_DTYPE),
                   jax.ShapeDtypeStruct((B * M, XATTN_WIDTH), MXU_DTYPE),
                   jax.ShapeDtypeStruct((1, hd), f32), jax.ShapeDtypeStruct((1, hd), f32)],
        scratch_shapes=[pltpu.VMEM((M, hd), f32), pltpu.VMEM((M, hd), f32)],
        compiler_params=_cparams("arbitrary", "arbitrary", "arbitrary"),
    )(cq, ckv, ckv, gq, gk, dco)
    return dq, jnp.concatenate([dk, dv], axis=1), dgq, dgk


FOX_PAIRS = FOX_HEADS // 2


def _fox_scores(qn, kn, ccol, crow, q0, tq, S, scale):
    s = _dot(_mx(qn), _mx(kn), NT) * scale + ccol - crow
    qpos = q0 + lax.broadcasted_iota(jnp.int32, (tq, S), 0)
    kpos = lax.broadcasted_iota(jnp.int32, (tq, S), 1)
    return jnp.where(kpos <= qpos, s, NEG_INF)


def fox_fwd(P, ccol, crow, gq, gk, go, *, B, tq=256):
    T = P.shape[0]
    S = T // B
    tq = min(tq, S)
    nq = S // tq
    hd = FOX_HEAD_DIM
    scale = hd ** -0.5

    def body(q_ref, k_ref, v_ref, ccol_ref, crow_ref, gq_ref, gk_ref, go_ref, o_ref, oa_ref):
        q0 = pl.program_id(2) * tq
        for e in range(2):
            sl = slice(e * hd, (e + 1) * hd)
            qn, _ = _head_rms(q_ref[:, sl], gq_ref[:, sl])
            kn, _ = _head_rms(k_ref[:, sl], gk_ref[:, sl])
            p = _softmax_rows(_fox_scores(qn, kn, ccol_ref[0, e], crow_ref[0, e], q0, tq, S, scale))
            o = _dot(_mx(p), _mx(v_ref[:, sl]), NN)
            o_ref[:, sl] = o
            oa_ref[:, sl] = _head_rms(o, go_ref[:, sl])[0].astype(oa_ref.dtype)

    W = 2 * hd
    vec = pl.BlockSpec((1, W), lambda b, h, i: (0, 0))
    ospec = pl.BlockSpec((tq, W), lambda b, h, i: (b * nq + i, h))
    return pl.pallas_call(
        body, name="fox_fwd", grid=(B, FOX_PAIRS, nq),
        in_specs=[pl.BlockSpec((tq, W), lambda b, h, i: (b * nq + i, h)),
                  pl.BlockSpec((S, W), lambda b, h, i: (b, FOX_PAIRS + h)),
                  pl.BlockSpec((S, W), lambda b, h, i: (b, 2 * FOX_PAIRS + h)),
                  pl.BlockSpec((1, 2, tq, 1), lambda b, h, i: (b, h, i, 0)),
                  pl.BlockSpec((1, 2, 1, S), lambda b, h, i: (b, h, 0, 0)), vec, vec, vec],
        out_specs=[ospec, ospec],
        out_shape=[jax.ShapeDtypeStruct((T, FOX_WIDTH), f32), jax.ShapeDtypeStruct((T, FOX_WIDTH), MXU_DTYPE)],
        compiler_params=_cparams("parallel", "parallel", "parallel"),
    )(P, P, P, ccol, crow, gq, gk, go)


def fox_bwd(P, ccol, crow, gq, gk, go, o_raw, d_oab, *, B, tq=256):
    T = P.shape[0]
    S = T // B
    tq = min(tq, S)
    nq = S // tq
    hd = FOX_HEAD_DIM
    scale = hd ** -0.5

    def body(q_ref, k_ref, v_ref, ccol_ref, crow_ref, gq_ref, gk_ref, go_ref, o_ref, doa_ref,
             dq_ref, dk_ref, dv_ref, dccol_ref, dcrow_ref, dgq_ref, dgk_ref, dgo_ref, dkn_acc, dv_acc, dcrow_acc):
        b, h, i = pl.program_id(0), pl.program_id(1), pl.program_id(2)
        q0 = i * tq

        @pl.when((b == 0) & (h == 0) & (i == 0))
        def _():
            dgq_ref[...] = jnp.zeros_like(dgq_ref)
            dgk_ref[...] = jnp.zeros_like(dgk_ref)
            dgo_ref[...] = jnp.zeros_like(dgo_ref)

        @pl.when(i == 0)
        def _():
            dkn_acc[...] = jnp.zeros_like(dkn_acc)
            dv_acc[...] = jnp.zeros_like(dv_acc)
            dcrow_acc[...] = jnp.zeros_like(dcrow_acc)

        for e in range(2):
            sl = slice(e * hd, (e + 1) * hd)
            q, k, v = q_ref[:, sl], k_ref[:, sl], v_ref[:, sl]
            gqv, gkv, gov = gq_ref[:, sl], gk_ref[:, sl], go_ref[:, sl]
            qn, rq = _head_rms(q, gqv)
            kn, rk = _head_rms(k, gkv)
            p = _softmax_rows(_fox_scores(qn, kn, ccol_ref[0, e], crow_ref[0, e], q0, tq, S, scale))
            o = o_ref[:, sl]
            ro = lax.rsqrt(jnp.mean(o * o, axis=-1, keepdims=True) + EPS)
            do, dgo = _head_rms_bwd(o, ro, gov, doa_ref[:, sl])
            dgo_ref[:, sl] += dgo
            dv_acc[e] += _dot(_mx(p), _mx(do), TN)
            dp = _dot(_mx(do), _mx(v), NT)
            ds = p * (dp - jnp.sum(do * o, axis=-1, keepdims=True))
            dccol_ref[0, e] = jnp.sum(ds, axis=1, keepdims=True)
            dcrow_acc[e] -= jnp.sum(ds, axis=0, keepdims=True)
            dqn = _dot(_mx(ds), _mx(kn), NN) * scale
            dkn_acc[e] += _dot(_mx(ds), _mx(qn), TN) * scale
            dq, dgq = _head_rms_bwd(q, rq, gqv, dqn)
            dq_ref[:, sl] = dq.astype(dq_ref.dtype)
            dgq_ref[:, sl] += dgq

        @pl.when(i == nq - 1)
        def _():
            for e in range(2):
                sl = slice(e * hd, (e + 1) * hd)
                k = k_ref[:, sl]
                gkv = gk_ref[:, sl]
                rk = lax.rsqrt(jnp.mean(k * k, axis=-1, keepdims=True) + EPS)
                dk, dgk = _head_rms_bwd(k, rk, gkv, dkn_acc[e])
                dk_ref[:, sl] = dk.astype(dk_ref.dtype)
                dv_ref[:, sl] = dv_acc[e].astype(dv_ref.dtype)
                dgk_ref[:, sl] += dgk
                dcrow_ref[0, e] = dcrow_acc[e]

    W = 2 * hd
    vec = pl.BlockSpec((1, W), lambda b, h, i: (0, 0))
    qspec = pl.BlockSpec((tq, W), lambda b, h, i: (b * nq + i, h))
    kvout = pl.BlockSpec((S, W), lambda b, h, i: (b, h))
    colspec = pl.BlockSpec((1, 2, tq, 1), lambda b, h, i: (b, h, i, 0))
    rowspec = pl.BlockSpec((1, 2, 1, S), lambda b, h, i: (b, h, 0, 0))
    return pl.pallas_call(
        body, name="fox_bwd", grid=(B, FOX_PAIRS, nq),
        in_specs=[qspec,
                  pl.BlockSpec((S, W), lambda b, h, i: (b, FOX_PAIRS + h)),
                  pl.BlockSpec((S, W), lambda b, h, i: (b, 2 * FOX_PAIRS + h)),
                  colspec, rowspec, vec, vec, vec, qspec, qspec],
        out_specs=[qspec, kvout, kvout, colspec, rowspec, vec, vec, vec],
        out_shape=[jax.ShapeDtypeStruct((T, FOX_WIDTH), MXU_DTYPE), jax.ShapeDtypeStruct((T, FOX_WIDTH), MXU_DTYPE),
                   jax.ShapeDtypeStruct((T, FOX_WIDTH), MXU_DTYPE),
                   jax.ShapeDtypeStruct((B, FOX_HEADS, S, 1), f32), jax.ShapeDtypeStruct((B, FOX_HEADS, 1, S), f32),
                   jax.ShapeDtypeStruct((1, W), f32), jax.ShapeDtypeStruct((1, W), f32), jax.ShapeDtypeStruct((1, W), f32)],
        scratch_shapes=[pltpu.VMEM((2, S, hd), f32), pltpu.VMEM((2, S, hd), f32), pltpu.VMEM((2, 1, S), f32)],
        compiler_params=_cparams("arbitrary", "arbitrary", "arbitrary"),
    )(P, P, P, ccol, crow, gq, gk, go, o_raw, d_oab)


FOX_TQ = 512
FOX_TK = 512
GROUP_PRECISION = lax.Precision.HIGH


def _head_mean(v):
    n = v.shape[1]
    r = lax.broadcasted_iota(jnp.int32, (n, n), 0) // FOX_HEAD_DIM
    c = lax.broadcasted_iota(jnp.int32, (n, n), 1) // FOX_HEAD_DIM
    ones = (r == c).astype(bf16)
    hi = v.astype(bf16)
    lo = (v - hi.astype(f32)).astype(bf16)
    return (_dot(hi, ones, NN) + _dot(lo, ones, NN)) * (1.0 / FOX_HEAD_DIM)


def fox_prep_fwd(P, gq, gk, *, tr=512):
    T = P.shape[0]
    tr = min(tr, T)
    scale = FOX_HEAD_DIM ** -0.5

    def body(q_ref, k_ref, v_ref, gq_ref, gk_ref, qn_ref, kn_ref, vb_ref):
        q, k = q_ref[...], k_ref[...]
        qn_ref[...] = (q * lax.rsqrt(_head_mean(q * q) + EPS) * (gq_ref[...] * scale)).astype(qn_ref.dtype)
        kn_ref[...] = (k * lax.rsqrt(_head_mean(k * k) + EPS) * gk_ref[...]).astype(kn_ref.dtype)
        vb_ref[...] = v_ref[...].astype(vb_ref.dtype)

    W = FOX_WIDTH
    col = lambda j: pl.BlockSpec((tr, W), lambda i: (i, j))
    vec = pl.BlockSpec((1, W), lambda i: (0, 0))
    out = jax.ShapeDtypeStruct((T, W), MXU_DTYPE)
    return pl.pallas_call(
        body, name="fox_prep_fwd", grid=(T // tr,), in_specs=[col(0), col(1), col(2), vec, vec],
        out_specs=[col(0)] * 3, out_shape=[out] * 3, compiler_params=_cparams("parallel"),
    )(P, P, P, gq, gk)


def fox_prep_bwd(P, gq, gk, dqn, dkn, *, tr=512):
    T = P.shape[0]
    tr = min(tr, T)
    scale = FOX_HEAD_DIM ** -0.5

    def body(q_ref, k_ref, gq_ref, gk_ref, dqn_ref, dkn_ref, dq_ref, dk_ref, dgq_ref, dgk_ref):
        @pl.when(pl.program_id(0) == 0)
        def _():
            dgq_ref[...] = jnp.zeros_like(dgq_ref)
            dgk_ref[...] = jnp.zeros_like(dgk_ref)

        def one(x, g, dn, dx_ref, dg_ref):
            r = lax.rsqrt(_head_mean(x * x) + EPS)
            xhat = x * r
            gd = dn * g
            dx_ref[...] = (r * (gd - xhat * _head_mean(gd * xhat))).astype(dx_ref.dtype)
            return jnp.sum(dn * xhat, axis=0, keepdims=True)

        dgq_ref[...] += scale * one(q_ref[...], gq_ref[...] * scale, dqn_ref[...], dq_ref, dgq_ref)
        dgk_ref[...] += one(k_ref[...], gk_ref[...], dkn_ref[...], dk_ref, dgk_ref)

    W = FOX_WIDTH
    col = lambda j: pl.BlockSpec((tr, W), lambda i: (i, j))
    vec = pl.BlockSpec((1, W), lambda i: (0, 0))
    return pl.pallas_call(
        body, name="fox_prep_bwd", grid=(T // tr,), in_specs=[col(0), col(1), vec, vec, col(0), col(0)],
        out_specs=[col(0), col(0), vec, vec],
        out_shape=[jax.ShapeDtypeStruct((T, W), MXU_DTYPE), jax.ShapeDtypeStruct((T, W), MXU_DTYPE),
                   jax.ShapeDtypeStruct((1, W), f32), jax.ShapeDtypeStruct((1, W), f32)],
        compiler_params=_cparams("arbitrary"),
    )(P, P, gq, gk, dqn, dkn)


def _fox_tile_scores(q, k_ref, ccol_ref, cq, e, j, sl, mask_off):
    tq, tk = FOX_TQ, FOX_TK
    rows = pl.ds(pl.multiple_of(j * tk, tk), tk)
    k = k_ref[rows, sl]
    s = _dot(k, q, NT) + cq - ccol_ref[0, e, rows, :]
    if mask_off is not None:
        key = lax.broadcasted_iota(jnp.int32, (tk, tq), 0) + mask_off
        query = lax.broadcasted_iota(jnp.int32, (tk, tq), 1)
        s = jnp.where(key <= query, s, NEG_INF)
    return s, k, rows


def _fox_sweep(i, update, carry):
    nd = FOX_TQ // FOX_TK
    carry = lax.fori_loop(0, i * nd, lambda j, cr: update(cr, j, None), carry)
    for d in range(nd):
        carry = update(carry, i * nd + d, d * FOX_TK)
    return carry


def fox_core_fwd(qn, kn, vb, ccol, crow, go, *, B):
    T = qn.shape[0]
    S = T // B
    tq = FOX_TQ
    nq = S // tq
    hd = FOX_HEAD_DIM

    def body(q_ref, k_ref, v_ref, ccol_ref, crow_ref, go_ref, o_ref, oa_ref, lse_ref):
        i = pl.program_id(2)
        for e in range(2):
            sl = slice(e * hd, (e + 1) * hd)
            q = q_ref[:, sl]
            cq = crow_ref[0, e, i]

            def update(carry, j, mask_off):
                m, l, acc = carry
                s, _, rows = _fox_tile_scores(q, k_ref, ccol_ref, cq, e, j, sl, mask_off)
                m2 = jnp.maximum(m, jnp.max(s, axis=0, keepdims=True))
                a = jnp.exp(m - m2)
                p = jnp.exp(s - m2)
                return m2, a * l + jnp.sum(p, axis=0, keepdims=True), a * acc + _dot(v_ref[rows, sl], _mx(p), TN)

            carry = (jnp.full((1, tq), NEG_INF, f32), jnp.zeros((1, tq), f32), jnp.zeros((hd, tq), f32))
            m, l, acc = _fox_sweep(i, update, carry)
            o = (acc / l).T
            o_ref[:, sl] = o
            oa_ref[:, sl] = _head_rms(o, go_ref[:, sl])[0].astype(oa_ref.dtype)
            lse_ref[0, e, 0] = m + jnp.log(l)

    W = 2 * hd
    qspec = pl.BlockSpec((tq, W), lambda b, h, i: (b * nq + i, h))
    kspec = pl.BlockSpec((S, W), lambda b, h, i: (b, h))
    return pl.pallas_call(
        body, name="fox_core_fwd", grid=(B, FOX_PAIRS, nq),
        in_specs=[qspec, kspec, kspec, pl.BlockSpec((1, 2, S, 1), lambda b, h, i: (b, h, 0, 0)),
                  pl.BlockSpec((1, 2, nq, 1, tq), lambda b, h, i: (b, h, 0, 0, 0)),
                  pl.BlockSpec((1, W), lambda b, h, i: (0, 0))],
        out_specs=[qspec, qspec, pl.BlockSpec((1, 2, 1, 1, tq), lambda b, h, i: (b, h, i, 0, 0))],
        out_shape=[jax.ShapeDtypeStruct((T, FOX_WIDTH), f32), jax.ShapeDtypeStruct((T, FOX_WIDTH), MXU_DTYPE),
                   jax.ShapeDtypeStruct((B, FOX_HEADS, nq, 1, tq), f32)],
        compiler_params=_cparams("parallel", "parallel", "parallel"),
    )(qn, kn, vb, ccol, crow, go)


def fox_core_bwd(qn, kn, vb, ccol, crow, go, o_raw, lse, d_oab, *, B):
    T = qn.shape[0]
    S = T // B
    tq = FOX_TQ
    nq = S // tq
    hd = FOX_HEAD_DIM

    def body(q_ref, k_ref, v_ref, ccol_ref, crow_ref, go_ref, o_ref, lse_ref, doa_ref,
             dq_ref, dk_ref, dv_ref, dccol_ref, dcrow_ref, dgo_ref, dk_acc, dv_acc, dck_acc):
        b, h, i = pl.program_id(0), pl.program_id(1), pl.program_id(2)

        @pl.when((b == 0) & (h == 0) & (i == 0))
        def _():
            dgo_ref[...] = jnp.zeros_like(dgo_ref)

        @pl.when(i == 0)
        def _():
            dk_acc[...] = jnp.zeros_like(dk_acc)
            dv_acc[...] = jnp.zeros_like(dv_acc)
            dck_acc[...] = jnp.zeros_like(dck_acc)

        for e in range(2):
            sl = slice(e * hd, (e + 1) * hd)
            q = q_ref[:, sl]
            cq = crow_ref[0, e, i]
            lse_e = lse_ref[0, e, 0]
            o = o_ref[:, sl]
            ro = lax.rsqrt(jnp.mean(o * o, axis=-1, keepdims=True) + EPS)
            do, dgo = _head_rms_bwd(o, ro, go_ref[:, sl], doa_ref[:, sl])
            dgo_ref[:, sl] += dgo
            delta = jnp.sum((do * o).T, axis=0, keepdims=True)
            do_b = _mx(do)

            def update(carry, j, mask_off):
                dq, dcq = carry
                s, k, rows = _fox_tile_scores(q, k_ref, ccol_ref, cq, e, j, sl, mask_off)
                p = jnp.exp(s - lse_e)
                dv_acc[e, rows, :] += _dot(_mx(p), do_b, NN)
                ds = p * (_dot(v_ref[rows, sl], do_b, NT) - delta)
                dck_acc[e, rows, :] -= jnp.sum(ds, axis=1, keepdims=True)
                ds_b = _mx(ds)
                dk_acc[e, rows, :] += _dot(ds_b, q, NN)
                return dq + _dot(ds_b, k, TN), dcq + jnp.sum(ds, axis=0, keepdims=True)

            dq, dcq = _fox_sweep(i, update, (jnp.zeros((tq, hd), f32), jnp.zeros((1, tq), f32)))
            dq_ref[:, sl] = dq
            dcrow_ref[0, e, 0] = dcq

        @pl.when(i == nq - 1)
        def _():
            for e in range(2):
                sl = slice(e * hd, (e + 1) * hd)
                dk_ref[:, sl] = dk_acc[e]
                dv_ref[:, sl] = dv_acc[e].astype(dv_ref.dtype)
            dccol_ref[0] = dck_acc[...]

    W = 2 * hd
    qspec = pl.BlockSpec((tq, W), lambda b, h, i: (b * nq + i, h))
    kspec = pl.BlockSpec((S, W), lambda b, h, i: (b, h))
    colspec = pl.BlockSpec((1, 2, S, 1), lambda b, h, i: (b, h, 0, 0))
    rowspec = pl.BlockSpec((1, 2, nq, 1, tq), lambda b, h, i: (b, h, 0, 0, 0))
    tilespec = pl.BlockSpec((1, 2, 1, 1, tq), lambda b, h, i: (b, h, i, 0, 0))
    vec = pl.BlockSpec((1, W), lambda b, h, i: (0, 0))
    return pl.pallas_call(
        body, name="fox_core_bwd", grid=(B, FOX_PAIRS, nq),
        in_specs=[qspec, kspec, kspec, colspec, rowspec, vec, qspec, tilespec, qspec],
        out_specs=[qspec, kspec, kspec, colspec, tilespec, vec],
        out_shape=[jax.ShapeDtypeStruct((T, FOX_WIDTH), f32), jax.ShapeDtypeStruct((T, FOX_WIDTH), f32),
                   jax.ShapeDtypeStruct((T, FOX_WIDTH), MXU_DTYPE),
                   jax.ShapeDtypeStruct((B, FOX_HEADS, S, 1), f32), jax.ShapeDtypeStruct((B, FOX_HEADS, nq, 1, tq), f32),
                   jax.ShapeDtypeStruct((1, W), f32)],
        scratch_shapes=[pltpu.VMEM((2, S, hd), f32), pltpu.VMEM((2, S, hd), f32), pltpu.VMEM((2, S, 1), f32)],
        compiler_params=_cparams("arbitrary", "arbitrary", "arbitrary"),
    )(qn, kn, vb, ccol, crow, go, o_raw, lse, d_oab)


def _lane_mask(lo, hi, shape):
    lane = lax.broadcasted_iota(jnp.int32, shape, 1)
    return (lane >= lo) & (lane < hi)


def _cumsum_rows(v, period, reverse=False):
    n = v.shape[0]
    pos = lax.broadcasted_iota(jnp.int32, v.shape, 0) % period
    sh = 1
    while sh < period:
        if reverse:
            v = v + jnp.where(pos + sh < period, pltpu.roll(v, n - sh, 0), 0.0)
        else:
            v = v + jnp.where(pos >= sh, pltpu.roll(v, sh, 0), 0.0)
        sh *= 2
    return v


def _gate_values(z, bias, alog):
    zb = z + bias
    ls = jax.nn.log_sigmoid(zb)
    beta = jax.nn.sigmoid(z)
    g = -jnp.exp(alog) * jax.nn.softplus(zb)
    return zb, ls, beta, g


def gates_fwd(P, bias, alog, *, B):
    T = P.shape[0]
    S = T // B

    def body(z_ref, bias_ref, alog_ref, o_ref):
        z = z_ref[...]
        _, ls, beta, g = _gate_values(z, bias_ref[...], alog_ref[...])
        c = _cumsum_rows(ls, S)
        gc = _cumsum_rows(g, GDN_CHUNK)
        o = jnp.where(_lane_mask(SM_F, SM_F + FOX_HEADS, z.shape), c, 0.0)
        o = jnp.where(_lane_mask(SM_B, SM_B + GDN_HEADS, z.shape), beta, o)
        o = jnp.where(_lane_mask(SM_A, SM_A + GDN_HEADS, z.shape), gc, o)
        o_ref[...] = o

    vec = pl.BlockSpec((1, LANES), lambda b: (0, 0))
    return pl.pallas_call(
        body, name="gates_fwd", grid=(B,),
        in_specs=[pl.BlockSpec((S, LANES), lambda b: (b, COL_SMALL // LANES)), vec, vec],
        out_specs=pl.BlockSpec((S, LANES), lambda b: (b, 0)),
        out_shape=jax.ShapeDtypeStruct((T, LANES), f32),
        compiler_params=_cparams("parallel"),
    )(P, bias, alog)


def gates_bwd(P, bias, alog, dgates, *, B):
    T = P.shape[0]
    S = T // B

    def body(z_ref, bias_ref, alog_ref, dg_ref, dz_ref, par_ref):
        z = z_ref[...]
        zb, ls, beta, g = _gate_values(z, bias_ref[...], alog_ref[...])
        d = dg_ref[...]
        dls = _cumsum_rows(d, S, reverse=True)
        dgr = _cumsum_rows(d, GDN_CHUNK, reverse=True)
        sig = jax.nn.sigmoid(zb)
        dz_f = dls * (1.0 - sig)
        dz_b = d * beta * (1.0 - beta)
        dz_a = dgr * (-jnp.exp(alog_ref[...])) * sig
        dz = jnp.where(_lane_mask(SM_F, SM_F + FOX_HEADS, z.shape), dz_f, 0.0)
        dz = jnp.where(_lane_mask(SM_B, SM_B + GDN_HEADS, z.shape), dz_b, dz)
        dz = jnp.where(_lane_mask(SM_A, SM_A + GDN_HEADS, z.shape), dz_a, dz)
        dz_ref[...] = dz.astype(dz_ref.dtype)

        @pl.when(pl.program_id(0) == 0)
        def _():
            par_ref[...] = jnp.zeros_like(par_ref)

        dalog = jnp.where(_lane_mask(SM_A, SM_A + GDN_HEADS, z.shape), dgr * g, 0.0)
        par_ref[0:1, :] += jnp.sum(dz, axis=0, keepdims=True)
        par_ref[1:2, :] += jnp.sum(dalog, axis=0, keepdims=True)

    vec = pl.BlockSpec((1, LANES), lambda b: (0, 0))
    return pl.pallas_call(
        body, name="gates_bwd", grid=(B,),
        in_specs=[pl.BlockSpec((S, LANES), lambda b: (b, COL_SMALL // LANES)), vec, vec,
                  pl.BlockSpec((S, LANES), lambda b: (b, 0))],
        out_specs=[pl.BlockSpec((S, LANES), lambda b: (b, 0)), pl.BlockSpec((8, LANES), lambda b: (0, 0))],
        out_shape=[jax.ShapeDtypeStruct((T, LANES), MXU_DTYPE), jax.ShapeDtypeStruct((8, LANES), f32)],
        compiler_params=_cparams("arbitrary"),
    )(P, bias, alog, dgates)


GDN_BLOCKS = 3 * GDN_HEADS


def _shift_rows(v, d, reverse=False):
    if d == 0:
        return v
    n = v.shape[0]
    row = lax.broadcasted_iota(jnp.int32, v.shape, 0)
    if reverse:
        return jnp.where(row + d < n, pltpu.roll(v, n - d, 0), 0.0)
    return jnp.where(row >= d, pltpu.roll(v, d, 0), 0.0)


def _conv_silu(x, w):
    pre = sum(w[j:j + 1, :] * _shift_rows(x, CONV_WIDTH - 1 - j) for j in range(CONV_WIDTH))
    return pre, pre * jax.nn.sigmoid(pre)


def gdn_prep_fwd(P, conv_w, *, B):
    T = P.shape[0]
    S = T // B

    def body(x_ref, w_ref, o_ref):
        _, y = _conv_silu(x_ref[...], w_ref[...])
        yn = y * lax.rsqrt(jnp.sum(y * y, axis=-1, keepdims=True) + EPS)
        o_ref[...] = jnp.where(pl.program_id(1) < 2 * GDN_HEADS, yn, y)

    return pl.pallas_call(
        body, name="gdn_prep_fwd", grid=(B, GDN_BLOCKS),
        in_specs=[pl.BlockSpec((S, LANES), lambda b, j: (b, COL_GDN // LANES + j)),
                  pl.BlockSpec((CONV_WIDTH, LANES), lambda b, j: (0, j))],
        out_specs=pl.BlockSpec((S, LANES), lambda b, j: (b, j)),
        out_shape=jax.ShapeDtypeStruct((T, 3 * GDN_WIDTH), f32),
        compiler_params=_cparams("parallel", "parallel"),
    )(P, conv_w)


def gdn_prep_bwd(P, conv_w, dG, *, B):
    T = P.shape[0]
    S = T // B

    def body(x_ref, w_ref, dg_ref, dx_ref, dw_ref):
        x, w = x_ref[...], w_ref[...]
        pre, y = _conv_silu(x, w)
        dn = dg_ref[...]
        r = lax.rsqrt(jnp.sum(y * y, axis=-1, keepdims=True) + EPS)
        n = y * r
        dy_norm = r * (dn - n * jnp.sum(dn * n, axis=-1, keepdims=True))
        dy = jnp.where(pl.program_id(0) < 2 * GDN_HEADS, dy_norm, dn)
        sg = jax.nn.sigmoid(pre)
        dpre = dy * (sg * (1.0 + pre * (1.0 - sg)))
        dx = sum(w[j:j + 1, :] * _shift_rows(dpre, CONV_WIDTH - 1 - j, reverse=True) for j in range(CONV_WIDTH))
        dx_ref[...] = dx.astype(dx_ref.dtype)

        @pl.when(pl.program_id(1) == 0)
        def _():
            dw_ref[...] = jnp.zeros_like(dw_ref)

        for j in range(CONV_WIDTH):
            dw_ref[j:j + 1, :] += jnp.sum(dpre * _shift_rows(x, CONV_WIDTH - 1 - j), axis=0, keepdims=True)

    return pl.pallas_call(
        body, name="gdn_prep_bwd", grid=(GDN_BLOCKS, B),
        in_specs=[pl.BlockSpec((S, LANES), lambda j, b: (b, COL_GDN // LANES + j)),
                  pl.BlockSpec((CONV_WIDTH, LANES), lambda j, b: (0, j)),
                  pl.BlockSpec((S, LANES), lambda j, b: (b, j))],
        out_specs=[pl.BlockSpec((S, LANES), lambda j, b: (b, j)),
                   pl.BlockSpec((CONV_WIDTH, LANES), lambda j, b: (0, j))],
        out_shape=[jax.ShapeDtypeStruct((T, 3 * GDN_WIDTH), MXU_DTYPE),
                   jax.ShapeDtypeStruct((CONV_WIDTH, 3 * GDN_WIDTH), f32)],
        compiler_params=_cparams("arbitrary", "arbitrary"),
    )(P, conv_w, dG)


GDN_GROUP = 16
B_NN = (((2,), (1,)), ((0,), (0,)))
B_NT = (((2,), (2,)), ((0,), (0,)))
B_TN = (((1,), (1,)), ((0,), (0,)))


def _bmm(a, b, dims, precision=None):
    if precision is None:
        a, b = _mx(a), _mx(b)
    return lax.dot_general(a, b, dims, preferred_element_type=f32, precision=precision)


def _tri_inverse(A):
    C = A.shape[-1]
    row = lax.broadcasted_iota(jnp.int32, A.shape, 1)
    col = lax.broadcasted_iota(jnp.int32, A.shape, 2)
    eye = (row == col).astype(f32)
    X = jnp.where((row // 4) == (col // 4), -A, 0.0)
    X2 = _bmm(X, X, B_NN, INV_PRECISION)
    Tm = eye + X + X2 + _bmm(X, X2, B_NN, INV_PRECISION)
    b = 4
    while b < C:
        off = ((row // (2 * b)) == (col // (2 * b))) & ((row // b) != (col // b))
        Tm = Tm - _bmm(_bmm(Tm, jnp.where(off, A, 0.0), B_NN, INV_PRECISION), Tm, B_NN, INV_PRECISION)
        b *= 2
    return Tm


def _pick_lane(block, lane_idx):
    lane = lax.broadcasted_iota(jnp.int32, block.shape, 1)
    return jnp.sum(jnp.where(lane == lane_idx, block, 0.0), axis=1, keepdims=True)


def _gdn_local(q, k, v, beta, gc, Tm=None, uwm=None):
    C = GDN_CHUNK
    n = q.shape[0] // C
    q = q.reshape(n, C, -1) * (GDN_HEAD_DIM ** -0.5)
    k = k.reshape(n, C, -1)
    v = v.reshape(n, C, -1)
    beta = beta.reshape(n, C, 1)
    gc = gc.reshape(n, C, 1)
    row = lax.broadcasted_iota(jnp.int32, (n, C, C), 1)
    col = lax.broadcasted_iota(jnp.int32, (n, C, C), 2)
    gcT = jnp.swapaxes(jnp.broadcast_to(gc, (n, C, C)), 1, 2)
    D = jnp.exp(jnp.where(row >= col, gc - gcT, NEG_INF))
    kb = k * beta
    vb = v * beta
    A = jnp.where(row > col, _bmm(kb, k, B_NT) * D, 0.0)
    Gam = jnp.exp(gc)
    kg = kb * Gam
    gl = gc[:, C - 1:C, :]
    kdec = jnp.exp(gl - gc)
    loc = dict(q=q, k=k, v=v, beta=beta, gc=gc, D=D, kb=kb, vb=vb, A=A, Gam=Gam, kg=kg,
               kdec=kdec, kd=k * kdec, qg=q * Gam, gam=jnp.exp(gl), row=row, col=col)
    uwm = Tm is None if uwm is None else uwm
    Tm = _tri_inverse(A) if Tm is None else Tm.reshape(n, C, C)
    if uwm:
        loc.update(u=_bmm(Tm, vb, B_NN), w=_bmm(Tm, kg, B_NN), M=_bmm(q, k, B_NT) * D)
    loc["Tm"] = Tm
    return loc


def _gdn_store_local(loc, r0, u_s, w_s, qg_s, kd_s, M_s, gam_s, c0):
    n = loc["u"].shape[0]
    R = n * GDN_CHUNK
    u_s[pl.ds(r0, R), :] = loc["u"].reshape(R, -1)
    w_s[pl.ds(r0, R), :] = loc["w"].reshape(R, -1)
    qg_s[pl.ds(r0, R), :] = loc["qg"].reshape(R, -1)
    kd_s[pl.ds(r0, R), :] = loc["kd"].reshape(R, -1)
    M_s[pl.ds(r0, R), :] = loc["M"].reshape(R, -1)
    gam_s[pl.ds(c0, n)] = jnp.broadcast_to(loc["gam"], (n, 1, LANES))


def _gdn_specs(S):
    blk = lambda off: pl.BlockSpec((S, LANES), lambda b, h: (b, off + h))
    return blk


def gdn_fwd(G, gates, P, g_on, *, B):
    T = G.shape[0]
    S = T // B
    C = GDN_CHUNK
    N = S // C
    grp = min(GDN_GROUP, N)
    R = grp * C
    hd = GDN_HEAD_DIM

    def body(q_ref, k_ref, v_ref, gt_ref, z_ref, gon_ref, o_ref, ob_ref, st_ref, tm_ref, A_s, B_s, Q_s, O_s, gam_s):
        h = pl.program_id(1)

        def local(gi, carry):
            r0 = pl.multiple_of(gi * R, R)
            gt = gt_ref[pl.ds(r0, R), :]
            loc = _gdn_local(q_ref[pl.ds(r0, R), :], k_ref[pl.ds(r0, R), :], v_ref[pl.ds(r0, R), :],
                             _pick_lane(gt, SM_B + h), _pick_lane(gt, SM_A + h))
            chunks = pl.ds(gi * grp, grp)
            tm_ref[0, 0, pl.ds(r0, R), :] = loc["Tm"].reshape(R, C)
            A_s[chunks] = -_bmm(loc["kd"], loc["w"], B_TN)
            B_s[chunks] = _bmm(loc["kd"], loc["u"], B_TN)
            Q_s[pl.ds(r0, R), :] = (loc["qg"] - _bmm(loc["M"], loc["w"], B_NN)).reshape(R, hd)
            O_s[pl.ds(r0, R), :] = _bmm(loc["M"], loc["u"], B_NN).reshape(R, hd)
            gam_s[chunks] = jnp.broadcast_to(loc["gam"], (grp, 1, LANES))
            return carry

        lax.fori_loop(0, N // grp, local, 0)

        def step(n, state):
            st_ref[0, 0, n] = state
            return state * gam_s[n] + _dotm(A_s[n], state, NN) + B_s[n]

        lax.fori_loop(0, N, step, jnp.zeros((hd, hd), f32))

        def outputs(gi, carry):
            r0 = pl.multiple_of(gi * R, R)
            Q = Q_s[pl.ds(r0, R), :].reshape(grp, C, hd)
            o = _bmm(Q, st_ref[0, 0, pl.ds(gi * grp, grp)], B_NN).reshape(R, hd) + O_s[pl.ds(r0, R), :]
            o_ref[pl.ds(r0, R), :] = o
            return carry

        lax.fori_loop(0, N // grp, outputs, 0)
        o = o_ref[...]
        z = z_ref[...]
        ob_ref[...] = (_head_rms(o, gon_ref[...])[0] * (z * jax.nn.sigmoid(z))).astype(ob_ref.dtype)

    blk = lambda off: pl.BlockSpec((S, LANES), lambda b, h: (b, off + h))
    rows = lambda: pltpu.VMEM((S, hd), f32)
    return pl.pallas_call(
        body, name="gdn_fwd", grid=(B, GDN_HEADS),
        in_specs=[blk(0), blk(GDN_HEADS), blk(2 * GDN_HEADS), pl.BlockSpec((S, LANES), lambda b, h: (b, 0)),
                  blk(COL_Z // LANES), pl.BlockSpec((1, hd), lambda b, h: (0, 0))],
        out_specs=[blk(0), blk(0), pl.BlockSpec((1, 1, N, hd, hd), lambda b, h: (b, h, 0, 0, 0)),
                   pl.BlockSpec((1, 1, S, C), lambda b, h: (b, h, 0, 0))],
        out_shape=[jax.ShapeDtypeStruct((T, GDN_WIDTH), f32), jax.ShapeDtypeStruct((T, GDN_WIDTH), MXU_DTYPE),
                   jax.ShapeDtypeStruct((B, GDN_HEADS, N, hd, hd), f32), jax.ShapeDtypeStruct((B, GDN_HEADS, S, C), f32)],
        scratch_shapes=[pltpu.VMEM((N, hd, hd), f32), pltpu.VMEM((N, hd, hd), f32), rows(), rows(),
                        pltpu.VMEM((N, 1, LANES), f32)],
        compiler_params=_cparams("parallel", "parallel"),
    )(G, G, G, gates, P, g_on)


def gdn_bwd(G, gates, P, g_on, o_raw, states, tm, d_oab, *, B):
    T = G.shape[0]
    S = T // B
    C = GDN_CHUNK
    N = S // C
    grp = min(GDN_GROUP, N)
    R = grp * C
    hd = GDN_HEAD_DIM

    def body(q_ref, k_ref, v_ref, gt_ref, z_ref, gon_ref, o_ref, st_ref, tm_ref, dob_ref,
             dq_ref, dk_ref, dv_ref, dgt_ref, dz_ref, dgon_ref,
             u_s, w_s, M_s, gam_s, do_s, A_s, C_s, dst_s):
        b, h = pl.program_id(0), pl.program_id(1)

        @pl.when((b == 0) & (h == 0))
        def _():
            dgon_ref[...] = jnp.zeros_like(dgon_ref)

        @pl.when(h == 0)
        def _():
            dgt_ref[...] = jnp.zeros_like(dgt_ref)

        def group_inputs(gi, uwm):
            r0 = pl.multiple_of(gi * R, R)
            gt = gt_ref[pl.ds(r0, R), :]
            return r0, _gdn_local(q_ref[pl.ds(r0, R), :], k_ref[pl.ds(r0, R), :], v_ref[pl.ds(r0, R), :],
                                  _pick_lane(gt, SM_B + h), _pick_lane(gt, SM_A + h), tm_ref[0, 0, pl.ds(r0, R), :], uwm)

        def local(gi, carry):
            r0, loc = group_inputs(gi, True)
            rows, chunks = pl.ds(r0, R), pl.ds(gi * grp, grp)
            u_s[rows, :] = loc["u"].reshape(R, hd)
            w_s[rows, :] = loc["w"].reshape(R, hd)
            M_s[rows, :] = loc["M"].reshape(R, C)
            gam_s[chunks] = jnp.broadcast_to(loc["gam"], (grp, 1, LANES))
            o, z, gon = o_ref[rows, :], z_ref[rows, :], gon_ref[...]
            dob = dob_ref[rows, :]
            on, ro = _head_rms(o, gon)
            sz = jax.nn.sigmoid(z)
            dz_ref[rows, :] = (dob * on * (sz * (1.0 + z * (1.0 - sz)))).astype(dz_ref.dtype)
            do, dgon = _head_rms_bwd(o, ro, gon, dob * (z * sz))
            do_s[rows, :] = do
            dgon_ref[...] += dgon
            A_s[chunks] = -_bmm(loc["kd"], loc["w"], B_TN)
            C_s[chunks] = _bmm(loc["qg"] - _bmm(loc["M"], loc["w"], B_NN), do.reshape(grp, C, hd), B_TN)
            return carry

        lax.fori_loop(0, N // grp, local, 0)

        def step(t, dS):
            n = N - 1 - t
            dst_s[n] = dS
            return dS * gam_s[n] + _dotm(A_s[n], dS, TN) + C_s[n]

        lax.fori_loop(0, N, step, jnp.zeros((hd, hd), f32))

        def finish(gi, carry):
            r0, L = group_inputs(gi, False)
            n = grp
            rows, chunks = pl.ds(r0, R), pl.ds(gi * grp, grp)
            g3 = lambda ref: ref[rows, :].reshape(n, C, -1)
            u, w, do = g3(u_s), g3(w_s), g3(do_s)
            L["M"] = g3(M_s)
            state, dS = st_ref[0, 0, chunks], dst_s[chunks]
            v_new = u - _bmm(w, state, B_NN)
            du = _bmm(L["M"], do, B_TN) + _bmm(L["kd"], dS, B_NN)
            dw = -_bmm(du, state, B_NT)
            dqg = _bmm(do, state, B_NT)
            dM = _bmm(do, v_new, B_NT)
            dkd = _bmm(v_new, dS, B_NT)
            dgl_state = jnp.sum(jnp.sum(dS * state, axis=2, keepdims=True), axis=1, keepdims=True) * L["gam"]
            TmT = jnp.swapaxes(L["Tm"], 1, 2)
            dTm = _bmm(du, L["vb"], B_NT) + _bmm(dw, L["kg"], B_NT)
            dvb = _bmm(TmT, du, B_NN)
            dkg = _bmm(TmT, dw, B_NN)
            dA = jnp.where(L["row"] > L["col"], -_bmm(_bmm(TmT, dTm, B_NN), TmT, B_NN), 0.0)
            dKK = dA * L["D"]
            dQK = dM * L["D"]
            dkb = _bmm(dKK, L["k"], B_NN) + dkg * L["Gam"]
            dk = (_bmm(dKK, L["kb"], B_TN) + _bmm(dQK, L["q"], B_TN) + dkd * L["kdec"] + L["beta"] * dkb)
            dq = (_bmm(dQK, L["k"], B_NN) + dqg * L["Gam"]) * (GDN_HEAD_DIM ** -0.5)
            E = dA * L["A"] + dM * L["M"]
            r = jnp.sum(dkd * L["kd"], axis=-1, keepdims=True)
            dgc = (jnp.sum(E, axis=2, keepdims=True) - jnp.sum(jnp.swapaxes(E, 1, 2), axis=2, keepdims=True)
                   + jnp.sum(dkg * L["kg"], axis=-1, keepdims=True) + jnp.sum(dqg * L["qg"], axis=-1, keepdims=True) - r)
            dgl = jnp.sum(r, axis=1, keepdims=True) + dgl_state
            rowc = lax.broadcasted_iota(jnp.int32, (n, C, 1), 1)
            dgc = dgc + jnp.where(rowc == C - 1, dgl, 0.0)
            dbeta = jnp.sum(dkb * L["k"], axis=-1, keepdims=True) + jnp.sum(dvb * L["v"], axis=-1, keepdims=True)
            dq_ref[rows, :] = dq.reshape(R, hd)
            dk_ref[rows, :] = dk.reshape(R, hd)
            dv_ref[rows, :] = (L["beta"] * dvb).reshape(R, hd)
            lane = lax.broadcasted_iota(jnp.int32, (R, LANES), 1)
            dgt_ref[rows, :] += (jnp.where(lane == SM_B + h, dbeta.reshape(R, 1), 0.0)
                                 + jnp.where(lane == SM_A + h, dgc.reshape(R, 1), 0.0))
            return carry

        lax.fori_loop(0, N // grp, finish, 0)

    blk = lambda off: pl.BlockSpec((S, LANES), lambda b, h: (b, off + h))
    rows = lambda: pltpu.VMEM((S, hd), f32)
    return pl.pallas_call(
        body, name="gdn_bwd", grid=(B, GDN_HEADS),
        in_specs=[blk(0), blk(GDN_HEADS), blk(2 * GDN_HEADS), pl.BlockSpec((S, LANES), lambda b, h: (b, 0)),
                  blk(COL_Z // LANES), pl.BlockSpec((1, hd), lambda b, h: (0, 0)), blk(0),
                  pl.BlockSpec((1, 1, N, hd, hd), lambda b, h: (b, h, 0, 0, 0)),
                  pl.BlockSpec((1, 1, S, C), lambda b, h: (b, h, 0, 0)), blk(GDN_HEADS)],
        out_specs=[blk(0), blk(0), blk(0), pl.BlockSpec((S, LANES), lambda b, h: (b, 0)), blk(0),
                   pl.BlockSpec((1, hd), lambda b, h: (0, 0))],
        out_shape=[jax.ShapeDtypeStruct((T, GDN_WIDTH), f32), jax.ShapeDtypeStruct((T, GDN_WIDTH), f32),
                   jax.ShapeDtypeStruct((T, GDN_WIDTH), f32), jax.ShapeDtypeStruct((T, LANES), f32),
                   jax.ShapeDtypeStruct((T, GDN_WIDTH), MXU_DTYPE), jax.ShapeDtypeStruct((1, hd), f32)],
        scratch_shapes=[rows(), rows(), pltpu.VMEM((S, C), f32), pltpu.VMEM((N, 1, LANES), f32), rows(),
                        pltpu.VMEM((N, hd, hd), f32), pltpu.VMEM((N, hd, hd), f32), pltpu.VMEM((N, hd, hd), f32)],
        compiler_params=_cparams("arbitrary", "arbitrary"),
    )(G, G, G, gates, P, g_on, o_raw, states, tm, d_oab)


IN_SPLIT = (0, 1536, 1544, 3080, 3088, 3600)


IN_SHARD = IN_DIM // 4
IN_SHARD_PAD = 928


def align_w_in_t(wt):
    s = IN_SPLIT
    pad = jnp.zeros((IN_ALIGNED - IN_DIM, wt.shape[1]), wt.dtype)
    return jnp.concatenate([wt[s[0]:s[1]], wt[s[2]:s[3]], wt[s[4]:s[5]], wt[s[1]:s[2]], wt[s[3]:s[4]], pad], axis=0)


def unalign_w_in_t(wa):
    return jnp.concatenate([wa[0:1536], wa[COL_SMALL:COL_SMALL + 8], wa[1536:3072],
                            wa[COL_SMALL + 8:COL_SMALL + 16], wa[3072:3584]], axis=0)


def _lanes_vec(pieces):
    v = jnp.zeros((1, LANES), f32)
    for off, a in pieces:
        v = lax.dynamic_update_slice(v, a.astype(f32), (0, off))
    return v


def local_step(x, mem, target, w, sp, *, B):
    T = x.shape[0]
    S = T // B
    gq8, gk8 = jnp.tile(sp["fox_qnorm_g"], (1, FOX_HEADS)), jnp.tile(sp["fox_knorm_g"], (1, FOX_HEADS))
    go2 = jnp.tile(sp["fox_onorm_g"], (1, 2))
    bias = _lanes_vec([(SM_F, sp["fox_f_bias"]), (SM_A, sp["gdn_dt_bias"])])
    alog = _lanes_vec([(SM_A, sp["gdn_A_log"])])

    h1 = rms_fwd(x, sp["norm_mix_g"], name="rms_mix")
    P = matmul(h1, w["wa_t"], tb=True, name="mm_in", tn=IN_TILE)
    gates = gates_fwd(P, bias, alog, B=B)
    c = gates[:, SM_F:SM_F + FOX_HEADS].reshape(B, S, FOX_HEADS).transpose(0, 2, 1)
    ccol, crow = c[..., None], c.reshape(B, FOX_HEADS, S // FOX_TQ, 1, FOX_TQ)
    qn, kn, vb = fox_prep_fwd(P, gq8, gk8)
    o_raw, o_a, lse = fox_core_fwd(qn, kn, vb, ccol, crow, go2, B=B)
    G = gdn_prep_fwd(P, w["conv_w"], B=B)
    ob_raw, o_b, states, gdn_tm = gdn_fwd(G, gates, P, sp["gdn_onorm_g"], B=B)
    oab = jnp.concatenate([o_a, o_b], axis=1)
    if "late" in w:
        w = {**w, **w["late"](oab)}
    x2 = matmul(oab, w["w_out"], residual=x, name="mm_out")
    hq = rms_fwd(x2, sp["norm_xattn_g"], name="rms_xattn")
    hm = rms_fwd(mem, sp["mem_norm_g"], name="rms_mem")
    cq = matmul(hq, w["w_cq"], name="mm_cq")
    ckv = matmul(hm, w["w_ckv"], name="mm_ckv")
    co = xattn_fwd(cq, ckv, sp["xattn_qnorm_g"], sp["xattn_knorm_g"], B=B)
    x3 = matmul(co, w["w_co"], b_stacked=True, residual=x2, name="mm_co")
    hf = rms_fwd(x3, sp["norm_mlp_g"], name="rms_mlp")
    act = matmul(hf, w["w_mlp1"], b_stacked=True, relu2_out=True, out_dtype=MXU_DTYPE, name="mm_mlp1")
    dy, loss = matmul_rows(act, w["w_mlp2"], (x3, target), mode="loss", name="mm_mlp2_loss")

    da = matmul(dy, w["w_mlp2"], tb=True, relu2_bwd_aux=act, out_dtype=MXU_DTYPE, name="mm_d_act")
    g_mlp2 = matmul(act, dy, ta=True, out_dtype=WIRE_DTYPE, name="mm_g_mlp2")
    g_mlp1 = matmul(hf, da, ta=True, out_stacked=True, out_dtype=WIRE_DTYPE, name="mm_g_mlp1")
    by_rows = lambda g: g.reshape(N_CHIPS, g.shape[0] // N_CHIPS, g.shape[1])
    early = w.get("grads_ready", lambda grads: jnp.zeros((1, 1), f32))
    tok = early(dict(w_mlp1=g_mlp1, w_mlp2=by_rows(g_mlp2)))[0, 0]
    dx3, g_norm_mlp = matmul_rows(da, w["w_mlp1"], (x3, sp["norm_mlp_g"] + tok, dy), mode="rms_bwd", tb=True,
                                  b_stacked=True, name="mm_d_hf_rms")
    dco = matmul(dx3, w["w_co"], tb=True, b_stacked=True, name="mm_d_co")
    g_co = matmul(co, dx3, ta=True, out_stacked=True, out_dtype=WIRE_DTYPE, name="mm_g_co")
    dcq, dckv, g_xq, g_xk = xattn_bwd(cq, ckv, sp["xattn_qnorm_g"], sp["xattn_knorm_g"], dco, B=B)
    g_cq = matmul(hq, dcq, ta=True, out_dtype=WIRE_DTYPE, name="mm_g_cq")
    g_ckv = matmul(hm, dckv, ta=True, out_dtype=WIRE_DTYPE, name="mm_g_ckv")
    _, g_mem_norm = matmul_rows(dckv, w["w_ckv"], (mem, sp["mem_norm_g"], None), mode="rms_bwd", tb=True, name="mm_d_hm_rms")
    dx2, g_norm_xattn = matmul_rows(dcq, w["w_cq"], (x2, sp["norm_xattn_g"], dx3), mode="rms_bwd", tb=True, name="mm_d_hq_rms")
    doab = matmul(dx2, w["w_out"], tb=True, name="mm_d_oab")
    g_out = matmul(oab, dx2, ta=True, out_dtype=WIRE_DTYPE, name="mm_g_out")
    tok = early(dict(w_co=g_co, w_cq=by_rows(g_cq), w_ckv=by_rows(g_ckv), w_out=by_rows(g_out)))[0, 0]
    dqn, dkn, dv_f, dccol, dcrow, dgo2 = fox_core_bwd(qn, kn, vb, ccol, crow, go2 + tok, o_raw, lse, doab, B=B)
    dq_f, dk_f, dgq8, dgk8 = fox_prep_bwd(P, gq8, gk8, dqn, dkn)
    dGq, dGk, dGv, dgt, dz, g_gdn_on = gdn_bwd(G, gates, P, sp["gdn_onorm_g"], ob_raw, states, gdn_tm, doab, B=B)
    dPg, g_conv = gdn_prep_bwd(P, w["conv_w"], jnp.concatenate([dGq, dGk, dGv], axis=1), B=B)
    dc = (dccol[..., 0] + dcrow.reshape(B, FOX_HEADS, S)).transpose(0, 2, 1).reshape(T, FOX_HEADS)
    dgates = dgt + jnp.pad(dc, ((0, 0), (SM_F, LANES - SM_F - FOX_HEADS)))
    dsmall, par = gates_bwd(P, bias, alog, dgates, B=B)
    dP = jnp.concatenate([dq_f, dk_f, dv_f, dPg, dz, dsmall, jnp.zeros((T, IN_ALIGNED - COL_SMALL - LANES), MXU_DTYPE)], axis=1)
    g_wa = matmul(dP, h1, ta=True, out_dtype=WIRE_DTYPE, name="mm_g_in", tm=IN_TILE)
    dx, g_norm_mix = matmul_rows(dP, w["wa_t"], (x, sp["norm_mix_g"], dx2), mode="rms_bwd", tk=IN_TILE, name="mm_d_h1_rms")

    fold = lambda g: jnp.sum(g.reshape(-1, FOX_HEAD_DIM), axis=0, keepdims=True)
    g_in = jnp.pad(unalign_w_in_t(g_wa).reshape(N_CHIPS, IN_SHARD, D_MODEL), ((0, 0), (0, IN_SHARD_PAD - IN_SHARD), (0, 0)))
    big = dict(w_in=g_in, w_out=by_rows(g_out), w_cq=by_rows(g_cq), w_ckv=by_rows(g_ckv), w_co=g_co, w_mlp1=g_mlp1,
               w_mlp2=by_rows(g_mlp2))
    small = dict(norm_mix_g=g_norm_mix, fox_qnorm_g=fold(dgq8), fox_knorm_g=fold(dgk8),
                 fox_f_bias=par[0:1, SM_F:SM_F + FOX_HEADS], fox_onorm_g=fold(dgo2), gdn_conv_w=g_conv,
                 gdn_A_log=par[1:2, SM_A:SM_A + GDN_HEADS], gdn_dt_bias=par[0:1, SM_A:SM_A + GDN_HEADS],
                 gdn_onorm_g=g_gdn_on, norm_xattn_g=g_norm_xattn, mem_norm_g=g_mem_norm,
                 xattn_qnorm_g=g_xq, xattn_knorm_g=g_xk, norm_mlp_g=g_norm_mlp)
    return loss, dx, big, small


MESH_IDS = pl.DeviceIdType.MESH
N_CHIPS = 4
HBM_SPEC = pl.BlockSpec(memory_space=pltpu.HBM)
PACK_ROWS = 30720
PACK_HALF = PACK_ROWS // 2
PACK_BLOCK = 3072


def _place():
    return lax.axis_index("x"), lax.axis_index("y"), lax.axis_index("c")


def _other_chips(x, y):
    return [(1 - x, y), (x, 1 - y), (1 - x, 1 - y)]


def _remote(src, dst, send_sem, recv_sem, to):
    return pltpu.make_async_remote_copy(src_ref=src, dst_ref=dst, send_sem=send_sem, recv_sem=recv_sem,
                                        device_id=to, device_id_type=MESH_IDS)


def all_gather_shards(packed):
    half = PACK_HALF

    def body(src_ref, out_ref, send_sems, recv_sems):
        x, y, c = _place()
        me_chip = 2 * x + y
        sibling = (x, y, 1 - c)
        chips = _other_chips(x, y)

        def rows(chip, core):
            return out_ref.at[chip, pl.ds(core * half, half), :]

        sends = [_remote(src_ref.at[pl.ds(c * half, half), :], rows(me_chip, c), send_sems.at[j], recv_sems.at[j], (px, py, c))
                 for j, (px, py) in enumerate(chips)]
        for cp in sends:
            cp.start()
        passed = []
        for j, (px, py) in enumerate(chips):
            theirs = rows(2 * px + py, c)
            _remote(theirs, theirs, send_sems.at[j], recv_sems.at[j], (px, py, c)).wait_recv()
            cp = _remote(theirs, theirs, send_sems.at[3 + j], recv_sems.at[3 + j], sibling)
            cp.start()
            passed.append(cp)
        for j, (px, py) in enumerate(chips):
            theirs = rows(2 * px + py, 1 - c)
            _remote(theirs, theirs, send_sems.at[3 + j], recv_sems.at[3 + j], sibling).wait_recv()
        for cp in sends + passed:
            cp.wait_send()

    return pl.pallas_call(
        body, name="all_gather_shards", in_specs=[HBM_SPEC], out_specs=HBM_SPEC,
        out_shape=jax.ShapeDtypeStruct((N_CHIPS,) + packed.shape, packed.dtype),
        scratch_shapes=[pltpu.SemaphoreType.DMA((6,)), pltpu.SemaphoreType.DMA((6,))],
    )(packed)


def exchange_core_halves(G):
    half = PACK_HALF

    def body(g_ref, land_ref, send_sem, recv_sem):
        x, y, c = _place()
        cp = _remote(g_ref.at[:, pl.ds((1 - c) * half, half), :], land_ref, send_sem, recv_sem, (x, y, 1 - c))
        cp.start()
        cp.wait()

    return pl.pallas_call(
        body, name="exchange_core_halves", in_specs=[HBM_SPEC], out_specs=HBM_SPEC,
        out_shape=jax.ShapeDtypeStruct((N_CHIPS, half, LANES), G.dtype),
        scratch_shapes=[pltpu.SemaphoreType.DMA(()), pltpu.SemaphoreType.DMA(())],
    )(G)


def add_core_halves(G, land, core):
    nb = PACK_HALF // PACK_BLOCK

    def body(c_ref, g_ref, l_ref, o_ref):
        o_ref[...] = (g_ref[...].astype(f32) + l_ref[...].astype(f32)).astype(o_ref.dtype)

    blk = (1, PACK_BLOCK, LANES)
    return pl.pallas_call(
        body, name="add_core_halves",
        grid_spec=pltpu.PrefetchScalarGridSpec(
            num_scalar_prefetch=1, grid=(N_CHIPS, nb),
            in_specs=[pl.BlockSpec(blk, lambda k, i, c_ref: (k, c_ref[0] * nb + i, 0)),
                      pl.BlockSpec(blk, lambda k, i, c_ref: (k, i, 0))],
            out_specs=pl.BlockSpec(blk, lambda k, i, c_ref: (k, i, 0))),
        out_shape=jax.ShapeDtypeStruct(land.shape, land.dtype),
        compiler_params=_cparams("parallel", "parallel"),
    )(core, G, land)


def scatter_to_chips(part):
    def body(p_ref, land_ref, send_sems, recv_sems):
        x, y, c = _place()
        me_chip = 2 * x + y
        chips = _other_chips(x, y)
        sends = [_remote(p_ref.at[2 * px + py], land_ref.at[me_chip], send_sems.at[j], recv_sems.at[j], (px, py, c))
                 for j, (px, py) in enumerate(chips)]
        for cp in sends:
            cp.start()
        for j, (px, py) in enumerate(chips):
            slot = land_ref.at[2 * px + py]
            _remote(slot, slot, send_sems.at[j], recv_sems.at[j], (px, py, c)).wait_recv()
        for cp in sends:
            cp.wait_send()

    return pl.pallas_call(
        body, name="scatter_to_chips", in_specs=[HBM_SPEC], out_specs=HBM_SPEC,
        out_shape=jax.ShapeDtypeStruct(part.shape, part.dtype),
        scratch_shapes=[pltpu.SemaphoreType.DMA((3,)), pltpu.SemaphoreType.DMA((3,))],
    )(part)


def sum_chips(part, land, order):
    nb = PACK_HALF // PACK_BLOCK

    def body(order_ref, p_ref, l1_ref, l2_ref, l3_ref, o_ref):
        o_ref[...] = ((p_ref[0].astype(f32) + l1_ref[0].astype(f32)) + l2_ref[0].astype(f32)) + l3_ref[0].astype(f32)

    slot = lambda j: pl.BlockSpec((1, PACK_BLOCK, LANES), lambda i, order_ref: (order_ref[j], i, 0))
    return pl.pallas_call(
        body, name="sum_chips",
        grid_spec=pltpu.PrefetchScalarGridSpec(
            num_scalar_prefetch=1, grid=(nb,), in_specs=[slot(0), slot(1), slot(2), slot(3)],
            out_specs=pl.BlockSpec((PACK_BLOCK, LANES), lambda i, order_ref: (i, 0))),
        out_shape=jax.ShapeDtypeStruct((PACK_HALF, LANES), f32),
        compiler_params=_cparams("parallel"),
    )(order, part, land, land, land)


def swap_core_halves(red):
    def body(r_ref, out_ref, send_sem, recv_sem):
        x, y, c = _place()
        cp = _remote(r_ref, out_ref, send_sem, recv_sem, (x, y, 1 - c))
        cp.start()
        cp.wait()

    return pl.pallas_call(
        body, name="swap_core_halves", in_specs=[HBM_SPEC], out_specs=HBM_SPEC,
        out_shape=jax.ShapeDtypeStruct(red.shape, red.dtype),
        scratch_shapes=[pltpu.SemaphoreType.DMA(()), pltpu.SemaphoreType.DMA(())],
    )(red)


def _half(ref, core):
    rows = ref.shape[-2] // 2
    return ref.at[(slice(None),) * (len(ref.shape) - 2) + (pl.ds(core * rows, rows), slice(None))]


def gather_weights(shards, conv):
    n = len(shards)

    def body(*refs):
        src, conv_src = refs[:n], refs[n]
        out, conv_out = refs[n + 1:2 * n + 1], refs[2 * n + 1]
        send_sems, recv_sems = refs[2 * n + 2], refs[2 * n + 3]
        x, y, c = _place()
        me_chip = 2 * x + y
        sibling = (x, y, 1 - c)
        chips = _other_chips(x, y)
        sends = []
        for a in range(n):
            for j, (px, py) in enumerate(chips):
                sends.append(_remote(_half(src[a], c), _half(out[a].at[me_chip], c),
                                     send_sems.at[6 * a + j], recv_sems.at[6 * a + j], (px, py, c)))
        for j, (px, py) in enumerate(chips):
            sends.append(_remote(conv_src, conv_out.at[me_chip], send_sems.at[6 * n + j], recv_sems.at[6 * n + j], (px, py, c)))
        for cp in sends:
            cp.start()
        passed = []
        for a in range(n):
            for j, (px, py) in enumerate(chips):
                theirs = _half(out[a].at[2 * px + py], c)
                _remote(theirs, theirs, send_sems.at[6 * a + j], recv_sems.at[6 * a + j], (px, py, c)).wait_recv()
                cp = _remote(theirs, theirs, send_sems.at[6 * a + 3 + j], recv_sems.at[6 * a + 3 + j], sibling)
                cp.start()
                passed.append(cp)
        for j, (px, py) in enumerate(chips):
            theirs = conv_out.at[2 * px + py]
            _remote(theirs, theirs, send_sems.at[6 * n + j], recv_sems.at[6 * n + j], (px, py, c)).wait_recv()
        for a in range(n):
            for j, (px, py) in enumerate(chips):
                theirs = _half(out[a].at[2 * px + py], 1 - c)
                _remote(theirs, theirs, send_sems.at[6 * a + 3 + j], recv_sems.at[6 * a + 3 + j], sibling).wait_recv()
        for cp in sends + passed:
            cp.wait_send()

    return pl.pallas_call(
        body, name="gather_weights", in_specs=[HBM_SPEC] * (n + 1), out_specs=[HBM_SPEC] * (n + 1),
        out_shape=[jax.ShapeDtypeStruct((N_CHIPS,) + s.shape, s.dtype) for s in list(shards) + [conv]],
        scratch_shapes=[pltpu.SemaphoreType.DMA((6 * n + 3,)), pltpu.SemaphoreType.DMA((6 * n + 3,))],
    )(*shards, conv)


SEM_SPEC = pl.BlockSpec(memory_space=pltpu.SEMAPHORE)
SPLIT_EFFECT = pltpu.SideEffectType.DATAFLOW_SIDE_EFFECTING


def _gather_async_copies(src, land, send_sems, recv_sems, x, y, c):
    me_chip = 2 * x + y
    sends, arrivals = [], []
    for a in range(len(src)):
        for j, (px, py) in enumerate(_other_chips(x, y)):
            for core in range(2):
                sends.append(_remote(_half(src[a], c), _half(land[a].at[me_chip], c), send_sems.at[6 * a + 2 * j + core],
                                     recv_sems.at[6 * a + 2 * j + c], (px, py, core)))
                theirs = _half(land[a].at[2 * px + py], core)
                arrivals.append(_remote(theirs, theirs, send_sems.at[6 * a + 2 * j + core],
                                        recv_sems.at[6 * a + 2 * j + core], (px, py, core)))
    return sends, arrivals


def gather_weights_start(shards, after):
    n = len(shards)

    def body(*refs):
        src, land = refs[:n], refs[n:2 * n]
        send_sems, recv_sems, token = refs[2 * n + 1], refs[2 * n + 2], refs[4 * n + 3]
        x, y, c = _place()
        for cp in _gather_async_copies(src, land, send_sems, recv_sems, x, y, c)[0]:
            cp.start()
        token[...] = jnp.zeros_like(token)

    zones = [pltpu.with_memory_space_constraint(lax.empty((N_CHIPS,) + s.shape, s.dtype), pltpu.HBM) for s in shards]
    srcs = [pltpu.with_memory_space_constraint(s, pltpu.HBM) for s in shards]
    out = pl.pallas_call(
        body, name="gather_weights_start",
        out_shape=[pltpu.SemaphoreType.DMA((6 * n,)), pltpu.SemaphoreType.DMA((6 * n,))]
        + [pltpu.HBM(s.shape, s.dtype) for s in shards] + [pltpu.HBM(z.shape, z.dtype) for z in zones]
        + [jax.ShapeDtypeStruct((8, LANES), f32)],
        in_specs=[HBM_SPEC] * (2 * n) + [pl.BlockSpec(memory_space=pl.ANY)],
        out_specs=[SEM_SPEC, SEM_SPEC] + [HBM_SPEC] * (2 * n) + [pl.BlockSpec(memory_space=pltpu.VMEM)],
        input_output_aliases={i: 2 + i for i in range(2 * n)},
        compiler_params=pltpu.CompilerParams(has_side_effects=SPLIT_EFFECT),
    )(*srcs, *zones, after)
    return out[0], out[1], out[2:2 + n], out[2 + n:2 + 2 * n], out[-1]


def gather_weights_wait(send_sems, recv_sems, shards, zones, after):
    n = len(shards)

    def body(*refs):
        src, land = refs[:n], refs[n:2 * n]
        send_sems, recv_sems = refs[2 * n], refs[2 * n + 1]
        x, y, c = _place()
        sends, arrivals = _gather_async_copies(src, land, send_sems, recv_sems, x, y, c)
        for cp in sends:
            cp.wait_send()
        for cp in arrivals:
            cp.wait_recv()

    out = pl.pallas_call(
        body, name="gather_weights_wait",
        out_shape=[pltpu.HBM(s.shape, s.dtype) for s in shards] + [pltpu.HBM(z.shape, z.dtype) for z in zones],
        in_specs=[HBM_SPEC] * (2 * n) + [SEM_SPEC, SEM_SPEC, pl.BlockSpec(memory_space=pl.ANY)],
        out_specs=[HBM_SPEC] * (2 * n),
        input_output_aliases={i: i for i in range(2 * n)},
        compiler_params=pltpu.CompilerParams(has_side_effects=SPLIT_EFFECT),
    )(*shards, *zones, send_sems, recv_sems, after)
    return out[n:]


def swap_grad_halves(grads, *, name):
    n = len(grads)

    def body(*refs):
        g, land, send_sems, recv_sems = refs[:n], refs[n:2 * n], refs[2 * n], refs[2 * n + 1]
        x, y, c = _place()
        copies = [_remote(_half(g[a], 1 - c), land[a], send_sems.at[a], recv_sems.at[a], (x, y, 1 - c)) for a in range(n)]
        for cp in copies:
            cp.start()
        for cp in copies:
            cp.wait()

    return pl.pallas_call(
        body, name=name, in_specs=[HBM_SPEC] * n, out_specs=[HBM_SPEC] * n,
        out_shape=[jax.ShapeDtypeStruct((N_CHIPS, g.shape[1] // 2, g.shape[2]), g.dtype) for g in grads],
        scratch_shapes=[pltpu.SemaphoreType.DMA((n,)), pltpu.SemaphoreType.DMA((n,))],
    )(*grads)


GRAD_ROWS = 256


def add_grad_halves(g, land, core, *, name):
    _, half, cols = land.shape
    tr = GRAD_ROWS if half % GRAD_ROWS == 0 else half
    nb = half // tr

    def body(c_ref, g_ref, l_ref, o_ref):
        o_ref[...] = (g_ref[...].astype(f32) + l_ref[...].astype(f32)).astype(o_ref.dtype)

    blk = (1, tr, cols)
    return pl.pallas_call(
        body, name=name,
        grid_spec=pltpu.PrefetchScalarGridSpec(
            num_scalar_prefetch=1, grid=(N_CHIPS, nb),
            in_specs=[pl.BlockSpec(blk, lambda k, i, c_ref: (k, c_ref[0] * nb + i, 0)),
                      pl.BlockSpec(blk, lambda k, i, c_ref: (k, i, 0))],
            out_specs=pl.BlockSpec(blk, lambda k, i, c_ref: (k, i, 0))),
        out_shape=jax.ShapeDtypeStruct(land.shape, land.dtype),
        compiler_params=_cparams("parallel", "parallel"),
    )(core, g, land)


def scatter_grads(parts):
    n = len(parts)

    def body(*refs):
        p, land, send_sems, recv_sems = refs[:n], refs[n:2 * n], refs[2 * n], refs[2 * n + 1]
        x, y, c = _place()
        me_chip = 2 * x + y
        chips = _other_chips(x, y)
        sends = [_remote(p[a].at[2 * px + py], land[a].at[me_chip], send_sems.at[3 * a + j], recv_sems.at[3 * a + j], (px, py, c))
                 for a in range(n) for j, (px, py) in enumerate(chips)]
        for cp in sends:
            cp.start()
        for a in range(n):
            for j, (px, py) in enumerate(chips):
                slot = land[a].at[2 * px + py]
                _remote(slot, slot, send_sems.at[3 * a + j], recv_sems.at[3 * a + j], (px, py, c)).wait_recv()
        for cp in sends:
            cp.wait_send()

    return pl.pallas_call(
        body, name="scatter_grads", in_specs=[HBM_SPEC] * n, out_specs=[HBM_SPEC] * n,
        out_shape=[jax.ShapeDtypeStruct(p.shape, p.dtype) for p in parts],
        scratch_shapes=[pltpu.SemaphoreType.DMA((3 * n,)), pltpu.SemaphoreType.DMA((3 * n,))],
    )(*parts)


def _scatter_async_copies(parts, land, send_sems, recv_sems, x, y, c):
    me_chip = 2 * x + y
    sends, arrivals = [], []
    for a in range(len(parts)):
        for j, (px, py) in enumerate(_other_chips(x, y)):
            sems = (send_sems.at[3 * a + j], recv_sems.at[3 * a + j], (px, py, c))
            sends.append(_remote(parts[a].at[2 * px + py], land[a].at[me_chip], *sems))
            slot = land[a].at[2 * px + py]
            arrivals.append(_remote(slot, slot, *sems))
    return sends, arrivals


def scatter_grads_start(parts, *, name):
    n = len(parts)

    def body(*refs):
        p, land = refs[:n], refs[n:2 * n]
        send_sems, recv_sems, token = refs[2 * n], refs[2 * n + 1], refs[4 * n + 2]
        x, y, c = _place()
        for cp in _scatter_async_copies(p, land, send_sems, recv_sems, x, y, c)[0]:
            cp.start()
        token[...] = jnp.zeros_like(token)

    zones = [pltpu.with_memory_space_constraint(lax.empty(p.shape, p.dtype), pltpu.HBM) for p in parts]
    srcs = [pltpu.with_memory_space_constraint(p, pltpu.HBM) for p in parts]
    hbm = [pltpu.HBM(p.shape, p.dtype) for p in parts]
    out = pl.pallas_call(
        body, name=name,
        out_shape=[pltpu.SemaphoreType.DMA((3 * n,)), pltpu.SemaphoreType.DMA((3 * n,))] + hbm + hbm
        + [jax.ShapeDtypeStruct((8, LANES), f32)],
        in_specs=[HBM_SPEC] * (2 * n),
        out_specs=[SEM_SPEC, SEM_SPEC] + [HBM_SPEC] * (2 * n) + [pl.BlockSpec(memory_space=pltpu.VMEM)],
        input_output_aliases={i: 2 + i for i in range(2 * n)},
        compiler_params=pltpu.CompilerParams(has_side_effects=SPLIT_EFFECT),
    )(*srcs, *zones)
    return out[0], out[1], out[2:2 + n], out[2 + n:2 + 2 * n], out[-1]


def scatter_grads_wait(send_sems, recv_sems, parts, zones, after, *, name):
    n = len(parts)

    def body(*refs):
        p, land = refs[:n], refs[n:2 * n]
        x, y, c = _place()
        sends, arrivals = _scatter_async_copies(p, land, refs[2 * n], refs[2 * n + 1], x, y, c)
        for cp in sends:
            cp.wait_send()
        for cp in arrivals:
            cp.wait_recv()

    hbm = [pltpu.HBM(p.shape, p.dtype) for p in parts]
    out = pl.pallas_call(
        body, name=name, out_shape=hbm + hbm,
        in_specs=[HBM_SPEC] * (2 * n) + [SEM_SPEC, SEM_SPEC, pl.BlockSpec(memory_space=pl.ANY)],
        out_specs=[HBM_SPEC] * (2 * n),
        input_output_aliases={i: i for i in range(2 * n)},
        compiler_params=pltpu.CompilerParams(has_side_effects=SPLIT_EFFECT),
    )(*parts, *zones, send_sems, recv_sems, after)
    return out[:n], out[n:]


def sum_grads(part, land, order, *, name):
    _, half, cols = part.shape
    tr = GRAD_ROWS if half % GRAD_ROWS == 0 else half

    def body(order_ref, p_ref, l1_ref, l2_ref, l3_ref, o_ref):
        o_ref[...] = ((p_ref[0].astype(f32) + l1_ref[0].astype(f32)) + l2_ref[0].astype(f32)) + l3_ref[0].astype(f32)

    slot = lambda j: pl.BlockSpec((1, tr, cols), lambda i, order_ref: (order_ref[j], i, 0))
    return pl.pallas_call(
        body, name=name,
        grid_spec=pltpu.PrefetchScalarGridSpec(
            num_scalar_prefetch=1, grid=(half // tr,), in_specs=[slot(0), slot(1), slot(2), slot(3)],
            out_specs=pl.BlockSpec((tr, cols), lambda i, order_ref: (i, 0))),
        out_shape=jax.ShapeDtypeStruct((half, cols), f32),
        compiler_params=_cparams("parallel"),
    )(order, part, land, land, land)


def _peer(x, y, c, r):
    return ((1 - x) if r & 4 else x, (1 - y) if r & 2 else y, (1 - c) if r & 1 else c)


def _reduce_async_copies(grads, land, send_sems, recv_sems, x, y, c):
    me = 4 * x + 2 * y + c
    sends, arrivals = [], []
    for a in range(len(grads)):
        for r in range(1, N_DEV):
            px, py, pc = _peer(x, y, c, r)
            sems = (send_sems.at[7 * a + r - 1], recv_sems.at[7 * a + r - 1], (px, py, pc))
            sends.append(_remote(_half(grads[a].at[2 * px + py], pc), land[a].at[me], *sems))
            slot = land[a].at[4 * px + 2 * py + pc]
            arrivals.append(_remote(slot, slot, *sems))
    return sends, arrivals


def reduce_grads_start(grads, *, name):
    n = len(grads)

    def body(*refs):
        g, land = refs[:n], refs[n:2 * n]
        send_sems, recv_sems, token = refs[2 * n], refs[2 * n + 1], refs[4 * n + 2]
        x, y, c = _place()
        for cp in _reduce_async_copies(g, land, send_sems, recv_sems, x, y, c)[0]:
            cp.start()
        token[...] = jnp.zeros_like(token)

    zones = [pltpu.with_memory_space_constraint(lax.empty((N_DEV, g.shape[1] // 2, g.shape[2]), g.dtype), pltpu.HBM)
             for g in grads]
    srcs = [pltpu.with_memory_space_constraint(g, pltpu.HBM) for g in grads]
    out = pl.pallas_call(
        body, name=name,
        out_shape=[pltpu.SemaphoreType.DMA((7 * n,)), pltpu.SemaphoreType.DMA((7 * n,))]
        + [pltpu.HBM(g.shape, g.dtype) for g in grads] + [pltpu.HBM(z.shape, z.dtype) for z in zones]
        + [jax.ShapeDtypeStruct((8, LANES), f32)],
        in_specs=[HBM_SPEC] * (2 * n),
        out_specs=[SEM_SPEC, SEM_SPEC] + [HBM_SPEC] * (2 * n) + [pl.BlockSpec(memory_space=pltpu.VMEM)],
        input_output_aliases={i: 2 + i for i in range(2 * n)},
        compiler_params=pltpu.CompilerParams(has_side_effects=SPLIT_EFFECT),
    )(*srcs, *zones)
    return out[0], out[1], out[2:2 + n], out[2 + n:2 + 2 * n], out[-1]


def reduce_grads_wait(send_sems, recv_sems, grads, zones, after, *, name):
    n = len(grads)

    def body(*refs):
        g, land = refs[:n], refs[n:2 * n]
        x, y, c = _place()
        sends, arrivals = _reduce_async_copies(g, land, refs[2 * n], refs[2 * n + 1], x, y, c)
        for cp in sends:
            cp.wait_send()
        for cp in arrivals:
            cp.wait_recv()

    hbm = [pltpu.HBM(a.shape, a.dtype) for a in list(grads) + list(zones)]
    out = pl.pallas_call(
        body, name=name, out_shape=hbm,
        in_specs=[HBM_SPEC] * (2 * n) + [SEM_SPEC, SEM_SPEC, pl.BlockSpec(memory_space=pl.ANY)],
        out_specs=[HBM_SPEC] * (2 * n),
        input_output_aliases={i: i for i in range(2 * n)},
        compiler_params=pltpu.CompilerParams(has_side_effects=SPLIT_EFFECT),
    )(*grads, *zones, send_sems, recv_sems, after)
    return out[:n], out[n:]


def sum_partials(g, land, where, *, name):
    _, half, cols = land.shape
    tr = GRAD_ROWS if half % GRAD_ROWS == 0 else half
    nb = half // tr

    def body(where_ref, g_ref, *rest):
        o_ref = rest[-1]
        acc = g_ref[0].astype(f32)
        for l_ref in rest[:-1]:
            acc = acc + l_ref[0].astype(f32)
        o_ref[...] = acc

    blk = (1, tr, cols)
    slot = lambda j: pl.BlockSpec(blk, lambda i, where_ref: (where_ref[2 + j], i, 0))
    return pl.pallas_call(
        body, name=name,
        grid_spec=pltpu.PrefetchScalarGridSpec(
            num_scalar_prefetch=1, grid=(nb,),
            in_specs=[pl.BlockSpec(blk, lambda i, where_ref: (where_ref[0], where_ref[1] * nb + i, 0))]
            + [slot(j) for j in range(N_DEV - 1)],
            out_specs=pl.BlockSpec((tr, cols), lambda i, where_ref: (i, 0))),
        out_shape=jax.ShapeDtypeStruct((half, cols), f32),
        compiler_params=_cparams("parallel"),
    )(where, g, *([land] * (N_DEV - 1)))


def swap_reduced_halves(mine, *, name):
    n = len(mine)

    def body(*refs):
        r, out, send_sems, recv_sems = refs[:n], refs[n:2 * n], refs[2 * n], refs[2 * n + 1]
        x, y, c = _place()
        copies = [_remote(r[a], out[a], send_sems.at[a], recv_sems.at[a], (x, y, 1 - c)) for a in range(n)]
        for cp in copies:
            cp.start()
        for cp in copies:
            cp.wait()

    return pl.pallas_call(
        body, name=name, in_specs=[HBM_SPEC] * n, out_specs=[HBM_SPEC] * n,
        out_shape=[jax.ShapeDtypeStruct(r.shape, r.dtype) for r in mine],
        scratch_shapes=[pltpu.SemaphoreType.DMA((n,)), pltpu.SemaphoreType.DMA((n,))],
    )(*mine)


def adamw_halves(w, mine, theirs, m, v, core, *, name):
    R, C = w.shape
    tr = min(GRAD_ROWS, R // 2)
    half_nb = R // 2 // tr

    def body(c_ref, w_ref, a_ref, b_ref, m_ref, v_ref, g_ref, d_ref, nm_ref, nv_ref):
        low = pl.program_id(0) < half_nb
        gv = jnp.where(low == (c_ref[0] == 0), a_ref[...], b_ref[...])
        nm = ADAM_B1 * m_ref[...] + (1.0 - ADAM_B1) * gv
        nv = ADAM_B2 * v_ref[...] + (1.0 - ADAM_B2) * jnp.square(gv)
        m_hat = nm / (1.0 - ADAM_B1 ** ADAM_STEP)
        v_hat = nv / (1.0 - ADAM_B2 ** ADAM_STEP)
        g_ref[...] = gv
        d_ref[...] = -ADAM_LR * (m_hat / (jnp.sqrt(v_hat) + ADAM_EPS) + ADAM_WD * w_ref[...])
        nm_ref[...] = nm
        nv_ref[...] = nv

    full = pl.BlockSpec((tr, C), lambda i, c_ref: (i, 0))
    part = pl.BlockSpec((tr, C), lambda i, c_ref: (i % half_nb, 0))
    out = jax.ShapeDtypeStruct((R, C), f32)
    return pl.pallas_call(
        body, name=name,
        grid_spec=pltpu.PrefetchScalarGridSpec(
            num_scalar_prefetch=1, grid=(2 * half_nb,), in_specs=[full, part, part, full, full], out_specs=[full] * 4),
        out_shape=[out] * 4, compiler_params=_cparams("parallel"),
    )(core, w, mine, theirs, m, v)


N_DEV = 8


def all_reduce_small(v):
    def body(src_ref, out_ref, land_ref, send_sems, recv_sems):
        x, y, c = _place()
        me = 4 * x + 2 * y + c
        copies = []
        for r in range(1, N_DEV):
            peer = ((1 - x) if r & 4 else x, (1 - y) if r & 2 else y, (1 - c) if r & 1 else c)
            copies.append(_remote(src_ref, land_ref.at[r], send_sems.at[r - 1], recv_sems.at[r - 1], peer))
        for cp in copies:
            cp.start()
        land_ref[0] = src_ref[...]
        for cp in copies:
            cp.wait()
        acc = land_ref[me]
        for d in range(1, N_DEV):
            acc = acc + land_ref[jnp.bitwise_xor(me, d)]
        out_ref[...] = acc

    vm = pl.BlockSpec(memory_space=pltpu.VMEM)
    return pl.pallas_call(
        body, name="all_reduce_small", in_specs=[vm], out_specs=vm,
        out_shape=jax.ShapeDtypeStruct(v.shape, v.dtype),
        scratch_shapes=[pltpu.VMEM((N_DEV,) + v.shape, v.dtype),
                        pltpu.SemaphoreType.DMA((N_DEV - 1,)), pltpu.SemaphoreType.DMA((N_DEV - 1,))],
    )(v)


def adamw(w, g, m, v, *, name, tr=None, tc=None):
    R, C = w.shape
    if tc is None:
        tr, tc = min(tr, R), C
        blk = pl.BlockSpec((tr, C), lambda i: (i, 0))
    else:
        tr = R
        blk = pl.BlockSpec((R, tc), lambda i: (0, i))

    def body(w_ref, g_ref, m_ref, v_ref, d_ref, nm_ref, nv_ref):
        gv = g_ref[...]
        nm = ADAM_B1 * m_ref[...] + (1.0 - ADAM_B1) * gv
        nv = ADAM_B2 * v_ref[...] + (1.0 - ADAM_B2) * jnp.square(gv)
        m_hat = nm / (1.0 - ADAM_B1 ** ADAM_STEP)
        v_hat = nv / (1.0 - ADAM_B2 ** ADAM_STEP)
        d_ref[...] = -ADAM_LR * (m_hat / (jnp.sqrt(v_hat) + ADAM_EPS) + ADAM_WD * w_ref[...])
        nm_ref[...] = nm
        nv_ref[...] = nv

    out = jax.ShapeDtypeStruct((R, C), f32)
    return pl.pallas_call(
        body, name=name, grid=((R // tr) * (C // tc),), in_specs=[blk] * 4, out_specs=[blk] * 3, out_shape=[out] * 3,
        compiler_params=_cparams("parallel"),
    )(w, g, m, v)


BIG_SHARDS = (("w_in", (1024, 900), True), ("w_out", (256, 1024), False), ("w_cq", (256, 512), False),
              ("w_ckv", (256, 1024), False), ("w_co", (512, 256), True), ("w_mlp1", (1024, 1024), True),
              ("w_mlp2", (1024, 1024), False))
CONV_SHARD = (CONV_WIDTH, 3 * GDN_WIDTH // N_CHIPS)
SMALL_DIMS = (("norm_mix_g", 1024), ("fox_qnorm_g", 64), ("fox_knorm_g", 64), ("fox_f_bias", 8), ("fox_onorm_g", 64),
              ("gdn_A_log", 4), ("gdn_dt_bias", 4), ("gdn_onorm_g", 128), ("norm_xattn_g", 1024), ("mem_norm_g", 1024),
              ("xattn_qnorm_g", 128), ("xattn_knorm_g", 128), ("norm_mlp_g", 1024))
WEIGHT_ORDER = ("norm_mix_g", "w_in", "fox_qnorm_g", "fox_knorm_g", "fox_f_bias", "fox_onorm_g", "gdn_conv_w", "gdn_A_log",
                "gdn_dt_bias", "gdn_onorm_g", "w_out", "norm_xattn_g", "mem_norm_g", "w_cq", "w_ckv", "xattn_qnorm_g",
                "xattn_knorm_g", "w_co", "norm_mlp_g", "w_mlp1", "w_mlp2")


def _pack_rows(pieces, rows, lead=()):
    cat = jnp.concatenate([p.reshape(lead + (-1,)) for p in pieces], axis=-1)
    cat = jnp.pad(cat, [(0, 0)] * len(lead) + [(0, rows * LANES - cat.shape[-1])])
    return cat.reshape(lead + (rows, LANES))


def _unpack_rows(buf, sizes, lead=()):
    flat = buf.reshape(lead + (-1,))
    out, off = [], 0
    for n in sizes:
        out.append(flat[..., off:off + n])
        off += n
    return out


def _conv_to_wire(conv):
    return lax.bitcast_convert_type(conv, bf16)


def _conv_from_wire(wire):
    return lax.bitcast_convert_type(wire, f32)


SMALL_ROWS = 96
SMALL_ADAM_ROWS = 56


def kernel(x, mem, norm_mix_g, w_in, fox_qnorm_g, fox_knorm_g, fox_f_bias, fox_onorm_g, gdn_conv_w, gdn_A_log, gdn_dt_bias, gdn_onorm_g, w_out, norm_xattn_g, mem_norm_g, w_cq, w_ckv, xattn_qnorm_g, xattn_knorm_g, w_co, norm_mlp_g, w_mlp1, w_mlp2, loss_target, m_norm_mix_g, m_w_in, m_fox_qnorm_g, m_fox_knorm_g, m_fox_f_bias, m_fox_onorm_g, m_gdn_conv_w, m_gdn_A_log, m_gdn_dt_bias, m_gdn_onorm_g, m_w_out, m_norm_xattn_g, m_mem_norm_g, m_w_cq, m_w_ckv, m_xattn_qnorm_g, m_xattn_knorm_g, m_w_co, m_norm_mlp_g, m_w_mlp1, m_w_mlp2, v_norm_mix_g, v_w_in, v_fox_qnorm_g, v_fox_knorm_g, v_fox_f_bias, v_fox_onorm_g, v_gdn_conv_w, v_gdn_A_log, v_gdn_dt_bias, v_gdn_onorm_g, v_w_out, v_norm_xattn_g, v_mem_norm_g, v_w_cq, v_w_ckv, v_xattn_qnorm_g, v_xattn_knorm_g, v_w_co, v_norm_mlp_g, v_w_mlp1, v_w_mlp2):
    wts = dict(norm_mix_g=norm_mix_g, w_in=w_in, fox_qnorm_g=fox_qnorm_g, fox_knorm_g=fox_knorm_g, fox_f_bias=fox_f_bias,
               fox_onorm_g=fox_onorm_g, gdn_conv_w=gdn_conv_w, gdn_A_log=gdn_A_log, gdn_dt_bias=gdn_dt_bias,
               gdn_onorm_g=gdn_onorm_g, w_out=w_out, norm_xattn_g=norm_xattn_g, mem_norm_g=mem_norm_g, w_cq=w_cq, w_ckv=w_ckv,
               xattn_qnorm_g=xattn_qnorm_g, xattn_knorm_g=xattn_knorm_g, w_co=w_co, norm_mlp_g=norm_mlp_g, w_mlp1=w_mlp1,
               w_mlp2=w_mlp2)
    mom = dict(norm_mix_g=m_norm_mix_g, w_in=m_w_in, fox_qnorm_g=m_fox_qnorm_g, fox_knorm_g=m_fox_knorm_g,
               fox_f_bias=m_fox_f_bias, fox_onorm_g=m_fox_onorm_g, gdn_conv_w=m_gdn_conv_w, gdn_A_log=m_gdn_A_log,
               gdn_dt_bias=m_gdn_dt_bias, gdn_onorm_g=m_gdn_onorm_g, w_out=m_w_out, norm_xattn_g=m_norm_xattn_g,
               mem_norm_g=m_mem_norm_g, w_cq=m_w_cq, w_ckv=m_w_ckv, xattn_qnorm_g=m_xattn_qnorm_g,
               xattn_knorm_g=m_xattn_knorm_g, w_co=m_w_co, norm_mlp_g=m_norm_mlp_g, w_mlp1=m_w_mlp1, w_mlp2=m_w_mlp2)
    var = dict(norm_mix_g=v_norm_mix_g, w_in=v_w_in, fox_qnorm_g=v_fox_qnorm_g, fox_knorm_g=v_fox_knorm_g,
               fox_f_bias=v_fox_f_bias, fox_onorm_g=v_fox_onorm_g, gdn_conv_w=v_gdn_conv_w, gdn_A_log=v_gdn_A_log,
               gdn_dt_bias=v_gdn_dt_bias, gdn_onorm_g=v_gdn_onorm_g, w_out=v_w_out, norm_xattn_g=v_norm_xattn_g,
               mem_norm_g=v_mem_norm_g, w_cq=v_w_cq, w_ckv=v_w_ckv, xattn_qnorm_g=v_xattn_qnorm_g,
               xattn_knorm_g=v_xattn_knorm_g, w_co=v_w_co, norm_mlp_g=v_norm_mlp_g, w_mlp1=v_w_mlp1, w_mlp2=v_w_mlp2)
    B, S, D = x.shape
    T = B * S
    big_names = [n for n, _, _ in BIG_SHARDS]
    chip = 2 * lax.axis_index("x") + lax.axis_index("y")
    core = lax.axis_index("c").astype(jnp.int32).reshape(1)

    shards = {n: wts[n][0].astype(MXU_DTYPE) for n in big_names[1:]}
    in_t = lambda p: jnp.swapaxes(p[0], 0, 1)
    shards["w_in"] = jnp.pad(in_t(w_in).astype(MXU_DTYPE), ((0, IN_SHARD_PAD - IN_SHARD), (0, 0)))
    w_in_all, conv_all = gather_weights([shards["w_in"]], gdn_conv_w[0])
    late = big_names[1:]
    send_sems, recv_sems, late_src, late_zones, token = gather_weights_start([shards[n] for n in late], conv_all)
    own = lambda g, s: lax.dynamic_update_slice(g, s[None], (chip,) + (0,) * s.ndim)
    full = {"w_in": own(w_in_all, shards["w_in"])}
    conv_full = own(conv_all, gdn_conv_w[0]).transpose(1, 0, 2).reshape(CONV_WIDTH, 3 * GDN_WIDTH)
    rows = lambda g: g.reshape(N_CHIPS * g.shape[1], g.shape[2])
    w_in_t = full["w_in"][:, :IN_SHARD].reshape(IN_DIM, D_MODEL)

    def late_weights(after):
        zones = gather_weights_wait(send_sems, recv_sems, late_src, late_zones, after)
        got = {n: own(z, shards[n]) for n, z in zip(late, zones)}
        return dict(w_out=rows(got["w_out"]), w_cq=rows(got["w_cq"]), w_ckv=rows(got["w_ckv"]), w_co=got["w_co"],
                    w_mlp1=got["w_mlp1"], w_mlp2=rows(got["w_mlp2"]))

    in_flight = []

    def grads_ready(ready):
        names = list(ready)
        *started, tok = reduce_grads_start([ready[n] for n in names], name="reduce_grads_start_%d" % len(in_flight))
        in_flight.append((names, *started))
        return tok

    w = dict(wa_t=align_w_in_t(w_in_t), conv_w=conv_full, late=late_weights, grads_ready=grads_ready)
    sp = {n: wts[n] for n, _ in SMALL_DIMS}
    sp["norm_mix_g"] = sp["norm_mix_g"] + token[0, 0]

    loss_part, grad_x, g_big, g_small = local_step(x.reshape(T, D), mem.reshape(-1, D), loss_target.reshape(T, D), w, sp, B=B)

    small_pieces = [g_small[n] for n, _ in SMALL_DIMS] + [g_small["gdn_conv_w"], loss_part]
    small_sizes = [d for _, d in SMALL_DIMS] + [CONV_WIDTH * 3 * GDN_WIDTH, LANES]
    red_small = _unpack_rows(all_reduce_small(_pack_rows(small_pieces, SMALL_ROWS)), small_sizes)
    grads = {n: p.reshape(1, d) for (n, d), p in zip(SMALL_DIMS, red_small)}
    conv_grad = lax.dynamic_slice(red_small[-2].reshape(CONV_WIDTH, 3 * GDN_WIDTH), (0, chip * CONV_SHARD[1]), CONV_SHARD)
    grads["gdn_conv_w"] = conv_grad.reshape((1,) + CONV_SHARD)
    loss = red_small[-1][0]

    tok_in = grads_ready({"w_in": g_big["w_in"]})
    parts, zones = {}, {}

    def wait_group(k, after):
        names, send_sems, recv_sems, thru, land = in_flight[k]
        thru, land = reduce_grads_wait(send_sems, recv_sems, thru, land, after, name="reduce_grads_wait_%d" % k)
        parts.update(zip(names, thru))
        zones.update(zip(names, land))

    wait_group(0, tok_in)
    wait_group(1, tok_in)
    dev = 2 * chip + core[0]
    where = jnp.stack([chip, core[0]] + [dev ^ r for r in range(1, N_DEV)]).astype(jnp.int32)
    mine = [sum_partials(parts[n], zones[n], where, name="sum_partials_" + n) for n in late]
    theirs = swap_reduced_halves(mine, name="swap_reduced_halves")

    delta, new_m, new_v = {}, {}, {}
    for n, a, b in zip(late, mine, theirs):
        g, d, nm, nv = adamw_halves(wts[n][0], a, b, mom[n][0], var[n][0], core, name="adamw_" + n)
        grads[n], delta[n], new_m[n], new_v[n] = g[None], d[None], nm[None], nv[None]
    wait_group(2, new_v[late[-1]])
    mine_in = sum_partials(parts["w_in"], zones["w_in"], where, name="sum_partials_w_in")
    (theirs_in,) = swap_reduced_halves([mine_in], name="swap_reduced_halves_w_in")
    south = core[0] == 0
    g_in_t = jnp.concatenate([jnp.where(south, mine_in, theirs_in), jnp.where(south, theirs_in, mine_in)])[:IN_SHARD]
    back = lambda t: jnp.swapaxes(t, 0, 1)[None]
    d, nm, nv = adamw(in_t(w_in), g_in_t, in_t(m_w_in), in_t(v_w_in), name="adamw_w_in", tc=256)
    grads["w_in"], delta["w_in"], new_m["w_in"], new_v["w_in"] = back(g_in_t), back(d), back(nm), back(nv)
    small_names = [n for n, _ in SMALL_DIMS] + ["gdn_conv_w"]
    small_sz = [d for _, d in SMALL_DIMS] + [CONV_SHARD[0] * CONV_SHARD[1]]
    packed4 = [_pack_rows([src[n] for n in small_names], SMALL_ADAM_ROWS) for src in (wts, grads, mom, var)]
    outs = adamw(*packed4, name="adamw_small", tr=SMALL_ADAM_ROWS)
    for dst, buf in zip((delta, new_m, new_v), outs):
        for n, p in zip(small_names, _unpack_rows(buf, small_sz)):
            dst[n] = p.reshape(wts[n].shape)

    return (loss, grad_x.reshape(B, S, D), *[grads[n] for n in WEIGHT_ORDER], *[delta[n] for n in WEIGHT_ORDER],
            *[new_m[n] for n in WEIGHT_ORDER], *[new_v[n] for n in WEIGHT_ORDER])
```

```python
import functools

import jax
import jax.numpy as jnp
import numpy as np
from jax import lax
from jax.experimental import pallas as pl
from jax.experimental.pallas import tpu as pltpu

f32 = jnp.float32
bf16 = jnp.bfloat16
MXU_DTYPE = jnp.bfloat16
WIRE_DTYPE = jnp.bfloat16
INV_PRECISION = lax.Precision.HIGH

D_MODEL = 1024
FOX_HEADS = 8
FOX_HEAD_DIM = 64
FOX_WIDTH = 512
GDN_HEADS = 4
GDN_HEAD_DIM = 128
GDN_WIDTH = 512
CONV_WIDTH = 4
GDN_CHUNK = 64
XATTN_HEADS = 4
XATTN_HEAD_DIM = 128
XATTN_WIDTH = 512
D_FF = 4096
IN_DIM = 3600
EPS = 1e-6
NEG_INF = -1e30
LANES = 128
ADAM_LR = 0.001
ADAM_B1 = 0.9
ADAM_B2 = 0.999
ADAM_EPS = 1e-08
ADAM_WD = 0.01
ADAM_STEP = 10
VMEM_LIMIT = 48 * 1024 * 1024

COL_FOX = 0
COL_GDN = 1536
COL_Z = 3072
COL_SMALL = 3584
IN_ALIGNED = 3840
IN_TILE = 768
SM_F = 0
SM_B = 8
SM_A = 12


def _cparams(*sem):
    return pltpu.CompilerParams(dimension_semantics=sem, vmem_limit_bytes=VMEM_LIMIT)


def _mx(v):
    return v.astype(MXU_DTYPE)


def _dot(a, b, dims, precision=None):
    return lax.dot_general(a, b, (dims, ((), ())), preferred_element_type=f32, precision=precision)


def _dotm(a, b, dims):
    return _dot(_mx(a), _mx(b), dims)


NN = ((1,), (0,))
NT = ((1,), (1,))
TN = ((0,), (0,))


def matmul(a, b, *, name, ta=False, tb=False, b_stacked=False, out_stacked=False, residual=None, relu2_out=False,
           relu2_bwd_aux=None, out_dtype=f32, tm=1024, tn=1024, tk=1024):
    M, K = (a.shape[1], a.shape[0]) if ta else a.shape
    if b_stacked:
        b_cols = b.shape[2]
        N, tk = (b.shape[1], min(tk, b_cols)) if tb else (N_CHIPS * b_cols, tk)
        tn = tn if tb else min(tn, b_cols)
        assert K == (N_CHIPS * b_cols if tb else b.shape[1]), (name, a.shape, b.shape)
    else:
        N = b.shape[0] if tb else b.shape[1]
    if out_stacked:
        tn = min(tn, N // N_CHIPS)
    tm, tn, tk = min(tm, M), min(tn, N), min(tk, K)
    assert M % tm == 0 and N % tn == 0 and K % tk == 0, (name, M, N, K)
    nk = K // tk
    has_res = residual is not None
    has_aux = relu2_bwd_aux is not None

    def body(*refs):
        a_ref, b_ref = refs[0], refs[1]
        pos = 2
        res_ref = aux_ref = None
        if has_res:
            res_ref = refs[pos]
            pos += 1
        if has_aux:
            aux_ref = refs[pos]
            pos += 1
        o_ref = refs[pos]
        k = pl.program_id(2)
        dims = ((0,) if ta else (1,), (1,) if tb else (0,))
        part = _dot(_mx(a_ref[...]), _mx(b_ref[...]), dims)

        def finish(r):
            if has_res:
                r = r + res_ref[...]
            if has_aux:
                r = r * (2.0 * jnp.sqrt(aux_ref[...].astype(f32)))
            if relu2_out:
                o_ref[...] = jnp.square(jnp.maximum(r, 0.0)).astype(o_ref.dtype)
            else:
                o_ref[...] = r.astype(o_ref.dtype)

        if nk == 1:
            finish(part)
            return
        acc_ref = refs[pos + 1]

        @pl.when(k == 0)
        def _():
            acc_ref[...] = part

        @pl.when((k > 0) & (k < nk - 1))
        def _():
            acc_ref[...] += part

        @pl.when(k == nk - 1)
        def _():
            finish(acc_ref[...] + part)

    a_spec = pl.BlockSpec((tk, tm), lambda i, j, k: (k, i)) if ta else pl.BlockSpec((tm, tk), lambda i, j, k: (i, k))
    if b_stacked and tb:
        per = b_cols // tk
        b_spec = pl.BlockSpec((None, tn, tk), lambda i, j, k: (k // per, j, k % per))
    elif b_stacked:
        per = b_cols // tn
        b_spec = pl.BlockSpec((None, tk, tn), lambda i, j, k: (j // per, k, j % per))
    else:
        b_spec = pl.BlockSpec((tn, tk), lambda i, j, k: (j, k)) if tb else pl.BlockSpec((tk, tn), lambda i, j, k: (k, j))
    if out_stacked:
        assert not (has_res or has_aux or relu2_out), name
        per_o = N // N_CHIPS // tn
        o_spec = pl.BlockSpec((None, tm, tn), lambda i, j, k: (j // per_o, i, j % per_o))
        out_full = (N_CHIPS, M, N // N_CHIPS)
    else:
        o_spec = pl.BlockSpec((tm, tn), lambda i, j, k: (i, j))
        out_full = (M, N)
    in_specs, args = [a_spec, b_spec], [a, b]
    if has_res:
        in_specs.append(o_spec)
        args.append(residual)
    if has_aux:
        in_specs.append(o_spec)
        args.append(relu2_bwd_aux)
    out_shape = [jax.ShapeDtypeStruct(out_full, out_dtype)]
    out_specs = [o_spec]
    res = pl.pallas_call(
        body, name=name, grid=(M // tm, N // tn, nk), in_specs=in_specs, out_specs=out_specs, out_shape=out_shape,
        scratch_shapes=[pltpu.VMEM((tm, tn), f32)] if nk > 1 else [],
        compiler_params=_cparams("parallel", "parallel", "arbitrary"),
    )(*args)
    return res[0]


def matmul_rows(a, b, extras, *, name, mode, tb=False, b_stacked=False, tm=1024, tk=1024):
    M, K = a.shape
    N = D_MODEL
    if b_stacked:
        assert tb, name
        tk = min(tk, b.shape[2])
        per = b.shape[2] // tk
        b_spec = pl.BlockSpec((None, N, tk), lambda i, k: (k // per, 0, k % per))
    elif tb:
        tk = min(tk, K)
        b_spec = pl.BlockSpec((N, tk), lambda i, k: (0, k))
    else:
        tk = min(tk, K)
        b_spec = pl.BlockSpec((tk, N), lambda i, k: (k, 0))
    tm = min(tm, M)
    assert M % tm == 0 and K % tk == 0, (name, M, K)
    nk = K // tk
    extras = [e for e in extras if e is not None]
    n_ex = len(extras)

    def body(*refs):
        a_ref, b_ref = refs[0], refs[1]
        ex = refs[2:2 + n_ex]
        o_ref = refs[2 + n_ex]
        n_out = 3 if mode == "loss" else 2
        s_ref = refs[1 + n_ex + n_out]
        i, k = pl.program_id(0), pl.program_id(1)
        part = _dot(_mx(a_ref[...]), _mx(b_ref[...]), ((1,), (1,) if tb else (0,)))

        def finish(y):
            @pl.when(i == 0)
            def _():
                s_ref[...] = jnp.zeros_like(s_ref)

            if mode == "rms_bwd":
                xv, gv = ex[0][...], ex[1][...]
                rstd = lax.rsqrt(jnp.mean(xv * xv, axis=-1, keepdims=True) + EPS)
                xhat = xv * rstd
                gd = y * gv
                dx = rstd * (gd - xhat * jnp.mean(gd * xhat, axis=-1, keepdims=True))
                o_ref[...] = dx + ex[2][...] if n_ex == 3 else dx
                s_ref[...] += jnp.sum(y * xhat, axis=0, keepdims=True)
            else:
                e = y + ex[0][...] - ex[1][...]
                o_ref[...] = e * (1.0 / N)
                refs[3 + n_ex][...] = (e * (1.0 / N)).astype(MXU_DTYPE)
                tot = 0.5 * jnp.sum(jnp.mean(e * e, axis=-1, keepdims=True), axis=0, keepdims=True)
                s_ref[...] += jnp.broadcast_to(tot, s_ref.shape)

        if nk == 1:
            finish(part)
            return
        acc_ref = refs[2 + n_ex + n_out]

        @pl.when(k == 0)
        def _():
            acc_ref[...] = part

        @pl.when((k > 0) & (k < nk - 1))
        def _():
            acc_ref[...] += part

        @pl.when(k == nk - 1)
        def _():
            finish(acc_ref[...] + part)

    row = pl.BlockSpec((tm, N), lambda i, k: (i, 0))
    vec = pl.BlockSpec((1, N), lambda i, k: (0, 0))
    if mode == "rms_bwd":
        ex_specs = [row, vec] + ([row] if n_ex == 3 else [])
        s_shape, s_spec = jax.ShapeDtypeStruct((1, N), f32), vec
    else:
        ex_specs = [row, row]
        s_shape, s_spec = jax.ShapeDtypeStruct((1, LANES), f32), pl.BlockSpec((1, LANES), lambda i, k: (0, 0))
    return pl.pallas_call(
        body, name=name, grid=(M // tm, nk),
        in_specs=[pl.BlockSpec((tm, tk), lambda i, k: (i, k)), b_spec] + ex_specs,
        out_specs=[row] * (2 if mode == "loss" else 1) + [s_spec],
        out_shape=[jax.ShapeDtypeStruct((M, N), f32)] + ([jax.ShapeDtypeStruct((M, N), MXU_DTYPE)] if mode == "loss" else [])
        + [s_shape],
        scratch_shapes=[pltpu.VMEM((tm, N), f32)] if nk > 1 else [],
        compiler_params=_cparams("arbitrary", "arbitrary"),
    )(a, b, *extras)


def rms_fwd(x, g, *, name, tr=512):
    R, D = x.shape
    tr = min(tr, R)

    def body(x_ref, g_ref, o_ref):
        xv = x_ref[...]
        y = xv * lax.rsqrt(jnp.mean(xv * xv, axis=-1, keepdims=True) + EPS)
        o_ref[...] = (y * g_ref[...]).astype(o_ref.dtype)

    return pl.pallas_call(
        body, name=name, grid=(R // tr,),
        in_specs=[pl.BlockSpec((tr, D), lambda i: (i, 0)), pl.BlockSpec((1, D), lambda i: (0, 0))],
        out_specs=pl.BlockSpec((tr, D), lambda i: (i, 0)),
        out_shape=jax.ShapeDtypeStruct((R, D), MXU_DTYPE),
        compiler_params=_cparams("parallel"),
    )(x, g)


def rms_bwd(x, g, dh, residual, *, name, tr=512):
    R, D = x.shape
    tr = min(tr, R)
    has_res = residual is not None

    def body(*refs):
        if has_res:
            x_ref, g_ref, dh_ref, res_ref, dx_ref, dg_ref = refs
        else:
            x_ref, g_ref, dh_ref, dx_ref, dg_ref = refs
        xv = x_ref[...]
        rstd = lax.rsqrt(jnp.mean(xv * xv, axis=-1, keepdims=True) + EPS)
        xhat = xv * rstd
        dh = dh_ref[...].astype(f32)
        gd = dh * g_ref[...]
        dx = rstd * (gd - xhat * jnp.mean(gd * xhat, axis=-1, keepdims=True))
        if has_res:
            dx = dx + res_ref[...]
        dx_ref[...] = dx

        @pl.when(pl.program_id(0) == 0)
        def _():
            dg_ref[...] = jnp.zeros_like(dg_ref)

        dg_ref[...] += jnp.sum(dh * xhat, axis=0, keepdims=True)

    row = pl.BlockSpec((tr, D), lambda i: (i, 0))
    vec = pl.BlockSpec((1, D), lambda i: (0, 0))
    in_specs = [row, vec, row] + ([row] if has_res else [])
    args = [x, g, dh] + ([residual] if has_res else [])
    return pl.pallas_call(
        body, name=name, grid=(R // tr,), in_specs=in_specs, out_specs=[row, vec],
        out_shape=[jax.ShapeDtypeStruct((R, D), f32), jax.ShapeDtypeStruct((1, D), f32)],
        compiler_params=_cparams("arbitrary"),
    )(*args)


def loss_head(y, target, *, tr=512):
    R, D = y.shape
    tr = min(tr, R)

    def body(y_ref, t_ref, dy_ref, loss_ref):
        e = y_ref[...] - t_ref[...]
        dy_ref[...] = e * (1.0 / D)

        @pl.when(pl.program_id(0) == 0)
        def _():
            loss_ref[...] = jnp.zeros_like(loss_ref)

        part = 0.5 * jnp.sum(jnp.mean(e * e, axis=-1, keepdims=True), axis=0, keepdims=True)
        loss_ref[...] += jnp.broadcast_to(part, loss_ref.shape)

    row = pl.BlockSpec((tr, D), lambda i: (i, 0))
    return pl.pallas_call(
        body, name="loss_head", grid=(R // tr,), in_specs=[row, row],
        out_specs=[row, pl.BlockSpec((1, LANES), lambda i: (0, 0))],
        out_shape=[jax.ShapeDtypeStruct((R, D), f32), jax.ShapeDtypeStruct((1, LANES), f32)],
        compiler_params=_cparams("arbitrary"),
    )(y, target)


def _head_rms(v, g):
    r = lax.rsqrt(jnp.mean(v * v, axis=-1, keepdims=True) + EPS)
    return v * r * g, r


def _head_rms_bwd(v, r, g, dn):
    vhat = v * r
    gd = dn * g
    dv = r * (gd - vhat * jnp.mean(gd * vhat, axis=-1, keepdims=True))
    return dv, jnp.sum(dn * vhat, axis=0, keepdims=True)


def _softmax_rows(s):
    m = jnp.max(s, axis=-1, keepdims=True)
    e = jnp.exp(s - m)
    return e / jnp.sum(e, axis=-1, keepdims=True)


def xattn_fwd(cq, ckv, gq, gk, *, B, tq=512):
    T = cq.shape[0]
    S = T // B
    M = ckv.shape[0] // B
    tq = min(tq, S)
    nq = S // tq
    scale = XATTN_HEAD_DIM ** -0.5

    def body(q_ref, k_ref, v_ref, gq_ref, gk_ref, o_ref):
        qn, _ = _head_rms(q_ref[...], gq_ref[...])
        kn, _ = _head_rms(k_ref[...], gk_ref[...])
        p = _softmax_rows(_dot(_mx(qn), _mx(kn), NT) * scale)
        o_ref[...] = _dot(_mx(p), _mx(v_ref[...]), NN).astype(o_ref.dtype)

    hd = XATTN_HEAD_DIM
    vec = pl.BlockSpec((1, hd), lambda b, h, i: (0, 0))
    return pl.pallas_call(
        body, name="xattn_fwd", grid=(B, XATTN_HEADS, nq),
        in_specs=[pl.BlockSpec((tq, hd), lambda b, h, i: (b * nq + i, h)),
                  pl.BlockSpec((M, hd), lambda b, h, i: (b, h)),
                  pl.BlockSpec((M, hd), lambda b, h, i: (b, XATTN_HEADS + h)), vec, vec],
        out_specs=pl.BlockSpec((tq, hd), lambda b, h, i: (b * nq + i, h)),
        out_shape=jax.ShapeDtypeStruct((T, XATTN_WIDTH), MXU_DTYPE),
        compiler_params=_cparams("parallel", "parallel", "parallel"),
    )(cq, ckv, ckv, gq, gk)


def xattn_bwd(cq, ckv, gq, gk, dco, *, B, tq=512):
    T = cq.shape[0]
    S = T // B
    M = ckv.shape[0] // B
    tq = min(tq, S)
    nq = S // tq
    scale = XATTN_HEAD_DIM ** -0.5
    hd = XATTN_HEAD_DIM

    def body(q_ref, k_ref, v_ref, gq_ref, gk_ref, do_ref, dq_ref, dk_ref, dv_ref, dgq_ref, dgk_ref, dkn_acc, dv_acc):
        b, h, i = pl.program_id(0), pl.program_id(1), pl.program_id(2)

        @pl.when((b == 0) & (h == 0) & (i == 0))
        def _():
            dgq_ref[...] = jnp.zeros_like(dgq_ref)
            dgk_ref[...] = jnp.zeros_like(dgk_ref)

        @pl.when(i == 0)
        def _():
            dkn_acc[...] = jnp.zeros_like(dkn_acc)
            dv_acc[...] = jnp.zeros_like(dv_acc)

        q, k, v = q_ref[...], k_ref[...], v_ref[...]
        gqv, gkv = gq_ref[...], gk_ref[...]
        qn, rq = _head_rms(q, gqv)
        kn, rk = _head_rms(k, gkv)
        p = _softmax_rows(_dot(_mx(qn), _mx(kn), NT) * scale)
        do = do_ref[...]
        dv_acc[...] += _dot(_mx(p), _mx(do), TN)
        dp = _dot(_mx(do), _mx(v), NT)
        ds = p * (dp - jnp.sum(dp * p, axis=-1, keepdims=True)) * scale
        dqn = _dot(_mx(ds), _mx(kn), NN)
        dkn_acc[...] += _dot(_mx(ds), _mx(qn), TN)
        dq, dgq = _head_rms_bwd(q, rq, gqv, dqn)
        dq_ref[...] = dq.astype(dq_ref.dtype)
        dgq_ref[...] += dgq

        @pl.when(i == nq - 1)
        def _():
            dk, dgk = _head_rms_bwd(k, rk, gkv, dkn_acc[...])
            dk_ref[...] = dk.astype(dk_ref.dtype)
            dv_ref[...] = dv_acc[...].astype(dv_ref.dtype)
            dgk_ref[...] += dgk

    vec = pl.BlockSpec((1, hd), lambda b, h, i: (0, 0))
    qspec = pl.BlockSpec((tq, hd), lambda b, h, i: (b * nq + i, h))
    kspec = pl.BlockSpec((M, hd), lambda b, h, i: (b, h))
    vspec = pl.BlockSpec((M, hd), lambda b, h, i: (b, XATTN_HEADS + h))
    dq, dk, dv, dgq, dgk = pl.pallas_call(
        body, name="xattn_bwd", grid=(B, XATTN_HEADS, nq),
        in_specs=[qspec, kspec, vspec, vec, vec, qspec],
        out_specs=[qspec, kspec, kspec, vec, vec],
        out_shape=[jax.ShapeDtypeStruct((T, XATTN_WIDTH), MXU_DTYPE),
                   jax.ShapeDtypeStruct((B * M, XATTN_WIDTH), MXU_DTYPE),
                   jax.ShapeDtypeStruct((B * M, XATTN_WIDTH), MXU_DTYPE),
                   jax.ShapeDtypeStruct((1, hd), f32), jax.ShapeDtypeStruct((1, hd), f32)],
        scratch_shapes=[pltpu.VMEM((M, hd), f32), pltpu.VMEM((M, hd), f32)],
        compiler_params=_cparams("arbitrary", "arbitrary", "arbitrary"),
    )(cq, ckv, ckv, gq, gk, dco)
    return dq, jnp.concatenate([dk, dv], axis=1), dgq, dgk


FOX_PAIRS = FOX_HEADS // 2


def _fox_scores(qn, kn, ccol, crow, q0, tq, S, scale):
    s = _dot(_mx(qn), _mx(kn), NT) * scale + ccol - crow
    qpos = q0 + lax.broadcasted_iota(jnp.int32, (tq, S), 0)
    kpos = lax.broadcasted_iota(jnp.int32, (tq, S), 1)
    return jnp.where(kpos <= qpos, s, NEG_INF)


def fox_fwd(P, ccol, crow, gq, gk, go, *, B, tq=256):
    T = P.shape[0]
    S = T // B
    tq = min(tq, S)
    nq = S // tq
    hd = FOX_HEAD_DIM
    scale = hd ** -0.5

    def body(q_ref, k_ref, v_ref, ccol_ref, crow_ref, gq_ref, gk_ref, go_ref, o_ref, oa_ref):
        q0 = pl.program_id(2) * tq
        for e in range(2):
            sl = slice(e * hd, (e + 1) * hd)
            qn, _ = _head_rms(q_ref[:, sl], gq_ref[:, sl])
            kn, _ = _head_rms(k_ref[:, sl], gk_ref[:, sl])
            p = _softmax_rows(_fox_scores(qn, kn, ccol_ref[0, e], crow_ref[0, e], q0, tq, S, scale))
            o = _dot(_mx(p), _mx(v_ref[:, sl]), NN)
            o_ref[:, sl] = o
            oa_ref[:, sl] = _head_rms(o, go_ref[:, sl])[0].astype(oa_ref.dtype)

    W = 2 * hd
    vec = pl.BlockSpec((1, W), lambda b, h, i: (0, 0))
    ospec = pl.BlockSpec((tq, W), lambda b, h, i: (b * nq + i, h))
    return pl.pallas_call(
        body, name="fox_fwd", grid=(B, FOX_PAIRS, nq),
        in_specs=[pl.BlockSpec((tq, W), lambda b, h, i: (b * nq + i, h)),
                  pl.BlockSpec((S, W), lambda b, h, i: (b, FOX_PAIRS + h)),
                  pl.BlockSpec((S, W), lambda b, h, i: (b, 2 * FOX_PAIRS + h)),
                  pl.BlockSpec((1, 2, tq, 1), lambda b, h, i: (b, h, i, 0)),
                  pl.BlockSpec((1, 2, 1, S), lambda b, h, i: (b, h, 0, 0)), vec, vec, vec],
        out_specs=[ospec, ospec],
        out_shape=[jax.ShapeDtypeStruct((T, FOX_WIDTH), f32), jax.ShapeDtypeStruct((T, FOX_WIDTH), MXU_DTYPE)],
        compiler_params=_cparams("parallel", "parallel", "parallel"),
    )(P, P, P, ccol, crow, gq, gk, go)


def fox_bwd(P, ccol, crow, gq, gk, go, o_raw, d_oab, *, B, tq=256):
    T = P.shape[0]
    S = T // B
    tq = min(tq, S)
    nq = S // tq
    hd = FOX_HEAD_DIM
    scale = hd ** -0.5

    def body(q_ref, k_ref, v_ref, ccol_ref, crow_ref, gq_ref, gk_ref, go_ref, o_ref, doa_ref,
             dq_ref, dk_ref, dv_ref, dccol_ref, dcrow_ref, dgq_ref, dgk_ref, dgo_ref, dkn_acc, dv_acc, dcrow_acc):
        b, h, i = pl.program_id(0), pl.program_id(1), pl.program_id(2)
        q0 = i * tq

        @pl.when((b == 0) & (h == 0) & (i == 0))
        def _():
            dgq_ref[...] = jnp.zeros_like(dgq_ref)
            dgk_ref[...] = jnp.zeros_like(dgk_ref)
            dgo_ref[...] = jnp.zeros_like(dgo_ref)

        @pl.when(i == 0)
        def _():
            dkn_acc[...] = jnp.zeros_like(dkn_acc)
            dv_acc[...] = jnp.zeros_like(dv_acc)
            dcrow_acc[...] = jnp.zeros_like(dcrow_acc)

        for e in range(2):
            sl = slice(e * hd, (e + 1) * hd)
            q, k, v = q_ref[:, sl], k_ref[:, sl], v_ref[:, sl]
            gqv, gkv, gov = gq_ref[:, sl], gk_ref[:, sl], go_ref[:, sl]
            qn, rq = _head_rms(q, gqv)
            kn, rk = _head_rms(k, gkv)
            p = _softmax_rows(_fox_scores(qn, kn, ccol_ref[0, e], crow_ref[0, e], q0, tq, S, scale))
            o = o_ref[:, sl]
            ro = lax.rsqrt(jnp.mean(o * o, axis=-1, keepdims=True) + EPS)
            do, dgo = _head_rms_bwd(o, ro, gov, doa_ref[:, sl])
            dgo_ref[:, sl] += dgo
            dv_acc[e] += _dot(_mx(p), _mx(do), TN)
            dp = _dot(_mx(do), _mx(v), NT)
            ds = p * (dp - jnp.sum(do * o, axis=-1, keepdims=True))
            dccol_ref[0, e] = jnp.sum(ds, axis=1, keepdims=True)
            dcrow_acc[e] -= jnp.sum(ds, axis=0, keepdims=True)
            dqn = _dot(_mx(ds), _mx(kn), NN) * scale
            dkn_acc[e] += _dot(_mx(ds), _mx(qn), TN) * scale
            dq, dgq = _head_rms_bwd(q, rq, gqv, dqn)
            dq_ref[:, sl] = dq.astype(dq_ref.dtype)
            dgq_ref[:, sl] += dgq

        @pl.when(i == nq - 1)
        def _():
            for e in range(2):
                sl = slice(e * hd, (e + 1) * hd)
                k = k_ref[:, sl]
                gkv = gk_ref[:, sl]
                rk = lax.rsqrt(jnp.mean(k * k, axis=-1, keepdims=True) + EPS)
                dk, dgk = _head_rms_bwd(k, rk, gkv, dkn_acc[e])
                dk_ref[:, sl] = dk.astype(dk_ref.dtype)
                dv_ref[:, sl] = dv_acc[e].astype(dv_ref.dtype)
                dgk_ref[:, sl] += dgk
                dcrow_ref[0, e] = dcrow_acc[e]

    W = 2 * hd
    vec = pl.BlockSpec((1, W), lambda b, h, i: (0, 0))
    qspec = pl.BlockSpec((tq, W), lambda b, h, i: (b * nq + i, h))
    kvout = pl.BlockSpec((S, W), lambda b, h, i: (b, h))
    colspec = pl.BlockSpec((1, 2, tq, 1), lambda b, h, i: (b, h, i, 0))
    rowspec = pl.BlockSpec((1, 2, 1, S), lambda b, h, i: (b, h, 0, 0))
    return pl.pallas_call(
        body, name="fox_bwd", grid=(B, FOX_PAIRS, nq),
        in_specs=[qspec,
                  pl.BlockSpec((S, W), lambda b, h, i: (b, FOX_PAIRS + h)),
                  pl.BlockSpec((S, W), lambda b, h, i: (b, 2 * FOX_PAIRS + h)),
                  colspec, rowspec, vec, vec, vec, qspec, qspec],
        out_specs=[qspec, kvout, kvout, colspec, rowspec, vec, vec, vec],
        out_shape=[jax.ShapeDtypeStruct((T, FOX_WIDTH), MXU_DTYPE), jax.ShapeDtypeStruct((T, FOX_WIDTH), MXU_DTYPE),
                   jax.ShapeDtypeStruct((T, FOX_WIDTH), MXU_DTYPE),
                   jax.ShapeDtypeStruct((B, FOX_HEADS, S, 1), f32), jax.ShapeDtypeStruct((B, FOX_HEADS, 1, S), f32),
                   jax.ShapeDtypeStruct((1, W), f32), jax.ShapeDtypeStruct((1, W), f32), jax.ShapeDtypeStruct((1, W), f32)],
        scratch_shapes=[pltpu.VMEM((2, S, hd), f32), pltpu.VMEM((2, S, hd), f32), pltpu.VMEM((2, 1, S), f32)],
        compiler_params=_cparams("arbitrary", "arbitrary", "arbitrary"),
    )(P, P, P, ccol, crow, gq, gk, go, o_raw, d_oab)


FOX_TQ = 512
FOX_TK = FOX_TQ
GROUP_PRECISION = lax.Precision.HIGH


def _head_mean(v):
    n = v.shape[1]
    r = lax.broadcasted_iota(jnp.int32, (n, n), 0) // FOX_HEAD_DIM
    c = lax.broadcasted_iota(jnp.int32, (n, n), 1) // FOX_HEAD_DIM
    ones = (r == c).astype(bf16)
    hi = v.astype(bf16)
    lo = (v - hi.astype(f32)).astype(bf16)
    return (_dot(hi, ones, NN) + _dot(lo, ones, NN)) * (1.0 / FOX_HEAD_DIM)


def fox_prep_fwd(P, gq, gk, *, tr=512):
    T = P.shape[0]
    tr = min(tr, T)
    scale = FOX_HEAD_DIM ** -0.5

    def body(q_ref, k_ref, v_ref, gq_ref, gk_ref, qn_ref, kn_ref, vb_ref):
        q, k = q_ref[...], k_ref[...]
        qn_ref[...] = (q * lax.rsqrt(_head_mean(q * q) + EPS) * (gq_ref[...] * scale)).astype(qn_ref.dtype)
        kn_ref[...] = (k * lax.rsqrt(_head_mean(k * k) + EPS) * gk_ref[...]).astype(kn_ref.dtype)
        vb_ref[...] = v_ref[...].astype(vb_ref.dtype)

    W = FOX_WIDTH
    col = lambda j: pl.BlockSpec((tr, W), lambda i: (i, j))
    vec = pl.BlockSpec((1, W), lambda i: (0, 0))
    out = jax.ShapeDtypeStruct((T, W), MXU_DTYPE)
    return pl.pallas_call(
        body, name="fox_prep_fwd", grid=(T // tr,), in_specs=[col(0), col(1), col(2), vec, vec],
        out_specs=[col(0)] * 3, out_shape=[out] * 3, compiler_params=_cparams("parallel"),
    )(P, P, P, gq, gk)


def fox_prep_bwd(P, gq, gk, dqn, dkn, *, tr=512):
    T = P.shape[0]
    tr = min(tr, T)
    scale = FOX_HEAD_DIM ** -0.5

    def body(q_ref, k_ref, gq_ref, gk_ref, dqn_ref, dkn_ref, dq_ref, dk_ref, dgq_ref, dgk_ref):
        @pl.when(pl.program_id(0) == 0)
        def _():
            dgq_ref[...] = jnp.zeros_like(dgq_ref)
            dgk_ref[...] = jnp.zeros_like(dgk_ref)

        def one(x, g, dn, dx_ref, dg_ref):
            r = lax.rsqrt(_head_mean(x * x) + EPS)
            xhat = x * r
            gd = dn * g
            dx_ref[...] = (r * (gd - xhat * _head_mean(gd * xhat))).astype(dx_ref.dtype)
            return jnp.sum(dn * xhat, axis=0, keepdims=True)

        dgq_ref[...] += scale * one(q_ref[...], gq_ref[...] * scale, dqn_ref[...], dq_ref, dgq_ref)
        dgk_ref[...] += one(k_ref[...], gk_ref[...], dkn_ref[...], dk_ref, dgk_ref)

    W = FOX_WIDTH
    col = lambda j: pl.BlockSpec((tr, W), lambda i: (i, j))
    vec = pl.BlockSpec((1, W), lambda i: (0, 0))
    return pl.pallas_call(
        body, name="fox_prep_bwd", grid=(T // tr,), in_specs=[col(0), col(1), vec, vec, col(0), col(0)],
        out_specs=[col(0), col(0), vec, vec],
        out_shape=[jax.ShapeDtypeStruct((T, W), MXU_DTYPE), jax.ShapeDtypeStruct((T, W), MXU_DTYPE),
                   jax.ShapeDtypeStruct((1, W), f32), jax.ShapeDtypeStruct((1, W), f32)],
        compiler_params=_cparams("arbitrary"),
    )(P, P, gq, gk, dqn, dkn)


def _fox_tile_scores(q, k_ref, crow_ref, cq, e, j, sl, mask_off):
    tq, tk = FOX_TQ, FOX_TK
    rows = pl.ds(pl.multiple_of(j * tk, tk), tk)
    k = k_ref[rows, sl]
    ck = jnp.transpose(jnp.broadcast_to(crow_ref[0, e, j], (LANES, tk)))[:, 0:1]
    s = _dot(k, q, NT) + cq - ck
    if mask_off is not None:
        key = lax.broadcasted_iota(jnp.int32, (tk, tq), 0) + mask_off
        query = lax.broadcasted_iota(jnp.int32, (tk, tq), 1)
        s = jnp.where(key <= query, s, NEG_INF)
    return s, k, rows


def _fox_sweep(i, update, carry):
    nd = FOX_TQ // FOX_TK
    carry = lax.fori_loop(0, i * nd, lambda j, cr: update(cr, j, None), carry)
    for d in range(nd):
        carry = update(carry, i * nd + d, d * FOX_TK)
    return carry


def fox_core_fwd(qn, kn, vb, crow, go, *, B):
    T = qn.shape[0]
    S = T // B
    tq = FOX_TQ
    nq = S // tq
    hd = FOX_HEAD_DIM

    def body(q_ref, k_ref, v_ref, crow_ref, go_ref, o_ref, oa_ref, lse_ref):
        i = pl.program_id(2)
        for e in range(2):
            sl = slice(e * hd, (e + 1) * hd)
            q = q_ref[:, sl]
            cq = crow_ref[0, e, i]

            def update(carry, j, mask_off):
                m, l, acc = carry
                s, _, rows = _fox_tile_scores(q, k_ref, crow_ref, cq, e, j, sl, mask_off)
                m2 = jnp.maximum(m, jnp.max(s, axis=0, keepdims=True))
                a = jnp.exp(m - m2)
                p = jnp.exp(s - m2)
                return m2, a * l + jnp.sum(p, axis=0, keepdims=True), a * acc + _dot(v_ref[rows, sl], _mx(p), TN)

            carry = (jnp.full((1, tq), NEG_INF, f32), jnp.zeros((1, tq), f32), jnp.zeros((hd, tq), f32))
            m, l, acc = _fox_sweep(i, update, carry)
            o = (acc / l).T
            o_ref[:, sl] = o
            oa_ref[:, sl] = _head_rms(o, go_ref[:, sl])[0].astype(oa_ref.dtype)
            lse_ref[0, e, 0] = m + jnp.log(l)

    W = 2 * hd
    qspec = pl.BlockSpec((tq, W), lambda b, h, i: (b * nq + i, h))
    kspec = pl.BlockSpec((S, W), lambda b, h, i: (b, h))
    return pl.pallas_call(
        body, name="fox_core_fwd", grid=(B, FOX_PAIRS, nq),
        in_specs=[qspec, kspec, kspec, pl.BlockSpec((1, 2, nq, 1, tq), lambda b, h, i: (b, h, 0, 0, 0)),
                  pl.BlockSpec((1, W), lambda b, h, i: (0, 0))],
        out_specs=[qspec, qspec, pl.BlockSpec((1, 2, 1, 1, tq), lambda b, h, i: (b, h, i, 0, 0))],
        out_shape=[jax.ShapeDtypeStruct((T, FOX_WIDTH), f32), jax.ShapeDtypeStruct((T, FOX_WIDTH), MXU_DTYPE),
                   jax.ShapeDtypeStruct((B, FOX_HEADS, nq, 1, tq), f32)],
        compiler_params=_cparams("parallel", "parallel", "parallel"),
    )(qn, kn, vb, crow, go)


def fox_core_bwd(qn, kn, vb, crow, go, o_raw, lse, d_oab, *, B):
    T = qn.shape[0]
    S = T // B
    tq = FOX_TQ
    nq = S // tq
    hd = FOX_HEAD_DIM

    def body(q_ref, k_ref, v_ref, crow_ref, go_ref, o_ref, lse_ref, doa_ref,
             dq_ref, dk_ref, dv_ref, dckey_ref, dcrow_ref, dgo_ref, dk_acc, dv_acc, dck_acc):
        b, h, i = pl.program_id(0), pl.program_id(1), pl.program_id(2)

        @pl.when((b == 0) & (h == 0) & (i == 0))
        def _():
            dgo_ref[...] = jnp.zeros_like(dgo_ref)

        @pl.when(i == 0)
        def _():
            dk_acc[...] = jnp.zeros_like(dk_acc)
            dv_acc[...] = jnp.zeros_like(dv_acc)
            dck_acc[...] = jnp.zeros_like(dck_acc)

        for e in range(2):
            sl = slice(e * hd, (e + 1) * hd)
            q = q_ref[:, sl]
            cq = crow_ref[0, e, i]
            lse_e = lse_ref[0, e, 0]
            o = o_ref[:, sl]
            ro = lax.rsqrt(jnp.mean(o * o, axis=-1, keepdims=True) + EPS)
            do, dgo = _head_rms_bwd(o, ro, go_ref[:, sl], doa_ref[:, sl])
            dgo_ref[:, sl] += dgo
            delta = jnp.sum((do * o).T, axis=0, keepdims=True)
            do_b = _mx(do)

            def update(carry, j, mask_off):
                dq, dcq = carry
                s, k, rows = _fox_tile_scores(q, k_ref, crow_ref, cq, e, j, sl, mask_off)
                p = jnp.exp(s - lse_e)
                dv_acc[e, rows, :] += _dot(_mx(p), do_b, NN)
                ds = p * (_dot(v_ref[rows, sl], do_b, NT) - delta)
                dck_acc[e, rows, :] -= jnp.sum(ds, axis=1, keepdims=True)
                ds_b = _mx(ds)
                dk_acc[e, rows, :] += _dot(ds_b, q, NN)
                return dq + _dot(ds_b, k, TN), dcq + jnp.sum(ds, axis=0, keepdims=True)

            dq, dcq = _fox_sweep(i, update, (jnp.zeros((tq, hd), f32), jnp.zeros((1, tq), f32)))
            dq_ref[:, sl] = dq
            dcrow_ref[0, e, 0] = dcq

        @pl.when(i == nq - 1)
        def _():
            for e in range(2):
                sl = slice(e * hd, (e + 1) * hd)
                dk_ref[:, sl] = dk_acc[e]
                dv_ref[:, sl] = dv_acc[e].astype(dv_ref.dtype)
                dckey_ref[0, e] = jnp.transpose(jnp.broadcast_to(dck_acc[e], (S, LANES)))[0:1, :]

    W = 2 * hd
    qspec = pl.BlockSpec((tq, W), lambda b, h, i: (b * nq + i, h))
    kspec = pl.BlockSpec((S, W), lambda b, h, i: (b, h))
    colspec = pl.BlockSpec((1, 2, S, 1), lambda b, h, i: (b, h, 0, 0))
    rowspec = pl.BlockSpec((1, 2, nq, 1, tq), lambda b, h, i: (b, h, 0, 0, 0))
    tilespec = pl.BlockSpec((1, 2, 1, 1, tq), lambda b, h, i: (b, h, i, 0, 0))
    vec = pl.BlockSpec((1, W), lambda b, h, i: (0, 0))
    return pl.pallas_call(
        body, name="fox_core_bwd", grid=(B, FOX_PAIRS, nq),
        in_specs=[qspec, kspec, kspec, rowspec, vec, qspec, tilespec, qspec],
        out_specs=[qspec, kspec, kspec, pl.BlockSpec((1, 2, 1, S), lambda b, h, i: (b, h, 0, 0)), tilespec, vec],
        out_shape=[jax.ShapeDtypeStruct((T, FOX_WIDTH), f32), jax.ShapeDtypeStruct((T, FOX_WIDTH), f32),
                   jax.ShapeDtypeStruct((T, FOX_WIDTH), MXU_DTYPE),
                   jax.ShapeDtypeStruct((B, FOX_HEADS, 1, S), f32), jax.ShapeDtypeStruct((B, FOX_HEADS, nq, 1, tq), f32),
                   jax.ShapeDtypeStruct((1, W), f32)],
        scratch_shapes=[pltpu.VMEM((2, S, hd), f32), pltpu.VMEM((2, S, hd), f32), pltpu.VMEM((2, S, 1), f32)],
        compiler_params=_cparams("arbitrary", "arbitrary", "arbitrary"),
    )(qn, kn, vb, crow, go, o_raw, lse, d_oab)


def _lane_mask(lo, hi, shape):
    lane = lax.broadcasted_iota(jnp.int32, shape, 1)
    return (lane >= lo) & (lane < hi)


def _cumsum_rows(v, period, reverse=False):
    n = v.shape[0]
    pos = lax.broadcasted_iota(jnp.int32, v.shape, 0) % period
    sh = 1
    while sh < period:
        if reverse:
            v = v + jnp.where(pos + sh < period, pltpu.roll(v, n - sh, 0), 0.0)
        else:
            v = v + jnp.where(pos >= sh, pltpu.roll(v, sh, 0), 0.0)
        sh *= 2
    return v


def _gate_values(z, bias, alog):
    zb = z + bias
    ls = jax.nn.log_sigmoid(zb)
    beta = jax.nn.sigmoid(z)
    g = -jnp.exp(alog) * jax.nn.softplus(zb)
    return zb, ls, beta, g


def gates_fwd(P, bias, alog, *, B):
    T = P.shape[0]
    S = T // B

    def body(z_ref, bias_ref, alog_ref, o_ref):
        z = z_ref[...]
        _, ls, beta, g = _gate_values(z, bias_ref[...], alog_ref[...])
        c = _cumsum_rows(ls, S)
        gc = _cumsum_rows(g, GDN_CHUNK)
        o = jnp.where(_lane_mask(SM_F, SM_F + FOX_HEADS, z.shape), c, 0.0)
        o = jnp.where(_lane_mask(SM_B, SM_B + GDN_HEADS, z.shape), beta, o)
        o = jnp.where(_lane_mask(SM_A, SM_A + GDN_HEADS, z.shape), gc, o)
        o_ref[...] = o

    vec = pl.BlockSpec((1, LANES), lambda b: (0, 0))
    return pl.pallas_call(
        body, name="gates_fwd", grid=(B,),
        in_specs=[pl.BlockSpec((S, LANES), lambda b: (b, COL_SMALL // LANES)), vec, vec],
        out_specs=pl.BlockSpec((S, LANES), lambda b: (b, 0)),
        out_shape=jax.ShapeDtypeStruct((T, LANES), f32),
        compiler_params=_cparams("parallel"),
    )(P, bias, alog)


def gates_bwd(P, bias, alog, dgates, *, B):
    T = P.shape[0]
    S = T // B

    def body(z_ref, bias_ref, alog_ref, dg_ref, dz_ref, par_ref):
        z = z_ref[...]
        zb, ls, beta, g = _gate_values(z, bias_ref[...], alog_ref[...])
        d = dg_ref[...]
        dls = _cumsum_rows(d, S, reverse=True)
        dgr = _cumsum_rows(d, GDN_CHUNK, reverse=True)
        sig = jax.nn.sigmoid(zb)
        dz_f = dls * (1.0 - sig)
        dz_b = d * beta * (1.0 - beta)
        dz_a = dgr * (-jnp.exp(alog_ref[...])) * sig
        dz = jnp.where(_lane_mask(SM_F, SM_F + FOX_HEADS, z.shape), dz_f, 0.0)
        dz = jnp.where(_lane_mask(SM_B, SM_B + GDN_HEADS, z.shape), dz_b, dz)
        dz = jnp.where(_lane_mask(SM_A, SM_A + GDN_HEADS, z.shape), dz_a, dz)
        dz_ref[...] = dz.astype(dz_ref.dtype)

        @pl.when(pl.program_id(0) == 0)
        def _():
            par_ref[...] = jnp.zeros_like(par_ref)

        dalog = jnp.where(_lane_mask(SM_A, SM_A + GDN_HEADS, z.shape), dgr * g, 0.0)
        par_ref[0:1, :] += jnp.sum(dz, axis=0, keepdims=True)
        par_ref[1:2, :] += jnp.sum(dalog, axis=0, keepdims=True)

    vec = pl.BlockSpec((1, LANES), lambda b: (0, 0))
    return pl.pallas_call(
        body, name="gates_bwd", grid=(B,),
        in_specs=[pl.BlockSpec((S, LANES), lambda b: (b, COL_SMALL // LANES)), vec, vec,
                  pl.BlockSpec((S, LANES), lambda b: (b, 0))],
        out_specs=[pl.BlockSpec((S, LANES), lambda b: (b, 0)), pl.BlockSpec((8, LANES), lambda b: (0, 0))],
        out_shape=[jax.ShapeDtypeStruct((T, LANES), MXU_DTYPE), jax.ShapeDtypeStruct((8, LANES), f32)],
        compiler_params=_cparams("arbitrary"),
    )(P, bias, alog, dgates)


GDN_BLOCKS = 3 * GDN_HEADS


def _shift_rows(v, d, reverse=False):
    if d == 0:
        return v
    n = v.shape[0]
    row = lax.broadcasted_iota(jnp.int32, v.shape, 0)
    if reverse:
        return jnp.where(row + d < n, pltpu.roll(v, n - d, 0), 0.0)
    return jnp.where(row >= d, pltpu.roll(v, d, 0), 0.0)


def _conv_silu(x, w):
    pre = sum(w[j:j + 1, :] * _shift_rows(x, CONV_WIDTH - 1 - j) for j in range(CONV_WIDTH))
    return pre, pre * jax.nn.sigmoid(pre)


def gdn_prep_fwd(P, conv_w, *, B):
    T = P.shape[0]
    S = T // B

    def body(x_ref, w_ref, o_ref):
        _, y = _conv_silu(x_ref[...], w_ref[...])
        yn = y * lax.rsqrt(jnp.sum(y * y, axis=-1, keepdims=True) + EPS)
        o_ref[...] = jnp.where(pl.program_id(1) < 2 * GDN_HEADS, yn, y)

    return pl.pallas_call(
        body, name="gdn_prep_fwd", grid=(B, GDN_BLOCKS),
        in_specs=[pl.BlockSpec((S, LANES), lambda b, j: (b, COL_GDN // LANES + j)),
                  pl.BlockSpec((CONV_WIDTH, LANES), lambda b, j: (0, j))],
        out_specs=pl.BlockSpec((S, LANES), lambda b, j: (b, j)),
        out_shape=jax.ShapeDtypeStruct((T, 3 * GDN_WIDTH), f32),
        compiler_params=_cparams("parallel", "parallel"),
    )(P, conv_w)


def gdn_prep_bwd(P, conv_w, dG, *, B):
    T = P.shape[0]
    S = T // B

    def body(x_ref, w_ref, dg_ref, dx_ref, dw_ref):
        x, w = x_ref[...], w_ref[...]
        pre, y = _conv_silu(x, w)
        dn = dg_ref[...]
        r = lax.rsqrt(jnp.sum(y * y, axis=-1, keepdims=True) + EPS)
        n = y * r
        dy_norm = r * (dn - n * jnp.sum(dn * n, axis=-1, keepdims=True))
        dy = jnp.where(pl.program_id(0) < 2 * GDN_HEADS, dy_norm, dn)
        sg = jax.nn.sigmoid(pre)
        dpre = dy * (sg * (1.0 + pre * (1.0 - sg)))
        dx = sum(w[j:j + 1, :] * _shift_rows(dpre, CONV_WIDTH - 1 - j, reverse=True) for j in range(CONV_WIDTH))
        dx_ref[...] = dx.astype(dx_ref.dtype)

        @pl.when(pl.program_id(1) == 0)
        def _():
            dw_ref[...] = jnp.zeros_like(dw_ref)

        for j in range(CONV_WIDTH):
            dw_ref[j:j + 1, :] += jnp.sum(dpre * _shift_rows(x, CONV_WIDTH - 1 - j), axis=0, keepdims=True)

    return pl.pallas_call(
        body, name="gdn_prep_bwd", grid=(GDN_BLOCKS, B),
        in_specs=[pl.BlockSpec((S, LANES), lambda j, b: (b, COL_GDN // LANES + j)),
                  pl.BlockSpec((CONV_WIDTH, LANES), lambda j, b: (0, j)),
                  pl.BlockSpec((S, LANES), lambda j, b: (b, j))],
        out_specs=[pl.BlockSpec((S, LANES), lambda j, b: (b, j)),
                   pl.BlockSpec((CONV_WIDTH, LANES), lambda j, b: (0, j))],
        out_shape=[jax.ShapeDtypeStruct((T, 3 * GDN_WIDTH), MXU_DTYPE),
                   jax.ShapeDtypeStruct((CONV_WIDTH, 3 * GDN_WIDTH), f32)],
        compiler_params=_cparams("arbitrary", "arbitrary"),
    )(P, conv_w, dG)


GDN_GROUP = 16
B_NN = (((2,), (1,)), ((0,), (0,)))
B_NT = (((2,), (2,)), ((0,), (0,)))
B_TN = (((1,), (1,)), ((0,), (0,)))


def _bmm(a, b, dims, precision=None):
    if precision is None:
        a, b = _mx(a), _mx(b)
    return lax.dot_general(a, b, dims, preferred_element_type=f32, precision=precision)


def _tri_inverse(A):
    C = A.shape[-1]
    row = lax.broadcasted_iota(jnp.int32, A.shape, 1)
    col = lax.broadcasted_iota(jnp.int32, A.shape, 2)
    eye = (row == col).astype(f32)
    X = jnp.where((row // 4) == (col // 4), -A, 0.0)
    X2 = _bmm(X, X, B_NN, INV_PRECISION)
    Tm = eye + X + X2 + _bmm(X, X2, B_NN, INV_PRECISION)
    b = 4
    while b < C:
        off = ((row // (2 * b)) == (col // (2 * b))) & ((row // b) != (col // b))
        Tm = Tm - _bmm(_bmm(Tm, jnp.where(off, A, 0.0), B_NN, INV_PRECISION), Tm, B_NN, INV_PRECISION)
        b *= 2
    return Tm


def _pick_lane(block, lane_idx):
    lane = lax.broadcasted_iota(jnp.int32, block.shape, 1)
    return jnp.sum(jnp.where(lane == lane_idx, block, 0.0), axis=1, keepdims=True)


def _gdn_local(q, k, v, beta, gc, Tm=None, uwm=None):
    C = GDN_CHUNK
    n = q.shape[0] // C
    q = q.reshape(n, C, -1) * (GDN_HEAD_DIM ** -0.5)
    k = k.reshape(n, C, -1)
    v = v.reshape(n, C, -1)
    beta = beta.reshape(n, C, 1)
    gc = gc.reshape(n, C, 1)
    row = lax.broadcasted_iota(jnp.int32, (n, C, C), 1)
    col = lax.broadcasted_iota(jnp.int32, (n, C, C), 2)
    gcT = jnp.swapaxes(jnp.broadcast_to(gc, (n, C, C)), 1, 2)
    D = jnp.exp(jnp.where(row >= col, gc - gcT, NEG_INF))
    kb = k * beta
    vb = v * beta
    A = jnp.where(row > col, _bmm(kb, k, B_NT) * D, 0.0)
    Gam = jnp.exp(gc)
    kg = kb * Gam
    gl = gc[:, C - 1:C, :]
    kdec = jnp.exp(gl - gc)
    loc = dict(q=q, k=k, v=v, beta=beta, gc=gc, D=D, kb=kb, vb=vb, A=A, Gam=Gam, kg=kg,
               kdec=kdec, kd=k * kdec, qg=q * Gam, gam=jnp.exp(gl), row=row, col=col)
    uwm = Tm is None if uwm is None else uwm
    Tm = _tri_inverse(A) if Tm is None else Tm.reshape(n, C, C)
    if uwm:
        loc.update(u=_bmm(Tm, vb, B_NN), w=_bmm(Tm, kg, B_NN), M=_bmm(q, k, B_NT) * D)
    loc["Tm"] = Tm
    return loc


def _gdn_store_local(loc, r0, u_s, w_s, qg_s, kd_s, M_s, gam_s, c0):
    n = loc["u"].shape[0]
    R = n * GDN_CHUNK
    u_s[pl.ds(r0, R), :] = loc["u"].reshape(R, -1)
    w_s[pl.ds(r0, R), :] = loc["w"].reshape(R, -1)
    qg_s[pl.ds(r0, R), :] = loc["qg"].reshape(R, -1)
    kd_s[pl.ds(r0, R), :] = loc["kd"].reshape(R, -1)
    M_s[pl.ds(r0, R), :] = loc["M"].reshape(R, -1)
    gam_s[pl.ds(c0, n)] = jnp.broadcast_to(loc["gam"], (n, 1, LANES))


def _gdn_specs(S):
    blk = lambda off: pl.BlockSpec((S, LANES), lambda b, h: (b, off + h))
    return blk


def gdn_fwd(G, gates, P, g_on, *, B):
    T = G.shape[0]
    S = T // B
    C = GDN_CHUNK
    N = S // C
    grp = min(GDN_GROUP, N)
    R = grp * C
    hd = GDN_HEAD_DIM

    def body(q_ref, k_ref, v_ref, gt_ref, z_ref, gon_ref, o_ref, ob_ref, st_ref, tm_ref, A_s, B_s, Q_s, O_s, gam_s):
        h = pl.program_id(1)

        def local(gi, carry):
            r0 = pl.multiple_of(gi * R, R)
            gt = gt_ref[pl.ds(r0, R), :]
            loc = _gdn_local(q_ref[pl.ds(r0, R), :], k_ref[pl.ds(r0, R), :], v_ref[pl.ds(r0, R), :],
                             _pick_lane(gt, SM_B + h), _pick_lane(gt, SM_A + h))
            chunks = pl.ds(gi * grp, grp)
            tm_ref[0, 0, pl.ds(r0, R), :] = loc["Tm"].reshape(R, C)
            A_s[chunks] = -_bmm(loc["kd"], loc["w"], B_TN)
            B_s[chunks] = _bmm(loc["kd"], loc["u"], B_TN)
            Q_s[pl.ds(r0, R), :] = (loc["qg"] - _bmm(loc["M"], loc["w"], B_NN)).reshape(R, hd)
            O_s[pl.ds(r0, R), :] = _bmm(loc["M"], loc["u"], B_NN).reshape(R, hd)
            gam_s[chunks] = jnp.broadcast_to(loc["gam"], (grp, 1, LANES))
            return carry

        lax.fori_loop(0, N // grp, local, 0)

        def step(n, state):
            st_ref[0, 0, n] = state
            return state * gam_s[n] + _dotm(A_s[n], state, NN) + B_s[n]

        lax.fori_loop(0, N, step, jnp.zeros((hd, hd), f32))

        def outputs(gi, carry):
            r0 = pl.multiple_of(gi * R, R)
            Q = Q_s[pl.ds(r0, R), :].reshape(grp, C, hd)
            o = _bmm(Q, st_ref[0, 0, pl.ds(gi * grp, grp)], B_NN).reshape(R, hd) + O_s[pl.ds(r0, R), :]
            o_ref[pl.ds(r0, R), :] = o
            return carry

        lax.fori_loop(0, N // grp, outputs, 0)
        o = o_ref[...]
        z = z_ref[...]
        ob_ref[...] = (_head_rms(o, gon_ref[...])[0] * (z * jax.nn.sigmoid(z))).astype(ob_ref.dtype)

    blk = lambda off: pl.BlockSpec((S, LANES), lambda b, h: (b, off + h))
    rows = lambda: pltpu.VMEM((S, hd), f32)
    return pl.pallas_call(
        body, name="gdn_fwd", grid=(B, GDN_HEADS),
        in_specs=[blk(0), blk(GDN_HEADS), blk(2 * GDN_HEADS), pl.BlockSpec((S, LANES), lambda b, h: (b, 0)),
                  blk(COL_Z // LANES), pl.BlockSpec((1, hd), lambda b, h: (0, 0))],
        out_specs=[blk(0), blk(0), pl.BlockSpec((1, 1, N, hd, hd), lambda b, h: (b, h, 0, 0, 0)),
                   pl.BlockSpec((1, 1, S, C), lambda b, h: (b, h, 0, 0))],
        out_shape=[jax.ShapeDtypeStruct((T, GDN_WIDTH), f32), jax.ShapeDtypeStruct((T, GDN_WIDTH), MXU_DTYPE),
                   jax.ShapeDtypeStruct((B, GDN_HEADS, N, hd, hd), f32), jax.ShapeDtypeStruct((B, GDN_HEADS, S, C), f32)],
        scratch_shapes=[pltpu.VMEM((N, hd, hd), f32), pltpu.VMEM((N, hd, hd), f32), rows(), rows(),
                        pltpu.VMEM((N, 1, LANES), f32)],
        compiler_params=_cparams("parallel", "parallel"),
    )(G, G, G, gates, P, g_on)


def gdn_bwd(G, gates, P, g_on, o_raw, states, tm, d_oab, *, B):
    T = G.shape[0]
    S = T // B
    C = GDN_CHUNK
    N = S // C
    grp = min(GDN_GROUP, N)
    R = grp * C
    hd = GDN_HEAD_DIM

    def body(q_ref, k_ref, v_ref, gt_ref, z_ref, gon_ref, o_ref, st_ref, tm_ref, dob_ref,
             dq_ref, dk_ref, dv_ref, dgt_ref, dz_ref, dgon_ref,
             u_s, w_s, M_s, gam_s, do_s, A_s, C_s, dst_s):
        b, h = pl.program_id(0), pl.program_id(1)

        @pl.when((b == 0) & (h == 0))
        def _():
            dgon_ref[...] = jnp.zeros_like(dgon_ref)

        @pl.when(h == 0)
        def _():
            dgt_ref[...] = jnp.zeros_like(dgt_ref)

        def group_inputs(gi, uwm):
            r0 = pl.multiple_of(gi * R, R)
            gt = gt_ref[pl.ds(r0, R), :]
            return r0, _gdn_local(q_ref[pl.ds(r0, R), :], k_ref[pl.ds(r0, R), :], v_ref[pl.ds(r0, R), :],
                                  _pick_lane(gt, SM_B + h), _pick_lane(gt, SM_A + h), tm_ref[0, 0, pl.ds(r0, R), :], uwm)

        def local(gi, carry):
            r0, loc = group_inputs(gi, True)
            rows, chunks = pl.ds(r0, R), pl.ds(gi * grp, grp)
            u_s[rows, :] = loc["u"].reshape(R, hd)
            w_s[rows, :] = loc["w"].reshape(R, hd)
            M_s[rows, :] = loc["M"].reshape(R, C)
            gam_s[chunks] = jnp.broadcast_to(loc["gam"], (grp, 1, LANES))
            o, z, gon = o_ref[rows, :], z_ref[rows, :], gon_ref[...]
            dob = dob_ref[rows, :]
            on, ro = _head_rms(o, gon)
            sz = jax.nn.sigmoid(z)
            dz_ref[rows, :] = (dob * on * (sz * (1.0 + z * (1.0 - sz)))).astype(dz_ref.dtype)
            do, dgon = _head_rms_bwd(o, ro, gon, dob * (z * sz))
            do_s[rows, :] = do
            dgon_ref[...] += dgon
            A_s[chunks] = -_bmm(loc["kd"], loc["w"], B_TN)
            C_s[chunks] = _bmm(loc["qg"] - _bmm(loc["M"], loc["w"], B_NN), do.reshape(grp, C, hd), B_TN)
            return carry

        lax.fori_loop(0, N // grp, local, 0)

        def step(t, dS):
            n = N - 1 - t
            dst_s[n] = dS
            return dS * gam_s[n] + _dotm(A_s[n], dS, TN) + C_s[n]

        lax.fori_loop(0, N, step, jnp.zeros((hd, hd), f32))

        def finish(gi, carry):
            r0, L = group_inputs(gi, False)
            n = grp
            rows, chunks = pl.ds(r0, R), pl.ds(gi * grp, grp)
            g3 = lambda ref: ref[rows, :].reshape(n, C, -1)
            u, w, do = g3(u_s), g3(w_s), g3(do_s)
            L["M"] = g3(M_s)
            state, dS = st_ref[0, 0, chunks], dst_s[chunks]
            v_new = u - _bmm(w, state, B_NN)
            du = _bmm(L["M"], do, B_TN) + _bmm(L["kd"], dS, B_NN)
            dw = -_bmm(du, state, B_NT)
            dqg = _bmm(do, state, B_NT)
            dM = _bmm(do, v_new, B_NT)
            dkd = _bmm(v_new, dS, B_NT)
            dgl_state = jnp.sum(jnp.sum(dS * state, axis=2, keepdims=True), axis=1, keepdims=True) * L["gam"]
            TmT = jnp.swapaxes(L["Tm"], 1, 2)
            dTm = _bmm(du, L["vb"], B_NT) + _bmm(dw, L["kg"], B_NT)
            dvb = _bmm(TmT, du, B_NN)
            dkg = _bmm(TmT, dw, B_NN)
            dA = jnp.where(L["row"] > L["col"], -_bmm(_bmm(TmT, dTm, B_NN), TmT, B_NN), 0.0)
            dKK = dA * L["D"]
            dQK = dM * L["D"]
            dkb = _bmm(dKK, L["k"], B_NN) + dkg * L["Gam"]
            dk = (_bmm(dKK, L["kb"], B_TN) + _bmm(dQK, L["q"], B_TN) + dkd * L["kdec"] + L["beta"] * dkb)
            dq = (_bmm(dQK, L["k"], B_NN) + dqg * L["Gam"]) * (GDN_HEAD_DIM ** -0.5)
            E = dA * L["A"] + dM * L["M"]
            r = jnp.sum(dkd * L["kd"], axis=-1, keepdims=True)
            dgc = (jnp.sum(E, axis=2, keepdims=True) - jnp.sum(jnp.swapaxes(E, 1, 2), axis=2, keepdims=True)
                   + jnp.sum(dkg * L["kg"], axis=-1, keepdims=True) + jnp.sum(dqg * L["qg"], axis=-1, keepdims=True) - r)
            dgl = jnp.sum(r, axis=1, keepdims=True) + dgl_state
            rowc = lax.broadcasted_iota(jnp.int32, (n, C, 1), 1)
            dgc = dgc + jnp.where(rowc == C - 1, dgl, 0.0)
            dbeta = jnp.sum(dkb * L["k"], axis=-1, keepdims=True) + jnp.sum(dvb * L["v"], axis=-1, keepdims=True)
            dq_ref[rows, :] = dq.reshape(R, hd)
            dk_ref[rows, :] = dk.reshape(R, hd)
            dv_ref[rows, :] = (L["beta"] * dvb).reshape(R, hd)
            lane = lax.broadcasted_iota(jnp.int32, (R, LANES), 1)
            dgt_ref[rows, :] += (jnp.where(lane == SM_B + h, dbeta.reshape(R, 1), 0.0)
                                 + jnp.where(lane == SM_A + h, dgc.reshape(R, 1), 0.0))
            return carry

        lax.fori_loop(0, N // grp, finish, 0)

    blk = lambda off: pl.BlockSpec((S, LANES), lambda b, h: (b, off + h))
    rows = lambda: pltpu.VMEM((S, hd), f32)
    return pl.pallas_call(
        body, name="gdn_bwd", grid=(B, GDN_HEADS),
        in_specs=[blk(0), blk(GDN_HEADS), blk(2 * GDN_HEADS), pl.BlockSpec((S, LANES), lambda b, h: (b, 0)),
                  blk(COL_Z // LANES), pl.BlockSpec((1, hd), lambda b, h: (0, 0)), blk(0),
                  pl.BlockSpec((1, 1, N, hd, hd), lambda b, h: (b, h, 0, 0, 0)),
                  pl.BlockSpec((1, 1, S, C), lambda b, h: (b, h, 0, 0)), blk(GDN_HEADS)],
        out_specs=[blk(0), blk(0), blk(0), pl.BlockSpec((S, LANES), lambda b, h: (b, 0)), blk(0),
                   pl.BlockSpec((1, hd), lambda b, h: (0, 0))],
        out_shape=[jax.ShapeDtypeStruct((T, GDN_WIDTH), f32), jax.ShapeDtypeStruct((T, GDN_WIDTH), f32),
                   jax.ShapeDtypeStruct((T, GDN_WIDTH), f32), jax.ShapeDtypeStruct((T, LANES), f32),
                   jax.ShapeDtypeStruct((T, GDN_WIDTH), MXU_DTYPE), jax.ShapeDtypeStruct((1, hd), f32)],
        scratch_shapes=[rows(), rows(), pltpu.VMEM((S, C), f32), pltpu.VMEM((N, 1, LANES), f32), rows(),
                        pltpu.VMEM((N, hd, hd), f32), pltpu.VMEM((N, hd, hd), f32), pltpu.VMEM((N, hd, hd), f32)],
        compiler_params=_cparams("arbitrary", "arbitrary"),
    )(G, G, G, gates, P, g_on, o_raw, states, tm, d_oab)


IN_SPLIT = (0, 1536, 1544, 3080, 3088, 3600)


IN_SHARD = IN_DIM // 4
IN_SHARD_PAD = 928


def align_w_in_t(wt):
    s = IN_SPLIT
    pad = jnp.zeros((IN_ALIGNED - IN_DIM, wt.shape[1]), wt.dtype)
    return jnp.concatenate([wt[s[0]:s[1]], wt[s[2]:s[3]], wt[s[4]:s[5]], wt[s[1]:s[2]], wt[s[3]:s[4]], pad], axis=0)


def unalign_w_in_t(wa):
    return jnp.concatenate([wa[0:1536], wa[COL_SMALL:COL_SMALL + 8], wa[1536:3072],
                            wa[COL_SMALL + 8:COL_SMALL + 16], wa[3072:3584]], axis=0)


def _lanes_vec(pieces):
    v = jnp.zeros((1, LANES), f32)
    for off, a in pieces:
        v = lax.dynamic_update_slice(v, a.astype(f32), (0, off))
    return v


def local_step(x, mem, target, w, sp, *, B):
    T = x.shape[0]
    S = T // B
    gq8, gk8 = jnp.tile(sp["fox_qnorm_g"], (1, FOX_HEADS)), jnp.tile(sp["fox_knorm_g"], (1, FOX_HEADS))
    go2 = jnp.tile(sp["fox_onorm_g"], (1, 2))
    bias = _lanes_vec([(SM_F, sp["fox_f_bias"]), (SM_A, sp["gdn_dt_bias"])])
    alog = _lanes_vec([(SM_A, sp["gdn_A_log"])])

    h1 = rms_fwd(x, sp["norm_mix_g"], name="rms_mix")
    P = matmul(h1, w["wa_t"], tb=True, name="mm_in", tn=IN_TILE)
    gates = gates_fwd(P, bias, alog, B=B)
    c = gates[:, SM_F:SM_F + FOX_HEADS].reshape(B, S, FOX_HEADS).transpose(0, 2, 1)
    crow = c.reshape(B, FOX_HEADS, S // FOX_TQ, 1, FOX_TQ)
    qn, kn, vb = fox_prep_fwd(P, gq8, gk8)
    o_raw, o_a, lse = fox_core_fwd(qn, kn, vb, crow, go2, B=B)
    G = gdn_prep_fwd(P, w["conv_w"], B=B)
    ob_raw, o_b, states, gdn_tm = gdn_fwd(G, gates, P, sp["gdn_onorm_g"], B=B)
    oab = jnp.concatenate([o_a, o_b], axis=1)
    if "late" in w:
        w = {**w, **w["late"](oab)}
    x2 = matmul(oab, w["w_out"], residual=x, name="mm_out")
    hq = rms_fwd(x2, sp["norm_xattn_g"], name="rms_xattn")
    hm = rms_fwd(mem, sp["mem_norm_g"], name="rms_mem")
    cq = matmul(hq, w["w_cq"], name="mm_cq")
    ckv = matmul(hm, w["w_ckv"], name="mm_ckv")
    co = xattn_fwd(cq, ckv, sp["xattn_qnorm_g"], sp["xattn_knorm_g"], B=B)
    x3 = matmul(co, w["w_co"], b_stacked=True, residual=x2, name="mm_co")
    hf = rms_fwd(x3, sp["norm_mlp_g"], name="rms_mlp")
    act = matmul(hf, w["w_mlp1"], b_stacked=True, relu2_out=True, out_dtype=MXU_DTYPE, name="mm_mlp1")
    dy, dy_op, loss = matmul_rows(act, w["w_mlp2"], (x3, target), mode="loss", name="mm_mlp2_loss")

    da = matmul(dy_op, w["w_mlp2"], tb=True, relu2_bwd_aux=act, out_dtype=MXU_DTYPE, name="mm_d_act")
    g_mlp2 = matmul(act, dy_op, ta=True, out_dtype=WIRE_DTYPE, name="mm_g_mlp2")
    g_mlp1 = matmul(hf, da, ta=True, out_stacked=True, out_dtype=WIRE_DTYPE, name="mm_g_mlp1")
    by_rows = lambda g: g.reshape(N_CHIPS, g.shape[0] // N_CHIPS, g.shape[1])
    early = w.get("grads_ready", lambda grads: jnp.zeros((1, 1), f32))
    tok = early(dict(w_mlp1=g_mlp1, w_mlp2=by_rows(g_mlp2)))[0, 0]
    dx3, g_norm_mlp = matmul_rows(da, w["w_mlp1"], (x3, sp["norm_mlp_g"] + tok, dy), mode="rms_bwd", tb=True,
                                  b_stacked=True, name="mm_d_hf_rms")
    dco = matmul(dx3, w["w_co"], tb=True, b_stacked=True, name="mm_d_co")
    g_co = matmul(co, dx3, ta=True, out_stacked=True, out_dtype=WIRE_DTYPE, name="mm_g_co")
    dcq, dckv, g_xq, g_xk = xattn_bwd(cq, ckv, sp["xattn_qnorm_g"], sp["xattn_knorm_g"], dco, B=B)
    g_cq = matmul(hq, dcq, ta=True, out_dtype=WIRE_DTYPE, name="mm_g_cq")
    g_ckv = matmul(hm, dckv, ta=True, out_dtype=WIRE_DTYPE, name="mm_g_ckv")
    _, g_mem_norm = matmul_rows(dckv, w["w_ckv"], (mem, sp["mem_norm_g"], None), mode="rms_bwd", tb=True, name="mm_d_hm_rms")
    dx2, g_norm_xattn = matmul_rows(dcq, w["w_cq"], (x2, sp["norm_xattn_g"], dx3), mode="rms_bwd", tb=True, name="mm_d_hq_rms")
    doab = matmul(dx2, w["w_out"], tb=True, name="mm_d_oab")
    g_out = matmul(oab, dx2, ta=True, out_dtype=WIRE_DTYPE, name="mm_g_out")
    tok = early(dict(w_co=g_co, w_cq=by_rows(g_cq), w_ckv=by_rows(g_ckv), w_out=by_rows(g_out)))[0, 0]
    dqn, dkn, dv_f, dckey, dcrow, dgo2 = fox_core_bwd(qn, kn, vb, crow, go2 + tok, o_raw, lse, doab, B=B)
    dq_f, dk_f, dgq8, dgk8 = fox_prep_bwd(P, gq8, gk8, dqn, dkn)
    dGq, dGk, dGv, dgt, dz, g_gdn_on = gdn_bwd(G, gates, P, sp["gdn_onorm_g"], ob_raw, states, gdn_tm, doab, B=B)
    dPg, g_conv = gdn_prep_bwd(P, w["conv_w"], jnp.concatenate([dGq, dGk, dGv], axis=1), B=B)
    dc = (dckey[:, :, 0, :] + dcrow.reshape(B, FOX_HEADS, S)).transpose(0, 2, 1).reshape(T, FOX_HEADS)
    dgates = dgt + jnp.pad(dc, ((0, 0), (SM_F, LANES - SM_F - FOX_HEADS)))
    dsmall, par = gates_bwd(P, bias, alog, dgates, B=B)
    dP = jnp.concatenate([dq_f, dk_f, dv_f, dPg, dz, dsmall, jnp.zeros((T, IN_ALIGNED - COL_SMALL - LANES), MXU_DTYPE)], axis=1)
    g_wa = matmul(dP, h1, ta=True, out_dtype=WIRE_DTYPE, name="mm_g_in", tm=IN_TILE)
    dx, g_norm_mix = matmul_rows(dP, w["wa_t"], (x, sp["norm_mix_g"], dx2), mode="rms_bwd", tk=IN_TILE, name="mm_d_h1_rms")

    fold = lambda g: jnp.sum(g.reshape(-1, FOX_HEAD_DIM), axis=0, keepdims=True)
    g_in = jnp.pad(unalign_w_in_t(g_wa).reshape(N_CHIPS, IN_SHARD, D_MODEL), ((0, 0), (0, IN_SHARD_PAD - IN_SHARD), (0, 0)))
    big = dict(w_in=g_in, w_out=by_rows(g_out), w_cq=by_rows(g_cq), w_ckv=by_rows(g_ckv), w_co=g_co, w_mlp1=g_mlp1,
               w_mlp2=by_rows(g_mlp2))
    small = dict(norm_mix_g=g_norm_mix, fox_qnorm_g=fold(dgq8), fox_knorm_g=fold(dgk8),
                 fox_f_bias=par[0:1, SM_F:SM_F + FOX_HEADS], fox_onorm_g=fold(dgo2), gdn_conv_w=g_conv,
                 gdn_A_log=par[1:2, SM_A:SM_A + GDN_HEADS], gdn_dt_bias=par[0:1, SM_A:SM_A + GDN_HEADS],
                 gdn_onorm_g=g_gdn_on, norm_xattn_g=g_norm_xattn, mem_norm_g=g_mem_norm,
                 xattn_qnorm_g=g_xq, xattn_knorm_g=g_xk, norm_mlp_g=g_norm_mlp)
    return loss, dx, big, small


MESH_IDS = pl.DeviceIdType.MESH
N_CHIPS = 4
HBM_SPEC = pl.BlockSpec(memory_space=pltpu.HBM)
PACK_ROWS = 30720
PACK_HALF = PACK_ROWS // 2
PACK_BLOCK = 3072


def _place():
    return lax.axis_index("x"), lax.axis_index("y"), lax.axis_index("c")


def _other_chips(x, y):
    return [(1 - x, y), (x, 1 - y), (1 - x, 1 - y)]


def _remote(src, dst, send_sem, recv_sem, to):
    return pltpu.make_async_remote_copy(src_ref=src, dst_ref=dst, send_sem=send_sem, recv_sem=recv_sem,
                                        device_id=to, device_id_type=MESH_IDS)


def all_gather_shards(packed):
    half = PACK_HALF

    def body(src_ref, out_ref, send_sems, recv_sems):
        x, y, c = _place()
        me_chip = 2 * x + y
        sibling = (x, y, 1 - c)
        chips = _other_chips(x, y)

        def rows(chip, core):
            return out_ref.at[chip, pl.ds(core * half, half), :]

        sends = [_remote(src_ref.at[pl.ds(c * half, half), :], rows(me_chip, c), send_sems.at[j], recv_sems.at[j], (px, py, c))
                 for j, (px, py) in enumerate(chips)]
        for cp in sends:
            cp.start()
        passed = []
        for j, (px, py) in enumerate(chips):
            theirs = rows(2 * px + py, c)
            _remote(theirs, theirs, send_sems.at[j], recv_sems.at[j], (px, py, c)).wait_recv()
            cp = _remote(theirs, theirs, send_sems.at[3 + j], recv_sems.at[3 + j], sibling)
            cp.start()
            passed.append(cp)
        for j, (px, py) in enumerate(chips):
            theirs = rows(2 * px + py, 1 - c)
            _remote(theirs, theirs, send_sems.at[3 + j], recv_sems.at[3 + j], sibling).wait_recv()
        for cp in sends + passed:
            cp.wait_send()

    return pl.pallas_call(
        body, name="all_gather_shards", in_specs=[HBM_SPEC], out_specs=HBM_SPEC,
        out_shape=jax.ShapeDtypeStruct((N_CHIPS,) + packed.shape, packed.dtype),
        scratch_shapes=[pltpu.SemaphoreType.DMA((6,)), pltpu.SemaphoreType.DMA((6,))],
    )(packed)


def exchange_core_halves(G):
    half = PACK_HALF

    def body(g_ref, land_ref, send_sem, recv_sem):
        x, y, c = _place()
        cp = _remote(g_ref.at[:, pl.ds((1 - c) * half, half), :], land_ref, send_sem, recv_sem, (x, y, 1 - c))
        cp.start()
        cp.wait()

    return pl.pallas_call(
        body, name="exchange_core_halves", in_specs=[HBM_SPEC], out_specs=HBM_SPEC,
        out_shape=jax.ShapeDtypeStruct((N_CHIPS, half, LANES), G.dtype),
        scratch_shapes=[pltpu.SemaphoreType.DMA(()), pltpu.SemaphoreType.DMA(())],
    )(G)


def add_core_halves(G, land, core):
    nb = PACK_HALF // PACK_BLOCK

    def body(c_ref, g_ref, l_ref, o_ref):
        o_ref[...] = (g_ref[...].astype(f32) + l_ref[...].astype(f32)).astype(o_ref.dtype)

    blk = (1, PACK_BLOCK, LANES)
    return pl.pallas_call(
        body, name="add_core_halves",
        grid_spec=pltpu.PrefetchScalarGridSpec(
            num_scalar_prefetch=1, grid=(N_CHIPS, nb),
            in_specs=[pl.BlockSpec(blk, lambda k, i, c_ref: (k, c_ref[0] * nb + i, 0)),
                      pl.BlockSpec(blk, lambda k, i, c_ref: (k, i, 0))],
            out_specs=pl.BlockSpec(blk, lambda k, i, c_ref: (k, i, 0))),
        out_shape=jax.ShapeDtypeStruct(land.shape, land.dtype),
        compiler_params=_cparams("parallel", "parallel"),
    )(core, G, land)


def scatter_to_chips(part):
    def body(p_ref, land_ref, send_sems, recv_sems):
        x, y, c = _place()
        me_chip = 2 * x + y
        chips = _other_chips(x, y)
        sends = [_remote(p_ref.at[2 * px + py], land_ref.at[me_chip], send_sems.at[j], recv_sems.at[j], (px, py, c))
                 for j, (px, py) in enumerate(chips)]
        for cp in sends:
            cp.start()
        for j, (px, py) in enumerate(chips):
            slot = land_ref.at[2 * px + py]
            _remote(slot, slot, send_sems.at[j], recv_sems.at[j], (px, py, c)).wait_recv()
        for cp in sends:
            cp.wait_send()

    return pl.pallas_call(
        body, name="scatter_to_chips", in_specs=[HBM_SPEC], out_specs=HBM_SPEC,
        out_shape=jax.ShapeDtypeStruct(part.shape, part.dtype),
        scratch_shapes=[pltpu.SemaphoreType.DMA((3,)), pltpu.SemaphoreType.DMA((3,))],
    )(part)


def sum_chips(part, land, order):
    nb = PACK_HALF // PACK_BLOCK

    def body(order_ref, p_ref, l1_ref, l2_ref, l3_ref, o_ref):
        o_ref[...] = ((p_ref[0].astype(f32) + l1_ref[0].astype(f32)) + l2_ref[0].astype(f32)) + l3_ref[0].astype(f32)

    slot = lambda j: pl.BlockSpec((1, PACK_BLOCK, LANES), lambda i, order_ref: (order_ref[j], i, 0))
    return pl.pallas_call(
        body, name="sum_chips",
        grid_spec=pltpu.PrefetchScalarGridSpec(
            num_scalar_prefetch=1, grid=(nb,), in_specs=[slot(0), slot(1), slot(2), slot(3)],
            out_specs=pl.BlockSpec((PACK_BLOCK, LANES), lambda i, order_ref: (i, 0))),
        out_shape=jax.ShapeDtypeStruct((PACK_HALF, LANES), f32),
        compiler_params=_cparams("parallel"),
    )(order, part, land, land, land)


def swap_core_halves(red):
    def body(r_ref, out_ref, send_sem, recv_sem):
        x, y, c = _place()
        cp = _remote(r_ref, out_ref, send_sem, recv_sem, (x, y, 1 - c))
        cp.start()
        cp.wait()

    return pl.pallas_call(
        body, name="swap_core_halves", in_specs=[HBM_SPEC], out_specs=HBM_SPEC,
        out_shape=jax.ShapeDtypeStruct(red.shape, red.dtype),
        scratch_shapes=[pltpu.SemaphoreType.DMA(()), pltpu.SemaphoreType.DMA(())],
    )(red)


def _half(ref, core):
    rows = ref.shape[-2] // 2
    return ref.at[(slice(None),) * (len(ref.shape) - 2) + (pl.ds(core * rows, rows), slice(None))]


def gather_weights(shards, conv):
    n = len(shards)

    def body(*refs):
        src, conv_src = refs[:n], refs[n]
        out, conv_out = refs[n + 1:2 * n + 1], refs[2 * n + 1]
        send_sems, recv_sems = refs[2 * n + 2], refs[2 * n + 3]
        x, y, c = _place()
        me_chip = 2 * x + y
        sibling = (x, y, 1 - c)
        chips = _other_chips(x, y)
        sends = []
        for a in range(n):
            for j, (px, py) in enumerate(chips):
                sends.append(_remote(_half(src[a], c), _half(out[a].at[me_chip], c),
                                     send_sems.at[6 * a + j], recv_sems.at[6 * a + j], (px, py, c)))
        for j, (px, py) in enumerate(chips):
            sends.append(_remote(conv_src, conv_out.at[me_chip], send_sems.at[6 * n + j], recv_sems.at[6 * n + j], (px, py, c)))
        for cp in sends:
            cp.start()
        passed = []
        for a in range(n):
            for j, (px, py) in enumerate(chips):
                theirs = _half(out[a].at[2 * px + py], c)
                _remote(theirs, theirs, send_sems.at[6 * a + j], recv_sems.at[6 * a + j], (px, py, c)).wait_recv()
                cp = _remote(theirs, theirs, send_sems.at[6 * a + 3 + j], recv_sems.at[6 * a + 3 + j], sibling)
                cp.start()
                passed.append(cp)
        for j, (px, py) in enumerate(chips):
            theirs = conv_out.at[2 * px + py]
            _remote(theirs, theirs, send_sems.at[6 * n + j], recv_sems.at[6 * n + j], (px, py, c)).wait_recv()
        for a in range(n):
            for j, (px, py) in enumerate(chips):
                theirs = _half(out[a].at[2 * px + py], 1 - c)
                _remote(theirs, theirs, send_sems.at[6 * a + 3 + j], recv_sems.at[6 * a + 3 + j], sibling).wait_recv()
        for cp in sends + passed:
            cp.wait_send()

    return pl.pallas_call(
        body, name="gather_weights", in_specs=[HBM_SPEC] * (n + 1), out_specs=[HBM_SPEC] * (n + 1),
        out_shape=[jax.ShapeDtypeStruct((N_CHIPS,) + s.shape, s.dtype) for s in list(shards) + [conv]],
        scratch_shapes=[pltpu.SemaphoreType.DMA((6 * n + 3,)), pltpu.SemaphoreType.DMA((6 * n + 3,))],
    )(*shards, conv)


SEM_SPEC = pl.BlockSpec(memory_space=pltpu.SEMAPHORE)
SPLIT_EFFECT = pltpu.SideEffectType.DATAFLOW_SIDE_EFFECTING


def _gather_async_copies(src, land, send_sems, recv_sems, x, y, c):
    me_chip = 2 * x + y
    sends, arrivals = [], []
    for a in range(len(src)):
        for j, (px, py) in enumerate(_other_chips(x, y)):
            for core in range(2):
                sends.append(_remote(_half(src[a], c), _half(land[a].at[me_chip], c), send_sems.at[6 * a + 2 * j + core],
                                     recv_sems.at[6 * a + 2 * j + c], (px, py, core)))
                theirs = _half(land[a].at[2 * px + py], core)
                arrivals.append(_remote(theirs, theirs, send_sems.at[6 * a + 2 * j + core],
                                        recv_sems.at[6 * a + 2 * j + core], (px, py, core)))
    return sends, arrivals


def gather_weights_start(shards, after):
    n = len(shards)

    def body(*refs):
        src, land = refs[:n], refs[n:2 * n]
        send_sems, recv_sems, token = refs[2 * n + 1], refs[2 * n + 2], refs[4 * n + 3]
        x, y, c = _place()
        for cp in _gather_async_copies(src, land, send_sems, recv_sems, x, y, c)[0]:
            cp.start()
        token[...] = jnp.zeros_like(token)

    zones = [pltpu.with_memory_space_constraint(lax.empty((N_CHIPS,) + s.shape, s.dtype), pltpu.HBM) for s in shards]
    srcs = [pltpu.with_memory_space_constraint(s, pltpu.HBM) for s in shards]
    out = pl.pallas_call(
        body, name="gather_weights_start",
        out_shape=[pltpu.SemaphoreType.DMA((6 * n,)), pltpu.SemaphoreType.DMA((6 * n,))]
        + [pltpu.HBM(s.shape, s.dtype) for s in shards] + [pltpu.HBM(z.shape, z.dtype) for z in zones]
        + [jax.ShapeDtypeStruct((8, LANES), f32)],
        in_specs=[HBM_SPEC] * (2 * n) + [pl.BlockSpec(memory_space=pl.ANY)],
        out_specs=[SEM_SPEC, SEM_SPEC] + [HBM_SPEC] * (2 * n) + [pl.BlockSpec(memory_space=pltpu.VMEM)],
        input_output_aliases={i: 2 + i for i in range(2 * n)},
        compiler_params=pltpu.CompilerParams(has_side_effects=SPLIT_EFFECT),
    )(*srcs, *zones, after)
    return out[0], out[1], out[2:2 + n], out[2 + n:2 + 2 * n], out[-1]


def gather_weights_wait(send_sems, recv_sems, shards, zones, after):
    n = len(shards)

    def body(*refs):
        src, land = refs[:n], refs[n:2 * n]
        send_sems, recv_sems = refs[2 * n], refs[2 * n + 1]
        x, y, c = _place()
        sends, arrivals = _gather_async_copies(src, land, send_sems, recv_sems, x, y, c)
        for cp in sends:
            cp.wait_send()
        for cp in arrivals:
            cp.wait_recv()

    out = pl.pallas_call(
        body, name="gather_weights_wait",
        out_shape=[pltpu.HBM(s.shape, s.dtype) for s in shards] + [pltpu.HBM(z.shape, z.dtype) for z in zones],
        in_specs=[HBM_SPEC] * (2 * n) + [SEM_SPEC, SEM_SPEC, pl.BlockSpec(memory_space=pl.ANY)],
        out_specs=[HBM_SPEC] * (2 * n),
        input_output_aliases={i: i for i in range(2 * n)},
        compiler_params=pltpu.CompilerParams(has_side_effects=SPLIT_EFFECT),
    )(*shards, *zones, send_sems, recv_sems, after)
    return out[n:]


def swap_grad_halves(grads, *, name):
    n = len(grads)

    def body(*refs):
        g, land, send_sems, recv_sems = refs[:n], refs[n:2 * n], refs[2 * n], refs[2 * n + 1]
        x, y, c = _place()
        copies = [_remote(_half(g[a], 1 - c), land[a], send_sems.at[a], recv_sems.at[a], (x, y, 1 - c)) for a in range(n)]
        for cp in copies:
            cp.start()
        for cp in copies:
            cp.wait()

    return pl.pallas_call(
        body, name=name, in_specs=[HBM_SPEC] * n, out_specs=[HBM_SPEC] * n,
        out_shape=[jax.ShapeDtypeStruct((N_CHIPS, g.shape[1] // 2, g.shape[2]), g.dtype) for g in grads],
        scratch_shapes=[pltpu.SemaphoreType.DMA((n,)), pltpu.SemaphoreType.DMA((n,))],
    )(*grads)


GRAD_ROWS = 256


def add_grad_halves(g, land, core, *, name):
    _, half, cols = land.shape
    tr = GRAD_ROWS if half % GRAD_ROWS == 0 else half
    nb = half // tr

    def body(c_ref, g_ref, l_ref, o_ref):
        o_ref[...] = (g_ref[...].astype(f32) + l_ref[...].astype(f32)).astype(o_ref.dtype)

    blk = (1, tr, cols)
    return pl.pallas_call(
        body, name=name,
        grid_spec=pltpu.PrefetchScalarGridSpec(
            num_scalar_prefetch=1, grid=(N_CHIPS, nb),
            in_specs=[pl.BlockSpec(blk, lambda k, i, c_ref: (k, c_ref[0] * nb + i, 0)),
                      pl.BlockSpec(blk, lambda k, i, c_ref: (k, i, 0))],
            out_specs=pl.BlockSpec(blk, lambda k, i, c_ref: (k, i, 0))),
        out_shape=jax.ShapeDtypeStruct(land.shape, land.dtype),
        compiler_params=_cparams("parallel", "parallel"),
    )(core, g, land)


def scatter_grads(parts):
    n = len(parts)

    def body(*refs):
        p, land, send_sems, recv_sems = refs[:n], refs[n:2 * n], refs[2 * n], refs[2 * n + 1]
        x, y, c = _place()
        me_chip = 2 * x + y
        chips = _other_chips(x, y)
        sends = [_remote(p[a].at[2 * px + py], land[a].at[me_chip], send_sems.at[3 * a + j], recv_sems.at[3 * a + j], (px, py, c))
                 for a in range(n) for j, (px, py) in enumerate(chips)]
        for cp in sends:
            cp.start()
        for a in range(n):
            for j, (px, py) in enumerate(chips):
                slot = land[a].at[2 * px + py]
                _remote(slot, slot, send_sems.at[3 * a + j], recv_sems.at[3 * a + j], (px, py, c)).wait_recv()
        for cp in sends:
            cp.wait_send()

    return pl.pallas_call(
        body, name="scatter_grads", in_specs=[HBM_SPEC] * n, out_specs=[HBM_SPEC] * n,
        out_shape=[jax.ShapeDtypeStruct(p.shape, p.dtype) for p in parts],
        scratch_shapes=[pltpu.SemaphoreType.DMA((3 * n,)), pltpu.SemaphoreType.DMA((3 * n,))],
    )(*parts)


def _scatter_async_copies(parts, land, send_sems, recv_sems, x, y, c):
    me_chip = 2 * x + y
    sends, arrivals = [], []
    for a in range(len(parts)):
        for j, (px, py) in enumerate(_other_chips(x, y)):
            sems = (send_sems.at[3 * a + j], recv_sems.at[3 * a + j], (px, py, c))
            sends.append(_remote(parts[a].at[2 * px + py], land[a].at[me_chip], *sems))
            slot = land[a].at[2 * px + py]
            arrivals.append(_remote(slot, slot, *sems))
    return sends, arrivals


def scatter_grads_start(parts, *, name):
    n = len(parts)

    def body(*refs):
        p, land = refs[:n], refs[n:2 * n]
        send_sems, recv_sems, token = refs[2 * n], refs[2 * n + 1], refs[4 * n + 2]
        x, y, c = _place()
        for cp in _scatter_async_copies(p, land, send_sems, recv_sems, x, y, c)[0]:
            cp.start()
        token[...] = jnp.zeros_like(token)

    zones = [pltpu.with_memory_space_constraint(lax.empty(p.shape, p.dtype), pltpu.HBM) for p in parts]
    srcs = [pltpu.with_memory_space_constraint(p, pltpu.HBM) for p in parts]
    hbm = [pltpu.HBM(p.shape, p.dtype) for p in parts]
    out = pl.pallas_call(
        body, name=name,
        out_shape=[pltpu.SemaphoreType.DMA((3 * n,)), pltpu.SemaphoreType.DMA((3 * n,))] + hbm + hbm
        + [jax.ShapeDtypeStruct((8, LANES), f32)],
        in_specs=[HBM_SPEC] * (2 * n),
        out_specs=[SEM_SPEC, SEM_SPEC] + [HBM_SPEC] * (2 * n) + [pl.BlockSpec(memory_space=pltpu.VMEM)],
        input_output_aliases={i: 2 + i for i in range(2 * n)},
        compiler_params=pltpu.CompilerParams(has_side_effects=SPLIT_EFFECT),
    )(*srcs, *zones)
    return out[0], out[1], out[2:2 + n], out[2 + n:2 + 2 * n], out[-1]


def scatter_grads_wait(send_sems, recv_sems, parts, zones, after, *, name):
    n = len(parts)

    def body(*refs):
        p, land = refs[:n], refs[n:2 * n]
        x, y, c = _place()
        sends, arrivals = _scatter_async_copies(p, land, refs[2 * n], refs[2 * n + 1], x, y, c)
        for cp in sends:
            cp.wait_send()
        for cp in arrivals:
            cp.wait_recv()

    hbm = [pltpu.HBM(p.shape, p.dtype) for p in parts]
    out = pl.pallas_call(
        body, name=name, out_shape=hbm + hbm,
        in_specs=[HBM_SPEC] * (2 * n) + [SEM_SPEC, SEM_SPEC, pl.BlockSpec(memory_space=pl.ANY)],
        out_specs=[HBM_SPEC] * (2 * n),
        input_output_aliases={i: i for i in range(2 * n)},
        compiler_params=pltpu.CompilerParams(has_side_effects=SPLIT_EFFECT),
    )(*parts, *zones, send_sems, recv_sems, after)
    return out[:n], out[n:]


def sum_grads(part, land, order, *, name):
    _, half, cols = part.shape
    tr = GRAD_ROWS if half % GRAD_ROWS == 0 else half

    def body(order_ref, p_ref, l1_ref, l2_ref, l3_ref, o_ref):
        o_ref[...] = ((p_ref[0].astype(f32) + l1_ref[0].astype(f32)) + l2_ref[0].astype(f32)) + l3_ref[0].astype(f32)

    slot = lambda j: pl.BlockSpec((1, tr, cols), lambda i, order_ref: (order_ref[j], i, 0))
    return pl.pallas_call(
        body, name=name,
        grid_spec=pltpu.PrefetchScalarGridSpec(
            num_scalar_prefetch=1, grid=(half // tr,), in_specs=[slot(0), slot(1), slot(2), slot(3)],
            out_specs=pl.BlockSpec((tr, cols), lambda i, order_ref: (i, 0))),
        out_shape=jax.ShapeDtypeStruct((half, cols), f32),
        compiler_params=_cparams("parallel"),
    )(order, part, land, land, land)


def _peer(x, y, c, r):
    return ((1 - x) if r & 4 else x, (1 - y) if r & 2 else y, (1 - c) if r & 1 else c)


def _reduce_async_copies(grads, land, send_sems, recv_sems, x, y, c):
    me = 4 * x + 2 * y + c
    sends, arrivals = [], []
    for a in range(len(grads)):
        for r in range(1, N_DEV):
            px, py, pc = _peer(x, y, c, r)
            sems = (send_sems.at[7 * a + r - 1], recv_sems.at[7 * a + r - 1], (px, py, pc))
            sends.append(_remote(_half(grads[a].at[2 * px + py], pc), land[a].at[me], *sems))
            slot = land[a].at[4 * px + 2 * py + pc]
            arrivals.append(_remote(slot, slot, *sems))
    return sends, arrivals


def reduce_grads_start(grads, *, name):
    n = len(grads)

    def body(*refs):
        g, land = refs[:n], refs[n:2 * n]
        send_sems, recv_sems, token = refs[2 * n], refs[2 * n + 1], refs[4 * n + 2]
        x, y, c = _place()
        for cp in _reduce_async_copies(g, land, send_sems, recv_sems, x, y, c)[0]:
            cp.start()
        token[...] = jnp.zeros_like(token)

    zones = [pltpu.with_memory_space_constraint(lax.empty((N_DEV, g.shape[1] // 2, g.shape[2]), g.dtype), pltpu.HBM)
             for g in grads]
    srcs = [pltpu.with_memory_space_constraint(g, pltpu.HBM) for g in grads]
    out = pl.pallas_call(
        body, name=name,
        out_shape=[pltpu.SemaphoreType.DMA((7 * n,)), pltpu.SemaphoreType.DMA((7 * n,))]
        + [pltpu.HBM(g.shape, g.dtype) for g in grads] + [pltpu.HBM(z.shape, z.dtype) for z in zones]
        + [jax.ShapeDtypeStruct((8, LANES), f32)],
        in_specs=[HBM_SPEC] * (2 * n),
        out_specs=[SEM_SPEC, SEM_SPEC] + [HBM_SPEC] * (2 * n) + [pl.BlockSpec(memory_space=pltpu.VMEM)],
        input_output_aliases={i: 2 + i for i in range(2 * n)},
        compiler_params=pltpu.CompilerParams(has_side_effects=SPLIT_EFFECT),
    )(*srcs, *zones)
    return out[0], out[1], out[2:2 + n], out[2 + n:2 + 2 * n], out[-1]


def reduce_grads_wait(send_sems, recv_sems, grads, zones, after, *, name):
    n = len(grads)

    def body(*refs):
        g, land = refs[:n], refs[n:2 * n]
        x, y, c = _place()
        sends, arrivals = _reduce_async_copies(g, land, refs[2 * n], refs[2 * n + 1], x, y, c)
        for cp in sends:
            cp.wait_send()
        for cp in arrivals:
            cp.wait_recv()

    hbm = [pltpu.HBM(a.shape, a.dtype) for a in list(grads) + list(zones)]
    out = pl.pallas_call(
        body, name=name, out_shape=hbm,
        in_specs=[HBM_SPEC] * (2 * n) + [SEM_SPEC, SEM_SPEC, pl.BlockSpec(memory_space=pl.ANY)],
        out_specs=[HBM_SPEC] * (2 * n),
        input_output_aliases={i: i for i in range(2 * n)},
        compiler_params=pltpu.CompilerParams(has_side_effects=SPLIT_EFFECT),
    )(*grads, *zones, send_sems, recv_sems, after)
    return out[:n], out[n:]


def sum_partials(g, land, where, *, name):
    _, half, cols = land.shape
    tr = GRAD_ROWS if half % GRAD_ROWS == 0 else half
    nb = half // tr

    def body(where_ref, g_ref, *rest):
        o_ref = rest[-1]
        acc = g_ref[0].astype(f32)
        for l_ref in rest[:-1]:
            acc = acc + l_ref[0].astype(f32)
        o_ref[...] = acc

    blk = (1, tr, cols)
    slot = lambda j: pl.BlockSpec(blk, lambda i, where_ref: (where_ref[2 + j], i, 0))
    return pl.pallas_call(
        body, name=name,
        grid_spec=pltpu.PrefetchScalarGridSpec(
            num_scalar_prefetch=1, grid=(nb,),
            in_specs=[pl.BlockSpec(blk, lambda i, where_ref: (where_ref[0], where_ref[1] * nb + i, 0))]
            + [slot(j) for j in range(N_DEV - 1)],
            out_specs=pl.BlockSpec((tr, cols), lambda i, where_ref: (i, 0))),
        out_shape=jax.ShapeDtypeStruct((half, cols), f32),
        compiler_params=_cparams("parallel"),
    )(where, g, *([land] * (N_DEV - 1)))


def swap_reduced_halves(mine, *, name):
    n = len(mine)

    def body(*refs):
        r, out, send_sems, recv_sems = refs[:n], refs[n:2 * n], refs[2 * n], refs[2 * n + 1]
        x, y, c = _place()
        copies = [_remote(r[a], out[a], send_sems.at[a], recv_sems.at[a], (x, y, 1 - c)) for a in range(n)]
        for cp in copies:
            cp.start()
        for cp in copies:
            cp.wait()

    return pl.pallas_call(
        body, name=name, in_specs=[HBM_SPEC] * n, out_specs=[HBM_SPEC] * n,
        out_shape=[jax.ShapeDtypeStruct(r.shape, r.dtype) for r in mine],
        scratch_shapes=[pltpu.SemaphoreType.DMA((n,)), pltpu.SemaphoreType.DMA((n,))],
    )(*mine)


def adamw_halves(w, mine, theirs, m, v, core, *, name):
    R, C = w.shape
    tr = min(GRAD_ROWS, R // 2)
    half_nb = R // 2 // tr

    def body(c_ref, w_ref, a_ref, b_ref, m_ref, v_ref, g_ref, d_ref, nm_ref, nv_ref):
        low = pl.program_id(0) < half_nb
        gv = jnp.where(low == (c_ref[0] == 0), a_ref[...], b_ref[...])
        nm = ADAM_B1 * m_ref[...] + (1.0 - ADAM_B1) * gv
        nv = ADAM_B2 * v_ref[...] + (1.0 - ADAM_B2) * jnp.square(gv)
        m_hat = nm / (1.0 - ADAM_B1 ** ADAM_STEP)
        v_hat = nv / (1.0 - ADAM_B2 ** ADAM_STEP)
        g_ref[...] = gv
        d_ref[...] = -ADAM_LR * (m_hat / (jnp.sqrt(v_hat) + ADAM_EPS) + ADAM_WD * w_ref[...])
        nm_ref[...] = nm
        nv_ref[...] = nv

    full = pl.BlockSpec((tr, C), lambda i, c_ref: (i, 0))
    part = pl.BlockSpec((tr, C), lambda i, c_ref: (i % half_nb, 0))
    out = jax.ShapeDtypeStruct((R, C), f32)
    return pl.pallas_call(
        body, name=name,
        grid_spec=pltpu.PrefetchScalarGridSpec(
            num_scalar_prefetch=1, grid=(2 * half_nb,), in_specs=[full, part, part, full, full], out_specs=[full] * 4),
        out_shape=[out] * 4, compiler_params=_cparams("parallel"),
    )(core, w, mine, theirs, m, v)


N_DEV = 8


def all_reduce_small(v):
    def body(src_ref, out_ref, land_ref, send_sems, recv_sems):
        x, y, c = _place()
        me = 4 * x + 2 * y + c
        copies = []
        for r in range(1, N_DEV):
            peer = ((1 - x) if r & 4 else x, (1 - y) if r & 2 else y, (1 - c) if r & 1 else c)
            copies.append(_remote(src_ref, land_ref.at[r], send_sems.at[r - 1], recv_sems.at[r - 1], peer))
        for cp in copies:
            cp.start()
        land_ref[0] = src_ref[...]
        for cp in copies:
            cp.wait()
        acc = land_ref[me]
        for d in range(1, N_DEV):
            acc = acc + land_ref[jnp.bitwise_xor(me, d)]
        out_ref[...] = acc

    vm = pl.BlockSpec(memory_space=pltpu.VMEM)
    return pl.pallas_call(
        body, name="all_reduce_small", in_specs=[vm], out_specs=vm,
        out_shape=jax.ShapeDtypeStruct(v.shape, v.dtype),
        scratch_shapes=[pltpu.VMEM((N_DEV,) + v.shape, v.dtype),
                        pltpu.SemaphoreType.DMA((N_DEV - 1,)), pltpu.SemaphoreType.DMA((N_DEV - 1,))],
    )(v)


def adamw(w, g, m, v, *, name, tr=None, tc=None):
    R, C = w.shape
    if tc is None:
        tr, tc = min(tr, R), C
        blk = pl.BlockSpec((tr, C), lambda i: (i, 0))
    else:
        tr = R
        blk = pl.BlockSpec((R, tc), lambda i: (0, i))

    def body(w_ref, g_ref, m_ref, v_ref, d_ref, nm_ref, nv_ref):
        gv = g_ref[...]
        nm = ADAM_B1 * m_ref[...] + (1.0 - ADAM_B1) * gv
        nv = ADAM_B2 * v_ref[...] + (1.0 - ADAM_B2) * jnp.square(gv)
        m_hat = nm / (1.0 - ADAM_B1 ** ADAM_STEP)
        v_hat = nv / (1.0 - ADAM_B2 ** ADAM_STEP)
        d_ref[...] = -ADAM_LR * (m_hat / (jnp.sqrt(v_hat) + ADAM_EPS) + ADAM_WD * w_ref[...])
        nm_ref[...] = nm
        nv_ref[...] = nv

    out = jax.ShapeDtypeStruct((R, C), f32)
    return pl.pallas_call(
        body, name=name, grid=((R // tr) * (C // tc),), in_specs=[blk] * 4, out_specs=[blk] * 3, out_shape=[out] * 3,
        compiler_params=_cparams("parallel"),
    )(w, g, m, v)


BIG_SHARDS = (("w_in", (1024, 900), True), ("w_out", (256, 1024), False), ("w_cq", (256, 512), False),
              ("w_ckv", (256, 1024), False), ("w_co", (512, 256), True), ("w_mlp1", (1024, 1024), True),
              ("w_mlp2", (1024, 1024), False))
CONV_SHARD = (CONV_WIDTH, 3 * GDN_WIDTH // N_CHIPS)
SMALL_DIMS = (("norm_mix_g", 1024), ("fox_qnorm_g", 64), ("fox_knorm_g", 64), ("fox_f_bias", 8), ("fox_onorm_g", 64),
              ("gdn_A_log", 4), ("gdn_dt_bias", 4), ("gdn_onorm_g", 128), ("norm_xattn_g", 1024), ("mem_norm_g", 1024),
              ("xattn_qnorm_g", 128), ("xattn_knorm_g", 128), ("norm_mlp_g", 1024))
WEIGHT_ORDER = ("norm_mix_g", "w_in", "fox_qnorm_g", "fox_knorm_g", "fox_f_bias", "fox_onorm_g", "gdn_conv_w", "gdn_A_log",
                "gdn_dt_bias", "gdn_onorm_g", "w_out", "norm_xattn_g", "mem_norm_g", "w_cq", "w_ckv", "xattn_qnorm_g",
                "xattn_knorm_g", "w_co", "norm_mlp_g", "w_mlp1", "w_mlp2")


def _pack_rows(pieces, rows, lead=()):
    cat = jnp.concatenate([p.reshape(lead + (-1,)) for p in pieces], axis=-1)
    cat = jnp.pad(cat, [(0, 0)] * len(lead) + [(0, rows * LANES - cat.shape[-1])])
    return cat.reshape(lead + (rows, LANES))


def _unpack_rows(buf, sizes, lead=()):
    flat = buf.reshape(lead + (-1,))
    out, off = [], 0
    for n in sizes:
        out.append(flat[..., off:off + n])
        off += n
    return out


def _conv_to_wire(conv):
    return lax.bitcast_convert_type(conv, bf16)


def _conv_from_wire(wire):
    return lax.bitcast_convert_type(wire, f32)


SMALL_ROWS = 96
SMALL_ADAM_ROWS = 56


def kernel(x, mem, norm_mix_g, w_in, fox_qnorm_g, fox_knorm_g, fox_f_bias, fox_onorm_g, gdn_conv_w, gdn_A_log, gdn_dt_bias, gdn_onorm_g, w_out, norm_xattn_g, mem_norm_g, w_cq, w_ckv, xattn_qnorm_g, xattn_knorm_g, w_co, norm_mlp_g, w_mlp1, w_mlp2, loss_target, m_norm_mix_g, m_w_in, m_fox_qnorm_g, m_fox_knorm_g, m_fox_f_bias, m_fox_onorm_g, m_gdn_conv_w, m_gdn_A_log, m_gdn_dt_bias, m_gdn_onorm_g, m_w_out, m_norm_xattn_g, m_mem_norm_g, m_w_cq, m_w_ckv, m_xattn_qnorm_g, m_xattn_knorm_g, m_w_co, m_norm_mlp_g, m_w_mlp1, m_w_mlp2, v_norm_mix_g, v_w_in, v_fox_qnorm_g, v_fox_knorm_g, v_fox_f_bias, v_fox_onorm_g, v_gdn_conv_w, v_gdn_A_log, v_gdn_dt_bias, v_gdn_onorm_g, v_w_out, v_norm_xattn_g, v_mem_norm_g, v_w_cq, v_w_ckv, v_xattn_qnorm_g, v_xattn_knorm_g, v_w_co, v_norm_mlp_g, v_w_mlp1, v_w_mlp2):
    wts = dict(norm_mix_g=norm_mix_g, w_in=w_in, fox_qnorm_g=fox_qnorm_g, fox_knorm_g=fox_knorm_g, fox_f_bias=fox_f_bias,
               fox_onorm_g=fox_onorm_g, gdn_conv_w=gdn_conv_w, gdn_A_log=gdn_A_log, gdn_dt_bias=gdn_dt_bias,
               gdn_onorm_g=gdn_onorm_g, w_out=w_out, norm_xattn_g=norm_xattn_g, mem_norm_g=mem_norm_g, w_cq=w_cq, w_ckv=w_ckv,
               xattn_qnorm_g=xattn_qnorm_g, xattn_knorm_g=xattn_knorm_g, w_co=w_co, norm_mlp_g=norm_mlp_g, w_mlp1=w_mlp1,
               w_mlp2=w_mlp2)
    mom = dict(norm_mix_g=m_norm_mix_g, w_in=m_w_in, fox_qnorm_g=m_fox_qnorm_g, fox_knorm_g=m_fox_knorm_g,
               fox_f_bias=m_fox_f_bias, fox_onorm_g=m_fox_onorm_g, gdn_conv_w=m_gdn_conv_w, gdn_A_log=m_gdn_A_log,
               gdn_dt_bias=m_gdn_dt_bias, gdn_onorm_g=m_gdn_onorm_g, w_out=m_w_out, norm_xattn_g=m_norm_xattn_g,
               mem_norm_g=m_mem_norm_g, w_cq=m_w_cq, w_ckv=m_w_ckv, xattn_qnorm_g=m_xattn_qnorm_g,
               xattn_knorm_g=m_xattn_knorm_g, w_co=m_w_co, norm_mlp_g=m_norm_mlp_g, w_mlp1=m_w_mlp1, w_mlp2=m_w_mlp2)
    var = dict(norm_mix_g=v_norm_mix_g, w_in=v_w_in, fox_qnorm_g=v_fox_qnorm_g, fox_knorm_g=v_fox_knorm_g,
               fox_f_bias=v_fox_f_bias, fox_onorm_g=v_fox_onorm_g, gdn_conv_w=v_gdn_conv_w, gdn_A_log=v_gdn_A_log,
               gdn_dt_bias=v_gdn_dt_bias, gdn_onorm_g=v_gdn_onorm_g, w_out=v_w_out, norm_xattn_g=v_norm_xattn_g,
               mem_norm_g=v_mem_norm_g, w_cq=v_w_cq, w_ckv=v_w_ckv, xattn_qnorm_g=v_xattn_qnorm_g,
               xattn_knorm_g=v_xattn_knorm_g, w_co=v_w_co, norm_mlp_g=v_norm_mlp_g, w_mlp1=v_w_mlp1, w_mlp2=v_w_mlp2)
    B, S, D = x.shape
    T = B * S
    big_names = [n for n, _, _ in BIG_SHARDS]
    chip = 2 * lax.axis_index("x") + lax.axis_index("y")
    core = lax.axis_index("c").astype(jnp.int32).reshape(1)

    shards = {n: wts[n][0].astype(MXU_DTYPE) for n in big_names[1:]}
    in_t = lambda p: jnp.swapaxes(p[0], 0, 1)
    shards["w_in"] = jnp.pad(in_t(w_in).astype(MXU_DTYPE), ((0, IN_SHARD_PAD - IN_SHARD), (0, 0)))
    w_in_all, conv_all = gather_weights([shards["w_in"]], gdn_conv_w[0])
    late = big_names[1:]
    send_sems, recv_sems, late_src, late_zones, token = gather_weights_start([shards[n] for n in late], conv_all)
    own = lambda g, s: lax.dynamic_update_slice(g, s[None], (chip,) + (0,) * s.ndim)
    full = {"w_in": own(w_in_all, shards["w_in"])}
    conv_full = own(conv_all, gdn_conv_w[0]).transpose(1, 0, 2).reshape(CONV_WIDTH, 3 * GDN_WIDTH)
    rows = lambda g: g.reshape(N_CHIPS * g.shape[1], g.shape[2])
    w_in_t = full["w_in"][:, :IN_SHARD].reshape(IN_DIM, D_MODEL)

    def late_weights(after):
        zones = gather_weights_wait(send_sems, recv_sems, late_src, late_zones, after)
        got = {n: own(z, shards[n]) for n, z in zip(late, zones)}
        return dict(w_out=rows(got["w_out"]), w_cq=rows(got["w_cq"]), w_ckv=rows(got["w_ckv"]), w_co=got["w_co"],
                    w_mlp1=got["w_mlp1"], w_mlp2=rows(got["w_mlp2"]))

    in_flight = []

    def grads_ready(ready):
        names = list(ready)
        *started, tok = reduce_grads_start([ready[n] for n in names], name="reduce_grads_start_%d" % len(in_flight))
        in_flight.append((names, *started))
        return tok

    w = dict(wa_t=align_w_in_t(w_in_t), conv_w=conv_full, late=late_weights, grads_ready=grads_ready)
    sp = {n: wts[n] for n, _ in SMALL_DIMS}
    sp["norm_mix_g"] = sp["norm_mix_g"] + token[0, 0]

    loss_part, grad_x, g_big, g_small = local_step(x.reshape(T, D), mem.reshape(-1, D), loss_target.reshape(T, D), w, sp, B=B)

    small_pieces = [g_small[n] for n, _ in SMALL_DIMS] + [g_small["gdn_conv_w"], loss_part]
    small_sizes = [d for _, d in SMALL_DIMS] + [CONV_WIDTH * 3 * GDN_WIDTH, LANES]
    red_small = _unpack_rows(all_reduce_small(_pack_rows(small_pieces, SMALL_ROWS)), small_sizes)
    grads = {n: p.reshape(1, d) for (n, d), p in zip(SMALL_DIMS, red_small)}
    conv_grad = lax.dynamic_slice(red_small[-2].reshape(CONV_WIDTH, 3 * GDN_WIDTH), (0, chip * CONV_SHARD[1]), CONV_SHARD)
    grads["gdn_conv_w"] = conv_grad.reshape((1,) + CONV_SHARD)
    loss = red_small[-1][0]

    tok_in = grads_ready({"w_in": g_big["w_in"]})
    parts, zones = {}, {}

    def wait_group(k, after):
        names, send_sems, recv_sems, thru, land = in_flight[k]
        thru, land = reduce_grads_wait(send_sems, recv_sems, thru, land, after, name="reduce_grads_wait_%d" % k)
        parts.update(zip(names, thru))
        zones.update(zip(names, land))

    wait_group(0, tok_in)
    wait_group(1, tok_in)
    dev = 2 * chip + core[0]
    where = jnp.stack([chip, core[0]] + [dev ^ r for r in range(1, N_DEV)]).astype(jnp.int32)
    mine = [sum_partials(parts[n], zones[n], where, name="sum_partials_" + n) for n in late]
    theirs = swap_reduced_halves(mine, name="swap_reduced_halves")

    delta, new_m, new_v = {}, {}, {}
    for n, a, b in zip(late, mine, theirs):
        g, d, nm, nv = adamw_halves(wts[n][0], a, b, mom[n][0], var[n][0], core, name="adamw_" + n)
        grads[n], delta[n], new_m[n], new_v[n] = g[None], d[None], nm[None], nv[None]
    wait_group(2, new_v[late[-1]])
    mine_in = sum_partials(parts["w_in"], zones["w_in"], where, name="sum_partials_w_in")
    (theirs_in,) = swap_reduced_halves([mine_in], name="swap_reduced_halves_w_in")
    south = core[0] == 0
    g_in_t = jnp.concatenate([jnp.where(south, mine_in, theirs_in), jnp.where(south, theirs_in, mine_in)])[:IN_SHARD]
    back = lambda t: jnp.swapaxes(t, 0, 1)[None]
    d, nm, nv = adamw(in_t(w_in), g_in_t, in_t(m_w_in), in_t(v_w_in), name="adamw_w_in", tc=256)
    grads["w_in"], delta["w_in"], new_m["w_in"], new_v["w_in"] = back(g_in_t), back(d), back(nm), back(nv)
    small_names = [n for n, _ in SMALL_DIMS] + ["gdn_conv_w"]
    small_sz = [d for _, d in SMALL_DIMS] + [CONV_SHARD[0] * CONV_SHARD[1]]
    packed4 = [_pack_rows([src[n] for n in small_names], SMALL_ADAM_ROWS) for src in (wts, grads, mom, var)]
    outs = adamw(*packed4, name="adamw_small", tr=SMALL_ADAM_ROWS)
    for dst, buf in zip((delta, new_m, new_v), outs):
        for n, p in zip(small_names, _unpack_rows(buf, small_sz)):
            dst[n] = p.reshape(wts[n].shape)

    return (loss, grad_x.reshape(B, S, D), *[grads[n] for n in WEIGHT_ORDER], *[delta[n] for n in WEIGHT_ORDER],
            *[new_m[n] for n in WEIGHT_ORDER], *[new_v[n] for n in WEIGHT_ORDER])
```

```python
import functools

import jax
import jax.numpy as jnp
import numpy as np
from jax import lax
from jax.experimental import pallas as pl
from jax.experimental.pallas import tpu as pltpu

f32 = jnp.float32
bf16 = jnp.bfloat16
MXU_DTYPE = jnp.bfloat16
WIRE_DTYPE = jnp.bfloat16
INV_PRECISION = lax.Precision.HIGH

D_MODEL = 1024
FOX_HEADS = 8
FOX_HEAD_DIM = 64
FOX_WIDTH = 512
GDN_HEADS = 4
GDN_HEAD_DIM = 128
GDN_WIDTH = 512
CONV_WIDTH = 4
GDN_CHUNK = 64
XATTN_HEADS = 4
XATTN_HEAD_DIM = 128
XATTN_WIDTH = 512
D_FF = 4096
IN_DIM = 3600
EPS = 1e-6
NEG_INF = -1e30
LANES = 128
ADAM_LR = 0.001
ADAM_B1 = 0.9
ADAM_B2 = 0.999
ADAM_EPS = 1e-08
ADAM_WD = 0.01
ADAM_STEP = 10
VMEM_LIMIT = 48 * 1024 * 1024

COL_FOX = 0
COL_GDN = 1536
COL_Z = 3072
COL_SMALL = 3584
IN_ALIGNED = 3840
IN_TILE = 768
SM_F = 0
SM_B = 8
SM_A = 12


def _cparams(*sem):
    return pltpu.CompilerParams(dimension_semantics=sem, vmem_limit_bytes=VMEM_LIMIT)


def _mx(v):
    return v.astype(MXU_DTYPE)


def _dot(a, b, dims, precision=None):
    return lax.dot_general(a, b, (dims, ((), ())), preferred_element_type=f32, precision=precision)


def _dotm(a, b, dims):
    return _dot(_mx(a), _mx(b), dims)


NN = ((1,), (0,))
NT = ((1,), (1,))
TN = ((0,), (0,))


def matmul(a, b, *, name, ta=False, tb=False, b_stacked=False, out_stacked=False, residual=None, relu2_out=False,
           relu2_bwd_aux=None, out_dtype=f32, tm=1024, tn=1024, tk=1024):
    M, K = (a.shape[1], a.shape[0]) if ta else a.shape
    if b_stacked:
        b_cols = b.shape[2]
        N, tk = (b.shape[1], min(tk, b_cols)) if tb else (N_CHIPS * b_cols, tk)
        tn = tn if tb else min(tn, b_cols)
        assert K == (N_CHIPS * b_cols if tb else b.shape[1]), (name, a.shape, b.shape)
    else:
        N = b.shape[0] if tb else b.shape[1]
    if out_stacked:
        tn = min(tn, N // N_CHIPS)
    tm, tn, tk = min(tm, M), min(tn, N), min(tk, K)
    assert M % tm == 0 and N % tn == 0 and K % tk == 0, (name, M, N, K)
    nk = K // tk
    has_res = residual is not None
    has_aux = relu2_bwd_aux is not None

    def body(*refs):
        a_ref, b_ref = refs[0], refs[1]
        pos = 2
        res_ref = aux_ref = None
        if has_res:
            res_ref = refs[pos]
            pos += 1
        if has_aux:
            aux_ref = refs[pos]
            pos += 1
        o_ref = refs[pos]
        k = pl.program_id(2)
        dims = ((0,) if ta else (1,), (1,) if tb else (0,))
        part = _dot(_mx(a_ref[...]), _mx(b_ref[...]), dims)

        def finish(r):
            if has_res:
                r = r + res_ref[...]
            if has_aux:
                r = r * (2.0 * jnp.sqrt(aux_ref[...].astype(f32)))
            if relu2_out:
                o_ref[...] = jnp.square(jnp.maximum(r, 0.0)).astype(o_ref.dtype)
            else:
                o_ref[...] = r.astype(o_ref.dtype)

        if nk == 1:
            finish(part)
            return
        acc_ref = refs[pos + 1]

        @pl.when(k == 0)
        def _():
            acc_ref[...] = part

        @pl.when((k > 0) & (k < nk - 1))
        def _():
            acc_ref[...] += part

        @pl.when(k == nk - 1)
        def _():
            finish(acc_ref[...] + part)

    a_spec = pl.BlockSpec((tk, tm), lambda i, j, k: (k, i)) if ta else pl.BlockSpec((tm, tk), lambda i, j, k: (i, k))
    if b_stacked and tb:
        per = b_cols // tk
        b_spec = pl.BlockSpec((None, tn, tk), lambda i, j, k: (k // per, j, k % per))
    elif b_stacked:
        per = b_cols // tn
        b_spec = pl.BlockSpec((None, tk, tn), lambda i, j, k: (j // per, k, j % per))
    else:
        b_spec = pl.BlockSpec((tn, tk), lambda i, j, k: (j, k)) if tb else pl.BlockSpec((tk, tn), lambda i, j, k: (k, j))
    if out_stacked:
        assert not (has_res or has_aux or relu2_out), name
        per_o = N // N_CHIPS // tn
        o_spec = pl.BlockSpec((None, tm, tn), lambda i, j, k: (j // per_o, i, j % per_o))
        out_full = (N_CHIPS, M, N // N_CHIPS)
    else:
        o_spec = pl.BlockSpec((tm, tn), lambda i, j, k: (i, j))
        out_full = (M, N)
    in_specs, args = [a_spec, b_spec], [a, b]
    if has_res:
        in_specs.append(o_spec)
        args.append(residual)
    if has_aux:
        in_specs.append(o_spec)
        args.append(relu2_bwd_aux)
    out_shape = [jax.ShapeDtypeStruct(out_full, out_dtype)]
    out_specs = [o_spec]
    res = pl.pallas_call(
        body, name=name, grid=(M // tm, N // tn, nk), in_specs=in_specs, out_specs=out_specs, out_shape=out_shape,
        scratch_shapes=[pltpu.VMEM((tm, tn), f32)] if nk > 1 else [],
        compiler_params=_cparams("parallel", "parallel", "arbitrary"),
    )(*args)
    return res[0]


def matmul_rows(a, b, extras, *, name, mode, tb=False, b_stacked=False, tm=1024, tk=1024):
    M, K = a.shape
    N = D_MODEL
    if b_stacked:
        assert tb, name
        tk = min(tk, b.shape[2])
        per = b.shape[2] // tk
        b_spec = pl.BlockSpec((None, N, tk), lambda i, k: (k // per, 0, k % per))
    elif tb:
        tk = min(tk, K)
        b_spec = pl.BlockSpec((N, tk), lambda i, k: (0, k))
    else:
        tk = min(tk, K)
        b_spec = pl.BlockSpec((tk, N), lambda i, k: (k, 0))
    tm = min(tm, M)
    assert M % tm == 0 and K % tk == 0, (name, M, K)
    nk = K // tk
    extras = [e for e in extras if e is not None]
    n_ex = len(extras)

    def body(*refs):
        a_ref, b_ref = refs[0], refs[1]
        ex = refs[2:2 + n_ex]
        o_ref = refs[2 + n_ex]
        n_out = 3 if mode == "loss" else 2
        s_ref = refs[1 + n_ex + n_out]
        i, k = pl.program_id(0), pl.program_id(1)
        part = _dot(_mx(a_ref[...]), _mx(b_ref[...]), ((1,), (1,) if tb else (0,)))

        def finish(y):
            @pl.when(i == 0)
            def _():
                s_ref[...] = jnp.zeros_like(s_ref)

            if mode == "rms_bwd":
                xv, gv = ex[0][...], ex[1][...]
                rstd = lax.rsqrt(jnp.mean(xv * xv, axis=-1, keepdims=True) + EPS)
                xhat = xv * rstd
                gd = y * gv
                dx = rstd * (gd - xhat * jnp.mean(gd * xhat, axis=-1, keepdims=True))
                o_ref[...] = dx + ex[2][...] if n_ex == 3 else dx
                s_ref[...] += jnp.sum(y * xhat, axis=0, keepdims=True)
            else:
                e = y + ex[0][...] - ex[1][...]
                o_ref[...] = e * (1.0 / N)
                refs[3 + n_ex][...] = (e * (1.0 / N)).astype(MXU_DTYPE)
                tot = 0.5 * jnp.sum(jnp.mean(e * e, axis=-1, keepdims=True), axis=0, keepdims=True)
                s_ref[...] += jnp.broadcast_to(tot, s_ref.shape)

        if nk == 1:
            finish(part)
            return
        acc_ref = refs[2 + n_ex + n_out]

        @pl.when(k == 0)
        def _():
            acc_ref[...] = part

        @pl.when((k > 0) & (k < nk - 1))
        def _():
            acc_ref[...] += part

        @pl.when(k == nk - 1)
        def _():
            finish(acc_ref[...] + part)

    row = pl.BlockSpec((tm, N), lambda i, k: (i, 0))
    vec = pl.BlockSpec((1, N), lambda i, k: (0, 0))
    if mode == "rms_bwd":
        ex_specs = [row, vec] + ([row] if n_ex == 3 else [])
        s_shape, s_spec = jax.ShapeDtypeStruct((1, N), f32), vec
    else:
        ex_specs = [row, row]
        s_shape, s_spec = jax.ShapeDtypeStruct((1, LANES), f32), pl.BlockSpec((1, LANES), lambda i, k: (0, 0))
    return pl.pallas_call(
        body, name=name, grid=(M // tm, nk),
        in_specs=[pl.BlockSpec((tm, tk), lambda i, k: (i, k)), b_spec] + ex_specs,
        out_specs=[row] * (2 if mode == "loss" else 1) + [s_spec],
        out_shape=[jax.ShapeDtypeStruct((M, N), f32)] + ([jax.ShapeDtypeStruct((M, N), MXU_DTYPE)] if mode == "loss" else [])
        + [s_shape],
        scratch_shapes=[pltpu.VMEM((tm, N), f32)] if nk > 1 else [],
        compiler_params=_cparams("arbitrary", "arbitrary"),
    )(a, b, *extras)


def rms_fwd(x, g, *, name, tr=512):
    R, D = x.shape
    tr = min(tr, R)

    def body(x_ref, g_ref, o_ref):
        xv = x_ref[...]
        y = xv * lax.rsqrt(jnp.mean(xv * xv, axis=-1, keepdims=True) + EPS)
        o_ref[...] = (y * g_ref[...]).astype(o_ref.dtype)

    return pl.pallas_call(
        body, name=name, grid=(R // tr,),
        in_specs=[pl.BlockSpec((tr, D), lambda i: (i, 0)), pl.BlockSpec((1, D), lambda i: (0, 0))],
        out_specs=pl.BlockSpec((tr, D), lambda i: (i, 0)),
        out_shape=jax.ShapeDtypeStruct((R, D), MXU_DTYPE),
        compiler_params=_cparams("parallel"),
    )(x, g)


def rms_bwd(x, g, dh, residual, *, name, tr=512):
    R, D = x.shape
    tr = min(tr, R)
    has_res = residual is not None

    def body(*refs):
        if has_res:
            x_ref, g_ref, dh_ref, res_ref, dx_ref, dg_ref = refs
        else:
            x_ref, g_ref, dh_ref, dx_ref, dg_ref = refs
        xv = x_ref[...]
        rstd = lax.rsqrt(jnp.mean(xv * xv, axis=-1, keepdims=True) + EPS)
        xhat = xv * rstd
        dh = dh_ref[...].astype(f32)
        gd = dh * g_ref[...]
        dx = rstd * (gd - xhat * jnp.mean(gd * xhat, axis=-1, keepdims=True))
        if has_res:
            dx = dx + res_ref[...]
        dx_ref[...] = dx

        @pl.when(pl.program_id(0) == 0)
        def _():
            dg_ref[...] = jnp.zeros_like(dg_ref)

        dg_ref[...] += jnp.sum(dh * xhat, axis=0, keepdims=True)

    row = pl.BlockSpec((tr, D), lambda i: (i, 0))
    vec = pl.BlockSpec((1, D), lambda i: (0, 0))
    in_specs = [row, vec, row] + ([row] if has_res else [])
    args = [x, g, dh] + ([residual] if has_res else [])
    return pl.pallas_call(
        body, name=name, grid=(R // tr,), in_specs=in_specs, out_specs=[row, vec],
        out_shape=[jax.ShapeDtypeStruct((R, D), f32), jax.ShapeDtypeStruct((1, D), f32)],
        compiler_params=_cparams("arbitrary"),
    )(*args)


def loss_head(y, target, *, tr=512):
    R, D = y.shape
    tr = min(tr, R)

    def body(y_ref, t_ref, dy_ref, loss_ref):
        e = y_ref[...] - t_ref[...]
        dy_ref[...] = e * (1.0 / D)

        @pl.when(pl.program_id(0) == 0)
        def _():
            loss_ref[...] = jnp.zeros_like(loss_ref)

        part = 0.5 * jnp.sum(jnp.mean(e * e, axis=-1, keepdims=True), axis=0, keepdims=True)
        loss_ref[...] += jnp.broadcast_to(part, loss_ref.shape)

    row = pl.BlockSpec((tr, D), lambda i: (i, 0))
    return pl.pallas_call(
        body, name="loss_head", grid=(R // tr,), in_specs=[row, row],
        out_specs=[row, pl.BlockSpec((1, LANES), lambda i: (0, 0))],
        out_shape=[jax.ShapeDtypeStruct((R, D), f32), jax.ShapeDtypeStruct((1, LANES), f32)],
        compiler_params=_cparams("arbitrary"),
    )(y, target)


def _head_rms(v, g):
    r = lax.rsqrt(jnp.mean(v * v, axis=-1, keepdims=True) + EPS)
    return v * r * g, r


def _head_rms_bwd(v, r, g, dn):
    vhat = v * r
    gd = dn * g
    dv = r * (gd - vhat * jnp.mean(gd * vhat, axis=-1, keepdims=True))
    return dv, jnp.sum(dn * vhat, axis=0, keepdims=True)


def _softmax_rows(s):
    m = jnp.max(s, axis=-1, keepdims=True)
    e = jnp.exp(s - m)
    return e / jnp.sum(e, axis=-1, keepdims=True)


def xattn_fwd(cq, ckv, gq, gk, *, B, tq=512):
    T = cq.shape[0]
    S = T // B
    M = ckv.shape[0] // B
    tq = min(tq, S)
    nq = S // tq
    hd, W = XATTN_HEAD_DIM, XATTN_WIDTH
    scale = hd ** -0.5

    def body(q_ref, k_ref, v_ref, gq_ref, gk_ref, o_ref):
        for h in range(XATTN_HEADS):
            sl = slice(h * hd, (h + 1) * hd)
            qn, _ = _head_rms(q_ref[:, sl], gq_ref[...])
            kn, _ = _head_rms(k_ref[:, sl], gk_ref[...])
            p = _softmax_rows(_dot(_mx(qn), _mx(kn), NT) * scale)
            o_ref[:, sl] = _dot(_mx(p), _mx(v_ref[:, sl]), NN).astype(o_ref.dtype)

    vec = pl.BlockSpec((1, hd), lambda b, i: (0, 0))
    qspec = pl.BlockSpec((tq, W), lambda b, i: (b * nq + i, 0))
    return pl.pallas_call(
        body, name="xattn_fwd", grid=(B, nq),
        in_specs=[qspec, pl.BlockSpec((M, W), lambda b, i: (b, 0)), pl.BlockSpec((M, W), lambda b, i: (b, 1)), vec, vec],
        out_specs=qspec, out_shape=jax.ShapeDtypeStruct((T, W), MXU_DTYPE),
        compiler_params=_cparams("parallel", "parallel"),
    )(cq, ckv, ckv, gq, gk)


def xattn_bwd(cq, ckv, gq, gk, dco, *, B, tq=512):
    T = cq.shape[0]
    S = T // B
    M = ckv.shape[0] // B
    tq = min(tq, S)
    nq = S // tq
    hd, W = XATTN_HEAD_DIM, XATTN_WIDTH
    scale = hd ** -0.5

    def body(q_ref, k_ref, v_ref, gq_ref, gk_ref, do_ref, dq_ref, dkv_ref, dgq_ref, dgk_ref, dkn_acc, dv_acc):
        b, i = pl.program_id(0), pl.program_id(1)

        @pl.when((b == 0) & (i == 0))
        def _():
            dgq_ref[...] = jnp.zeros_like(dgq_ref)
            dgk_ref[...] = jnp.zeros_like(dgk_ref)

        @pl.when(i == 0)
        def _():
            dkn_acc[...] = jnp.zeros_like(dkn_acc)
            dv_acc[...] = jnp.zeros_like(dv_acc)

        gqv, gkv = gq_ref[...], gk_ref[...]
        for h in range(XATTN_HEADS):
            sl = slice(h * hd, (h + 1) * hd)
            q, k, v = q_ref[:, sl], k_ref[:, sl], v_ref[:, sl]
            qn, rq = _head_rms(q, gqv)
            kn, _ = _head_rms(k, gkv)
            p = _softmax_rows(_dot(_mx(qn), _mx(kn), NT) * scale)
            do = do_ref[:, sl]
            dv_acc[:, sl] += _dot(_mx(p), _mx(do), TN)
            dp = _dot(_mx(do), _mx(v), NT)
            ds = p * (dp - jnp.sum(dp * p, axis=-1, keepdims=True)) * scale
            dqn = _dot(_mx(ds), _mx(kn), NN)
            dkn_acc[:, sl] += _dot(_mx(ds), _mx(qn), TN)
            dq, dgq = _head_rms_bwd(q, rq, gqv, dqn)
            dq_ref[:, sl] = dq.astype(dq_ref.dtype)
            dgq_ref[...] += dgq

        @pl.when(i == nq - 1)
        def _():
            for h in range(XATTN_HEADS):
                sl = slice(h * hd, (h + 1) * hd)
                k = k_ref[:, sl]
                rk = lax.rsqrt(jnp.mean(k * k, axis=-1, keepdims=True) + EPS)
                dk, dgk = _head_rms_bwd(k, rk, gkv, dkn_acc[:, sl])
                dkv_ref[:, sl] = dk.astype(dkv_ref.dtype)
                dkv_ref[:, slice(W + h * hd, W + (h + 1) * hd)] = dv_acc[:, sl].astype(dkv_ref.dtype)
                dgk_ref[...] += dgk

    vec = pl.BlockSpec((1, hd), lambda b, i: (0, 0))
    qspec = pl.BlockSpec((tq, W), lambda b, i: (b * nq + i, 0))
    return pl.pallas_call(
        body, name="xattn_bwd", grid=(B, nq),
        in_specs=[qspec, pl.BlockSpec((M, W), lambda b, i: (b, 0)), pl.BlockSpec((M, W), lambda b, i: (b, 1)), vec, vec, qspec],
        out_specs=[qspec, pl.BlockSpec((M, 2 * W), lambda b, i: (b, 0)), vec, vec],
        out_shape=[jax.ShapeDtypeStruct((T, W), MXU_DTYPE), jax.ShapeDtypeStruct((B * M, 2 * W), MXU_DTYPE),
                   jax.ShapeDtypeStruct((1, hd), f32), jax.ShapeDtypeStruct((1, hd), f32)],
        scratch_shapes=[pltpu.VMEM((M, W), f32), pltpu.VMEM((M, W), f32)],
        compiler_params=_cparams("arbitrary", "arbitrary"),
    )(cq, ckv, ckv, gq, gk, dco)


FOX_PAIRS = FOX_HEADS // 2


def _fox_scores(qn, kn, ccol, crow, q0, tq, S, scale):
    s = _dot(_mx(qn), _mx(kn), NT) * scale + ccol - crow
    qpos = q0 + lax.broadcasted_iota(jnp.int32, (tq, S), 0)
    kpos = lax.broadcasted_iota(jnp.int32, (tq, S), 1)
    return jnp.where(kpos <= qpos, s, NEG_INF)


def fox_fwd(P, ccol, crow, gq, gk, go, *, B, tq=256):
    T = P.shape[0]
    S = T // B
    tq = min(tq, S)
    nq = S // tq
    hd = FOX_HEAD_DIM
    scale = hd ** -0.5

    def body(q_ref, k_ref, v_ref, ccol_ref, crow_ref, gq_ref, gk_ref, go_ref, o_ref, oa_ref):
        q0 = pl.program_id(2) * tq
        for e in range(2):
            sl = slice(e * hd, (e + 1) * hd)
            qn, _ = _head_rms(q_ref[:, sl], gq_ref[:, sl])
            kn, _ = _head_rms(k_ref[:, sl], gk_ref[:, sl])
            p = _softmax_rows(_fox_scores(qn, kn, ccol_ref[0, e], crow_ref[0, e], q0, tq, S, scale))
            o = _dot(_mx(p), _mx(v_ref[:, sl]), NN)
            o_ref[:, sl] = o
            oa_ref[:, sl] = _head_rms(o, go_ref[:, sl])[0].astype(oa_ref.dtype)

    W = 2 * hd
    vec = pl.BlockSpec((1, W), lambda b, h, i: (0, 0))
    ospec = pl.BlockSpec((tq, W), lambda b, h, i: (b * nq + i, h))
    return pl.pallas_call(
        body, name="fox_fwd", grid=(B, FOX_PAIRS, nq),
        in_specs=[pl.BlockSpec((tq, W), lambda b, h, i: (b * nq + i, h)),
                  pl.BlockSpec((S, W), lambda b, h, i: (b, FOX_PAIRS + h)),
                  pl.BlockSpec((S, W), lambda b, h, i: (b, 2 * FOX_PAIRS + h)),
                  pl.BlockSpec((1, 2, tq, 1), lambda b, h, i: (b, h, i, 0)),
                  pl.BlockSpec((1, 2, 1, S), lambda b, h, i: (b, h, 0, 0)), vec, vec, vec],
        out_specs=[ospec, ospec],
        out_shape=[jax.ShapeDtypeStruct((T, FOX_WIDTH), f32), jax.ShapeDtypeStruct((T, FOX_WIDTH), MXU_DTYPE)],
        compiler_params=_cparams("parallel", "parallel", "parallel"),
    )(P, P, P, ccol, crow, gq, gk, go)


def fox_bwd(P, ccol, crow, gq, gk, go, o_raw, d_oab, *, B, tq=256):
    T = P.shape[0]
    S = T // B
    tq = min(tq, S)
    nq = S // tq
    hd = FOX_HEAD_DIM
    scale = hd ** -0.5

    def body(q_ref, k_ref, v_ref, ccol_ref, crow_ref, gq_ref, gk_ref, go_ref, o_ref, doa_ref,
             dq_ref, dk_ref, dv_ref, dccol_ref, dcrow_ref, dgq_ref, dgk_ref, dgo_ref, dkn_acc, dv_acc, dcrow_acc):
        b, h, i = pl.program_id(0), pl.program_id(1), pl.program_id(2)
        q0 = i * tq

        @pl.when((b == 0) & (h == 0) & (i == 0))
        def _():
            dgq_ref[...] = jnp.zeros_like(dgq_ref)
            dgk_ref[...] = jnp.zeros_like(dgk_ref)
            dgo_ref[...] = jnp.zeros_like(dgo_ref)

        @pl.when(i == 0)
        def _():
            dkn_acc[...] = jnp.zeros_like(dkn_acc)
            dv_acc[...] = jnp.zeros_like(dv_acc)
            dcrow_acc[...] = jnp.zeros_like(dcrow_acc)

        for e in range(2):
            sl = slice(e * hd, (e + 1) * hd)
            q, k, v = q_ref[:, sl], k_ref[:, sl], v_ref[:, sl]
            gqv, gkv, gov = gq_ref[:, sl], gk_ref[:, sl], go_ref[:, sl]
            qn, rq = _head_rms(q, gqv)
            kn, rk = _head_rms(k, gkv)
            p = _softmax_rows(_fox_scores(qn, kn, ccol_ref[0, e], crow_ref[0, e], q0, tq, S, scale))
            o = o_ref[:, sl]
            ro = lax.rsqrt(jnp.mean(o * o, axis=-1, keepdims=True) + EPS)
            do, dgo = _head_rms_bwd(o, ro, gov, doa_ref[:, sl])
            dgo_ref[:, sl] += dgo
            dv_acc[e] += _dot(_mx(p), _mx(do), TN)
            dp = _dot(_mx(do), _mx(v), NT)
            ds = p * (dp - jnp.sum(do * o, axis=-1, keepdims=True))
            dccol_ref[0, e] = jnp.sum(ds, axis=1, keepdims=True)
            dcrow_acc[e] -= jnp.sum(ds, axis=0, keepdims=True)
            dqn = _dot(_mx(ds), _mx(kn), NN) * scale
            dkn_acc[e] += _dot(_mx(ds), _mx(qn), TN) * scale
            dq, dgq = _head_rms_bwd(q, rq, gqv, dqn)
            dq_ref[:, sl] = dq.astype(dq_ref.dtype)
            dgq_ref[:, sl] += dgq

        @pl.when(i == nq - 1)
        def _():
            for e in range(2):
                sl = slice(e * hd, (e + 1) * hd)
                k = k_ref[:, sl]
                gkv = gk_ref[:, sl]
                rk = lax.rsqrt(jnp.mean(k * k, axis=-1, keepdims=True) + EPS)
                dk, dgk = _head_rms_bwd(k, rk, gkv, dkn_acc[e])
                dk_ref[:, sl] = dk.astype(dk_ref.dtype)
                dv_ref[:, sl] = dv_acc[e].astype(dv_ref.dtype)
                dgk_ref[:, sl] += dgk
                dcrow_ref[0, e] = dcrow_acc[e]

    W = 2 * hd
    vec = pl.BlockSpec((1, W), lambda b, h, i: (0, 0))
    qspec = pl.BlockSpec((tq, W), lambda b, h, i: (b * nq + i, h))
    kvout = pl.BlockSpec((S, W), lambda b, h, i: (b, h))
    colspec = pl.BlockSpec((1, 2, tq, 1), lambda b, h, i: (b, h, i, 0))
    rowspec = pl.BlockSpec((1, 2, 1, S), lambda b, h, i: (b, h, 0, 0))
    return pl.pallas_call(
        body, name="fox_bwd", grid=(B, FOX_PAIRS, nq),
        in_specs=[qspec,
                  pl.BlockSpec((S, W), lambda b, h, i: (b, FOX_PAIRS + h)),
                  pl.BlockSpec((S, W), lambda b, h, i: (b, 2 * FOX_PAIRS + h)),
                  colspec, rowspec, vec, vec, vec, qspec, qspec],
        out_specs=[qspec, kvout, kvout, colspec, rowspec, vec, vec, vec],
        out_shape=[jax.ShapeDtypeStruct((T, FOX_WIDTH), MXU_DTYPE), jax.ShapeDtypeStruct((T, FOX_WIDTH), MXU_DTYPE),
                   jax.ShapeDtypeStruct((T, FOX_WIDTH), MXU_DTYPE),
                   jax.ShapeDtypeStruct((B, FOX_HEADS, S, 1), f32), jax.ShapeDtypeStruct((B, FOX_HEADS, 1, S), f32),
                   jax.ShapeDtypeStruct((1, W), f32), jax.ShapeDtypeStruct((1, W), f32), jax.ShapeDtypeStruct((1, W), f32)],
        scratch_shapes=[pltpu.VMEM((2, S, hd), f32), pltpu.VMEM((2, S, hd), f32), pltpu.VMEM((2, 1, S), f32)],
        compiler_params=_cparams("arbitrary", "arbitrary", "arbitrary"),
    )(P, P, P, ccol, crow, gq, gk, go, o_raw, d_oab)


FOX_TQ = 512
FOX_TK = FOX_TQ
GROUP_PRECISION = lax.Precision.HIGH


def _head_mean(v):
    n = v.shape[1]
    r = lax.broadcasted_iota(jnp.int32, (n, n), 0) // FOX_HEAD_DIM
    c = lax.broadcasted_iota(jnp.int32, (n, n), 1) // FOX_HEAD_DIM
    ones = (r == c).astype(bf16)
    hi = v.astype(bf16)
    lo = (v - hi.astype(f32)).astype(bf16)
    return (_dot(hi, ones, NN) + _dot(lo, ones, NN)) * (1.0 / FOX_HEAD_DIM)


def fox_prep_fwd(P, gq, gk, *, tr=512):
    T = P.shape[0]
    tr = min(tr, T)
    scale = FOX_HEAD_DIM ** -0.5

    def body(q_ref, k_ref, v_ref, gq_ref, gk_ref, qn_ref, kn_ref, vb_ref):
        q, k = q_ref[...], k_ref[...]
        qn_ref[...] = (q * lax.rsqrt(_head_mean(q * q) + EPS) * (gq_ref[...] * scale)).astype(qn_ref.dtype)
        kn_ref[...] = (k * lax.rsqrt(_head_mean(k * k) + EPS) * gk_ref[...]).astype(kn_ref.dtype)
        vb_ref[...] = v_ref[...].astype(vb_ref.dtype)

    W = FOX_WIDTH
    col = lambda j: pl.BlockSpec((tr, W), lambda i: (i, j))
    vec = pl.BlockSpec((1, W), lambda i: (0, 0))
    out = jax.ShapeDtypeStruct((T, W), MXU_DTYPE)
    return pl.pallas_call(
        body, name="fox_prep_fwd", grid=(T // tr,), in_specs=[col(0), col(1), col(2), vec, vec],
        out_specs=[col(0)] * 3, out_shape=[out] * 3, compiler_params=_cparams("parallel"),
    )(P, P, P, gq, gk)


def fox_prep_bwd(P, gq, gk, dqn, dkn, *, tr=512):
    T = P.shape[0]
    tr = min(tr, T)
    scale = FOX_HEAD_DIM ** -0.5

    def body(q_ref, k_ref, gq_ref, gk_ref, dqn_ref, dkn_ref, dq_ref, dk_ref, dgq_ref, dgk_ref):
        @pl.when(pl.program_id(0) == 0)
        def _():
            dgq_ref[...] = jnp.zeros_like(dgq_ref)
            dgk_ref[...] = jnp.zeros_like(dgk_ref)

        def one(x, g, dn, dx_ref, dg_ref):
            r = lax.rsqrt(_head_mean(x * x) + EPS)
            xhat = x * r
            gd = dn * g
            dx_ref[...] = (r * (gd - xhat * _head_mean(gd * xhat))).astype(dx_ref.dtype)
            return jnp.sum(dn * xhat, axis=0, keepdims=True)

        dgq_ref[...] += scale * one(q_ref[...], gq_ref[...] * scale, dqn_ref[...], dq_ref, dgq_ref)
        dgk_ref[...] += one(k_ref[...], gk_ref[...], dkn_ref[...], dk_ref, dgk_ref)

    W = FOX_WIDTH
    col = lambda j: pl.BlockSpec((tr, W), lambda i: (i, j))
    vec = pl.BlockSpec((1, W), lambda i: (0, 0))
    return pl.pallas_call(
        body, name="fox_prep_bwd", grid=(T // tr,), in_specs=[col(0), col(1), vec, vec, col(0), col(0)],
        out_specs=[col(0), col(0), vec, vec],
        out_shape=[jax.ShapeDtypeStruct((T, W), MXU_DTYPE), jax.ShapeDtypeStruct((T, W), MXU_DTYPE),
                   jax.ShapeDtypeStruct((1, W), f32), jax.ShapeDtypeStruct((1, W), f32)],
        compiler_params=_cparams("arbitrary"),
    )(P, P, gq, gk, dqn, dkn)


def _fox_tile_scores(q, k_ref, ccol_ref, cq, e, j, sl, mask_off):
    tq, tk = FOX_TQ, FOX_TK
    rows = pl.ds(pl.multiple_of(j * tk, tk), tk)
    k = k_ref[rows, sl]
    s = _dot(k, q, NT) + cq - ccol_ref[0, e, rows, :]
    if mask_off is not None:
        key = lax.broadcasted_iota(jnp.int32, (tk, tq), 0) + mask_off
        query = lax.broadcasted_iota(jnp.int32, (tk, tq), 1)
        s = jnp.where(key <= query, s, NEG_INF)
    return s, k, rows


def _fox_sweep(i, update, carry):
    nd = FOX_TQ // FOX_TK
    carry = lax.fori_loop(0, i * nd, lambda j, cr: update(cr, j, None), carry)
    for d in range(nd):
        carry = update(carry, i * nd + d, d * FOX_TK)
    return carry


def fox_core_fwd(qn, kn, vb, ccol, crow, go, *, B):
    T = qn.shape[0]
    S = T // B
    tq = FOX_TQ
    nq = S // tq
    hd = FOX_HEAD_DIM

    def body(q_ref, k_ref, v_ref, ccol_ref, crow_ref, go_ref, o_ref, oa_ref, lse_ref):
        i = pl.program_id(2)
        for e in range(2):
            sl = slice(e * hd, (e + 1) * hd)
            q = q_ref[:, sl]
            cq = crow_ref[0, e, i]

            def update(carry, j, mask_off):
                m, l, acc = carry
                s, _, rows = _fox_tile_scores(q, k_ref, ccol_ref, cq, e, j, sl, mask_off)
                m2 = jnp.maximum(m, jnp.max(s, axis=0, keepdims=True))
                a = jnp.exp(m - m2)
                p = jnp.exp(s - m2)
                return m2, a * l + jnp.sum(p, axis=0, keepdims=True), a * acc + _dot(v_ref[rows, sl], _mx(p), TN)

            carry = (jnp.full((1, tq), NEG_INF, f32), jnp.zeros((1, tq), f32), jnp.zeros((hd, tq), f32))
            m, l, acc = _fox_sweep(i, update, carry)
            o = (acc / l).T
            o_ref[:, sl] = o
            oa_ref[:, sl] = _head_rms(o, go_ref[:, sl])[0].astype(oa_ref.dtype)
            lse_ref[0, e, 0] = m + jnp.log(l)

    W = 2 * hd
    qspec = pl.BlockSpec((tq, W), lambda b, h, i: (b * nq + i, h))
    kspec = pl.BlockSpec((S, W), lambda b, h, i: (b, h))
    return pl.pallas_call(
        body, name="fox_core_fwd", grid=(B, FOX_PAIRS, nq),
        in_specs=[qspec, kspec, kspec, pl.BlockSpec((1, 2, S, 1), lambda b, h, i: (b, h, 0, 0)),
                  pl.BlockSpec((1, 2, nq, 1, tq), lambda b, h, i: (b, h, 0, 0, 0)),
                  pl.BlockSpec((1, W), lambda b, h, i: (0, 0))],
        out_specs=[qspec, qspec, pl.BlockSpec((1, 2, 1, 1, tq), lambda b, h, i: (b, h, i, 0, 0))],
        out_shape=[jax.ShapeDtypeStruct((T, FOX_WIDTH), f32), jax.ShapeDtypeStruct((T, FOX_WIDTH), MXU_DTYPE),
                   jax.ShapeDtypeStruct((B, FOX_HEADS, nq, 1, tq), f32)],
        compiler_params=_cparams("parallel", "parallel", "parallel"),
    )(qn, kn, vb, ccol, crow, go)


def fox_core_bwd(qn, kn, vb, ccol, crow, go, o_raw, lse, d_oab, *, B):
    T = qn.shape[0]
    S = T // B
    tq = FOX_TQ
    nq = S // tq
    hd = FOX_HEAD_DIM

    def body(q_ref, k_ref, v_ref, ccol_ref, crow_ref, go_ref, o_ref, lse_ref, doa_ref,
             dq_ref, dk_ref, dv_ref, dckey_ref, dcrow_ref, dgo_ref, dk_acc, dv_acc, dck_acc):
        b, h, i = pl.program_id(0), pl.program_id(1), pl.program_id(2)

        @pl.when((b == 0) & (h == 0) & (i == 0))
        def _():
            dgo_ref[...] = jnp.zeros_like(dgo_ref)

        @pl.when(i == 0)
        def _():
            dk_acc[...] = jnp.zeros_like(dk_acc)
            dv_acc[...] = jnp.zeros_like(dv_acc)
            dck_acc[...] = jnp.zeros_like(dck_acc)

        for e in range(2):
            sl = slice(e * hd, (e + 1) * hd)
            q = q_ref[:, sl]
            cq = crow_ref[0, e, i]
            lse_e = lse_ref[0, e, 0]
            o = o_ref[:, sl]
            ro = lax.rsqrt(jnp.mean(o * o, axis=-1, keepdims=True) + EPS)
            do, dgo = _head_rms_bwd(o, ro, go_ref[:, sl], doa_ref[:, sl])
            dgo_ref[:, sl] += dgo
            delta = jnp.sum((do * o).T, axis=0, keepdims=True)
            do_b = _mx(do)

            def update(carry, j, mask_off):
                dq, dcq = carry
                s, k, rows = _fox_tile_scores(q, k_ref, ccol_ref, cq, e, j, sl, mask_off)
                p = jnp.exp(s - lse_e)
                dv_acc[e, rows, :] += _dot(_mx(p), do_b, NN)
                ds = p * (_dot(v_ref[rows, sl], do_b, NT) - delta)
                dck_acc[e, rows, :] -= jnp.sum(ds, axis=1, keepdims=True)
                ds_b = _mx(ds)
                dk_acc[e, rows, :] += _dot(ds_b, q, NN)
                return dq + _dot(ds_b, k, TN), dcq + jnp.sum(ds, axis=0, keepdims=True)

            dq, dcq = _fox_sweep(i, update, (jnp.zeros((tq, hd), f32), jnp.zeros((1, tq), f32)))
            dq_ref[:, sl] = dq
            dcrow_ref[0, e, 0] = dcq

        @pl.when(i == nq - 1)
        def _():
            for e in range(2):
                sl = slice(e * hd, (e + 1) * hd)
                dk_ref[:, sl] = dk_acc[e]
                dv_ref[:, sl] = dv_acc[e].astype(dv_ref.dtype)
                dckey_ref[0, e] = jnp.transpose(jnp.broadcast_to(dck_acc[e], (S, LANES)))[0:1, :]

    W = 2 * hd
    qspec = pl.BlockSpec((tq, W), lambda b, h, i: (b * nq + i, h))
    kspec = pl.BlockSpec((S, W), lambda b, h, i: (b, h))
    colspec = pl.BlockSpec((1, 2, S, 1), lambda b, h, i: (b, h, 0, 0))
    rowspec = pl.BlockSpec((1, 2, nq, 1, tq), lambda b, h, i: (b, h, 0, 0, 0))
    tilespec = pl.BlockSpec((1, 2, 1, 1, tq), lambda b, h, i: (b, h, i, 0, 0))
    vec = pl.BlockSpec((1, W), lambda b, h, i: (0, 0))
    return pl.pallas_call(
        body, name="fox_core_bwd", grid=(B, FOX_PAIRS, nq),
        in_specs=[qspec, kspec, kspec, colspec, rowspec, vec, qspec, tilespec, qspec],
        out_specs=[qspec, kspec, kspec, pl.BlockSpec((1, 2, 1, S), lambda b, h, i: (b, h, 0, 0)), tilespec, vec],
        out_shape=[jax.ShapeDtypeStruct((T, FOX_WIDTH), f32), jax.ShapeDtypeStruct((T, FOX_WIDTH), f32),
                   jax.ShapeDtypeStruct((T, FOX_WIDTH), MXU_DTYPE),
                   jax.ShapeDtypeStruct((B, FOX_HEADS, 1, S), f32), jax.ShapeDtypeStruct((B, FOX_HEADS, nq, 1, tq), f32),
                   jax.ShapeDtypeStruct((1, W), f32)],
        scratch_shapes=[pltpu.VMEM((2, S, hd), f32), pltpu.VMEM((2, S, hd), f32), pltpu.VMEM((2, S, 1), f32)],
        compiler_params=_cparams("arbitrary", "arbitrary", "arbitrary"),
    )(qn, kn, vb, ccol, crow, go, o_raw, lse, d_oab)


def _lane_mask(lo, hi, shape):
    lane = lax.broadcasted_iota(jnp.int32, shape, 1)
    return (lane >= lo) & (lane < hi)


def _cumsum_rows(v, period, reverse=False):
    n = v.shape[0]
    pos = lax.broadcasted_iota(jnp.int32, v.shape, 0) % period
    sh = 1
    while sh < period:
        if reverse:
            v = v + jnp.where(pos + sh < period, pltpu.roll(v, n - sh, 0), 0.0)
        else:
            v = v + jnp.where(pos >= sh, pltpu.roll(v, sh, 0), 0.0)
        sh *= 2
    return v


def _gate_values(z, bias, alog):
    zb = z + bias
    ls = jax.nn.log_sigmoid(zb)
    beta = jax.nn.sigmoid(z)
    g = -jnp.exp(alog) * jax.nn.softplus(zb)
    return zb, ls, beta, g


def gates_fwd(P, bias, alog, *, B):
    T = P.shape[0]
    S = T // B

    def body(z_ref, bias_ref, alog_ref, o_ref):
        z = z_ref[...]
        _, ls, beta, g = _gate_values(z, bias_ref[...], alog_ref[...])
        c = _cumsum_rows(ls, S)
        gc = _cumsum_rows(g, GDN_CHUNK)
        o = jnp.where(_lane_mask(SM_F, SM_F + FOX_HEADS, z.shape), c, 0.0)
        o = jnp.where(_lane_mask(SM_B, SM_B + GDN_HEADS, z.shape), beta, o)
        o = jnp.where(_lane_mask(SM_A, SM_A + GDN_HEADS, z.shape), gc, o)
        o_ref[...] = o

    vec = pl.BlockSpec((1, LANES), lambda b: (0, 0))
    return pl.pallas_call(
        body, name="gates_fwd", grid=(B,),
        in_specs=[pl.BlockSpec((S, LANES), lambda b: (b, COL_SMALL // LANES)), vec, vec],
        out_specs=pl.BlockSpec((S, LANES), lambda b: (b, 0)),
        out_shape=jax.ShapeDtypeStruct((T, LANES), f32),
        compiler_params=_cparams("parallel"),
    )(P, bias, alog)


def gates_bwd(P, bias, alog, dgates, *, B):
    T = P.shape[0]
    S = T // B

    def body(z_ref, bias_ref, alog_ref, dg_ref, dz_ref, par_ref):
        z = z_ref[...]
        zb, ls, beta, g = _gate_values(z, bias_ref[...], alog_ref[...])
        d = dg_ref[...]
        dls = _cumsum_rows(d, S, reverse=True)
        dgr = _cumsum_rows(d, GDN_CHUNK, reverse=True)
        sig = jax.nn.sigmoid(zb)
        dz_f = dls * (1.0 - sig)
        dz_b = d * beta * (1.0 - beta)
        dz_a = dgr * (-jnp.exp(alog_ref[...])) * sig
        dz = jnp.where(_lane_mask(SM_F, SM_F + FOX_HEADS, z.shape), dz_f, 0.0)
        dz = jnp.where(_lane_mask(SM_B, SM_B + GDN_HEADS, z.shape), dz_b, dz)
        dz = jnp.where(_lane_mask(SM_A, SM_A + GDN_HEADS, z.shape), dz_a, dz)
        dz_ref[...] = dz.astype(dz_ref.dtype)

        @pl.when(pl.program_id(0) == 0)
        def _():
            par_ref[...] = jnp.zeros_like(par_ref)

        dalog = jnp.where(_lane_mask(SM_A, SM_A + GDN_HEADS, z.shape), dgr * g, 0.0)
        par_ref[0:1, :] += jnp.sum(dz, axis=0, keepdims=True)
        par_ref[1:2, :] += jnp.sum(dalog, axis=0, keepdims=True)

    vec = pl.BlockSpec((1, LANES), lambda b: (0, 0))
    return pl.pallas_call(
        body, name="gates_bwd", grid=(B,),
        in_specs=[pl.BlockSpec((S, LANES), lambda b: (b, COL_SMALL // LANES)), vec, vec,
                  pl.BlockSpec((S, LANES), lambda b: (b, 0))],
        out_specs=[pl.BlockSpec((S, LANES), lambda b: (b, 0)), pl.BlockSpec((8, LANES), lambda b: (0, 0))],
        out_shape=[jax.ShapeDtypeStruct((T, LANES), MXU_DTYPE), jax.ShapeDtypeStruct((8, LANES), f32)],
        compiler_params=_cparams("arbitrary"),
    )(P, bias, alog, dgates)


GDN_BLOCKS = 3 * GDN_HEADS


def _shift_rows(v, d, reverse=False):
    if d == 0:
        return v
    n = v.shape[0]
    row = lax.broadcasted_iota(jnp.int32, v.shape, 0)
    if reverse:
        return jnp.where(row + d < n, pltpu.roll(v, n - d, 0), 0.0)
    return jnp.where(row >= d, pltpu.roll(v, d, 0), 0.0)


def _conv_silu(x, w):
    pre = sum(w[j:j + 1, :] * _shift_rows(x, CONV_WIDTH - 1 - j) for j in range(CONV_WIDTH))
    return pre, pre * jax.nn.sigmoid(pre)


def gdn_prep_fwd(P, conv_w, *, B):
    T = P.shape[0]
    S = T // B

    def body(x_ref, w_ref, o_ref):
        _, y = _conv_silu(x_ref[...], w_ref[...])
        yn = y * lax.rsqrt(jnp.sum(y * y, axis=-1, keepdims=True) + EPS)
        o_ref[...] = jnp.where(pl.program_id(1) < 2 * GDN_HEADS, yn, y)

    return pl.pallas_call(
        body, name="gdn_prep_fwd", grid=(B, GDN_BLOCKS),
        in_specs=[pl.BlockSpec((S, LANES), lambda b, j: (b, COL_GDN // LANES + j)),
                  pl.BlockSpec((CONV_WIDTH, LANES), lambda b, j: (0, j))],
        out_specs=pl.BlockSpec((S, LANES), lambda b, j: (b, j)),
        out_shape=jax.ShapeDtypeStruct((T, 3 * GDN_WIDTH), f32),
        compiler_params=_cparams("parallel", "parallel"),
    )(P, conv_w)


def gdn_prep_bwd(P, conv_w, dG, *, B):
    T = P.shape[0]
    S = T // B

    def body(x_ref, w_ref, dg_ref, dx_ref, dw_ref):
        x, w = x_ref[...], w_ref[...]
        pre, y = _conv_silu(x, w)
        dn = dg_ref[...]
        r = lax.rsqrt(jnp.sum(y * y, axis=-1, keepdims=True) + EPS)
        n = y * r
        dy_norm = r * (dn - n * jnp.sum(dn * n, axis=-1, keepdims=True))
        dy = jnp.where(pl.program_id(0) < 2 * GDN_HEADS, dy_norm, dn)
        sg = jax.nn.sigmoid(pre)
        dpre = dy * (sg * (1.0 + pre * (1.0 - sg)))
        dx = sum(w[j:j + 1, :] * _shift_rows(dpre, CONV_WIDTH - 1 - j, reverse=True) for j in range(CONV_WIDTH))
        dx_ref[...] = dx.astype(dx_ref.dtype)

        @pl.when(pl.program_id(1) == 0)
        def _():
            dw_ref[...] = jnp.zeros_like(dw_ref)

        for j in range(CONV_WIDTH):
            dw_ref[j:j + 1, :] += jnp.sum(dpre * _shift_rows(x, CONV_WIDTH - 1 - j), axis=0, keepdims=True)

    return pl.pallas_call(
        body, name="gdn_prep_bwd", grid=(GDN_BLOCKS, B),
        in_specs=[pl.BlockSpec((S, LANES), lambda j, b: (b, COL_GDN // LANES + j)),
                  pl.BlockSpec((CONV_WIDTH, LANES), lambda j, b: (0, j)),
                  pl.BlockSpec((S, LANES), lambda j, b: (b, j))],
        out_specs=[pl.BlockSpec((S, LANES), lambda j, b: (b, j)),
                   pl.BlockSpec((CONV_WIDTH, LANES), lambda j, b: (0, j))],
        out_shape=[jax.ShapeDtypeStruct((T, 3 * GDN_WIDTH), MXU_DTYPE),
                   jax.ShapeDtypeStruct((CONV_WIDTH, 3 * GDN_WIDTH), f32)],
        compiler_params=_cparams("arbitrary", "arbitrary"),
    )(P, conv_w, dG)


GDN_GROUP = 16
B_NN = (((2,), (1,)), ((0,), (0,)))
B_NT = (((2,), (2,)), ((0,), (0,)))
B_TN = (((1,), (1,)), ((0,), (0,)))


def _bmm(a, b, dims, precision=None):
    if precision is None:
        a, b = _mx(a), _mx(b)
    return lax.dot_general(a, b, dims, preferred_element_type=f32, precision=precision)


def _tri_inverse(A):
    C = A.shape[-1]
    row = lax.broadcasted_iota(jnp.int32, A.shape, 1)
    col = lax.broadcasted_iota(jnp.int32, A.shape, 2)
    eye = (row == col).astype(f32)
    X = jnp.where((row // 4) == (col // 4), -A, 0.0)
    X2 = _bmm(X, X, B_NN, INV_PRECISION)
    Tm = eye + X + X2 + _bmm(X, X2, B_NN, INV_PRECISION)
    b = 4
    while b < C:
        off = ((row // (2 * b)) == (col // (2 * b))) & ((row // b) != (col // b))
        Tm = Tm - _bmm(_bmm(Tm, jnp.where(off, A, 0.0), B_NN, INV_PRECISION), Tm, B_NN, INV_PRECISION)
        b *= 2
    return Tm


def _pick_lane(block, lane_idx):
    lane = lax.broadcasted_iota(jnp.int32, block.shape, 1)
    return jnp.sum(jnp.where(lane == lane_idx, block, 0.0), axis=1, keepdims=True)


def _gdn_local(q, k, v, beta, gc, Tm=None, uwm=None):
    C = GDN_CHUNK
    n = q.shape[0] // C
    q = q.reshape(n, C, -1) * (GDN_HEAD_DIM ** -0.5)
    k = k.reshape(n, C, -1)
    v = v.reshape(n, C, -1)
    beta = beta.reshape(n, C, 1)
    gc = gc.reshape(n, C, 1)
    row = lax.broadcasted_iota(jnp.int32, (n, C, C), 1)
    col = lax.broadcasted_iota(jnp.int32, (n, C, C), 2)
    gcT = jnp.swapaxes(jnp.broadcast_to(gc, (n, C, C)), 1, 2)
    D = jnp.exp(jnp.where(row >= col, gc - gcT, NEG_INF))
    kb = k * beta
    vb = v * beta
    A = jnp.where(row > col, _bmm(kb, k, B_NT) * D, 0.0)
    Gam = jnp.exp(gc)
    kg = kb * Gam
    gl = gc[:, C - 1:C, :]
    kdec = jnp.exp(gl - gc)
    loc = dict(q=q, k=k, v=v, beta=beta, gc=gc, D=D, kb=kb, vb=vb, A=A, Gam=Gam, kg=kg,
               kdec=kdec, kd=k * kdec, qg=q * Gam, gam=jnp.exp(gl), row=row, col=col)
    uwm = Tm is None if uwm is None else uwm
    Tm = _tri_inverse(A) if Tm is None else Tm.reshape(n, C, C)
    if uwm:
        loc.update(u=_bmm(Tm, vb, B_NN), w=_bmm(Tm, kg, B_NN), M=_bmm(q, k, B_NT) * D)
    loc["Tm"] = Tm
    return loc


def _gdn_store_local(loc, r0, u_s, w_s, qg_s, kd_s, M_s, gam_s, c0):
    n = loc["u"].shape[0]
    R = n * GDN_CHUNK
    u_s[pl.ds(r0, R), :] = loc["u"].reshape(R, -1)
    w_s[pl.ds(r0, R), :] = loc["w"].reshape(R, -1)
    qg_s[pl.ds(r0, R), :] = loc["qg"].reshape(R, -1)
    kd_s[pl.ds(r0, R), :] = loc["kd"].reshape(R, -1)
    M_s[pl.ds(r0, R), :] = loc["M"].reshape(R, -1)
    gam_s[pl.ds(c0, n)] = jnp.broadcast_to(loc["gam"], (n, 1, LANES))


def _gdn_specs(S):
    blk = lambda off: pl.BlockSpec((S, LANES), lambda b, h: (b, off + h))
    return blk


def gdn_fwd(G, gates, P, g_on, *, B):
    T = G.shape[0]
    S = T // B
    C = GDN_CHUNK
    N = S // C
    grp = min(GDN_GROUP, N)
    R = grp * C
    hd = GDN_HEAD_DIM

    def body(q_ref, k_ref, v_ref, gt_ref, z_ref, gon_ref, o_ref, ob_ref, st_ref, tm_ref, A_s, B_s, Q_s, O_s, gam_s):
        h = pl.program_id(1)

        def local(gi, carry):
            r0 = pl.multiple_of(gi * R, R)
            gt = gt_ref[pl.ds(r0, R), :]
            loc = _gdn_local(q_ref[pl.ds(r0, R), :], k_ref[pl.ds(r0, R), :], v_ref[pl.ds(r0, R), :],
                             _pick_lane(gt, SM_B + h), _pick_lane(gt, SM_A + h))
            chunks = pl.ds(gi * grp, grp)
            tm_ref[0, 0, pl.ds(r0, R), :] = loc["Tm"].reshape(R, C)
            A_s[chunks] = -_bmm(loc["kd"], loc["w"], B_TN)
            B_s[chunks] = _bmm(loc["kd"], loc["u"], B_TN)
            Q_s[pl.ds(r0, R), :] = (loc["qg"] - _bmm(loc["M"], loc["w"], B_NN)).reshape(R, hd)
            O_s[pl.ds(r0, R), :] = _bmm(loc["M"], loc["u"], B_NN).reshape(R, hd)
            gam_s[chunks] = jnp.broadcast_to(loc["gam"], (grp, 1, LANES))
            return carry

        lax.fori_loop(0, N // grp, local, 0)

        def step(n, state):
            st_ref[0, 0, n] = state
            return state * gam_s[n] + _dotm(A_s[n], state, NN) + B_s[n]

        lax.fori_loop(0, N, step, jnp.zeros((hd, hd), f32))

        def outputs(gi, carry):
            r0 = pl.multiple_of(gi * R, R)
            Q = Q_s[pl.ds(r0, R), :].reshape(grp, C, hd)
            o = _bmm(Q, st_ref[0, 0, pl.ds(gi * grp, grp)], B_NN).reshape(R, hd) + O_s[pl.ds(r0, R), :]
            o_ref[pl.ds(r0, R), :] = o
            return carry

        lax.fori_loop(0, N // grp, outputs, 0)
        o = o_ref[...]
        z = z_ref[...]
        ob_ref[...] = (_head_rms(o, gon_ref[...])[0] * (z * jax.nn.sigmoid(z))).astype(ob_ref.dtype)

    blk = lambda off: pl.BlockSpec((S, LANES), lambda b, h: (b, off + h))
    rows = lambda: pltpu.VMEM((S, hd), f32)
    return pl.pallas_call(
        body, name="gdn_fwd", grid=(B, GDN_HEADS),
        in_specs=[blk(0), blk(GDN_HEADS), blk(2 * GDN_HEADS), pl.BlockSpec((S, LANES), lambda b, h: (b, 0)),
                  blk(COL_Z // LANES), pl.BlockSpec((1, hd), lambda b, h: (0, 0))],
        out_specs=[blk(0), blk(0), pl.BlockSpec((1, 1, N, hd, hd), lambda b, h: (b, h, 0, 0, 0)),
                   pl.BlockSpec((1, 1, S, C), lambda b, h: (b, h, 0, 0))],
        out_shape=[jax.ShapeDtypeStruct((T, GDN_WIDTH), f32), jax.ShapeDtypeStruct((T, GDN_WIDTH), MXU_DTYPE),
                   jax.ShapeDtypeStruct((B, GDN_HEADS, N, hd, hd), f32), jax.ShapeDtypeStruct((B, GDN_HEADS, S, C), f32)],
        scratch_shapes=[pltpu.VMEM((N, hd, hd), f32), pltpu.VMEM((N, hd, hd), f32), rows(), rows(),
                        pltpu.VMEM((N, 1, LANES), f32)],
        compiler_params=_cparams("parallel", "parallel"),
    )(G, G, G, gates, P, g_on)


def gdn_bwd(G, gates, P, g_on, o_raw, states, tm, d_oab, *, B):
    T = G.shape[0]
    S = T // B
    C = GDN_CHUNK
    N = S // C
    grp = min(GDN_GROUP, N)
    R = grp * C
    hd = GDN_HEAD_DIM

    def body(q_ref, k_ref, v_ref, gt_ref, z_ref, gon_ref, o_ref, st_ref, tm_ref, dob_ref,
             dq_ref, dk_ref, dv_ref, dgt_ref, dz_ref, dgon_ref,
             u_s, w_s, M_s, gam_s, do_s, A_s, C_s, dst_s):
        b, h = pl.program_id(0), pl.program_id(1)

        @pl.when((b == 0) & (h == 0))
        def _():
            dgon_ref[...] = jnp.zeros_like(dgon_ref)

        @pl.when(h == 0)
        def _():
            dgt_ref[...] = jnp.zeros_like(dgt_ref)

        def group_inputs(gi, uwm):
            r0 = pl.multiple_of(gi * R, R)
            gt = gt_ref[pl.ds(r0, R), :]
            return r0, _gdn_local(q_ref[pl.ds(r0, R), :], k_ref[pl.ds(r0, R), :], v_ref[pl.ds(r0, R), :],
                                  _pick_lane(gt, SM_B + h), _pick_lane(gt, SM_A + h), tm_ref[0, 0, pl.ds(r0, R), :], uwm)

        def local(gi, carry):
            r0, loc = group_inputs(gi, True)
            rows, chunks = pl.ds(r0, R), pl.ds(gi * grp, grp)
            u_s[rows, :] = loc["u"].reshape(R, hd)
            w_s[rows, :] = loc["w"].reshape(R, hd)
            M_s[rows, :] = loc["M"].reshape(R, C)
            gam_s[chunks] = jnp.broadcast_to(loc["gam"], (grp, 1, LANES))
            o, z, gon = o_ref[rows, :], z_ref[rows, :], gon_ref[...]
            dob = dob_ref[rows, :]
            on, ro = _head_rms(o, gon)
            sz = jax.nn.sigmoid(z)
            dz_ref[rows, :] = (dob * on * (sz * (1.0 + z * (1.0 - sz)))).astype(dz_ref.dtype)
            do, dgon = _head_rms_bwd(o, ro, gon, dob * (z * sz))
            do_s[rows, :] = do
            dgon_ref[...] += dgon
            A_s[chunks] = -_bmm(loc["kd"], loc["w"], B_TN)
            C_s[chunks] = _bmm(loc["qg"] - _bmm(loc["M"], loc["w"], B_NN), do.reshape(grp, C, hd), B_TN)
            return carry

        lax.fori_loop(0, N // grp, local, 0)

        def step(t, dS):
            n = N - 1 - t
            dst_s[n] = dS
            return dS * gam_s[n] + _dotm(A_s[n], dS, TN) + C_s[n]

        lax.fori_loop(0, N, step, jnp.zeros((hd, hd), f32))

        def finish(gi, carry):
            r0, L = group_inputs(gi, False)
            n = grp
            rows, chunks = pl.ds(r0, R), pl.ds(gi * grp, grp)
            g3 = lambda ref: ref[rows, :].reshape(n, C, -1)
            u, w, do = g3(u_s), g3(w_s), g3(do_s)
            L["M"] = g3(M_s)
            state, dS = st_ref[0, 0, chunks], dst_s[chunks]
            v_new = u - _bmm(w, state, B_NN)
            du = _bmm(L["M"], do, B_TN) + _bmm(L["kd"], dS, B_NN)
            dw = -_bmm(du, state, B_NT)
            dqg = _bmm(do, state, B_NT)
            dM = _bmm(do, v_new, B_NT)
            dkd = _bmm(v_new, dS, B_NT)
            dgl_state = jnp.sum(jnp.sum(dS * state, axis=2, keepdims=True), axis=1, keepdims=True) * L["gam"]
            TmT = jnp.swapaxes(L["Tm"], 1, 2)
            dTm = _bmm(du, L["vb"], B_NT) + _bmm(dw, L["kg"], B_NT)
            dvb = _bmm(TmT, du, B_NN)
            dkg = _bmm(TmT, dw, B_NN)
            dA = jnp.where(L["row"] > L["col"], -_bmm(_bmm(TmT, dTm, B_NN), TmT, B_NN), 0.0)
            dKK = dA * L["D"]
            dQK = dM * L["D"]
            dkb = _bmm(dKK, L["k"], B_NN) + dkg * L["Gam"]
            dk = (_bmm(dKK, L["kb"], B_TN) + _bmm(dQK, L["q"], B_TN) + dkd * L["kdec"] + L["beta"] * dkb)
            dq = (_bmm(dQK, L["k"], B_NN) + dqg * L["Gam"]) * (GDN_HEAD_DIM ** -0.5)
            E = dA * L["A"] + dM * L["M"]
            r = jnp.sum(dkd * L["kd"], axis=-1, keepdims=True)
            dgc = (jnp.sum(E, axis=2, keepdims=True) - jnp.sum(jnp.swapaxes(E, 1, 2), axis=2, keepdims=True)
                   + jnp.sum(dkg * L["kg"], axis=-1, keepdims=True) + jnp.sum(dqg * L["qg"], axis=-1, keepdims=True) - r)
            dgl = jnp.sum(r, axis=1, keepdims=True) + dgl_state
            rowc = lax.broadcasted_iota(jnp.int32, (n, C, 1), 1)
            dgc = dgc + jnp.where(rowc == C - 1, dgl, 0.0)
            dbeta = jnp.sum(dkb * L["k"], axis=-1, keepdims=True) + jnp.sum(dvb * L["v"], axis=-1, keepdims=True)
            dq_ref[rows, :] = dq.reshape(R, hd)
            dk_ref[rows, :] = dk.reshape(R, hd)
            dv_ref[rows, :] = (L["beta"] * dvb).reshape(R, hd)
            lane = lax.broadcasted_iota(jnp.int32, (R, LANES), 1)
            dgt_ref[rows, :] += (jnp.where(lane == SM_B + h, dbeta.reshape(R, 1), 0.0)
                                 + jnp.where(lane == SM_A + h, dgc.reshape(R, 1), 0.0))
            return carry

        lax.fori_loop(0, N // grp, finish, 0)

    blk = lambda off: pl.BlockSpec((S, LANES), lambda b, h: (b, off + h))
    rows = lambda: pltpu.VMEM((S, hd), f32)
    return pl.pallas_call(
        body, name="gdn_bwd", grid=(B, GDN_HEADS),
        in_specs=[blk(0), blk(GDN_HEADS), blk(2 * GDN_HEADS), pl.BlockSpec((S, LANES), lambda b, h: (b, 0)),
                  blk(COL_Z // LANES), pl.BlockSpec((1, hd), lambda b, h: (0, 0)), blk(0),
                  pl.BlockSpec((1, 1, N, hd, hd), lambda b, h: (b, h, 0, 0, 0)),
                  pl.BlockSpec((1, 1, S, C), lambda b, h: (b, h, 0, 0)), blk(GDN_HEADS)],
        out_specs=[blk(0), blk(0), blk(0), pl.BlockSpec((S, LANES), lambda b, h: (b, 0)), blk(0),
                   pl.BlockSpec((1, hd), lambda b, h: (0, 0))],
        out_shape=[jax.ShapeDtypeStruct((T, GDN_WIDTH), f32), jax.ShapeDtypeStruct((T, GDN_WIDTH), f32),
                   jax.ShapeDtypeStruct((T, GDN_WIDTH), f32), jax.ShapeDtypeStruct((T, LANES), f32),
                   jax.ShapeDtypeStruct((T, GDN_WIDTH), MXU_DTYPE), jax.ShapeDtypeStruct((1, hd), f32)],
        scratch_shapes=[rows(), rows(), pltpu.VMEM((S, C), f32), pltpu.VMEM((N, 1, LANES), f32), rows(),
                        pltpu.VMEM((N, hd, hd), f32), pltpu.VMEM((N, hd, hd), f32), pltpu.VMEM((N, hd, hd), f32)],
        compiler_params=_cparams("arbitrary", "arbitrary"),
    )(G, G, G, gates, P, g_on, o_raw, states, tm, d_oab)


IN_SPLIT = (0, 1536, 1544, 3080, 3088, 3600)


IN_SHARD = IN_DIM // 4
IN_SHARD_PAD = 928


def align_w_in_t(wt):
    s = IN_SPLIT
    pad = jnp.zeros((IN_ALIGNED - IN_DIM, wt.shape[1]), wt.dtype)
    return jnp.concatenate([wt[s[0]:s[1]], wt[s[2]:s[3]], wt[s[4]:s[5]], wt[s[1]:s[2]], wt[s[3]:s[4]], pad], axis=0)


def unalign_w_in_t(wa):
    return jnp.concatenate([wa[0:1536], wa[COL_SMALL:COL_SMALL + 8], wa[1536:3072],
                            wa[COL_SMALL + 8:COL_SMALL + 16], wa[3072:3584]], axis=0)


def _lanes_vec(pieces):
    v = jnp.zeros((1, LANES), f32)
    for off, a in pieces:
        v = lax.dynamic_update_slice(v, a.astype(f32), (0, off))
    return v


def local_step(x, mem, target, w, sp, *, B):
    T = x.shape[0]
    S = T // B
    gq8, gk8 = jnp.tile(sp["fox_qnorm_g"], (1, FOX_HEADS)), jnp.tile(sp["fox_knorm_g"], (1, FOX_HEADS))
    go2 = jnp.tile(sp["fox_onorm_g"], (1, 2))
    bias = _lanes_vec([(SM_F, sp["fox_f_bias"]), (SM_A, sp["gdn_dt_bias"])])
    alog = _lanes_vec([(SM_A, sp["gdn_A_log"])])

    h1 = rms_fwd(x, sp["norm_mix_g"], name="rms_mix")
    P = matmul(h1, w["wa_t"], tb=True, name="mm_in", tn=IN_TILE)
    gates = gates_fwd(P, bias, alog, B=B)
    c = gates[:, SM_F:SM_F + FOX_HEADS].reshape(B, S, FOX_HEADS).transpose(0, 2, 1)
    ccol, crow = c[..., None], c.reshape(B, FOX_HEADS, S // FOX_TQ, 1, FOX_TQ)
    qn, kn, vb = fox_prep_fwd(P, gq8, gk8)
    o_raw, o_a, lse = fox_core_fwd(qn, kn, vb, ccol, crow, go2, B=B)
    G = gdn_prep_fwd(P, w["conv_w"], B=B)
    ob_raw, o_b, states, gdn_tm = gdn_fwd(G, gates, P, sp["gdn_onorm_g"], B=B)
    oab = jnp.concatenate([o_a, o_b], axis=1)
    if "late" in w:
        w = {**w, **w["late"](oab)}
    x2 = matmul(oab, w["w_out"], residual=x, name="mm_out")
    hq = rms_fwd(x2, sp["norm_xattn_g"], name="rms_xattn")
    hm = rms_fwd(mem, sp["mem_norm_g"], name="rms_mem")
    cq = matmul(hq, w["w_cq"], name="mm_cq")
    ckv = matmul(hm, w["w_ckv"], name="mm_ckv")
    co = xattn_fwd(cq, ckv, sp["xattn_qnorm_g"], sp["xattn_knorm_g"], B=B)
    x3 = matmul(co, w["w_co"], b_stacked=True, residual=x2, name="mm_co")
    hf = rms_fwd(x3, sp["norm_mlp_g"], name="rms_mlp")
    act = matmul(hf, w["w_mlp1"], b_stacked=True, relu2_out=True, out_dtype=MXU_DTYPE, name="mm_mlp1")
    dy, dy_op, loss = matmul_rows(act, w["w_mlp2"], (x3, target), mode="loss", name="mm_mlp2_loss")

    da = matmul(dy_op, w["w_mlp2"], tb=True, relu2_bwd_aux=act, out_dtype=MXU_DTYPE, name="mm_d_act")
    g_mlp2 = matmul(act, dy_op, ta=True, out_dtype=WIRE_DTYPE, name="mm_g_mlp2")
    g_mlp1 = matmul(hf, da, ta=True, out_stacked=True, out_dtype=WIRE_DTYPE, name="mm_g_mlp1")
    by_rows = lambda g: g.reshape(N_CHIPS, g.shape[0] // N_CHIPS, g.shape[1])
    early = w.get("grads_ready", lambda grads: jnp.zeros((1, 1), f32))
    tok = early(dict(w_mlp1=g_mlp1, w_mlp2=by_rows(g_mlp2)))[0, 0]
    dx3, g_norm_mlp = matmul_rows(da, w["w_mlp1"], (x3, sp["norm_mlp_g"] + tok, dy), mode="rms_bwd", tb=True,
                                  b_stacked=True, name="mm_d_hf_rms")
    dco = matmul(dx3, w["w_co"], tb=True, b_stacked=True, name="mm_d_co")
    g_co = matmul(co, dx3, ta=True, out_stacked=True, out_dtype=WIRE_DTYPE, name="mm_g_co")
    dcq, dckv, g_xq, g_xk = xattn_bwd(cq, ckv, sp["xattn_qnorm_g"], sp["xattn_knorm_g"], dco, B=B)
    g_cq = matmul(hq, dcq, ta=True, out_dtype=WIRE_DTYPE, name="mm_g_cq")
    g_ckv = matmul(hm, dckv, ta=True, out_dtype=WIRE_DTYPE, name="mm_g_ckv")
    _, g_mem_norm = matmul_rows(dckv, w["w_ckv"], (mem, sp["mem_norm_g"], None), mode="rms_bwd", tb=True, name="mm_d_hm_rms")
    dx2, g_norm_xattn = matmul_rows(dcq, w["w_cq"], (x2, sp["norm_xattn_g"], dx3), mode="rms_bwd", tb=True, name="mm_d_hq_rms")
    doab = matmul(dx2, w["w_out"], tb=True, name="mm_d_oab")
    g_out = matmul(oab, dx2, ta=True, out_dtype=WIRE_DTYPE, name="mm_g_out")
    tok = early(dict(w_co=g_co, w_cq=by_rows(g_cq), w_ckv=by_rows(g_ckv), w_out=by_rows(g_out)))[0, 0]
    dqn, dkn, dv_f, dckey, dcrow, dgo2 = fox_core_bwd(qn, kn, vb, ccol, crow, go2 + tok, o_raw, lse, doab, B=B)
    dq_f, dk_f, dgq8, dgk8 = fox_prep_bwd(P, gq8, gk8, dqn, dkn)
    dGq, dGk, dGv, dgt, dz, g_gdn_on = gdn_bwd(G, gates, P, sp["gdn_onorm_g"], ob_raw, states, gdn_tm, doab, B=B)
    dPg, g_conv = gdn_prep_bwd(P, w["conv_w"], jnp.concatenate([dGq, dGk, dGv], axis=1), B=B)
    dc = (dckey[:, :, 0, :] + dcrow.reshape(B, FOX_HEADS, S)).transpose(0, 2, 1).reshape(T, FOX_HEADS)
    dgates = dgt + jnp.pad(dc, ((0, 0), (SM_F, LANES - SM_F - FOX_HEADS)))
    dsmall, par = gates_bwd(P, bias, alog, dgates, B=B)
    dP = jnp.concatenate([dq_f, dk_f, dv_f, dPg, dz, dsmall, jnp.zeros((T, IN_ALIGNED - COL_SMALL - LANES), MXU_DTYPE)], axis=1)
    g_wa = matmul(dP, h1, ta=True, out_dtype=WIRE_DTYPE, name="mm_g_in", tm=IN_TILE)
    dx, g_norm_mix = matmul_rows(dP, w["wa_t"], (x, sp["norm_mix_g"], dx2), mode="rms_bwd", tk=IN_TILE, name="mm_d_h1_rms")

    fold = lambda g: jnp.sum(g.reshape(-1, FOX_HEAD_DIM), axis=0, keepdims=True)
    g_in = jnp.pad(unalign_w_in_t(g_wa).reshape(N_CHIPS, IN_SHARD, D_MODEL), ((0, 0), (0, IN_SHARD_PAD - IN_SHARD), (0, 0)))
    big = dict(w_in=g_in, w_out=by_rows(g_out), w_cq=by_rows(g_cq), w_ckv=by_rows(g_ckv), w_co=g_co, w_mlp1=g_mlp1,
               w_mlp2=by_rows(g_mlp2))
    small = dict(norm_mix_g=g_norm_mix, fox_qnorm_g=fold(dgq8), fox_knorm_g=fold(dgk8),
                 fox_f_bias=par[0:1, SM_F:SM_F + FOX_HEADS], fox_onorm_g=fold(dgo2), gdn_conv_w=g_conv,
                 gdn_A_log=par[1:2, SM_A:SM_A + GDN_HEADS], gdn_dt_bias=par[0:1, SM_A:SM_A + GDN_HEADS],
                 gdn_onorm_g=g_gdn_on, norm_xattn_g=g_norm_xattn, mem_norm_g=g_mem_norm,
                 xattn_qnorm_g=g_xq, xattn_knorm_g=g_xk, norm_mlp_g=g_norm_mlp)
    return loss, dx, big, small


MESH_IDS = pl.DeviceIdType.MESH
N_CHIPS = 4
HBM_SPEC = pl.BlockSpec(memory_space=pltpu.HBM)
PACK_ROWS = 30720
PACK_HALF = PACK_ROWS // 2
PACK_BLOCK = 3072


def _place():
    return lax.axis_index("x"), lax.axis_index("y"), lax.axis_index("c")


def _other_chips(x, y):
    return [(1 - x, y), (x, 1 - y), (1 - x, 1 - y)]


def _remote(src, dst, send_sem, recv_sem, to):
    return pltpu.make_async_remote_copy(src_ref=src, dst_ref=dst, send_sem=send_sem, recv_sem=recv_sem,
                                        device_id=to, device_id_type=MESH_IDS)


def all_gather_shards(packed):
    half = PACK_HALF

    def body(src_ref, out_ref, send_sems, recv_sems):
        x, y, c = _place()
        me_chip = 2 * x + y
        sibling = (x, y, 1 - c)
        chips = _other_chips(x, y)

        def rows(chip, core):
            return out_ref.at[chip, pl.ds(core * half, half), :]

        sends = [_remote(src_ref.at[pl.ds(c * half, half), :], rows(me_chip, c), send_sems.at[j], recv_sems.at[j], (px, py, c))
                 for j, (px, py) in enumerate(chips)]
        for cp in sends:
            cp.start()
        passed = []
        for j, (px, py) in enumerate(chips):
            theirs = rows(2 * px + py, c)
            _remote(theirs, theirs, send_sems.at[j], recv_sems.at[j], (px, py, c)).wait_recv()
            cp = _remote(theirs, theirs, send_sems.at[3 + j], recv_sems.at[3 + j], sibling)
            cp.start()
            passed.append(cp)
        for j, (px, py) in enumerate(chips):
            theirs = rows(2 * px + py, 1 - c)
            _remote(theirs, theirs, send_sems.at[3 + j], recv_sems.at[3 + j], sibling).wait_recv()
        for cp in sends + passed:
            cp.wait_send()

    return pl.pallas_call(
        body, name="all_gather_shards", in_specs=[HBM_SPEC], out_specs=HBM_SPEC,
        out_shape=jax.ShapeDtypeStruct((N_CHIPS,) + packed.shape, packed.dtype),
        scratch_shapes=[pltpu.SemaphoreType.DMA((6,)), pltpu.SemaphoreType.DMA((6,))],
    )(packed)


def exchange_core_halves(G):
    half = PACK_HALF

    def body(g_ref, land_ref, send_sem, recv_sem):
        x, y, c = _place()
        cp = _remote(g_ref.at[:, pl.ds((1 - c) * half, half), :], land_ref, send_sem, recv_sem, (x, y, 1 - c))
        cp.start()
        cp.wait()

    return pl.pallas_call(
        body, name="exchange_core_halves", in_specs=[HBM_SPEC], out_specs=HBM_SPEC,
        out_shape=jax.ShapeDtypeStruct((N_CHIPS, half, LANES), G.dtype),
        scratch_shapes=[pltpu.SemaphoreType.DMA(()), pltpu.SemaphoreType.DMA(())],
    )(G)


def add_core_halves(G, land, core):
    nb = PACK_HALF // PACK_BLOCK

    def body(c_ref, g_ref, l_ref, o_ref):
        o_ref[...] = (g_ref[...].astype(f32) + l_ref[...].astype(f32)).astype(o_ref.dtype)

    blk = (1, PACK_BLOCK, LANES)
    return pl.pallas_call(
        body, name="add_core_halves",
        grid_spec=pltpu.PrefetchScalarGridSpec(
            num_scalar_prefetch=1, grid=(N_CHIPS, nb),
            in_specs=[pl.BlockSpec(blk, lambda k, i, c_ref: (k, c_ref[0] * nb + i, 0)),
                      pl.BlockSpec(blk, lambda k, i, c_ref: (k, i, 0))],
            out_specs=pl.BlockSpec(blk, lambda k, i, c_ref: (k, i, 0))),
        out_shape=jax.ShapeDtypeStruct(land.shape, land.dtype),
        compiler_params=_cparams("parallel", "parallel"),
    )(core, G, land)


def scatter_to_chips(part):
    def body(p_ref, land_ref, send_sems, recv_sems):
        x, y, c = _place()
        me_chip = 2 * x + y
        chips = _other_chips(x, y)
        sends = [_remote(p_ref.at[2 * px + py], land_ref.at[me_chip], send_sems.at[j], recv_sems.at[j], (px, py, c))
                 for j, (px, py) in enumerate(chips)]
        for cp in sends:
            cp.start()
        for j, (px, py) in enumerate(chips):
            slot = land_ref.at[2 * px + py]
            _remote(slot, slot, send_sems.at[j], recv_sems.at[j], (px, py, c)).wait_recv()
        for cp in sends:
            cp.wait_send()

    return pl.pallas_call(
        body, name="scatter_to_chips", in_specs=[HBM_SPEC], out_specs=HBM_SPEC,
        out_shape=jax.ShapeDtypeStruct(part.shape, part.dtype),
        scratch_shapes=[pltpu.SemaphoreType.DMA((3,)), pltpu.SemaphoreType.DMA((3,))],
    )(part)


def sum_chips(part, land, order):
    nb = PACK_HALF // PACK_BLOCK

    def body(order_ref, p_ref, l1_ref, l2_ref, l3_ref, o_ref):
        o_ref[...] = ((p_ref[0].astype(f32) + l1_ref[0].astype(f32)) + l2_ref[0].astype(f32)) + l3_ref[0].astype(f32)

    slot = lambda j: pl.BlockSpec((1, PACK_BLOCK, LANES), lambda i, order_ref: (order_ref[j], i, 0))
    return pl.pallas_call(
        body, name="sum_chips",
        grid_spec=pltpu.PrefetchScalarGridSpec(
            num_scalar_prefetch=1, grid=(nb,), in_specs=[slot(0), slot(1), slot(2), slot(3)],
            out_specs=pl.BlockSpec((PACK_BLOCK, LANES), lambda i, order_ref: (i, 0))),
        out_shape=jax.ShapeDtypeStruct((PACK_HALF, LANES), f32),
        compiler_params=_cparams("parallel"),
    )(order, part, land, land, land)


def swap_core_halves(red):
    def body(r_ref, out_ref, send_sem, recv_sem):
        x, y, c = _place()
        cp = _remote(r_ref, out_ref, send_sem, recv_sem, (x, y, 1 - c))
        cp.start()
        cp.wait()

    return pl.pallas_call(
        body, name="swap_core_halves", in_specs=[HBM_SPEC], out_specs=HBM_SPEC,
        out_shape=jax.ShapeDtypeStruct(red.shape, red.dtype),
        scratch_shapes=[pltpu.SemaphoreType.DMA(()), pltpu.SemaphoreType.DMA(())],
    )(red)


def _half(ref, core):
    rows = ref.shape[-2] // 2
    return ref.at[(slice(None),) * (len(ref.shape) - 2) + (pl.ds(core * rows, rows), slice(None))]


def gather_weights(shards, conv):
    n = len(shards)

    def body(*refs):
        src, conv_src = refs[:n], refs[n]
        out, conv_out = refs[n + 1:2 * n + 1], refs[2 * n + 1]
        send_sems, recv_sems = refs[2 * n + 2], refs[2 * n + 3]
        x, y, c = _place()
        me_chip = 2 * x + y
        sibling = (x, y, 1 - c)
        chips = _other_chips(x, y)
        sends = []
        for a in range(n):
            for j, (px, py) in enumerate(chips):
                sends.append(_remote(_half(src[a], c), _half(out[a].at[me_chip], c),
                                     send_sems.at[6 * a + j], recv_sems.at[6 * a + j], (px, py, c)))
        for j, (px, py) in enumerate(chips):
            sends.append(_remote(conv_src, conv_out.at[me_chip], send_sems.at[6 * n + j], recv_sems.at[6 * n + j], (px, py, c)))
        for cp in sends:
            cp.start()
        passed = []
        for a in range(n):
            for j, (px, py) in enumerate(chips):
                theirs = _half(out[a].at[2 * px + py], c)
                _remote(theirs, theirs, send_sems.at[6 * a + j], recv_sems.at[6 * a + j], (px, py, c)).wait_recv()
                cp = _remote(theirs, theirs, send_sems.at[6 * a + 3 + j], recv_sems.at[6 * a + 3 + j], sibling)
                cp.start()
                passed.append(cp)
        for j, (px, py) in enumerate(chips):
            theirs = conv_out.at[2 * px + py]
            _remote(theirs, theirs, send_sems.at[6 * n + j], recv_sems.at[6 * n + j], (px, py, c)).wait_recv()
        for a in range(n):
            for j, (px, py) in enumerate(chips):
                theirs = _half(out[a].at[2 * px + py], 1 - c)
                _remote(theirs, theirs, send_sems.at[6 * a + 3 + j], recv_sems.at[6 * a + 3 + j], sibling).wait_recv()
        for cp in sends + passed:
            cp.wait_send()

    return pl.pallas_call(
        body, name="gather_weights", in_specs=[HBM_SPEC] * (n + 1), out_specs=[HBM_SPEC] * (n + 1),
        out_shape=[jax.ShapeDtypeStruct((N_CHIPS,) + s.shape, s.dtype) for s in list(shards) + [conv]],
        scratch_shapes=[pltpu.SemaphoreType.DMA((6 * n + 3,)), pltpu.SemaphoreType.DMA((6 * n + 3,))],
    )(*shards, conv)


SEM_SPEC = pl.BlockSpec(memory_space=pltpu.SEMAPHORE)
SPLIT_EFFECT = pltpu.SideEffectType.DATAFLOW_SIDE_EFFECTING


def _gather_async_copies(src, land, send_sems, recv_sems, x, y, c):
    me_chip = 2 * x + y
    sends, arrivals = [], []
    for a in range(len(src)):
        for j, (px, py) in enumerate(_other_chips(x, y)):
            for core in range(2):
                sends.append(_remote(_half(src[a], c), _half(land[a].at[me_chip], c), send_sems.at[6 * a + 2 * j + core],
                                     recv_sems.at[6 * a + 2 * j + c], (px, py, core)))
                theirs = _half(land[a].at[2 * px + py], core)
                arrivals.append(_remote(theirs, theirs, send_sems.at[6 * a + 2 * j + core],
                                        recv_sems.at[6 * a + 2 * j + core], (px, py, core)))
    return sends, arrivals


def gather_weights_start(shards, after):
    n = len(shards)

    def body(*refs):
        src, land = refs[:n], refs[n:2 * n]
        send_sems, recv_sems, token = refs[2 * n + 1], refs[2 * n + 2], refs[4 * n + 3]
        x, y, c = _place()
        for cp in _gather_async_copies(src, land, send_sems, recv_sems, x, y, c)[0]:
            cp.start()
        token[...] = jnp.zeros_like(token)

    zones = [pltpu.with_memory_space_constraint(lax.empty((N_CHIPS,) + s.shape, s.dtype), pltpu.HBM) for s in shards]
    srcs = [pltpu.with_memory_space_constraint(s, pltpu.HBM) for s in shards]
    out = pl.pallas_call(
        body, name="gather_weights_start",
        out_shape=[pltpu.SemaphoreType.DMA((6 * n,)), pltpu.SemaphoreType.DMA((6 * n,))]
        + [pltpu.HBM(s.shape, s.dtype) for s in shards] + [pltpu.HBM(z.shape, z.dtype) for z in zones]
        + [jax.ShapeDtypeStruct((8, LANES), f32)],
        in_specs=[HBM_SPEC] * (2 * n) + [pl.BlockSpec(memory_space=pl.ANY)],
        out_specs=[SEM_SPEC, SEM_SPEC] + [HBM_SPEC] * (2 * n) + [pl.BlockSpec(memory_space=pltpu.VMEM)],
        input_output_aliases={i: 2 + i for i in range(2 * n)},
        compiler_params=pltpu.CompilerParams(has_side_effects=SPLIT_EFFECT),
    )(*srcs, *zones, after)
    return out[0], out[1], out[2:2 + n], out[2 + n:2 + 2 * n], out[-1]


def gather_weights_wait(send_sems, recv_sems, shards, zones, after):
    n = len(shards)

    def body(*refs):
        src, land = refs[:n], refs[n:2 * n]
        send_sems, recv_sems = refs[2 * n], refs[2 * n + 1]
        x, y, c = _place()
        sends, arrivals = _gather_async_copies(src, land, send_sems, recv_sems, x, y, c)
        for cp in sends:
            cp.wait_send()
        for cp in arrivals:
            cp.wait_recv()

    out = pl.pallas_call(
        body, name="gather_weights_wait",
        out_shape=[pltpu.HBM(s.shape, s.dtype) for s in shards] + [pltpu.HBM(z.shape, z.dtype) for z in zones],
        in_specs=[HBM_SPEC] * (2 * n) + [SEM_SPEC, SEM_SPEC, pl.BlockSpec(memory_space=pl.ANY)],
        out_specs=[HBM_SPEC] * (2 * n),
        input_output_aliases={i: i for i in range(2 * n)},
        compiler_params=pltpu.CompilerParams(has_side_effects=SPLIT_EFFECT),
    )(*shards, *zones, send_sems, recv_sems, after)
    return out[n:]


def swap_grad_halves(grads, *, name):
    n = len(grads)

    def body(*refs):
        g, land, send_sems, recv_sems = refs[:n], refs[n:2 * n], refs[2 * n], refs[2 * n + 1]
        x, y, c = _place()
        copies = [_remote(_half(g[a], 1 - c), land[a], send_sems.at[a], recv_sems.at[a], (x, y, 1 - c)) for a in range(n)]
        for cp in copies:
            cp.start()
        for cp in copies:
            cp.wait()

    return pl.pallas_call(
        body, name=name, in_specs=[HBM_SPEC] * n, out_specs=[HBM_SPEC] * n,
        out_shape=[jax.ShapeDtypeStruct((N_CHIPS, g.shape[1] // 2, g.shape[2]), g.dtype) for g in grads],
        scratch_shapes=[pltpu.SemaphoreType.DMA((n,)), pltpu.SemaphoreType.DMA((n,))],
    )(*grads)


GRAD_ROWS = 256


def add_grad_halves(g, land, core, *, name):
    _, half, cols = land.shape
    tr = GRAD_ROWS if half % GRAD_ROWS == 0 else half
    nb = half // tr

    def body(c_ref, g_ref, l_ref, o_ref):
        o_ref[...] = (g_ref[...].astype(f32) + l_ref[...].astype(f32)).astype(o_ref.dtype)

    blk = (1, tr, cols)
    return pl.pallas_call(
        body, name=name,
        grid_spec=pltpu.PrefetchScalarGridSpec(
            num_scalar_prefetch=1, grid=(N_CHIPS, nb),
            in_specs=[pl.BlockSpec(blk, lambda k, i, c_ref: (k, c_ref[0] * nb + i, 0)),
                      pl.BlockSpec(blk, lambda k, i, c_ref: (k, i, 0))],
            out_specs=pl.BlockSpec(blk, lambda k, i, c_ref: (k, i, 0))),
        out_shape=jax.ShapeDtypeStruct(land.shape, land.dtype),
        compiler_params=_cparams("parallel", "parallel"),
    )(core, g, land)


def scatter_grads(parts):
    n = len(parts)

    def body(*refs):
        p, land, send_sems, recv_sems = refs[:n], refs[n:2 * n], refs[2 * n], refs[2 * n + 1]
        x, y, c = _place()
        me_chip = 2 * x + y
        chips = _other_chips(x, y)
        sends = [_remote(p[a].at[2 * px + py], land[a].at[me_chip], send_sems.at[3 * a + j], recv_sems.at[3 * a + j], (px, py, c))
                 for a in range(n) for j, (px, py) in enumerate(chips)]
        for cp in sends:
            cp.start()
        for a in range(n):
            for j, (px, py) in enumerate(chips):
                slot = land[a].at[2 * px + py]
                _remote(slot, slot, send_sems.at[3 * a + j], recv_sems.at[3 * a + j], (px, py, c)).wait_recv()
        for cp in sends:
            cp.wait_send()

    return pl.pallas_call(
        body, name="scatter_grads", in_specs=[HBM_SPEC] * n, out_specs=[HBM_SPEC] * n,
        out_shape=[jax.ShapeDtypeStruct(p.shape, p.dtype) for p in parts],
        scratch_shapes=[pltpu.SemaphoreType.DMA((3 * n,)), pltpu.SemaphoreType.DMA((3 * n,))],
    )(*parts)


def _scatter_async_copies(parts, land, send_sems, recv_sems, x, y, c):
    me_chip = 2 * x + y
    sends, arrivals = [], []
    for a in range(len(parts)):
        for j, (px, py) in enumerate(_other_chips(x, y)):
            sems = (send_sems.at[3 * a + j], recv_sems.at[3 * a + j], (px, py, c))
            sends.append(_remote(parts[a].at[2 * px + py], land[a].at[me_chip], *sems))
            slot = land[a].at[2 * px + py]
            arrivals.append(_remote(slot, slot, *sems))
    return sends, arrivals


def scatter_grads_start(parts, *, name):
    n = len(parts)

    def body(*refs):
        p, land = refs[:n], refs[n:2 * n]
        send_sems, recv_sems, token = refs[2 * n], refs[2 * n + 1], refs[4 * n + 2]
        x, y, c = _place()
        for cp in _scatter_async_copies(p, land, send_sems, recv_sems, x, y, c)[0]:
            cp.start()
        token[...] = jnp.zeros_like(token)

    zones = [pltpu.with_memory_space_constraint(lax.empty(p.shape, p.dtype), pltpu.HBM) for p in parts]
    srcs = [pltpu.with_memory_space_constraint(p, pltpu.HBM) for p in parts]
    hbm = [pltpu.HBM(p.shape, p.dtype) for p in parts]
    out = pl.pallas_call(
        body, name=name,
        out_shape=[pltpu.SemaphoreType.DMA((3 * n,)), pltpu.SemaphoreType.DMA((3 * n,))] + hbm + hbm
        + [jax.ShapeDtypeStruct((8, LANES), f32)],
        in_specs=[HBM_SPEC] * (2 * n),
        out_specs=[SEM_SPEC, SEM_SPEC] + [HBM_SPEC] * (2 * n) + [pl.BlockSpec(memory_space=pltpu.VMEM)],
        input_output_aliases={i: 2 + i for i in range(2 * n)},
        compiler_params=pltpu.CompilerParams(has_side_effects=SPLIT_EFFECT),
    )(*srcs, *zones)
    return out[0], out[1], out[2:2 + n], out[2 + n:2 + 2 * n], out[-1]


def scatter_grads_wait(send_sems, recv_sems, parts, zones, after, *, name):
    n = len(parts)

    def body(*refs):
        p, land = refs[:n], refs[n:2 * n]
        x, y, c = _place()
        sends, arrivals = _scatter_async_copies(p, land, refs[2 * n], refs[2 * n + 1], x, y, c)
        for cp in sends:
            cp.wait_send()
        for cp in arrivals:
            cp.wait_recv()

    hbm = [pltpu.HBM(p.shape, p.dtype) for p in parts]
    out = pl.pallas_call(
        body, name=name, out_shape=hbm + hbm,
        in_specs=[HBM_SPEC] * (2 * n) + [SEM_SPEC, SEM_SPEC, pl.BlockSpec(memory_space=pl.ANY)],
        out_specs=[HBM_SPEC] * (2 * n),
        input_output_aliases={i: i for i in range(2 * n)},
        compiler_params=pltpu.CompilerParams(has_side_effects=SPLIT_EFFECT),
    )(*parts, *zones, send_sems, recv_sems, after)
    return out[:n], out[n:]


def sum_grads(part, land, order, *, name):
    _, half, cols = part.shape
    tr = GRAD_ROWS if half % GRAD_ROWS == 0 else half

    def body(order_ref, p_ref, l1_ref, l2_ref, l3_ref, o_ref):
        o_ref[...] = ((p_ref[0].astype(f32) + l1_ref[0].astype(f32)) + l2_ref[0].astype(f32)) + l3_ref[0].astype(f32)

    slot = lambda j: pl.BlockSpec((1, tr, cols), lambda i, order_ref: (order_ref[j], i, 0))
    return pl.pallas_call(
        body, name=name,
        grid_spec=pltpu.PrefetchScalarGridSpec(
            num_scalar_prefetch=1, grid=(half // tr,), in_specs=[slot(0), slot(1), slot(2), slot(3)],
            out_specs=pl.BlockSpec((tr, cols), lambda i, order_ref: (i, 0))),
        out_shape=jax.ShapeDtypeStruct((half, cols), f32),
        compiler_params=_cparams("parallel"),
    )(order, part, land, land, land)


def _peer(x, y, c, r):
    return ((1 - x) if r & 4 else x, (1 - y) if r & 2 else y, (1 - c) if r & 1 else c)


def _reduce_async_copies(grads, land, send_sems, recv_sems, x, y, c):
    me = 4 * x + 2 * y + c
    sends, arrivals = [], []
    for a in range(len(grads)):
        for r in range(1, N_DEV):
            px, py, pc = _peer(x, y, c, r)
            sems = (send_sems.at[7 * a + r - 1], recv_sems.at[7 * a + r - 1], (px, py, pc))
            sends.append(_remote(_half(grads[a].at[2 * px + py], pc), land[a].at[me], *sems))
            slot = land[a].at[4 * px + 2 * py + pc]
            arrivals.append(_remote(slot, slot, *sems))
    return sends, arrivals


def reduce_grads_start(grads, *, name):
    n = len(grads)

    def body(*refs):
        g, land = refs[:n], refs[n:2 * n]
        send_sems, recv_sems, token = refs[2 * n], refs[2 * n + 1], refs[4 * n + 2]
        x, y, c = _place()
        for cp in _reduce_async_copies(g, land, send_sems, recv_sems, x, y, c)[0]:
            cp.start()
        token[...] = jnp.zeros_like(token)

    zones = [pltpu.with_memory_space_constraint(lax.empty((N_DEV, g.shape[1] // 2, g.shape[2]), g.dtype), pltpu.HBM)
             for g in grads]
    srcs = [pltpu.with_memory_space_constraint(g, pltpu.HBM) for g in grads]
    out = pl.pallas_call(
        body, name=name,
        out_shape=[pltpu.SemaphoreType.DMA((7 * n,)), pltpu.SemaphoreType.DMA((7 * n,))]
        + [pltpu.HBM(g.shape, g.dtype) for g in grads] + [pltpu.HBM(z.shape, z.dtype) for z in zones]
        + [jax.ShapeDtypeStruct((8, LANES), f32)],
        in_specs=[HBM_SPEC] * (2 * n),
        out_specs=[SEM_SPEC, SEM_SPEC] + [HBM_SPEC] * (2 * n) + [pl.BlockSpec(memory_space=pltpu.VMEM)],
        input_output_aliases={i: 2 + i for i in range(2 * n)},
        compiler_params=pltpu.CompilerParams(has_side_effects=SPLIT_EFFECT),
    )(*srcs, *zones)
    return out[0], out[1], out[2:2 + n], out[2 + n:2 + 2 * n], out[-1]


def reduce_grads_wait(send_sems, recv_sems, grads, zones, after, *, name):
    n = len(grads)

    def body(*refs):
        g, land = refs[:n], refs[n:2 * n]
        x, y, c = _place()
        sends, arrivals = _reduce_async_copies(g, land, refs[2 * n], refs[2 * n + 1], x, y, c)
        for cp in sends:
            cp.wait_send()
        for cp in arrivals:
            cp.wait_recv()

    hbm = [pltpu.HBM(a.shape, a.dtype) for a in list(grads) + list(zones)]
    out = pl.pallas_call(
        body, name=name, out_shape=hbm,
        in_specs=[HBM_SPEC] * (2 * n) + [SEM_SPEC, SEM_SPEC, pl.BlockSpec(memory_space=pl.ANY)],
        out_specs=[HBM_SPEC] * (2 * n),
        input_output_aliases={i: i for i in range(2 * n)},
        compiler_params=pltpu.CompilerParams(has_side_effects=SPLIT_EFFECT),
    )(*grads, *zones, send_sems, recv_sems, after)
    return out[:n], out[n:]


def sum_partials(g, land, where, *, name):
    _, half, cols = land.shape
    tr = GRAD_ROWS if half % GRAD_ROWS == 0 else half
    nb = half // tr

    def body(where_ref, g_ref, *rest):
        o_ref = rest[-1]
        acc = g_ref[0].astype(f32)
        for l_ref in rest[:-1]:
            acc = acc + l_ref[0].astype(f32)
        o_ref[...] = acc

    blk = (1, tr, cols)
    slot = lambda j: pl.BlockSpec(blk, lambda i, where_ref: (where_ref[2 + j], i, 0))
    return pl.pallas_call(
        body, name=name,
        grid_spec=pltpu.PrefetchScalarGridSpec(
            num_scalar_prefetch=1, grid=(nb,),
            in_specs=[pl.BlockSpec(blk, lambda i, where_ref: (where_ref[0], where_ref[1] * nb + i, 0))]
            + [slot(j) for j in range(N_DEV - 1)],
            out_specs=pl.BlockSpec((tr, cols), lambda i, where_ref: (i, 0))),
        out_shape=jax.ShapeDtypeStruct((half, cols), f32),
        compiler_params=_cparams("parallel"),
    )(where, g, *([land] * (N_DEV - 1)))


def swap_reduced_halves(mine, *, name):
    n = len(mine)

    def body(*refs):
        r, out, send_sems, recv_sems = refs[:n], refs[n:2 * n], refs[2 * n], refs[2 * n + 1]
        x, y, c = _place()
        copies = [_remote(r[a], out[a], send_sems.at[a], recv_sems.at[a], (x, y, 1 - c)) for a in range(n)]
        for cp in copies:
            cp.start()
        for cp in copies:
            cp.wait()

    return pl.pallas_call(
        body, name=name, in_specs=[HBM_SPEC] * n, out_specs=[HBM_SPEC] * n,
        out_shape=[jax.ShapeDtypeStruct(r.shape, r.dtype) for r in mine],
        scratch_shapes=[pltpu.SemaphoreType.DMA((n,)), pltpu.SemaphoreType.DMA((n,))],
    )(*mine)


def adamw_halves(w, mine, theirs, m, v, core, *, name):
    R, C = w.shape
    tr = min(GRAD_ROWS, R // 2)
    half_nb = R // 2 // tr

    def body(c_ref, w_ref, a_ref, b_ref, m_ref, v_ref, g_ref, d_ref, nm_ref, nv_ref):
        low = pl.program_id(0) < half_nb
        gv = jnp.where(low == (c_ref[0] == 0), a_ref[...], b_ref[...])
        nm = ADAM_B1 * m_ref[...] + (1.0 - ADAM_B1) * gv
        nv = ADAM_B2 * v_ref[...] + (1.0 - ADAM_B2) * jnp.square(gv)
        m_hat = nm / (1.0 - ADAM_B1 ** ADAM_STEP)
        v_hat = nv / (1.0 - ADAM_B2 ** ADAM_STEP)
        g_ref[...] = gv
        d_ref[...] = -ADAM_LR * (m_hat / (jnp.sqrt(v_hat) + ADAM_EPS) + ADAM_WD * w_ref[...])
        nm_ref[...] = nm
        nv_ref[...] = nv

    full = pl.BlockSpec((tr, C), lambda i, c_ref: (i, 0))
    part = pl.BlockSpec((tr, C), lambda i, c_ref: (i % half_nb, 0))
    out = jax.ShapeDtypeStruct((R, C), f32)
    return pl.pallas_call(
        body, name=name,
        grid_spec=pltpu.PrefetchScalarGridSpec(
            num_scalar_prefetch=1, grid=(2 * half_nb,), in_specs=[full, part, part, full, full], out_specs=[full] * 4),
        out_shape=[out] * 4, compiler_params=_cparams("parallel"),
    )(core, w, mine, theirs, m, v)


N_DEV = 8


def all_reduce_small(v):
    def body(src_ref, out_ref, land_ref, send_sems, recv_sems):
        x, y, c = _place()
        me = 4 * x + 2 * y + c
        copies = []
        for r in range(1, N_DEV):
            peer = ((1 - x) if r & 4 else x, (1 - y) if r & 2 else y, (1 - c) if r & 1 else c)
            copies.append(_remote(src_ref, land_ref.at[r], send_sems.at[r - 1], recv_sems.at[r - 1], peer))
        for cp in copies:
            cp.start()
        land_ref[0] = src_ref[...]
        for cp in copies:
            cp.wait()
        acc = land_ref[me]
        for d in range(1, N_DEV):
            acc = acc + land_ref[jnp.bitwise_xor(me, d)]
        out_ref[...] = acc

    vm = pl.BlockSpec(memory_space=pltpu.VMEM)
    return pl.pallas_call(
        body, name="all_reduce_small", in_specs=[vm], out_specs=vm,
        out_shape=jax.ShapeDtypeStruct(v.shape, v.dtype),
        scratch_shapes=[pltpu.VMEM((N_DEV,) + v.shape, v.dtype),
                        pltpu.SemaphoreType.DMA((N_DEV - 1,)), pltpu.SemaphoreType.DMA((N_DEV - 1,))],
    )(v)


def adamw(w, g, m, v, *, name, tr=None, tc=None):
    R, C = w.shape
    if tc is None:
        tr, tc = min(tr, R), C
        blk = pl.BlockSpec((tr, C), lambda i: (i, 0))
    else:
        tr = R
        blk = pl.BlockSpec((R, tc), lambda i: (0, i))

    def body(w_ref, g_ref, m_ref, v_ref, d_ref, nm_ref, nv_ref):
        gv = g_ref[...]
        nm = ADAM_B1 * m_ref[...] + (1.0 - ADAM_B1) * gv
        nv = ADAM_B2 * v_ref[...] + (1.0 - ADAM_B2) * jnp.square(gv)
        m_hat = nm / (1.0 - ADAM_B1 ** ADAM_STEP)
        v_hat = nv / (1.0 - ADAM_B2 ** ADAM_STEP)
        d_ref[...] = -ADAM_LR * (m_hat / (jnp.sqrt(v_hat) + ADAM_EPS) + ADAM_WD * w_ref[...])
        nm_ref[...] = nm
        nv_ref[...] = nv

    out = jax.ShapeDtypeStruct((R, C), f32)
    return pl.pallas_call(
        body, name=name, grid=((R // tr) * (C // tc),), in_specs=[blk] * 4, out_specs=[blk] * 3, out_shape=[out] * 3,
        compiler_params=_cparams("parallel"),
    )(w, g, m, v)


BIG_SHARDS = (("w_in", (1024, 900), True), ("w_out", (256, 1024), False), ("w_cq", (256, 512), False),
              ("w_ckv", (256, 1024), False), ("w_co", (512, 256), True), ("w_mlp1", (1024, 1024), True),
              ("w_mlp2", (1024, 1024), False))
CONV_SHARD = (CONV_WIDTH, 3 * GDN_WIDTH // N_CHIPS)
SMALL_DIMS = (("norm_mix_g", 1024), ("fox_qnorm_g", 64), ("fox_knorm_g", 64), ("fox_f_bias", 8), ("fox_onorm_g", 64),
              ("gdn_A_log", 4), ("gdn_dt_bias", 4), ("gdn_onorm_g", 128), ("norm_xattn_g", 1024), ("mem_norm_g", 1024),
              ("xattn_qnorm_g", 128), ("xattn_knorm_g", 128), ("norm_mlp_g", 1024))
WEIGHT_ORDER = ("norm_mix_g", "w_in", "fox_qnorm_g", "fox_knorm_g", "fox_f_bias", "fox_onorm_g", "gdn_conv_w", "gdn_A_log",
                "gdn_dt_bias", "gdn_onorm_g", "w_out", "norm_xattn_g", "mem_norm_g", "w_cq", "w_ckv", "xattn_qnorm_g",
                "xattn_knorm_g", "w_co", "norm_mlp_g", "w_mlp1", "w_mlp2")


def _pack_rows(pieces, rows, lead=()):
    cat = jnp.concatenate([p.reshape(lead + (-1,)) for p in pieces], axis=-1)
    cat = jnp.pad(cat, [(0, 0)] * len(lead) + [(0, rows * LANES - cat.shape[-1])])
    return cat.reshape(lead + (rows, LANES))


def _unpack_rows(buf, sizes, lead=()):
    flat = buf.reshape(lead + (-1,))
    out, off = [], 0
    for n in sizes:
        out.append(flat[..., off:off + n])
        off += n
    return out


def _conv_to_wire(conv):
    return lax.bitcast_convert_type(conv, bf16)


def _conv_from_wire(wire):
    return lax.bitcast_convert_type(wire, f32)


SMALL_ROWS = 96
SMALL_ADAM_ROWS = 56


def kernel(x, mem, norm_mix_g, w_in, fox_qnorm_g, fox_knorm_g, fox_f_bias, fox_onorm_g, gdn_conv_w, gdn_A_log, gdn_dt_bias, gdn_onorm_g, w_out, norm_xattn_g, mem_norm_g, w_cq, w_ckv, xattn_qnorm_g, xattn_knorm_g, w_co, norm_mlp_g, w_mlp1, w_mlp2, loss_target, m_norm_mix_g, m_w_in, m_fox_qnorm_g, m_fox_knorm_g, m_fox_f_bias, m_fox_onorm_g, m_gdn_conv_w, m_gdn_A_log, m_gdn_dt_bias, m_gdn_onorm_g, m_w_out, m_norm_xattn_g, m_mem_norm_g, m_w_cq, m_w_ckv, m_xattn_qnorm_g, m_xattn_knorm_g, m_w_co, m_norm_mlp_g, m_w_mlp1, m_w_mlp2, v_norm_mix_g, v_w_in, v_fox_qnorm_g, v_fox_knorm_g, v_fox_f_bias, v_fox_onorm_g, v_gdn_conv_w, v_gdn_A_log, v_gdn_dt_bias, v_gdn_onorm_g, v_w_out, v_norm_xattn_g, v_mem_norm_g, v_w_cq, v_w_ckv, v_xattn_qnorm_g, v_xattn_knorm_g, v_w_co, v_norm_mlp_g, v_w_mlp1, v_w_mlp2):
    wts = dict(norm_mix_g=norm_mix_g, w_in=w_in, fox_qnorm_g=fox_qnorm_g, fox_knorm_g=fox_knorm_g, fox_f_bias=fox_f_bias,
               fox_onorm_g=fox_onorm_g, gdn_conv_w=gdn_conv_w, gdn_A_log=gdn_A_log, gdn_dt_bias=gdn_dt_bias,
               gdn_onorm_g=gdn_onorm_g, w_out=w_out, norm_xattn_g=norm_xattn_g, mem_norm_g=mem_norm_g, w_cq=w_cq, w_ckv=w_ckv,
               xattn_qnorm_g=xattn_qnorm_g, xattn_knorm_g=xattn_knorm_g, w_co=w_co, norm_mlp_g=norm_mlp_g, w_mlp1=w_mlp1,
               w_mlp2=w_mlp2)
    mom = dict(norm_mix_g=m_norm_mix_g, w_in=m_w_in, fox_qnorm_g=m_fox_qnorm_g, fox_knorm_g=m_fox_knorm_g,
               fox_f_bias=m_fox_f_bias, fox_onorm_g=m_fox_onorm_g, gdn_conv_w=m_gdn_conv_w, gdn_A_log=m_gdn_A_log,
               gdn_dt_bias=m_gdn_dt_bias, gdn_onorm_g=m_gdn_onorm_g, w_out=m_w_out, norm_xattn_g=m_norm_xattn_g,
               mem_norm_g=m_mem_norm_g, w_cq=m_w_cq, w_ckv=m_w_ckv, xattn_qnorm_g=m_xattn_qnorm_g,
               xattn_knorm_g=m_xattn_knorm_g, w_co=m_w_co, norm_mlp_g=m_norm_mlp_g, w_mlp1=m_w_mlp1, w_mlp2=m_w_mlp2)
    var = dict(norm_mix_g=v_norm_mix_g, w_in=v_w_in, fox_qnorm_g=v_fox_qnorm_g, fox_knorm_g=v_fox_knorm_g,
               fox_f_bias=v_fox_f_bias, fox_onorm_g=v_fox_onorm_g, gdn_conv_w=v_gdn_conv_w, gdn_A_log=v_gdn_A_log,
               gdn_dt_bias=v_gdn_dt_bias, gdn_onorm_g=v_gdn_onorm_g, w_out=v_w_out, norm_xattn_g=v_norm_xattn_g,
               mem_norm_g=v_mem_norm_g, w_cq=v_w_cq, w_ckv=v_w_ckv, xattn_qnorm_g=v_xattn_qnorm_g,
               xattn_knorm_g=v_xattn_knorm_g, w_co=v_w_co, norm_mlp_g=v_norm_mlp_g, w_mlp1=v_w_mlp1, w_mlp2=v_w_mlp2)
    B, S, D = x.shape
    T = B * S
    big_names = [n for n, _, _ in BIG_SHARDS]
    chip = 2 * lax.axis_index("x") + lax.axis_index("y")
    core = lax.axis_index("c").astype(jnp.int32).reshape(1)

    shards = {n: wts[n][0].astype(MXU_DTYPE) for n in big_names[1:]}
    in_t = lambda p: jnp.swapaxes(p[0], 0, 1)
    shards["w_in"] = jnp.pad(in_t(w_in).astype(MXU_DTYPE), ((0, IN_SHARD_PAD - IN_SHARD), (0, 0)))
    w_in_all, conv_all = gather_weights([shards["w_in"]], gdn_conv_w[0])
    late = big_names[1:]
    send_sems, recv_sems, late_src, late_zones, token = gather_weights_start([shards[n] for n in late], conv_all)
    own = lambda g, s: lax.dynamic_update_slice(g, s[None], (chip,) + (0,) * s.ndim)
    full = {"w_in": own(w_in_all, shards["w_in"])}
    conv_full = own(conv_all, gdn_conv_w[0]).transpose(1, 0, 2).reshape(CONV_WIDTH, 3 * GDN_WIDTH)
    rows = lambda g: g.reshape(N_CHIPS * g.shape[1], g.shape[2])
    w_in_t = full["w_in"][:, :IN_SHARD].reshape(IN_DIM, D_MODEL)

    def late_weights(after):
        zones = gather_weights_wait(send_sems, recv_sems, late_src, late_zones, after)
        got = {n: own(z, shards[n]) for n, z in zip(late, zones)}
        return dict(w_out=rows(got["w_out"]), w_cq=rows(got["w_cq"]), w_ckv=rows(got["w_ckv"]), w_co=got["w_co"],
                    w_mlp1=got["w_mlp1"], w_mlp2=rows(got["w_mlp2"]))

    in_flight = []

    def grads_ready(ready):
        names = list(ready)
        *started, tok = reduce_grads_start([ready[n] for n in names], name="reduce_grads_start_%d" % len(in_flight))
        in_flight.append((names, *started))
        return tok

    w = dict(wa_t=align_w_in_t(w_in_t), conv_w=conv_full, late=late_weights, grads_ready=grads_ready)
    sp = {n: wts[n] for n, _ in SMALL_DIMS}
    sp["norm_mix_g"] = sp["norm_mix_g"] + token[0, 0]

    loss_part, grad_x, g_big, g_small = local_step(x.reshape(T, D), mem.reshape(-1, D), loss_target.reshape(T, D), w, sp, B=B)

    small_pieces = [g_small[n] for n, _ in SMALL_DIMS] + [g_small["gdn_conv_w"], loss_part]
    small_sizes = [d for _, d in SMALL_DIMS] + [CONV_WIDTH * 3 * GDN_WIDTH, LANES]
    red_small = _unpack_rows(all_reduce_small(_pack_rows(small_pieces, SMALL_ROWS)), small_sizes)
    grads = {n: p.reshape(1, d) for (n, d), p in zip(SMALL_DIMS, red_small)}
    conv_grad = lax.dynamic_slice(red_small[-2].reshape(CONV_WIDTH, 3 * GDN_WIDTH), (0, chip * CONV_SHARD[1]), CONV_SHARD)
    grads["gdn_conv_w"] = conv_grad.reshape((1,) + CONV_SHARD)
    loss = red_small[-1][0]

    tok_in = grads_ready({"w_in": g_big["w_in"]})
    parts, zones = {}, {}

    def wait_group(k, after):
        names, send_sems, recv_sems, thru, land = in_flight[k]
        thru, land = reduce_grads_wait(send_sems, recv_sems, thru, land, after, name="reduce_grads_wait_%d" % k)
        parts.update(zip(names, thru))
        zones.update(zip(names, land))

    wait_group(0, tok_in)
    wait_group(1, tok_in)
    dev = 2 * chip + core[0]
    where = jnp.stack([chip, core[0]] + [dev ^ r for r in range(1, N_DEV)]).astype(jnp.int32)
    mine = [sum_partials(parts[n], zones[n], where, name="sum_partials_" + n) for n in late]
    theirs = swap_reduced_halves(mine, name="swap_reduced_halves")

    delta, new_m, new_v = {}, {}, {}
    for n, a, b in zip(late, mine, theirs):
        g, d, nm, nv = adamw_halves(wts[n][0], a, b, mom[n][0], var[n][0], core, name="adamw_" + n)
        grads[n], delta[n], new_m[n], new_v[n] = g[None], d[None], nm[None], nv[None]
    wait_group(2, new_v[late[-1]])
    mine_in = sum_partials(parts["w_in"], zones["w_in"], where, name="sum_partials_w_in")
    (theirs_in,) = swap_reduced_halves([mine_in], name="swap_reduced_halves_w_in")
    south = core[0] == 0
    g_in_t = jnp.concatenate([jnp.where(south, mine_in, theirs_in), jnp.where(south, theirs_in, mine_in)])[:IN_SHARD]
    back = lambda t: jnp.swapaxes(t, 0, 1)[None]
    d, nm, nv = adamw(in_t(w_in), g_in_t, in_t(m_w_in), in_t(v_w_in), name="adamw_w_in", tc=256)
    grads["w_in"], delta["w_in"], new_m["w_in"], new_v["w_in"] = back(g_in_t), back(d), back(nm), back(nv)
    small_names = [n for n, _ in SMALL_DIMS] + ["gdn_conv_w"]
    small_sz = [d for _, d in SMALL_DIMS] + [CONV_SHARD[0] * CONV_SHARD[1]]
    packed4 = [_pack_rows([src[n] for n in small_names], SMALL_ADAM_ROWS) for src in (wts, grads, mom, var)]
    outs = adamw(*packed4, name="adamw_small", tr=SMALL_ADAM_ROWS)
    for dst, buf in zip((delta, new_m, new_v), outs):
        for n, p in zip(small_names, _unpack_rows(buf, small_sz)):
            dst[n] = p.reshape(wts[n].shape)

    return (loss, grad_x.reshape(B, S, D), *[grads[n] for n in WEIGHT_ORDER], *[delta[n] for n in WEIGHT_ORDER],
            *[new_m[n] for n in WEIGHT_ORDER], *[new_v[n] for n in WEIGHT_ORDER])
```

```python
import functools

import jax
import jax.numpy as jnp
import numpy as np
from jax import lax
from jax.experimental import pallas as pl
from jax.experimental.pallas import tpu as pltpu

f32 = jnp.float32
bf16 = jnp.bfloat16
MXU_DTYPE = jnp.bfloat16
WIRE_DTYPE = jnp.bfloat16
INV_PRECISION = lax.Precision.HIGH

D_MODEL = 1024
FOX_HEADS = 8
FOX_HEAD_DIM = 64
FOX_WIDTH = 512
GDN_HEADS = 4
GDN_HEAD_DIM = 128
GDN_WIDTH = 512
CONV_WIDTH = 4
GDN_CHUNK = 64
XATTN_HEADS = 4
XATTN_HEAD_DIM = 128
XATTN_WIDTH = 512
D_FF = 4096
IN_DIM = 3600
EPS = 1e-6
NEG_INF = -1e30
LANES = 128
ADAM_LR = 0.001
ADAM_B1 = 0.9
ADAM_B2 = 0.999
ADAM_EPS = 1e-08
ADAM_WD = 0.01
ADAM_STEP = 10
VMEM_LIMIT = 48 * 1024 * 1024

COL_FOX = 0
COL_GDN = 1536
COL_Z = 3072
COL_SMALL = 3584
IN_ALIGNED = 3840
IN_TILE = 768
SM_F = 0
SM_B = 8
SM_A = 12


def _cparams(*sem):
    return pltpu.CompilerParams(dimension_semantics=sem, vmem_limit_bytes=VMEM_LIMIT)


def _mx(v):
    return v.astype(MXU_DTYPE)


def _dot(a, b, dims, precision=None):
    return lax.dot_general(a, b, (dims, ((), ())), preferred_element_type=f32, precision=precision)


def _dotm(a, b, dims):
    return _dot(_mx(a), _mx(b), dims)


NN = ((1,), (0,))
NT = ((1,), (1,))
TN = ((0,), (0,))


def matmul(a, b, *, name, ta=False, tb=False, b_stacked=False, out_stacked=False, residual=None, relu2_out=False,
           relu2_bwd_aux=None, out_dtype=f32, tm=1024, tn=1024, tk=1024):
    M, K = (a.shape[1], a.shape[0]) if ta else a.shape
    if b_stacked:
        b_cols = b.shape[2]
        N, tk = (b.shape[1], min(tk, b_cols)) if tb else (N_CHIPS * b_cols, tk)
        tn = tn if tb else min(tn, b_cols)
        assert K == (N_CHIPS * b_cols if tb else b.shape[1]), (name, a.shape, b.shape)
    else:
        N = b.shape[0] if tb else b.shape[1]
    if out_stacked:
        tn = min(tn, N // N_CHIPS)
    tm, tn, tk = min(tm, M), min(tn, N), min(tk, K)
    assert M % tm == 0 and N % tn == 0 and K % tk == 0, (name, M, N, K)
    nk = K // tk
    has_res = residual is not None
    has_aux = relu2_bwd_aux is not None

    def body(*refs):
        a_ref, b_ref = refs[0], refs[1]
        pos = 2
        res_ref = aux_ref = None
        if has_res:
            res_ref = refs[pos]
            pos += 1
        if has_aux:
            aux_ref = refs[pos]
            pos += 1
        o_ref = refs[pos]
        k = pl.program_id(2)
        dims = ((0,) if ta else (1,), (1,) if tb else (0,))
        part = _dot(_mx(a_ref[...]), _mx(b_ref[...]), dims)

        def finish(r):
            if has_res:
                r = r + res_ref[...]
            if has_aux:
                r = r * (2.0 * jnp.sqrt(aux_ref[...].astype(f32)))
            if relu2_out:
                o_ref[...] = jnp.square(jnp.maximum(r, 0.0)).astype(o_ref.dtype)
            else:
                o_ref[...] = r.astype(o_ref.dtype)

        if nk == 1:
            finish(part)
            return
        acc_ref = refs[pos + 1]

        @pl.when(k == 0)
        def _():
            acc_ref[...] = part

        @pl.when((k > 0) & (k < nk - 1))
        def _():
            acc_ref[...] += part

        @pl.when(k == nk - 1)
        def _():
            finish(acc_ref[...] + part)

    a_spec = pl.BlockSpec((tk, tm), lambda i, j, k: (k, i)) if ta else pl.BlockSpec((tm, tk), lambda i, j, k: (i, k))
    if b_stacked and tb:
        per = b_cols // tk
        b_spec = pl.BlockSpec((None, tn, tk), lambda i, j, k: (k // per, j, k % per))
    elif b_stacked:
        per = b_cols // tn
        b_spec = pl.BlockSpec((None, tk, tn), lambda i, j, k: (j // per, k, j % per))
    else:
        b_spec = pl.BlockSpec((tn, tk), lambda i, j, k: (j, k)) if tb else pl.BlockSpec((tk, tn), lambda i, j, k: (k, j))
    if out_stacked:
        assert not (has_res or has_aux or relu2_out), name
        per_o = N // N_CHIPS // tn
        o_spec = pl.BlockSpec((None, tm, tn), lambda i, j, k: (j // per_o, i, j % per_o))
        out_full = (N_CHIPS, M, N // N_CHIPS)
    else:
        o_spec = pl.BlockSpec((tm, tn), lambda i, j, k: (i, j))
        out_full = (M, N)
    in_specs, args = [a_spec, b_spec], [a, b]
    if has_res:
        in_specs.append(o_spec)
        args.append(residual)
    if has_aux:
        in_specs.append(o_spec)
        args.append(relu2_bwd_aux)
    out_shape = [jax.ShapeDtypeStruct(out_full, out_dtype)]
    out_specs = [o_spec]
    res = pl.pallas_call(
        body, name=name, grid=(M // tm, N // tn, nk), in_specs=in_specs, out_specs=out_specs, out_shape=out_shape,
        scratch_shapes=[pltpu.VMEM((tm, tn), f32)] if nk > 1 else [],
        compiler_params=_cparams("parallel", "parallel", "arbitrary"),
    )(*args)
    return res[0]


def matmul_rows(a, b, extras, *, name, mode, tb=False, b_stacked=False, tm=1024, tk=1024):
    M, K = a.shape
    N = D_MODEL
    if b_stacked:
        assert tb, name
        tk = min(tk, b.shape[2])
        per = b.shape[2] // tk
        b_spec = pl.BlockSpec((None, N, tk), lambda i, k: (k // per, 0, k % per))
    elif tb:
        tk = min(tk, K)
        b_spec = pl.BlockSpec((N, tk), lambda i, k: (0, k))
    else:
        tk = min(tk, K)
        b_spec = pl.BlockSpec((tk, N), lambda i, k: (k, 0))
    tm = min(tm, M)
    assert M % tm == 0 and K % tk == 0, (name, M, K)
    nk = K // tk
    extras = [e for e in extras if e is not None]
    n_ex = len(extras)

    def body(*refs):
        a_ref, b_ref = refs[0], refs[1]
        ex = refs[2:2 + n_ex]
        o_ref = refs[2 + n_ex]
        n_out = 3 if mode == "loss" else 2
        s_ref = refs[1 + n_ex + n_out]
        i, k = pl.program_id(0), pl.program_id(1)
        part = _dot(_mx(a_ref[...]), _mx(b_ref[...]), ((1,), (1,) if tb else (0,)))

        def finish(y):
            @pl.when(i == 0)
            def _():
                s_ref[...] = jnp.zeros_like(s_ref)

            if mode == "rms_bwd":
                xv, gv = ex[0][...], ex[1][...]
                rstd = lax.rsqrt(jnp.mean(xv * xv, axis=-1, keepdims=True) + EPS)
                xhat = xv * rstd
                gd = y * gv
                dx = rstd * (gd - xhat * jnp.mean(gd * xhat, axis=-1, keepdims=True))
                o_ref[...] = dx + ex[2][...] if n_ex == 3 else dx
                s_ref[...] += jnp.sum(y * xhat, axis=0, keepdims=True)
            else:
                e = y + ex[0][...] - ex[1][...]
                o_ref[...] = e * (1.0 / N)
                refs[3 + n_ex][...] = (e * (1.0 / N)).astype(MXU_DTYPE)
                tot = 0.5 * jnp.sum(jnp.mean(e * e, axis=-1, keepdims=True), axis=0, keepdims=True)
                s_ref[...] += jnp.broadcast_to(tot, s_ref.shape)

        if nk == 1:
            finish(part)
            return
        acc_ref = refs[2 + n_ex + n_out]

        @pl.when(k == 0)
        def _():
            acc_ref[...] = part

        @pl.when((k > 0) & (k < nk - 1))
        def _():
            acc_ref[...] += part

        @pl.when(k == nk - 1)
        def _():
            finish(acc_ref[...] + part)

    row = pl.BlockSpec((tm, N), lambda i, k: (i, 0))
    vec = pl.BlockSpec((1, N), lambda i, k: (0, 0))
    if mode == "rms_bwd":
        ex_specs = [row, vec] + ([row] if n_ex == 3 else [])
        s_shape, s_spec = jax.ShapeDtypeStruct((1, N), f32), vec
    else:
        ex_specs = [row, row]
        s_shape, s_spec = jax.ShapeDtypeStruct((1, LANES), f32), pl.BlockSpec((1, LANES), lambda i, k: (0, 0))
    return pl.pallas_call(
        body, name=name, grid=(M // tm, nk),
        in_specs=[pl.BlockSpec((tm, tk), lambda i, k: (i, k)), b_spec] + ex_specs,
        out_specs=[row] * (2 if mode == "loss" else 1) + [s_spec],
        out_shape=[jax.ShapeDtypeStruct((M, N), f32)] + ([jax.ShapeDtypeStruct((M, N), MXU_DTYPE)] if mode == "loss" else [])
        + [s_shape],
        scratch_shapes=[pltpu.VMEM((tm, N), f32)] if nk > 1 else [],
        compiler_params=_cparams("arbitrary", "arbitrary"),
    )(a, b, *extras)


def rms_fwd(x, g, *, name, tr=512):
    R, D = x.shape
    tr = min(tr, R)

    def body(x_ref, g_ref, o_ref):
        xv = x_ref[...]
        y = xv * lax.rsqrt(jnp.mean(xv * xv, axis=-1, keepdims=True) + EPS)
        o_ref[...] = (y * g_ref[...]).astype(o_ref.dtype)

    return pl.pallas_call(
        body, name=name, grid=(R // tr,),
        in_specs=[pl.BlockSpec((tr, D), lambda i: (i, 0)), pl.BlockSpec((1, D), lambda i: (0, 0))],
        out_specs=pl.BlockSpec((tr, D), lambda i: (i, 0)),
        out_shape=jax.ShapeDtypeStruct((R, D), MXU_DTYPE),
        compiler_params=_cparams("parallel"),
    )(x, g)


def rms_bwd(x, g, dh, residual, *, name, tr=512):
    R, D = x.shape
    tr = min(tr, R)
    has_res = residual is not None

    def body(*refs):
        if has_res:
            x_ref, g_ref, dh_ref, res_ref, dx_ref, dg_ref = refs
        else:
            x_ref, g_ref, dh_ref, dx_ref, dg_ref = refs
        xv = x_ref[...]
        rstd = lax.rsqrt(jnp.mean(xv * xv, axis=-1, keepdims=True) + EPS)
        xhat = xv * rstd
        dh = dh_ref[...].astype(f32)
        gd = dh * g_ref[...]
        dx = rstd * (gd - xhat * jnp.mean(gd * xhat, axis=-1, keepdims=True))
        if has_res:
            dx = dx + res_ref[...]
        dx_ref[...] = dx

        @pl.when(pl.program_id(0) == 0)
        def _():
            dg_ref[...] = jnp.zeros_like(dg_ref)

        dg_ref[...] += jnp.sum(dh * xhat, axis=0, keepdims=True)

    row = pl.BlockSpec((tr, D), lambda i: (i, 0))
    vec = pl.BlockSpec((1, D), lambda i: (0, 0))
    in_specs = [row, vec, row] + ([row] if has_res else [])
    args = [x, g, dh] + ([residual] if has_res else [])
    return pl.pallas_call(
        body, name=name, grid=(R // tr,), in_specs=in_specs, out_specs=[row, vec],
        out_shape=[jax.ShapeDtypeStruct((R, D), f32), jax.ShapeDtypeStruct((1, D), f32)],
        compiler_params=_cparams("arbitrary"),
    )(*args)


def loss_head(y, target, *, tr=512):
    R, D = y.shape
    tr = min(tr, R)

    def body(y_ref, t_ref, dy_ref, loss_ref):
        e = y_ref[...] - t_ref[...]
        dy_ref[...] = e * (1.0 / D)

        @pl.when(pl.program_id(0) == 0)
        def _():
            loss_ref[...] = jnp.zeros_like(loss_ref)

        part = 0.5 * jnp.sum(jnp.mean(e * e, axis=-1, keepdims=True), axis=0, keepdims=True)
        loss_ref[...] += jnp.broadcast_to(part, loss_ref.shape)

    row = pl.BlockSpec((tr, D), lambda i: (i, 0))
    return pl.pallas_call(
        body, name="loss_head", grid=(R // tr,), in_specs=[row, row],
        out_specs=[row, pl.BlockSpec((1, LANES), lambda i: (0, 0))],
        out_shape=[jax.ShapeDtypeStruct((R, D), f32), jax.ShapeDtypeStruct((1, LANES), f32)],
        compiler_params=_cparams("arbitrary"),
    )(y, target)


def _head_rms(v, g):
    r = lax.rsqrt(jnp.mean(v * v, axis=-1, keepdims=True) + EPS)
    return v * r * g, r


def _head_rms_bwd(v, r, g, dn):
    vhat = v * r
    gd = dn * g
    dv = r * (gd - vhat * jnp.mean(gd * vhat, axis=-1, keepdims=True))
    return dv, jnp.sum(dn * vhat, axis=0, keepdims=True)


def _softmax_rows(s):
    m = jnp.max(s, axis=-1, keepdims=True)
    e = jnp.exp(s - m)
    return e / jnp.sum(e, axis=-1, keepdims=True)


def xattn_fwd(cq, ckv, gq, gk, *, B, tq=512):
    T = cq.shape[0]
    S = T // B
    M = ckv.shape[0] // B
    tq = min(tq, S)
    nq = S // tq
    hd, W = XATTN_HEAD_DIM, XATTN_WIDTH
    scale = hd ** -0.5

    def body(q_ref, k_ref, v_ref, gq_ref, gk_ref, o_ref):
        for h in range(XATTN_HEADS):
            sl = slice(h * hd, (h + 1) * hd)
            qn, _ = _head_rms(q_ref[:, sl], gq_ref[...])
            kn, _ = _head_rms(k_ref[:, sl], gk_ref[...])
            p = _softmax_rows(_dot(_mx(qn), _mx(kn), NT) * scale)
            o_ref[:, sl] = _dot(_mx(p), _mx(v_ref[:, sl]), NN).astype(o_ref.dtype)

    vec = pl.BlockSpec((1, hd), lambda b, i: (0, 0))
    qspec = pl.BlockSpec((tq, W), lambda b, i: (b * nq + i, 0))
    return pl.pallas_call(
        body, name="xattn_fwd", grid=(B, nq),
        in_specs=[qspec, pl.BlockSpec((M, W), lambda b, i: (b, 0)), pl.BlockSpec((M, W), lambda b, i: (b, 1)), vec, vec],
        out_specs=qspec, out_shape=jax.ShapeDtypeStruct((T, W), MXU_DTYPE),
        compiler_params=_cparams("parallel", "parallel"),
    )(cq, ckv, ckv, gq, gk)


def xattn_bwd(cq, ckv, gq, gk, dco, *, B, tq=512):
    T = cq.shape[0]
    S = T // B
    M = ckv.shape[0] // B
    tq = min(tq, S)
    nq = S // tq
    hd, W = XATTN_HEAD_DIM, XATTN_WIDTH
    scale = hd ** -0.5

    def body(q_ref, k_ref, v_ref, gq_ref, gk_ref, do_ref, dq_ref, dkv_ref, dgq_ref, dgk_ref, dkn_acc, dv_acc):
        b, i = pl.program_id(0), pl.program_id(1)

        @pl.when((b == 0) & (i == 0))
        def _():
            dgq_ref[...] = jnp.zeros_like(dgq_ref)
            dgk_ref[...] = jnp.zeros_like(dgk_ref)

        @pl.when(i == 0)
        def _():
            dkn_acc[...] = jnp.zeros_like(dkn_acc)
            dv_acc[...] = jnp.zeros_like(dv_acc)

        gqv, gkv = gq_ref[...], gk_ref[...]
        for h in range(XATTN_HEADS):
            sl = slice(h * hd, (h + 1) * hd)
            q, k, v = q_ref[:, sl], k_ref[:, sl], v_ref[:, sl]
            qn, rq = _head_rms(q, gqv)
            kn, _ = _head_rms(k, gkv)
            p = _softmax_rows(_dot(_mx(qn), _mx(kn), NT) * scale)
            do = do_ref[:, sl]
            dv_acc[:, sl] += _dot(_mx(p), _mx(do), TN)
            dp = _dot(_mx(do), _mx(v), NT)
            ds = p * (dp - jnp.sum(dp * p, axis=-1, keepdims=True)) * scale
            dqn = _dot(_mx(ds), _mx(kn), NN)
            dkn_acc[:, sl] += _dot(_mx(ds), _mx(qn), TN)
            dq, dgq = _head_rms_bwd(q, rq, gqv, dqn)
            dq_ref[:, sl] = dq.astype(dq_ref.dtype)
            dgq_ref[...] += dgq

        @pl.when(i == nq - 1)
        def _():
            for h in range(XATTN_HEADS):
                sl = slice(h * hd, (h + 1) * hd)
                k = k_ref[:, sl]
                rk = lax.rsqrt(jnp.mean(k * k, axis=-1, keepdims=True) + EPS)
                dk, dgk = _head_rms_bwd(k, rk, gkv, dkn_acc[:, sl])
                dkv_ref[:, sl] = dk.astype(dkv_ref.dtype)
                dkv_ref[:, slice(W + h * hd, W + (h + 1) * hd)] = dv_acc[:, sl].astype(dkv_ref.dtype)
                dgk_ref[...] += dgk

    vec = pl.BlockSpec((1, hd), lambda b, i: (0, 0))
    qspec = pl.BlockSpec((tq, W), lambda b, i: (b * nq + i, 0))
    return pl.pallas_call(
        body, name="xattn_bwd", grid=(B, nq),
        in_specs=[qspec, pl.BlockSpec((M, W), lambda b, i: (b, 0)), pl.BlockSpec((M, W), lambda b, i: (b, 1)), vec, vec, qspec],
        out_specs=[qspec, pl.BlockSpec((M, 2 * W), lambda b, i: (b, 0)), vec, vec],
        out_shape=[jax.ShapeDtypeStruct((T, W), MXU_DTYPE), jax.ShapeDtypeStruct((B * M, 2 * W), MXU_DTYPE),
                   jax.ShapeDtypeStruct((1, hd), f32), jax.ShapeDtypeStruct((1, hd), f32)],
        scratch_shapes=[pltpu.VMEM((M, W), f32), pltpu.VMEM((M, W), f32)],
        compiler_params=_cparams("arbitrary", "arbitrary"),
    )(cq, ckv, ckv, gq, gk, dco)


FOX_PAIRS = FOX_HEADS // 2


def _fox_scores(qn, kn, ccol, crow, q0, tq, S, scale):
    s = _dot(_mx(qn), _mx(kn), NT) * scale + ccol - crow
    qpos = q0 + lax.broadcasted_iota(jnp.int32, (tq, S), 0)
    kpos = lax.broadcasted_iota(jnp.int32, (tq, S), 1)
    return jnp.where(kpos <= qpos, s, NEG_INF)


def fox_fwd(P, ccol, crow, gq, gk, go, *, B, tq=256):
    T = P.shape[0]
    S = T // B
    tq = min(tq, S)
    nq = S // tq
    hd = FOX_HEAD_DIM
    scale = hd ** -0.5

    def body(q_ref, k_ref, v_ref, ccol_ref, crow_ref, gq_ref, gk_ref, go_ref, o_ref, oa_ref):
        q0 = pl.program_id(2) * tq
        for e in range(2):
            sl = slice(e * hd, (e + 1) * hd)
            qn, _ = _head_rms(q_ref[:, sl], gq_ref[:, sl])
            kn, _ = _head_rms(k_ref[:, sl], gk_ref[:, sl])
            p = _softmax_rows(_fox_scores(qn, kn, ccol_ref[0, e], crow_ref[0, e], q0, tq, S, scale))
            o = _dot(_mx(p), _mx(v_ref[:, sl]), NN)
            o_ref[:, sl] = o
            oa_ref[:, sl] = _head_rms(o, go_ref[:, sl])[0].astype(oa_ref.dtype)

    W = 2 * hd
    vec = pl.BlockSpec((1, W), lambda b, h, i: (0, 0))
    ospec = pl.BlockSpec((tq, W), lambda b, h, i: (b * nq + i, h))
    return pl.pallas_call(
        body, name="fox_fwd", grid=(B, FOX_PAIRS, nq),
        in_specs=[pl.BlockSpec((tq, W), lambda b, h, i: (b * nq + i, h)),
                  pl.BlockSpec((S, W), lambda b, h, i: (b, FOX_PAIRS + h)),
                  pl.BlockSpec((S, W), lambda b, h, i: (b, 2 * FOX_PAIRS + h)),
                  pl.BlockSpec((1, 2, tq, 1), lambda b, h, i: (b, h, i, 0)),
                  pl.BlockSpec((1, 2, 1, S), lambda b, h, i: (b, h, 0, 0)), vec, vec, vec],
        out_specs=[ospec, ospec],
        out_shape=[jax.ShapeDtypeStruct((T, FOX_WIDTH), f32), jax.ShapeDtypeStruct((T, FOX_WIDTH), MXU_DTYPE)],
        compiler_params=_cparams("parallel", "parallel", "parallel"),
    )(P, P, P, ccol, crow, gq, gk, go)


def fox_bwd(P, ccol, crow, gq, gk, go, o_raw, d_oab, *, B, tq=256):
    T = P.shape[0]
    S = T // B
    tq = min(tq, S)
    nq = S // tq
    hd = FOX_HEAD_DIM
    scale = hd ** -0.5

    def body(q_ref, k_ref, v_ref, ccol_ref, crow_ref, gq_ref, gk_ref, go_ref, o_ref, doa_ref,
             dq_ref, dk_ref, dv_ref, dccol_ref, dcrow_ref, dgq_ref, dgk_ref, dgo_ref, dkn_acc, dv_acc, dcrow_acc):
        b, h, i = pl.program_id(0), pl.program_id(1), pl.program_id(2)
        q0 = i * tq

        @pl.when((b == 0) & (h == 0) & (i == 0))
        def _():
            dgq_ref[...] = jnp.zeros_like(dgq_ref)
            dgk_ref[...] = jnp.zeros_like(dgk_ref)
            dgo_ref[...] = jnp.zeros_like(dgo_ref)

        @pl.when(i == 0)
        def _():
            dkn_acc[...] = jnp.zeros_like(dkn_acc)
            dv_acc[...] = jnp.zeros_like(dv_acc)
            dcrow_acc[...] = jnp.zeros_like(dcrow_acc)

        for e in range(2):
            sl = slice(e * hd, (e + 1) * hd)
            q, k, v = q_ref[:, sl], k_ref[:, sl], v_ref[:, sl]
            gqv, gkv, gov = gq_ref[:, sl], gk_ref[:, sl], go_ref[:, sl]
            qn, rq = _head_rms(q, gqv)
            kn, rk = _head_rms(k, gkv)
            p = _softmax_rows(_fox_scores(qn, kn, ccol_ref[0, e], crow_ref[0, e], q0, tq, S, scale))
            o = o_ref[:, sl]
            ro = lax.rsqrt(jnp.mean(o * o, axis=-1, keepdims=True) + EPS)
            do, dgo = _head_rms_bwd(o, ro, gov, doa_ref[:, sl])
            dgo_ref[:, sl] += dgo
            dv_acc[e] += _dot(_mx(p), _mx(do), TN)
            dp = _dot(_mx(do), _mx(v), NT)
            ds = p * (dp - jnp.sum(do * o, axis=-1, keepdims=True))
            dccol_ref[0, e] = jnp.sum(ds, axis=1, keepdims=True)
            dcrow_acc[e] -= jnp.sum(ds, axis=0, keepdims=True)
            dqn = _dot(_mx(ds), _mx(kn), NN) * scale
            dkn_acc[e] += _dot(_mx(ds), _mx(qn), TN) * scale
            dq, dgq = _head_rms_bwd(q, rq, gqv, dqn)
            dq_ref[:, sl] = dq.astype(dq_ref.dtype)
            dgq_ref[:, sl] += dgq

        @pl.when(i == nq - 1)
        def _():
            for e in range(2):
                sl = slice(e * hd, (e + 1) * hd)
                k = k_ref[:, sl]
                gkv = gk_ref[:, sl]
                rk = lax.rsqrt(jnp.mean(k * k, axis=-1, keepdims=True) + EPS)
                dk, dgk = _head_rms_bwd(k, rk, gkv, dkn_acc[e])
                dk_ref[:, sl] = dk.astype(dk_ref.dtype)
                dv_ref[:, sl] = dv_acc[e].astype(dv_ref.dtype)
                dgk_ref[:, sl] += dgk
                dcrow_ref[0, e] = dcrow_acc[e]

    W = 2 * hd
    vec = pl.BlockSpec((1, W), lambda b, h, i: (0, 0))
    qspec = pl.BlockSpec((tq, W), lambda b, h, i: (b * nq + i, h))
    kvout = pl.BlockSpec((S, W), lambda b, h, i: (b, h))
    colspec = pl.BlockSpec((1, 2, tq, 1), lambda b, h, i: (b, h, i, 0))
    rowspec = pl.BlockSpec((1, 2, 1, S), lambda b, h, i: (b, h, 0, 0))
    return pl.pallas_call(
        body, name="fox_bwd", grid=(B, FOX_PAIRS, nq),
        in_specs=[qspec,
                  pl.BlockSpec((S, W), lambda b, h, i: (b, FOX_PAIRS + h)),
                  pl.BlockSpec((S, W), lambda b, h, i: (b, 2 * FOX_PAIRS + h)),
                  colspec, rowspec, vec, vec, vec, qspec, qspec],
        out_specs=[qspec, kvout, kvout, colspec, rowspec, vec, vec, vec],
        out_shape=[jax.ShapeDtypeStruct((T, FOX_WIDTH), MXU_DTYPE), jax.ShapeDtypeStruct((T, FOX_WIDTH), MXU_DTYPE),
                   jax.ShapeDtypeStruct((T, FOX_WIDTH), MXU_DTYPE),
                   jax.ShapeDtypeStruct((B, FOX_HEADS, S, 1), f32), jax.ShapeDtypeStruct((B, FOX_HEADS, 1, S), f32),
                   jax.ShapeDtypeStruct((1, W), f32), jax.ShapeDtypeStruct((1, W), f32), jax.ShapeDtypeStruct((1, W), f32)],
        scratch_shapes=[pltpu.VMEM((2, S, hd), f32), pltpu.VMEM((2, S, hd), f32), pltpu.VMEM((2, 1, S), f32)],
        compiler_params=_cparams("arbitrary", "arbitrary", "arbitrary"),
    )(P, P, P, ccol, crow, gq, gk, go, o_raw, d_oab)


FOX_TQ = 512
FOX_TK = FOX_TQ
GROUP_PRECISION = lax.Precision.HIGH


def _head_mean(v):
    n = v.shape[1]
    r = lax.broadcasted_iota(jnp.int32, (n, n), 0) // FOX_HEAD_DIM
    c = lax.broadcasted_iota(jnp.int32, (n, n), 1) // FOX_HEAD_DIM
    ones = (r == c).astype(bf16)
    hi = v.astype(bf16)
    lo = (v - hi.astype(f32)).astype(bf16)
    return (_dot(hi, ones, NN) + _dot(lo, ones, NN)) * (1.0 / FOX_HEAD_DIM)


def fox_prep_fwd(P, gq, gk, *, tr=512):
    T = P.shape[0]
    tr = min(tr, T)
    scale = FOX_HEAD_DIM ** -0.5

    def body(q_ref, k_ref, v_ref, gq_ref, gk_ref, qn_ref, kn_ref, vb_ref):
        q, k = q_ref[...], k_ref[...]
        qn_ref[...] = (q * lax.rsqrt(_head_mean(q * q) + EPS) * (gq_ref[...] * scale)).astype(qn_ref.dtype)
        kn_ref[...] = (k * lax.rsqrt(_head_mean(k * k) + EPS) * gk_ref[...]).astype(kn_ref.dtype)
        vb_ref[...] = v_ref[...].astype(vb_ref.dtype)

    W = FOX_WIDTH
    col = lambda j: pl.BlockSpec((tr, W), lambda i: (i, j))
    vec = pl.BlockSpec((1, W), lambda i: (0, 0))
    out = jax.ShapeDtypeStruct((T, W), MXU_DTYPE)
    return pl.pallas_call(
        body, name="fox_prep_fwd", grid=(T // tr,), in_specs=[col(0), col(1), col(2), vec, vec],
        out_specs=[col(0)] * 3, out_shape=[out] * 3, compiler_params=_cparams("parallel"),
    )(P, P, P, gq, gk)


def fox_prep_bwd(P, gq, gk, dqn, dkn, *, tr=512):
    T = P.shape[0]
    tr = min(tr, T)
    scale = FOX_HEAD_DIM ** -0.5

    def body(q_ref, k_ref, gq_ref, gk_ref, dqn_ref, dkn_ref, dq_ref, dk_ref, dgq_ref, dgk_ref):
        @pl.when(pl.program_id(0) == 0)
        def _():
            dgq_ref[...] = jnp.zeros_like(dgq_ref)
            dgk_ref[...] = jnp.zeros_like(dgk_ref)

        def one(x, g, dn, dx_ref, dg_ref):
            r = lax.rsqrt(_head_mean(x * x) + EPS)
            xhat = x * r
            gd = dn * g
            dx_ref[...] = (r * (gd - xhat * _head_mean(gd * xhat))).astype(dx_ref.dtype)
            return jnp.sum(dn * xhat, axis=0, keepdims=True)

        dgq_ref[...] += scale * one(q_ref[...], gq_ref[...] * scale, dqn_ref[...], dq_ref, dgq_ref)
        dgk_ref[...] += one(k_ref[...], gk_ref[...], dkn_ref[...], dk_ref, dgk_ref)

    W = FOX_WIDTH
    col = lambda j: pl.BlockSpec((tr, W), lambda i: (i, j))
    vec = pl.BlockSpec((1, W), lambda i: (0, 0))
    return pl.pallas_call(
        body, name="fox_prep_bwd", grid=(T // tr,), in_specs=[col(0), col(1), vec, vec, col(0), col(0)],
        out_specs=[col(0), col(0), vec, vec],
        out_shape=[jax.ShapeDtypeStruct((T, W), MXU_DTYPE), jax.ShapeDtypeStruct((T, W), MXU_DTYPE),
                   jax.ShapeDtypeStruct((1, W), f32), jax.ShapeDtypeStruct((1, W), f32)],
        compiler_params=_cparams("arbitrary"),
    )(P, P, gq, gk, dqn, dkn)


def _fox_tile_scores(q, k_ref, ccol_ref, cq, e, j, sl, mask_off):
    tq, tk = FOX_TQ, FOX_TK
    rows = pl.ds(pl.multiple_of(j * tk, tk), tk)
    k = k_ref[rows, sl]
    s = _dot(k, q, NT) + cq - ccol_ref[0, e, rows, :]
    if mask_off is not None:
        key = lax.broadcasted_iota(jnp.int32, (tk, tq), 0) + mask_off
        query = lax.broadcasted_iota(jnp.int32, (tk, tq), 1)
        s = jnp.where(key <= query, s, NEG_INF)
    return s, k, rows


def _fox_sweep(i, update, carry):
    nd = FOX_TQ // FOX_TK
    carry = lax.fori_loop(0, i * nd, lambda j, cr: update(cr, j, None), carry)
    for d in range(nd):
        carry = update(carry, i * nd + d, d * FOX_TK)
    return carry


def fox_core_fwd(qn, kn, vb, ccol, crow, go, *, B):
    T = qn.shape[0]
    S = T // B
    tq = FOX_TQ
    nq = S // tq
    hd = FOX_HEAD_DIM

    def body(q_ref, k_ref, v_ref, ccol_ref, crow_ref, go_ref, o_ref, oa_ref, lse_ref):
        i = pl.program_id(2)
        for e in range(2):
            sl = slice(e * hd, (e + 1) * hd)
            q = q_ref[:, sl]
            cq = crow_ref[0, e, i]

            def update(carry, j, mask_off):
                m, l, acc = carry
                s, _, rows = _fox_tile_scores(q, k_ref, ccol_ref, cq, e, j, sl, mask_off)
                m2 = jnp.maximum(m, jnp.max(s, axis=0, keepdims=True))
                a = jnp.exp(m - m2)
                p = jnp.exp(s - m2)
                return m2, a * l + jnp.sum(p, axis=0, keepdims=True), a * acc + _dot(v_ref[rows, sl], _mx(p), TN)

            carry = (jnp.full((1, tq), NEG_INF, f32), jnp.zeros((1, tq), f32), jnp.zeros((hd, tq), f32))
            m, l, acc = _fox_sweep(i, update, carry)
            o = (acc / l).T
            o_ref[:, sl] = o
            oa_ref[:, sl] = _head_rms(o, go_ref[:, sl])[0].astype(oa_ref.dtype)
            lse_ref[0, e, 0] = m + jnp.log(l)

    W = 2 * hd
    qspec = pl.BlockSpec((tq, W), lambda b, h, i: (b * nq + i, h))
    kspec = pl.BlockSpec((S, W), lambda b, h, i: (b, h))
    return pl.pallas_call(
        body, name="fox_core_fwd", grid=(B, FOX_PAIRS, nq),
        in_specs=[qspec, kspec, kspec, pl.BlockSpec((1, 2, S, 1), lambda b, h, i: (b, h, 0, 0)),
                  pl.BlockSpec((1, 2, nq, 1, tq), lambda b, h, i: (b, h, 0, 0, 0)),
                  pl.BlockSpec((1, W), lambda b, h, i: (0, 0))],
        out_specs=[qspec, qspec, pl.BlockSpec((1, 2, 1, 1, tq), lambda b, h, i: (b, h, i, 0, 0))],
        out_shape=[jax.ShapeDtypeStruct((T, FOX_WIDTH), f32), jax.ShapeDtypeStruct((T, FOX_WIDTH), MXU_DTYPE),
                   jax.ShapeDtypeStruct((B, FOX_HEADS, nq, 1, tq), f32)],
        compiler_params=_cparams("parallel", "parallel", "parallel"),
    )(qn, kn, vb, ccol, crow, go)


def fox_core_bwd(qn, kn, vb, ccol, crow, go, o_raw, lse, d_oab, *, B):
    T = qn.shape[0]
    S = T // B
    tq = FOX_TQ
    nq = S // tq
    hd = FOX_HEAD_DIM

    def body(q_ref, k_ref, v_ref, ccol_ref, crow_ref, go_ref, o_ref, lse_ref, doa_ref,
             dq_ref, dk_ref, dv_ref, dckey_ref, dcrow_ref, dgo_ref, dk_acc, dv_acc, dck_acc):
        b, h, i = pl.program_id(0), pl.program_id(1), pl.program_id(2)

        @pl.when((b == 0) & (h == 0) & (i == 0))
        def _():
            dgo_ref[...] = jnp.zeros_like(dgo_ref)

        @pl.when(i == 0)
        def _():
            dk_acc[...] = jnp.zeros_like(dk_acc)
            dv_acc[...] = jnp.zeros_like(dv_acc)
            dck_acc[...] = jnp.zeros_like(dck_acc)

        for e in range(2):
            sl = slice(e * hd, (e + 1) * hd)
            q = q_ref[:, sl]
            cq = crow_ref[0, e, i]
            lse_e = lse_ref[0, e, 0]
            o = o_ref[:, sl]
            ro = lax.rsqrt(jnp.mean(o * o, axis=-1, keepdims=True) + EPS)
            do, dgo = _head_rms_bwd(o, ro, go_ref[:, sl], doa_ref[:, sl])
            dgo_ref[:, sl] += dgo
            delta = jnp.sum((do * o).T, axis=0, keepdims=True)
            do_b = _mx(do)

            def update(carry, j, mask_off):
                dq, dcq = carry
                s, k, rows = _fox_tile_scores(q, k_ref, ccol_ref, cq, e, j, sl, mask_off)
                p = jnp.exp(s - lse_e)
                dv_acc[e, rows, :] += _dot(_mx(p), do_b, NN)
                ds = p * (_dot(v_ref[rows, sl], do_b, NT) - delta)
                dck_acc[e, rows, :] -= jnp.sum(ds, axis=1, keepdims=True)
                ds_b = _mx(ds)
                dk_acc[e, rows, :] += _dot(ds_b, q, NN)
                return dq + _dot(ds_b, k, TN), dcq + jnp.sum(ds, axis=0, keepdims=True)

            dq, dcq = _fox_sweep(i, update, (jnp.zeros((tq, hd), f32), jnp.zeros((1, tq), f32)))
            dq_ref[:, sl] = dq
            dcrow_ref[0, e, 0] = dcq

        @pl.when(i == nq - 1)
        def _():
            for e in range(2):
                sl = slice(e * hd, (e + 1) * hd)
                dk_ref[:, sl] = dk_acc[e]
                dv_ref[:, sl] = dv_acc[e].astype(dv_ref.dtype)
                dckey_ref[0, e] = jnp.transpose(jnp.broadcast_to(dck_acc[e], (S, LANES)))[0:1, :]

    W = 2 * hd
    qspec = pl.BlockSpec((tq, W), lambda b, h, i: (b * nq + i, h))
    kspec = pl.BlockSpec((S, W), lambda b, h, i: (b, h))
    colspec = pl.BlockSpec((1, 2, S, 1), lambda b, h, i: (b, h, 0, 0))
    rowspec = pl.BlockSpec((1, 2, nq, 1, tq), lambda b, h, i: (b, h, 0, 0, 0))
    tilespec = pl.BlockSpec((1, 2, 1, 1, tq), lambda b, h, i: (b, h, i, 0, 0))
    vec = pl.BlockSpec((1, W), lambda b, h, i: (0, 0))
    return pl.pallas_call(
        body, name="fox_core_bwd", grid=(B, FOX_PAIRS, nq),
        in_specs=[qspec, kspec, kspec, colspec, rowspec, vec, qspec, tilespec, qspec],
        out_specs=[qspec, kspec, kspec, pl.BlockSpec((1, 2, 1, S), lambda b, h, i: (b, h, 0, 0)), tilespec, vec],
        out_shape=[jax.ShapeDtypeStruct((T, FOX_WIDTH), f32), jax.ShapeDtypeStruct((T, FOX_WIDTH), f32),
                   jax.ShapeDtypeStruct((T, FOX_WIDTH), MXU_DTYPE),
                   jax.ShapeDtypeStruct((B, FOX_HEADS, 1, S), f32), jax.ShapeDtypeStruct((B, FOX_HEADS, nq, 1, tq), f32),
                   jax.ShapeDtypeStruct((1, W), f32)],
        scratch_shapes=[pltpu.VMEM((2, S, hd), f32), pltpu.VMEM((2, S, hd), f32), pltpu.VMEM((2, S, 1), f32)],
        compiler_params=_cparams("arbitrary", "arbitrary", "arbitrary"),
    )(qn, kn, vb, ccol, crow, go, o_raw, lse, d_oab)


def _lane_mask(lo, hi, shape):
    lane = lax.broadcasted_iota(jnp.int32, shape, 1)
    return (lane >= lo) & (lane < hi)


def _cumsum_rows(v, period, reverse=False):
    n = v.shape[0]
    pos = lax.broadcasted_iota(jnp.int32, v.shape, 0) % period
    sh = 1
    while sh < period:
        if reverse:
            v = v + jnp.where(pos + sh < period, pltpu.roll(v, n - sh, 0), 0.0)
        else:
            v = v + jnp.where(pos >= sh, pltpu.roll(v, sh, 0), 0.0)
        sh *= 2
    return v


def _gate_values(z, bias, alog):
    zb = z + bias
    ls = jax.nn.log_sigmoid(zb)
    beta = jax.nn.sigmoid(z)
    g = -jnp.exp(alog) * jax.nn.softplus(zb)
    return zb, ls, beta, g


def gates_fwd(P, bias, alog, *, B):
    T = P.shape[0]
    S = T // B

    def body(z_ref, bias_ref, alog_ref, o_ref):
        z = z_ref[...]
        _, ls, beta, g = _gate_values(z, bias_ref[...], alog_ref[...])
        c = _cumsum_rows(ls, S)
        gc = _cumsum_rows(g, GDN_CHUNK)
        o = jnp.where(_lane_mask(SM_F, SM_F + FOX_HEADS, z.shape), c, 0.0)
        o = jnp.where(_lane_mask(SM_B, SM_B + GDN_HEADS, z.shape), beta, o)
        o = jnp.where(_lane_mask(SM_A, SM_A + GDN_HEADS, z.shape), gc, o)
        o_ref[...] = o

    vec = pl.BlockSpec((1, LANES), lambda b: (0, 0))
    return pl.pallas_call(
        body, name="gates_fwd", grid=(B,),
        in_specs=[pl.BlockSpec((S, LANES), lambda b: (b, COL_SMALL // LANES)), vec, vec],
        out_specs=pl.BlockSpec((S, LANES), lambda b: (b, 0)),
        out_shape=jax.ShapeDtypeStruct((T, LANES), f32),
        compiler_params=_cparams("parallel"),
    )(P, bias, alog)


def gates_bwd(P, bias, alog, dgates, *, B):
    T = P.shape[0]
    S = T // B

    def body(z_ref, bias_ref, alog_ref, dg_ref, dz_ref, par_ref):
        z = z_ref[...]
        zb, ls, beta, g = _gate_values(z, bias_ref[...], alog_ref[...])
        d = dg_ref[...]
        dls = _cumsum_rows(d, S, reverse=True)
        dgr = _cumsum_rows(d, GDN_CHUNK, reverse=True)
        sig = jax.nn.sigmoid(zb)
        dz_f = dls * (1.0 - sig)
        dz_b = d * beta * (1.0 - beta)
        dz_a = dgr * (-jnp.exp(alog_ref[...])) * sig
        dz = jnp.where(_lane_mask(SM_F, SM_F + FOX_HEADS, z.shape), dz_f, 0.0)
        dz = jnp.where(_lane_mask(SM_B, SM_B + GDN_HEADS, z.shape), dz_b, dz)
        dz = jnp.where(_lane_mask(SM_A, SM_A + GDN_HEADS, z.shape), dz_a, dz)
        dz_ref[...] = dz.astype(dz_ref.dtype)

        @pl.when(pl.program_id(0) == 0)
        def _():
            par_ref[...] = jnp.zeros_like(par_ref)

        dalog = jnp.where(_lane_mask(SM_A, SM_A + GDN_HEADS, z.shape), dgr * g, 0.0)
        par_ref[0:1, :] += jnp.sum(dz, axis=0, keepdims=True)
        par_ref[1:2, :] += jnp.sum(dalog, axis=0, keepdims=True)

    vec = pl.BlockSpec((1, LANES), lambda b: (0, 0))
    return pl.pallas_call(
        body, name="gates_bwd", grid=(B,),
        in_specs=[pl.BlockSpec((S, LANES), lambda b: (b, COL_SMALL // LANES)), vec, vec,
                  pl.BlockSpec((S, LANES), lambda b: (b, 0))],
        out_specs=[pl.BlockSpec((S, LANES), lambda b: (b, 0)), pl.BlockSpec((8, LANES), lambda b: (0, 0))],
        out_shape=[jax.ShapeDtypeStruct((T, LANES), MXU_DTYPE), jax.ShapeDtypeStruct((8, LANES), f32)],
        compiler_params=_cparams("arbitrary"),
    )(P, bias, alog, dgates)


GDN_BLOCKS = 3 * GDN_HEADS


def _shift_rows(v, d, reverse=False):
    if d == 0:
        return v
    n = v.shape[0]
    row = lax.broadcasted_iota(jnp.int32, v.shape, 0)
    if reverse:
        return jnp.where(row + d < n, pltpu.roll(v, n - d, 0), 0.0)
    return jnp.where(row >= d, pltpu.roll(v, d, 0), 0.0)


def _conv_silu(x, w):
    pre = sum(w[j:j + 1, :] * _shift_rows(x, CONV_WIDTH - 1 - j) for j in range(CONV_WIDTH))
    return pre, pre * jax.nn.sigmoid(pre)


def gdn_prep_fwd(P, conv_w, *, B):
    T = P.shape[0]
    S = T // B

    def body(x_ref, w_ref, o_ref):
        _, y = _conv_silu(x_ref[...], w_ref[...])
        yn = y * lax.rsqrt(jnp.sum(y * y, axis=-1, keepdims=True) + EPS)
        o_ref[...] = jnp.where(pl.program_id(1) < 2 * GDN_HEADS, yn, y)

    return pl.pallas_call(
        body, name="gdn_prep_fwd", grid=(B, GDN_BLOCKS),
        in_specs=[pl.BlockSpec((S, LANES), lambda b, j: (b, COL_GDN // LANES + j)),
                  pl.BlockSpec((CONV_WIDTH, LANES), lambda b, j: (0, j))],
        out_specs=pl.BlockSpec((S, LANES), lambda b, j: (b, j)),
        out_shape=jax.ShapeDtypeStruct((T, 3 * GDN_WIDTH), f32),
        compiler_params=_cparams("parallel", "parallel"),
    )(P, conv_w)


def gdn_prep_bwd(P, conv_w, dG, *, B):
    T = P.shape[0]
    S = T // B

    def body(x_ref, w_ref, dg_ref, dx_ref, dw_ref):
        x, w = x_ref[...], w_ref[...]
        pre, y = _conv_silu(x, w)
        dn = dg_ref[...]
        r = lax.rsqrt(jnp.sum(y * y, axis=-1, keepdims=True) + EPS)
        n = y * r
        dy_norm = r * (dn - n * jnp.sum(dn * n, axis=-1, keepdims=True))
        dy = jnp.where(pl.program_id(0) < 2 * GDN_HEADS, dy_norm, dn)
        sg = jax.nn.sigmoid(pre)
        dpre = dy * (sg * (1.0 + pre * (1.0 - sg)))
        dx = sum(w[j:j + 1, :] * _shift_rows(dpre, CONV_WIDTH - 1 - j, reverse=True) for j in range(CONV_WIDTH))
        dx_ref[...] = dx.astype(dx_ref.dtype)

        @pl.when(pl.program_id(1) == 0)
        def _():
            dw_ref[...] = jnp.zeros_like(dw_ref)

        for j in range(CONV_WIDTH):
            dw_ref[j:j + 1, :] += jnp.sum(dpre * _shift_rows(x, CONV_WIDTH - 1 - j), axis=0, keepdims=True)

    return pl.pallas_call(
        body, name="gdn_prep_bwd", grid=(GDN_BLOCKS, B),
        in_specs=[pl.BlockSpec((S, LANES), lambda j, b: (b, COL_GDN // LANES + j)),
                  pl.BlockSpec((CONV_WIDTH, LANES), lambda j, b: (0, j)),
                  pl.BlockSpec((S, LANES), lambda j, b: (b, j))],
        out_specs=[pl.BlockSpec((S, LANES), lambda j, b: (b, j)),
                   pl.BlockSpec((CONV_WIDTH, LANES), lambda j, b: (0, j))],
        out_shape=[jax.ShapeDtypeStruct((T, 3 * GDN_WIDTH), MXU_DTYPE),
                   jax.ShapeDtypeStruct((CONV_WIDTH, 3 * GDN_WIDTH), f32)],
        compiler_params=_cparams("arbitrary", "arbitrary"),
    )(P, conv_w, dG)


GDN_GROUP = 16
GDN_GROUP_FWD = 16
B_NN = (((2,), (1,)), ((0,), (0,)))
B_NT = (((2,), (2,)), ((0,), (0,)))
B_TN = (((1,), (1,)), ((0,), (0,)))


def _bmm(a, b, dims, precision=None):
    if precision is None:
        a, b = _mx(a), _mx(b)
    return lax.dot_general(a, b, dims, preferred_element_type=f32, precision=precision)


def _tri_inverse(A):
    C = A.shape[-1]
    row = lax.broadcasted_iota(jnp.int32, A.shape, 1)
    col = lax.broadcasted_iota(jnp.int32, A.shape, 2)
    eye = (row == col).astype(f32)
    X = jnp.where((row // 4) == (col // 4), -A, 0.0)
    X2 = _bmm(X, X, B_NN, INV_PRECISION)
    Tm = eye + X + X2 + _bmm(X, X2, B_NN, INV_PRECISION)
    b = 4
    while b < C:
        off = ((row // (2 * b)) == (col // (2 * b))) & ((row // b) != (col // b))
        Tm = Tm - _bmm(_bmm(Tm, jnp.where(off, A, 0.0), B_NN, INV_PRECISION), Tm, B_NN, INV_PRECISION)
        b *= 2
    return Tm


def _pick_lane(block, lane_idx):
    lane = lax.broadcasted_iota(jnp.int32, block.shape, 1)
    return jnp.sum(jnp.where(lane == lane_idx, block, 0.0), axis=1, keepdims=True)


def _gdn_local(q, k, v, beta, gc, Tm=None, uwm=None):
    C = GDN_CHUNK
    n = q.shape[0] // C
    q = q.reshape(n, C, -1) * (GDN_HEAD_DIM ** -0.5)
    k = k.reshape(n, C, -1)
    v = v.reshape(n, C, -1)
    beta = beta.reshape(n, C, 1)
    gc = gc.reshape(n, C, 1)
    row = lax.broadcasted_iota(jnp.int32, (n, C, C), 1)
    col = lax.broadcasted_iota(jnp.int32, (n, C, C), 2)
    gcT = jnp.swapaxes(jnp.broadcast_to(gc, (n, C, C)), 1, 2)
    D = jnp.exp(jnp.where(row >= col, gc - gcT, NEG_INF))
    kb = k * beta
    vb = v * beta
    A = jnp.where(row > col, _bmm(kb, k, B_NT) * D, 0.0)
    Gam = jnp.exp(gc)
    kg = kb * Gam
    gl = gc[:, C - 1:C, :]
    kdec = jnp.exp(gl - gc)
    loc = dict(q=q, k=k, v=v, beta=beta, gc=gc, D=D, kb=kb, vb=vb, A=A, Gam=Gam, kg=kg,
               kdec=kdec, kd=k * kdec, qg=q * Gam, gam=jnp.exp(gl), row=row, col=col)
    uwm = Tm is None if uwm is None else uwm
    Tm = _tri_inverse(A) if Tm is None else Tm.reshape(n, C, C)
    if uwm:
        loc.update(u=_bmm(Tm, vb, B_NN), w=_bmm(Tm, kg, B_NN), M=_bmm(q, k, B_NT) * D)
    loc["Tm"] = Tm
    return loc


def _gdn_store_local(loc, r0, u_s, w_s, qg_s, kd_s, M_s, gam_s, c0):
    n = loc["u"].shape[0]
    R = n * GDN_CHUNK
    u_s[pl.ds(r0, R), :] = loc["u"].reshape(R, -1)
    w_s[pl.ds(r0, R), :] = loc["w"].reshape(R, -1)
    qg_s[pl.ds(r0, R), :] = loc["qg"].reshape(R, -1)
    kd_s[pl.ds(r0, R), :] = loc["kd"].reshape(R, -1)
    M_s[pl.ds(r0, R), :] = loc["M"].reshape(R, -1)
    gam_s[pl.ds(c0, n)] = jnp.broadcast_to(loc["gam"], (n, 1, LANES))


def _gdn_specs(S):
    blk = lambda off: pl.BlockSpec((S, LANES), lambda b, h: (b, off + h))
    return blk


def gdn_fwd(G, gates, P, g_on, *, B):
    T = G.shape[0]
    S = T // B
    C = GDN_CHUNK
    N = S // C
    grp = min(GDN_GROUP_FWD, N)
    R = grp * C
    hd = GDN_HEAD_DIM

    def body(q_ref, k_ref, v_ref, gt_ref, z_ref, gon_ref, o_ref, ob_ref, st_ref, tm_ref, A_s, B_s, Q_s, O_s, gam_s):
        h = pl.program_id(1)

        def local(gi, carry):
            r0 = pl.multiple_of(gi * R, R)
            gt = gt_ref[pl.ds(r0, R), :]
            loc = _gdn_local(q_ref[pl.ds(r0, R), :], k_ref[pl.ds(r0, R), :], v_ref[pl.ds(r0, R), :],
                             _pick_lane(gt, SM_B + h), _pick_lane(gt, SM_A + h))
            chunks = pl.ds(gi * grp, grp)
            tm_ref[0, 0, pl.ds(r0, R), :] = loc["Tm"].reshape(R, C)
            A_s[chunks] = -_bmm(loc["kd"], loc["w"], B_TN)
            B_s[chunks] = _bmm(loc["kd"], loc["u"], B_TN)
            Q_s[pl.ds(r0, R), :] = (loc["qg"] - _bmm(loc["M"], loc["w"], B_NN)).reshape(R, hd)
            O_s[pl.ds(r0, R), :] = _bmm(loc["M"], loc["u"], B_NN).reshape(R, hd)
            gam_s[chunks] = jnp.broadcast_to(loc["gam"], (grp, 1, LANES))
            return carry

        lax.fori_loop(0, N // grp, local, 0)

        def step(n, state):
            st_ref[0, 0, n] = state
            return state * gam_s[n] + _dotm(A_s[n], state, NN) + B_s[n]

        lax.fori_loop(0, N, step, jnp.zeros((hd, hd), f32))

        def outputs(gi, carry):
            r0 = pl.multiple_of(gi * R, R)
            Q = Q_s[pl.ds(r0, R), :].reshape(grp, C, hd)
            o = _bmm(Q, st_ref[0, 0, pl.ds(gi * grp, grp)], B_NN).reshape(R, hd) + O_s[pl.ds(r0, R), :]
            o_ref[pl.ds(r0, R), :] = o
            return carry

        lax.fori_loop(0, N // grp, outputs, 0)
        o = o_ref[...]
        z = z_ref[...]
        ob_ref[...] = (_head_rms(o, gon_ref[...])[0] * (z * jax.nn.sigmoid(z))).astype(ob_ref.dtype)

    blk = lambda off: pl.BlockSpec((S, LANES), lambda b, h: (b, off + h))
    rows = lambda: pltpu.VMEM((S, hd), f32)
    return pl.pallas_call(
        body, name="gdn_fwd", grid=(B, GDN_HEADS),
        in_specs=[blk(0), blk(GDN_HEADS), blk(2 * GDN_HEADS), pl.BlockSpec((S, LANES), lambda b, h: (b, 0)),
                  blk(COL_Z // LANES), pl.BlockSpec((1, hd), lambda b, h: (0, 0))],
        out_specs=[blk(0), blk(0), pl.BlockSpec((1, 1, N, hd, hd), lambda b, h: (b, h, 0, 0, 0)),
                   pl.BlockSpec((1, 1, S, C), lambda b, h: (b, h, 0, 0))],
        out_shape=[jax.ShapeDtypeStruct((T, GDN_WIDTH), f32), jax.ShapeDtypeStruct((T, GDN_WIDTH), MXU_DTYPE),
                   jax.ShapeDtypeStruct((B, GDN_HEADS, N, hd, hd), f32), jax.ShapeDtypeStruct((B, GDN_HEADS, S, C), f32)],
        scratch_shapes=[pltpu.VMEM((N, hd, hd), f32), pltpu.VMEM((N, hd, hd), f32), rows(), rows(),
                        pltpu.VMEM((N, 1, LANES), f32)],
        compiler_params=_cparams("parallel", "parallel"),
    )(G, G, G, gates, P, g_on)


def gdn_bwd(G, gates, P, g_on, o_raw, states, tm, d_oab, *, B):
    T = G.shape[0]
    S = T // B
    C = GDN_CHUNK
    N = S // C
    grp = min(GDN_GROUP, N)
    R = grp * C
    hd = GDN_HEAD_DIM

    def body(q_ref, k_ref, v_ref, gt_ref, z_ref, gon_ref, o_ref, st_ref, tm_ref, dob_ref,
             dq_ref, dk_ref, dv_ref, dgt_ref, dz_ref, dgon_ref,
             u_s, w_s, M_s, gam_s, do_s, A_s, C_s, dst_s):
        b, h = pl.program_id(0), pl.program_id(1)

        @pl.when((b == 0) & (h == 0))
        def _():
            dgon_ref[...] = jnp.zeros_like(dgon_ref)

        @pl.when(h == 0)
        def _():
            dgt_ref[...] = jnp.zeros_like(dgt_ref)

        def group_inputs(gi, uwm):
            r0 = pl.multiple_of(gi * R, R)
            gt = gt_ref[pl.ds(r0, R), :]
            return r0, _gdn_local(q_ref[pl.ds(r0, R), :], k_ref[pl.ds(r0, R), :], v_ref[pl.ds(r0, R), :],
                                  _pick_lane(gt, SM_B + h), _pick_lane(gt, SM_A + h), tm_ref[0, 0, pl.ds(r0, R), :], uwm)

        def local(gi, carry):
            r0, loc = group_inputs(gi, True)
            rows, chunks = pl.ds(r0, R), pl.ds(gi * grp, grp)
            u_s[rows, :] = loc["u"].reshape(R, hd)
            w_s[rows, :] = loc["w"].reshape(R, hd)
            M_s[rows, :] = loc["M"].reshape(R, C)
            gam_s[chunks] = jnp.broadcast_to(loc["gam"], (grp, 1, LANES))
            o, z, gon = o_ref[rows, :], z_ref[rows, :], gon_ref[...]
            dob = dob_ref[rows, :]
            on, ro = _head_rms(o, gon)
            sz = jax.nn.sigmoid(z)
            dz_ref[rows, :] = (dob * on * (sz * (1.0 + z * (1.0 - sz)))).astype(dz_ref.dtype)
            do, dgon = _head_rms_bwd(o, ro, gon, dob * (z * sz))
            do_s[rows, :] = do
            dgon_ref[...] += dgon
            A_s[chunks] = -_bmm(loc["kd"], loc["w"], B_TN)
            C_s[chunks] = _bmm(loc["qg"] - _bmm(loc["M"], loc["w"], B_NN), do.reshape(grp, C, hd), B_TN)
            return carry

        lax.fori_loop(0, N // grp, local, 0)

        def step(t, dS):
            n = N - 1 - t
            dst_s[n] = dS
            return dS * gam_s[n] + _dotm(A_s[n], dS, TN) + C_s[n]

        lax.fori_loop(0, N, step, jnp.zeros((hd, hd), f32))

        def finish(gi, carry):
            r0, L = group_inputs(gi, False)
            n = grp
            rows, chunks = pl.ds(r0, R), pl.ds(gi * grp, grp)
            g3 = lambda ref: ref[rows, :].reshape(n, C, -1)
            u, w, do = g3(u_s), g3(w_s), g3(do_s)
            L["M"] = g3(M_s)
            state, dS = st_ref[0, 0, chunks], dst_s[chunks]
            v_new = u - _bmm(w, state, B_NN)
            du = _bmm(L["M"], do, B_TN) + _bmm(L["kd"], dS, B_NN)
            dw = -_bmm(du, state, B_NT)
            dqg = _bmm(do, state, B_NT)
            dM = _bmm(do, v_new, B_NT)
            dkd = _bmm(v_new, dS, B_NT)
            dgl_state = jnp.sum(jnp.sum(dS * state, axis=2, keepdims=True), axis=1, keepdims=True) * L["gam"]
            TmT = jnp.swapaxes(L["Tm"], 1, 2)
            dTm = _bmm(du, L["vb"], B_NT) + _bmm(dw, L["kg"], B_NT)
            dvb = _bmm(TmT, du, B_NN)
            dkg = _bmm(TmT, dw, B_NN)
            dA = jnp.where(L["row"] > L["col"], -_bmm(_bmm(TmT, dTm, B_NN), TmT, B_NN), 0.0)
            dKK = dA * L["D"]
            dQK = dM * L["D"]
            dkb = _bmm(dKK, L["k"], B_NN) + dkg * L["Gam"]
            dk = (_bmm(dKK, L["kb"], B_TN) + _bmm(dQK, L["q"], B_TN) + dkd * L["kdec"] + L["beta"] * dkb)
            dq = (_bmm(dQK, L["k"], B_NN) + dqg * L["Gam"]) * (GDN_HEAD_DIM ** -0.5)
            E = dA * L["A"] + dM * L["M"]
            r = jnp.sum(dkd * L["kd"], axis=-1, keepdims=True)
            dgc = (jnp.sum(E, axis=2, keepdims=True) - jnp.sum(jnp.swapaxes(E, 1, 2), axis=2, keepdims=True)
                   + jnp.sum(dkg * L["kg"], axis=-1, keepdims=True) + jnp.sum(dqg * L["qg"], axis=-1, keepdims=True) - r)
            dgl = jnp.sum(r, axis=1, keepdims=True) + dgl_state
            rowc = lax.broadcasted_iota(jnp.int32, (n, C, 1), 1)
            dgc = dgc + jnp.where(rowc == C - 1, dgl, 0.0)
            dbeta = jnp.sum(dkb * L["k"], axis=-1, keepdims=True) + jnp.sum(dvb * L["v"], axis=-1, keepdims=True)
            dq_ref[rows, :] = dq.reshape(R, hd)
            dk_ref[rows, :] = dk.reshape(R, hd)
            dv_ref[rows, :] = (L["beta"] * dvb).reshape(R, hd)
            lane = lax.broadcasted_iota(jnp.int32, (R, LANES), 1)
            dgt_ref[rows, :] += (jnp.where(lane == SM_B + h, dbeta.reshape(R, 1), 0.0)
                                 + jnp.where(lane == SM_A + h, dgc.reshape(R, 1), 0.0))
            return carry

        lax.fori_loop(0, N // grp, finish, 0)

    blk = lambda off: pl.BlockSpec((S, LANES), lambda b, h: (b, off + h))
    rows = lambda: pltpu.VMEM((S, hd), f32)
    return pl.pallas_call(
        body, name="gdn_bwd", grid=(B, GDN_HEADS),
        in_specs=[blk(0), blk(GDN_HEADS), blk(2 * GDN_HEADS), pl.BlockSpec((S, LANES), lambda b, h: (b, 0)),
                  blk(COL_Z // LANES), pl.BlockSpec((1, hd), lambda b, h: (0, 0)), blk(0),
                  pl.BlockSpec((1, 1, N, hd, hd), lambda b, h: (b, h, 0, 0, 0)),
                  pl.BlockSpec((1, 1, S, C), lambda b, h: (b, h, 0, 0)), blk(GDN_HEADS)],
        out_specs=[blk(0), blk(0), blk(0), pl.BlockSpec((S, LANES), lambda b, h: (b, 0)), blk(0),
                   pl.BlockSpec((1, hd), lambda b, h: (0, 0))],
        out_shape=[jax.ShapeDtypeStruct((T, GDN_WIDTH), f32), jax.ShapeDtypeStruct((T, GDN_WIDTH), f32),
                   jax.ShapeDtypeStruct((T, GDN_WIDTH), f32), jax.ShapeDtypeStruct((T, LANES), f32),
                   jax.ShapeDtypeStruct((T, GDN_WIDTH), MXU_DTYPE), jax.ShapeDtypeStruct((1, hd), f32)],
        scratch_shapes=[rows(), rows(), pltpu.VMEM((S, C), f32), pltpu.VMEM((N, 1, LANES), f32), rows(),
                        pltpu.VMEM((N, hd, hd), f32), pltpu.VMEM((N, hd, hd), f32), pltpu.VMEM((N, hd, hd), f32)],
        compiler_params=_cparams("arbitrary", "arbitrary"),
    )(G, G, G, gates, P, g_on, o_raw, states, tm, d_oab)


IN_SPLIT = (0, 1536, 1544, 3080, 3088, 3600)


IN_SHARD = IN_DIM // 4
IN_SHARD_PAD = 928


def align_w_in_t(wt):
    s = IN_SPLIT
    pad = jnp.zeros((IN_ALIGNED - IN_DIM, wt.shape[1]), wt.dtype)
    return jnp.concatenate([wt[s[0]:s[1]], wt[s[2]:s[3]], wt[s[4]:s[5]], wt[s[1]:s[2]], wt[s[3]:s[4]], pad], axis=0)


def unalign_w_in_t(wa):
    return jnp.concatenate([wa[0:1536], wa[COL_SMALL:COL_SMALL + 8], wa[1536:3072],
                            wa[COL_SMALL + 8:COL_SMALL + 16], wa[3072:3584]], axis=0)


def _lanes_vec(pieces):
    v = jnp.zeros((1, LANES), f32)
    for off, a in pieces:
        v = lax.dynamic_update_slice(v, a.astype(f32), (0, off))
    return v


def local_step(x, mem, target, w, sp, *, B):
    T = x.shape[0]
    S = T // B
    gq8, gk8 = jnp.tile(sp["fox_qnorm_g"], (1, FOX_HEADS)), jnp.tile(sp["fox_knorm_g"], (1, FOX_HEADS))
    go2 = jnp.tile(sp["fox_onorm_g"], (1, 2))
    bias = _lanes_vec([(SM_F, sp["fox_f_bias"]), (SM_A, sp["gdn_dt_bias"])])
    alog = _lanes_vec([(SM_A, sp["gdn_A_log"])])

    h1 = rms_fwd(x, sp["norm_mix_g"], name="rms_mix")
    P = matmul(h1, w["wa_t"], tb=True, name="mm_in", tn=IN_TILE)
    gates = gates_fwd(P, bias, alog, B=B)
    c = gates[:, SM_F:SM_F + FOX_HEADS].reshape(B, S, FOX_HEADS).transpose(0, 2, 1)
    ccol, crow = c[..., None], c.reshape(B, FOX_HEADS, S // FOX_TQ, 1, FOX_TQ)
    qn, kn, vb = fox_prep_fwd(P, gq8, gk8)
    o_raw, o_a, lse = fox_core_fwd(qn, kn, vb, ccol, crow, go2, B=B)
    G = gdn_prep_fwd(P, w["conv_w"], B=B)
    ob_raw, o_b, states, gdn_tm = gdn_fwd(G, gates, P, sp["gdn_onorm_g"], B=B)
    oab = jnp.concatenate([o_a, o_b], axis=1)
    if "late" in w:
        w = {**w, **w["late"](oab)}
    x2 = matmul(oab, w["w_out"], residual=x, name="mm_out")
    hq = rms_fwd(x2, sp["norm_xattn_g"], name="rms_xattn")
    hm = rms_fwd(mem, sp["mem_norm_g"], name="rms_mem")
    cq = matmul(hq, w["w_cq"], name="mm_cq")
    ckv = matmul(hm, w["w_ckv"], name="mm_ckv")
    co = xattn_fwd(cq, ckv, sp["xattn_qnorm_g"], sp["xattn_knorm_g"], B=B)
    x3 = matmul(co, w["w_co"], b_stacked=True, residual=x2, name="mm_co")
    hf = rms_fwd(x3, sp["norm_mlp_g"], name="rms_mlp")
    act = matmul(hf, w["w_mlp1"], b_stacked=True, relu2_out=True, out_dtype=MXU_DTYPE, name="mm_mlp1")
    dy, dy_op, loss = matmul_rows(act, w["w_mlp2"], (x3, target), mode="loss", name="mm_mlp2_loss")

    da = matmul(dy_op, w["w_mlp2"], tb=True, relu2_bwd_aux=act, out_dtype=MXU_DTYPE, name="mm_d_act")
    g_mlp2 = matmul(act, dy_op, ta=True, out_dtype=WIRE_DTYPE, name="mm_g_mlp2")
    g_mlp1 = matmul(hf, da, ta=True, out_stacked=True, out_dtype=WIRE_DTYPE, name="mm_g_mlp1")
    by_rows = lambda g: g.reshape(N_CHIPS, g.shape[0] // N_CHIPS, g.shape[1])
    early = w.get("grads_ready", lambda grads: jnp.zeros((1, 1), f32))
    tok = early(dict(w_mlp1=g_mlp1, w_mlp2=by_rows(g_mlp2)))[0, 0]
    dx3, g_norm_mlp = matmul_rows(da, w["w_mlp1"], (x3, sp["norm_mlp_g"] + tok, dy), mode="rms_bwd", tb=True,
                                  b_stacked=True, name="mm_d_hf_rms")
    dco = matmul(dx3, w["w_co"], tb=True, b_stacked=True, name="mm_d_co")
    g_co = matmul(co, dx3, ta=True, out_stacked=True, out_dtype=WIRE_DTYPE, name="mm_g_co")
    dcq, dckv, g_xq, g_xk = xattn_bwd(cq, ckv, sp["xattn_qnorm_g"], sp["xattn_knorm_g"], dco, B=B)
    g_cq = matmul(hq, dcq, ta=True, out_dtype=WIRE_DTYPE, name="mm_g_cq")
    g_ckv = matmul(hm, dckv, ta=True, out_dtype=WIRE_DTYPE, name="mm_g_ckv")
    _, g_mem_norm = matmul_rows(dckv, w["w_ckv"], (mem, sp["mem_norm_g"], None), mode="rms_bwd", tb=True, name="mm_d_hm_rms")
    dx2, g_norm_xattn = matmul_rows(dcq, w["w_cq"], (x2, sp["norm_xattn_g"], dx3), mode="rms_bwd", tb=True, name="mm_d_hq_rms")
    doab = matmul(dx2, w["w_out"], tb=True, name="mm_d_oab")
    g_out = matmul(oab, dx2, ta=True, out_dtype=WIRE_DTYPE, name="mm_g_out")
    tok = early(dict(w_co=g_co, w_cq=by_rows(g_cq), w_ckv=by_rows(g_ckv), w_out=by_rows(g_out)))[0, 0]
    dqn, dkn, dv_f, dckey, dcrow, dgo2 = fox_core_bwd(qn, kn, vb, ccol, crow, go2 + tok, o_raw, lse, doab, B=B)
    dq_f, dk_f, dgq8, dgk8 = fox_prep_bwd(P, gq8, gk8, dqn, dkn)
    dGq, dGk, dGv, dgt, dz, g_gdn_on = gdn_bwd(G, gates, P, sp["gdn_onorm_g"], ob_raw, states, gdn_tm, doab, B=B)
    dPg, g_conv = gdn_prep_bwd(P, w["conv_w"], jnp.concatenate([dGq, dGk, dGv], axis=1), B=B)
    dc = (dckey[:, :, 0, :] + dcrow.reshape(B, FOX_HEADS, S)).transpose(0, 2, 1).reshape(T, FOX_HEADS)
    dgates = dgt + jnp.pad(dc, ((0, 0), (SM_F, LANES - SM_F - FOX_HEADS)))
    dsmall, par = gates_bwd(P, bias, alog, dgates, B=B)
    dP = jnp.concatenate([dq_f, dk_f, dv_f, dPg, dz, dsmall, jnp.zeros((T, IN_ALIGNED - COL_SMALL - LANES), MXU_DTYPE)], axis=1)
    g_wa = matmul(dP, h1, ta=True, out_dtype=WIRE_DTYPE, name="mm_g_in", tm=IN_TILE)
    g_in = jnp.pad(unalign_w_in_t(g_wa).reshape(N_CHIPS, IN_SHARD, D_MODEL), ((0, 0), (0, IN_SHARD_PAD - IN_SHARD), (0, 0)))
    tok = early(dict(w_in=g_in))[0, 0]
    dx, g_norm_mix = matmul_rows(dP, w["wa_t"], (x, sp["norm_mix_g"] + tok, dx2), mode="rms_bwd", tk=IN_TILE,
                                 name="mm_d_h1_rms")

    fold = lambda g: jnp.sum(g.reshape(-1, FOX_HEAD_DIM), axis=0, keepdims=True)
    big = dict(w_in=g_in, w_out=by_rows(g_out), w_cq=by_rows(g_cq), w_ckv=by_rows(g_ckv), w_co=g_co, w_mlp1=g_mlp1,
               w_mlp2=by_rows(g_mlp2))
    small = dict(norm_mix_g=g_norm_mix, fox_qnorm_g=fold(dgq8), fox_knorm_g=fold(dgk8),
                 fox_f_bias=par[0:1, SM_F:SM_F + FOX_HEADS], fox_onorm_g=fold(dgo2), gdn_conv_w=g_conv,
                 gdn_A_log=par[1:2, SM_A:SM_A + GDN_HEADS], gdn_dt_bias=par[0:1, SM_A:SM_A + GDN_HEADS],
                 gdn_onorm_g=g_gdn_on, norm_xattn_g=g_norm_xattn, mem_norm_g=g_mem_norm,
                 xattn_qnorm_g=g_xq, xattn_knorm_g=g_xk, norm_mlp_g=g_norm_mlp)
    return loss, dx, big, small


MESH_IDS = pl.DeviceIdType.MESH
N_CHIPS = 4
HBM_SPEC = pl.BlockSpec(memory_space=pltpu.HBM)
PACK_ROWS = 30720
PACK_HALF = PACK_ROWS // 2
PACK_BLOCK = 3072


def _place():
    return lax.axis_index("x"), lax.axis_index("y"), lax.axis_index("c")


def _other_chips(x, y):
    return [(1 - x, y), (x, 1 - y), (1 - x, 1 - y)]


def _remote(src, dst, send_sem, recv_sem, to):
    return pltpu.make_async_remote_copy(src_ref=src, dst_ref=dst, send_sem=send_sem, recv_sem=recv_sem,
                                        device_id=to, device_id_type=MESH_IDS)


def all_gather_shards(packed):
    half = PACK_HALF

    def body(src_ref, out_ref, send_sems, recv_sems):
        x, y, c = _place()
        me_chip = 2 * x + y
        sibling = (x, y, 1 - c)
        chips = _other_chips(x, y)

        def rows(chip, core):
            return out_ref.at[chip, pl.ds(core * half, half), :]

        sends = [_remote(src_ref.at[pl.ds(c * half, half), :], rows(me_chip, c), send_sems.at[j], recv_sems.at[j], (px, py, c))
                 for j, (px, py) in enumerate(chips)]
        for cp in sends:
            cp.start()
        passed = []
        for j, (px, py) in enumerate(chips):
            theirs = rows(2 * px + py, c)
            _remote(theirs, theirs, send_sems.at[j], recv_sems.at[j], (px, py, c)).wait_recv()
            cp = _remote(theirs, theirs, send_sems.at[3 + j], recv_sems.at[3 + j], sibling)
            cp.start()
            passed.append(cp)
        for j, (px, py) in enumerate(chips):
            theirs = rows(2 * px + py, 1 - c)
            _remote(theirs, theirs, send_sems.at[3 + j], recv_sems.at[3 + j], sibling).wait_recv()
        for cp in sends + passed:
            cp.wait_send()

    return pl.pallas_call(
        body, name="all_gather_shards", in_specs=[HBM_SPEC], out_specs=HBM_SPEC,
        out_shape=jax.ShapeDtypeStruct((N_CHIPS,) + packed.shape, packed.dtype),
        scratch_shapes=[pltpu.SemaphoreType.DMA((6,)), pltpu.SemaphoreType.DMA((6,))],
    )(packed)


def exchange_core_halves(G):
    half = PACK_HALF

    def body(g_ref, land_ref, send_sem, recv_sem):
        x, y, c = _place()
        cp = _remote(g_ref.at[:, pl.ds((1 - c) * half, half), :], land_ref, send_sem, recv_sem, (x, y, 1 - c))
        cp.start()
        cp.wait()

    return pl.pallas_call(
        body, name="exchange_core_halves", in_specs=[HBM_SPEC], out_specs=HBM_SPEC,
        out_shape=jax.ShapeDtypeStruct((N_CHIPS, half, LANES), G.dtype),
        scratch_shapes=[pltpu.SemaphoreType.DMA(()), pltpu.SemaphoreType.DMA(())],
    )(G)


def add_core_halves(G, land, core):
    nb = PACK_HALF // PACK_BLOCK

    def body(c_ref, g_ref, l_ref, o_ref):
        o_ref[...] = (g_ref[...].astype(f32) + l_ref[...].astype(f32)).astype(o_ref.dtype)

    blk = (1, PACK_BLOCK, LANES)
    return pl.pallas_call(
        body, name="add_core_halves",
        grid_spec=pltpu.PrefetchScalarGridSpec(
            num_scalar_prefetch=1, grid=(N_CHIPS, nb),
            in_specs=[pl.BlockSpec(blk, lambda k, i, c_ref: (k, c_ref[0] * nb + i, 0)),
                      pl.BlockSpec(blk, lambda k, i, c_ref: (k, i, 0))],
            out_specs=pl.BlockSpec(blk, lambda k, i, c_ref: (k, i, 0))),
        out_shape=jax.ShapeDtypeStruct(land.shape, land.dtype),
        compiler_params=_cparams("parallel", "parallel"),
    )(core, G, land)


def scatter_to_chips(part):
    def body(p_ref, land_ref, send_sems, recv_sems):
        x, y, c = _place()
        me_chip = 2 * x + y
        chips = _other_chips(x, y)
        sends = [_remote(p_ref.at[2 * px + py], land_ref.at[me_chip], send_sems.at[j], recv_sems.at[j], (px, py, c))
                 for j, (px, py) in enumerate(chips)]
        for cp in sends:
            cp.start()
        for j, (px, py) in enumerate(chips):
            slot = land_ref.at[2 * px + py]
            _remote(slot, slot, send_sems.at[j], recv_sems.at[j], (px, py, c)).wait_recv()
        for cp in sends:
            cp.wait_send()

    return pl.pallas_call(
        body, name="scatter_to_chips", in_specs=[HBM_SPEC], out_specs=HBM_SPEC,
        out_shape=jax.ShapeDtypeStruct(part.shape, part.dtype),
        scratch_shapes=[pltpu.SemaphoreType.DMA((3,)), pltpu.SemaphoreType.DMA((3,))],
    )(part)


def sum_chips(part, land, order):
    nb = PACK_HALF // PACK_BLOCK

    def body(order_ref, p_ref, l1_ref, l2_ref, l3_ref, o_ref):
        o_ref[...] = ((p_ref[0].astype(f32) + l1_ref[0].astype(f32)) + l2_ref[0].astype(f32)) + l3_ref[0].astype(f32)

    slot = lambda j: pl.BlockSpec((1, PACK_BLOCK, LANES), lambda i, order_ref: (order_ref[j], i, 0))
    return pl.pallas_call(
        body, name="sum_chips",
        grid_spec=pltpu.PrefetchScalarGridSpec(
            num_scalar_prefetch=1, grid=(nb,), in_specs=[slot(0), slot(1), slot(2), slot(3)],
            out_specs=pl.BlockSpec((PACK_BLOCK, LANES), lambda i, order_ref: (i, 0))),
        out_shape=jax.ShapeDtypeStruct((PACK_HALF, LANES), f32),
        compiler_params=_cparams("parallel"),
    )(order, part, land, land, land)


def swap_core_halves(red):
    def body(r_ref, out_ref, send_sem, recv_sem):
        x, y, c = _place()
        cp = _remote(r_ref, out_ref, send_sem, recv_sem, (x, y, 1 - c))
        cp.start()
        cp.wait()

    return pl.pallas_call(
        body, name="swap_core_halves", in_specs=[HBM_SPEC], out_specs=HBM_SPEC,
        out_shape=jax.ShapeDtypeStruct(red.shape, red.dtype),
        scratch_shapes=[pltpu.SemaphoreType.DMA(()), pltpu.SemaphoreType.DMA(())],
    )(red)


def _half(ref, core):
    rows = ref.shape[-2] // 2
    return ref.at[(slice(None),) * (len(ref.shape) - 2) + (pl.ds(core * rows, rows), slice(None))]


def gather_weights(shards, conv):
    n = len(shards)

    def body(*refs):
        src, conv_src = refs[:n], refs[n]
        out, conv_out = refs[n + 1:2 * n + 1], refs[2 * n + 1]
        send_sems, recv_sems = refs[2 * n + 2], refs[2 * n + 3]
        x, y, c = _place()
        me_chip = 2 * x + y
        sibling = (x, y, 1 - c)
        chips = _other_chips(x, y)
        sends = []
        for a in range(n):
            for j, (px, py) in enumerate(chips):
                sends.append(_remote(_half(src[a], c), _half(out[a].at[me_chip], c),
                                     send_sems.at[6 * a + j], recv_sems.at[6 * a + j], (px, py, c)))
        for j, (px, py) in enumerate(chips):
            sends.append(_remote(conv_src, conv_out.at[me_chip], send_sems.at[6 * n + j], recv_sems.at[6 * n + j], (px, py, c)))
        for cp in sends:
            cp.start()
        passed = []
        for a in range(n):
            for j, (px, py) in enumerate(chips):
                theirs = _half(out[a].at[2 * px + py], c)
                _remote(theirs, theirs, send_sems.at[6 * a + j], recv_sems.at[6 * a + j], (px, py, c)).wait_recv()
                cp = _remote(theirs, theirs, send_sems.at[6 * a + 3 + j], recv_sems.at[6 * a + 3 + j], sibling)
                cp.start()
                passed.append(cp)
        for j, (px, py) in enumerate(chips):
            theirs = conv_out.at[2 * px + py]
            _remote(theirs, theirs, send_sems.at[6 * n + j], recv_sems.at[6 * n + j], (px, py, c)).wait_recv()
        for a in range(n):
            for j, (px, py) in enumerate(chips):
                theirs = _half(out[a].at[2 * px + py], 1 - c)
                _remote(theirs, theirs, send_sems.at[6 * a + 3 + j], recv_sems.at[6 * a + 3 + j], sibling).wait_recv()
        for cp in sends + passed:
            cp.wait_send()

    return pl.pallas_call(
        body, name="gather_weights", in_specs=[HBM_SPEC] * (n + 1), out_specs=[HBM_SPEC] * (n + 1),
        out_shape=[jax.ShapeDtypeStruct((N_CHIPS,) + s.shape, s.dtype) for s in list(shards) + [conv]],
        scratch_shapes=[pltpu.SemaphoreType.DMA((6 * n + 3,)), pltpu.SemaphoreType.DMA((6 * n + 3,))],
    )(*shards, conv)


SEM_SPEC = pl.BlockSpec(memory_space=pltpu.SEMAPHORE)
SPLIT_EFFECT = pltpu.SideEffectType.DATAFLOW_SIDE_EFFECTING


def _gather_async_copies(src, land, send_sems, recv_sems, x, y, c):
    me_chip = 2 * x + y
    sends, arrivals = [], []
    for a in range(len(src)):
        for j, (px, py) in enumerate(_other_chips(x, y)):
            for core in range(2):
                sends.append(_remote(_half(src[a], c), _half(land[a].at[me_chip], c), send_sems.at[6 * a + 2 * j + core],
                                     recv_sems.at[6 * a + 2 * j + c], (px, py, core)))
                theirs = _half(land[a].at[2 * px + py], core)
                arrivals.append(_remote(theirs, theirs, send_sems.at[6 * a + 2 * j + core],
                                        recv_sems.at[6 * a + 2 * j + core], (px, py, core)))
    return sends, arrivals


def gather_weights_start(shards, after):
    n = len(shards)

    def body(*refs):
        src, land = refs[:n], refs[n:2 * n]
        send_sems, recv_sems, token = refs[2 * n + 1], refs[2 * n + 2], refs[4 * n + 3]
        x, y, c = _place()
        for cp in _gather_async_copies(src, land, send_sems, recv_sems, x, y, c)[0]:
            cp.start()
        token[...] = jnp.zeros_like(token)

    zones = [pltpu.with_memory_space_constraint(lax.empty((N_CHIPS,) + s.shape, s.dtype), pltpu.HBM) for s in shards]
    srcs = [pltpu.with_memory_space_constraint(s, pltpu.HBM) for s in shards]
    out = pl.pallas_call(
        body, name="gather_weights_start",
        out_shape=[pltpu.SemaphoreType.DMA((6 * n,)), pltpu.SemaphoreType.DMA((6 * n,))]
        + [pltpu.HBM(s.shape, s.dtype) for s in shards] + [pltpu.HBM(z.shape, z.dtype) for z in zones]
        + [jax.ShapeDtypeStruct((8, LANES), f32)],
        in_specs=[HBM_SPEC] * (2 * n) + [pl.BlockSpec(memory_space=pl.ANY)],
        out_specs=[SEM_SPEC, SEM_SPEC] + [HBM_SPEC] * (2 * n) + [pl.BlockSpec(memory_space=pltpu.VMEM)],
        input_output_aliases={i: 2 + i for i in range(2 * n)},
        compiler_params=pltpu.CompilerParams(has_side_effects=SPLIT_EFFECT),
    )(*srcs, *zones, after)
    return out[0], out[1], out[2:2 + n], out[2 + n:2 + 2 * n], out[-1]


def gather_weights_wait(send_sems, recv_sems, shards, zones, after):
    n = len(shards)

    def body(*refs):
        src, land = refs[:n], refs[n:2 * n]
        send_sems, recv_sems = refs[2 * n], refs[2 * n + 1]
        x, y, c = _place()
        sends, arrivals = _gather_async_copies(src, land, send_sems, recv_sems, x, y, c)
        for cp in sends:
            cp.wait_send()
        for cp in arrivals:
            cp.wait_recv()

    out = pl.pallas_call(
        body, name="gather_weights_wait",
        out_shape=[pltpu.HBM(s.shape, s.dtype) for s in shards] + [pltpu.HBM(z.shape, z.dtype) for z in zones],
        in_specs=[HBM_SPEC] * (2 * n) + [SEM_SPEC, SEM_SPEC, pl.BlockSpec(memory_space=pl.ANY)],
        out_specs=[HBM_SPEC] * (2 * n),
        input_output_aliases={i: i for i in range(2 * n)},
        compiler_params=pltpu.CompilerParams(has_side_effects=SPLIT_EFFECT),
    )(*shards, *zones, send_sems, recv_sems, after)
    return out[n:]


def swap_grad_halves(grads, *, name):
    n = len(grads)

    def body(*refs):
        g, land, send_sems, recv_sems = refs[:n], refs[n:2 * n], refs[2 * n], refs[2 * n + 1]
        x, y, c = _place()
        copies = [_remote(_half(g[a], 1 - c), land[a], send_sems.at[a], recv_sems.at[a], (x, y, 1 - c)) for a in range(n)]
        for cp in copies:
            cp.start()
        for cp in copies:
            cp.wait()

    return pl.pallas_call(
        body, name=name, in_specs=[HBM_SPEC] * n, out_specs=[HBM_SPEC] * n,
        out_shape=[jax.ShapeDtypeStruct((N_CHIPS, g.shape[1] // 2, g.shape[2]), g.dtype) for g in grads],
        scratch_shapes=[pltpu.SemaphoreType.DMA((n,)), pltpu.SemaphoreType.DMA((n,))],
    )(*grads)


GRAD_ROWS = 256


def add_grad_halves(g, land, core, *, name):
    _, half, cols = land.shape
    tr = GRAD_ROWS if half % GRAD_ROWS == 0 else half
    nb = half // tr

    def body(c_ref, g_ref, l_ref, o_ref):
        o_ref[...] = (g_ref[...].astype(f32) + l_ref[...].astype(f32)).astype(o_ref.dtype)

    blk = (1, tr, cols)
    return pl.pallas_call(
        body, name=name,
        grid_spec=pltpu.PrefetchScalarGridSpec(
            num_scalar_prefetch=1, grid=(N_CHIPS, nb),
            in_specs=[pl.BlockSpec(blk, lambda k, i, c_ref: (k, c_ref[0] * nb + i, 0)),
                      pl.BlockSpec(blk, lambda k, i, c_ref: (k, i, 0))],
            out_specs=pl.BlockSpec(blk, lambda k, i, c_ref: (k, i, 0))),
        out_shape=jax.ShapeDtypeStruct(land.shape, land.dtype),
        compiler_params=_cparams("parallel", "parallel"),
    )(core, g, land)


def scatter_grads(parts):
    n = len(parts)

    def body(*refs):
        p, land, send_sems, recv_sems = refs[:n], refs[n:2 * n], refs[2 * n], refs[2 * n + 1]
        x, y, c = _place()
        me_chip = 2 * x + y
        chips = _other_chips(x, y)
        sends = [_remote(p[a].at[2 * px + py], land[a].at[me_chip], send_sems.at[3 * a + j], recv_sems.at[3 * a + j], (px, py, c))
                 for a in range(n) for j, (px, py) in enumerate(chips)]
        for cp in sends:
            cp.start()
        for a in range(n):
            for j, (px, py) in enumerate(chips):
                slot = land[a].at[2 * px + py]
                _remote(slot, slot, send_sems.at[3 * a + j], recv_sems.at[3 * a + j], (px, py, c)).wait_recv()
        for cp in sends:
            cp.wait_send()

    return pl.pallas_call(
        body, name="scatter_grads", in_specs=[HBM_SPEC] * n, out_specs=[HBM_SPEC] * n,
        out_shape=[jax.ShapeDtypeStruct(p.shape, p.dtype) for p in parts],
        scratch_shapes=[pltpu.SemaphoreType.DMA((3 * n,)), pltpu.SemaphoreType.DMA((3 * n,))],
    )(*parts)


def _scatter_async_copies(parts, land, send_sems, recv_sems, x, y, c):
    me_chip = 2 * x + y
    sends, arrivals = [], []
    for a in range(len(parts)):
        for j, (px, py) in enumerate(_other_chips(x, y)):
            sems = (send_sems.at[3 * a + j], recv_sems.at[3 * a + j], (px, py, c))
            sends.append(_remote(parts[a].at[2 * px + py], land[a].at[me_chip], *sems))
            slot = land[a].at[2 * px + py]
            arrivals.append(_remote(slot, slot, *sems))
    return sends, arrivals


def scatter_grads_start(parts, *, name):
    n = len(parts)

    def body(*refs):
        p, land = refs[:n], refs[n:2 * n]
        send_sems, recv_sems, token = refs[2 * n], refs[2 * n + 1], refs[4 * n + 2]
        x, y, c = _place()
        for cp in _scatter_async_copies(p, land, send_sems, recv_sems, x, y, c)[0]:
            cp.start()
        token[...] = jnp.zeros_like(token)

    zones = [pltpu.with_memory_space_constraint(lax.empty(p.shape, p.dtype), pltpu.HBM) for p in parts]
    srcs = [pltpu.with_memory_space_constraint(p, pltpu.HBM) for p in parts]
    hbm = [pltpu.HBM(p.shape, p.dtype) for p in parts]
    out = pl.pallas_call(
        body, name=name,
        out_shape=[pltpu.SemaphoreType.DMA((3 * n,)), pltpu.SemaphoreType.DMA((3 * n,))] + hbm + hbm
        + [jax.ShapeDtypeStruct((8, LANES), f32)],
        in_specs=[HBM_SPEC] * (2 * n),
        out_specs=[SEM_SPEC, SEM_SPEC] + [HBM_SPEC] * (2 * n) + [pl.BlockSpec(memory_space=pltpu.VMEM)],
        input_output_aliases={i: 2 + i for i in range(2 * n)},
        compiler_params=pltpu.CompilerParams(has_side_effects=SPLIT_EFFECT),
    )(*srcs, *zones)
    return out[0], out[1], out[2:2 + n], out[2 + n:2 + 2 * n], out[-1]


def scatter_grads_wait(send_sems, recv_sems, parts, zones, after, *, name):
    n = len(parts)

    def body(*refs):
        p, land = refs[:n], refs[n:2 * n]
        x, y, c = _place()
        sends, arrivals = _scatter_async_copies(p, land, refs[2 * n], refs[2 * n + 1], x, y, c)
        for cp in sends:
            cp.wait_send()
        for cp in arrivals:
            cp.wait_recv()

    hbm = [pltpu.HBM(p.shape, p.dtype) for p in parts]
    out = pl.pallas_call(
        body, name=name, out_shape=hbm + hbm,
        in_specs=[HBM_SPEC] * (2 * n) + [SEM_SPEC, SEM_SPEC, pl.BlockSpec(memory_space=pl.ANY)],
        out_specs=[HBM_SPEC] * (2 * n),
        input_output_aliases={i: i for i in range(2 * n)},
        compiler_params=pltpu.CompilerParams(has_side_effects=SPLIT_EFFECT),
    )(*parts, *zones, send_sems, recv_sems, after)
    return out[:n], out[n:]


def sum_grads(part, land, order, *, name):
    _, half, cols = part.shape
    tr = GRAD_ROWS if half % GRAD_ROWS == 0 else half

    def body(order_ref, p_ref, l1_ref, l2_ref, l3_ref, o_ref):
        o_ref[...] = ((p_ref[0].astype(f32) + l1_ref[0].astype(f32)) + l2_ref[0].astype(f32)) + l3_ref[0].astype(f32)

    slot = lambda j: pl.BlockSpec((1, tr, cols), lambda i, order_ref: (order_ref[j], i, 0))
    return pl.pallas_call(
        body, name=name,
        grid_spec=pltpu.PrefetchScalarGridSpec(
            num_scalar_prefetch=1, grid=(half // tr,), in_specs=[slot(0), slot(1), slot(2), slot(3)],
            out_specs=pl.BlockSpec((tr, cols), lambda i, order_ref: (i, 0))),
        out_shape=jax.ShapeDtypeStruct((half, cols), f32),
        compiler_params=_cparams("parallel"),
    )(order, part, land, land, land)


def _peer(x, y, c, r):
    return ((1 - x) if r & 4 else x, (1 - y) if r & 2 else y, (1 - c) if r & 1 else c)


def _reduce_async_copies(grads, land, send_sems, recv_sems, x, y, c):
    me = 4 * x + 2 * y + c
    sends, arrivals = [], []
    for a in range(len(grads)):
        for r in range(1, N_DEV):
            px, py, pc = _peer(x, y, c, r)
            sems = (send_sems.at[7 * a + r - 1], recv_sems.at[7 * a + r - 1], (px, py, pc))
            sends.append(_remote(_half(grads[a].at[2 * px + py], pc), land[a].at[me], *sems))
            slot = land[a].at[4 * px + 2 * py + pc]
            arrivals.append(_remote(slot, slot, *sems))
    return sends, arrivals


def reduce_grads_start(grads, *, name):
    n = len(grads)

    def body(*refs):
        g, land = refs[:n], refs[n:2 * n]
        send_sems, recv_sems, token = refs[2 * n], refs[2 * n + 1], refs[4 * n + 2]
        x, y, c = _place()
        for cp in _reduce_async_copies(g, land, send_sems, recv_sems, x, y, c)[0]:
            cp.start()
        token[...] = jnp.zeros_like(token)

    zones = [pltpu.with_memory_space_constraint(lax.empty((N_DEV, g.shape[1] // 2, g.shape[2]), g.dtype), pltpu.HBM)
             for g in grads]
    srcs = [pltpu.with_memory_space_constraint(g, pltpu.HBM) for g in grads]
    out = pl.pallas_call(
        body, name=name,
        out_shape=[pltpu.SemaphoreType.DMA((7 * n,)), pltpu.SemaphoreType.DMA((7 * n,))]
        + [pltpu.HBM(g.shape, g.dtype) for g in grads] + [pltpu.HBM(z.shape, z.dtype) for z in zones]
        + [jax.ShapeDtypeStruct((8, LANES), f32)],
        in_specs=[HBM_SPEC] * (2 * n),
        out_specs=[SEM_SPEC, SEM_SPEC] + [HBM_SPEC] * (2 * n) + [pl.BlockSpec(memory_space=pltpu.VMEM)],
        input_output_aliases={i: 2 + i for i in range(2 * n)},
        compiler_params=pltpu.CompilerParams(has_side_effects=SPLIT_EFFECT),
    )(*srcs, *zones)
    return out[0], out[1], out[2:2 + n], out[2 + n:2 + 2 * n], out[-1]


def reduce_grads_wait(send_sems, recv_sems, grads, zones, after, *, name):
    n = len(grads)

    def body(*refs):
        g, land = refs[:n], refs[n:2 * n]
        x, y, c = _place()
        sends, arrivals = _reduce_async_copies(g, land, refs[2 * n], refs[2 * n + 1], x, y, c)
        for cp in sends:
            cp.wait_send()
        for cp in arrivals:
            cp.wait_recv()

    hbm = [pltpu.HBM(a.shape, a.dtype) for a in list(grads) + list(zones)]
    out = pl.pallas_call(
        body, name=name, out_shape=hbm,
        in_specs=[HBM_SPEC] * (2 * n) + [SEM_SPEC, SEM_SPEC, pl.BlockSpec(memory_space=pl.ANY)],
        out_specs=[HBM_SPEC] * (2 * n),
        input_output_aliases={i: i for i in range(2 * n)},
        compiler_params=pltpu.CompilerParams(has_side_effects=SPLIT_EFFECT),
    )(*grads, *zones, send_sems, recv_sems, after)
    return out[:n], out[n:]


def sum_partials(g, land, where, *, name):
    _, half, cols = land.shape
    tr = GRAD_ROWS if half % GRAD_ROWS == 0 else half
    nb = half // tr

    def body(where_ref, g_ref, *rest):
        o_ref = rest[-1]
        acc = g_ref[0].astype(f32)
        for l_ref in rest[:-1]:
            acc = acc + l_ref[0].astype(f32)
        o_ref[...] = acc

    blk = (1, tr, cols)
    slot = lambda j: pl.BlockSpec(blk, lambda i, where_ref: (where_ref[2 + j], i, 0))
    return pl.pallas_call(
        body, name=name,
        grid_spec=pltpu.PrefetchScalarGridSpec(
            num_scalar_prefetch=1, grid=(nb,),
            in_specs=[pl.BlockSpec(blk, lambda i, where_ref: (where_ref[0], where_ref[1] * nb + i, 0))]
            + [slot(j) for j in range(N_DEV - 1)],
            out_specs=pl.BlockSpec((tr, cols), lambda i, where_ref: (i, 0))),
        out_shape=jax.ShapeDtypeStruct((half, cols), f32),
        compiler_params=_cparams("parallel"),
    )(where, g, *([land] * (N_DEV - 1)))


def swap_reduced_halves(mine, *, name):
    n = len(mine)

    def body(*refs):
        r, out, send_sems, recv_sems = refs[:n], refs[n:2 * n], refs[2 * n], refs[2 * n + 1]
        x, y, c = _place()
        copies = [_remote(r[a], out[a], send_sems.at[a], recv_sems.at[a], (x, y, 1 - c)) for a in range(n)]
        for cp in copies:
            cp.start()
        for cp in copies:
            cp.wait()

    return pl.pallas_call(
        body, name=name, in_specs=[HBM_SPEC] * n, out_specs=[HBM_SPEC] * n,
        out_shape=[jax.ShapeDtypeStruct(r.shape, r.dtype) for r in mine],
        scratch_shapes=[pltpu.SemaphoreType.DMA((n,)), pltpu.SemaphoreType.DMA((n,))],
    )(*mine)


def adamw_halves(w, mine, theirs, m, v, core, *, name):
    R, C = w.shape
    tr = min(GRAD_ROWS, R // 2)
    half_nb = R // 2 // tr

    def body(c_ref, w_ref, a_ref, b_ref, m_ref, v_ref, g_ref, d_ref, nm_ref, nv_ref):
        low = pl.program_id(0) < half_nb
        gv = jnp.where(low == (c_ref[0] == 0), a_ref[...], b_ref[...])
        nm = ADAM_B1 * m_ref[...] + (1.0 - ADAM_B1) * gv
        nv = ADAM_B2 * v_ref[...] + (1.0 - ADAM_B2) * jnp.square(gv)
        m_hat = nm / (1.0 - ADAM_B1 ** ADAM_STEP)
        v_hat = nv / (1.0 - ADAM_B2 ** ADAM_STEP)
        g_ref[...] = gv
        d_ref[...] = -ADAM_LR * (m_hat / (jnp.sqrt(v_hat) + ADAM_EPS) + ADAM_WD * w_ref[...])
        nm_ref[...] = nm
        nv_ref[...] = nv

    full = pl.BlockSpec((tr, C), lambda i, c_ref: (i, 0))
    part = pl.BlockSpec((tr, C), lambda i, c_ref: (i % half_nb, 0))
    out = jax.ShapeDtypeStruct((R, C), f32)
    return pl.pallas_call(
        body, name=name,
        grid_spec=pltpu.PrefetchScalarGridSpec(
            num_scalar_prefetch=1, grid=(2 * half_nb,), in_specs=[full, part, part, full, full], out_specs=[full] * 4),
        out_shape=[out] * 4, compiler_params=_cparams("parallel"),
    )(core, w, mine, theirs, m, v)


N_DEV = 8


def all_reduce_small(v):
    def body(src_ref, out_ref, land_ref, send_sems, recv_sems):
        x, y, c = _place()
        me = 4 * x + 2 * y + c
        copies = []
        for r in range(1, N_DEV):
            peer = ((1 - x) if r & 4 else x, (1 - y) if r & 2 else y, (1 - c) if r & 1 else c)
            copies.append(_remote(src_ref, land_ref.at[r], send_sems.at[r - 1], recv_sems.at[r - 1], peer))
        for cp in copies:
            cp.start()
        land_ref[0] = src_ref[...]
        for cp in copies:
            cp.wait()
        acc = land_ref[me]
        for d in range(1, N_DEV):
            acc = acc + land_ref[jnp.bitwise_xor(me, d)]
        out_ref[...] = acc

    vm = pl.BlockSpec(memory_space=pltpu.VMEM)
    return pl.pallas_call(
        body, name="all_reduce_small", in_specs=[vm], out_specs=vm,
        out_shape=jax.ShapeDtypeStruct(v.shape, v.dtype),
        scratch_shapes=[pltpu.VMEM((N_DEV,) + v.shape, v.dtype),
                        pltpu.SemaphoreType.DMA((N_DEV - 1,)), pltpu.SemaphoreType.DMA((N_DEV - 1,))],
    )(v)


def adamw(w, g, m, v, *, name, tr=None, tc=None):
    R, C = w.shape
    if tc is None:
        tr, tc = min(tr, R), C
        blk = pl.BlockSpec((tr, C), lambda i: (i, 0))
    else:
        tr = R
        blk = pl.BlockSpec((R, tc), lambda i: (0, i))

    def body(w_ref, g_ref, m_ref, v_ref, d_ref, nm_ref, nv_ref):
        gv = g_ref[...]
        nm = ADAM_B1 * m_ref[...] + (1.0 - ADAM_B1) * gv
        nv = ADAM_B2 * v_ref[...] + (1.0 - ADAM_B2) * jnp.square(gv)
        m_hat = nm / (1.0 - ADAM_B1 ** ADAM_STEP)
        v_hat = nv / (1.0 - ADAM_B2 ** ADAM_STEP)
        d_ref[...] = -ADAM_LR * (m_hat / (jnp.sqrt(v_hat) + ADAM_EPS) + ADAM_WD * w_ref[...])
        nm_ref[...] = nm
        nv_ref[...] = nv

    out = jax.ShapeDtypeStruct((R, C), f32)
    return pl.pallas_call(
        body, name=name, grid=((R // tr) * (C // tc),), in_specs=[blk] * 4, out_specs=[blk] * 3, out_shape=[out] * 3,
        compiler_params=_cparams("parallel"),
    )(w, g, m, v)


BIG_SHARDS = (("w_in", (1024, 900), True), ("w_out", (256, 1024), False), ("w_cq", (256, 512), False),
              ("w_ckv", (256, 1024), False), ("w_co", (512, 256), True), ("w_mlp1", (1024, 1024), True),
              ("w_mlp2", (1024, 1024), False))
CONV_SHARD = (CONV_WIDTH, 3 * GDN_WIDTH // N_CHIPS)
SMALL_DIMS = (("norm_mix_g", 1024), ("fox_qnorm_g", 64), ("fox_knorm_g", 64), ("fox_f_bias", 8), ("fox_onorm_g", 64),
              ("gdn_A_log", 4), ("gdn_dt_bias", 4), ("gdn_onorm_g", 128), ("norm_xattn_g", 1024), ("mem_norm_g", 1024),
              ("xattn_qnorm_g", 128), ("xattn_knorm_g", 128), ("norm_mlp_g", 1024))
WEIGHT_ORDER = ("norm_mix_g", "w_in", "fox_qnorm_g", "fox_knorm_g", "fox_f_bias", "fox_onorm_g", "gdn_conv_w", "gdn_A_log",
                "gdn_dt_bias", "gdn_onorm_g", "w_out", "norm_xattn_g", "mem_norm_g", "w_cq", "w_ckv", "xattn_qnorm_g",
                "xattn_knorm_g", "w_co", "norm_mlp_g", "w_mlp1", "w_mlp2")


def _pack_rows(pieces, rows, lead=()):
    cat = jnp.concatenate([p.reshape(lead + (-1,)) for p in pieces], axis=-1)
    cat = jnp.pad(cat, [(0, 0)] * len(lead) + [(0, rows * LANES - cat.shape[-1])])
    return cat.reshape(lead + (rows, LANES))


def _unpack_rows(buf, sizes, lead=()):
    flat = buf.reshape(lead + (-1,))
    out, off = [], 0
    for n in sizes:
        out.append(flat[..., off:off + n])
        off += n
    return out


def _conv_to_wire(conv):
    return lax.bitcast_convert_type(conv, bf16)


def _conv_from_wire(wire):
    return lax.bitcast_convert_type(wire, f32)


SMALL_ROWS = 96
SMALL_ADAM_ROWS = 56


def kernel(x, mem, norm_mix_g, w_in, fox_qnorm_g, fox_knorm_g, fox_f_bias, fox_onorm_g, gdn_conv_w, gdn_A_log, gdn_dt_bias, gdn_onorm_g, w_out, norm_xattn_g, mem_norm_g, w_cq, w_ckv, xattn_qnorm_g, xattn_knorm_g, w_co, norm_mlp_g, w_mlp1, w_mlp2, loss_target, m_norm_mix_g, m_w_in, m_fox_qnorm_g, m_fox_knorm_g, m_fox_f_bias, m_fox_onorm_g, m_gdn_conv_w, m_gdn_A_log, m_gdn_dt_bias, m_gdn_onorm_g, m_w_out, m_norm_xattn_g, m_mem_norm_g, m_w_cq, m_w_ckv, m_xattn_qnorm_g, m_xattn_knorm_g, m_w_co, m_norm_mlp_g, m_w_mlp1, m_w_mlp2, v_norm_mix_g, v_w_in, v_fox_qnorm_g, v_fox_knorm_g, v_fox_f_bias, v_fox_onorm_g, v_gdn_conv_w, v_gdn_A_log, v_gdn_dt_bias, v_gdn_onorm_g, v_w_out, v_norm_xattn_g, v_mem_norm_g, v_w_cq, v_w_ckv, v_xattn_qnorm_g, v_xattn_knorm_g, v_w_co, v_norm_mlp_g, v_w_mlp1, v_w_mlp2):
    wts = dict(norm_mix_g=norm_mix_g, w_in=w_in, fox_qnorm_g=fox_qnorm_g, fox_knorm_g=fox_knorm_g, fox_f_bias=fox_f_bias,
               fox_onorm_g=fox_onorm_g, gdn_conv_w=gdn_conv_w, gdn_A_log=gdn_A_log, gdn_dt_bias=gdn_dt_bias,
               gdn_onorm_g=gdn_onorm_g, w_out=w_out, norm_xattn_g=norm_xattn_g, mem_norm_g=mem_norm_g, w_cq=w_cq, w_ckv=w_ckv,
               xattn_qnorm_g=xattn_qnorm_g, xattn_knorm_g=xattn_knorm_g, w_co=w_co, norm_mlp_g=norm_mlp_g, w_mlp1=w_mlp1,
               w_mlp2=w_mlp2)
    mom = dict(norm_mix_g=m_norm_mix_g, w_in=m_w_in, fox_qnorm_g=m_fox_qnorm_g, fox_knorm_g=m_fox_knorm_g,
               fox_f_bias=m_fox_f_bias, fox_onorm_g=m_fox_onorm_g, gdn_conv_w=m_gdn_conv_w, gdn_A_log=m_gdn_A_log,
               gdn_dt_bias=m_gdn_dt_bias, gdn_onorm_g=m_gdn_onorm_g, w_out=m_w_out, norm_xattn_g=m_norm_xattn_g,
               mem_norm_g=m_mem_norm_g, w_cq=m_w_cq, w_ckv=m_w_ckv, xattn_qnorm_g=m_xattn_qnorm_g,
               xattn_knorm_g=m_xattn_knorm_g, w_co=m_w_co, norm_mlp_g=m_norm_mlp_g, w_mlp1=m_w_mlp1, w_mlp2=m_w_mlp2)
    var = dict(norm_mix_g=v_norm_mix_g, w_in=v_w_in, fox_qnorm_g=v_fox_qnorm_g, fox_knorm_g=v_fox_knorm_g,
               fox_f_bias=v_fox_f_bias, fox_onorm_g=v_fox_onorm_g, gdn_conv_w=v_gdn_conv_w, gdn_A_log=v_gdn_A_log,
               gdn_dt_bias=v_gdn_dt_bias, gdn_onorm_g=v_gdn_onorm_g, w_out=v_w_out, norm_xattn_g=v_norm_xattn_g,
               mem_norm_g=v_mem_norm_g, w_cq=v_w_cq, w_ckv=v_w_ckv, xattn_qnorm_g=v_xattn_qnorm_g,
               xattn_knorm_g=v_xattn_knorm_g, w_co=v_w_co, norm_mlp_g=v_norm_mlp_g, w_mlp1=v_w_mlp1, w_mlp2=v_w_mlp2)
    B, S, D = x.shape
    T = B * S
    big_names = [n for n, _, _ in BIG_SHARDS]
    chip = 2 * lax.axis_index("x") + lax.axis_index("y")
    core = lax.axis_index("c").astype(jnp.int32).reshape(1)

    shards = {n: wts[n][0].astype(MXU_DTYPE) for n in big_names[1:]}
    in_t = lambda p: jnp.swapaxes(p[0], 0, 1)
    shards["w_in"] = jnp.pad(in_t(w_in).astype(MXU_DTYPE), ((0, IN_SHARD_PAD - IN_SHARD), (0, 0)))
    w_in_all, conv_all = gather_weights([shards["w_in"]], gdn_conv_w[0])
    late = big_names[1:]
    send_sems, recv_sems, late_src, late_zones, token = gather_weights_start([shards[n] for n in late], conv_all)
    own = lambda g, s: lax.dynamic_update_slice(g, s[None], (chip,) + (0,) * s.ndim)
    full = {"w_in": own(w_in_all, shards["w_in"])}
    conv_full = own(conv_all, gdn_conv_w[0]).transpose(1, 0, 2).reshape(CONV_WIDTH, 3 * GDN_WIDTH)
    rows = lambda g: g.reshape(N_CHIPS * g.shape[1], g.shape[2])
    w_in_t = full["w_in"][:, :IN_SHARD].reshape(IN_DIM, D_MODEL)

    def late_weights(after):
        zones = gather_weights_wait(send_sems, recv_sems, late_src, late_zones, after)
        got = {n: own(z, shards[n]) for n, z in zip(late, zones)}
        return dict(w_out=rows(got["w_out"]), w_cq=rows(got["w_cq"]), w_ckv=rows(got["w_ckv"]), w_co=got["w_co"],
                    w_mlp1=got["w_mlp1"], w_mlp2=rows(got["w_mlp2"]))

    in_flight = []

    def grads_ready(ready):
        names = list(ready)
        *started, tok = reduce_grads_start([ready[n] for n in names], name="reduce_grads_start_%d" % len(in_flight))
        in_flight.append((names, *started))
        return tok

    w = dict(wa_t=align_w_in_t(w_in_t), conv_w=conv_full, late=late_weights, grads_ready=grads_ready)
    sp = {n: wts[n] for n, _ in SMALL_DIMS}
    sp["norm_mix_g"] = sp["norm_mix_g"] + token[0, 0]

    loss_part, grad_x, g_big, g_small = local_step(x.reshape(T, D), mem.reshape(-1, D), loss_target.reshape(T, D), w, sp, B=B)

    small_pieces = [g_small[n] for n, _ in SMALL_DIMS] + [g_small["gdn_conv_w"], loss_part]
    small_sizes = [d for _, d in SMALL_DIMS] + [CONV_WIDTH * 3 * GDN_WIDTH, LANES]
    red_small = _unpack_rows(all_reduce_small(_pack_rows(small_pieces, SMALL_ROWS)), small_sizes)
    grads = {n: p.reshape(1, d) for (n, d), p in zip(SMALL_DIMS, red_small)}
    conv_grad = lax.dynamic_slice(red_small[-2].reshape(CONV_WIDTH, 3 * GDN_WIDTH), (0, chip * CONV_SHARD[1]), CONV_SHARD)
    grads["gdn_conv_w"] = conv_grad.reshape((1,) + CONV_SHARD)
    loss = red_small[-1][0]

    parts, zones = {}, {}

    def wait_group(k, after):
        names, send_sems, recv_sems, thru, land = in_flight[k]
        thru, land = reduce_grads_wait(send_sems, recv_sems, thru, land, after, name="reduce_grads_wait_%d" % k)
        parts.update(zip(names, thru))
        zones.update(zip(names, land))

    wait_group(0, grad_x)
    wait_group(1, grad_x)
    dev = 2 * chip + core[0]
    where = jnp.stack([chip, core[0]] + [dev ^ r for r in range(1, N_DEV)]).astype(jnp.int32)
    mine = [sum_partials(parts[n], zones[n], where, name="sum_partials_" + n) for n in late]
    theirs = swap_reduced_halves(mine, name="swap_reduced_halves")

    delta, new_m, new_v = {}, {}, {}
    for n, a, b in zip(late, mine, theirs):
        g, d, nm, nv = adamw_halves(wts[n][0], a, b, mom[n][0], var[n][0], core, name="adamw_" + n)
        grads[n], delta[n], new_m[n], new_v[n] = g[None], d[None], nm[None], nv[None]
    wait_group(2, new_v[late[-1]])
    mine_in = sum_partials(parts["w_in"], zones["w_in"], where, name="sum_partials_w_in")
    (theirs_in,) = swap_reduced_halves([mine_in], name="swap_reduced_halves_w_in")
    south = core[0] == 0
    g_in_t = jnp.concatenate([jnp.where(south, mine_in, theirs_in), jnp.where(south, theirs_in, mine_in)])[:IN_SHARD]
    back = lambda t: jnp.swapaxes(t, 0, 1)[None]
    d, nm, nv = adamw(in_t(w_in), g_in_t, in_t(m_w_in), in_t(v_w_in), name="adamw_w_in", tc=256)
    grads["w_in"], delta["w_in"], new_m["w_in"], new_v["w_in"] = back(g_in_t), back(d), back(nm), back(nv)
    small_names = [n for n, _ in SMALL_DIMS] + ["gdn_conv_w"]
    small_sz = [d for _, d in SMALL_DIMS] + [CONV_SHARD[0] * CONV_SHARD[1]]
    packed4 = [_pack_rows([src[n] for n in small_names], SMALL_ADAM_ROWS) for src in (wts, grads, mom, var)]
    outs = adamw(*packed4, name="adamw_small", tr=SMALL_ADAM_ROWS)
    for dst, buf in zip((delta, new_m, new_v), outs):
        for n, p in zip(small_names, _unpack_rows(buf, small_sz)):
            dst[n] = p.reshape(wts[n].shape)

    return (loss, grad_x.reshape(B, S, D), *[grads[n] for n in WEIGHT_ORDER], *[delta[n] for n in WEIGHT_ORDER],
            *[new_m[n] for n in WEIGHT_ORDER], *[new_v[n] for n in WEIGHT_ORDER])
```

```python
import functools

import jax
import jax.numpy as jnp
import numpy as np
from jax import lax
from jax.experimental import pallas as pl
from jax.experimental.pallas import tpu as pltpu

f32 = jnp.float32
bf16 = jnp.bfloat16
MXU_DTYPE = jnp.bfloat16
WIRE_DTYPE = jnp.bfloat16
INV_PRECISION = lax.Precision.HIGH

D_MODEL = 1024
FOX_HEADS = 8
FOX_HEAD_DIM = 64
FOX_WIDTH = 512
GDN_HEADS = 4
GDN_HEAD_DIM = 128
GDN_WIDTH = 512
CONV_WIDTH = 4
GDN_CHUNK = 64
XATTN_HEADS = 4
XATTN_HEAD_DIM = 128
XATTN_WIDTH = 512
D_FF = 4096
IN_DIM = 3600
EPS = 1e-6
NEG_INF = -1e30
LANES = 128
ADAM_LR = 0.001
ADAM_B1 = 0.9
ADAM_B2 = 0.999
ADAM_EPS = 1e-08
ADAM_WD = 0.01
ADAM_STEP = 10
VMEM_LIMIT = 48 * 1024 * 1024

COL_FOX = 0
COL_GDN = 1536
COL_Z = 3072
COL_SMALL = 3584
IN_ALIGNED = 3840
IN_TILE = 768
SM_F = 0
SM_B = 8
SM_A = 12


def _cparams(*sem):
    return pltpu.CompilerParams(dimension_semantics=sem, vmem_limit_bytes=VMEM_LIMIT)


def _mx(v):
    return v.astype(MXU_DTYPE)


def _dot(a, b, dims, precision=None):
    return lax.dot_general(a, b, (dims, ((), ())), preferred_element_type=f32, precision=precision)


def _dotm(a, b, dims):
    return _dot(_mx(a), _mx(b), dims)


NN = ((1,), (0,))
NT = ((1,), (1,))
TN = ((0,), (0,))


def matmul(a, b, *, name, ta=False, tb=False, b_stacked=False, out_stacked=False, residual=None, relu2_out=False,
           relu2_bwd_aux=None, out_dtype=f32, tm=1024, tn=1024, tk=1024):
    M, K = (a.shape[1], a.shape[0]) if ta else a.shape
    if b_stacked:
        b_cols = b.shape[2]
        N, tk = (b.shape[1], min(tk, b_cols)) if tb else (N_CHIPS * b_cols, tk)
        tn = tn if tb else min(tn, b_cols)
        assert K == (N_CHIPS * b_cols if tb else b.shape[1]), (name, a.shape, b.shape)
    else:
        N = b.shape[0] if tb else b.shape[1]
    if out_stacked:
        tn = min(tn, N // N_CHIPS)
    tm, tn, tk = min(tm, M), min(tn, N), min(tk, K)
    assert M % tm == 0 and N % tn == 0 and K % tk == 0, (name, M, N, K)
    nk = K // tk
    has_res = residual is not None
    has_aux = relu2_bwd_aux is not None

    def body(*refs):
        a_ref, b_ref = refs[0], refs[1]
        pos = 2
        res_ref = aux_ref = None
        if has_res:
            res_ref = refs[pos]
            pos += 1
        if has_aux:
            aux_ref = refs[pos]
            pos += 1
        o_ref = refs[pos]
        k = pl.program_id(2)
        dims = ((0,) if ta else (1,), (1,) if tb else (0,))
        part = _dot(_mx(a_ref[...]), _mx(b_ref[...]), dims)

        def finish(r):
            if has_res:
                r = r + res_ref[...]
            if has_aux:
                r = r * (2.0 * jnp.sqrt(aux_ref[...].astype(f32)))
            if relu2_out:
                o_ref[...] = jnp.square(jnp.maximum(r, 0.0)).astype(o_ref.dtype)
            else:
                o_ref[...] = r.astype(o_ref.dtype)

        if nk == 1:
            finish(part)
            return
        acc_ref = refs[pos + 1]

        @pl.when(k == 0)
        def _():
            acc_ref[...] = part

        @pl.when((k > 0) & (k < nk - 1))
        def _():
            acc_ref[...] += part

        @pl.when(k == nk - 1)
        def _():
            finish(acc_ref[...] + part)

    a_spec = pl.BlockSpec((tk, tm), lambda i, j, k: (k, i)) if ta else pl.BlockSpec((tm, tk), lambda i, j, k: (i, k))
    if b_stacked and tb:
        per = b_cols // tk
        b_spec = pl.BlockSpec((None, tn, tk), lambda i, j, k: (k // per, j, k % per))
    elif b_stacked:
        per = b_cols // tn
        b_spec = pl.BlockSpec((None, tk, tn), lambda i, j, k: (j // per, k, j % per))
    else:
        b_spec = pl.BlockSpec((tn, tk), lambda i, j, k: (j, k)) if tb else pl.BlockSpec((tk, tn), lambda i, j, k: (k, j))
    if out_stacked:
        assert not (has_res or has_aux or relu2_out), name
        per_o = N // N_CHIPS // tn
        o_spec = pl.BlockSpec((None, tm, tn), lambda i, j, k: (j // per_o, i, j % per_o))
        out_full = (N_CHIPS, M, N // N_CHIPS)
    else:
        o_spec = pl.BlockSpec((tm, tn), lambda i, j, k: (i, j))
        out_full = (M, N)
    in_specs, args = [a_spec, b_spec], [a, b]
    if has_res:
        in_specs.append(o_spec)
        args.append(residual)
    if has_aux:
        in_specs.append(o_spec)
        args.append(relu2_bwd_aux)
    out_shape = [jax.ShapeDtypeStruct(out_full, out_dtype)]
    out_specs = [o_spec]
    res = pl.pallas_call(
        body, name=name, grid=(M // tm, N // tn, nk), in_specs=in_specs, out_specs=out_specs, out_shape=out_shape,
        scratch_shapes=[pltpu.VMEM((tm, tn), f32)] if nk > 1 else [],
        compiler_params=_cparams("parallel", "parallel", "arbitrary"),
    )(*args)
    return res[0]


def matmul_rows(a, b, extras, *, name, mode, tb=False, b_stacked=False, tm=1024, tk=1024):
    M, K = a.shape
    N = D_MODEL
    if b_stacked:
        assert tb, name
        tk = min(tk, b.shape[2])
        per = b.shape[2] // tk
        b_spec = pl.BlockSpec((None, N, tk), lambda i, k: (k // per, 0, k % per))
    elif tb:
        tk = min(tk, K)
        b_spec = pl.BlockSpec((N, tk), lambda i, k: (0, k))
    else:
        tk = min(tk, K)
        b_spec = pl.BlockSpec((tk, N), lambda i, k: (k, 0))
    tm = min(tm, M)
    assert M % tm == 0 and K % tk == 0, (name, M, K)
    nk = K // tk
    extras = [e for e in extras if e is not None]
    n_ex = len(extras)

    def body(*refs):
        a_ref, b_ref = refs[0], refs[1]
        ex = refs[2:2 + n_ex]
        o_ref = refs[2 + n_ex]
        n_out = 3 if mode == "loss" else 2
        s_ref = refs[1 + n_ex + n_out]
        i, k = pl.program_id(0), pl.program_id(1)
        part = _dot(_mx(a_ref[...]), _mx(b_ref[...]), ((1,), (1,) if tb else (0,)))

        def finish(y):
            @pl.when(i == 0)
            def _():
                s_ref[...] = jnp.zeros_like(s_ref)

            if mode == "rms_bwd":
                xv, gv = ex[0][...], ex[1][...]
                rstd = lax.rsqrt(jnp.mean(xv * xv, axis=-1, keepdims=True) + EPS)
                xhat = xv * rstd
                gd = y * gv
                dx = rstd * (gd - xhat * jnp.mean(gd * xhat, axis=-1, keepdims=True))
                o_ref[...] = dx + ex[2][...] if n_ex == 3 else dx
                s_ref[...] += jnp.sum(y * xhat, axis=0, keepdims=True)
            else:
                e = y + ex[0][...] - ex[1][...]
                o_ref[...] = e * (1.0 / N)
                refs[3 + n_ex][...] = (e * (1.0 / N)).astype(MXU_DTYPE)
                tot = 0.5 * jnp.sum(jnp.mean(e * e, axis=-1, keepdims=True), axis=0, keepdims=True)
                s_ref[...] += jnp.broadcast_to(tot, s_ref.shape)

        if nk == 1:
            finish(part)
            return
        acc_ref = refs[2 + n_ex + n_out]

        @pl.when(k == 0)
        def _():
            acc_ref[...] = part

        @pl.when((k > 0) & (k < nk - 1))
        def _():
            acc_ref[...] += part

        @pl.when(k == nk - 1)
        def _():
            finish(acc_ref[...] + part)

    row = pl.BlockSpec((tm, N), lambda i, k: (i, 0))
    vec = pl.BlockSpec((1, N), lambda i, k: (0, 0))
    if mode == "rms_bwd":
        ex_specs = [row, vec] + ([row] if n_ex == 3 else [])
        s_shape, s_spec = jax.ShapeDtypeStruct((1, N), f32), vec
    else:
        ex_specs = [row, row]
        s_shape, s_spec = jax.ShapeDtypeStruct((1, LANES), f32), pl.BlockSpec((1, LANES), lambda i, k: (0, 0))
    return pl.pallas_call(
        body, name=name, grid=(M // tm, nk),
        in_specs=[pl.BlockSpec((tm, tk), lambda i, k: (i, k)), b_spec] + ex_specs,
        out_specs=[row] * (2 if mode == "loss" else 1) + [s_spec],
        out_shape=[jax.ShapeDtypeStruct((M, N), f32)] + ([jax.ShapeDtypeStruct((M, N), MXU_DTYPE)] if mode == "loss" else [])
        + [s_shape],
        scratch_shapes=[pltpu.VMEM((tm, N), f32)] if nk > 1 else [],
        compiler_params=_cparams("arbitrary", "arbitrary"),
    )(a, b, *extras)


def rms_fwd(x, g, *, name, tr=512):
    R, D = x.shape
    tr = min(tr, R)

    def body(x_ref, g_ref, o_ref):
        xv = x_ref[...]
        y = xv * lax.rsqrt(jnp.mean(xv * xv, axis=-1, keepdims=True) + EPS)
        o_ref[...] = (y * g_ref[...]).astype(o_ref.dtype)

    return pl.pallas_call(
        body, name=name, grid=(R // tr,),
        in_specs=[pl.BlockSpec((tr, D), lambda i: (i, 0)), pl.BlockSpec((1, D), lambda i: (0, 0))],
        out_specs=pl.BlockSpec((tr, D), lambda i: (i, 0)),
        out_shape=jax.ShapeDtypeStruct((R, D), MXU_DTYPE),
        compiler_params=_cparams("parallel"),
    )(x, g)


def rms_bwd(x, g, dh, residual, *, name, tr=512):
    R, D = x.shape
    tr = min(tr, R)
    has_res = residual is not None

    def body(*refs):
        if has_res:
            x_ref, g_ref, dh_ref, res_ref, dx_ref, dg_ref = refs
        else:
            x_ref, g_ref, dh_ref, dx_ref, dg_ref = refs
        xv = x_ref[...]
        rstd = lax.rsqrt(jnp.mean(xv * xv, axis=-1, keepdims=True) + EPS)
        xhat = xv * rstd
        dh = dh_ref[...].astype(f32)
        gd = dh * g_ref[...]
        dx = rstd * (gd - xhat * jnp.mean(gd * xhat, axis=-1, keepdims=True))
        if has_res:
            dx = dx + res_ref[...]
        dx_ref[...] = dx

        @pl.when(pl.program_id(0) == 0)
        def _():
            dg_ref[...] = jnp.zeros_like(dg_ref)

        dg_ref[...] += jnp.sum(dh * xhat, axis=0, keepdims=True)

    row = pl.BlockSpec((tr, D), lambda i: (i, 0))
    vec = pl.BlockSpec((1, D), lambda i: (0, 0))
    in_specs = [row, vec, row] + ([row] if has_res else [])
    args = [x, g, dh] + ([residual] if has_res else [])
    return pl.pallas_call(
        body, name=name, grid=(R // tr,), in_specs=in_specs, out_specs=[row, vec],
        out_shape=[jax.ShapeDtypeStruct((R, D), f32), jax.ShapeDtypeStruct((1, D), f32)],
        compiler_params=_cparams("arbitrary"),
    )(*args)


def loss_head(y, target, *, tr=512):
    R, D = y.shape
    tr = min(tr, R)

    def body(y_ref, t_ref, dy_ref, loss_ref):
        e = y_ref[...] - t_ref[...]
        dy_ref[...] = e * (1.0 / D)

        @pl.when(pl.program_id(0) == 0)
        def _():
            loss_ref[...] = jnp.zeros_like(loss_ref)

        part = 0.5 * jnp.sum(jnp.mean(e * e, axis=-1, keepdims=True), axis=0, keepdims=True)
        loss_ref[...] += jnp.broadcast_to(part, loss_ref.shape)

    row = pl.BlockSpec((tr, D), lambda i: (i, 0))
    return pl.pallas_call(
        body, name="loss_head", grid=(R // tr,), in_specs=[row, row],
        out_specs=[row, pl.BlockSpec((1, LANES), lambda i: (0, 0))],
        out_shape=[jax.ShapeDtypeStruct((R, D), f32), jax.ShapeDtypeStruct((1, LANES), f32)],
        compiler_params=_cparams("arbitrary"),
    )(y, target)


def _head_rms(v, g):
    r = lax.rsqrt(jnp.mean(v * v, axis=-1, keepdims=True) + EPS)
    return v * r * g, r


def _head_rms_bwd(v, r, g, dn):
    vhat = v * r
    gd = dn * g
    dv = r * (gd - vhat * jnp.mean(gd * vhat, axis=-1, keepdims=True))
    return dv, jnp.sum(dn * vhat, axis=0, keepdims=True)


def _softmax_rows(s):
    m = jnp.max(s, axis=-1, keepdims=True)
    e = jnp.exp(s - m)
    return e / jnp.sum(e, axis=-1, keepdims=True)


def xattn_fwd(cq, ckv, gq, gk, *, B, tq=512):
    T = cq.shape[0]
    S = T // B
    M = ckv.shape[0] // B
    tq = min(tq, S)
    nq = S // tq
    hd, W = XATTN_HEAD_DIM, XATTN_WIDTH
    scale = hd ** -0.5

    def body(q_ref, k_ref, v_ref, gq_ref, gk_ref, o_ref):
        for h in range(XATTN_HEADS):
            sl = slice(h * hd, (h + 1) * hd)
            qn, _ = _head_rms(q_ref[:, sl], gq_ref[...])
            kn, _ = _head_rms(k_ref[:, sl], gk_ref[...])
            p = _softmax_rows(_dot(_mx(qn), _mx(kn), NT) * scale)
            o_ref[:, sl] = _dot(_mx(p), _mx(v_ref[:, sl]), NN).astype(o_ref.dtype)

    vec = pl.BlockSpec((1, hd), lambda b, i: (0, 0))
    qspec = pl.BlockSpec((tq, W), lambda b, i: (b * nq + i, 0))
    return pl.pallas_call(
        body, name="xattn_fwd", grid=(B, nq),
        in_specs=[qspec, pl.BlockSpec((M, W), lambda b, i: (b, 0)), pl.BlockSpec((M, W), lambda b, i: (b, 1)), vec, vec],
        out_specs=qspec, out_shape=jax.ShapeDtypeStruct((T, W), MXU_DTYPE),
        compiler_params=_cparams("parallel", "parallel"),
    )(cq, ckv, ckv, gq, gk)


def xattn_bwd(cq, ckv, gq, gk, dco, *, B, tq=512):
    T = cq.shape[0]
    S = T // B
    M = ckv.shape[0] // B
    tq = min(tq, S)
    nq = S // tq
    hd, W = XATTN_HEAD_DIM, XATTN_WIDTH
    scale = hd ** -0.5

    def body(q_ref, k_ref, v_ref, gq_ref, gk_ref, do_ref, dq_ref, dkv_ref, dgq_ref, dgk_ref, dkn_acc, dv_acc):
        b, i = pl.program_id(0), pl.program_id(1)

        @pl.when((b == 0) & (i == 0))
        def _():
            dgq_ref[...] = jnp.zeros_like(dgq_ref)
            dgk_ref[...] = jnp.zeros_like(dgk_ref)

        @pl.when(i == 0)
        def _():
            dkn_acc[...] = jnp.zeros_like(dkn_acc)
            dv_acc[...] = jnp.zeros_like(dv_acc)

        gqv, gkv = gq_ref[...], gk_ref[...]
        for h in range(XATTN_HEADS):
            sl = slice(h * hd, (h + 1) * hd)
            q, k, v = q_ref[:, sl], k_ref[:, sl], v_ref[:, sl]
            qn, rq = _head_rms(q, gqv)
            kn, _ = _head_rms(k, gkv)
            p = _softmax_rows(_dot(_mx(qn), _mx(kn), NT) * scale)
            do = do_ref[:, sl]
            dv_acc[:, sl] += _dot(_mx(p), _mx(do), TN)
            dp = _dot(_mx(do), _mx(v), NT)
            ds = p * (dp - jnp.sum(dp * p, axis=-1, keepdims=True)) * scale
            dqn = _dot(_mx(ds), _mx(kn), NN)
            dkn_acc[:, sl] += _dot(_mx(ds), _mx(qn), TN)
            dq, dgq = _head_rms_bwd(q, rq, gqv, dqn)
            dq_ref[:, sl] = dq.astype(dq_ref.dtype)
            dgq_ref[...] += dgq

        @pl.when(i == nq - 1)
        def _():
            for h in range(XATTN_HEADS):
                sl = slice(h * hd, (h + 1) * hd)
                k = k_ref[:, sl]
                rk = lax.rsqrt(jnp.mean(k * k, axis=-1, keepdims=True) + EPS)
                dk, dgk = _head_rms_bwd(k, rk, gkv, dkn_acc[:, sl])
                dkv_ref[:, sl] = dk.astype(dkv_ref.dtype)
                dkv_ref[:, slice(W + h * hd, W + (h + 1) * hd)] = dv_acc[:, sl].astype(dkv_ref.dtype)
                dgk_ref[...] += dgk

    vec = pl.BlockSpec((1, hd), lambda b, i: (0, 0))
    qspec = pl.BlockSpec((tq, W), lambda b, i: (b * nq + i, 0))
    return pl.pallas_call(
        body, name="xattn_bwd", grid=(B, nq),
        in_specs=[qspec, pl.BlockSpec((M, W), lambda b, i: (b, 0)), pl.BlockSpec((M, W), lambda b, i: (b, 1)), vec, vec, qspec],
        out_specs=[qspec, pl.BlockSpec((M, 2 * W), lambda b, i: (b, 0)), vec, vec],
        out_shape=[jax.ShapeDtypeStruct((T, W), MXU_DTYPE), jax.ShapeDtypeStruct((B * M, 2 * W), MXU_DTYPE),
                   jax.ShapeDtypeStruct((1, hd), f32), jax.ShapeDtypeStruct((1, hd), f32)],
        scratch_shapes=[pltpu.VMEM((M, W), f32), pltpu.VMEM((M, W), f32)],
        compiler_params=_cparams("arbitrary", "arbitrary"),
    )(cq, ckv, ckv, gq, gk, dco)


FOX_PAIRS = FOX_HEADS // 2


def _fox_scores(qn, kn, ccol, crow, q0, tq, S, scale):
    s = _dot(_mx(qn), _mx(kn), NT) * scale + ccol - crow
    qpos = q0 + lax.broadcasted_iota(jnp.int32, (tq, S), 0)
    kpos = lax.broadcasted_iota(jnp.int32, (tq, S), 1)
    return jnp.where(kpos <= qpos, s, NEG_INF)


def fox_fwd(P, ccol, crow, gq, gk, go, *, B, tq=256):
    T = P.shape[0]
    S = T // B
    tq = min(tq, S)
    nq = S // tq
    hd = FOX_HEAD_DIM
    scale = hd ** -0.5

    def body(q_ref, k_ref, v_ref, ccol_ref, crow_ref, gq_ref, gk_ref, go_ref, o_ref, oa_ref):
        q0 = pl.program_id(2) * tq
        for e in range(2):
            sl = slice(e * hd, (e + 1) * hd)
            qn, _ = _head_rms(q_ref[:, sl], gq_ref[:, sl])
            kn, _ = _head_rms(k_ref[:, sl], gk_ref[:, sl])
            p = _softmax_rows(_fox_scores(qn, kn, ccol_ref[0, e], crow_ref[0, e], q0, tq, S, scale))
            o = _dot(_mx(p), _mx(v_ref[:, sl]), NN)
            o_ref[:, sl] = o
            oa_ref[:, sl] = _head_rms(o, go_ref[:, sl])[0].astype(oa_ref.dtype)

    W = 2 * hd
    vec = pl.BlockSpec((1, W), lambda b, h, i: (0, 0))
    ospec = pl.BlockSpec((tq, W), lambda b, h, i: (b * nq + i, h))
    return pl.pallas_call(
        body, name="fox_fwd", grid=(B, FOX_PAIRS, nq),
        in_specs=[pl.BlockSpec((tq, W), lambda b, h, i: (b * nq + i, h)),
                  pl.BlockSpec((S, W), lambda b, h, i: (b, FOX_PAIRS + h)),
                  pl.BlockSpec((S, W), lambda b, h, i: (b, 2 * FOX_PAIRS + h)),
                  pl.BlockSpec((1, 2, tq, 1), lambda b, h, i: (b, h, i, 0)),
                  pl.BlockSpec((1, 2, 1, S), lambda b, h, i: (b, h, 0, 0)), vec, vec, vec],
        out_specs=[ospec, ospec],
        out_shape=[jax.ShapeDtypeStruct((T, FOX_WIDTH), f32), jax.ShapeDtypeStruct((T, FOX_WIDTH), MXU_DTYPE)],
        compiler_params=_cparams("parallel", "parallel", "parallel"),
    )(P, P, P, ccol, crow, gq, gk, go)


def fox_bwd(P, ccol, crow, gq, gk, go, o_raw, d_oab, *, B, tq=256):
    T = P.shape[0]
    S = T // B
    tq = min(tq, S)
    nq = S // tq
    hd = FOX_HEAD_DIM
    scale = hd ** -0.5

    def body(q_ref, k_ref, v_ref, ccol_ref, crow_ref, gq_ref, gk_ref, go_ref, o_ref, doa_ref,
             dq_ref, dk_ref, dv_ref, dccol_ref, dcrow_ref, dgq_ref, dgk_ref, dgo_ref, dkn_acc, dv_acc, dcrow_acc):
        b, h, i = pl.program_id(0), pl.program_id(1), pl.program_id(2)
        q0 = i * tq

        @pl.when((b == 0) & (h == 0) & (i == 0))
        def _():
            dgq_ref[...] = jnp.zeros_like(dgq_ref)
            dgk_ref[...] = jnp.zeros_like(dgk_ref)
            dgo_ref[...] = jnp.zeros_like(dgo_ref)

        @pl.when(i == 0)
        def _():
            dkn_acc[...] = jnp.zeros_like(dkn_acc)
            dv_acc[...] = jnp.zeros_like(dv_acc)
            dcrow_acc[...] = jnp.zeros_like(dcrow_acc)

        for e in range(2):
            sl = slice(e * hd, (e + 1) * hd)
            q, k, v = q_ref[:, sl], k_ref[:, sl], v_ref[:, sl]
            gqv, gkv, gov = gq_ref[:, sl], gk_ref[:, sl], go_ref[:, sl]
            qn, rq = _head_rms(q, gqv)
            kn, rk = _head_rms(k, gkv)
            p = _softmax_rows(_fox_scores(qn, kn, ccol_ref[0, e], crow_ref[0, e], q0, tq, S, scale))
            o = o_ref[:, sl]
            ro = lax.rsqrt(jnp.mean(o * o, axis=-1, keepdims=True) + EPS)
            do, dgo = _head_rms_bwd(o, ro, gov, doa_ref[:, sl])
            dgo_ref[:, sl] += dgo
            dv_acc[e] += _dot(_mx(p), _mx(do), TN)
            dp = _dot(_mx(do), _mx(v), NT)
            ds = p * (dp - jnp.sum(do * o, axis=-1, keepdims=True))
            dccol_ref[0, e] = jnp.sum(ds, axis=1, keepdims=True)
            dcrow_acc[e] -= jnp.sum(ds, axis=0, keepdims=True)
            dqn = _dot(_mx(ds), _mx(kn), NN) * scale
            dkn_acc[e] += _dot(_mx(ds), _mx(qn), TN) * scale
            dq, dgq = _head_rms_bwd(q, rq, gqv, dqn)
            dq_ref[:, sl] = dq.astype(dq_ref.dtype)
            dgq_ref[:, sl] += dgq

        @pl.when(i == nq - 1)
        def _():
            for e in range(2):
                sl = slice(e * hd, (e + 1) * hd)
                k = k_ref[:, sl]
                gkv = gk_ref[:, sl]
                rk = lax.rsqrt(jnp.mean(k * k, axis=-1, keepdims=True) + EPS)
                dk, dgk = _head_rms_bwd(k, rk, gkv, dkn_acc[e])
                dk_ref[:, sl] = dk.astype(dk_ref.dtype)
                dv_ref[:, sl] = dv_acc[e].astype(dv_ref.dtype)
                dgk_ref[:, sl] += dgk
                dcrow_ref[0, e] = dcrow_acc[e]

    W = 2 * hd
    vec = pl.BlockSpec((1, W), lambda b, h, i: (0, 0))
    qspec = pl.BlockSpec((tq, W), lambda b, h, i: (b * nq + i, h))
    kvout = pl.BlockSpec((S, W), lambda b, h, i: (b, h))
    colspec = pl.BlockSpec((1, 2, tq, 1), lambda b, h, i: (b, h, i, 0))
    rowspec = pl.BlockSpec((1, 2, 1, S), lambda b, h, i: (b, h, 0, 0))
    return pl.pallas_call(
        body, name="fox_bwd", grid=(B, FOX_PAIRS, nq),
        in_specs=[qspec,
                  pl.BlockSpec((S, W), lambda b, h, i: (b, FOX_PAIRS + h)),
                  pl.BlockSpec((S, W), lambda b, h, i: (b, 2 * FOX_PAIRS + h)),
                  colspec, rowspec, vec, vec, vec, qspec, qspec],
        out_specs=[qspec, kvout, kvout, colspec, rowspec, vec, vec, vec],
        out_shape=[jax.ShapeDtypeStruct((T, FOX_WIDTH), MXU_DTYPE), jax.ShapeDtypeStruct((T, FOX_WIDTH), MXU_DTYPE),
                   jax.ShapeDtypeStruct((T, FOX_WIDTH), MXU_DTYPE),
                   jax.ShapeDtypeStruct((B, FOX_HEADS, S, 1), f32), jax.ShapeDtypeStruct((B, FOX_HEADS, 1, S), f32),
                   jax.ShapeDtypeStruct((1, W), f32), jax.ShapeDtypeStruct((1, W), f32), jax.ShapeDtypeStruct((1, W), f32)],
        scratch_shapes=[pltpu.VMEM((2, S, hd), f32), pltpu.VMEM((2, S, hd), f32), pltpu.VMEM((2, 1, S), f32)],
        compiler_params=_cparams("arbitrary", "arbitrary", "arbitrary"),
    )(P, P, P, ccol, crow, gq, gk, go, o_raw, d_oab)


FOX_TQ = 512
FOX_TK = FOX_TQ
GROUP_PRECISION = lax.Precision.HIGH


def _head_mean(v):
    n = v.shape[1]
    r = lax.broadcasted_iota(jnp.int32, (n, n), 0) // FOX_HEAD_DIM
    c = lax.broadcasted_iota(jnp.int32, (n, n), 1) // FOX_HEAD_DIM
    ones = (r == c).astype(bf16)
    hi = v.astype(bf16)
    lo = (v - hi.astype(f32)).astype(bf16)
    return (_dot(hi, ones, NN) + _dot(lo, ones, NN)) * (1.0 / FOX_HEAD_DIM)


def fox_prep_fwd(P, gq, gk, *, tr=512):
    T = P.shape[0]
    tr = min(tr, T)
    scale = FOX_HEAD_DIM ** -0.5

    def body(q_ref, k_ref, v_ref, gq_ref, gk_ref, qn_ref, kn_ref, vb_ref):
        q, k = q_ref[...], k_ref[...]
        qn_ref[...] = (q * lax.rsqrt(_head_mean(q * q) + EPS) * (gq_ref[...] * scale)).astype(qn_ref.dtype)
        kn_ref[...] = (k * lax.rsqrt(_head_mean(k * k) + EPS) * gk_ref[...]).astype(kn_ref.dtype)
        vb_ref[...] = v_ref[...].astype(vb_ref.dtype)

    W = FOX_WIDTH
    col = lambda j: pl.BlockSpec((tr, W), lambda i: (i, j))
    vec = pl.BlockSpec((1, W), lambda i: (0, 0))
    out = jax.ShapeDtypeStruct((T, W), MXU_DTYPE)
    return pl.pallas_call(
        body, name="fox_prep_fwd", grid=(T // tr,), in_specs=[col(0), col(1), col(2), vec, vec],
        out_specs=[col(0)] * 3, out_shape=[out] * 3, compiler_params=_cparams("parallel"),
    )(P, P, P, gq, gk)


def fox_prep_bwd(P, gq, gk, dqn, dkn, *, tr=512):
    T = P.shape[0]
    tr = min(tr, T)
    scale = FOX_HEAD_DIM ** -0.5

    def body(q_ref, k_ref, gq_ref, gk_ref, dqn_ref, dkn_ref, dq_ref, dk_ref, dgq_ref, dgk_ref):
        @pl.when(pl.program_id(0) == 0)
        def _():
            dgq_ref[...] = jnp.zeros_like(dgq_ref)
            dgk_ref[...] = jnp.zeros_like(dgk_ref)

        def one(x, g, dn, dx_ref, dg_ref):
            r = lax.rsqrt(_head_mean(x * x) + EPS)
            xhat = x * r
            gd = dn * g
            dx_ref[...] = (r * (gd - xhat * _head_mean(gd * xhat))).astype(dx_ref.dtype)
            return jnp.sum(dn * xhat, axis=0, keepdims=True)

        dgq_ref[...] += scale * one(q_ref[...], gq_ref[...] * scale, dqn_ref[...], dq_ref, dgq_ref)
        dgk_ref[...] += one(k_ref[...], gk_ref[...], dkn_ref[...], dk_ref, dgk_ref)

    W = FOX_WIDTH
    col = lambda j: pl.BlockSpec((tr, W), lambda i: (i, j))
    vec = pl.BlockSpec((1, W), lambda i: (0, 0))
    return pl.pallas_call(
        body, name="fox_prep_bwd", grid=(T // tr,), in_specs=[col(0), col(1), vec, vec, col(0), col(0)],
        out_specs=[col(0), col(0), vec, vec],
        out_shape=[jax.ShapeDtypeStruct((T, W), MXU_DTYPE), jax.ShapeDtypeStruct((T, W), MXU_DTYPE),
                   jax.ShapeDtypeStruct((1, W), f32), jax.ShapeDtypeStruct((1, W), f32)],
        compiler_params=_cparams("arbitrary"),
    )(P, P, gq, gk, dqn, dkn)


def _fox_tile_scores(q, k_ref, ccol_ref, cq, e, j, sl, mask_off):
    tq, tk = FOX_TQ, FOX_TK
    rows = pl.ds(pl.multiple_of(j * tk, tk), tk)
    k = k_ref[rows, sl]
    s = _dot(k, q, NT) + cq - ccol_ref[0, e, rows, :]
    if mask_off is not None:
        key = lax.broadcasted_iota(jnp.int32, (tk, tq), 0) + mask_off
        query = lax.broadcasted_iota(jnp.int32, (tk, tq), 1)
        s = jnp.where(key <= query, s, NEG_INF)
    return s, k, rows


def _fox_sweep(i, update, carry):
    nd = FOX_TQ // FOX_TK
    carry = lax.fori_loop(0, i * nd, lambda j, cr: update(cr, j, None), carry)
    for d in range(nd):
        carry = update(carry, i * nd + d, d * FOX_TK)
    return carry


def fox_core_fwd(qn, kn, vb, ccol, crow, go, *, B):
    T = qn.shape[0]
    S = T // B
    tq = FOX_TQ
    nq = S // tq
    hd = FOX_HEAD_DIM

    def body(q_ref, k_ref, v_ref, ccol_ref, crow_ref, go_ref, o_ref, oa_ref, lse_ref):
        i = pl.program_id(2)
        for e in range(2):
            sl = slice(e * hd, (e + 1) * hd)
            q = q_ref[:, sl]
            cq = crow_ref[0, e, i]

            def update(carry, j, mask_off):
                m, l, acc = carry
                s, _, rows = _fox_tile_scores(q, k_ref, ccol_ref, cq, e, j, sl, mask_off)
                m2 = jnp.maximum(m, jnp.max(s, axis=0, keepdims=True))
                a = jnp.exp(m - m2)
                p = jnp.exp(s - m2)
                return m2, a * l + jnp.sum(p, axis=0, keepdims=True), a * acc + _dot(v_ref[rows, sl], _mx(p), TN)

            carry = (jnp.full((1, tq), NEG_INF, f32), jnp.zeros((1, tq), f32), jnp.zeros((hd, tq), f32))
            m, l, acc = _fox_sweep(i, update, carry)
            o = (acc / l).T
            o_ref[:, sl] = o
            oa_ref[:, sl] = _head_rms(o, go_ref[:, sl])[0].astype(oa_ref.dtype)
            lse_ref[0, e, 0] = m + jnp.log(l)

    W = 2 * hd
    qspec = pl.BlockSpec((tq, W), lambda b, h, i: (b * nq + i, h))
    kspec = pl.BlockSpec((S, W), lambda b, h, i: (b, h))
    return pl.pallas_call(
        body, name="fox_core_fwd", grid=(B, FOX_PAIRS, nq),
        in_specs=[qspec, kspec, kspec, pl.BlockSpec((1, 2, S, 1), lambda b, h, i: (b, h, 0, 0)),
                  pl.BlockSpec((1, 2, nq, 1, tq), lambda b, h, i: (b, h, 0, 0, 0)),
                  pl.BlockSpec((1, W), lambda b, h, i: (0, 0))],
        out_specs=[qspec, qspec, pl.BlockSpec((1, 2, 1, 1, tq), lambda b, h, i: (b, h, i, 0, 0))],
        out_shape=[jax.ShapeDtypeStruct((T, FOX_WIDTH), f32), jax.ShapeDtypeStruct((T, FOX_WIDTH), MXU_DTYPE),
                   jax.ShapeDtypeStruct((B, FOX_HEADS, nq, 1, tq), f32)],
        compiler_params=_cparams("parallel", "parallel", "parallel"),
    )(qn, kn, vb, ccol, crow, go)


def fox_core_bwd(qn, kn, vb, ccol, crow, go, o_raw, lse, d_oab, *, B):
    T = qn.shape[0]
    S = T // B
    tq = FOX_TQ
    nq = S // tq
    hd = FOX_HEAD_DIM

    def body(q_ref, k_ref, v_ref, ccol_ref, crow_ref, go_ref, o_ref, lse_ref, doa_ref,
             dq_ref, dk_ref, dv_ref, dckey_ref, dcrow_ref, dgo_ref, dk_acc, dv_acc, dck_acc):
        b, h, i = pl.program_id(0), pl.program_id(1), pl.program_id(2)

        @pl.when((b == 0) & (h == 0) & (i == 0))
        def _():
            dgo_ref[...] = jnp.zeros_like(dgo_ref)

        @pl.when(i == 0)
        def _():
            dk_acc[...] = jnp.zeros_like(dk_acc)
            dv_acc[...] = jnp.zeros_like(dv_acc)
            dck_acc[...] = jnp.zeros_like(dck_acc)

        for e in range(2):
            sl = slice(e * hd, (e + 1) * hd)
            q = q_ref[:, sl]
            cq = crow_ref[0, e, i]
            lse_e = lse_ref[0, e, 0]
            o = o_ref[:, sl]
            ro = lax.rsqrt(jnp.mean(o * o, axis=-1, keepdims=True) + EPS)
            do, dgo = _head_rms_bwd(o, ro, go_ref[:, sl], doa_ref[:, sl])
            dgo_ref[:, sl] += dgo
            delta = jnp.sum((do * o).T, axis=0, keepdims=True)
            do_b = _mx(do)

            def update(carry, j, mask_off):
                dq, dcq = carry
                s, k, rows = _fox_tile_scores(q, k_ref, ccol_ref, cq, e, j, sl, mask_off)
                p = jnp.exp(s - lse_e)
                dv_acc[e, rows, :] += _dot(_mx(p), do_b, NN)
                ds = p * (_dot(v_ref[rows, sl], do_b, NT) - delta)
                dck_acc[e, rows, :] -= jnp.sum(ds, axis=1, keepdims=True)
                ds_b = _mx(ds)
                dk_acc[e, rows, :] += _dot(ds_b, q, NN)
                return dq + _dot(ds_b, k, TN), dcq + jnp.sum(ds, axis=0, keepdims=True)

            dq, dcq = _fox_sweep(i, update, (jnp.zeros((tq, hd), f32), jnp.zeros((1, tq), f32)))
            dq_ref[:, sl] = dq
            dcrow_ref[0, e, 0] = dcq

        @pl.when(i == nq - 1)
        def _():
            for e in range(2):
                sl = slice(e * hd, (e + 1) * hd)
                dk_ref[:, sl] = dk_acc[e]
                dv_ref[:, sl] = dv_acc[e].astype(dv_ref.dtype)
                dckey_ref[0, e] = jnp.transpose(jnp.broadcast_to(dck_acc[e], (S, LANES)))[0:1, :]

    W = 2 * hd
    qspec = pl.BlockSpec((tq, W), lambda b, h, i: (b * nq + i, h))
    kspec = pl.BlockSpec((S, W), lambda b, h, i: (b, h))
    colspec = pl.BlockSpec((1, 2, S, 1), lambda b, h, i: (b, h, 0, 0))
    rowspec = pl.BlockSpec((1, 2, nq, 1, tq), lambda b, h, i: (b, h, 0, 0, 0))
    tilespec = pl.BlockSpec((1, 2, 1, 1, tq), lambda b, h, i: (b, h, i, 0, 0))
    vec = pl.BlockSpec((1, W), lambda b, h, i: (0, 0))
    return pl.pallas_call(
        body, name="fox_core_bwd", grid=(B, FOX_PAIRS, nq),
        in_specs=[qspec, kspec, kspec, colspec, rowspec, vec, qspec, tilespec, qspec],
        out_specs=[qspec, kspec, kspec, pl.BlockSpec((1, 2, 1, S), lambda b, h, i: (b, h, 0, 0)), tilespec, vec],
        out_shape=[jax.ShapeDtypeStruct((T, FOX_WIDTH), f32), jax.ShapeDtypeStruct((T, FOX_WIDTH), f32),
                   jax.ShapeDtypeStruct((T, FOX_WIDTH), MXU_DTYPE),
                   jax.ShapeDtypeStruct((B, FOX_HEADS, 1, S), f32), jax.ShapeDtypeStruct((B, FOX_HEADS, nq, 1, tq), f32),
                   jax.ShapeDtypeStruct((1, W), f32)],
        scratch_shapes=[pltpu.VMEM((2, S, hd), f32), pltpu.VMEM((2, S, hd), f32), pltpu.VMEM((2, S, 1), f32)],
        compiler_params=_cparams("arbitrary", "arbitrary", "arbitrary"),
    )(qn, kn, vb, ccol, crow, go, o_raw, lse, d_oab)


def _lane_mask(lo, hi, shape):
    lane = lax.broadcasted_iota(jnp.int32, shape, 1)
    return (lane >= lo) & (lane < hi)


def _cumsum_rows(v, period, reverse=False):
    n = v.shape[0]
    pos = lax.broadcasted_iota(jnp.int32, v.shape, 0) % period
    sh = 1
    while sh < period:
        if reverse:
            v = v + jnp.where(pos + sh < period, pltpu.roll(v, n - sh, 0), 0.0)
        else:
            v = v + jnp.where(pos >= sh, pltpu.roll(v, sh, 0), 0.0)
        sh *= 2
    return v


def _gate_values(z, bias, alog):
    zb = z + bias
    ls = jax.nn.log_sigmoid(zb)
    beta = jax.nn.sigmoid(z)
    g = -jnp.exp(alog) * jax.nn.softplus(zb)
    return zb, ls, beta, g


def gates_fwd(P, bias, alog, *, B):
    T = P.shape[0]
    S = T // B

    def body(z_ref, bias_ref, alog_ref, o_ref):
        z = z_ref[...]
        _, ls, beta, g = _gate_values(z, bias_ref[...], alog_ref[...])
        c = _cumsum_rows(ls, S)
        gc = _cumsum_rows(g, GDN_CHUNK)
        o = jnp.where(_lane_mask(SM_F, SM_F + FOX_HEADS, z.shape), c, 0.0)
        o = jnp.where(_lane_mask(SM_B, SM_B + GDN_HEADS, z.shape), beta, o)
        o = jnp.where(_lane_mask(SM_A, SM_A + GDN_HEADS, z.shape), gc, o)
        o_ref[...] = o

    vec = pl.BlockSpec((1, LANES), lambda b: (0, 0))
    return pl.pallas_call(
        body, name="gates_fwd", grid=(B,),
        in_specs=[pl.BlockSpec((S, LANES), lambda b: (b, COL_SMALL // LANES)), vec, vec],
        out_specs=pl.BlockSpec((S, LANES), lambda b: (b, 0)),
        out_shape=jax.ShapeDtypeStruct((T, LANES), f32),
        compiler_params=_cparams("parallel"),
    )(P, bias, alog)


def gates_bwd(P, bias, alog, dgates, *, B):
    T = P.shape[0]
    S = T // B

    def body(z_ref, bias_ref, alog_ref, dg_ref, dz_ref, par_ref):
        z = z_ref[...]
        zb, ls, beta, g = _gate_values(z, bias_ref[...], alog_ref[...])
        d = dg_ref[...]
        dls = _cumsum_rows(d, S, reverse=True)
        dgr = _cumsum_rows(d, GDN_CHUNK, reverse=True)
        sig = jax.nn.sigmoid(zb)
        dz_f = dls * (1.0 - sig)
        dz_b = d * beta * (1.0 - beta)
        dz_a = dgr * (-jnp.exp(alog_ref[...])) * sig
        dz = jnp.where(_lane_mask(SM_F, SM_F + FOX_HEADS, z.shape), dz_f, 0.0)
        dz = jnp.where(_lane_mask(SM_B, SM_B + GDN_HEADS, z.shape), dz_b, dz)
        dz = jnp.where(_lane_mask(SM_A, SM_A + GDN_HEADS, z.shape), dz_a, dz)
        dz_ref[...] = dz.astype(dz_ref.dtype)

        @pl.when(pl.program_id(0) == 0)
        def _():
            par_ref[...] = jnp.zeros_like(par_ref)

        dalog = jnp.where(_lane_mask(SM_A, SM_A + GDN_HEADS, z.shape), dgr * g, 0.0)
        par_ref[0:1, :] += jnp.sum(dz, axis=0, keepdims=True)
        par_ref[1:2, :] += jnp.sum(dalog, axis=0, keepdims=True)

    vec = pl.BlockSpec((1, LANES), lambda b: (0, 0))
    return pl.pallas_call(
        body, name="gates_bwd", grid=(B,),
        in_specs=[pl.BlockSpec((S, LANES), lambda b: (b, COL_SMALL // LANES)), vec, vec,
                  pl.BlockSpec((S, LANES), lambda b: (b, 0))],
        out_specs=[pl.BlockSpec((S, LANES), lambda b: (b, 0)), pl.BlockSpec((8, LANES), lambda b: (0, 0))],
        out_shape=[jax.ShapeDtypeStruct((T, LANES), MXU_DTYPE), jax.ShapeDtypeStruct((8, LANES), f32)],
        compiler_params=_cparams("arbitrary"),
    )(P, bias, alog, dgates)


GDN_BLOCKS = 3 * GDN_HEADS


def _shift_rows(v, d, reverse=False):
    if d == 0:
        return v
    n = v.shape[0]
    row = lax.broadcasted_iota(jnp.int32, v.shape, 0)
    if reverse:
        return jnp.where(row + d < n, pltpu.roll(v, n - d, 0), 0.0)
    return jnp.where(row >= d, pltpu.roll(v, d, 0), 0.0)


def _conv_silu(x, w):
    pre = sum(w[j:j + 1, :] * _shift_rows(x, CONV_WIDTH - 1 - j) for j in range(CONV_WIDTH))
    return pre, pre * jax.nn.sigmoid(pre)


def gdn_prep_fwd(P, conv_w, *, B):
    T = P.shape[0]
    S = T // B

    def body(x_ref, w_ref, o_ref):
        _, y = _conv_silu(x_ref[...], w_ref[...])
        yn = y * lax.rsqrt(jnp.sum(y * y, axis=-1, keepdims=True) + EPS)
        o_ref[...] = jnp.where(pl.program_id(1) < 2 * GDN_HEADS, yn, y)

    return pl.pallas_call(
        body, name="gdn_prep_fwd", grid=(B, GDN_BLOCKS),
        in_specs=[pl.BlockSpec((S, LANES), lambda b, j: (b, COL_GDN // LANES + j)),
                  pl.BlockSpec((CONV_WIDTH, LANES), lambda b, j: (0, j))],
        out_specs=pl.BlockSpec((S, LANES), lambda b, j: (b, j)),
        out_shape=jax.ShapeDtypeStruct((T, 3 * GDN_WIDTH), f32),
        compiler_params=_cparams("parallel", "parallel"),
    )(P, conv_w)


def gdn_prep_bwd(P, conv_w, dGq, dGk, dGv, *, B):
    T = P.shape[0]
    S = T // B
    H = GDN_HEADS

    def body(x_ref, w_ref, dq_ref, dk_ref, dv_ref, dx_ref, dw_ref):
        x, w = x_ref[...], w_ref[...]
        pre, y = _conv_silu(x, w)
        jb = pl.program_id(0)
        dn = jnp.where(jb < H, dq_ref[...], jnp.where(jb < 2 * H, dk_ref[...], dv_ref[...]))
        r = lax.rsqrt(jnp.sum(y * y, axis=-1, keepdims=True) + EPS)
        n = y * r
        dy_norm = r * (dn - n * jnp.sum(dn * n, axis=-1, keepdims=True))
        dy = jnp.where(pl.program_id(0) < 2 * GDN_HEADS, dy_norm, dn)
        sg = jax.nn.sigmoid(pre)
        dpre = dy * (sg * (1.0 + pre * (1.0 - sg)))
        dx = sum(w[j:j + 1, :] * _shift_rows(dpre, CONV_WIDTH - 1 - j, reverse=True) for j in range(CONV_WIDTH))
        dx_ref[...] = dx.astype(dx_ref.dtype)

        @pl.when(pl.program_id(1) == 0)
        def _():
            dw_ref[...] = jnp.zeros_like(dw_ref)

        for j in range(CONV_WIDTH):
            dw_ref[j:j + 1, :] += jnp.sum(dpre * _shift_rows(x, CONV_WIDTH - 1 - j), axis=0, keepdims=True)

    return pl.pallas_call(
        body, name="gdn_prep_bwd", grid=(GDN_BLOCKS, B),
        in_specs=[pl.BlockSpec((S, LANES), lambda j, b: (b, COL_GDN // LANES + j)),
                  pl.BlockSpec((CONV_WIDTH, LANES), lambda j, b: (0, j))]
        + [pl.BlockSpec((S, LANES), lambda j, b, t=t: (b, jnp.clip(j - t * H, 0, H - 1))) for t in range(3)],
        out_specs=[pl.BlockSpec((S, LANES), lambda j, b: (b, j)),
                   pl.BlockSpec((CONV_WIDTH, LANES), lambda j, b: (0, j))],
        out_shape=[jax.ShapeDtypeStruct((T, 3 * GDN_WIDTH), MXU_DTYPE),
                   jax.ShapeDtypeStruct((CONV_WIDTH, 3 * GDN_WIDTH), f32)],
        compiler_params=_cparams("arbitrary", "arbitrary"),
    )(P, conv_w, dGq, dGk, dGv)


GDN_GROUP = 16
GDN_GROUP_FWD = 16
B_NN = (((2,), (1,)), ((0,), (0,)))
B_NT = (((2,), (2,)), ((0,), (0,)))
B_TN = (((1,), (1,)), ((0,), (0,)))


def _bmm(a, b, dims, precision=None):
    if precision is None:
        a, b = _mx(a), _mx(b)
    return lax.dot_general(a, b, dims, preferred_element_type=f32, precision=precision)


def _tri_inverse(A):
    C = A.shape[-1]
    row = lax.broadcasted_iota(jnp.int32, A.shape, 1)
    col = lax.broadcasted_iota(jnp.int32, A.shape, 2)
    eye = (row == col).astype(f32)
    X = jnp.where((row // 4) == (col // 4), -A, 0.0)
    X2 = _bmm(X, X, B_NN, INV_PRECISION)
    Tm = eye + X + X2 + _bmm(X, X2, B_NN, INV_PRECISION)
    b = 4
    while b < C:
        off = ((row // (2 * b)) == (col // (2 * b))) & ((row // b) != (col // b))
        Tm = Tm - _bmm(_bmm(Tm, jnp.where(off, A, 0.0), B_NN, INV_PRECISION), Tm, B_NN, INV_PRECISION)
        b *= 2
    return Tm


def _pick_lane(block, lane_idx):
    lane = lax.broadcasted_iota(jnp.int32, block.shape, 1)
    return jnp.sum(jnp.where(lane == lane_idx, block, 0.0), axis=1, keepdims=True)


def _gdn_local(q, k, v, beta, gc, Tm=None, uwm=None):
    C = GDN_CHUNK
    n = q.shape[0] // C
    q = q.reshape(n, C, -1) * (GDN_HEAD_DIM ** -0.5)
    k = k.reshape(n, C, -1)
    v = v.reshape(n, C, -1)
    beta = beta.reshape(n, C, 1)
    gc = gc.reshape(n, C, 1)
    row = lax.broadcasted_iota(jnp.int32, (n, C, C), 1)
    col = lax.broadcasted_iota(jnp.int32, (n, C, C), 2)
    gcT = jnp.swapaxes(jnp.broadcast_to(gc, (n, C, C)), 1, 2)
    D = jnp.exp(jnp.where(row >= col, gc - gcT, NEG_INF))
    kb = k * beta
    vb = v * beta
    A = jnp.where(row > col, _bmm(kb, k, B_NT) * D, 0.0)
    Gam = jnp.exp(gc)
    kg = kb * Gam
    gl = gc[:, C - 1:C, :]
    kdec = jnp.exp(gl - gc)
    loc = dict(q=q, k=k, v=v, beta=beta, gc=gc, D=D, kb=kb, vb=vb, A=A, Gam=Gam, kg=kg,
               kdec=kdec, kd=k * kdec, qg=q * Gam, gam=jnp.exp(gl), row=row, col=col)
    uwm = Tm is None if uwm is None else uwm
    Tm = _tri_inverse(A) if Tm is None else Tm.reshape(n, C, C)
    if uwm:
        loc.update(u=_bmm(Tm, vb, B_NN), w=_bmm(Tm, kg, B_NN), M=_bmm(q, k, B_NT) * D)
    loc["Tm"] = Tm
    return loc


def _gdn_store_local(loc, r0, u_s, w_s, qg_s, kd_s, M_s, gam_s, c0):
    n = loc["u"].shape[0]
    R = n * GDN_CHUNK
    u_s[pl.ds(r0, R), :] = loc["u"].reshape(R, -1)
    w_s[pl.ds(r0, R), :] = loc["w"].reshape(R, -1)
    qg_s[pl.ds(r0, R), :] = loc["qg"].reshape(R, -1)
    kd_s[pl.ds(r0, R), :] = loc["kd"].reshape(R, -1)
    M_s[pl.ds(r0, R), :] = loc["M"].reshape(R, -1)
    gam_s[pl.ds(c0, n)] = jnp.broadcast_to(loc["gam"], (n, 1, LANES))


def _gdn_specs(S):
    blk = lambda off: pl.BlockSpec((S, LANES), lambda b, h: (b, off + h))
    return blk


def gdn_fwd(G, gates, P, g_on, *, B):
    T = G.shape[0]
    S = T // B
    C = GDN_CHUNK
    N = S // C
    grp = min(GDN_GROUP_FWD, N)
    R = grp * C
    hd = GDN_HEAD_DIM

    def body(q_ref, k_ref, v_ref, gt_ref, z_ref, gon_ref, o_ref, ob_ref, st_ref, tm_ref, A_s, B_s, Q_s, O_s, gam_s):
        h = pl.program_id(1)

        def local(gi, carry):
            r0 = pl.multiple_of(gi * R, R)
            gt = gt_ref[pl.ds(r0, R), :]
            loc = _gdn_local(q_ref[pl.ds(r0, R), :], k_ref[pl.ds(r0, R), :], v_ref[pl.ds(r0, R), :],
                             _pick_lane(gt, SM_B + h), _pick_lane(gt, SM_A + h))
            chunks = pl.ds(gi * grp, grp)
            tm_ref[0, 0, pl.ds(r0, R), :] = loc["Tm"].reshape(R, C)
            A_s[chunks] = -_bmm(loc["kd"], loc["w"], B_TN)
            B_s[chunks] = _bmm(loc["kd"], loc["u"], B_TN)
            Q_s[pl.ds(r0, R), :] = (loc["qg"] - _bmm(loc["M"], loc["w"], B_NN)).reshape(R, hd)
            O_s[pl.ds(r0, R), :] = _bmm(loc["M"], loc["u"], B_NN).reshape(R, hd)
            gam_s[chunks] = jnp.broadcast_to(loc["gam"], (grp, 1, LANES))
            return carry

        lax.fori_loop(0, N // grp, local, 0)

        def step(n, state):
            st_ref[0, 0, n] = state
            return state * gam_s[n] + _dotm(A_s[n], state, NN) + B_s[n]

        lax.fori_loop(0, N, step, jnp.zeros((hd, hd), f32))

        def outputs(gi, carry):
            r0 = pl.multiple_of(gi * R, R)
            Q = Q_s[pl.ds(r0, R), :].reshape(grp, C, hd)
            o = _bmm(Q, st_ref[0, 0, pl.ds(gi * grp, grp)], B_NN).reshape(R, hd) + O_s[pl.ds(r0, R), :]
            o_ref[pl.ds(r0, R), :] = o
            return carry

        lax.fori_loop(0, N // grp, outputs, 0)
        o = o_ref[...]
        z = z_ref[...]
        ob_ref[...] = (_head_rms(o, gon_ref[...])[0] * (z * jax.nn.sigmoid(z))).astype(ob_ref.dtype)

    blk = lambda off: pl.BlockSpec((S, LANES), lambda b, h: (b, off + h))
    rows = lambda: pltpu.VMEM((S, hd), f32)
    return pl.pallas_call(
        body, name="gdn_fwd", grid=(B, GDN_HEADS),
        in_specs=[blk(0), blk(GDN_HEADS), blk(2 * GDN_HEADS), pl.BlockSpec((S, LANES), lambda b, h: (b, 0)),
                  blk(COL_Z // LANES), pl.BlockSpec((1, hd), lambda b, h: (0, 0))],
        out_specs=[blk(0), blk(0), pl.BlockSpec((1, 1, N, hd, hd), lambda b, h: (b, h, 0, 0, 0)),
                   pl.BlockSpec((1, 1, S, C), lambda b, h: (b, h, 0, 0))],
        out_shape=[jax.ShapeDtypeStruct((T, GDN_WIDTH), f32), jax.ShapeDtypeStruct((T, GDN_WIDTH), MXU_DTYPE),
                   jax.ShapeDtypeStruct((B, GDN_HEADS, N, hd, hd), f32), jax.ShapeDtypeStruct((B, GDN_HEADS, S, C), f32)],
        scratch_shapes=[pltpu.VMEM((N, hd, hd), f32), pltpu.VMEM((N, hd, hd), f32), rows(), rows(),
                        pltpu.VMEM((N, 1, LANES), f32)],
        compiler_params=_cparams("parallel", "parallel"),
    )(G, G, G, gates, P, g_on)


def gdn_bwd(G, gates, P, g_on, o_raw, states, tm, d_oab, *, B):
    T = G.shape[0]
    S = T // B
    C = GDN_CHUNK
    N = S // C
    grp = min(GDN_GROUP, N)
    R = grp * C
    hd = GDN_HEAD_DIM

    def body(q_ref, k_ref, v_ref, gt_ref, z_ref, gon_ref, o_ref, st_ref, tm_ref, dob_ref,
             dq_ref, dk_ref, dv_ref, dgt_ref, dz_ref, dgon_ref,
             u_s, w_s, M_s, gam_s, do_s, A_s, C_s, dst_s):
        b, h = pl.program_id(0), pl.program_id(1)

        @pl.when((b == 0) & (h == 0))
        def _():
            dgon_ref[...] = jnp.zeros_like(dgon_ref)

        @pl.when(h == 0)
        def _():
            dgt_ref[...] = jnp.zeros_like(dgt_ref)

        def group_inputs(gi, uwm):
            r0 = pl.multiple_of(gi * R, R)
            gt = gt_ref[pl.ds(r0, R), :]
            return r0, _gdn_local(q_ref[pl.ds(r0, R), :], k_ref[pl.ds(r0, R), :], v_ref[pl.ds(r0, R), :],
                                  _pick_lane(gt, SM_B + h), _pick_lane(gt, SM_A + h), tm_ref[0, 0, pl.ds(r0, R), :], uwm)

        def local(gi, carry):
            r0, loc = group_inputs(gi, True)
            rows, chunks = pl.ds(r0, R), pl.ds(gi * grp, grp)
            u_s[rows, :] = loc["u"].reshape(R, hd)
            w_s[rows, :] = loc["w"].reshape(R, hd)
            M_s[rows, :] = loc["M"].reshape(R, C)
            gam_s[chunks] = jnp.broadcast_to(loc["gam"], (grp, 1, LANES))
            o, z, gon = o_ref[rows, :], z_ref[rows, :], gon_ref[...]
            dob = dob_ref[rows, :]
            on, ro = _head_rms(o, gon)
            sz = jax.nn.sigmoid(z)
            dz_ref[rows, :] = (dob * on * (sz * (1.0 + z * (1.0 - sz)))).astype(dz_ref.dtype)
            do, dgon = _head_rms_bwd(o, ro, gon, dob * (z * sz))
            do_s[rows, :] = do
            dgon_ref[...] += dgon
            A_s[chunks] = -_bmm(loc["kd"], loc["w"], B_TN)
            C_s[chunks] = _bmm(loc["qg"] - _bmm(loc["M"], loc["w"], B_NN), do.reshape(grp, C, hd), B_TN)
            return carry

        lax.fori_loop(0, N // grp, local, 0)

        def step(t, dS):
            n = N - 1 - t
            dst_s[n] = dS
            return dS * gam_s[n] + _dotm(A_s[n], dS, TN) + C_s[n]

        lax.fori_loop(0, N, step, jnp.zeros((hd, hd), f32))

        def finish(gi, carry):
            r0, L = group_inputs(gi, False)
            n = grp
            rows, chunks = pl.ds(r0, R), pl.ds(gi * grp, grp)
            g3 = lambda ref: ref[rows, :].reshape(n, C, -1)
            u, w, do = g3(u_s), g3(w_s), g3(do_s)
            L["M"] = g3(M_s)
            state, dS = st_ref[0, 0, chunks], dst_s[chunks]
            v_new = u - _bmm(w, state, B_NN)
            du = _bmm(L["M"], do, B_TN) + _bmm(L["kd"], dS, B_NN)
            dw = -_bmm(du, state, B_NT)
            dqg = _bmm(do, state, B_NT)
            dM = _bmm(do, v_new, B_NT)
            dkd = _bmm(v_new, dS, B_NT)
            dgl_state = jnp.sum(jnp.sum(dS * state, axis=2, keepdims=True), axis=1, keepdims=True) * L["gam"]
            TmT = jnp.swapaxes(L["Tm"], 1, 2)
            dTm = _bmm(du, L["vb"], B_NT) + _bmm(dw, L["kg"], B_NT)
            dvb = _bmm(TmT, du, B_NN)
            dkg = _bmm(TmT, dw, B_NN)
            dA = jnp.where(L["row"] > L["col"], -_bmm(_bmm(TmT, dTm, B_NN), TmT, B_NN), 0.0)
            dKK = dA * L["D"]
            dQK = dM * L["D"]
            dkb = _bmm(dKK, L["k"], B_NN) + dkg * L["Gam"]
            dk = (_bmm(dKK, L["kb"], B_TN) + _bmm(dQK, L["q"], B_TN) + dkd * L["kdec"] + L["beta"] * dkb)
            dq = (_bmm(dQK, L["k"], B_NN) + dqg * L["Gam"]) * (GDN_HEAD_DIM ** -0.5)
            E = dA * L["A"] + dM * L["M"]
            r = jnp.sum(dkd * L["kd"], axis=-1, keepdims=True)
            dgc = (jnp.sum(E, axis=2, keepdims=True) - jnp.sum(jnp.swapaxes(E, 1, 2), axis=2, keepdims=True)
                   + jnp.sum(dkg * L["kg"], axis=-1, keepdims=True) + jnp.sum(dqg * L["qg"], axis=-1, keepdims=True) - r)
            dgl = jnp.sum(r, axis=1, keepdims=True) + dgl_state
            rowc = lax.broadcasted_iota(jnp.int32, (n, C, 1), 1)
            dgc = dgc + jnp.where(rowc == C - 1, dgl, 0.0)
            dbeta = jnp.sum(dkb * L["k"], axis=-1, keepdims=True) + jnp.sum(dvb * L["v"], axis=-1, keepdims=True)
            dq_ref[rows, :] = dq.reshape(R, hd)
            dk_ref[rows, :] = dk.reshape(R, hd)
            dv_ref[rows, :] = (L["beta"] * dvb).reshape(R, hd)
            lane = lax.broadcasted_iota(jnp.int32, (R, LANES), 1)
            dgt_ref[rows, :] += (jnp.where(lane == SM_B + h, dbeta.reshape(R, 1), 0.0)
                                 + jnp.where(lane == SM_A + h, dgc.reshape(R, 1), 0.0))
            return carry

        lax.fori_loop(0, N // grp, finish, 0)

    blk = lambda off: pl.BlockSpec((S, LANES), lambda b, h: (b, off + h))
    rows = lambda: pltpu.VMEM((S, hd), f32)
    return pl.pallas_call(
        body, name="gdn_bwd", grid=(B, GDN_HEADS),
        in_specs=[blk(0), blk(GDN_HEADS), blk(2 * GDN_HEADS), pl.BlockSpec((S, LANES), lambda b, h: (b, 0)),
                  blk(COL_Z // LANES), pl.BlockSpec((1, hd), lambda b, h: (0, 0)), blk(0),
                  pl.BlockSpec((1, 1, N, hd, hd), lambda b, h: (b, h, 0, 0, 0)),
                  pl.BlockSpec((1, 1, S, C), lambda b, h: (b, h, 0, 0)), blk(GDN_HEADS)],
        out_specs=[blk(0), blk(0), blk(0), pl.BlockSpec((S, LANES), lambda b, h: (b, 0)), blk(0),
                   pl.BlockSpec((1, hd), lambda b, h: (0, 0))],
        out_shape=[jax.ShapeDtypeStruct((T, GDN_WIDTH), f32), jax.ShapeDtypeStruct((T, GDN_WIDTH), f32),
                   jax.ShapeDtypeStruct((T, GDN_WIDTH), f32), jax.ShapeDtypeStruct((T, LANES), f32),
                   jax.ShapeDtypeStruct((T, GDN_WIDTH), MXU_DTYPE), jax.ShapeDtypeStruct((1, hd), f32)],
        scratch_shapes=[rows(), rows(), pltpu.VMEM((S, C), f32), pltpu.VMEM((N, 1, LANES), f32), rows(),
                        pltpu.VMEM((N, hd, hd), f32), pltpu.VMEM((N, hd, hd), f32), pltpu.VMEM((N, hd, hd), f32)],
        compiler_params=_cparams("arbitrary", "arbitrary"),
    )(G, G, G, gates, P, g_on, o_raw, states, tm, d_oab)


IN_SPLIT = (0, 1536, 1544, 3080, 3088, 3600)


IN_SHARD = IN_DIM // 4
IN_SHARD_PAD = 928


def align_w_in_t(wt):
    s = IN_SPLIT
    pad = jnp.zeros((IN_ALIGNED - IN_DIM, wt.shape[1]), wt.dtype)
    return jnp.concatenate([wt[s[0]:s[1]], wt[s[2]:s[3]], wt[s[4]:s[5]], wt[s[1]:s[2]], wt[s[3]:s[4]], pad], axis=0)


def unalign_w_in_t(wa):
    return jnp.concatenate([wa[0:1536], wa[COL_SMALL:COL_SMALL + 8], wa[1536:3072],
                            wa[COL_SMALL + 8:COL_SMALL + 16], wa[3072:3584]], axis=0)


IN_SEGMENTS = (((0, 1536), 0), ((1536, 1544), COL_SMALL), ((1544, 3080), 1536), ((3080, 3088), COL_SMALL + 8),
               ((3088, 3600), 3072))


def align_w_in_slots(slots):
    pieces = []
    for (lo, hi), _ in sorted(IN_SEGMENTS, key=lambda seg: seg[1]):
        for k in range(N_CHIPS):
            a, b = max(lo, k * IN_SHARD), min(hi, (k + 1) * IN_SHARD)
            if a < b:
                pieces.append(slots[k, a - k * IN_SHARD:b - k * IN_SHARD])
    pieces.append(jnp.zeros((IN_ALIGNED - IN_DIM, slots.shape[2]), slots.dtype))
    return jnp.concatenate(pieces, axis=0)


def unalign_to_slots(wa):
    slots = []
    for k in range(N_CHIPS):
        lo, hi = k * IN_SHARD, (k + 1) * IN_SHARD
        pieces = []
        for (a, b), first in IN_SEGMENTS:
            x, y = max(a, lo), min(b, hi)
            if x < y:
                pieces.append(wa[first + x - a:first + y - a])
        pieces.append(jnp.zeros((IN_SHARD_PAD - IN_SHARD, wa.shape[1]), wa.dtype))
        slots.append(jnp.concatenate(pieces, axis=0))
    return jnp.stack(slots)


def _lanes_vec(pieces):
    v = jnp.zeros((1, LANES), f32)
    for off, a in pieces:
        v = lax.dynamic_update_slice(v, a.astype(f32), (0, off))
    return v


def local_step(x, mem, target, w, sp, *, B):
    T = x.shape[0]
    S = T // B
    gq8, gk8 = jnp.tile(sp["fox_qnorm_g"], (1, FOX_HEADS)), jnp.tile(sp["fox_knorm_g"], (1, FOX_HEADS))
    go2 = jnp.tile(sp["fox_onorm_g"], (1, 2))
    bias = _lanes_vec([(SM_F, sp["fox_f_bias"]), (SM_A, sp["gdn_dt_bias"])])
    alog = _lanes_vec([(SM_A, sp["gdn_A_log"])])

    h1 = rms_fwd(x, sp["norm_mix_g"], name="rms_mix")
    P = matmul(h1, w["wa_t"], tb=True, name="mm_in", tn=IN_TILE)
    gates = gates_fwd(P, bias, alog, B=B)
    c = gates[:, SM_F:SM_F + FOX_HEADS].reshape(B, S, FOX_HEADS).transpose(0, 2, 1)
    ccol, crow = c[..., None], c.reshape(B, FOX_HEADS, S // FOX_TQ, 1, FOX_TQ)
    qn, kn, vb = fox_prep_fwd(P, gq8, gk8)
    o_raw, o_a, lse = fox_core_fwd(qn, kn, vb, ccol, crow, go2, B=B)
    G = gdn_prep_fwd(P, w["conv_w"], B=B)
    ob_raw, o_b, states, gdn_tm = gdn_fwd(G, gates, P, sp["gdn_onorm_g"], B=B)
    oab = jnp.concatenate([o_a, o_b], axis=1)
    if "late" in w:
        w = {**w, **w["late"](oab)}
    x2 = matmul(oab, w["w_out"], residual=x, name="mm_out")
    hq = rms_fwd(x2, sp["norm_xattn_g"], name="rms_xattn")
    hm = rms_fwd(mem, sp["mem_norm_g"], name="rms_mem")
    cq = matmul(hq, w["w_cq"], name="mm_cq")
    ckv = matmul(hm, w["w_ckv"], name="mm_ckv")
    co = xattn_fwd(cq, ckv, sp["xattn_qnorm_g"], sp["xattn_knorm_g"], B=B)
    x3 = matmul(co, w["w_co"], b_stacked=True, residual=x2, name="mm_co")
    hf = rms_fwd(x3, sp["norm_mlp_g"], name="rms_mlp")
    act = matmul(hf, w["w_mlp1"], b_stacked=True, relu2_out=True, out_dtype=MXU_DTYPE, name="mm_mlp1")
    dy, dy_op, loss = matmul_rows(act, w["w_mlp2"], (x3, target), mode="loss", name="mm_mlp2_loss")

    da = matmul(dy_op, w["w_mlp2"], tb=True, relu2_bwd_aux=act, out_dtype=MXU_DTYPE, name="mm_d_act")
    g_mlp2 = matmul(act, dy_op, ta=True, out_dtype=WIRE_DTYPE, name="mm_g_mlp2")
    g_mlp1 = matmul(hf, da, ta=True, out_stacked=True, out_dtype=WIRE_DTYPE, name="mm_g_mlp1")
    by_rows = lambda g: g.reshape(N_CHIPS, g.shape[0] // N_CHIPS, g.shape[1])
    early = w.get("grads_ready", lambda grads: jnp.zeros((1, 1), f32))
    tok = early(dict(w_mlp1=g_mlp1, w_mlp2=by_rows(g_mlp2)))[0, 0]
    dx3, g_norm_mlp = matmul_rows(da, w["w_mlp1"], (x3, sp["norm_mlp_g"] + tok, dy), mode="rms_bwd", tb=True,
                                  b_stacked=True, name="mm_d_hf_rms")
    dco = matmul(dx3, w["w_co"], tb=True, b_stacked=True, name="mm_d_co")
    g_co = matmul(co, dx3, ta=True, out_stacked=True, out_dtype=WIRE_DTYPE, name="mm_g_co")
    dcq, dckv, g_xq, g_xk = xattn_bwd(cq, ckv, sp["xattn_qnorm_g"], sp["xattn_knorm_g"], dco, B=B)
    g_cq = matmul(hq, dcq, ta=True, out_dtype=WIRE_DTYPE, name="mm_g_cq")
    g_ckv = matmul(hm, dckv, ta=True, out_dtype=WIRE_DTYPE, name="mm_g_ckv")
    _, g_mem_norm = matmul_rows(dckv, w["w_ckv"], (mem, sp["mem_norm_g"], None), mode="rms_bwd", tb=True, name="mm_d_hm_rms")
    dx2, g_norm_xattn = matmul_rows(dcq, w["w_cq"], (x2, sp["norm_xattn_g"], dx3), mode="rms_bwd", tb=True, name="mm_d_hq_rms")
    doab = matmul(dx2, w["w_out"], tb=True, name="mm_d_oab")
    g_out = matmul(oab, dx2, ta=True, out_dtype=WIRE_DTYPE, name="mm_g_out")
    tok = early(dict(w_co=g_co, w_cq=by_rows(g_cq), w_ckv=by_rows(g_ckv), w_out=by_rows(g_out)))[0, 0]
    dqn, dkn, dv_f, dckey, dcrow, dgo2 = fox_core_bwd(qn, kn, vb, ccol, crow, go2 + tok, o_raw, lse, doab, B=B)
    dq_f, dk_f, dgq8, dgk8 = fox_prep_bwd(P, gq8, gk8, dqn, dkn)
    dGq, dGk, dGv, dgt, dz, g_gdn_on = gdn_bwd(G, gates, P, sp["gdn_onorm_g"], ob_raw, states, gdn_tm, doab, B=B)
    dPg, g_conv = gdn_prep_bwd(P, w["conv_w"], dGq, dGk, dGv, B=B)
    dc = (dckey[:, :, 0, :] + dcrow.reshape(B, FOX_HEADS, S)).transpose(0, 2, 1).reshape(T, FOX_HEADS)
    dgates = dgt + jnp.pad(dc, ((0, 0), (SM_F, LANES - SM_F - FOX_HEADS)))
    dsmall, par = gates_bwd(P, bias, alog, dgates, B=B)
    dP = jnp.concatenate([dq_f, dk_f, dv_f, dPg, dz, dsmall, jnp.zeros((T, IN_ALIGNED - COL_SMALL - LANES), MXU_DTYPE)], axis=1)
    g_wa = matmul(dP, h1, ta=True, out_dtype=WIRE_DTYPE, name="mm_g_in", tm=IN_TILE)
    g_in = unalign_to_slots(g_wa)
    tok = early(dict(w_in=g_in))[0, 0]
    dx, g_norm_mix = matmul_rows(dP, w["wa_t"], (x, sp["norm_mix_g"] + tok, dx2), mode="rms_bwd", tk=IN_TILE,
                                 name="mm_d_h1_rms")

    fold = lambda g: jnp.sum(g.reshape(-1, FOX_HEAD_DIM), axis=0, keepdims=True)
    big = dict(w_in=g_in, w_out=by_rows(g_out), w_cq=by_rows(g_cq), w_ckv=by_rows(g_ckv), w_co=g_co, w_mlp1=g_mlp1,
               w_mlp2=by_rows(g_mlp2))
    small = dict(norm_mix_g=g_norm_mix, fox_qnorm_g=fold(dgq8), fox_knorm_g=fold(dgk8),
                 fox_f_bias=par[0:1, SM_F:SM_F + FOX_HEADS], fox_onorm_g=fold(dgo2), gdn_conv_w=g_conv,
                 gdn_A_log=par[1:2, SM_A:SM_A + GDN_HEADS], gdn_dt_bias=par[0:1, SM_A:SM_A + GDN_HEADS],
                 gdn_onorm_g=g_gdn_on, norm_xattn_g=g_norm_xattn, mem_norm_g=g_mem_norm,
                 xattn_qnorm_g=g_xq, xattn_knorm_g=g_xk, norm_mlp_g=g_norm_mlp)
    return loss, dx, big, small


MESH_IDS = pl.DeviceIdType.MESH
N_CHIPS = 4
HBM_SPEC = pl.BlockSpec(memory_space=pltpu.HBM)
PACK_ROWS = 30720
PACK_HALF = PACK_ROWS // 2
PACK_BLOCK = 3072


def _place():
    return lax.axis_index("x"), lax.axis_index("y"), lax.axis_index("c")


def _other_chips(x, y):
    return [(1 - x, y), (x, 1 - y), (1 - x, 1 - y)]


def _remote(src, dst, send_sem, recv_sem, to):
    return pltpu.make_async_remote_copy(src_ref=src, dst_ref=dst, send_sem=send_sem, recv_sem=recv_sem,
                                        device_id=to, device_id_type=MESH_IDS)


def all_gather_shards(packed):
    half = PACK_HALF

    def body(src_ref, out_ref, send_sems, recv_sems):
        x, y, c = _place()
        me_chip = 2 * x + y
        sibling = (x, y, 1 - c)
        chips = _other_chips(x, y)

        def rows(chip, core):
            return out_ref.at[chip, pl.ds(core * half, half), :]

        sends = [_remote(src_ref.at[pl.ds(c * half, half), :], rows(me_chip, c), send_sems.at[j], recv_sems.at[j], (px, py, c))
                 for j, (px, py) in enumerate(chips)]
        for cp in sends:
            cp.start()
        passed = []
        for j, (px, py) in enumerate(chips):
            theirs = rows(2 * px + py, c)
            _remote(theirs, theirs, send_sems.at[j], recv_sems.at[j], (px, py, c)).wait_recv()
            cp = _remote(theirs, theirs, send_sems.at[3 + j], recv_sems.at[3 + j], sibling)
            cp.start()
            passed.append(cp)
        for j, (px, py) in enumerate(chips):
            theirs = rows(2 * px + py, 1 - c)
            _remote(theirs, theirs, send_sems.at[3 + j], recv_sems.at[3 + j], sibling).wait_recv()
        for cp in sends + passed:
            cp.wait_send()

    return pl.pallas_call(
        body, name="all_gather_shards", in_specs=[HBM_SPEC], out_specs=HBM_SPEC,
        out_shape=jax.ShapeDtypeStruct((N_CHIPS,) + packed.shape, packed.dtype),
        scratch_shapes=[pltpu.SemaphoreType.DMA((6,)), pltpu.SemaphoreType.DMA((6,))],
    )(packed)


def exchange_core_halves(G):
    half = PACK_HALF

    def body(g_ref, land_ref, send_sem, recv_sem):
        x, y, c = _place()
        cp = _remote(g_ref.at[:, pl.ds((1 - c) * half, half), :], land_ref, send_sem, recv_sem, (x, y, 1 - c))
        cp.start()
        cp.wait()

    return pl.pallas_call(
        body, name="exchange_core_halves", in_specs=[HBM_SPEC], out_specs=HBM_SPEC,
        out_shape=jax.ShapeDtypeStruct((N_CHIPS, half, LANES), G.dtype),
        scratch_shapes=[pltpu.SemaphoreType.DMA(()), pltpu.SemaphoreType.DMA(())],
    )(G)


def add_core_halves(G, land, core):
    nb = PACK_HALF // PACK_BLOCK

    def body(c_ref, g_ref, l_ref, o_ref):
        o_ref[...] = (g_ref[...].astype(f32) + l_ref[...].astype(f32)).astype(o_ref.dtype)

    blk = (1, PACK_BLOCK, LANES)
    return pl.pallas_call(
        body, name="add_core_halves",
        grid_spec=pltpu.PrefetchScalarGridSpec(
            num_scalar_prefetch=1, grid=(N_CHIPS, nb),
            in_specs=[pl.BlockSpec(blk, lambda k, i, c_ref: (k, c_ref[0] * nb + i, 0)),
                      pl.BlockSpec(blk, lambda k, i, c_ref: (k, i, 0))],
            out_specs=pl.BlockSpec(blk, lambda k, i, c_ref: (k, i, 0))),
        out_shape=jax.ShapeDtypeStruct(land.shape, land.dtype),
        compiler_params=_cparams("parallel", "parallel"),
    )(core, G, land)


def scatter_to_chips(part):
    def body(p_ref, land_ref, send_sems, recv_sems):
        x, y, c = _place()
        me_chip = 2 * x + y
        chips = _other_chips(x, y)
        sends = [_remote(p_ref.at[2 * px + py], land_ref.at[me_chip], send_sems.at[j], recv_sems.at[j], (px, py, c))
                 for j, (px, py) in enumerate(chips)]
        for cp in sends:
            cp.start()
        for j, (px, py) in enumerate(chips):
            slot = land_ref.at[2 * px + py]
            _remote(slot, slot, send_sems.at[j], recv_sems.at[j], (px, py, c)).wait_recv()
        for cp in sends:
            cp.wait_send()

    return pl.pallas_call(
        body, name="scatter_to_chips", in_specs=[HBM_SPEC], out_specs=HBM_SPEC,
        out_shape=jax.ShapeDtypeStruct(part.shape, part.dtype),
        scratch_shapes=[pltpu.SemaphoreType.DMA((3,)), pltpu.SemaphoreType.DMA((3,))],
    )(part)


def sum_chips(part, land, order):
    nb = PACK_HALF // PACK_BLOCK

    def body(order_ref, p_ref, l1_ref, l2_ref, l3_ref, o_ref):
        o_ref[...] = ((p_ref[0].astype(f32) + l1_ref[0].astype(f32)) + l2_ref[0].astype(f32)) + l3_ref[0].astype(f32)

    slot = lambda j: pl.BlockSpec((1, PACK_BLOCK, LANES), lambda i, order_ref: (order_ref[j], i, 0))
    return pl.pallas_call(
        body, name="sum_chips",
        grid_spec=pltpu.PrefetchScalarGridSpec(
            num_scalar_prefetch=1, grid=(nb,), in_specs=[slot(0), slot(1), slot(2), slot(3)],
            out_specs=pl.BlockSpec((PACK_BLOCK, LANES), lambda i, order_ref: (i, 0))),
        out_shape=jax.ShapeDtypeStruct((PACK_HALF, LANES), f32),
        compiler_params=_cparams("parallel"),
    )(order, part, land, land, land)


def swap_core_halves(red):
    def body(r_ref, out_ref, send_sem, recv_sem):
        x, y, c = _place()
        cp = _remote(r_ref, out_ref, send_sem, recv_sem, (x, y, 1 - c))
        cp.start()
        cp.wait()

    return pl.pallas_call(
        body, name="swap_core_halves", in_specs=[HBM_SPEC], out_specs=HBM_SPEC,
        out_shape=jax.ShapeDtypeStruct(red.shape, red.dtype),
        scratch_shapes=[pltpu.SemaphoreType.DMA(()), pltpu.SemaphoreType.DMA(())],
    )(red)


def _half(ref, core):
    rows = ref.shape[-2] // 2
    return ref.at[(slice(None),) * (len(ref.shape) - 2) + (pl.ds(core * rows, rows), slice(None))]


def gather_weights(shards, conv):
    n = len(shards)

    def body(*refs):
        src, conv_src = refs[:n], refs[n]
        out, conv_out = refs[n + 1:2 * n + 1], refs[2 * n + 1]
        send_sems, recv_sems = refs[2 * n + 2], refs[2 * n + 3]
        x, y, c = _place()
        me_chip = 2 * x + y
        sibling = (x, y, 1 - c)
        chips = _other_chips(x, y)
        sends = []
        for a in range(n):
            for j, (px, py) in enumerate(chips):
                sends.append(_remote(_half(src[a], c), _half(out[a].at[me_chip], c),
                                     send_sems.at[6 * a + j], recv_sems.at[6 * a + j], (px, py, c)))
        for j, (px, py) in enumerate(chips):
            sends.append(_remote(conv_src, conv_out.at[me_chip], send_sems.at[6 * n + j], recv_sems.at[6 * n + j], (px, py, c)))
        for cp in sends:
            cp.start()
        passed = []
        for a in range(n):
            for j, (px, py) in enumerate(chips):
                theirs = _half(out[a].at[2 * px + py], c)
                _remote(theirs, theirs, send_sems.at[6 * a + j], recv_sems.at[6 * a + j], (px, py, c)).wait_recv()
                cp = _remote(theirs, theirs, send_sems.at[6 * a + 3 + j], recv_sems.at[6 * a + 3 + j], sibling)
                cp.start()
                passed.append(cp)
        for j, (px, py) in enumerate(chips):
            theirs = conv_out.at[2 * px + py]
            _remote(theirs, theirs, send_sems.at[6 * n + j], recv_sems.at[6 * n + j], (px, py, c)).wait_recv()
        for a in range(n):
            for j, (px, py) in enumerate(chips):
                theirs = _half(out[a].at[2 * px + py], 1 - c)
                _remote(theirs, theirs, send_sems.at[6 * a + 3 + j], recv_sems.at[6 * a + 3 + j], sibling).wait_recv()
        for cp in sends + passed:
            cp.wait_send()

    return pl.pallas_call(
        body, name="gather_weights", in_specs=[HBM_SPEC] * (n + 1), out_specs=[HBM_SPEC] * (n + 1),
        out_shape=[jax.ShapeDtypeStruct((N_CHIPS,) + s.shape, s.dtype) for s in list(shards) + [conv]],
        scratch_shapes=[pltpu.SemaphoreType.DMA((6 * n + 3,)), pltpu.SemaphoreType.DMA((6 * n + 3,))],
    )(*shards, conv)


SEM_SPEC = pl.BlockSpec(memory_space=pltpu.SEMAPHORE)
SPLIT_EFFECT = pltpu.SideEffectType.DATAFLOW_SIDE_EFFECTING


def _gather_async_copies(src, land, send_sems, recv_sems, x, y, c):
    me_chip = 2 * x + y
    sends, arrivals = [], []
    for a in range(len(src)):
        for j, (px, py) in enumerate(_other_chips(x, y)):
            for core in range(2):
                sends.append(_remote(_half(src[a], c), _half(land[a].at[me_chip], c), send_sems.at[6 * a + 2 * j + core],
                                     recv_sems.at[6 * a + 2 * j + c], (px, py, core)))
                theirs = _half(land[a].at[2 * px + py], core)
                arrivals.append(_remote(theirs, theirs, send_sems.at[6 * a + 2 * j + core],
                                        recv_sems.at[6 * a + 2 * j + core], (px, py, core)))
    return sends, arrivals


def gather_weights_start(shards, after):
    n = len(shards)

    def body(*refs):
        src, land = refs[:n], refs[n:2 * n]
        send_sems, recv_sems, token = refs[2 * n + 1], refs[2 * n + 2], refs[4 * n + 3]
        x, y, c = _place()
        for cp in _gather_async_copies(src, land, send_sems, recv_sems, x, y, c)[0]:
            cp.start()
        token[...] = jnp.zeros_like(token)

    zones = [pltpu.with_memory_space_constraint(lax.empty((N_CHIPS,) + s.shape, s.dtype), pltpu.HBM) for s in shards]
    srcs = [pltpu.with_memory_space_constraint(s, pltpu.HBM) for s in shards]
    out = pl.pallas_call(
        body, name="gather_weights_start",
        out_shape=[pltpu.SemaphoreType.DMA((6 * n,)), pltpu.SemaphoreType.DMA((6 * n,))]
        + [pltpu.HBM(s.shape, s.dtype) for s in shards] + [pltpu.HBM(z.shape, z.dtype) for z in zones]
        + [jax.ShapeDtypeStruct((8, LANES), f32)],
        in_specs=[HBM_SPEC] * (2 * n) + [pl.BlockSpec(memory_space=pl.ANY)],
        out_specs=[SEM_SPEC, SEM_SPEC] + [HBM_SPEC] * (2 * n) + [pl.BlockSpec(memory_space=pltpu.VMEM)],
        input_output_aliases={i: 2 + i for i in range(2 * n)},
        compiler_params=pltpu.CompilerParams(has_side_effects=SPLIT_EFFECT),
    )(*srcs, *zones, after)
    return out[0], out[1], out[2:2 + n], out[2 + n:2 + 2 * n], out[-1]


def gather_weights_wait(send_sems, recv_sems, shards, zones, after):
    n = len(shards)

    def body(*refs):
        src, land = refs[:n], refs[n:2 * n]
        send_sems, recv_sems = refs[2 * n], refs[2 * n + 1]
        x, y, c = _place()
        sends, arrivals = _gather_async_copies(src, land, send_sems, recv_sems, x, y, c)
        for cp in sends:
            cp.wait_send()
        for cp in arrivals:
            cp.wait_recv()

    out = pl.pallas_call(
        body, name="gather_weights_wait",
        out_shape=[pltpu.HBM(s.shape, s.dtype) for s in shards] + [pltpu.HBM(z.shape, z.dtype) for z in zones],
        in_specs=[HBM_SPEC] * (2 * n) + [SEM_SPEC, SEM_SPEC, pl.BlockSpec(memory_space=pl.ANY)],
        out_specs=[HBM_SPEC] * (2 * n),
        input_output_aliases={i: i for i in range(2 * n)},
        compiler_params=pltpu.CompilerParams(has_side_effects=SPLIT_EFFECT),
    )(*shards, *zones, send_sems, recv_sems, after)
    return out[n:]


def swap_grad_halves(grads, *, name):
    n = len(grads)

    def body(*refs):
        g, land, send_sems, recv_sems = refs[:n], refs[n:2 * n], refs[2 * n], refs[2 * n + 1]
        x, y, c = _place()
        copies = [_remote(_half(g[a], 1 - c), land[a], send_sems.at[a], recv_sems.at[a], (x, y, 1 - c)) for a in range(n)]
        for cp in copies:
            cp.start()
        for cp in copies:
            cp.wait()

    return pl.pallas_call(
        body, name=name, in_specs=[HBM_SPEC] * n, out_specs=[HBM_SPEC] * n,
        out_shape=[jax.ShapeDtypeStruct((N_CHIPS, g.shape[1] // 2, g.shape[2]), g.dtype) for g in grads],
        scratch_shapes=[pltpu.SemaphoreType.DMA((n,)), pltpu.SemaphoreType.DMA((n,))],
    )(*grads)


GRAD_ROWS = 256


def add_grad_halves(g, land, core, *, name):
    _, half, cols = land.shape
    tr = GRAD_ROWS if half % GRAD_ROWS == 0 else half
    nb = half // tr

    def body(c_ref, g_ref, l_ref, o_ref):
        o_ref[...] = (g_ref[...].astype(f32) + l_ref[...].astype(f32)).astype(o_ref.dtype)

    blk = (1, tr, cols)
    return pl.pallas_call(
        body, name=name,
        grid_spec=pltpu.PrefetchScalarGridSpec(
            num_scalar_prefetch=1, grid=(N_CHIPS, nb),
            in_specs=[pl.BlockSpec(blk, lambda k, i, c_ref: (k, c_ref[0] * nb + i, 0)),
                      pl.BlockSpec(blk, lambda k, i, c_ref: (k, i, 0))],
            out_specs=pl.BlockSpec(blk, lambda k, i, c_ref: (k, i, 0))),
        out_shape=jax.ShapeDtypeStruct(land.shape, land.dtype),
        compiler_params=_cparams("parallel", "parallel"),
    )(core, g, land)


def scatter_grads(parts):
    n = len(parts)

    def body(*refs):
        p, land, send_sems, recv_sems = refs[:n], refs[n:2 * n], refs[2 * n], refs[2 * n + 1]
        x, y, c = _place()
        me_chip = 2 * x + y
        chips = _other_chips(x, y)
        sends = [_remote(p[a].at[2 * px + py], land[a].at[me_chip], send_sems.at[3 * a + j], recv_sems.at[3 * a + j], (px, py, c))
                 for a in range(n) for j, (px, py) in enumerate(chips)]
        for cp in sends:
            cp.start()
        for a in range(n):
            for j, (px, py) in enumerate(chips):
                slot = land[a].at[2 * px + py]
                _remote(slot, slot, send_sems.at[3 * a + j], recv_sems.at[3 * a + j], (px, py, c)).wait_recv()
        for cp in sends:
            cp.wait_send()

    return pl.pallas_call(
        body, name="scatter_grads", in_specs=[HBM_SPEC] * n, out_specs=[HBM_SPEC] * n,
        out_shape=[jax.ShapeDtypeStruct(p.shape, p.dtype) for p in parts],
        scratch_shapes=[pltpu.SemaphoreType.DMA((3 * n,)), pltpu.SemaphoreType.DMA((3 * n,))],
    )(*parts)


def _scatter_async_copies(parts, land, send_sems, recv_sems, x, y, c):
    me_chip = 2 * x + y
    sends, arrivals = [], []
    for a in range(len(parts)):
        for j, (px, py) in enumerate(_other_chips(x, y)):
            sems = (send_sems.at[3 * a + j], recv_sems.at[3 * a + j], (px, py, c))
            sends.append(_remote(parts[a].at[2 * px + py], land[a].at[me_chip], *sems))
            slot = land[a].at[2 * px + py]
            arrivals.append(_remote(slot, slot, *sems))
    return sends, arrivals


def scatter_grads_start(parts, *, name):
    n = len(parts)

    def body(*refs):
        p, land = refs[:n], refs[n:2 * n]
        send_sems, recv_sems, token = refs[2 * n], refs[2 * n + 1], refs[4 * n + 2]
        x, y, c = _place()
        for cp in _scatter_async_copies(p, land, send_sems, recv_sems, x, y, c)[0]:
            cp.start()
        token[...] = jnp.zeros_like(token)

    zones = [pltpu.with_memory_space_constraint(lax.empty(p.shape, p.dtype), pltpu.HBM) for p in parts]
    srcs = [pltpu.with_memory_space_constraint(p, pltpu.HBM) for p in parts]
    hbm = [pltpu.HBM(p.shape, p.dtype) for p in parts]
    out = pl.pallas_call(
        body, name=name,
        out_shape=[pltpu.SemaphoreType.DMA((3 * n,)), pltpu.SemaphoreType.DMA((3 * n,))] + hbm + hbm
        + [jax.ShapeDtypeStruct((8, LANES), f32)],
        in_specs=[HBM_SPEC] * (2 * n),
        out_specs=[SEM_SPEC, SEM_SPEC] + [HBM_SPEC] * (2 * n) + [pl.BlockSpec(memory_space=pltpu.VMEM)],
        input_output_aliases={i: 2 + i for i in range(2 * n)},
        compiler_params=pltpu.CompilerParams(has_side_effects=SPLIT_EFFECT),
    )(*srcs, *zones)
    return out[0], out[1], out[2:2 + n], out[2 + n:2 + 2 * n], out[-1]


def scatter_grads_wait(send_sems, recv_sems, parts, zones, after, *, name):
    n = len(parts)

    def body(*refs):
        p, land = refs[:n], refs[n:2 * n]
        x, y, c = _place()
        sends, arrivals = _scatter_async_copies(p, land, refs[2 * n], refs[2 * n + 1], x, y, c)
        for cp in sends:
            cp.wait_send()
        for cp in arrivals:
            cp.wait_recv()

    hbm = [pltpu.HBM(p.shape, p.dtype) for p in parts]
    out = pl.pallas_call(
        body, name=name, out_shape=hbm + hbm,
        in_specs=[HBM_SPEC] * (2 * n) + [SEM_SPEC, SEM_SPEC, pl.BlockSpec(memory_space=pl.ANY)],
        out_specs=[HBM_SPEC] * (2 * n),
        input_output_aliases={i: i for i in range(2 * n)},
        compiler_params=pltpu.CompilerParams(has_side_effects=SPLIT_EFFECT),
    )(*parts, *zones, send_sems, recv_sems, after)
    return out[:n], out[n:]


def sum_grads(part, land, order, *, name):
    _, half, cols = part.shape
    tr = GRAD_ROWS if half % GRAD_ROWS == 0 else half

    def body(order_ref, p_ref, l1_ref, l2_ref, l3_ref, o_ref):
        o_ref[...] = ((p_ref[0].astype(f32) + l1_ref[0].astype(f32)) + l2_ref[0].astype(f32)) + l3_ref[0].astype(f32)

    slot = lambda j: pl.BlockSpec((1, tr, cols), lambda i, order_ref: (order_ref[j], i, 0))
    return pl.pallas_call(
        body, name=name,
        grid_spec=pltpu.PrefetchScalarGridSpec(
            num_scalar_prefetch=1, grid=(half // tr,), in_specs=[slot(0), slot(1), slot(2), slot(3)],
            out_specs=pl.BlockSpec((tr, cols), lambda i, order_ref: (i, 0))),
        out_shape=jax.ShapeDtypeStruct((half, cols), f32),
        compiler_params=_cparams("parallel"),
    )(order, part, land, land, land)


def _peer(x, y, c, r):
    return ((1 - x) if r & 4 else x, (1 - y) if r & 2 else y, (1 - c) if r & 1 else c)


def _reduce_async_copies(grads, land, send_sems, recv_sems, x, y, c):
    me = 4 * x + 2 * y + c
    sends, arrivals = [], []
    for a in range(len(grads)):
        for r in range(1, N_DEV):
            px, py, pc = _peer(x, y, c, r)
            sems = (send_sems.at[7 * a + r - 1], recv_sems.at[7 * a + r - 1], (px, py, pc))
            sends.append(_remote(_half(grads[a].at[2 * px + py], pc), land[a].at[me], *sems))
            slot = land[a].at[4 * px + 2 * py + pc]
            arrivals.append(_remote(slot, slot, *sems))
    return sends, arrivals


def reduce_grads_start(grads, *, name):
    n = len(grads)

    def body(*refs):
        g, land = refs[:n], refs[n:2 * n]
        send_sems, recv_sems, token = refs[2 * n], refs[2 * n + 1], refs[4 * n + 2]
        x, y, c = _place()
        for cp in _reduce_async_copies(g, land, send_sems, recv_sems, x, y, c)[0]:
            cp.start()
        token[...] = jnp.zeros_like(token)

    zones = [pltpu.with_memory_space_constraint(lax.empty((N_DEV, g.shape[1] // 2, g.shape[2]), g.dtype), pltpu.HBM)
             for g in grads]
    srcs = [pltpu.with_memory_space_constraint(g, pltpu.HBM) for g in grads]
    out = pl.pallas_call(
        body, name=name,
        out_shape=[pltpu.SemaphoreType.DMA((7 * n,)), pltpu.SemaphoreType.DMA((7 * n,))]
        + [pltpu.HBM(g.shape, g.dtype) for g in grads] + [pltpu.HBM(z.shape, z.dtype) for z in zones]
        + [jax.ShapeDtypeStruct((8, LANES), f32)],
        in_specs=[HBM_SPEC] * (2 * n),
        out_specs=[SEM_SPEC, SEM_SPEC] + [HBM_SPEC] * (2 * n) + [pl.BlockSpec(memory_space=pltpu.VMEM)],
        input_output_aliases={i: 2 + i for i in range(2 * n)},
        compiler_params=pltpu.CompilerParams(has_side_effects=SPLIT_EFFECT),
    )(*srcs, *zones)
    return out[0], out[1], out[2:2 + n], out[2 + n:2 + 2 * n], out[-1]


def reduce_grads_wait(send_sems, recv_sems, grads, zones, after, *, name):
    n = len(grads)

    def body(*refs):
        g, land = refs[:n], refs[n:2 * n]
        x, y, c = _place()
        sends, arrivals = _reduce_async_copies(g, land, refs[2 * n], refs[2 * n + 1], x, y, c)
        for cp in sends:
            cp.wait_send()
        for cp in arrivals:
            cp.wait_recv()

    hbm = [pltpu.HBM(a.shape, a.dtype) for a in list(grads) + list(zones)]
    out = pl.pallas_call(
        body, name=name, out_shape=hbm,
        in_specs=[HBM_SPEC] * (2 * n) + [SEM_SPEC, SEM_SPEC, pl.BlockSpec(memory_space=pl.ANY)],
        out_specs=[HBM_SPEC] * (2 * n),
        input_output_aliases={i: i for i in range(2 * n)},
        compiler_params=pltpu.CompilerParams(has_side_effects=SPLIT_EFFECT),
    )(*grads, *zones, send_sems, recv_sems, after)
    return out[:n], out[n:]


def sum_partials(g, land, where, *, name):
    _, half, cols = land.shape
    tr = GRAD_ROWS if half % GRAD_ROWS == 0 else half
    nb = half // tr

    def body(where_ref, g_ref, *rest):
        o_ref = rest[-1]
        acc = g_ref[0].astype(f32)
        for l_ref in rest[:-1]:
            acc = acc + l_ref[0].astype(f32)
        o_ref[...] = acc

    blk = (1, tr, cols)
    slot = lambda j: pl.BlockSpec(blk, lambda i, where_ref: (where_ref[2 + j], i, 0))
    return pl.pallas_call(
        body, name=name,
        grid_spec=pltpu.PrefetchScalarGridSpec(
            num_scalar_prefetch=1, grid=(nb,),
            in_specs=[pl.BlockSpec(blk, lambda i, where_ref: (where_ref[0], where_ref[1] * nb + i, 0))]
            + [slot(j) for j in range(N_DEV - 1)],
            out_specs=pl.BlockSpec((tr, cols), lambda i, where_ref: (i, 0))),
        out_shape=jax.ShapeDtypeStruct((half, cols), f32),
        compiler_params=_cparams("parallel"),
    )(where, g, *([land] * (N_DEV - 1)))


def swap_reduced_halves(mine, *, name):
    n = len(mine)

    def body(*refs):
        r, out, send_sems, recv_sems = refs[:n], refs[n:2 * n], refs[2 * n], refs[2 * n + 1]
        x, y, c = _place()
        copies = [_remote(r[a], out[a], send_sems.at[a], recv_sems.at[a], (x, y, 1 - c)) for a in range(n)]
        for cp in copies:
            cp.start()
        for cp in copies:
            cp.wait()

    return pl.pallas_call(
        body, name=name, in_specs=[HBM_SPEC] * n, out_specs=[HBM_SPEC] * n,
        out_shape=[jax.ShapeDtypeStruct(r.shape, r.dtype) for r in mine],
        scratch_shapes=[pltpu.SemaphoreType.DMA((n,)), pltpu.SemaphoreType.DMA((n,))],
    )(*mine)


def adamw_halves(w, mine, theirs, m, v, core, *, name):
    R, C = w.shape
    tr = min(GRAD_ROWS, R // 2)
    half_nb = R // 2 // tr

    def body(c_ref, w_ref, a_ref, b_ref, m_ref, v_ref, g_ref, d_ref, nm_ref, nv_ref):
        low = pl.program_id(0) < half_nb
        gv = jnp.where(low == (c_ref[0] == 0), a_ref[...], b_ref[...])
        nm = ADAM_B1 * m_ref[...] + (1.0 - ADAM_B1) * gv
        nv = ADAM_B2 * v_ref[...] + (1.0 - ADAM_B2) * jnp.square(gv)
        m_hat = nm / (1.0 - ADAM_B1 ** ADAM_STEP)
        v_hat = nv / (1.0 - ADAM_B2 ** ADAM_STEP)
        g_ref[...] = gv
        d_ref[...] = -ADAM_LR * (m_hat / (jnp.sqrt(v_hat) + ADAM_EPS) + ADAM_WD * w_ref[...])
        nm_ref[...] = nm
        nv_ref[...] = nv

    full = pl.BlockSpec((tr, C), lambda i, c_ref: (i, 0))
    part = pl.BlockSpec((tr, C), lambda i, c_ref: (i % half_nb, 0))
    out = jax.ShapeDtypeStruct((R, C), f32)
    return pl.pallas_call(
        body, name=name,
        grid_spec=pltpu.PrefetchScalarGridSpec(
            num_scalar_prefetch=1, grid=(2 * half_nb,), in_specs=[full, part, part, full, full], out_specs=[full] * 4),
        out_shape=[out] * 4, compiler_params=_cparams("parallel"),
    )(core, w, mine, theirs, m, v)


N_DEV = 8


def all_reduce_small(v):
    def body(src_ref, out_ref, land_ref, send_sems, recv_sems):
        x, y, c = _place()
        me = 4 * x + 2 * y + c
        copies = []
        for r in range(1, N_DEV):
            peer = ((1 - x) if r & 4 else x, (1 - y) if r & 2 else y, (1 - c) if r & 1 else c)
            copies.append(_remote(src_ref, land_ref.at[r], send_sems.at[r - 1], recv_sems.at[r - 1], peer))
        for cp in copies:
            cp.start()
        land_ref[0] = src_ref[...]
        for cp in copies:
            cp.wait()
        acc = land_ref[me]
        for d in range(1, N_DEV):
            acc = acc + land_ref[jnp.bitwise_xor(me, d)]
        out_ref[...] = acc

    vm = pl.BlockSpec(memory_space=pltpu.VMEM)
    return pl.pallas_call(
        body, name="all_reduce_small", in_specs=[vm], out_specs=vm,
        out_shape=jax.ShapeDtypeStruct(v.shape, v.dtype),
        scratch_shapes=[pltpu.VMEM((N_DEV,) + v.shape, v.dtype),
                        pltpu.SemaphoreType.DMA((N_DEV - 1,)), pltpu.SemaphoreType.DMA((N_DEV - 1,))],
    )(v)


def adamw(w, g, m, v, *, name, tr=None, tc=None):
    R, C = w.shape
    if tc is None:
        tr, tc = min(tr, R), C
        blk = pl.BlockSpec((tr, C), lambda i: (i, 0))
    else:
        tr = R
        blk = pl.BlockSpec((R, tc), lambda i: (0, i))

    def body(w_ref, g_ref, m_ref, v_ref, d_ref, nm_ref, nv_ref):
        gv = g_ref[...]
        nm = ADAM_B1 * m_ref[...] + (1.0 - ADAM_B1) * gv
        nv = ADAM_B2 * v_ref[...] + (1.0 - ADAM_B2) * jnp.square(gv)
        m_hat = nm / (1.0 - ADAM_B1 ** ADAM_STEP)
        v_hat = nv / (1.0 - ADAM_B2 ** ADAM_STEP)
        d_ref[...] = -ADAM_LR * (m_hat / (jnp.sqrt(v_hat) + ADAM_EPS) + ADAM_WD * w_ref[...])
        nm_ref[...] = nm
        nv_ref[...] = nv

    out = jax.ShapeDtypeStruct((R, C), f32)
    return pl.pallas_call(
        body, name=name, grid=((R // tr) * (C // tc),), in_specs=[blk] * 4, out_specs=[blk] * 3, out_shape=[out] * 3,
        compiler_params=_cparams("parallel"),
    )(w, g, m, v)


BIG_SHARDS = (("w_in", (1024, 900), True), ("w_out", (256, 1024), False), ("w_cq", (256, 512), False),
              ("w_ckv", (256, 1024), False), ("w_co", (512, 256), True), ("w_mlp1", (1024, 1024), True),
              ("w_mlp2", (1024, 1024), False))
CONV_SHARD = (CONV_WIDTH, 3 * GDN_WIDTH // N_CHIPS)
SMALL_DIMS = (("norm_mix_g", 1024), ("fox_qnorm_g", 64), ("fox_knorm_g", 64), ("fox_f_bias", 8), ("fox_onorm_g", 64),
              ("gdn_A_log", 4), ("gdn_dt_bias", 4), ("gdn_onorm_g", 128), ("norm_xattn_g", 1024), ("mem_norm_g", 1024),
              ("xattn_qnorm_g", 128), ("xattn_knorm_g", 128), ("norm_mlp_g", 1024))
WEIGHT_ORDER = ("norm_mix_g", "w_in", "fox_qnorm_g", "fox_knorm_g", "fox_f_bias", "fox_onorm_g", "gdn_conv_w", "gdn_A_log",
                "gdn_dt_bias", "gdn_onorm_g", "w_out", "norm_xattn_g", "mem_norm_g", "w_cq", "w_ckv", "xattn_qnorm_g",
                "xattn_knorm_g", "w_co", "norm_mlp_g", "w_mlp1", "w_mlp2")


def _pack_rows(pieces, rows, lead=()):
    cat = jnp.concatenate([p.reshape(lead + (-1,)) for p in pieces], axis=-1)
    cat = jnp.pad(cat, [(0, 0)] * len(lead) + [(0, rows * LANES - cat.shape[-1])])
    return cat.reshape(lead + (rows, LANES))


def _unpack_rows(buf, sizes, lead=()):
    flat = buf.reshape(lead + (-1,))
    out, off = [], 0
    for n in sizes:
        out.append(flat[..., off:off + n])
        off += n
    return out


def _conv_to_wire(conv):
    return lax.bitcast_convert_type(conv, bf16)


def _conv_from_wire(wire):
    return lax.bitcast_convert_type(wire, f32)


SMALL_ROWS = 96
SMALL_ADAM_ROWS = 56


def kernel(x, mem, norm_mix_g, w_in, fox_qnorm_g, fox_knorm_g, fox_f_bias, fox_onorm_g, gdn_conv_w, gdn_A_log, gdn_dt_bias, gdn_onorm_g, w_out, norm_xattn_g, mem_norm_g, w_cq, w_ckv, xattn_qnorm_g, xattn_knorm_g, w_co, norm_mlp_g, w_mlp1, w_mlp2, loss_target, m_norm_mix_g, m_w_in, m_fox_qnorm_g, m_fox_knorm_g, m_fox_f_bias, m_fox_onorm_g, m_gdn_conv_w, m_gdn_A_log, m_gdn_dt_bias, m_gdn_onorm_g, m_w_out, m_norm_xattn_g, m_mem_norm_g, m_w_cq, m_w_ckv, m_xattn_qnorm_g, m_xattn_knorm_g, m_w_co, m_norm_mlp_g, m_w_mlp1, m_w_mlp2, v_norm_mix_g, v_w_in, v_fox_qnorm_g, v_fox_knorm_g, v_fox_f_bias, v_fox_onorm_g, v_gdn_conv_w, v_gdn_A_log, v_gdn_dt_bias, v_gdn_onorm_g, v_w_out, v_norm_xattn_g, v_mem_norm_g, v_w_cq, v_w_ckv, v_xattn_qnorm_g, v_xattn_knorm_g, v_w_co, v_norm_mlp_g, v_w_mlp1, v_w_mlp2):
    wts = dict(norm_mix_g=norm_mix_g, w_in=w_in, fox_qnorm_g=fox_qnorm_g, fox_knorm_g=fox_knorm_g, fox_f_bias=fox_f_bias,
               fox_onorm_g=fox_onorm_g, gdn_conv_w=gdn_conv_w, gdn_A_log=gdn_A_log, gdn_dt_bias=gdn_dt_bias,
               gdn_onorm_g=gdn_onorm_g, w_out=w_out, norm_xattn_g=norm_xattn_g, mem_norm_g=mem_norm_g, w_cq=w_cq, w_ckv=w_ckv,
               xattn_qnorm_g=xattn_qnorm_g, xattn_knorm_g=xattn_knorm_g, w_co=w_co, norm_mlp_g=norm_mlp_g, w_mlp1=w_mlp1,
               w_mlp2=w_mlp2)
    mom = dict(norm_mix_g=m_norm_mix_g, w_in=m_w_in, fox_qnorm_g=m_fox_qnorm_g, fox_knorm_g=m_fox_knorm_g,
               fox_f_bias=m_fox_f_bias, fox_onorm_g=m_fox_onorm_g, gdn_conv_w=m_gdn_conv_w, gdn_A_log=m_gdn_A_log,
               gdn_dt_bias=m_gdn_dt_bias, gdn_onorm_g=m_gdn_onorm_g, w_out=m_w_out, norm_xattn_g=m_norm_xattn_g,
               mem_norm_g=m_mem_norm_g, w_cq=m_w_cq, w_ckv=m_w_ckv, xattn_qnorm_g=m_xattn_qnorm_g,
               xattn_knorm_g=m_xattn_knorm_g, w_co=m_w_co, norm_mlp_g=m_norm_mlp_g, w_mlp1=m_w_mlp1, w_mlp2=m_w_mlp2)
    var = dict(norm_mix_g=v_norm_mix_g, w_in=v_w_in, fox_qnorm_g=v_fox_qnorm_g, fox_knorm_g=v_fox_knorm_g,
               fox_f_bias=v_fox_f_bias, fox_onorm_g=v_fox_onorm_g, gdn_conv_w=v_gdn_conv_w, gdn_A_log=v_gdn_A_log,
               gdn_dt_bias=v_gdn_dt_bias, gdn_onorm_g=v_gdn_onorm_g, w_out=v_w_out, norm_xattn_g=v_norm_xattn_g,
               mem_norm_g=v_mem_norm_g, w_cq=v_w_cq, w_ckv=v_w_ckv, xattn_qnorm_g=v_xattn_qnorm_g,
               xattn_knorm_g=v_xattn_knorm_g, w_co=v_w_co, norm_mlp_g=v_norm_mlp_g, w_mlp1=v_w_mlp1, w_mlp2=v_w_mlp2)
    B, S, D = x.shape
    T = B * S
    big_names = [n for n, _, _ in BIG_SHARDS]
    chip = 2 * lax.axis_index("x") + lax.axis_index("y")
    core = lax.axis_index("c").astype(jnp.int32).reshape(1)

    shards = {n: wts[n][0].astype(MXU_DTYPE) for n in big_names[1:]}
    in_t = lambda p: jnp.swapaxes(p[0], 0, 1)
    shards["w_in"] = jnp.pad(in_t(w_in).astype(MXU_DTYPE), ((0, IN_SHARD_PAD - IN_SHARD), (0, 0)))
    w_in_all, conv_all = gather_weights([shards["w_in"]], gdn_conv_w[0])
    late = big_names[1:]
    send_sems, recv_sems, late_src, late_zones, token = gather_weights_start([shards[n] for n in late], conv_all)
    own = lambda g, s: lax.dynamic_update_slice(g, s[None], (chip,) + (0,) * s.ndim)
    full = {"w_in": own(w_in_all, shards["w_in"])}
    conv_full = own(conv_all, gdn_conv_w[0]).transpose(1, 0, 2).reshape(CONV_WIDTH, 3 * GDN_WIDTH)
    rows = lambda g: g.reshape(N_CHIPS * g.shape[1], g.shape[2])

    def late_weights(after):
        zones = gather_weights_wait(send_sems, recv_sems, late_src, late_zones, after)
        got = {n: own(z, shards[n]) for n, z in zip(late, zones)}
        return dict(w_out=rows(got["w_out"]), w_cq=rows(got["w_cq"]), w_ckv=rows(got["w_ckv"]), w_co=got["w_co"],
                    w_mlp1=got["w_mlp1"], w_mlp2=rows(got["w_mlp2"]))

    in_flight = []

    def grads_ready(ready):
        names = list(ready)
        *started, tok = reduce_grads_start([ready[n] for n in names], name="reduce_grads_start_%d" % len(in_flight))
        in_flight.append((names, *started))
        return tok

    w_in_t = full["w_in"][:, :IN_SHARD].reshape(IN_DIM, D_MODEL)
    w = dict(wa_t=align_w_in_t(w_in_t), conv_w=conv_full, late=late_weights, grads_ready=grads_ready)
    sp = {n: wts[n] for n, _ in SMALL_DIMS}
    sp["norm_mix_g"] = sp["norm_mix_g"] + token[0, 0]

    loss_part, grad_x, g_big, g_small = local_step(x.reshape(T, D), mem.reshape(-1, D), loss_target.reshape(T, D), w, sp, B=B)

    small_pieces = [g_small[n] for n, _ in SMALL_DIMS] + [g_small["gdn_conv_w"], loss_part]
    small_sizes = [d for _, d in SMALL_DIMS] + [CONV_WIDTH * 3 * GDN_WIDTH, LANES]
    red_small = _unpack_rows(all_reduce_small(_pack_rows(small_pieces, SMALL_ROWS)), small_sizes)
    grads = {n: p.reshape(1, d) for (n, d), p in zip(SMALL_DIMS, red_small)}
    conv_grad = lax.dynamic_slice(red_small[-2].reshape(CONV_WIDTH, 3 * GDN_WIDTH), (0, chip * CONV_SHARD[1]), CONV_SHARD)
    grads["gdn_conv_w"] = conv_grad.reshape((1,) + CONV_SHARD)
    loss = red_small[-1][0]

    parts, zones = {}, {}

    def wait_group(k, after):
        names, send_sems, recv_sems, thru, land = in_flight[k]
        thru, land = reduce_grads_wait(send_sems, recv_sems, thru, land, after, name="reduce_grads_wait_%d" % k)
        parts.update(zip(names, thru))
        zones.update(zip(names, land))

    wait_group(0, grad_x)
    wait_group(1, grad_x)
    dev = 2 * chip + core[0]
    where = jnp.stack([chip, core[0]] + [dev ^ r for r in range(1, N_DEV)]).astype(jnp.int32)
    mine = [sum_partials(parts[n], zones[n], where, name="sum_partials_" + n) for n in late]
    theirs = swap_reduced_halves(mine, name="swap_reduced_halves")

    delta, new_m, new_v = {}, {}, {}
    for n, a, b in zip(late, mine, theirs):
        g, d, nm, nv = adamw_halves(wts[n][0], a, b, mom[n][0], var[n][0], core, name="adamw_" + n)
        grads[n], delta[n], new_m[n], new_v[n] = g[None], d[None], nm[None], nv[None]
    wait_group(2, new_v[late[-1]])
    mine_in = sum_partials(parts["w_in"], zones["w_in"], where, name="sum_partials_w_in")
    (theirs_in,) = swap_reduced_halves([mine_in], name="swap_reduced_halves_w_in")
    south = core[0] == 0
    g_in_t = jnp.concatenate([jnp.where(south, mine_in, theirs_in), jnp.where(south, theirs_in, mine_in)])[:IN_SHARD]
    back = lambda t: jnp.swapaxes(t, 0, 1)[None]
    d, nm, nv = adamw(in_t(w_in), g_in_t, in_t(m_w_in), in_t(v_w_in), name="adamw_w_in", tc=256)
    grads["w_in"], delta["w_in"], new_m["w_in"], new_v["w_in"] = back(g_in_t), back(d), back(nm), back(nv)
    small_names = [n for n, _ in SMALL_DIMS] + ["gdn_conv_w"]
    small_sz = [d for _, d in SMALL_DIMS] + [CONV_SHARD[0] * CONV_SHARD[1]]
    packed4 = [_pack_rows([src[n] for n in small_names], SMALL_ADAM_ROWS) for src in (wts, grads, mom, var)]
    outs = adamw(*packed4, name="adamw_small", tr=SMALL_ADAM_ROWS)
    for dst, buf in zip((delta, new_m, new_v), outs):
        for n, p in zip(small_names, _unpack_rows(buf, small_sz)):
            dst[n] = p.reshape(wts[n].shape)

    return (loss, grad_x.reshape(B, S, D), *[grads[n] for n in WEIGHT_ORDER], *[delta[n] for n in WEIGHT_ORDER],
            *[new_m[n] for n in WEIGHT_ORDER], *[new_v[n] for n in WEIGHT_ORDER])
```

```python
import functools

import jax
import jax.numpy as jnp
import numpy as np
from jax import lax
from jax.experimental import pallas as pl
from jax.experimental.pallas import tpu as pltpu

f32 = jnp.float32
bf16 = jnp.bfloat16
MXU_DTYPE = jnp.bfloat16
WIRE_DTYPE = jnp.bfloat16
INV_PRECISION = lax.Precision.HIGH

D_MODEL = 1024
FOX_HEADS = 8
FOX_HEAD_DIM = 64
FOX_WIDTH = 512
GDN_HEADS = 4
GDN_HEAD_DIM = 128
GDN_WIDTH = 512
CONV_WIDTH = 4
GDN_CHUNK = 64
XATTN_HEADS = 4
XATTN_HEAD_DIM = 128
XATTN_WIDTH = 512
D_FF = 4096
IN_DIM = 3600
EPS = 1e-6
NEG_INF = -1e30
LANES = 128
ADAM_LR = 0.001
ADAM_B1 = 0.9
ADAM_B2 = 0.999
ADAM_EPS = 1e-08
ADAM_WD = 0.01
ADAM_STEP = 10
VMEM_LIMIT = 48 * 1024 * 1024

COL_FOX = 0
COL_GDN = 1536
COL_Z = 3072
COL_SMALL = 3584
IN_ALIGNED = 3840
IN_TILE = 768
SM_F = 0
SM_B = 8
SM_A = 12


def _cparams(*sem):
    return pltpu.CompilerParams(dimension_semantics=sem, vmem_limit_bytes=VMEM_LIMIT)


def _mx(v):
    return v.astype(MXU_DTYPE)


def _dot(a, b, dims, precision=None):
    return lax.dot_general(a, b, (dims, ((), ())), preferred_element_type=f32, precision=precision)


def _dotm(a, b, dims):
    return _dot(_mx(a), _mx(b), dims)


NN = ((1,), (0,))
NT = ((1,), (1,))
TN = ((0,), (0,))


def matmul(a, b, *, name, ta=False, tb=False, b_stacked=False, out_stacked=False, residual=None, relu2_out=False,
           relu2_bwd_aux=None, out_dtype=f32, tm=1024, tn=1024, tk=1024):
    M, K = (a.shape[1], a.shape[0]) if ta else a.shape
    if b_stacked:
        b_cols = b.shape[2]
        N, tk = (b.shape[1], min(tk, b_cols)) if tb else (N_CHIPS * b_cols, tk)
        tn = tn if tb else min(tn, b_cols)
        assert K == (N_CHIPS * b_cols if tb else b.shape[1]), (name, a.shape, b.shape)
    else:
        N = b.shape[0] if tb else b.shape[1]
    if out_stacked:
        tn = min(tn, N // N_CHIPS)
    tm, tn, tk = min(tm, M), min(tn, N), min(tk, K)
    assert M % tm == 0 and N % tn == 0 and K % tk == 0, (name, M, N, K)
    nk = K // tk
    has_res = residual is not None
    has_aux = relu2_bwd_aux is not None

    def body(*refs):
        a_ref, b_ref = refs[0], refs[1]
        pos = 2
        res_ref = aux_ref = None
        if has_res:
            res_ref = refs[pos]
            pos += 1
        if has_aux:
            aux_ref = refs[pos]
            pos += 1
        o_ref = refs[pos]
        k = pl.program_id(2)
        dims = ((0,) if ta else (1,), (1,) if tb else (0,))
        part = _dot(_mx(a_ref[...]), _mx(b_ref[...]), dims)

        def finish(r):
            if has_res:
                r = r + res_ref[...]
            if has_aux:
                r = r * (2.0 * jnp.sqrt(aux_ref[...].astype(f32)))
            if relu2_out:
                o_ref[...] = jnp.square(jnp.maximum(r, 0.0)).astype(o_ref.dtype)
            else:
                o_ref[...] = r.astype(o_ref.dtype)

        if nk == 1:
            finish(part)
            return
        acc_ref = refs[pos + 1]

        @pl.when(k == 0)
        def _():
            acc_ref[...] = part

        @pl.when((k > 0) & (k < nk - 1))
        def _():
            acc_ref[...] += part

        @pl.when(k == nk - 1)
        def _():
            finish(acc_ref[...] + part)

    a_spec = pl.BlockSpec((tk, tm), lambda i, j, k: (k, i)) if ta else pl.BlockSpec((tm, tk), lambda i, j, k: (i, k))
    if b_stacked and tb:
        per = b_cols // tk
        b_spec = pl.BlockSpec((None, tn, tk), lambda i, j, k: (k // per, j, k % per))
    elif b_stacked:
        per = b_cols // tn
        b_spec = pl.BlockSpec((None, tk, tn), lambda i, j, k: (j // per, k, j % per))
    else:
        b_spec = pl.BlockSpec((tn, tk), lambda i, j, k: (j, k)) if tb else pl.BlockSpec((tk, tn), lambda i, j, k: (k, j))
    if out_stacked:
        assert not (has_res or has_aux or relu2_out), name
        per_o = N // N_CHIPS // tn
        o_spec = pl.BlockSpec((None, tm, tn), lambda i, j, k: (j // per_o, i, j % per_o))
        out_full = (N_CHIPS, M, N // N_CHIPS)
    else:
        o_spec = pl.BlockSpec((tm, tn), lambda i, j, k: (i, j))
        out_full = (M, N)
    in_specs, args = [a_spec, b_spec], [a, b]
    if has_res:
        in_specs.append(o_spec)
        args.append(residual)
    if has_aux:
        in_specs.append(o_spec)
        args.append(relu2_bwd_aux)
    out_shape = [jax.ShapeDtypeStruct(out_full, out_dtype)]
    out_specs = [o_spec]
    res = pl.pallas_call(
        body, name=name, grid=(M // tm, N // tn, nk), in_specs=in_specs, out_specs=out_specs, out_shape=out_shape,
        scratch_shapes=[pltpu.VMEM((tm, tn), f32)] if nk > 1 else [],
        compiler_params=_cparams("parallel", "parallel", "arbitrary"),
    )(*args)
    return res[0]


def matmul_rows(a, b, extras, *, name, mode, tb=False, b_stacked=False, tm=1024, tk=1024):
    M, K = a.shape
    N = D_MODEL
    if b_stacked:
        assert tb, name
        tk = min(tk, b.shape[2])
        per = b.shape[2] // tk
        b_spec = pl.BlockSpec((None, N, tk), lambda i, k: (k // per, 0, k % per))
    elif tb:
        tk = min(tk, K)
        b_spec = pl.BlockSpec((N, tk), lambda i, k: (0, k))
    else:
        tk = min(tk, K)
        b_spec = pl.BlockSpec((tk, N), lambda i, k: (k, 0))
    tm = min(tm, M)
    assert M % tm == 0 and K % tk == 0, (name, M, K)
    nk = K // tk
    extras = [e for e in extras if e is not None]
    n_ex = len(extras)

    def body(*refs):
        a_ref, b_ref = refs[0], refs[1]
        ex = refs[2:2 + n_ex]
        o_ref = refs[2 + n_ex]
        n_out = 3 if mode == "loss" else 2
        s_ref = refs[1 + n_ex + n_out]
        i, k = pl.program_id(0), pl.program_id(1)
        part = _dot(_mx(a_ref[...]), _mx(b_ref[...]), ((1,), (1,) if tb else (0,)))

        def finish(y):
            @pl.when(i == 0)
            def _():
                s_ref[...] = jnp.zeros_like(s_ref)

            if mode == "rms_bwd":
                xv, gv = ex[0][...], ex[1][...]
                rstd = lax.rsqrt(jnp.mean(xv * xv, axis=-1, keepdims=True) + EPS)
                xhat = xv * rstd
                gd = y * gv
                dx = rstd * (gd - xhat * jnp.mean(gd * xhat, axis=-1, keepdims=True))
                o_ref[...] = dx + ex[2][...] if n_ex == 3 else dx
                s_ref[...] += jnp.sum(y * xhat, axis=0, keepdims=True)
            else:
                e = y + ex[0][...] - ex[1][...]
                o_ref[...] = e * (1.0 / N)
                refs[3 + n_ex][...] = (e * (1.0 / N)).astype(MXU_DTYPE)
                tot = 0.5 * jnp.sum(jnp.mean(e * e, axis=-1, keepdims=True), axis=0, keepdims=True)
                s_ref[...] += jnp.broadcast_to(tot, s_ref.shape)

        if nk == 1:
            finish(part)
            return
        acc_ref = refs[2 + n_ex + n_out]

        @pl.when(k == 0)
        def _():
            acc_ref[...] = part

        @pl.when((k > 0) & (k < nk - 1))
        def _():
            acc_ref[...] += part

        @pl.when(k == nk - 1)
        def _():
            finish(acc_ref[...] + part)

    row = pl.BlockSpec((tm, N), lambda i, k: (i, 0))
    vec = pl.BlockSpec((1, N), lambda i, k: (0, 0))
    if mode == "rms_bwd":
        ex_specs = [row, vec] + ([row] if n_ex == 3 else [])
        s_shape, s_spec = jax.ShapeDtypeStruct((1, N), f32), vec
    else:
        ex_specs = [row, row]
        s_shape, s_spec = jax.ShapeDtypeStruct((1, LANES), f32), pl.BlockSpec((1, LANES), lambda i, k: (0, 0))
    return pl.pallas_call(
        body, name=name, grid=(M // tm, nk),
        in_specs=[pl.BlockSpec((tm, tk), lambda i, k: (i, k)), b_spec] + ex_specs,
        out_specs=[row] * (2 if mode == "loss" else 1) + [s_spec],
        out_shape=[jax.ShapeDtypeStruct((M, N), f32)] + ([jax.ShapeDtypeStruct((M, N), MXU_DTYPE)] if mode == "loss" else [])
        + [s_shape],
        scratch_shapes=[pltpu.VMEM((tm, N), f32)] if nk > 1 else [],
        compiler_params=_cparams("arbitrary", "arbitrary"),
    )(a, b, *extras)


def rms_fwd(x, g, *, name, tr=512):
    R, D = x.shape
    tr = min(tr, R)

    def body(x_ref, g_ref, o_ref):
        xv = x_ref[...]
        y = xv * lax.rsqrt(jnp.mean(xv * xv, axis=-1, keepdims=True) + EPS)
        o_ref[...] = (y * g_ref[...]).astype(o_ref.dtype)

    return pl.pallas_call(
        body, name=name, grid=(R // tr,),
        in_specs=[pl.BlockSpec((tr, D), lambda i: (i, 0)), pl.BlockSpec((1, D), lambda i: (0, 0))],
        out_specs=pl.BlockSpec((tr, D), lambda i: (i, 0)),
        out_shape=jax.ShapeDtypeStruct((R, D), MXU_DTYPE),
        compiler_params=_cparams("parallel"),
    )(x, g)


def rms_bwd(x, g, dh, residual, *, name, tr=512):
    R, D = x.shape
    tr = min(tr, R)
    has_res = residual is not None

    def body(*refs):
        if has_res:
            x_ref, g_ref, dh_ref, res_ref, dx_ref, dg_ref = refs
        else:
            x_ref, g_ref, dh_ref, dx_ref, dg_ref = refs
        xv = x_ref[...]
        rstd = lax.rsqrt(jnp.mean(xv * xv, axis=-1, keepdims=True) + EPS)
        xhat = xv * rstd
        dh = dh_ref[...].astype(f32)
        gd = dh * g_ref[...]
        dx = rstd * (gd - xhat * jnp.mean(gd * xhat, axis=-1, keepdims=True))
        if has_res:
            dx = dx + res_ref[...]
        dx_ref[...] = dx

        @pl.when(pl.program_id(0) == 0)
        def _():
            dg_ref[...] = jnp.zeros_like(dg_ref)

        dg_ref[...] += jnp.sum(dh * xhat, axis=0, keepdims=True)

    row = pl.BlockSpec((tr, D), lambda i: (i, 0))
    vec = pl.BlockSpec((1, D), lambda i: (0, 0))
    in_specs = [row, vec, row] + ([row] if has_res else [])
    args = [x, g, dh] + ([residual] if has_res else [])
    return pl.pallas_call(
        body, name=name, grid=(R // tr,), in_specs=in_specs, out_specs=[row, vec],
        out_shape=[jax.ShapeDtypeStruct((R, D), f32), jax.ShapeDtypeStruct((1, D), f32)],
        compiler_params=_cparams("arbitrary"),
    )(*args)


def loss_head(y, target, *, tr=512):
    R, D = y.shape
    tr = min(tr, R)

    def body(y_ref, t_ref, dy_ref, loss_ref):
        e = y_ref[...] - t_ref[...]
        dy_ref[...] = e * (1.0 / D)

        @pl.when(pl.program_id(0) == 0)
        def _():
            loss_ref[...] = jnp.zeros_like(loss_ref)

        part = 0.5 * jnp.sum(jnp.mean(e * e, axis=-1, keepdims=True), axis=0, keepdims=True)
        loss_ref[...] += jnp.broadcast_to(part, loss_ref.shape)

    row = pl.BlockSpec((tr, D), lambda i: (i, 0))
    return pl.pallas_call(
        body, name="loss_head", grid=(R // tr,), in_specs=[row, row],
        out_specs=[row, pl.BlockSpec((1, LANES), lambda i: (0, 0))],
        out_shape=[jax.ShapeDtypeStruct((R, D), f32), jax.ShapeDtypeStruct((1, LANES), f32)],
        compiler_params=_cparams("arbitrary"),
    )(y, target)


def _head_rms(v, g):
    r = lax.rsqrt(jnp.mean(v * v, axis=-1, keepdims=True) + EPS)
    return v * r * g, r


def _head_rms_bwd(v, r, g, dn):
    vhat = v * r
    gd = dn * g
    dv = r * (gd - vhat * jnp.mean(gd * vhat, axis=-1, keepdims=True))
    return dv, jnp.sum(dn * vhat, axis=0, keepdims=True)


def _softmax_rows(s):
    m = jnp.max(s, axis=-1, keepdims=True)
    e = jnp.exp(s - m)
    return e / jnp.sum(e, axis=-1, keepdims=True)


def xattn_fwd(cq, ckv, gq, gk, *, B, tq=512):
    T = cq.shape[0]
    S = T // B
    M = ckv.shape[0] // B
    tq = min(tq, S)
    nq = S // tq
    hd, W = XATTN_HEAD_DIM, XATTN_WIDTH
    scale = hd ** -0.5

    def body(q_ref, k_ref, v_ref, gq_ref, gk_ref, o_ref):
        for h in range(XATTN_HEADS):
            sl = slice(h * hd, (h + 1) * hd)
            qn, _ = _head_rms(q_ref[:, sl], gq_ref[...])
            kn, _ = _head_rms(k_ref[:, sl], gk_ref[...])
            p = _softmax_rows(_dot(_mx(qn), _mx(kn), NT) * scale)
            o_ref[:, sl] = _dot(_mx(p), _mx(v_ref[:, sl]), NN).astype(o_ref.dtype)

    vec = pl.BlockSpec((1, hd), lambda b, i: (0, 0))
    qspec = pl.BlockSpec((tq, W), lambda b, i: (b * nq + i, 0))
    return pl.pallas_call(
        body, name="xattn_fwd", grid=(B, nq),
        in_specs=[qspec, pl.BlockSpec((M, W), lambda b, i: (b, 0)), pl.BlockSpec((M, W), lambda b, i: (b, 1)), vec, vec],
        out_specs=qspec, out_shape=jax.ShapeDtypeStruct((T, W), MXU_DTYPE),
        compiler_params=_cparams("parallel", "parallel"),
    )(cq, ckv, ckv, gq, gk)


def xattn_bwd(cq, ckv, gq, gk, dco, *, B, tq=512):
    T = cq.shape[0]
    S = T // B
    M = ckv.shape[0] // B
    tq = min(tq, S)
    nq = S // tq
    hd, W = XATTN_HEAD_DIM, XATTN_WIDTH
    scale = hd ** -0.5

    def body(q_ref, k_ref, v_ref, gq_ref, gk_ref, do_ref, dq_ref, dkv_ref, dgq_ref, dgk_ref, dkn_acc, dv_acc):
        b, i = pl.program_id(0), pl.program_id(1)

        @pl.when((b == 0) & (i == 0))
        def _():
            dgq_ref[...] = jnp.zeros_like(dgq_ref)
            dgk_ref[...] = jnp.zeros_like(dgk_ref)

        @pl.when(i == 0)
        def _():
            dkn_acc[...] = jnp.zeros_like(dkn_acc)
            dv_acc[...] = jnp.zeros_like(dv_acc)

        gqv, gkv = gq_ref[...], gk_ref[...]
        for h in range(XATTN_HEADS):
            sl = slice(h * hd, (h + 1) * hd)
            q, k, v = q_ref[:, sl], k_ref[:, sl], v_ref[:, sl]
            qn, rq = _head_rms(q, gqv)
            kn, _ = _head_rms(k, gkv)
            p = _softmax_rows(_dot(_mx(qn), _mx(kn), NT) * scale)
            do = do_ref[:, sl]
            dv_acc[:, sl] += _dot(_mx(p), _mx(do), TN)
            dp = _dot(_mx(do), _mx(v), NT)
            ds = p * (dp - jnp.sum(dp * p, axis=-1, keepdims=True)) * scale
            dqn = _dot(_mx(ds), _mx(kn), NN)
            dkn_acc[:, sl] += _dot(_mx(ds), _mx(qn), TN)
            dq, dgq = _head_rms_bwd(q, rq, gqv, dqn)
            dq_ref[:, sl] = dq.astype(dq_ref.dtype)
            dgq_ref[...] += dgq

        @pl.when(i == nq - 1)
        def _():
            for h in range(XATTN_HEADS):
                sl = slice(h * hd, (h + 1) * hd)
                k = k_ref[:, sl]
                rk = lax.rsqrt(jnp.mean(k * k, axis=-1, keepdims=True) + EPS)
                dk, dgk = _head_rms_bwd(k, rk, gkv, dkn_acc[:, sl])
                dkv_ref[:, sl] = dk.astype(dkv_ref.dtype)
                dkv_ref[:, slice(W + h * hd, W + (h + 1) * hd)] = dv_acc[:, sl].astype(dkv_ref.dtype)
                dgk_ref[...] += dgk

    vec = pl.BlockSpec((1, hd), lambda b, i: (0, 0))
    qspec = pl.BlockSpec((tq, W), lambda b, i: (b * nq + i, 0))
    return pl.pallas_call(
        body, name="xattn_bwd", grid=(B, nq),
        in_specs=[qspec, pl.BlockSpec((M, W), lambda b, i: (b, 0)), pl.BlockSpec((M, W), lambda b, i: (b, 1)), vec, vec, qspec],
        out_specs=[qspec, pl.BlockSpec((M, 2 * W), lambda b, i: (b, 0)), vec, vec],
        out_shape=[jax.ShapeDtypeStruct((T, W), MXU_DTYPE), jax.ShapeDtypeStruct((B * M, 2 * W), MXU_DTYPE),
                   jax.ShapeDtypeStruct((1, hd), f32), jax.ShapeDtypeStruct((1, hd), f32)],
        scratch_shapes=[pltpu.VMEM((M, W), f32), pltpu.VMEM((M, W), f32)],
        compiler_params=_cparams("arbitrary", "arbitrary"),
    )(cq, ckv, ckv, gq, gk, dco)


FOX_PAIRS = FOX_HEADS // 2


def _fox_scores(qn, kn, ccol, crow, q0, tq, S, scale):
    s = _dot(_mx(qn), _mx(kn), NT) * scale + ccol - crow
    qpos = q0 + lax.broadcasted_iota(jnp.int32, (tq, S), 0)
    kpos = lax.broadcasted_iota(jnp.int32, (tq, S), 1)
    return jnp.where(kpos <= qpos, s, NEG_INF)


def fox_fwd(P, ccol, crow, gq, gk, go, *, B, tq=256):
    T = P.shape[0]
    S = T // B
    tq = min(tq, S)
    nq = S // tq
    hd = FOX_HEAD_DIM
    scale = hd ** -0.5

    def body(q_ref, k_ref, v_ref, ccol_ref, crow_ref, gq_ref, gk_ref, go_ref, o_ref, oa_ref):
        q0 = pl.program_id(2) * tq
        for e in range(2):
            sl = slice(e * hd, (e + 1) * hd)
            qn, _ = _head_rms(q_ref[:, sl], gq_ref[:, sl])
            kn, _ = _head_rms(k_ref[:, sl], gk_ref[:, sl])
            p = _softmax_rows(_fox_scores(qn, kn, ccol_ref[0, e], crow_ref[0, e], q0, tq, S, scale))
            o = _dot(_mx(p), _mx(v_ref[:, sl]), NN)
            o_ref[:, sl] = o
            oa_ref[:, sl] = _head_rms(o, go_ref[:, sl])[0].astype(oa_ref.dtype)

    W = 2 * hd
    vec = pl.BlockSpec((1, W), lambda b, h, i: (0, 0))
    ospec = pl.BlockSpec((tq, W), lambda b, h, i: (b * nq + i, h))
    return pl.pallas_call(
        body, name="fox_fwd", grid=(B, FOX_PAIRS, nq),
        in_specs=[pl.BlockSpec((tq, W), lambda b, h, i: (b * nq + i, h)),
                  pl.BlockSpec((S, W), lambda b, h, i: (b, FOX_PAIRS + h)),
                  pl.BlockSpec((S, W), lambda b, h, i: (b, 2 * FOX_PAIRS + h)),
                  pl.BlockSpec((1, 2, tq, 1), lambda b, h, i: (b, h, i, 0)),
                  pl.BlockSpec((1, 2, 1, S), lambda b, h, i: (b, h, 0, 0)), vec, vec, vec],
        out_specs=[ospec, ospec],
        out_shape=[jax.ShapeDtypeStruct((T, FOX_WIDTH), f32), jax.ShapeDtypeStruct((T, FOX_WIDTH), MXU_DTYPE)],
        compiler_params=_cparams("parallel", "parallel", "parallel"),
    )(P, P, P, ccol, crow, gq, gk, go)


def fox_bwd(P, ccol, crow, gq, gk, go, o_raw, d_oab, *, B, tq=256):
    T = P.shape[0]
    S = T // B
    tq = min(tq, S)
    nq = S // tq
    hd = FOX_HEAD_DIM
    scale = hd ** -0.5

    def body(q_ref, k_ref, v_ref, ccol_ref, crow_ref, gq_ref, gk_ref, go_ref, o_ref, doa_ref,
             dq_ref, dk_ref, dv_ref, dccol_ref, dcrow_ref, dgq_ref, dgk_ref, dgo_ref, dkn_acc, dv_acc, dcrow_acc):
        b, h, i = pl.program_id(0), pl.program_id(1), pl.program_id(2)
        q0 = i * tq

        @pl.when((b == 0) & (h == 0) & (i == 0))
        def _():
            dgq_ref[...] = jnp.zeros_like(dgq_ref)
            dgk_ref[...] = jnp.zeros_like(dgk_ref)
            dgo_ref[...] = jnp.zeros_like(dgo_ref)

        @pl.when(i == 0)
        def _():
            dkn_acc[...] = jnp.zeros_like(dkn_acc)
            dv_acc[...] = jnp.zeros_like(dv_acc)
            dcrow_acc[...] = jnp.zeros_like(dcrow_acc)

        for e in range(2):
            sl = slice(e * hd, (e + 1) * hd)
            q, k, v = q_ref[:, sl], k_ref[:, sl], v_ref[:, sl]
            gqv, gkv, gov = gq_ref[:, sl], gk_ref[:, sl], go_ref[:, sl]
            qn, rq = _head_rms(q, gqv)
            kn, rk = _head_rms(k, gkv)
            p = _softmax_rows(_fox_scores(qn, kn, ccol_ref[0, e], crow_ref[0, e], q0, tq, S, scale))
            o = o_ref[:, sl]
            ro = lax.rsqrt(jnp.mean(o * o, axis=-1, keepdims=True) + EPS)
            do, dgo = _head_rms_bwd(o, ro, gov, doa_ref[:, sl])
            dgo_ref[:, sl] += dgo
            dv_acc[e] += _dot(_mx(p), _mx(do), TN)
            dp = _dot(_mx(do), _mx(v), NT)
            ds = p * (dp - jnp.sum(do * o, axis=-1, keepdims=True))
            dccol_ref[0, e] = jnp.sum(ds, axis=1, keepdims=True)
            dcrow_acc[e] -= jnp.sum(ds, axis=0, keepdims=True)
            dqn = _dot(_mx(ds), _mx(kn), NN) * scale
            dkn_acc[e] += _dot(_mx(ds), _mx(qn), TN) * scale
            dq, dgq = _head_rms_bwd(q, rq, gqv, dqn)
            dq_ref[:, sl] = dq.astype(dq_ref.dtype)
            dgq_ref[:, sl] += dgq

        @pl.when(i == nq - 1)
        def _():
            for e in range(2):
                sl = slice(e * hd, (e + 1) * hd)
                k = k_ref[:, sl]
                gkv = gk_ref[:, sl]
                rk = lax.rsqrt(jnp.mean(k * k, axis=-1, keepdims=True) + EPS)
                dk, dgk = _head_rms_bwd(k, rk, gkv, dkn_acc[e])
                dk_ref[:, sl] = dk.astype(dk_ref.dtype)
                dv_ref[:, sl] = dv_acc[e].astype(dv_ref.dtype)
                dgk_ref[:, sl] += dgk
                dcrow_ref[0, e] = dcrow_acc[e]

    W = 2 * hd
    vec = pl.BlockSpec((1, W), lambda b, h, i: (0, 0))
    qspec = pl.BlockSpec((tq, W), lambda b, h, i: (b * nq + i, h))
    kvout = pl.BlockSpec((S, W), lambda b, h, i: (b, h))
    colspec = pl.BlockSpec((1, 2, tq, 1), lambda b, h, i: (b, h, i, 0))
    rowspec = pl.BlockSpec((1, 2, 1, S), lambda b, h, i: (b, h, 0, 0))
    return pl.pallas_call(
        body, name="fox_bwd", grid=(B, FOX_PAIRS, nq),
        in_specs=[qspec,
                  pl.BlockSpec((S, W), lambda b, h, i: (b, FOX_PAIRS + h)),
                  pl.BlockSpec((S, W), lambda b, h, i: (b, 2 * FOX_PAIRS + h)),
                  colspec, rowspec, vec, vec, vec, qspec, qspec],
        out_specs=[qspec, kvout, kvout, colspec, rowspec, vec, vec, vec],
        out_shape=[jax.ShapeDtypeStruct((T, FOX_WIDTH), MXU_DTYPE), jax.ShapeDtypeStruct((T, FOX_WIDTH), MXU_DTYPE),
                   jax.ShapeDtypeStruct((T, FOX_WIDTH), MXU_DTYPE),
                   jax.ShapeDtypeStruct((B, FOX_HEADS, S, 1), f32), jax.ShapeDtypeStruct((B, FOX_HEADS, 1, S), f32),
                   jax.ShapeDtypeStruct((1, W), f32), jax.ShapeDtypeStruct((1, W), f32), jax.ShapeDtypeStruct((1, W), f32)],
        scratch_shapes=[pltpu.VMEM((2, S, hd), f32), pltpu.VMEM((2, S, hd), f32), pltpu.VMEM((2, 1, S), f32)],
        compiler_params=_cparams("arbitrary", "arbitrary", "arbitrary"),
    )(P, P, P, ccol, crow, gq, gk, go, o_raw, d_oab)


FOX_TQ = 512
FOX_TK = FOX_TQ
GROUP_PRECISION = lax.Precision.HIGH


def _head_mean(v):
    n = v.shape[1]
    r = lax.broadcasted_iota(jnp.int32, (n, n), 0) // FOX_HEAD_DIM
    c = lax.broadcasted_iota(jnp.int32, (n, n), 1) // FOX_HEAD_DIM
    ones = (r == c).astype(bf16)
    hi = v.astype(bf16)
    lo = (v - hi.astype(f32)).astype(bf16)
    return (_dot(hi, ones, NN) + _dot(lo, ones, NN)) * (1.0 / FOX_HEAD_DIM)


def fox_prep_fwd(P, gq, gk, *, tr=512):
    T = P.shape[0]
    tr = min(tr, T)
    scale = FOX_HEAD_DIM ** -0.5

    def body(q_ref, k_ref, v_ref, gq_ref, gk_ref, qn_ref, kn_ref, vb_ref):
        q, k = q_ref[...], k_ref[...]
        qn_ref[...] = (q * lax.rsqrt(_head_mean(q * q) + EPS) * (gq_ref[...] * scale)).astype(qn_ref.dtype)
        kn_ref[...] = (k * lax.rsqrt(_head_mean(k * k) + EPS) * gk_ref[...]).astype(kn_ref.dtype)
        vb_ref[...] = v_ref[...].astype(vb_ref.dtype)

    W = FOX_WIDTH
    col = lambda j: pl.BlockSpec((tr, W), lambda i: (i, j))
    vec = pl.BlockSpec((1, W), lambda i: (0, 0))
    out = jax.ShapeDtypeStruct((T, W), MXU_DTYPE)
    return pl.pallas_call(
        body, name="fox_prep_fwd", grid=(T // tr,), in_specs=[col(0), col(1), col(2), vec, vec],
        out_specs=[col(0)] * 3, out_shape=[out] * 3, compiler_params=_cparams("parallel"),
    )(P, P, P, gq, gk)


def fox_prep_bwd(P, gq, gk, dqn, dkn, *, tr=512):
    T = P.shape[0]
    tr = min(tr, T)
    scale = FOX_HEAD_DIM ** -0.5

    def body(q_ref, k_ref, gq_ref, gk_ref, dqn_ref, dkn_ref, dq_ref, dk_ref, dgq_ref, dgk_ref):
        @pl.when(pl.program_id(0) == 0)
        def _():
            dgq_ref[...] = jnp.zeros_like(dgq_ref)
            dgk_ref[...] = jnp.zeros_like(dgk_ref)

        def one(x, g, dn, dx_ref, dg_ref):
            r = lax.rsqrt(_head_mean(x * x) + EPS)
            xhat = x * r
            gd = dn * g
            dx_ref[...] = (r * (gd - xhat * _head_mean(gd * xhat))).astype(dx_ref.dtype)
            return jnp.sum(dn * xhat, axis=0, keepdims=True)

        dgq_ref[...] += scale * one(q_ref[...], gq_ref[...] * scale, dqn_ref[...], dq_ref, dgq_ref)
        dgk_ref[...] += one(k_ref[...], gk_ref[...], dkn_ref[...], dk_ref, dgk_ref)

    W = FOX_WIDTH
    col = lambda j: pl.BlockSpec((tr, W), lambda i: (i, j))
    vec = pl.BlockSpec((1, W), lambda i: (0, 0))
    return pl.pallas_call(
        body, name="fox_prep_bwd", grid=(T // tr,), in_specs=[col(0), col(1), vec, vec, col(0), col(0)],
        out_specs=[col(0), col(0), vec, vec],
        out_shape=[jax.ShapeDtypeStruct((T, W), MXU_DTYPE), jax.ShapeDtypeStruct((T, W), MXU_DTYPE),
                   jax.ShapeDtypeStruct((1, W), f32), jax.ShapeDtypeStruct((1, W), f32)],
        compiler_params=_cparams("arbitrary"),
    )(P, P, gq, gk, dqn, dkn)


def _fox_tile_scores(q, k_ref, ccol_ref, cq, e, j, sl, mask_off):
    tq, tk = FOX_TQ, FOX_TK
    rows = pl.ds(pl.multiple_of(j * tk, tk), tk)
    k = k_ref[rows, sl]
    s = _dot(k, q, NT) + cq - ccol_ref[0, e, rows, :]
    if mask_off is not None:
        key = lax.broadcasted_iota(jnp.int32, (tk, tq), 0) + mask_off
        query = lax.broadcasted_iota(jnp.int32, (tk, tq), 1)
        s = jnp.where(key <= query, s, NEG_INF)
    return s, k, rows


def _fox_sweep(i, update, carry):
    nd = FOX_TQ // FOX_TK
    carry = lax.fori_loop(0, i * nd, lambda j, cr: update(cr, j, None), carry)
    for d in range(nd):
        carry = update(carry, i * nd + d, d * FOX_TK)
    return carry


def fox_core_fwd(qn, kn, vb, ccol, crow, go, *, B):
    T = qn.shape[0]
    S = T // B
    tq = FOX_TQ
    nq = S // tq
    hd = FOX_HEAD_DIM

    def body(q_ref, k_ref, v_ref, ccol_ref, crow_ref, go_ref, o_ref, oa_ref, lse_ref):
        i = pl.program_id(2)
        for e in range(2):
            sl = slice(e * hd, (e + 1) * hd)
            q = q_ref[:, sl]
            cq = crow_ref[0, e, i]

            def update(carry, j, mask_off):
                m, l, acc = carry
                s, _, rows = _fox_tile_scores(q, k_ref, ccol_ref, cq, e, j, sl, mask_off)
                m2 = jnp.maximum(m, jnp.max(s, axis=0, keepdims=True))
                a = jnp.exp(m - m2)
                p = jnp.exp(s - m2)
                return m2, a * l + jnp.sum(p, axis=0, keepdims=True), a * acc + _dot(v_ref[rows, sl], _mx(p), TN)

            carry = (jnp.full((1, tq), NEG_INF, f32), jnp.zeros((1, tq), f32), jnp.zeros((hd, tq), f32))
            m, l, acc = _fox_sweep(i, update, carry)
            o = (acc / l).T
            o_ref[:, sl] = o
            oa_ref[:, sl] = _head_rms(o, go_ref[:, sl])[0].astype(oa_ref.dtype)
            lse_ref[0, e, 0] = m + jnp.log(l)

    W = 2 * hd
    qspec = pl.BlockSpec((tq, W), lambda b, h, i: (b * nq + i, h))
    kspec = pl.BlockSpec((S, W), lambda b, h, i: (b, h))
    return pl.pallas_call(
        body, name="fox_core_fwd", grid=(B, FOX_PAIRS, nq),
        in_specs=[qspec, kspec, kspec, pl.BlockSpec((1, 2, S, 1), lambda b, h, i: (b, h, 0, 0)),
                  pl.BlockSpec((1, 2, nq, 1, tq), lambda b, h, i: (b, h, 0, 0, 0)),
                  pl.BlockSpec((1, W), lambda b, h, i: (0, 0))],
        out_specs=[qspec, qspec, pl.BlockSpec((1, 2, 1, 1, tq), lambda b, h, i: (b, h, i, 0, 0))],
        out_shape=[jax.ShapeDtypeStruct((T, FOX_WIDTH), f32), jax.ShapeDtypeStruct((T, FOX_WIDTH), MXU_DTYPE),
                   jax.ShapeDtypeStruct((B, FOX_HEADS, nq, 1, tq), f32)],
        compiler_params=_cparams("parallel", "parallel", "parallel"),
    )(qn, kn, vb, ccol, crow, go)


def fox_core_bwd(qn, kn, vb, ccol, crow, go, o_raw, lse, d_oab, *, B):
    T = qn.shape[0]
    S = T // B
    tq = FOX_TQ
    nq = S // tq
    hd = FOX_HEAD_DIM

    def body(q_ref, k_ref, v_ref, ccol_ref, crow_ref, go_ref, o_ref, lse_ref, doa_ref,
             dq_ref, dk_ref, dv_ref, dckey_ref, dcrow_ref, dgo_ref, dk_acc, dv_acc, dck_acc):
        b, h, i = pl.program_id(0), pl.program_id(1), pl.program_id(2)

        @pl.when((b == 0) & (h == 0) & (i == 0))
        def _():
            dgo_ref[...] = jnp.zeros_like(dgo_ref)

        @pl.when(i == 0)
        def _():
            dk_acc[...] = jnp.zeros_like(dk_acc)
            dv_acc[...] = jnp.zeros_like(dv_acc)
            dck_acc[...] = jnp.zeros_like(dck_acc)

        for e in range(2):
            sl = slice(e * hd, (e + 1) * hd)
            q = q_ref[:, sl]
            cq = crow_ref[0, e, i]
            lse_e = lse_ref[0, e, 0]
            o = o_ref[:, sl]
            ro = lax.rsqrt(jnp.mean(o * o, axis=-1, keepdims=True) + EPS)
            do, dgo = _head_rms_bwd(o, ro, go_ref[:, sl], doa_ref[:, sl])
            dgo_ref[:, sl] += dgo
            delta = jnp.sum((do * o).T, axis=0, keepdims=True)
            do_b = _mx(do)

            def update(carry, j, mask_off):
                dq, dcq = carry
                s, k, rows = _fox_tile_scores(q, k_ref, ccol_ref, cq, e, j, sl, mask_off)
                p = jnp.exp(s - lse_e)
                dv_acc[e, rows, :] += _dot(_mx(p), do_b, NN)
                ds = p * (_dot(v_ref[rows, sl], do_b, NT) - delta)
                dck_acc[e, rows, :] -= jnp.sum(ds, axis=1, keepdims=True)
                ds_b = _mx(ds)
                dk_acc[e, rows, :] += _dot(ds_b, q, NN)
                return dq + _dot(ds_b, k, TN), dcq + jnp.sum(ds, axis=0, keepdims=True)

            dq, dcq = _fox_sweep(i, update, (jnp.zeros((tq, hd), f32), jnp.zeros((1, tq), f32)))
            dq_ref[:, sl] = dq
            dcrow_ref[0, e, 0] = dcq

        @pl.when(i == nq - 1)
        def _():
            for e in range(2):
                sl = slice(e * hd, (e + 1) * hd)
                dk_ref[:, sl] = dk_acc[e]
                dv_ref[:, sl] = dv_acc[e].astype(dv_ref.dtype)
                dckey_ref[0, e] = jnp.transpose(jnp.broadcast_to(dck_acc[e], (S, LANES)))[0:1, :]

    W = 2 * hd
    qspec = pl.BlockSpec((tq, W), lambda b, h, i: (b * nq + i, h))
    kspec = pl.BlockSpec((S, W), lambda b, h, i: (b, h))
    colspec = pl.BlockSpec((1, 2, S, 1), lambda b, h, i: (b, h, 0, 0))
    rowspec = pl.BlockSpec((1, 2, nq, 1, tq), lambda b, h, i: (b, h, 0, 0, 0))
    tilespec = pl.BlockSpec((1, 2, 1, 1, tq), lambda b, h, i: (b, h, i, 0, 0))
    vec = pl.BlockSpec((1, W), lambda b, h, i: (0, 0))
    return pl.pallas_call(
        body, name="fox_core_bwd", grid=(B, FOX_PAIRS, nq),
        in_specs=[qspec, kspec, kspec, colspec, rowspec, vec, qspec, tilespec, qspec],
        out_specs=[qspec, kspec, kspec, pl.BlockSpec((1, 2, 1, S), lambda b, h, i: (b, h, 0, 0)), tilespec, vec],
        out_shape=[jax.ShapeDtypeStruct((T, FOX_WIDTH), f32), jax.ShapeDtypeStruct((T, FOX_WIDTH), f32),
                   jax.ShapeDtypeStruct((T, FOX_WIDTH), MXU_DTYPE),
                   jax.ShapeDtypeStruct((B, FOX_HEADS, 1, S), f32), jax.ShapeDtypeStruct((B, FOX_HEADS, nq, 1, tq), f32),
                   jax.ShapeDtypeStruct((1, W), f32)],
        scratch_shapes=[pltpu.VMEM((2, S, hd), f32), pltpu.VMEM((2, S, hd), f32), pltpu.VMEM((2, S, 1), f32)],
        compiler_params=_cparams("arbitrary", "arbitrary", "arbitrary"),
    )(qn, kn, vb, ccol, crow, go, o_raw, lse, d_oab)


def _lane_mask(lo, hi, shape):
    lane = lax.broadcasted_iota(jnp.int32, shape, 1)
    return (lane >= lo) & (lane < hi)


def _cumsum_rows(v, period, reverse=False):
    n = v.shape[0]
    pos = lax.broadcasted_iota(jnp.int32, v.shape, 0) % period
    sh = 1
    while sh < period:
        if reverse:
            v = v + jnp.where(pos + sh < period, pltpu.roll(v, n - sh, 0), 0.0)
        else:
            v = v + jnp.where(pos >= sh, pltpu.roll(v, sh, 0), 0.0)
        sh *= 2
    return v


def _gate_values(z, bias, alog):
    zb = z + bias
    ls = jax.nn.log_sigmoid(zb)
    beta = jax.nn.sigmoid(z)
    g = -jnp.exp(alog) * jax.nn.softplus(zb)
    return zb, ls, beta, g


def gates_fwd(P, bias, alog, *, B):
    T = P.shape[0]
    S = T // B

    def body(z_ref, bias_ref, alog_ref, o_ref):
        z = z_ref[...]
        _, ls, beta, g = _gate_values(z, bias_ref[...], alog_ref[...])
        c = _cumsum_rows(ls, S)
        gc = _cumsum_rows(g, GDN_CHUNK)
        o = jnp.where(_lane_mask(SM_F, SM_F + FOX_HEADS, z.shape), c, 0.0)
        o = jnp.where(_lane_mask(SM_B, SM_B + GDN_HEADS, z.shape), beta, o)
        o = jnp.where(_lane_mask(SM_A, SM_A + GDN_HEADS, z.shape), gc, o)
        o_ref[...] = o

    vec = pl.BlockSpec((1, LANES), lambda b: (0, 0))
    return pl.pallas_call(
        body, name="gates_fwd", grid=(B,),
        in_specs=[pl.BlockSpec((S, LANES), lambda b: (b, COL_SMALL // LANES)), vec, vec],
        out_specs=pl.BlockSpec((S, LANES), lambda b: (b, 0)),
        out_shape=jax.ShapeDtypeStruct((T, LANES), f32),
        compiler_params=_cparams("parallel"),
    )(P, bias, alog)


def gates_bwd(P, bias, alog, dgates, *, B):
    T = P.shape[0]
    S = T // B

    def body(z_ref, bias_ref, alog_ref, dg_ref, dz_ref, par_ref):
        z = z_ref[...]
        zb, ls, beta, g = _gate_values(z, bias_ref[...], alog_ref[...])
        d = dg_ref[...]
        dls = _cumsum_rows(d, S, reverse=True)
        dgr = _cumsum_rows(d, GDN_CHUNK, reverse=True)
        sig = jax.nn.sigmoid(zb)
        dz_f = dls * (1.0 - sig)
        dz_b = d * beta * (1.0 - beta)
        dz_a = dgr * (-jnp.exp(alog_ref[...])) * sig
        dz = jnp.where(_lane_mask(SM_F, SM_F + FOX_HEADS, z.shape), dz_f, 0.0)
        dz = jnp.where(_lane_mask(SM_B, SM_B + GDN_HEADS, z.shape), dz_b, dz)
        dz = jnp.where(_lane_mask(SM_A, SM_A + GDN_HEADS, z.shape), dz_a, dz)
        dz_ref[...] = dz.astype(dz_ref.dtype)

        @pl.when(pl.program_id(0) == 0)
        def _():
            par_ref[...] = jnp.zeros_like(par_ref)

        dalog = jnp.where(_lane_mask(SM_A, SM_A + GDN_HEADS, z.shape), dgr * g, 0.0)
        par_ref[0:1, :] += jnp.sum(dz, axis=0, keepdims=True)
        par_ref[1:2, :] += jnp.sum(dalog, axis=0, keepdims=True)

    vec = pl.BlockSpec((1, LANES), lambda b: (0, 0))
    return pl.pallas_call(
        body, name="gates_bwd", grid=(B,),
        in_specs=[pl.BlockSpec((S, LANES), lambda b: (b, COL_SMALL // LANES)), vec, vec,
                  pl.BlockSpec((S, LANES), lambda b: (b, 0))],
        out_specs=[pl.BlockSpec((S, LANES), lambda b: (b, 0)), pl.BlockSpec((8, LANES), lambda b: (0, 0))],
        out_shape=[jax.ShapeDtypeStruct((T, LANES), MXU_DTYPE), jax.ShapeDtypeStruct((8, LANES), f32)],
        compiler_params=_cparams("arbitrary"),
    )(P, bias, alog, dgates)


GDN_BLOCKS = 3 * GDN_HEADS


def _shift_rows(v, d, reverse=False):
    if d == 0:
        return v
    n = v.shape[0]
    row = lax.broadcasted_iota(jnp.int32, v.shape, 0)
    if reverse:
        return jnp.where(row + d < n, pltpu.roll(v, n - d, 0), 0.0)
    return jnp.where(row >= d, pltpu.roll(v, d, 0), 0.0)


def _conv_silu(x, w):
    pre = sum(w[j:j + 1, :] * _shift_rows(x, CONV_WIDTH - 1 - j) for j in range(CONV_WIDTH))
    return pre, pre * jax.nn.sigmoid(pre)


def gdn_prep_fwd(P, conv_w, *, B):
    T = P.shape[0]
    S = T // B

    def body(x_ref, w_ref, o_ref):
        _, y = _conv_silu(x_ref[...], w_ref[...])
        yn = y * lax.rsqrt(jnp.sum(y * y, axis=-1, keepdims=True) + EPS)
        o_ref[...] = jnp.where(pl.program_id(1) < 2 * GDN_HEADS, yn, y)

    return pl.pallas_call(
        body, name="gdn_prep_fwd", grid=(B, GDN_BLOCKS),
        in_specs=[pl.BlockSpec((S, LANES), lambda b, j: (b, COL_GDN // LANES + j)),
                  pl.BlockSpec((CONV_WIDTH, LANES), lambda b, j: (0, j))],
        out_specs=pl.BlockSpec((S, LANES), lambda b, j: (b, j)),
        out_shape=jax.ShapeDtypeStruct((T, 3 * GDN_WIDTH), f32),
        compiler_params=_cparams("parallel", "parallel"),
    )(P, conv_w)


def gdn_prep_bwd(P, conv_w, dGq, dGk, dGv, *, B):
    T = P.shape[0]
    S = T // B
    H = GDN_HEADS

    def body(x_ref, w_ref, dq_ref, dk_ref, dv_ref, dx_ref, dw_ref):
        x, w = x_ref[...], w_ref[...]
        pre, y = _conv_silu(x, w)
        jb = pl.program_id(0)
        dn = jnp.where(jb < H, dq_ref[...], jnp.where(jb < 2 * H, dk_ref[...], dv_ref[...]))
        r = lax.rsqrt(jnp.sum(y * y, axis=-1, keepdims=True) + EPS)
        n = y * r
        dy_norm = r * (dn - n * jnp.sum(dn * n, axis=-1, keepdims=True))
        dy = jnp.where(pl.program_id(0) < 2 * GDN_HEADS, dy_norm, dn)
        sg = jax.nn.sigmoid(pre)
        dpre = dy * (sg * (1.0 + pre * (1.0 - sg)))
        dx = sum(w[j:j + 1, :] * _shift_rows(dpre, CONV_WIDTH - 1 - j, reverse=True) for j in range(CONV_WIDTH))
        dx_ref[...] = dx.astype(dx_ref.dtype)

        @pl.when(pl.program_id(1) == 0)
        def _():
            dw_ref[...] = jnp.zeros_like(dw_ref)

        for j in range(CONV_WIDTH):
            dw_ref[j:j + 1, :] += jnp.sum(dpre * _shift_rows(x, CONV_WIDTH - 1 - j), axis=0, keepdims=True)

    return pl.pallas_call(
        body, name="gdn_prep_bwd", grid=(GDN_BLOCKS, B),
        in_specs=[pl.BlockSpec((S, LANES), lambda j, b: (b, COL_GDN // LANES + j)),
                  pl.BlockSpec((CONV_WIDTH, LANES), lambda j, b: (0, j))]
        + [pl.BlockSpec((S, LANES), lambda j, b, t=t: (b, jnp.clip(j - t * H, 0, H - 1))) for t in range(3)],
        out_specs=[pl.BlockSpec((S, LANES), lambda j, b: (b, j)),
                   pl.BlockSpec((CONV_WIDTH, LANES), lambda j, b: (0, j))],
        out_shape=[jax.ShapeDtypeStruct((T, 3 * GDN_WIDTH), MXU_DTYPE),
                   jax.ShapeDtypeStruct((CONV_WIDTH, 3 * GDN_WIDTH), f32)],
        compiler_params=_cparams("arbitrary", "arbitrary"),
    )(P, conv_w, dGq, dGk, dGv)


GDN_GROUP = 16
GDN_GROUP_FWD = 16
B_NN = (((2,), (1,)), ((0,), (0,)))
B_NT = (((2,), (2,)), ((0,), (0,)))
B_TN = (((1,), (1,)), ((0,), (0,)))


def _bmm(a, b, dims, precision=None):
    if precision is None:
        a, b = _mx(a), _mx(b)
    return lax.dot_general(a, b, dims, preferred_element_type=f32, precision=precision)


def _tri_inverse(A):
    C = A.shape[-1]
    row = lax.broadcasted_iota(jnp.int32, A.shape, 1)
    col = lax.broadcasted_iota(jnp.int32, A.shape, 2)
    eye = (row == col).astype(f32)
    X = jnp.where((row // 4) == (col // 4), -A, 0.0)
    X2 = _bmm(X, X, B_NN, INV_PRECISION)
    Tm = eye + X + X2 + _bmm(X, X2, B_NN, INV_PRECISION)
    b = 4
    while b < C:
        off = ((row // (2 * b)) == (col // (2 * b))) & ((row // b) != (col // b))
        Tm = Tm - _bmm(_bmm(Tm, jnp.where(off, A, 0.0), B_NN, INV_PRECISION), Tm, B_NN, INV_PRECISION)
        b *= 2
    return Tm


def _pick_lane(block, lane_idx):
    lane = lax.broadcasted_iota(jnp.int32, block.shape, 1)
    return jnp.sum(jnp.where(lane == lane_idx, block, 0.0), axis=1, keepdims=True)


def _gdn_local(q, k, v, beta, gc, Tm=None, uwm=None):
    C = GDN_CHUNK
    n = q.shape[0] // C
    q = q.reshape(n, C, -1) * (GDN_HEAD_DIM ** -0.5)
    k = k.reshape(n, C, -1)
    v = v.reshape(n, C, -1)
    beta = beta.reshape(n, C, 1)
    gc = gc.reshape(n, C, 1)
    row = lax.broadcasted_iota(jnp.int32, (n, C, C), 1)
    col = lax.broadcasted_iota(jnp.int32, (n, C, C), 2)
    gcT = jnp.swapaxes(jnp.broadcast_to(gc, (n, C, C)), 1, 2)
    D = jnp.exp(jnp.where(row >= col, gc - gcT, NEG_INF))
    kb = k * beta
    vb = v * beta
    A = jnp.where(row > col, _bmm(kb, k, B_NT) * D, 0.0)
    Gam = jnp.exp(gc)
    kg = kb * Gam
    gl = gc[:, C - 1:C, :]
    kdec = jnp.exp(gl - gc)
    loc = dict(q=q, k=k, v=v, beta=beta, gc=gc, D=D, kb=kb, vb=vb, A=A, Gam=Gam, kg=kg,
               kdec=kdec, kd=k * kdec, qg=q * Gam, gam=jnp.exp(gl), row=row, col=col)
    uwm = Tm is None if uwm is None else uwm
    Tm = _tri_inverse(A) if Tm is None else Tm.reshape(n, C, C)
    if uwm:
        loc.update(u=_bmm(Tm, vb, B_NN), w=_bmm(Tm, kg, B_NN), M=_bmm(q, k, B_NT) * D)
    loc["Tm"] = Tm
    return loc


def _gdn_store_local(loc, r0, u_s, w_s, qg_s, kd_s, M_s, gam_s, c0):
    n = loc["u"].shape[0]
    R = n * GDN_CHUNK
    u_s[pl.ds(r0, R), :] = loc["u"].reshape(R, -1)
    w_s[pl.ds(r0, R), :] = loc["w"].reshape(R, -1)
    qg_s[pl.ds(r0, R), :] = loc["qg"].reshape(R, -1)
    kd_s[pl.ds(r0, R), :] = loc["kd"].reshape(R, -1)
    M_s[pl.ds(r0, R), :] = loc["M"].reshape(R, -1)
    gam_s[pl.ds(c0, n)] = jnp.broadcast_to(loc["gam"], (n, 1, LANES))


def _gdn_specs(S):
    blk = lambda off: pl.BlockSpec((S, LANES), lambda b, h: (b, off + h))
    return blk


def gdn_fwd(G, gates, P, g_on, *, B):
    T = G.shape[0]
    S = T // B
    C = GDN_CHUNK
    N = S // C
    grp = min(GDN_GROUP_FWD, N)
    R = grp * C
    hd = GDN_HEAD_DIM

    def body(q_ref, k_ref, v_ref, gt_ref, z_ref, gon_ref, o_ref, ob_ref, st_ref, tm_ref, A_s, B_s, Q_s, O_s, gam_s):
        h = pl.program_id(1)

        def local(gi, carry):
            r0 = pl.multiple_of(gi * R, R)
            gt = gt_ref[pl.ds(r0, R), :]
            loc = _gdn_local(q_ref[pl.ds(r0, R), :], k_ref[pl.ds(r0, R), :], v_ref[pl.ds(r0, R), :],
                             _pick_lane(gt, SM_B + h), _pick_lane(gt, SM_A + h))
            chunks = pl.ds(gi * grp, grp)
            tm_ref[0, 0, pl.ds(r0, R), :] = loc["Tm"].reshape(R, C)
            A_s[chunks] = -_bmm(loc["kd"], loc["w"], B_TN)
            B_s[chunks] = _bmm(loc["kd"], loc["u"], B_TN)
            Q_s[pl.ds(r0, R), :] = (loc["qg"] - _bmm(loc["M"], loc["w"], B_NN)).reshape(R, hd)
            O_s[pl.ds(r0, R), :] = _bmm(loc["M"], loc["u"], B_NN).reshape(R, hd)
            gam_s[chunks] = jnp.broadcast_to(loc["gam"], (grp, 1, LANES))
            return carry

        lax.fori_loop(0, N // grp, local, 0)

        def step(n, state):
            st_ref[0, 0, n] = state
            return state * gam_s[n] + _dotm(A_s[n], state, NN) + B_s[n]

        lax.fori_loop(0, N, step, jnp.zeros((hd, hd), f32))

        def outputs(gi, carry):
            r0 = pl.multiple_of(gi * R, R)
            Q = Q_s[pl.ds(r0, R), :].reshape(grp, C, hd)
            o = _bmm(Q, st_ref[0, 0, pl.ds(gi * grp, grp)], B_NN).reshape(R, hd) + O_s[pl.ds(r0, R), :]
            o_ref[pl.ds(r0, R), :] = o
            return carry

        lax.fori_loop(0, N // grp, outputs, 0)
        o = o_ref[...]
        z = z_ref[...]
        ob_ref[...] = (_head_rms(o, gon_ref[...])[0] * (z * jax.nn.sigmoid(z))).astype(ob_ref.dtype)

    blk = lambda off: pl.BlockSpec((S, LANES), lambda b, h: (b, off + h))
    rows = lambda: pltpu.VMEM((S, hd), f32)
    return pl.pallas_call(
        body, name="gdn_fwd", grid=(B, GDN_HEADS),
        in_specs=[blk(0), blk(GDN_HEADS), blk(2 * GDN_HEADS), pl.BlockSpec((S, LANES), lambda b, h: (b, 0)),
                  blk(COL_Z // LANES), pl.BlockSpec((1, hd), lambda b, h: (0, 0))],
        out_specs=[blk(0), blk(0), pl.BlockSpec((1, 1, N, hd, hd), lambda b, h: (b, h, 0, 0, 0)),
                   pl.BlockSpec((1, 1, S, C), lambda b, h: (b, h, 0, 0))],
        out_shape=[jax.ShapeDtypeStruct((T, GDN_WIDTH), f32), jax.ShapeDtypeStruct((T, GDN_WIDTH), MXU_DTYPE),
                   jax.ShapeDtypeStruct((B, GDN_HEADS, N, hd, hd), f32), jax.ShapeDtypeStruct((B, GDN_HEADS, S, C), f32)],
        scratch_shapes=[pltpu.VMEM((N, hd, hd), f32), pltpu.VMEM((N, hd, hd), f32), rows(), rows(),
                        pltpu.VMEM((N, 1, LANES), f32)],
        compiler_params=_cparams("parallel", "parallel"),
    )(G, G, G, gates, P, g_on)


def gdn_bwd(G, gates, P, g_on, o_raw, states, tm, d_oab, *, B):
    T = G.shape[0]
    S = T // B
    C = GDN_CHUNK
    N = S // C
    grp = min(GDN_GROUP, N)
    R = grp * C
    hd = GDN_HEAD_DIM

    def body(q_ref, k_ref, v_ref, gt_ref, z_ref, gon_ref, o_ref, st_ref, tm_ref, dob_ref,
             dq_ref, dk_ref, dv_ref, dgt_ref, dz_ref, dgon_ref,
             u_s, w_s, M_s, gam_s, do_s, A_s, C_s, dst_s):
        b, h = pl.program_id(0), pl.program_id(1)

        @pl.when((b == 0) & (h == 0))
        def _():
            dgon_ref[...] = jnp.zeros_like(dgon_ref)

        @pl.when(h == 0)
        def _():
            dgt_ref[...] = jnp.zeros_like(dgt_ref)

        def group_inputs(gi, uwm):
            r0 = pl.multiple_of(gi * R, R)
            gt = gt_ref[pl.ds(r0, R), :]
            return r0, _gdn_local(q_ref[pl.ds(r0, R), :], k_ref[pl.ds(r0, R), :], v_ref[pl.ds(r0, R), :],
                                  _pick_lane(gt, SM_B + h), _pick_lane(gt, SM_A + h), tm_ref[0, 0, pl.ds(r0, R), :], uwm)

        def local(gi, carry):
            r0, loc = group_inputs(gi, True)
            rows, chunks = pl.ds(r0, R), pl.ds(gi * grp, grp)
            u_s[rows, :] = loc["u"].reshape(R, hd)
            w_s[rows, :] = loc["w"].reshape(R, hd)
            M_s[rows, :] = loc["M"].reshape(R, C)
            gam_s[chunks] = jnp.broadcast_to(loc["gam"], (grp, 1, LANES))
            o, z, gon = o_ref[rows, :], z_ref[rows, :], gon_ref[...]
            dob = dob_ref[rows, :]
            on, ro = _head_rms(o, gon)
            sz = jax.nn.sigmoid(z)
            dz_ref[rows, :] = (dob * on * (sz * (1.0 + z * (1.0 - sz)))).astype(dz_ref.dtype)
            do, dgon = _head_rms_bwd(o, ro, gon, dob * (z * sz))
            do_s[rows, :] = do
            dgon_ref[...] += dgon
            A_s[chunks] = -_bmm(loc["kd"], loc["w"], B_TN)
            C_s[chunks] = _bmm(loc["qg"] - _bmm(loc["M"], loc["w"], B_NN), do.reshape(grp, C, hd), B_TN)
            return carry

        lax.fori_loop(0, N // grp, local, 0)

        def step(t, dS):
            n = N - 1 - t
            dst_s[n] = dS
            return dS * gam_s[n] + _dotm(A_s[n], dS, TN) + C_s[n]

        lax.fori_loop(0, N, step, jnp.zeros((hd, hd), f32))

        def finish(gi, carry):
            r0, L = group_inputs(gi, False)
            n = grp
            rows, chunks = pl.ds(r0, R), pl.ds(gi * grp, grp)
            g3 = lambda ref: ref[rows, :].reshape(n, C, -1)
            u, w, do = g3(u_s), g3(w_s), g3(do_s)
            L["M"] = g3(M_s)
            state, dS = st_ref[0, 0, chunks], dst_s[chunks]
            v_new = u - _bmm(w, state, B_NN)
            du = _bmm(L["M"], do, B_TN) + _bmm(L["kd"], dS, B_NN)
            dw = -_bmm(du, state, B_NT)
            dqg = _bmm(do, state, B_NT)
            dM = _bmm(do, v_new, B_NT)
            dkd = _bmm(v_new, dS, B_NT)
            dgl_state = jnp.sum(jnp.sum(dS * state, axis=2, keepdims=True), axis=1, keepdims=True) * L["gam"]
            TmT = jnp.swapaxes(L["Tm"], 1, 2)
            dTm = _bmm(du, L["vb"], B_NT) + _bmm(dw, L["kg"], B_NT)
            dvb = _bmm(TmT, du, B_NN)
            dkg = _bmm(TmT, dw, B_NN)
            dA = jnp.where(L["row"] > L["col"], -_bmm(_bmm(TmT, dTm, B_NN), TmT, B_NN), 0.0)
            dKK = dA * L["D"]
            dQK = dM * L["D"]
            dkb = _bmm(dKK, L["k"], B_NN) + dkg * L["Gam"]
            dk = (_bmm(dKK, L["kb"], B_TN) + _bmm(dQK, L["q"], B_TN) + dkd * L["kdec"] + L["beta"] * dkb)
            dq = (_bmm(dQK, L["k"], B_NN) + dqg * L["Gam"]) * (GDN_HEAD_DIM ** -0.5)
            E = dA * L["A"] + dM * L["M"]
            r = jnp.sum(dkd * L["kd"], axis=-1, keepdims=True)
            dgc = (jnp.sum(E, axis=2, keepdims=True) - jnp.sum(jnp.swapaxes(E, 1, 2), axis=2, keepdims=True)
                   + jnp.sum(dkg * L["kg"], axis=-1, keepdims=True) + jnp.sum(dqg * L["qg"], axis=-1, keepdims=True) - r)
            dgl = jnp.sum(r, axis=1, keepdims=True) + dgl_state
            rowc = lax.broadcasted_iota(jnp.int32, (n, C, 1), 1)
            dgc = dgc + jnp.where(rowc == C - 1, dgl, 0.0)
            dbeta = jnp.sum(dkb * L["k"], axis=-1, keepdims=True) + jnp.sum(dvb * L["v"], axis=-1, keepdims=True)
            dq_ref[rows, :] = dq.reshape(R, hd)
            dk_ref[rows, :] = dk.reshape(R, hd)
            dv_ref[rows, :] = (L["beta"] * dvb).reshape(R, hd)
            lane = lax.broadcasted_iota(jnp.int32, (R, LANES), 1)
            dgt_ref[rows, :] += (jnp.where(lane == SM_B + h, dbeta.reshape(R, 1), 0.0)
                                 + jnp.where(lane == SM_A + h, dgc.reshape(R, 1), 0.0))
            return carry

        lax.fori_loop(0, N // grp, finish, 0)

    blk = lambda off: pl.BlockSpec((S, LANES), lambda b, h: (b, off + h))
    rows = lambda: pltpu.VMEM((S, hd), f32)
    return pl.pallas_call(
        body, name="gdn_bwd", grid=(B, GDN_HEADS),
        in_specs=[blk(0), blk(GDN_HEADS), blk(2 * GDN_HEADS), pl.BlockSpec((S, LANES), lambda b, h: (b, 0)),
                  blk(COL_Z // LANES), pl.BlockSpec((1, hd), lambda b, h: (0, 0)), blk(0),
                  pl.BlockSpec((1, 1, N, hd, hd), lambda b, h: (b, h, 0, 0, 0)),
                  pl.BlockSpec((1, 1, S, C), lambda b, h: (b, h, 0, 0)), blk(GDN_HEADS)],
        out_specs=[blk(0), blk(0), blk(0), pl.BlockSpec((S, LANES), lambda b, h: (b, 0)), blk(0),
                   pl.BlockSpec((1, hd), lambda b, h: (0, 0))],
        out_shape=[jax.ShapeDtypeStruct((T, GDN_WIDTH), f32), jax.ShapeDtypeStruct((T, GDN_WIDTH), f32),
                   jax.ShapeDtypeStruct((T, GDN_WIDTH), f32), jax.ShapeDtypeStruct((T, LANES), f32),
                   jax.ShapeDtypeStruct((T, GDN_WIDTH), MXU_DTYPE), jax.ShapeDtypeStruct((1, hd), f32)],
        scratch_shapes=[rows(), rows(), pltpu.VMEM((S, C), f32), pltpu.VMEM((N, 1, LANES), f32), rows(),
                        pltpu.VMEM((N, hd, hd), f32), pltpu.VMEM((N, hd, hd), f32), pltpu.VMEM((N, hd, hd), f32)],
        compiler_params=_cparams("arbitrary", "arbitrary"),
    )(G, G, G, gates, P, g_on, o_raw, states, tm, d_oab)


IN_SPLIT = (0, 1536, 1544, 3080, 3088, 3600)


IN_SHARD = IN_DIM // 4
IN_SHARD_PAD = 928


def align_w_in_t(wt):
    s = IN_SPLIT
    pad = jnp.zeros((IN_ALIGNED - IN_DIM, wt.shape[1]), wt.dtype)
    return jnp.concatenate([wt[s[0]:s[1]], wt[s[2]:s[3]], wt[s[4]:s[5]], wt[s[1]:s[2]], wt[s[3]:s[4]], pad], axis=0)


def unalign_w_in_t(wa):
    return jnp.concatenate([wa[0:1536], wa[COL_SMALL:COL_SMALL + 8], wa[1536:3072],
                            wa[COL_SMALL + 8:COL_SMALL + 16], wa[3072:3584]], axis=0)


IN_SEGMENTS = (((0, 1536), 0), ((1536, 1544), COL_SMALL), ((1544, 3080), 1536), ((3080, 3088), COL_SMALL + 8),
               ((3088, 3600), 3072))


def align_w_in_slots(slots):
    pieces = []
    for (lo, hi), _ in sorted(IN_SEGMENTS, key=lambda seg: seg[1]):
        for k in range(N_CHIPS):
            a, b = max(lo, k * IN_SHARD), min(hi, (k + 1) * IN_SHARD)
            if a < b:
                pieces.append(slots[k, a - k * IN_SHARD:b - k * IN_SHARD])
    pieces.append(jnp.zeros((IN_ALIGNED - IN_DIM, slots.shape[2]), slots.dtype))
    return jnp.concatenate(pieces, axis=0)


def unalign_to_slots(wa):
    slots = []
    for k in range(N_CHIPS):
        lo, hi = k * IN_SHARD, (k + 1) * IN_SHARD
        pieces = []
        for (a, b), first in IN_SEGMENTS:
            x, y = max(a, lo), min(b, hi)
            if x < y:
                pieces.append(wa[first + x - a:first + y - a])
        pieces.append(jnp.zeros((IN_SHARD_PAD - IN_SHARD, wa.shape[1]), wa.dtype))
        slots.append(jnp.concatenate(pieces, axis=0))
    return jnp.stack(slots)


def _lanes_vec(pieces):
    v = jnp.zeros((1, LANES), f32)
    for off, a in pieces:
        v = lax.dynamic_update_slice(v, a.astype(f32), (0, off))
    return v


def local_step(x, mem, target, w, sp, *, B):
    T = x.shape[0]
    S = T // B
    gq8, gk8 = jnp.tile(sp["fox_qnorm_g"], (1, FOX_HEADS)), jnp.tile(sp["fox_knorm_g"], (1, FOX_HEADS))
    go2 = jnp.tile(sp["fox_onorm_g"], (1, 2))
    bias = _lanes_vec([(SM_F, sp["fox_f_bias"]), (SM_A, sp["gdn_dt_bias"])])
    alog = _lanes_vec([(SM_A, sp["gdn_A_log"])])

    h1 = rms_fwd(x, sp["norm_mix_g"], name="rms_mix")
    P = matmul(h1, w["wa_t"], tb=True, name="mm_in", tn=IN_TILE)
    gates = gates_fwd(P, bias, alog, B=B)
    c = gates[:, SM_F:SM_F + FOX_HEADS].reshape(B, S, FOX_HEADS).transpose(0, 2, 1)
    ccol, crow = c[..., None], c.reshape(B, FOX_HEADS, S // FOX_TQ, 1, FOX_TQ)
    qn, kn, vb = fox_prep_fwd(P, gq8, gk8)
    o_raw, o_a, lse = fox_core_fwd(qn, kn, vb, ccol, crow, go2, B=B)
    G = gdn_prep_fwd(P, w["conv_w"], B=B)
    ob_raw, o_b, states, gdn_tm = gdn_fwd(G, gates, P, sp["gdn_onorm_g"], B=B)
    oab = jnp.concatenate([o_a, o_b], axis=1)
    if "late" in w:
        w = {**w, **w["late"](oab)}
    x2 = matmul(oab, w["w_out"], residual=x, name="mm_out")
    hq = rms_fwd(x2, sp["norm_xattn_g"], name="rms_xattn")
    hm = rms_fwd(mem, sp["mem_norm_g"], name="rms_mem")
    cq = matmul(hq, w["w_cq"], name="mm_cq")
    ckv = matmul(hm, w["w_ckv"], name="mm_ckv")
    co = xattn_fwd(cq, ckv, sp["xattn_qnorm_g"], sp["xattn_knorm_g"], B=B)
    x3 = matmul(co, w["w_co"], residual=x2, name="mm_co")
    hf = rms_fwd(x3, sp["norm_mlp_g"], name="rms_mlp")
    act = matmul(hf, w["w_mlp1"], b_stacked=True, relu2_out=True, out_dtype=MXU_DTYPE, name="mm_mlp1")
    dy, dy_op, loss = matmul_rows(act, w["w_mlp2"], (x3, target), mode="loss", name="mm_mlp2_loss")

    da = matmul(dy_op, w["w_mlp2"], tb=True, relu2_bwd_aux=act, out_dtype=MXU_DTYPE, name="mm_d_act")
    g_mlp2 = matmul(act, dy_op, ta=True, out_dtype=WIRE_DTYPE, name="mm_g_mlp2")
    g_mlp1 = matmul(hf, da, ta=True, out_stacked=True, out_dtype=WIRE_DTYPE, name="mm_g_mlp1")
    by_rows = lambda g: g.reshape(N_CHIPS, g.shape[0] // N_CHIPS, g.shape[1])
    early = w.get("grads_ready", lambda grads: jnp.zeros((1, 1), f32))
    tok = early(dict(w_mlp1=g_mlp1, w_mlp2=by_rows(g_mlp2)))[0, 0]
    dx3, g_norm_mlp = matmul_rows(da, w["w_mlp1"], (x3, sp["norm_mlp_g"] + tok, dy), mode="rms_bwd", tb=True,
                                  b_stacked=True, name="mm_d_hf_rms")
    dco = matmul(dx3, w["w_co"], tb=True, name="mm_d_co")
    g_co = matmul(co, dx3, ta=True, out_dtype=WIRE_DTYPE, name="mm_g_co")
    g_co = g_co.reshape(XATTN_WIDTH, N_CHIPS, D_MODEL // N_CHIPS).transpose(1, 0, 2)
    dcq, dckv, g_xq, g_xk = xattn_bwd(cq, ckv, sp["xattn_qnorm_g"], sp["xattn_knorm_g"], dco, B=B)
    g_cq = matmul(hq, dcq, ta=True, out_dtype=WIRE_DTYPE, name="mm_g_cq")
    g_ckv = matmul(hm, dckv, ta=True, out_dtype=WIRE_DTYPE, name="mm_g_ckv")
    _, g_mem_norm = matmul_rows(dckv, w["w_ckv"], (mem, sp["mem_norm_g"], None), mode="rms_bwd", tb=True, name="mm_d_hm_rms")
    dx2, g_norm_xattn = matmul_rows(dcq, w["w_cq"], (x2, sp["norm_xattn_g"], dx3), mode="rms_bwd", tb=True, name="mm_d_hq_rms")
    doab = matmul(dx2, w["w_out"], tb=True, name="mm_d_oab")
    g_out = matmul(oab, dx2, ta=True, out_dtype=WIRE_DTYPE, name="mm_g_out")
    tok = early(dict(w_co=g_co, w_cq=by_rows(g_cq), w_ckv=by_rows(g_ckv), w_out=by_rows(g_out)))[0, 0]
    dqn, dkn, dv_f, dckey, dcrow, dgo2 = fox_core_bwd(qn, kn, vb, ccol, crow, go2 + tok, o_raw, lse, doab, B=B)
    dq_f, dk_f, dgq8, dgk8 = fox_prep_bwd(P, gq8, gk8, dqn, dkn)
    dGq, dGk, dGv, dgt, dz, g_gdn_on = gdn_bwd(G, gates, P, sp["gdn_onorm_g"], ob_raw, states, gdn_tm, doab, B=B)
    dPg, g_conv = gdn_prep_bwd(P, w["conv_w"], dGq, dGk, dGv, B=B)
    dc = (dckey[:, :, 0, :] + dcrow.reshape(B, FOX_HEADS, S)).transpose(0, 2, 1).reshape(T, FOX_HEADS)
    dgates = dgt + jnp.pad(dc, ((0, 0), (SM_F, LANES - SM_F - FOX_HEADS)))
    dsmall, par = gates_bwd(P, bias, alog, dgates, B=B)
    dP = jnp.concatenate([dq_f, dk_f, dv_f, dPg, dz, dsmall, jnp.zeros((T, IN_ALIGNED - COL_SMALL - LANES), MXU_DTYPE)], axis=1)
    g_wa = matmul(dP, h1, ta=True, out_dtype=WIRE_DTYPE, name="mm_g_in", tm=IN_TILE)
    g_in = unalign_to_slots(g_wa)
    tok = early(dict(w_in=g_in))[0, 0]
    dx, g_norm_mix = matmul_rows(dP, w["wa_t"], (x, sp["norm_mix_g"] + tok, dx2), mode="rms_bwd", tk=IN_TILE,
                                 name="mm_d_h1_rms")

    fold = lambda g: jnp.sum(g.reshape(-1, FOX_HEAD_DIM), axis=0, keepdims=True)
    big = dict(w_in=g_in, w_out=by_rows(g_out), w_cq=by_rows(g_cq), w_ckv=by_rows(g_ckv), w_co=g_co, w_mlp1=g_mlp1,
               w_mlp2=by_rows(g_mlp2))
    small = dict(norm_mix_g=g_norm_mix, fox_qnorm_g=fold(dgq8), fox_knorm_g=fold(dgk8),
                 fox_f_bias=par[0:1, SM_F:SM_F + FOX_HEADS], fox_onorm_g=fold(dgo2), gdn_conv_w=g_conv,
                 gdn_A_log=par[1:2, SM_A:SM_A + GDN_HEADS], gdn_dt_bias=par[0:1, SM_A:SM_A + GDN_HEADS],
                 gdn_onorm_g=g_gdn_on, norm_xattn_g=g_norm_xattn, mem_norm_g=g_mem_norm,
                 xattn_qnorm_g=g_xq, xattn_knorm_g=g_xk, norm_mlp_g=g_norm_mlp)
    return loss, dx, big, small


MESH_IDS = pl.DeviceIdType.MESH
N_CHIPS = 4
HBM_SPEC = pl.BlockSpec(memory_space=pltpu.HBM)
PACK_ROWS = 30720
PACK_HALF = PACK_ROWS // 2
PACK_BLOCK = 3072


def _place():
    return lax.axis_index("x"), lax.axis_index("y"), lax.axis_index("c")


def _other_chips(x, y):
    return [(1 - x, y), (x, 1 - y), (1 - x, 1 - y)]


def _remote(src, dst, send_sem, recv_sem, to):
    return pltpu.make_async_remote_copy(src_ref=src, dst_ref=dst, send_sem=send_sem, recv_sem=recv_sem,
                                        device_id=to, device_id_type=MESH_IDS)


def all_gather_shards(packed):
    half = PACK_HALF

    def body(src_ref, out_ref, send_sems, recv_sems):
        x, y, c = _place()
        me_chip = 2 * x + y
        sibling = (x, y, 1 - c)
        chips = _other_chips(x, y)

        def rows(chip, core):
            return out_ref.at[chip, pl.ds(core * half, half), :]

        sends = [_remote(src_ref.at[pl.ds(c * half, half), :], rows(me_chip, c), send_sems.at[j], recv_sems.at[j], (px, py, c))
                 for j, (px, py) in enumerate(chips)]
        for cp in sends:
            cp.start()
        passed = []
        for j, (px, py) in enumerate(chips):
            theirs = rows(2 * px + py, c)
            _remote(theirs, theirs, send_sems.at[j], recv_sems.at[j], (px, py, c)).wait_recv()
            cp = _remote(theirs, theirs, send_sems.at[3 + j], recv_sems.at[3 + j], sibling)
            cp.start()
            passed.append(cp)
        for j, (px, py) in enumerate(chips):
            theirs = rows(2 * px + py, 1 - c)
            _remote(theirs, theirs, send_sems.at[3 + j], recv_sems.at[3 + j], sibling).wait_recv()
        for cp in sends + passed:
            cp.wait_send()

    return pl.pallas_call(
        body, name="all_gather_shards", in_specs=[HBM_SPEC], out_specs=HBM_SPEC,
        out_shape=jax.ShapeDtypeStruct((N_CHIPS,) + packed.shape, packed.dtype),
        scratch_shapes=[pltpu.SemaphoreType.DMA((6,)), pltpu.SemaphoreType.DMA((6,))],
    )(packed)


def exchange_core_halves(G):
    half = PACK_HALF

    def body(g_ref, land_ref, send_sem, recv_sem):
        x, y, c = _place()
        cp = _remote(g_ref.at[:, pl.ds((1 - c) * half, half), :], land_ref, send_sem, recv_sem, (x, y, 1 - c))
        cp.start()
        cp.wait()

    return pl.pallas_call(
        body, name="exchange_core_halves", in_specs=[HBM_SPEC], out_specs=HBM_SPEC,
        out_shape=jax.ShapeDtypeStruct((N_CHIPS, half, LANES), G.dtype),
        scratch_shapes=[pltpu.SemaphoreType.DMA(()), pltpu.SemaphoreType.DMA(())],
    )(G)


def add_core_halves(G, land, core):
    nb = PACK_HALF // PACK_BLOCK

    def body(c_ref, g_ref, l_ref, o_ref):
        o_ref[...] = (g_ref[...].astype(f32) + l_ref[...].astype(f32)).astype(o_ref.dtype)

    blk = (1, PACK_BLOCK, LANES)
    return pl.pallas_call(
        body, name="add_core_halves",
        grid_spec=pltpu.PrefetchScalarGridSpec(
            num_scalar_prefetch=1, grid=(N_CHIPS, nb),
            in_specs=[pl.BlockSpec(blk, lambda k, i, c_ref: (k, c_ref[0] * nb + i, 0)),
                      pl.BlockSpec(blk, lambda k, i, c_ref: (k, i, 0))],
            out_specs=pl.BlockSpec(blk, lambda k, i, c_ref: (k, i, 0))),
        out_shape=jax.ShapeDtypeStruct(land.shape, land.dtype),
        compiler_params=_cparams("parallel", "parallel"),
    )(core, G, land)


def scatter_to_chips(part):
    def body(p_ref, land_ref, send_sems, recv_sems):
        x, y, c = _place()
        me_chip = 2 * x + y
        chips = _other_chips(x, y)
        sends = [_remote(p_ref.at[2 * px + py], land_ref.at[me_chip], send_sems.at[j], recv_sems.at[j], (px, py, c))
                 for j, (px, py) in enumerate(chips)]
        for cp in sends:
            cp.start()
        for j, (px, py) in enumerate(chips):
            slot = land_ref.at[2 * px + py]
            _remote(slot, slot, send_sems.at[j], recv_sems.at[j], (px, py, c)).wait_recv()
        for cp in sends:
            cp.wait_send()

    return pl.pallas_call(
        body, name="scatter_to_chips", in_specs=[HBM_SPEC], out_specs=HBM_SPEC,
        out_shape=jax.ShapeDtypeStruct(part.shape, part.dtype),
        scratch_shapes=[pltpu.SemaphoreType.DMA((3,)), pltpu.SemaphoreType.DMA((3,))],
    )(part)


def sum_chips(part, land, order):
    nb = PACK_HALF // PACK_BLOCK

    def body(order_ref, p_ref, l1_ref, l2_ref, l3_ref, o_ref):
        o_ref[...] = ((p_ref[0].astype(f32) + l1_ref[0].astype(f32)) + l2_ref[0].astype(f32)) + l3_ref[0].astype(f32)

    slot = lambda j: pl.BlockSpec((1, PACK_BLOCK, LANES), lambda i, order_ref: (order_ref[j], i, 0))
    return pl.pallas_call(
        body, name="sum_chips",
        grid_spec=pltpu.PrefetchScalarGridSpec(
            num_scalar_prefetch=1, grid=(nb,), in_specs=[slot(0), slot(1), slot(2), slot(3)],
            out_specs=pl.BlockSpec((PACK_BLOCK, LANES), lambda i, order_ref: (i, 0))),
        out_shape=jax.ShapeDtypeStruct((PACK_HALF, LANES), f32),
        compiler_params=_cparams("parallel"),
    )(order, part, land, land, land)


def swap_core_halves(red):
    def body(r_ref, out_ref, send_sem, recv_sem):
        x, y, c = _place()
        cp = _remote(r_ref, out_ref, send_sem, recv_sem, (x, y, 1 - c))
        cp.start()
        cp.wait()

    return pl.pallas_call(
        body, name="swap_core_halves", in_specs=[HBM_SPEC], out_specs=HBM_SPEC,
        out_shape=jax.ShapeDtypeStruct(red.shape, red.dtype),
        scratch_shapes=[pltpu.SemaphoreType.DMA(()), pltpu.SemaphoreType.DMA(())],
    )(red)


def _half(ref, core):
    rows = ref.shape[-2] // 2
    return ref.at[(slice(None),) * (len(ref.shape) - 2) + (pl.ds(core * rows, rows), slice(None))]


def gather_weights(shards, conv):
    n = len(shards)

    def body(*refs):
        src, conv_src = refs[:n], refs[n]
        out, conv_out = refs[n + 1:2 * n + 1], refs[2 * n + 1]
        send_sems, recv_sems = refs[2 * n + 2], refs[2 * n + 3]
        x, y, c = _place()
        me_chip = 2 * x + y
        sibling = (x, y, 1 - c)
        chips = _other_chips(x, y)
        sends = []
        for a in range(n):
            for j, (px, py) in enumerate(chips):
                sends.append(_remote(_half(src[a], c), _half(out[a].at[me_chip], c),
                                     send_sems.at[6 * a + j], recv_sems.at[6 * a + j], (px, py, c)))
        for j, (px, py) in enumerate(chips):
            sends.append(_remote(conv_src, conv_out.at[me_chip], send_sems.at[6 * n + j], recv_sems.at[6 * n + j], (px, py, c)))
        for cp in sends:
            cp.start()
        passed = []
        for a in range(n):
            for j, (px, py) in enumerate(chips):
                theirs = _half(out[a].at[2 * px + py], c)
                _remote(theirs, theirs, send_sems.at[6 * a + j], recv_sems.at[6 * a + j], (px, py, c)).wait_recv()
                cp = _remote(theirs, theirs, send_sems.at[6 * a + 3 + j], recv_sems.at[6 * a + 3 + j], sibling)
                cp.start()
                passed.append(cp)
        for j, (px, py) in enumerate(chips):
            theirs = conv_out.at[2 * px + py]
            _remote(theirs, theirs, send_sems.at[6 * n + j], recv_sems.at[6 * n + j], (px, py, c)).wait_recv()
        for a in range(n):
            for j, (px, py) in enumerate(chips):
                theirs = _half(out[a].at[2 * px + py], 1 - c)
                _remote(theirs, theirs, send_sems.at[6 * a + 3 + j], recv_sems.at[6 * a + 3 + j], sibling).wait_recv()
        for cp in sends + passed:
            cp.wait_send()

    return pl.pallas_call(
        body, name="gather_weights", in_specs=[HBM_SPEC] * (n + 1), out_specs=[HBM_SPEC] * (n + 1),
        out_shape=[jax.ShapeDtypeStruct((N_CHIPS,) + s.shape, s.dtype) for s in list(shards) + [conv]],
        scratch_shapes=[pltpu.SemaphoreType.DMA((6 * n + 3,)), pltpu.SemaphoreType.DMA((6 * n + 3,))],
    )(*shards, conv)


SEM_SPEC = pl.BlockSpec(memory_space=pltpu.SEMAPHORE)
SPLIT_EFFECT = pltpu.SideEffectType.DATAFLOW_SIDE_EFFECTING


def _gather_async_copies(src, land, send_sems, recv_sems, x, y, c):
    me_chip = 2 * x + y
    sends, arrivals = [], []
    for a in range(len(src)):
        for j, (px, py) in enumerate(_other_chips(x, y)):
            for core in range(2):
                sends.append(_remote(_half(src[a], c), _half(land[a].at[me_chip], c), send_sems.at[6 * a + 2 * j + core],
                                     recv_sems.at[6 * a + 2 * j + c], (px, py, core)))
                theirs = _half(land[a].at[2 * px + py], core)
                arrivals.append(_remote(theirs, theirs, send_sems.at[6 * a + 2 * j + core],
                                        recv_sems.at[6 * a + 2 * j + core], (px, py, core)))
    return sends, arrivals


def gather_weights_start(shards, after):
    n = len(shards)

    def body(*refs):
        src, land = refs[:n], refs[n:2 * n]
        send_sems, recv_sems, token = refs[2 * n + 1], refs[2 * n + 2], refs[4 * n + 3]
        x, y, c = _place()
        for cp in _gather_async_copies(src, land, send_sems, recv_sems, x, y, c)[0]:
            cp.start()
        token[...] = jnp.zeros_like(token)

    zones = [pltpu.with_memory_space_constraint(lax.empty((N_CHIPS,) + s.shape, s.dtype), pltpu.HBM) for s in shards]
    srcs = [pltpu.with_memory_space_constraint(s, pltpu.HBM) for s in shards]
    out = pl.pallas_call(
        body, name="gather_weights_start",
        out_shape=[pltpu.SemaphoreType.DMA((6 * n,)), pltpu.SemaphoreType.DMA((6 * n,))]
        + [pltpu.HBM(s.shape, s.dtype) for s in shards] + [pltpu.HBM(z.shape, z.dtype) for z in zones]
        + [jax.ShapeDtypeStruct((8, LANES), f32)],
        in_specs=[HBM_SPEC] * (2 * n) + [pl.BlockSpec(memory_space=pl.ANY)],
        out_specs=[SEM_SPEC, SEM_SPEC] + [HBM_SPEC] * (2 * n) + [pl.BlockSpec(memory_space=pltpu.VMEM)],
        input_output_aliases={i: 2 + i for i in range(2 * n)},
        compiler_params=pltpu.CompilerParams(has_side_effects=SPLIT_EFFECT),
    )(*srcs, *zones, after)
    return out[0], out[1], out[2:2 + n], out[2 + n:2 + 2 * n], out[-1]


def gather_weights_wait(send_sems, recv_sems, shards, zones, after):
    n = len(shards)

    def body(*refs):
        src, land = refs[:n], refs[n:2 * n]
        send_sems, recv_sems = refs[2 * n], refs[2 * n + 1]
        x, y, c = _place()
        sends, arrivals = _gather_async_copies(src, land, send_sems, recv_sems, x, y, c)
        for cp in sends:
            cp.wait_send()
        for cp in arrivals:
            cp.wait_recv()

    out = pl.pallas_call(
        body, name="gather_weights_wait",
        out_shape=[pltpu.HBM(s.shape, s.dtype) for s in shards] + [pltpu.HBM(z.shape, z.dtype) for z in zones],
        in_specs=[HBM_SPEC] * (2 * n) + [SEM_SPEC, SEM_SPEC, pl.BlockSpec(memory_space=pl.ANY)],
        out_specs=[HBM_SPEC] * (2 * n),
        input_output_aliases={i: i for i in range(2 * n)},
        compiler_params=pltpu.CompilerParams(has_side_effects=SPLIT_EFFECT),
    )(*shards, *zones, send_sems, recv_sems, after)
    return out[n:]


def swap_grad_halves(grads, *, name):
    n = len(grads)

    def body(*refs):
        g, land, send_sems, recv_sems = refs[:n], refs[n:2 * n], refs[2 * n], refs[2 * n + 1]
        x, y, c = _place()
        copies = [_remote(_half(g[a], 1 - c), land[a], send_sems.at[a], recv_sems.at[a], (x, y, 1 - c)) for a in range(n)]
        for cp in copies:
            cp.start()
        for cp in copies:
            cp.wait()

    return pl.pallas_call(
        body, name=name, in_specs=[HBM_SPEC] * n, out_specs=[HBM_SPEC] * n,
        out_shape=[jax.ShapeDtypeStruct((N_CHIPS, g.shape[1] // 2, g.shape[2]), g.dtype) for g in grads],
        scratch_shapes=[pltpu.SemaphoreType.DMA((n,)), pltpu.SemaphoreType.DMA((n,))],
    )(*grads)


GRAD_ROWS = 256


def add_grad_halves(g, land, core, *, name):
    _, half, cols = land.shape
    tr = GRAD_ROWS if half % GRAD_ROWS == 0 else half
    nb = half // tr

    def body(c_ref, g_ref, l_ref, o_ref):
        o_ref[...] = (g_ref[...].astype(f32) + l_ref[...].astype(f32)).astype(o_ref.dtype)

    blk = (1, tr, cols)
    return pl.pallas_call(
        body, name=name,
        grid_spec=pltpu.PrefetchScalarGridSpec(
            num_scalar_prefetch=1, grid=(N_CHIPS, nb),
            in_specs=[pl.BlockSpec(blk, lambda k, i, c_ref: (k, c_ref[0] * nb + i, 0)),
                      pl.BlockSpec(blk, lambda k, i, c_ref: (k, i, 0))],
            out_specs=pl.BlockSpec(blk, lambda k, i, c_ref: (k, i, 0))),
        out_shape=jax.ShapeDtypeStruct(land.shape, land.dtype),
        compiler_params=_cparams("parallel", "parallel"),
    )(core, g, land)


def scatter_grads(parts):
    n = len(parts)

    def body(*refs):
        p, land, send_sems, recv_sems = refs[:n], refs[n:2 * n], refs[2 * n], refs[2 * n + 1]
        x, y, c = _place()
        me_chip = 2 * x + y
        chips = _other_chips(x, y)
        sends = [_remote(p[a].at[2 * px + py], land[a].at[me_chip], send_sems.at[3 * a + j], recv_sems.at[3 * a + j], (px, py, c))
                 for a in range(n) for j, (px, py) in enumerate(chips)]
        for cp in sends:
            cp.start()
        for a in range(n):
            for j, (px, py) in enumerate(chips):
                slot = land[a].at[2 * px + py]
                _remote(slot, slot, send_sems.at[3 * a + j], recv_sems.at[3 * a + j], (px, py, c)).wait_recv()
        for cp in sends:
            cp.wait_send()

    return pl.pallas_call(
        body, name="scatter_grads", in_specs=[HBM_SPEC] * n, out_specs=[HBM_SPEC] * n,
        out_shape=[jax.ShapeDtypeStruct(p.shape, p.dtype) for p in parts],
        scratch_shapes=[pltpu.SemaphoreType.DMA((3 * n,)), pltpu.SemaphoreType.DMA((3 * n,))],
    )(*parts)


def _scatter_async_copies(parts, land, send_sems, recv_sems, x, y, c):
    me_chip = 2 * x + y
    sends, arrivals = [], []
    for a in range(len(parts)):
        for j, (px, py) in enumerate(_other_chips(x, y)):
            sems = (send_sems.at[3 * a + j], recv_sems.at[3 * a + j], (px, py, c))
            sends.append(_remote(parts[a].at[2 * px + py], land[a].at[me_chip], *sems))
            slot = land[a].at[2 * px + py]
            arrivals.append(_remote(slot, slot, *sems))
    return sends, arrivals


def scatter_grads_start(parts, *, name):
    n = len(parts)

    def body(*refs):
        p, land = refs[:n], refs[n:2 * n]
        send_sems, recv_sems, token = refs[2 * n], refs[2 * n + 1], refs[4 * n + 2]
        x, y, c = _place()
        for cp in _scatter_async_copies(p, land, send_sems, recv_sems, x, y, c)[0]:
            cp.start()
        token[...] = jnp.zeros_like(token)

    zones = [pltpu.with_memory_space_constraint(lax.empty(p.shape, p.dtype), pltpu.HBM) for p in parts]
    srcs = [pltpu.with_memory_space_constraint(p, pltpu.HBM) for p in parts]
    hbm = [pltpu.HBM(p.shape, p.dtype) for p in parts]
    out = pl.pallas_call(
        body, name=name,
        out_shape=[pltpu.SemaphoreType.DMA((3 * n,)), pltpu.SemaphoreType.DMA((3 * n,))] + hbm + hbm
        + [jax.ShapeDtypeStruct((8, LANES), f32)],
        in_specs=[HBM_SPEC] * (2 * n),
        out_specs=[SEM_SPEC, SEM_SPEC] + [HBM_SPEC] * (2 * n) + [pl.BlockSpec(memory_space=pltpu.VMEM)],
        input_output_aliases={i: 2 + i for i in range(2 * n)},
        compiler_params=pltpu.CompilerParams(has_side_effects=SPLIT_EFFECT),
    )(*srcs, *zones)
    return out[0], out[1], out[2:2 + n], out[2 + n:2 + 2 * n], out[-1]


def scatter_grads_wait(send_sems, recv_sems, parts, zones, after, *, name):
    n = len(parts)

    def body(*refs):
        p, land = refs[:n], refs[n:2 * n]
        x, y, c = _place()
        sends, arrivals = _scatter_async_copies(p, land, refs[2 * n], refs[2 * n + 1], x, y, c)
        for cp in sends:
            cp.wait_send()
        for cp in arrivals:
            cp.wait_recv()

    hbm = [pltpu.HBM(p.shape, p.dtype) for p in parts]
    out = pl.pallas_call(
        body, name=name, out_shape=hbm + hbm,
        in_specs=[HBM_SPEC] * (2 * n) + [SEM_SPEC, SEM_SPEC, pl.BlockSpec(memory_space=pl.ANY)],
        out_specs=[HBM_SPEC] * (2 * n),
        input_output_aliases={i: i for i in range(2 * n)},
        compiler_params=pltpu.CompilerParams(has_side_effects=SPLIT_EFFECT),
    )(*parts, *zones, send_sems, recv_sems, after)
    return out[:n], out[n:]


def sum_grads(part, land, order, *, name):
    _, half, cols = part.shape
    tr = GRAD_ROWS if half % GRAD_ROWS == 0 else half

    def body(order_ref, p_ref, l1_ref, l2_ref, l3_ref, o_ref):
        o_ref[...] = ((p_ref[0].astype(f32) + l1_ref[0].astype(f32)) + l2_ref[0].astype(f32)) + l3_ref[0].astype(f32)

    slot = lambda j: pl.BlockSpec((1, tr, cols), lambda i, order_ref: (order_ref[j], i, 0))
    return pl.pallas_call(
        body, name=name,
        grid_spec=pltpu.PrefetchScalarGridSpec(
            num_scalar_prefetch=1, grid=(half // tr,), in_specs=[slot(0), slot(1), slot(2), slot(3)],
            out_specs=pl.BlockSpec((tr, cols), lambda i, order_ref: (i, 0))),
        out_shape=jax.ShapeDtypeStruct((half, cols), f32),
        compiler_params=_cparams("parallel"),
    )(order, part, land, land, land)


def _peer(x, y, c, r):
    return ((1 - x) if r & 4 else x, (1 - y) if r & 2 else y, (1 - c) if r & 1 else c)


def _reduce_async_copies(grads, land, send_sems, recv_sems, x, y, c):
    me = 4 * x + 2 * y + c
    sends, arrivals = [], []
    for a in range(len(grads)):
        for r in range(1, N_DEV):
            px, py, pc = _peer(x, y, c, r)
            sems = (send_sems.at[7 * a + r - 1], recv_sems.at[7 * a + r - 1], (px, py, pc))
            sends.append(_remote(_half(grads[a].at[2 * px + py], pc), land[a].at[me], *sems))
            slot = land[a].at[4 * px + 2 * py + pc]
            arrivals.append(_remote(slot, slot, *sems))
    return sends, arrivals


def reduce_grads_start(grads, *, name):
    n = len(grads)

    def body(*refs):
        g, land = refs[:n], refs[n:2 * n]
        send_sems, recv_sems, token = refs[2 * n], refs[2 * n + 1], refs[4 * n + 2]
        x, y, c = _place()
        for cp in _reduce_async_copies(g, land, send_sems, recv_sems, x, y, c)[0]:
            cp.start()
        token[...] = jnp.zeros_like(token)

    zones = [pltpu.with_memory_space_constraint(lax.empty((N_DEV, g.shape[1] // 2, g.shape[2]), g.dtype), pltpu.HBM)
             for g in grads]
    srcs = [pltpu.with_memory_space_constraint(g, pltpu.HBM) for g in grads]
    out = pl.pallas_call(
        body, name=name,
        out_shape=[pltpu.SemaphoreType.DMA((7 * n,)), pltpu.SemaphoreType.DMA((7 * n,))]
        + [pltpu.HBM(g.shape, g.dtype) for g in grads] + [pltpu.HBM(z.shape, z.dtype) for z in zones]
        + [jax.ShapeDtypeStruct((8, LANES), f32)],
        in_specs=[HBM_SPEC] * (2 * n),
        out_specs=[SEM_SPEC, SEM_SPEC] + [HBM_SPEC] * (2 * n) + [pl.BlockSpec(memory_space=pltpu.VMEM)],
        input_output_aliases={i: 2 + i for i in range(2 * n)},
        compiler_params=pltpu.CompilerParams(has_side_effects=SPLIT_EFFECT),
    )(*srcs, *zones)
    return out[0], out[1], out[2:2 + n], out[2 + n:2 + 2 * n], out[-1]


def reduce_grads_wait(send_sems, recv_sems, grads, zones, after, *, name):
    n = len(grads)

    def body(*refs):
        g, land = refs[:n], refs[n:2 * n]
        x, y, c = _place()
        sends, arrivals = _reduce_async_copies(g, land, refs[2 * n], refs[2 * n + 1], x, y, c)
        for cp in sends:
            cp.wait_send()
        for cp in arrivals:
            cp.wait_recv()

    hbm = [pltpu.HBM(a.shape, a.dtype) for a in list(grads) + list(zones)]
    out = pl.pallas_call(
        body, name=name, out_shape=hbm,
        in_specs=[HBM_SPEC] * (2 * n) + [SEM_SPEC, SEM_SPEC, pl.BlockSpec(memory_space=pl.ANY)],
        out_specs=[HBM_SPEC] * (2 * n),
        input_output_aliases={i: i for i in range(2 * n)},
        compiler_params=pltpu.CompilerParams(has_side_effects=SPLIT_EFFECT),
    )(*grads, *zones, send_sems, recv_sems, after)
    return out[:n], out[n:]


def sum_partials(g, land, where, *, name):
    _, half, cols = land.shape
    tr = GRAD_ROWS if half % GRAD_ROWS == 0 else half
    nb = half // tr

    def body(where_ref, g_ref, *rest):
        o_ref = rest[-1]
        acc = g_ref[0].astype(f32)
        for l_ref in rest[:-1]:
            acc = acc + l_ref[0].astype(f32)
        o_ref[...] = acc

    blk = (1, tr, cols)
    slot = lambda j: pl.BlockSpec(blk, lambda i, where_ref: (where_ref[2 + j], i, 0))
    return pl.pallas_call(
        body, name=name,
        grid_spec=pltpu.PrefetchScalarGridSpec(
            num_scalar_prefetch=1, grid=(nb,),
            in_specs=[pl.BlockSpec(blk, lambda i, where_ref: (where_ref[0], where_ref[1] * nb + i, 0))]
            + [slot(j) for j in range(N_DEV - 1)],
            out_specs=pl.BlockSpec((tr, cols), lambda i, where_ref: (i, 0))),
        out_shape=jax.ShapeDtypeStruct((half, cols), f32),
        compiler_params=_cparams("parallel"),
    )(where, g, *([land] * (N_DEV - 1)))


def swap_reduced_halves(mine, *, name):
    n = len(mine)

    def body(*refs):
        r, out, send_sems, recv_sems = refs[:n], refs[n:2 * n], refs[2 * n], refs[2 * n + 1]
        x, y, c = _place()
        copies = [_remote(r[a], out[a], send_sems.at[a], recv_sems.at[a], (x, y, 1 - c)) for a in range(n)]
        for cp in copies:
            cp.start()
        for cp in copies:
            cp.wait()

    return pl.pallas_call(
        body, name=name, in_specs=[HBM_SPEC] * n, out_specs=[HBM_SPEC] * n,
        out_shape=[jax.ShapeDtypeStruct(r.shape, r.dtype) for r in mine],
        scratch_shapes=[pltpu.SemaphoreType.DMA((n,)), pltpu.SemaphoreType.DMA((n,))],
    )(*mine)


def adamw_halves(w, mine, theirs, m, v, core, *, name):
    R, C = w.shape
    tr = min(GRAD_ROWS, R // 2)
    half_nb = R // 2 // tr

    def body(c_ref, w_ref, a_ref, b_ref, m_ref, v_ref, g_ref, d_ref, nm_ref, nv_ref):
        low = pl.program_id(0) < half_nb
        gv = jnp.where(low == (c_ref[0] == 0), a_ref[...], b_ref[...])
        nm = ADAM_B1 * m_ref[...] + (1.0 - ADAM_B1) * gv
        nv = ADAM_B2 * v_ref[...] + (1.0 - ADAM_B2) * jnp.square(gv)
        m_hat = nm / (1.0 - ADAM_B1 ** ADAM_STEP)
        v_hat = nv / (1.0 - ADAM_B2 ** ADAM_STEP)
        g_ref[...] = gv
        d_ref[...] = -ADAM_LR * (m_hat / (jnp.sqrt(v_hat) + ADAM_EPS) + ADAM_WD * w_ref[...])
        nm_ref[...] = nm
        nv_ref[...] = nv

    full = pl.BlockSpec((tr, C), lambda i, c_ref: (i, 0))
    part = pl.BlockSpec((tr, C), lambda i, c_ref: (i % half_nb, 0))
    out = jax.ShapeDtypeStruct((R, C), f32)
    return pl.pallas_call(
        body, name=name,
        grid_spec=pltpu.PrefetchScalarGridSpec(
            num_scalar_prefetch=1, grid=(2 * half_nb,), in_specs=[full, part, part, full, full], out_specs=[full] * 4),
        out_shape=[out] * 4, compiler_params=_cparams("parallel"),
    )(core, w, mine, theirs, m, v)


N_DEV = 8


def all_reduce_small(v):
    def body(src_ref, out_ref, land_ref, send_sems, recv_sems):
        x, y, c = _place()
        me = 4 * x + 2 * y + c
        copies = []
        for r in range(1, N_DEV):
            peer = ((1 - x) if r & 4 else x, (1 - y) if r & 2 else y, (1 - c) if r & 1 else c)
            copies.append(_remote(src_ref, land_ref.at[r], send_sems.at[r - 1], recv_sems.at[r - 1], peer))
        for cp in copies:
            cp.start()
        land_ref[0] = src_ref[...]
        for cp in copies:
            cp.wait()
        acc = land_ref[me]
        for d in range(1, N_DEV):
            acc = acc + land_ref[jnp.bitwise_xor(me, d)]
        out_ref[...] = acc

    vm = pl.BlockSpec(memory_space=pltpu.VMEM)
    return pl.pallas_call(
        body, name="all_reduce_small", in_specs=[vm], out_specs=vm,
        out_shape=jax.ShapeDtypeStruct(v.shape, v.dtype),
        scratch_shapes=[pltpu.VMEM((N_DEV,) + v.shape, v.dtype),
                        pltpu.SemaphoreType.DMA((N_DEV - 1,)), pltpu.SemaphoreType.DMA((N_DEV - 1,))],
    )(v)


def adamw(w, g, m, v, *, name, tr=None, tc=None):
    R, C = w.shape
    if tc is None:
        tr, tc = min(tr, R), C
        blk = pl.BlockSpec((tr, C), lambda i: (i, 0))
    else:
        tr = R
        blk = pl.BlockSpec((R, tc), lambda i: (0, i))

    def body(w_ref, g_ref, m_ref, v_ref, d_ref, nm_ref, nv_ref):
        gv = g_ref[...]
        nm = ADAM_B1 * m_ref[...] + (1.0 - ADAM_B1) * gv
        nv = ADAM_B2 * v_ref[...] + (1.0 - ADAM_B2) * jnp.square(gv)
        m_hat = nm / (1.0 - ADAM_B1 ** ADAM_STEP)
        v_hat = nv / (1.0 - ADAM_B2 ** ADAM_STEP)
        d_ref[...] = -ADAM_LR * (m_hat / (jnp.sqrt(v_hat) + ADAM_EPS) + ADAM_WD * w_ref[...])
        nm_ref[...] = nm
        nv_ref[...] = nv

    out = jax.ShapeDtypeStruct((R, C), f32)
    return pl.pallas_call(
        body, name=name, grid=((R // tr) * (C // tc),), in_specs=[blk] * 4, out_specs=[blk] * 3, out_shape=[out] * 3,
        compiler_params=_cparams("parallel"),
    )(w, g, m, v)


BIG_SHARDS = (("w_in", (1024, 900), True), ("w_out", (256, 1024), False), ("w_cq", (256, 512), False),
              ("w_ckv", (256, 1024), False), ("w_co", (512, 256), True), ("w_mlp1", (1024, 1024), True),
              ("w_mlp2", (1024, 1024), False))
CONV_SHARD = (CONV_WIDTH, 3 * GDN_WIDTH // N_CHIPS)
SMALL_DIMS = (("norm_mix_g", 1024), ("fox_qnorm_g", 64), ("fox_knorm_g", 64), ("fox_f_bias", 8), ("fox_onorm_g", 64),
              ("gdn_A_log", 4), ("gdn_dt_bias", 4), ("gdn_onorm_g", 128), ("norm_xattn_g", 1024), ("mem_norm_g", 1024),
              ("xattn_qnorm_g", 128), ("xattn_knorm_g", 128), ("norm_mlp_g", 1024))
WEIGHT_ORDER = ("norm_mix_g", "w_in", "fox_qnorm_g", "fox_knorm_g", "fox_f_bias", "fox_onorm_g", "gdn_conv_w", "gdn_A_log",
                "gdn_dt_bias", "gdn_onorm_g", "w_out", "norm_xattn_g", "mem_norm_g", "w_cq", "w_ckv", "xattn_qnorm_g",
                "xattn_knorm_g", "w_co", "norm_mlp_g", "w_mlp1", "w_mlp2")


def _pack_rows(pieces, rows, lead=()):
    cat = jnp.concatenate([p.reshape(lead + (-1,)) for p in pieces], axis=-1)
    cat = jnp.pad(cat, [(0, 0)] * len(lead) + [(0, rows * LANES - cat.shape[-1])])
    return cat.reshape(lead + (rows, LANES))


def _unpack_rows(buf, sizes, lead=()):
    flat = buf.reshape(lead + (-1,))
    out, off = [], 0
    for n in sizes:
        out.append(flat[..., off:off + n])
        off += n
    return out


def _conv_to_wire(conv):
    return lax.bitcast_convert_type(conv, bf16)


def _conv_from_wire(wire):
    return lax.bitcast_convert_type(wire, f32)


SMALL_ROWS = 96
SMALL_ADAM_ROWS = 56


def kernel(x, mem, norm_mix_g, w_in, fox_qnorm_g, fox_knorm_g, fox_f_bias, fox_onorm_g, gdn_conv_w, gdn_A_log, gdn_dt_bias, gdn_onorm_g, w_out, norm_xattn_g, mem_norm_g, w_cq, w_ckv, xattn_qnorm_g, xattn_knorm_g, w_co, norm_mlp_g, w_mlp1, w_mlp2, loss_target, m_norm_mix_g, m_w_in, m_fox_qnorm_g, m_fox_knorm_g, m_fox_f_bias, m_fox_onorm_g, m_gdn_conv_w, m_gdn_A_log, m_gdn_dt_bias, m_gdn_onorm_g, m_w_out, m_norm_xattn_g, m_mem_norm_g, m_w_cq, m_w_ckv, m_xattn_qnorm_g, m_xattn_knorm_g, m_w_co, m_norm_mlp_g, m_w_mlp1, m_w_mlp2, v_norm_mix_g, v_w_in, v_fox_qnorm_g, v_fox_knorm_g, v_fox_f_bias, v_fox_onorm_g, v_gdn_conv_w, v_gdn_A_log, v_gdn_dt_bias, v_gdn_onorm_g, v_w_out, v_norm_xattn_g, v_mem_norm_g, v_w_cq, v_w_ckv, v_xattn_qnorm_g, v_xattn_knorm_g, v_w_co, v_norm_mlp_g, v_w_mlp1, v_w_mlp2):
    wts = dict(norm_mix_g=norm_mix_g, w_in=w_in, fox_qnorm_g=fox_qnorm_g, fox_knorm_g=fox_knorm_g, fox_f_bias=fox_f_bias,
               fox_onorm_g=fox_onorm_g, gdn_conv_w=gdn_conv_w, gdn_A_log=gdn_A_log, gdn_dt_bias=gdn_dt_bias,
               gdn_onorm_g=gdn_onorm_g, w_out=w_out, norm_xattn_g=norm_xattn_g, mem_norm_g=mem_norm_g, w_cq=w_cq, w_ckv=w_ckv,
               xattn_qnorm_g=xattn_qnorm_g, xattn_knorm_g=xattn_knorm_g, w_co=w_co, norm_mlp_g=norm_mlp_g, w_mlp1=w_mlp1,
               w_mlp2=w_mlp2)
    mom = dict(norm_mix_g=m_norm_mix_g, w_in=m_w_in, fox_qnorm_g=m_fox_qnorm_g, fox_knorm_g=m_fox_knorm_g,
               fox_f_bias=m_fox_f_bias, fox_onorm_g=m_fox_onorm_g, gdn_conv_w=m_gdn_conv_w, gdn_A_log=m_gdn_A_log,
               gdn_dt_bias=m_gdn_dt_bias, gdn_onorm_g=m_gdn_onorm_g, w_out=m_w_out, norm_xattn_g=m_norm_xattn_g,
               mem_norm_g=m_mem_norm_g, w_cq=m_w_cq, w_ckv=m_w_ckv, xattn_qnorm_g=m_xattn_qnorm_g,
               xattn_knorm_g=m_xattn_knorm_g, w_co=m_w_co, norm_mlp_g=m_norm_mlp_g, w_mlp1=m_w_mlp1, w_mlp2=m_w_mlp2)
    var = dict(norm_mix_g=v_norm_mix_g, w_in=v_w_in, fox_qnorm_g=v_fox_qnorm_g, fox_knorm_g=v_fox_knorm_g,
               fox_f_bias=v_fox_f_bias, fox_onorm_g=v_fox_onorm_g, gdn_conv_w=v_gdn_conv_w, gdn_A_log=v_gdn_A_log,
               gdn_dt_bias=v_gdn_dt_bias, gdn_onorm_g=v_gdn_onorm_g, w_out=v_w_out, norm_xattn_g=v_norm_xattn_g,
               mem_norm_g=v_mem_norm_g, w_cq=v_w_cq, w_ckv=v_w_ckv, xattn_qnorm_g=v_xattn_qnorm_g,
               xattn_knorm_g=v_xattn_knorm_g, w_co=v_w_co, norm_mlp_g=v_norm_mlp_g, w_mlp1=v_w_mlp1, w_mlp2=v_w_mlp2)
    B, S, D = x.shape
    T = B * S
    big_names = [n for n, _, _ in BIG_SHARDS]
    chip = 2 * lax.axis_index("x") + lax.axis_index("y")
    core = lax.axis_index("c").astype(jnp.int32).reshape(1)

    shards = {n: wts[n][0].astype(MXU_DTYPE) for n in big_names[1:]}
    in_t = lambda p: jnp.swapaxes(p[0], 0, 1)
    shards["w_in"] = jnp.pad(in_t(w_in).astype(MXU_DTYPE), ((0, IN_SHARD_PAD - IN_SHARD), (0, 0)))
    w_in_all, conv_all = gather_weights([shards["w_in"]], gdn_conv_w[0])
    late = big_names[1:]
    send_sems, recv_sems, late_src, late_zones, token = gather_weights_start([shards[n] for n in late], conv_all)
    own = lambda g, s: lax.dynamic_update_slice(g, s[None], (chip,) + (0,) * s.ndim)
    full = {"w_in": own(w_in_all, shards["w_in"])}
    conv_full = own(conv_all, gdn_conv_w[0]).transpose(1, 0, 2).reshape(CONV_WIDTH, 3 * GDN_WIDTH)
    rows = lambda g: g.reshape(N_CHIPS * g.shape[1], g.shape[2])

    def late_weights(after):
        zones = gather_weights_wait(send_sems, recv_sems, late_src, late_zones, after)
        got = {n: own(z, shards[n]) for n, z in zip(late, zones)}
        return dict(w_out=rows(got["w_out"]), w_cq=rows(got["w_cq"]), w_ckv=rows(got["w_ckv"]),
                    w_co=got["w_co"].transpose(1, 0, 2).reshape(XATTN_WIDTH, D_MODEL),
                    w_mlp1=got["w_mlp1"], w_mlp2=rows(got["w_mlp2"]))

    in_flight = []

    def grads_ready(ready):
        names = list(ready)
        *started, tok = reduce_grads_start([ready[n] for n in names], name="reduce_grads_start_%d" % len(in_flight))
        in_flight.append((names, *started))
        return tok

    w_in_t = full["w_in"][:, :IN_SHARD].reshape(IN_DIM, D_MODEL)
    w = dict(wa_t=align_w_in_t(w_in_t), conv_w=conv_full, late=late_weights, grads_ready=grads_ready)
    sp = {n: wts[n] for n, _ in SMALL_DIMS}
    sp["norm_mix_g"] = sp["norm_mix_g"] + token[0, 0]

    loss_part, grad_x, g_big, g_small = local_step(x.reshape(T, D), mem.reshape(-1, D), loss_target.reshape(T, D), w, sp, B=B)

    small_pieces = [g_small[n] for n, _ in SMALL_DIMS] + [g_small["gdn_conv_w"], loss_part]
    small_sizes = [d for _, d in SMALL_DIMS] + [CONV_WIDTH * 3 * GDN_WIDTH, LANES]
    red_small = _unpack_rows(all_reduce_small(_pack_rows(small_pieces, SMALL_ROWS)), small_sizes)
    grads = {n: p.reshape(1, d) for (n, d), p in zip(SMALL_DIMS, red_small)}
    conv_grad = lax.dynamic_slice(red_small[-2].reshape(CONV_WIDTH, 3 * GDN_WIDTH), (0, chip * CONV_SHARD[1]), CONV_SHARD)
    grads["gdn_conv_w"] = conv_grad.reshape((1,) + CONV_SHARD)
    loss = red_small[-1][0]

    parts, zones = {}, {}

    def wait_group(k, after):
        names, send_sems, recv_sems, thru, land = in_flight[k]
        thru, land = reduce_grads_wait(send_sems, recv_sems, thru, land, after, name="reduce_grads_wait_%d" % k)
        parts.update(zip(names, thru))
        zones.update(zip(names, land))

    wait_group(0, grad_x)
    wait_group(1, grad_x)
    dev = 2 * chip + core[0]
    where = jnp.stack([chip, core[0]] + [dev ^ r for r in range(1, N_DEV)]).astype(jnp.int32)
    mine = [sum_partials(parts[n], zones[n], where, name="sum_partials_" + n) for n in late]
    theirs = swap_reduced_halves(mine, name="swap_reduced_halves")

    delta, new_m, new_v = {}, {}, {}
    for n, a, b in zip(late, mine, theirs):
        g, d, nm, nv = adamw_halves(wts[n][0], a, b, mom[n][0], var[n][0], core, name="adamw_" + n)
        grads[n], delta[n], new_m[n], new_v[n] = g[None], d[None], nm[None], nv[None]
    wait_group(2, new_v[late[-1]])
    mine_in = sum_partials(parts["w_in"], zones["w_in"], where, name="sum_partials_w_in")
    (theirs_in,) = swap_reduced_halves([mine_in], name="swap_reduced_halves_w_in")
    south = core[0] == 0
    g_in_t = jnp.concatenate([jnp.where(south, mine_in, theirs_in), jnp.where(south, theirs_in, mine_in)])[:IN_SHARD]
    back = lambda t: jnp.swapaxes(t, 0, 1)[None]
    d, nm, nv = adamw(in_t(w_in), g_in_t, in_t(m_w_in), in_t(v_w_in), name="adamw_w_in", tc=256)
    grads["w_in"], delta["w_in"], new_m["w_in"], new_v["w_in"] = back(g_in_t), back(d), back(nm), back(nv)
    small_names = [n for n, _ in SMALL_DIMS] + ["gdn_conv_w"]
    small_sz = [d for _, d in SMALL_DIMS] + [CONV_SHARD[0] * CONV_SHARD[1]]
    packed4 = [_pack_rows([src[n] for n in small_names], SMALL_ADAM_ROWS) for src in (wts, grads, mom, var)]
    outs = adamw(*packed4, name="adamw_small", tr=SMALL_ADAM_ROWS)
    for dst, buf in zip((delta, new_m, new_v), outs):
        for n, p in zip(small_names, _unpack_rows(buf, small_sz)):
            dst[n] = p.reshape(wts[n].shape)

    return (loss, grad_x.reshape(B, S, D), *[grads[n] for n in WEIGHT_ORDER], *[delta[n] for n in WEIGHT_ORDER],
            *[new_m[n] for n in WEIGHT_ORDER], *[new_v[n] for n in WEIGHT_ORDER])
```

```python
import functools

import jax
import jax.numpy as jnp
import numpy as np
from jax import lax
from jax.experimental import pallas as pl
from jax.experimental.pallas import tpu as pltpu

f32 = jnp.float32
bf16 = jnp.bfloat16
MXU_DTYPE = jnp.bfloat16
WIRE_DTYPE = jnp.bfloat16
INV_PRECISION = lax.Precision.HIGH

D_MODEL = 1024
FOX_HEADS = 8
FOX_HEAD_DIM = 64
FOX_WIDTH = 512
GDN_HEADS = 4
GDN_HEAD_DIM = 128
GDN_WIDTH = 512
CONV_WIDTH = 4
GDN_CHUNK = 64
XATTN_HEADS = 4
XATTN_HEAD_DIM = 128
XATTN_WIDTH = 512
D_FF = 4096
IN_DIM = 3600
EPS = 1e-6
NEG_INF = -1e30
LANES = 128
ADAM_LR = 0.001
ADAM_B1 = 0.9
ADAM_B2 = 0.999
ADAM_EPS = 1e-08
ADAM_WD = 0.01
ADAM_STEP = 10
VMEM_LIMIT = 48 * 1024 * 1024

COL_FOX = 0
COL_GDN = 1536
COL_Z = 3072
COL_SMALL = 3584
IN_ALIGNED = 3840
IN_TILE = 768
SM_F = 0
SM_B = 8
SM_A = 12


def _cparams(*sem):
    return pltpu.CompilerParams(dimension_semantics=sem, vmem_limit_bytes=VMEM_LIMIT)


def _mx(v):
    return v.astype(MXU_DTYPE)


def _dot(a, b, dims, precision=None):
    return lax.dot_general(a, b, (dims, ((), ())), preferred_element_type=f32, precision=precision)


def _dotm(a, b, dims):
    return _dot(_mx(a), _mx(b), dims)


NN = ((1,), (0,))
NT = ((1,), (1,))
TN = ((0,), (0,))


def matmul(a, b, *, name, ta=False, tb=False, b_stacked=False, out_stacked=False, residual=None, relu2_out=False,
           relu2_bwd_aux=None, out_dtype=f32, tm=1024, tn=1024, tk=1024):
    M, K = (a.shape[1], a.shape[0]) if ta else a.shape
    if b_stacked:
        b_cols = b.shape[2]
        N, tk = (b.shape[1], min(tk, b_cols)) if tb else (N_CHIPS * b_cols, tk)
        tn = tn if tb else min(tn, b_cols)
        assert K == (N_CHIPS * b_cols if tb else b.shape[1]), (name, a.shape, b.shape)
    else:
        N = b.shape[0] if tb else b.shape[1]
    if out_stacked:
        tn = min(tn, N // N_CHIPS)
    tm, tn, tk = min(tm, M), min(tn, N), min(tk, K)
    assert M % tm == 0 and N % tn == 0 and K % tk == 0, (name, M, N, K)
    nk = K // tk
    has_res = residual is not None
    has_aux = relu2_bwd_aux is not None

    def body(*refs):
        a_ref, b_ref = refs[0], refs[1]
        pos = 2
        res_ref = aux_ref = None
        if has_res:
            res_ref = refs[pos]
            pos += 1
        if has_aux:
            aux_ref = refs[pos]
            pos += 1
        o_ref = refs[pos]
        k = pl.program_id(2)
        dims = ((0,) if ta else (1,), (1,) if tb else (0,))
        part = _dot(_mx(a_ref[...]), _mx(b_ref[...]), dims)

        def finish(r):
            if has_res:
                r = r + res_ref[...]
            if has_aux:
                r = r * (2.0 * jnp.sqrt(aux_ref[...].astype(f32)))
            if relu2_out:
                o_ref[...] = jnp.square(jnp.maximum(r, 0.0)).astype(o_ref.dtype)
            else:
                o_ref[...] = r.astype(o_ref.dtype)

        if nk == 1:
            finish(part)
            return
        acc_ref = refs[pos + 1]

        @pl.when(k == 0)
        def _():
            acc_ref[...] = part

        @pl.when((k > 0) & (k < nk - 1))
        def _():
            acc_ref[...] += part

        @pl.when(k == nk - 1)
        def _():
            finish(acc_ref[...] + part)

    a_spec = pl.BlockSpec((tk, tm), lambda i, j, k: (k, i)) if ta else pl.BlockSpec((tm, tk), lambda i, j, k: (i, k))
    if b_stacked and tb:
        per = b_cols // tk
        b_spec = pl.BlockSpec((None, tn, tk), lambda i, j, k: (k // per, j, k % per))
    elif b_stacked:
        per = b_cols // tn
        b_spec = pl.BlockSpec((None, tk, tn), lambda i, j, k: (j // per, k, j % per))
    else:
        b_spec = pl.BlockSpec((tn, tk), lambda i, j, k: (j, k)) if tb else pl.BlockSpec((tk, tn), lambda i, j, k: (k, j))
    if out_stacked:
        assert not (has_res or has_aux or relu2_out), name
        per_o = N // N_CHIPS // tn
        o_spec = pl.BlockSpec((None, tm, tn), lambda i, j, k: (j // per_o, i, j % per_o))
        out_full = (N_CHIPS, M, N // N_CHIPS)
    else:
        o_spec = pl.BlockSpec((tm, tn), lambda i, j, k: (i, j))
        out_full = (M, N)
    in_specs, args = [a_spec, b_spec], [a, b]
    if has_res:
        in_specs.append(o_spec)
        args.append(residual)
    if has_aux:
        in_specs.append(o_spec)
        args.append(relu2_bwd_aux)
    out_shape = [jax.ShapeDtypeStruct(out_full, out_dtype)]
    out_specs = [o_spec]
    res = pl.pallas_call(
        body, name=name, grid=(M // tm, N // tn, nk), in_specs=in_specs, out_specs=out_specs, out_shape=out_shape,
        scratch_shapes=[pltpu.VMEM((tm, tn), f32)] if nk > 1 else [],
        compiler_params=_cparams("parallel", "parallel", "arbitrary"),
    )(*args)
    return res[0]


def matmul_rows(a, b, extras, *, name, mode, tb=False, b_stacked=False, tm=1024, tk=1024):
    M, K = a.shape
    N = D_MODEL
    if b_stacked:
        assert tb, name
        tk = min(tk, b.shape[2])
        per = b.shape[2] // tk
        b_spec = pl.BlockSpec((None, N, tk), lambda i, k: (k // per, 0, k % per))
    elif tb:
        tk = min(tk, K)
        b_spec = pl.BlockSpec((N, tk), lambda i, k: (0, k))
    else:
        tk = min(tk, K)
        b_spec = pl.BlockSpec((tk, N), lambda i, k: (k, 0))
    tm = min(tm, M)
    assert M % tm == 0 and K % tk == 0, (name, M, K)
    nk = K // tk
    extras = [e for e in extras if e is not None]
    n_ex = len(extras)

    def body(*refs):
        a_ref, b_ref = refs[0], refs[1]
        ex = refs[2:2 + n_ex]
        o_ref = refs[2 + n_ex]
        n_out = 3 if mode == "loss" else 2
        s_ref = refs[1 + n_ex + n_out]
        i, k = pl.program_id(0), pl.program_id(1)
        part = _dot(_mx(a_ref[...]), _mx(b_ref[...]), ((1,), (1,) if tb else (0,)))

        def finish(y):
            if mode == "rms_fwd":
                y = y + ex[0][...]
                o_ref[...] = y
                s_ref[...] = (y * lax.rsqrt(jnp.mean(y * y, axis=-1, keepdims=True) + EPS) * ex[1][...]).astype(MXU_DTYPE)
                return

            @pl.when(i == 0)
            def _():
                s_ref[...] = jnp.zeros_like(s_ref)

            if mode == "rms_bwd":
                xv, gv = ex[0][...], ex[1][...]
                rstd = lax.rsqrt(jnp.mean(xv * xv, axis=-1, keepdims=True) + EPS)
                xhat = xv * rstd
                gd = y * gv
                dx = rstd * (gd - xhat * jnp.mean(gd * xhat, axis=-1, keepdims=True))
                o_ref[...] = dx + ex[2][...] if n_ex == 3 else dx
                s_ref[...] += jnp.sum(y * xhat, axis=0, keepdims=True)
            else:
                e = y + ex[0][...] - ex[1][...]
                o_ref[...] = e * (1.0 / N)
                refs[3 + n_ex][...] = (e * (1.0 / N)).astype(MXU_DTYPE)
                tot = 0.5 * jnp.sum(jnp.mean(e * e, axis=-1, keepdims=True), axis=0, keepdims=True)
                s_ref[...] += jnp.broadcast_to(tot, s_ref.shape)

        if nk == 1:
            finish(part)
            return
        acc_ref = refs[2 + n_ex + n_out]

        @pl.when(k == 0)
        def _():
            acc_ref[...] = part

        @pl.when((k > 0) & (k < nk - 1))
        def _():
            acc_ref[...] += part

        @pl.when(k == nk - 1)
        def _():
            finish(acc_ref[...] + part)

    row = pl.BlockSpec((tm, N), lambda i, k: (i, 0))
    vec = pl.BlockSpec((1, N), lambda i, k: (0, 0))
    if mode == "rms_bwd":
        ex_specs = [row, vec] + ([row] if n_ex == 3 else [])
        s_shape, s_spec = jax.ShapeDtypeStruct((1, N), f32), vec
    elif mode == "rms_fwd":
        ex_specs = [row, vec]
        s_shape, s_spec = jax.ShapeDtypeStruct((M, N), MXU_DTYPE), row
    else:
        ex_specs = [row, row]
        s_shape, s_spec = jax.ShapeDtypeStruct((1, LANES), f32), pl.BlockSpec((1, LANES), lambda i, k: (0, 0))
    return pl.pallas_call(
        body, name=name, grid=(M // tm, nk),
        in_specs=[pl.BlockSpec((tm, tk), lambda i, k: (i, k)), b_spec] + ex_specs,
        out_specs=[row] * (2 if mode == "loss" else 1) + [s_spec],
        out_shape=[jax.ShapeDtypeStruct((M, N), f32)] + ([jax.ShapeDtypeStruct((M, N), MXU_DTYPE)] if mode == "loss" else [])
        + [s_shape],
        scratch_shapes=[pltpu.VMEM((tm, N), f32)] if nk > 1 else [],
        compiler_params=_cparams("arbitrary", "arbitrary"),
    )(a, b, *extras)


def rms_fwd(x, g, *, name, tr=512):
    R, D = x.shape
    tr = min(tr, R)

    def body(x_ref, g_ref, o_ref):
        xv = x_ref[...]
        y = xv * lax.rsqrt(jnp.mean(xv * xv, axis=-1, keepdims=True) + EPS)
        o_ref[...] = (y * g_ref[...]).astype(o_ref.dtype)

    return pl.pallas_call(
        body, name=name, grid=(R // tr,),
        in_specs=[pl.BlockSpec((tr, D), lambda i: (i, 0)), pl.BlockSpec((1, D), lambda i: (0, 0))],
        out_specs=pl.BlockSpec((tr, D), lambda i: (i, 0)),
        out_shape=jax.ShapeDtypeStruct((R, D), MXU_DTYPE),
        compiler_params=_cparams("parallel"),
    )(x, g)


def rms_bwd(x, g, dh, residual, *, name, tr=512):
    R, D = x.shape
    tr = min(tr, R)
    has_res = residual is not None

    def body(*refs):
        if has_res:
            x_ref, g_ref, dh_ref, res_ref, dx_ref, dg_ref = refs
        else:
            x_ref, g_ref, dh_ref, dx_ref, dg_ref = refs
        xv = x_ref[...]
        rstd = lax.rsqrt(jnp.mean(xv * xv, axis=-1, keepdims=True) + EPS)
        xhat = xv * rstd
        dh = dh_ref[...].astype(f32)
        gd = dh * g_ref[...]
        dx = rstd * (gd - xhat * jnp.mean(gd * xhat, axis=-1, keepdims=True))
        if has_res:
            dx = dx + res_ref[...]
        dx_ref[...] = dx

        @pl.when(pl.program_id(0) == 0)
        def _():
            dg_ref[...] = jnp.zeros_like(dg_ref)

        dg_ref[...] += jnp.sum(dh * xhat, axis=0, keepdims=True)

    row = pl.BlockSpec((tr, D), lambda i: (i, 0))
    vec = pl.BlockSpec((1, D), lambda i: (0, 0))
    in_specs = [row, vec, row] + ([row] if has_res else [])
    args = [x, g, dh] + ([residual] if has_res else [])
    return pl.pallas_call(
        body, name=name, grid=(R // tr,), in_specs=in_specs, out_specs=[row, vec],
        out_shape=[jax.ShapeDtypeStruct((R, D), f32), jax.ShapeDtypeStruct((1, D), f32)],
        compiler_params=_cparams("arbitrary"),
    )(*args)


def loss_head(y, target, *, tr=512):
    R, D = y.shape
    tr = min(tr, R)

    def body(y_ref, t_ref, dy_ref, loss_ref):
        e = y_ref[...] - t_ref[...]
        dy_ref[...] = e * (1.0 / D)

        @pl.when(pl.program_id(0) == 0)
        def _():
            loss_ref[...] = jnp.zeros_like(loss_ref)

        part = 0.5 * jnp.sum(jnp.mean(e * e, axis=-1, keepdims=True), axis=0, keepdims=True)
        loss_ref[...] += jnp.broadcast_to(part, loss_ref.shape)

    row = pl.BlockSpec((tr, D), lambda i: (i, 0))
    return pl.pallas_call(
        body, name="loss_head", grid=(R // tr,), in_specs=[row, row],
        out_specs=[row, pl.BlockSpec((1, LANES), lambda i: (0, 0))],
        out_shape=[jax.ShapeDtypeStruct((R, D), f32), jax.ShapeDtypeStruct((1, LANES), f32)],
        compiler_params=_cparams("arbitrary"),
    )(y, target)


def _head_rms(v, g):
    r = lax.rsqrt(jnp.mean(v * v, axis=-1, keepdims=True) + EPS)
    return v * r * g, r


def _head_rms_bwd(v, r, g, dn):
    vhat = v * r
    gd = dn * g
    dv = r * (gd - vhat * jnp.mean(gd * vhat, axis=-1, keepdims=True))
    return dv, jnp.sum(dn * vhat, axis=0, keepdims=True)


def _softmax_rows(s):
    m = jnp.max(s, axis=-1, keepdims=True)
    e = jnp.exp(s - m)
    return e / jnp.sum(e, axis=-1, keepdims=True)


def xattn_fwd(cq, ckv, gq, gk, *, B, tq=512):
    T = cq.shape[0]
    S = T // B
    M = ckv.shape[0] // B
    tq = min(tq, S)
    nq = S // tq
    hd, W = XATTN_HEAD_DIM, XATTN_WIDTH
    scale = hd ** -0.5

    def body(q_ref, k_ref, v_ref, gq_ref, gk_ref, o_ref):
        for h in range(XATTN_HEADS):
            sl = slice(h * hd, (h + 1) * hd)
            qn, _ = _head_rms(q_ref[:, sl], gq_ref[...])
            kn, _ = _head_rms(k_ref[:, sl], gk_ref[...])
            p = _softmax_rows(_dot(_mx(qn), _mx(kn), NT) * scale)
            o_ref[:, sl] = _dot(_mx(p), _mx(v_ref[:, sl]), NN).astype(o_ref.dtype)

    vec = pl.BlockSpec((1, hd), lambda b, i: (0, 0))
    qspec = pl.BlockSpec((tq, W), lambda b, i: (b * nq + i, 0))
    return pl.pallas_call(
        body, name="xattn_fwd", grid=(B, nq),
        in_specs=[qspec, pl.BlockSpec((M, W), lambda b, i: (b, 0)), pl.BlockSpec((M, W), lambda b, i: (b, 1)), vec, vec],
        out_specs=qspec, out_shape=jax.ShapeDtypeStruct((T, W), MXU_DTYPE),
        compiler_params=_cparams("parallel", "parallel"),
    )(cq, ckv, ckv, gq, gk)


def xattn_bwd(cq, ckv, gq, gk, dco, *, B, tq=512):
    T = cq.shape[0]
    S = T // B
    M = ckv.shape[0] // B
    tq = min(tq, S)
    nq = S // tq
    hd, W = XATTN_HEAD_DIM, XATTN_WIDTH
    scale = hd ** -0.5

    def body(q_ref, k_ref, v_ref, gq_ref, gk_ref, do_ref, dq_ref, dkv_ref, dgq_ref, dgk_ref, dkn_acc, dv_acc):
        b, i = pl.program_id(0), pl.program_id(1)

        @pl.when((b == 0) & (i == 0))
        def _():
            dgq_ref[...] = jnp.zeros_like(dgq_ref)
            dgk_ref[...] = jnp.zeros_like(dgk_ref)

        @pl.when(i == 0)
        def _():
            dkn_acc[...] = jnp.zeros_like(dkn_acc)
            dv_acc[...] = jnp.zeros_like(dv_acc)

        gqv, gkv = gq_ref[...], gk_ref[...]
        for h in range(XATTN_HEADS):
            sl = slice(h * hd, (h + 1) * hd)
            q, k, v = q_ref[:, sl], k_ref[:, sl], v_ref[:, sl]
            qn, rq = _head_rms(q, gqv)
            kn, _ = _head_rms(k, gkv)
            p = _softmax_rows(_dot(_mx(qn), _mx(kn), NT) * scale)
            do = do_ref[:, sl]
            dv_acc[:, sl] += _dot(_mx(p), _mx(do), TN)
            dp = _dot(_mx(do), _mx(v), NT)
            ds = p * (dp - jnp.sum(dp * p, axis=-1, keepdims=True)) * scale
            dqn = _dot(_mx(ds), _mx(kn), NN)
            dkn_acc[:, sl] += _dot(_mx(ds), _mx(qn), TN)
            dq, dgq = _head_rms_bwd(q, rq, gqv, dqn)
            dq_ref[:, sl] = dq.astype(dq_ref.dtype)
            dgq_ref[...] += dgq

        @pl.when(i == nq - 1)
        def _():
            for h in range(XATTN_HEADS):
                sl = slice(h * hd, (h + 1) * hd)
                k = k_ref[:, sl]
                rk = lax.rsqrt(jnp.mean(k * k, axis=-1, keepdims=True) + EPS)
                dk, dgk = _head_rms_bwd(k, rk, gkv, dkn_acc[:, sl])
                dkv_ref[:, sl] = dk.astype(dkv_ref.dtype)
                dkv_ref[:, slice(W + h * hd, W + (h + 1) * hd)] = dv_acc[:, sl].astype(dkv_ref.dtype)
                dgk_ref[...] += dgk

    vec = pl.BlockSpec((1, hd), lambda b, i: (0, 0))
    qspec = pl.BlockSpec((tq, W), lambda b, i: (b * nq + i, 0))
    return pl.pallas_call(
        body, name="xattn_bwd", grid=(B, nq),
        in_specs=[qspec, pl.BlockSpec((M, W), lambda b, i: (b, 0)), pl.BlockSpec((M, W), lambda b, i: (b, 1)), vec, vec, qspec],
        out_specs=[qspec, pl.BlockSpec((M, 2 * W), lambda b, i: (b, 0)), vec, vec],
        out_shape=[jax.ShapeDtypeStruct((T, W), MXU_DTYPE), jax.ShapeDtypeStruct((B * M, 2 * W), MXU_DTYPE),
                   jax.ShapeDtypeStruct((1, hd), f32), jax.ShapeDtypeStruct((1, hd), f32)],
        scratch_shapes=[pltpu.VMEM((M, W), f32), pltpu.VMEM((M, W), f32)],
        compiler_params=_cparams("arbitrary", "arbitrary"),
    )(cq, ckv, ckv, gq, gk, dco)


FOX_PAIRS = FOX_HEADS // 2


def _fox_scores(qn, kn, ccol, crow, q0, tq, S, scale):
    s = _dot(_mx(qn), _mx(kn), NT) * scale + ccol - crow
    qpos = q0 + lax.broadcasted_iota(jnp.int32, (tq, S), 0)
    kpos = lax.broadcasted_iota(jnp.int32, (tq, S), 1)
    return jnp.where(kpos <= qpos, s, NEG_INF)


def fox_fwd(P, ccol, crow, gq, gk, go, *, B, tq=256):
    T = P.shape[0]
    S = T // B
    tq = min(tq, S)
    nq = S // tq
    hd = FOX_HEAD_DIM
    scale = hd ** -0.5

    def body(q_ref, k_ref, v_ref, ccol_ref, crow_ref, gq_ref, gk_ref, go_ref, o_ref, oa_ref):
        q0 = pl.program_id(2) * tq
        for e in range(2):
            sl = slice(e * hd, (e + 1) * hd)
            qn, _ = _head_rms(q_ref[:, sl], gq_ref[:, sl])
            kn, _ = _head_rms(k_ref[:, sl], gk_ref[:, sl])
            p = _softmax_rows(_fox_scores(qn, kn, ccol_ref[0, e], crow_ref[0, e], q0, tq, S, scale))
            o = _dot(_mx(p), _mx(v_ref[:, sl]), NN)
            o_ref[:, sl] = o
            oa_ref[:, sl] = _head_rms(o, go_ref[:, sl])[0].astype(oa_ref.dtype)

    W = 2 * hd
    vec = pl.BlockSpec((1, W), lambda b, h, i: (0, 0))
    ospec = pl.BlockSpec((tq, W), lambda b, h, i: (b * nq + i, h))
    return pl.pallas_call(
        body, name="fox_fwd", grid=(B, FOX_PAIRS, nq),
        in_specs=[pl.BlockSpec((tq, W), lambda b, h, i: (b * nq + i, h)),
                  pl.BlockSpec((S, W), lambda b, h, i: (b, FOX_PAIRS + h)),
                  pl.BlockSpec((S, W), lambda b, h, i: (b, 2 * FOX_PAIRS + h)),
                  pl.BlockSpec((1, 2, tq, 1), lambda b, h, i: (b, h, i, 0)),
                  pl.BlockSpec((1, 2, 1, S), lambda b, h, i: (b, h, 0, 0)), vec, vec, vec],
        out_specs=[ospec, ospec],
        out_shape=[jax.ShapeDtypeStruct((T, FOX_WIDTH), f32), jax.ShapeDtypeStruct((T, FOX_WIDTH), MXU_DTYPE)],
        compiler_params=_cparams("parallel", "parallel", "parallel"),
    )(P, P, P, ccol, crow, gq, gk, go)


def fox_bwd(P, ccol, crow, gq, gk, go, o_raw, d_oab, *, B, tq=256):
    T = P.shape[0]
    S = T // B
    tq = min(tq, S)
    nq = S // tq
    hd = FOX_HEAD_DIM
    scale = hd ** -0.5

    def body(q_ref, k_ref, v_ref, ccol_ref, crow_ref, gq_ref, gk_ref, go_ref, o_ref, doa_ref,
             dq_ref, dk_ref, dv_ref, dccol_ref, dcrow_ref, dgq_ref, dgk_ref, dgo_ref, dkn_acc, dv_acc, dcrow_acc):
        b, h, i = pl.program_id(0), pl.program_id(1), pl.program_id(2)
        q0 = i * tq

        @pl.when((b == 0) & (h == 0) & (i == 0))
        def _():
            dgq_ref[...] = jnp.zeros_like(dgq_ref)
            dgk_ref[...] = jnp.zeros_like(dgk_ref)
            dgo_ref[...] = jnp.zeros_like(dgo_ref)

        @pl.when(i == 0)
        def _():
            dkn_acc[...] = jnp.zeros_like(dkn_acc)
            dv_acc[...] = jnp.zeros_like(dv_acc)
            dcrow_acc[...] = jnp.zeros_like(dcrow_acc)

        for e in range(2):
            sl = slice(e * hd, (e + 1) * hd)
            q, k, v = q_ref[:, sl], k_ref[:, sl], v_ref[:, sl]
            gqv, gkv, gov = gq_ref[:, sl], gk_ref[:, sl], go_ref[:, sl]
            qn, rq = _head_rms(q, gqv)
            kn, rk = _head_rms(k, gkv)
            p = _softmax_rows(_fox_scores(qn, kn, ccol_ref[0, e], crow_ref[0, e], q0, tq, S, scale))
            o = o_ref[:, sl]
            ro = lax.rsqrt(jnp.mean(o * o, axis=-1, keepdims=True) + EPS)
            do, dgo = _head_rms_bwd(o, ro, gov, doa_ref[:, sl])
            dgo_ref[:, sl] += dgo
            dv_acc[e] += _dot(_mx(p), _mx(do), TN)
            dp = _dot(_mx(do), _mx(v), NT)
            ds = p * (dp - jnp.sum(do * o, axis=-1, keepdims=True))
            dccol_ref[0, e] = jnp.sum(ds, axis=1, keepdims=True)
            dcrow_acc[e] -= jnp.sum(ds, axis=0, keepdims=True)
            dqn = _dot(_mx(ds), _mx(kn), NN) * scale
            dkn_acc[e] += _dot(_mx(ds), _mx(qn), TN) * scale
            dq, dgq = _head_rms_bwd(q, rq, gqv, dqn)
            dq_ref[:, sl] = dq.astype(dq_ref.dtype)
            dgq_ref[:, sl] += dgq

        @pl.when(i == nq - 1)
        def _():
            for e in range(2):
                sl = slice(e * hd, (e + 1) * hd)
                k = k_ref[:, sl]
                gkv = gk_ref[:, sl]
                rk = lax.rsqrt(jnp.mean(k * k, axis=-1, keepdims=True) + EPS)
                dk, dgk = _head_rms_bwd(k, rk, gkv, dkn_acc[e])
                dk_ref[:, sl] = dk.astype(dk_ref.dtype)
                dv_ref[:, sl] = dv_acc[e].astype(dv_ref.dtype)
                dgk_ref[:, sl] += dgk
                dcrow_ref[0, e] = dcrow_acc[e]

    W = 2 * hd
    vec = pl.BlockSpec((1, W), lambda b, h, i: (0, 0))
    qspec = pl.BlockSpec((tq, W), lambda b, h, i: (b * nq + i, h))
    kvout = pl.BlockSpec((S, W), lambda b, h, i: (b, h))
    colspec = pl.BlockSpec((1, 2, tq, 1), lambda b, h, i: (b, h, i, 0))
    rowspec = pl.BlockSpec((1, 2, 1, S), lambda b, h, i: (b, h, 0, 0))
    return pl.pallas_call(
        body, name="fox_bwd", grid=(B, FOX_PAIRS, nq),
        in_specs=[qspec,
                  pl.BlockSpec((S, W), lambda b, h, i: (b, FOX_PAIRS + h)),
                  pl.BlockSpec((S, W), lambda b, h, i: (b, 2 * FOX_PAIRS + h)),
                  colspec, rowspec, vec, vec, vec, qspec, qspec],
        out_specs=[qspec, kvout, kvout, colspec, rowspec, vec, vec, vec],
        out_shape=[jax.ShapeDtypeStruct((T, FOX_WIDTH), MXU_DTYPE), jax.ShapeDtypeStruct((T, FOX_WIDTH), MXU_DTYPE),
                   jax.ShapeDtypeStruct((T, FOX_WIDTH), MXU_DTYPE),
                   jax.ShapeDtypeStruct((B, FOX_HEADS, S, 1), f32), jax.ShapeDtypeStruct((B, FOX_HEADS, 1, S), f32),
                   jax.ShapeDtypeStruct((1, W), f32), jax.ShapeDtypeStruct((1, W), f32), jax.ShapeDtypeStruct((1, W), f32)],
        scratch_shapes=[pltpu.VMEM((2, S, hd), f32), pltpu.VMEM((2, S, hd), f32), pltpu.VMEM((2, 1, S), f32)],
        compiler_params=_cparams("arbitrary", "arbitrary", "arbitrary"),
    )(P, P, P, ccol, crow, gq, gk, go, o_raw, d_oab)


FOX_TQ = 512
FOX_TK = FOX_TQ
GROUP_PRECISION = lax.Precision.HIGH


def _head_mean(v):
    n = v.shape[1]
    r = lax.broadcasted_iota(jnp.int32, (n, n), 0) // FOX_HEAD_DIM
    c = lax.broadcasted_iota(jnp.int32, (n, n), 1) // FOX_HEAD_DIM
    ones = (r == c).astype(bf16)
    hi = v.astype(bf16)
    lo = (v - hi.astype(f32)).astype(bf16)
    return (_dot(hi, ones, NN) + _dot(lo, ones, NN)) * (1.0 / FOX_HEAD_DIM)


def fox_prep_fwd(P, gq, gk, *, tr=512):
    T = P.shape[0]
    tr = min(tr, T)
    scale = FOX_HEAD_DIM ** -0.5

    def body(q_ref, k_ref, v_ref, gq_ref, gk_ref, qn_ref, kn_ref, vb_ref):
        q, k = q_ref[...], k_ref[...]
        qn_ref[...] = (q * lax.rsqrt(_head_mean(q * q) + EPS) * (gq_ref[...] * scale)).astype(qn_ref.dtype)
        kn_ref[...] = (k * lax.rsqrt(_head_mean(k * k) + EPS) * gk_ref[...]).astype(kn_ref.dtype)
        vb_ref[...] = v_ref[...].astype(vb_ref.dtype)

    W = FOX_WIDTH
    col = lambda j: pl.BlockSpec((tr, W), lambda i: (i, j))
    vec = pl.BlockSpec((1, W), lambda i: (0, 0))
    out = jax.ShapeDtypeStruct((T, W), MXU_DTYPE)
    return pl.pallas_call(
        body, name="fox_prep_fwd", grid=(T // tr,), in_specs=[col(0), col(1), col(2), vec, vec],
        out_specs=[col(0)] * 3, out_shape=[out] * 3, compiler_params=_cparams("parallel"),
    )(P, P, P, gq, gk)


def fox_prep_bwd(P, gq, gk, dqn, dkn, *, tr=512):
    T = P.shape[0]
    tr = min(tr, T)
    scale = FOX_HEAD_DIM ** -0.5

    def body(q_ref, k_ref, gq_ref, gk_ref, dqn_ref, dkn_ref, dq_ref, dk_ref, dgq_ref, dgk_ref):
        @pl.when(pl.program_id(0) == 0)
        def _():
            dgq_ref[...] = jnp.zeros_like(dgq_ref)
            dgk_ref[...] = jnp.zeros_like(dgk_ref)

        def one(x, g, dn, dx_ref, dg_ref):
            r = lax.rsqrt(_head_mean(x * x) + EPS)
            xhat = x * r
            gd = dn * g
            dx_ref[...] = (r * (gd - xhat * _head_mean(gd * xhat))).astype(dx_ref.dtype)
            return jnp.sum(dn * xhat, axis=0, keepdims=True)

        dgq_ref[...] += scale * one(q_ref[...], gq_ref[...] * scale, dqn_ref[...], dq_ref, dgq_ref)
        dgk_ref[...] += one(k_ref[...], gk_ref[...], dkn_ref[...], dk_ref, dgk_ref)

    W = FOX_WIDTH
    col = lambda j: pl.BlockSpec((tr, W), lambda i: (i, j))
    vec = pl.BlockSpec((1, W), lambda i: (0, 0))
    return pl.pallas_call(
        body, name="fox_prep_bwd", grid=(T // tr,), in_specs=[col(0), col(1), vec, vec, col(0), col(0)],
        out_specs=[col(0), col(0), vec, vec],
        out_shape=[jax.ShapeDtypeStruct((T, W), MXU_DTYPE), jax.ShapeDtypeStruct((T, W), MXU_DTYPE),
                   jax.ShapeDtypeStruct((1, W), f32), jax.ShapeDtypeStruct((1, W), f32)],
        compiler_params=_cparams("arbitrary"),
    )(P, P, gq, gk, dqn, dkn)


def _fox_tile_scores(q, k_ref, ccol_ref, cq, e, j, sl, mask_off):
    tq, tk = FOX_TQ, FOX_TK
    rows = pl.ds(pl.multiple_of(j * tk, tk), tk)
    k = k_ref[rows, sl]
    s = _dot(k, q, NT) + cq - ccol_ref[0, e, rows, :]
    if mask_off is not None:
        key = lax.broadcasted_iota(jnp.int32, (tk, tq), 0) + mask_off
        query = lax.broadcasted_iota(jnp.int32, (tk, tq), 1)
        s = jnp.where(key <= query, s, NEG_INF)
    return s, k, rows


def _fox_sweep(i, update, carry):
    nd = FOX_TQ // FOX_TK
    carry = lax.fori_loop(0, i * nd, lambda j, cr: update(cr, j, None), carry)
    for d in range(nd):
        carry = update(carry, i * nd + d, d * FOX_TK)
    return carry


def fox_core_fwd(qn, kn, vb, ccol, crow, go, *, B):
    T = qn.shape[0]
    S = T // B
    tq = FOX_TQ
    nq = S // tq
    hd = FOX_HEAD_DIM

    def body(q_ref, k_ref, v_ref, ccol_ref, crow_ref, go_ref, o_ref, oa_ref, lse_ref):
        i = pl.program_id(2)
        for e in range(2):
            sl = slice(e * hd, (e + 1) * hd)
            q = q_ref[:, sl]
            cq = crow_ref[0, e, i]

            def update(carry, j, mask_off):
                m, l, acc = carry
                s, _, rows = _fox_tile_scores(q, k_ref, ccol_ref, cq, e, j, sl, mask_off)
                m2 = jnp.maximum(m, jnp.max(s, axis=0, keepdims=True))
                a = jnp.exp(m - m2)
                p = jnp.exp(s - m2)
                return m2, a * l + jnp.sum(p, axis=0, keepdims=True), a * acc + _dot(v_ref[rows, sl], _mx(p), TN)

            carry = (jnp.full((1, tq), NEG_INF, f32), jnp.zeros((1, tq), f32), jnp.zeros((hd, tq), f32))
            m, l, acc = _fox_sweep(i, update, carry)
            o = (acc / l).T
            o_ref[:, sl] = o
            oa_ref[:, sl] = _head_rms(o, go_ref[:, sl])[0].astype(oa_ref.dtype)
            lse_ref[0, e, 0] = m + jnp.log(l)

    W = 2 * hd
    qspec = pl.BlockSpec((tq, W), lambda b, h, i: (b * nq + i, h))
    kspec = pl.BlockSpec((S, W), lambda b, h, i: (b, h))
    return pl.pallas_call(
        body, name="fox_core_fwd", grid=(B, FOX_PAIRS, nq),
        in_specs=[qspec, kspec, kspec, pl.BlockSpec((1, 2, S, 1), lambda b, h, i: (b, h, 0, 0)),
                  pl.BlockSpec((1, 2, nq, 1, tq), lambda b, h, i: (b, h, 0, 0, 0)),
                  pl.BlockSpec((1, W), lambda b, h, i: (0, 0))],
        out_specs=[qspec, qspec, pl.BlockSpec((1, 2, 1, 1, tq), lambda b, h, i: (b, h, i, 0, 0))],
        out_shape=[jax.ShapeDtypeStruct((T, FOX_WIDTH), f32), jax.ShapeDtypeStruct((T, FOX_WIDTH), MXU_DTYPE),
                   jax.ShapeDtypeStruct((B, FOX_HEADS, nq, 1, tq), f32)],
        compiler_params=_cparams("parallel", "parallel", "parallel"),
    )(qn, kn, vb, ccol, crow, go)


def fox_core_bwd(qn, kn, vb, ccol, crow, go, o_raw, lse, d_oab, *, B):
    T = qn.shape[0]
    S = T // B
    tq = FOX_TQ
    nq = S // tq
    hd = FOX_HEAD_DIM

    def body(q_ref, k_ref, v_ref, ccol_ref, crow_ref, go_ref, o_ref, lse_ref, doa_ref,
             dq_ref, dk_ref, dv_ref, dckey_ref, dcrow_ref, dgo_ref, dk_acc, dv_acc, dck_acc):
        b, h, i = pl.program_id(0), pl.program_id(1), pl.program_id(2)

        @pl.when((b == 0) & (h == 0) & (i == 0))
        def _():
            dgo_ref[...] = jnp.zeros_like(dgo_ref)

        @pl.when(i == 0)
        def _():
            dk_acc[...] = jnp.zeros_like(dk_acc)
            dv_acc[...] = jnp.zeros_like(dv_acc)
            dck_acc[...] = jnp.zeros_like(dck_acc)

        for e in range(2):
            sl = slice(e * hd, (e + 1) * hd)
            q = q_ref[:, sl]
            cq = crow_ref[0, e, i]
            lse_e = lse_ref[0, e, 0]
            o = o_ref[:, sl]
            ro = lax.rsqrt(jnp.mean(o * o, axis=-1, keepdims=True) + EPS)
            do, dgo = _head_rms_bwd(o, ro, go_ref[:, sl], doa_ref[:, sl])
            dgo_ref[:, sl] += dgo
            delta = jnp.sum((do * o).T, axis=0, keepdims=True)
            do_b = _mx(do)

            def update(carry, j, mask_off):
                dq, dcq = carry
                s, k, rows = _fox_tile_scores(q, k_ref, ccol_ref, cq, e, j, sl, mask_off)
                p = jnp.exp(s - lse_e)
                dv_acc[e, rows, :] += _dot(_mx(p), do_b, NN)
                ds = p * (_dot(v_ref[rows, sl], do_b, NT) - delta)
                dck_acc[e, rows, :] -= jnp.sum(ds, axis=1, keepdims=True)
                ds_b = _mx(ds)
                dk_acc[e, rows, :] += _dot(ds_b, q, NN)
                return dq + _dot(ds_b, k, TN), dcq + jnp.sum(ds, axis=0, keepdims=True)

            dq, dcq = _fox_sweep(i, update, (jnp.zeros((tq, hd), f32), jnp.zeros((1, tq), f32)))
            dq_ref[:, sl] = dq
            dcrow_ref[0, e, 0] = dcq

        @pl.when(i == nq - 1)
        def _():
            for e in range(2):
                sl = slice(e * hd, (e + 1) * hd)
                dk_ref[:, sl] = dk_acc[e]
                dv_ref[:, sl] = dv_acc[e].astype(dv_ref.dtype)
                dckey_ref[0, e] = jnp.transpose(jnp.broadcast_to(dck_acc[e], (S, LANES)))[0:1, :]

    W = 2 * hd
    qspec = pl.BlockSpec((tq, W), lambda b, h, i: (b * nq + i, h))
    kspec = pl.BlockSpec((S, W), lambda b, h, i: (b, h))
    colspec = pl.BlockSpec((1, 2, S, 1), lambda b, h, i: (b, h, 0, 0))
    rowspec = pl.BlockSpec((1, 2, nq, 1, tq), lambda b, h, i: (b, h, 0, 0, 0))
    tilespec = pl.BlockSpec((1, 2, 1, 1, tq), lambda b, h, i: (b, h, i, 0, 0))
    vec = pl.BlockSpec((1, W), lambda b, h, i: (0, 0))
    return pl.pallas_call(
        body, name="fox_core_bwd", grid=(B, FOX_PAIRS, nq),
        in_specs=[qspec, kspec, kspec, colspec, rowspec, vec, qspec, tilespec, qspec],
        out_specs=[qspec, kspec, kspec, pl.BlockSpec((1, 2, 1, S), lambda b, h, i: (b, h, 0, 0)), tilespec, vec],
        out_shape=[jax.ShapeDtypeStruct((T, FOX_WIDTH), f32), jax.ShapeDtypeStruct((T, FOX_WIDTH), f32),
                   jax.ShapeDtypeStruct((T, FOX_WIDTH), MXU_DTYPE),
                   jax.ShapeDtypeStruct((B, FOX_HEADS, 1, S), f32), jax.ShapeDtypeStruct((B, FOX_HEADS, nq, 1, tq), f32),
                   jax.ShapeDtypeStruct((1, W), f32)],
        scratch_shapes=[pltpu.VMEM((2, S, hd), f32), pltpu.VMEM((2, S, hd), f32), pltpu.VMEM((2, S, 1), f32)],
        compiler_params=_cparams("arbitrary", "arbitrary", "arbitrary"),
    )(qn, kn, vb, ccol, crow, go, o_raw, lse, d_oab)


def _lane_mask(lo, hi, shape):
    lane = lax.broadcasted_iota(jnp.int32, shape, 1)
    return (lane >= lo) & (lane < hi)


def _cumsum_rows(v, period, reverse=False):
    n = v.shape[0]
    pos = lax.broadcasted_iota(jnp.int32, v.shape, 0) % period
    sh = 1
    while sh < period:
        if reverse:
            v = v + jnp.where(pos + sh < period, pltpu.roll(v, n - sh, 0), 0.0)
        else:
            v = v + jnp.where(pos >= sh, pltpu.roll(v, sh, 0), 0.0)
        sh *= 2
    return v


def _gate_values(z, bias, alog):
    zb = z + bias
    ls = jax.nn.log_sigmoid(zb)
    beta = jax.nn.sigmoid(z)
    g = -jnp.exp(alog) * jax.nn.softplus(zb)
    return zb, ls, beta, g


def gates_fwd(P, bias, alog, *, B):
    T = P.shape[0]
    S = T // B

    def body(z_ref, bias_ref, alog_ref, o_ref):
        z = z_ref[...]
        _, ls, beta, g = _gate_values(z, bias_ref[...], alog_ref[...])
        c = _cumsum_rows(ls, S)
        gc = _cumsum_rows(g, GDN_CHUNK)
        o = jnp.where(_lane_mask(SM_F, SM_F + FOX_HEADS, z.shape), c, 0.0)
        o = jnp.where(_lane_mask(SM_B, SM_B + GDN_HEADS, z.shape), beta, o)
        o = jnp.where(_lane_mask(SM_A, SM_A + GDN_HEADS, z.shape), gc, o)
        o_ref[...] = o

    vec = pl.BlockSpec((1, LANES), lambda b: (0, 0))
    return pl.pallas_call(
        body, name="gates_fwd", grid=(B,),
        in_specs=[pl.BlockSpec((S, LANES), lambda b: (b, COL_SMALL // LANES)), vec, vec],
        out_specs=pl.BlockSpec((S, LANES), lambda b: (b, 0)),
        out_shape=jax.ShapeDtypeStruct((T, LANES), f32),
        compiler_params=_cparams("parallel"),
    )(P, bias, alog)


def gates_bwd(P, bias, alog, dgates, *, B):
    T = P.shape[0]
    S = T // B

    def body(z_ref, bias_ref, alog_ref, dg_ref, dz_ref, par_ref):
        z = z_ref[...]
        zb, ls, beta, g = _gate_values(z, bias_ref[...], alog_ref[...])
        d = dg_ref[...]
        dls = _cumsum_rows(d, S, reverse=True)
        dgr = _cumsum_rows(d, GDN_CHUNK, reverse=True)
        sig = jax.nn.sigmoid(zb)
        dz_f = dls * (1.0 - sig)
        dz_b = d * beta * (1.0 - beta)
        dz_a = dgr * (-jnp.exp(alog_ref[...])) * sig
        dz = jnp.where(_lane_mask(SM_F, SM_F + FOX_HEADS, z.shape), dz_f, 0.0)
        dz = jnp.where(_lane_mask(SM_B, SM_B + GDN_HEADS, z.shape), dz_b, dz)
        dz = jnp.where(_lane_mask(SM_A, SM_A + GDN_HEADS, z.shape), dz_a, dz)
        dz_ref[...] = dz.astype(dz_ref.dtype)

        @pl.when(pl.program_id(0) == 0)
        def _():
            par_ref[...] = jnp.zeros_like(par_ref)

        dalog = jnp.where(_lane_mask(SM_A, SM_A + GDN_HEADS, z.shape), dgr * g, 0.0)
        par_ref[0:1, :] += jnp.sum(dz, axis=0, keepdims=True)
        par_ref[1:2, :] += jnp.sum(dalog, axis=0, keepdims=True)

    vec = pl.BlockSpec((1, LANES), lambda b: (0, 0))
    return pl.pallas_call(
        body, name="gates_bwd", grid=(B,),
        in_specs=[pl.BlockSpec((S, LANES), lambda b: (b, COL_SMALL // LANES)), vec, vec,
                  pl.BlockSpec((S, LANES), lambda b: (b, 0))],
        out_specs=[pl.BlockSpec((S, LANES), lambda b: (b, 0)), pl.BlockSpec((8, LANES), lambda b: (0, 0))],
        out_shape=[jax.ShapeDtypeStruct((T, LANES), MXU_DTYPE), jax.ShapeDtypeStruct((8, LANES), f32)],
        compiler_params=_cparams("arbitrary"),
    )(P, bias, alog, dgates)


GDN_BLOCKS = 3 * GDN_HEADS


def _shift_rows(v, d, reverse=False):
    if d == 0:
        return v
    n = v.shape[0]
    row = lax.broadcasted_iota(jnp.int32, v.shape, 0)
    if reverse:
        return jnp.where(row + d < n, pltpu.roll(v, n - d, 0), 0.0)
    return jnp.where(row >= d, pltpu.roll(v, d, 0), 0.0)


def _conv_silu(x, w):
    pre = sum(w[j:j + 1, :] * _shift_rows(x, CONV_WIDTH - 1 - j) for j in range(CONV_WIDTH))
    return pre, pre * jax.nn.sigmoid(pre)


def gdn_prep_fwd(P, conv_w, *, B):
    T = P.shape[0]
    S = T // B

    def body(x_ref, w_ref, o_ref):
        _, y = _conv_silu(x_ref[...], w_ref[...])
        yn = y * lax.rsqrt(jnp.sum(y * y, axis=-1, keepdims=True) + EPS)
        o_ref[...] = jnp.where(pl.program_id(1) < 2 * GDN_HEADS, yn, y)

    return pl.pallas_call(
        body, name="gdn_prep_fwd", grid=(B, GDN_BLOCKS),
        in_specs=[pl.BlockSpec((S, LANES), lambda b, j: (b, COL_GDN // LANES + j)),
                  pl.BlockSpec((CONV_WIDTH, LANES), lambda b, j: (0, j))],
        out_specs=pl.BlockSpec((S, LANES), lambda b, j: (b, j)),
        out_shape=jax.ShapeDtypeStruct((T, 3 * GDN_WIDTH), f32),
        compiler_params=_cparams("parallel", "parallel"),
    )(P, conv_w)


def gdn_prep_bwd(P, conv_w, dGq, dGk, dGv, *, B):
    T = P.shape[0]
    S = T // B
    H = GDN_HEADS

    def body(x_ref, w_ref, dq_ref, dk_ref, dv_ref, dx_ref, dw_ref):
        x, w = x_ref[...], w_ref[...]
        pre, y = _conv_silu(x, w)
        jb = pl.program_id(0)
        dn = jnp.where(jb < H, dq_ref[...], jnp.where(jb < 2 * H, dk_ref[...], dv_ref[...]))
        r = lax.rsqrt(jnp.sum(y * y, axis=-1, keepdims=True) + EPS)
        n = y * r
        dy_norm = r * (dn - n * jnp.sum(dn * n, axis=-1, keepdims=True))
        dy = jnp.where(pl.program_id(0) < 2 * GDN_HEADS, dy_norm, dn)
        sg = jax.nn.sigmoid(pre)
        dpre = dy * (sg * (1.0 + pre * (1.0 - sg)))
        dx = sum(w[j:j + 1, :] * _shift_rows(dpre, CONV_WIDTH - 1 - j, reverse=True) for j in range(CONV_WIDTH))
        dx_ref[...] = dx.astype(dx_ref.dtype)

        @pl.when(pl.program_id(1) == 0)
        def _():
            dw_ref[...] = jnp.zeros_like(dw_ref)

        for j in range(CONV_WIDTH):
            dw_ref[j:j + 1, :] += jnp.sum(dpre * _shift_rows(x, CONV_WIDTH - 1 - j), axis=0, keepdims=True)

    return pl.pallas_call(
        body, name="gdn_prep_bwd", grid=(GDN_BLOCKS, B),
        in_specs=[pl.BlockSpec((S, LANES), lambda j, b: (b, COL_GDN // LANES + j)),
                  pl.BlockSpec((CONV_WIDTH, LANES), lambda j, b: (0, j))]
        + [pl.BlockSpec((S, LANES), lambda j, b, t=t: (b, jnp.clip(j - t * H, 0, H - 1))) for t in range(3)],
        out_specs=[pl.BlockSpec((S, LANES), lambda j, b: (b, j)),
                   pl.BlockSpec((CONV_WIDTH, LANES), lambda j, b: (0, j))],
        out_shape=[jax.ShapeDtypeStruct((T, 3 * GDN_WIDTH), MXU_DTYPE),
                   jax.ShapeDtypeStruct((CONV_WIDTH, 3 * GDN_WIDTH), f32)],
        compiler_params=_cparams("arbitrary", "arbitrary"),
    )(P, conv_w, dGq, dGk, dGv)


GDN_GROUP = 16
GDN_GROUP_FWD = 16
B_NN = (((2,), (1,)), ((0,), (0,)))
B_NT = (((2,), (2,)), ((0,), (0,)))
B_TN = (((1,), (1,)), ((0,), (0,)))


def _bmm(a, b, dims, precision=None):
    if precision is None:
        a, b = _mx(a), _mx(b)
    return lax.dot_general(a, b, dims, preferred_element_type=f32, precision=precision)


def _tri_inverse(A):
    C = A.shape[-1]
    row = lax.broadcasted_iota(jnp.int32, A.shape, 1)
    col = lax.broadcasted_iota(jnp.int32, A.shape, 2)
    eye = (row == col).astype(f32)
    X = jnp.where((row // 4) == (col // 4), -A, 0.0)
    X2 = _bmm(X, X, B_NN, INV_PRECISION)
    Tm = eye + X + X2 + _bmm(X, X2, B_NN, INV_PRECISION)
    b = 4
    while b < C:
        off = ((row // (2 * b)) == (col // (2 * b))) & ((row // b) != (col // b))
        Tm = Tm - _bmm(_bmm(Tm, jnp.where(off, A, 0.0), B_NN, INV_PRECISION), Tm, B_NN, INV_PRECISION)
        b *= 2
    return Tm


def _pick_lane(block, lane_idx):
    lane = lax.broadcasted_iota(jnp.int32, block.shape, 1)
    return jnp.sum(jnp.where(lane == lane_idx, block, 0.0), axis=1, keepdims=True)


def _gdn_local(q, k, v, beta, gc, Tm=None, uwm=None):
    C = GDN_CHUNK
    n = q.shape[0] // C
    q = q.reshape(n, C, -1) * (GDN_HEAD_DIM ** -0.5)
    k = k.reshape(n, C, -1)
    v = v.reshape(n, C, -1)
    beta = beta.reshape(n, C, 1)
    gc = gc.reshape(n, C, 1)
    row = lax.broadcasted_iota(jnp.int32, (n, C, C), 1)
    col = lax.broadcasted_iota(jnp.int32, (n, C, C), 2)
    gcT = jnp.swapaxes(jnp.broadcast_to(gc, (n, C, C)), 1, 2)
    D = jnp.exp(jnp.where(row >= col, gc - gcT, NEG_INF))
    kb = k * beta
    vb = v * beta
    A = jnp.where(row > col, _bmm(kb, k, B_NT) * D, 0.0)
    Gam = jnp.exp(gc)
    kg = kb * Gam
    gl = gc[:, C - 1:C, :]
    kdec = jnp.exp(gl - gc)
    loc = dict(q=q, k=k, v=v, beta=beta, gc=gc, D=D, kb=kb, vb=vb, A=A, Gam=Gam, kg=kg,
               kdec=kdec, kd=k * kdec, qg=q * Gam, gam=jnp.exp(gl), row=row, col=col)
    uwm = Tm is None if uwm is None else uwm
    Tm = _tri_inverse(A) if Tm is None else Tm.reshape(n, C, C)
    if uwm:
        loc.update(u=_bmm(Tm, vb, B_NN), w=_bmm(Tm, kg, B_NN), M=_bmm(q, k, B_NT) * D)
    loc["Tm"] = Tm
    return loc


def _gdn_store_local(loc, r0, u_s, w_s, qg_s, kd_s, M_s, gam_s, c0):
    n = loc["u"].shape[0]
    R = n * GDN_CHUNK
    u_s[pl.ds(r0, R), :] = loc["u"].reshape(R, -1)
    w_s[pl.ds(r0, R), :] = loc["w"].reshape(R, -1)
    qg_s[pl.ds(r0, R), :] = loc["qg"].reshape(R, -1)
    kd_s[pl.ds(r0, R), :] = loc["kd"].reshape(R, -1)
    M_s[pl.ds(r0, R), :] = loc["M"].reshape(R, -1)
    gam_s[pl.ds(c0, n)] = jnp.broadcast_to(loc["gam"], (n, 1, LANES))


def _gdn_specs(S):
    blk = lambda off: pl.BlockSpec((S, LANES), lambda b, h: (b, off + h))
    return blk


def gdn_fwd(G, gates, P, g_on, *, B):
    T = G.shape[0]
    S = T // B
    C = GDN_CHUNK
    N = S // C
    grp = min(GDN_GROUP_FWD, N)
    R = grp * C
    hd = GDN_HEAD_DIM

    def body(q_ref, k_ref, v_ref, gt_ref, z_ref, gon_ref, o_ref, ob_ref, st_ref, tm_ref, A_s, B_s, Q_s, O_s, gam_s):
        h = pl.program_id(1)

        def local(gi, carry):
            r0 = pl.multiple_of(gi * R, R)
            gt = gt_ref[pl.ds(r0, R), :]
            loc = _gdn_local(q_ref[pl.ds(r0, R), :], k_ref[pl.ds(r0, R), :], v_ref[pl.ds(r0, R), :],
                             _pick_lane(gt, SM_B + h), _pick_lane(gt, SM_A + h))
            chunks = pl.ds(gi * grp, grp)
            tm_ref[0, 0, pl.ds(r0, R), :] = loc["Tm"].reshape(R, C)
            A_s[chunks] = -_bmm(loc["kd"], loc["w"], B_TN)
            B_s[chunks] = _bmm(loc["kd"], loc["u"], B_TN)
            Q_s[pl.ds(r0, R), :] = (loc["qg"] - _bmm(loc["M"], loc["w"], B_NN)).reshape(R, hd)
            O_s[pl.ds(r0, R), :] = _bmm(loc["M"], loc["u"], B_NN).reshape(R, hd)
            gam_s[chunks] = jnp.broadcast_to(loc["gam"], (grp, 1, LANES))
            return carry

        lax.fori_loop(0, N // grp, local, 0)

        def step(n, state):
            st_ref[0, 0, n] = state
            return state * gam_s[n] + _dotm(A_s[n], state, NN) + B_s[n]

        lax.fori_loop(0, N, step, jnp.zeros((hd, hd), f32))

        def outputs(gi, carry):
            r0 = pl.multiple_of(gi * R, R)
            Q = Q_s[pl.ds(r0, R), :].reshape(grp, C, hd)
            o = _bmm(Q, st_ref[0, 0, pl.ds(gi * grp, grp)], B_NN).reshape(R, hd) + O_s[pl.ds(r0, R), :]
            o_ref[pl.ds(r0, R), :] = o
            return carry

        lax.fori_loop(0, N // grp, outputs, 0)
        o = o_ref[...]
        z = z_ref[...]
        ob_ref[...] = (_head_rms(o, gon_ref[...])[0] * (z * jax.nn.sigmoid(z))).astype(ob_ref.dtype)

    blk = lambda off: pl.BlockSpec((S, LANES), lambda b, h: (b, off + h))
    rows = lambda: pltpu.VMEM((S, hd), f32)
    return pl.pallas_call(
        body, name="gdn_fwd", grid=(B, GDN_HEADS),
        in_specs=[blk(0), blk(GDN_HEADS), blk(2 * GDN_HEADS), pl.BlockSpec((S, LANES), lambda b, h: (b, 0)),
                  blk(COL_Z // LANES), pl.BlockSpec((1, hd), lambda b, h: (0, 0))],
        out_specs=[blk(0), blk(0), pl.BlockSpec((1, 1, N, hd, hd), lambda b, h: (b, h, 0, 0, 0)),
                   pl.BlockSpec((1, 1, S, C), lambda b, h: (b, h, 0, 0))],
        out_shape=[jax.ShapeDtypeStruct((T, GDN_WIDTH), f32), jax.ShapeDtypeStruct((T, GDN_WIDTH), MXU_DTYPE),
                   jax.ShapeDtypeStruct((B, GDN_HEADS, N, hd, hd), f32), jax.ShapeDtypeStruct((B, GDN_HEADS, S, C), f32)],
        scratch_shapes=[pltpu.VMEM((N, hd, hd), f32), pltpu.VMEM((N, hd, hd), f32), rows(), rows(),
                        pltpu.VMEM((N, 1, LANES), f32)],
        compiler_params=_cparams("parallel", "parallel"),
    )(G, G, G, gates, P, g_on)


def gdn_bwd(G, gates, P, g_on, o_raw, states, tm, d_oab, *, B):
    T = G.shape[0]
    S = T // B
    C = GDN_CHUNK
    N = S // C
    grp = min(GDN_GROUP, N)
    R = grp * C
    hd = GDN_HEAD_DIM

    def body(q_ref, k_ref, v_ref, gt_ref, z_ref, gon_ref, o_ref, st_ref, tm_ref, dob_ref,
             dq_ref, dk_ref, dv_ref, dgt_ref, dz_ref, dgon_ref,
             u_s, w_s, M_s, gam_s, do_s, A_s, C_s, dst_s):
        b, h = pl.program_id(0), pl.program_id(1)

        @pl.when((b == 0) & (h == 0))
        def _():
            dgon_ref[...] = jnp.zeros_like(dgon_ref)

        @pl.when(h == 0)
        def _():
            dgt_ref[...] = jnp.zeros_like(dgt_ref)

        def group_inputs(gi, uwm):
            r0 = pl.multiple_of(gi * R, R)
            gt = gt_ref[pl.ds(r0, R), :]
            return r0, _gdn_local(q_ref[pl.ds(r0, R), :], k_ref[pl.ds(r0, R), :], v_ref[pl.ds(r0, R), :],
                                  _pick_lane(gt, SM_B + h), _pick_lane(gt, SM_A + h), tm_ref[0, 0, pl.ds(r0, R), :], uwm)

        def local(gi, carry):
            r0, loc = group_inputs(gi, True)
            rows, chunks = pl.ds(r0, R), pl.ds(gi * grp, grp)
            u_s[rows, :] = loc["u"].reshape(R, hd)
            w_s[rows, :] = loc["w"].reshape(R, hd)
            M_s[rows, :] = loc["M"].reshape(R, C)
            gam_s[chunks] = jnp.broadcast_to(loc["gam"], (grp, 1, LANES))
            o, z, gon = o_ref[rows, :], z_ref[rows, :], gon_ref[...]
            dob = dob_ref[rows, :]
            on, ro = _head_rms(o, gon)
            sz = jax.nn.sigmoid(z)
            dz_ref[rows, :] = (dob * on * (sz * (1.0 + z * (1.0 - sz)))).astype(dz_ref.dtype)
            do, dgon = _head_rms_bwd(o, ro, gon, dob * (z * sz))
            do_s[rows, :] = do
            dgon_ref[...] += dgon
            A_s[chunks] = -_bmm(loc["kd"], loc["w"], B_TN)
            C_s[chunks] = _bmm(loc["qg"] - _bmm(loc["M"], loc["w"], B_NN), do.reshape(grp, C, hd), B_TN)
            return carry

        lax.fori_loop(0, N // grp, local, 0)

        def step(t, dS):
            n = N - 1 - t
            dst_s[n] = dS
            return dS * gam_s[n] + _dotm(A_s[n], dS, TN) + C_s[n]

        lax.fori_loop(0, N, step, jnp.zeros((hd, hd), f32))

        def finish(gi, carry):
            r0, L = group_inputs(gi, False)
            n = grp
            rows, chunks = pl.ds(r0, R), pl.ds(gi * grp, grp)
            g3 = lambda ref: ref[rows, :].reshape(n, C, -1)
            u, w, do = g3(u_s), g3(w_s), g3(do_s)
            L["M"] = g3(M_s)
            state, dS = st_ref[0, 0, chunks], dst_s[chunks]
            v_new = u - _bmm(w, state, B_NN)
            du = _bmm(L["M"], do, B_TN) + _bmm(L["kd"], dS, B_NN)
            dw = -_bmm(du, state, B_NT)
            dqg = _bmm(do, state, B_NT)
            dM = _bmm(do, v_new, B_NT)
            dkd = _bmm(v_new, dS, B_NT)
            dgl_state = jnp.sum(jnp.sum(dS * state, axis=2, keepdims=True), axis=1, keepdims=True) * L["gam"]
            TmT = jnp.swapaxes(L["Tm"], 1, 2)
            dTm = _bmm(du, L["vb"], B_NT) + _bmm(dw, L["kg"], B_NT)
            dvb = _bmm(TmT, du, B_NN)
            dkg = _bmm(TmT, dw, B_NN)
            dA = jnp.where(L["row"] > L["col"], -_bmm(_bmm(TmT, dTm, B_NN), TmT, B_NN), 0.0)
            dKK = dA * L["D"]
            dQK = dM * L["D"]
            dkb = _bmm(dKK, L["k"], B_NN) + dkg * L["Gam"]
            dk = (_bmm(dKK, L["kb"], B_TN) + _bmm(dQK, L["q"], B_TN) + dkd * L["kdec"] + L["beta"] * dkb)
            dq = (_bmm(dQK, L["k"], B_NN) + dqg * L["Gam"]) * (GDN_HEAD_DIM ** -0.5)
            E = dA * L["A"] + dM * L["M"]
            r = jnp.sum(dkd * L["kd"], axis=-1, keepdims=True)
            dgc = (jnp.sum(E, axis=2, keepdims=True) - jnp.sum(jnp.swapaxes(E, 1, 2), axis=2, keepdims=True)
                   + jnp.sum(dkg * L["kg"], axis=-1, keepdims=True) + jnp.sum(dqg * L["qg"], axis=-1, keepdims=True) - r)
            dgl = jnp.sum(r, axis=1, keepdims=True) + dgl_state
            rowc = lax.broadcasted_iota(jnp.int32, (n, C, 1), 1)
            dgc = dgc + jnp.where(rowc == C - 1, dgl, 0.0)
            dbeta = jnp.sum(dkb * L["k"], axis=-1, keepdims=True) + jnp.sum(dvb * L["v"], axis=-1, keepdims=True)
            dq_ref[rows, :] = dq.reshape(R, hd)
            dk_ref[rows, :] = dk.reshape(R, hd)
            dv_ref[rows, :] = (L["beta"] * dvb).reshape(R, hd)
            lane = lax.broadcasted_iota(jnp.int32, (R, LANES), 1)
            dgt_ref[rows, :] += (jnp.where(lane == SM_B + h, dbeta.reshape(R, 1), 0.0)
                                 + jnp.where(lane == SM_A + h, dgc.reshape(R, 1), 0.0))
            return carry

        lax.fori_loop(0, N // grp, finish, 0)

    blk = lambda off: pl.BlockSpec((S, LANES), lambda b, h: (b, off + h))
    rows = lambda: pltpu.VMEM((S, hd), f32)
    return pl.pallas_call(
        body, name="gdn_bwd", grid=(B, GDN_HEADS),
        in_specs=[blk(0), blk(GDN_HEADS), blk(2 * GDN_HEADS), pl.BlockSpec((S, LANES), lambda b, h: (b, 0)),
                  blk(COL_Z // LANES), pl.BlockSpec((1, hd), lambda b, h: (0, 0)), blk(0),
                  pl.BlockSpec((1, 1, N, hd, hd), lambda b, h: (b, h, 0, 0, 0)),
                  pl.BlockSpec((1, 1, S, C), lambda b, h: (b, h, 0, 0)), blk(GDN_HEADS)],
        out_specs=[blk(0), blk(0), blk(0), pl.BlockSpec((S, LANES), lambda b, h: (b, 0)), blk(0),
                   pl.BlockSpec((1, hd), lambda b, h: (0, 0))],
        out_shape=[jax.ShapeDtypeStruct((T, GDN_WIDTH), f32), jax.ShapeDtypeStruct((T, GDN_WIDTH), f32),
                   jax.ShapeDtypeStruct((T, GDN_WIDTH), f32), jax.ShapeDtypeStruct((T, LANES), f32),
                   jax.ShapeDtypeStruct((T, GDN_WIDTH), MXU_DTYPE), jax.ShapeDtypeStruct((1, hd), f32)],
        scratch_shapes=[rows(), rows(), pltpu.VMEM((S, C), f32), pltpu.VMEM((N, 1, LANES), f32), rows(),
                        pltpu.VMEM((N, hd, hd), f32), pltpu.VMEM((N, hd, hd), f32), pltpu.VMEM((N, hd, hd), f32)],
        compiler_params=_cparams("arbitrary", "arbitrary"),
    )(G, G, G, gates, P, g_on, o_raw, states, tm, d_oab)


IN_SPLIT = (0, 1536, 1544, 3080, 3088, 3600)


IN_SHARD = IN_DIM // 4
IN_SHARD_PAD = 928


def align_w_in_t(wt):
    s = IN_SPLIT
    pad = jnp.zeros((IN_ALIGNED - IN_DIM, wt.shape[1]), wt.dtype)
    return jnp.concatenate([wt[s[0]:s[1]], wt[s[2]:s[3]], wt[s[4]:s[5]], wt[s[1]:s[2]], wt[s[3]:s[4]], pad], axis=0)


def unalign_w_in_t(wa):
    return jnp.concatenate([wa[0:1536], wa[COL_SMALL:COL_SMALL + 8], wa[1536:3072],
                            wa[COL_SMALL + 8:COL_SMALL + 16], wa[3072:3584]], axis=0)


IN_SEGMENTS = (((0, 1536), 0), ((1536, 1544), COL_SMALL), ((1544, 3080), 1536), ((3080, 3088), COL_SMALL + 8),
               ((3088, 3600), 3072))


def align_w_in_slots(slots):
    pieces = []
    for (lo, hi), _ in sorted(IN_SEGMENTS, key=lambda seg: seg[1]):
        for k in range(N_CHIPS):
            a, b = max(lo, k * IN_SHARD), min(hi, (k + 1) * IN_SHARD)
            if a < b:
                pieces.append(slots[k, a - k * IN_SHARD:b - k * IN_SHARD])
    pieces.append(jnp.zeros((IN_ALIGNED - IN_DIM, slots.shape[2]), slots.dtype))
    return jnp.concatenate(pieces, axis=0)


def unalign_to_slots(wa):
    slots = []
    for k in range(N_CHIPS):
        lo, hi = k * IN_SHARD, (k + 1) * IN_SHARD
        pieces = []
        for (a, b), first in IN_SEGMENTS:
            x, y = max(a, lo), min(b, hi)
            if x < y:
                pieces.append(wa[first + x - a:first + y - a])
        pieces.append(jnp.zeros((IN_SHARD_PAD - IN_SHARD, wa.shape[1]), wa.dtype))
        slots.append(jnp.concatenate(pieces, axis=0))
    return jnp.stack(slots)


def _lanes_vec(pieces):
    v = jnp.zeros((1, LANES), f32)
    for off, a in pieces:
        v = lax.dynamic_update_slice(v, a.astype(f32), (0, off))
    return v


def local_step(x, mem, target, w, sp, *, B):
    T = x.shape[0]
    S = T // B
    gq8, gk8 = jnp.tile(sp["fox_qnorm_g"], (1, FOX_HEADS)), jnp.tile(sp["fox_knorm_g"], (1, FOX_HEADS))
    go2 = jnp.tile(sp["fox_onorm_g"], (1, 2))
    bias = _lanes_vec([(SM_F, sp["fox_f_bias"]), (SM_A, sp["gdn_dt_bias"])])
    alog = _lanes_vec([(SM_A, sp["gdn_A_log"])])

    h1 = rms_fwd(x, sp["norm_mix_g"], name="rms_mix")
    P = matmul(h1, w["wa_t"], tb=True, name="mm_in", tn=IN_TILE)
    gates = gates_fwd(P, bias, alog, B=B)
    c = gates[:, SM_F:SM_F + FOX_HEADS].reshape(B, S, FOX_HEADS).transpose(0, 2, 1)
    ccol, crow = c[..., None], c.reshape(B, FOX_HEADS, S // FOX_TQ, 1, FOX_TQ)
    qn, kn, vb = fox_prep_fwd(P, gq8, gk8)
    o_raw, o_a, lse = fox_core_fwd(qn, kn, vb, ccol, crow, go2, B=B)
    G = gdn_prep_fwd(P, w["conv_w"], B=B)
    ob_raw, o_b, states, gdn_tm = gdn_fwd(G, gates, P, sp["gdn_onorm_g"], B=B)
    oab = jnp.concatenate([o_a, o_b], axis=1)
    if "late" in w:
        w = {**w, **w["late"](oab)}
    x2, hq = matmul_rows(oab, w["w_out"], (x, sp["norm_xattn_g"]), mode="rms_fwd", name="mm_out_rms")
    hm = rms_fwd(mem, sp["mem_norm_g"], name="rms_mem")
    cq = matmul(hq, w["w_cq"], name="mm_cq")
    ckv = matmul(hm, w["w_ckv"], name="mm_ckv")
    co = xattn_fwd(cq, ckv, sp["xattn_qnorm_g"], sp["xattn_knorm_g"], B=B)
    x3, hf = matmul_rows(co, w["w_co"], (x2, sp["norm_mlp_g"]), mode="rms_fwd", name="mm_co_rms")
    act = matmul(hf, w["w_mlp1"], b_stacked=True, relu2_out=True, out_dtype=MXU_DTYPE, name="mm_mlp1")
    dy, dy_op, loss = matmul_rows(act, w["w_mlp2"], (x3, target), mode="loss", name="mm_mlp2_loss")

    da = matmul(dy_op, w["w_mlp2"], tb=True, relu2_bwd_aux=act, out_dtype=MXU_DTYPE, name="mm_d_act")
    g_mlp2 = matmul(act, dy_op, ta=True, out_dtype=WIRE_DTYPE, name="mm_g_mlp2")
    g_mlp1 = matmul(hf, da, ta=True, out_stacked=True, out_dtype=WIRE_DTYPE, name="mm_g_mlp1")
    by_rows = lambda g: g.reshape(N_CHIPS, g.shape[0] // N_CHIPS, g.shape[1])
    early = w.get("grads_ready", lambda grads: jnp.zeros((1, 1), f32))
    tok = early(dict(w_mlp1=g_mlp1, w_mlp2=by_rows(g_mlp2)))[0, 0]
    dx3, g_norm_mlp = matmul_rows(da, w["w_mlp1"], (x3, sp["norm_mlp_g"] + tok, dy), mode="rms_bwd", tb=True,
                                  b_stacked=True, name="mm_d_hf_rms")
    dco = matmul(dx3, w["w_co"], tb=True, name="mm_d_co")
    g_co = matmul(co, dx3, ta=True, out_dtype=WIRE_DTYPE, name="mm_g_co")
    g_co = g_co.reshape(XATTN_WIDTH, N_CHIPS, D_MODEL // N_CHIPS).transpose(1, 0, 2)
    dcq, dckv, g_xq, g_xk = xattn_bwd(cq, ckv, sp["xattn_qnorm_g"], sp["xattn_knorm_g"], dco, B=B)
    g_cq = matmul(hq, dcq, ta=True, out_dtype=WIRE_DTYPE, name="mm_g_cq")
    g_ckv = matmul(hm, dckv, ta=True, out_dtype=WIRE_DTYPE, name="mm_g_ckv")
    _, g_mem_norm = matmul_rows(dckv, w["w_ckv"], (mem, sp["mem_norm_g"], None), mode="rms_bwd", tb=True, name="mm_d_hm_rms")
    dx2, g_norm_xattn = matmul_rows(dcq, w["w_cq"], (x2, sp["norm_xattn_g"], dx3), mode="rms_bwd", tb=True, name="mm_d_hq_rms")
    doab = matmul(dx2, w["w_out"], tb=True, name="mm_d_oab")
    g_out = matmul(oab, dx2, ta=True, out_dtype=WIRE_DTYPE, name="mm_g_out")
    tok = early(dict(w_co=g_co, w_cq=by_rows(g_cq), w_ckv=by_rows(g_ckv), w_out=by_rows(g_out)))[0, 0]
    dqn, dkn, dv_f, dckey, dcrow, dgo2 = fox_core_bwd(qn, kn, vb, ccol, crow, go2 + tok, o_raw, lse, doab, B=B)
    dq_f, dk_f, dgq8, dgk8 = fox_prep_bwd(P, gq8, gk8, dqn, dkn)
    dGq, dGk, dGv, dgt, dz, g_gdn_on = gdn_bwd(G, gates, P, sp["gdn_onorm_g"], ob_raw, states, gdn_tm, doab, B=B)
    dPg, g_conv = gdn_prep_bwd(P, w["conv_w"], dGq, dGk, dGv, B=B)
    dc = (dckey[:, :, 0, :] + dcrow.reshape(B, FOX_HEADS, S)).transpose(0, 2, 1).reshape(T, FOX_HEADS)
    dgates = dgt + jnp.pad(dc, ((0, 0), (SM_F, LANES - SM_F - FOX_HEADS)))
    dsmall, par = gates_bwd(P, bias, alog, dgates, B=B)
    dP = jnp.concatenate([dq_f, dk_f, dv_f, dPg, dz, dsmall, jnp.zeros((T, IN_ALIGNED - COL_SMALL - LANES), MXU_DTYPE)], axis=1)
    g_wa = matmul(dP, h1, ta=True, out_dtype=WIRE_DTYPE, name="mm_g_in", tm=IN_TILE)
    g_in = unalign_to_slots(g_wa)
    tok = early(dict(w_in=g_in))[0, 0]
    dx, g_norm_mix = matmul_rows(dP, w["wa_t"], (x, sp["norm_mix_g"] + tok, dx2), mode="rms_bwd", tk=IN_TILE,
                                 name="mm_d_h1_rms")

    fold = lambda g: jnp.sum(g.reshape(-1, FOX_HEAD_DIM), axis=0, keepdims=True)
    big = dict(w_in=g_in, w_out=by_rows(g_out), w_cq=by_rows(g_cq), w_ckv=by_rows(g_ckv), w_co=g_co, w_mlp1=g_mlp1,
               w_mlp2=by_rows(g_mlp2))
    small = dict(norm_mix_g=g_norm_mix, fox_qnorm_g=fold(dgq8), fox_knorm_g=fold(dgk8),
                 fox_f_bias=par[0:1, SM_F:SM_F + FOX_HEADS], fox_onorm_g=fold(dgo2), gdn_conv_w=g_conv,
                 gdn_A_log=par[1:2, SM_A:SM_A + GDN_HEADS], gdn_dt_bias=par[0:1, SM_A:SM_A + GDN_HEADS],
                 gdn_onorm_g=g_gdn_on, norm_xattn_g=g_norm_xattn, mem_norm_g=g_mem_norm,
                 xattn_qnorm_g=g_xq, xattn_knorm_g=g_xk, norm_mlp_g=g_norm_mlp)
    return loss, dx, big, small


MESH_IDS = pl.DeviceIdType.MESH
N_CHIPS = 4
HBM_SPEC = pl.BlockSpec(memory_space=pltpu.HBM)
PACK_ROWS = 30720
PACK_HALF = PACK_ROWS // 2
PACK_BLOCK = 3072


def _place():
    return lax.axis_index("x"), lax.axis_index("y"), lax.axis_index("c")


def _other_chips(x, y):
    return [(1 - x, y), (x, 1 - y), (1 - x, 1 - y)]


def _remote(src, dst, send_sem, recv_sem, to):
    return pltpu.make_async_remote_copy(src_ref=src, dst_ref=dst, send_sem=send_sem, recv_sem=recv_sem,
                                        device_id=to, device_id_type=MESH_IDS)


def all_gather_shards(packed):
    half = PACK_HALF

    def body(src_ref, out_ref, send_sems, recv_sems):
        x, y, c = _place()
        me_chip = 2 * x + y
        sibling = (x, y, 1 - c)
        chips = _other_chips(x, y)

        def rows(chip, core):
            return out_ref.at[chip, pl.ds(core * half, half), :]

        sends = [_remote(src_ref.at[pl.ds(c * half, half), :], rows(me_chip, c), send_sems.at[j], recv_sems.at[j], (px, py, c))
                 for j, (px, py) in enumerate(chips)]
        for cp in sends:
            cp.start()
        passed = []
        for j, (px, py) in enumerate(chips):
            theirs = rows(2 * px + py, c)
            _remote(theirs, theirs, send_sems.at[j], recv_sems.at[j], (px, py, c)).wait_recv()
            cp = _remote(theirs, theirs, send_sems.at[3 + j], recv_sems.at[3 + j], sibling)
            cp.start()
            passed.append(cp)
        for j, (px, py) in enumerate(chips):
            theirs = rows(2 * px + py, 1 - c)
            _remote(theirs, theirs, send_sems.at[3 + j], recv_sems.at[3 + j], sibling).wait_recv()
        for cp in sends + passed:
            cp.wait_send()

    return pl.pallas_call(
        body, name="all_gather_shards", in_specs=[HBM_SPEC], out_specs=HBM_SPEC,
        out_shape=jax.ShapeDtypeStruct((N_CHIPS,) + packed.shape, packed.dtype),
        scratch_shapes=[pltpu.SemaphoreType.DMA((6,)), pltpu.SemaphoreType.DMA((6,))],
    )(packed)


def exchange_core_halves(G):
    half = PACK_HALF

    def body(g_ref, land_ref, send_sem, recv_sem):
        x, y, c = _place()
        cp = _remote(g_ref.at[:, pl.ds((1 - c) * half, half), :], land_ref, send_sem, recv_sem, (x, y, 1 - c))
        cp.start()
        cp.wait()

    return pl.pallas_call(
        body, name="exchange_core_halves", in_specs=[HBM_SPEC], out_specs=HBM_SPEC,
        out_shape=jax.ShapeDtypeStruct((N_CHIPS, half, LANES), G.dtype),
        scratch_shapes=[pltpu.SemaphoreType.DMA(()), pltpu.SemaphoreType.DMA(())],
    )(G)


def add_core_halves(G, land, core):
    nb = PACK_HALF // PACK_BLOCK

    def body(c_ref, g_ref, l_ref, o_ref):
        o_ref[...] = (g_ref[...].astype(f32) + l_ref[...].astype(f32)).astype(o_ref.dtype)

    blk = (1, PACK_BLOCK, LANES)
    return pl.pallas_call(
        body, name="add_core_halves",
        grid_spec=pltpu.PrefetchScalarGridSpec(
            num_scalar_prefetch=1, grid=(N_CHIPS, nb),
            in_specs=[pl.BlockSpec(blk, lambda k, i, c_ref: (k, c_ref[0] * nb + i, 0)),
                      pl.BlockSpec(blk, lambda k, i, c_ref: (k, i, 0))],
            out_specs=pl.BlockSpec(blk, lambda k, i, c_ref: (k, i, 0))),
        out_shape=jax.ShapeDtypeStruct(land.shape, land.dtype),
        compiler_params=_cparams("parallel", "parallel"),
    )(core, G, land)


def scatter_to_chips(part):
    def body(p_ref, land_ref, send_sems, recv_sems):
        x, y, c = _place()
        me_chip = 2 * x + y
        chips = _other_chips(x, y)
        sends = [_remote(p_ref.at[2 * px + py], land_ref.at[me_chip], send_sems.at[j], recv_sems.at[j], (px, py, c))
                 for j, (px, py) in enumerate(chips)]
        for cp in sends:
            cp.start()
        for j, (px, py) in enumerate(chips):
            slot = land_ref.at[2 * px + py]
            _remote(slot, slot, send_sems.at[j], recv_sems.at[j], (px, py, c)).wait_recv()
        for cp in sends:
            cp.wait_send()

    return pl.pallas_call(
        body, name="scatter_to_chips", in_specs=[HBM_SPEC], out_specs=HBM_SPEC,
        out_shape=jax.ShapeDtypeStruct(part.shape, part.dtype),
        scratch_shapes=[pltpu.SemaphoreType.DMA((3,)), pltpu.SemaphoreType.DMA((3,))],
    )(part)


def sum_chips(part, land, order):
    nb = PACK_HALF // PACK_BLOCK

    def body(order_ref, p_ref, l1_ref, l2_ref, l3_ref, o_ref):
        o_ref[...] = ((p_ref[0].astype(f32) + l1_ref[0].astype(f32)) + l2_ref[0].astype(f32)) + l3_ref[0].astype(f32)

    slot = lambda j: pl.BlockSpec((1, PACK_BLOCK, LANES), lambda i, order_ref: (order_ref[j], i, 0))
    return pl.pallas_call(
        body, name="sum_chips",
        grid_spec=pltpu.PrefetchScalarGridSpec(
            num_scalar_prefetch=1, grid=(nb,), in_specs=[slot(0), slot(1), slot(2), slot(3)],
            out_specs=pl.BlockSpec((PACK_BLOCK, LANES), lambda i, order_ref: (i, 0))),
        out_shape=jax.ShapeDtypeStruct((PACK_HALF, LANES), f32),
        compiler_params=_cparams("parallel"),
    )(order, part, land, land, land)


def swap_core_halves(red):
    def body(r_ref, out_ref, send_sem, recv_sem):
        x, y, c = _place()
        cp = _remote(r_ref, out_ref, send_sem, recv_sem, (x, y, 1 - c))
        cp.start()
        cp.wait()

    return pl.pallas_call(
        body, name="swap_core_halves", in_specs=[HBM_SPEC], out_specs=HBM_SPEC,
        out_shape=jax.ShapeDtypeStruct(red.shape, red.dtype),
        scratch_shapes=[pltpu.SemaphoreType.DMA(()), pltpu.SemaphoreType.DMA(())],
    )(red)


def _half(ref, core):
    rows = ref.shape[-2] // 2
    return ref.at[(slice(None),) * (len(ref.shape) - 2) + (pl.ds(core * rows, rows), slice(None))]


def gather_weights(shards, conv):
    n = len(shards)

    def body(*refs):
        src, conv_src = refs[:n], refs[n]
        out, conv_out = refs[n + 1:2 * n + 1], refs[2 * n + 1]
        send_sems, recv_sems = refs[2 * n + 2], refs[2 * n + 3]
        x, y, c = _place()
        me_chip = 2 * x + y
        sibling = (x, y, 1 - c)
        chips = _other_chips(x, y)
        sends = []
        for a in range(n):
            for j, (px, py) in enumerate(chips):
                sends.append(_remote(_half(src[a], c), _half(out[a].at[me_chip], c),
                                     send_sems.at[6 * a + j], recv_sems.at[6 * a + j], (px, py, c)))
        for j, (px, py) in enumerate(chips):
            sends.append(_remote(conv_src, conv_out.at[me_chip], send_sems.at[6 * n + j], recv_sems.at[6 * n + j], (px, py, c)))
        for cp in sends:
            cp.start()
        passed = []
        for a in range(n):
            for j, (px, py) in enumerate(chips):
                theirs = _half(out[a].at[2 * px + py], c)
                _remote(theirs, theirs, send_sems.at[6 * a + j], recv_sems.at[6 * a + j], (px, py, c)).wait_recv()
                cp = _remote(theirs, theirs, send_sems.at[6 * a + 3 + j], recv_sems.at[6 * a + 3 + j], sibling)
                cp.start()
                passed.append(cp)
        for j, (px, py) in enumerate(chips):
            theirs = conv_out.at[2 * px + py]
            _remote(theirs, theirs, send_sems.at[6 * n + j], recv_sems.at[6 * n + j], (px, py, c)).wait_recv()
        for a in range(n):
            for j, (px, py) in enumerate(chips):
                theirs = _half(out[a].at[2 * px + py], 1 - c)
                _remote(theirs, theirs, send_sems.at[6 * a + 3 + j], recv_sems.at[6 * a + 3 + j], sibling).wait_recv()
        for cp in sends + passed:
            cp.wait_send()

    return pl.pallas_call(
        body, name="gather_weights", in_specs=[HBM_SPEC] * (n + 1), out_specs=[HBM_SPEC] * (n + 1),
        out_shape=[jax.ShapeDtypeStruct((N_CHIPS,) + s.shape, s.dtype) for s in list(shards) + [conv]],
        scratch_shapes=[pltpu.SemaphoreType.DMA((6 * n + 3,)), pltpu.SemaphoreType.DMA((6 * n + 3,))],
    )(*shards, conv)


SEM_SPEC = pl.BlockSpec(memory_space=pltpu.SEMAPHORE)
SPLIT_EFFECT = pltpu.SideEffectType.DATAFLOW_SIDE_EFFECTING


def _gather_async_copies(src, land, send_sems, recv_sems, x, y, c):
    me_chip = 2 * x + y
    sends, arrivals = [], []
    for a in range(len(src)):
        for j, (px, py) in enumerate(_other_chips(x, y)):
            for core in range(2):
                sends.append(_remote(_half(src[a], c), _half(land[a].at[me_chip], c), send_sems.at[6 * a + 2 * j + core],
                                     recv_sems.at[6 * a + 2 * j + c], (px, py, core)))
                theirs = _half(land[a].at[2 * px + py], core)
                arrivals.append(_remote(theirs, theirs, send_sems.at[6 * a + 2 * j + core],
                                        recv_sems.at[6 * a + 2 * j + core], (px, py, core)))
    return sends, arrivals


def gather_weights_start(shards, after):
    n = len(shards)

    def body(*refs):
        src, land = refs[:n], refs[n:2 * n]
        send_sems, recv_sems, token = refs[2 * n + 1], refs[2 * n + 2], refs[4 * n + 3]
        x, y, c = _place()
        for cp in _gather_async_copies(src, land, send_sems, recv_sems, x, y, c)[0]:
            cp.start()
        token[...] = jnp.zeros_like(token)

    zones = [pltpu.with_memory_space_constraint(lax.empty((N_CHIPS,) + s.shape, s.dtype), pltpu.HBM) for s in shards]
    srcs = [pltpu.with_memory_space_constraint(s, pltpu.HBM) for s in shards]
    out = pl.pallas_call(
        body, name="gather_weights_start",
        out_shape=[pltpu.SemaphoreType.DMA((6 * n,)), pltpu.SemaphoreType.DMA((6 * n,))]
        + [pltpu.HBM(s.shape, s.dtype) for s in shards] + [pltpu.HBM(z.shape, z.dtype) for z in zones]
        + [jax.ShapeDtypeStruct((8, LANES), f32)],
        in_specs=[HBM_SPEC] * (2 * n) + [pl.BlockSpec(memory_space=pl.ANY)],
        out_specs=[SEM_SPEC, SEM_SPEC] + [HBM_SPEC] * (2 * n) + [pl.BlockSpec(memory_space=pltpu.VMEM)],
        input_output_aliases={i: 2 + i for i in range(2 * n)},
        compiler_params=pltpu.CompilerParams(has_side_effects=SPLIT_EFFECT),
    )(*srcs, *zones, after)
    return out[0], out[1], out[2:2 + n], out[2 + n:2 + 2 * n], out[-1]


def gather_weights_wait(send_sems, recv_sems, shards, zones, after):
    n = len(shards)

    def body(*refs):
        src, land = refs[:n], refs[n:2 * n]
        send_sems, recv_sems = refs[2 * n], refs[2 * n + 1]
        x, y, c = _place()
        sends, arrivals = _gather_async_copies(src, land, send_sems, recv_sems, x, y, c)
        for cp in sends:
            cp.wait_send()
        for cp in arrivals:
            cp.wait_recv()

    out = pl.pallas_call(
        body, name="gather_weights_wait",
        out_shape=[pltpu.HBM(s.shape, s.dtype) for s in shards] + [pltpu.HBM(z.shape, z.dtype) for z in zones],
        in_specs=[HBM_SPEC] * (2 * n) + [SEM_SPEC, SEM_SPEC, pl.BlockSpec(memory_space=pl.ANY)],
        out_specs=[HBM_SPEC] * (2 * n),
        input_output_aliases={i: i for i in range(2 * n)},
        compiler_params=pltpu.CompilerParams(has_side_effects=SPLIT_EFFECT),
    )(*shards, *zones, send_sems, recv_sems, after)
    return out[n:]


def swap_grad_halves(grads, *, name):
    n = len(grads)

    def body(*refs):
        g, land, send_sems, recv_sems = refs[:n], refs[n:2 * n], refs[2 * n], refs[2 * n + 1]
        x, y, c = _place()
        copies = [_remote(_half(g[a], 1 - c), land[a], send_sems.at[a], recv_sems.at[a], (x, y, 1 - c)) for a in range(n)]
        for cp in copies:
            cp.start()
        for cp in copies:
            cp.wait()

    return pl.pallas_call(
        body, name=name, in_specs=[HBM_SPEC] * n, out_specs=[HBM_SPEC] * n,
        out_shape=[jax.ShapeDtypeStruct((N_CHIPS, g.shape[1] // 2, g.shape[2]), g.dtype) for g in grads],
        scratch_shapes=[pltpu.SemaphoreType.DMA((n,)), pltpu.SemaphoreType.DMA((n,))],
    )(*grads)


GRAD_ROWS = 256


def add_grad_halves(g, land, core, *, name):
    _, half, cols = land.shape
    tr = GRAD_ROWS if half % GRAD_ROWS == 0 else half
    nb = half // tr

    def body(c_ref, g_ref, l_ref, o_ref):
        o_ref[...] = (g_ref[...].astype(f32) + l_ref[...].astype(f32)).astype(o_ref.dtype)

    blk = (1, tr, cols)
    return pl.pallas_call(
        body, name=name,
        grid_spec=pltpu.PrefetchScalarGridSpec(
            num_scalar_prefetch=1, grid=(N_CHIPS, nb),
            in_specs=[pl.BlockSpec(blk, lambda k, i, c_ref: (k, c_ref[0] * nb + i, 0)),
                      pl.BlockSpec(blk, lambda k, i, c_ref: (k, i, 0))],
            out_specs=pl.BlockSpec(blk, lambda k, i, c_ref: (k, i, 0))),
        out_shape=jax.ShapeDtypeStruct(land.shape, land.dtype),
        compiler_params=_cparams("parallel", "parallel"),
    )(core, g, land)


def scatter_grads(parts):
    n = len(parts)

    def body(*refs):
        p, land, send_sems, recv_sems = refs[:n], refs[n:2 * n], refs[2 * n], refs[2 * n + 1]
        x, y, c = _place()
        me_chip = 2 * x + y
        chips = _other_chips(x, y)
        sends = [_remote(p[a].at[2 * px + py], land[a].at[me_chip], send_sems.at[3 * a + j], recv_sems.at[3 * a + j], (px, py, c))
                 for a in range(n) for j, (px, py) in enumerate(chips)]
        for cp in sends:
            cp.start()
        for a in range(n):
            for j, (px, py) in enumerate(chips):
                slot = land[a].at[2 * px + py]
                _remote(slot, slot, send_sems.at[3 * a + j], recv_sems.at[3 * a + j], (px, py, c)).wait_recv()
        for cp in sends:
            cp.wait_send()

    return pl.pallas_call(
        body, name="scatter_grads", in_specs=[HBM_SPEC] * n, out_specs=[HBM_SPEC] * n,
        out_shape=[jax.ShapeDtypeStruct(p.shape, p.dtype) for p in parts],
        scratch_shapes=[pltpu.SemaphoreType.DMA((3 * n,)), pltpu.SemaphoreType.DMA((3 * n,))],
    )(*parts)


def _scatter_async_copies(parts, land, send_sems, recv_sems, x, y, c):
    me_chip = 2 * x + y
    sends, arrivals = [], []
    for a in range(len(parts)):
        for j, (px, py) in enumerate(_other_chips(x, y)):
            sems = (send_sems.at[3 * a + j], recv_sems.at[3 * a + j], (px, py, c))
            sends.append(_remote(parts[a].at[2 * px + py], land[a].at[me_chip], *sems))
            slot = land[a].at[2 * px + py]
            arrivals.append(_remote(slot, slot, *sems))
    return sends, arrivals


def scatter_grads_start(parts, *, name):
    n = len(parts)

    def body(*refs):
        p, land = refs[:n], refs[n:2 * n]
        send_sems, recv_sems, token = refs[2 * n], refs[2 * n + 1], refs[4 * n + 2]
        x, y, c = _place()
        for cp in _scatter_async_copies(p, land, send_sems, recv_sems, x, y, c)[0]:
            cp.start()
        token[...] = jnp.zeros_like(token)

    zones = [pltpu.with_memory_space_constraint(lax.empty(p.shape, p.dtype), pltpu.HBM) for p in parts]
    srcs = [pltpu.with_memory_space_constraint(p, pltpu.HBM) for p in parts]
    hbm = [pltpu.HBM(p.shape, p.dtype) for p in parts]
    out = pl.pallas_call(
        body, name=name,
        out_shape=[pltpu.SemaphoreType.DMA((3 * n,)), pltpu.SemaphoreType.DMA((3 * n,))] + hbm + hbm
        + [jax.ShapeDtypeStruct((8, LANES), f32)],
        in_specs=[HBM_SPEC] * (2 * n),
        out_specs=[SEM_SPEC, SEM_SPEC] + [HBM_SPEC] * (2 * n) + [pl.BlockSpec(memory_space=pltpu.VMEM)],
        input_output_aliases={i: 2 + i for i in range(2 * n)},
        compiler_params=pltpu.CompilerParams(has_side_effects=SPLIT_EFFECT),
    )(*srcs, *zones)
    return out[0], out[1], out[2:2 + n], out[2 + n:2 + 2 * n], out[-1]


def scatter_grads_wait(send_sems, recv_sems, parts, zones, after, *, name):
    n = len(parts)

    def body(*refs):
        p, land = refs[:n], refs[n:2 * n]
        x, y, c = _place()
        sends, arrivals = _scatter_async_copies(p, land, refs[2 * n], refs[2 * n + 1], x, y, c)
        for cp in sends:
            cp.wait_send()
        for cp in arrivals:
            cp.wait_recv()

    hbm = [pltpu.HBM(p.shape, p.dtype) for p in parts]
    out = pl.pallas_call(
        body, name=name, out_shape=hbm + hbm,
        in_specs=[HBM_SPEC] * (2 * n) + [SEM_SPEC, SEM_SPEC, pl.BlockSpec(memory_space=pl.ANY)],
        out_specs=[HBM_SPEC] * (2 * n),
        input_output_aliases={i: i for i in range(2 * n)},
        compiler_params=pltpu.CompilerParams(has_side_effects=SPLIT_EFFECT),
    )(*parts, *zones, send_sems, recv_sems, after)
    return out[:n], out[n:]


def sum_grads(part, land, order, *, name):
    _, half, cols = part.shape
    tr = GRAD_ROWS if half % GRAD_ROWS == 0 else half

    def body(order_ref, p_ref, l1_ref, l2_ref, l3_ref, o_ref):
        o_ref[...] = ((p_ref[0].astype(f32) + l1_ref[0].astype(f32)) + l2_ref[0].astype(f32)) + l3_ref[0].astype(f32)

    slot = lambda j: pl.BlockSpec((1, tr, cols), lambda i, order_ref: (order_ref[j], i, 0))
    return pl.pallas_call(
        body, name=name,
        grid_spec=pltpu.PrefetchScalarGridSpec(
            num_scalar_prefetch=1, grid=(half // tr,), in_specs=[slot(0), slot(1), slot(2), slot(3)],
            out_specs=pl.BlockSpec((tr, cols), lambda i, order_ref: (i, 0))),
        out_shape=jax.ShapeDtypeStruct((half, cols), f32),
        compiler_params=_cparams("parallel"),
    )(order, part, land, land, land)


def _peer(x, y, c, r):
    return ((1 - x) if r & 4 else x, (1 - y) if r & 2 else y, (1 - c) if r & 1 else c)


def _reduce_async_copies(grads, land, send_sems, recv_sems, x, y, c):
    me = 4 * x + 2 * y + c
    sends, arrivals = [], []
    for a in range(len(grads)):
        for r in range(1, N_DEV):
            px, py, pc = _peer(x, y, c, r)
            sems = (send_sems.at[7 * a + r - 1], recv_sems.at[7 * a + r - 1], (px, py, pc))
            sends.append(_remote(_half(grads[a].at[2 * px + py], pc), land[a].at[me], *sems))
            slot = land[a].at[4 * px + 2 * py + pc]
            arrivals.append(_remote(slot, slot, *sems))
    return sends, arrivals


def reduce_grads_start(grads, *, name):
    n = len(grads)

    def body(*refs):
        g, land = refs[:n], refs[n:2 * n]
        send_sems, recv_sems, token = refs[2 * n], refs[2 * n + 1], refs[4 * n + 2]
        x, y, c = _place()
        for cp in _reduce_async_copies(g, land, send_sems, recv_sems, x, y, c)[0]:
            cp.start()
        token[...] = jnp.zeros_like(token)

    zones = [pltpu.with_memory_space_constraint(lax.empty((N_DEV, g.shape[1] // 2, g.shape[2]), g.dtype), pltpu.HBM)
             for g in grads]
    srcs = [pltpu.with_memory_space_constraint(g, pltpu.HBM) for g in grads]
    out = pl.pallas_call(
        body, name=name,
        out_shape=[pltpu.SemaphoreType.DMA((7 * n,)), pltpu.SemaphoreType.DMA((7 * n,))]
        + [pltpu.HBM(g.shape, g.dtype) for g in grads] + [pltpu.HBM(z.shape, z.dtype) for z in zones]
        + [jax.ShapeDtypeStruct((8, LANES), f32)],
        in_specs=[HBM_SPEC] * (2 * n),
        out_specs=[SEM_SPEC, SEM_SPEC] + [HBM_SPEC] * (2 * n) + [pl.BlockSpec(memory_space=pltpu.VMEM)],
        input_output_aliases={i: 2 + i for i in range(2 * n)},
        compiler_params=pltpu.CompilerParams(has_side_effects=SPLIT_EFFECT),
    )(*srcs, *zones)
    return out[0], out[1], out[2:2 + n], out[2 + n:2 + 2 * n], out[-1]


def reduce_grads_wait(send_sems, recv_sems, grads, zones, after, *, name):
    n = len(grads)

    def body(*refs):
        g, land = refs[:n], refs[n:2 * n]
        x, y, c = _place()
        sends, arrivals = _reduce_async_copies(g, land, refs[2 * n], refs[2 * n + 1], x, y, c)
        for cp in sends:
            cp.wait_send()
        for cp in arrivals:
            cp.wait_recv()

    hbm = [pltpu.HBM(a.shape, a.dtype) for a in list(grads) + list(zones)]
    out = pl.pallas_call(
        body, name=name, out_shape=hbm,
        in_specs=[HBM_SPEC] * (2 * n) + [SEM_SPEC, SEM_SPEC, pl.BlockSpec(memory_space=pl.ANY)],
        out_specs=[HBM_SPEC] * (2 * n),
        input_output_aliases={i: i for i in range(2 * n)},
        compiler_params=pltpu.CompilerParams(has_side_effects=SPLIT_EFFECT),
    )(*grads, *zones, send_sems, recv_sems, after)
    return out[:n], out[n:]


def sum_partials(g, land, where, *, name):
    _, half, cols = land.shape
    tr = GRAD_ROWS if half % GRAD_ROWS == 0 else half
    nb = half // tr

    def body(where_ref, g_ref, *rest):
        o_ref = rest[-1]
        acc = g_ref[0].astype(f32)
        for l_ref in rest[:-1]:
            acc = acc + l_ref[0].astype(f32)
        o_ref[...] = acc

    blk = (1, tr, cols)
    slot = lambda j: pl.BlockSpec(blk, lambda i, where_ref: (where_ref[2 + j], i, 0))
    return pl.pallas_call(
        body, name=name,
        grid_spec=pltpu.PrefetchScalarGridSpec(
            num_scalar_prefetch=1, grid=(nb,),
            in_specs=[pl.BlockSpec(blk, lambda i, where_ref: (where_ref[0], where_ref[1] * nb + i, 0))]
            + [slot(j) for j in range(N_DEV - 1)],
            out_specs=pl.BlockSpec((tr, cols), lambda i, where_ref: (i, 0))),
        out_shape=jax.ShapeDtypeStruct((half, cols), f32),
        compiler_params=_cparams("parallel"),
    )(where, g, *([land] * (N_DEV - 1)))


def swap_reduced_halves(mine, *, name):
    n = len(mine)

    def body(*refs):
        r, out, send_sems, recv_sems = refs[:n], refs[n:2 * n], refs[2 * n], refs[2 * n + 1]
        x, y, c = _place()
        copies = [_remote(r[a], out[a], send_sems.at[a], recv_sems.at[a], (x, y, 1 - c)) for a in range(n)]
        for cp in copies:
            cp.start()
        for cp in copies:
            cp.wait()

    return pl.pallas_call(
        body, name=name, in_specs=[HBM_SPEC] * n, out_specs=[HBM_SPEC] * n,
        out_shape=[jax.ShapeDtypeStruct(r.shape, r.dtype) for r in mine],
        scratch_shapes=[pltpu.SemaphoreType.DMA((n,)), pltpu.SemaphoreType.DMA((n,))],
    )(*mine)


def adamw_halves(w, mine, theirs, m, v, core, *, name):
    R, C = w.shape
    tr = min(GRAD_ROWS, R // 2)
    half_nb = R // 2 // tr

    def body(c_ref, w_ref, a_ref, b_ref, m_ref, v_ref, g_ref, d_ref, nm_ref, nv_ref):
        low = pl.program_id(0) < half_nb
        gv = jnp.where(low == (c_ref[0] == 0), a_ref[...], b_ref[...])
        nm = ADAM_B1 * m_ref[...] + (1.0 - ADAM_B1) * gv
        nv = ADAM_B2 * v_ref[...] + (1.0 - ADAM_B2) * jnp.square(gv)
        m_hat = nm / (1.0 - ADAM_B1 ** ADAM_STEP)
        v_hat = nv / (1.0 - ADAM_B2 ** ADAM_STEP)
        g_ref[...] = gv
        d_ref[...] = -ADAM_LR * (m_hat / (jnp.sqrt(v_hat) + ADAM_EPS) + ADAM_WD * w_ref[...])
        nm_ref[...] = nm
        nv_ref[...] = nv

    full = pl.BlockSpec((tr, C), lambda i, c_ref: (i, 0))
    part = pl.BlockSpec((tr, C), lambda i, c_ref: (i % half_nb, 0))
    out = jax.ShapeDtypeStruct((R, C), f32)
    return pl.pallas_call(
        body, name=name,
        grid_spec=pltpu.PrefetchScalarGridSpec(
            num_scalar_prefetch=1, grid=(2 * half_nb,), in_specs=[full, part, part, full, full], out_specs=[full] * 4),
        out_shape=[out] * 4, compiler_params=_cparams("parallel"),
    )(core, w, mine, theirs, m, v)


N_DEV = 8


def all_reduce_small(v):
    def body(src_ref, out_ref, land_ref, send_sems, recv_sems):
        x, y, c = _place()
        me = 4 * x + 2 * y + c
        copies = []
        for r in range(1, N_DEV):
            peer = ((1 - x) if r & 4 else x, (1 - y) if r & 2 else y, (1 - c) if r & 1 else c)
            copies.append(_remote(src_ref, land_ref.at[r], send_sems.at[r - 1], recv_sems.at[r - 1], peer))
        for cp in copies:
            cp.start()
        land_ref[0] = src_ref[...]
        for cp in copies:
            cp.wait()
        acc = land_ref[me]
        for d in range(1, N_DEV):
            acc = acc + land_ref[jnp.bitwise_xor(me, d)]
        out_ref[...] = acc

    vm = pl.BlockSpec(memory_space=pltpu.VMEM)
    return pl.pallas_call(
        body, name="all_reduce_small", in_specs=[vm], out_specs=vm,
        out_shape=jax.ShapeDtypeStruct(v.shape, v.dtype),
        scratch_shapes=[pltpu.VMEM((N_DEV,) + v.shape, v.dtype),
                        pltpu.SemaphoreType.DMA((N_DEV - 1,)), pltpu.SemaphoreType.DMA((N_DEV - 1,))],
    )(v)


def adamw(w, g, m, v, *, name, tr=None, tc=None):
    R, C = w.shape
    if tc is None:
        tr, tc = min(tr, R), C
        blk = pl.BlockSpec((tr, C), lambda i: (i, 0))
    else:
        tr = R
        blk = pl.BlockSpec((R, tc), lambda i: (0, i))

    def body(w_ref, g_ref, m_ref, v_ref, d_ref, nm_ref, nv_ref):
        gv = g_ref[...]
        nm = ADAM_B1 * m_ref[...] + (1.0 - ADAM_B1) * gv
        nv = ADAM_B2 * v_ref[...] + (1.0 - ADAM_B2) * jnp.square(gv)
        m_hat = nm / (1.0 - ADAM_B1 ** ADAM_STEP)
        v_hat = nv / (1.0 - ADAM_B2 ** ADAM_STEP)
        d_ref[...] = -ADAM_LR * (m_hat / (jnp.sqrt(v_hat) + ADAM_EPS) + ADAM_WD * w_ref[...])
        nm_ref[...] = nm
        nv_ref[...] = nv

    out = jax.ShapeDtypeStruct((R, C), f32)
    return pl.pallas_call(
        body, name=name, grid=((R // tr) * (C // tc),), in_specs=[blk] * 4, out_specs=[blk] * 3, out_shape=[out] * 3,
        compiler_params=_cparams("parallel"),
    )(w, g, m, v)


BIG_SHARDS = (("w_in", (1024, 900), True), ("w_out", (256, 1024), False), ("w_cq", (256, 512), False),
              ("w_ckv", (256, 1024), False), ("w_co", (512, 256), True), ("w_mlp1", (1024, 1024), True),
              ("w_mlp2", (1024, 1024), False))
CONV_SHARD = (CONV_WIDTH, 3 * GDN_WIDTH // N_CHIPS)
SMALL_DIMS = (("norm_mix_g", 1024), ("fox_qnorm_g", 64), ("fox_knorm_g", 64), ("fox_f_bias", 8), ("fox_onorm_g", 64),
              ("gdn_A_log", 4), ("gdn_dt_bias", 4), ("gdn_onorm_g", 128), ("norm_xattn_g", 1024), ("mem_norm_g", 1024),
              ("xattn_qnorm_g", 128), ("xattn_knorm_g", 128), ("norm_mlp_g", 1024))
WEIGHT_ORDER = ("norm_mix_g", "w_in", "fox_qnorm_g", "fox_knorm_g", "fox_f_bias", "fox_onorm_g", "gdn_conv_w", "gdn_A_log",
                "gdn_dt_bias", "gdn_onorm_g", "w_out", "norm_xattn_g", "mem_norm_g", "w_cq", "w_ckv", "xattn_qnorm_g",
                "xattn_knorm_g", "w_co", "norm_mlp_g", "w_mlp1", "w_mlp2")


def _pack_rows(pieces, rows, lead=()):
    cat = jnp.concatenate([p.reshape(lead + (-1,)) for p in pieces], axis=-1)
    cat = jnp.pad(cat, [(0, 0)] * len(lead) + [(0, rows * LANES - cat.shape[-1])])
    return cat.reshape(lead + (rows, LANES))


def _unpack_rows(buf, sizes, lead=()):
    flat = buf.reshape(lead + (-1,))
    out, off = [], 0
    for n in sizes:
        out.append(flat[..., off:off + n])
        off += n
    return out


def _conv_to_wire(conv):
    return lax.bitcast_convert_type(conv, bf16)


def _conv_from_wire(wire):
    return lax.bitcast_convert_type(wire, f32)


SMALL_ROWS = 96
SMALL_ADAM_ROWS = 56


def kernel(x, mem, norm_mix_g, w_in, fox_qnorm_g, fox_knorm_g, fox_f_bias, fox_onorm_g, gdn_conv_w, gdn_A_log, gdn_dt_bias, gdn_onorm_g, w_out, norm_xattn_g, mem_norm_g, w_cq, w_ckv, xattn_qnorm_g, xattn_knorm_g, w_co, norm_mlp_g, w_mlp1, w_mlp2, loss_target, m_norm_mix_g, m_w_in, m_fox_qnorm_g, m_fox_knorm_g, m_fox_f_bias, m_fox_onorm_g, m_gdn_conv_w, m_gdn_A_log, m_gdn_dt_bias, m_gdn_onorm_g, m_w_out, m_norm_xattn_g, m_mem_norm_g, m_w_cq, m_w_ckv, m_xattn_qnorm_g, m_xattn_knorm_g, m_w_co, m_norm_mlp_g, m_w_mlp1, m_w_mlp2, v_norm_mix_g, v_w_in, v_fox_qnorm_g, v_fox_knorm_g, v_fox_f_bias, v_fox_onorm_g, v_gdn_conv_w, v_gdn_A_log, v_gdn_dt_bias, v_gdn_onorm_g, v_w_out, v_norm_xattn_g, v_mem_norm_g, v_w_cq, v_w_ckv, v_xattn_qnorm_g, v_xattn_knorm_g, v_w_co, v_norm_mlp_g, v_w_mlp1, v_w_mlp2):
    wts = dict(norm_mix_g=norm_mix_g, w_in=w_in, fox_qnorm_g=fox_qnorm_g, fox_knorm_g=fox_knorm_g, fox_f_bias=fox_f_bias,
               fox_onorm_g=fox_onorm_g, gdn_conv_w=gdn_conv_w, gdn_A_log=gdn_A_log, gdn_dt_bias=gdn_dt_bias,
               gdn_onorm_g=gdn_onorm_g, w_out=w_out, norm_xattn_g=norm_xattn_g, mem_norm_g=mem_norm_g, w_cq=w_cq, w_ckv=w_ckv,
               xattn_qnorm_g=xattn_qnorm_g, xattn_knorm_g=xattn_knorm_g, w_co=w_co, norm_mlp_g=norm_mlp_g, w_mlp1=w_mlp1,
               w_mlp2=w_mlp2)
    mom = dict(norm_mix_g=m_norm_mix_g, w_in=m_w_in, fox_qnorm_g=m_fox_qnorm_g, fox_knorm_g=m_fox_knorm_g,
               fox_f_bias=m_fox_f_bias, fox_onorm_g=m_fox_onorm_g, gdn_conv_w=m_gdn_conv_w, gdn_A_log=m_gdn_A_log,
               gdn_dt_bias=m_gdn_dt_bias, gdn_onorm_g=m_gdn_onorm_g, w_out=m_w_out, norm_xattn_g=m_norm_xattn_g,
               mem_norm_g=m_mem_norm_g, w_cq=m_w_cq, w_ckv=m_w_ckv, xattn_qnorm_g=m_xattn_qnorm_g,
               xattn_knorm_g=m_xattn_knorm_g, w_co=m_w_co, norm_mlp_g=m_norm_mlp_g, w_mlp1=m_w_mlp1, w_mlp2=m_w_mlp2)
    var = dict(norm_mix_g=v_norm_mix_g, w_in=v_w_in, fox_qnorm_g=v_fox_qnorm_g, fox_knorm_g=v_fox_knorm_g,
               fox_f_bias=v_fox_f_bias, fox_onorm_g=v_fox_onorm_g, gdn_conv_w=v_gdn_conv_w, gdn_A_log=v_gdn_A_log,
               gdn_dt_bias=v_gdn_dt_bias, gdn_onorm_g=v_gdn_onorm_g, w_out=v_w_out, norm_xattn_g=v_norm_xattn_g,
               mem_norm_g=v_mem_norm_g, w_cq=v_w_cq, w_ckv=v_w_ckv, xattn_qnorm_g=v_xattn_qnorm_g,
               xattn_knorm_g=v_xattn_knorm_g, w_co=v_w_co, norm_mlp_g=v_norm_mlp_g, w_mlp1=v_w_mlp1, w_mlp2=v_w_mlp2)
    B, S, D = x.shape
    T = B * S
    big_names = [n for n, _, _ in BIG_SHARDS]
    chip = 2 * lax.axis_index("x") + lax.axis_index("y")
    core = lax.axis_index("c").astype(jnp.int32).reshape(1)

    shards = {n: wts[n][0].astype(MXU_DTYPE) for n in big_names[1:]}
    in_t = lambda p: jnp.swapaxes(p[0], 0, 1)
    shards["w_in"] = jnp.pad(in_t(w_in).astype(MXU_DTYPE), ((0, IN_SHARD_PAD - IN_SHARD), (0, 0)))
    w_in_all, conv_all = gather_weights([shards["w_in"]], gdn_conv_w[0])
    late = big_names[1:]
    send_sems, recv_sems, late_src, late_zones, token = gather_weights_start([shards[n] for n in late], conv_all)
    own = lambda g, s: lax.dynamic_update_slice(g, s[None], (chip,) + (0,) * s.ndim)
    full = {"w_in": own(w_in_all, shards["w_in"])}
    conv_full = own(conv_all, gdn_conv_w[0]).transpose(1, 0, 2).reshape(CONV_WIDTH, 3 * GDN_WIDTH)
    rows = lambda g: g.reshape(N_CHIPS * g.shape[1], g.shape[2])

    def late_weights(after):
        zones = gather_weights_wait(send_sems, recv_sems, late_src, late_zones, after)
        got = {n: own(z, shards[n]) for n, z in zip(late, zones)}
        return dict(w_out=rows(got["w_out"]), w_cq=rows(got["w_cq"]), w_ckv=rows(got["w_ckv"]),
                    w_co=got["w_co"].transpose(1, 0, 2).reshape(XATTN_WIDTH, D_MODEL),
                    w_mlp1=got["w_mlp1"], w_mlp2=rows(got["w_mlp2"]))

    in_flight = []

    def grads_ready(ready):
        names = list(ready)
        *started, tok = reduce_grads_start([ready[n] for n in names], name="reduce_grads_start_%d" % len(in_flight))
        in_flight.append((names, *started))
        return tok

    w_in_t = full["w_in"][:, :IN_SHARD].reshape(IN_DIM, D_MODEL)
    w = dict(wa_t=align_w_in_t(w_in_t), conv_w=conv_full, late=late_weights, grads_ready=grads_ready)
    sp = {n: wts[n] for n, _ in SMALL_DIMS}
    sp["norm_mix_g"] = sp["norm_mix_g"] + token[0, 0]

    loss_part, grad_x, g_big, g_small = local_step(x.reshape(T, D), mem.reshape(-1, D), loss_target.reshape(T, D), w, sp, B=B)

    small_pieces = [g_small[n] for n, _ in SMALL_DIMS] + [g_small["gdn_conv_w"], loss_part]
    small_sizes = [d for _, d in SMALL_DIMS] + [CONV_WIDTH * 3 * GDN_WIDTH, LANES]
    red_small = _unpack_rows(all_reduce_small(_pack_rows(small_pieces, SMALL_ROWS)), small_sizes)
    grads = {n: p.reshape(1, d) for (n, d), p in zip(SMALL_DIMS, red_small)}
    conv_grad = lax.dynamic_slice(red_small[-2].reshape(CONV_WIDTH, 3 * GDN_WIDTH), (0, chip * CONV_SHARD[1]), CONV_SHARD)
    grads["gdn_conv_w"] = conv_grad.reshape((1,) + CONV_SHARD)
    loss = red_small[-1][0]

    parts, zones = {}, {}

    def wait_group(k, after):
        names, send_sems, recv_sems, thru, land = in_flight[k]
        thru, land = reduce_grads_wait(send_sems, recv_sems, thru, land, after, name="reduce_grads_wait_%d" % k)
        parts.update(zip(names, thru))
        zones.update(zip(names, land))

    wait_group(0, grad_x)
    wait_group(1, grad_x)
    dev = 2 * chip + core[0]
    where = jnp.stack([chip, core[0]] + [dev ^ r for r in range(1, N_DEV)]).astype(jnp.int32)
    mine = [sum_partials(parts[n], zones[n], where, name="sum_partials_" + n) for n in late]
    theirs = swap_reduced_halves(mine, name="swap_reduced_halves")

    delta, new_m, new_v = {}, {}, {}
    for n, a, b in zip(late, mine, theirs):
        g, d, nm, nv = adamw_halves(wts[n][0], a, b, mom[n][0], var[n][0], core, name="adamw_" + n)
        grads[n], delta[n], new_m[n], new_v[n] = g[None], d[None], nm[None], nv[None]
    wait_group(2, new_v[late[-1]])
    mine_in = sum_partials(parts["w_in"], zones["w_in"], where, name="sum_partials_w_in")
    (theirs_in,) = swap_reduced_halves([mine_in], name="swap_reduced_halves_w_in")
    south = core[0] == 0
    g_in_t = jnp.concatenate([jnp.where(south, mine_in, theirs_in), jnp.where(south, theirs_in, mine_in)])[:IN_SHARD]
    back = lambda t: jnp.swapaxes(t, 0, 1)[None]
    d, nm, nv = adamw(in_t(w_in), g_in_t, in_t(m_w_in), in_t(v_w_in), name="adamw_w_in", tc=256)
    grads["w_in"], delta["w_in"], new_m["w_in"], new_v["w_in"] = back(g_in_t), back(d), back(nm), back(nv)
    small_names = [n for n, _ in SMALL_DIMS] + ["gdn_conv_w"]
    small_sz = [d for _, d in SMALL_DIMS] + [CONV_SHARD[0] * CONV_SHARD[1]]
    packed4 = [_pack_rows([src[n] for n in small_names], SMALL_ADAM_ROWS) for src in (wts, grads, mom, var)]
    outs = adamw(*packed4, name="adamw_small", tr=SMALL_ADAM_ROWS)
    for dst, buf in zip((delta, new_m, new_v), outs):
        for n, p in zip(small_names, _unpack_rows(buf, small_sz)):
            dst[n] = p.reshape(wts[n].shape)

    return (loss, grad_x.reshape(B, S, D), *[grads[n] for n in WEIGHT_ORDER], *[delta[n] for n in WEIGHT_ORDER],
            *[new_m[n] for n in WEIGHT_ORDER], *[new_v[n] for n in WEIGHT_ORDER])
```

```python
import functools

import jax
import jax.numpy as jnp
import numpy as np
from jax import lax
from jax.experimental import pallas as pl
from jax.experimental.pallas import tpu as pltpu

f32 = jnp.float32
bf16 = jnp.bfloat16
MXU_DTYPE = jnp.bfloat16
WIRE_DTYPE = jnp.bfloat16
INV_PRECISION = lax.Precision.HIGH

D_MODEL = 1024
FOX_HEADS = 8
FOX_HEAD_DIM = 64
FOX_WIDTH = 512
GDN_HEADS = 4
GDN_HEAD_DIM = 128
GDN_WIDTH = 512
CONV_WIDTH = 4
GDN_CHUNK = 64
XATTN_HEADS = 4
XATTN_HEAD_DIM = 128
XATTN_WIDTH = 512
D_FF = 4096
IN_DIM = 3600
EPS = 1e-6
NEG_INF = -1e30
LANES = 128
ADAM_LR = 0.001
ADAM_B1 = 0.9
ADAM_B2 = 0.999
ADAM_EPS = 1e-08
ADAM_WD = 0.01
ADAM_STEP = 10
VMEM_LIMIT = 48 * 1024 * 1024

COL_FOX = 0
COL_GDN = 1536
COL_Z = 3072
COL_SMALL = 3584
IN_ALIGNED = 3840
IN_TILE = 768
SM_F = 0
SM_B = 8
SM_A = 12


def _cparams(*sem):
    return pltpu.CompilerParams(dimension_semantics=sem, vmem_limit_bytes=VMEM_LIMIT)


def _mx(v):
    return v.astype(MXU_DTYPE)


def _dot(a, b, dims, precision=None):
    return lax.dot_general(a, b, (dims, ((), ())), preferred_element_type=f32, precision=precision)


def _dotm(a, b, dims):
    return _dot(_mx(a), _mx(b), dims)


NN = ((1,), (0,))
NT = ((1,), (1,))
TN = ((0,), (0,))


def matmul(a, b, *, name, ta=False, tb=False, b_stacked=False, out_stacked=False, residual=None, relu2_out=False,
           relu2_bwd_aux=None, out_dtype=f32, tm=1024, tn=1024, tk=1024):
    M, K = (a.shape[1], a.shape[0]) if ta else a.shape
    if b_stacked:
        b_cols = b.shape[2]
        N, tk = (b.shape[1], min(tk, b_cols)) if tb else (N_CHIPS * b_cols, tk)
        tn = tn if tb else min(tn, b_cols)
        assert K == (N_CHIPS * b_cols if tb else b.shape[1]), (name, a.shape, b.shape)
    else:
        N = b.shape[0] if tb else b.shape[1]
    if out_stacked:
        tn = min(tn, N // N_CHIPS)
    tm, tn, tk = min(tm, M), min(tn, N), min(tk, K)
    assert M % tm == 0 and N % tn == 0 and K % tk == 0, (name, M, N, K)
    nk = K // tk
    has_res = residual is not None
    has_aux = relu2_bwd_aux is not None

    def body(*refs):
        a_ref, b_ref = refs[0], refs[1]
        pos = 2
        res_ref = aux_ref = None
        if has_res:
            res_ref = refs[pos]
            pos += 1
        if has_aux:
            aux_ref = refs[pos]
            pos += 1
        o_ref = refs[pos]
        k = pl.program_id(2)
        dims = ((0,) if ta else (1,), (1,) if tb else (0,))
        part = _dot(_mx(a_ref[...]), _mx(b_ref[...]), dims)

        def finish(r):
            if has_res:
                r = r + res_ref[...]
            if has_aux:
                r = r * (2.0 * jnp.sqrt(aux_ref[...].astype(f32)))
            if relu2_out:
                o_ref[...] = jnp.square(jnp.maximum(r, 0.0)).astype(o_ref.dtype)
            else:
                o_ref[...] = r.astype(o_ref.dtype)

        if nk == 1:
            finish(part)
            return
        acc_ref = refs[pos + 1]

        @pl.when(k == 0)
        def _():
            acc_ref[...] = part

        @pl.when((k > 0) & (k < nk - 1))
        def _():
            acc_ref[...] += part

        @pl.when(k == nk - 1)
        def _():
            finish(acc_ref[...] + part)

    a_spec = pl.BlockSpec((tk, tm), lambda i, j, k: (k, i)) if ta else pl.BlockSpec((tm, tk), lambda i, j, k: (i, k))
    if b_stacked and tb:
        per = b_cols // tk
        b_spec = pl.BlockSpec((None, tn, tk), lambda i, j, k: (k // per, j, k % per))
    elif b_stacked:
        per = b_cols // tn
        b_spec = pl.BlockSpec((None, tk, tn), lambda i, j, k: (j // per, k, j % per))
    else:
        b_spec = pl.BlockSpec((tn, tk), lambda i, j, k: (j, k)) if tb else pl.BlockSpec((tk, tn), lambda i, j, k: (k, j))
    if out_stacked:
        assert not (has_res or has_aux or relu2_out), name
        per_o = N // N_CHIPS // tn
        o_spec = pl.BlockSpec((None, tm, tn), lambda i, j, k: (j // per_o, i, j % per_o))
        out_full = (N_CHIPS, M, N // N_CHIPS)
    else:
        o_spec = pl.BlockSpec((tm, tn), lambda i, j, k: (i, j))
        out_full = (M, N)
    in_specs, args = [a_spec, b_spec], [a, b]
    if has_res:
        in_specs.append(o_spec)
        args.append(residual)
    if has_aux:
        in_specs.append(o_spec)
        args.append(relu2_bwd_aux)
    out_shape = [jax.ShapeDtypeStruct(out_full, out_dtype)]
    out_specs = [o_spec]
    res = pl.pallas_call(
        body, name=name, grid=(M // tm, N // tn, nk), in_specs=in_specs, out_specs=out_specs, out_shape=out_shape,
        scratch_shapes=[pltpu.VMEM((tm, tn), f32)] if nk > 1 else [],
        compiler_params=_cparams("parallel", "parallel", "arbitrary"),
    )(*args)
    return res[0]


def matmul_rows(a, b, extras, *, name, mode, tb=False, b_stacked=False, tm=1024, tk=1024):
    M, K = a.shape
    N = D_MODEL
    if b_stacked:
        assert tb, name
        tk = min(tk, b.shape[2])
        per = b.shape[2] // tk
        b_spec = pl.BlockSpec((None, N, tk), lambda i, k: (k // per, 0, k % per))
    elif tb:
        tk = min(tk, K)
        b_spec = pl.BlockSpec((N, tk), lambda i, k: (0, k))
    else:
        tk = min(tk, K)
        b_spec = pl.BlockSpec((tk, N), lambda i, k: (k, 0))
    tm = min(tm, M)
    assert M % tm == 0 and K % tk == 0, (name, M, K)
    nk = K // tk
    extras = [e for e in extras if e is not None]
    n_ex = len(extras)

    def body(*refs):
        a_ref, b_ref = refs[0], refs[1]
        ex = refs[2:2 + n_ex]
        o_ref = refs[2 + n_ex]
        n_out = 3 if mode == "loss" else 2
        s_ref = refs[1 + n_ex + n_out]
        i, k = pl.program_id(0), pl.program_id(1)
        part = _dot(_mx(a_ref[...]), _mx(b_ref[...]), ((1,), (1,) if tb else (0,)))

        def finish(y):
            if mode == "rms_fwd":
                y = y + ex[0][...]
                o_ref[...] = y
                s_ref[...] = (y * lax.rsqrt(jnp.mean(y * y, axis=-1, keepdims=True) + EPS) * ex[1][...]).astype(MXU_DTYPE)
                return

            @pl.when(i == 0)
            def _():
                s_ref[...] = jnp.zeros_like(s_ref)

            if mode == "rms_bwd":
                xv, gv = ex[0][...], ex[1][...]
                rstd = lax.rsqrt(jnp.mean(xv * xv, axis=-1, keepdims=True) + EPS)
                xhat = xv * rstd
                gd = y * gv
                dx = rstd * (gd - xhat * jnp.mean(gd * xhat, axis=-1, keepdims=True))
                o_ref[...] = dx + ex[2][...] if n_ex == 3 else dx
                s_ref[...] += jnp.sum(y * xhat, axis=0, keepdims=True)
            else:
                e = y + ex[0][...] - ex[1][...]
                o_ref[...] = e * (1.0 / N)
                refs[3 + n_ex][...] = (e * (1.0 / N)).astype(MXU_DTYPE)
                tot = 0.5 * jnp.sum(jnp.mean(e * e, axis=-1, keepdims=True), axis=0, keepdims=True)
                s_ref[...] += jnp.broadcast_to(tot, s_ref.shape)

        if nk == 1:
            finish(part)
            return
        acc_ref = refs[2 + n_ex + n_out]

        @pl.when(k == 0)
        def _():
            acc_ref[...] = part

        @pl.when((k > 0) & (k < nk - 1))
        def _():
            acc_ref[...] += part

        @pl.when(k == nk - 1)
        def _():
            finish(acc_ref[...] + part)

    row = pl.BlockSpec((tm, N), lambda i, k: (i, 0))
    vec = pl.BlockSpec((1, N), lambda i, k: (0, 0))
    if mode == "rms_bwd":
        ex_specs = [row, vec] + ([row] if n_ex == 3 else [])
        s_shape, s_spec = jax.ShapeDtypeStruct((1, N), f32), vec
    elif mode == "rms_fwd":
        ex_specs = [row, vec]
        s_shape, s_spec = jax.ShapeDtypeStruct((M, N), MXU_DTYPE), row
    else:
        ex_specs = [row, row]
        s_shape, s_spec = jax.ShapeDtypeStruct((1, LANES), f32), pl.BlockSpec((1, LANES), lambda i, k: (0, 0))
    return pl.pallas_call(
        body, name=name, grid=(M // tm, nk),
        in_specs=[pl.BlockSpec((tm, tk), lambda i, k: (i, k)), b_spec] + ex_specs,
        out_specs=[row] * (2 if mode == "loss" else 1) + [s_spec],
        out_shape=[jax.ShapeDtypeStruct((M, N), f32)] + ([jax.ShapeDtypeStruct((M, N), MXU_DTYPE)] if mode == "loss" else [])
        + [s_shape],
        scratch_shapes=[pltpu.VMEM((tm, N), f32)] if nk > 1 else [],
        compiler_params=_cparams("arbitrary", "arbitrary"),
    )(a, b, *extras)


def rms_fwd(x, g, *, name, tr=1024):
    R, D = x.shape
    tr = min(tr, R)

    def body(x_ref, g_ref, o_ref):
        xv = x_ref[...]
        y = xv * lax.rsqrt(jnp.mean(xv * xv, axis=-1, keepdims=True) + EPS)
        o_ref[...] = (y * g_ref[...]).astype(o_ref.dtype)

    return pl.pallas_call(
        body, name=name, grid=(R // tr,),
        in_specs=[pl.BlockSpec((tr, D), lambda i: (i, 0)), pl.BlockSpec((1, D), lambda i: (0, 0))],
        out_specs=pl.BlockSpec((tr, D), lambda i: (i, 0)),
        out_shape=jax.ShapeDtypeStruct((R, D), MXU_DTYPE),
        compiler_params=_cparams("parallel"),
    )(x, g)


def rms_bwd(x, g, dh, residual, *, name, tr=512):
    R, D = x.shape
    tr = min(tr, R)
    has_res = residual is not None

    def body(*refs):
        if has_res:
            x_ref, g_ref, dh_ref, res_ref, dx_ref, dg_ref = refs
        else:
            x_ref, g_ref, dh_ref, dx_ref, dg_ref = refs
        xv = x_ref[...]
        rstd = lax.rsqrt(jnp.mean(xv * xv, axis=-1, keepdims=True) + EPS)
        xhat = xv * rstd
        dh = dh_ref[...].astype(f32)
        gd = dh * g_ref[...]
        dx = rstd * (gd - xhat * jnp.mean(gd * xhat, axis=-1, keepdims=True))
        if has_res:
            dx = dx + res_ref[...]
        dx_ref[...] = dx

        @pl.when(pl.program_id(0) == 0)
        def _():
            dg_ref[...] = jnp.zeros_like(dg_ref)

        dg_ref[...] += jnp.sum(dh * xhat, axis=0, keepdims=True)

    row = pl.BlockSpec((tr, D), lambda i: (i, 0))
    vec = pl.BlockSpec((1, D), lambda i: (0, 0))
    in_specs = [row, vec, row] + ([row] if has_res else [])
    args = [x, g, dh] + ([residual] if has_res else [])
    return pl.pallas_call(
        body, name=name, grid=(R // tr,), in_specs=in_specs, out_specs=[row, vec],
        out_shape=[jax.ShapeDtypeStruct((R, D), f32), jax.ShapeDtypeStruct((1, D), f32)],
        compiler_params=_cparams("arbitrary"),
    )(*args)


def loss_head(y, target, *, tr=512):
    R, D = y.shape
    tr = min(tr, R)

    def body(y_ref, t_ref, dy_ref, loss_ref):
        e = y_ref[...] - t_ref[...]
        dy_ref[...] = e * (1.0 / D)

        @pl.when(pl.program_id(0) == 0)
        def _():
            loss_ref[...] = jnp.zeros_like(loss_ref)

        part = 0.5 * jnp.sum(jnp.mean(e * e, axis=-1, keepdims=True), axis=0, keepdims=True)
        loss_ref[...] += jnp.broadcast_to(part, loss_ref.shape)

    row = pl.BlockSpec((tr, D), lambda i: (i, 0))
    return pl.pallas_call(
        body, name="loss_head", grid=(R // tr,), in_specs=[row, row],
        out_specs=[row, pl.BlockSpec((1, LANES), lambda i: (0, 0))],
        out_shape=[jax.ShapeDtypeStruct((R, D), f32), jax.ShapeDtypeStruct((1, LANES), f32)],
        compiler_params=_cparams("arbitrary"),
    )(y, target)


def _head_rms(v, g):
    r = lax.rsqrt(jnp.mean(v * v, axis=-1, keepdims=True) + EPS)
    return v * r * g, r


def _head_rms_bwd(v, r, g, dn):
    vhat = v * r
    gd = dn * g
    dv = r * (gd - vhat * jnp.mean(gd * vhat, axis=-1, keepdims=True))
    return dv, jnp.sum(dn * vhat, axis=0, keepdims=True)


def _softmax_rows(s):
    m = jnp.max(s, axis=-1, keepdims=True)
    e = jnp.exp(s - m)
    return e / jnp.sum(e, axis=-1, keepdims=True)


def xattn_fwd(cq, ckv, gq, gk, *, B, tq=1024):
    T = cq.shape[0]
    S = T // B
    M = ckv.shape[0] // B
    tq = min(tq, S)
    nq = S // tq
    hd, W = XATTN_HEAD_DIM, XATTN_WIDTH
    scale = hd ** -0.5

    def body(q_ref, k_ref, v_ref, gq_ref, gk_ref, o_ref):
        for h in range(XATTN_HEADS):
            sl = slice(h * hd, (h + 1) * hd)
            qn, _ = _head_rms(q_ref[:, sl], gq_ref[...])
            kn, _ = _head_rms(k_ref[:, sl], gk_ref[...])
            p = _softmax_rows(_dot(_mx(qn), _mx(kn), NT) * scale)
            o_ref[:, sl] = _dot(_mx(p), _mx(v_ref[:, sl]), NN).astype(o_ref.dtype)

    vec = pl.BlockSpec((1, hd), lambda b, i: (0, 0))
    qspec = pl.BlockSpec((tq, W), lambda b, i: (b * nq + i, 0))
    return pl.pallas_call(
        body, name="xattn_fwd", grid=(B, nq),
        in_specs=[qspec, pl.BlockSpec((M, W), lambda b, i: (b, 0)), pl.BlockSpec((M, W), lambda b, i: (b, 1)), vec, vec],
        out_specs=qspec, out_shape=jax.ShapeDtypeStruct((T, W), MXU_DTYPE),
        compiler_params=_cparams("parallel", "parallel"),
    )(cq, ckv, ckv, gq, gk)


def xattn_bwd(cq, ckv, gq, gk, dco, *, B, tq=1024):
    T = cq.shape[0]
    S = T // B
    M = ckv.shape[0] // B
    tq = min(tq, S)
    nq = S // tq
    hd, W = XATTN_HEAD_DIM, XATTN_WIDTH
    scale = hd ** -0.5

    def body(q_ref, k_ref, v_ref, gq_ref, gk_ref, do_ref, dq_ref, dkv_ref, dgq_ref, dgk_ref, dkn_acc, dv_acc):
        b, i = pl.program_id(0), pl.program_id(1)

        @pl.when((b == 0) & (i == 0))
        def _():
            dgq_ref[...] = jnp.zeros_like(dgq_ref)
            dgk_ref[...] = jnp.zeros_like(dgk_ref)

        @pl.when(i == 0)
        def _():
            dkn_acc[...] = jnp.zeros_like(dkn_acc)
            dv_acc[...] = jnp.zeros_like(dv_acc)

        gqv, gkv = gq_ref[...], gk_ref[...]
        for h in range(XATTN_HEADS):
            sl = slice(h * hd, (h + 1) * hd)
            q, k, v = q_ref[:, sl], k_ref[:, sl], v_ref[:, sl]
            qn, rq = _head_rms(q, gqv)
            kn, _ = _head_rms(k, gkv)
            p = _softmax_rows(_dot(_mx(qn), _mx(kn), NT) * scale)
            do = do_ref[:, sl]
            dv_acc[:, sl] += _dot(_mx(p), _mx(do), TN)
            dp = _dot(_mx(do), _mx(v), NT)
            ds = p * (dp - jnp.sum(dp * p, axis=-1, keepdims=True)) * scale
            dqn = _dot(_mx(ds), _mx(kn), NN)
            dkn_acc[:, sl] += _dot(_mx(ds), _mx(qn), TN)
            dq, dgq = _head_rms_bwd(q, rq, gqv, dqn)
            dq_ref[:, sl] = dq.astype(dq_ref.dtype)
            dgq_ref[...] += dgq

        @pl.when(i == nq - 1)
        def _():
            for h in range(XATTN_HEADS):
                sl = slice(h * hd, (h + 1) * hd)
                k = k_ref[:, sl]
                rk = lax.rsqrt(jnp.mean(k * k, axis=-1, keepdims=True) + EPS)
                dk, dgk = _head_rms_bwd(k, rk, gkv, dkn_acc[:, sl])
                dkv_ref[:, sl] = dk.astype(dkv_ref.dtype)
                dkv_ref[:, slice(W + h * hd, W + (h + 1) * hd)] = dv_acc[:, sl].astype(dkv_ref.dtype)
                dgk_ref[...] += dgk

    vec = pl.BlockSpec((1, hd), lambda b, i: (0, 0))
    qspec = pl.BlockSpec((tq, W), lambda b, i: (b * nq + i, 0))
    return pl.pallas_call(
        body, name="xattn_bwd", grid=(B, nq),
        in_specs=[qspec, pl.BlockSpec((M, W), lambda b, i: (b, 0)), pl.BlockSpec((M, W), lambda b, i: (b, 1)), vec, vec, qspec],
        out_specs=[qspec, pl.BlockSpec((M, 2 * W), lambda b, i: (b, 0)), vec, vec],
        out_shape=[jax.ShapeDtypeStruct((T, W), MXU_DTYPE), jax.ShapeDtypeStruct((B * M, 2 * W), MXU_DTYPE),
                   jax.ShapeDtypeStruct((1, hd), f32), jax.ShapeDtypeStruct((1, hd), f32)],
        scratch_shapes=[pltpu.VMEM((M, W), f32), pltpu.VMEM((M, W), f32)],
        compiler_params=_cparams("arbitrary", "arbitrary"),
    )(cq, ckv, ckv, gq, gk, dco)


FOX_PAIRS = FOX_HEADS // 2


def _fox_scores(qn, kn, ccol, crow, q0, tq, S, scale):
    s = _dot(_mx(qn), _mx(kn), NT) * scale + ccol - crow
    qpos = q0 + lax.broadcasted_iota(jnp.int32, (tq, S), 0)
    kpos = lax.broadcasted_iota(jnp.int32, (tq, S), 1)
    return jnp.where(kpos <= qpos, s, NEG_INF)


def fox_fwd(P, ccol, crow, gq, gk, go, *, B, tq=256):
    T = P.shape[0]
    S = T // B
    tq = min(tq, S)
    nq = S // tq
    hd = FOX_HEAD_DIM
    scale = hd ** -0.5

    def body(q_ref, k_ref, v_ref, ccol_ref, crow_ref, gq_ref, gk_ref, go_ref, o_ref, oa_ref):
        q0 = pl.program_id(2) * tq
        for e in range(2):
            sl = slice(e * hd, (e + 1) * hd)
            qn, _ = _head_rms(q_ref[:, sl], gq_ref[:, sl])
            kn, _ = _head_rms(k_ref[:, sl], gk_ref[:, sl])
            p = _softmax_rows(_fox_scores(qn, kn, ccol_ref[0, e], crow_ref[0, e], q0, tq, S, scale))
            o = _dot(_mx(p), _mx(v_ref[:, sl]), NN)
            o_ref[:, sl] = o
            oa_ref[:, sl] = _head_rms(o, go_ref[:, sl])[0].astype(oa_ref.dtype)

    W = 2 * hd
    vec = pl.BlockSpec((1, W), lambda b, h, i: (0, 0))
    ospec = pl.BlockSpec((tq, W), lambda b, h, i: (b * nq + i, h))
    return pl.pallas_call(
        body, name="fox_fwd", grid=(B, FOX_PAIRS, nq),
        in_specs=[pl.BlockSpec((tq, W), lambda b, h, i: (b * nq + i, h)),
                  pl.BlockSpec((S, W), lambda b, h, i: (b, FOX_PAIRS + h)),
                  pl.BlockSpec((S, W), lambda b, h, i: (b, 2 * FOX_PAIRS + h)),
                  pl.BlockSpec((1, 2, tq, 1), lambda b, h, i: (b, h, i, 0)),
                  pl.BlockSpec((1, 2, 1, S), lambda b, h, i: (b, h, 0, 0)), vec, vec, vec],
        out_specs=[ospec, ospec],
        out_shape=[jax.ShapeDtypeStruct((T, FOX_WIDTH), f32), jax.ShapeDtypeStruct((T, FOX_WIDTH), MXU_DTYPE)],
        compiler_params=_cparams("parallel", "parallel", "parallel"),
    )(P, P, P, ccol, crow, gq, gk, go)


def fox_bwd(P, ccol, crow, gq, gk, go, o_raw, d_oab, *, B, tq=256):
    T = P.shape[0]
    S = T // B
    tq = min(tq, S)
    nq = S // tq
    hd = FOX_HEAD_DIM
    scale = hd ** -0.5

    def body(q_ref, k_ref, v_ref, ccol_ref, crow_ref, gq_ref, gk_ref, go_ref, o_ref, doa_ref,
             dq_ref, dk_ref, dv_ref, dccol_ref, dcrow_ref, dgq_ref, dgk_ref, dgo_ref, dkn_acc, dv_acc, dcrow_acc):
        b, h, i = pl.program_id(0), pl.program_id(1), pl.program_id(2)
        q0 = i * tq

        @pl.when((b == 0) & (h == 0) & (i == 0))
        def _():
            dgq_ref[...] = jnp.zeros_like(dgq_ref)
            dgk_ref[...] = jnp.zeros_like(dgk_ref)
            dgo_ref[...] = jnp.zeros_like(dgo_ref)

        @pl.when(i == 0)
        def _():
            dkn_acc[...] = jnp.zeros_like(dkn_acc)
            dv_acc[...] = jnp.zeros_like(dv_acc)
            dcrow_acc[...] = jnp.zeros_like(dcrow_acc)

        for e in range(2):
            sl = slice(e * hd, (e + 1) * hd)
            q, k, v = q_ref[:, sl], k_ref[:, sl], v_ref[:, sl]
            gqv, gkv, gov = gq_ref[:, sl], gk_ref[:, sl], go_ref[:, sl]
            qn, rq = _head_rms(q, gqv)
            kn, rk = _head_rms(k, gkv)
            p = _softmax_rows(_fox_scores(qn, kn, ccol_ref[0, e], crow_ref[0, e], q0, tq, S, scale))
            o = o_ref[:, sl]
            ro = lax.rsqrt(jnp.mean(o * o, axis=-1, keepdims=True) + EPS)
            do, dgo = _head_rms_bwd(o, ro, gov, doa_ref[:, sl])
            dgo_ref[:, sl] += dgo
            dv_acc[e] += _dot(_mx(p), _mx(do), TN)
            dp = _dot(_mx(do), _mx(v), NT)
            ds = p * (dp - jnp.sum(do * o, axis=-1, keepdims=True))
            dccol_ref[0, e] = jnp.sum(ds, axis=1, keepdims=True)
            dcrow_acc[e] -= jnp.sum(ds, axis=0, keepdims=True)
            dqn = _dot(_mx(ds), _mx(kn), NN) * scale
            dkn_acc[e] += _dot(_mx(ds), _mx(qn), TN) * scale
            dq, dgq = _head_rms_bwd(q, rq, gqv, dqn)
            dq_ref[:, sl] = dq.astype(dq_ref.dtype)
            dgq_ref[:, sl] += dgq

        @pl.when(i == nq - 1)
        def _():
            for e in range(2):
                sl = slice(e * hd, (e + 1) * hd)
                k = k_ref[:, sl]
                gkv = gk_ref[:, sl]
                rk = lax.rsqrt(jnp.mean(k * k, axis=-1, keepdims=True) + EPS)
                dk, dgk = _head_rms_bwd(k, rk, gkv, dkn_acc[e])
                dk_ref[:, sl] = dk.astype(dk_ref.dtype)
                dv_ref[:, sl] = dv_acc[e].astype(dv_ref.dtype)
                dgk_ref[:, sl] += dgk
                dcrow_ref[0, e] = dcrow_acc[e]

    W = 2 * hd
    vec = pl.BlockSpec((1, W), lambda b, h, i: (0, 0))
    qspec = pl.BlockSpec((tq, W), lambda b, h, i: (b * nq + i, h))
    kvout = pl.BlockSpec((S, W), lambda b, h, i: (b, h))
    colspec = pl.BlockSpec((1, 2, tq, 1), lambda b, h, i: (b, h, i, 0))
    rowspec = pl.BlockSpec((1, 2, 1, S), lambda b, h, i: (b, h, 0, 0))
    return pl.pallas_call(
        body, name="fox_bwd", grid=(B, FOX_PAIRS, nq),
        in_specs=[qspec,
                  pl.BlockSpec((S, W), lambda b, h, i: (b, FOX_PAIRS + h)),
                  pl.BlockSpec((S, W), lambda b, h, i: (b, 2 * FOX_PAIRS + h)),
                  colspec, rowspec, vec, vec, vec, qspec, qspec],
        out_specs=[qspec, kvout, kvout, colspec, rowspec, vec, vec, vec],
        out_shape=[jax.ShapeDtypeStruct((T, FOX_WIDTH), MXU_DTYPE), jax.ShapeDtypeStruct((T, FOX_WIDTH), MXU_DTYPE),
                   jax.ShapeDtypeStruct((T, FOX_WIDTH), MXU_DTYPE),
                   jax.ShapeDtypeStruct((B, FOX_HEADS, S, 1), f32), jax.ShapeDtypeStruct((B, FOX_HEADS, 1, S), f32),
                   jax.ShapeDtypeStruct((1, W), f32), jax.ShapeDtypeStruct((1, W), f32), jax.ShapeDtypeStruct((1, W), f32)],
        scratch_shapes=[pltpu.VMEM((2, S, hd), f32), pltpu.VMEM((2, S, hd), f32), pltpu.VMEM((2, 1, S), f32)],
        compiler_params=_cparams("arbitrary", "arbitrary", "arbitrary"),
    )(P, P, P, ccol, crow, gq, gk, go, o_raw, d_oab)


FOX_TQ = 512
FOX_TK = FOX_TQ
GROUP_PRECISION = lax.Precision.HIGH


def _head_mean(v):
    n = v.shape[1]
    r = lax.broadcasted_iota(jnp.int32, (n, n), 0) // FOX_HEAD_DIM
    c = lax.broadcasted_iota(jnp.int32, (n, n), 1) // FOX_HEAD_DIM
    ones = (r == c).astype(bf16)
    hi = v.astype(bf16)
    lo = (v - hi.astype(f32)).astype(bf16)
    return (_dot(hi, ones, NN) + _dot(lo, ones, NN)) * (1.0 / FOX_HEAD_DIM)


def fox_prep_fwd(P, gq, gk, *, tr=1024):
    T = P.shape[0]
    tr = min(tr, T)
    scale = FOX_HEAD_DIM ** -0.5

    def body(q_ref, k_ref, v_ref, gq_ref, gk_ref, qn_ref, kn_ref, vb_ref):
        q, k = q_ref[...], k_ref[...]
        qn_ref[...] = (q * lax.rsqrt(_head_mean(q * q) + EPS) * (gq_ref[...] * scale)).astype(qn_ref.dtype)
        kn_ref[...] = (k * lax.rsqrt(_head_mean(k * k) + EPS) * gk_ref[...]).astype(kn_ref.dtype)
        vb_ref[...] = v_ref[...].astype(vb_ref.dtype)

    W = FOX_WIDTH
    col = lambda j: pl.BlockSpec((tr, W), lambda i: (i, j))
    vec = pl.BlockSpec((1, W), lambda i: (0, 0))
    out = jax.ShapeDtypeStruct((T, W), MXU_DTYPE)
    return pl.pallas_call(
        body, name="fox_prep_fwd", grid=(T // tr,), in_specs=[col(0), col(1), col(2), vec, vec],
        out_specs=[col(0)] * 3, out_shape=[out] * 3, compiler_params=_cparams("parallel"),
    )(P, P, P, gq, gk)


def fox_prep_bwd(P, gq, gk, dqn, dkn, *, tr=1024):
    T = P.shape[0]
    tr = min(tr, T)
    scale = FOX_HEAD_DIM ** -0.5

    def body(q_ref, k_ref, gq_ref, gk_ref, dqn_ref, dkn_ref, dq_ref, dk_ref, dgq_ref, dgk_ref):
        @pl.when(pl.program_id(0) == 0)
        def _():
            dgq_ref[...] = jnp.zeros_like(dgq_ref)
            dgk_ref[...] = jnp.zeros_like(dgk_ref)

        def one(x, g, dn, dx_ref, dg_ref):
            r = lax.rsqrt(_head_mean(x * x) + EPS)
            xhat = x * r
            gd = dn * g
            dx_ref[...] = (r * (gd - xhat * _head_mean(gd * xhat))).astype(dx_ref.dtype)
            return jnp.sum(dn * xhat, axis=0, keepdims=True)

        dgq_ref[...] += scale * one(q_ref[...], gq_ref[...] * scale, dqn_ref[...], dq_ref, dgq_ref)
        dgk_ref[...] += one(k_ref[...], gk_ref[...], dkn_ref[...], dk_ref, dgk_ref)

    W = FOX_WIDTH
    col = lambda j: pl.BlockSpec((tr, W), lambda i: (i, j))
    vec = pl.BlockSpec((1, W), lambda i: (0, 0))
    return pl.pallas_call(
        body, name="fox_prep_bwd", grid=(T // tr,), in_specs=[col(0), col(1), vec, vec, col(0), col(0)],
        out_specs=[col(0), col(0), vec, vec],
        out_shape=[jax.ShapeDtypeStruct((T, W), MXU_DTYPE), jax.ShapeDtypeStruct((T, W), MXU_DTYPE),
                   jax.ShapeDtypeStruct((1, W), f32), jax.ShapeDtypeStruct((1, W), f32)],
        compiler_params=_cparams("arbitrary"),
    )(P, P, gq, gk, dqn, dkn)


def _fox_tile_scores(q, k_ref, ccol_ref, cq, e, j, sl, mask_off):
    tq, tk = FOX_TQ, FOX_TK
    rows = pl.ds(pl.multiple_of(j * tk, tk), tk)
    k = k_ref[rows, sl]
    s = _dot(k, q, NT) + cq - ccol_ref[0, e, rows, :]
    if mask_off is not None:
        key = lax.broadcasted_iota(jnp.int32, (tk, tq), 0) + mask_off
        query = lax.broadcasted_iota(jnp.int32, (tk, tq), 1)
        s = jnp.where(key <= query, s, NEG_INF)
    return s, k, rows


def _fox_sweep(i, update, carry):
    nd = FOX_TQ // FOX_TK
    carry = lax.fori_loop(0, i * nd, lambda j, cr: update(cr, j, None), carry)
    for d in range(nd):
        carry = update(carry, i * nd + d, d * FOX_TK)
    return carry


def fox_core_fwd(qn, kn, vb, ccol, crow, go, *, B):
    T = qn.shape[0]
    S = T // B
    tq = FOX_TQ
    nq = S // tq
    hd = FOX_HEAD_DIM

    def body(q_ref, k_ref, v_ref, ccol_ref, crow_ref, go_ref, o_ref, oa_ref, lse_ref):
        i = pl.program_id(2)
        for e in range(2):
            sl = slice(e * hd, (e + 1) * hd)
            q = q_ref[:, sl]
            cq = crow_ref[0, e, i]

            def update(carry, j, mask_off):
                m, l, acc = carry
                s, _, rows = _fox_tile_scores(q, k_ref, ccol_ref, cq, e, j, sl, mask_off)
                m2 = jnp.maximum(m, jnp.max(s, axis=0, keepdims=True))
                a = jnp.exp(m - m2)
                p = jnp.exp(s - m2)
                return m2, a * l + jnp.sum(p, axis=0, keepdims=True), a * acc + _dot(v_ref[rows, sl], _mx(p), TN)

            carry = (jnp.full((1, tq), NEG_INF, f32), jnp.zeros((1, tq), f32), jnp.zeros((hd, tq), f32))
            m, l, acc = _fox_sweep(i, update, carry)
            o = (acc / l).T
            o_ref[:, sl] = o
            oa_ref[:, sl] = _head_rms(o, go_ref[:, sl])[0].astype(oa_ref.dtype)
            lse_ref[0, e, 0] = m + jnp.log(l)

    W = 2 * hd
    qspec = pl.BlockSpec((tq, W), lambda b, h, i: (b * nq + i, h))
    kspec = pl.BlockSpec((S, W), lambda b, h, i: (b, h))
    return pl.pallas_call(
        body, name="fox_core_fwd", grid=(B, FOX_PAIRS, nq),
        in_specs=[qspec, kspec, kspec, pl.BlockSpec((1, 2, S, 1), lambda b, h, i: (b, h, 0, 0)),
                  pl.BlockSpec((1, 2, nq, 1, tq), lambda b, h, i: (b, h, 0, 0, 0)),
                  pl.BlockSpec((1, W), lambda b, h, i: (0, 0))],
        out_specs=[qspec, qspec, pl.BlockSpec((1, 2, 1, 1, tq), lambda b, h, i: (b, h, i, 0, 0))],
        out_shape=[jax.ShapeDtypeStruct((T, FOX_WIDTH), f32), jax.ShapeDtypeStruct((T, FOX_WIDTH), MXU_DTYPE),
                   jax.ShapeDtypeStruct((B, FOX_HEADS, nq, 1, tq), f32)],
        compiler_params=_cparams("parallel", "parallel", "parallel"),
    )(qn, kn, vb, ccol, crow, go)


def fox_core_bwd(qn, kn, vb, ccol, crow, go, o_raw, lse, d_oab, *, B):
    T = qn.shape[0]
    S = T // B
    tq = FOX_TQ
    nq = S // tq
    hd = FOX_HEAD_DIM

    def body(q_ref, k_ref, v_ref, ccol_ref, crow_ref, go_ref, o_ref, lse_ref, doa_ref,
             dq_ref, dk_ref, dv_ref, dckey_ref, dcrow_ref, dgo_ref, dk_acc, dv_acc, dck_acc):
        b, h, i = pl.program_id(0), pl.program_id(1), pl.program_id(2)

        @pl.when((b == 0) & (h == 0) & (i == 0))
        def _():
            dgo_ref[...] = jnp.zeros_like(dgo_ref)

        @pl.when(i == 0)
        def _():
            dk_acc[...] = jnp.zeros_like(dk_acc)
            dv_acc[...] = jnp.zeros_like(dv_acc)
            dck_acc[...] = jnp.zeros_like(dck_acc)

        for e in range(2):
            sl = slice(e * hd, (e + 1) * hd)
            q = q_ref[:, sl]
            cq = crow_ref[0, e, i]
            lse_e = lse_ref[0, e, 0]
            o = o_ref[:, sl]
            ro = lax.rsqrt(jnp.mean(o * o, axis=-1, keepdims=True) + EPS)
            do, dgo = _head_rms_bwd(o, ro, go_ref[:, sl], doa_ref[:, sl])
            dgo_ref[:, sl] += dgo
            delta = jnp.sum((do * o).T, axis=0, keepdims=True)
            do_b = _mx(do)

            def update(carry, j, mask_off):
                dq, dcq = carry
                s, k, rows = _fox_tile_scores(q, k_ref, ccol_ref, cq, e, j, sl, mask_off)
                p = jnp.exp(s - lse_e)
                dv_acc[e, rows, :] += _dot(_mx(p), do_b, NN)
                ds = p * (_dot(v_ref[rows, sl], do_b, NT) - delta)
                dck_acc[e, rows, :] -= jnp.sum(ds, axis=1, keepdims=True)
                ds_b = _mx(ds)
                dk_acc[e, rows, :] += _dot(ds_b, q, NN)
                return dq + _dot(ds_b, k, TN), dcq + jnp.sum(ds, axis=0, keepdims=True)

            dq, dcq = _fox_sweep(i, update, (jnp.zeros((tq, hd), f32), jnp.zeros((1, tq), f32)))
            dq_ref[:, sl] = dq
            dcrow_ref[0, e, 0] = dcq

        @pl.when(i == nq - 1)
        def _():
            for e in range(2):
                sl = slice(e * hd, (e + 1) * hd)
                dk_ref[:, sl] = dk_acc[e]
                dv_ref[:, sl] = dv_acc[e].astype(dv_ref.dtype)
                dckey_ref[0, e] = jnp.transpose(jnp.broadcast_to(dck_acc[e], (S, LANES)))[0:1, :]

    W = 2 * hd
    qspec = pl.BlockSpec((tq, W), lambda b, h, i: (b * nq + i, h))
    kspec = pl.BlockSpec((S, W), lambda b, h, i: (b, h))
    colspec = pl.BlockSpec((1, 2, S, 1), lambda b, h, i: (b, h, 0, 0))
    rowspec = pl.BlockSpec((1, 2, nq, 1, tq), lambda b, h, i: (b, h, 0, 0, 0))
    tilespec = pl.BlockSpec((1, 2, 1, 1, tq), lambda b, h, i: (b, h, i, 0, 0))
    vec = pl.BlockSpec((1, W), lambda b, h, i: (0, 0))
    return pl.pallas_call(
        body, name="fox_core_bwd", grid=(B, FOX_PAIRS, nq),
        in_specs=[qspec, kspec, kspec, colspec, rowspec, vec, qspec, tilespec, qspec],
        out_specs=[qspec, kspec, kspec, pl.BlockSpec((1, 2, 1, S), lambda b, h, i: (b, h, 0, 0)), tilespec, vec],
        out_shape=[jax.ShapeDtypeStruct((T, FOX_WIDTH), f32), jax.ShapeDtypeStruct((T, FOX_WIDTH), f32),
                   jax.ShapeDtypeStruct((T, FOX_WIDTH), MXU_DTYPE),
                   jax.ShapeDtypeStruct((B, FOX_HEADS, 1, S), f32), jax.ShapeDtypeStruct((B, FOX_HEADS, nq, 1, tq), f32),
                   jax.ShapeDtypeStruct((1, W), f32)],
        scratch_shapes=[pltpu.VMEM((2, S, hd), f32), pltpu.VMEM((2, S, hd), f32), pltpu.VMEM((2, S, 1), f32)],
        compiler_params=_cparams("arbitrary", "arbitrary", "arbitrary"),
    )(qn, kn, vb, ccol, crow, go, o_raw, lse, d_oab)


def _lane_mask(lo, hi, shape):
    lane = lax.broadcasted_iota(jnp.int32, shape, 1)
    return (lane >= lo) & (lane < hi)


def _cumsum_rows(v, period, reverse=False):
    n = v.shape[0]
    pos = lax.broadcasted_iota(jnp.int32, v.shape, 0) % period
    sh = 1
    while sh < period:
        if reverse:
            v = v + jnp.where(pos + sh < period, pltpu.roll(v, n - sh, 0), 0.0)
        else:
            v = v + jnp.where(pos >= sh, pltpu.roll(v, sh, 0), 0.0)
        sh *= 2
    return v


def _gate_values(z, bias, alog):
    zb = z + bias
    ls = jax.nn.log_sigmoid(zb)
    beta = jax.nn.sigmoid(z)
    g = -jnp.exp(alog) * jax.nn.softplus(zb)
    return zb, ls, beta, g


def gates_fwd(P, bias, alog, *, B):
    T = P.shape[0]
    S = T // B

    def body(z_ref, bias_ref, alog_ref, o_ref):
        z = z_ref[...]
        _, ls, beta, g = _gate_values(z, bias_ref[...], alog_ref[...])
        c = _cumsum_rows(ls, S)
        gc = _cumsum_rows(g, GDN_CHUNK)
        o = jnp.where(_lane_mask(SM_F, SM_F + FOX_HEADS, z.shape), c, 0.0)
        o = jnp.where(_lane_mask(SM_B, SM_B + GDN_HEADS, z.shape), beta, o)
        o = jnp.where(_lane_mask(SM_A, SM_A + GDN_HEADS, z.shape), gc, o)
        o_ref[...] = o

    vec = pl.BlockSpec((1, LANES), lambda b: (0, 0))
    return pl.pallas_call(
        body, name="gates_fwd", grid=(B,),
        in_specs=[pl.BlockSpec((S, LANES), lambda b: (b, COL_SMALL // LANES)), vec, vec],
        out_specs=pl.BlockSpec((S, LANES), lambda b: (b, 0)),
        out_shape=jax.ShapeDtypeStruct((T, LANES), f32),
        compiler_params=_cparams("parallel"),
    )(P, bias, alog)


def gates_bwd(P, bias, alog, dgates, *, B):
    T = P.shape[0]
    S = T // B

    def body(z_ref, bias_ref, alog_ref, dg_ref, dz_ref, par_ref):
        z = z_ref[...]
        zb, ls, beta, g = _gate_values(z, bias_ref[...], alog_ref[...])
        d = dg_ref[...]
        dls = _cumsum_rows(d, S, reverse=True)
        dgr = _cumsum_rows(d, GDN_CHUNK, reverse=True)
        sig = jax.nn.sigmoid(zb)
        dz_f = dls * (1.0 - sig)
        dz_b = d * beta * (1.0 - beta)
        dz_a = dgr * (-jnp.exp(alog_ref[...])) * sig
        dz = jnp.where(_lane_mask(SM_F, SM_F + FOX_HEADS, z.shape), dz_f, 0.0)
        dz = jnp.where(_lane_mask(SM_B, SM_B + GDN_HEADS, z.shape), dz_b, dz)
        dz = jnp.where(_lane_mask(SM_A, SM_A + GDN_HEADS, z.shape), dz_a, dz)
        dz_ref[...] = dz.astype(dz_ref.dtype)

        @pl.when(pl.program_id(0) == 0)
        def _():
            par_ref[...] = jnp.zeros_like(par_ref)

        dalog = jnp.where(_lane_mask(SM_A, SM_A + GDN_HEADS, z.shape), dgr * g, 0.0)
        par_ref[0:1, :] += jnp.sum(dz, axis=0, keepdims=True)
        par_ref[1:2, :] += jnp.sum(dalog, axis=0, keepdims=True)

    vec = pl.BlockSpec((1, LANES), lambda b: (0, 0))
    return pl.pallas_call(
        body, name="gates_bwd", grid=(B,),
        in_specs=[pl.BlockSpec((S, LANES), lambda b: (b, COL_SMALL // LANES)), vec, vec,
                  pl.BlockSpec((S, LANES), lambda b: (b, 0))],
        out_specs=[pl.BlockSpec((S, LANES), lambda b: (b, 0)), pl.BlockSpec((8, LANES), lambda b: (0, 0))],
        out_shape=[jax.ShapeDtypeStruct((T, LANES), MXU_DTYPE), jax.ShapeDtypeStruct((8, LANES), f32)],
        compiler_params=_cparams("arbitrary"),
    )(P, bias, alog, dgates)


GDN_BLOCKS = 3 * GDN_HEADS


def _shift_rows(v, d, reverse=False):
    if d == 0:
        return v
    n = v.shape[0]
    row = lax.broadcasted_iota(jnp.int32, v.shape, 0)
    if reverse:
        return jnp.where(row + d < n, pltpu.roll(v, n - d, 0), 0.0)
    return jnp.where(row >= d, pltpu.roll(v, d, 0), 0.0)


def _conv_silu(x, w):
    pre = sum(w[j:j + 1, :] * _shift_rows(x, CONV_WIDTH - 1 - j) for j in range(CONV_WIDTH))
    return pre, pre * jax.nn.sigmoid(pre)


def gdn_prep_fwd(P, conv_w, *, B):
    T = P.shape[0]
    S = T // B

    def body(x_ref, w_ref, o_ref):
        _, y = _conv_silu(x_ref[...], w_ref[...])
        yn = y * lax.rsqrt(jnp.sum(y * y, axis=-1, keepdims=True) + EPS)
        o_ref[...] = jnp.where(pl.program_id(1) < 2 * GDN_HEADS, yn, y)

    return pl.pallas_call(
        body, name="gdn_prep_fwd", grid=(B, GDN_BLOCKS),
        in_specs=[pl.BlockSpec((S, LANES), lambda b, j: (b, COL_GDN // LANES + j)),
                  pl.BlockSpec((CONV_WIDTH, LANES), lambda b, j: (0, j))],
        out_specs=pl.BlockSpec((S, LANES), lambda b, j: (b, j)),
        out_shape=jax.ShapeDtypeStruct((T, 3 * GDN_WIDTH), f32),
        compiler_params=_cparams("parallel", "parallel"),
    )(P, conv_w)


def gdn_prep_bwd(P, conv_w, dGq, dGk, dGv, *, B):
    T = P.shape[0]
    S = T // B
    H = GDN_HEADS

    def body(x_ref, w_ref, dq_ref, dk_ref, dv_ref, dx_ref, dw_ref):
        x, w = x_ref[...], w_ref[...]
        pre, y = _conv_silu(x, w)
        jb = pl.program_id(0)
        dn = jnp.where(jb < H, dq_ref[...], jnp.where(jb < 2 * H, dk_ref[...], dv_ref[...]))
        r = lax.rsqrt(jnp.sum(y * y, axis=-1, keepdims=True) + EPS)
        n = y * r
        dy_norm = r * (dn - n * jnp.sum(dn * n, axis=-1, keepdims=True))
        dy = jnp.where(pl.program_id(0) < 2 * GDN_HEADS, dy_norm, dn)
        sg = jax.nn.sigmoid(pre)
        dpre = dy * (sg * (1.0 + pre * (1.0 - sg)))
        dx = sum(w[j:j + 1, :] * _shift_rows(dpre, CONV_WIDTH - 1 - j, reverse=True) for j in range(CONV_WIDTH))
        dx_ref[...] = dx.astype(dx_ref.dtype)

        @pl.when(pl.program_id(1) == 0)
        def _():
            dw_ref[...] = jnp.zeros_like(dw_ref)

        for j in range(CONV_WIDTH):
            dw_ref[j:j + 1, :] += jnp.sum(dpre * _shift_rows(x, CONV_WIDTH - 1 - j), axis=0, keepdims=True)

    return pl.pallas_call(
        body, name="gdn_prep_bwd", grid=(GDN_BLOCKS, B),
        in_specs=[pl.BlockSpec((S, LANES), lambda j, b: (b, COL_GDN // LANES + j)),
                  pl.BlockSpec((CONV_WIDTH, LANES), lambda j, b: (0, j))]
        + [pl.BlockSpec((S, LANES), lambda j, b, t=t: (b, jnp.clip(j - t * H, 0, H - 1))) for t in range(3)],
        out_specs=[pl.BlockSpec((S, LANES), lambda j, b: (b, j)),
                   pl.BlockSpec((CONV_WIDTH, LANES), lambda j, b: (0, j))],
        out_shape=[jax.ShapeDtypeStruct((T, 3 * GDN_WIDTH), MXU_DTYPE),
                   jax.ShapeDtypeStruct((CONV_WIDTH, 3 * GDN_WIDTH), f32)],
        compiler_params=_cparams("arbitrary", "arbitrary"),
    )(P, conv_w, dGq, dGk, dGv)


GDN_GROUP = 16
GDN_GROUP_FWD = 16
B_NN = (((2,), (1,)), ((0,), (0,)))
B_NT = (((2,), (2,)), ((0,), (0,)))
B_TN = (((1,), (1,)), ((0,), (0,)))


def _bmm(a, b, dims, precision=None):
    if precision is None:
        a, b = _mx(a), _mx(b)
    return lax.dot_general(a, b, dims, preferred_element_type=f32, precision=precision)


def _tri_inverse(A):
    C = A.shape[-1]
    row = lax.broadcasted_iota(jnp.int32, A.shape, 1)
    col = lax.broadcasted_iota(jnp.int32, A.shape, 2)
    eye = (row == col).astype(f32)
    X = jnp.where((row // 4) == (col // 4), -A, 0.0)
    X2 = _bmm(X, X, B_NN, INV_PRECISION)
    Tm = eye + X + X2 + _bmm(X, X2, B_NN, INV_PRECISION)
    b = 4
    while b < C:
        off = ((row // (2 * b)) == (col // (2 * b))) & ((row // b) != (col // b))
        Tm = Tm - _bmm(_bmm(Tm, jnp.where(off, A, 0.0), B_NN, INV_PRECISION), Tm, B_NN, INV_PRECISION)
        b *= 2
    return Tm


def _pick_lane(block, lane_idx):
    lane = lax.broadcasted_iota(jnp.int32, block.shape, 1)
    return jnp.sum(jnp.where(lane == lane_idx, block, 0.0), axis=1, keepdims=True)


def _gdn_local(q, k, v, beta, gc, Tm=None, uwm=None):
    C = GDN_CHUNK
    n = q.shape[0] // C
    q = q.reshape(n, C, -1) * (GDN_HEAD_DIM ** -0.5)
    k = k.reshape(n, C, -1)
    v = v.reshape(n, C, -1)
    beta = beta.reshape(n, C, 1)
    gc = gc.reshape(n, C, 1)
    row = lax.broadcasted_iota(jnp.int32, (n, C, C), 1)
    col = lax.broadcasted_iota(jnp.int32, (n, C, C), 2)
    gcT = jnp.swapaxes(jnp.broadcast_to(gc, (n, C, C)), 1, 2)
    D = jnp.exp(jnp.where(row >= col, gc - gcT, NEG_INF))
    kb = k * beta
    vb = v * beta
    A = jnp.where(row > col, _bmm(kb, k, B_NT) * D, 0.0)
    Gam = jnp.exp(gc)
    kg = kb * Gam
    gl = gc[:, C - 1:C, :]
    kdec = jnp.exp(gl - gc)
    loc = dict(q=q, k=k, v=v, beta=beta, gc=gc, D=D, kb=kb, vb=vb, A=A, Gam=Gam, kg=kg,
               kdec=kdec, kd=k * kdec, qg=q * Gam, gam=jnp.exp(gl), row=row, col=col)
    uwm = Tm is None if uwm is None else uwm
    Tm = _tri_inverse(A) if Tm is None else Tm.reshape(n, C, C)
    if uwm:
        loc.update(u=_bmm(Tm, vb, B_NN), w=_bmm(Tm, kg, B_NN), M=_bmm(q, k, B_NT) * D)
    loc["Tm"] = Tm
    return loc


def _gdn_store_local(loc, r0, u_s, w_s, qg_s, kd_s, M_s, gam_s, c0):
    n = loc["u"].shape[0]
    R = n * GDN_CHUNK
    u_s[pl.ds(r0, R), :] = loc["u"].reshape(R, -1)
    w_s[pl.ds(r0, R), :] = loc["w"].reshape(R, -1)
    qg_s[pl.ds(r0, R), :] = loc["qg"].reshape(R, -1)
    kd_s[pl.ds(r0, R), :] = loc["kd"].reshape(R, -1)
    M_s[pl.ds(r0, R), :] = loc["M"].reshape(R, -1)
    gam_s[pl.ds(c0, n)] = jnp.broadcast_to(loc["gam"], (n, 1, LANES))


def _gdn_specs(S):
    blk = lambda off: pl.BlockSpec((S, LANES), lambda b, h: (b, off + h))
    return blk


def gdn_fwd(G, gates, P, g_on, *, B):
    T = G.shape[0]
    S = T // B
    C = GDN_CHUNK
    N = S // C
    grp = min(GDN_GROUP_FWD, N)
    R = grp * C
    hd = GDN_HEAD_DIM

    def body(q_ref, k_ref, v_ref, gt_ref, z_ref, gon_ref, o_ref, ob_ref, st_ref, tm_ref, A_s, B_s, Q_s, O_s, gam_s):
        h = pl.program_id(1)

        def local(gi, carry):
            r0 = pl.multiple_of(gi * R, R)
            gt = gt_ref[pl.ds(r0, R), :]
            loc = _gdn_local(q_ref[pl.ds(r0, R), :], k_ref[pl.ds(r0, R), :], v_ref[pl.ds(r0, R), :],
                             _pick_lane(gt, SM_B + h), _pick_lane(gt, SM_A + h))
            chunks = pl.ds(gi * grp, grp)
            tm_ref[0, 0, pl.ds(r0, R), :] = loc["Tm"].reshape(R, C)
            A_s[chunks] = -_bmm(loc["kd"], loc["w"], B_TN)
            B_s[chunks] = _bmm(loc["kd"], loc["u"], B_TN)
            Q_s[pl.ds(r0, R), :] = (loc["qg"] - _bmm(loc["M"], loc["w"], B_NN)).reshape(R, hd)
            O_s[pl.ds(r0, R), :] = _bmm(loc["M"], loc["u"], B_NN).reshape(R, hd)
            gam_s[chunks] = jnp.broadcast_to(loc["gam"], (grp, 1, LANES))
            return carry

        lax.fori_loop(0, N // grp, local, 0)

        def step(n, state):
            st_ref[0, 0, n] = state
            return state * gam_s[n] + _dotm(A_s[n], state, NN) + B_s[n]

        lax.fori_loop(0, N, step, jnp.zeros((hd, hd), f32))

        def outputs(gi, carry):
            r0 = pl.multiple_of(gi * R, R)
            Q = Q_s[pl.ds(r0, R), :].reshape(grp, C, hd)
            o = _bmm(Q, st_ref[0, 0, pl.ds(gi * grp, grp)], B_NN).reshape(R, hd) + O_s[pl.ds(r0, R), :]
            o_ref[pl.ds(r0, R), :] = o
            return carry

        lax.fori_loop(0, N // grp, outputs, 0)
        o = o_ref[...]
        z = z_ref[...]
        ob_ref[...] = (_head_rms(o, gon_ref[...])[0] * (z * jax.nn.sigmoid(z))).astype(ob_ref.dtype)

    blk = lambda off: pl.BlockSpec((S, LANES), lambda b, h: (b, off + h))
    rows = lambda: pltpu.VMEM((S, hd), f32)
    return pl.pallas_call(
        body, name="gdn_fwd", grid=(B, GDN_HEADS),
        in_specs=[blk(0), blk(GDN_HEADS), blk(2 * GDN_HEADS), pl.BlockSpec((S, LANES), lambda b, h: (b, 0)),
                  blk(COL_Z // LANES), pl.BlockSpec((1, hd), lambda b, h: (0, 0))],
        out_specs=[blk(0), blk(0), pl.BlockSpec((1, 1, N, hd, hd), lambda b, h: (b, h, 0, 0, 0)),
                   pl.BlockSpec((1, 1, S, C), lambda b, h: (b, h, 0, 0))],
        out_shape=[jax.ShapeDtypeStruct((T, GDN_WIDTH), f32), jax.ShapeDtypeStruct((T, GDN_WIDTH), MXU_DTYPE),
                   jax.ShapeDtypeStruct((B, GDN_HEADS, N, hd, hd), f32), jax.ShapeDtypeStruct((B, GDN_HEADS, S, C), f32)],
        scratch_shapes=[pltpu.VMEM((N, hd, hd), f32), pltpu.VMEM((N, hd, hd), f32), rows(), rows(),
                        pltpu.VMEM((N, 1, LANES), f32)],
        compiler_params=_cparams("parallel", "parallel"),
    )(G, G, G, gates, P, g_on)


def gdn_bwd(G, gates, P, g_on, o_raw, states, tm, d_oab, *, B):
    T = G.shape[0]
    S = T // B
    C = GDN_CHUNK
    N = S // C
    grp = min(GDN_GROUP, N)
    R = grp * C
    hd = GDN_HEAD_DIM

    def body(q_ref, k_ref, v_ref, gt_ref, z_ref, gon_ref, o_ref, st_ref, tm_ref, dob_ref,
             dq_ref, dk_ref, dv_ref, dgt_ref, dz_ref, dgon_ref,
             u_s, w_s, M_s, gam_s, do_s, A_s, C_s, dst_s):
        b, h = pl.program_id(0), pl.program_id(1)

        @pl.when((b == 0) & (h == 0))
        def _():
            dgon_ref[...] = jnp.zeros_like(dgon_ref)

        @pl.when(h == 0)
        def _():
            dgt_ref[...] = jnp.zeros_like(dgt_ref)

        def group_inputs(gi, uwm):
            r0 = pl.multiple_of(gi * R, R)
            gt = gt_ref[pl.ds(r0, R), :]
            return r0, _gdn_local(q_ref[pl.ds(r0, R), :], k_ref[pl.ds(r0, R), :], v_ref[pl.ds(r0, R), :],
                                  _pick_lane(gt, SM_B + h), _pick_lane(gt, SM_A + h), tm_ref[0, 0, pl.ds(r0, R), :], uwm)

        def local(gi, carry):
            r0, loc = group_inputs(gi, True)
            rows, chunks = pl.ds(r0, R), pl.ds(gi * grp, grp)
            u_s[rows, :] = loc["u"].reshape(R, hd)
            w_s[rows, :] = loc["w"].reshape(R, hd)
            M_s[rows, :] = loc["M"].reshape(R, C)
            gam_s[chunks] = jnp.broadcast_to(loc["gam"], (grp, 1, LANES))
            o, z, gon = o_ref[rows, :], z_ref[rows, :], gon_ref[...]
            dob = dob_ref[rows, :]
            on, ro = _head_rms(o, gon)
            sz = jax.nn.sigmoid(z)
            dz_ref[rows, :] = (dob * on * (sz * (1.0 + z * (1.0 - sz)))).astype(dz_ref.dtype)
            do, dgon = _head_rms_bwd(o, ro, gon, dob * (z * sz))
            do_s[rows, :] = do
            dgon_ref[...] += dgon
            A_s[chunks] = -_bmm(loc["kd"], loc["w"], B_TN)
            C_s[chunks] = _bmm(loc["qg"] - _bmm(loc["M"], loc["w"], B_NN), do.reshape(grp, C, hd), B_TN)
            return carry

        lax.fori_loop(0, N // grp, local, 0)

        def step(t, dS):
            n = N - 1 - t
            dst_s[n] = dS
            return dS * gam_s[n] + _dotm(A_s[n], dS, TN) + C_s[n]

        lax.fori_loop(0, N, step, jnp.zeros((hd, hd), f32))

        def finish(gi, carry):
            r0, L = group_inputs(gi, False)
            n = grp
            rows, chunks = pl.ds(r0, R), pl.ds(gi * grp, grp)
            g3 = lambda ref: ref[rows, :].reshape(n, C, -1)
            u, w, do = g3(u_s), g3(w_s), g3(do_s)
            L["M"] = g3(M_s)
            state, dS = st_ref[0, 0, chunks], dst_s[chunks]
            v_new = u - _bmm(w, state, B_NN)
            du = _bmm(L["M"], do, B_TN) + _bmm(L["kd"], dS, B_NN)
            dw = -_bmm(du, state, B_NT)
            dqg = _bmm(do, state, B_NT)
            dM = _bmm(do, v_new, B_NT)
            dkd = _bmm(v_new, dS, B_NT)
            dgl_state = jnp.sum(jnp.sum(dS * state, axis=2, keepdims=True), axis=1, keepdims=True) * L["gam"]
            TmT = jnp.swapaxes(L["Tm"], 1, 2)
            dTm = _bmm(du, L["vb"], B_NT) + _bmm(dw, L["kg"], B_NT)
            dvb = _bmm(TmT, du, B_NN)
            dkg = _bmm(TmT, dw, B_NN)
            dA = jnp.where(L["row"] > L["col"], -_bmm(_bmm(TmT, dTm, B_NN), TmT, B_NN), 0.0)
            dKK = dA * L["D"]
            dQK = dM * L["D"]
            dkb = _bmm(dKK, L["k"], B_NN) + dkg * L["Gam"]
            dk = (_bmm(dKK, L["kb"], B_TN) + _bmm(dQK, L["q"], B_TN) + dkd * L["kdec"] + L["beta"] * dkb)
            dq = (_bmm(dQK, L["k"], B_NN) + dqg * L["Gam"]) * (GDN_HEAD_DIM ** -0.5)
            E = dA * L["A"] + dM * L["M"]
            r = jnp.sum(dkd * L["kd"], axis=-1, keepdims=True)
            dgc = (jnp.sum(E, axis=2, keepdims=True) - jnp.sum(jnp.swapaxes(E, 1, 2), axis=2, keepdims=True)
                   + jnp.sum(dkg * L["kg"], axis=-1, keepdims=True) + jnp.sum(dqg * L["qg"], axis=-1, keepdims=True) - r)
            dgl = jnp.sum(r, axis=1, keepdims=True) + dgl_state
            rowc = lax.broadcasted_iota(jnp.int32, (n, C, 1), 1)
            dgc = dgc + jnp.where(rowc == C - 1, dgl, 0.0)
            dbeta = jnp.sum(dkb * L["k"], axis=-1, keepdims=True) + jnp.sum(dvb * L["v"], axis=-1, keepdims=True)
            dq_ref[rows, :] = dq.reshape(R, hd)
            dk_ref[rows, :] = dk.reshape(R, hd)
            dv_ref[rows, :] = (L["beta"] * dvb).reshape(R, hd)
            lane = lax.broadcasted_iota(jnp.int32, (R, LANES), 1)
            dgt_ref[rows, :] += (jnp.where(lane == SM_B + h, dbeta.reshape(R, 1), 0.0)
                                 + jnp.where(lane == SM_A + h, dgc.reshape(R, 1), 0.0))
            return carry

        lax.fori_loop(0, N // grp, finish, 0)

    blk = lambda off: pl.BlockSpec((S, LANES), lambda b, h: (b, off + h))
    rows = lambda: pltpu.VMEM((S, hd), f32)
    return pl.pallas_call(
        body, name="gdn_bwd", grid=(B, GDN_HEADS),
        in_specs=[blk(0), blk(GDN_HEADS), blk(2 * GDN_HEADS), pl.BlockSpec((S, LANES), lambda b, h: (b, 0)),
                  blk(COL_Z // LANES), pl.BlockSpec((1, hd), lambda b, h: (0, 0)), blk(0),
                  pl.BlockSpec((1, 1, N, hd, hd), lambda b, h: (b, h, 0, 0, 0)),
                  pl.BlockSpec((1, 1, S, C), lambda b, h: (b, h, 0, 0)), blk(GDN_HEADS)],
        out_specs=[blk(0), blk(0), blk(0), pl.BlockSpec((S, LANES), lambda b, h: (b, 0)), blk(0),
                   pl.BlockSpec((1, hd), lambda b, h: (0, 0))],
        out_shape=[jax.ShapeDtypeStruct((T, GDN_WIDTH), f32), jax.ShapeDtypeStruct((T, GDN_WIDTH), f32),
                   jax.ShapeDtypeStruct((T, GDN_WIDTH), f32), jax.ShapeDtypeStruct((T, LANES), f32),
                   jax.ShapeDtypeStruct((T, GDN_WIDTH), MXU_DTYPE), jax.ShapeDtypeStruct((1, hd), f32)],
        scratch_shapes=[rows(), rows(), pltpu.VMEM((S, C), f32), pltpu.VMEM((N, 1, LANES), f32), rows(),
                        pltpu.VMEM((N, hd, hd), f32), pltpu.VMEM((N, hd, hd), f32), pltpu.VMEM((N, hd, hd), f32)],
        compiler_params=_cparams("arbitrary", "arbitrary"),
    )(G, G, G, gates, P, g_on, o_raw, states, tm, d_oab)


IN_SPLIT = (0, 1536, 1544, 3080, 3088, 3600)


IN_SHARD = IN_DIM // 4
IN_SHARD_PAD = 928


def align_w_in_t(wt):
    s = IN_SPLIT
    pad = jnp.zeros((IN_ALIGNED - IN_DIM, wt.shape[1]), wt.dtype)
    return jnp.concatenate([wt[s[0]:s[1]], wt[s[2]:s[3]], wt[s[4]:s[5]], wt[s[1]:s[2]], wt[s[3]:s[4]], pad], axis=0)


def unalign_w_in_t(wa):
    return jnp.concatenate([wa[0:1536], wa[COL_SMALL:COL_SMALL + 8], wa[1536:3072],
                            wa[COL_SMALL + 8:COL_SMALL + 16], wa[3072:3584]], axis=0)


IN_SEGMENTS = (((0, 1536), 0), ((1536, 1544), COL_SMALL), ((1544, 3080), 1536), ((3080, 3088), COL_SMALL + 8),
               ((3088, 3600), 3072))


def align_w_in_slots(slots):
    pieces = []
    for (lo, hi), _ in sorted(IN_SEGMENTS, key=lambda seg: seg[1]):
        for k in range(N_CHIPS):
            a, b = max(lo, k * IN_SHARD), min(hi, (k + 1) * IN_SHARD)
            if a < b:
                pieces.append(slots[k, a - k * IN_SHARD:b - k * IN_SHARD])
    pieces.append(jnp.zeros((IN_ALIGNED - IN_DIM, slots.shape[2]), slots.dtype))
    return jnp.concatenate(pieces, axis=0)


def unalign_to_slots(wa):
    slots = []
    for k in range(N_CHIPS):
        lo, hi = k * IN_SHARD, (k + 1) * IN_SHARD
        pieces = []
        for (a, b), first in IN_SEGMENTS:
            x, y = max(a, lo), min(b, hi)
            if x < y:
                pieces.append(wa[first + x - a:first + y - a])
        pieces.append(jnp.zeros((IN_SHARD_PAD - IN_SHARD, wa.shape[1]), wa.dtype))
        slots.append(jnp.concatenate(pieces, axis=0))
    return jnp.stack(slots)


def _lanes_vec(pieces):
    v = jnp.zeros((1, LANES), f32)
    for off, a in pieces:
        v = lax.dynamic_update_slice(v, a.astype(f32), (0, off))
    return v


def local_step(x, mem, target, w, sp, *, B):
    T = x.shape[0]
    S = T // B
    gq8, gk8 = jnp.tile(sp["fox_qnorm_g"], (1, FOX_HEADS)), jnp.tile(sp["fox_knorm_g"], (1, FOX_HEADS))
    go2 = jnp.tile(sp["fox_onorm_g"], (1, 2))
    bias = _lanes_vec([(SM_F, sp["fox_f_bias"]), (SM_A, sp["gdn_dt_bias"])])
    alog = _lanes_vec([(SM_A, sp["gdn_A_log"])])

    h1 = rms_fwd(x, sp["norm_mix_g"], name="rms_mix")
    P = matmul(h1, w["wa_t"], tb=True, name="mm_in", tn=IN_TILE)
    gates = gates_fwd(P, bias, alog, B=B)
    c = gates[:, SM_F:SM_F + FOX_HEADS].reshape(B, S, FOX_HEADS).transpose(0, 2, 1)
    ccol, crow = c[..., None], c.reshape(B, FOX_HEADS, S // FOX_TQ, 1, FOX_TQ)
    qn, kn, vb = fox_prep_fwd(P, gq8, gk8)
    o_raw, o_a, lse = fox_core_fwd(qn, kn, vb, ccol, crow, go2, B=B)
    G = gdn_prep_fwd(P, w["conv_w"], B=B)
    ob_raw, o_b, states, gdn_tm = gdn_fwd(G, gates, P, sp["gdn_onorm_g"], B=B)
    oab = jnp.concatenate([o_a, o_b], axis=1)
    if "late" in w:
        w = {**w, **w["late"](oab)}
    x2, hq = matmul_rows(oab, w["w_out"], (x, sp["norm_xattn_g"]), mode="rms_fwd", name="mm_out_rms")
    hm = rms_fwd(mem, sp["mem_norm_g"], name="rms_mem")
    cq = matmul(hq, w["w_cq"], name="mm_cq")
    ckv = matmul(hm, w["w_ckv"], name="mm_ckv")
    co = xattn_fwd(cq, ckv, sp["xattn_qnorm_g"], sp["xattn_knorm_g"], B=B)
    x3, hf = matmul_rows(co, w["w_co"], (x2, sp["norm_mlp_g"]), mode="rms_fwd", name="mm_co_rms")
    act = matmul(hf, w["w_mlp1"], b_stacked=True, relu2_out=True, out_dtype=MXU_DTYPE, name="mm_mlp1")
    dy, dy_op, loss = matmul_rows(act, w["w_mlp2"], (x3, target), mode="loss", name="mm_mlp2_loss")

    da = matmul(dy_op, w["w_mlp2"], tb=True, relu2_bwd_aux=act, out_dtype=MXU_DTYPE, name="mm_d_act")
    g_mlp2 = matmul(act, dy_op, ta=True, out_dtype=WIRE_DTYPE, name="mm_g_mlp2")
    g_mlp1 = matmul(hf, da, ta=True, out_stacked=True, out_dtype=WIRE_DTYPE, name="mm_g_mlp1")
    by_rows = lambda g: g.reshape(N_CHIPS, g.shape[0] // N_CHIPS, g.shape[1])
    early = w.get("grads_ready", lambda grads: jnp.zeros((1, 1), f32))
    tok = early(dict(w_mlp1=g_mlp1, w_mlp2=by_rows(g_mlp2)))[0, 0]
    dx3, g_norm_mlp = matmul_rows(da, w["w_mlp1"], (x3, sp["norm_mlp_g"] + tok, dy), mode="rms_bwd", tb=True,
                                  b_stacked=True, name="mm_d_hf_rms")
    dco = matmul(dx3, w["w_co"], tb=True, name="mm_d_co")
    g_co = matmul(co, dx3, ta=True, out_dtype=WIRE_DTYPE, name="mm_g_co")
    g_co = g_co.reshape(XATTN_WIDTH, N_CHIPS, D_MODEL // N_CHIPS).transpose(1, 0, 2)
    dcq, dckv, g_xq, g_xk = xattn_bwd(cq, ckv, sp["xattn_qnorm_g"], sp["xattn_knorm_g"], dco, B=B)
    g_cq = matmul(hq, dcq, ta=True, out_dtype=WIRE_DTYPE, name="mm_g_cq")
    g_ckv = matmul(hm, dckv, ta=True, out_dtype=WIRE_DTYPE, name="mm_g_ckv")
    _, g_mem_norm = matmul_rows(dckv, w["w_ckv"], (mem, sp["mem_norm_g"], None), mode="rms_bwd", tb=True, name="mm_d_hm_rms")
    dx2, g_norm_xattn = matmul_rows(dcq, w["w_cq"], (x2, sp["norm_xattn_g"], dx3), mode="rms_bwd", tb=True, name="mm_d_hq_rms")
    doab = matmul(dx2, w["w_out"], tb=True, name="mm_d_oab")
    g_out = matmul(oab, dx2, ta=True, out_dtype=WIRE_DTYPE, name="mm_g_out")
    tok = early(dict(w_co=g_co, w_cq=by_rows(g_cq), w_ckv=by_rows(g_ckv), w_out=by_rows(g_out)))[0, 0]
    dqn, dkn, dv_f, dckey, dcrow, dgo2 = fox_core_bwd(qn, kn, vb, ccol, crow, go2 + tok, o_raw, lse, doab, B=B)
    dq_f, dk_f, dgq8, dgk8 = fox_prep_bwd(P, gq8, gk8, dqn, dkn)
    dGq, dGk, dGv, dgt, dz, g_gdn_on = gdn_bwd(G, gates, P, sp["gdn_onorm_g"], ob_raw, states, gdn_tm, doab, B=B)
    dPg, g_conv = gdn_prep_bwd(P, w["conv_w"], dGq, dGk, dGv, B=B)
    dc = (dckey[:, :, 0, :] + dcrow.reshape(B, FOX_HEADS, S)).transpose(0, 2, 1).reshape(T, FOX_HEADS)
    dgates = dgt + jnp.pad(dc, ((0, 0), (SM_F, LANES - SM_F - FOX_HEADS)))
    dsmall, par = gates_bwd(P, bias, alog, dgates, B=B)
    dP = jnp.concatenate([dq_f, dk_f, dv_f, dPg, dz, dsmall, jnp.zeros((T, IN_ALIGNED - COL_SMALL - LANES), MXU_DTYPE)], axis=1)
    g_wa = matmul(dP, h1, ta=True, out_dtype=WIRE_DTYPE, name="mm_g_in", tm=IN_TILE)
    g_in = unalign_to_slots(g_wa)
    tok = early(dict(w_in=g_in))[0, 0]
    dx, g_norm_mix = matmul_rows(dP, w["wa_t"], (x, sp["norm_mix_g"] + tok, dx2), mode="rms_bwd", tk=IN_TILE,
                                 name="mm_d_h1_rms")

    fold = lambda g: jnp.sum(g.reshape(-1, FOX_HEAD_DIM), axis=0, keepdims=True)
    big = dict(w_in=g_in, w_out=by_rows(g_out), w_cq=by_rows(g_cq), w_ckv=by_rows(g_ckv), w_co=g_co, w_mlp1=g_mlp1,
               w_mlp2=by_rows(g_mlp2))
    small = dict(norm_mix_g=g_norm_mix, fox_qnorm_g=fold(dgq8), fox_knorm_g=fold(dgk8),
                 fox_f_bias=par[0:1, SM_F:SM_F + FOX_HEADS], fox_onorm_g=fold(dgo2), gdn_conv_w=g_conv,
                 gdn_A_log=par[1:2, SM_A:SM_A + GDN_HEADS], gdn_dt_bias=par[0:1, SM_A:SM_A + GDN_HEADS],
                 gdn_onorm_g=g_gdn_on, norm_xattn_g=g_norm_xattn, mem_norm_g=g_mem_norm,
                 xattn_qnorm_g=g_xq, xattn_knorm_g=g_xk, norm_mlp_g=g_norm_mlp)
    return loss, dx, big, small


MESH_IDS = pl.DeviceIdType.MESH
N_CHIPS = 4
HBM_SPEC = pl.BlockSpec(memory_space=pltpu.HBM)
PACK_ROWS = 30720
PACK_HALF = PACK_ROWS // 2
PACK_BLOCK = 3072


def _place():
    return lax.axis_index("x"), lax.axis_index("y"), lax.axis_index("c")


def _other_chips(x, y):
    return [(1 - x, y), (x, 1 - y), (1 - x, 1 - y)]


def _remote(src, dst, send_sem, recv_sem, to):
    return pltpu.make_async_remote_copy(src_ref=src, dst_ref=dst, send_sem=send_sem, recv_sem=recv_sem,
                                        device_id=to, device_id_type=MESH_IDS)


def all_gather_shards(packed):
    half = PACK_HALF

    def body(src_ref, out_ref, send_sems, recv_sems):
        x, y, c = _place()
        me_chip = 2 * x + y
        sibling = (x, y, 1 - c)
        chips = _other_chips(x, y)

        def rows(chip, core):
            return out_ref.at[chip, pl.ds(core * half, half), :]

        sends = [_remote(src_ref.at[pl.ds(c * half, half), :], rows(me_chip, c), send_sems.at[j], recv_sems.at[j], (px, py, c))
                 for j, (px, py) in enumerate(chips)]
        for cp in sends:
            cp.start()
        passed = []
        for j, (px, py) in enumerate(chips):
            theirs = rows(2 * px + py, c)
            _remote(theirs, theirs, send_sems.at[j], recv_sems.at[j], (px, py, c)).wait_recv()
            cp = _remote(theirs, theirs, send_sems.at[3 + j], recv_sems.at[3 + j], sibling)
            cp.start()
            passed.append(cp)
        for j, (px, py) in enumerate(chips):
            theirs = rows(2 * px + py, 1 - c)
            _remote(theirs, theirs, send_sems.at[3 + j], recv_sems.at[3 + j], sibling).wait_recv()
        for cp in sends + passed:
            cp.wait_send()

    return pl.pallas_call(
        body, name="all_gather_shards", in_specs=[HBM_SPEC], out_specs=HBM_SPEC,
        out_shape=jax.ShapeDtypeStruct((N_CHIPS,) + packed.shape, packed.dtype),
        scratch_shapes=[pltpu.SemaphoreType.DMA((6,)), pltpu.SemaphoreType.DMA((6,))],
    )(packed)


def exchange_core_halves(G):
    half = PACK_HALF

    def body(g_ref, land_ref, send_sem, recv_sem):
        x, y, c = _place()
        cp = _remote(g_ref.at[:, pl.ds((1 - c) * half, half), :], land_ref, send_sem, recv_sem, (x, y, 1 - c))
        cp.start()
        cp.wait()

    return pl.pallas_call(
        body, name="exchange_core_halves", in_specs=[HBM_SPEC], out_specs=HBM_SPEC,
        out_shape=jax.ShapeDtypeStruct((N_CHIPS, half, LANES), G.dtype),
        scratch_shapes=[pltpu.SemaphoreType.DMA(()), pltpu.SemaphoreType.DMA(())],
    )(G)


def add_core_halves(G, land, core):
    nb = PACK_HALF // PACK_BLOCK

    def body(c_ref, g_ref, l_ref, o_ref):
        o_ref[...] = (g_ref[...].astype(f32) + l_ref[...].astype(f32)).astype(o_ref.dtype)

    blk = (1, PACK_BLOCK, LANES)
    return pl.pallas_call(
        body, name="add_core_halves",
        grid_spec=pltpu.PrefetchScalarGridSpec(
            num_scalar_prefetch=1, grid=(N_CHIPS, nb),
            in_specs=[pl.BlockSpec(blk, lambda k, i, c_ref: (k, c_ref[0] * nb + i, 0)),
                      pl.BlockSpec(blk, lambda k, i, c_ref: (k, i, 0))],
            out_specs=pl.BlockSpec(blk, lambda k, i, c_ref: (k, i, 0))),
        out_shape=jax.ShapeDtypeStruct(land.shape, land.dtype),
        compiler_params=_cparams("parallel", "parallel"),
    )(core, G, land)


def scatter_to_chips(part):
    def body(p_ref, land_ref, send_sems, recv_sems):
        x, y, c = _place()
        me_chip = 2 * x + y
        chips = _other_chips(x, y)
        sends = [_remote(p_ref.at[2 * px + py], land_ref.at[me_chip], send_sems.at[j], recv_sems.at[j], (px, py, c))
                 for j, (px, py) in enumerate(chips)]
        for cp in sends:
            cp.start()
        for j, (px, py) in enumerate(chips):
            slot = land_ref.at[2 * px + py]
            _remote(slot, slot, send_sems.at[j], recv_sems.at[j], (px, py, c)).wait_recv()
        for cp in sends:
            cp.wait_send()

    return pl.pallas_call(
        body, name="scatter_to_chips", in_specs=[HBM_SPEC], out_specs=HBM_SPEC,
        out_shape=jax.ShapeDtypeStruct(part.shape, part.dtype),
        scratch_shapes=[pltpu.SemaphoreType.DMA((3,)), pltpu.SemaphoreType.DMA((3,))],
    )(part)


def sum_chips(part, land, order):
    nb = PACK_HALF // PACK_BLOCK

    def body(order_ref, p_ref, l1_ref, l2_ref, l3_ref, o_ref):
        o_ref[...] = ((p_ref[0].astype(f32) + l1_ref[0].astype(f32)) + l2_ref[0].astype(f32)) + l3_ref[0].astype(f32)

    slot = lambda j: pl.BlockSpec((1, PACK_BLOCK, LANES), lambda i, order_ref: (order_ref[j], i, 0))
    return pl.pallas_call(
        body, name="sum_chips",
        grid_spec=pltpu.PrefetchScalarGridSpec(
            num_scalar_prefetch=1, grid=(nb,), in_specs=[slot(0), slot(1), slot(2), slot(3)],
            out_specs=pl.BlockSpec((PACK_BLOCK, LANES), lambda i, order_ref: (i, 0))),
        out_shape=jax.ShapeDtypeStruct((PACK_HALF, LANES), f32),
        compiler_params=_cparams("parallel"),
    )(order, part, land, land, land)


def swap_core_halves(red):
    def body(r_ref, out_ref, send_sem, recv_sem):
        x, y, c = _place()
        cp = _remote(r_ref, out_ref, send_sem, recv_sem, (x, y, 1 - c))
        cp.start()
        cp.wait()

    return pl.pallas_call(
        body, name="swap_core_halves", in_specs=[HBM_SPEC], out_specs=HBM_SPEC,
        out_shape=jax.ShapeDtypeStruct(red.shape, red.dtype),
        scratch_shapes=[pltpu.SemaphoreType.DMA(()), pltpu.SemaphoreType.DMA(())],
    )(red)


def _half(ref, core):
    rows = ref.shape[-2] // 2
    return ref.at[(slice(None),) * (len(ref.shape) - 2) + (pl.ds(core * rows, rows), slice(None))]


def gather_weights(shards, conv):
    n = len(shards)

    def body(*refs):
        src, conv_src = refs[:n], refs[n]
        out, conv_out = refs[n + 1:2 * n + 1], refs[2 * n + 1]
        send_sems, recv_sems = refs[2 * n + 2], refs[2 * n + 3]
        x, y, c = _place()
        me_chip = 2 * x + y
        sibling = (x, y, 1 - c)
        chips = _other_chips(x, y)
        sends = []
        for a in range(n):
            for j, (px, py) in enumerate(chips):
                sends.append(_remote(_half(src[a], c), _half(out[a].at[me_chip], c),
                                     send_sems.at[6 * a + j], recv_sems.at[6 * a + j], (px, py, c)))
        for j, (px, py) in enumerate(chips):
            sends.append(_remote(conv_src, conv_out.at[me_chip], send_sems.at[6 * n + j], recv_sems.at[6 * n + j], (px, py, c)))
        for cp in sends:
            cp.start()
        passed = []
        for a in range(n):
            for j, (px, py) in enumerate(chips):
                theirs = _half(out[a].at[2 * px + py], c)
                _remote(theirs, theirs, send_sems.at[6 * a + j], recv_sems.at[6 * a + j], (px, py, c)).wait_recv()
                cp = _remote(theirs, theirs, send_sems.at[6 * a + 3 + j], recv_sems.at[6 * a + 3 + j], sibling)
                cp.start()
                passed.append(cp)
        for j, (px, py) in enumerate(chips):
            theirs = conv_out.at[2 * px + py]
            _remote(theirs, theirs, send_sems.at[6 * n + j], recv_sems.at[6 * n + j], (px, py, c)).wait_recv()
        for a in range(n):
            for j, (px, py) in enumerate(chips):
                theirs = _half(out[a].at[2 * px + py], 1 - c)
                _remote(theirs, theirs, send_sems.at[6 * a + 3 + j], recv_sems.at[6 * a + 3 + j], sibling).wait_recv()
        for cp in sends + passed:
            cp.wait_send()

    return pl.pallas_call(
        body, name="gather_weights", in_specs=[HBM_SPEC] * (n + 1), out_specs=[HBM_SPEC] * (n + 1),
        out_shape=[jax.ShapeDtypeStruct((N_CHIPS,) + s.shape, s.dtype) for s in list(shards) + [conv]],
        scratch_shapes=[pltpu.SemaphoreType.DMA((6 * n + 3,)), pltpu.SemaphoreType.DMA((6 * n + 3,))],
    )(*shards, conv)


SEM_SPEC = pl.BlockSpec(memory_space=pltpu.SEMAPHORE)
SPLIT_EFFECT = pltpu.SideEffectType.DATAFLOW_SIDE_EFFECTING


def _gather_async_copies(src, land, send_sems, recv_sems, x, y, c):
    me_chip = 2 * x + y
    sends, arrivals = [], []
    for a in range(len(src)):
        for j, (px, py) in enumerate(_other_chips(x, y)):
            for core in range(2):
                sends.append(_remote(_half(src[a], c), _half(land[a].at[me_chip], c), send_sems.at[6 * a + 2 * j + core],
                                     recv_sems.at[6 * a + 2 * j + c], (px, py, core)))
                theirs = _half(land[a].at[2 * px + py], core)
                arrivals.append(_remote(theirs, theirs, send_sems.at[6 * a + 2 * j + core],
                                        recv_sems.at[6 * a + 2 * j + core], (px, py, core)))
    return sends, arrivals


def gather_weights_start(shards, after):
    n = len(shards)

    def body(*refs):
        src, land = refs[:n], refs[n:2 * n]
        send_sems, recv_sems, token = refs[2 * n + 1], refs[2 * n + 2], refs[4 * n + 3]
        x, y, c = _place()
        for cp in _gather_async_copies(src, land, send_sems, recv_sems, x, y, c)[0]:
            cp.start()
        token[...] = jnp.zeros_like(token)

    zones = [pltpu.with_memory_space_constraint(lax.empty((N_CHIPS,) + s.shape, s.dtype), pltpu.HBM) for s in shards]
    srcs = [pltpu.with_memory_space_constraint(s, pltpu.HBM) for s in shards]
    out = pl.pallas_call(
        body, name="gather_weights_start",
        out_shape=[pltpu.SemaphoreType.DMA((6 * n,)), pltpu.SemaphoreType.DMA((6 * n,))]
        + [pltpu.HBM(s.shape, s.dtype) for s in shards] + [pltpu.HBM(z.shape, z.dtype) for z in zones]
        + [jax.ShapeDtypeStruct((8, LANES), f32)],
        in_specs=[HBM_SPEC] * (2 * n) + [pl.BlockSpec(memory_space=pl.ANY)],
        out_specs=[SEM_SPEC, SEM_SPEC] + [HBM_SPEC] * (2 * n) + [pl.BlockSpec(memory_space=pltpu.VMEM)],
        input_output_aliases={i: 2 + i for i in range(2 * n)},
        compiler_params=pltpu.CompilerParams(has_side_effects=SPLIT_EFFECT),
    )(*srcs, *zones, after)
    return out[0], out[1], out[2:2 + n], out[2 + n:2 + 2 * n], out[-1]


def gather_weights_wait(send_sems, recv_sems, shards, zones, after):
    n = len(shards)

    def body(*refs):
        src, land = refs[:n], refs[n:2 * n]
        send_sems, recv_sems = refs[2 * n], refs[2 * n + 1]
        x, y, c = _place()
        sends, arrivals = _gather_async_copies(src, land, send_sems, recv_sems, x, y, c)
        for cp in sends:
            cp.wait_send()
        for cp in arrivals:
            cp.wait_recv()

    out = pl.pallas_call(
        body, name="gather_weights_wait",
        out_shape=[pltpu.HBM(s.shape, s.dtype) for s in shards] + [pltpu.HBM(z.shape, z.dtype) for z in zones],
        in_specs=[HBM_SPEC] * (2 * n) + [SEM_SPEC, SEM_SPEC, pl.BlockSpec(memory_space=pl.ANY)],
        out_specs=[HBM_SPEC] * (2 * n),
        input_output_aliases={i: i for i in range(2 * n)},
        compiler_params=pltpu.CompilerParams(has_side_effects=SPLIT_EFFECT),
    )(*shards, *zones, send_sems, recv_sems, after)
    return out[n:]


def swap_grad_halves(grads, *, name):
    n = len(grads)

    def body(*refs):
        g, land, send_sems, recv_sems = refs[:n], refs[n:2 * n], refs[2 * n], refs[2 * n + 1]
        x, y, c = _place()
        copies = [_remote(_half(g[a], 1 - c), land[a], send_sems.at[a], recv_sems.at[a], (x, y, 1 - c)) for a in range(n)]
        for cp in copies:
            cp.start()
        for cp in copies:
            cp.wait()

    return pl.pallas_call(
        body, name=name, in_specs=[HBM_SPEC] * n, out_specs=[HBM_SPEC] * n,
        out_shape=[jax.ShapeDtypeStruct((N_CHIPS, g.shape[1] // 2, g.shape[2]), g.dtype) for g in grads],
        scratch_shapes=[pltpu.SemaphoreType.DMA((n,)), pltpu.SemaphoreType.DMA((n,))],
    )(*grads)


GRAD_ROWS = 256


def add_grad_halves(g, land, core, *, name):
    _, half, cols = land.shape
    tr = GRAD_ROWS if half % GRAD_ROWS == 0 else half
    nb = half // tr

    def body(c_ref, g_ref, l_ref, o_ref):
        o_ref[...] = (g_ref[...].astype(f32) + l_ref[...].astype(f32)).astype(o_ref.dtype)

    blk = (1, tr, cols)
    return pl.pallas_call(
        body, name=name,
        grid_spec=pltpu.PrefetchScalarGridSpec(
            num_scalar_prefetch=1, grid=(N_CHIPS, nb),
            in_specs=[pl.BlockSpec(blk, lambda k, i, c_ref: (k, c_ref[0] * nb + i, 0)),
                      pl.BlockSpec(blk, lambda k, i, c_ref: (k, i, 0))],
            out_specs=pl.BlockSpec(blk, lambda k, i, c_ref: (k, i, 0))),
        out_shape=jax.ShapeDtypeStruct(land.shape, land.dtype),
        compiler_params=_cparams("parallel", "parallel"),
    )(core, g, land)


def scatter_grads(parts):
    n = len(parts)

    def body(*refs):
        p, land, send_sems, recv_sems = refs[:n], refs[n:2 * n], refs[2 * n], refs[2 * n + 1]
        x, y, c = _place()
        me_chip = 2 * x + y
        chips = _other_chips(x, y)
        sends = [_remote(p[a].at[2 * px + py], land[a].at[me_chip], send_sems.at[3 * a + j], recv_sems.at[3 * a + j], (px, py, c))
                 for a in range(n) for j, (px, py) in enumerate(chips)]
        for cp in sends:
            cp.start()
        for a in range(n):
            for j, (px, py) in enumerate(chips):
                slot = land[a].at[2 * px + py]
                _remote(slot, slot, send_sems.at[3 * a + j], recv_sems.at[3 * a + j], (px, py, c)).wait_recv()
        for cp in sends:
            cp.wait_send()

    return pl.pallas_call(
        body, name="scatter_grads", in_specs=[HBM_SPEC] * n, out_specs=[HBM_SPEC] * n,
        out_shape=[jax.ShapeDtypeStruct(p.shape, p.dtype) for p in parts],
        scratch_shapes=[pltpu.SemaphoreType.DMA((3 * n,)), pltpu.SemaphoreType.DMA((3 * n,))],
    )(*parts)


def _scatter_async_copies(parts, land, send_sems, recv_sems, x, y, c):
    me_chip = 2 * x + y
    sends, arrivals = [], []
    for a in range(len(parts)):
        for j, (px, py) in enumerate(_other_chips(x, y)):
            sems = (send_sems.at[3 * a + j], recv_sems.at[3 * a + j], (px, py, c))
            sends.append(_remote(parts[a].at[2 * px + py], land[a].at[me_chip], *sems))
            slot = land[a].at[2 * px + py]
            arrivals.append(_remote(slot, slot, *sems))
    return sends, arrivals


def scatter_grads_start(parts, *, name):
    n = len(parts)

    def body(*refs):
        p, land = refs[:n], refs[n:2 * n]
        send_sems, recv_sems, token = refs[2 * n], refs[2 * n + 1], refs[4 * n + 2]
        x, y, c = _place()
        for cp in _scatter_async_copies(p, land, send_sems, recv_sems, x, y, c)[0]:
            cp.start()
        token[...] = jnp.zeros_like(token)

    zones = [pltpu.with_memory_space_constraint(lax.empty(p.shape, p.dtype), pltpu.HBM) for p in parts]
    srcs = [pltpu.with_memory_space_constraint(p, pltpu.HBM) for p in parts]
    hbm = [pltpu.HBM(p.shape, p.dtype) for p in parts]
    out = pl.pallas_call(
        body, name=name,
        out_shape=[pltpu.SemaphoreType.DMA((3 * n,)), pltpu.SemaphoreType.DMA((3 * n,))] + hbm + hbm
        + [jax.ShapeDtypeStruct((8, LANES), f32)],
        in_specs=[HBM_SPEC] * (2 * n),
        out_specs=[SEM_SPEC, SEM_SPEC] + [HBM_SPEC] * (2 * n) + [pl.BlockSpec(memory_space=pltpu.VMEM)],
        input_output_aliases={i: 2 + i for i in range(2 * n)},
        compiler_params=pltpu.CompilerParams(has_side_effects=SPLIT_EFFECT),
    )(*srcs, *zones)
    return out[0], out[1], out[2:2 + n], out[2 + n:2 + 2 * n], out[-1]


def scatter_grads_wait(send_sems, recv_sems, parts, zones, after, *, name):
    n = len(parts)

    def body(*refs):
        p, land = refs[:n], refs[n:2 * n]
        x, y, c = _place()
        sends, arrivals = _scatter_async_copies(p, land, refs[2 * n], refs[2 * n + 1], x, y, c)
        for cp in sends:
            cp.wait_send()
        for cp in arrivals:
            cp.wait_recv()

    hbm = [pltpu.HBM(p.shape, p.dtype) for p in parts]
    out = pl.pallas_call(
        body, name=name, out_shape=hbm + hbm,
        in_specs=[HBM_SPEC] * (2 * n) + [SEM_SPEC, SEM_SPEC, pl.BlockSpec(memory_space=pl.ANY)],
        out_specs=[HBM_SPEC] * (2 * n),
        input_output_aliases={i: i for i in range(2 * n)},
        compiler_params=pltpu.CompilerParams(has_side_effects=SPLIT_EFFECT),
    )(*parts, *zones, send_sems, recv_sems, after)
    return out[:n], out[n:]


def sum_grads(part, land, order, *, name):
    _, half, cols = part.shape
    tr = GRAD_ROWS if half % GRAD_ROWS == 0 else half

    def body(order_ref, p_ref, l1_ref, l2_ref, l3_ref, o_ref):
        o_ref[...] = ((p_ref[0].astype(f32) + l1_ref[0].astype(f32)) + l2_ref[0].astype(f32)) + l3_ref[0].astype(f32)

    slot = lambda j: pl.BlockSpec((1, tr, cols), lambda i, order_ref: (order_ref[j], i, 0))
    return pl.pallas_call(
        body, name=name,
        grid_spec=pltpu.PrefetchScalarGridSpec(
            num_scalar_prefetch=1, grid=(half // tr,), in_specs=[slot(0), slot(1), slot(2), slot(3)],
            out_specs=pl.BlockSpec((tr, cols), lambda i, order_ref: (i, 0))),
        out_shape=jax.ShapeDtypeStruct((half, cols), f32),
        compiler_params=_cparams("parallel"),
    )(order, part, land, land, land)


def _peer(x, y, c, r):
    return ((1 - x) if r & 4 else x, (1 - y) if r & 2 else y, (1 - c) if r & 1 else c)


def _reduce_async_copies(grads, land, send_sems, recv_sems, x, y, c):
    me = 4 * x + 2 * y + c
    sends, arrivals = [], []
    for a in range(len(grads)):
        for r in range(1, N_DEV):
            px, py, pc = _peer(x, y, c, r)
            sems = (send_sems.at[7 * a + r - 1], recv_sems.at[7 * a + r - 1], (px, py, pc))
            sends.append(_remote(_half(grads[a].at[2 * px + py], pc), land[a].at[me], *sems))
            slot = land[a].at[4 * px + 2 * py + pc]
            arrivals.append(_remote(slot, slot, *sems))
    return sends, arrivals


def reduce_grads_start(grads, *, name):
    n = len(grads)

    def body(*refs):
        g, land = refs[:n], refs[n:2 * n]
        send_sems, recv_sems, token = refs[2 * n], refs[2 * n + 1], refs[4 * n + 2]
        x, y, c = _place()
        for cp in _reduce_async_copies(g, land, send_sems, recv_sems, x, y, c)[0]:
            cp.start()
        token[...] = jnp.zeros_like(token)

    zones = [pltpu.with_memory_space_constraint(lax.empty((N_DEV, g.shape[1] // 2, g.shape[2]), g.dtype), pltpu.HBM)
             for g in grads]
    srcs = [pltpu.with_memory_space_constraint(g, pltpu.HBM) for g in grads]
    out = pl.pallas_call(
        body, name=name,
        out_shape=[pltpu.SemaphoreType.DMA((7 * n,)), pltpu.SemaphoreType.DMA((7 * n,))]
        + [pltpu.HBM(g.shape, g.dtype) for g in grads] + [pltpu.HBM(z.shape, z.dtype) for z in zones]
        + [jax.ShapeDtypeStruct((8, LANES), f32)],
        in_specs=[HBM_SPEC] * (2 * n),
        out_specs=[SEM_SPEC, SEM_SPEC] + [HBM_SPEC] * (2 * n) + [pl.BlockSpec(memory_space=pltpu.VMEM)],
        input_output_aliases={i: 2 + i for i in range(2 * n)},
        compiler_params=pltpu.CompilerParams(has_side_effects=SPLIT_EFFECT),
    )(*srcs, *zones)
    return out[0], out[1], out[2:2 + n], out[2 + n:2 + 2 * n], out[-1]


def reduce_grads_wait(send_sems, recv_sems, grads, zones, after, *, name):
    n = len(grads)

    def body(*refs):
        g, land = refs[:n], refs[n:2 * n]
        x, y, c = _place()
        sends, arrivals = _reduce_async_copies(g, land, refs[2 * n], refs[2 * n + 1], x, y, c)
        for cp in sends:
            cp.wait_send()
        for cp in arrivals:
            cp.wait_recv()

    hbm = [pltpu.HBM(a.shape, a.dtype) for a in list(grads) + list(zones)]
    out = pl.pallas_call(
        body, name=name, out_shape=hbm,
        in_specs=[HBM_SPEC] * (2 * n) + [SEM_SPEC, SEM_SPEC, pl.BlockSpec(memory_space=pl.ANY)],
        out_specs=[HBM_SPEC] * (2 * n),
        input_output_aliases={i: i for i in range(2 * n)},
        compiler_params=pltpu.CompilerParams(has_side_effects=SPLIT_EFFECT),
    )(*grads, *zones, send_sems, recv_sems, after)
    return out[:n], out[n:]


def sum_partials(g, land, where, *, name):
    _, half, cols = land.shape
    tr = GRAD_ROWS if half % GRAD_ROWS == 0 else half
    nb = half // tr

    def body(where_ref, g_ref, *rest):
        o_ref = rest[-1]
        acc = g_ref[0].astype(f32)
        for l_ref in rest[:-1]:
            acc = acc + l_ref[0].astype(f32)
        o_ref[...] = acc

    blk = (1, tr, cols)
    slot = lambda j: pl.BlockSpec(blk, lambda i, where_ref: (where_ref[2 + j], i, 0))
    return pl.pallas_call(
        body, name=name,
        grid_spec=pltpu.PrefetchScalarGridSpec(
            num_scalar_prefetch=1, grid=(nb,),
            in_specs=[pl.BlockSpec(blk, lambda i, where_ref: (where_ref[0], where_ref[1] * nb + i, 0))]
            + [slot(j) for j in range(N_DEV - 1)],
            out_specs=pl.BlockSpec((tr, cols), lambda i, where_ref: (i, 0))),
        out_shape=jax.ShapeDtypeStruct((half, cols), f32),
        compiler_params=_cparams("parallel"),
    )(where, g, *([land] * (N_DEV - 1)))


def swap_reduced_halves(mine, *, name):
    n = len(mine)

    def body(*refs):
        r, out, send_sems, recv_sems = refs[:n], refs[n:2 * n], refs[2 * n], refs[2 * n + 1]
        x, y, c = _place()
        copies = [_remote(r[a], out[a], send_sems.at[a], recv_sems.at[a], (x, y, 1 - c)) for a in range(n)]
        for cp in copies:
            cp.start()
        for cp in copies:
            cp.wait()

    return pl.pallas_call(
        body, name=name, in_specs=[HBM_SPEC] * n, out_specs=[HBM_SPEC] * n,
        out_shape=[jax.ShapeDtypeStruct(r.shape, r.dtype) for r in mine],
        scratch_shapes=[pltpu.SemaphoreType.DMA((n,)), pltpu.SemaphoreType.DMA((n,))],
    )(*mine)


def adamw_halves(w, mine, theirs, m, v, core, *, name):
    R, C = w.shape
    tr = min(GRAD_ROWS, R // 2)
    half_nb = R // 2 // tr

    def body(c_ref, w_ref, a_ref, b_ref, m_ref, v_ref, g_ref, d_ref, nm_ref, nv_ref):
        low = pl.program_id(0) < half_nb
        gv = jnp.where(low == (c_ref[0] == 0), a_ref[...], b_ref[...])
        nm = ADAM_B1 * m_ref[...] + (1.0 - ADAM_B1) * gv
        nv = ADAM_B2 * v_ref[...] + (1.0 - ADAM_B2) * jnp.square(gv)
        m_hat = nm / (1.0 - ADAM_B1 ** ADAM_STEP)
        v_hat = nv / (1.0 - ADAM_B2 ** ADAM_STEP)
        g_ref[...] = gv
        d_ref[...] = -ADAM_LR * (m_hat / (jnp.sqrt(v_hat) + ADAM_EPS) + ADAM_WD * w_ref[...])
        nm_ref[...] = nm
        nv_ref[...] = nv

    full = pl.BlockSpec((tr, C), lambda i, c_ref: (i, 0))
    part = pl.BlockSpec((tr, C), lambda i, c_ref: (i % half_nb, 0))
    out = jax.ShapeDtypeStruct((R, C), f32)
    return pl.pallas_call(
        body, name=name,
        grid_spec=pltpu.PrefetchScalarGridSpec(
            num_scalar_prefetch=1, grid=(2 * half_nb,), in_specs=[full, part, part, full, full], out_specs=[full] * 4),
        out_shape=[out] * 4, compiler_params=_cparams("parallel"),
    )(core, w, mine, theirs, m, v)


N_DEV = 8


def all_reduce_small(v):
    def body(src_ref, out_ref, land_ref, send_sems, recv_sems):
        x, y, c = _place()
        me = 4 * x + 2 * y + c
        copies = []
        for r in range(1, N_DEV):
            peer = ((1 - x) if r & 4 else x, (1 - y) if r & 2 else y, (1 - c) if r & 1 else c)
            copies.append(_remote(src_ref, land_ref.at[r], send_sems.at[r - 1], recv_sems.at[r - 1], peer))
        for cp in copies:
            cp.start()
        land_ref[0] = src_ref[...]
        for cp in copies:
            cp.wait()
        acc = land_ref[me]
        for d in range(1, N_DEV):
            acc = acc + land_ref[jnp.bitwise_xor(me, d)]
        out_ref[...] = acc

    vm = pl.BlockSpec(memory_space=pltpu.VMEM)
    return pl.pallas_call(
        body, name="all_reduce_small", in_specs=[vm], out_specs=vm,
        out_shape=jax.ShapeDtypeStruct(v.shape, v.dtype),
        scratch_shapes=[pltpu.VMEM((N_DEV,) + v.shape, v.dtype),
                        pltpu.SemaphoreType.DMA((N_DEV - 1,)), pltpu.SemaphoreType.DMA((N_DEV - 1,))],
    )(v)


def adamw(w, g, m, v, *, name, tr=None, tc=None):
    R, C = w.shape
    if tc is None:
        tr, tc = min(tr, R), C
        blk = pl.BlockSpec((tr, C), lambda i: (i, 0))
    else:
        tr = R
        blk = pl.BlockSpec((R, tc), lambda i: (0, i))

    def body(w_ref, g_ref, m_ref, v_ref, d_ref, nm_ref, nv_ref):
        gv = g_ref[...]
        nm = ADAM_B1 * m_ref[...] + (1.0 - ADAM_B1) * gv
        nv = ADAM_B2 * v_ref[...] + (1.0 - ADAM_B2) * jnp.square(gv)
        m_hat = nm / (1.0 - ADAM_B1 ** ADAM_STEP)
        v_hat = nv / (1.0 - ADAM_B2 ** ADAM_STEP)
        d_ref[...] = -ADAM_LR * (m_hat / (jnp.sqrt(v_hat) + ADAM_EPS) + ADAM_WD * w_ref[...])
        nm_ref[...] = nm
        nv_ref[...] = nv

    out = jax.ShapeDtypeStruct((R, C), f32)
    return pl.pallas_call(
        body, name=name, grid=((R // tr) * (C // tc),), in_specs=[blk] * 4, out_specs=[blk] * 3, out_shape=[out] * 3,
        compiler_params=_cparams("parallel"),
    )(w, g, m, v)


BIG_SHARDS = (("w_in", (1024, 900), True), ("w_out", (256, 1024), False), ("w_cq", (256, 512), False),
              ("w_ckv", (256, 1024), False), ("w_co", (512, 256), True), ("w_mlp1", (1024, 1024), True),
              ("w_mlp2", (1024, 1024), False))
CONV_SHARD = (CONV_WIDTH, 3 * GDN_WIDTH // N_CHIPS)
SMALL_DIMS = (("norm_mix_g", 1024), ("fox_qnorm_g", 64), ("fox_knorm_g", 64), ("fox_f_bias", 8), ("fox_onorm_g", 64),
              ("gdn_A_log", 4), ("gdn_dt_bias", 4), ("gdn_onorm_g", 128), ("norm_xattn_g", 1024), ("mem_norm_g", 1024),
              ("xattn_qnorm_g", 128), ("xattn_knorm_g", 128), ("norm_mlp_g", 1024))
WEIGHT_ORDER = ("norm_mix_g", "w_in", "fox_qnorm_g", "fox_knorm_g", "fox_f_bias", "fox_onorm_g", "gdn_conv_w", "gdn_A_log",
                "gdn_dt_bias", "gdn_onorm_g", "w_out", "norm_xattn_g", "mem_norm_g", "w_cq", "w_ckv", "xattn_qnorm_g",
                "xattn_knorm_g", "w_co", "norm_mlp_g", "w_mlp1", "w_mlp2")


def _pack_rows(pieces, rows, lead=()):
    cat = jnp.concatenate([p.reshape(lead + (-1,)) for p in pieces], axis=-1)
    cat = jnp.pad(cat, [(0, 0)] * len(lead) + [(0, rows * LANES - cat.shape[-1])])
    return cat.reshape(lead + (rows, LANES))


def _unpack_rows(buf, sizes, lead=()):
    flat = buf.reshape(lead + (-1,))
    out, off = [], 0
    for n in sizes:
        out.append(flat[..., off:off + n])
        off += n
    return out


def _conv_to_wire(conv):
    return lax.bitcast_convert_type(conv, bf16)


def _conv_from_wire(wire):
    return lax.bitcast_convert_type(wire, f32)


SMALL_ROWS = 96
SMALL_ADAM_ROWS = 56


def kernel(x, mem, norm_mix_g, w_in, fox_qnorm_g, fox_knorm_g, fox_f_bias, fox_onorm_g, gdn_conv_w, gdn_A_log, gdn_dt_bias, gdn_onorm_g, w_out, norm_xattn_g, mem_norm_g, w_cq, w_ckv, xattn_qnorm_g, xattn_knorm_g, w_co, norm_mlp_g, w_mlp1, w_mlp2, loss_target, m_norm_mix_g, m_w_in, m_fox_qnorm_g, m_fox_knorm_g, m_fox_f_bias, m_fox_onorm_g, m_gdn_conv_w, m_gdn_A_log, m_gdn_dt_bias, m_gdn_onorm_g, m_w_out, m_norm_xattn_g, m_mem_norm_g, m_w_cq, m_w_ckv, m_xattn_qnorm_g, m_xattn_knorm_g, m_w_co, m_norm_mlp_g, m_w_mlp1, m_w_mlp2, v_norm_mix_g, v_w_in, v_fox_qnorm_g, v_fox_knorm_g, v_fox_f_bias, v_fox_onorm_g, v_gdn_conv_w, v_gdn_A_log, v_gdn_dt_bias, v_gdn_onorm_g, v_w_out, v_norm_xattn_g, v_mem_norm_g, v_w_cq, v_w_ckv, v_xattn_qnorm_g, v_xattn_knorm_g, v_w_co, v_norm_mlp_g, v_w_mlp1, v_w_mlp2):
    wts = dict(norm_mix_g=norm_mix_g, w_in=w_in, fox_qnorm_g=fox_qnorm_g, fox_knorm_g=fox_knorm_g, fox_f_bias=fox_f_bias,
               fox_onorm_g=fox_onorm_g, gdn_conv_w=gdn_conv_w, gdn_A_log=gdn_A_log, gdn_dt_bias=gdn_dt_bias,
               gdn_onorm_g=gdn_onorm_g, w_out=w_out, norm_xattn_g=norm_xattn_g, mem_norm_g=mem_norm_g, w_cq=w_cq, w_ckv=w_ckv,
               xattn_qnorm_g=xattn_qnorm_g, xattn_knorm_g=xattn_knorm_g, w_co=w_co, norm_mlp_g=norm_mlp_g, w_mlp1=w_mlp1,
               w_mlp2=w_mlp2)
    mom = dict(norm_mix_g=m_norm_mix_g, w_in=m_w_in, fox_qnorm_g=m_fox_qnorm_g, fox_knorm_g=m_fox_knorm_g,
               fox_f_bias=m_fox_f_bias, fox_onorm_g=m_fox_onorm_g, gdn_conv_w=m_gdn_conv_w, gdn_A_log=m_gdn_A_log,
               gdn_dt_bias=m_gdn_dt_bias, gdn_onorm_g=m_gdn_onorm_g, w_out=m_w_out, norm_xattn_g=m_norm_xattn_g,
               mem_norm_g=m_mem_norm_g, w_cq=m_w_cq, w_ckv=m_w_ckv, xattn_qnorm_g=m_xattn_qnorm_g,
               xattn_knorm_g=m_xattn_knorm_g, w_co=m_w_co, norm_mlp_g=m_norm_mlp_g, w_mlp1=m_w_mlp1, w_mlp2=m_w_mlp2)
    var = dict(norm_mix_g=v_norm_mix_g, w_in=v_w_in, fox_qnorm_g=v_fox_qnorm_g, fox_knorm_g=v_fox_knorm_g,
               fox_f_bias=v_fox_f_bias, fox_onorm_g=v_fox_onorm_g, gdn_conv_w=v_gdn_conv_w, gdn_A_log=v_gdn_A_log,
               gdn_dt_bias=v_gdn_dt_bias, gdn_onorm_g=v_gdn_onorm_g, w_out=v_w_out, norm_xattn_g=v_norm_xattn_g,
               mem_norm_g=v_mem_norm_g, w_cq=v_w_cq, w_ckv=v_w_ckv, xattn_qnorm_g=v_xattn_qnorm_g,
               xattn_knorm_g=v_xattn_knorm_g, w_co=v_w_co, norm_mlp_g=v_norm_mlp_g, w_mlp1=v_w_mlp1, w_mlp2=v_w_mlp2)
    B, S, D = x.shape
    T = B * S
    big_names = [n for n, _, _ in BIG_SHARDS]
    chip = 2 * lax.axis_index("x") + lax.axis_index("y")
    core = lax.axis_index("c").astype(jnp.int32).reshape(1)

    shards = {n: wts[n][0].astype(MXU_DTYPE) for n in big_names[1:]}
    in_t = lambda p: jnp.swapaxes(p[0], 0, 1)
    shards["w_in"] = jnp.pad(in_t(w_in).astype(MXU_DTYPE), ((0, IN_SHARD_PAD - IN_SHARD), (0, 0)))
    w_in_all, conv_all = gather_weights([shards["w_in"]], gdn_conv_w[0])
    late = big_names[1:]
    send_sems, recv_sems, late_src, late_zones, token = gather_weights_start([shards[n] for n in late], conv_all)
    own = lambda g, s: lax.dynamic_update_slice(g, s[None], (chip,) + (0,) * s.ndim)
    full = {"w_in": own(w_in_all, shards["w_in"])}
    conv_full = own(conv_all, gdn_conv_w[0]).transpose(1, 0, 2).reshape(CONV_WIDTH, 3 * GDN_WIDTH)
    rows = lambda g: g.reshape(N_CHIPS * g.shape[1], g.shape[2])

    def late_weights(after):
        zones = gather_weights_wait(send_sems, recv_sems, late_src, late_zones, after)
        got = {n: own(z, shards[n]) for n, z in zip(late, zones)}
        return dict(w_out=rows(got["w_out"]), w_cq=rows(got["w_cq"]), w_ckv=rows(got["w_ckv"]),
                    w_co=got["w_co"].transpose(1, 0, 2).reshape(XATTN_WIDTH, D_MODEL),
                    w_mlp1=got["w_mlp1"], w_mlp2=rows(got["w_mlp2"]))

    in_flight = []

    def grads_ready(ready):
        names = list(ready)
        *started, tok = reduce_grads_start([ready[n] for n in names], name="reduce_grads_start_%d" % len(in_flight))
        in_flight.append((names, *started))
        return tok

    w_in_t = full["w_in"][:, :IN_SHARD].reshape(IN_DIM, D_MODEL)
    w = dict(wa_t=align_w_in_t(w_in_t), conv_w=conv_full, late=late_weights, grads_ready=grads_ready)
    sp = {n: wts[n] for n, _ in SMALL_DIMS}
    sp["norm_mix_g"] = sp["norm_mix_g"] + token[0, 0]

    loss_part, grad_x, g_big, g_small = local_step(x.reshape(T, D), mem.reshape(-1, D), loss_target.reshape(T, D), w, sp, B=B)

    small_pieces = [g_small[n] for n, _ in SMALL_DIMS] + [g_small["gdn_conv_w"], loss_part]
    small_sizes = [d for _, d in SMALL_DIMS] + [CONV_WIDTH * 3 * GDN_WIDTH, LANES]
    red_small = _unpack_rows(all_reduce_small(_pack_rows(small_pieces, SMALL_ROWS)), small_sizes)
    grads = {n: p.reshape(1, d) for (n, d), p in zip(SMALL_DIMS, red_small)}
    conv_grad = lax.dynamic_slice(red_small[-2].reshape(CONV_WIDTH, 3 * GDN_WIDTH), (0, chip * CONV_SHARD[1]), CONV_SHARD)
    grads["gdn_conv_w"] = conv_grad.reshape((1,) + CONV_SHARD)
    loss = red_small[-1][0]

    parts, zones = {}, {}

    def wait_group(k, after):
        names, send_sems, recv_sems, thru, land = in_flight[k]
        thru, land = reduce_grads_wait(send_sems, recv_sems, thru, land, after, name="reduce_grads_wait_%d" % k)
        parts.update(zip(names, thru))
        zones.update(zip(names, land))

    wait_group(0, grad_x)
    wait_group(1, grad_x)
    dev = 2 * chip + core[0]
    where = jnp.stack([chip, core[0]] + [dev ^ r for r in range(1, N_DEV)]).astype(jnp.int32)
    mine = [sum_partials(parts[n], zones[n], where, name="sum_partials_" + n) for n in late]
    theirs = swap_reduced_halves(mine, name="swap_reduced_halves")

    delta, new_m, new_v = {}, {}, {}
    for n, a, b in zip(late, mine, theirs):
        g, d, nm, nv = adamw_halves(wts[n][0], a, b, mom[n][0], var[n][0], core, name="adamw_" + n)
        grads[n], delta[n], new_m[n], new_v[n] = g[None], d[None], nm[None], nv[None]
    wait_group(2, new_v[late[-1]])
    mine_in = sum_partials(parts["w_in"], zones["w_in"], where, name="sum_partials_w_in")
    (theirs_in,) = swap_reduced_halves([mine_in], name="swap_reduced_halves_w_in")
    south = core[0] == 0
    g_in_t = jnp.concatenate([jnp.where(south, mine_in, theirs_in), jnp.where(south, theirs_in, mine_in)])[:IN_SHARD]
    back = lambda t: jnp.swapaxes(t, 0, 1)[None]
    d, nm, nv = adamw(in_t(w_in), g_in_t, in_t(m_w_in), in_t(v_w_in), name="adamw_w_in", tc=256)
    grads["w_in"], delta["w_in"], new_m["w_in"], new_v["w_in"] = back(g_in_t), back(d), back(nm), back(nv)
    small_names = [n for n, _ in SMALL_DIMS] + ["gdn_conv_w"]
    small_sz = [d for _, d in SMALL_DIMS] + [CONV_SHARD[0] * CONV_SHARD[1]]
    packed4 = [_pack_rows([src[n] for n in small_names], SMALL_ADAM_ROWS) for src in (wts, grads, mom, var)]
    outs = adamw(*packed4, name="adamw_small", tr=SMALL_ADAM_ROWS)
    for dst, buf in zip((delta, new_m, new_v), outs):
        for n, p in zip(small_names, _unpack_rows(buf, small_sz)):
            dst[n] = p.reshape(wts[n].shape)

    return (loss, grad_x.reshape(B, S, D), *[grads[n] for n in WEIGHT_ORDER], *[delta[n] for n in WEIGHT_ORDER],
            *[new_m[n] for n in WEIGHT_ORDER], *[new_v[n] for n in WEIGHT_ORDER])
```

```python
import functools

import jax
import jax.numpy as jnp
import numpy as np
from jax import lax
from jax.experimental import pallas as pl
from jax.experimental.pallas import tpu as pltpu

f32 = jnp.float32
bf16 = jnp.bfloat16
MXU_DTYPE = jnp.bfloat16
WIRE_DTYPE = jnp.bfloat16
INV_PRECISION = lax.Precision.HIGH

D_MODEL = 1024
FOX_HEADS = 8
FOX_HEAD_DIM = 64
FOX_WIDTH = 512
GDN_HEADS = 4
GDN_HEAD_DIM = 128
GDN_WIDTH = 512
CONV_WIDTH = 4
GDN_CHUNK = 64
XATTN_HEADS = 4
XATTN_HEAD_DIM = 128
XATTN_WIDTH = 512
D_FF = 4096
IN_DIM = 3600
EPS = 1e-6
NEG_INF = -1e30
LANES = 128
ADAM_LR = 0.001
ADAM_B1 = 0.9
ADAM_B2 = 0.999
ADAM_EPS = 1e-08
ADAM_WD = 0.01
ADAM_STEP = 10
VMEM_LIMIT = 48 * 1024 * 1024

COL_FOX = 0
COL_GDN = 1536
COL_Z = 3072
COL_SMALL = 3584
IN_ALIGNED = 3840
IN_TILE = 768
SM_F = 0
SM_B = 8
SM_A = 12


def _cparams(*sem):
    return pltpu.CompilerParams(dimension_semantics=sem, vmem_limit_bytes=VMEM_LIMIT)


def _mx(v):
    return v.astype(MXU_DTYPE)


def _dot(a, b, dims, precision=None):
    return lax.dot_general(a, b, (dims, ((), ())), preferred_element_type=f32, precision=precision)


def _dotm(a, b, dims):
    return _dot(_mx(a), _mx(b), dims)


NN = ((1,), (0,))
NT = ((1,), (1,))
TN = ((0,), (0,))


def matmul(a, b, *, name, ta=False, tb=False, b_stacked=False, out_stacked=False, residual=None, relu2_out=False,
           relu2_bwd_aux=None, out_dtype=f32, tm=1024, tn=1024, tk=1024):
    M, K = (a.shape[1], a.shape[0]) if ta else a.shape
    if b_stacked:
        b_cols = b.shape[2]
        N, tk = (b.shape[1], min(tk, b_cols)) if tb else (N_CHIPS * b_cols, tk)
        tn = tn if tb else min(tn, b_cols)
        assert K == (N_CHIPS * b_cols if tb else b.shape[1]), (name, a.shape, b.shape)
    else:
        N = b.shape[0] if tb else b.shape[1]
    if out_stacked:
        tn = min(tn, N // N_CHIPS)
    tm, tn, tk = min(tm, M), min(tn, N), min(tk, K)
    assert M % tm == 0 and N % tn == 0 and K % tk == 0, (name, M, N, K)
    nk = K // tk
    has_res = residual is not None
    has_aux = relu2_bwd_aux is not None

    def body(*refs):
        a_ref, b_ref = refs[0], refs[1]
        pos = 2
        res_ref = aux_ref = None
        if has_res:
            res_ref = refs[pos]
            pos += 1
        if has_aux:
            aux_ref = refs[pos]
            pos += 1
        o_ref = refs[pos]
        k = pl.program_id(2)
        dims = ((0,) if ta else (1,), (1,) if tb else (0,))
        part = _dot(_mx(a_ref[...]), _mx(b_ref[...]), dims)

        def finish(r):
            if has_res:
                r = r + res_ref[...]
            if has_aux:
                r = r * (2.0 * jnp.sqrt(aux_ref[...].astype(f32)))
            if relu2_out:
                o_ref[...] = jnp.square(jnp.maximum(r, 0.0)).astype(o_ref.dtype)
            else:
                o_ref[...] = r.astype(o_ref.dtype)

        if nk == 1:
            finish(part)
            return
        acc_ref = refs[pos + 1]

        @pl.when(k == 0)
        def _():
            acc_ref[...] = part

        @pl.when((k > 0) & (k < nk - 1))
        def _():
            acc_ref[...] += part

        @pl.when(k == nk - 1)
        def _():
            finish(acc_ref[...] + part)

    a_spec = pl.BlockSpec((tk, tm), lambda i, j, k: (k, i)) if ta else pl.BlockSpec((tm, tk), lambda i, j, k: (i, k))
    if b_stacked and tb:
        per = b_cols // tk
        b_spec = pl.BlockSpec((None, tn, tk), lambda i, j, k: (k // per, j, k % per))
    elif b_stacked:
        per = b_cols // tn
        b_spec = pl.BlockSpec((None, tk, tn), lambda i, j, k: (j // per, k, j % per))
    else:
        b_spec = pl.BlockSpec((tn, tk), lambda i, j, k: (j, k)) if tb else pl.BlockSpec((tk, tn), lambda i, j, k: (k, j))
    if out_stacked:
        assert not (has_res or has_aux or relu2_out), name
        per_o = N // N_CHIPS // tn
        o_spec = pl.BlockSpec((None, tm, tn), lambda i, j, k: (j // per_o, i, j % per_o))
        out_full = (N_CHIPS, M, N // N_CHIPS)
    else:
        o_spec = pl.BlockSpec((tm, tn), lambda i, j, k: (i, j))
        out_full = (M, N)
    in_specs, args = [a_spec, b_spec], [a, b]
    if has_res:
        in_specs.append(o_spec)
        args.append(residual)
    if has_aux:
        in_specs.append(o_spec)
        args.append(relu2_bwd_aux)
    out_shape = [jax.ShapeDtypeStruct(out_full, out_dtype)]
    out_specs = [o_spec]
    res = pl.pallas_call(
        body, name=name, grid=(M // tm, N // tn, nk), in_specs=in_specs, out_specs=out_specs, out_shape=out_shape,
        scratch_shapes=[pltpu.VMEM((tm, tn), f32)] if nk > 1 else [],
        compiler_params=_cparams("parallel", "parallel", "arbitrary"),
    )(*args)
    return res[0]


def matmul_rows(a, b, extras, *, name, mode, tb=False, b_stacked=False, tm=1024, tk=1024):
    M, K = a.shape
    N = D_MODEL
    if b_stacked:
        assert tb, name
        tk = min(tk, b.shape[2])
        per = b.shape[2] // tk
        b_spec = pl.BlockSpec((None, N, tk), lambda i, k: (k // per, 0, k % per))
    elif tb:
        tk = min(tk, K)
        b_spec = pl.BlockSpec((N, tk), lambda i, k: (0, k))
    else:
        tk = min(tk, K)
        b_spec = pl.BlockSpec((tk, N), lambda i, k: (k, 0))
    tm = min(tm, M)
    assert M % tm == 0 and K % tk == 0, (name, M, K)
    nk = K // tk
    extras = [e for e in extras if e is not None]
    n_ex = len(extras)

    def body(*refs):
        a_ref, b_ref = refs[0], refs[1]
        ex = refs[2:2 + n_ex]
        o_ref = refs[2 + n_ex]
        n_out = 3 if mode == "loss" else 2
        s_ref = refs[1 + n_ex + n_out]
        i, k = pl.program_id(0), pl.program_id(1)
        part = _dot(_mx(a_ref[...]), _mx(b_ref[...]), ((1,), (1,) if tb else (0,)))

        def finish(y):
            if mode == "rms_fwd":
                y = y + ex[0][...]
                o_ref[...] = y
                s_ref[...] = (y * lax.rsqrt(jnp.mean(y * y, axis=-1, keepdims=True) + EPS) * ex[1][...]).astype(MXU_DTYPE)
                return

            @pl.when(i == 0)
            def _():
                s_ref[...] = jnp.zeros_like(s_ref)

            if mode == "rms_bwd":
                xv, gv = ex[0][...], ex[1][...]
                rstd = lax.rsqrt(jnp.mean(xv * xv, axis=-1, keepdims=True) + EPS)
                xhat = xv * rstd
                gd = y * gv
                dx = rstd * (gd - xhat * jnp.mean(gd * xhat, axis=-1, keepdims=True))
                o_ref[...] = dx + ex[2][...] if n_ex == 3 else dx
                s_ref[...] += jnp.sum(y * xhat, axis=0, keepdims=True)
            else:
                e = y + ex[0][...] - ex[1][...]
                o_ref[...] = e * (1.0 / N)
                refs[3 + n_ex][...] = (e * (1.0 / N)).astype(MXU_DTYPE)
                tot = 0.5 * jnp.sum(jnp.mean(e * e, axis=-1, keepdims=True), axis=0, keepdims=True)
                s_ref[...] += jnp.broadcast_to(tot, s_ref.shape)

        if nk == 1:
            finish(part)
            return
        acc_ref = refs[2 + n_ex + n_out]

        @pl.when(k == 0)
        def _():
            acc_ref[...] = part

        @pl.when((k > 0) & (k < nk - 1))
        def _():
            acc_ref[...] += part

        @pl.when(k == nk - 1)
        def _():
            finish(acc_ref[...] + part)

    row = pl.BlockSpec((tm, N), lambda i, k: (i, 0))
    vec = pl.BlockSpec((1, N), lambda i, k: (0, 0))
    if mode == "rms_bwd":
        ex_specs = [row, vec] + ([row] if n_ex == 3 else [])
        s_shape, s_spec = jax.ShapeDtypeStruct((1, N), f32), vec
    elif mode == "rms_fwd":
        ex_specs = [row, vec]
        s_shape, s_spec = jax.ShapeDtypeStruct((M, N), MXU_DTYPE), row
    else:
        ex_specs = [row, row]
        s_shape, s_spec = jax.ShapeDtypeStruct((1, LANES), f32), pl.BlockSpec((1, LANES), lambda i, k: (0, 0))
    return pl.pallas_call(
        body, name=name, grid=(M // tm, nk),
        in_specs=[pl.BlockSpec((tm, tk), lambda i, k: (i, k)), b_spec] + ex_specs,
        out_specs=[row] * (2 if mode == "loss" else 1) + [s_spec],
        out_shape=[jax.ShapeDtypeStruct((M, N), f32)] + ([jax.ShapeDtypeStruct((M, N), MXU_DTYPE)] if mode == "loss" else [])
        + [s_shape],
        scratch_shapes=[pltpu.VMEM((tm, N), f32)] if nk > 1 else [],
        compiler_params=_cparams("arbitrary", "arbitrary"),
    )(a, b, *extras)


def rms_fwd(x, g, *, name, tr=1024):
    R, D = x.shape
    tr = min(tr, R)

    def body(x_ref, g_ref, o_ref):
        xv = x_ref[...]
        y = xv * lax.rsqrt(jnp.mean(xv * xv, axis=-1, keepdims=True) + EPS)
        o_ref[...] = (y * g_ref[...]).astype(o_ref.dtype)

    return pl.pallas_call(
        body, name=name, grid=(R // tr,),
        in_specs=[pl.BlockSpec((tr, D), lambda i: (i, 0)), pl.BlockSpec((1, D), lambda i: (0, 0))],
        out_specs=pl.BlockSpec((tr, D), lambda i: (i, 0)),
        out_shape=jax.ShapeDtypeStruct((R, D), MXU_DTYPE),
        compiler_params=_cparams("parallel"),
    )(x, g)


def rms_bwd(x, g, dh, residual, *, name, tr=512):
    R, D = x.shape
    tr = min(tr, R)
    has_res = residual is not None

    def body(*refs):
        if has_res:
            x_ref, g_ref, dh_ref, res_ref, dx_ref, dg_ref = refs
        else:
            x_ref, g_ref, dh_ref, dx_ref, dg_ref = refs
        xv = x_ref[...]
        rstd = lax.rsqrt(jnp.mean(xv * xv, axis=-1, keepdims=True) + EPS)
        xhat = xv * rstd
        dh = dh_ref[...].astype(f32)
        gd = dh * g_ref[...]
        dx = rstd * (gd - xhat * jnp.mean(gd * xhat, axis=-1, keepdims=True))
        if has_res:
            dx = dx + res_ref[...]
        dx_ref[...] = dx

        @pl.when(pl.program_id(0) == 0)
        def _():
            dg_ref[...] = jnp.zeros_like(dg_ref)

        dg_ref[...] += jnp.sum(dh * xhat, axis=0, keepdims=True)

    row = pl.BlockSpec((tr, D), lambda i: (i, 0))
    vec = pl.BlockSpec((1, D), lambda i: (0, 0))
    in_specs = [row, vec, row] + ([row] if has_res else [])
    args = [x, g, dh] + ([residual] if has_res else [])
    return pl.pallas_call(
        body, name=name, grid=(R // tr,), in_specs=in_specs, out_specs=[row, vec],
        out_shape=[jax.ShapeDtypeStruct((R, D), f32), jax.ShapeDtypeStruct((1, D), f32)],
        compiler_params=_cparams("arbitrary"),
    )(*args)


def loss_head(y, target, *, tr=512):
    R, D = y.shape
    tr = min(tr, R)

    def body(y_ref, t_ref, dy_ref, loss_ref):
        e = y_ref[...] - t_ref[...]
        dy_ref[...] = e * (1.0 / D)

        @pl.when(pl.program_id(0) == 0)
        def _():
            loss_ref[...] = jnp.zeros_like(loss_ref)

        part = 0.5 * jnp.sum(jnp.mean(e * e, axis=-1, keepdims=True), axis=0, keepdims=True)
        loss_ref[...] += jnp.broadcast_to(part, loss_ref.shape)

    row = pl.BlockSpec((tr, D), lambda i: (i, 0))
    return pl.pallas_call(
        body, name="loss_head", grid=(R // tr,), in_specs=[row, row],
        out_specs=[row, pl.BlockSpec((1, LANES), lambda i: (0, 0))],
        out_shape=[jax.ShapeDtypeStruct((R, D), f32), jax.ShapeDtypeStruct((1, LANES), f32)],
        compiler_params=_cparams("arbitrary"),
    )(y, target)


def _head_rms(v, g):
    r = lax.rsqrt(jnp.mean(v * v, axis=-1, keepdims=True) + EPS)
    return v * r * g, r


def _head_rms_bwd(v, r, g, dn):
    vhat = v * r
    gd = dn * g
    dv = r * (gd - vhat * jnp.mean(gd * vhat, axis=-1, keepdims=True))
    return dv, jnp.sum(dn * vhat, axis=0, keepdims=True)


def _softmax_rows(s):
    m = jnp.max(s, axis=-1, keepdims=True)
    e = jnp.exp(s - m)
    return e / jnp.sum(e, axis=-1, keepdims=True)


def xattn_fwd(cq, ckv, gq, gk, *, B, tq=1024):
    T = cq.shape[0]
    S = T // B
    M = ckv.shape[0] // B
    tq = min(tq, S)
    nq = S // tq
    hd, W = XATTN_HEAD_DIM, XATTN_WIDTH
    scale = hd ** -0.5

    def body(q_ref, k_ref, v_ref, gq_ref, gk_ref, o_ref):
        for h in range(XATTN_HEADS):
            sl = slice(h * hd, (h + 1) * hd)
            qn, _ = _head_rms(q_ref[:, sl], gq_ref[...])
            kn, _ = _head_rms(k_ref[:, sl], gk_ref[...])
            p = _softmax_rows(_dot(_mx(qn), _mx(kn), NT) * scale)
            o_ref[:, sl] = _dot(_mx(p), _mx(v_ref[:, sl]), NN).astype(o_ref.dtype)

    vec = pl.BlockSpec((1, hd), lambda b, i: (0, 0))
    qspec = pl.BlockSpec((tq, W), lambda b, i: (b * nq + i, 0))
    return pl.pallas_call(
        body, name="xattn_fwd", grid=(B, nq),
        in_specs=[qspec, pl.BlockSpec((M, W), lambda b, i: (b, 0)), pl.BlockSpec((M, W), lambda b, i: (b, 1)), vec, vec],
        out_specs=qspec, out_shape=jax.ShapeDtypeStruct((T, W), MXU_DTYPE),
        compiler_params=_cparams("parallel", "parallel"),
    )(cq, ckv, ckv, gq, gk)


def xattn_bwd(cq, ckv, gq, gk, dco, *, B, tq=1024):
    T = cq.shape[0]
    S = T // B
    M = ckv.shape[0] // B
    tq = min(tq, S)
    nq = S // tq
    hd, W = XATTN_HEAD_DIM, XATTN_WIDTH
    scale = hd ** -0.5

    def body(q_ref, k_ref, v_ref, gq_ref, gk_ref, do_ref, dq_ref, dkv_ref, dgq_ref, dgk_ref, dkn_acc, dv_acc):
        b, i = pl.program_id(0), pl.program_id(1)

        @pl.when((b == 0) & (i == 0))
        def _():
            dgq_ref[...] = jnp.zeros_like(dgq_ref)
            dgk_ref[...] = jnp.zeros_like(dgk_ref)

        @pl.when(i == 0)
        def _():
            dkn_acc[...] = jnp.zeros_like(dkn_acc)
            dv_acc[...] = jnp.zeros_like(dv_acc)

        gqv, gkv = gq_ref[...], gk_ref[...]
        for h in range(XATTN_HEADS):
            sl = slice(h * hd, (h + 1) * hd)
            q, k, v = q_ref[:, sl], k_ref[:, sl], v_ref[:, sl]
            qn, rq = _head_rms(q, gqv)
            kn, _ = _head_rms(k, gkv)
            p = _softmax_rows(_dot(_mx(qn), _mx(kn), NT) * scale)
            do = do_ref[:, sl]
            dv_acc[:, sl] += _dot(_mx(p), _mx(do), TN)
            dp = _dot(_mx(do), _mx(v), NT)
            ds = p * (dp - jnp.sum(dp * p, axis=-1, keepdims=True)) * scale
            dqn = _dot(_mx(ds), _mx(kn), NN)
            dkn_acc[:, sl] += _dot(_mx(ds), _mx(qn), TN)
            dq, dgq = _head_rms_bwd(q, rq, gqv, dqn)
            dq_ref[:, sl] = dq.astype(dq_ref.dtype)
            dgq_ref[...] += dgq

        @pl.when(i == nq - 1)
        def _():
            for h in range(XATTN_HEADS):
                sl = slice(h * hd, (h + 1) * hd)
                k = k_ref[:, sl]
                rk = lax.rsqrt(jnp.mean(k * k, axis=-1, keepdims=True) + EPS)
                dk, dgk = _head_rms_bwd(k, rk, gkv, dkn_acc[:, sl])
                dkv_ref[:, sl] = dk.astype(dkv_ref.dtype)
                dkv_ref[:, slice(W + h * hd, W + (h + 1) * hd)] = dv_acc[:, sl].astype(dkv_ref.dtype)
                dgk_ref[...] += dgk

    vec = pl.BlockSpec((1, hd), lambda b, i: (0, 0))
    qspec = pl.BlockSpec((tq, W), lambda b, i: (b * nq + i, 0))
    return pl.pallas_call(
        body, name="xattn_bwd", grid=(B, nq),
        in_specs=[qspec, pl.BlockSpec((M, W), lambda b, i: (b, 0)), pl.BlockSpec((M, W), lambda b, i: (b, 1)), vec, vec, qspec],
        out_specs=[qspec, pl.BlockSpec((M, 2 * W), lambda b, i: (b, 0)), vec, vec],
        out_shape=[jax.ShapeDtypeStruct((T, W), MXU_DTYPE), jax.ShapeDtypeStruct((B * M, 2 * W), MXU_DTYPE),
                   jax.ShapeDtypeStruct((1, hd), f32), jax.ShapeDtypeStruct((1, hd), f32)],
        scratch_shapes=[pltpu.VMEM((M, W), f32), pltpu.VMEM((M, W), f32)],
        compiler_params=_cparams("arbitrary", "arbitrary"),
    )(cq, ckv, ckv, gq, gk, dco)


FOX_PAIRS = FOX_HEADS // 2


def _fox_scores(qn, kn, ccol, crow, q0, tq, S, scale):
    s = _dot(_mx(qn), _mx(kn), NT) * scale + ccol - crow
    qpos = q0 + lax.broadcasted_iota(jnp.int32, (tq, S), 0)
    kpos = lax.broadcasted_iota(jnp.int32, (tq, S), 1)
    return jnp.where(kpos <= qpos, s, NEG_INF)


def fox_fwd(P, ccol, crow, gq, gk, go, *, B, tq=256):
    T = P.shape[0]
    S = T // B
    tq = min(tq, S)
    nq = S // tq
    hd = FOX_HEAD_DIM
    scale = hd ** -0.5

    def body(q_ref, k_ref, v_ref, ccol_ref, crow_ref, gq_ref, gk_ref, go_ref, o_ref, oa_ref):
        q0 = pl.program_id(2) * tq
        for e in range(2):
            sl = slice(e * hd, (e + 1) * hd)
            qn, _ = _head_rms(q_ref[:, sl], gq_ref[:, sl])
            kn, _ = _head_rms(k_ref[:, sl], gk_ref[:, sl])
            p = _softmax_rows(_fox_scores(qn, kn, ccol_ref[0, e], crow_ref[0, e], q0, tq, S, scale))
            o = _dot(_mx(p), _mx(v_ref[:, sl]), NN)
            o_ref[:, sl] = o
            oa_ref[:, sl] = _head_rms(o, go_ref[:, sl])[0].astype(oa_ref.dtype)

    W = 2 * hd
    vec = pl.BlockSpec((1, W), lambda b, h, i: (0, 0))
    ospec = pl.BlockSpec((tq, W), lambda b, h, i: (b * nq + i, h))
    return pl.pallas_call(
        body, name="fox_fwd", grid=(B, FOX_PAIRS, nq),
        in_specs=[pl.BlockSpec((tq, W), lambda b, h, i: (b * nq + i, h)),
                  pl.BlockSpec((S, W), lambda b, h, i: (b, FOX_PAIRS + h)),
                  pl.BlockSpec((S, W), lambda b, h, i: (b, 2 * FOX_PAIRS + h)),
                  pl.BlockSpec((1, 2, tq, 1), lambda b, h, i: (b, h, i, 0)),
                  pl.BlockSpec((1, 2, 1, S), lambda b, h, i: (b, h, 0, 0)), vec, vec, vec],
        out_specs=[ospec, ospec],
        out_shape=[jax.ShapeDtypeStruct((T, FOX_WIDTH), f32), jax.ShapeDtypeStruct((T, FOX_WIDTH), MXU_DTYPE)],
        compiler_params=_cparams("parallel", "parallel", "parallel"),
    )(P, P, P, ccol, crow, gq, gk, go)


def fox_bwd(P, ccol, crow, gq, gk, go, o_raw, d_oab, *, B, tq=256):
    T = P.shape[0]
    S = T // B
    tq = min(tq, S)
    nq = S // tq
    hd = FOX_HEAD_DIM
    scale = hd ** -0.5

    def body(q_ref, k_ref, v_ref, ccol_ref, crow_ref, gq_ref, gk_ref, go_ref, o_ref, doa_ref,
             dq_ref, dk_ref, dv_ref, dccol_ref, dcrow_ref, dgq_ref, dgk_ref, dgo_ref, dkn_acc, dv_acc, dcrow_acc):
        b, h, i = pl.program_id(0), pl.program_id(1), pl.program_id(2)
        q0 = i * tq

        @pl.when((b == 0) & (h == 0) & (i == 0))
        def _():
            dgq_ref[...] = jnp.zeros_like(dgq_ref)
            dgk_ref[...] = jnp.zeros_like(dgk_ref)
            dgo_ref[...] = jnp.zeros_like(dgo_ref)

        @pl.when(i == 0)
        def _():
            dkn_acc[...] = jnp.zeros_like(dkn_acc)
            dv_acc[...] = jnp.zeros_like(dv_acc)
            dcrow_acc[...] = jnp.zeros_like(dcrow_acc)

        for e in range(2):
            sl = slice(e * hd, (e + 1) * hd)
            q, k, v = q_ref[:, sl], k_ref[:, sl], v_ref[:, sl]
            gqv, gkv, gov = gq_ref[:, sl], gk_ref[:, sl], go_ref[:, sl]
            qn, rq = _head_rms(q, gqv)
            kn, rk = _head_rms(k, gkv)
            p = _softmax_rows(_fox_scores(qn, kn, ccol_ref[0, e], crow_ref[0, e], q0, tq, S, scale))
            o = o_ref[:, sl]
            ro = lax.rsqrt(jnp.mean(o * o, axis=-1, keepdims=True) + EPS)
            do, dgo = _head_rms_bwd(o, ro, gov, doa_ref[:, sl])
            dgo_ref[:, sl] += dgo
            dv_acc[e] += _dot(_mx(p), _mx(do), TN)
            dp = _dot(_mx(do), _mx(v), NT)
            ds = p * (dp - jnp.sum(do * o, axis=-1, keepdims=True))
            dccol_ref[0, e] = jnp.sum(ds, axis=1, keepdims=True)
            dcrow_acc[e] -= jnp.sum(ds, axis=0, keepdims=True)
            dqn = _dot(_mx(ds), _mx(kn), NN) * scale
            dkn_acc[e] += _dot(_mx(ds), _mx(qn), TN) * scale
            dq, dgq = _head_rms_bwd(q, rq, gqv, dqn)
            dq_ref[:, sl] = dq.astype(dq_ref.dtype)
            dgq_ref[:, sl] += dgq

        @pl.when(i == nq - 1)
        def _():
            for e in range(2):
                sl = slice(e * hd, (e + 1) * hd)
                k = k_ref[:, sl]
                gkv = gk_ref[:, sl]
                rk = lax.rsqrt(jnp.mean(k * k, axis=-1, keepdims=True) + EPS)
                dk, dgk = _head_rms_bwd(k, rk, gkv, dkn_acc[e])
                dk_ref[:, sl] = dk.astype(dk_ref.dtype)
                dv_ref[:, sl] = dv_acc[e].astype(dv_ref.dtype)
                dgk_ref[:, sl] += dgk
                dcrow_ref[0, e] = dcrow_acc[e]

    W = 2 * hd
    vec = pl.BlockSpec((1, W), lambda b, h, i: (0, 0))
    qspec = pl.BlockSpec((tq, W), lambda b, h, i: (b * nq + i, h))
    kvout = pl.BlockSpec((S, W), lambda b, h, i: (b, h))
    colspec = pl.BlockSpec((1, 2, tq, 1), lambda b, h, i: (b, h, i, 0))
    rowspec = pl.BlockSpec((1, 2, 1, S), lambda b, h, i: (b, h, 0, 0))
    return pl.pallas_call(
        body, name="fox_bwd", grid=(B, FOX_PAIRS, nq),
        in_specs=[qspec,
                  pl.BlockSpec((S, W), lambda b, h, i: (b, FOX_PAIRS + h)),
                  pl.BlockSpec((S, W), lambda b, h, i: (b, 2 * FOX_PAIRS + h)),
                  colspec, rowspec, vec, vec, vec, qspec, qspec],
        out_specs=[qspec, kvout, kvout, colspec, rowspec, vec, vec, vec],
        out_shape=[jax.ShapeDtypeStruct((T, FOX_WIDTH), MXU_DTYPE), jax.ShapeDtypeStruct((T, FOX_WIDTH), MXU_DTYPE),
                   jax.ShapeDtypeStruct((T, FOX_WIDTH), MXU_DTYPE),
                   jax.ShapeDtypeStruct((B, FOX_HEADS, S, 1), f32), jax.ShapeDtypeStruct((B, FOX_HEADS, 1, S), f32),
                   jax.ShapeDtypeStruct((1, W), f32), jax.ShapeDtypeStruct((1, W), f32), jax.ShapeDtypeStruct((1, W), f32)],
        scratch_shapes=[pltpu.VMEM((2, S, hd), f32), pltpu.VMEM((2, S, hd), f32), pltpu.VMEM((2, 1, S), f32)],
        compiler_params=_cparams("arbitrary", "arbitrary", "arbitrary"),
    )(P, P, P, ccol, crow, gq, gk, go, o_raw, d_oab)


FOX_TQ = 512
FOX_TK = FOX_TQ
GROUP_PRECISION = lax.Precision.HIGH


def _head_mean(v):
    n = v.shape[1]
    r = lax.broadcasted_iota(jnp.int32, (n, n), 0) // FOX_HEAD_DIM
    c = lax.broadcasted_iota(jnp.int32, (n, n), 1) // FOX_HEAD_DIM
    ones = (r == c).astype(bf16)
    hi = v.astype(bf16)
    lo = (v - hi.astype(f32)).astype(bf16)
    return (_dot(hi, ones, NN) + _dot(lo, ones, NN)) * (1.0 / FOX_HEAD_DIM)


def fox_prep_fwd(P, gq, gk, *, tr=1024):
    T = P.shape[0]
    tr = min(tr, T)
    scale = FOX_HEAD_DIM ** -0.5

    def body(q_ref, k_ref, v_ref, gq_ref, gk_ref, qn_ref, kn_ref, vb_ref):
        q, k = q_ref[...], k_ref[...]
        qn_ref[...] = (q * lax.rsqrt(_head_mean(q * q) + EPS) * (gq_ref[...] * scale)).astype(qn_ref.dtype)
        kn_ref[...] = (k * lax.rsqrt(_head_mean(k * k) + EPS) * gk_ref[...]).astype(kn_ref.dtype)
        vb_ref[...] = v_ref[...].astype(vb_ref.dtype)

    W = FOX_WIDTH
    col = lambda j: pl.BlockSpec((tr, W), lambda i: (i, j))
    vec = pl.BlockSpec((1, W), lambda i: (0, 0))
    out = jax.ShapeDtypeStruct((T, W), MXU_DTYPE)
    return pl.pallas_call(
        body, name="fox_prep_fwd", grid=(T // tr,), in_specs=[col(0), col(1), col(2), vec, vec],
        out_specs=[col(0)] * 3, out_shape=[out] * 3, compiler_params=_cparams("parallel"),
    )(P, P, P, gq, gk)


def fox_prep_bwd(P, gq, gk, dqn, dkn, *, tr=1024):
    T = P.shape[0]
    tr = min(tr, T)
    scale = FOX_HEAD_DIM ** -0.5

    def body(q_ref, k_ref, gq_ref, gk_ref, dqn_ref, dkn_ref, dq_ref, dk_ref, dgq_ref, dgk_ref):
        @pl.when(pl.program_id(0) == 0)
        def _():
            dgq_ref[...] = jnp.zeros_like(dgq_ref)
            dgk_ref[...] = jnp.zeros_like(dgk_ref)

        def one(x, g, dn, dx_ref, dg_ref):
            r = lax.rsqrt(_head_mean(x * x) + EPS)
            xhat = x * r
            gd = dn * g
            dx_ref[...] = (r * (gd - xhat * _head_mean(gd * xhat))).astype(dx_ref.dtype)
            return jnp.sum(dn * xhat, axis=0, keepdims=True)

        dgq_ref[...] += scale * one(q_ref[...], gq_ref[...] * scale, dqn_ref[...], dq_ref, dgq_ref)
        dgk_ref[...] += one(k_ref[...], gk_ref[...], dkn_ref[...], dk_ref, dgk_ref)

    W = FOX_WIDTH
    col = lambda j: pl.BlockSpec((tr, W), lambda i: (i, j))
    vec = pl.BlockSpec((1, W), lambda i: (0, 0))
    return pl.pallas_call(
        body, name="fox_prep_bwd", grid=(T // tr,), in_specs=[col(0), col(1), vec, vec, col(0), col(0)],
        out_specs=[col(0), col(0), vec, vec],
        out_shape=[jax.ShapeDtypeStruct((T, W), MXU_DTYPE), jax.ShapeDtypeStruct((T, W), MXU_DTYPE),
                   jax.ShapeDtypeStruct((1, W), f32), jax.ShapeDtypeStruct((1, W), f32)],
        compiler_params=_cparams("arbitrary"),
    )(P, P, gq, gk, dqn, dkn)


def _fox_tile_scores(q, k_ref, ccol_ref, cq, e, j, sl, mask_off):
    tq, tk = FOX_TQ, FOX_TK
    rows = pl.ds(pl.multiple_of(j * tk, tk), tk)
    k = k_ref[rows, sl]
    s = _dot(k, q, NT) + cq - ccol_ref[0, e, rows, :]
    if mask_off is not None:
        key = lax.broadcasted_iota(jnp.int32, (tk, tq), 0) + mask_off
        query = lax.broadcasted_iota(jnp.int32, (tk, tq), 1)
        s = jnp.where(key <= query, s, NEG_INF)
    return s, k, rows


def _fox_sweep(i, update, carry):
    nd = FOX_TQ // FOX_TK
    carry = lax.fori_loop(0, i * nd, lambda j, cr: update(cr, j, None), carry)
    for d in range(nd):
        carry = update(carry, i * nd + d, d * FOX_TK)
    return carry


def fox_core_fwd(qn, kn, vb, ccol, crow, go, *, B):
    T = qn.shape[0]
    S = T // B
    tq = FOX_TQ
    nq = S // tq
    hd = FOX_HEAD_DIM

    def body(q_ref, k_ref, v_ref, ccol_ref, crow_ref, go_ref, o_ref, oa_ref, lse_ref):
        i = pl.program_id(2)
        for e in range(2):
            sl = slice(e * hd, (e + 1) * hd)
            q = q_ref[:, sl]
            cq = crow_ref[0, e, i]

            def update(carry, j, mask_off):
                m, l, acc = carry
                s, _, rows = _fox_tile_scores(q, k_ref, ccol_ref, cq, e, j, sl, mask_off)
                m2 = jnp.maximum(m, jnp.max(s, axis=0, keepdims=True))
                a = jnp.exp(m - m2)
                p = jnp.exp(s - m2)
                return m2, a * l + jnp.sum(p, axis=0, keepdims=True), a * acc + _dot(v_ref[rows, sl], _mx(p), TN)

            carry = (jnp.full((1, tq), NEG_INF, f32), jnp.zeros((1, tq), f32), jnp.zeros((hd, tq), f32))
            m, l, acc = _fox_sweep(i, update, carry)
            o = (acc / l).T
            o_ref[:, sl] = o
            oa_ref[:, sl] = _head_rms(o, go_ref[:, sl])[0].astype(oa_ref.dtype)
            lse_ref[0, e, 0] = m + jnp.log(l)

    W = 2 * hd
    qspec = pl.BlockSpec((tq, W), lambda b, h, i: (b * nq + i, h))
    kspec = pl.BlockSpec((S, W), lambda b, h, i: (b, h))
    return pl.pallas_call(
        body, name="fox_core_fwd", grid=(B, FOX_PAIRS, nq),
        in_specs=[qspec, kspec, kspec, pl.BlockSpec((1, 2, S, 1), lambda b, h, i: (b, h, 0, 0)),
                  pl.BlockSpec((1, 2, nq, 1, tq), lambda b, h, i: (b, h, 0, 0, 0)),
                  pl.BlockSpec((1, W), lambda b, h, i: (0, 0))],
        out_specs=[qspec, qspec, pl.BlockSpec((1, 2, 1, 1, tq), lambda b, h, i: (b, h, i, 0, 0))],
        out_shape=[jax.ShapeDtypeStruct((T, FOX_WIDTH), f32), jax.ShapeDtypeStruct((T, FOX_WIDTH), MXU_DTYPE),
                   jax.ShapeDtypeStruct((B, FOX_HEADS, nq, 1, tq), f32)],
        compiler_params=_cparams("parallel", "parallel", "parallel"),
    )(qn, kn, vb, ccol, crow, go)


def fox_core_bwd(qn, kn, vb, ccol, crow, go, o_raw, lse, d_oab, *, B):
    T = qn.shape[0]
    S = T // B
    tq = FOX_TQ
    nq = S // tq
    hd = FOX_HEAD_DIM

    def body(q_ref, k_ref, v_ref, ccol_ref, crow_ref, go_ref, o_ref, lse_ref, doa_ref,
             dq_ref, dk_ref, dv_ref, dckey_ref, dcrow_ref, dgo_ref, dk_acc, dv_acc, dck_acc):
        b, h, i = pl.program_id(0), pl.program_id(1), pl.program_id(2)

        @pl.when((b == 0) & (h == 0) & (i == 0))
        def _():
            dgo_ref[...] = jnp.zeros_like(dgo_ref)

        @pl.when(i == 0)
        def _():
            dk_acc[...] = jnp.zeros_like(dk_acc)
            dv_acc[...] = jnp.zeros_like(dv_acc)
            dck_acc[...] = jnp.zeros_like(dck_acc)

        for e in range(2):
            sl = slice(e * hd, (e + 1) * hd)
            q = q_ref[:, sl]
            cq = crow_ref[0, e, i]
            lse_e = lse_ref[0, e, 0]
            o = o_ref[:, sl]
            ro = lax.rsqrt(jnp.mean(o * o, axis=-1, keepdims=True) + EPS)
            do, dgo = _head_rms_bwd(o, ro, go_ref[:, sl], doa_ref[:, sl])
            dgo_ref[:, sl] += dgo
            delta = jnp.sum((do * o).T, axis=0, keepdims=True)
            do_b = _mx(do)

            def update(carry, j, mask_off):
                dq, dcq = carry
                s, k, rows = _fox_tile_scores(q, k_ref, ccol_ref, cq, e, j, sl, mask_off)
                p = jnp.exp(s - lse_e)
                dv_acc[e, rows, :] += _dot(_mx(p), do_b, NN)
                ds = p * (_dot(v_ref[rows, sl], do_b, NT) - delta)
                dck_acc[e, rows, :] -= jnp.sum(ds, axis=1, keepdims=True)
                ds_b = _mx(ds)
                dk_acc[e, rows, :] += _dot(ds_b, q, NN)
                return dq + _dot(ds_b, k, TN), dcq + jnp.sum(ds, axis=0, keepdims=True)

            dq, dcq = _fox_sweep(i, update, (jnp.zeros((tq, hd), f32), jnp.zeros((1, tq), f32)))
            dq_ref[:, sl] = dq
            dcrow_ref[0, e, 0] = dcq

        @pl.when(i == nq - 1)
        def _():
            for e in range(2):
                sl = slice(e * hd, (e + 1) * hd)
                dk_ref[:, sl] = dk_acc[e]
                dv_ref[:, sl] = dv_acc[e].astype(dv_ref.dtype)
                dckey_ref[0, e] = jnp.transpose(jnp.broadcast_to(dck_acc[e], (S, LANES)))[0:1, :]

    W = 2 * hd
    qspec = pl.BlockSpec((tq, W), lambda b, h, i: (b * nq + i, h))
    kspec = pl.BlockSpec((S, W), lambda b, h, i: (b, h))
    colspec = pl.BlockSpec((1, 2, S, 1), lambda b, h, i: (b, h, 0, 0))
    rowspec = pl.BlockSpec((1, 2, nq, 1, tq), lambda b, h, i: (b, h, 0, 0, 0))
    tilespec = pl.BlockSpec((1, 2, 1, 1, tq), lambda b, h, i: (b, h, i, 0, 0))
    vec = pl.BlockSpec((1, W), lambda b, h, i: (0, 0))
    return pl.pallas_call(
        body, name="fox_core_bwd", grid=(B, FOX_PAIRS, nq),
        in_specs=[qspec, kspec, kspec, colspec, rowspec, vec, qspec, tilespec, qspec],
        out_specs=[qspec, kspec, kspec, pl.BlockSpec((1, 2, 1, S), lambda b, h, i: (b, h, 0, 0)), tilespec, vec],
        out_shape=[jax.ShapeDtypeStruct((T, FOX_WIDTH), f32), jax.ShapeDtypeStruct((T, FOX_WIDTH), f32),
                   jax.ShapeDtypeStruct((T, FOX_WIDTH), MXU_DTYPE),
                   jax.ShapeDtypeStruct((B, FOX_HEADS, 1, S), f32), jax.ShapeDtypeStruct((B, FOX_HEADS, nq, 1, tq), f32),
                   jax.ShapeDtypeStruct((1, W), f32)],
        scratch_shapes=[pltpu.VMEM((2, S, hd), f32), pltpu.VMEM((2, S, hd), f32), pltpu.VMEM((2, S, 1), f32)],
        compiler_params=_cparams("arbitrary", "arbitrary", "arbitrary"),
    )(qn, kn, vb, ccol, crow, go, o_raw, lse, d_oab)


def _lane_mask(lo, hi, shape):
    lane = lax.broadcasted_iota(jnp.int32, shape, 1)
    return (lane >= lo) & (lane < hi)


def _cumsum_rows(v, period, reverse=False):
    n = v.shape[0]
    pos = lax.broadcasted_iota(jnp.int32, v.shape, 0) % period
    sh = 1
    while sh < period:
        if reverse:
            v = v + jnp.where(pos + sh < period, pltpu.roll(v, n - sh, 0), 0.0)
        else:
            v = v + jnp.where(pos >= sh, pltpu.roll(v, sh, 0), 0.0)
        sh *= 2
    return v


def _gate_values(z, bias, alog):
    zb = z + bias
    ls = jax.nn.log_sigmoid(zb)
    beta = jax.nn.sigmoid(z)
    g = -jnp.exp(alog) * jax.nn.softplus(zb)
    return zb, ls, beta, g


def gates_fwd(P, bias, alog, *, B):
    T = P.shape[0]
    S = T // B

    def body(z_ref, bias_ref, alog_ref, o_ref):
        z = z_ref[...]
        _, ls, beta, g = _gate_values(z, bias_ref[...], alog_ref[...])
        c = _cumsum_rows(ls, S)
        gc = _cumsum_rows(g, GDN_CHUNK)
        o = jnp.where(_lane_mask(SM_F, SM_F + FOX_HEADS, z.shape), c, 0.0)
        o = jnp.where(_lane_mask(SM_B, SM_B + GDN_HEADS, z.shape), beta, o)
        o = jnp.where(_lane_mask(SM_A, SM_A + GDN_HEADS, z.shape), gc, o)
        o_ref[...] = o

    vec = pl.BlockSpec((1, LANES), lambda b: (0, 0))
    return pl.pallas_call(
        body, name="gates_fwd", grid=(B,),
        in_specs=[pl.BlockSpec((S, LANES), lambda b: (b, COL_SMALL // LANES)), vec, vec],
        out_specs=pl.BlockSpec((S, LANES), lambda b: (b, 0)),
        out_shape=jax.ShapeDtypeStruct((T, LANES), f32),
        compiler_params=_cparams("parallel"),
    )(P, bias, alog)


def gates_bwd(P, bias, alog, dgates, *, B):
    T = P.shape[0]
    S = T // B

    def body(z_ref, bias_ref, alog_ref, dg_ref, dz_ref, par_ref):
        z = z_ref[...]
        zb, ls, beta, g = _gate_values(z, bias_ref[...], alog_ref[...])
        d = dg_ref[...]
        dls = _cumsum_rows(d, S, reverse=True)
        dgr = _cumsum_rows(d, GDN_CHUNK, reverse=True)
        sig = jax.nn.sigmoid(zb)
        dz_f = dls * (1.0 - sig)
        dz_b = d * beta * (1.0 - beta)
        dz_a = dgr * (-jnp.exp(alog_ref[...])) * sig
        dz = jnp.where(_lane_mask(SM_F, SM_F + FOX_HEADS, z.shape), dz_f, 0.0)
        dz = jnp.where(_lane_mask(SM_B, SM_B + GDN_HEADS, z.shape), dz_b, dz)
        dz = jnp.where(_lane_mask(SM_A, SM_A + GDN_HEADS, z.shape), dz_a, dz)
        dz_ref[...] = dz.astype(dz_ref.dtype)

        @pl.when(pl.program_id(0) == 0)
        def _():
            par_ref[...] = jnp.zeros_like(par_ref)

        dalog = jnp.where(_lane_mask(SM_A, SM_A + GDN_HEADS, z.shape), dgr * g, 0.0)
        par_ref[0:1, :] += jnp.sum(dz, axis=0, keepdims=True)
        par_ref[1:2, :] += jnp.sum(dalog, axis=0, keepdims=True)

    vec = pl.BlockSpec((1, LANES), lambda b: (0, 0))
    return pl.pallas_call(
        body, name="gates_bwd", grid=(B,),
        in_specs=[pl.BlockSpec((S, LANES), lambda b: (b, COL_SMALL // LANES)), vec, vec,
                  pl.BlockSpec((S, LANES), lambda b: (b, 0))],
        out_specs=[pl.BlockSpec((S, LANES), lambda b: (b, 0)), pl.BlockSpec((8, LANES), lambda b: (0, 0))],
        out_shape=[jax.ShapeDtypeStruct((T, LANES), MXU_DTYPE), jax.ShapeDtypeStruct((8, LANES), f32)],
        compiler_params=_cparams("arbitrary"),
    )(P, bias, alog, dgates)


GDN_BLOCKS = 3 * GDN_HEADS


def _shift_rows(v, d, reverse=False):
    if d == 0:
        return v
    n = v.shape[0]
    row = lax.broadcasted_iota(jnp.int32, v.shape, 0)
    if reverse:
        return jnp.where(row + d < n, pltpu.roll(v, n - d, 0), 0.0)
    return jnp.where(row >= d, pltpu.roll(v, d, 0), 0.0)


def _conv_silu(x, w):
    pre = sum(w[j:j + 1, :] * _shift_rows(x, CONV_WIDTH - 1 - j) for j in range(CONV_WIDTH))
    return pre, pre * jax.nn.sigmoid(pre)


def gdn_prep_fwd(P, conv_w, *, B):
    T = P.shape[0]
    S = T // B

    def body(x_ref, w_ref, o_ref):
        _, y = _conv_silu(x_ref[...], w_ref[...])
        yn = y * lax.rsqrt(jnp.sum(y * y, axis=-1, keepdims=True) + EPS)
        o_ref[...] = jnp.where(pl.program_id(1) < 2 * GDN_HEADS, yn, y)

    return pl.pallas_call(
        body, name="gdn_prep_fwd", grid=(B, GDN_BLOCKS),
        in_specs=[pl.BlockSpec((S, LANES), lambda b, j: (b, COL_GDN // LANES + j)),
                  pl.BlockSpec((CONV_WIDTH, LANES), lambda b, j: (0, j))],
        out_specs=pl.BlockSpec((S, LANES), lambda b, j: (b, j)),
        out_shape=jax.ShapeDtypeStruct((T, 3 * GDN_WIDTH), f32),
        compiler_params=_cparams("parallel", "parallel"),
    )(P, conv_w)


def gdn_prep_bwd(P, conv_w, dGq, dGk, dGv, *, B):
    T = P.shape[0]
    S = T // B
    H = GDN_HEADS

    def body(x_ref, w_ref, dq_ref, dk_ref, dv_ref, dx_ref, dw_ref):
        x, w = x_ref[...], w_ref[...]
        pre, y = _conv_silu(x, w)
        jb = pl.program_id(0)
        dn = jnp.where(jb < H, dq_ref[...], jnp.where(jb < 2 * H, dk_ref[...], dv_ref[...]))
        r = lax.rsqrt(jnp.sum(y * y, axis=-1, keepdims=True) + EPS)
        n = y * r
        dy_norm = r * (dn - n * jnp.sum(dn * n, axis=-1, keepdims=True))
        dy = jnp.where(pl.program_id(0) < 2 * GDN_HEADS, dy_norm, dn)
        sg = jax.nn.sigmoid(pre)
        dpre = dy * (sg * (1.0 + pre * (1.0 - sg)))
        dx = sum(w[j:j + 1, :] * _shift_rows(dpre, CONV_WIDTH - 1 - j, reverse=True) for j in range(CONV_WIDTH))
        dx_ref[...] = dx.astype(dx_ref.dtype)

        @pl.when(pl.program_id(1) == 0)
        def _():
            dw_ref[...] = jnp.zeros_like(dw_ref)

        for j in range(CONV_WIDTH):
            dw_ref[j:j + 1, :] += jnp.sum(dpre * _shift_rows(x, CONV_WIDTH - 1 - j), axis=0, keepdims=True)

    return pl.pallas_call(
        body, name="gdn_prep_bwd", grid=(GDN_BLOCKS, B),
        in_specs=[pl.BlockSpec((S, LANES), lambda j, b: (b, COL_GDN // LANES + j)),
                  pl.BlockSpec((CONV_WIDTH, LANES), lambda j, b: (0, j))]
        + [pl.BlockSpec((S, LANES), lambda j, b, t=t: (b, jnp.clip(j - t * H, 0, H - 1))) for t in range(3)],
        out_specs=[pl.BlockSpec((S, LANES), lambda j, b: (b, j)),
                   pl.BlockSpec((CONV_WIDTH, LANES), lambda j, b: (0, j))],
        out_shape=[jax.ShapeDtypeStruct((T, 3 * GDN_WIDTH), MXU_DTYPE),
                   jax.ShapeDtypeStruct((CONV_WIDTH, 3 * GDN_WIDTH), f32)],
        compiler_params=_cparams("arbitrary", "arbitrary"),
    )(P, conv_w, dGq, dGk, dGv)


GDN_GROUP = 16
GDN_GROUP_FWD = 16
B_NN = (((2,), (1,)), ((0,), (0,)))
B_NT = (((2,), (2,)), ((0,), (0,)))
B_TN = (((1,), (1,)), ((0,), (0,)))


def _bmm(a, b, dims, precision=None):
    if precision is None:
        a, b = _mx(a), _mx(b)
    return lax.dot_general(a, b, dims, preferred_element_type=f32, precision=precision)


def _tri_inverse(A):
    C = A.shape[-1]
    row = lax.broadcasted_iota(jnp.int32, A.shape, 1)
    col = lax.broadcasted_iota(jnp.int32, A.shape, 2)
    eye = (row == col).astype(f32)
    X = jnp.where((row // 4) == (col // 4), -A, 0.0)
    X2 = _bmm(X, X, B_NN, INV_PRECISION)
    Tm = eye + X + X2 + _bmm(X, X2, B_NN, INV_PRECISION)
    b = 4
    while b < C:
        off = ((row // (2 * b)) == (col // (2 * b))) & ((row // b) != (col // b))
        Tm = Tm - _bmm(_bmm(Tm, jnp.where(off, A, 0.0), B_NN, INV_PRECISION), Tm, B_NN, INV_PRECISION)
        b *= 2
    return Tm


def _pick_lane(block, lane_idx):
    lane = lax.broadcasted_iota(jnp.int32, block.shape, 1)
    return jnp.sum(jnp.where(lane == lane_idx, block, 0.0), axis=1, keepdims=True)


def _gdn_local(q, k, v, beta, gc, Tm=None, uwm=None):
    C = GDN_CHUNK
    n = q.shape[0] // C
    q = q.reshape(n, C, -1) * (GDN_HEAD_DIM ** -0.5)
    k = k.reshape(n, C, -1)
    v = v.reshape(n, C, -1)
    beta = beta.reshape(n, C, 1)
    gc = gc.reshape(n, C, 1)
    row = lax.broadcasted_iota(jnp.int32, (n, C, C), 1)
    col = lax.broadcasted_iota(jnp.int32, (n, C, C), 2)
    gcT = jnp.swapaxes(jnp.broadcast_to(gc, (n, C, C)), 1, 2)
    D = jnp.exp(jnp.where(row >= col, gc - gcT, NEG_INF))
    kb = k * beta
    vb = v * beta
    A = jnp.where(row > col, _bmm(kb, k, B_NT) * D, 0.0)
    Gam = jnp.exp(gc)
    kg = kb * Gam
    gl = gc[:, C - 1:C, :]
    kdec = jnp.exp(gl - gc)
    loc = dict(q=q, k=k, v=v, beta=beta, gc=gc, D=D, kb=kb, vb=vb, A=A, Gam=Gam, kg=kg,
               kdec=kdec, kd=k * kdec, qg=q * Gam, gam=jnp.exp(gl), row=row, col=col)
    uwm = Tm is None if uwm is None else uwm
    Tm = _tri_inverse(A) if Tm is None else Tm.reshape(n, C, C)
    if uwm:
        loc.update(u=_bmm(Tm, vb, B_NN), w=_bmm(Tm, kg, B_NN), M=_bmm(q, k, B_NT) * D)
    loc["Tm"] = Tm
    return loc


def _gdn_store_local(loc, r0, u_s, w_s, qg_s, kd_s, M_s, gam_s, c0):
    n = loc["u"].shape[0]
    R = n * GDN_CHUNK
    u_s[pl.ds(r0, R), :] = loc["u"].reshape(R, -1)
    w_s[pl.ds(r0, R), :] = loc["w"].reshape(R, -1)
    qg_s[pl.ds(r0, R), :] = loc["qg"].reshape(R, -1)
    kd_s[pl.ds(r0, R), :] = loc["kd"].reshape(R, -1)
    M_s[pl.ds(r0, R), :] = loc["M"].reshape(R, -1)
    gam_s[pl.ds(c0, n)] = jnp.broadcast_to(loc["gam"], (n, 1, LANES))


def _gdn_specs(S):
    blk = lambda off: pl.BlockSpec((S, LANES), lambda b, h: (b, off + h))
    return blk


def gdn_fwd(G, gates, P, g_on, *, B):
    T = G.shape[0]
    S = T // B
    C = GDN_CHUNK
    N = S // C
    grp = min(GDN_GROUP_FWD, N)
    R = grp * C
    hd = GDN_HEAD_DIM

    def body(q_ref, k_ref, v_ref, gt_ref, z_ref, gon_ref, o_ref, ob_ref, st_ref, tm_ref, A_s, B_s, Q_s, O_s, gam_s):
        h = pl.program_id(1)

        def local(gi, carry):
            r0 = pl.multiple_of(gi * R, R)
            gt = gt_ref[pl.ds(r0, R), :]
            loc = _gdn_local(q_ref[pl.ds(r0, R), :], k_ref[pl.ds(r0, R), :], v_ref[pl.ds(r0, R), :],
                             _pick_lane(gt, SM_B + h), _pick_lane(gt, SM_A + h))
            chunks = pl.ds(gi * grp, grp)
            tm_ref[0, 0, pl.ds(r0, R), :] = loc["Tm"].reshape(R, C)
            A_s[chunks] = -_bmm(loc["kd"], loc["w"], B_TN)
            B_s[chunks] = _bmm(loc["kd"], loc["u"], B_TN)
            Q_s[pl.ds(r0, R), :] = (loc["qg"] - _bmm(loc["M"], loc["w"], B_NN)).reshape(R, hd)
            O_s[pl.ds(r0, R), :] = _bmm(loc["M"], loc["u"], B_NN).reshape(R, hd)
            gam_s[chunks] = jnp.broadcast_to(loc["gam"], (grp, 1, LANES))
            return carry

        lax.fori_loop(0, N // grp, local, 0)

        def step(n, state):
            st_ref[0, 0, n] = state
            return state * gam_s[n] + _dotm(A_s[n], state, NN) + B_s[n]

        lax.fori_loop(0, N, step, jnp.zeros((hd, hd), f32))

        def outputs(gi, carry):
            r0 = pl.multiple_of(gi * R, R)
            Q = Q_s[pl.ds(r0, R), :].reshape(grp, C, hd)
            o = _bmm(Q, st_ref[0, 0, pl.ds(gi * grp, grp)], B_NN).reshape(R, hd) + O_s[pl.ds(r0, R), :]
            o_ref[pl.ds(r0, R), :] = o
            return carry

        lax.fori_loop(0, N // grp, outputs, 0)
        o = o_ref[...]
        z = z_ref[...]
        ob_ref[...] = (_head_rms(o, gon_ref[...])[0] * (z * jax.nn.sigmoid(z))).astype(ob_ref.dtype)

    blk = lambda off: pl.BlockSpec((S, LANES), lambda b, h: (b, off + h))
    rows = lambda: pltpu.VMEM((S, hd), f32)
    return pl.pallas_call(
        body, name="gdn_fwd", grid=(B, GDN_HEADS),
        in_specs=[blk(0), blk(GDN_HEADS), blk(2 * GDN_HEADS), pl.BlockSpec((S, LANES), lambda b, h: (b, 0)),
                  blk(COL_Z // LANES), pl.BlockSpec((1, hd), lambda b, h: (0, 0))],
        out_specs=[blk(0), blk(0), pl.BlockSpec((1, 1, N, hd, hd), lambda b, h: (b, h, 0, 0, 0)),
                   pl.BlockSpec((1, 1, S, C), lambda b, h: (b, h, 0, 0))],
        out_shape=[jax.ShapeDtypeStruct((T, GDN_WIDTH), f32), jax.ShapeDtypeStruct((T, GDN_WIDTH), MXU_DTYPE),
                   jax.ShapeDtypeStruct((B, GDN_HEADS, N, hd, hd), f32), jax.ShapeDtypeStruct((B, GDN_HEADS, S, C), f32)],
        scratch_shapes=[pltpu.VMEM((N, hd, hd), f32), pltpu.VMEM((N, hd, hd), f32), rows(), rows(),
                        pltpu.VMEM((N, 1, LANES), f32)],
        compiler_params=_cparams("parallel", "parallel"),
    )(G, G, G, gates, P, g_on)


def gdn_bwd(G, gates, P, g_on, o_raw, states, tm, d_oab, *, B):
    T = G.shape[0]
    S = T // B
    C = GDN_CHUNK
    N = S // C
    grp = min(GDN_GROUP, N)
    R = grp * C
    hd = GDN_HEAD_DIM

    def body(q_ref, k_ref, v_ref, gt_ref, z_ref, gon_ref, o_ref, st_ref, tm_ref, dob_ref,
             dq_ref, dk_ref, dv_ref, dgt_ref, dz_ref, dgon_ref,
             u_s, w_s, M_s, gam_s, do_s, A_s, C_s, dst_s):
        b, h = pl.program_id(0), pl.program_id(1)

        @pl.when((b == 0) & (h == 0))
        def _():
            dgon_ref[...] = jnp.zeros_like(dgon_ref)

        @pl.when(h == 0)
        def _():
            dgt_ref[...] = jnp.zeros_like(dgt_ref)

        def group_inputs(gi, uwm):
            r0 = pl.multiple_of(gi * R, R)
            gt = gt_ref[pl.ds(r0, R), :]
            return r0, _gdn_local(q_ref[pl.ds(r0, R), :], k_ref[pl.ds(r0, R), :], v_ref[pl.ds(r0, R), :],
                                  _pick_lane(gt, SM_B + h), _pick_lane(gt, SM_A + h), tm_ref[0, 0, pl.ds(r0, R), :], uwm)

        def local(gi, carry):
            r0, loc = group_inputs(gi, True)
            rows, chunks = pl.ds(r0, R), pl.ds(gi * grp, grp)
            u_s[rows, :] = loc["u"].reshape(R, hd)
            w_s[rows, :] = loc["w"].reshape(R, hd)
            M_s[rows, :] = loc["M"].reshape(R, C)
            gam_s[chunks] = jnp.broadcast_to(loc["gam"], (grp, 1, LANES))
            o, z, gon = o_ref[rows, :], z_ref[rows, :], gon_ref[...]
            dob = dob_ref[rows, :]
            on, ro = _head_rms(o, gon)
            sz = jax.nn.sigmoid(z)
            dz_ref[rows, :] = (dob * on * (sz * (1.0 + z * (1.0 - sz)))).astype(dz_ref.dtype)
            do, dgon = _head_rms_bwd(o, ro, gon, dob * (z * sz))
            do_s[rows, :] = do
            dgon_ref[...] += dgon
            A_s[chunks] = -_bmm(loc["kd"], loc["w"], B_TN)
            C_s[chunks] = _bmm(loc["qg"] - _bmm(loc["M"], loc["w"], B_NN), do.reshape(grp, C, hd), B_TN)
            return carry

        lax.fori_loop(0, N // grp, local, 0)

        def step(t, dS):
            n = N - 1 - t
            dst_s[n] = dS
            return dS * gam_s[n] + _dotm(A_s[n], dS, TN) + C_s[n]

        lax.fori_loop(0, N, step, jnp.zeros((hd, hd), f32))

        def finish(gi, carry):
            r0, L = group_inputs(gi, False)
            n = grp
            rows, chunks = pl.ds(r0, R), pl.ds(gi * grp, grp)
            g3 = lambda ref: ref[rows, :].reshape(n, C, -1)
            u, w, do = g3(u_s), g3(w_s), g3(do_s)
            L["M"] = g3(M_s)
            state, dS = st_ref[0, 0, chunks], dst_s[chunks]
            v_new = u - _bmm(w, state, B_NN)
            du = _bmm(L["M"], do, B_TN) + _bmm(L["kd"], dS, B_NN)
            dw = -_bmm(du, state, B_NT)
            dqg = _bmm(do, state, B_NT)
            dM = _bmm(do, v_new, B_NT)
            dkd = _bmm(v_new, dS, B_NT)
            dgl_state = jnp.sum(jnp.sum(dS * state, axis=2, keepdims=True), axis=1, keepdims=True) * L["gam"]
            TmT = jnp.swapaxes(L["Tm"], 1, 2)
            dTm = _bmm(du, L["vb"], B_NT) + _bmm(dw, L["kg"], B_NT)
            dvb = _bmm(TmT, du, B_NN)
            dkg = _bmm(TmT, dw, B_NN)
            dA = jnp.where(L["row"] > L["col"], -_bmm(_bmm(TmT, dTm, B_NN), TmT, B_NN), 0.0)
            dKK = dA * L["D"]
            dQK = dM * L["D"]
            dkb = _bmm(dKK, L["k"], B_NN) + dkg * L["Gam"]
            dk = (_bmm(dKK, L["kb"], B_TN) + _bmm(dQK, L["q"], B_TN) + dkd * L["kdec"] + L["beta"] * dkb)
            dq = (_bmm(dQK, L["k"], B_NN) + dqg * L["Gam"]) * (GDN_HEAD_DIM ** -0.5)
            E = dA * L["A"] + dM * L["M"]
            r = jnp.sum(dkd * L["kd"], axis=-1, keepdims=True)
            dgc = (jnp.sum(E, axis=2, keepdims=True) - jnp.sum(jnp.swapaxes(E, 1, 2), axis=2, keepdims=True)
                   + jnp.sum(dkg * L["kg"], axis=-1, keepdims=True) + jnp.sum(dqg * L["qg"], axis=-1, keepdims=True) - r)
            dgl = jnp.sum(r, axis=1, keepdims=True) + dgl_state
            rowc = lax.broadcasted_iota(jnp.int32, (n, C, 1), 1)
            dgc = dgc + jnp.where(rowc == C - 1, dgl, 0.0)
            dbeta = jnp.sum(dkb * L["k"], axis=-1, keepdims=True) + jnp.sum(dvb * L["v"], axis=-1, keepdims=True)
            dq_ref[rows, :] = dq.reshape(R, hd)
            dk_ref[rows, :] = dk.reshape(R, hd)
            dv_ref[rows, :] = (L["beta"] * dvb).reshape(R, hd)
            lane = lax.broadcasted_iota(jnp.int32, (R, LANES), 1)
            dgt_ref[rows, :] += (jnp.where(lane == SM_B + h, dbeta.reshape(R, 1), 0.0)
                                 + jnp.where(lane == SM_A + h, dgc.reshape(R, 1), 0.0))
            return carry

        lax.fori_loop(0, N // grp, finish, 0)

    blk = lambda off: pl.BlockSpec((S, LANES), lambda b, h: (b, off + h))
    rows = lambda: pltpu.VMEM((S, hd), f32)
    return pl.pallas_call(
        body, name="gdn_bwd", grid=(B, GDN_HEADS),
        in_specs=[blk(0), blk(GDN_HEADS), blk(2 * GDN_HEADS), pl.BlockSpec((S, LANES), lambda b, h: (b, 0)),
                  blk(COL_Z // LANES), pl.BlockSpec((1, hd), lambda b, h: (0, 0)), blk(0),
                  pl.BlockSpec((1, 1, N, hd, hd), lambda b, h: (b, h, 0, 0, 0)),
                  pl.BlockSpec((1, 1, S, C), lambda b, h: (b, h, 0, 0)), blk(GDN_HEADS)],
        out_specs=[blk(0), blk(0), blk(0), pl.BlockSpec((S, LANES), lambda b, h: (b, 0)), blk(0),
                   pl.BlockSpec((1, hd), lambda b, h: (0, 0))],
        out_shape=[jax.ShapeDtypeStruct((T, GDN_WIDTH), f32), jax.ShapeDtypeStruct((T, GDN_WIDTH), f32),
                   jax.ShapeDtypeStruct((T, GDN_WIDTH), f32), jax.ShapeDtypeStruct((T, LANES), f32),
                   jax.ShapeDtypeStruct((T, GDN_WIDTH), MXU_DTYPE), jax.ShapeDtypeStruct((1, hd), f32)],
        scratch_shapes=[rows(), rows(), pltpu.VMEM((S, C), f32), pltpu.VMEM((N, 1, LANES), f32), rows(),
                        pltpu.VMEM((N, hd, hd), f32), pltpu.VMEM((N, hd, hd), f32), pltpu.VMEM((N, hd, hd), f32)],
        compiler_params=_cparams("arbitrary", "arbitrary"),
    )(G, G, G, gates, P, g_on, o_raw, states, tm, d_oab)


IN_SPLIT = (0, 1536, 1544, 3080, 3088, 3600)


IN_SHARD = IN_DIM // 4
IN_SHARD_PAD = 928


def align_w_in_t(wt):
    s = IN_SPLIT
    pad = jnp.zeros((IN_ALIGNED - IN_DIM, wt.shape[1]), wt.dtype)
    return jnp.concatenate([wt[s[0]:s[1]], wt[s[2]:s[3]], wt[s[4]:s[5]], wt[s[1]:s[2]], wt[s[3]:s[4]], pad], axis=0)


def unalign_w_in_t(wa):
    return jnp.concatenate([wa[0:1536], wa[COL_SMALL:COL_SMALL + 8], wa[1536:3072],
                            wa[COL_SMALL + 8:COL_SMALL + 16], wa[3072:3584]], axis=0)


IN_SEGMENTS = (((0, 1536), 0), ((1536, 1544), COL_SMALL), ((1544, 3080), 1536), ((3080, 3088), COL_SMALL + 8),
               ((3088, 3600), 3072))


def align_w_in_slots(slots):
    pieces = []
    for (lo, hi), _ in sorted(IN_SEGMENTS, key=lambda seg: seg[1]):
        for k in range(N_CHIPS):
            a, b = max(lo, k * IN_SHARD), min(hi, (k + 1) * IN_SHARD)
            if a < b:
                pieces.append(slots[k, a - k * IN_SHARD:b - k * IN_SHARD])
    pieces.append(jnp.zeros((IN_ALIGNED - IN_DIM, slots.shape[2]), slots.dtype))
    return jnp.concatenate(pieces, axis=0)


def unalign_to_slots(wa):
    slots = []
    for k in range(N_CHIPS):
        lo, hi = k * IN_SHARD, (k + 1) * IN_SHARD
        pieces = []
        for (a, b), first in IN_SEGMENTS:
            x, y = max(a, lo), min(b, hi)
            if x < y:
                pieces.append(wa[first + x - a:first + y - a])
        pieces.append(jnp.zeros((IN_SHARD_PAD - IN_SHARD, wa.shape[1]), wa.dtype))
        slots.append(jnp.concatenate(pieces, axis=0))
    return jnp.stack(slots)


def _lanes_vec(pieces):
    v = jnp.zeros((1, LANES), f32)
    for off, a in pieces:
        v = lax.dynamic_update_slice(v, a.astype(f32), (0, off))
    return v


def local_step(x, mem, target, w, sp, *, B):
    T = x.shape[0]
    S = T // B
    gq8, gk8 = jnp.tile(sp["fox_qnorm_g"], (1, FOX_HEADS)), jnp.tile(sp["fox_knorm_g"], (1, FOX_HEADS))
    go2 = jnp.tile(sp["fox_onorm_g"], (1, 2))
    bias = _lanes_vec([(SM_F, sp["fox_f_bias"]), (SM_A, sp["gdn_dt_bias"])])
    alog = _lanes_vec([(SM_A, sp["gdn_A_log"])])

    h1 = rms_fwd(x, sp["norm_mix_g"], name="rms_mix")
    P = matmul(h1, w["wa_t"], tb=True, name="mm_in", tn=IN_TILE)
    gates = gates_fwd(P, bias, alog, B=B)
    c = gates[:, SM_F:SM_F + FOX_HEADS].reshape(B, S, FOX_HEADS).transpose(0, 2, 1)
    ccol, crow = c[..., None], c.reshape(B, FOX_HEADS, S // FOX_TQ, 1, FOX_TQ)
    qn, kn, vb = fox_prep_fwd(P, gq8, gk8)
    o_raw, o_a, lse = fox_core_fwd(qn, kn, vb, ccol, crow, go2, B=B)
    G = gdn_prep_fwd(P, w["conv_w"], B=B)
    ob_raw, o_b, states, gdn_tm = gdn_fwd(G, gates, P, sp["gdn_onorm_g"], B=B)
    oab = jnp.concatenate([o_a, o_b], axis=1)
    if "late" in w:
        w = {**w, **w["late"](oab)}
    x2, hq = matmul_rows(oab, w["w_out"], (x, sp["norm_xattn_g"]), mode="rms_fwd", name="mm_out_rms")
    hm = rms_fwd(mem, sp["mem_norm_g"], name="rms_mem")
    cq = matmul(hq, w["w_cq"], name="mm_cq")
    ckv = matmul(hm, w["w_ckv"], name="mm_ckv")
    co = xattn_fwd(cq, ckv, sp["xattn_qnorm_g"], sp["xattn_knorm_g"], B=B)
    x3, hf = matmul_rows(co, w["w_co"], (x2, sp["norm_mlp_g"]), mode="rms_fwd", name="mm_co_rms")
    act = matmul(hf, w["w_mlp1"], b_stacked=True, relu2_out=True, out_dtype=MXU_DTYPE, name="mm_mlp1")
    dy, dy_op, loss = matmul_rows(act, w["w_mlp2"], (x3, target), mode="loss", name="mm_mlp2_loss")

    da = matmul(dy_op, w["w_mlp2"], tb=True, relu2_bwd_aux=act, out_dtype=MXU_DTYPE, name="mm_d_act")
    g_mlp2 = matmul(act, dy_op, ta=True, out_dtype=WIRE_DTYPE, name="mm_g_mlp2")
    g_mlp1 = matmul(hf, da, ta=True, out_stacked=True, out_dtype=WIRE_DTYPE, name="mm_g_mlp1")
    by_rows = lambda g: g.reshape(N_CHIPS, g.shape[0] // N_CHIPS, g.shape[1])
    early = w.get("grads_ready", lambda grads: jnp.zeros((1, 1), f32))
    tok = early(dict(w_mlp1=g_mlp1, w_mlp2=by_rows(g_mlp2)))[0, 0]
    dx3, g_norm_mlp = matmul_rows(da, w["w_mlp1"], (x3, sp["norm_mlp_g"] + tok, dy), mode="rms_bwd", tb=True,
                                  b_stacked=True, name="mm_d_hf_rms")
    dco = matmul(dx3, w["w_co"], tb=True, name="mm_d_co")
    g_co = matmul(co, dx3, ta=True, out_dtype=WIRE_DTYPE, name="mm_g_co")
    g_co = g_co.reshape(XATTN_WIDTH, N_CHIPS, D_MODEL // N_CHIPS).transpose(1, 0, 2)
    dcq, dckv, g_xq, g_xk = xattn_bwd(cq, ckv, sp["xattn_qnorm_g"], sp["xattn_knorm_g"], dco, B=B)
    g_cq = matmul(hq, dcq, ta=True, out_dtype=WIRE_DTYPE, name="mm_g_cq")
    g_ckv = matmul(hm, dckv, ta=True, out_dtype=WIRE_DTYPE, name="mm_g_ckv")
    _, g_mem_norm = matmul_rows(dckv, w["w_ckv"], (mem, sp["mem_norm_g"], None), mode="rms_bwd", tb=True, name="mm_d_hm_rms")
    dx2, g_norm_xattn = matmul_rows(dcq, w["w_cq"], (x2, sp["norm_xattn_g"], dx3), mode="rms_bwd", tb=True, name="mm_d_hq_rms")
    doab = matmul(dx2, w["w_out"], tb=True, name="mm_d_oab")
    g_out = matmul(oab, dx2, ta=True, out_dtype=WIRE_DTYPE, name="mm_g_out")
    tok = early(dict(w_co=g_co, w_cq=by_rows(g_cq), w_ckv=by_rows(g_ckv), w_out=by_rows(g_out)))[0, 0]
    dqn, dkn, dv_f, dckey, dcrow, dgo2 = fox_core_bwd(qn, kn, vb, ccol, crow, go2 + tok, o_raw, lse, doab, B=B)
    dq_f, dk_f, dgq8, dgk8 = fox_prep_bwd(P, gq8, gk8, dqn, dkn)
    dGq, dGk, dGv, dgt, dz, g_gdn_on = gdn_bwd(G, gates, P, sp["gdn_onorm_g"], ob_raw, states, gdn_tm, doab, B=B)
    dPg, g_conv = gdn_prep_bwd(P, w["conv_w"], dGq, dGk, dGv, B=B)
    dc = (dckey[:, :, 0, :] + dcrow.reshape(B, FOX_HEADS, S)).transpose(0, 2, 1).reshape(T, FOX_HEADS)
    dgates = dgt + jnp.pad(dc, ((0, 0), (SM_F, LANES - SM_F - FOX_HEADS)))
    dsmall, par = gates_bwd(P, bias, alog, dgates, B=B)
    dP = jnp.concatenate([dq_f, dk_f, dv_f, dPg, dz, dsmall, jnp.zeros((T, IN_ALIGNED - COL_SMALL - LANES), MXU_DTYPE)], axis=1)
    g_wa = matmul(dP, h1, ta=True, out_dtype=WIRE_DTYPE, name="mm_g_in", tm=IN_TILE)
    g_in = unalign_to_slots(g_wa)
    tok = early(dict(w_in=g_in))[0, 0]
    dx, g_norm_mix = matmul_rows(dP, w["wa_t"], (x, sp["norm_mix_g"] + tok, dx2), mode="rms_bwd", tk=IN_TILE,
                                 name="mm_d_h1_rms")

    fold = lambda g: jnp.sum(g.reshape(-1, FOX_HEAD_DIM), axis=0, keepdims=True)
    big = dict(w_in=g_in, w_out=by_rows(g_out), w_cq=by_rows(g_cq), w_ckv=by_rows(g_ckv), w_co=g_co, w_mlp1=g_mlp1,
               w_mlp2=by_rows(g_mlp2))
    small = dict(norm_mix_g=g_norm_mix, fox_qnorm_g=fold(dgq8), fox_knorm_g=fold(dgk8),
                 fox_f_bias=par[0:1, SM_F:SM_F + FOX_HEADS], fox_onorm_g=fold(dgo2), gdn_conv_w=g_conv,
                 gdn_A_log=par[1:2, SM_A:SM_A + GDN_HEADS], gdn_dt_bias=par[0:1, SM_A:SM_A + GDN_HEADS],
                 gdn_onorm_g=g_gdn_on, norm_xattn_g=g_norm_xattn, mem_norm_g=g_mem_norm,
                 xattn_qnorm_g=g_xq, xattn_knorm_g=g_xk, norm_mlp_g=g_norm_mlp)
    return loss, dx, big, small


MESH_IDS = pl.DeviceIdType.MESH
N_CHIPS = 4
HBM_SPEC = pl.BlockSpec(memory_space=pltpu.HBM)
PACK_ROWS = 30720
PACK_HALF = PACK_ROWS // 2
PACK_BLOCK = 3072


def _place():
    return lax.axis_index("x"), lax.axis_index("y"), lax.axis_index("c")


def _other_chips(x, y):
    return [(1 - x, y), (x, 1 - y), (1 - x, 1 - y)]


def _remote(src, dst, send_sem, recv_sem, to):
    return pltpu.make_async_remote_copy(src_ref=src, dst_ref=dst, send_sem=send_sem, recv_sem=recv_sem,
                                        device_id=to, device_id_type=MESH_IDS)


def all_gather_shards(packed):
    half = PACK_HALF

    def body(src_ref, out_ref, send_sems, recv_sems):
        x, y, c = _place()
        me_chip = 2 * x + y
        sibling = (x, y, 1 - c)
        chips = _other_chips(x, y)

        def rows(chip, core):
            return out_ref.at[chip, pl.ds(core * half, half), :]

        sends = [_remote(src_ref.at[pl.ds(c * half, half), :], rows(me_chip, c), send_sems.at[j], recv_sems.at[j], (px, py, c))
                 for j, (px, py) in enumerate(chips)]
        for cp in sends:
            cp.start()
        passed = []
        for j, (px, py) in enumerate(chips):
            theirs = rows(2 * px + py, c)
            _remote(theirs, theirs, send_sems.at[j], recv_sems.at[j], (px, py, c)).wait_recv()
            cp = _remote(theirs, theirs, send_sems.at[3 + j], recv_sems.at[3 + j], sibling)
            cp.start()
            passed.append(cp)
        for j, (px, py) in enumerate(chips):
            theirs = rows(2 * px + py, 1 - c)
            _remote(theirs, theirs, send_sems.at[3 + j], recv_sems.at[3 + j], sibling).wait_recv()
        for cp in sends + passed:
            cp.wait_send()

    return pl.pallas_call(
        body, name="all_gather_shards", in_specs=[HBM_SPEC], out_specs=HBM_SPEC,
        out_shape=jax.ShapeDtypeStruct((N_CHIPS,) + packed.shape, packed.dtype),
        scratch_shapes=[pltpu.SemaphoreType.DMA((6,)), pltpu.SemaphoreType.DMA((6,))],
    )(packed)


def exchange_core_halves(G):
    half = PACK_HALF

    def body(g_ref, land_ref, send_sem, recv_sem):
        x, y, c = _place()
        cp = _remote(g_ref.at[:, pl.ds((1 - c) * half, half), :], land_ref, send_sem, recv_sem, (x, y, 1 - c))
        cp.start()
        cp.wait()

    return pl.pallas_call(
        body, name="exchange_core_halves", in_specs=[HBM_SPEC], out_specs=HBM_SPEC,
        out_shape=jax.ShapeDtypeStruct((N_CHIPS, half, LANES), G.dtype),
        scratch_shapes=[pltpu.SemaphoreType.DMA(()), pltpu.SemaphoreType.DMA(())],
    )(G)


def add_core_halves(G, land, core):
    nb = PACK_HALF // PACK_BLOCK

    def body(c_ref, g_ref, l_ref, o_ref):
        o_ref[...] = (g_ref[...].astype(f32) + l_ref[...].astype(f32)).astype(o_ref.dtype)

    blk = (1, PACK_BLOCK, LANES)
    return pl.pallas_call(
        body, name="add_core_halves",
        grid_spec=pltpu.PrefetchScalarGridSpec(
            num_scalar_prefetch=1, grid=(N_CHIPS, nb),
            in_specs=[pl.BlockSpec(blk, lambda k, i, c_ref: (k, c_ref[0] * nb + i, 0)),
                      pl.BlockSpec(blk, lambda k, i, c_ref: (k, i, 0))],
            out_specs=pl.BlockSpec(blk, lambda k, i, c_ref: (k, i, 0))),
        out_shape=jax.ShapeDtypeStruct(land.shape, land.dtype),
        compiler_params=_cparams("parallel", "parallel"),
    )(core, G, land)


def scatter_to_chips(part):
    def body(p_ref, land_ref, send_sems, recv_sems):
        x, y, c = _place()
        me_chip = 2 * x + y
        chips = _other_chips(x, y)
        sends = [_remote(p_ref.at[2 * px + py], land_ref.at[me_chip], send_sems.at[j], recv_sems.at[j], (px, py, c))
                 for j, (px, py) in enumerate(chips)]
        for cp in sends:
            cp.start()
        for j, (px, py) in enumerate(chips):
            slot = land_ref.at[2 * px + py]
            _remote(slot, slot, send_sems.at[j], recv_sems.at[j], (px, py, c)).wait_recv()
        for cp in sends:
            cp.wait_send()

    return pl.pallas_call(
        body, name="scatter_to_chips", in_specs=[HBM_SPEC], out_specs=HBM_SPEC,
        out_shape=jax.ShapeDtypeStruct(part.shape, part.dtype),
        scratch_shapes=[pltpu.SemaphoreType.DMA((3,)), pltpu.SemaphoreType.DMA((3,))],
    )(part)


def sum_chips(part, land, order):
    nb = PACK_HALF // PACK_BLOCK

    def body(order_ref, p_ref, l1_ref, l2_ref, l3_ref, o_ref):
        o_ref[...] = ((p_ref[0].astype(f32) + l1_ref[0].astype(f32)) + l2_ref[0].astype(f32)) + l3_ref[0].astype(f32)

    slot = lambda j: pl.BlockSpec((1, PACK_BLOCK, LANES), lambda i, order_ref: (order_ref[j], i, 0))
    return pl.pallas_call(
        body, name="sum_chips",
        grid_spec=pltpu.PrefetchScalarGridSpec(
            num_scalar_prefetch=1, grid=(nb,), in_specs=[slot(0), slot(1), slot(2), slot(3)],
            out_specs=pl.BlockSpec((PACK_BLOCK, LANES), lambda i, order_ref: (i, 0))),
        out_shape=jax.ShapeDtypeStruct((PACK_HALF, LANES), f32),
        compiler_params=_cparams("parallel"),
    )(order, part, land, land, land)


def swap_core_halves(red):
    def body(r_ref, out_ref, send_sem, recv_sem):
        x, y, c = _place()
        cp = _remote(r_ref, out_ref, send_sem, recv_sem, (x, y, 1 - c))
        cp.start()
        cp.wait()

    return pl.pallas_call(
        body, name="swap_core_halves", in_specs=[HBM_SPEC], out_specs=HBM_SPEC,
        out_shape=jax.ShapeDtypeStruct(red.shape, red.dtype),
        scratch_shapes=[pltpu.SemaphoreType.DMA(()), pltpu.SemaphoreType.DMA(())],
    )(red)


def _half(ref, core):
    rows = ref.shape[-2] // 2
    return ref.at[(slice(None),) * (len(ref.shape) - 2) + (pl.ds(core * rows, rows), slice(None))]


def gather_weights(shards, conv):
    n = len(shards)

    def body(*refs):
        src, conv_src = refs[:n], refs[n]
        out, conv_out = refs[n + 1:2 * n + 1], refs[2 * n + 1]
        send_sems, recv_sems = refs[2 * n + 2], refs[2 * n + 3]
        x, y, c = _place()
        me_chip = 2 * x + y
        sibling = (x, y, 1 - c)
        chips = _other_chips(x, y)
        sends = []
        for a in range(n):
            for j, (px, py) in enumerate(chips):
                sends.append(_remote(_half(src[a], c), _half(out[a].at[me_chip], c),
                                     send_sems.at[6 * a + j], recv_sems.at[6 * a + j], (px, py, c)))
        for j, (px, py) in enumerate(chips):
            sends.append(_remote(conv_src, conv_out.at[me_chip], send_sems.at[6 * n + j], recv_sems.at[6 * n + j], (px, py, c)))
        for cp in sends:
            cp.start()
        passed = []
        for a in range(n):
            for j, (px, py) in enumerate(chips):
                theirs = _half(out[a].at[2 * px + py], c)
                _remote(theirs, theirs, send_sems.at[6 * a + j], recv_sems.at[6 * a + j], (px, py, c)).wait_recv()
                cp = _remote(theirs, theirs, send_sems.at[6 * a + 3 + j], recv_sems.at[6 * a + 3 + j], sibling)
                cp.start()
                passed.append(cp)
        for j, (px, py) in enumerate(chips):
            theirs = conv_out.at[2 * px + py]
            _remote(theirs, theirs, send_sems.at[6 * n + j], recv_sems.at[6 * n + j], (px, py, c)).wait_recv()
        for a in range(n):
            for j, (px, py) in enumerate(chips):
                theirs = _half(out[a].at[2 * px + py], 1 - c)
                _remote(theirs, theirs, send_sems.at[6 * a + 3 + j], recv_sems.at[6 * a + 3 + j], sibling).wait_recv()
        for cp in sends + passed:
            cp.wait_send()

    return pl.pallas_call(
        body, name="gather_weights", in_specs=[HBM_SPEC] * (n + 1), out_specs=[HBM_SPEC] * (n + 1),
        out_shape=[jax.ShapeDtypeStruct((N_CHIPS,) + s.shape, s.dtype) for s in list(shards) + [conv]],
        scratch_shapes=[pltpu.SemaphoreType.DMA((6 * n + 3,)), pltpu.SemaphoreType.DMA((6 * n + 3,))],
    )(*shards, conv)


SEM_SPEC = pl.BlockSpec(memory_space=pltpu.SEMAPHORE)
SPLIT_EFFECT = pltpu.SideEffectType.DATAFLOW_SIDE_EFFECTING


def _gather_async_copies(src, land, send_sems, recv_sems, x, y, c):
    me_chip = 2 * x + y
    sends, arrivals = [], []
    for a in range(len(src)):
        for j, (px, py) in enumerate(_other_chips(x, y)):
            for core in range(2):
                sends.append(_remote(_half(src[a], c), _half(land[a].at[me_chip], c), send_sems.at[6 * a + 2 * j + core],
                                     recv_sems.at[6 * a + 2 * j + c], (px, py, core)))
                theirs = _half(land[a].at[2 * px + py], core)
                arrivals.append(_remote(theirs, theirs, send_sems.at[6 * a + 2 * j + core],
                                        recv_sems.at[6 * a + 2 * j + core], (px, py, core)))
    return sends, arrivals


def gather_weights_start(shards, after):
    n = len(shards)

    def body(*refs):
        src, land = refs[:n], refs[n:2 * n]
        send_sems, recv_sems, token = refs[2 * n + 1], refs[2 * n + 2], refs[4 * n + 3]
        x, y, c = _place()
        for cp in _gather_async_copies(src, land, send_sems, recv_sems, x, y, c)[0]:
            cp.start()
        token[...] = jnp.zeros_like(token)

    zones = [pltpu.with_memory_space_constraint(lax.empty((N_CHIPS,) + s.shape, s.dtype), pltpu.HBM) for s in shards]
    srcs = [pltpu.with_memory_space_constraint(s, pltpu.HBM) for s in shards]
    out = pl.pallas_call(
        body, name="gather_weights_start",
        out_shape=[pltpu.SemaphoreType.DMA((6 * n,)), pltpu.SemaphoreType.DMA((6 * n,))]
        + [pltpu.HBM(s.shape, s.dtype) for s in shards] + [pltpu.HBM(z.shape, z.dtype) for z in zones]
        + [jax.ShapeDtypeStruct((8, LANES), f32)],
        in_specs=[HBM_SPEC] * (2 * n) + [pl.BlockSpec(memory_space=pl.ANY)],
        out_specs=[SEM_SPEC, SEM_SPEC] + [HBM_SPEC] * (2 * n) + [pl.BlockSpec(memory_space=pltpu.VMEM)],
        input_output_aliases={i: 2 + i for i in range(2 * n)},
        compiler_params=pltpu.CompilerParams(has_side_effects=SPLIT_EFFECT),
    )(*srcs, *zones, after)
    return out[0], out[1], out[2:2 + n], out[2 + n:2 + 2 * n], out[-1]


def gather_weights_wait(send_sems, recv_sems, shards, zones, after):
    n = len(shards)

    def body(*refs):
        src, land = refs[:n], refs[n:2 * n]
        send_sems, recv_sems = refs[2 * n], refs[2 * n + 1]
        x, y, c = _place()
        sends, arrivals = _gather_async_copies(src, land, send_sems, recv_sems, x, y, c)
        for cp in sends:
            cp.wait_send()
        for cp in arrivals:
            cp.wait_recv()

    out = pl.pallas_call(
        body, name="gather_weights_wait",
        out_shape=[pltpu.HBM(s.shape, s.dtype) for s in shards] + [pltpu.HBM(z.shape, z.dtype) for z in zones],
        in_specs=[HBM_SPEC] * (2 * n) + [SEM_SPEC, SEM_SPEC, pl.BlockSpec(memory_space=pl.ANY)],
        out_specs=[HBM_SPEC] * (2 * n),
        input_output_aliases={i: i for i in range(2 * n)},
        compiler_params=pltpu.CompilerParams(has_side_effects=SPLIT_EFFECT),
    )(*shards, *zones, send_sems, recv_sems, after)
    return out[n:]


def swap_grad_halves(grads, *, name):
    n = len(grads)

    def body(*refs):
        g, land, send_sems, recv_sems = refs[:n], refs[n:2 * n], refs[2 * n], refs[2 * n + 1]
        x, y, c = _place()
        copies = [_remote(_half(g[a], 1 - c), land[a], send_sems.at[a], recv_sems.at[a], (x, y, 1 - c)) for a in range(n)]
        for cp in copies:
            cp.start()
        for cp in copies:
            cp.wait()

    return pl.pallas_call(
        body, name=name, in_specs=[HBM_SPEC] * n, out_specs=[HBM_SPEC] * n,
        out_shape=[jax.ShapeDtypeStruct((N_CHIPS, g.shape[1] // 2, g.shape[2]), g.dtype) for g in grads],
        scratch_shapes=[pltpu.SemaphoreType.DMA((n,)), pltpu.SemaphoreType.DMA((n,))],
    )(*grads)


GRAD_ROWS = 512


def add_grad_halves(g, land, core, *, name):
    _, half, cols = land.shape
    tr = GRAD_ROWS if half % GRAD_ROWS == 0 else half
    nb = half // tr

    def body(c_ref, g_ref, l_ref, o_ref):
        o_ref[...] = (g_ref[...].astype(f32) + l_ref[...].astype(f32)).astype(o_ref.dtype)

    blk = (1, tr, cols)
    return pl.pallas_call(
        body, name=name,
        grid_spec=pltpu.PrefetchScalarGridSpec(
            num_scalar_prefetch=1, grid=(N_CHIPS, nb),
            in_specs=[pl.BlockSpec(blk, lambda k, i, c_ref: (k, c_ref[0] * nb + i, 0)),
                      pl.BlockSpec(blk, lambda k, i, c_ref: (k, i, 0))],
            out_specs=pl.BlockSpec(blk, lambda k, i, c_ref: (k, i, 0))),
        out_shape=jax.ShapeDtypeStruct(land.shape, land.dtype),
        compiler_params=_cparams("parallel", "parallel"),
    )(core, g, land)


def scatter_grads(parts):
    n = len(parts)

    def body(*refs):
        p, land, send_sems, recv_sems = refs[:n], refs[n:2 * n], refs[2 * n], refs[2 * n + 1]
        x, y, c = _place()
        me_chip = 2 * x + y
        chips = _other_chips(x, y)
        sends = [_remote(p[a].at[2 * px + py], land[a].at[me_chip], send_sems.at[3 * a + j], recv_sems.at[3 * a + j], (px, py, c))
                 for a in range(n) for j, (px, py) in enumerate(chips)]
        for cp in sends:
            cp.start()
        for a in range(n):
            for j, (px, py) in enumerate(chips):
                slot = land[a].at[2 * px + py]
                _remote(slot, slot, send_sems.at[3 * a + j], recv_sems.at[3 * a + j], (px, py, c)).wait_recv()
        for cp in sends:
            cp.wait_send()

    return pl.pallas_call(
        body, name="scatter_grads", in_specs=[HBM_SPEC] * n, out_specs=[HBM_SPEC] * n,
        out_shape=[jax.ShapeDtypeStruct(p.shape, p.dtype) for p in parts],
        scratch_shapes=[pltpu.SemaphoreType.DMA((3 * n,)), pltpu.SemaphoreType.DMA((3 * n,))],
    )(*parts)


def _scatter_async_copies(parts, land, send_sems, recv_sems, x, y, c):
    me_chip = 2 * x + y
    sends, arrivals = [], []
    for a in range(len(parts)):
        for j, (px, py) in enumerate(_other_chips(x, y)):
            sems = (send_sems.at[3 * a + j], recv_sems.at[3 * a + j], (px, py, c))
            sends.append(_remote(parts[a].at[2 * px + py], land[a].at[me_chip], *sems))
            slot = land[a].at[2 * px + py]
            arrivals.append(_remote(slot, slot, *sems))
    return sends, arrivals


def scatter_grads_start(parts, *, name):
    n = len(parts)

    def body(*refs):
        p, land = refs[:n], refs[n:2 * n]
        send_sems, recv_sems, token = refs[2 * n], refs[2 * n + 1], refs[4 * n + 2]
        x, y, c = _place()
        for cp in _scatter_async_copies(p, land, send_sems, recv_sems, x, y, c)[0]:
            cp.start()
        token[...] = jnp.zeros_like(token)

    zones = [pltpu.with_memory_space_constraint(lax.empty(p.shape, p.dtype), pltpu.HBM) for p in parts]
    srcs = [pltpu.with_memory_space_constraint(p, pltpu.HBM) for p in parts]
    hbm = [pltpu.HBM(p.shape, p.dtype) for p in parts]
    out = pl.pallas_call(
        body, name=name,
        out_shape=[pltpu.SemaphoreType.DMA((3 * n,)), pltpu.SemaphoreType.DMA((3 * n,))] + hbm + hbm
        + [jax.ShapeDtypeStruct((8, LANES), f32)],
        in_specs=[HBM_SPEC] * (2 * n),
        out_specs=[SEM_SPEC, SEM_SPEC] + [HBM_SPEC] * (2 * n) + [pl.BlockSpec(memory_space=pltpu.VMEM)],
        input_output_aliases={i: 2 + i for i in range(2 * n)},
        compiler_params=pltpu.CompilerParams(has_side_effects=SPLIT_EFFECT),
    )(*srcs, *zones)
    return out[0], out[1], out[2:2 + n], out[2 + n:2 + 2 * n], out[-1]


def scatter_grads_wait(send_sems, recv_sems, parts, zones, after, *, name):
    n = len(parts)

    def body(*refs):
        p, land = refs[:n], refs[n:2 * n]
        x, y, c = _place()
        sends, arrivals = _scatter_async_copies(p, land, refs[2 * n], refs[2 * n + 1], x, y, c)
        for cp in sends:
            cp.wait_send()
        for cp in arrivals:
            cp.wait_recv()

    hbm = [pltpu.HBM(p.shape, p.dtype) for p in parts]
    out = pl.pallas_call(
        body, name=name, out_shape=hbm + hbm,
        in_specs=[HBM_SPEC] * (2 * n) + [SEM_SPEC, SEM_SPEC, pl.BlockSpec(memory_space=pl.ANY)],
        out_specs=[HBM_SPEC] * (2 * n),
        input_output_aliases={i: i for i in range(2 * n)},
        compiler_params=pltpu.CompilerParams(has_side_effects=SPLIT_EFFECT),
    )(*parts, *zones, send_sems, recv_sems, after)
    return out[:n], out[n:]


def sum_grads(part, land, order, *, name):
    _, half, cols = part.shape
    tr = GRAD_ROWS if half % GRAD_ROWS == 0 else half

    def body(order_ref, p_ref, l1_ref, l2_ref, l3_ref, o_ref):
        o_ref[...] = ((p_ref[0].astype(f32) + l1_ref[0].astype(f32)) + l2_ref[0].astype(f32)) + l3_ref[0].astype(f32)

    slot = lambda j: pl.BlockSpec((1, tr, cols), lambda i, order_ref: (order_ref[j], i, 0))
    return pl.pallas_call(
        body, name=name,
        grid_spec=pltpu.PrefetchScalarGridSpec(
            num_scalar_prefetch=1, grid=(half // tr,), in_specs=[slot(0), slot(1), slot(2), slot(3)],
            out_specs=pl.BlockSpec((tr, cols), lambda i, order_ref: (i, 0))),
        out_shape=jax.ShapeDtypeStruct((half, cols), f32),
        compiler_params=_cparams("parallel"),
    )(order, part, land, land, land)


def _peer(x, y, c, r):
    return ((1 - x) if r & 4 else x, (1 - y) if r & 2 else y, (1 - c) if r & 1 else c)


def _reduce_async_copies(grads, land, send_sems, recv_sems, x, y, c):
    me = 4 * x + 2 * y + c
    sends, arrivals = [], []
    for a in range(len(grads)):
        for r in range(1, N_DEV):
            px, py, pc = _peer(x, y, c, r)
            sems = (send_sems.at[7 * a + r - 1], recv_sems.at[7 * a + r - 1], (px, py, pc))
            sends.append(_remote(_half(grads[a].at[2 * px + py], pc), land[a].at[me], *sems))
            slot = land[a].at[4 * px + 2 * py + pc]
            arrivals.append(_remote(slot, slot, *sems))
    return sends, arrivals


def reduce_grads_start(grads, *, name):
    n = len(grads)

    def body(*refs):
        g, land = refs[:n], refs[n:2 * n]
        send_sems, recv_sems, token = refs[2 * n], refs[2 * n + 1], refs[4 * n + 2]
        x, y, c = _place()
        for cp in _reduce_async_copies(g, land, send_sems, recv_sems, x, y, c)[0]:
            cp.start()
        token[...] = jnp.zeros_like(token)

    zones = [pltpu.with_memory_space_constraint(lax.empty((N_DEV, g.shape[1] // 2, g.shape[2]), g.dtype), pltpu.HBM)
             for g in grads]
    srcs = [pltpu.with_memory_space_constraint(g, pltpu.HBM) for g in grads]
    out = pl.pallas_call(
        body, name=name,
        out_shape=[pltpu.SemaphoreType.DMA((7 * n,)), pltpu.SemaphoreType.DMA((7 * n,))]
        + [pltpu.HBM(g.shape, g.dtype) for g in grads] + [pltpu.HBM(z.shape, z.dtype) for z in zones]
        + [jax.ShapeDtypeStruct((8, LANES), f32)],
        in_specs=[HBM_SPEC] * (2 * n),
        out_specs=[SEM_SPEC, SEM_SPEC] + [HBM_SPEC] * (2 * n) + [pl.BlockSpec(memory_space=pltpu.VMEM)],
        input_output_aliases={i: 2 + i for i in range(2 * n)},
        compiler_params=pltpu.CompilerParams(has_side_effects=SPLIT_EFFECT),
    )(*srcs, *zones)
    return out[0], out[1], out[2:2 + n], out[2 + n:2 + 2 * n], out[-1]


def reduce_grads_wait(send_sems, recv_sems, grads, zones, after, *, name):
    n = len(grads)

    def body(*refs):
        g, land = refs[:n], refs[n:2 * n]
        x, y, c = _place()
        sends, arrivals = _reduce_async_copies(g, land, refs[2 * n], refs[2 * n + 1], x, y, c)
        for cp in sends:
            cp.wait_send()
        for cp in arrivals:
            cp.wait_recv()

    hbm = [pltpu.HBM(a.shape, a.dtype) for a in list(grads) + list(zones)]
    out = pl.pallas_call(
        body, name=name, out_shape=hbm,
        in_specs=[HBM_SPEC] * (2 * n) + [SEM_SPEC, SEM_SPEC, pl.BlockSpec(memory_space=pl.ANY)],
        out_specs=[HBM_SPEC] * (2 * n),
        input_output_aliases={i: i for i in range(2 * n)},
        compiler_params=pltpu.CompilerParams(has_side_effects=SPLIT_EFFECT),
    )(*grads, *zones, send_sems, recv_sems, after)
    return out[:n], out[n:]


def sum_partials(g, land, where, *, name):
    _, half, cols = land.shape
    tr = GRAD_ROWS if half % GRAD_ROWS == 0 else half
    nb = half // tr

    def body(where_ref, g_ref, *rest):
        o_ref = rest[-1]
        acc = g_ref[0].astype(f32)
        for l_ref in rest[:-1]:
            acc = acc + l_ref[0].astype(f32)
        o_ref[...] = acc

    blk = (1, tr, cols)
    slot = lambda j: pl.BlockSpec(blk, lambda i, where_ref: (where_ref[2 + j], i, 0))
    return pl.pallas_call(
        body, name=name,
        grid_spec=pltpu.PrefetchScalarGridSpec(
            num_scalar_prefetch=1, grid=(nb,),
            in_specs=[pl.BlockSpec(blk, lambda i, where_ref: (where_ref[0], where_ref[1] * nb + i, 0))]
            + [slot(j) for j in range(N_DEV - 1)],
            out_specs=pl.BlockSpec((tr, cols), lambda i, where_ref: (i, 0))),
        out_shape=jax.ShapeDtypeStruct((half, cols), f32),
        compiler_params=_cparams("parallel"),
    )(where, g, *([land] * (N_DEV - 1)))


def swap_reduced_halves(mine, *, name):
    n = len(mine)

    def body(*refs):
        r, out, send_sems, recv_sems = refs[:n], refs[n:2 * n], refs[2 * n], refs[2 * n + 1]
        x, y, c = _place()
        copies = [_remote(r[a], out[a], send_sems.at[a], recv_sems.at[a], (x, y, 1 - c)) for a in range(n)]
        for cp in copies:
            cp.start()
        for cp in copies:
            cp.wait()

    return pl.pallas_call(
        body, name=name, in_specs=[HBM_SPEC] * n, out_specs=[HBM_SPEC] * n,
        out_shape=[jax.ShapeDtypeStruct(r.shape, r.dtype) for r in mine],
        scratch_shapes=[pltpu.SemaphoreType.DMA((n,)), pltpu.SemaphoreType.DMA((n,))],
    )(*mine)


def adamw_halves(w, mine, theirs, m, v, core, *, name):
    R, C = w.shape
    tr = min(GRAD_ROWS, R // 2)
    half_nb = R // 2 // tr

    def body(c_ref, w_ref, a_ref, b_ref, m_ref, v_ref, g_ref, d_ref, nm_ref, nv_ref):
        low = pl.program_id(0) < half_nb
        gv = jnp.where(low == (c_ref[0] == 0), a_ref[...], b_ref[...])
        nm = ADAM_B1 * m_ref[...] + (1.0 - ADAM_B1) * gv
        nv = ADAM_B2 * v_ref[...] + (1.0 - ADAM_B2) * jnp.square(gv)
        m_hat = nm / (1.0 - ADAM_B1 ** ADAM_STEP)
        v_hat = nv / (1.0 - ADAM_B2 ** ADAM_STEP)
        g_ref[...] = gv
        d_ref[...] = -ADAM_LR * (m_hat / (jnp.sqrt(v_hat) + ADAM_EPS) + ADAM_WD * w_ref[...])
        nm_ref[...] = nm
        nv_ref[...] = nv

    full = pl.BlockSpec((tr, C), lambda i, c_ref: (i, 0))
    part = pl.BlockSpec((tr, C), lambda i, c_ref: (i % half_nb, 0))
    out = jax.ShapeDtypeStruct((R, C), f32)
    return pl.pallas_call(
        body, name=name,
        grid_spec=pltpu.PrefetchScalarGridSpec(
            num_scalar_prefetch=1, grid=(2 * half_nb,), in_specs=[full, part, part, full, full], out_specs=[full] * 4),
        out_shape=[out] * 4, compiler_params=_cparams("parallel"),
    )(core, w, mine, theirs, m, v)


N_DEV = 8


def all_reduce_small(v):
    def body(src_ref, out_ref, land_ref, send_sems, recv_sems):
        x, y, c = _place()
        me = 4 * x + 2 * y + c
        copies = []
        for r in range(1, N_DEV):
            peer = ((1 - x) if r & 4 else x, (1 - y) if r & 2 else y, (1 - c) if r & 1 else c)
            copies.append(_remote(src_ref, land_ref.at[r], send_sems.at[r - 1], recv_sems.at[r - 1], peer))
        for cp in copies:
            cp.start()
        land_ref[0] = src_ref[...]
        for cp in copies:
            cp.wait()
        acc = land_ref[me]
        for d in range(1, N_DEV):
            acc = acc + land_ref[jnp.bitwise_xor(me, d)]
        out_ref[...] = acc

    vm = pl.BlockSpec(memory_space=pltpu.VMEM)
    return pl.pallas_call(
        body, name="all_reduce_small", in_specs=[vm], out_specs=vm,
        out_shape=jax.ShapeDtypeStruct(v.shape, v.dtype),
        scratch_shapes=[pltpu.VMEM((N_DEV,) + v.shape, v.dtype),
                        pltpu.SemaphoreType.DMA((N_DEV - 1,)), pltpu.SemaphoreType.DMA((N_DEV - 1,))],
    )(v)


def adamw(w, g, m, v, *, name, tr=None, tc=None):
    R, C = w.shape
    if tc is None:
        tr, tc = min(tr, R), C
        blk = pl.BlockSpec((tr, C), lambda i: (i, 0))
    else:
        tr = R
        blk = pl.BlockSpec((R, tc), lambda i: (0, i))

    def body(w_ref, g_ref, m_ref, v_ref, d_ref, nm_ref, nv_ref):
        gv = g_ref[...]
        nm = ADAM_B1 * m_ref[...] + (1.0 - ADAM_B1) * gv
        nv = ADAM_B2 * v_ref[...] + (1.0 - ADAM_B2) * jnp.square(gv)
        m_hat = nm / (1.0 - ADAM_B1 ** ADAM_STEP)
        v_hat = nv / (1.0 - ADAM_B2 ** ADAM_STEP)
        d_ref[...] = -ADAM_LR * (m_hat / (jnp.sqrt(v_hat) + ADAM_EPS) + ADAM_WD * w_ref[...])
        nm_ref[...] = nm
        nv_ref[...] = nv

    out = jax.ShapeDtypeStruct((R, C), f32)
    return pl.pallas_call(
        body, name=name, grid=((R // tr) * (C // tc),), in_specs=[blk] * 4, out_specs=[blk] * 3, out_shape=[out] * 3,
        compiler_params=_cparams("parallel"),
    )(w, g, m, v)


BIG_SHARDS = (("w_in", (1024, 900), True), ("w_out", (256, 1024), False), ("w_cq", (256, 512), False),
              ("w_ckv", (256, 1024), False), ("w_co", (512, 256), True), ("w_mlp1", (1024, 1024), True),
              ("w_mlp2", (1024, 1024), False))
CONV_SHARD = (CONV_WIDTH, 3 * GDN_WIDTH // N_CHIPS)
SMALL_DIMS = (("norm_mix_g", 1024), ("fox_qnorm_g", 64), ("fox_knorm_g", 64), ("fox_f_bias", 8), ("fox_onorm_g", 64),
              ("gdn_A_log", 4), ("gdn_dt_bias", 4), ("gdn_onorm_g", 128), ("norm_xattn_g", 1024), ("mem_norm_g", 1024),
              ("xattn_qnorm_g", 128), ("xattn_knorm_g", 128), ("norm_mlp_g", 1024))
WEIGHT_ORDER = ("norm_mix_g", "w_in", "fox_qnorm_g", "fox_knorm_g", "fox_f_bias", "fox_onorm_g", "gdn_conv_w", "gdn_A_log",
                "gdn_dt_bias", "gdn_onorm_g", "w_out", "norm_xattn_g", "mem_norm_g", "w_cq", "w_ckv", "xattn_qnorm_g",
                "xattn_knorm_g", "w_co", "norm_mlp_g", "w_mlp1", "w_mlp2")


def _pack_rows(pieces, rows, lead=()):
    cat = jnp.concatenate([p.reshape(lead + (-1,)) for p in pieces], axis=-1)
    cat = jnp.pad(cat, [(0, 0)] * len(lead) + [(0, rows * LANES - cat.shape[-1])])
    return cat.reshape(lead + (rows, LANES))


def _unpack_rows(buf, sizes, lead=()):
    flat = buf.reshape(lead + (-1,))
    out, off = [], 0
    for n in sizes:
        out.append(flat[..., off:off + n])
        off += n
    return out


def _conv_to_wire(conv):
    return lax.bitcast_convert_type(conv, bf16)


def _conv_from_wire(wire):
    return lax.bitcast_convert_type(wire, f32)


SMALL_ROWS = 96
SMALL_ADAM_ROWS = 56


def kernel(x, mem, norm_mix_g, w_in, fox_qnorm_g, fox_knorm_g, fox_f_bias, fox_onorm_g, gdn_conv_w, gdn_A_log, gdn_dt_bias, gdn_onorm_g, w_out, norm_xattn_g, mem_norm_g, w_cq, w_ckv, xattn_qnorm_g, xattn_knorm_g, w_co, norm_mlp_g, w_mlp1, w_mlp2, loss_target, m_norm_mix_g, m_w_in, m_fox_qnorm_g, m_fox_knorm_g, m_fox_f_bias, m_fox_onorm_g, m_gdn_conv_w, m_gdn_A_log, m_gdn_dt_bias, m_gdn_onorm_g, m_w_out, m_norm_xattn_g, m_mem_norm_g, m_w_cq, m_w_ckv, m_xattn_qnorm_g, m_xattn_knorm_g, m_w_co, m_norm_mlp_g, m_w_mlp1, m_w_mlp2, v_norm_mix_g, v_w_in, v_fox_qnorm_g, v_fox_knorm_g, v_fox_f_bias, v_fox_onorm_g, v_gdn_conv_w, v_gdn_A_log, v_gdn_dt_bias, v_gdn_onorm_g, v_w_out, v_norm_xattn_g, v_mem_norm_g, v_w_cq, v_w_ckv, v_xattn_qnorm_g, v_xattn_knorm_g, v_w_co, v_norm_mlp_g, v_w_mlp1, v_w_mlp2):
    wts = dict(norm_mix_g=norm_mix_g, w_in=w_in, fox_qnorm_g=fox_qnorm_g, fox_knorm_g=fox_knorm_g, fox_f_bias=fox_f_bias,
               fox_onorm_g=fox_onorm_g, gdn_conv_w=gdn_conv_w, gdn_A_log=gdn_A_log, gdn_dt_bias=gdn_dt_bias,
               gdn_onorm_g=gdn_onorm_g, w_out=w_out, norm_xattn_g=norm_xattn_g, mem_norm_g=mem_norm_g, w_cq=w_cq, w_ckv=w_ckv,
               xattn_qnorm_g=xattn_qnorm_g, xattn_knorm_g=xattn_knorm_g, w_co=w_co, norm_mlp_g=norm_mlp_g, w_mlp1=w_mlp1,
               w_mlp2=w_mlp2)
    mom = dict(norm_mix_g=m_norm_mix_g, w_in=m_w_in, fox_qnorm_g=m_fox_qnorm_g, fox_knorm_g=m_fox_knorm_g,
               fox_f_bias=m_fox_f_bias, fox_onorm_g=m_fox_onorm_g, gdn_conv_w=m_gdn_conv_w, gdn_A_log=m_gdn_A_log,
               gdn_dt_bias=m_gdn_dt_bias, gdn_onorm_g=m_gdn_onorm_g, w_out=m_w_out, norm_xattn_g=m_norm_xattn_g,
               mem_norm_g=m_mem_norm_g, w_cq=m_w_cq, w_ckv=m_w_ckv, xattn_qnorm_g=m_xattn_qnorm_g,
               xattn_knorm_g=m_xattn_knorm_g, w_co=m_w_co, norm_mlp_g=m_norm_mlp_g, w_mlp1=m_w_mlp1, w_mlp2=m_w_mlp2)
    var = dict(norm_mix_g=v_norm_mix_g, w_in=v_w_in, fox_qnorm_g=v_fox_qnorm_g, fox_knorm_g=v_fox_knorm_g,
               fox_f_bias=v_fox_f_bias, fox_onorm_g=v_fox_onorm_g, gdn_conv_w=v_gdn_conv_w, gdn_A_log=v_gdn_A_log,
               gdn_dt_bias=v_gdn_dt_bias, gdn_onorm_g=v_gdn_onorm_g, w_out=v_w_out, norm_xattn_g=v_norm_xattn_g,
               mem_norm_g=v_mem_norm_g, w_cq=v_w_cq, w_ckv=v_w_ckv, xattn_qnorm_g=v_xattn_qnorm_g,
               xattn_knorm_g=v_xattn_knorm_g, w_co=v_w_co, norm_mlp_g=v_norm_mlp_g, w_mlp1=v_w_mlp1, w_mlp2=v_w_mlp2)
    B, S, D = x.shape
    T = B * S
    big_names = [n for n, _, _ in BIG_SHARDS]
    chip = 2 * lax.axis_index("x") + lax.axis_index("y")
    core = lax.axis_index("c").astype(jnp.int32).reshape(1)

    shards = {n: wts[n][0].astype(MXU_DTYPE) for n in big_names[1:]}
    in_t = lambda p: jnp.swapaxes(p[0], 0, 1)
    shards["w_in"] = jnp.pad(in_t(w_in).astype(MXU_DTYPE), ((0, IN_SHARD_PAD - IN_SHARD), (0, 0)))
    w_in_all, conv_all = gather_weights([shards["w_in"]], gdn_conv_w[0])
    late = big_names[1:]
    send_sems, recv_sems, late_src, late_zones, token = gather_weights_start([shards[n] for n in late], conv_all)
    own = lambda g, s: lax.dynamic_update_slice(g, s[None], (chip,) + (0,) * s.ndim)
    full = {"w_in": own(w_in_all, shards["w_in"])}
    conv_full = own(conv_all, gdn_conv_w[0]).transpose(1, 0, 2).reshape(CONV_WIDTH, 3 * GDN_WIDTH)
    rows = lambda g: g.reshape(N_CHIPS * g.shape[1], g.shape[2])

    def late_weights(after):
        zones = gather_weights_wait(send_sems, recv_sems, late_src, late_zones, after)
        got = {n: own(z, shards[n]) for n, z in zip(late, zones)}
        return dict(w_out=rows(got["w_out"]), w_cq=rows(got["w_cq"]), w_ckv=rows(got["w_ckv"]),
                    w_co=got["w_co"].transpose(1, 0, 2).reshape(XATTN_WIDTH, D_MODEL),
                    w_mlp1=got["w_mlp1"], w_mlp2=rows(got["w_mlp2"]))

    in_flight = []

    def grads_ready(ready):
        names = list(ready)
        *started, tok = reduce_grads_start([ready[n] for n in names], name="reduce_grads_start_%d" % len(in_flight))
        in_flight.append((names, *started))
        return tok

    w_in_t = full["w_in"][:, :IN_SHARD].reshape(IN_DIM, D_MODEL)
    w = dict(wa_t=align_w_in_t(w_in_t), conv_w=conv_full, late=late_weights, grads_ready=grads_ready)
    sp = {n: wts[n] for n, _ in SMALL_DIMS}
    sp["norm_mix_g"] = sp["norm_mix_g"] + token[0, 0]

    loss_part, grad_x, g_big, g_small = local_step(x.reshape(T, D), mem.reshape(-1, D), loss_target.reshape(T, D), w, sp, B=B)

    small_pieces = [g_small[n] for n, _ in SMALL_DIMS] + [g_small["gdn_conv_w"], loss_part]
    small_sizes = [d for _, d in SMALL_DIMS] + [CONV_WIDTH * 3 * GDN_WIDTH, LANES]
    red_small = _unpack_rows(all_reduce_small(_pack_rows(small_pieces, SMALL_ROWS)), small_sizes)
    grads = {n: p.reshape(1, d) for (n, d), p in zip(SMALL_DIMS, red_small)}
    conv_grad = lax.dynamic_slice(red_small[-2].reshape(CONV_WIDTH, 3 * GDN_WIDTH), (0, chip * CONV_SHARD[1]), CONV_SHARD)
    grads["gdn_conv_w"] = conv_grad.reshape((1,) + CONV_SHARD)
    loss = red_small[-1][0]

    parts, zones = {}, {}

    def wait_group(k, after):
        names, send_sems, recv_sems, thru, land = in_flight[k]
        thru, land = reduce_grads_wait(send_sems, recv_sems, thru, land, after, name="reduce_grads_wait_%d" % k)
        parts.update(zip(names, thru))
        zones.update(zip(names, land))

    wait_group(0, grad_x)
    wait_group(1, grad_x)
    dev = 2 * chip + core[0]
    where = jnp.stack([chip, core[0]] + [dev ^ r for r in range(1, N_DEV)]).astype(jnp.int32)
    mine = [sum_partials(parts[n], zones[n], where, name="sum_partials_" + n) for n in late]
    theirs = swap_reduced_halves(mine, name="swap_reduced_halves")

    delta, new_m, new_v = {}, {}, {}
    for n, a, b in zip(late, mine, theirs):
        g, d, nm, nv = adamw_halves(wts[n][0], a, b, mom[n][0], var[n][0], core, name="adamw_" + n)
        grads[n], delta[n], new_m[n], new_v[n] = g[None], d[None], nm[None], nv[None]
    wait_group(2, new_v[late[-1]])
    mine_in = sum_partials(parts["w_in"], zones["w_in"], where, name="sum_partials_w_in")
    (theirs_in,) = swap_reduced_halves([mine_in], name="swap_reduced_halves_w_in")
    south = core[0] == 0
    g_in_t = jnp.concatenate([jnp.where(south, mine_in, theirs_in), jnp.where(south, theirs_in, mine_in)])[:IN_SHARD]
    back = lambda t: jnp.swapaxes(t, 0, 1)[None]
    d, nm, nv = adamw(in_t(w_in), g_in_t, in_t(m_w_in), in_t(v_w_in), name="adamw_w_in", tc=256)
    grads["w_in"], delta["w_in"], new_m["w_in"], new_v["w_in"] = back(g_in_t), back(d), back(nm), back(nv)
    small_names = [n for n, _ in SMALL_DIMS] + ["gdn_conv_w"]
    small_sz = [d for _, d in SMALL_DIMS] + [CONV_SHARD[0] * CONV_SHARD[1]]
    packed4 = [_pack_rows([src[n] for n in small_names], SMALL_ADAM_ROWS) for src in (wts, grads, mom, var)]
    outs = adamw(*packed4, name="adamw_small", tr=SMALL_ADAM_ROWS)
    for dst, buf in zip((delta, new_m, new_v), outs):
        for n, p in zip(small_names, _unpack_rows(buf, small_sz)):
            dst[n] = p.reshape(wts[n].shape)

    return (loss, grad_x.reshape(B, S, D), *[grads[n] for n in WEIGHT_ORDER], *[delta[n] for n in WEIGHT_ORDER],
            *[new_m[n] for n in WEIGHT_ORDER], *[new_v[n] for n in WEIGHT_ORDER])
```

```python
import functools

import jax
import jax.numpy as jnp
import numpy as np
from jax import lax
from jax.experimental import pallas as pl
from jax.experimental.pallas import tpu as pltpu

f32 = jnp.float32
bf16 = jnp.bfloat16
MXU_DTYPE = jnp.bfloat16
WIRE_DTYPE = jnp.bfloat16
INV_PRECISION = lax.Precision.HIGH

D_MODEL = 1024
FOX_HEADS = 8
FOX_HEAD_DIM = 64
FOX_WIDTH = 512
GDN_HEADS = 4
GDN_HEAD_DIM = 128
GDN_WIDTH = 512
CONV_WIDTH = 4
GDN_CHUNK = 64
XATTN_HEADS = 4
XATTN_HEAD_DIM = 128
XATTN_WIDTH = 512
D_FF = 4096
IN_DIM = 3600
EPS = 1e-6
NEG_INF = -1e30
LANES = 128
ADAM_LR = 0.001
ADAM_B1 = 0.9
ADAM_B2 = 0.999
ADAM_EPS = 1e-08
ADAM_WD = 0.01
ADAM_STEP = 10
VMEM_LIMIT = 48 * 1024 * 1024

COL_FOX = 0
COL_GDN = 1536
COL_Z = 3072
COL_SMALL = 3584
IN_ALIGNED = 3840
IN_TILE = 768
SM_F = 0
SM_B = 8
SM_A = 12


def _cparams(*sem):
    return pltpu.CompilerParams(dimension_semantics=sem, vmem_limit_bytes=VMEM_LIMIT)


def _mx(v):
    return v.astype(MXU_DTYPE)


def _dot(a, b, dims, precision=None):
    return lax.dot_general(a, b, (dims, ((), ())), preferred_element_type=f32, precision=precision)


def _dotm(a, b, dims):
    return _dot(_mx(a), _mx(b), dims)


NN = ((1,), (0,))
NT = ((1,), (1,))
TN = ((0,), (0,))


def matmul(a, b, *, name, ta=False, tb=False, b_stacked=False, out_stacked=False, residual=None, relu2_out=False,
           relu2_bwd_aux=None, out_dtype=f32, tm=1024, tn=1024, tk=1024):
    M, K = (a.shape[1], a.shape[0]) if ta else a.shape
    if b_stacked:
        b_cols = b.shape[2]
        N, tk = (b.shape[1], min(tk, b_cols)) if tb else (N_CHIPS * b_cols, tk)
        tn = tn if tb else min(tn, b_cols)
        assert K == (N_CHIPS * b_cols if tb else b.shape[1]), (name, a.shape, b.shape)
    else:
        N = b.shape[0] if tb else b.shape[1]
    if out_stacked:
        tn = min(tn, N // N_CHIPS)
    tm, tn, tk = min(tm, M), min(tn, N), min(tk, K)
    assert M % tm == 0 and N % tn == 0 and K % tk == 0, (name, M, N, K)
    nk = K // tk
    has_res = residual is not None
    has_aux = relu2_bwd_aux is not None

    def body(*refs):
        a_ref, b_ref = refs[0], refs[1]
        pos = 2
        res_ref = aux_ref = None
        if has_res:
            res_ref = refs[pos]
            pos += 1
        if has_aux:
            aux_ref = refs[pos]
            pos += 1
        o_ref = refs[pos]
        k = pl.program_id(2)
        dims = ((0,) if ta else (1,), (1,) if tb else (0,))
        part = _dot(_mx(a_ref[...]), _mx(b_ref[...]), dims)

        def finish(r):
            if has_res:
                r = r + res_ref[...]
            if has_aux:
                r = r * (2.0 * jnp.sqrt(aux_ref[...].astype(f32)))
            if relu2_out:
                o_ref[...] = jnp.square(jnp.maximum(r, 0.0)).astype(o_ref.dtype)
            else:
                o_ref[...] = r.astype(o_ref.dtype)

        if nk == 1:
            finish(part)
            return
        acc_ref = refs[pos + 1]

        @pl.when(k == 0)
        def _():
            acc_ref[...] = part

        @pl.when((k > 0) & (k < nk - 1))
        def _():
            acc_ref[...] += part

        @pl.when(k == nk - 1)
        def _():
            finish(acc_ref[...] + part)

    a_spec = pl.BlockSpec((tk, tm), lambda i, j, k: (k, i)) if ta else pl.BlockSpec((tm, tk), lambda i, j, k: (i, k))
    if b_stacked and tb:
        per = b_cols // tk
        b_spec = pl.BlockSpec((None, tn, tk), lambda i, j, k: (k // per, j, k % per))
    elif b_stacked:
        per = b_cols // tn
        b_spec = pl.BlockSpec((None, tk, tn), lambda i, j, k: (j // per, k, j % per))
    else:
        b_spec = pl.BlockSpec((tn, tk), lambda i, j, k: (j, k)) if tb else pl.BlockSpec((tk, tn), lambda i, j, k: (k, j))
    if out_stacked:
        assert not (has_res or has_aux or relu2_out), name
        per_o = N // N_CHIPS // tn
        o_spec = pl.BlockSpec((None, tm, tn), lambda i, j, k: (j // per_o, i, j % per_o))
        out_full = (N_CHIPS, M, N // N_CHIPS)
    else:
        o_spec = pl.BlockSpec((tm, tn), lambda i, j, k: (i, j))
        out_full = (M, N)
    in_specs, args = [a_spec, b_spec], [a, b]
    if has_res:
        in_specs.append(o_spec)
        args.append(residual)
    if has_aux:
        in_specs.append(o_spec)
        args.append(relu2_bwd_aux)
    out_shape = [jax.ShapeDtypeStruct(out_full, out_dtype)]
    out_specs = [o_spec]
    res = pl.pallas_call(
        body, name=name, grid=(M // tm, N // tn, nk), in_specs=in_specs, out_specs=out_specs, out_shape=out_shape,
        scratch_shapes=[pltpu.VMEM((tm, tn), f32)] if nk > 1 else [],
        compiler_params=_cparams("parallel", "parallel", "arbitrary"),
    )(*args)
    return res[0]


def matmul_rows(a, b, extras, *, name, mode, tb=False, b_stacked=False, tm=1024, tk=1024):
    M, K = a.shape
    N = D_MODEL
    if b_stacked:
        assert tb, name
        tk = min(tk, b.shape[2])
        per = b.shape[2] // tk
        b_spec = pl.BlockSpec((None, N, tk), lambda i, k: (k // per, 0, k % per))
    elif tb:
        tk = min(tk, K)
        b_spec = pl.BlockSpec((N, tk), lambda i, k: (0, k))
    else:
        tk = min(tk, K)
        b_spec = pl.BlockSpec((tk, N), lambda i, k: (k, 0))
    tm = min(tm, M)
    assert M % tm == 0 and K % tk == 0, (name, M, K)
    nk = K // tk
    extras = [e for e in extras if e is not None]
    n_ex = len(extras)

    def body(*refs):
        a_ref, b_ref = refs[0], refs[1]
        ex = refs[2:2 + n_ex]
        o_ref = refs[2 + n_ex]
        n_out = 3 if mode == "loss" else 2
        s_ref = refs[1 + n_ex + n_out]
        i, k = pl.program_id(0), pl.program_id(1)
        part = _dot(_mx(a_ref[...]), _mx(b_ref[...]), ((1,), (1,) if tb else (0,)))

        def finish(y):
            if mode == "rms_fwd":
                y = y + ex[0][...]
                o_ref[...] = y
                s_ref[...] = (y * lax.rsqrt(jnp.mean(y * y, axis=-1, keepdims=True) + EPS) * ex[1][...]).astype(MXU_DTYPE)
                return

            @pl.when(i == 0)
            def _():
                s_ref[...] = jnp.zeros_like(s_ref)

            if mode == "rms_bwd":
                xv, gv = ex[0][...], ex[1][...]
                rstd = lax.rsqrt(jnp.mean(xv * xv, axis=-1, keepdims=True) + EPS)
                xhat = xv * rstd
                gd = y * gv
                dx = rstd * (gd - xhat * jnp.mean(gd * xhat, axis=-1, keepdims=True))
                o_ref[...] = dx + ex[2][...] if n_ex == 3 else dx
                s_ref[...] += jnp.sum(y * xhat, axis=0, keepdims=True)
            else:
                e = y + ex[0][...] - ex[1][...]
                o_ref[...] = e * (1.0 / N)
                refs[3 + n_ex][...] = (e * (1.0 / N)).astype(MXU_DTYPE)
                tot = 0.5 * jnp.sum(jnp.mean(e * e, axis=-1, keepdims=True), axis=0, keepdims=True)
                s_ref[...] += jnp.broadcast_to(tot, s_ref.shape)

        if nk == 1:
            finish(part)
            return
        acc_ref = refs[2 + n_ex + n_out]

        @pl.when(k == 0)
        def _():
            acc_ref[...] = part

        @pl.when((k > 0) & (k < nk - 1))
        def _():
            acc_ref[...] += part

        @pl.when(k == nk - 1)
        def _():
            finish(acc_ref[...] + part)

    row = pl.BlockSpec((tm, N), lambda i, k: (i, 0))
    vec = pl.BlockSpec((1, N), lambda i, k: (0, 0))
    if mode == "rms_bwd":
        ex_specs = [row, vec] + ([row] if n_ex == 3 else [])
        s_shape, s_spec = jax.ShapeDtypeStruct((1, N), f32), vec
    elif mode == "rms_fwd":
        ex_specs = [row, vec]
        s_shape, s_spec = jax.ShapeDtypeStruct((M, N), MXU_DTYPE), row
    else:
        ex_specs = [row, row]
        s_shape, s_spec = jax.ShapeDtypeStruct((1, LANES), f32), pl.BlockSpec((1, LANES), lambda i, k: (0, 0))
    return pl.pallas_call(
        body, name=name, grid=(M // tm, nk),
        in_specs=[pl.BlockSpec((tm, tk), lambda i, k: (i, k)), b_spec] + ex_specs,
        out_specs=[row] * (2 if mode == "loss" else 1) + [s_spec],
        out_shape=[jax.ShapeDtypeStruct((M, N), f32)] + ([jax.ShapeDtypeStruct((M, N), MXU_DTYPE)] if mode == "loss" else [])
        + [s_shape],
        scratch_shapes=[pltpu.VMEM((tm, N), f32)] if nk > 1 else [],
        compiler_params=_cparams("arbitrary", "arbitrary"),
    )(a, b, *extras)


def rms_fwd(x, g, *, name, tr=1024):
    R, D = x.shape
    tr = min(tr, R)

    def body(x_ref, g_ref, o_ref):
        xv = x_ref[...]
        y = xv * lax.rsqrt(jnp.mean(xv * xv, axis=-1, keepdims=True) + EPS)
        o_ref[...] = (y * g_ref[...]).astype(o_ref.dtype)

    return pl.pallas_call(
        body, name=name, grid=(R // tr,),
        in_specs=[pl.BlockSpec((tr, D), lambda i: (i, 0)), pl.BlockSpec((1, D), lambda i: (0, 0))],
        out_specs=pl.BlockSpec((tr, D), lambda i: (i, 0)),
        out_shape=jax.ShapeDtypeStruct((R, D), MXU_DTYPE),
        compiler_params=_cparams("parallel"),
    )(x, g)


def rms_bwd(x, g, dh, residual, *, name, tr=512):
    R, D = x.shape
    tr = min(tr, R)
    has_res = residual is not None

    def body(*refs):
        if has_res:
            x_ref, g_ref, dh_ref, res_ref, dx_ref, dg_ref = refs
        else:
            x_ref, g_ref, dh_ref, dx_ref, dg_ref = refs
        xv = x_ref[...]
        rstd = lax.rsqrt(jnp.mean(xv * xv, axis=-1, keepdims=True) + EPS)
        xhat = xv * rstd
        dh = dh_ref[...].astype(f32)
        gd = dh * g_ref[...]
        dx = rstd * (gd - xhat * jnp.mean(gd * xhat, axis=-1, keepdims=True))
        if has_res:
            dx = dx + res_ref[...]
        dx_ref[...] = dx

        @pl.when(pl.program_id(0) == 0)
        def _():
            dg_ref[...] = jnp.zeros_like(dg_ref)

        dg_ref[...] += jnp.sum(dh * xhat, axis=0, keepdims=True)

    row = pl.BlockSpec((tr, D), lambda i: (i, 0))
    vec = pl.BlockSpec((1, D), lambda i: (0, 0))
    in_specs = [row, vec, row] + ([row] if has_res else [])
    args = [x, g, dh] + ([residual] if has_res else [])
    return pl.pallas_call(
        body, name=name, grid=(R // tr,), in_specs=in_specs, out_specs=[row, vec],
        out_shape=[jax.ShapeDtypeStruct((R, D), f32), jax.ShapeDtypeStruct((1, D), f32)],
        compiler_params=_cparams("arbitrary"),
    )(*args)


def loss_head(y, target, *, tr=512):
    R, D = y.shape
    tr = min(tr, R)

    def body(y_ref, t_ref, dy_ref, loss_ref):
        e = y_ref[...] - t_ref[...]
        dy_ref[...] = e * (1.0 / D)

        @pl.when(pl.program_id(0) == 0)
        def _():
            loss_ref[...] = jnp.zeros_like(loss_ref)

        part = 0.5 * jnp.sum(jnp.mean(e * e, axis=-1, keepdims=True), axis=0, keepdims=True)
        loss_ref[...] += jnp.broadcast_to(part, loss_ref.shape)

    row = pl.BlockSpec((tr, D), lambda i: (i, 0))
    return pl.pallas_call(
        body, name="loss_head", grid=(R // tr,), in_specs=[row, row],
        out_specs=[row, pl.BlockSpec((1, LANES), lambda i: (0, 0))],
        out_shape=[jax.ShapeDtypeStruct((R, D), f32), jax.ShapeDtypeStruct((1, LANES), f32)],
        compiler_params=_cparams("arbitrary"),
    )(y, target)


def _head_rms(v, g):
    r = lax.rsqrt(jnp.mean(v * v, axis=-1, keepdims=True) + EPS)
    return v * r * g, r


def _head_rms_bwd(v, r, g, dn):
    vhat = v * r
    gd = dn * g
    dv = r * (gd - vhat * jnp.mean(gd * vhat, axis=-1, keepdims=True))
    return dv, jnp.sum(dn * vhat, axis=0, keepdims=True)


def _softmax_rows(s):
    m = jnp.max(s, axis=-1, keepdims=True)
    e = jnp.exp(s - m)
    return e / jnp.sum(e, axis=-1, keepdims=True)


def xattn_fwd(cq, ckv, gq, gk, *, B, tq=1024):
    T = cq.shape[0]
    S = T // B
    M = ckv.shape[0] // B
    tq = min(tq, S)
    nq = S // tq
    hd, W = XATTN_HEAD_DIM, XATTN_WIDTH
    scale = hd ** -0.5

    def body(q_ref, k_ref, v_ref, gq_ref, gk_ref, o_ref):
        for h in range(XATTN_HEADS):
            sl = slice(h * hd, (h + 1) * hd)
            qn, _ = _head_rms(q_ref[:, sl], gq_ref[...])
            kn, _ = _head_rms(k_ref[:, sl], gk_ref[...])
            p = _softmax_rows(_dot(_mx(qn), _mx(kn), NT) * scale)
            o_ref[:, sl] = _dot(_mx(p), _mx(v_ref[:, sl]), NN).astype(o_ref.dtype)

    vec = pl.BlockSpec((1, hd), lambda b, i: (0, 0))
    qspec = pl.BlockSpec((tq, W), lambda b, i: (b * nq + i, 0))
    return pl.pallas_call(
        body, name="xattn_fwd", grid=(B, nq),
        in_specs=[qspec, pl.BlockSpec((M, W), lambda b, i: (b, 0)), pl.BlockSpec((M, W), lambda b, i: (b, 1)), vec, vec],
        out_specs=qspec, out_shape=jax.ShapeDtypeStruct((T, W), MXU_DTYPE),
        compiler_params=_cparams("parallel", "parallel"),
    )(cq, ckv, ckv, gq, gk)


def xattn_bwd(cq, ckv, gq, gk, dco, *, B, tq=1024):
    T = cq.shape[0]
    S = T // B
    M = ckv.shape[0] // B
    tq = min(tq, S)
    nq = S // tq
    hd, W = XATTN_HEAD_DIM, XATTN_WIDTH
    scale = hd ** -0.5

    def body(q_ref, k_ref, v_ref, gq_ref, gk_ref, do_ref, dq_ref, dkv_ref, dgq_ref, dgk_ref, dkn_acc, dv_acc):
        b, i = pl.program_id(0), pl.program_id(1)

        @pl.when((b == 0) & (i == 0))
        def _():
            dgq_ref[...] = jnp.zeros_like(dgq_ref)
            dgk_ref[...] = jnp.zeros_like(dgk_ref)

        @pl.when(i == 0)
        def _():
            dkn_acc[...] = jnp.zeros_like(dkn_acc)
            dv_acc[...] = jnp.zeros_like(dv_acc)

        gqv, gkv = gq_ref[...], gk_ref[...]
        for h in range(XATTN_HEADS):
            sl = slice(h * hd, (h + 1) * hd)
            q, k, v = q_ref[:, sl], k_ref[:, sl], v_ref[:, sl]
            qn, rq = _head_rms(q, gqv)
            kn, _ = _head_rms(k, gkv)
            p = _softmax_rows(_dot(_mx(qn), _mx(kn), NT) * scale)
            do = do_ref[:, sl]
            dv_acc[:, sl] += _dot(_mx(p), _mx(do), TN)
            dp = _dot(_mx(do), _mx(v), NT)
            ds = p * (dp - jnp.sum(dp * p, axis=-1, keepdims=True)) * scale
            dqn = _dot(_mx(ds), _mx(kn), NN)
            dkn_acc[:, sl] += _dot(_mx(ds), _mx(qn), TN)
            dq, dgq = _head_rms_bwd(q, rq, gqv, dqn)
            dq_ref[:, sl] = dq.astype(dq_ref.dtype)
            dgq_ref[...] += dgq

        @pl.when(i == nq - 1)
        def _():
            for h in range(XATTN_HEADS):
                sl = slice(h * hd, (h + 1) * hd)
                k = k_ref[:, sl]
                rk = lax.rsqrt(jnp.mean(k * k, axis=-1, keepdims=True) + EPS)
                dk, dgk = _head_rms_bwd(k, rk, gkv, dkn_acc[:, sl])
                dkv_ref[:, sl] = dk.astype(dkv_ref.dtype)
                dkv_ref[:, slice(W + h * hd, W + (h + 1) * hd)] = dv_acc[:, sl].astype(dkv_ref.dtype)
                dgk_ref[...] += dgk

    vec = pl.BlockSpec((1, hd), lambda b, i: (0, 0))
    qspec = pl.BlockSpec((tq, W), lambda b, i: (b * nq + i, 0))
    return pl.pallas_call(
        body, name="xattn_bwd", grid=(B, nq),
        in_specs=[qspec, pl.BlockSpec((M, W), lambda b, i: (b, 0)), pl.BlockSpec((M, W), lambda b, i: (b, 1)), vec, vec, qspec],
        out_specs=[qspec, pl.BlockSpec((M, 2 * W), lambda b, i: (b, 0)), vec, vec],
        out_shape=[jax.ShapeDtypeStruct((T, W), MXU_DTYPE), jax.ShapeDtypeStruct((B * M, 2 * W), MXU_DTYPE),
                   jax.ShapeDtypeStruct((1, hd), f32), jax.ShapeDtypeStruct((1, hd), f32)],
        scratch_shapes=[pltpu.VMEM((M, W), f32), pltpu.VMEM((M, W), f32)],
        compiler_params=_cparams("arbitrary", "arbitrary"),
    )(cq, ckv, ckv, gq, gk, dco)


FOX_PAIRS = FOX_HEADS // 2


def _fox_scores(qn, kn, ccol, crow, q0, tq, S, scale):
    s = _dot(_mx(qn), _mx(kn), NT) * scale + ccol - crow
    qpos = q0 + lax.broadcasted_iota(jnp.int32, (tq, S), 0)
    kpos = lax.broadcasted_iota(jnp.int32, (tq, S), 1)
    return jnp.where(kpos <= qpos, s, NEG_INF)


def fox_fwd(P, ccol, crow, gq, gk, go, *, B, tq=256):
    T = P.shape[0]
    S = T // B
    tq = min(tq, S)
    nq = S // tq
    hd = FOX_HEAD_DIM
    scale = hd ** -0.5

    def body(q_ref, k_ref, v_ref, ccol_ref, crow_ref, gq_ref, gk_ref, go_ref, o_ref, oa_ref):
        q0 = pl.program_id(2) * tq
        for e in range(2):
            sl = slice(e * hd, (e + 1) * hd)
            qn, _ = _head_rms(q_ref[:, sl], gq_ref[:, sl])
            kn, _ = _head_rms(k_ref[:, sl], gk_ref[:, sl])
            p = _softmax_rows(_fox_scores(qn, kn, ccol_ref[0, e], crow_ref[0, e], q0, tq, S, scale))
            o = _dot(_mx(p), _mx(v_ref[:, sl]), NN)
            o_ref[:, sl] = o
            oa_ref[:, sl] = _head_rms(o, go_ref[:, sl])[0].astype(oa_ref.dtype)

    W = 2 * hd
    vec = pl.BlockSpec((1, W), lambda b, h, i: (0, 0))
    ospec = pl.BlockSpec((tq, W), lambda b, h, i: (b * nq + i, h))
    return pl.pallas_call(
        body, name="fox_fwd", grid=(B, FOX_PAIRS, nq),
        in_specs=[pl.BlockSpec((tq, W), lambda b, h, i: (b * nq + i, h)),
                  pl.BlockSpec((S, W), lambda b, h, i: (b, FOX_PAIRS + h)),
                  pl.BlockSpec((S, W), lambda b, h, i: (b, 2 * FOX_PAIRS + h)),
                  pl.BlockSpec((1, 2, tq, 1), lambda b, h, i: (b, h, i, 0)),
                  pl.BlockSpec((1, 2, 1, S), lambda b, h, i: (b, h, 0, 0)), vec, vec, vec],
        out_specs=[ospec, ospec],
        out_shape=[jax.ShapeDtypeStruct((T, FOX_WIDTH), f32), jax.ShapeDtypeStruct((T, FOX_WIDTH), MXU_DTYPE)],
        compiler_params=_cparams("parallel", "parallel", "parallel"),
    )(P, P, P, ccol, crow, gq, gk, go)


def fox_bwd(P, ccol, crow, gq, gk, go, o_raw, d_oab, *, B, tq=256):
    T = P.shape[0]
    S = T // B
    tq = min(tq, S)
    nq = S // tq
    hd = FOX_HEAD_DIM
    scale = hd ** -0.5

    def body(q_ref, k_ref, v_ref, ccol_ref, crow_ref, gq_ref, gk_ref, go_ref, o_ref, doa_ref,
             dq_ref, dk_ref, dv_ref, dccol_ref, dcrow_ref, dgq_ref, dgk_ref, dgo_ref, dkn_acc, dv_acc, dcrow_acc):
        b, h, i = pl.program_id(0), pl.program_id(1), pl.program_id(2)
        q0 = i * tq

        @pl.when((b == 0) & (h == 0) & (i == 0))
        def _():
            dgq_ref[...] = jnp.zeros_like(dgq_ref)
            dgk_ref[...] = jnp.zeros_like(dgk_ref)
            dgo_ref[...] = jnp.zeros_like(dgo_ref)

        @pl.when(i == 0)
        def _():
            dkn_acc[...] = jnp.zeros_like(dkn_acc)
            dv_acc[...] = jnp.zeros_like(dv_acc)
            dcrow_acc[...] = jnp.zeros_like(dcrow_acc)

        for e in range(2):
            sl = slice(e * hd, (e + 1) * hd)
            q, k, v = q_ref[:, sl], k_ref[:, sl], v_ref[:, sl]
            gqv, gkv, gov = gq_ref[:, sl], gk_ref[:, sl], go_ref[:, sl]
            qn, rq = _head_rms(q, gqv)
            kn, rk = _head_rms(k, gkv)
            p = _softmax_rows(_fox_scores(qn, kn, ccol_ref[0, e], crow_ref[0, e], q0, tq, S, scale))
            o = o_ref[:, sl]
            ro = lax.rsqrt(jnp.mean(o * o, axis=-1, keepdims=True) + EPS)
            do, dgo = _head_rms_bwd(o, ro, gov, doa_ref[:, sl])
            dgo_ref[:, sl] += dgo
            dv_acc[e] += _dot(_mx(p), _mx(do), TN)
            dp = _dot(_mx(do), _mx(v), NT)
            ds = p * (dp - jnp.sum(do * o, axis=-1, keepdims=True))
            dccol_ref[0, e] = jnp.sum(ds, axis=1, keepdims=True)
            dcrow_acc[e] -= jnp.sum(ds, axis=0, keepdims=True)
            dqn = _dot(_mx(ds), _mx(kn), NN) * scale
            dkn_acc[e] += _dot(_mx(ds), _mx(qn), TN) * scale
            dq, dgq = _head_rms_bwd(q, rq, gqv, dqn)
            dq_ref[:, sl] = dq.astype(dq_ref.dtype)
            dgq_ref[:, sl] += dgq

        @pl.when(i == nq - 1)
        def _():
            for e in range(2):
                sl = slice(e * hd, (e + 1) * hd)
                k = k_ref[:, sl]
                gkv = gk_ref[:, sl]
                rk = lax.rsqrt(jnp.mean(k * k, axis=-1, keepdims=True) + EPS)
                dk, dgk = _head_rms_bwd(k, rk, gkv, dkn_acc[e])
                dk_ref[:, sl] = dk.astype(dk_ref.dtype)
                dv_ref[:, sl] = dv_acc[e].astype(dv_ref.dtype)
                dgk_ref[:, sl] += dgk
                dcrow_ref[0, e] = dcrow_acc[e]

    W = 2 * hd
    vec = pl.BlockSpec((1, W), lambda b, h, i: (0, 0))
    qspec = pl.BlockSpec((tq, W), lambda b, h, i: (b * nq + i, h))
    kvout = pl.BlockSpec((S, W), lambda b, h, i: (b, h))
    colspec = pl.BlockSpec((1, 2, tq, 1), lambda b, h, i: (b, h, i, 0))
    rowspec = pl.BlockSpec((1, 2, 1, S), lambda b, h, i: (b, h, 0, 0))
    return pl.pallas_call(
        body, name="fox_bwd", grid=(B, FOX_PAIRS, nq),
        in_specs=[qspec,
                  pl.BlockSpec((S, W), lambda b, h, i: (b, FOX_PAIRS + h)),
                  pl.BlockSpec((S, W), lambda b, h, i: (b, 2 * FOX_PAIRS + h)),
                  colspec, rowspec, vec, vec, vec, qspec, qspec],
        out_specs=[qspec, kvout, kvout, colspec, rowspec, vec, vec, vec],
        out_shape=[jax.ShapeDtypeStruct((T, FOX_WIDTH), MXU_DTYPE), jax.ShapeDtypeStruct((T, FOX_WIDTH), MXU_DTYPE),
                   jax.ShapeDtypeStruct((T, FOX_WIDTH), MXU_DTYPE),
                   jax.ShapeDtypeStruct((B, FOX_HEADS, S, 1), f32), jax.ShapeDtypeStruct((B, FOX_HEADS, 1, S), f32),
                   jax.ShapeDtypeStruct((1, W), f32), jax.ShapeDtypeStruct((1, W), f32), jax.ShapeDtypeStruct((1, W), f32)],
        scratch_shapes=[pltpu.VMEM((2, S, hd), f32), pltpu.VMEM((2, S, hd), f32), pltpu.VMEM((2, 1, S), f32)],
        compiler_params=_cparams("arbitrary", "arbitrary", "arbitrary"),
    )(P, P, P, ccol, crow, gq, gk, go, o_raw, d_oab)


FOX_TQ = 512
FOX_TK = FOX_TQ
GROUP_PRECISION = lax.Precision.HIGH


def _head_mean(v):
    n = v.shape[1]
    r = lax.broadcasted_iota(jnp.int32, (n, n), 0) // FOX_HEAD_DIM
    c = lax.broadcasted_iota(jnp.int32, (n, n), 1) // FOX_HEAD_DIM
    ones = (r == c).astype(bf16)
    hi = v.astype(bf16)
    lo = (v - hi.astype(f32)).astype(bf16)
    return (_dot(hi, ones, NN) + _dot(lo, ones, NN)) * (1.0 / FOX_HEAD_DIM)


def fox_prep_fwd(P, gq, gk, *, tr=1024):
    T = P.shape[0]
    tr = min(tr, T)
    scale = FOX_HEAD_DIM ** -0.5

    def body(q_ref, k_ref, v_ref, gq_ref, gk_ref, qn_ref, kn_ref, vb_ref):
        q, k = q_ref[...], k_ref[...]
        qn_ref[...] = (q * lax.rsqrt(_head_mean(q * q) + EPS) * (gq_ref[...] * scale)).astype(qn_ref.dtype)
        kn_ref[...] = (k * lax.rsqrt(_head_mean(k * k) + EPS) * gk_ref[...]).astype(kn_ref.dtype)
        vb_ref[...] = v_ref[...].astype(vb_ref.dtype)

    W = FOX_WIDTH
    col = lambda j: pl.BlockSpec((tr, W), lambda i: (i, j))
    vec = pl.BlockSpec((1, W), lambda i: (0, 0))
    out = jax.ShapeDtypeStruct((T, W), MXU_DTYPE)
    return pl.pallas_call(
        body, name="fox_prep_fwd", grid=(T // tr,), in_specs=[col(0), col(1), col(2), vec, vec],
        out_specs=[col(0)] * 3, out_shape=[out] * 3, compiler_params=_cparams("parallel"),
    )(P, P, P, gq, gk)


def fox_prep_bwd(P, gq, gk, dqn, dkn, *, tr=1024):
    T = P.shape[0]
    tr = min(tr, T)
    scale = FOX_HEAD_DIM ** -0.5

    def body(q_ref, k_ref, gq_ref, gk_ref, dqn_ref, dkn_ref, dq_ref, dk_ref, dgq_ref, dgk_ref):
        @pl.when(pl.program_id(0) == 0)
        def _():
            dgq_ref[...] = jnp.zeros_like(dgq_ref)
            dgk_ref[...] = jnp.zeros_like(dgk_ref)

        def one(x, g, dn, dx_ref, dg_ref):
            r = lax.rsqrt(_head_mean(x * x) + EPS)
            xhat = x * r
            gd = dn * g
            dx_ref[...] = (r * (gd - xhat * _head_mean(gd * xhat))).astype(dx_ref.dtype)
            return jnp.sum(dn * xhat, axis=0, keepdims=True)

        dgq_ref[...] += scale * one(q_ref[...], gq_ref[...] * scale, dqn_ref[...], dq_ref, dgq_ref)
        dgk_ref[...] += one(k_ref[...], gk_ref[...], dkn_ref[...], dk_ref, dgk_ref)

    W = FOX_WIDTH
    col = lambda j: pl.BlockSpec((tr, W), lambda i: (i, j))
    vec = pl.BlockSpec((1, W), lambda i: (0, 0))
    return pl.pallas_call(
        body, name="fox_prep_bwd", grid=(T // tr,), in_specs=[col(0), col(1), vec, vec, col(0), col(0)],
        out_specs=[col(0), col(0), vec, vec],
        out_shape=[jax.ShapeDtypeStruct((T, W), MXU_DTYPE), jax.ShapeDtypeStruct((T, W), MXU_DTYPE),
                   jax.ShapeDtypeStruct((1, W), f32), jax.ShapeDtypeStruct((1, W), f32)],
        compiler_params=_cparams("arbitrary"),
    )(P, P, gq, gk, dqn, dkn)


def _fox_tile_scores(q, k_ref, ccol_ref, cq, e, j, sl, mask_off):
    tq, tk = FOX_TQ, FOX_TK
    rows = pl.ds(pl.multiple_of(j * tk, tk), tk)
    k = k_ref[rows, sl]
    s = _dot(k, q, NT) + cq - ccol_ref[0, e, rows, :]
    if mask_off is not None:
        key = lax.broadcasted_iota(jnp.int32, (tk, tq), 0) + mask_off
        query = lax.broadcasted_iota(jnp.int32, (tk, tq), 1)
        s = jnp.where(key <= query, s, NEG_INF)
    return s, k, rows


def _fox_sweep(i, update, carry):
    nd = FOX_TQ // FOX_TK
    carry = lax.fori_loop(0, i * nd, lambda j, cr: update(cr, j, None), carry)
    for d in range(nd):
        carry = update(carry, i * nd + d, d * FOX_TK)
    return carry


def fox_core_fwd(qn, kn, vb, ccol, crow, go, *, B):
    T = qn.shape[0]
    S = T // B
    tq = FOX_TQ
    nq = S // tq
    hd = FOX_HEAD_DIM

    def body(q_ref, k_ref, v_ref, ccol_ref, crow_ref, go_ref, o_ref, oa_ref, lse_ref):
        i = pl.program_id(2)
        for e in range(2):
            sl = slice(e * hd, (e + 1) * hd)
            q = q_ref[:, sl]
            cq = crow_ref[0, e, i]

            def update(carry, j, mask_off):
                m, l, acc = carry
                s, _, rows = _fox_tile_scores(q, k_ref, ccol_ref, cq, e, j, sl, mask_off)
                m2 = jnp.maximum(m, jnp.max(s, axis=0, keepdims=True))
                a = jnp.exp(m - m2)
                p = jnp.exp(s - m2)
                return m2, a * l + jnp.sum(p, axis=0, keepdims=True), a * acc + _dot(v_ref[rows, sl], _mx(p), TN)

            carry = (jnp.full((1, tq), NEG_INF, f32), jnp.zeros((1, tq), f32), jnp.zeros((hd, tq), f32))
            m, l, acc = _fox_sweep(i, update, carry)
            o = (acc / l).T
            o_ref[:, sl] = o
            oa_ref[:, sl] = _head_rms(o, go_ref[:, sl])[0].astype(oa_ref.dtype)
            lse_ref[0, e, 0] = m + jnp.log(l)

    W = 2 * hd
    qspec = pl.BlockSpec((tq, W), lambda b, h, i: (b * nq + i, h))
    kspec = pl.BlockSpec((S, W), lambda b, h, i: (b, h))
    return pl.pallas_call(
        body, name="fox_core_fwd", grid=(B, FOX_PAIRS, nq),
        in_specs=[qspec, kspec, kspec, pl.BlockSpec((1, 2, S, 1), lambda b, h, i: (b, h, 0, 0)),
                  pl.BlockSpec((1, 2, nq, 1, tq), lambda b, h, i: (b, h, 0, 0, 0)),
                  pl.BlockSpec((1, W), lambda b, h, i: (0, 0))],
        out_specs=[qspec, qspec, pl.BlockSpec((1, 2, 1, 1, tq), lambda b, h, i: (b, h, i, 0, 0))],
        out_shape=[jax.ShapeDtypeStruct((T, FOX_WIDTH), f32), jax.ShapeDtypeStruct((T, FOX_WIDTH), MXU_DTYPE),
                   jax.ShapeDtypeStruct((B, FOX_HEADS, nq, 1, tq), f32)],
        compiler_params=_cparams("parallel", "parallel", "parallel"),
    )(qn, kn, vb, ccol, crow, go)


def fox_core_bwd(qn, kn, vb, ccol, crow, go, o_raw, lse, d_oab, *, B):
    T = qn.shape[0]
    S = T // B
    tq = FOX_TQ
    nq = S // tq
    hd = FOX_HEAD_DIM

    def body(q_ref, k_ref, v_ref, ccol_ref, crow_ref, go_ref, o_ref, lse_ref, doa_ref,
             dq_ref, dk_ref, dv_ref, dckey_ref, dcrow_ref, dgo_ref, dk_acc, dv_acc, dck_acc):
        b, h, i = pl.program_id(0), pl.program_id(1), pl.program_id(2)

        @pl.when((b == 0) & (h == 0) & (i == 0))
        def _():
            dgo_ref[...] = jnp.zeros_like(dgo_ref)

        @pl.when(i == 0)
        def _():
            dk_acc[...] = jnp.zeros_like(dk_acc)
            dv_acc[...] = jnp.zeros_like(dv_acc)
            dck_acc[...] = jnp.zeros_like(dck_acc)

        for e in range(2):
            sl = slice(e * hd, (e + 1) * hd)
            q = q_ref[:, sl]
            cq = crow_ref[0, e, i]
            lse_e = lse_ref[0, e, 0]
            o = o_ref[:, sl]
            ro = lax.rsqrt(jnp.mean(o * o, axis=-1, keepdims=True) + EPS)
            do, dgo = _head_rms_bwd(o, ro, go_ref[:, sl], doa_ref[:, sl])
            dgo_ref[:, sl] += dgo
            delta = jnp.sum((do * o).T, axis=0, keepdims=True)
            do_b = _mx(do)

            def update(carry, j, mask_off):
                dq, dcq = carry
                s, k, rows = _fox_tile_scores(q, k_ref, ccol_ref, cq, e, j, sl, mask_off)
                p = jnp.exp(s - lse_e)
                dv_acc[e, rows, :] += _dot(_mx(p), do_b, NN)
                ds = p * (_dot(v_ref[rows, sl], do_b, NT) - delta)
                dck_acc[e, rows, :] -= jnp.sum(ds, axis=1, keepdims=True)
                ds_b = _mx(ds)
                dk_acc[e, rows, :] += _dot(ds_b, q, NN)
                return dq + _dot(ds_b, k, TN), dcq + jnp.sum(ds, axis=0, keepdims=True)

            dq, dcq = _fox_sweep(i, update, (jnp.zeros((tq, hd), f32), jnp.zeros((1, tq), f32)))
            dq_ref[:, sl] = dq
            dcrow_ref[0, e, 0] = dcq

        @pl.when(i == nq - 1)
        def _():
            for e in range(2):
                sl = slice(e * hd, (e + 1) * hd)
                dk_ref[:, sl] = dk_acc[e]
                dv_ref[:, sl] = dv_acc[e].astype(dv_ref.dtype)
                dckey_ref[0, e] = jnp.transpose(jnp.broadcast_to(dck_acc[e], (S, LANES)))[0:1, :]

    W = 2 * hd
    qspec = pl.BlockSpec((tq, W), lambda b, h, i: (b * nq + i, h))
    kspec = pl.BlockSpec((S, W), lambda b, h, i: (b, h))
    colspec = pl.BlockSpec((1, 2, S, 1), lambda b, h, i: (b, h, 0, 0))
    rowspec = pl.BlockSpec((1, 2, nq, 1, tq), lambda b, h, i: (b, h, 0, 0, 0))
    tilespec = pl.BlockSpec((1, 2, 1, 1, tq), lambda b, h, i: (b, h, i, 0, 0))
    vec = pl.BlockSpec((1, W), lambda b, h, i: (0, 0))
    return pl.pallas_call(
        body, name="fox_core_bwd", grid=(B, FOX_PAIRS, nq),
        in_specs=[qspec, kspec, kspec, colspec, rowspec, vec, qspec, tilespec, qspec],
        out_specs=[qspec, kspec, kspec, pl.BlockSpec((1, 2, 1, S), lambda b, h, i: (b, h, 0, 0)), tilespec, vec],
        out_shape=[jax.ShapeDtypeStruct((T, FOX_WIDTH), f32), jax.ShapeDtypeStruct((T, FOX_WIDTH), f32),
                   jax.ShapeDtypeStruct((T, FOX_WIDTH), MXU_DTYPE),
                   jax.ShapeDtypeStruct((B, FOX_HEADS, 1, S), f32), jax.ShapeDtypeStruct((B, FOX_HEADS, nq, 1, tq), f32),
                   jax.ShapeDtypeStruct((1, W), f32)],
        scratch_shapes=[pltpu.VMEM((2, S, hd), f32), pltpu.VMEM((2, S, hd), f32), pltpu.VMEM((2, S, 1), f32)],
        compiler_params=_cparams("arbitrary", "arbitrary", "arbitrary"),
    )(qn, kn, vb, ccol, crow, go, o_raw, lse, d_oab)


def _lane_mask(lo, hi, shape):
    lane = lax.broadcasted_iota(jnp.int32, shape, 1)
    return (lane >= lo) & (lane < hi)


def _cumsum_rows(v, period, reverse=False):
    n = v.shape[0]
    pos = lax.broadcasted_iota(jnp.int32, v.shape, 0) % period
    sh = 1
    while sh < period:
        if reverse:
            v = v + jnp.where(pos + sh < period, pltpu.roll(v, n - sh, 0), 0.0)
        else:
            v = v + jnp.where(pos >= sh, pltpu.roll(v, sh, 0), 0.0)
        sh *= 2
    return v


def _gate_values(z, bias, alog):
    zb = z + bias
    ls = jax.nn.log_sigmoid(zb)
    beta = jax.nn.sigmoid(z)
    g = -jnp.exp(alog) * jax.nn.softplus(zb)
    return zb, ls, beta, g


def gates_fwd(P, bias, alog, *, B):
    T = P.shape[0]
    S = T // B

    def body(z_ref, bias_ref, alog_ref, o_ref):
        z = z_ref[...]
        _, ls, beta, g = _gate_values(z, bias_ref[...], alog_ref[...])
        c = _cumsum_rows(ls, S)
        gc = _cumsum_rows(g, GDN_CHUNK)
        o = jnp.where(_lane_mask(SM_F, SM_F + FOX_HEADS, z.shape), c, 0.0)
        o = jnp.where(_lane_mask(SM_B, SM_B + GDN_HEADS, z.shape), beta, o)
        o = jnp.where(_lane_mask(SM_A, SM_A + GDN_HEADS, z.shape), gc, o)
        o_ref[...] = o

    vec = pl.BlockSpec((1, LANES), lambda b: (0, 0))
    return pl.pallas_call(
        body, name="gates_fwd", grid=(B,),
        in_specs=[pl.BlockSpec((S, LANES), lambda b: (b, COL_SMALL // LANES)), vec, vec],
        out_specs=pl.BlockSpec((S, LANES), lambda b: (b, 0)),
        out_shape=jax.ShapeDtypeStruct((T, LANES), f32),
        compiler_params=_cparams("parallel"),
    )(P, bias, alog)


def gates_bwd(P, bias, alog, dgates, *, B):
    T = P.shape[0]
    S = T // B

    def body(z_ref, bias_ref, alog_ref, dg_ref, dz_ref, par_ref):
        z = z_ref[...]
        zb, ls, beta, g = _gate_values(z, bias_ref[...], alog_ref[...])
        d = dg_ref[...]
        dls = _cumsum_rows(d, S, reverse=True)
        dgr = _cumsum_rows(d, GDN_CHUNK, reverse=True)
        sig = jax.nn.sigmoid(zb)
        dz_f = dls * (1.0 - sig)
        dz_b = d * beta * (1.0 - beta)
        dz_a = dgr * (-jnp.exp(alog_ref[...])) * sig
        dz = jnp.where(_lane_mask(SM_F, SM_F + FOX_HEADS, z.shape), dz_f, 0.0)
        dz = jnp.where(_lane_mask(SM_B, SM_B + GDN_HEADS, z.shape), dz_b, dz)
        dz = jnp.where(_lane_mask(SM_A, SM_A + GDN_HEADS, z.shape), dz_a, dz)
        dz_ref[...] = dz.astype(dz_ref.dtype)

        @pl.when(pl.program_id(0) == 0)
        def _():
            par_ref[...] = jnp.zeros_like(par_ref)

        dalog = jnp.where(_lane_mask(SM_A, SM_A + GDN_HEADS, z.shape), dgr * g, 0.0)
        par_ref[0:1, :] += jnp.sum(dz, axis=0, keepdims=True)
        par_ref[1:2, :] += jnp.sum(dalog, axis=0, keepdims=True)

    vec = pl.BlockSpec((1, LANES), lambda b: (0, 0))
    return pl.pallas_call(
        body, name="gates_bwd", grid=(B,),
        in_specs=[pl.BlockSpec((S, LANES), lambda b: (b, COL_SMALL // LANES)), vec, vec,
                  pl.BlockSpec((S, LANES), lambda b: (b, 0))],
        out_specs=[pl.BlockSpec((S, LANES), lambda b: (b, 0)), pl.BlockSpec((8, LANES), lambda b: (0, 0))],
        out_shape=[jax.ShapeDtypeStruct((T, LANES), MXU_DTYPE), jax.ShapeDtypeStruct((8, LANES), f32)],
        compiler_params=_cparams("arbitrary"),
    )(P, bias, alog, dgates)


GDN_BLOCKS = 3 * GDN_HEADS


def _shift_rows(v, d, reverse=False):
    if d == 0:
        return v
    n = v.shape[0]
    row = lax.broadcasted_iota(jnp.int32, v.shape, 0)
    if reverse:
        return jnp.where(row + d < n, pltpu.roll(v, n - d, 0), 0.0)
    return jnp.where(row >= d, pltpu.roll(v, d, 0), 0.0)


def _conv_silu(x, w):
    pre = sum(w[j:j + 1, :] * _shift_rows(x, CONV_WIDTH - 1 - j) for j in range(CONV_WIDTH))
    return pre, pre * jax.nn.sigmoid(pre)


def gdn_prep_fwd(P, conv_w, *, B):
    T = P.shape[0]
    S = T // B

    def body(x_ref, w_ref, o_ref):
        _, y = _conv_silu(x_ref[...], w_ref[...])
        yn = y * lax.rsqrt(jnp.sum(y * y, axis=-1, keepdims=True) + EPS)
        o_ref[...] = jnp.where(pl.program_id(1) < 2 * GDN_HEADS, yn, y)

    return pl.pallas_call(
        body, name="gdn_prep_fwd", grid=(B, GDN_BLOCKS),
        in_specs=[pl.BlockSpec((S, LANES), lambda b, j: (b, COL_GDN // LANES + j)),
                  pl.BlockSpec((CONV_WIDTH, LANES), lambda b, j: (0, j))],
        out_specs=pl.BlockSpec((S, LANES), lambda b, j: (b, j)),
        out_shape=jax.ShapeDtypeStruct((T, 3 * GDN_WIDTH), f32),
        compiler_params=_cparams("parallel", "parallel"),
    )(P, conv_w)


def gdn_prep_bwd(P, conv_w, dGq, dGk, dGv, *, B):
    T = P.shape[0]
    S = T // B
    H = GDN_HEADS

    def body(x_ref, w_ref, dq_ref, dk_ref, dv_ref, dx_ref, dw_ref):
        x, w = x_ref[...], w_ref[...]
        pre, y = _conv_silu(x, w)
        jb = pl.program_id(0)
        dn = jnp.where(jb < H, dq_ref[...], jnp.where(jb < 2 * H, dk_ref[...], dv_ref[...]))
        r = lax.rsqrt(jnp.sum(y * y, axis=-1, keepdims=True) + EPS)
        n = y * r
        dy_norm = r * (dn - n * jnp.sum(dn * n, axis=-1, keepdims=True))
        dy = jnp.where(pl.program_id(0) < 2 * GDN_HEADS, dy_norm, dn)
        sg = jax.nn.sigmoid(pre)
        dpre = dy * (sg * (1.0 + pre * (1.0 - sg)))
        dx = sum(w[j:j + 1, :] * _shift_rows(dpre, CONV_WIDTH - 1 - j, reverse=True) for j in range(CONV_WIDTH))
        dx_ref[...] = dx.astype(dx_ref.dtype)

        @pl.when(pl.program_id(1) == 0)
        def _():
            dw_ref[...] = jnp.zeros_like(dw_ref)

        for j in range(CONV_WIDTH):
            dw_ref[j:j + 1, :] += jnp.sum(dpre * _shift_rows(x, CONV_WIDTH - 1 - j), axis=0, keepdims=True)

    return pl.pallas_call(
        body, name="gdn_prep_bwd", grid=(GDN_BLOCKS, B),
        in_specs=[pl.BlockSpec((S, LANES), lambda j, b: (b, COL_GDN // LANES + j)),
                  pl.BlockSpec((CONV_WIDTH, LANES), lambda j, b: (0, j))]
        + [pl.BlockSpec((S, LANES), lambda j, b, t=t: (jnp.where(j // H == t, b, 0), jnp.clip(j - t * H, 0, H - 1)))
           for t in range(3)],
        out_specs=[pl.BlockSpec((S, LANES), lambda j, b: (b, j)),
                   pl.BlockSpec((CONV_WIDTH, LANES), lambda j, b: (0, j))],
        out_shape=[jax.ShapeDtypeStruct((T, 3 * GDN_WIDTH), MXU_DTYPE),
                   jax.ShapeDtypeStruct((CONV_WIDTH, 3 * GDN_WIDTH), f32)],
        compiler_params=_cparams("arbitrary", "arbitrary"),
    )(P, conv_w, dGq, dGk, dGv)


GDN_GROUP = 16
GDN_GROUP_FWD = 16
B_NN = (((2,), (1,)), ((0,), (0,)))
B_NT = (((2,), (2,)), ((0,), (0,)))
B_TN = (((1,), (1,)), ((0,), (0,)))


def _bmm(a, b, dims, precision=None):
    if precision is None:
        a, b = _mx(a), _mx(b)
    return lax.dot_general(a, b, dims, preferred_element_type=f32, precision=precision)


def _tri_inverse(A):
    C = A.shape[-1]
    row = lax.broadcasted_iota(jnp.int32, A.shape, 1)
    col = lax.broadcasted_iota(jnp.int32, A.shape, 2)
    eye = (row == col).astype(f32)
    X = jnp.where((row // 4) == (col // 4), -A, 0.0)
    X2 = _bmm(X, X, B_NN, INV_PRECISION)
    Tm = eye + X + X2 + _bmm(X, X2, B_NN, INV_PRECISION)
    b = 4
    while b < C:
        off = ((row // (2 * b)) == (col // (2 * b))) & ((row // b) != (col // b))
        Tm = Tm - _bmm(_bmm(Tm, jnp.where(off, A, 0.0), B_NN, INV_PRECISION), Tm, B_NN, INV_PRECISION)
        b *= 2
    return Tm


def _pick_lane(block, lane_idx):
    lane = lax.broadcasted_iota(jnp.int32, block.shape, 1)
    return jnp.sum(jnp.where(lane == lane_idx, block, 0.0), axis=1, keepdims=True)


def _gdn_local(q, k, v, beta, gc, Tm=None, uwm=None):
    C = GDN_CHUNK
    n = q.shape[0] // C
    q = q.reshape(n, C, -1) * (GDN_HEAD_DIM ** -0.5)
    k = k.reshape(n, C, -1)
    v = v.reshape(n, C, -1)
    beta = beta.reshape(n, C, 1)
    gc = gc.reshape(n, C, 1)
    row = lax.broadcasted_iota(jnp.int32, (n, C, C), 1)
    col = lax.broadcasted_iota(jnp.int32, (n, C, C), 2)
    gcT = jnp.swapaxes(jnp.broadcast_to(gc, (n, C, C)), 1, 2)
    D = jnp.exp(jnp.where(row >= col, gc - gcT, NEG_INF))
    kb = k * beta
    vb = v * beta
    A = jnp.where(row > col, _bmm(kb, k, B_NT) * D, 0.0)
    Gam = jnp.exp(gc)
    kg = kb * Gam
    gl = gc[:, C - 1:C, :]
    kdec = jnp.exp(gl - gc)
    loc = dict(q=q, k=k, v=v, beta=beta, gc=gc, D=D, kb=kb, vb=vb, A=A, Gam=Gam, kg=kg,
               kdec=kdec, kd=k * kdec, qg=q * Gam, gam=jnp.exp(gl), row=row, col=col)
    uwm = Tm is None if uwm is None else uwm
    Tm = _tri_inverse(A) if Tm is None else Tm.reshape(n, C, C)
    if uwm:
        loc.update(u=_bmm(Tm, vb, B_NN), w=_bmm(Tm, kg, B_NN), M=_bmm(q, k, B_NT) * D)
    loc["Tm"] = Tm
    return loc


def _gdn_store_local(loc, r0, u_s, w_s, qg_s, kd_s, M_s, gam_s, c0):
    n = loc["u"].shape[0]
    R = n * GDN_CHUNK
    u_s[pl.ds(r0, R), :] = loc["u"].reshape(R, -1)
    w_s[pl.ds(r0, R), :] = loc["w"].reshape(R, -1)
    qg_s[pl.ds(r0, R), :] = loc["qg"].reshape(R, -1)
    kd_s[pl.ds(r0, R), :] = loc["kd"].reshape(R, -1)
    M_s[pl.ds(r0, R), :] = loc["M"].reshape(R, -1)
    gam_s[pl.ds(c0, n)] = jnp.broadcast_to(loc["gam"], (n, 1, LANES))


def _gdn_specs(S):
    blk = lambda off: pl.BlockSpec((S, LANES), lambda b, h: (b, off + h))
    return blk


def gdn_fwd(G, gates, P, g_on, *, B):
    T = G.shape[0]
    S = T // B
    C = GDN_CHUNK
    N = S // C
    grp = min(GDN_GROUP_FWD, N)
    R = grp * C
    hd = GDN_HEAD_DIM

    def body(q_ref, k_ref, v_ref, gt_ref, z_ref, gon_ref, o_ref, ob_ref, st_ref, tm_ref, A_s, B_s, Q_s, O_s, gam_s):
        h = pl.program_id(1)

        def local(gi, carry):
            r0 = pl.multiple_of(gi * R, R)
            gt = gt_ref[pl.ds(r0, R), :]
            loc = _gdn_local(q_ref[pl.ds(r0, R), :], k_ref[pl.ds(r0, R), :], v_ref[pl.ds(r0, R), :],
                             _pick_lane(gt, SM_B + h), _pick_lane(gt, SM_A + h))
            chunks = pl.ds(gi * grp, grp)
            tm_ref[0, 0, pl.ds(r0, R), :] = loc["Tm"].reshape(R, C)
            A_s[chunks] = -_bmm(loc["kd"], loc["w"], B_TN)
            B_s[chunks] = _bmm(loc["kd"], loc["u"], B_TN)
            Q_s[pl.ds(r0, R), :] = (loc["qg"] - _bmm(loc["M"], loc["w"], B_NN)).reshape(R, hd)
            O_s[pl.ds(r0, R), :] = _bmm(loc["M"], loc["u"], B_NN).reshape(R, hd)
            gam_s[chunks] = jnp.broadcast_to(loc["gam"], (grp, 1, LANES))
            return carry

        lax.fori_loop(0, N // grp, local, 0)

        def step(n, state):
            st_ref[0, 0, n] = state
            return state * gam_s[n] + _dotm(A_s[n], state, NN) + B_s[n]

        lax.fori_loop(0, N, step, jnp.zeros((hd, hd), f32))

        def outputs(gi, carry):
            r0 = pl.multiple_of(gi * R, R)
            Q = Q_s[pl.ds(r0, R), :].reshape(grp, C, hd)
            o = _bmm(Q, st_ref[0, 0, pl.ds(gi * grp, grp)], B_NN).reshape(R, hd) + O_s[pl.ds(r0, R), :]
            o_ref[pl.ds(r0, R), :] = o
            return carry

        lax.fori_loop(0, N // grp, outputs, 0)
        o = o_ref[...]
        z = z_ref[...]
        ob_ref[...] = (_head_rms(o, gon_ref[...])[0] * (z * jax.nn.sigmoid(z))).astype(ob_ref.dtype)

    blk = lambda off: pl.BlockSpec((S, LANES), lambda b, h: (b, off + h))
    rows = lambda: pltpu.VMEM((S, hd), f32)
    return pl.pallas_call(
        body, name="gdn_fwd", grid=(B, GDN_HEADS),
        in_specs=[blk(0), blk(GDN_HEADS), blk(2 * GDN_HEADS), pl.BlockSpec((S, LANES), lambda b, h: (b, 0)),
                  blk(COL_Z // LANES), pl.BlockSpec((1, hd), lambda b, h: (0, 0))],
        out_specs=[blk(0), blk(0), pl.BlockSpec((1, 1, N, hd, hd), lambda b, h: (b, h, 0, 0, 0)),
                   pl.BlockSpec((1, 1, S, C), lambda b, h: (b, h, 0, 0))],
        out_shape=[jax.ShapeDtypeStruct((T, GDN_WIDTH), f32), jax.ShapeDtypeStruct((T, GDN_WIDTH), MXU_DTYPE),
                   jax.ShapeDtypeStruct((B, GDN_HEADS, N, hd, hd), f32), jax.ShapeDtypeStruct((B, GDN_HEADS, S, C), f32)],
        scratch_shapes=[pltpu.VMEM((N, hd, hd), f32), pltpu.VMEM((N, hd, hd), f32), rows(), rows(),
                        pltpu.VMEM((N, 1, LANES), f32)],
        compiler_params=_cparams("parallel", "parallel"),
    )(G, G, G, gates, P, g_on)


def gdn_bwd(G, gates, P, g_on, o_raw, states, tm, d_oab, *, B):
    T = G.shape[0]
    S = T // B
    C = GDN_CHUNK
    N = S // C
    grp = min(GDN_GROUP, N)
    R = grp * C
    hd = GDN_HEAD_DIM

    def body(q_ref, k_ref, v_ref, gt_ref, z_ref, gon_ref, o_ref, st_ref, tm_ref, dob_ref,
             dq_ref, dk_ref, dv_ref, dgt_ref, dz_ref, dgon_ref,
             u_s, w_s, M_s, gam_s, do_s, A_s, C_s, dst_s):
        b, h = pl.program_id(0), pl.program_id(1)

        @pl.when((b == 0) & (h == 0))
        def _():
            dgon_ref[...] = jnp.zeros_like(dgon_ref)

        @pl.when(h == 0)
        def _():
            dgt_ref[...] = jnp.zeros_like(dgt_ref)

        def group_inputs(gi, uwm):
            r0 = pl.multiple_of(gi * R, R)
            gt = gt_ref[pl.ds(r0, R), :]
            return r0, _gdn_local(q_ref[pl.ds(r0, R), :], k_ref[pl.ds(r0, R), :], v_ref[pl.ds(r0, R), :],
                                  _pick_lane(gt, SM_B + h), _pick_lane(gt, SM_A + h), tm_ref[0, 0, pl.ds(r0, R), :], uwm)

        def local(gi, carry):
            r0, loc = group_inputs(gi, True)
            rows, chunks = pl.ds(r0, R), pl.ds(gi * grp, grp)
            u_s[rows, :] = loc["u"].reshape(R, hd)
            w_s[rows, :] = loc["w"].reshape(R, hd)
            M_s[rows, :] = loc["M"].reshape(R, C)
            gam_s[chunks] = jnp.broadcast_to(loc["gam"], (grp, 1, LANES))
            o, z, gon = o_ref[rows, :], z_ref[rows, :], gon_ref[...]
            dob = dob_ref[rows, :]
            on, ro = _head_rms(o, gon)
            sz = jax.nn.sigmoid(z)
            dz_ref[rows, :] = (dob * on * (sz * (1.0 + z * (1.0 - sz)))).astype(dz_ref.dtype)
            do, dgon = _head_rms_bwd(o, ro, gon, dob * (z * sz))
            do_s[rows, :] = do
            dgon_ref[...] += dgon
            A_s[chunks] = -_bmm(loc["kd"], loc["w"], B_TN)
            C_s[chunks] = _bmm(loc["qg"] - _bmm(loc["M"], loc["w"], B_NN), do.reshape(grp, C, hd), B_TN)
            return carry

        lax.fori_loop(0, N // grp, local, 0)

        def step(t, dS):
            n = N - 1 - t
            dst_s[n] = dS
            return dS * gam_s[n] + _dotm(A_s[n], dS, TN) + C_s[n]

        lax.fori_loop(0, N, step, jnp.zeros((hd, hd), f32))

        def finish(gi, carry):
            r0, L = group_inputs(gi, False)
            n = grp
            rows, chunks = pl.ds(r0, R), pl.ds(gi * grp, grp)
            g3 = lambda ref: ref[rows, :].reshape(n, C, -1)
            u, w, do = g3(u_s), g3(w_s), g3(do_s)
            L["M"] = g3(M_s)
            state, dS = st_ref[0, 0, chunks], dst_s[chunks]
            v_new = u - _bmm(w, state, B_NN)
            du = _bmm(L["M"], do, B_TN) + _bmm(L["kd"], dS, B_NN)
            dw = -_bmm(du, state, B_NT)
            dqg = _bmm(do, state, B_NT)
            dM = _bmm(do, v_new, B_NT)
            dkd = _bmm(v_new, dS, B_NT)
            dgl_state = jnp.sum(jnp.sum(dS * state, axis=2, keepdims=True), axis=1, keepdims=True) * L["gam"]
            TmT = jnp.swapaxes(L["Tm"], 1, 2)
            dTm = _bmm(du, L["vb"], B_NT) + _bmm(dw, L["kg"], B_NT)
            dvb = _bmm(TmT, du, B_NN)
            dkg = _bmm(TmT, dw, B_NN)
            dA = jnp.where(L["row"] > L["col"], -_bmm(_bmm(TmT, dTm, B_NN), TmT, B_NN), 0.0)
            dKK = dA * L["D"]
            dQK = dM * L["D"]
            dkb = _bmm(dKK, L["k"], B_NN) + dkg * L["Gam"]
            dk = (_bmm(dKK, L["kb"], B_TN) + _bmm(dQK, L["q"], B_TN) + dkd * L["kdec"] + L["beta"] * dkb)
            dq = (_bmm(dQK, L["k"], B_NN) + dqg * L["Gam"]) * (GDN_HEAD_DIM ** -0.5)
            E = dA * L["A"] + dM * L["M"]
            r = jnp.sum(dkd * L["kd"], axis=-1, keepdims=True)
            dgc = (jnp.sum(E, axis=2, keepdims=True) - jnp.sum(jnp.swapaxes(E, 1, 2), axis=2, keepdims=True)
                   + jnp.sum(dkg * L["kg"], axis=-1, keepdims=True) + jnp.sum(dqg * L["qg"], axis=-1, keepdims=True) - r)
            dgl = jnp.sum(r, axis=1, keepdims=True) + dgl_state
            rowc = lax.broadcasted_iota(jnp.int32, (n, C, 1), 1)
            dgc = dgc + jnp.where(rowc == C - 1, dgl, 0.0)
            dbeta = jnp.sum(dkb * L["k"], axis=-1, keepdims=True) + jnp.sum(dvb * L["v"], axis=-1, keepdims=True)
            dq_ref[rows, :] = dq.reshape(R, hd)
            dk_ref[rows, :] = dk.reshape(R, hd)
            dv_ref[rows, :] = (L["beta"] * dvb).reshape(R, hd)
            lane = lax.broadcasted_iota(jnp.int32, (R, LANES), 1)
            dgt_ref[rows, :] += (jnp.where(lane == SM_B + h, dbeta.reshape(R, 1), 0.0)
                                 + jnp.where(lane == SM_A + h, dgc.reshape(R, 1), 0.0))
            return carry

        lax.fori_loop(0, N // grp, finish, 0)

    blk = lambda off: pl.BlockSpec((S, LANES), lambda b, h: (b, off + h))
    rows = lambda: pltpu.VMEM((S, hd), f32)
    return pl.pallas_call(
        body, name="gdn_bwd", grid=(B, GDN_HEADS),
        in_specs=[blk(0), blk(GDN_HEADS), blk(2 * GDN_HEADS), pl.BlockSpec((S, LANES), lambda b, h: (b, 0)),
                  blk(COL_Z // LANES), pl.BlockSpec((1, hd), lambda b, h: (0, 0)), blk(0),
                  pl.BlockSpec((1, 1, N, hd, hd), lambda b, h: (b, h, 0, 0, 0)),
                  pl.BlockSpec((1, 1, S, C), lambda b, h: (b, h, 0, 0)), blk(GDN_HEADS)],
        out_specs=[blk(0), blk(0), blk(0), pl.BlockSpec((S, LANES), lambda b, h: (b, 0)), blk(0),
                   pl.BlockSpec((1, hd), lambda b, h: (0, 0))],
        out_shape=[jax.ShapeDtypeStruct((T, GDN_WIDTH), f32), jax.ShapeDtypeStruct((T, GDN_WIDTH), f32),
                   jax.ShapeDtypeStruct((T, GDN_WIDTH), f32), jax.ShapeDtypeStruct((T, LANES), f32),
                   jax.ShapeDtypeStruct((T, GDN_WIDTH), MXU_DTYPE), jax.ShapeDtypeStruct((1, hd), f32)],
        scratch_shapes=[rows(), rows(), pltpu.VMEM((S, C), f32), pltpu.VMEM((N, 1, LANES), f32), rows(),
                        pltpu.VMEM((N, hd, hd), f32), pltpu.VMEM((N, hd, hd), f32), pltpu.VMEM((N, hd, hd), f32)],
        compiler_params=_cparams("arbitrary", "arbitrary"),
    )(G, G, G, gates, P, g_on, o_raw, states, tm, d_oab)


IN_SPLIT = (0, 1536, 1544, 3080, 3088, 3600)


IN_SHARD = IN_DIM // 4
IN_SHARD_PAD = 928


def align_w_in_t(wt):
    s = IN_SPLIT
    pad = jnp.zeros((IN_ALIGNED - IN_DIM, wt.shape[1]), wt.dtype)
    return jnp.concatenate([wt[s[0]:s[1]], wt[s[2]:s[3]], wt[s[4]:s[5]], wt[s[1]:s[2]], wt[s[3]:s[4]], pad], axis=0)


def unalign_w_in_t(wa):
    return jnp.concatenate([wa[0:1536], wa[COL_SMALL:COL_SMALL + 8], wa[1536:3072],
                            wa[COL_SMALL + 8:COL_SMALL + 16], wa[3072:3584]], axis=0)


IN_SEGMENTS = (((0, 1536), 0), ((1536, 1544), COL_SMALL), ((1544, 3080), 1536), ((3080, 3088), COL_SMALL + 8),
               ((3088, 3600), 3072))


def align_w_in_slots(slots):
    pieces = []
    for (lo, hi), _ in sorted(IN_SEGMENTS, key=lambda seg: seg[1]):
        for k in range(N_CHIPS):
            a, b = max(lo, k * IN_SHARD), min(hi, (k + 1) * IN_SHARD)
            if a < b:
                pieces.append(slots[k, a - k * IN_SHARD:b - k * IN_SHARD])
    pieces.append(jnp.zeros((IN_ALIGNED - IN_DIM, slots.shape[2]), slots.dtype))
    return jnp.concatenate(pieces, axis=0)


def unalign_to_slots(wa):
    slots = []
    for k in range(N_CHIPS):
        lo, hi = k * IN_SHARD, (k + 1) * IN_SHARD
        pieces = []
        for (a, b), first in IN_SEGMENTS:
            x, y = max(a, lo), min(b, hi)
            if x < y:
                pieces.append(wa[first + x - a:first + y - a])
        pieces.append(jnp.zeros((IN_SHARD_PAD - IN_SHARD, wa.shape[1]), wa.dtype))
        slots.append(jnp.concatenate(pieces, axis=0))
    return jnp.stack(slots)


def _lanes_vec(pieces):
    v = jnp.zeros((1, LANES), f32)
    for off, a in pieces:
        v = lax.dynamic_update_slice(v, a.astype(f32), (0, off))
    return v


def local_step(x, mem, target, w, sp, *, B):
    T = x.shape[0]
    S = T // B
    gq8, gk8 = jnp.tile(sp["fox_qnorm_g"], (1, FOX_HEADS)), jnp.tile(sp["fox_knorm_g"], (1, FOX_HEADS))
    go2 = jnp.tile(sp["fox_onorm_g"], (1, 2))
    bias = _lanes_vec([(SM_F, sp["fox_f_bias"]), (SM_A, sp["gdn_dt_bias"])])
    alog = _lanes_vec([(SM_A, sp["gdn_A_log"])])

    h1 = rms_fwd(x, sp["norm_mix_g"], name="rms_mix")
    P = matmul(h1, w["wa_t"], tb=True, name="mm_in", tn=IN_TILE)
    gates = gates_fwd(P, bias, alog, B=B)
    c = gates[:, SM_F:SM_F + FOX_HEADS].reshape(B, S, FOX_HEADS).transpose(0, 2, 1)
    ccol, crow = c[..., None], c.reshape(B, FOX_HEADS, S // FOX_TQ, 1, FOX_TQ)
    qn, kn, vb = fox_prep_fwd(P, gq8, gk8)
    o_raw, o_a, lse = fox_core_fwd(qn, kn, vb, ccol, crow, go2, B=B)
    G = gdn_prep_fwd(P, w["conv_w"], B=B)
    ob_raw, o_b, states, gdn_tm = gdn_fwd(G, gates, P, sp["gdn_onorm_g"], B=B)
    oab = jnp.concatenate([o_a, o_b], axis=1)
    if "late" in w:
        w = {**w, **w["late"](oab)}
    x2, hq = matmul_rows(oab, w["w_out"], (x, sp["norm_xattn_g"]), mode="rms_fwd", name="mm_out_rms")
    hm = rms_fwd(mem, sp["mem_norm_g"], name="rms_mem")
    cq = matmul(hq, w["w_cq"], name="mm_cq")
    ckv = matmul(hm, w["w_ckv"], name="mm_ckv")
    co = xattn_fwd(cq, ckv, sp["xattn_qnorm_g"], sp["xattn_knorm_g"], B=B)
    x3, hf = matmul_rows(co, w["w_co"], (x2, sp["norm_mlp_g"]), mode="rms_fwd", name="mm_co_rms")
    act = matmul(hf, w["w_mlp1"], b_stacked=True, relu2_out=True, out_dtype=MXU_DTYPE, name="mm_mlp1")
    dy, dy_op, loss = matmul_rows(act, w["w_mlp2"], (x3, target), mode="loss", name="mm_mlp2_loss")

    da = matmul(dy_op, w["w_mlp2"], tb=True, relu2_bwd_aux=act, out_dtype=MXU_DTYPE, name="mm_d_act")
    g_mlp2 = matmul(act, dy_op, ta=True, out_dtype=WIRE_DTYPE, name="mm_g_mlp2")
    g_mlp1 = matmul(hf, da, ta=True, out_stacked=True, out_dtype=WIRE_DTYPE, name="mm_g_mlp1")
    by_rows = lambda g: g.reshape(N_CHIPS, g.shape[0] // N_CHIPS, g.shape[1])
    early = w.get("grads_ready", lambda grads: jnp.zeros((1, 1), f32))
    tok = early(dict(w_mlp1=g_mlp1, w_mlp2=by_rows(g_mlp2)))[0, 0]
    dx3, g_norm_mlp = matmul_rows(da, w["w_mlp1"], (x3, sp["norm_mlp_g"] + tok, dy), mode="rms_bwd", tb=True,
                                  b_stacked=True, name="mm_d_hf_rms")
    dco = matmul(dx3, w["w_co"], tb=True, name="mm_d_co")
    g_co = matmul(co, dx3, ta=True, out_dtype=WIRE_DTYPE, name="mm_g_co")
    g_co = g_co.reshape(XATTN_WIDTH, N_CHIPS, D_MODEL // N_CHIPS).transpose(1, 0, 2)
    dcq, dckv, g_xq, g_xk = xattn_bwd(cq, ckv, sp["xattn_qnorm_g"], sp["xattn_knorm_g"], dco, B=B)
    g_cq = matmul(hq, dcq, ta=True, out_dtype=WIRE_DTYPE, name="mm_g_cq")
    g_ckv = matmul(hm, dckv, ta=True, out_dtype=WIRE_DTYPE, name="mm_g_ckv")
    _, g_mem_norm = matmul_rows(dckv, w["w_ckv"], (mem, sp["mem_norm_g"], None), mode="rms_bwd", tb=True, name="mm_d_hm_rms")
    dx2, g_norm_xattn = matmul_rows(dcq, w["w_cq"], (x2, sp["norm_xattn_g"], dx3), mode="rms_bwd", tb=True, name="mm_d_hq_rms")
    doab = matmul(dx2, w["w_out"], tb=True, name="mm_d_oab")
    g_out = matmul(oab, dx2, ta=True, out_dtype=WIRE_DTYPE, name="mm_g_out")
    tok = early(dict(w_co=g_co, w_cq=by_rows(g_cq), w_ckv=by_rows(g_ckv), w_out=by_rows(g_out)))[0, 0]
    dqn, dkn, dv_f, dckey, dcrow, dgo2 = fox_core_bwd(qn, kn, vb, ccol, crow, go2 + tok, o_raw, lse, doab, B=B)
    dq_f, dk_f, dgq8, dgk8 = fox_prep_bwd(P, gq8, gk8, dqn, dkn)
    dGq, dGk, dGv, dgt, dz, g_gdn_on = gdn_bwd(G, gates, P, sp["gdn_onorm_g"], ob_raw, states, gdn_tm, doab, B=B)
    dPg, g_conv = gdn_prep_bwd(P, w["conv_w"], dGq, dGk, dGv, B=B)
    dc = (dckey[:, :, 0, :] + dcrow.reshape(B, FOX_HEADS, S)).transpose(0, 2, 1).reshape(T, FOX_HEADS)
    dgates = dgt + jnp.pad(dc, ((0, 0), (SM_F, LANES - SM_F - FOX_HEADS)))
    dsmall, par = gates_bwd(P, bias, alog, dgates, B=B)
    dP = jnp.concatenate([dq_f, dk_f, dv_f, dPg, dz, dsmall, jnp.zeros((T, IN_ALIGNED - COL_SMALL - LANES), MXU_DTYPE)], axis=1)
    g_wa = matmul(dP, h1, ta=True, out_dtype=WIRE_DTYPE, name="mm_g_in", tm=IN_TILE)
    g_in = unalign_to_slots(g_wa)
    tok = early(dict(w_in=g_in))[0, 0]
    dx, g_norm_mix = matmul_rows(dP, w["wa_t"], (x, sp["norm_mix_g"] + tok, dx2), mode="rms_bwd", tk=IN_TILE,
                                 name="mm_d_h1_rms")

    fold = lambda g: jnp.sum(g.reshape(-1, FOX_HEAD_DIM), axis=0, keepdims=True)
    big = dict(w_in=g_in, w_out=by_rows(g_out), w_cq=by_rows(g_cq), w_ckv=by_rows(g_ckv), w_co=g_co, w_mlp1=g_mlp1,
               w_mlp2=by_rows(g_mlp2))
    small = dict(norm_mix_g=g_norm_mix, fox_qnorm_g=fold(dgq8), fox_knorm_g=fold(dgk8),
                 fox_f_bias=par[0:1, SM_F:SM_F + FOX_HEADS], fox_onorm_g=fold(dgo2), gdn_conv_w=g_conv,
                 gdn_A_log=par[1:2, SM_A:SM_A + GDN_HEADS], gdn_dt_bias=par[0:1, SM_A:SM_A + GDN_HEADS],
                 gdn_onorm_g=g_gdn_on, norm_xattn_g=g_norm_xattn, mem_norm_g=g_mem_norm,
                 xattn_qnorm_g=g_xq, xattn_knorm_g=g_xk, norm_mlp_g=g_norm_mlp)
    return loss, dx, big, small


MESH_IDS = pl.DeviceIdType.MESH
N_CHIPS = 4
HBM_SPEC = pl.BlockSpec(memory_space=pltpu.HBM)
PACK_ROWS = 30720
PACK_HALF = PACK_ROWS // 2
PACK_BLOCK = 3072


def _place():
    return lax.axis_index("x"), lax.axis_index("y"), lax.axis_index("c")


def _other_chips(x, y):
    return [(1 - x, y), (x, 1 - y), (1 - x, 1 - y)]


def _remote(src, dst, send_sem, recv_sem, to):
    return pltpu.make_async_remote_copy(src_ref=src, dst_ref=dst, send_sem=send_sem, recv_sem=recv_sem,
                                        device_id=to, device_id_type=MESH_IDS)


def all_gather_shards(packed):
    half = PACK_HALF

    def body(src_ref, out_ref, send_sems, recv_sems):
        x, y, c = _place()
        me_chip = 2 * x + y
        sibling = (x, y, 1 - c)
        chips = _other_chips(x, y)

        def rows(chip, core):
            return out_ref.at[chip, pl.ds(core * half, half), :]

        sends = [_remote(src_ref.at[pl.ds(c * half, half), :], rows(me_chip, c), send_sems.at[j], recv_sems.at[j], (px, py, c))
                 for j, (px, py) in enumerate(chips)]
        for cp in sends:
            cp.start()
        passed = []
        for j, (px, py) in enumerate(chips):
            theirs = rows(2 * px + py, c)
            _remote(theirs, theirs, send_sems.at[j], recv_sems.at[j], (px, py, c)).wait_recv()
            cp = _remote(theirs, theirs, send_sems.at[3 + j], recv_sems.at[3 + j], sibling)
            cp.start()
            passed.append(cp)
        for j, (px, py) in enumerate(chips):
            theirs = rows(2 * px + py, 1 - c)
            _remote(theirs, theirs, send_sems.at[3 + j], recv_sems.at[3 + j], sibling).wait_recv()
        for cp in sends + passed:
            cp.wait_send()

    return pl.pallas_call(
        body, name="all_gather_shards", in_specs=[HBM_SPEC], out_specs=HBM_SPEC,
        out_shape=jax.ShapeDtypeStruct((N_CHIPS,) + packed.shape, packed.dtype),
        scratch_shapes=[pltpu.SemaphoreType.DMA((6,)), pltpu.SemaphoreType.DMA((6,))],
    )(packed)


def exchange_core_halves(G):
    half = PACK_HALF

    def body(g_ref, land_ref, send_sem, recv_sem):
        x, y, c = _place()
        cp = _remote(g_ref.at[:, pl.ds((1 - c) * half, half), :], land_ref, send_sem, recv_sem, (x, y, 1 - c))
        cp.start()
        cp.wait()

    return pl.pallas_call(
        body, name="exchange_core_halves", in_specs=[HBM_SPEC], out_specs=HBM_SPEC,
        out_shape=jax.ShapeDtypeStruct((N_CHIPS, half, LANES), G.dtype),
        scratch_shapes=[pltpu.SemaphoreType.DMA(()), pltpu.SemaphoreType.DMA(())],
    )(G)


def add_core_halves(G, land, core):
    nb = PACK_HALF // PACK_BLOCK

    def body(c_ref, g_ref, l_ref, o_ref):
        o_ref[...] = (g_ref[...].astype(f32) + l_ref[...].astype(f32)).astype(o_ref.dtype)

    blk = (1, PACK_BLOCK, LANES)
    return pl.pallas_call(
        body, name="add_core_halves",
        grid_spec=pltpu.PrefetchScalarGridSpec(
            num_scalar_prefetch=1, grid=(N_CHIPS, nb),
            in_specs=[pl.BlockSpec(blk, lambda k, i, c_ref: (k, c_ref[0] * nb + i, 0)),
                      pl.BlockSpec(blk, lambda k, i, c_ref: (k, i, 0))],
            out_specs=pl.BlockSpec(blk, lambda k, i, c_ref: (k, i, 0))),
        out_shape=jax.ShapeDtypeStruct(land.shape, land.dtype),
        compiler_params=_cparams("parallel", "parallel"),
    )(core, G, land)


def scatter_to_chips(part):
    def body(p_ref, land_ref, send_sems, recv_sems):
        x, y, c = _place()
        me_chip = 2 * x + y
        chips = _other_chips(x, y)
        sends = [_remote(p_ref.at[2 * px + py], land_ref.at[me_chip], send_sems.at[j], recv_sems.at[j], (px, py, c))
                 for j, (px, py) in enumerate(chips)]
        for cp in sends:
            cp.start()
        for j, (px, py) in enumerate(chips):
            slot = land_ref.at[2 * px + py]
            _remote(slot, slot, send_sems.at[j], recv_sems.at[j], (px, py, c)).wait_recv()
        for cp in sends:
            cp.wait_send()

    return pl.pallas_call(
        body, name="scatter_to_chips", in_specs=[HBM_SPEC], out_specs=HBM_SPEC,
        out_shape=jax.ShapeDtypeStruct(part.shape, part.dtype),
        scratch_shapes=[pltpu.SemaphoreType.DMA((3,)), pltpu.SemaphoreType.DMA((3,))],
    )(part)


def sum_chips(part, land, order):
    nb = PACK_HALF // PACK_BLOCK

    def body(order_ref, p_ref, l1_ref, l2_ref, l3_ref, o_ref):
        o_ref[...] = ((p_ref[0].astype(f32) + l1_ref[0].astype(f32)) + l2_ref[0].astype(f32)) + l3_ref[0].astype(f32)

    slot = lambda j: pl.BlockSpec((1, PACK_BLOCK, LANES), lambda i, order_ref: (order_ref[j], i, 0))
    return pl.pallas_call(
        body, name="sum_chips",
        grid_spec=pltpu.PrefetchScalarGridSpec(
            num_scalar_prefetch=1, grid=(nb,), in_specs=[slot(0), slot(1), slot(2), slot(3)],
            out_specs=pl.BlockSpec((PACK_BLOCK, LANES), lambda i, order_ref: (i, 0))),
        out_shape=jax.ShapeDtypeStruct((PACK_HALF, LANES), f32),
        compiler_params=_cparams("parallel"),
    )(order, part, land, land, land)


def swap_core_halves(red):
    def body(r_ref, out_ref, send_sem, recv_sem):
        x, y, c = _place()
        cp = _remote(r_ref, out_ref, send_sem, recv_sem, (x, y, 1 - c))
        cp.start()
        cp.wait()

    return pl.pallas_call(
        body, name="swap_core_halves", in_specs=[HBM_SPEC], out_specs=HBM_SPEC,
        out_shape=jax.ShapeDtypeStruct(red.shape, red.dtype),
        scratch_shapes=[pltpu.SemaphoreType.DMA(()), pltpu.SemaphoreType.DMA(())],
    )(red)


def _half(ref, core):
    rows = ref.shape[-2] // 2
    return ref.at[(slice(None),) * (len(ref.shape) - 2) + (pl.ds(core * rows, rows), slice(None))]


def gather_weights(shards, conv):
    n = len(shards)

    def body(*refs):
        src, conv_src = refs[:n], refs[n]
        out, conv_out = refs[n + 1:2 * n + 1], refs[2 * n + 1]
        send_sems, recv_sems = refs[2 * n + 2], refs[2 * n + 3]
        x, y, c = _place()
        me_chip = 2 * x + y
        sibling = (x, y, 1 - c)
        chips = _other_chips(x, y)
        sends = []
        for a in range(n):
            for j, (px, py) in enumerate(chips):
                sends.append(_remote(_half(src[a], c), _half(out[a].at[me_chip], c),
                                     send_sems.at[6 * a + j], recv_sems.at[6 * a + j], (px, py, c)))
        for j, (px, py) in enumerate(chips):
            sends.append(_remote(conv_src, conv_out.at[me_chip], send_sems.at[6 * n + j], recv_sems.at[6 * n + j], (px, py, c)))
        for cp in sends:
            cp.start()
        passed = []
        for a in range(n):
            for j, (px, py) in enumerate(chips):
                theirs = _half(out[a].at[2 * px + py], c)
                _remote(theirs, theirs, send_sems.at[6 * a + j], recv_sems.at[6 * a + j], (px, py, c)).wait_recv()
                cp = _remote(theirs, theirs, send_sems.at[6 * a + 3 + j], recv_sems.at[6 * a + 3 + j], sibling)
                cp.start()
                passed.append(cp)
        for j, (px, py) in enumerate(chips):
            theirs = conv_out.at[2 * px + py]
            _remote(theirs, theirs, send_sems.at[6 * n + j], recv_sems.at[6 * n + j], (px, py, c)).wait_recv()
        for a in range(n):
            for j, (px, py) in enumerate(chips):
                theirs = _half(out[a].at[2 * px + py], 1 - c)
                _remote(theirs, theirs, send_sems.at[6 * a + 3 + j], recv_sems.at[6 * a + 3 + j], sibling).wait_recv()
        for cp in sends + passed:
            cp.wait_send()

    return pl.pallas_call(
        body, name="gather_weights", in_specs=[HBM_SPEC] * (n + 1), out_specs=[HBM_SPEC] * (n + 1),
        out_shape=[jax.ShapeDtypeStruct((N_CHIPS,) + s.shape, s.dtype) for s in list(shards) + [conv]],
        scratch_shapes=[pltpu.SemaphoreType.DMA((6 * n + 3,)), pltpu.SemaphoreType.DMA((6 * n + 3,))],
    )(*shards, conv)


SEM_SPEC = pl.BlockSpec(memory_space=pltpu.SEMAPHORE)
SPLIT_EFFECT = pltpu.SideEffectType.DATAFLOW_SIDE_EFFECTING


def _gather_async_copies(src, land, send_sems, recv_sems, x, y, c):
    me_chip = 2 * x + y
    sends, arrivals = [], []
    for a in range(len(src)):
        for j, (px, py) in enumerate(_other_chips(x, y)):
            for core in range(2):
                sends.append(_remote(_half(src[a], c), _half(land[a].at[me_chip], c), send_sems.at[6 * a + 2 * j + core],
                                     recv_sems.at[6 * a + 2 * j + c], (px, py, core)))
                theirs = _half(land[a].at[2 * px + py], core)
                arrivals.append(_remote(theirs, theirs, send_sems.at[6 * a + 2 * j + core],
                                        recv_sems.at[6 * a + 2 * j + core], (px, py, core)))
    return sends, arrivals


def gather_weights_start(shards, after):
    n = len(shards)

    def body(*refs):
        src, land = refs[:n], refs[n:2 * n]
        send_sems, recv_sems, token = refs[2 * n + 1], refs[2 * n + 2], refs[4 * n + 3]
        x, y, c = _place()
        for cp in _gather_async_copies(src, land, send_sems, recv_sems, x, y, c)[0]:
            cp.start()
        token[...] = jnp.zeros_like(token)

    zones = [pltpu.with_memory_space_constraint(lax.empty((N_CHIPS,) + s.shape, s.dtype), pltpu.HBM) for s in shards]
    srcs = [pltpu.with_memory_space_constraint(s, pltpu.HBM) for s in shards]
    out = pl.pallas_call(
        body, name="gather_weights_start",
        out_shape=[pltpu.SemaphoreType.DMA((6 * n,)), pltpu.SemaphoreType.DMA((6 * n,))]
        + [pltpu.HBM(s.shape, s.dtype) for s in shards] + [pltpu.HBM(z.shape, z.dtype) for z in zones]
        + [jax.ShapeDtypeStruct((8, LANES), f32)],
        in_specs=[HBM_SPEC] * (2 * n) + [pl.BlockSpec(memory_space=pl.ANY)],
        out_specs=[SEM_SPEC, SEM_SPEC] + [HBM_SPEC] * (2 * n) + [pl.BlockSpec(memory_space=pltpu.VMEM)],
        input_output_aliases={i: 2 + i for i in range(2 * n)},
        compiler_params=pltpu.CompilerParams(has_side_effects=SPLIT_EFFECT),
    )(*srcs, *zones, after)
    return out[0], out[1], out[2:2 + n], out[2 + n:2 + 2 * n], out[-1]


def gather_weights_wait(send_sems, recv_sems, shards, zones, after):
    n = len(shards)

    def body(*refs):
        src, land = refs[:n], refs[n:2 * n]
        send_sems, recv_sems = refs[2 * n], refs[2 * n + 1]
        x, y, c = _place()
        sends, arrivals = _gather_async_copies(src, land, send_sems, recv_sems, x, y, c)
        for cp in sends:
            cp.wait_send()
        for cp in arrivals:
            cp.wait_recv()

    out = pl.pallas_call(
        body, name="gather_weights_wait",
        out_shape=[pltpu.HBM(s.shape, s.dtype) for s in shards] + [pltpu.HBM(z.shape, z.dtype) for z in zones],
        in_specs=[HBM_SPEC] * (2 * n) + [SEM_SPEC, SEM_SPEC, pl.BlockSpec(memory_space=pl.ANY)],
        out_specs=[HBM_SPEC] * (2 * n),
        input_output_aliases={i: i for i in range(2 * n)},
        compiler_params=pltpu.CompilerParams(has_side_effects=SPLIT_EFFECT),
    )(*shards, *zones, send_sems, recv_sems, after)
    return out[n:]


def swap_grad_halves(grads, *, name):
    n = len(grads)

    def body(*refs):
        g, land, send_sems, recv_sems = refs[:n], refs[n:2 * n], refs[2 * n], refs[2 * n + 1]
        x, y, c = _place()
        copies = [_remote(_half(g[a], 1 - c), land[a], send_sems.at[a], recv_sems.at[a], (x, y, 1 - c)) for a in range(n)]
        for cp in copies:
            cp.start()
        for cp in copies:
            cp.wait()

    return pl.pallas_call(
        body, name=name, in_specs=[HBM_SPEC] * n, out_specs=[HBM_SPEC] * n,
        out_shape=[jax.ShapeDtypeStruct((N_CHIPS, g.shape[1] // 2, g.shape[2]), g.dtype) for g in grads],
        scratch_shapes=[pltpu.SemaphoreType.DMA((n,)), pltpu.SemaphoreType.DMA((n,))],
    )(*grads)


GRAD_ROWS = 512


def add_grad_halves(g, land, core, *, name):
    _, half, cols = land.shape
    tr = GRAD_ROWS if half % GRAD_ROWS == 0 else half
    nb = half // tr

    def body(c_ref, g_ref, l_ref, o_ref):
        o_ref[...] = (g_ref[...].astype(f32) + l_ref[...].astype(f32)).astype(o_ref.dtype)

    blk = (1, tr, cols)
    return pl.pallas_call(
        body, name=name,
        grid_spec=pltpu.PrefetchScalarGridSpec(
            num_scalar_prefetch=1, grid=(N_CHIPS, nb),
            in_specs=[pl.BlockSpec(blk, lambda k, i, c_ref: (k, c_ref[0] * nb + i, 0)),
                      pl.BlockSpec(blk, lambda k, i, c_ref: (k, i, 0))],
            out_specs=pl.BlockSpec(blk, lambda k, i, c_ref: (k, i, 0))),
        out_shape=jax.ShapeDtypeStruct(land.shape, land.dtype),
        compiler_params=_cparams("parallel", "parallel"),
    )(core, g, land)


def scatter_grads(parts):
    n = len(parts)

    def body(*refs):
        p, land, send_sems, recv_sems = refs[:n], refs[n:2 * n], refs[2 * n], refs[2 * n + 1]
        x, y, c = _place()
        me_chip = 2 * x + y
        chips = _other_chips(x, y)
        sends = [_remote(p[a].at[2 * px + py], land[a].at[me_chip], send_sems.at[3 * a + j], recv_sems.at[3 * a + j], (px, py, c))
                 for a in range(n) for j, (px, py) in enumerate(chips)]
        for cp in sends:
            cp.start()
        for a in range(n):
            for j, (px, py) in enumerate(chips):
                slot = land[a].at[2 * px + py]
                _remote(slot, slot, send_sems.at[3 * a + j], recv_sems.at[3 * a + j], (px, py, c)).wait_recv()
        for cp in sends:
            cp.wait_send()

    return pl.pallas_call(
        body, name="scatter_grads", in_specs=[HBM_SPEC] * n, out_specs=[HBM_SPEC] * n,
        out_shape=[jax.ShapeDtypeStruct(p.shape, p.dtype) for p in parts],
        scratch_shapes=[pltpu.SemaphoreType.DMA((3 * n,)), pltpu.SemaphoreType.DMA((3 * n,))],
    )(*parts)


def _scatter_async_copies(parts, land, send_sems, recv_sems, x, y, c):
    me_chip = 2 * x + y
    sends, arrivals = [], []
    for a in range(len(parts)):
        for j, (px, py) in enumerate(_other_chips(x, y)):
            sems = (send_sems.at[3 * a + j], recv_sems.at[3 * a + j], (px, py, c))
            sends.append(_remote(parts[a].at[2 * px + py], land[a].at[me_chip], *sems))
            slot = land[a].at[2 * px + py]
            arrivals.append(_remote(slot, slot, *sems))
    return sends, arrivals


def scatter_grads_start(parts, *, name):
    n = len(parts)

    def body(*refs):
        p, land = refs[:n], refs[n:2 * n]
        send_sems, recv_sems, token = refs[2 * n], refs[2 * n + 1], refs[4 * n + 2]
        x, y, c = _place()
        for cp in _scatter_async_copies(p, land, send_sems, recv_sems, x, y, c)[0]:
            cp.start()
        token[...] = jnp.zeros_like(token)

    zones = [pltpu.with_memory_space_constraint(lax.empty(p.shape, p.dtype), pltpu.HBM) for p in parts]
    srcs = [pltpu.with_memory_space_constraint(p, pltpu.HBM) for p in parts]
    hbm = [pltpu.HBM(p.shape, p.dtype) for p in parts]
    out = pl.pallas_call(
        body, name=name,
        out_shape=[pltpu.SemaphoreType.DMA((3 * n,)), pltpu.SemaphoreType.DMA((3 * n,))] + hbm + hbm
        + [jax.ShapeDtypeStruct((8, LANES), f32)],
        in_specs=[HBM_SPEC] * (2 * n),
        out_specs=[SEM_SPEC, SEM_SPEC] + [HBM_SPEC] * (2 * n) + [pl.BlockSpec(memory_space=pltpu.VMEM)],
        input_output_aliases={i: 2 + i for i in range(2 * n)},
        compiler_params=pltpu.CompilerParams(has_side_effects=SPLIT_EFFECT),
    )(*srcs, *zones)
    return out[0], out[1], out[2:2 + n], out[2 + n:2 + 2 * n], out[-1]


def scatter_grads_wait(send_sems, recv_sems, parts, zones, after, *, name):
    n = len(parts)

    def body(*refs):
        p, land = refs[:n], refs[n:2 * n]
        x, y, c = _place()
        sends, arrivals = _scatter_async_copies(p, land, refs[2 * n], refs[2 * n + 1], x, y, c)
        for cp in sends:
            cp.wait_send()
        for cp in arrivals:
            cp.wait_recv()

    hbm = [pltpu.HBM(p.shape, p.dtype) for p in parts]
    out = pl.pallas_call(
        body, name=name, out_shape=hbm + hbm,
        in_specs=[HBM_SPEC] * (2 * n) + [SEM_SPEC, SEM_SPEC, pl.BlockSpec(memory_space=pl.ANY)],
        out_specs=[HBM_SPEC] * (2 * n),
        input_output_aliases={i: i for i in range(2 * n)},
        compiler_params=pltpu.CompilerParams(has_side_effects=SPLIT_EFFECT),
    )(*parts, *zones, send_sems, recv_sems, after)
    return out[:n], out[n:]


def sum_grads(part, land, order, *, name):
    _, half, cols = part.shape
    tr = GRAD_ROWS if half % GRAD_ROWS == 0 else half

    def body(order_ref, p_ref, l1_ref, l2_ref, l3_ref, o_ref):
        o_ref[...] = ((p_ref[0].astype(f32) + l1_ref[0].astype(f32)) + l2_ref[0].astype(f32)) + l3_ref[0].astype(f32)

    slot = lambda j: pl.BlockSpec((1, tr, cols), lambda i, order_ref: (order_ref[j], i, 0))
    return pl.pallas_call(
        body, name=name,
        grid_spec=pltpu.PrefetchScalarGridSpec(
            num_scalar_prefetch=1, grid=(half // tr,), in_specs=[slot(0), slot(1), slot(2), slot(3)],
            out_specs=pl.BlockSpec((tr, cols), lambda i, order_ref: (i, 0))),
        out_shape=jax.ShapeDtypeStruct((half, cols), f32),
        compiler_params=_cparams("parallel"),
    )(order, part, land, land, land)


def _peer(x, y, c, r):
    return ((1 - x) if r & 4 else x, (1 - y) if r & 2 else y, (1 - c) if r & 1 else c)


def _reduce_async_copies(grads, land, send_sems, recv_sems, x, y, c):
    me = 4 * x + 2 * y + c
    sends, arrivals = [], []
    for a in range(len(grads)):
        for r in range(1, N_DEV):
            px, py, pc = _peer(x, y, c, r)
            sems = (send_sems.at[7 * a + r - 1], recv_sems.at[7 * a + r - 1], (px, py, pc))
            sends.append(_remote(_half(grads[a].at[2 * px + py], pc), land[a].at[me], *sems))
            slot = land[a].at[4 * px + 2 * py + pc]
            arrivals.append(_remote(slot, slot, *sems))
    return sends, arrivals


def reduce_grads_start(grads, *, name):
    n = len(grads)

    def body(*refs):
        g, land = refs[:n], refs[n:2 * n]
        send_sems, recv_sems, token = refs[2 * n], refs[2 * n + 1], refs[4 * n + 2]
        x, y, c = _place()
        for cp in _reduce_async_copies(g, land, send_sems, recv_sems, x, y, c)[0]:
            cp.start()
        token[...] = jnp.zeros_like(token)

    zones = [pltpu.with_memory_space_constraint(lax.empty((N_DEV, g.shape[1] // 2, g.shape[2]), g.dtype), pltpu.HBM)
             for g in grads]
    srcs = [pltpu.with_memory_space_constraint(g, pltpu.HBM) for g in grads]
    out = pl.pallas_call(
        body, name=name,
        out_shape=[pltpu.SemaphoreType.DMA((7 * n,)), pltpu.SemaphoreType.DMA((7 * n,))]
        + [pltpu.HBM(g.shape, g.dtype) for g in grads] + [pltpu.HBM(z.shape, z.dtype) for z in zones]
        + [jax.ShapeDtypeStruct((8, LANES), f32)],
        in_specs=[HBM_SPEC] * (2 * n),
        out_specs=[SEM_SPEC, SEM_SPEC] + [HBM_SPEC] * (2 * n) + [pl.BlockSpec(memory_space=pltpu.VMEM)],
        input_output_aliases={i: 2 + i for i in range(2 * n)},
        compiler_params=pltpu.CompilerParams(has_side_effects=SPLIT_EFFECT),
    )(*srcs, *zones)
    return out[0], out[1], out[2:2 + n], out[2 + n:2 + 2 * n], out[-1]


def reduce_grads_wait(send_sems, recv_sems, grads, zones, after, *, name):
    n = len(grads)

    def body(*refs):
        g, land = refs[:n], refs[n:2 * n]
        x, y, c = _place()
        sends, arrivals = _reduce_async_copies(g, land, refs[2 * n], refs[2 * n + 1], x, y, c)
        for cp in sends:
            cp.wait_send()
        for cp in arrivals:
            cp.wait_recv()

    hbm = [pltpu.HBM(a.shape, a.dtype) for a in list(grads) + list(zones)]
    out = pl.pallas_call(
        body, name=name, out_shape=hbm,
        in_specs=[HBM_SPEC] * (2 * n) + [SEM_SPEC, SEM_SPEC, pl.BlockSpec(memory_space=pl.ANY)],
        out_specs=[HBM_SPEC] * (2 * n),
        input_output_aliases={i: i for i in range(2 * n)},
        compiler_params=pltpu.CompilerParams(has_side_effects=SPLIT_EFFECT),
    )(*grads, *zones, send_sems, recv_sems, after)
    return out[:n], out[n:]


def sum_partials(g, land, where, *, name):
    _, half, cols = land.shape
    tr = GRAD_ROWS if half % GRAD_ROWS == 0 else half
    nb = half // tr

    def body(where_ref, g_ref, *rest):
        o_ref = rest[-1]
        acc = g_ref[0].astype(f32)
        for l_ref in rest[:-1]:
            acc = acc + l_ref[0].astype(f32)
        o_ref[...] = acc

    blk = (1, tr, cols)
    slot = lambda j: pl.BlockSpec(blk, lambda i, where_ref: (where_ref[2 + j], i, 0))
    return pl.pallas_call(
        body, name=name,
        grid_spec=pltpu.PrefetchScalarGridSpec(
            num_scalar_prefetch=1, grid=(nb,),
            in_specs=[pl.BlockSpec(blk, lambda i, where_ref: (where_ref[0], where_ref[1] * nb + i, 0))]
            + [slot(j) for j in range(N_DEV - 1)],
            out_specs=pl.BlockSpec((tr, cols), lambda i, where_ref: (i, 0))),
        out_shape=jax.ShapeDtypeStruct((half, cols), f32),
        compiler_params=_cparams("parallel"),
    )(where, g, *([land] * (N_DEV - 1)))


def swap_reduced_halves(mine, *, name):
    n = len(mine)

    def body(*refs):
        r, out, send_sems, recv_sems = refs[:n], refs[n:2 * n], refs[2 * n], refs[2 * n + 1]
        x, y, c = _place()
        copies = [_remote(r[a], out[a], send_sems.at[a], recv_sems.at[a], (x, y, 1 - c)) for a in range(n)]
        for cp in copies:
            cp.start()
        for cp in copies:
            cp.wait()

    return pl.pallas_call(
        body, name=name, in_specs=[HBM_SPEC] * n, out_specs=[HBM_SPEC] * n,
        out_shape=[jax.ShapeDtypeStruct(r.shape, r.dtype) for r in mine],
        scratch_shapes=[pltpu.SemaphoreType.DMA((n,)), pltpu.SemaphoreType.DMA((n,))],
    )(*mine)


def adamw_halves(w, mine, theirs, m, v, core, *, name):
    R, C = w.shape
    tr = min(GRAD_ROWS, R // 2)
    half_nb = R // 2 // tr

    def body(c_ref, w_ref, a_ref, b_ref, m_ref, v_ref, g_ref, d_ref, nm_ref, nv_ref):
        low = pl.program_id(0) < half_nb
        gv = jnp.where(low == (c_ref[0] == 0), a_ref[...], b_ref[...])
        nm = ADAM_B1 * m_ref[...] + (1.0 - ADAM_B1) * gv
        nv = ADAM_B2 * v_ref[...] + (1.0 - ADAM_B2) * jnp.square(gv)
        m_hat = nm / (1.0 - ADAM_B1 ** ADAM_STEP)
        v_hat = nv / (1.0 - ADAM_B2 ** ADAM_STEP)
        g_ref[...] = gv
        d_ref[...] = -ADAM_LR * (m_hat / (jnp.sqrt(v_hat) + ADAM_EPS) + ADAM_WD * w_ref[...])
        nm_ref[...] = nm
        nv_ref[...] = nv

    full = pl.BlockSpec((tr, C), lambda i, c_ref: (i, 0))
    part = pl.BlockSpec((tr, C), lambda i, c_ref: (i % half_nb, 0))
    out = jax.ShapeDtypeStruct((R, C), f32)
    return pl.pallas_call(
        body, name=name,
        grid_spec=pltpu.PrefetchScalarGridSpec(
            num_scalar_prefetch=1, grid=(2 * half_nb,), in_specs=[full, part, part, full, full], out_specs=[full] * 4),
        out_shape=[out] * 4, compiler_params=_cparams("parallel"),
    )(core, w, mine, theirs, m, v)


N_DEV = 8


def all_reduce_small(v):
    def body(src_ref, out_ref, land_ref, send_sems, recv_sems):
        x, y, c = _place()
        me = 4 * x + 2 * y + c
        copies = []
        for r in range(1, N_DEV):
            peer = ((1 - x) if r & 4 else x, (1 - y) if r & 2 else y, (1 - c) if r & 1 else c)
            copies.append(_remote(src_ref, land_ref.at[r], send_sems.at[r - 1], recv_sems.at[r - 1], peer))
        for cp in copies:
            cp.start()
        land_ref[0] = src_ref[...]
        for cp in copies:
            cp.wait()
        acc = land_ref[me]
        for d in range(1, N_DEV):
            acc = acc + land_ref[jnp.bitwise_xor(me, d)]
        out_ref[...] = acc

    vm = pl.BlockSpec(memory_space=pltpu.VMEM)
    return pl.pallas_call(
        body, name="all_reduce_small", in_specs=[vm], out_specs=vm,
        out_shape=jax.ShapeDtypeStruct(v.shape, v.dtype),
        scratch_shapes=[pltpu.VMEM((N_DEV,) + v.shape, v.dtype),
                        pltpu.SemaphoreType.DMA((N_DEV - 1,)), pltpu.SemaphoreType.DMA((N_DEV - 1,))],
    )(v)


def adamw(w, g, m, v, *, name, tr=None, tc=None):
    R, C = w.shape
    if tc is None:
        tr, tc = min(tr, R), C
        blk = pl.BlockSpec((tr, C), lambda i: (i, 0))
    else:
        tr = R
        blk = pl.BlockSpec((R, tc), lambda i: (0, i))

    def body(w_ref, g_ref, m_ref, v_ref, d_ref, nm_ref, nv_ref):
        gv = g_ref[...]
        nm = ADAM_B1 * m_ref[...] + (1.0 - ADAM_B1) * gv
        nv = ADAM_B2 * v_ref[...] + (1.0 - ADAM_B2) * jnp.square(gv)
        m_hat = nm / (1.0 - ADAM_B1 ** ADAM_STEP)
        v_hat = nv / (1.0 - ADAM_B2 ** ADAM_STEP)
        d_ref[...] = -ADAM_LR * (m_hat / (jnp.sqrt(v_hat) + ADAM_EPS) + ADAM_WD * w_ref[...])
        nm_ref[...] = nm
        nv_ref[...] = nv

    out = jax.ShapeDtypeStruct((R, C), f32)
    return pl.pallas_call(
        body, name=name, grid=((R // tr) * (C // tc),), in_specs=[blk] * 4, out_specs=[blk] * 3, out_shape=[out] * 3,
        compiler_params=_cparams("parallel"),
    )(w, g, m, v)


BIG_SHARDS = (("w_in", (1024, 900), True), ("w_out", (256, 1024), False), ("w_cq", (256, 512), False),
              ("w_ckv", (256, 1024), False), ("w_co", (512, 256), True), ("w_mlp1", (1024, 1024), True),
              ("w_mlp2", (1024, 1024), False))
CONV_SHARD = (CONV_WIDTH, 3 * GDN_WIDTH // N_CHIPS)
SMALL_DIMS = (("norm_mix_g", 1024), ("fox_qnorm_g", 64), ("fox_knorm_g", 64), ("fox_f_bias", 8), ("fox_onorm_g", 64),
              ("gdn_A_log", 4), ("gdn_dt_bias", 4), ("gdn_onorm_g", 128), ("norm_xattn_g", 1024), ("mem_norm_g", 1024),
              ("xattn_qnorm_g", 128), ("xattn_knorm_g", 128), ("norm_mlp_g", 1024))
WEIGHT_ORDER = ("norm_mix_g", "w_in", "fox_qnorm_g", "fox_knorm_g", "fox_f_bias", "fox_onorm_g", "gdn_conv_w", "gdn_A_log",
                "gdn_dt_bias", "gdn_onorm_g", "w_out", "norm_xattn_g", "mem_norm_g", "w_cq", "w_ckv", "xattn_qnorm_g",
                "xattn_knorm_g", "w_co", "norm_mlp_g", "w_mlp1", "w_mlp2")


def _pack_rows(pieces, rows, lead=()):
    cat = jnp.concatenate([p.reshape(lead + (-1,)) for p in pieces], axis=-1)
    cat = jnp.pad(cat, [(0, 0)] * len(lead) + [(0, rows * LANES - cat.shape[-1])])
    return cat.reshape(lead + (rows, LANES))


def _unpack_rows(buf, sizes, lead=()):
    flat = buf.reshape(lead + (-1,))
    out, off = [], 0
    for n in sizes:
        out.append(flat[..., off:off + n])
        off += n
    return out


def _conv_to_wire(conv):
    return lax.bitcast_convert_type(conv, bf16)


def _conv_from_wire(wire):
    return lax.bitcast_convert_type(wire, f32)


SMALL_ROWS = 96
SMALL_ADAM_ROWS = 56


def kernel(x, mem, norm_mix_g, w_in, fox_qnorm_g, fox_knorm_g, fox_f_bias, fox_onorm_g, gdn_conv_w, gdn_A_log, gdn_dt_bias, gdn_onorm_g, w_out, norm_xattn_g, mem_norm_g, w_cq, w_ckv, xattn_qnorm_g, xattn_knorm_g, w_co, norm_mlp_g, w_mlp1, w_mlp2, loss_target, m_norm_mix_g, m_w_in, m_fox_qnorm_g, m_fox_knorm_g, m_fox_f_bias, m_fox_onorm_g, m_gdn_conv_w, m_gdn_A_log, m_gdn_dt_bias, m_gdn_onorm_g, m_w_out, m_norm_xattn_g, m_mem_norm_g, m_w_cq, m_w_ckv, m_xattn_qnorm_g, m_xattn_knorm_g, m_w_co, m_norm_mlp_g, m_w_mlp1, m_w_mlp2, v_norm_mix_g, v_w_in, v_fox_qnorm_g, v_fox_knorm_g, v_fox_f_bias, v_fox_onorm_g, v_gdn_conv_w, v_gdn_A_log, v_gdn_dt_bias, v_gdn_onorm_g, v_w_out, v_norm_xattn_g, v_mem_norm_g, v_w_cq, v_w_ckv, v_xattn_qnorm_g, v_xattn_knorm_g, v_w_co, v_norm_mlp_g, v_w_mlp1, v_w_mlp2):
    wts = dict(norm_mix_g=norm_mix_g, w_in=w_in, fox_qnorm_g=fox_qnorm_g, fox_knorm_g=fox_knorm_g, fox_f_bias=fox_f_bias,
               fox_onorm_g=fox_onorm_g, gdn_conv_w=gdn_conv_w, gdn_A_log=gdn_A_log, gdn_dt_bias=gdn_dt_bias,
               gdn_onorm_g=gdn_onorm_g, w_out=w_out, norm_xattn_g=norm_xattn_g, mem_norm_g=mem_norm_g, w_cq=w_cq, w_ckv=w_ckv,
               xattn_qnorm_g=xattn_qnorm_g, xattn_knorm_g=xattn_knorm_g, w_co=w_co, norm_mlp_g=norm_mlp_g, w_mlp1=w_mlp1,
               w_mlp2=w_mlp2)
    mom = dict(norm_mix_g=m_norm_mix_g, w_in=m_w_in, fox_qnorm_g=m_fox_qnorm_g, fox_knorm_g=m_fox_knorm_g,
               fox_f_bias=m_fox_f_bias, fox_onorm_g=m_fox_onorm_g, gdn_conv_w=m_gdn_conv_w, gdn_A_log=m_gdn_A_log,
               gdn_dt_bias=m_gdn_dt_bias, gdn_onorm_g=m_gdn_onorm_g, w_out=m_w_out, norm_xattn_g=m_norm_xattn_g,
               mem_norm_g=m_mem_norm_g, w_cq=m_w_cq, w_ckv=m_w_ckv, xattn_qnorm_g=m_xattn_qnorm_g,
               xattn_knorm_g=m_xattn_knorm_g, w_co=m_w_co, norm_mlp_g=m_norm_mlp_g, w_mlp1=m_w_mlp1, w_mlp2=m_w_mlp2)
    var = dict(norm_mix_g=v_norm_mix_g, w_in=v_w_in, fox_qnorm_g=v_fox_qnorm_g, fox_knorm_g=v_fox_knorm_g,
               fox_f_bias=v_fox_f_bias, fox_onorm_g=v_fox_onorm_g, gdn_conv_w=v_gdn_conv_w, gdn_A_log=v_gdn_A_log,
               gdn_dt_bias=v_gdn_dt_bias, gdn_onorm_g=v_gdn_onorm_g, w_out=v_w_out, norm_xattn_g=v_norm_xattn_g,
               mem_norm_g=v_mem_norm_g, w_cq=v_w_cq, w_ckv=v_w_ckv, xattn_qnorm_g=v_xattn_qnorm_g,
               xattn_knorm_g=v_xattn_knorm_g, w_co=v_w_co, norm_mlp_g=v_norm_mlp_g, w_mlp1=v_w_mlp1, w_mlp2=v_w_mlp2)
    B, S, D = x.shape
    T = B * S
    big_names = [n for n, _, _ in BIG_SHARDS]
    chip = 2 * lax.axis_index("x") + lax.axis_index("y")
    core = lax.axis_index("c").astype(jnp.int32).reshape(1)

    shards = {n: wts[n][0].astype(MXU_DTYPE) for n in big_names[1:]}
    in_t = lambda p: jnp.swapaxes(p[0], 0, 1)
    shards["w_in"] = jnp.pad(in_t(w_in).astype(MXU_DTYPE), ((0, IN_SHARD_PAD - IN_SHARD), (0, 0)))
    w_in_all, conv_all = gather_weights([shards["w_in"]], gdn_conv_w[0])
    late = big_names[1:]
    send_sems, recv_sems, late_src, late_zones, token = gather_weights_start([shards[n] for n in late], conv_all)
    own = lambda g, s: lax.dynamic_update_slice(g, s[None], (chip,) + (0,) * s.ndim)
    full = {"w_in": own(w_in_all, shards["w_in"])}
    conv_full = own(conv_all, gdn_conv_w[0]).transpose(1, 0, 2).reshape(CONV_WIDTH, 3 * GDN_WIDTH)
    rows = lambda g: g.reshape(N_CHIPS * g.shape[1], g.shape[2])

    def late_weights(after):
        zones = gather_weights_wait(send_sems, recv_sems, late_src, late_zones, after)
        got = {n: own(z, shards[n]) for n, z in zip(late, zones)}
        return dict(w_out=rows(got["w_out"]), w_cq=rows(got["w_cq"]), w_ckv=rows(got["w_ckv"]),
                    w_co=got["w_co"].transpose(1, 0, 2).reshape(XATTN_WIDTH, D_MODEL),
                    w_mlp1=got["w_mlp1"], w_mlp2=rows(got["w_mlp2"]))

    in_flight = []

    def grads_ready(ready):
        names = list(ready)
        *started, tok = reduce_grads_start([ready[n] for n in names], name="reduce_grads_start_%d" % len(in_flight))
        in_flight.append((names, *started))
        return tok

    w_in_t = full["w_in"][:, :IN_SHARD].reshape(IN_DIM, D_MODEL)
    w = dict(wa_t=align_w_in_t(w_in_t), conv_w=conv_full, late=late_weights, grads_ready=grads_ready)
    sp = {n: wts[n] for n, _ in SMALL_DIMS}
    sp["norm_mix_g"] = sp["norm_mix_g"] + token[0, 0]

    loss_part, grad_x, g_big, g_small = local_step(x.reshape(T, D), mem.reshape(-1, D), loss_target.reshape(T, D), w, sp, B=B)

    small_pieces = [g_small[n] for n, _ in SMALL_DIMS] + [g_small["gdn_conv_w"], loss_part]
    small_sizes = [d for _, d in SMALL_DIMS] + [CONV_WIDTH * 3 * GDN_WIDTH, LANES]
    red_small = _unpack_rows(all_reduce_small(_pack_rows(small_pieces, SMALL_ROWS)), small_sizes)
    grads = {n: p.reshape(1, d) for (n, d), p in zip(SMALL_DIMS, red_small)}
    conv_grad = lax.dynamic_slice(red_small[-2].reshape(CONV_WIDTH, 3 * GDN_WIDTH), (0, chip * CONV_SHARD[1]), CONV_SHARD)
    grads["gdn_conv_w"] = conv_grad.reshape((1,) + CONV_SHARD)
    loss = red_small[-1][0]

    parts, zones = {}, {}

    def wait_group(k, after):
        names, send_sems, recv_sems, thru, land = in_flight[k]
        thru, land = reduce_grads_wait(send_sems, recv_sems, thru, land, after, name="reduce_grads_wait_%d" % k)
        parts.update(zip(names, thru))
        zones.update(zip(names, land))

    wait_group(0, grad_x)
    wait_group(1, grad_x)
    dev = 2 * chip + core[0]
    where = jnp.stack([chip, core[0]] + [dev ^ r for r in range(1, N_DEV)]).astype(jnp.int32)
    mine = [sum_partials(parts[n], zones[n], where, name="sum_partials_" + n) for n in late]
    theirs = swap_reduced_halves(mine, name="swap_reduced_halves")

    delta, new_m, new_v = {}, {}, {}
    for n, a, b in zip(late, mine, theirs):
        g, d, nm, nv = adamw_halves(wts[n][0], a, b, mom[n][0], var[n][0], core, name="adamw_" + n)
        grads[n], delta[n], new_m[n], new_v[n] = g[None], d[None], nm[None], nv[None]
    wait_group(2, new_v[late[-1]])
    mine_in = sum_partials(parts["w_in"], zones["w_in"], where, name="sum_partials_w_in")
    (theirs_in,) = swap_reduced_halves([mine_in], name="swap_reduced_halves_w_in")
    south = core[0] == 0
    g_in_t = jnp.concatenate([jnp.where(south, mine_in, theirs_in), jnp.where(south, theirs_in, mine_in)])[:IN_SHARD]
    back = lambda t: jnp.swapaxes(t, 0, 1)[None]
    d, nm, nv = adamw(in_t(w_in), g_in_t, in_t(m_w_in), in_t(v_w_in), name="adamw_w_in", tc=256)
    grads["w_in"], delta["w_in"], new_m["w_in"], new_v["w_in"] = back(g_in_t), back(d), back(nm), back(nv)
    small_names = [n for n, _ in SMALL_DIMS] + ["gdn_conv_w"]
    small_sz = [d for _, d in SMALL_DIMS] + [CONV_SHARD[0] * CONV_SHARD[1]]
    packed4 = [_pack_rows([src[n] for n in small_names], SMALL_ADAM_ROWS) for src in (wts, grads, mom, var)]
    outs = adamw(*packed4, name="adamw_small", tr=SMALL_ADAM_ROWS)
    for dst, buf in zip((delta, new_m, new_v), outs):
        for n, p in zip(small_names, _unpack_rows(buf, small_sz)):
            dst[n] = p.reshape(wts[n].shape)

    return (loss, grad_x.reshape(B, S, D), *[grads[n] for n in WEIGHT_ORDER], *[delta[n] for n in WEIGHT_ORDER],
            *[new_m[n] for n in WEIGHT_ORDER], *[new_v[n] for n in WEIGHT_ORDER])
```

```python
import functools

import jax
import jax.numpy as jnp
import numpy as np
from jax import lax
from jax.experimental import pallas as pl
from jax.experimental.pallas import tpu as pltpu

f32 = jnp.float32
bf16 = jnp.bfloat16
MXU_DTYPE = jnp.bfloat16
WIRE_DTYPE = jnp.bfloat16
INV_PRECISION = None

D_MODEL = 1024
FOX_HEADS = 8
FOX_HEAD_DIM = 64
FOX_WIDTH = 512
GDN_HEADS = 4
GDN_HEAD_DIM = 128
GDN_WIDTH = 512
CONV_WIDTH = 4
GDN_CHUNK = 64
XATTN_HEADS = 4
XATTN_HEAD_DIM = 128
XATTN_WIDTH = 512
D_FF = 4096
IN_DIM = 3600
EPS = 1e-6
NEG_INF = -1e30
LANES = 128
ADAM_LR = 0.001
ADAM_B1 = 0.9
ADAM_B2 = 0.999
ADAM_EPS = 1e-08
ADAM_WD = 0.01
ADAM_STEP = 10
VMEM_LIMIT = 48 * 1024 * 1024

COL_FOX = 0
COL_GDN = 1536
COL_Z = 3072
COL_SMALL = 3584
IN_ALIGNED = 3840
IN_TILE = 768
SM_F = 0
SM_B = 8
SM_A = 12


def _cparams(*sem):
    return pltpu.CompilerParams(dimension_semantics=sem, vmem_limit_bytes=VMEM_LIMIT)


def _mx(v):
    return v.astype(MXU_DTYPE)


def _dot(a, b, dims, precision=None):
    return lax.dot_general(a, b, (dims, ((), ())), preferred_element_type=f32, precision=precision)


def _dotm(a, b, dims):
    return _dot(_mx(a), _mx(b), dims)


NN = ((1,), (0,))
NT = ((1,), (1,))
TN = ((0,), (0,))


def matmul(a, b, *, name, ta=False, tb=False, b_stacked=False, out_stacked=False, residual=None, relu2_out=False,
           relu2_bwd_aux=None, out_dtype=f32, tm=1024, tn=1024, tk=1024):
    M, K = (a.shape[1], a.shape[0]) if ta else a.shape
    if b_stacked:
        b_cols = b.shape[2]
        N, tk = (b.shape[1], min(tk, b_cols)) if tb else (N_CHIPS * b_cols, tk)
        tn = tn if tb else min(tn, b_cols)
        assert K == (N_CHIPS * b_cols if tb else b.shape[1]), (name, a.shape, b.shape)
    else:
        N = b.shape[0] if tb else b.shape[1]
    if out_stacked:
        tn = min(tn, N // N_CHIPS)
    tm, tn, tk = min(tm, M), min(tn, N), min(tk, K)
    assert M % tm == 0 and N % tn == 0 and K % tk == 0, (name, M, N, K)
    nk = K // tk
    has_res = residual is not None
    has_aux = relu2_bwd_aux is not None

    def body(*refs):
        a_ref, b_ref = refs[0], refs[1]
        pos = 2
        res_ref = aux_ref = None
        if has_res:
            res_ref = refs[pos]
            pos += 1
        if has_aux:
            aux_ref = refs[pos]
            pos += 1
        o_ref = refs[pos]
        k = pl.program_id(2)
        dims = ((0,) if ta else (1,), (1,) if tb else (0,))
        part = _dot(_mx(a_ref[...]), _mx(b_ref[...]), dims)

        def finish(r):
            if has_res:
                r = r + res_ref[...]
            if has_aux:
                r = r * (2.0 * jnp.sqrt(aux_ref[...].astype(f32)))
            if relu2_out:
                o_ref[...] = jnp.square(jnp.maximum(r, 0.0)).astype(o_ref.dtype)
            else:
                o_ref[...] = r.astype(o_ref.dtype)

        if nk == 1:
            finish(part)
            return
        acc_ref = refs[pos + 1]

        @pl.when(k == 0)
        def _():
            acc_ref[...] = part

        @pl.when((k > 0) & (k < nk - 1))
        def _():
            acc_ref[...] += part

        @pl.when(k == nk - 1)
        def _():
            finish(acc_ref[...] + part)

    a_spec = pl.BlockSpec((tk, tm), lambda i, j, k: (k, i)) if ta else pl.BlockSpec((tm, tk), lambda i, j, k: (i, k))
    if b_stacked and tb:
        per = b_cols // tk
        b_spec = pl.BlockSpec((None, tn, tk), lambda i, j, k: (k // per, j, k % per))
    elif b_stacked:
        per = b_cols // tn
        b_spec = pl.BlockSpec((None, tk, tn), lambda i, j, k: (j // per, k, j % per))
    else:
        b_spec = pl.BlockSpec((tn, tk), lambda i, j, k: (j, k)) if tb else pl.BlockSpec((tk, tn), lambda i, j, k: (k, j))
    if out_stacked:
        assert not (has_res or has_aux or relu2_out), name
        per_o = N // N_CHIPS // tn
        o_spec = pl.BlockSpec((None, tm, tn), lambda i, j, k: (j // per_o, i, j % per_o))
        out_full = (N_CHIPS, M, N // N_CHIPS)
    else:
        o_spec = pl.BlockSpec((tm, tn), lambda i, j, k: (i, j))
        out_full = (M, N)
    in_specs, args = [a_spec, b_spec], [a, b]
    if has_res:
        in_specs.append(o_spec)
        args.append(residual)
    if has_aux:
        in_specs.append(o_spec)
        args.append(relu2_bwd_aux)
    out_shape = [jax.ShapeDtypeStruct(out_full, out_dtype)]
    out_specs = [o_spec]
    res = pl.pallas_call(
        body, name=name, grid=(M // tm, N // tn, nk), in_specs=in_specs, out_specs=out_specs, out_shape=out_shape,
        scratch_shapes=[pltpu.VMEM((tm, tn), f32)] if nk > 1 else [],
        compiler_params=_cparams("parallel", "parallel", "arbitrary"),
    )(*args)
    return res[0]


def matmul_rows(a, b, extras, *, name, mode, tb=False, b_stacked=False, tm=1024, tk=1024):
    M, K = a.shape
    N = D_MODEL
    if b_stacked:
        assert tb, name
        tk = min(tk, b.shape[2])
        per = b.shape[2] // tk
        b_spec = pl.BlockSpec((None, N, tk), lambda i, k: (k // per, 0, k % per))
    elif tb:
        tk = min(tk, K)
        b_spec = pl.BlockSpec((N, tk), lambda i, k: (0, k))
    else:
        tk = min(tk, K)
        b_spec = pl.BlockSpec((tk, N), lambda i, k: (k, 0))
    tm = min(tm, M)
    assert M % tm == 0 and K % tk == 0, (name, M, K)
    nk = K // tk
    extras = [e for e in extras if e is not None]
    n_ex = len(extras)

    def body(*refs):
        a_ref, b_ref = refs[0], refs[1]
        ex = refs[2:2 + n_ex]
        o_ref = refs[2 + n_ex]
        n_out = 3 if mode == "loss" else 2
        s_ref = refs[1 + n_ex + n_out]
        i, k = pl.program_id(0), pl.program_id(1)
        part = _dot(_mx(a_ref[...]), _mx(b_ref[...]), ((1,), (1,) if tb else (0,)))

        def finish(y):
            if mode == "rms_fwd":
                y = y + ex[0][...]
                o_ref[...] = y
                s_ref[...] = (y * lax.rsqrt(jnp.mean(y * y, axis=-1, keepdims=True) + EPS) * ex[1][...]).astype(MXU_DTYPE)
                return

            @pl.when(i == 0)
            def _():
                s_ref[...] = jnp.zeros_like(s_ref)

            if mode == "rms_bwd":
                xv, gv = ex[0][...], ex[1][...]
                rstd = lax.rsqrt(jnp.mean(xv * xv, axis=-1, keepdims=True) + EPS)
                xhat = xv * rstd
                gd = y * gv
                dx = rstd * (gd - xhat * jnp.mean(gd * xhat, axis=-1, keepdims=True))
                o_ref[...] = dx + ex[2][...] if n_ex == 3 else dx
                s_ref[...] += jnp.sum(y * xhat, axis=0, keepdims=True)
            else:
                e = y + ex[0][...] - ex[1][...]
                o_ref[...] = e * (1.0 / N)
                refs[3 + n_ex][...] = (e * (1.0 / N)).astype(MXU_DTYPE)
                tot = 0.5 * jnp.sum(jnp.mean(e * e, axis=-1, keepdims=True), axis=0, keepdims=True)
                s_ref[...] += jnp.broadcast_to(tot, s_ref.shape)

        if nk == 1:
            finish(part)
            return
        acc_ref = refs[2 + n_ex + n_out]

        @pl.when(k == 0)
        def _():
            acc_ref[...] = part

        @pl.when((k > 0) & (k < nk - 1))
        def _():
            acc_ref[...] += part

        @pl.when(k == nk - 1)
        def _():
            finish(acc_ref[...] + part)

    row = pl.BlockSpec((tm, N), lambda i, k: (i, 0))
    vec = pl.BlockSpec((1, N), lambda i, k: (0, 0))
    if mode == "rms_bwd":
        ex_specs = [row, vec] + ([row] if n_ex == 3 else [])
        s_shape, s_spec = jax.ShapeDtypeStruct((1, N), f32), vec
    elif mode == "rms_fwd":
        ex_specs = [row, vec]
        s_shape, s_spec = jax.ShapeDtypeStruct((M, N), MXU_DTYPE), row
    else:
        ex_specs = [row, row]
        s_shape, s_spec = jax.ShapeDtypeStruct((1, LANES), f32), pl.BlockSpec((1, LANES), lambda i, k: (0, 0))
    return pl.pallas_call(
        body, name=name, grid=(M // tm, nk),
        in_specs=[pl.BlockSpec((tm, tk), lambda i, k: (i, k)), b_spec] + ex_specs,
        out_specs=[row] * (2 if mode == "loss" else 1) + [s_spec],
        out_shape=[jax.ShapeDtypeStruct((M, N), f32)] + ([jax.ShapeDtypeStruct((M, N), MXU_DTYPE)] if mode == "loss" else [])
        + [s_shape],
        scratch_shapes=[pltpu.VMEM((tm, N), f32)] if nk > 1 else [],
        compiler_params=_cparams("arbitrary", "arbitrary"),
    )(a, b, *extras)


def rms_fwd(x, g, *, name, tr=1024):
    R, D = x.shape
    tr = min(tr, R)

    def body(x_ref, g_ref, o_ref):
        xv = x_ref[...]
        y = xv * lax.rsqrt(jnp.mean(xv * xv, axis=-1, keepdims=True) + EPS)
        o_ref[...] = (y * g_ref[...]).astype(o_ref.dtype)

    return pl.pallas_call(
        body, name=name, grid=(R // tr,),
        in_specs=[pl.BlockSpec((tr, D), lambda i: (i, 0)), pl.BlockSpec((1, D), lambda i: (0, 0))],
        out_specs=pl.BlockSpec((tr, D), lambda i: (i, 0)),
        out_shape=jax.ShapeDtypeStruct((R, D), MXU_DTYPE),
        compiler_params=_cparams("parallel"),
    )(x, g)


def rms_bwd(x, g, dh, residual, *, name, tr=512):
    R, D = x.shape
    tr = min(tr, R)
    has_res = residual is not None

    def body(*refs):
        if has_res:
            x_ref, g_ref, dh_ref, res_ref, dx_ref, dg_ref = refs
        else:
            x_ref, g_ref, dh_ref, dx_ref, dg_ref = refs
        xv = x_ref[...]
        rstd = lax.rsqrt(jnp.mean(xv * xv, axis=-1, keepdims=True) + EPS)
        xhat = xv * rstd
        dh = dh_ref[...].astype(f32)
        gd = dh * g_ref[...]
        dx = rstd * (gd - xhat * jnp.mean(gd * xhat, axis=-1, keepdims=True))
        if has_res:
            dx = dx + res_ref[...]
        dx_ref[...] = dx

        @pl.when(pl.program_id(0) == 0)
        def _():
            dg_ref[...] = jnp.zeros_like(dg_ref)

        dg_ref[...] += jnp.sum(dh * xhat, axis=0, keepdims=True)

    row = pl.BlockSpec((tr, D), lambda i: (i, 0))
    vec = pl.BlockSpec((1, D), lambda i: (0, 0))
    in_specs = [row, vec, row] + ([row] if has_res else [])
    args = [x, g, dh] + ([residual] if has_res else [])
    return pl.pallas_call(
        body, name=name, grid=(R // tr,), in_specs=in_specs, out_specs=[row, vec],
        out_shape=[jax.ShapeDtypeStruct((R, D), f32), jax.ShapeDtypeStruct((1, D), f32)],
        compiler_params=_cparams("arbitrary"),
    )(*args)


def loss_head(y, target, *, tr=512):
    R, D = y.shape
    tr = min(tr, R)

    def body(y_ref, t_ref, dy_ref, loss_ref):
        e = y_ref[...] - t_ref[...]
        dy_ref[...] = e * (1.0 / D)

        @pl.when(pl.program_id(0) == 0)
        def _():
            loss_ref[...] = jnp.zeros_like(loss_ref)

        part = 0.5 * jnp.sum(jnp.mean(e * e, axis=-1, keepdims=True), axis=0, keepdims=True)
        loss_ref[...] += jnp.broadcast_to(part, loss_ref.shape)

    row = pl.BlockSpec((tr, D), lambda i: (i, 0))
    return pl.pallas_call(
        body, name="loss_head", grid=(R // tr,), in_specs=[row, row],
        out_specs=[row, pl.BlockSpec((1, LANES), lambda i: (0, 0))],
        out_shape=[jax.ShapeDtypeStruct((R, D), f32), jax.ShapeDtypeStruct((1, LANES), f32)],
        compiler_params=_cparams("arbitrary"),
    )(y, target)


def _head_rms(v, g):
    r = lax.rsqrt(jnp.mean(v * v, axis=-1, keepdims=True) + EPS)
    return v * r * g, r


def _head_rms_bwd(v, r, g, dn):
    vhat = v * r
    gd = dn * g
    dv = r * (gd - vhat * jnp.mean(gd * vhat, axis=-1, keepdims=True))
    return dv, jnp.sum(dn * vhat, axis=0, keepdims=True)


def _softmax_rows(s):
    m = jnp.max(s, axis=-1, keepdims=True)
    e = jnp.exp(s - m)
    return e / jnp.sum(e, axis=-1, keepdims=True)


def xattn_fwd(cq, ckv, gq, gk, *, B, tq=1024):
    T = cq.shape[0]
    S = T // B
    M = ckv.shape[0] // B
    tq = min(tq, S)
    nq = S // tq
    hd, W = XATTN_HEAD_DIM, XATTN_WIDTH
    scale = hd ** -0.5

    def body(q_ref, k_ref, v_ref, gq_ref, gk_ref, o_ref):
        for h in range(XATTN_HEADS):
            sl = slice(h * hd, (h + 1) * hd)
            qn, _ = _head_rms(q_ref[:, sl], gq_ref[...])
            kn, _ = _head_rms(k_ref[:, sl], gk_ref[...])
            p = _softmax_rows(_dot(_mx(qn), _mx(kn), NT) * scale)
            o_ref[:, sl] = _dot(_mx(p), _mx(v_ref[:, sl]), NN).astype(o_ref.dtype)

    vec = pl.BlockSpec((1, hd), lambda b, i: (0, 0))
    qspec = pl.BlockSpec((tq, W), lambda b, i: (b * nq + i, 0))
    return pl.pallas_call(
        body, name="xattn_fwd", grid=(B, nq),
        in_specs=[qspec, pl.BlockSpec((M, W), lambda b, i: (b, 0)), pl.BlockSpec((M, W), lambda b, i: (b, 1)), vec, vec],
        out_specs=qspec, out_shape=jax.ShapeDtypeStruct((T, W), MXU_DTYPE),
        compiler_params=_cparams("parallel", "parallel"),
    )(cq, ckv, ckv, gq, gk)


def xattn_bwd(cq, ckv, gq, gk, dco, *, B, tq=1024):
    T = cq.shape[0]
    S = T // B
    M = ckv.shape[0] // B
    tq = min(tq, S)
    nq = S // tq
    hd, W = XATTN_HEAD_DIM, XATTN_WIDTH
    scale = hd ** -0.5

    def body(q_ref, k_ref, v_ref, gq_ref, gk_ref, do_ref, dq_ref, dkv_ref, dgq_ref, dgk_ref, dkn_acc, dv_acc):
        b, i = pl.program_id(0), pl.program_id(1)

        @pl.when((b == 0) & (i == 0))
        def _():
            dgq_ref[...] = jnp.zeros_like(dgq_ref)
            dgk_ref[...] = jnp.zeros_like(dgk_ref)

        @pl.when(i == 0)
        def _():
            dkn_acc[...] = jnp.zeros_like(dkn_acc)
            dv_acc[...] = jnp.zeros_like(dv_acc)

        gqv, gkv = gq_ref[...], gk_ref[...]
        for h in range(XATTN_HEADS):
            sl = slice(h * hd, (h + 1) * hd)
            q, k, v = q_ref[:, sl], k_ref[:, sl], v_ref[:, sl]
            qn, rq = _head_rms(q, gqv)
            kn, _ = _head_rms(k, gkv)
            p = _softmax_rows(_dot(_mx(qn), _mx(kn), NT) * scale)
            do = do_ref[:, sl]
            dv_acc[:, sl] += _dot(_mx(p), _mx(do), TN)
            dp = _dot(_mx(do), _mx(v), NT)
            ds = p * (dp - jnp.sum(dp * p, axis=-1, keepdims=True)) * scale
            dqn = _dot(_mx(ds), _mx(kn), NN)
            dkn_acc[:, sl] += _dot(_mx(ds), _mx(qn), TN)
            dq, dgq = _head_rms_bwd(q, rq, gqv, dqn)
            dq_ref[:, sl] = dq.astype(dq_ref.dtype)
            dgq_ref[...] += dgq

        @pl.when(i == nq - 1)
        def _():
            for h in range(XATTN_HEADS):
                sl = slice(h * hd, (h + 1) * hd)
                k = k_ref[:, sl]
                rk = lax.rsqrt(jnp.mean(k * k, axis=-1, keepdims=True) + EPS)
                dk, dgk = _head_rms_bwd(k, rk, gkv, dkn_acc[:, sl])
                dkv_ref[:, sl] = dk.astype(dkv_ref.dtype)
                dkv_ref[:, slice(W + h * hd, W + (h + 1) * hd)] = dv_acc[:, sl].astype(dkv_ref.dtype)
                dgk_ref[...] += dgk

    vec = pl.BlockSpec((1, hd), lambda b, i: (0, 0))
    qspec = pl.BlockSpec((tq, W), lambda b, i: (b * nq + i, 0))
    return pl.pallas_call(
        body, name="xattn_bwd", grid=(B, nq),
        in_specs=[qspec, pl.BlockSpec((M, W), lambda b, i: (b, 0)), pl.BlockSpec((M, W), lambda b, i: (b, 1)), vec, vec, qspec],
        out_specs=[qspec, pl.BlockSpec((M, 2 * W), lambda b, i: (b, 0)), vec, vec],
        out_shape=[jax.ShapeDtypeStruct((T, W), MXU_DTYPE), jax.ShapeDtypeStruct((B * M, 2 * W), MXU_DTYPE),
                   jax.ShapeDtypeStruct((1, hd), f32), jax.ShapeDtypeStruct((1, hd), f32)],
        scratch_shapes=[pltpu.VMEM((M, W), f32), pltpu.VMEM((M, W), f32)],
        compiler_params=_cparams("arbitrary", "arbitrary"),
    )(cq, ckv, ckv, gq, gk, dco)


FOX_PAIRS = FOX_HEADS // 2


def _fox_scores(qn, kn, ccol, crow, q0, tq, S, scale):
    s = _dot(_mx(qn), _mx(kn), NT) * scale + ccol - crow
    qpos = q0 + lax.broadcasted_iota(jnp.int32, (tq, S), 0)
    kpos = lax.broadcasted_iota(jnp.int32, (tq, S), 1)
    return jnp.where(kpos <= qpos, s, NEG_INF)


def fox_fwd(P, ccol, crow, gq, gk, go, *, B, tq=256):
    T = P.shape[0]
    S = T // B
    tq = min(tq, S)
    nq = S // tq
    hd = FOX_HEAD_DIM
    scale = hd ** -0.5

    def body(q_ref, k_ref, v_ref, ccol_ref, crow_ref, gq_ref, gk_ref, go_ref, o_ref, oa_ref):
        q0 = pl.program_id(2) * tq
        for e in range(2):
            sl = slice(e * hd, (e + 1) * hd)
            qn, _ = _head_rms(q_ref[:, sl], gq_ref[:, sl])
            kn, _ = _head_rms(k_ref[:, sl], gk_ref[:, sl])
            p = _softmax_rows(_fox_scores(qn, kn, ccol_ref[0, e], crow_ref[0, e], q0, tq, S, scale))
            o = _dot(_mx(p), _mx(v_ref[:, sl]), NN)
            o_ref[:, sl] = o
            oa_ref[:, sl] = _head_rms(o, go_ref[:, sl])[0].astype(oa_ref.dtype)

    W = 2 * hd
    vec = pl.BlockSpec((1, W), lambda b, h, i: (0, 0))
    ospec = pl.BlockSpec((tq, W), lambda b, h, i: (b * nq + i, h))
    return pl.pallas_call(
        body, name="fox_fwd", grid=(B, FOX_PAIRS, nq),
        in_specs=[pl.BlockSpec((tq, W), lambda b, h, i: (b * nq + i, h)),
                  pl.BlockSpec((S, W), lambda b, h, i: (b, FOX_PAIRS + h)),
                  pl.BlockSpec((S, W), lambda b, h, i: (b, 2 * FOX_PAIRS + h)),
                  pl.BlockSpec((1, 2, tq, 1), lambda b, h, i: (b, h, i, 0)),
                  pl.BlockSpec((1, 2, 1, S), lambda b, h, i: (b, h, 0, 0)), vec, vec, vec],
        out_specs=[ospec, ospec],
        out_shape=[jax.ShapeDtypeStruct((T, FOX_WIDTH), f32), jax.ShapeDtypeStruct((T, FOX_WIDTH), MXU_DTYPE)],
        compiler_params=_cparams("parallel", "parallel", "parallel"),
    )(P, P, P, ccol, crow, gq, gk, go)


def fox_bwd(P, ccol, crow, gq, gk, go, o_raw, d_oab, *, B, tq=256):
    T = P.shape[0]
    S = T // B
    tq = min(tq, S)
    nq = S // tq
    hd = FOX_HEAD_DIM
    scale = hd ** -0.5

    def body(q_ref, k_ref, v_ref, ccol_ref, crow_ref, gq_ref, gk_ref, go_ref, o_ref, doa_ref,
             dq_ref, dk_ref, dv_ref, dccol_ref, dcrow_ref, dgq_ref, dgk_ref, dgo_ref, dkn_acc, dv_acc, dcrow_acc):
        b, h, i = pl.program_id(0), pl.program_id(1), pl.program_id(2)
        q0 = i * tq

        @pl.when((b == 0) & (h == 0) & (i == 0))
        def _():
            dgq_ref[...] = jnp.zeros_like(dgq_ref)
            dgk_ref[...] = jnp.zeros_like(dgk_ref)
            dgo_ref[...] = jnp.zeros_like(dgo_ref)

        @pl.when(i == 0)
        def _():
            dkn_acc[...] = jnp.zeros_like(dkn_acc)
            dv_acc[...] = jnp.zeros_like(dv_acc)
            dcrow_acc[...] = jnp.zeros_like(dcrow_acc)

        for e in range(2):
            sl = slice(e * hd, (e + 1) * hd)
            q, k, v = q_ref[:, sl], k_ref[:, sl], v_ref[:, sl]
            gqv, gkv, gov = gq_ref[:, sl], gk_ref[:, sl], go_ref[:, sl]
            qn, rq = _head_rms(q, gqv)
            kn, rk = _head_rms(k, gkv)
            p = _softmax_rows(_fox_scores(qn, kn, ccol_ref[0, e], crow_ref[0, e], q0, tq, S, scale))
            o = o_ref[:, sl]
            ro = lax.rsqrt(jnp.mean(o * o, axis=-1, keepdims=True) + EPS)
            do, dgo = _head_rms_bwd(o, ro, gov, doa_ref[:, sl])
            dgo_ref[:, sl] += dgo
            dv_acc[e] += _dot(_mx(p), _mx(do), TN)
            dp = _dot(_mx(do), _mx(v), NT)
            ds = p * (dp - jnp.sum(do * o, axis=-1, keepdims=True))
            dccol_ref[0, e] = jnp.sum(ds, axis=1, keepdims=True)
            dcrow_acc[e] -= jnp.sum(ds, axis=0, keepdims=True)
            dqn = _dot(_mx(ds), _mx(kn), NN) * scale
            dkn_acc[e] += _dot(_mx(ds), _mx(qn), TN) * scale
            dq, dgq = _head_rms_bwd(q, rq, gqv, dqn)
            dq_ref[:, sl] = dq.astype(dq_ref.dtype)
            dgq_ref[:, sl] += dgq

        @pl.when(i == nq - 1)
        def _():
            for e in range(2):
                sl = slice(e * hd, (e + 1) * hd)
                k = k_ref[:, sl]
                gkv = gk_ref[:, sl]
                rk = lax.rsqrt(jnp.mean(k * k, axis=-1, keepdims=True) + EPS)
                dk, dgk = _head_rms_bwd(k, rk, gkv, dkn_acc[e])
                dk_ref[:, sl] = dk.astype(dk_ref.dtype)
                dv_ref[:, sl] = dv_acc[e].astype(dv_ref.dtype)
                dgk_ref[:, sl] += dgk
                dcrow_ref[0, e] = dcrow_acc[e]

    W = 2 * hd
    vec = pl.BlockSpec((1, W), lambda b, h, i: (0, 0))
    qspec = pl.BlockSpec((tq, W), lambda b, h, i: (b * nq + i, h))
    kvout = pl.BlockSpec((S, W), lambda b, h, i: (b, h))
    colspec = pl.BlockSpec((1, 2, tq, 1), lambda b, h, i: (b, h, i, 0))
    rowspec = pl.BlockSpec((1, 2, 1, S), lambda b, h, i: (b, h, 0, 0))
    return pl.pallas_call(
        body, name="fox_bwd", grid=(B, FOX_PAIRS, nq),
        in_specs=[qspec,
                  pl.BlockSpec((S, W), lambda b, h, i: (b, FOX_PAIRS + h)),
                  pl.BlockSpec((S, W), lambda b, h, i: (b, 2 * FOX_PAIRS + h)),
                  colspec, rowspec, vec, vec, vec, qspec, qspec],
        out_specs=[qspec, kvout, kvout, colspec, rowspec, vec, vec, vec],
        out_shape=[jax.ShapeDtypeStruct((T, FOX_WIDTH), MXU_DTYPE), jax.ShapeDtypeStruct((T, FOX_WIDTH), MXU_DTYPE),
                   jax.ShapeDtypeStruct((T, FOX_WIDTH), MXU_DTYPE),
                   jax.ShapeDtypeStruct((B, FOX_HEADS, S, 1), f32), jax.ShapeDtypeStruct((B, FOX_HEADS, 1, S), f32),
                   jax.ShapeDtypeStruct((1, W), f32), jax.ShapeDtypeStruct((1, W), f32), jax.ShapeDtypeStruct((1, W), f32)],
        scratch_shapes=[pltpu.VMEM((2, S, hd), f32), pltpu.VMEM((2, S, hd), f32), pltpu.VMEM((2, 1, S), f32)],
        compiler_params=_cparams("arbitrary", "arbitrary", "arbitrary"),
    )(P, P, P, ccol, crow, gq, gk, go, o_raw, d_oab)


FOX_TQ = 512
FOX_TK = FOX_TQ
GROUP_PRECISION = lax.Precision.HIGH


def _head_mean(v):
    n = v.shape[1]
    r = lax.broadcasted_iota(jnp.int32, (n, n), 0) // FOX_HEAD_DIM
    c = lax.broadcasted_iota(jnp.int32, (n, n), 1) // FOX_HEAD_DIM
    ones = (r == c).astype(bf16)
    hi = v.astype(bf16)
    lo = (v - hi.astype(f32)).astype(bf16)
    return (_dot(hi, ones, NN) + _dot(lo, ones, NN)) * (1.0 / FOX_HEAD_DIM)


def fox_prep_fwd(P, gq, gk, *, tr=1024):
    T = P.shape[0]
    tr = min(tr, T)
    scale = FOX_HEAD_DIM ** -0.5

    def body(q_ref, k_ref, v_ref, gq_ref, gk_ref, qn_ref, kn_ref, vb_ref):
        q, k = q_ref[...], k_ref[...]
        qn_ref[...] = (q * lax.rsqrt(_head_mean(q * q) + EPS) * (gq_ref[...] * scale)).astype(qn_ref.dtype)
        kn_ref[...] = (k * lax.rsqrt(_head_mean(k * k) + EPS) * gk_ref[...]).astype(kn_ref.dtype)
        vb_ref[...] = v_ref[...].astype(vb_ref.dtype)

    W = FOX_WIDTH
    col = lambda j: pl.BlockSpec((tr, W), lambda i: (i, j))
    vec = pl.BlockSpec((1, W), lambda i: (0, 0))
    out = jax.ShapeDtypeStruct((T, W), MXU_DTYPE)
    return pl.pallas_call(
        body, name="fox_prep_fwd", grid=(T // tr,), in_specs=[col(0), col(1), col(2), vec, vec],
        out_specs=[col(0)] * 3, out_shape=[out] * 3, compiler_params=_cparams("parallel"),
    )(P, P, P, gq, gk)


def fox_prep_bwd(P, gq, gk, dqn, dkn, *, tr=1024):
    T = P.shape[0]
    tr = min(tr, T)
    scale = FOX_HEAD_DIM ** -0.5

    def body(q_ref, k_ref, gq_ref, gk_ref, dqn_ref, dkn_ref, dq_ref, dk_ref, dgq_ref, dgk_ref):
        @pl.when(pl.program_id(0) == 0)
        def _():
            dgq_ref[...] = jnp.zeros_like(dgq_ref)
            dgk_ref[...] = jnp.zeros_like(dgk_ref)

        def one(x, g, dn, dx_ref, dg_ref):
            r = lax.rsqrt(_head_mean(x * x) + EPS)
            xhat = x * r
            gd = dn * g
            dx_ref[...] = (r * (gd - xhat * _head_mean(gd * xhat))).astype(dx_ref.dtype)
            return jnp.sum(dn * xhat, axis=0, keepdims=True)

        dgq_ref[...] += scale * one(q_ref[...], gq_ref[...] * scale, dqn_ref[...], dq_ref, dgq_ref)
        dgk_ref[...] += one(k_ref[...], gk_ref[...], dkn_ref[...], dk_ref, dgk_ref)

    W = FOX_WIDTH
    col = lambda j: pl.BlockSpec((tr, W), lambda i: (i, j))
    vec = pl.BlockSpec((1, W), lambda i: (0, 0))
    return pl.pallas_call(
        body, name="fox_prep_bwd", grid=(T // tr,), in_specs=[col(0), col(1), vec, vec, col(0), col(0)],
        out_specs=[col(0), col(0), vec, vec],
        out_shape=[jax.ShapeDtypeStruct((T, W), MXU_DTYPE), jax.ShapeDtypeStruct((T, W), MXU_DTYPE),
                   jax.ShapeDtypeStruct((1, W), f32), jax.ShapeDtypeStruct((1, W), f32)],
        compiler_params=_cparams("arbitrary"),
    )(P, P, gq, gk, dqn, dkn)


def _fox_tile_scores(q, k_ref, ccol_ref, cq, e, j, sl, mask_off):
    tq, tk = FOX_TQ, FOX_TK
    rows = pl.ds(pl.multiple_of(j * tk, tk), tk)
    k = k_ref[rows, sl]
    s = _dot(k, q, NT) + cq - ccol_ref[0, e, rows, :]
    if mask_off is not None:
        key = lax.broadcasted_iota(jnp.int32, (tk, tq), 0) + mask_off
        query = lax.broadcasted_iota(jnp.int32, (tk, tq), 1)
        s = jnp.where(key <= query, s, NEG_INF)
    return s, k, rows


def _fox_sweep(i, update, carry):
    nd = FOX_TQ // FOX_TK
    carry = lax.fori_loop(0, i * nd, lambda j, cr: update(cr, j, None), carry)
    for d in range(nd):
        carry = update(carry, i * nd + d, d * FOX_TK)
    return carry


def fox_core_fwd(qn, kn, vb, ccol, crow, go, *, B):
    T = qn.shape[0]
    S = T // B
    tq = FOX_TQ
    nq = S // tq
    hd = FOX_HEAD_DIM

    def body(q_ref, k_ref, v_ref, ccol_ref, crow_ref, go_ref, o_ref, oa_ref, lse_ref):
        i = pl.program_id(2)
        for e in range(2):
            sl = slice(e * hd, (e + 1) * hd)
            q = q_ref[:, sl]
            cq = crow_ref[0, e, i]

            def update(carry, j, mask_off):
                m, l, acc = carry
                s, _, rows = _fox_tile_scores(q, k_ref, ccol_ref, cq, e, j, sl, mask_off)
                m2 = jnp.maximum(m, jnp.max(s, axis=0, keepdims=True))
                a = jnp.exp(m - m2)
                p = jnp.exp(s - m2)
                return m2, a * l + jnp.sum(p, axis=0, keepdims=True), a * acc + _dot(v_ref[rows, sl], _mx(p), TN)

            carry = (jnp.full((1, tq), NEG_INF, f32), jnp.zeros((1, tq), f32), jnp.zeros((hd, tq), f32))
            m, l, acc = _fox_sweep(i, update, carry)
            o = (acc / l).T
            o_ref[:, sl] = o
            oa_ref[:, sl] = _head_rms(o, go_ref[:, sl])[0].astype(oa_ref.dtype)
            lse_ref[0, e, 0] = m + jnp.log(l)

    W = 2 * hd
    qspec = pl.BlockSpec((tq, W), lambda b, h, i: (b * nq + i, h))
    kspec = pl.BlockSpec((S, W), lambda b, h, i: (b, h))
    return pl.pallas_call(
        body, name="fox_core_fwd", grid=(B, FOX_PAIRS, nq),
        in_specs=[qspec, kspec, kspec, pl.BlockSpec((1, 2, S, 1), lambda b, h, i: (b, h, 0, 0)),
                  pl.BlockSpec((1, 2, nq, 1, tq), lambda b, h, i: (b, h, 0, 0, 0)),
                  pl.BlockSpec((1, W), lambda b, h, i: (0, 0))],
        out_specs=[qspec, qspec, pl.BlockSpec((1, 2, 1, 1, tq), lambda b, h, i: (b, h, i, 0, 0))],
        out_shape=[jax.ShapeDtypeStruct((T, FOX_WIDTH), f32), jax.ShapeDtypeStruct((T, FOX_WIDTH), MXU_DTYPE),
                   jax.ShapeDtypeStruct((B, FOX_HEADS, nq, 1, tq), f32)],
        compiler_params=_cparams("parallel", "parallel", "parallel"),
    )(qn, kn, vb, ccol, crow, go)


def fox_core_bwd(qn, kn, vb, ccol, crow, go, o_raw, lse, d_oab, *, B):
    T = qn.shape[0]
    S = T // B
    tq = FOX_TQ
    nq = S // tq
    hd = FOX_HEAD_DIM

    def body(q_ref, k_ref, v_ref, ccol_ref, crow_ref, go_ref, o_ref, lse_ref, doa_ref,
             dq_ref, dk_ref, dv_ref, dckey_ref, dcrow_ref, dgo_ref, dk_acc, dv_acc, dck_acc):
        b, h, i = pl.program_id(0), pl.program_id(1), pl.program_id(2)

        @pl.when((b == 0) & (h == 0) & (i == 0))
        def _():
            dgo_ref[...] = jnp.zeros_like(dgo_ref)

        @pl.when(i == 0)
        def _():
            dk_acc[...] = jnp.zeros_like(dk_acc)
            dv_acc[...] = jnp.zeros_like(dv_acc)
            dck_acc[...] = jnp.zeros_like(dck_acc)

        for e in range(2):
            sl = slice(e * hd, (e + 1) * hd)
            q = q_ref[:, sl]
            cq = crow_ref[0, e, i]
            lse_e = lse_ref[0, e, 0]
            o = o_ref[:, sl]
            ro = lax.rsqrt(jnp.mean(o * o, axis=-1, keepdims=True) + EPS)
            do, dgo = _head_rms_bwd(o, ro, go_ref[:, sl], doa_ref[:, sl])
            dgo_ref[:, sl] += dgo
            delta = jnp.sum((do * o).T, axis=0, keepdims=True)
            do_b = _mx(do)

            def update(carry, j, mask_off):
                dq, dcq = carry
                s, k, rows = _fox_tile_scores(q, k_ref, ccol_ref, cq, e, j, sl, mask_off)
                p = jnp.exp(s - lse_e)
                dv_acc[e, rows, :] += _dot(_mx(p), do_b, NN)
                ds = p * (_dot(v_ref[rows, sl], do_b, NT) - delta)
                dck_acc[e, rows, :] -= jnp.sum(ds, axis=1, keepdims=True)
                ds_b = _mx(ds)
                dk_acc[e, rows, :] += _dot(ds_b, q, NN)
                return dq + _dot(ds_b, k, TN), dcq + jnp.sum(ds, axis=0, keepdims=True)

            dq, dcq = _fox_sweep(i, update, (jnp.zeros((tq, hd), f32), jnp.zeros((1, tq), f32)))
            dq_ref[:, sl] = dq
            dcrow_ref[0, e, 0] = dcq

        @pl.when(i == nq - 1)
        def _():
            for e in range(2):
                sl = slice(e * hd, (e + 1) * hd)
                dk_ref[:, sl] = dk_acc[e]
                dv_ref[:, sl] = dv_acc[e].astype(dv_ref.dtype)
                dckey_ref[0, e] = jnp.transpose(jnp.broadcast_to(dck_acc[e], (S, LANES)))[0:1, :]

    W = 2 * hd
    qspec = pl.BlockSpec((tq, W), lambda b, h, i: (b * nq + i, h))
    kspec = pl.BlockSpec((S, W), lambda b, h, i: (b, h))
    colspec = pl.BlockSpec((1, 2, S, 1), lambda b, h, i: (b, h, 0, 0))
    rowspec = pl.BlockSpec((1, 2, nq, 1, tq), lambda b, h, i: (b, h, 0, 0, 0))
    tilespec = pl.BlockSpec((1, 2, 1, 1, tq), lambda b, h, i: (b, h, i, 0, 0))
    vec = pl.BlockSpec((1, W), lambda b, h, i: (0, 0))
    return pl.pallas_call(
        body, name="fox_core_bwd", grid=(B, FOX_PAIRS, nq),
        in_specs=[qspec, kspec, kspec, colspec, rowspec, vec, qspec, tilespec, qspec],
        out_specs=[qspec, kspec, kspec, pl.BlockSpec((1, 2, 1, S), lambda b, h, i: (b, h, 0, 0)), tilespec, vec],
        out_shape=[jax.ShapeDtypeStruct((T, FOX_WIDTH), f32), jax.ShapeDtypeStruct((T, FOX_WIDTH), f32),
                   jax.ShapeDtypeStruct((T, FOX_WIDTH), MXU_DTYPE),
                   jax.ShapeDtypeStruct((B, FOX_HEADS, 1, S), f32), jax.ShapeDtypeStruct((B, FOX_HEADS, nq, 1, tq), f32),
                   jax.ShapeDtypeStruct((1, W), f32)],
        scratch_shapes=[pltpu.VMEM((2, S, hd), f32), pltpu.VMEM((2, S, hd), f32), pltpu.VMEM((2, S, 1), f32)],
        compiler_params=_cparams("arbitrary", "arbitrary", "arbitrary"),
    )(qn, kn, vb, ccol, crow, go, o_raw, lse, d_oab)


def _lane_mask(lo, hi, shape):
    lane = lax.broadcasted_iota(jnp.int32, shape, 1)
    return (lane >= lo) & (lane < hi)


def _cumsum_rows(v, period, reverse=False):
    n = v.shape[0]
    pos = lax.broadcasted_iota(jnp.int32, v.shape, 0) % period
    sh = 1
    while sh < period:
        if reverse:
            v = v + jnp.where(pos + sh < period, pltpu.roll(v, n - sh, 0), 0.0)
        else:
            v = v + jnp.where(pos >= sh, pltpu.roll(v, sh, 0), 0.0)
        sh *= 2
    return v


def _gate_values(z, bias, alog):
    zb = z + bias
    ls = jax.nn.log_sigmoid(zb)
    beta = jax.nn.sigmoid(z)
    g = -jnp.exp(alog) * jax.nn.softplus(zb)
    return zb, ls, beta, g


def gates_fwd(P, bias, alog, *, B):
    T = P.shape[0]
    S = T // B

    def body(z_ref, bias_ref, alog_ref, o_ref):
        z = z_ref[...]
        _, ls, beta, g = _gate_values(z, bias_ref[...], alog_ref[...])
        c = _cumsum_rows(ls, S)
        gc = _cumsum_rows(g, GDN_CHUNK)
        o = jnp.where(_lane_mask(SM_F, SM_F + FOX_HEADS, z.shape), c, 0.0)
        o = jnp.where(_lane_mask(SM_B, SM_B + GDN_HEADS, z.shape), beta, o)
        o = jnp.where(_lane_mask(SM_A, SM_A + GDN_HEADS, z.shape), gc, o)
        o_ref[...] = o

    vec = pl.BlockSpec((1, LANES), lambda b: (0, 0))
    return pl.pallas_call(
        body, name="gates_fwd", grid=(B,),
        in_specs=[pl.BlockSpec((S, LANES), lambda b: (b, COL_SMALL // LANES)), vec, vec],
        out_specs=pl.BlockSpec((S, LANES), lambda b: (b, 0)),
        out_shape=jax.ShapeDtypeStruct((T, LANES), f32),
        compiler_params=_cparams("parallel"),
    )(P, bias, alog)


def gates_bwd(P, bias, alog, dgates, *, B):
    T = P.shape[0]
    S = T // B

    def body(z_ref, bias_ref, alog_ref, dg_ref, dz_ref, par_ref):
        z = z_ref[...]
        zb, ls, beta, g = _gate_values(z, bias_ref[...], alog_ref[...])
        d = dg_ref[...]
        dls = _cumsum_rows(d, S, reverse=True)
        dgr = _cumsum_rows(d, GDN_CHUNK, reverse=True)
        sig = jax.nn.sigmoid(zb)
        dz_f = dls * (1.0 - sig)
        dz_b = d * beta * (1.0 - beta)
        dz_a = dgr * (-jnp.exp(alog_ref[...])) * sig
        dz = jnp.where(_lane_mask(SM_F, SM_F + FOX_HEADS, z.shape), dz_f, 0.0)
        dz = jnp.where(_lane_mask(SM_B, SM_B + GDN_HEADS, z.shape), dz_b, dz)
        dz = jnp.where(_lane_mask(SM_A, SM_A + GDN_HEADS, z.shape), dz_a, dz)
        dz_ref[...] = dz.astype(dz_ref.dtype)

        @pl.when(pl.program_id(0) == 0)
        def _():
            par_ref[...] = jnp.zeros_like(par_ref)

        dalog = jnp.where(_lane_mask(SM_A, SM_A + GDN_HEADS, z.shape), dgr * g, 0.0)
        par_ref[0:1, :] += jnp.sum(dz, axis=0, keepdims=True)
        par_ref[1:2, :] += jnp.sum(dalog, axis=0, keepdims=True)

    vec = pl.BlockSpec((1, LANES), lambda b: (0, 0))
    return pl.pallas_call(
        body, name="gates_bwd", grid=(B,),
        in_specs=[pl.BlockSpec((S, LANES), lambda b: (b, COL_SMALL // LANES)), vec, vec,
                  pl.BlockSpec((S, LANES), lambda b: (b, 0))],
        out_specs=[pl.BlockSpec((S, LANES), lambda b: (b, 0)), pl.BlockSpec((8, LANES), lambda b: (0, 0))],
        out_shape=[jax.ShapeDtypeStruct((T, LANES), MXU_DTYPE), jax.ShapeDtypeStruct((8, LANES), f32)],
        compiler_params=_cparams("arbitrary"),
    )(P, bias, alog, dgates)


GDN_BLOCKS = 3 * GDN_HEADS


def _shift_rows(v, d, reverse=False):
    if d == 0:
        return v
    n = v.shape[0]
    row = lax.broadcasted_iota(jnp.int32, v.shape, 0)
    if reverse:
        return jnp.where(row + d < n, pltpu.roll(v, n - d, 0), 0.0)
    return jnp.where(row >= d, pltpu.roll(v, d, 0), 0.0)


def _conv_silu(x, w):
    pre = sum(w[j:j + 1, :] * _shift_rows(x, CONV_WIDTH - 1 - j) for j in range(CONV_WIDTH))
    return pre, pre * jax.nn.sigmoid(pre)


def gdn_prep_fwd(P, conv_w, *, B):
    T = P.shape[0]
    S = T // B

    def body(x_ref, w_ref, o_ref):
        _, y = _conv_silu(x_ref[...], w_ref[...])
        yn = y * lax.rsqrt(jnp.sum(y * y, axis=-1, keepdims=True) + EPS)
        o_ref[...] = jnp.where(pl.program_id(1) < 2 * GDN_HEADS, yn, y)

    return pl.pallas_call(
        body, name="gdn_prep_fwd", grid=(B, GDN_BLOCKS),
        in_specs=[pl.BlockSpec((S, LANES), lambda b, j: (b, COL_GDN // LANES + j)),
                  pl.BlockSpec((CONV_WIDTH, LANES), lambda b, j: (0, j))],
        out_specs=pl.BlockSpec((S, LANES), lambda b, j: (b, j)),
        out_shape=jax.ShapeDtypeStruct((T, 3 * GDN_WIDTH), f32),
        compiler_params=_cparams("parallel", "parallel"),
    )(P, conv_w)


def gdn_prep_bwd(P, conv_w, dGq, dGk, dGv, *, B):
    T = P.shape[0]
    S = T // B
    H = GDN_HEADS

    def body(x_ref, w_ref, dq_ref, dk_ref, dv_ref, dx_ref, dw_ref):
        x, w = x_ref[...], w_ref[...]
        pre, y = _conv_silu(x, w)
        jb = pl.program_id(0)
        dn = jnp.where(jb < H, dq_ref[...], jnp.where(jb < 2 * H, dk_ref[...], dv_ref[...]))
        r = lax.rsqrt(jnp.sum(y * y, axis=-1, keepdims=True) + EPS)
        n = y * r
        dy_norm = r * (dn - n * jnp.sum(dn * n, axis=-1, keepdims=True))
        dy = jnp.where(pl.program_id(0) < 2 * GDN_HEADS, dy_norm, dn)
        sg = jax.nn.sigmoid(pre)
        dpre = dy * (sg * (1.0 + pre * (1.0 - sg)))
        dx = sum(w[j:j + 1, :] * _shift_rows(dpre, CONV_WIDTH - 1 - j, reverse=True) for j in range(CONV_WIDTH))
        dx_ref[...] = dx.astype(dx_ref.dtype)

        @pl.when(pl.program_id(1) == 0)
        def _():
            dw_ref[...] = jnp.zeros_like(dw_ref)

        for j in range(CONV_WIDTH):
            dw_ref[j:j + 1, :] += jnp.sum(dpre * _shift_rows(x, CONV_WIDTH - 1 - j), axis=0, keepdims=True)

    return pl.pallas_call(
        body, name="gdn_prep_bwd", grid=(GDN_BLOCKS, B),
        in_specs=[pl.BlockSpec((S, LANES), lambda j, b: (b, COL_GDN // LANES + j)),
                  pl.BlockSpec((CONV_WIDTH, LANES), lambda j, b: (0, j))]
        + [pl.BlockSpec((S, LANES), lambda j, b, t=t: (jnp.where(j // H == t, b, 0), jnp.clip(j - t * H, 0, H - 1)))
           for t in range(3)],
        out_specs=[pl.BlockSpec((S, LANES), lambda j, b: (b, j)),
                   pl.BlockSpec((CONV_WIDTH, LANES), lambda j, b: (0, j))],
        out_shape=[jax.ShapeDtypeStruct((T, 3 * GDN_WIDTH), MXU_DTYPE),
                   jax.ShapeDtypeStruct((CONV_WIDTH, 3 * GDN_WIDTH), f32)],
        compiler_params=_cparams("arbitrary", "arbitrary"),
    )(P, conv_w, dGq, dGk, dGv)


GDN_GROUP = 16
GDN_GROUP_FWD = 16
B_NN = (((2,), (1,)), ((0,), (0,)))
B_NT = (((2,), (2,)), ((0,), (0,)))
B_TN = (((1,), (1,)), ((0,), (0,)))


def _bmm(a, b, dims, precision=None):
    if precision is None:
        a, b = _mx(a), _mx(b)
    return lax.dot_general(a, b, dims, preferred_element_type=f32, precision=precision)


def _tri_inverse(A):
    C = A.shape[-1]
    row = lax.broadcasted_iota(jnp.int32, A.shape, 1)
    col = lax.broadcasted_iota(jnp.int32, A.shape, 2)
    eye = (row == col).astype(f32)
    X = jnp.where((row // 4) == (col // 4), -A, 0.0)
    X2 = _bmm(X, X, B_NN, INV_PRECISION)
    Tm = eye + X + X2 + _bmm(X, X2, B_NN, INV_PRECISION)
    b = 4
    while b < C:
        off = ((row // (2 * b)) == (col // (2 * b))) & ((row // b) != (col // b))
        Tm = Tm - _bmm(_bmm(Tm, jnp.where(off, A, 0.0), B_NN, INV_PRECISION), Tm, B_NN, INV_PRECISION)
        b *= 2
    return Tm


def _pick_lane(block, lane_idx):
    lane = lax.broadcasted_iota(jnp.int32, block.shape, 1)
    return jnp.sum(jnp.where(lane == lane_idx, block, 0.0), axis=1, keepdims=True)


def _gdn_local(q, k, v, beta, gc, Tm=None, uwm=None):
    C = GDN_CHUNK
    n = q.shape[0] // C
    q = q.reshape(n, C, -1) * (GDN_HEAD_DIM ** -0.5)
    k = k.reshape(n, C, -1)
    v = v.reshape(n, C, -1)
    beta = beta.reshape(n, C, 1)
    gc = gc.reshape(n, C, 1)
    row = lax.broadcasted_iota(jnp.int32, (n, C, C), 1)
    col = lax.broadcasted_iota(jnp.int32, (n, C, C), 2)
    gcT = jnp.swapaxes(jnp.broadcast_to(gc, (n, C, C)), 1, 2)
    D = jnp.exp(jnp.where(row >= col, gc - gcT, NEG_INF))
    kb = k * beta
    vb = v * beta
    A = jnp.where(row > col, _bmm(kb, k, B_NT) * D, 0.0)
    Gam = jnp.exp(gc)
    kg = kb * Gam
    gl = gc[:, C - 1:C, :]
    kdec = jnp.exp(gl - gc)
    loc = dict(q=q, k=k, v=v, beta=beta, gc=gc, D=D, kb=kb, vb=vb, A=A, Gam=Gam, kg=kg,
               kdec=kdec, kd=k * kdec, qg=q * Gam, gam=jnp.exp(gl), row=row, col=col)
    uwm = Tm is None if uwm is None else uwm
    Tm = _tri_inverse(A) if Tm is None else Tm.reshape(n, C, C)
    if uwm:
        loc.update(u=_bmm(Tm, vb, B_NN), w=_bmm(Tm, kg, B_NN), M=_bmm(q, k, B_NT) * D)
    loc["Tm"] = Tm
    return loc


def _gdn_store_local(loc, r0, u_s, w_s, qg_s, kd_s, M_s, gam_s, c0):
    n = loc["u"].shape[0]
    R = n * GDN_CHUNK
    u_s[pl.ds(r0, R), :] = loc["u"].reshape(R, -1)
    w_s[pl.ds(r0, R), :] = loc["w"].reshape(R, -1)
    qg_s[pl.ds(r0, R), :] = loc["qg"].reshape(R, -1)
    kd_s[pl.ds(r0, R), :] = loc["kd"].reshape(R, -1)
    M_s[pl.ds(r0, R), :] = loc["M"].reshape(R, -1)
    gam_s[pl.ds(c0, n)] = jnp.broadcast_to(loc["gam"], (n, 1, LANES))


def _gdn_specs(S):
    blk = lambda off: pl.BlockSpec((S, LANES), lambda b, h: (b, off + h))
    return blk


def gdn_fwd(G, gates, P, g_on, *, B):
    T = G.shape[0]
    S = T // B
    C = GDN_CHUNK
    N = S // C
    grp = min(GDN_GROUP_FWD, N)
    R = grp * C
    hd = GDN_HEAD_DIM

    def body(q_ref, k_ref, v_ref, gt_ref, z_ref, gon_ref, o_ref, ob_ref, st_ref, tm_ref, A_s, B_s, Q_s, O_s, gam_s):
        h = pl.program_id(1)

        def local(gi, carry):
            r0 = pl.multiple_of(gi * R, R)
            gt = gt_ref[pl.ds(r0, R), :]
            loc = _gdn_local(q_ref[pl.ds(r0, R), :], k_ref[pl.ds(r0, R), :], v_ref[pl.ds(r0, R), :],
                             _pick_lane(gt, SM_B + h), _pick_lane(gt, SM_A + h))
            chunks = pl.ds(gi * grp, grp)
            tm_ref[0, 0, pl.ds(r0, R), :] = loc["Tm"].reshape(R, C)
            A_s[chunks] = -_bmm(loc["kd"], loc["w"], B_TN)
            B_s[chunks] = _bmm(loc["kd"], loc["u"], B_TN)
            Q_s[pl.ds(r0, R), :] = (loc["qg"] - _bmm(loc["M"], loc["w"], B_NN)).reshape(R, hd)
            O_s[pl.ds(r0, R), :] = _bmm(loc["M"], loc["u"], B_NN).reshape(R, hd)
            gam_s[chunks] = jnp.broadcast_to(loc["gam"], (grp, 1, LANES))
            return carry

        lax.fori_loop(0, N // grp, local, 0)

        def step(n, state):
            st_ref[0, 0, n] = state
            return state * gam_s[n] + _dotm(A_s[n], state, NN) + B_s[n]

        lax.fori_loop(0, N, step, jnp.zeros((hd, hd), f32))

        def outputs(gi, carry):
            r0 = pl.multiple_of(gi * R, R)
            Q = Q_s[pl.ds(r0, R), :].reshape(grp, C, hd)
            o = _bmm(Q, st_ref[0, 0, pl.ds(gi * grp, grp)], B_NN).reshape(R, hd) + O_s[pl.ds(r0, R), :]
            o_ref[pl.ds(r0, R), :] = o
            return carry

        lax.fori_loop(0, N // grp, outputs, 0)
        o = o_ref[...]
        z = z_ref[...]
        ob_ref[...] = (_head_rms(o, gon_ref[...])[0] * (z * jax.nn.sigmoid(z))).astype(ob_ref.dtype)

    blk = lambda off: pl.BlockSpec((S, LANES), lambda b, h: (b, off + h))
    rows = lambda: pltpu.VMEM((S, hd), f32)
    return pl.pallas_call(
        body, name="gdn_fwd", grid=(B, GDN_HEADS),
        in_specs=[blk(0), blk(GDN_HEADS), blk(2 * GDN_HEADS), pl.BlockSpec((S, LANES), lambda b, h: (b, 0)),
                  blk(COL_Z // LANES), pl.BlockSpec((1, hd), lambda b, h: (0, 0))],
        out_specs=[blk(0), blk(0), pl.BlockSpec((1, 1, N, hd, hd), lambda b, h: (b, h, 0, 0, 0)),
                   pl.BlockSpec((1, 1, S, C), lambda b, h: (b, h, 0, 0))],
        out_shape=[jax.ShapeDtypeStruct((T, GDN_WIDTH), f32), jax.ShapeDtypeStruct((T, GDN_WIDTH), MXU_DTYPE),
                   jax.ShapeDtypeStruct((B, GDN_HEADS, N, hd, hd), f32), jax.ShapeDtypeStruct((B, GDN_HEADS, S, C), f32)],
        scratch_shapes=[pltpu.VMEM((N, hd, hd), f32), pltpu.VMEM((N, hd, hd), f32), rows(), rows(),
                        pltpu.VMEM((N, 1, LANES), f32)],
        compiler_params=_cparams("parallel", "parallel"),
    )(G, G, G, gates, P, g_on)


def gdn_bwd(G, gates, P, g_on, o_raw, states, tm, d_oab, *, B):
    T = G.shape[0]
    S = T // B
    C = GDN_CHUNK
    N = S // C
    grp = min(GDN_GROUP, N)
    R = grp * C
    hd = GDN_HEAD_DIM

    def body(q_ref, k_ref, v_ref, gt_ref, z_ref, gon_ref, o_ref, st_ref, tm_ref, dob_ref,
             dq_ref, dk_ref, dv_ref, dgt_ref, dz_ref, dgon_ref,
             u_s, w_s, M_s, gam_s, do_s, A_s, C_s, dst_s):
        b, h = pl.program_id(0), pl.program_id(1)

        @pl.when((b == 0) & (h == 0))
        def _():
            dgon_ref[...] = jnp.zeros_like(dgon_ref)

        @pl.when(h == 0)
        def _():
            dgt_ref[...] = jnp.zeros_like(dgt_ref)

        def group_inputs(gi, uwm):
            r0 = pl.multiple_of(gi * R, R)
            gt = gt_ref[pl.ds(r0, R), :]
            return r0, _gdn_local(q_ref[pl.ds(r0, R), :], k_ref[pl.ds(r0, R), :], v_ref[pl.ds(r0, R), :],
                                  _pick_lane(gt, SM_B + h), _pick_lane(gt, SM_A + h), tm_ref[0, 0, pl.ds(r0, R), :], uwm)

        def local(gi, carry):
            r0, loc = group_inputs(gi, True)
            rows, chunks = pl.ds(r0, R), pl.ds(gi * grp, grp)
            u_s[rows, :] = loc["u"].reshape(R, hd)
            w_s[rows, :] = loc["w"].reshape(R, hd)
            M_s[rows, :] = loc["M"].reshape(R, C)
            gam_s[chunks] = jnp.broadcast_to(loc["gam"], (grp, 1, LANES))
            o, z, gon = o_ref[rows, :], z_ref[rows, :], gon_ref[...]
            dob = dob_ref[rows, :]
            on, ro = _head_rms(o, gon)
            sz = jax.nn.sigmoid(z)
            dz_ref[rows, :] = (dob * on * (sz * (1.0 + z * (1.0 - sz)))).astype(dz_ref.dtype)
            do, dgon = _head_rms_bwd(o, ro, gon, dob * (z * sz))
            do_s[rows, :] = do
            dgon_ref[...] += dgon
            A_s[chunks] = -_bmm(loc["kd"], loc["w"], B_TN)
            C_s[chunks] = _bmm(loc["qg"] - _bmm(loc["M"], loc["w"], B_NN), do.reshape(grp, C, hd), B_TN)
            return carry

        lax.fori_loop(0, N // grp, local, 0)

        def step(t, dS):
            n = N - 1 - t
            dst_s[n] = dS
            return dS * gam_s[n] + _dotm(A_s[n], dS, TN) + C_s[n]

        lax.fori_loop(0, N, step, jnp.zeros((hd, hd), f32))

        def finish(gi, carry):
            r0, L = group_inputs(gi, False)
            n = grp
            rows, chunks = pl.ds(r0, R), pl.ds(gi * grp, grp)
            g3 = lambda ref: ref[rows, :].reshape(n, C, -1)
            u, w, do = g3(u_s), g3(w_s), g3(do_s)
            L["M"] = g3(M_s)
            state, dS = st_ref[0, 0, chunks], dst_s[chunks]
            v_new = u - _bmm(w, state, B_NN)
            du = _bmm(L["M"], do, B_TN) + _bmm(L["kd"], dS, B_NN)
            dw = -_bmm(du, state, B_NT)
            dqg = _bmm(do, state, B_NT)
            dM = _bmm(do, v_new, B_NT)
            dkd = _bmm(v_new, dS, B_NT)
            dgl_state = jnp.sum(jnp.sum(dS * state, axis=2, keepdims=True), axis=1, keepdims=True) * L["gam"]
            TmT = jnp.swapaxes(L["Tm"], 1, 2)
            dTm = _bmm(du, L["vb"], B_NT) + _bmm(dw, L["kg"], B_NT)
            dvb = _bmm(TmT, du, B_NN)
            dkg = _bmm(TmT, dw, B_NN)
            dA = jnp.where(L["row"] > L["col"], -_bmm(_bmm(TmT, dTm, B_NN), TmT, B_NN), 0.0)
            dKK = dA * L["D"]
            dQK = dM * L["D"]
            dkb = _bmm(dKK, L["k"], B_NN) + dkg * L["Gam"]
            dk = (_bmm(dKK, L["kb"], B_TN) + _bmm(dQK, L["q"], B_TN) + dkd * L["kdec"] + L["beta"] * dkb)
            dq = (_bmm(dQK, L["k"], B_NN) + dqg * L["Gam"]) * (GDN_HEAD_DIM ** -0.5)
            E = dA * L["A"] + dM * L["M"]
            r = jnp.sum(dkd * L["kd"], axis=-1, keepdims=True)
            dgc = (jnp.sum(E, axis=2, keepdims=True) - jnp.sum(jnp.swapaxes(E, 1, 2), axis=2, keepdims=True)
                   + jnp.sum(dkg * L["kg"], axis=-1, keepdims=True) + jnp.sum(dqg * L["qg"], axis=-1, keepdims=True) - r)
            dgl = jnp.sum(r, axis=1, keepdims=True) + dgl_state
            rowc = lax.broadcasted_iota(jnp.int32, (n, C, 1), 1)
            dgc = dgc + jnp.where(rowc == C - 1, dgl, 0.0)
            dbeta = jnp.sum(dkb * L["k"], axis=-1, keepdims=True) + jnp.sum(dvb * L["v"], axis=-1, keepdims=True)
            dq_ref[rows, :] = dq.reshape(R, hd)
            dk_ref[rows, :] = dk.reshape(R, hd)
            dv_ref[rows, :] = (L["beta"] * dvb).reshape(R, hd)
            lane = lax.broadcasted_iota(jnp.int32, (R, LANES), 1)
            dgt_ref[rows, :] += (jnp.where(lane == SM_B + h, dbeta.reshape(R, 1), 0.0)
                                 + jnp.where(lane == SM_A + h, dgc.reshape(R, 1), 0.0))
            return carry

        lax.fori_loop(0, N // grp, finish, 0)

    blk = lambda off: pl.BlockSpec((S, LANES), lambda b, h: (b, off + h))
    rows = lambda: pltpu.VMEM((S, hd), f32)
    return pl.pallas_call(
        body, name="gdn_bwd", grid=(B, GDN_HEADS),
        in_specs=[blk(0), blk(GDN_HEADS), blk(2 * GDN_HEADS), pl.BlockSpec((S, LANES), lambda b, h: (b, 0)),
                  blk(COL_Z // LANES), pl.BlockSpec((1, hd), lambda b, h: (0, 0)), blk(0),
                  pl.BlockSpec((1, 1, N, hd, hd), lambda b, h: (b, h, 0, 0, 0)),
                  pl.BlockSpec((1, 1, S, C), lambda b, h: (b, h, 0, 0)), blk(GDN_HEADS)],
        out_specs=[blk(0), blk(0), blk(0), pl.BlockSpec((S, LANES), lambda b, h: (b, 0)), blk(0),
                   pl.BlockSpec((1, hd), lambda b, h: (0, 0))],
        out_shape=[jax.ShapeDtypeStruct((T, GDN_WIDTH), f32), jax.ShapeDtypeStruct((T, GDN_WIDTH), f32),
                   jax.ShapeDtypeStruct((T, GDN_WIDTH), f32), jax.ShapeDtypeStruct((T, LANES), f32),
                   jax.ShapeDtypeStruct((T, GDN_WIDTH), MXU_DTYPE), jax.ShapeDtypeStruct((1, hd), f32)],
        scratch_shapes=[rows(), rows(), pltpu.VMEM((S, C), f32), pltpu.VMEM((N, 1, LANES), f32), rows(),
                        pltpu.VMEM((N, hd, hd), f32), pltpu.VMEM((N, hd, hd), f32), pltpu.VMEM((N, hd, hd), f32)],
        compiler_params=_cparams("arbitrary", "arbitrary"),
    )(G, G, G, gates, P, g_on, o_raw, states, tm, d_oab)


IN_SPLIT = (0, 1536, 1544, 3080, 3088, 3600)


IN_SHARD = IN_DIM // 4
IN_SHARD_PAD = 928


def align_w_in_t(wt):
    s = IN_SPLIT
    pad = jnp.zeros((IN_ALIGNED - IN_DIM, wt.shape[1]), wt.dtype)
    return jnp.concatenate([wt[s[0]:s[1]], wt[s[2]:s[3]], wt[s[4]:s[5]], wt[s[1]:s[2]], wt[s[3]:s[4]], pad], axis=0)


def unalign_w_in_t(wa):
    return jnp.concatenate([wa[0:1536], wa[COL_SMALL:COL_SMALL + 8], wa[1536:3072],
                            wa[COL_SMALL + 8:COL_SMALL + 16], wa[3072:3584]], axis=0)


IN_SEGMENTS = (((0, 1536), 0), ((1536, 1544), COL_SMALL), ((1544, 3080), 1536), ((3080, 3088), COL_SMALL + 8),
               ((3088, 3600), 3072))


def align_w_in_slots(slots):
    pieces = []
    for (lo, hi), _ in sorted(IN_SEGMENTS, key=lambda seg: seg[1]):
        for k in range(N_CHIPS):
            a, b = max(lo, k * IN_SHARD), min(hi, (k + 1) * IN_SHARD)
            if a < b:
                pieces.append(slots[k, a - k * IN_SHARD:b - k * IN_SHARD])
    pieces.append(jnp.zeros((IN_ALIGNED - IN_DIM, slots.shape[2]), slots.dtype))
    return jnp.concatenate(pieces, axis=0)


def unalign_to_slots(wa):
    slots = []
    for k in range(N_CHIPS):
        lo, hi = k * IN_SHARD, (k + 1) * IN_SHARD
        pieces = []
        for (a, b), first in IN_SEGMENTS:
            x, y = max(a, lo), min(b, hi)
            if x < y:
                pieces.append(wa[first + x - a:first + y - a])
        pieces.append(jnp.zeros((IN_SHARD_PAD - IN_SHARD, wa.shape[1]), wa.dtype))
        slots.append(jnp.concatenate(pieces, axis=0))
    return jnp.stack(slots)


def _lanes_vec(pieces):
    v = jnp.zeros((1, LANES), f32)
    for off, a in pieces:
        v = lax.dynamic_update_slice(v, a.astype(f32), (0, off))
    return v


def local_step(x, mem, target, w, sp, *, B):
    T = x.shape[0]
    S = T // B
    gq8, gk8 = jnp.tile(sp["fox_qnorm_g"], (1, FOX_HEADS)), jnp.tile(sp["fox_knorm_g"], (1, FOX_HEADS))
    go2 = jnp.tile(sp["fox_onorm_g"], (1, 2))
    bias = _lanes_vec([(SM_F, sp["fox_f_bias"]), (SM_A, sp["gdn_dt_bias"])])
    alog = _lanes_vec([(SM_A, sp["gdn_A_log"])])

    h1 = rms_fwd(x, sp["norm_mix_g"], name="rms_mix")
    P = matmul(h1, w["wa_t"], tb=True, name="mm_in", tn=IN_TILE)
    gates = gates_fwd(P, bias, alog, B=B)
    c = gates[:, SM_F:SM_F + FOX_HEADS].reshape(B, S, FOX_HEADS).transpose(0, 2, 1)
    ccol, crow = c[..., None], c.reshape(B, FOX_HEADS, S // FOX_TQ, 1, FOX_TQ)
    qn, kn, vb = fox_prep_fwd(P, gq8, gk8)
    o_raw, o_a, lse = fox_core_fwd(qn, kn, vb, ccol, crow, go2, B=B)
    G = gdn_prep_fwd(P, w["conv_w"], B=B)
    ob_raw, o_b, states, gdn_tm = gdn_fwd(G, gates, P, sp["gdn_onorm_g"], B=B)
    oab = jnp.concatenate([o_a, o_b], axis=1)
    if "late" in w:
        w = {**w, **w["late"](oab)}
    x2, hq = matmul_rows(oab, w["w_out"], (x, sp["norm_xattn_g"]), mode="rms_fwd", name="mm_out_rms")
    hm = rms_fwd(mem, sp["mem_norm_g"], name="rms_mem")
    cq = matmul(hq, w["w_cq"], name="mm_cq")
    ckv = matmul(hm, w["w_ckv"], name="mm_ckv")
    co = xattn_fwd(cq, ckv, sp["xattn_qnorm_g"], sp["xattn_knorm_g"], B=B)
    x3, hf = matmul_rows(co, w["w_co"], (x2, sp["norm_mlp_g"]), mode="rms_fwd", name="mm_co_rms")
    act = matmul(hf, w["w_mlp1"], b_stacked=True, relu2_out=True, out_dtype=MXU_DTYPE, name="mm_mlp1")
    dy, dy_op, loss = matmul_rows(act, w["w_mlp2"], (x3, target), mode="loss", name="mm_mlp2_loss")

    da = matmul(dy_op, w["w_mlp2"], tb=True, relu2_bwd_aux=act, out_dtype=MXU_DTYPE, name="mm_d_act")
    g_mlp2 = matmul(act, dy_op, ta=True, out_dtype=WIRE_DTYPE, name="mm_g_mlp2")
    g_mlp1 = matmul(hf, da, ta=True, out_stacked=True, out_dtype=WIRE_DTYPE, name="mm_g_mlp1")
    by_rows = lambda g: g.reshape(N_CHIPS, g.shape[0] // N_CHIPS, g.shape[1])
    early = w.get("grads_ready", lambda grads: jnp.zeros((1, 1), f32))
    tok = early(dict(w_mlp1=g_mlp1, w_mlp2=by_rows(g_mlp2)))[0, 0]
    dx3, g_norm_mlp = matmul_rows(da, w["w_mlp1"], (x3, sp["norm_mlp_g"] + tok, dy), mode="rms_bwd", tb=True,
                                  b_stacked=True, name="mm_d_hf_rms")
    dco = matmul(dx3, w["w_co"], tb=True, name="mm_d_co")
    g_co = matmul(co, dx3, ta=True, out_dtype=WIRE_DTYPE, name="mm_g_co")
    g_co = g_co.reshape(XATTN_WIDTH, N_CHIPS, D_MODEL // N_CHIPS).transpose(1, 0, 2)
    dcq, dckv, g_xq, g_xk = xattn_bwd(cq, ckv, sp["xattn_qnorm_g"], sp["xattn_knorm_g"], dco, B=B)
    g_cq = matmul(hq, dcq, ta=True, out_dtype=WIRE_DTYPE, name="mm_g_cq")
    g_ckv = matmul(hm, dckv, ta=True, out_dtype=WIRE_DTYPE, name="mm_g_ckv")
    _, g_mem_norm = matmul_rows(dckv, w["w_ckv"], (mem, sp["mem_norm_g"], None), mode="rms_bwd", tb=True, name="mm_d_hm_rms")
    dx2, g_norm_xattn = matmul_rows(dcq, w["w_cq"], (x2, sp["norm_xattn_g"], dx3), mode="rms_bwd", tb=True, name="mm_d_hq_rms")
    doab = matmul(dx2, w["w_out"], tb=True, name="mm_d_oab")
    g_out = matmul(oab, dx2, ta=True, out_dtype=WIRE_DTYPE, name="mm_g_out")
    tok = early(dict(w_co=g_co, w_cq=by_rows(g_cq), w_ckv=by_rows(g_ckv), w_out=by_rows(g_out)))[0, 0]
    dqn, dkn, dv_f, dckey, dcrow, dgo2 = fox_core_bwd(qn, kn, vb, ccol, crow, go2 + tok, o_raw, lse, doab, B=B)
    dq_f, dk_f, dgq8, dgk8 = fox_prep_bwd(P, gq8, gk8, dqn, dkn)
    dGq, dGk, dGv, dgt, dz, g_gdn_on = gdn_bwd(G, gates, P, sp["gdn_onorm_g"], ob_raw, states, gdn_tm, doab, B=B)
    dPg, g_conv = gdn_prep_bwd(P, w["conv_w"], dGq, dGk, dGv, B=B)
    dc = (dckey[:, :, 0, :] + dcrow.reshape(B, FOX_HEADS, S)).transpose(0, 2, 1).reshape(T, FOX_HEADS)
    dgates = dgt + jnp.pad(dc, ((0, 0), (SM_F, LANES - SM_F - FOX_HEADS)))
    dsmall, par = gates_bwd(P, bias, alog, dgates, B=B)
    dP = jnp.concatenate([dq_f, dk_f, dv_f, dPg, dz, dsmall, jnp.zeros((T, IN_ALIGNED - COL_SMALL - LANES), MXU_DTYPE)], axis=1)
    g_wa = matmul(dP, h1, ta=True, out_dtype=WIRE_DTYPE, name="mm_g_in", tm=IN_TILE)
    g_in = unalign_to_slots(g_wa)
    tok = early(dict(w_in=g_in))[0, 0]
    dx, g_norm_mix = matmul_rows(dP, w["wa_t"], (x, sp["norm_mix_g"] + tok, dx2), mode="rms_bwd", tk=IN_TILE,
                                 name="mm_d_h1_rms")

    fold = lambda g: jnp.sum(g.reshape(-1, FOX_HEAD_DIM), axis=0, keepdims=True)
    big = dict(w_in=g_in, w_out=by_rows(g_out), w_cq=by_rows(g_cq), w_ckv=by_rows(g_ckv), w_co=g_co, w_mlp1=g_mlp1,
               w_mlp2=by_rows(g_mlp2))
    small = dict(norm_mix_g=g_norm_mix, fox_qnorm_g=fold(dgq8), fox_knorm_g=fold(dgk8),
                 fox_f_bias=par[0:1, SM_F:SM_F + FOX_HEADS], fox_onorm_g=fold(dgo2), gdn_conv_w=g_conv,
                 gdn_A_log=par[1:2, SM_A:SM_A + GDN_HEADS], gdn_dt_bias=par[0:1, SM_A:SM_A + GDN_HEADS],
                 gdn_onorm_g=g_gdn_on, norm_xattn_g=g_norm_xattn, mem_norm_g=g_mem_norm,
                 xattn_qnorm_g=g_xq, xattn_knorm_g=g_xk, norm_mlp_g=g_norm_mlp)
    return loss, dx, big, small


MESH_IDS = pl.DeviceIdType.MESH
N_CHIPS = 4
HBM_SPEC = pl.BlockSpec(memory_space=pltpu.HBM)
PACK_ROWS = 30720
PACK_HALF = PACK_ROWS // 2
PACK_BLOCK = 3072


def _place():
    return lax.axis_index("x"), lax.axis_index("y"), lax.axis_index("c")


def _other_chips(x, y):
    return [(1 - x, y), (x, 1 - y), (1 - x, 1 - y)]


def _remote(src, dst, send_sem, recv_sem, to):
    return pltpu.make_async_remote_copy(src_ref=src, dst_ref=dst, send_sem=send_sem, recv_sem=recv_sem,
                                        device_id=to, device_id_type=MESH_IDS)


def all_gather_shards(packed):
    half = PACK_HALF

    def body(src_ref, out_ref, send_sems, recv_sems):
        x, y, c = _place()
        me_chip = 2 * x + y
        sibling = (x, y, 1 - c)
        chips = _other_chips(x, y)

        def rows(chip, core):
            return out_ref.at[chip, pl.ds(core * half, half), :]

        sends = [_remote(src_ref.at[pl.ds(c * half, half), :], rows(me_chip, c), send_sems.at[j], recv_sems.at[j], (px, py, c))
                 for j, (px, py) in enumerate(chips)]
        for cp in sends:
            cp.start()
        passed = []
        for j, (px, py) in enumerate(chips):
            theirs = rows(2 * px + py, c)
            _remote(theirs, theirs, send_sems.at[j], recv_sems.at[j], (px, py, c)).wait_recv()
            cp = _remote(theirs, theirs, send_sems.at[3 + j], recv_sems.at[3 + j], sibling)
            cp.start()
            passed.append(cp)
        for j, (px, py) in enumerate(chips):
            theirs = rows(2 * px + py, 1 - c)
            _remote(theirs, theirs, send_sems.at[3 + j], recv_sems.at[3 + j], sibling).wait_recv()
        for cp in sends + passed:
            cp.wait_send()

    return pl.pallas_call(
        body, name="all_gather_shards", in_specs=[HBM_SPEC], out_specs=HBM_SPEC,
        out_shape=jax.ShapeDtypeStruct((N_CHIPS,) + packed.shape, packed.dtype),
        scratch_shapes=[pltpu.SemaphoreType.DMA((6,)), pltpu.SemaphoreType.DMA((6,))],
    )(packed)


def exchange_core_halves(G):
    half = PACK_HALF

    def body(g_ref, land_ref, send_sem, recv_sem):
        x, y, c = _place()
        cp = _remote(g_ref.at[:, pl.ds((1 - c) * half, half), :], land_ref, send_sem, recv_sem, (x, y, 1 - c))
        cp.start()
        cp.wait()

    return pl.pallas_call(
        body, name="exchange_core_halves", in_specs=[HBM_SPEC], out_specs=HBM_SPEC,
        out_shape=jax.ShapeDtypeStruct((N_CHIPS, half, LANES), G.dtype),
        scratch_shapes=[pltpu.SemaphoreType.DMA(()), pltpu.SemaphoreType.DMA(())],
    )(G)


def add_core_halves(G, land, core):
    nb = PACK_HALF // PACK_BLOCK

    def body(c_ref, g_ref, l_ref, o_ref):
        o_ref[...] = (g_ref[...].astype(f32) + l_ref[...].astype(f32)).astype(o_ref.dtype)

    blk = (1, PACK_BLOCK, LANES)
    return pl.pallas_call(
        body, name="add_core_halves",
        grid_spec=pltpu.PrefetchScalarGridSpec(
            num_scalar_prefetch=1, grid=(N_CHIPS, nb),
            in_specs=[pl.BlockSpec(blk, lambda k, i, c_ref: (k, c_ref[0] * nb + i, 0)),
                      pl.BlockSpec(blk, lambda k, i, c_ref: (k, i, 0))],
            out_specs=pl.BlockSpec(blk, lambda k, i, c_ref: (k, i, 0))),
        out_shape=jax.ShapeDtypeStruct(land.shape, land.dtype),
        compiler_params=_cparams("parallel", "parallel"),
    )(core, G, land)


def scatter_to_chips(part):
    def body(p_ref, land_ref, send_sems, recv_sems):
        x, y, c = _place()
        me_chip = 2 * x + y
        chips = _other_chips(x, y)
        sends = [_remote(p_ref.at[2 * px + py], land_ref.at[me_chip], send_sems.at[j], recv_sems.at[j], (px, py, c))
                 for j, (px, py) in enumerate(chips)]
        for cp in sends:
            cp.start()
        for j, (px, py) in enumerate(chips):
            slot = land_ref.at[2 * px + py]
            _remote(slot, slot, send_sems.at[j], recv_sems.at[j], (px, py, c)).wait_recv()
        for cp in sends:
            cp.wait_send()

    return pl.pallas_call(
        body, name="scatter_to_chips", in_specs=[HBM_SPEC], out_specs=HBM_SPEC,
        out_shape=jax.ShapeDtypeStruct(part.shape, part.dtype),
        scratch_shapes=[pltpu.SemaphoreType.DMA((3,)), pltpu.SemaphoreType.DMA((3,))],
    )(part)


def sum_chips(part, land, order):
    nb = PACK_HALF // PACK_BLOCK

    def body(order_ref, p_ref, l1_ref, l2_ref, l3_ref, o_ref):
        o_ref[...] = ((p_ref[0].astype(f32) + l1_ref[0].astype(f32)) + l2_ref[0].astype(f32)) + l3_ref[0].astype(f32)

    slot = lambda j: pl.BlockSpec((1, PACK_BLOCK, LANES), lambda i, order_ref: (order_ref[j], i, 0))
    return pl.pallas_call(
        body, name="sum_chips",
        grid_spec=pltpu.PrefetchScalarGridSpec(
            num_scalar_prefetch=1, grid=(nb,), in_specs=[slot(0), slot(1), slot(2), slot(3)],
            out_specs=pl.BlockSpec((PACK_BLOCK, LANES), lambda i, order_ref: (i, 0))),
        out_shape=jax.ShapeDtypeStruct((PACK_HALF, LANES), f32),
        compiler_params=_cparams("parallel"),
    )(order, part, land, land, land)


def swap_core_halves(red):
    def body(r_ref, out_ref, send_sem, recv_sem):
        x, y, c = _place()
        cp = _remote(r_ref, out_ref, send_sem, recv_sem, (x, y, 1 - c))
        cp.start()
        cp.wait()

    return pl.pallas_call(
        body, name="swap_core_halves", in_specs=[HBM_SPEC], out_specs=HBM_SPEC,
        out_shape=jax.ShapeDtypeStruct(red.shape, red.dtype),
        scratch_shapes=[pltpu.SemaphoreType.DMA(()), pltpu.SemaphoreType.DMA(())],
    )(red)


def _half(ref, core):
    rows = ref.shape[-2] // 2
    return ref.at[(slice(None),) * (len(ref.shape) - 2) + (pl.ds(core * rows, rows), slice(None))]


def gather_weights(shards, conv):
    n = len(shards)

    def body(*refs):
        src, conv_src = refs[:n], refs[n]
        out, conv_out = refs[n + 1:2 * n + 1], refs[2 * n + 1]
        send_sems, recv_sems = refs[2 * n + 2], refs[2 * n + 3]
        x, y, c = _place()
        me_chip = 2 * x + y
        sibling = (x, y, 1 - c)
        chips = _other_chips(x, y)
        sends = []
        for a in range(n):
            for j, (px, py) in enumerate(chips):
                sends.append(_remote(_half(src[a], c), _half(out[a].at[me_chip], c),
                                     send_sems.at[6 * a + j], recv_sems.at[6 * a + j], (px, py, c)))
        for j, (px, py) in enumerate(chips):
            sends.append(_remote(conv_src, conv_out.at[me_chip], send_sems.at[6 * n + j], recv_sems.at[6 * n + j], (px, py, c)))
        for cp in sends:
            cp.start()
        passed = []
        for a in range(n):
            for j, (px, py) in enumerate(chips):
                theirs = _half(out[a].at[2 * px + py], c)
                _remote(theirs, theirs, send_sems.at[6 * a + j], recv_sems.at[6 * a + j], (px, py, c)).wait_recv()
                cp = _remote(theirs, theirs, send_sems.at[6 * a + 3 + j], recv_sems.at[6 * a + 3 + j], sibling)
                cp.start()
                passed.append(cp)
        for j, (px, py) in enumerate(chips):
            theirs = conv_out.at[2 * px + py]
            _remote(theirs, theirs, send_sems.at[6 * n + j], recv_sems.at[6 * n + j], (px, py, c)).wait_recv()
        for a in range(n):
            for j, (px, py) in enumerate(chips):
                theirs = _half(out[a].at[2 * px + py], 1 - c)
                _remote(theirs, theirs, send_sems.at[6 * a + 3 + j], recv_sems.at[6 * a + 3 + j], sibling).wait_recv()
        for cp in sends + passed:
            cp.wait_send()

    return pl.pallas_call(
        body, name="gather_weights", in_specs=[HBM_SPEC] * (n + 1), out_specs=[HBM_SPEC] * (n + 1),
        out_shape=[jax.ShapeDtypeStruct((N_CHIPS,) + s.shape, s.dtype) for s in list(shards) + [conv]],
        scratch_shapes=[pltpu.SemaphoreType.DMA((6 * n + 3,)), pltpu.SemaphoreType.DMA((6 * n + 3,))],
    )(*shards, conv)


SEM_SPEC = pl.BlockSpec(memory_space=pltpu.SEMAPHORE)
SPLIT_EFFECT = pltpu.SideEffectType.DATAFLOW_SIDE_EFFECTING


def _gather_async_copies(src, land, send_sems, recv_sems, x, y, c):
    me_chip = 2 * x + y
    sends, arrivals = [], []
    for a in range(len(src)):
        for j, (px, py) in enumerate(_other_chips(x, y)):
            for core in range(2):
                sends.append(_remote(_half(src[a], c), _half(land[a].at[me_chip], c), send_sems.at[6 * a + 2 * j + core],
                                     recv_sems.at[6 * a + 2 * j + c], (px, py, core)))
                theirs = _half(land[a].at[2 * px + py], core)
                arrivals.append(_remote(theirs, theirs, send_sems.at[6 * a + 2 * j + core],
                                        recv_sems.at[6 * a + 2 * j + core], (px, py, core)))
    return sends, arrivals


def gather_weights_start(shards, after):
    n = len(shards)

    def body(*refs):
        src, land = refs[:n], refs[n:2 * n]
        send_sems, recv_sems, token = refs[2 * n + 1], refs[2 * n + 2], refs[4 * n + 3]
        x, y, c = _place()
        for cp in _gather_async_copies(src, land, send_sems, recv_sems, x, y, c)[0]:
            cp.start()
        token[...] = jnp.zeros_like(token)

    zones = [pltpu.with_memory_space_constraint(lax.empty((N_CHIPS,) + s.shape, s.dtype), pltpu.HBM) for s in shards]
    srcs = [pltpu.with_memory_space_constraint(s, pltpu.HBM) for s in shards]
    out = pl.pallas_call(
        body, name="gather_weights_start",
        out_shape=[pltpu.SemaphoreType.DMA((6 * n,)), pltpu.SemaphoreType.DMA((6 * n,))]
        + [pltpu.HBM(s.shape, s.dtype) for s in shards] + [pltpu.HBM(z.shape, z.dtype) for z in zones]
        + [jax.ShapeDtypeStruct((8, LANES), f32)],
        in_specs=[HBM_SPEC] * (2 * n) + [pl.BlockSpec(memory_space=pl.ANY)],
        out_specs=[SEM_SPEC, SEM_SPEC] + [HBM_SPEC] * (2 * n) + [pl.BlockSpec(memory_space=pltpu.VMEM)],
        input_output_aliases={i: 2 + i for i in range(2 * n)},
        compiler_params=pltpu.CompilerParams(has_side_effects=SPLIT_EFFECT),
    )(*srcs, *zones, after)
    return out[0], out[1], out[2:2 + n], out[2 + n:2 + 2 * n], out[-1]


def gather_weights_wait(send_sems, recv_sems, shards, zones, after):
    n = len(shards)

    def body(*refs):
        src, land = refs[:n], refs[n:2 * n]
        send_sems, recv_sems = refs[2 * n], refs[2 * n + 1]
        x, y, c = _place()
        sends, arrivals = _gather_async_copies(src, land, send_sems, recv_sems, x, y, c)
        for cp in sends:
            cp.wait_send()
        for cp in arrivals:
            cp.wait_recv()

    out = pl.pallas_call(
        body, name="gather_weights_wait",
        out_shape=[pltpu.HBM(s.shape, s.dtype) for s in shards] + [pltpu.HBM(z.shape, z.dtype) for z in zones],
        in_specs=[HBM_SPEC] * (2 * n) + [SEM_SPEC, SEM_SPEC, pl.BlockSpec(memory_space=pl.ANY)],
        out_specs=[HBM_SPEC] * (2 * n),
        input_output_aliases={i: i for i in range(2 * n)},
        compiler_params=pltpu.CompilerParams(has_side_effects=SPLIT_EFFECT),
    )(*shards, *zones, send_sems, recv_sems, after)
    return out[n:]


def swap_grad_halves(grads, *, name):
    n = len(grads)

    def body(*refs):
        g, land, send_sems, recv_sems = refs[:n], refs[n:2 * n], refs[2 * n], refs[2 * n + 1]
        x, y, c = _place()
        copies = [_remote(_half(g[a], 1 - c), land[a], send_sems.at[a], recv_sems.at[a], (x, y, 1 - c)) for a in range(n)]
        for cp in copies:
            cp.start()
        for cp in copies:
            cp.wait()

    return pl.pallas_call(
        body, name=name, in_specs=[HBM_SPEC] * n, out_specs=[HBM_SPEC] * n,
        out_shape=[jax.ShapeDtypeStruct((N_CHIPS, g.shape[1] // 2, g.shape[2]), g.dtype) for g in grads],
        scratch_shapes=[pltpu.SemaphoreType.DMA((n,)), pltpu.SemaphoreType.DMA((n,))],
    )(*grads)


GRAD_ROWS = 512


def add_grad_halves(g, land, core, *, name):
    _, half, cols = land.shape
    tr = GRAD_ROWS if half % GRAD_ROWS == 0 else half
    nb = half // tr

    def body(c_ref, g_ref, l_ref, o_ref):
        o_ref[...] = (g_ref[...].astype(f32) + l_ref[...].astype(f32)).astype(o_ref.dtype)

    blk = (1, tr, cols)
    return pl.pallas_call(
        body, name=name,
        grid_spec=pltpu.PrefetchScalarGridSpec(
            num_scalar_prefetch=1, grid=(N_CHIPS, nb),
            in_specs=[pl.BlockSpec(blk, lambda k, i, c_ref: (k, c_ref[0] * nb + i, 0)),
                      pl.BlockSpec(blk, lambda k, i, c_ref: (k, i, 0))],
            out_specs=pl.BlockSpec(blk, lambda k, i, c_ref: (k, i, 0))),
        out_shape=jax.ShapeDtypeStruct(land.shape, land.dtype),
        compiler_params=_cparams("parallel", "parallel"),
    )(core, g, land)


def scatter_grads(parts):
    n = len(parts)

    def body(*refs):
        p, land, send_sems, recv_sems = refs[:n], refs[n:2 * n], refs[2 * n], refs[2 * n + 1]
        x, y, c = _place()
        me_chip = 2 * x + y
        chips = _other_chips(x, y)
        sends = [_remote(p[a].at[2 * px + py], land[a].at[me_chip], send_sems.at[3 * a + j], recv_sems.at[3 * a + j], (px, py, c))
                 for a in range(n) for j, (px, py) in enumerate(chips)]
        for cp in sends:
            cp.start()
        for a in range(n):
            for j, (px, py) in enumerate(chips):
                slot = land[a].at[2 * px + py]
                _remote(slot, slot, send_sems.at[3 * a + j], recv_sems.at[3 * a + j], (px, py, c)).wait_recv()
        for cp in sends:
            cp.wait_send()

    return pl.pallas_call(
        body, name="scatter_grads", in_specs=[HBM_SPEC] * n, out_specs=[HBM_SPEC] * n,
        out_shape=[jax.ShapeDtypeStruct(p.shape, p.dtype) for p in parts],
        scratch_shapes=[pltpu.SemaphoreType.DMA((3 * n,)), pltpu.SemaphoreType.DMA((3 * n,))],
    )(*parts)


def _scatter_async_copies(parts, land, send_sems, recv_sems, x, y, c):
    me_chip = 2 * x + y
    sends, arrivals = [], []
    for a in range(len(parts)):
        for j, (px, py) in enumerate(_other_chips(x, y)):
            sems = (send_sems.at[3 * a + j], recv_sems.at[3 * a + j], (px, py, c))
            sends.append(_remote(parts[a].at[2 * px + py], land[a].at[me_chip], *sems))
            slot = land[a].at[2 * px + py]
            arrivals.append(_remote(slot, slot, *sems))
    return sends, arrivals


def scatter_grads_start(parts, *, name):
    n = len(parts)

    def body(*refs):
        p, land = refs[:n], refs[n:2 * n]
        send_sems, recv_sems, token = refs[2 * n], refs[2 * n + 1], refs[4 * n + 2]
        x, y, c = _place()
        for cp in _scatter_async_copies(p, land, send_sems, recv_sems, x, y, c)[0]:
            cp.start()
        token[...] = jnp.zeros_like(token)

    zones = [pltpu.with_memory_space_constraint(lax.empty(p.shape, p.dtype), pltpu.HBM) for p in parts]
    srcs = [pltpu.with_memory_space_constraint(p, pltpu.HBM) for p in parts]
    hbm = [pltpu.HBM(p.shape, p.dtype) for p in parts]
    out = pl.pallas_call(
        body, name=name,
        out_shape=[pltpu.SemaphoreType.DMA((3 * n,)), pltpu.SemaphoreType.DMA((3 * n,))] + hbm + hbm
        + [jax.ShapeDtypeStruct((8, LANES), f32)],
        in_specs=[HBM_SPEC] * (2 * n),
        out_specs=[SEM_SPEC, SEM_SPEC] + [HBM_SPEC] * (2 * n) + [pl.BlockSpec(memory_space=pltpu.VMEM)],
        input_output_aliases={i: 2 + i for i in range(2 * n)},
        compiler_params=pltpu.CompilerParams(has_side_effects=SPLIT_EFFECT),
    )(*srcs, *zones)
    return out[0], out[1], out[2:2 + n], out[2 + n:2 + 2 * n], out[-1]


def scatter_grads_wait(send_sems, recv_sems, parts, zones, after, *, name):
    n = len(parts)

    def body(*refs):
        p, land = refs[:n], refs[n:2 * n]
        x, y, c = _place()
        sends, arrivals = _scatter_async_copies(p, land, refs[2 * n], refs[2 * n + 1], x, y, c)
        for cp in sends:
            cp.wait_send()
        for cp in arrivals:
            cp.wait_recv()

    hbm = [pltpu.HBM(p.shape, p.dtype) for p in parts]
    out = pl.pallas_call(
        body, name=name, out_shape=hbm + hbm,
        in_specs=[HBM_SPEC] * (2 * n) + [SEM_SPEC, SEM_SPEC, pl.BlockSpec(memory_space=pl.ANY)],
        out_specs=[HBM_SPEC] * (2 * n),
        input_output_aliases={i: i for i in range(2 * n)},
        compiler_params=pltpu.CompilerParams(has_side_effects=SPLIT_EFFECT),
    )(*parts, *zones, send_sems, recv_sems, after)
    return out[:n], out[n:]


def sum_grads(part, land, order, *, name):
    _, half, cols = part.shape
    tr = GRAD_ROWS if half % GRAD_ROWS == 0 else half

    def body(order_ref, p_ref, l1_ref, l2_ref, l3_ref, o_ref):
        o_ref[...] = ((p_ref[0].astype(f32) + l1_ref[0].astype(f32)) + l2_ref[0].astype(f32)) + l3_ref[0].astype(f32)

    slot = lambda j: pl.BlockSpec((1, tr, cols), lambda i, order_ref: (order_ref[j], i, 0))
    return pl.pallas_call(
        body, name=name,
        grid_spec=pltpu.PrefetchScalarGridSpec(
            num_scalar_prefetch=1, grid=(half // tr,), in_specs=[slot(0), slot(1), slot(2), slot(3)],
            out_specs=pl.BlockSpec((tr, cols), lambda i, order_ref: (i, 0))),
        out_shape=jax.ShapeDtypeStruct((half, cols), f32),
        compiler_params=_cparams("parallel"),
    )(order, part, land, land, land)


def _peer(x, y, c, r):
    return ((1 - x) if r & 4 else x, (1 - y) if r & 2 else y, (1 - c) if r & 1 else c)


def _reduce_async_copies(grads, land, send_sems, recv_sems, x, y, c):
    me = 4 * x + 2 * y + c
    sends, arrivals = [], []
    for a in range(len(grads)):
        for r in range(1, N_DEV):
            px, py, pc = _peer(x, y, c, r)
            sems = (send_sems.at[7 * a + r - 1], recv_sems.at[7 * a + r - 1], (px, py, pc))
            sends.append(_remote(_half(grads[a].at[2 * px + py], pc), land[a].at[me], *sems))
            slot = land[a].at[4 * px + 2 * py + pc]
            arrivals.append(_remote(slot, slot, *sems))
    return sends, arrivals


def reduce_grads_start(grads, *, name):
    n = len(grads)

    def body(*refs):
        g, land = refs[:n], refs[n:2 * n]
        send_sems, recv_sems, token = refs[2 * n], refs[2 * n + 1], refs[4 * n + 2]
        x, y, c = _place()
        for cp in _reduce_async_copies(g, land, send_sems, recv_sems, x, y, c)[0]:
            cp.start()
        token[...] = jnp.zeros_like(token)

    zones = [pltpu.with_memory_space_constraint(lax.empty((N_DEV, g.shape[1] // 2, g.shape[2]), g.dtype), pltpu.HBM)
             for g in grads]
    srcs = [pltpu.with_memory_space_constraint(g, pltpu.HBM) for g in grads]
    out = pl.pallas_call(
        body, name=name,
        out_shape=[pltpu.SemaphoreType.DMA((7 * n,)), pltpu.SemaphoreType.DMA((7 * n,))]
        + [pltpu.HBM(g.shape, g.dtype) for g in grads] + [pltpu.HBM(z.shape, z.dtype) for z in zones]
        + [jax.ShapeDtypeStruct((8, LANES), f32)],
        in_specs=[HBM_SPEC] * (2 * n),
        out_specs=[SEM_SPEC, SEM_SPEC] + [HBM_SPEC] * (2 * n) + [pl.BlockSpec(memory_space=pltpu.VMEM)],
        input_output_aliases={i: 2 + i for i in range(2 * n)},
        compiler_params=pltpu.CompilerParams(has_side_effects=SPLIT_EFFECT),
    )(*srcs, *zones)
    return out[0], out[1], out[2:2 + n], out[2 + n:2 + 2 * n], out[-1]


def reduce_grads_wait(send_sems, recv_sems, grads, zones, after, *, name):
    n = len(grads)

    def body(*refs):
        g, land = refs[:n], refs[n:2 * n]
        x, y, c = _place()
        sends, arrivals = _reduce_async_copies(g, land, refs[2 * n], refs[2 * n + 1], x, y, c)
        for cp in sends:
            cp.wait_send()
        for cp in arrivals:
            cp.wait_recv()

    hbm = [pltpu.HBM(a.shape, a.dtype) for a in list(grads) + list(zones)]
    out = pl.pallas_call(
        body, name=name, out_shape=hbm,
        in_specs=[HBM_SPEC] * (2 * n) + [SEM_SPEC, SEM_SPEC, pl.BlockSpec(memory_space=pl.ANY)],
        out_specs=[HBM_SPEC] * (2 * n),
        input_output_aliases={i: i for i in range(2 * n)},
        compiler_params=pltpu.CompilerParams(has_side_effects=SPLIT_EFFECT),
    )(*grads, *zones, send_sems, recv_sems, after)
    return out[:n], out[n:]


def sum_partials(g, land, where, *, name):
    _, half, cols = land.shape
    tr = GRAD_ROWS if half % GRAD_ROWS == 0 else half
    nb = half // tr

    def body(where_ref, g_ref, *rest):
        o_ref = rest[-1]
        acc = g_ref[0].astype(f32)
        for l_ref in rest[:-1]:
            acc = acc + l_ref[0].astype(f32)
        o_ref[...] = acc

    blk = (1, tr, cols)
    slot = lambda j: pl.BlockSpec(blk, lambda i, where_ref: (where_ref[2 + j], i, 0))
    return pl.pallas_call(
        body, name=name,
        grid_spec=pltpu.PrefetchScalarGridSpec(
            num_scalar_prefetch=1, grid=(nb,),
            in_specs=[pl.BlockSpec(blk, lambda i, where_ref: (where_ref[0], where_ref[1] * nb + i, 0))]
            + [slot(j) for j in range(N_DEV - 1)],
            out_specs=pl.BlockSpec((tr, cols), lambda i, where_ref: (i, 0))),
        out_shape=jax.ShapeDtypeStruct((half, cols), f32),
        compiler_params=_cparams("parallel"),
    )(where, g, *([land] * (N_DEV - 1)))


def swap_reduced_halves(mine, *, name):
    n = len(mine)

    def body(*refs):
        r, out, send_sems, recv_sems = refs[:n], refs[n:2 * n], refs[2 * n], refs[2 * n + 1]
        x, y, c = _place()
        copies = [_remote(r[a], out[a], send_sems.at[a], recv_sems.at[a], (x, y, 1 - c)) for a in range(n)]
        for cp in copies:
            cp.start()
        for cp in copies:
            cp.wait()

    return pl.pallas_call(
        body, name=name, in_specs=[HBM_SPEC] * n, out_specs=[HBM_SPEC] * n,
        out_shape=[jax.ShapeDtypeStruct(r.shape, r.dtype) for r in mine],
        scratch_shapes=[pltpu.SemaphoreType.DMA((n,)), pltpu.SemaphoreType.DMA((n,))],
    )(*mine)


def adamw_halves(w, mine, theirs, m, v, core, *, name):
    R, C = w.shape
    tr = min(GRAD_ROWS, R // 2)
    half_nb = R // 2 // tr

    def body(c_ref, w_ref, a_ref, b_ref, m_ref, v_ref, g_ref, d_ref, nm_ref, nv_ref):
        low = pl.program_id(0) < half_nb
        gv = jnp.where(low == (c_ref[0] == 0), a_ref[...], b_ref[...])
        nm = ADAM_B1 * m_ref[...] + (1.0 - ADAM_B1) * gv
        nv = ADAM_B2 * v_ref[...] + (1.0 - ADAM_B2) * jnp.square(gv)
        m_hat = nm / (1.0 - ADAM_B1 ** ADAM_STEP)
        v_hat = nv / (1.0 - ADAM_B2 ** ADAM_STEP)
        g_ref[...] = gv
        d_ref[...] = -ADAM_LR * (m_hat / (jnp.sqrt(v_hat) + ADAM_EPS) + ADAM_WD * w_ref[...])
        nm_ref[...] = nm
        nv_ref[...] = nv

    full = pl.BlockSpec((tr, C), lambda i, c_ref: (i, 0))
    part = pl.BlockSpec((tr, C), lambda i, c_ref: (i % half_nb, 0))
    out = jax.ShapeDtypeStruct((R, C), f32)
    return pl.pallas_call(
        body, name=name,
        grid_spec=pltpu.PrefetchScalarGridSpec(
            num_scalar_prefetch=1, grid=(2 * half_nb,), in_specs=[full, part, part, full, full], out_specs=[full] * 4),
        out_shape=[out] * 4, compiler_params=_cparams("parallel"),
    )(core, w, mine, theirs, m, v)


N_DEV = 8


def all_reduce_small(v):
    def body(src_ref, out_ref, land_ref, send_sems, recv_sems):
        x, y, c = _place()
        me = 4 * x + 2 * y + c
        copies = []
        for r in range(1, N_DEV):
            peer = ((1 - x) if r & 4 else x, (1 - y) if r & 2 else y, (1 - c) if r & 1 else c)
            copies.append(_remote(src_ref, land_ref.at[r], send_sems.at[r - 1], recv_sems.at[r - 1], peer))
        for cp in copies:
            cp.start()
        land_ref[0] = src_ref[...]
        for cp in copies:
            cp.wait()
        acc = land_ref[me]
        for d in range(1, N_DEV):
            acc = acc + land_ref[jnp.bitwise_xor(me, d)]
        out_ref[...] = acc

    vm = pl.BlockSpec(memory_space=pltpu.VMEM)
    return pl.pallas_call(
        body, name="all_reduce_small", in_specs=[vm], out_specs=vm,
        out_shape=jax.ShapeDtypeStruct(v.shape, v.dtype),
        scratch_shapes=[pltpu.VMEM((N_DEV,) + v.shape, v.dtype),
                        pltpu.SemaphoreType.DMA((N_DEV - 1,)), pltpu.SemaphoreType.DMA((N_DEV - 1,))],
    )(v)


def adamw(w, g, m, v, *, name, tr=None, tc=None):
    R, C = w.shape
    if tc is None:
        tr, tc = min(tr, R), C
        blk = pl.BlockSpec((tr, C), lambda i: (i, 0))
    else:
        tr = R
        blk = pl.BlockSpec((R, tc), lambda i: (0, i))

    def body(w_ref, g_ref, m_ref, v_ref, d_ref, nm_ref, nv_ref):
        gv = g_ref[...]
        nm = ADAM_B1 * m_ref[...] + (1.0 - ADAM_B1) * gv
        nv = ADAM_B2 * v_ref[...] + (1.0 - ADAM_B2) * jnp.square(gv)
        m_hat = nm / (1.0 - ADAM_B1 ** ADAM_STEP)
        v_hat = nv / (1.0 - ADAM_B2 ** ADAM_STEP)
        d_ref[...] = -ADAM_LR * (m_hat / (jnp.sqrt(v_hat) + ADAM_EPS) + ADAM_WD * w_ref[...])
        nm_ref[...] = nm
        nv_ref[...] = nv

    out = jax.ShapeDtypeStruct((R, C), f32)
    return pl.pallas_call(
        body, name=name, grid=((R // tr) * (C // tc),), in_specs=[blk] * 4, out_specs=[blk] * 3, out_shape=[out] * 3,
        compiler_params=_cparams("parallel"),
    )(w, g, m, v)


BIG_SHARDS = (("w_in", (1024, 900), True), ("w_out", (256, 1024), False), ("w_cq", (256, 512), False),
              ("w_ckv", (256, 1024), False), ("w_co", (512, 256), True), ("w_mlp1", (1024, 1024), True),
              ("w_mlp2", (1024, 1024), False))
CONV_SHARD = (CONV_WIDTH, 3 * GDN_WIDTH // N_CHIPS)
SMALL_DIMS = (("norm_mix_g", 1024), ("fox_qnorm_g", 64), ("fox_knorm_g", 64), ("fox_f_bias", 8), ("fox_onorm_g", 64),
              ("gdn_A_log", 4), ("gdn_dt_bias", 4), ("gdn_onorm_g", 128), ("norm_xattn_g", 1024), ("mem_norm_g", 1024),
              ("xattn_qnorm_g", 128), ("xattn_knorm_g", 128), ("norm_mlp_g", 1024))
WEIGHT_ORDER = ("norm_mix_g", "w_in", "fox_qnorm_g", "fox_knorm_g", "fox_f_bias", "fox_onorm_g", "gdn_conv_w", "gdn_A_log",
                "gdn_dt_bias", "gdn_onorm_g", "w_out", "norm_xattn_g", "mem_norm_g", "w_cq", "w_ckv", "xattn_qnorm_g",
                "xattn_knorm_g", "w_co", "norm_mlp_g", "w_mlp1", "w_mlp2")


def _pack_rows(pieces, rows, lead=()):
    cat = jnp.concatenate([p.reshape(lead + (-1,)) for p in pieces], axis=-1)
    cat = jnp.pad(cat, [(0, 0)] * len(lead) + [(0, rows * LANES - cat.shape[-1])])
    return cat.reshape(lead + (rows, LANES))


def _unpack_rows(buf, sizes, lead=()):
    flat = buf.reshape(lead + (-1,))
    out, off = [], 0
    for n in sizes:
        out.append(flat[..., off:off + n])
        off += n
    return out


def _conv_to_wire(conv):
    return lax.bitcast_convert_type(conv, bf16)


def _conv_from_wire(wire):
    return lax.bitcast_convert_type(wire, f32)


SMALL_ROWS = 96
SMALL_ADAM_ROWS = 56


def kernel(x, mem, norm_mix_g, w_in, fox_qnorm_g, fox_knorm_g, fox_f_bias, fox_onorm_g, gdn_conv_w, gdn_A_log, gdn_dt_bias, gdn_onorm_g, w_out, norm_xattn_g, mem_norm_g, w_cq, w_ckv, xattn_qnorm_g, xattn_knorm_g, w_co, norm_mlp_g, w_mlp1, w_mlp2, loss_target, m_norm_mix_g, m_w_in, m_fox_qnorm_g, m_fox_knorm_g, m_fox_f_bias, m_fox_onorm_g, m_gdn_conv_w, m_gdn_A_log, m_gdn_dt_bias, m_gdn_onorm_g, m_w_out, m_norm_xattn_g, m_mem_norm_g, m_w_cq, m_w_ckv, m_xattn_qnorm_g, m_xattn_knorm_g, m_w_co, m_norm_mlp_g, m_w_mlp1, m_w_mlp2, v_norm_mix_g, v_w_in, v_fox_qnorm_g, v_fox_knorm_g, v_fox_f_bias, v_fox_onorm_g, v_gdn_conv_w, v_gdn_A_log, v_gdn_dt_bias, v_gdn_onorm_g, v_w_out, v_norm_xattn_g, v_mem_norm_g, v_w_cq, v_w_ckv, v_xattn_qnorm_g, v_xattn_knorm_g, v_w_co, v_norm_mlp_g, v_w_mlp1, v_w_mlp2):
    wts = dict(norm_mix_g=norm_mix_g, w_in=w_in, fox_qnorm_g=fox_qnorm_g, fox_knorm_g=fox_knorm_g, fox_f_bias=fox_f_bias,
               fox_onorm_g=fox_onorm_g, gdn_conv_w=gdn_conv_w, gdn_A_log=gdn_A_log, gdn_dt_bias=gdn_dt_bias,
               gdn_onorm_g=gdn_onorm_g, w_out=w_out, norm_xattn_g=norm_xattn_g, mem_norm_g=mem_norm_g, w_cq=w_cq, w_ckv=w_ckv,
               xattn_qnorm_g=xattn_qnorm_g, xattn_knorm_g=xattn_knorm_g, w_co=w_co, norm_mlp_g=norm_mlp_g, w_mlp1=w_mlp1,
               w_mlp2=w_mlp2)
    mom = dict(norm_mix_g=m_norm_mix_g, w_in=m_w_in, fox_qnorm_g=m_fox_qnorm_g, fox_knorm_g=m_fox_knorm_g,
               fox_f_bias=m_fox_f_bias, fox_onorm_g=m_fox_onorm_g, gdn_conv_w=m_gdn_conv_w, gdn_A_log=m_gdn_A_log,
               gdn_dt_bias=m_gdn_dt_bias, gdn_onorm_g=m_gdn_onorm_g, w_out=m_w_out, norm_xattn_g=m_norm_xattn_g,
               mem_norm_g=m_mem_norm_g, w_cq=m_w_cq, w_ckv=m_w_ckv, xattn_qnorm_g=m_xattn_qnorm_g,
               xattn_knorm_g=m_xattn_knorm_g, w_co=m_w_co, norm_mlp_g=m_norm_mlp_g, w_mlp1=m_w_mlp1, w_mlp2=m_w_mlp2)
    var = dict(norm_mix_g=v_norm_mix_g, w_in=v_w_in, fox_qnorm_g=v_fox_qnorm_g, fox_knorm_g=v_fox_knorm_g,
               fox_f_bias=v_fox_f_bias, fox_onorm_g=v_fox_onorm_g, gdn_conv_w=v_gdn_conv_w, gdn_A_log=v_gdn_A_log,
               gdn_dt_bias=v_gdn_dt_bias, gdn_onorm_g=v_gdn_onorm_g, w_out=v_w_out, norm_xattn_g=v_norm_xattn_g,
               mem_norm_g=v_mem_norm_g, w_cq=v_w_cq, w_ckv=v_w_ckv, xattn_qnorm_g=v_xattn_qnorm_g,
               xattn_knorm_g=v_xattn_knorm_g, w_co=v_w_co, norm_mlp_g=v_norm_mlp_g, w_mlp1=v_w_mlp1, w_mlp2=v_w_mlp2)
    B, S, D = x.shape
    T = B * S
    big_names = [n for n, _, _ in BIG_SHARDS]
    chip = 2 * lax.axis_index("x") + lax.axis_index("y")
    core = lax.axis_index("c").astype(jnp.int32).reshape(1)

    shards = {n: wts[n][0].astype(MXU_DTYPE) for n in big_names[1:]}
    in_t = lambda p: jnp.swapaxes(p[0], 0, 1)
    shards["w_in"] = jnp.pad(in_t(w_in).astype(MXU_DTYPE), ((0, IN_SHARD_PAD - IN_SHARD), (0, 0)))
    w_in_all, conv_all = gather_weights([shards["w_in"]], gdn_conv_w[0])
    late = big_names[1:]
    send_sems, recv_sems, late_src, late_zones, token = gather_weights_start([shards[n] for n in late], conv_all)
    own = lambda g, s: lax.dynamic_update_slice(g, s[None], (chip,) + (0,) * s.ndim)
    full = {"w_in": own(w_in_all, shards["w_in"])}
    conv_full = own(conv_all, gdn_conv_w[0]).transpose(1, 0, 2).reshape(CONV_WIDTH, 3 * GDN_WIDTH)
    rows = lambda g: g.reshape(N_CHIPS * g.shape[1], g.shape[2])

    def late_weights(after):
        zones = gather_weights_wait(send_sems, recv_sems, late_src, late_zones, after)
        got = {n: own(z, shards[n]) for n, z in zip(late, zones)}
        return dict(w_out=rows(got["w_out"]), w_cq=rows(got["w_cq"]), w_ckv=rows(got["w_ckv"]),
                    w_co=got["w_co"].transpose(1, 0, 2).reshape(XATTN_WIDTH, D_MODEL),
                    w_mlp1=got["w_mlp1"], w_mlp2=rows(got["w_mlp2"]))

    in_flight = []

    def grads_ready(ready):
        names = list(ready)
        *started, tok = reduce_grads_start([ready[n] for n in names], name="reduce_grads_start_%d" % len(in_flight))
        in_flight.append((names, *started))
        return tok

    w_in_t = full["w_in"][:, :IN_SHARD].reshape(IN_DIM, D_MODEL)
    w = dict(wa_t=align_w_in_t(w_in_t), conv_w=conv_full, late=late_weights, grads_ready=grads_ready)
    sp = {n: wts[n] for n, _ in SMALL_DIMS}
    sp["norm_mix_g"] = sp["norm_mix_g"] + token[0, 0]

    loss_part, grad_x, g_big, g_small = local_step(x.reshape(T, D), mem.reshape(-1, D), loss_target.reshape(T, D), w, sp, B=B)

    small_pieces = [g_small[n] for n, _ in SMALL_DIMS] + [g_small["gdn_conv_w"], loss_part]
    small_sizes = [d for _, d in SMALL_DIMS] + [CONV_WIDTH * 3 * GDN_WIDTH, LANES]
    red_small = _unpack_rows(all_reduce_small(_pack_rows(small_pieces, SMALL_ROWS)), small_sizes)
    grads = {n: p.reshape(1, d) for (n, d), p in zip(SMALL_DIMS, red_small)}
    conv_grad = lax.dynamic_slice(red_small[-2].reshape(CONV_WIDTH, 3 * GDN_WIDTH), (0, chip * CONV_SHARD[1]), CONV_SHARD)
    grads["gdn_conv_w"] = conv_grad.reshape((1,) + CONV_SHARD)
    loss = red_small[-1][0]

    parts, zones = {}, {}

    def wait_group(k, after):
        names, send_sems, recv_sems, thru, land = in_flight[k]
        thru, land = reduce_grads_wait(send_sems, recv_sems, thru, land, after, name="reduce_grads_wait_%d" % k)
        parts.update(zip(names, thru))
        zones.update(zip(names, land))

    wait_group(0, grad_x)
    wait_group(1, grad_x)
    dev = 2 * chip + core[0]
    where = jnp.stack([chip, core[0]] + [dev ^ r for r in range(1, N_DEV)]).astype(jnp.int32)
    mine = [sum_partials(parts[n], zones[n], where, name="sum_partials_" + n) for n in late]
    theirs = swap_reduced_halves(mine, name="swap_reduced_halves")

    delta, new_m, new_v = {}, {}, {}
    for n, a, b in zip(late, mine, theirs):
        g, d, nm, nv = adamw_halves(wts[n][0], a, b, mom[n][0], var[n][0], core, name="adamw_" + n)
        grads[n], delta[n], new_m[n], new_v[n] = g[None], d[None], nm[None], nv[None]
    wait_group(2, new_v[late[-1]])
    mine_in = sum_partials(parts["w_in"], zones["w_in"], where, name="sum_partials_w_in")
    (theirs_in,) = swap_reduced_halves([mine_in], name="swap_reduced_halves_w_in")
    south = core[0] == 0
    g_in_t = jnp.concatenate([jnp.where(south, mine_in, theirs_in), jnp.where(south, theirs_in, mine_in)])[:IN_SHARD]
    back = lambda t: jnp.swapaxes(t, 0, 1)[None]
    d, nm, nv = adamw(in_t(w_in), g_in_t, in_t(m_w_in), in_t(v_w_in), name="adamw_w_in", tc=256)
    grads["w_in"], delta["w_in"], new_m["w_in"], new_v["w_in"] = back(g_in_t), back(d), back(nm), back(nv)
    small_names = [n for n, _ in SMALL_DIMS] + ["gdn_conv_w"]
    small_sz = [d for _, d in SMALL_DIMS] + [CONV_SHARD[0] * CONV_SHARD[1]]
    packed4 = [_pack_rows([src[n] for n in small_names], SMALL_ADAM_ROWS) for src in (wts, grads, mom, var)]
    outs = adamw(*packed4, name="adamw_small", tr=SMALL_ADAM_ROWS)
    for dst, buf in zip((delta, new_m, new_v), outs):
        for n, p in zip(small_names, _unpack_rows(buf, small_sz)):
            dst[n] = p.reshape(wts[n].shape)

    return (loss, grad_x.reshape(B, S, D), *[grads[n] for n in WEIGHT_ORDER], *[delta[n] for n in WEIGHT_ORDER],
            *[new_m[n] for n in WEIGHT_ORDER], *[new_v[n] for n in WEIGHT_ORDER])
```

```python
import functools

import jax
import jax.numpy as jnp
import numpy as np
from jax import lax
from jax.experimental import pallas as pl
from jax.experimental.pallas import tpu as pltpu

f32 = jnp.float32
bf16 = jnp.bfloat16
MXU_DTYPE = jnp.bfloat16
WIRE_DTYPE = jnp.bfloat16
INV_PRECISION = None

D_MODEL = 1024
FOX_HEADS = 8
FOX_HEAD_DIM = 64
FOX_WIDTH = 512
GDN_HEADS = 4
GDN_HEAD_DIM = 128
GDN_WIDTH = 512
CONV_WIDTH = 4
GDN_CHUNK = 64
XATTN_HEADS = 4
XATTN_HEAD_DIM = 128
XATTN_WIDTH = 512
D_FF = 4096
IN_DIM = 3600
EPS = 1e-6
NEG_INF = -1e30
LANES = 128
ADAM_LR = 0.001
ADAM_B1 = 0.9
ADAM_B2 = 0.999
ADAM_EPS = 1e-08
ADAM_WD = 0.01
ADAM_STEP = 10
VMEM_LIMIT = 48 * 1024 * 1024

COL_FOX = 0
COL_GDN = 1536
COL_Z = 3072
COL_SMALL = 3584
IN_ALIGNED = 3840
IN_TILE = 768
SM_F = 0
SM_B = 8
SM_A = 12


def _cparams(*sem):
    return pltpu.CompilerParams(dimension_semantics=sem, vmem_limit_bytes=VMEM_LIMIT)


def _mx(v):
    return v.astype(MXU_DTYPE)


def _dot(a, b, dims, precision=None):
    return lax.dot_general(a, b, (dims, ((), ())), preferred_element_type=f32, precision=precision)


def _dotm(a, b, dims):
    return _dot(_mx(a), _mx(b), dims)


NN = ((1,), (0,))
NT = ((1,), (1,))
TN = ((0,), (0,))


def matmul(a, b, *, name, ta=False, tb=False, b_stacked=False, out_stacked=False, residual=None, relu2_out=False,
           relu2_bwd_aux=None, out_dtype=f32, tm=1024, tn=1024, tk=1024):
    M, K = (a.shape[1], a.shape[0]) if ta else a.shape
    if b_stacked:
        b_cols = b.shape[2]
        N, tk = (b.shape[1], min(tk, b_cols)) if tb else (N_CHIPS * b_cols, tk)
        tn = tn if tb else min(tn, b_cols)
        assert K == (N_CHIPS * b_cols if tb else b.shape[1]), (name, a.shape, b.shape)
    else:
        N = b.shape[0] if tb else b.shape[1]
    if out_stacked:
        tn = min(tn, N // N_CHIPS)
    tm, tn, tk = min(tm, M), min(tn, N), min(tk, K)
    assert M % tm == 0 and N % tn == 0 and K % tk == 0, (name, M, N, K)
    nk = K // tk
    has_res = residual is not None
    has_aux = relu2_bwd_aux is not None

    def body(*refs):
        a_ref, b_ref = refs[0], refs[1]
        pos = 2
        res_ref = aux_ref = None
        if has_res:
            res_ref = refs[pos]
            pos += 1
        if has_aux:
            aux_ref = refs[pos]
            pos += 1
        o_ref = refs[pos]
        k = pl.program_id(2)
        dims = ((0,) if ta else (1,), (1,) if tb else (0,))
        part = _dot(_mx(a_ref[...]), _mx(b_ref[...]), dims)

        def finish(r):
            if has_res:
                r = r + res_ref[...]
            if has_aux:
                r = r * (2.0 * jnp.sqrt(aux_ref[...].astype(f32)))
            if relu2_out:
                o_ref[...] = jnp.square(jnp.maximum(r, 0.0)).astype(o_ref.dtype)
            else:
                o_ref[...] = r.astype(o_ref.dtype)

        if nk == 1:
            finish(part)
            return
        acc_ref = refs[pos + 1]

        @pl.when(k == 0)
        def _():
            acc_ref[...] = part

        @pl.when((k > 0) & (k < nk - 1))
        def _():
            acc_ref[...] += part

        @pl.when(k == nk - 1)
        def _():
            finish(acc_ref[...] + part)

    a_spec = pl.BlockSpec((tk, tm), lambda i, j, k: (k, i)) if ta else pl.BlockSpec((tm, tk), lambda i, j, k: (i, k))
    if b_stacked and tb:
        per = b_cols // tk
        b_spec = pl.BlockSpec((None, tn, tk), lambda i, j, k: (k // per, j, k % per))
    elif b_stacked:
        per = b_cols // tn
        b_spec = pl.BlockSpec((None, tk, tn), lambda i, j, k: (j // per, k, j % per))
    else:
        b_spec = pl.BlockSpec((tn, tk), lambda i, j, k: (j, k)) if tb else pl.BlockSpec((tk, tn), lambda i, j, k: (k, j))
    if out_stacked:
        assert not (has_res or has_aux or relu2_out), name
        per_o = N // N_CHIPS // tn
        o_spec = pl.BlockSpec((None, tm, tn), lambda i, j, k: (j // per_o, i, j % per_o))
        out_full = (N_CHIPS, M, N // N_CHIPS)
    else:
        o_spec = pl.BlockSpec((tm, tn), lambda i, j, k: (i, j))
        out_full = (M, N)
    in_specs, args = [a_spec, b_spec], [a, b]
    if has_res:
        in_specs.append(o_spec)
        args.append(residual)
    if has_aux:
        in_specs.append(o_spec)
        args.append(relu2_bwd_aux)
    out_shape = [jax.ShapeDtypeStruct(out_full, out_dtype)]
    out_specs = [o_spec]
    res = pl.pallas_call(
        body, name=name, grid=(M // tm, N // tn, nk), in_specs=in_specs, out_specs=out_specs, out_shape=out_shape,
        scratch_shapes=[pltpu.VMEM((tm, tn), f32)] if nk > 1 else [],
        compiler_params=_cparams("parallel", "parallel", "arbitrary"),
    )(*args)
    return res[0]


def matmul_rows(a, b, extras, *, name, mode, tb=False, b_stacked=False, tm=1024, tk=1024):
    M, K = a.shape
    N = D_MODEL
    if b_stacked:
        assert tb, name
        tk = min(tk, b.shape[2])
        per = b.shape[2] // tk
        b_spec = pl.BlockSpec((None, N, tk), lambda i, k: (k // per, 0, k % per))
    elif tb:
        tk = min(tk, K)
        b_spec = pl.BlockSpec((N, tk), lambda i, k: (0, k))
    else:
        tk = min(tk, K)
        b_spec = pl.BlockSpec((tk, N), lambda i, k: (k, 0))
    tm = min(tm, M)
    assert M % tm == 0 and K % tk == 0, (name, M, K)
    nk = K // tk
    extras = [e for e in extras if e is not None]
    n_ex = len(extras)

    def body(*refs):
        a_ref, b_ref = refs[0], refs[1]
        ex = refs[2:2 + n_ex]
        o_ref = refs[2 + n_ex]
        n_out = 3 if mode == "loss" else 2
        s_ref = refs[1 + n_ex + n_out]
        i, k = pl.program_id(0), pl.program_id(1)
        part = _dot(_mx(a_ref[...]), _mx(b_ref[...]), ((1,), (1,) if tb else (0,)))

        def finish(y):
            if mode == "rms_fwd":
                y = y + ex[0][...]
                o_ref[...] = y
                s_ref[...] = (y * lax.rsqrt(jnp.mean(y * y, axis=-1, keepdims=True) + EPS) * ex[1][...]).astype(MXU_DTYPE)
                return

            @pl.when(i == 0)
            def _():
                s_ref[...] = jnp.zeros_like(s_ref)

            if mode == "rms_bwd":
                xv, gv = ex[0][...], ex[1][...]
                rstd = lax.rsqrt(jnp.mean(xv * xv, axis=-1, keepdims=True) + EPS)
                xhat = xv * rstd
                gd = y * gv
                dx = rstd * (gd - xhat * jnp.mean(gd * xhat, axis=-1, keepdims=True))
                o_ref[...] = dx + ex[2][...] if n_ex == 3 else dx
                s_ref[...] += jnp.sum(y * xhat, axis=0, keepdims=True)
            else:
                e = y + ex[0][...] - ex[1][...]
                o_ref[...] = e * (1.0 / N)
                refs[3 + n_ex][...] = (e * (1.0 / N)).astype(MXU_DTYPE)
                tot = 0.5 * jnp.sum(jnp.mean(e * e, axis=-1, keepdims=True), axis=0, keepdims=True)
                s_ref[...] += jnp.broadcast_to(tot, s_ref.shape)

        if nk == 1:
            finish(part)
            return
        acc_ref = refs[2 + n_ex + n_out]

        @pl.when(k == 0)
        def _():
            acc_ref[...] = part

        @pl.when((k > 0) & (k < nk - 1))
        def _():
            acc_ref[...] += part

        @pl.when(k == nk - 1)
        def _():
            finish(acc_ref[...] + part)

    row = pl.BlockSpec((tm, N), lambda i, k: (i, 0))
    vec = pl.BlockSpec((1, N), lambda i, k: (0, 0))
    if mode == "rms_bwd":
        ex_specs = [row, vec] + ([row] if n_ex == 3 else [])
        s_shape, s_spec = jax.ShapeDtypeStruct((1, N), f32), vec
    elif mode == "rms_fwd":
        ex_specs = [row, vec]
        s_shape, s_spec = jax.ShapeDtypeStruct((M, N), MXU_DTYPE), row
    else:
        ex_specs = [row, row]
        s_shape, s_spec = jax.ShapeDtypeStruct((1, LANES), f32), pl.BlockSpec((1, LANES), lambda i, k: (0, 0))
    return pl.pallas_call(
        body, name=name, grid=(M // tm, nk),
        in_specs=[pl.BlockSpec((tm, tk), lambda i, k: (i, k)), b_spec] + ex_specs,
        out_specs=[row] * (2 if mode == "loss" else 1) + [s_spec],
        out_shape=[jax.ShapeDtypeStruct((M, N), f32)] + ([jax.ShapeDtypeStruct((M, N), MXU_DTYPE)] if mode == "loss" else [])
        + [s_shape],
        scratch_shapes=[pltpu.VMEM((tm, N), f32)] if nk > 1 else [],
        compiler_params=_cparams("arbitrary", "arbitrary"),
    )(a, b, *extras)


def rms_fwd(x, g, *, name, tr=1024):
    R, D = x.shape
    tr = min(tr, R)

    def body(x_ref, g_ref, o_ref):
        xv = x_ref[...]
        y = xv * lax.rsqrt(jnp.mean(xv * xv, axis=-1, keepdims=True) + EPS)
        o_ref[...] = (y * g_ref[...]).astype(o_ref.dtype)

    return pl.pallas_call(
        body, name=name, grid=(R // tr,),
        in_specs=[pl.BlockSpec((tr, D), lambda i: (i, 0)), pl.BlockSpec((1, D), lambda i: (0, 0))],
        out_specs=pl.BlockSpec((tr, D), lambda i: (i, 0)),
        out_shape=jax.ShapeDtypeStruct((R, D), MXU_DTYPE),
        compiler_params=_cparams("parallel"),
    )(x, g)


def rms_bwd(x, g, dh, residual, *, name, tr=512):
    R, D = x.shape
    tr = min(tr, R)
    has_res = residual is not None

    def body(*refs):
        if has_res:
            x_ref, g_ref, dh_ref, res_ref, dx_ref, dg_ref = refs
        else:
            x_ref, g_ref, dh_ref, dx_ref, dg_ref = refs
        xv = x_ref[...]
        rstd = lax.rsqrt(jnp.mean(xv * xv, axis=-1, keepdims=True) + EPS)
        xhat = xv * rstd
        dh = dh_ref[...].astype(f32)
        gd = dh * g_ref[...]
        dx = rstd * (gd - xhat * jnp.mean(gd * xhat, axis=-1, keepdims=True))
        if has_res:
            dx = dx + res_ref[...]
        dx_ref[...] = dx

        @pl.when(pl.program_id(0) == 0)
        def _():
            dg_ref[...] = jnp.zeros_like(dg_ref)

        dg_ref[...] += jnp.sum(dh * xhat, axis=0, keepdims=True)

    row = pl.BlockSpec((tr, D), lambda i: (i, 0))
    vec = pl.BlockSpec((1, D), lambda i: (0, 0))
    in_specs = [row, vec, row] + ([row] if has_res else [])
    args = [x, g, dh] + ([residual] if has_res else [])
    return pl.pallas_call(
        body, name=name, grid=(R // tr,), in_specs=in_specs, out_specs=[row, vec],
        out_shape=[jax.ShapeDtypeStruct((R, D), f32), jax.ShapeDtypeStruct((1, D), f32)],
        compiler_params=_cparams("arbitrary"),
    )(*args)


def loss_head(y, target, *, tr=512):
    R, D = y.shape
    tr = min(tr, R)

    def body(y_ref, t_ref, dy_ref, loss_ref):
        e = y_ref[...] - t_ref[...]
        dy_ref[...] = e * (1.0 / D)

        @pl.when(pl.program_id(0) == 0)
        def _():
            loss_ref[...] = jnp.zeros_like(loss_ref)

        part = 0.5 * jnp.sum(jnp.mean(e * e, axis=-1, keepdims=True), axis=0, keepdims=True)
        loss_ref[...] += jnp.broadcast_to(part, loss_ref.shape)

    row = pl.BlockSpec((tr, D), lambda i: (i, 0))
    return pl.pallas_call(
        body, name="loss_head", grid=(R // tr,), in_specs=[row, row],
        out_specs=[row, pl.BlockSpec((1, LANES), lambda i: (0, 0))],
        out_shape=[jax.ShapeDtypeStruct((R, D), f32), jax.ShapeDtypeStruct((1, LANES), f32)],
        compiler_params=_cparams("arbitrary"),
    )(y, target)


def _head_rms(v, g):
    r = lax.rsqrt(jnp.mean(v * v, axis=-1, keepdims=True) + EPS)
    return v * r * g, r


def _head_rms_bwd(v, r, g, dn):
    vhat = v * r
    gd = dn * g
    dv = r * (gd - vhat * jnp.mean(gd * vhat, axis=-1, keepdims=True))
    return dv, jnp.sum(dn * vhat, axis=0, keepdims=True)


def _softmax_rows(s):
    m = jnp.max(s, axis=-1, keepdims=True)
    e = jnp.exp(s - m)
    return e / jnp.sum(e, axis=-1, keepdims=True)


def xattn_fwd(cq, ckv, gq, gk, *, B, tq=1024):
    T = cq.shape[0]
    S = T // B
    M = ckv.shape[0] // B
    tq = min(tq, S)
    nq = S // tq
    hd, W = XATTN_HEAD_DIM, XATTN_WIDTH
    scale = hd ** -0.5

    def body(q_ref, k_ref, v_ref, gq_ref, gk_ref, o_ref):
        for h in range(XATTN_HEADS):
            sl = slice(h * hd, (h + 1) * hd)
            qn, _ = _head_rms(q_ref[:, sl], gq_ref[...])
            kn, _ = _head_rms(k_ref[:, sl], gk_ref[...])
            p = _softmax_rows(_dot(_mx(qn), _mx(kn), NT) * scale)
            o_ref[:, sl] = _dot(_mx(p), _mx(v_ref[:, sl]), NN).astype(o_ref.dtype)

    vec = pl.BlockSpec((1, hd), lambda b, i: (0, 0))
    qspec = pl.BlockSpec((tq, W), lambda b, i: (b * nq + i, 0))
    return pl.pallas_call(
        body, name="xattn_fwd", grid=(B, nq),
        in_specs=[qspec, pl.BlockSpec((M, W), lambda b, i: (b, 0)), pl.BlockSpec((M, W), lambda b, i: (b, 1)), vec, vec],
        out_specs=qspec, out_shape=jax.ShapeDtypeStruct((T, W), MXU_DTYPE),
        compiler_params=_cparams("parallel", "parallel"),
    )(cq, ckv, ckv, gq, gk)


def xattn_bwd(cq, ckv, gq, gk, dco, *, B, tq=1024):
    T = cq.shape[0]
    S = T // B
    M = ckv.shape[0] // B
    tq = min(tq, S)
    nq = S // tq
    hd, W = XATTN_HEAD_DIM, XATTN_WIDTH
    scale = hd ** -0.5

    def body(q_ref, k_ref, v_ref, gq_ref, gk_ref, do_ref, dq_ref, dkv_ref, dgq_ref, dgk_ref, dkn_acc, dv_acc):
        b, i = pl.program_id(0), pl.program_id(1)

        @pl.when((b == 0) & (i == 0))
        def _():
            dgq_ref[...] = jnp.zeros_like(dgq_ref)
            dgk_ref[...] = jnp.zeros_like(dgk_ref)

        @pl.when(i == 0)
        def _():
            dkn_acc[...] = jnp.zeros_like(dkn_acc)
            dv_acc[...] = jnp.zeros_like(dv_acc)

        gqv, gkv = gq_ref[...], gk_ref[...]
        for h in range(XATTN_HEADS):
            sl = slice(h * hd, (h + 1) * hd)
            q, k, v = q_ref[:, sl], k_ref[:, sl], v_ref[:, sl]
            qn, rq = _head_rms(q, gqv)
            kn, _ = _head_rms(k, gkv)
            p = _softmax_rows(_dot(_mx(qn), _mx(kn), NT) * scale)
            do = do_ref[:, sl]
            dv_acc[:, sl] += _dot(_mx(p), _mx(do), TN)
            dp = _dot(_mx(do), _mx(v), NT)
            ds = p * (dp - jnp.sum(dp * p, axis=-1, keepdims=True)) * scale
            dqn = _dot(_mx(ds), _mx(kn), NN)
            dkn_acc[:, sl] += _dot(_mx(ds), _mx(qn), TN)
            dq, dgq = _head_rms_bwd(q, rq, gqv, dqn)
            dq_ref[:, sl] = dq.astype(dq_ref.dtype)
            dgq_ref[...] += dgq

        @pl.when(i == nq - 1)
        def _():
            for h in range(XATTN_HEADS):
                sl = slice(h * hd, (h + 1) * hd)
                k = k_ref[:, sl]
                rk = lax.rsqrt(jnp.mean(k * k, axis=-1, keepdims=True) + EPS)
                dk, dgk = _head_rms_bwd(k, rk, gkv, dkn_acc[:, sl])
                dkv_ref[:, sl] = dk.astype(dkv_ref.dtype)
                dkv_ref[:, slice(W + h * hd, W + (h + 1) * hd)] = dv_acc[:, sl].astype(dkv_ref.dtype)
                dgk_ref[...] += dgk

    vec = pl.BlockSpec((1, hd), lambda b, i: (0, 0))
    qspec = pl.BlockSpec((tq, W), lambda b, i: (b * nq + i, 0))
    return pl.pallas_call(
        body, name="xattn_bwd", grid=(B, nq),
        in_specs=[qspec, pl.BlockSpec((M, W), lambda b, i: (b, 0)), pl.BlockSpec((M, W), lambda b, i: (b, 1)), vec, vec, qspec],
        out_specs=[qspec, pl.BlockSpec((M, 2 * W), lambda b, i: (b, 0)), vec, vec],
        out_shape=[jax.ShapeDtypeStruct((T, W), MXU_DTYPE), jax.ShapeDtypeStruct((B * M, 2 * W), MXU_DTYPE),
                   jax.ShapeDtypeStruct((1, hd), f32), jax.ShapeDtypeStruct((1, hd), f32)],
        scratch_shapes=[pltpu.VMEM((M, W), f32), pltpu.VMEM((M, W), f32)],
        compiler_params=_cparams("arbitrary", "arbitrary"),
    )(cq, ckv, ckv, gq, gk, dco)


FOX_PAIRS = FOX_HEADS // 2


def _fox_scores(qn, kn, ccol, crow, q0, tq, S, scale):
    s = _dot(_mx(qn), _mx(kn), NT) * scale + ccol - crow
    qpos = q0 + lax.broadcasted_iota(jnp.int32, (tq, S), 0)
    kpos = lax.broadcasted_iota(jnp.int32, (tq, S), 1)
    return jnp.where(kpos <= qpos, s, NEG_INF)


def fox_fwd(P, ccol, crow, gq, gk, go, *, B, tq=256):
    T = P.shape[0]
    S = T // B
    tq = min(tq, S)
    nq = S // tq
    hd = FOX_HEAD_DIM
    scale = hd ** -0.5

    def body(q_ref, k_ref, v_ref, ccol_ref, crow_ref, gq_ref, gk_ref, go_ref, o_ref, oa_ref):
        q0 = pl.program_id(2) * tq
        for e in range(2):
            sl = slice(e * hd, (e + 1) * hd)
            qn, _ = _head_rms(q_ref[:, sl], gq_ref[:, sl])
            kn, _ = _head_rms(k_ref[:, sl], gk_ref[:, sl])
            p = _softmax_rows(_fox_scores(qn, kn, ccol_ref[0, e], crow_ref[0, e], q0, tq, S, scale))
            o = _dot(_mx(p), _mx(v_ref[:, sl]), NN)
            o_ref[:, sl] = o
            oa_ref[:, sl] = _head_rms(o, go_ref[:, sl])[0].astype(oa_ref.dtype)

    W = 2 * hd
    vec = pl.BlockSpec((1, W), lambda b, h, i: (0, 0))
    ospec = pl.BlockSpec((tq, W), lambda b, h, i: (b * nq + i, h))
    return pl.pallas_call(
        body, name="fox_fwd", grid=(B, FOX_PAIRS, nq),
        in_specs=[pl.BlockSpec((tq, W), lambda b, h, i: (b * nq + i, h)),
                  pl.BlockSpec((S, W), lambda b, h, i: (b, FOX_PAIRS + h)),
                  pl.BlockSpec((S, W), lambda b, h, i: (b, 2 * FOX_PAIRS + h)),
                  pl.BlockSpec((1, 2, tq, 1), lambda b, h, i: (b, h, i, 0)),
                  pl.BlockSpec((1, 2, 1, S), lambda b, h, i: (b, h, 0, 0)), vec, vec, vec],
        out_specs=[ospec, ospec],
        out_shape=[jax.ShapeDtypeStruct((T, FOX_WIDTH), f32), jax.ShapeDtypeStruct((T, FOX_WIDTH), MXU_DTYPE)],
        compiler_params=_cparams("parallel", "parallel", "parallel"),
    )(P, P, P, ccol, crow, gq, gk, go)


def fox_bwd(P, ccol, crow, gq, gk, go, o_raw, d_oab, *, B, tq=256):
    T = P.shape[0]
    S = T // B
    tq = min(tq, S)
    nq = S // tq
    hd = FOX_HEAD_DIM
    scale = hd ** -0.5

    def body(q_ref, k_ref, v_ref, ccol_ref, crow_ref, gq_ref, gk_ref, go_ref, o_ref, doa_ref,
             dq_ref, dk_ref, dv_ref, dccol_ref, dcrow_ref, dgq_ref, dgk_ref, dgo_ref, dkn_acc, dv_acc, dcrow_acc):
        b, h, i = pl.program_id(0), pl.program_id(1), pl.program_id(2)
        q0 = i * tq

        @pl.when((b == 0) & (h == 0) & (i == 0))
        def _():
            dgq_ref[...] = jnp.zeros_like(dgq_ref)
            dgk_ref[...] = jnp.zeros_like(dgk_ref)
            dgo_ref[...] = jnp.zeros_like(dgo_ref)

        @pl.when(i == 0)
        def _():
            dkn_acc[...] = jnp.zeros_like(dkn_acc)
            dv_acc[...] = jnp.zeros_like(dv_acc)
            dcrow_acc[...] = jnp.zeros_like(dcrow_acc)

        for e in range(2):
            sl = slice(e * hd, (e + 1) * hd)
            q, k, v = q_ref[:, sl], k_ref[:, sl], v_ref[:, sl]
            gqv, gkv, gov = gq_ref[:, sl], gk_ref[:, sl], go_ref[:, sl]
            qn, rq = _head_rms(q, gqv)
            kn, rk = _head_rms(k, gkv)
            p = _softmax_rows(_fox_scores(qn, kn, ccol_ref[0, e], crow_ref[0, e], q0, tq, S, scale))
            o = o_ref[:, sl]
            ro = lax.rsqrt(jnp.mean(o * o, axis=-1, keepdims=True) + EPS)
            do, dgo = _head_rms_bwd(o, ro, gov, doa_ref[:, sl])
            dgo_ref[:, sl] += dgo
            dv_acc[e] += _dot(_mx(p), _mx(do), TN)
            dp = _dot(_mx(do), _mx(v), NT)
            ds = p * (dp - jnp.sum(do * o, axis=-1, keepdims=True))
            dccol_ref[0, e] = jnp.sum(ds, axis=1, keepdims=True)
            dcrow_acc[e] -= jnp.sum(ds, axis=0, keepdims=True)
            dqn = _dot(_mx(ds), _mx(kn), NN) * scale
            dkn_acc[e] += _dot(_mx(ds), _mx(qn), TN) * scale
            dq, dgq = _head_rms_bwd(q, rq, gqv, dqn)
            dq_ref[:, sl] = dq.astype(dq_ref.dtype)
            dgq_ref[:, sl] += dgq

        @pl.when(i == nq - 1)
        def _():
            for e in range(2):
                sl = slice(e * hd, (e + 1) * hd)
                k = k_ref[:, sl]
                gkv = gk_ref[:, sl]
                rk = lax.rsqrt(jnp.mean(k * k, axis=-1, keepdims=True) + EPS)
                dk, dgk = _head_rms_bwd(k, rk, gkv, dkn_acc[e])
                dk_ref[:, sl] = dk.astype(dk_ref.dtype)
                dv_ref[:, sl] = dv_acc[e].astype(dv_ref.dtype)
                dgk_ref[:, sl] += dgk
                dcrow_ref[0, e] = dcrow_acc[e]

    W = 2 * hd
    vec = pl.BlockSpec((1, W), lambda b, h, i: (0, 0))
    qspec = pl.BlockSpec((tq, W), lambda b, h, i: (b * nq + i, h))
    kvout = pl.BlockSpec((S, W), lambda b, h, i: (b, h))
    colspec = pl.BlockSpec((1, 2, tq, 1), lambda b, h, i: (b, h, i, 0))
    rowspec = pl.BlockSpec((1, 2, 1, S), lambda b, h, i: (b, h, 0, 0))
    return pl.pallas_call(
        body, name="fox_bwd", grid=(B, FOX_PAIRS, nq),
        in_specs=[qspec,
                  pl.BlockSpec((S, W), lambda b, h, i: (b, FOX_PAIRS + h)),
                  pl.BlockSpec((S, W), lambda b, h, i: (b, 2 * FOX_PAIRS + h)),
                  colspec, rowspec, vec, vec, vec, qspec, qspec],
        out_specs=[qspec, kvout, kvout, colspec, rowspec, vec, vec, vec],
        out_shape=[jax.ShapeDtypeStruct((T, FOX_WIDTH), MXU_DTYPE), jax.ShapeDtypeStruct((T, FOX_WIDTH), MXU_DTYPE),
                   jax.ShapeDtypeStruct((T, FOX_WIDTH), MXU_DTYPE),
                   jax.ShapeDtypeStruct((B, FOX_HEADS, S, 1), f32), jax.ShapeDtypeStruct((B, FOX_HEADS, 1, S), f32),
                   jax.ShapeDtypeStruct((1, W), f32), jax.ShapeDtypeStruct((1, W), f32), jax.ShapeDtypeStruct((1, W), f32)],
        scratch_shapes=[pltpu.VMEM((2, S, hd), f32), pltpu.VMEM((2, S, hd), f32), pltpu.VMEM((2, 1, S), f32)],
        compiler_params=_cparams("arbitrary", "arbitrary", "arbitrary"),
    )(P, P, P, ccol, crow, gq, gk, go, o_raw, d_oab)


FOX_TQ = 512
FOX_TK = FOX_TQ
GROUP_PRECISION = lax.Precision.HIGH


def _head_mean(v):
    n = v.shape[1]
    r = lax.broadcasted_iota(jnp.int32, (n, n), 0) // FOX_HEAD_DIM
    c = lax.broadcasted_iota(jnp.int32, (n, n), 1) // FOX_HEAD_DIM
    ones = (r == c).astype(bf16)
    hi = v.astype(bf16)
    lo = (v - hi.astype(f32)).astype(bf16)
    return (_dot(hi, ones, NN) + _dot(lo, ones, NN)) * (1.0 / FOX_HEAD_DIM)


def fox_prep_fwd(P, gq, gk, *, tr=1024):
    T = P.shape[0]
    tr = min(tr, T)
    scale = FOX_HEAD_DIM ** -0.5

    def body(q_ref, k_ref, v_ref, gq_ref, gk_ref, qn_ref, kn_ref, vb_ref):
        q, k = q_ref[...], k_ref[...]
        qn_ref[...] = (q * lax.rsqrt(_head_mean(q * q) + EPS) * (gq_ref[...] * scale)).astype(qn_ref.dtype)
        kn_ref[...] = (k * lax.rsqrt(_head_mean(k * k) + EPS) * gk_ref[...]).astype(kn_ref.dtype)
        vb_ref[...] = v_ref[...].astype(vb_ref.dtype)

    W = FOX_WIDTH
    col = lambda j: pl.BlockSpec((tr, W), lambda i: (i, j))
    vec = pl.BlockSpec((1, W), lambda i: (0, 0))
    out = jax.ShapeDtypeStruct((T, W), MXU_DTYPE)
    return pl.pallas_call(
        body, name="fox_prep_fwd", grid=(T // tr,), in_specs=[col(0), col(1), col(2), vec, vec],
        out_specs=[col(0)] * 3, out_shape=[out] * 3, compiler_params=_cparams("parallel"),
    )(P, P, P, gq, gk)


def fox_prep_bwd(P, gq, gk, dqn, dkn, *, tr=1024):
    T = P.shape[0]
    tr = min(tr, T)
    scale = FOX_HEAD_DIM ** -0.5

    def body(q_ref, k_ref, gq_ref, gk_ref, dqn_ref, dkn_ref, dq_ref, dk_ref, dgq_ref, dgk_ref):
        @pl.when(pl.program_id(0) == 0)
        def _():
            dgq_ref[...] = jnp.zeros_like(dgq_ref)
            dgk_ref[...] = jnp.zeros_like(dgk_ref)

        def one(x, g, dn, dx_ref, dg_ref):
            r = lax.rsqrt(_head_mean(x * x) + EPS)
            xhat = x * r
            gd = dn * g
            dx_ref[...] = (r * (gd - xhat * _head_mean(gd * xhat))).astype(dx_ref.dtype)
            return jnp.sum(dn * xhat, axis=0, keepdims=True)

        dgq_ref[...] += scale * one(q_ref[...], gq_ref[...] * scale, dqn_ref[...], dq_ref, dgq_ref)
        dgk_ref[...] += one(k_ref[...], gk_ref[...], dkn_ref[...], dk_ref, dgk_ref)

    W = FOX_WIDTH
    col = lambda j: pl.BlockSpec((tr, W), lambda i: (i, j))
    vec = pl.BlockSpec((1, W), lambda i: (0, 0))
    return pl.pallas_call(
        body, name="fox_prep_bwd", grid=(T // tr,), in_specs=[col(0), col(1), vec, vec, col(0), col(0)],
        out_specs=[col(0), col(0), vec, vec],
        out_shape=[jax.ShapeDtypeStruct((T, W), MXU_DTYPE), jax.ShapeDtypeStruct((T, W), MXU_DTYPE),
                   jax.ShapeDtypeStruct((1, W), f32), jax.ShapeDtypeStruct((1, W), f32)],
        compiler_params=_cparams("arbitrary"),
    )(P, P, gq, gk, dqn, dkn)


def _fox_tile_scores(q, k_ref, ccol_ref, cq, e, j, sl, mask_off):
    tq, tk = FOX_TQ, FOX_TK
    rows = pl.ds(pl.multiple_of(j * tk, tk), tk)
    k = k_ref[rows, sl]
    s = _dot(k, q, NT) + cq - ccol_ref[0, e, rows, :]
    if mask_off is not None:
        key = lax.broadcasted_iota(jnp.int32, (tk, tq), 0) + mask_off
        query = lax.broadcasted_iota(jnp.int32, (tk, tq), 1)
        s = jnp.where(key <= query, s, NEG_INF)
    return s, k, rows


def _fox_sweep(i, update, carry):
    nd = FOX_TQ // FOX_TK
    carry = lax.fori_loop(0, i * nd, lambda j, cr: update(cr, j, None), carry)
    for d in range(nd):
        carry = update(carry, i * nd + d, d * FOX_TK)
    return carry


def fox_core_fwd(qn, kn, vb, ccol, crow, go, *, B):
    T = qn.shape[0]
    S = T // B
    tq = FOX_TQ
    nq = S // tq
    hd = FOX_HEAD_DIM

    def body(q_ref, k_ref, v_ref, ccol_ref, crow_ref, go_ref, o_ref, oa_ref, lse_ref):
        i = pl.program_id(2)
        for e in range(2):
            sl = slice(e * hd, (e + 1) * hd)
            q = q_ref[:, sl]
            cq = crow_ref[0, e, i]

            def update(carry, j, mask_off):
                m, l, acc = carry
                s, _, rows = _fox_tile_scores(q, k_ref, ccol_ref, cq, e, j, sl, mask_off)
                m2 = jnp.maximum(m, jnp.max(s, axis=0, keepdims=True))
                a = jnp.exp(m - m2)
                p = jnp.exp(s - m2)
                return m2, a * l + jnp.sum(p, axis=0, keepdims=True), a * acc + _dot(v_ref[rows, sl], _mx(p), TN)

            carry = (jnp.full((1, tq), NEG_INF, f32), jnp.zeros((1, tq), f32), jnp.zeros((hd, tq), f32))
            m, l, acc = _fox_sweep(i, update, carry)
            o = (acc / l).T
            o_ref[:, sl] = o
            oa_ref[:, sl] = _head_rms(o, go_ref[:, sl])[0].astype(oa_ref.dtype)
            lse_ref[0, e, 0] = m + jnp.log(l)

    W = 2 * hd
    qspec = pl.BlockSpec((tq, W), lambda b, h, i: (b * nq + i, h))
    kspec = pl.BlockSpec((S, W), lambda b, h, i: (b, h))
    return pl.pallas_call(
        body, name="fox_core_fwd", grid=(B, FOX_PAIRS, nq),
        in_specs=[qspec, kspec, kspec, pl.BlockSpec((1, 2, S, 1), lambda b, h, i: (b, h, 0, 0)),
                  pl.BlockSpec((1, 2, nq, 1, tq), lambda b, h, i: (b, h, 0, 0, 0)),
                  pl.BlockSpec((1, W), lambda b, h, i: (0, 0))],
        out_specs=[qspec, qspec, pl.BlockSpec((1, 2, 1, 1, tq), lambda b, h, i: (b, h, i, 0, 0))],
        out_shape=[jax.ShapeDtypeStruct((T, FOX_WIDTH), f32), jax.ShapeDtypeStruct((T, FOX_WIDTH), MXU_DTYPE),
                   jax.ShapeDtypeStruct((B, FOX_HEADS, nq, 1, tq), f32)],
        compiler_params=_cparams("parallel", "parallel", "parallel"),
    )(qn, kn, vb, ccol, crow, go)


def fox_core_bwd(qn, kn, vb, ccol, crow, go, o_raw, lse, d_oab, *, B):
    T = qn.shape[0]
    S = T // B
    tq = FOX_TQ
    nq = S // tq
    hd = FOX_HEAD_DIM

    def body(q_ref, k_ref, v_ref, ccol_ref, crow_ref, go_ref, o_ref, lse_ref, doa_ref,
             dq_ref, dk_ref, dv_ref, dckey_ref, dcrow_ref, dgo_ref, dk_acc, dv_acc, dck_acc):
        b, h, i = pl.program_id(0), pl.program_id(1), pl.program_id(2)

        @pl.when((b == 0) & (h == 0) & (i == 0))
        def _():
            dgo_ref[...] = jnp.zeros_like(dgo_ref)

        @pl.when(i == 0)
        def _():
            dk_acc[...] = jnp.zeros_like(dk_acc)
            dv_acc[...] = jnp.zeros_like(dv_acc)
            dck_acc[...] = jnp.zeros_like(dck_acc)

        for e in range(2):
            sl = slice(e * hd, (e + 1) * hd)
            q = q_ref[:, sl]
            cq = crow_ref[0, e, i]
            lse_e = lse_ref[0, e, 0]
            o = o_ref[:, sl]
            ro = lax.rsqrt(jnp.mean(o * o, axis=-1, keepdims=True) + EPS)
            do, dgo = _head_rms_bwd(o, ro, go_ref[:, sl], doa_ref[:, sl])
            dgo_ref[:, sl] += dgo
            delta = jnp.sum((do * o).T, axis=0, keepdims=True)
            do_b = _mx(do)

            def update(carry, j, mask_off):
                dq, dcq = carry
                s, k, rows = _fox_tile_scores(q, k_ref, ccol_ref, cq, e, j, sl, mask_off)
                p = jnp.exp(s - lse_e)
                dv_acc[e, rows, :] += _dot(_mx(p), do_b, NN)
                ds = p * (_dot(v_ref[rows, sl], do_b, NT) - delta)
                dck_acc[e, rows, :] -= jnp.sum(ds, axis=1, keepdims=True)
                ds_b = _mx(ds)
                dk_acc[e, rows, :] += _dot(ds_b, q, NN)
                return dq + _dot(k, ds_b, TN), dcq + jnp.sum(ds, axis=0, keepdims=True)

            dq, dcq = _fox_sweep(i, update, (jnp.zeros((hd, tq), f32), jnp.zeros((1, tq), f32)))
            dq_ref[:, sl] = dq.T
            dcrow_ref[0, e, 0] = dcq

        @pl.when(i == nq - 1)
        def _():
            for e in range(2):
                sl = slice(e * hd, (e + 1) * hd)
                dk_ref[:, sl] = dk_acc[e]
                dv_ref[:, sl] = dv_acc[e].astype(dv_ref.dtype)
                dckey_ref[0, e] = jnp.transpose(jnp.broadcast_to(dck_acc[e], (S, LANES)))[0:1, :]

    W = 2 * hd
    qspec = pl.BlockSpec((tq, W), lambda b, h, i: (b * nq + i, h))
    kspec = pl.BlockSpec((S, W), lambda b, h, i: (b, h))
    colspec = pl.BlockSpec((1, 2, S, 1), lambda b, h, i: (b, h, 0, 0))
    rowspec = pl.BlockSpec((1, 2, nq, 1, tq), lambda b, h, i: (b, h, 0, 0, 0))
    tilespec = pl.BlockSpec((1, 2, 1, 1, tq), lambda b, h, i: (b, h, i, 0, 0))
    vec = pl.BlockSpec((1, W), lambda b, h, i: (0, 0))
    return pl.pallas_call(
        body, name="fox_core_bwd", grid=(B, FOX_PAIRS, nq),
        in_specs=[qspec, kspec, kspec, colspec, rowspec, vec, qspec, tilespec, qspec],
        out_specs=[qspec, kspec, kspec, pl.BlockSpec((1, 2, 1, S), lambda b, h, i: (b, h, 0, 0)), tilespec, vec],
        out_shape=[jax.ShapeDtypeStruct((T, FOX_WIDTH), f32), jax.ShapeDtypeStruct((T, FOX_WIDTH), f32),
                   jax.ShapeDtypeStruct((T, FOX_WIDTH), MXU_DTYPE),
                   jax.ShapeDtypeStruct((B, FOX_HEADS, 1, S), f32), jax.ShapeDtypeStruct((B, FOX_HEADS, nq, 1, tq), f32),
                   jax.ShapeDtypeStruct((1, W), f32)],
        scratch_shapes=[pltpu.VMEM((2, S, hd), f32), pltpu.VMEM((2, S, hd), f32), pltpu.VMEM((2, S, 1), f32)],
        compiler_params=_cparams("arbitrary", "arbitrary", "arbitrary"),
    )(qn, kn, vb, ccol, crow, go, o_raw, lse, d_oab)


def _lane_mask(lo, hi, shape):
    lane = lax.broadcasted_iota(jnp.int32, shape, 1)
    return (lane >= lo) & (lane < hi)


def _cumsum_rows(v, period, reverse=False):
    n = v.shape[0]
    pos = lax.broadcasted_iota(jnp.int32, v.shape, 0) % period
    sh = 1
    while sh < period:
        if reverse:
            v = v + jnp.where(pos + sh < period, pltpu.roll(v, n - sh, 0), 0.0)
        else:
            v = v + jnp.where(pos >= sh, pltpu.roll(v, sh, 0), 0.0)
        sh *= 2
    return v


def _gate_values(z, bias, alog):
    zb = z + bias
    ls = jax.nn.log_sigmoid(zb)
    beta = jax.nn.sigmoid(z)
    g = -jnp.exp(alog) * jax.nn.softplus(zb)
    return zb, ls, beta, g


def gates_fwd(P, bias, alog, *, B):
    T = P.shape[0]
    S = T // B

    def body(z_ref, bias_ref, alog_ref, o_ref):
        z = z_ref[...]
        _, ls, beta, g = _gate_values(z, bias_ref[...], alog_ref[...])
        c = _cumsum_rows(ls, S)
        gc = _cumsum_rows(g, GDN_CHUNK)
        o = jnp.where(_lane_mask(SM_F, SM_F + FOX_HEADS, z.shape), c, 0.0)
        o = jnp.where(_lane_mask(SM_B, SM_B + GDN_HEADS, z.shape), beta, o)
        o = jnp.where(_lane_mask(SM_A, SM_A + GDN_HEADS, z.shape), gc, o)
        o_ref[...] = o

    vec = pl.BlockSpec((1, LANES), lambda b: (0, 0))
    return pl.pallas_call(
        body, name="gates_fwd", grid=(B,),
        in_specs=[pl.BlockSpec((S, LANES), lambda b: (b, COL_SMALL // LANES)), vec, vec],
        out_specs=pl.BlockSpec((S, LANES), lambda b: (b, 0)),
        out_shape=jax.ShapeDtypeStruct((T, LANES), f32),
        compiler_params=_cparams("parallel"),
    )(P, bias, alog)


def gates_bwd(P, bias, alog, dgates, *, B):
    T = P.shape[0]
    S = T // B

    def body(z_ref, bias_ref, alog_ref, dg_ref, dz_ref, par_ref):
        z = z_ref[...]
        zb, ls, beta, g = _gate_values(z, bias_ref[...], alog_ref[...])
        d = dg_ref[...]
        dls = _cumsum_rows(d, S, reverse=True)
        dgr = _cumsum_rows(d, GDN_CHUNK, reverse=True)
        sig = jax.nn.sigmoid(zb)
        dz_f = dls * (1.0 - sig)
        dz_b = d * beta * (1.0 - beta)
        dz_a = dgr * (-jnp.exp(alog_ref[...])) * sig
        dz = jnp.where(_lane_mask(SM_F, SM_F + FOX_HEADS, z.shape), dz_f, 0.0)
        dz = jnp.where(_lane_mask(SM_B, SM_B + GDN_HEADS, z.shape), dz_b, dz)
        dz = jnp.where(_lane_mask(SM_A, SM_A + GDN_HEADS, z.shape), dz_a, dz)
        dz_ref[...] = dz.astype(dz_ref.dtype)

        @pl.when(pl.program_id(0) == 0)
        def _():
            par_ref[...] = jnp.zeros_like(par_ref)

        dalog = jnp.where(_lane_mask(SM_A, SM_A + GDN_HEADS, z.shape), dgr * g, 0.0)
        par_ref[0:1, :] += jnp.sum(dz, axis=0, keepdims=True)
        par_ref[1:2, :] += jnp.sum(dalog, axis=0, keepdims=True)

    vec = pl.BlockSpec((1, LANES), lambda b: (0, 0))
    return pl.pallas_call(
        body, name="gates_bwd", grid=(B,),
        in_specs=[pl.BlockSpec((S, LANES), lambda b: (b, COL_SMALL // LANES)), vec, vec,
                  pl.BlockSpec((S, LANES), lambda b: (b, 0))],
        out_specs=[pl.BlockSpec((S, LANES), lambda b: (b, 0)), pl.BlockSpec((8, LANES), lambda b: (0, 0))],
        out_shape=[jax.ShapeDtypeStruct((T, LANES), MXU_DTYPE), jax.ShapeDtypeStruct((8, LANES), f32)],
        compiler_params=_cparams("arbitrary"),
    )(P, bias, alog, dgates)


GDN_BLOCKS = 3 * GDN_HEADS


def _shift_rows(v, d, reverse=False):
    if d == 0:
        return v
    n = v.shape[0]
    row = lax.broadcasted_iota(jnp.int32, v.shape, 0)
    if reverse:
        return jnp.where(row + d < n, pltpu.roll(v, n - d, 0), 0.0)
    return jnp.where(row >= d, pltpu.roll(v, d, 0), 0.0)


def _conv_silu(x, w):
    pre = sum(w[j:j + 1, :] * _shift_rows(x, CONV_WIDTH - 1 - j) for j in range(CONV_WIDTH))
    return pre, pre * jax.nn.sigmoid(pre)


def gdn_prep_fwd(P, conv_w, *, B):
    T = P.shape[0]
    S = T // B

    def body(x_ref, w_ref, o_ref):
        _, y = _conv_silu(x_ref[...], w_ref[...])
        yn = y * lax.rsqrt(jnp.sum(y * y, axis=-1, keepdims=True) + EPS)
        o_ref[...] = jnp.where(pl.program_id(1) < 2 * GDN_HEADS, yn, y)

    return pl.pallas_call(
        body, name="gdn_prep_fwd", grid=(B, GDN_BLOCKS),
        in_specs=[pl.BlockSpec((S, LANES), lambda b, j: (b, COL_GDN // LANES + j)),
                  pl.BlockSpec((CONV_WIDTH, LANES), lambda b, j: (0, j))],
        out_specs=pl.BlockSpec((S, LANES), lambda b, j: (b, j)),
        out_shape=jax.ShapeDtypeStruct((T, 3 * GDN_WIDTH), f32),
        compiler_params=_cparams("parallel", "parallel"),
    )(P, conv_w)


def gdn_prep_bwd(P, conv_w, dGq, dGk, dGv, *, B):
    T = P.shape[0]
    S = T // B
    H = GDN_HEADS

    def body(x_ref, w_ref, dq_ref, dk_ref, dv_ref, dx_ref, dw_ref):
        x, w = x_ref[...], w_ref[...]
        pre, y = _conv_silu(x, w)
        jb = pl.program_id(0)
        dn = jnp.where(jb < H, dq_ref[...], jnp.where(jb < 2 * H, dk_ref[...], dv_ref[...]))
        r = lax.rsqrt(jnp.sum(y * y, axis=-1, keepdims=True) + EPS)
        n = y * r
        dy_norm = r * (dn - n * jnp.sum(dn * n, axis=-1, keepdims=True))
        dy = jnp.where(pl.program_id(0) < 2 * GDN_HEADS, dy_norm, dn)
        sg = jax.nn.sigmoid(pre)
        dpre = dy * (sg * (1.0 + pre * (1.0 - sg)))
        dx = sum(w[j:j + 1, :] * _shift_rows(dpre, CONV_WIDTH - 1 - j, reverse=True) for j in range(CONV_WIDTH))
        dx_ref[...] = dx.astype(dx_ref.dtype)

        @pl.when(pl.program_id(1) == 0)
        def _():
            dw_ref[...] = jnp.zeros_like(dw_ref)

        for j in range(CONV_WIDTH):
            dw_ref[j:j + 1, :] += jnp.sum(dpre * _shift_rows(x, CONV_WIDTH - 1 - j), axis=0, keepdims=True)

    return pl.pallas_call(
        body, name="gdn_prep_bwd", grid=(GDN_BLOCKS, B),
        in_specs=[pl.BlockSpec((S, LANES), lambda j, b: (b, COL_GDN // LANES + j)),
                  pl.BlockSpec((CONV_WIDTH, LANES), lambda j, b: (0, j))]
        + [pl.BlockSpec((S, LANES), lambda j, b, t=t: (jnp.where(j // H == t, b, 0), jnp.clip(j - t * H, 0, H - 1)))
           for t in range(3)],
        out_specs=[pl.BlockSpec((S, LANES), lambda j, b: (b, j)),
                   pl.BlockSpec((CONV_WIDTH, LANES), lambda j, b: (0, j))],
        out_shape=[jax.ShapeDtypeStruct((T, 3 * GDN_WIDTH), MXU_DTYPE),
                   jax.ShapeDtypeStruct((CONV_WIDTH, 3 * GDN_WIDTH), f32)],
        compiler_params=_cparams("arbitrary", "arbitrary"),
    )(P, conv_w, dGq, dGk, dGv)


GDN_GROUP = 16
GDN_GROUP_FWD = 16
B_NN = (((2,), (1,)), ((0,), (0,)))
B_NT = (((2,), (2,)), ((0,), (0,)))
B_TN = (((1,), (1,)), ((0,), (0,)))


def _bmm(a, b, dims, precision=None):
    if precision is None:
        a, b = _mx(a), _mx(b)
    return lax.dot_general(a, b, dims, preferred_element_type=f32, precision=precision)


def _tri_inverse(A):
    C = A.shape[-1]
    row = lax.broadcasted_iota(jnp.int32, A.shape, 1)
    col = lax.broadcasted_iota(jnp.int32, A.shape, 2)
    eye = (row == col).astype(f32)
    X = jnp.where((row // 4) == (col // 4), -A, 0.0)
    X2 = _bmm(X, X, B_NN, INV_PRECISION)
    Tm = eye + X + X2 + _bmm(X, X2, B_NN, INV_PRECISION)
    b = 4
    while b < C:
        off = ((row // (2 * b)) == (col // (2 * b))) & ((row // b) != (col // b))
        Tm = Tm - _bmm(_bmm(Tm, jnp.where(off, A, 0.0), B_NN, INV_PRECISION), Tm, B_NN, INV_PRECISION)
        b *= 2
    return Tm


def _pick_lane(block, lane_idx):
    lane = lax.broadcasted_iota(jnp.int32, block.shape, 1)
    return jnp.sum(jnp.where(lane == lane_idx, block, 0.0), axis=1, keepdims=True)


def _gdn_local(q, k, v, beta, gc, Tm=None, uwm=None):
    C = GDN_CHUNK
    n = q.shape[0] // C
    q = q.reshape(n, C, -1) * (GDN_HEAD_DIM ** -0.5)
    k = k.reshape(n, C, -1)
    v = v.reshape(n, C, -1)
    beta = beta.reshape(n, C, 1)
    gc = gc.reshape(n, C, 1)
    row = lax.broadcasted_iota(jnp.int32, (n, C, C), 1)
    col = lax.broadcasted_iota(jnp.int32, (n, C, C), 2)
    gcT = jnp.swapaxes(jnp.broadcast_to(gc, (n, C, C)), 1, 2)
    D = jnp.exp(jnp.where(row >= col, gc - gcT, NEG_INF))
    kb = k * beta
    vb = v * beta
    A = jnp.where(row > col, _bmm(kb, k, B_NT) * D, 0.0)
    Gam = jnp.exp(gc)
    kg = kb * Gam
    gl = gc[:, C - 1:C, :]
    kdec = jnp.exp(gl - gc)
    loc = dict(q=q, k=k, v=v, beta=beta, gc=gc, D=D, kb=kb, vb=vb, A=A, Gam=Gam, kg=kg,
               kdec=kdec, kd=k * kdec, qg=q * Gam, gam=jnp.exp(gl), row=row, col=col)
    uwm = Tm is None if uwm is None else uwm
    Tm = _tri_inverse(A) if Tm is None else Tm.reshape(n, C, C)
    if uwm:
        loc.update(u=_bmm(Tm, vb, B_NN), w=_bmm(Tm, kg, B_NN), M=_bmm(q, k, B_NT) * D)
    loc["Tm"] = Tm
    return loc


def _gdn_store_local(loc, r0, u_s, w_s, qg_s, kd_s, M_s, gam_s, c0):
    n = loc["u"].shape[0]
    R = n * GDN_CHUNK
    u_s[pl.ds(r0, R), :] = loc["u"].reshape(R, -1)
    w_s[pl.ds(r0, R), :] = loc["w"].reshape(R, -1)
    qg_s[pl.ds(r0, R), :] = loc["qg"].reshape(R, -1)
    kd_s[pl.ds(r0, R), :] = loc["kd"].reshape(R, -1)
    M_s[pl.ds(r0, R), :] = loc["M"].reshape(R, -1)
    gam_s[pl.ds(c0, n)] = jnp.broadcast_to(loc["gam"], (n, 1, LANES))


def _gdn_specs(S):
    blk = lambda off: pl.BlockSpec((S, LANES), lambda b, h: (b, off + h))
    return blk


def gdn_fwd(G, gates, P, g_on, *, B):
    T = G.shape[0]
    S = T // B
    C = GDN_CHUNK
    N = S // C
    grp = min(GDN_GROUP_FWD, N)
    R = grp * C
    hd = GDN_HEAD_DIM

    def body(q_ref, k_ref, v_ref, gt_ref, z_ref, gon_ref, o_ref, ob_ref, st_ref, tm_ref, A_s, B_s, Q_s, O_s, gam_s):
        h = pl.program_id(1)

        def local(gi, carry):
            r0 = pl.multiple_of(gi * R, R)
            gt = gt_ref[pl.ds(r0, R), :]
            loc = _gdn_local(q_ref[pl.ds(r0, R), :], k_ref[pl.ds(r0, R), :], v_ref[pl.ds(r0, R), :],
                             _pick_lane(gt, SM_B + h), _pick_lane(gt, SM_A + h))
            chunks = pl.ds(gi * grp, grp)
            tm_ref[0, 0, pl.ds(r0, R), :] = loc["Tm"].reshape(R, C)
            A_s[chunks] = -_bmm(loc["kd"], loc["w"], B_TN)
            B_s[chunks] = _bmm(loc["kd"], loc["u"], B_TN)
            Q_s[pl.ds(r0, R), :] = (loc["qg"] - _bmm(loc["M"], loc["w"], B_NN)).reshape(R, hd)
            O_s[pl.ds(r0, R), :] = _bmm(loc["M"], loc["u"], B_NN).reshape(R, hd)
            gam_s[chunks] = jnp.broadcast_to(loc["gam"], (grp, 1, LANES))
            return carry

        lax.fori_loop(0, N // grp, local, 0)

        def step(n, state):
            st_ref[0, 0, n] = state
            return state * gam_s[n] + _dotm(A_s[n], state, NN) + B_s[n]

        lax.fori_loop(0, N, step, jnp.zeros((hd, hd), f32))

        def outputs(gi, carry):
            r0 = pl.multiple_of(gi * R, R)
            Q = Q_s[pl.ds(r0, R), :].reshape(grp, C, hd)
            o = _bmm(Q, st_ref[0, 0, pl.ds(gi * grp, grp)], B_NN).reshape(R, hd) + O_s[pl.ds(r0, R), :]
            o_ref[pl.ds(r0, R), :] = o
            return carry

        lax.fori_loop(0, N // grp, outputs, 0)
        o = o_ref[...]
        z = z_ref[...]
        ob_ref[...] = (_head_rms(o, gon_ref[...])[0] * (z * jax.nn.sigmoid(z))).astype(ob_ref.dtype)

    blk = lambda off: pl.BlockSpec((S, LANES), lambda b, h: (b, off + h))
    rows = lambda: pltpu.VMEM((S, hd), f32)
    return pl.pallas_call(
        body, name="gdn_fwd", grid=(B, GDN_HEADS),
        in_specs=[blk(0), blk(GDN_HEADS), blk(2 * GDN_HEADS), pl.BlockSpec((S, LANES), lambda b, h: (b, 0)),
                  blk(COL_Z // LANES), pl.BlockSpec((1, hd), lambda b, h: (0, 0))],
        out_specs=[blk(0), blk(0), pl.BlockSpec((1, 1, N, hd, hd), lambda b, h: (b, h, 0, 0, 0)),
                   pl.BlockSpec((1, 1, S, C), lambda b, h: (b, h, 0, 0))],
        out_shape=[jax.ShapeDtypeStruct((T, GDN_WIDTH), f32), jax.ShapeDtypeStruct((T, GDN_WIDTH), MXU_DTYPE),
                   jax.ShapeDtypeStruct((B, GDN_HEADS, N, hd, hd), f32), jax.ShapeDtypeStruct((B, GDN_HEADS, S, C), f32)],
        scratch_shapes=[pltpu.VMEM((N, hd, hd), f32), pltpu.VMEM((N, hd, hd), f32), rows(), rows(),
                        pltpu.VMEM((N, 1, LANES), f32)],
        compiler_params=_cparams("parallel", "parallel"),
    )(G, G, G, gates, P, g_on)


def gdn_bwd(G, gates, P, g_on, o_raw, states, tm, d_oab, *, B):
    T = G.shape[0]
    S = T // B
    C = GDN_CHUNK
    N = S // C
    grp = min(GDN_GROUP, N)
    R = grp * C
    hd = GDN_HEAD_DIM

    def body(q_ref, k_ref, v_ref, gt_ref, z_ref, gon_ref, o_ref, st_ref, tm_ref, dob_ref,
             dq_ref, dk_ref, dv_ref, dgt_ref, dz_ref, dgon_ref,
             u_s, w_s, M_s, gam_s, do_s, A_s, C_s, dst_s):
        b, h = pl.program_id(0), pl.program_id(1)

        @pl.when((b == 0) & (h == 0))
        def _():
            dgon_ref[...] = jnp.zeros_like(dgon_ref)

        @pl.when(h == 0)
        def _():
            dgt_ref[...] = jnp.zeros_like(dgt_ref)

        def group_inputs(gi, uwm):
            r0 = pl.multiple_of(gi * R, R)
            gt = gt_ref[pl.ds(r0, R), :]
            return r0, _gdn_local(q_ref[pl.ds(r0, R), :], k_ref[pl.ds(r0, R), :], v_ref[pl.ds(r0, R), :],
                                  _pick_lane(gt, SM_B + h), _pick_lane(gt, SM_A + h), tm_ref[0, 0, pl.ds(r0, R), :], uwm)

        def local(gi, carry):
            r0, loc = group_inputs(gi, True)
            rows, chunks = pl.ds(r0, R), pl.ds(gi * grp, grp)
            u_s[rows, :] = loc["u"].reshape(R, hd)
            w_s[rows, :] = loc["w"].reshape(R, hd)
            M_s[rows, :] = loc["M"].reshape(R, C)
            gam_s[chunks] = jnp.broadcast_to(loc["gam"], (grp, 1, LANES))
            o, z, gon = o_ref[rows, :], z_ref[rows, :], gon_ref[...]
            dob = dob_ref[rows, :]
            on, ro = _head_rms(o, gon)
            sz = jax.nn.sigmoid(z)
            dz_ref[rows, :] = (dob * on * (sz * (1.0 + z * (1.0 - sz)))).astype(dz_ref.dtype)
            do, dgon = _head_rms_bwd(o, ro, gon, dob * (z * sz))
            do_s[rows, :] = do
            dgon_ref[...] += dgon
            A_s[chunks] = -_bmm(loc["kd"], loc["w"], B_TN)
            C_s[chunks] = _bmm(loc["qg"] - _bmm(loc["M"], loc["w"], B_NN), do.reshape(grp, C, hd), B_TN)
            return carry

        lax.fori_loop(0, N // grp, local, 0)

        def step(t, dS):
            n = N - 1 - t
            dst_s[n] = dS
            return dS * gam_s[n] + _dotm(A_s[n], dS, TN) + C_s[n]

        lax.fori_loop(0, N, step, jnp.zeros((hd, hd), f32))

        def finish(gi, carry):
            r0, L = group_inputs(gi, False)
            n = grp
            rows, chunks = pl.ds(r0, R), pl.ds(gi * grp, grp)
            g3 = lambda ref: ref[rows, :].reshape(n, C, -1)
            u, w, do = g3(u_s), g3(w_s), g3(do_s)
            L["M"] = g3(M_s)
            state, dS = st_ref[0, 0, chunks], dst_s[chunks]
            v_new = u - _bmm(w, state, B_NN)
            du = _bmm(L["M"], do, B_TN) + _bmm(L["kd"], dS, B_NN)
            dw = -_bmm(du, state, B_NT)
            dqg = _bmm(do, state, B_NT)
            dM = _bmm(do, v_new, B_NT)
            dkd = _bmm(v_new, dS, B_NT)
            dgl_state = jnp.sum(jnp.sum(dS * state, axis=2, keepdims=True), axis=1, keepdims=True) * L["gam"]
            TmT = jnp.swapaxes(L["Tm"], 1, 2)
            dTm = _bmm(du, L["vb"], B_NT) + _bmm(dw, L["kg"], B_NT)
            dvb = _bmm(TmT, du, B_NN)
            dkg = _bmm(TmT, dw, B_NN)
            dA = jnp.where(L["row"] > L["col"], -_bmm(_bmm(TmT, dTm, B_NN), TmT, B_NN), 0.0)
            dKK = dA * L["D"]
            dQK = dM * L["D"]
            dkb = _bmm(dKK, L["k"], B_NN) + dkg * L["Gam"]
            dk = (_bmm(dKK, L["kb"], B_TN) + _bmm(dQK, L["q"], B_TN) + dkd * L["kdec"] + L["beta"] * dkb)
            dq = (_bmm(dQK, L["k"], B_NN) + dqg * L["Gam"]) * (GDN_HEAD_DIM ** -0.5)
            E = dA * L["A"] + dM * L["M"]
            r = jnp.sum(dkd * L["kd"], axis=-1, keepdims=True)
            dgc = (jnp.sum(E, axis=2, keepdims=True) - jnp.sum(jnp.swapaxes(E, 1, 2), axis=2, keepdims=True)
                   + jnp.sum(dkg * L["kg"], axis=-1, keepdims=True) + jnp.sum(dqg * L["qg"], axis=-1, keepdims=True) - r)
            dgl = jnp.sum(r, axis=1, keepdims=True) + dgl_state
            rowc = lax.broadcasted_iota(jnp.int32, (n, C, 1), 1)
            dgc = dgc + jnp.where(rowc == C - 1, dgl, 0.0)
            dbeta = jnp.sum(dkb * L["k"], axis=-1, keepdims=True) + jnp.sum(dvb * L["v"], axis=-1, keepdims=True)
            dq_ref[rows, :] = dq.reshape(R, hd)
            dk_ref[rows, :] = dk.reshape(R, hd)
            dv_ref[rows, :] = (L["beta"] * dvb).reshape(R, hd)
            lane = lax.broadcasted_iota(jnp.int32, (R, LANES), 1)
            dgt_ref[rows, :] += (jnp.where(lane == SM_B + h, dbeta.reshape(R, 1), 0.0)
                                 + jnp.where(lane == SM_A + h, dgc.reshape(R, 1), 0.0))
            return carry

        lax.fori_loop(0, N // grp, finish, 0)

    blk = lambda off: pl.BlockSpec((S, LANES), lambda b, h: (b, off + h))
    rows = lambda: pltpu.VMEM((S, hd), f32)
    return pl.pallas_call(
        body, name="gdn_bwd", grid=(B, GDN_HEADS),
        in_specs=[blk(0), blk(GDN_HEADS), blk(2 * GDN_HEADS), pl.BlockSpec((S, LANES), lambda b, h: (b, 0)),
                  blk(COL_Z // LANES), pl.BlockSpec((1, hd), lambda b, h: (0, 0)), blk(0),
                  pl.BlockSpec((1, 1, N, hd, hd), lambda b, h: (b, h, 0, 0, 0)),
                  pl.BlockSpec((1, 1, S, C), lambda b, h: (b, h, 0, 0)), blk(GDN_HEADS)],
        out_specs=[blk(0), blk(0), blk(0), pl.BlockSpec((S, LANES), lambda b, h: (b, 0)), blk(0),
                   pl.BlockSpec((1, hd), lambda b, h: (0, 0))],
        out_shape=[jax.ShapeDtypeStruct((T, GDN_WIDTH), f32), jax.ShapeDtypeStruct((T, GDN_WIDTH), f32),
                   jax.ShapeDtypeStruct((T, GDN_WIDTH), f32), jax.ShapeDtypeStruct((T, LANES), f32),
                   jax.ShapeDtypeStruct((T, GDN_WIDTH), MXU_DTYPE), jax.ShapeDtypeStruct((1, hd), f32)],
        scratch_shapes=[rows(), rows(), pltpu.VMEM((S, C), f32), pltpu.VMEM((N, 1, LANES), f32), rows(),
                        pltpu.VMEM((N, hd, hd), f32), pltpu.VMEM((N, hd, hd), f32), pltpu.VMEM((N, hd, hd), f32)],
        compiler_params=_cparams("arbitrary", "arbitrary"),
    )(G, G, G, gates, P, g_on, o_raw, states, tm, d_oab)


IN_SPLIT = (0, 1536, 1544, 3080, 3088, 3600)


IN_SHARD = IN_DIM // 4
IN_SHARD_PAD = 928


def align_w_in_t(wt):
    s = IN_SPLIT
    pad = jnp.zeros((IN_ALIGNED - IN_DIM, wt.shape[1]), wt.dtype)
    return jnp.concatenate([wt[s[0]:s[1]], wt[s[2]:s[3]], wt[s[4]:s[5]], wt[s[1]:s[2]], wt[s[3]:s[4]], pad], axis=0)


def unalign_w_in_t(wa):
    return jnp.concatenate([wa[0:1536], wa[COL_SMALL:COL_SMALL + 8], wa[1536:3072],
                            wa[COL_SMALL + 8:COL_SMALL + 16], wa[3072:3584]], axis=0)


IN_SEGMENTS = (((0, 1536), 0), ((1536, 1544), COL_SMALL), ((1544, 3080), 1536), ((3080, 3088), COL_SMALL + 8),
               ((3088, 3600), 3072))


def align_w_in_slots(slots):
    pieces = []
    for (lo, hi), _ in sorted(IN_SEGMENTS, key=lambda seg: seg[1]):
        for k in range(N_CHIPS):
            a, b = max(lo, k * IN_SHARD), min(hi, (k + 1) * IN_SHARD)
            if a < b:
                pieces.append(slots[k, a - k * IN_SHARD:b - k * IN_SHARD])
    pieces.append(jnp.zeros((IN_ALIGNED - IN_DIM, slots.shape[2]), slots.dtype))
    return jnp.concatenate(pieces, axis=0)


def unalign_to_slots(wa):
    slots = []
    for k in range(N_CHIPS):
        lo, hi = k * IN_SHARD, (k + 1) * IN_SHARD
        pieces = []
        for (a, b), first in IN_SEGMENTS:
            x, y = max(a, lo), min(b, hi)
            if x < y:
                pieces.append(wa[first + x - a:first + y - a])
        pieces.append(jnp.zeros((IN_SHARD_PAD - IN_SHARD, wa.shape[1]), wa.dtype))
        slots.append(jnp.concatenate(pieces, axis=0))
    return jnp.stack(slots)


def _lanes_vec(pieces):
    v = jnp.zeros((1, LANES), f32)
    for off, a in pieces:
        v = lax.dynamic_update_slice(v, a.astype(f32), (0, off))
    return v


def local_step(x, mem, target, w, sp, *, B):
    T = x.shape[0]
    S = T // B
    gq8, gk8 = jnp.tile(sp["fox_qnorm_g"], (1, FOX_HEADS)), jnp.tile(sp["fox_knorm_g"], (1, FOX_HEADS))
    go2 = jnp.tile(sp["fox_onorm_g"], (1, 2))
    bias = _lanes_vec([(SM_F, sp["fox_f_bias"]), (SM_A, sp["gdn_dt_bias"])])
    alog = _lanes_vec([(SM_A, sp["gdn_A_log"])])

    h1 = rms_fwd(x, sp["norm_mix_g"], name="rms_mix")
    P = matmul(h1, w["wa_t"], tb=True, name="mm_in", tn=IN_TILE)
    gates = gates_fwd(P, bias, alog, B=B)
    c = gates[:, SM_F:SM_F + FOX_HEADS].reshape(B, S, FOX_HEADS).transpose(0, 2, 1)
    ccol, crow = c[..., None], c.reshape(B, FOX_HEADS, S // FOX_TQ, 1, FOX_TQ)
    qn, kn, vb = fox_prep_fwd(P, gq8, gk8)
    o_raw, o_a, lse = fox_core_fwd(qn, kn, vb, ccol, crow, go2, B=B)
    G = gdn_prep_fwd(P, w["conv_w"], B=B)
    ob_raw, o_b, states, gdn_tm = gdn_fwd(G, gates, P, sp["gdn_onorm_g"], B=B)
    oab = jnp.concatenate([o_a, o_b], axis=1)
    if "late" in w:
        w = {**w, **w["late"](oab)}
    x2, hq = matmul_rows(oab, w["w_out"], (x, sp["norm_xattn_g"]), mode="rms_fwd", name="mm_out_rms")
    hm = rms_fwd(mem, sp["mem_norm_g"], name="rms_mem")
    cq = matmul(hq, w["w_cq"], name="mm_cq")
    ckv = matmul(hm, w["w_ckv"], name="mm_ckv")
    co = xattn_fwd(cq, ckv, sp["xattn_qnorm_g"], sp["xattn_knorm_g"], B=B)
    x3, hf = matmul_rows(co, w["w_co"], (x2, sp["norm_mlp_g"]), mode="rms_fwd", name="mm_co_rms")
    act = matmul(hf, w["w_mlp1"], b_stacked=True, relu2_out=True, out_dtype=MXU_DTYPE, name="mm_mlp1")
    dy, dy_op, loss = matmul_rows(act, w["w_mlp2"], (x3, target), mode="loss", name="mm_mlp2_loss")

    da = matmul(dy_op, w["w_mlp2"], tb=True, relu2_bwd_aux=act, out_dtype=MXU_DTYPE, name="mm_d_act")
    g_mlp2 = matmul(act, dy_op, ta=True, out_dtype=WIRE_DTYPE, name="mm_g_mlp2")
    g_mlp1 = matmul(hf, da, ta=True, out_stacked=True, out_dtype=WIRE_DTYPE, name="mm_g_mlp1")
    by_rows = lambda g: g.reshape(N_CHIPS, g.shape[0] // N_CHIPS, g.shape[1])
    early = w.get("grads_ready", lambda grads: jnp.zeros((1, 1), f32))
    tok = early(dict(w_mlp1=g_mlp1, w_mlp2=by_rows(g_mlp2)))[0, 0]
    dx3, g_norm_mlp = matmul_rows(da, w["w_mlp1"], (x3, sp["norm_mlp_g"] + tok, dy), mode="rms_bwd", tb=True,
                                  b_stacked=True, name="mm_d_hf_rms")
    dco = matmul(dx3, w["w_co"], tb=True, name="mm_d_co")
    g_co = matmul(co, dx3, ta=True, out_dtype=WIRE_DTYPE, name="mm_g_co")
    g_co = g_co.reshape(XATTN_WIDTH, N_CHIPS, D_MODEL // N_CHIPS).transpose(1, 0, 2)
    dcq, dckv, g_xq, g_xk = xattn_bwd(cq, ckv, sp["xattn_qnorm_g"], sp["xattn_knorm_g"], dco, B=B)
    g_cq = matmul(hq, dcq, ta=True, out_dtype=WIRE_DTYPE, name="mm_g_cq")
    g_ckv = matmul(hm, dckv, ta=True, out_dtype=WIRE_DTYPE, name="mm_g_ckv")
    _, g_mem_norm = matmul_rows(dckv, w["w_ckv"], (mem, sp["mem_norm_g"], None), mode="rms_bwd", tb=True, name="mm_d_hm_rms")
    dx2, g_norm_xattn = matmul_rows(dcq, w["w_cq"], (x2, sp["norm_xattn_g"], dx3), mode="rms_bwd", tb=True, name="mm_d_hq_rms")
    doab = matmul(dx2, w["w_out"], tb=True, name="mm_d_oab")
    g_out = matmul(oab, dx2, ta=True, out_dtype=WIRE_DTYPE, name="mm_g_out")
    tok = early(dict(w_co=g_co, w_cq=by_rows(g_cq), w_ckv=by_rows(g_ckv), w_out=by_rows(g_out)))[0, 0]
    dqn, dkn, dv_f, dckey, dcrow, dgo2 = fox_core_bwd(qn, kn, vb, ccol, crow, go2 + tok, o_raw, lse, doab, B=B)
    dq_f, dk_f, dgq8, dgk8 = fox_prep_bwd(P, gq8, gk8, dqn, dkn)
    dGq, dGk, dGv, dgt, dz, g_gdn_on = gdn_bwd(G, gates, P, sp["gdn_onorm_g"], ob_raw, states, gdn_tm, doab, B=B)
    dPg, g_conv = gdn_prep_bwd(P, w["conv_w"], dGq, dGk, dGv, B=B)
    dc = (dckey[:, :, 0, :] + dcrow.reshape(B, FOX_HEADS, S)).transpose(0, 2, 1).reshape(T, FOX_HEADS)
    dgates = dgt + jnp.pad(dc, ((0, 0), (SM_F, LANES - SM_F - FOX_HEADS)))
    dsmall, par = gates_bwd(P, bias, alog, dgates, B=B)
    dP = jnp.concatenate([dq_f, dk_f, dv_f, dPg, dz, dsmall, jnp.zeros((T, IN_ALIGNED - COL_SMALL - LANES), MXU_DTYPE)], axis=1)
    g_wa = matmul(dP, h1, ta=True, out_dtype=WIRE_DTYPE, name="mm_g_in", tm=IN_TILE)
    g_in = unalign_to_slots(g_wa)
    tok = early(dict(w_in=g_in))[0, 0]
    dx, g_norm_mix = matmul_rows(dP, w["wa_t"], (x, sp["norm_mix_g"] + tok, dx2), mode="rms_bwd", tk=IN_TILE,
                                 name="mm_d_h1_rms")

    fold = lambda g: jnp.sum(g.reshape(-1, FOX_HEAD_DIM), axis=0, keepdims=True)
    big = dict(w_in=g_in, w_out=by_rows(g_out), w_cq=by_rows(g_cq), w_ckv=by_rows(g_ckv), w_co=g_co, w_mlp1=g_mlp1,
               w_mlp2=by_rows(g_mlp2))
    small = dict(norm_mix_g=g_norm_mix, fox_qnorm_g=fold(dgq8), fox_knorm_g=fold(dgk8),
                 fox_f_bias=par[0:1, SM_F:SM_F + FOX_HEADS], fox_onorm_g=fold(dgo2), gdn_conv_w=g_conv,
                 gdn_A_log=par[1:2, SM_A:SM_A + GDN_HEADS], gdn_dt_bias=par[0:1, SM_A:SM_A + GDN_HEADS],
                 gdn_onorm_g=g_gdn_on, norm_xattn_g=g_norm_xattn, mem_norm_g=g_mem_norm,
                 xattn_qnorm_g=g_xq, xattn_knorm_g=g_xk, norm_mlp_g=g_norm_mlp)
    return loss, dx, big, small


MESH_IDS = pl.DeviceIdType.MESH
N_CHIPS = 4
HBM_SPEC = pl.BlockSpec(memory_space=pltpu.HBM)
PACK_ROWS = 30720
PACK_HALF = PACK_ROWS // 2
PACK_BLOCK = 3072


def _place():
    return lax.axis_index("x"), lax.axis_index("y"), lax.axis_index("c")


def _other_chips(x, y):
    return [(1 - x, y), (x, 1 - y), (1 - x, 1 - y)]


def _remote(src, dst, send_sem, recv_sem, to):
    return pltpu.make_async_remote_copy(src_ref=src, dst_ref=dst, send_sem=send_sem, recv_sem=recv_sem,
                                        device_id=to, device_id_type=MESH_IDS)


def all_gather_shards(packed):
    half = PACK_HALF

    def body(src_ref, out_ref, send_sems, recv_sems):
        x, y, c = _place()
        me_chip = 2 * x + y
        sibling = (x, y, 1 - c)
        chips = _other_chips(x, y)

        def rows(chip, core):
            return out_ref.at[chip, pl.ds(core * half, half), :]

        sends = [_remote(src_ref.at[pl.ds(c * half, half), :], rows(me_chip, c), send_sems.at[j], recv_sems.at[j], (px, py, c))
                 for j, (px, py) in enumerate(chips)]
        for cp in sends:
            cp.start()
        passed = []
        for j, (px, py) in enumerate(chips):
            theirs = rows(2 * px + py, c)
            _remote(theirs, theirs, send_sems.at[j], recv_sems.at[j], (px, py, c)).wait_recv()
            cp = _remote(theirs, theirs, send_sems.at[3 + j], recv_sems.at[3 + j], sibling)
            cp.start()
            passed.append(cp)
        for j, (px, py) in enumerate(chips):
            theirs = rows(2 * px + py, 1 - c)
            _remote(theirs, theirs, send_sems.at[3 + j], recv_sems.at[3 + j], sibling).wait_recv()
        for cp in sends + passed:
            cp.wait_send()

    return pl.pallas_call(
        body, name="all_gather_shards", in_specs=[HBM_SPEC], out_specs=HBM_SPEC,
        out_shape=jax.ShapeDtypeStruct((N_CHIPS,) + packed.shape, packed.dtype),
        scratch_shapes=[pltpu.SemaphoreType.DMA((6,)), pltpu.SemaphoreType.DMA((6,))],
    )(packed)


def exchange_core_halves(G):
    half = PACK_HALF

    def body(g_ref, land_ref, send_sem, recv_sem):
        x, y, c = _place()
        cp = _remote(g_ref.at[:, pl.ds((1 - c) * half, half), :], land_ref, send_sem, recv_sem, (x, y, 1 - c))
        cp.start()
        cp.wait()

    return pl.pallas_call(
        body, name="exchange_core_halves", in_specs=[HBM_SPEC], out_specs=HBM_SPEC,
        out_shape=jax.ShapeDtypeStruct((N_CHIPS, half, LANES), G.dtype),
        scratch_shapes=[pltpu.SemaphoreType.DMA(()), pltpu.SemaphoreType.DMA(())],
    )(G)


def add_core_halves(G, land, core):
    nb = PACK_HALF // PACK_BLOCK

    def body(c_ref, g_ref, l_ref, o_ref):
        o_ref[...] = (g_ref[...].astype(f32) + l_ref[...].astype(f32)).astype(o_ref.dtype)

    blk = (1, PACK_BLOCK, LANES)
    return pl.pallas_call(
        body, name="add_core_halves",
        grid_spec=pltpu.PrefetchScalarGridSpec(
            num_scalar_prefetch=1, grid=(N_CHIPS, nb),
            in_specs=[pl.BlockSpec(blk, lambda k, i, c_ref: (k, c_ref[0] * nb + i, 0)),
                      pl.BlockSpec(blk, lambda k, i, c_ref: (k, i, 0))],
            out_specs=pl.BlockSpec(blk, lambda k, i, c_ref: (k, i, 0))),
        out_shape=jax.ShapeDtypeStruct(land.shape, land.dtype),
        compiler_params=_cparams("parallel", "parallel"),
    )(core, G, land)


def scatter_to_chips(part):
    def body(p_ref, land_ref, send_sems, recv_sems):
        x, y, c = _place()
        me_chip = 2 * x + y
        chips = _other_chips(x, y)
        sends = [_remote(p_ref.at[2 * px + py], land_ref.at[me_chip], send_sems.at[j], recv_sems.at[j], (px, py, c))
                 for j, (px, py) in enumerate(chips)]
        for cp in sends:
            cp.start()
        for j, (px, py) in enumerate(chips):
            slot = land_ref.at[2 * px + py]
            _remote(slot, slot, send_sems.at[j], recv_sems.at[j], (px, py, c)).wait_recv()
        for cp in sends:
            cp.wait_send()

    return pl.pallas_call(
        body, name="scatter_to_chips", in_specs=[HBM_SPEC], out_specs=HBM_SPEC,
        out_shape=jax.ShapeDtypeStruct(part.shape, part.dtype),
        scratch_shapes=[pltpu.SemaphoreType.DMA((3,)), pltpu.SemaphoreType.DMA((3,))],
    )(part)


def sum_chips(part, land, order):
    nb = PACK_HALF // PACK_BLOCK

    def body(order_ref, p_ref, l1_ref, l2_ref, l3_ref, o_ref):
        o_ref[...] = ((p_ref[0].astype(f32) + l1_ref[0].astype(f32)) + l2_ref[0].astype(f32)) + l3_ref[0].astype(f32)

    slot = lambda j: pl.BlockSpec((1, PACK_BLOCK, LANES), lambda i, order_ref: (order_ref[j], i, 0))
    return pl.pallas_call(
        body, name="sum_chips",
        grid_spec=pltpu.PrefetchScalarGridSpec(
            num_scalar_prefetch=1, grid=(nb,), in_specs=[slot(0), slot(1), slot(2), slot(3)],
            out_specs=pl.BlockSpec((PACK_BLOCK, LANES), lambda i, order_ref: (i, 0))),
        out_shape=jax.ShapeDtypeStruct((PACK_HALF, LANES), f32),
        compiler_params=_cparams("parallel"),
    )(order, part, land, land, land)


def swap_core_halves(red):
    def body(r_ref, out_ref, send_sem, recv_sem):
        x, y, c = _place()
        cp = _remote(r_ref, out_ref, send_sem, recv_sem, (x, y, 1 - c))
        cp.start()
        cp.wait()

    return pl.pallas_call(
        body, name="swap_core_halves", in_specs=[HBM_SPEC], out_specs=HBM_SPEC,
        out_shape=jax.ShapeDtypeStruct(red.shape, red.dtype),
        scratch_shapes=[pltpu.SemaphoreType.DMA(()), pltpu.SemaphoreType.DMA(())],
    )(red)


def _half(ref, core):
    rows = ref.shape[-2] // 2
    return ref.at[(slice(None),) * (len(ref.shape) - 2) + (pl.ds(core * rows, rows), slice(None))]


def gather_weights(shards, conv):
    n = len(shards)

    def body(*refs):
        src, conv_src = refs[:n], refs[n]
        out, conv_out = refs[n + 1:2 * n + 1], refs[2 * n + 1]
        send_sems, recv_sems = refs[2 * n + 2], refs[2 * n + 3]
        x, y, c = _place()
        me_chip = 2 * x + y
        sibling = (x, y, 1 - c)
        chips = _other_chips(x, y)
        sends = []
        for a in range(n):
            for j, (px, py) in enumerate(chips):
                sends.append(_remote(_half(src[a], c), _half(out[a].at[me_chip], c),
                                     send_sems.at[6 * a + j], recv_sems.at[6 * a + j], (px, py, c)))
        for j, (px, py) in enumerate(chips):
            sends.append(_remote(conv_src, conv_out.at[me_chip], send_sems.at[6 * n + j], recv_sems.at[6 * n + j], (px, py, c)))
        for cp in sends:
            cp.start()
        passed = []
        for a in range(n):
            for j, (px, py) in enumerate(chips):
                theirs = _half(out[a].at[2 * px + py], c)
                _remote(theirs, theirs, send_sems.at[6 * a + j], recv_sems.at[6 * a + j], (px, py, c)).wait_recv()
                cp = _remote(theirs, theirs, send_sems.at[6 * a + 3 + j], recv_sems.at[6 * a + 3 + j], sibling)
                cp.start()
                passed.append(cp)
        for j, (px, py) in enumerate(chips):
            theirs = conv_out.at[2 * px + py]
            _remote(theirs, theirs, send_sems.at[6 * n + j], recv_sems.at[6 * n + j], (px, py, c)).wait_recv()
        for a in range(n):
            for j, (px, py) in enumerate(chips):
                theirs = _half(out[a].at[2 * px + py], 1 - c)
                _remote(theirs, theirs, send_sems.at[6 * a + 3 + j], recv_sems.at[6 * a + 3 + j], sibling).wait_recv()
        for cp in sends + passed:
            cp.wait_send()

    return pl.pallas_call(
        body, name="gather_weights", in_specs=[HBM_SPEC] * (n + 1), out_specs=[HBM_SPEC] * (n + 1),
        out_shape=[jax.ShapeDtypeStruct((N_CHIPS,) + s.shape, s.dtype) for s in list(shards) + [conv]],
        scratch_shapes=[pltpu.SemaphoreType.DMA((6 * n + 3,)), pltpu.SemaphoreType.DMA((6 * n + 3,))],
    )(*shards, conv)


SEM_SPEC = pl.BlockSpec(memory_space=pltpu.SEMAPHORE)
SPLIT_EFFECT = pltpu.SideEffectType.DATAFLOW_SIDE_EFFECTING


def _gather_async_copies(src, land, send_sems, recv_sems, x, y, c):
    me_chip = 2 * x + y
    sends, arrivals = [], []
    for a in range(len(src)):
        for j, (px, py) in enumerate(_other_chips(x, y)):
            for core in range(2):
                sends.append(_remote(_half(src[a], c), _half(land[a].at[me_chip], c), send_sems.at[6 * a + 2 * j + core],
                                     recv_sems.at[6 * a + 2 * j + c], (px, py, core)))
                theirs = _half(land[a].at[2 * px + py], core)
                arrivals.append(_remote(theirs, theirs, send_sems.at[6 * a + 2 * j + core],
                                        recv_sems.at[6 * a + 2 * j + core], (px, py, core)))
    return sends, arrivals


def gather_weights_start(shards, after):
    n = len(shards)

    def body(*refs):
        src, land = refs[:n], refs[n:2 * n]
        send_sems, recv_sems, token = refs[2 * n + 1], refs[2 * n + 2], refs[4 * n + 3]
        x, y, c = _place()
        for cp in _gather_async_copies(src, land, send_sems, recv_sems, x, y, c)[0]:
            cp.start()
        token[...] = jnp.zeros_like(token)

    zones = [pltpu.with_memory_space_constraint(lax.empty((N_CHIPS,) + s.shape, s.dtype), pltpu.HBM) for s in shards]
    srcs = [pltpu.with_memory_space_constraint(s, pltpu.HBM) for s in shards]
    out = pl.pallas_call(
        body, name="gather_weights_start",
        out_shape=[pltpu.SemaphoreType.DMA((6 * n,)), pltpu.SemaphoreType.DMA((6 * n,))]
        + [pltpu.HBM(s.shape, s.dtype) for s in shards] + [pltpu.HBM(z.shape, z.dtype) for z in zones]
        + [jax.ShapeDtypeStruct((8, LANES), f32)],
        in_specs=[HBM_SPEC] * (2 * n) + [pl.BlockSpec(memory_space=pl.ANY)],
        out_specs=[SEM_SPEC, SEM_SPEC] + [HBM_SPEC] * (2 * n) + [pl.BlockSpec(memory_space=pltpu.VMEM)],
        input_output_aliases={i: 2 + i for i in range(2 * n)},
        compiler_params=pltpu.CompilerParams(has_side_effects=SPLIT_EFFECT),
    )(*srcs, *zones, after)
    return out[0], out[1], out[2:2 + n], out[2 + n:2 + 2 * n], out[-1]


def gather_weights_wait(send_sems, recv_sems, shards, zones, after):
    n = len(shards)

    def body(*refs):
        src, land = refs[:n], refs[n:2 * n]
        send_sems, recv_sems = refs[2 * n], refs[2 * n + 1]
        x, y, c = _place()
        sends, arrivals = _gather_async_copies(src, land, send_sems, recv_sems, x, y, c)
        for cp in sends:
            cp.wait_send()
        for cp in arrivals:
            cp.wait_recv()

    out = pl.pallas_call(
        body, name="gather_weights_wait",
        out_shape=[pltpu.HBM(s.shape, s.dtype) for s in shards] + [pltpu.HBM(z.shape, z.dtype) for z in zones],
        in_specs=[HBM_SPEC] * (2 * n) + [SEM_SPEC, SEM_SPEC, pl.BlockSpec(memory_space=pl.ANY)],
        out_specs=[HBM_SPEC] * (2 * n),
        input_output_aliases={i: i for i in range(2 * n)},
        compiler_params=pltpu.CompilerParams(has_side_effects=SPLIT_EFFECT),
    )(*shards, *zones, send_sems, recv_sems, after)
    return out[n:]


def swap_grad_halves(grads, *, name):
    n = len(grads)

    def body(*refs):
        g, land, send_sems, recv_sems = refs[:n], refs[n:2 * n], refs[2 * n], refs[2 * n + 1]
        x, y, c = _place()
        copies = [_remote(_half(g[a], 1 - c), land[a], send_sems.at[a], recv_sems.at[a], (x, y, 1 - c)) for a in range(n)]
        for cp in copies:
            cp.start()
        for cp in copies:
            cp.wait()

    return pl.pallas_call(
        body, name=name, in_specs=[HBM_SPEC] * n, out_specs=[HBM_SPEC] * n,
        out_shape=[jax.ShapeDtypeStruct((N_CHIPS, g.shape[1] // 2, g.shape[2]), g.dtype) for g in grads],
        scratch_shapes=[pltpu.SemaphoreType.DMA((n,)), pltpu.SemaphoreType.DMA((n,))],
    )(*grads)


GRAD_ROWS = 512


def add_grad_halves(g, land, core, *, name):
    _, half, cols = land.shape
    tr = GRAD_ROWS if half % GRAD_ROWS == 0 else half
    nb = half // tr

    def body(c_ref, g_ref, l_ref, o_ref):
        o_ref[...] = (g_ref[...].astype(f32) + l_ref[...].astype(f32)).astype(o_ref.dtype)

    blk = (1, tr, cols)
    return pl.pallas_call(
        body, name=name,
        grid_spec=pltpu.PrefetchScalarGridSpec(
            num_scalar_prefetch=1, grid=(N_CHIPS, nb),
            in_specs=[pl.BlockSpec(blk, lambda k, i, c_ref: (k, c_ref[0] * nb + i, 0)),
                      pl.BlockSpec(blk, lambda k, i, c_ref: (k, i, 0))],
            out_specs=pl.BlockSpec(blk, lambda k, i, c_ref: (k, i, 0))),
        out_shape=jax.ShapeDtypeStruct(land.shape, land.dtype),
        compiler_params=_cparams("parallel", "parallel"),
    )(core, g, land)


def scatter_grads(parts):
    n = len(parts)

    def body(*refs):
        p, land, send_sems, recv_sems = refs[:n], refs[n:2 * n], refs[2 * n], refs[2 * n + 1]
        x, y, c = _place()
        me_chip = 2 * x + y
        chips = _other_chips(x, y)
        sends = [_remote(p[a].at[2 * px + py], land[a].at[me_chip], send_sems.at[3 * a + j], recv_sems.at[3 * a + j], (px, py, c))
                 for a in range(n) for j, (px, py) in enumerate(chips)]
        for cp in sends:
            cp.start()
        for a in range(n):
            for j, (px, py) in enumerate(chips):
                slot = land[a].at[2 * px + py]
                _remote(slot, slot, send_sems.at[3 * a + j], recv_sems.at[3 * a + j], (px, py, c)).wait_recv()
        for cp in sends:
            cp.wait_send()

    return pl.pallas_call(
        body, name="scatter_grads", in_specs=[HBM_SPEC] * n, out_specs=[HBM_SPEC] * n,
        out_shape=[jax.ShapeDtypeStruct(p.shape, p.dtype) for p in parts],
        scratch_shapes=[pltpu.SemaphoreType.DMA((3 * n,)), pltpu.SemaphoreType.DMA((3 * n,))],
    )(*parts)


def _scatter_async_copies(parts, land, send_sems, recv_sems, x, y, c):
    me_chip = 2 * x + y
    sends, arrivals = [], []
    for a in range(len(parts)):
        for j, (px, py) in enumerate(_other_chips(x, y)):
            sems = (send_sems.at[3 * a + j], recv_sems.at[3 * a + j], (px, py, c))
            sends.append(_remote(parts[a].at[2 * px + py], land[a].at[me_chip], *sems))
            slot = land[a].at[2 * px + py]
            arrivals.append(_remote(slot, slot, *sems))
    return sends, arrivals


def scatter_grads_start(parts, *, name):
    n = len(parts)

    def body(*refs):
        p, land = refs[:n], refs[n:2 * n]
        send_sems, recv_sems, token = refs[2 * n], refs[2 * n + 1], refs[4 * n + 2]
        x, y, c = _place()
        for cp in _scatter_async_copies(p, land, send_sems, recv_sems, x, y, c)[0]:
            cp.start()
        token[...] = jnp.zeros_like(token)

    zones = [pltpu.with_memory_space_constraint(lax.empty(p.shape, p.dtype), pltpu.HBM) for p in parts]
    srcs = [pltpu.with_memory_space_constraint(p, pltpu.HBM) for p in parts]
    hbm = [pltpu.HBM(p.shape, p.dtype) for p in parts]
    out = pl.pallas_call(
        body, name=name,
        out_shape=[pltpu.SemaphoreType.DMA((3 * n,)), pltpu.SemaphoreType.DMA((3 * n,))] + hbm + hbm
        + [jax.ShapeDtypeStruct((8, LANES), f32)],
        in_specs=[HBM_SPEC] * (2 * n),
        out_specs=[SEM_SPEC, SEM_SPEC] + [HBM_SPEC] * (2 * n) + [pl.BlockSpec(memory_space=pltpu.VMEM)],
        input_output_aliases={i: 2 + i for i in range(2 * n)},
        compiler_params=pltpu.CompilerParams(has_side_effects=SPLIT_EFFECT),
    )(*srcs, *zones)
    return out[0], out[1], out[2:2 + n], out[2 + n:2 + 2 * n], out[-1]


def scatter_grads_wait(send_sems, recv_sems, parts, zones, after, *, name):
    n = len(parts)

    def body(*refs):
        p, land = refs[:n], refs[n:2 * n]
        x, y, c = _place()
        sends, arrivals = _scatter_async_copies(p, land, refs[2 * n], refs[2 * n + 1], x, y, c)
        for cp in sends:
            cp.wait_send()
        for cp in arrivals:
            cp.wait_recv()

    hbm = [pltpu.HBM(p.shape, p.dtype) for p in parts]
    out = pl.pallas_call(
        body, name=name, out_shape=hbm + hbm,
        in_specs=[HBM_SPEC] * (2 * n) + [SEM_SPEC, SEM_SPEC, pl.BlockSpec(memory_space=pl.ANY)],
        out_specs=[HBM_SPEC] * (2 * n),
        input_output_aliases={i: i for i in range(2 * n)},
        compiler_params=pltpu.CompilerParams(has_side_effects=SPLIT_EFFECT),
    )(*parts, *zones, send_sems, recv_sems, after)
    return out[:n], out[n:]


def sum_grads(part, land, order, *, name):
    _, half, cols = part.shape
    tr = GRAD_ROWS if half % GRAD_ROWS == 0 else half

    def body(order_ref, p_ref, l1_ref, l2_ref, l3_ref, o_ref):
        o_ref[...] = ((p_ref[0].astype(f32) + l1_ref[0].astype(f32)) + l2_ref[0].astype(f32)) + l3_ref[0].astype(f32)

    slot = lambda j: pl.BlockSpec((1, tr, cols), lambda i, order_ref: (order_ref[j], i, 0))
    return pl.pallas_call(
        body, name=name,
        grid_spec=pltpu.PrefetchScalarGridSpec(
            num_scalar_prefetch=1, grid=(half // tr,), in_specs=[slot(0), slot(1), slot(2), slot(3)],
            out_specs=pl.BlockSpec((tr, cols), lambda i, order_ref: (i, 0))),
        out_shape=jax.ShapeDtypeStruct((half, cols), f32),
        compiler_params=_cparams("parallel"),
    )(order, part, land, land, land)


def _peer(x, y, c, r):
    return ((1 - x) if r & 4 else x, (1 - y) if r & 2 else y, (1 - c) if r & 1 else c)


def _reduce_async_copies(grads, land, send_sems, recv_sems, x, y, c):
    me = 4 * x + 2 * y + c
    sends, arrivals = [], []
    for a in range(len(grads)):
        for r in range(1, N_DEV):
            px, py, pc = _peer(x, y, c, r)
            sems = (send_sems.at[7 * a + r - 1], recv_sems.at[7 * a + r - 1], (px, py, pc))
            sends.append(_remote(_half(grads[a].at[2 * px + py], pc), land[a].at[me], *sems))
            slot = land[a].at[4 * px + 2 * py + pc]
            arrivals.append(_remote(slot, slot, *sems))
    return sends, arrivals


def reduce_grads_start(grads, *, name):
    n = len(grads)

    def body(*refs):
        g, land = refs[:n], refs[n:2 * n]
        send_sems, recv_sems, token = refs[2 * n], refs[2 * n + 1], refs[4 * n + 2]
        x, y, c = _place()
        for cp in _reduce_async_copies(g, land, send_sems, recv_sems, x, y, c)[0]:
            cp.start()
        token[...] = jnp.zeros_like(token)

    zones = [pltpu.with_memory_space_constraint(lax.empty((N_DEV, g.shape[1] // 2, g.shape[2]), g.dtype), pltpu.HBM)
             for g in grads]
    srcs = [pltpu.with_memory_space_constraint(g, pltpu.HBM) for g in grads]
    out = pl.pallas_call(
        body, name=name,
        out_shape=[pltpu.SemaphoreType.DMA((7 * n,)), pltpu.SemaphoreType.DMA((7 * n,))]
        + [pltpu.HBM(g.shape, g.dtype) for g in grads] + [pltpu.HBM(z.shape, z.dtype) for z in zones]
        + [jax.ShapeDtypeStruct((8, LANES), f32)],
        in_specs=[HBM_SPEC] * (2 * n),
        out_specs=[SEM_SPEC, SEM_SPEC] + [HBM_SPEC] * (2 * n) + [pl.BlockSpec(memory_space=pltpu.VMEM)],
        input_output_aliases={i: 2 + i for i in range(2 * n)},
        compiler_params=pltpu.CompilerParams(has_side_effects=SPLIT_EFFECT),
    )(*srcs, *zones)
    return out[0], out[1], out[2:2 + n], out[2 + n:2 + 2 * n], out[-1]


def reduce_grads_wait(send_sems, recv_sems, grads, zones, after, *, name):
    n = len(grads)

    def body(*refs):
        g, land = refs[:n], refs[n:2 * n]
        x, y, c = _place()
        sends, arrivals = _reduce_async_copies(g, land, refs[2 * n], refs[2 * n + 1], x, y, c)
        for cp in sends:
            cp.wait_send()
        for cp in arrivals:
            cp.wait_recv()

    hbm = [pltpu.HBM(a.shape, a.dtype) for a in list(grads) + list(zones)]
    out = pl.pallas_call(
        body, name=name, out_shape=hbm,
        in_specs=[HBM_SPEC] * (2 * n) + [SEM_SPEC, SEM_SPEC, pl.BlockSpec(memory_space=pl.ANY)],
        out_specs=[HBM_SPEC] * (2 * n),
        input_output_aliases={i: i for i in range(2 * n)},
        compiler_params=pltpu.CompilerParams(has_side_effects=SPLIT_EFFECT),
    )(*grads, *zones, send_sems, recv_sems, after)
    return out[:n], out[n:]


def sum_partials(g, land, where, *, name):
    _, half, cols = land.shape
    tr = GRAD_ROWS if half % GRAD_ROWS == 0 else half
    nb = half // tr

    def body(where_ref, g_ref, *rest):
        o_ref = rest[-1]
        acc = g_ref[0].astype(f32)
        for l_ref in rest[:-1]:
            acc = acc + l_ref[0].astype(f32)
        o_ref[...] = acc

    blk = (1, tr, cols)
    slot = lambda j: pl.BlockSpec(blk, lambda i, where_ref: (where_ref[2 + j], i, 0))
    return pl.pallas_call(
        body, name=name,
        grid_spec=pltpu.PrefetchScalarGridSpec(
            num_scalar_prefetch=1, grid=(nb,),
            in_specs=[pl.BlockSpec(blk, lambda i, where_ref: (where_ref[0], where_ref[1] * nb + i, 0))]
            + [slot(j) for j in range(N_DEV - 1)],
            out_specs=pl.BlockSpec((tr, cols), lambda i, where_ref: (i, 0))),
        out_shape=jax.ShapeDtypeStruct((half, cols), f32),
        compiler_params=_cparams("parallel"),
    )(where, g, *([land] * (N_DEV - 1)))


def swap_reduced_halves(mine, *, name):
    n = len(mine)

    def body(*refs):
        r, out, send_sems, recv_sems = refs[:n], refs[n:2 * n], refs[2 * n], refs[2 * n + 1]
        x, y, c = _place()
        copies = [_remote(r[a], out[a], send_sems.at[a], recv_sems.at[a], (x, y, 1 - c)) for a in range(n)]
        for cp in copies:
            cp.start()
        for cp in copies:
            cp.wait()

    return pl.pallas_call(
        body, name=name, in_specs=[HBM_SPEC] * n, out_specs=[HBM_SPEC] * n,
        out_shape=[jax.ShapeDtypeStruct(r.shape, r.dtype) for r in mine],
        scratch_shapes=[pltpu.SemaphoreType.DMA((n,)), pltpu.SemaphoreType.DMA((n,))],
    )(*mine)


def adamw_halves(w, mine, theirs, m, v, core, *, name):
    R, C = w.shape
    tr = min(GRAD_ROWS, R // 2)
    half_nb = R // 2 // tr

    def body(c_ref, w_ref, a_ref, b_ref, m_ref, v_ref, g_ref, d_ref, nm_ref, nv_ref):
        low = pl.program_id(0) < half_nb
        gv = jnp.where(low == (c_ref[0] == 0), a_ref[...], b_ref[...])
        nm = ADAM_B1 * m_ref[...] + (1.0 - ADAM_B1) * gv
        nv = ADAM_B2 * v_ref[...] + (1.0 - ADAM_B2) * jnp.square(gv)
        m_hat = nm / (1.0 - ADAM_B1 ** ADAM_STEP)
        v_hat = nv / (1.0 - ADAM_B2 ** ADAM_STEP)
        g_ref[...] = gv
        d_ref[...] = -ADAM_LR * (m_hat / (jnp.sqrt(v_hat) + ADAM_EPS) + ADAM_WD * w_ref[...])
        nm_ref[...] = nm
        nv_ref[...] = nv

    full = pl.BlockSpec((tr, C), lambda i, c_ref: (i, 0))
    part = pl.BlockSpec((tr, C), lambda i, c_ref: (i % half_nb, 0))
    out = jax.ShapeDtypeStruct((R, C), f32)
    return pl.pallas_call(
        body, name=name,
        grid_spec=pltpu.PrefetchScalarGridSpec(
            num_scalar_prefetch=1, grid=(2 * half_nb,), in_specs=[full, part, part, full, full], out_specs=[full] * 4),
        out_shape=[out] * 4, compiler_params=_cparams("parallel"),
    )(core, w, mine, theirs, m, v)


N_DEV = 8


def all_reduce_small(v):
    def body(src_ref, out_ref, land_ref, send_sems, recv_sems):
        x, y, c = _place()
        me = 4 * x + 2 * y + c
        copies = []
        for r in range(1, N_DEV):
            peer = ((1 - x) if r & 4 else x, (1 - y) if r & 2 else y, (1 - c) if r & 1 else c)
            copies.append(_remote(src_ref, land_ref.at[r], send_sems.at[r - 1], recv_sems.at[r - 1], peer))
        for cp in copies:
            cp.start()
        land_ref[0] = src_ref[...]
        for cp in copies:
            cp.wait()
        acc = land_ref[me]
        for d in range(1, N_DEV):
            acc = acc + land_ref[jnp.bitwise_xor(me, d)]
        out_ref[...] = acc

    vm = pl.BlockSpec(memory_space=pltpu.VMEM)
    return pl.pallas_call(
        body, name="all_reduce_small", in_specs=[vm], out_specs=vm,
        out_shape=jax.ShapeDtypeStruct(v.shape, v.dtype),
        scratch_shapes=[pltpu.VMEM((N_DEV,) + v.shape, v.dtype),
                        pltpu.SemaphoreType.DMA((N_DEV - 1,)), pltpu.SemaphoreType.DMA((N_DEV - 1,))],
    )(v)


def adamw(w, g, m, v, *, name, tr=None, tc=None):
    R, C = w.shape
    if tc is None:
        tr, tc = min(tr, R), C
        blk = pl.BlockSpec((tr, C), lambda i: (i, 0))
    else:
        tr = R
        blk = pl.BlockSpec((R, tc), lambda i: (0, i))

    def body(w_ref, g_ref, m_ref, v_ref, d_ref, nm_ref, nv_ref):
        gv = g_ref[...]
        nm = ADAM_B1 * m_ref[...] + (1.0 - ADAM_B1) * gv
        nv = ADAM_B2 * v_ref[...] + (1.0 - ADAM_B2) * jnp.square(gv)
        m_hat = nm / (1.0 - ADAM_B1 ** ADAM_STEP)
        v_hat = nv / (1.0 - ADAM_B2 ** ADAM_STEP)
        d_ref[...] = -ADAM_LR * (m_hat / (jnp.sqrt(v_hat) + ADAM_EPS) + ADAM_WD * w_ref[...])
        nm_ref[...] = nm
        nv_ref[...] = nv

    out = jax.ShapeDtypeStruct((R, C), f32)
    return pl.pallas_call(
        body, name=name, grid=((R // tr) * (C // tc),), in_specs=[blk] * 4, out_specs=[blk] * 3, out_shape=[out] * 3,
        compiler_params=_cparams("parallel"),
    )(w, g, m, v)


BIG_SHARDS = (("w_in", (1024, 900), True), ("w_out", (256, 1024), False), ("w_cq", (256, 512), False),
              ("w_ckv", (256, 1024), False), ("w_co", (512, 256), True), ("w_mlp1", (1024, 1024), True),
              ("w_mlp2", (1024, 1024), False))
CONV_SHARD = (CONV_WIDTH, 3 * GDN_WIDTH // N_CHIPS)
SMALL_DIMS = (("norm_mix_g", 1024), ("fox_qnorm_g", 64), ("fox_knorm_g", 64), ("fox_f_bias", 8), ("fox_onorm_g", 64),
              ("gdn_A_log", 4), ("gdn_dt_bias", 4), ("gdn_onorm_g", 128), ("norm_xattn_g", 1024), ("mem_norm_g", 1024),
              ("xattn_qnorm_g", 128), ("xattn_knorm_g", 128), ("norm_mlp_g", 1024))
WEIGHT_ORDER = ("norm_mix_g", "w_in", "fox_qnorm_g", "fox_knorm_g", "fox_f_bias", "fox_onorm_g", "gdn_conv_w", "gdn_A_log",
                "gdn_dt_bias", "gdn_onorm_g", "w_out", "norm_xattn_g", "mem_norm_g", "w_cq", "w_ckv", "xattn_qnorm_g",
                "xattn_knorm_g", "w_co", "norm_mlp_g", "w_mlp1", "w_mlp2")


def _pack_rows(pieces, rows, lead=()):
    cat = jnp.concatenate([p.reshape(lead + (-1,)) for p in pieces], axis=-1)
    cat = jnp.pad(cat, [(0, 0)] * len(lead) + [(0, rows * LANES - cat.shape[-1])])
    return cat.reshape(lead + (rows, LANES))


def _unpack_rows(buf, sizes, lead=()):
    flat = buf.reshape(lead + (-1,))
    out, off = [], 0
    for n in sizes:
        out.append(flat[..., off:off + n])
        off += n
    return out


def _conv_to_wire(conv):
    return lax.bitcast_convert_type(conv, bf16)


def _conv_from_wire(wire):
    return lax.bitcast_convert_type(wire, f32)


SMALL_ROWS = 96
SMALL_ADAM_ROWS = 56


def kernel(x, mem, norm_mix_g, w_in, fox_qnorm_g, fox_knorm_g, fox_f_bias, fox_onorm_g, gdn_conv_w, gdn_A_log, gdn_dt_bias, gdn_onorm_g, w_out, norm_xattn_g, mem_norm_g, w_cq, w_ckv, xattn_qnorm_g, xattn_knorm_g, w_co, norm_mlp_g, w_mlp1, w_mlp2, loss_target, m_norm_mix_g, m_w_in, m_fox_qnorm_g, m_fox_knorm_g, m_fox_f_bias, m_fox_onorm_g, m_gdn_conv_w, m_gdn_A_log, m_gdn_dt_bias, m_gdn_onorm_g, m_w_out, m_norm_xattn_g, m_mem_norm_g, m_w_cq, m_w_ckv, m_xattn_qnorm_g, m_xattn_knorm_g, m_w_co, m_norm_mlp_g, m_w_mlp1, m_w_mlp2, v_norm_mix_g, v_w_in, v_fox_qnorm_g, v_fox_knorm_g, v_fox_f_bias, v_fox_onorm_g, v_gdn_conv_w, v_gdn_A_log, v_gdn_dt_bias, v_gdn_onorm_g, v_w_out, v_norm_xattn_g, v_mem_norm_g, v_w_cq, v_w_ckv, v_xattn_qnorm_g, v_xattn_knorm_g, v_w_co, v_norm_mlp_g, v_w_mlp1, v_w_mlp2):
    wts = dict(norm_mix_g=norm_mix_g, w_in=w_in, fox_qnorm_g=fox_qnorm_g, fox_knorm_g=fox_knorm_g, fox_f_bias=fox_f_bias,
               fox_onorm_g=fox_onorm_g, gdn_conv_w=gdn_conv_w, gdn_A_log=gdn_A_log, gdn_dt_bias=gdn_dt_bias,
               gdn_onorm_g=gdn_onorm_g, w_out=w_out, norm_xattn_g=norm_xattn_g, mem_norm_g=mem_norm_g, w_cq=w_cq, w_ckv=w_ckv,
               xattn_qnorm_g=xattn_qnorm_g, xattn_knorm_g=xattn_knorm_g, w_co=w_co, norm_mlp_g=norm_mlp_g, w_mlp1=w_mlp1,
               w_mlp2=w_mlp2)
    mom = dict(norm_mix_g=m_norm_mix_g, w_in=m_w_in, fox_qnorm_g=m_fox_qnorm_g, fox_knorm_g=m_fox_knorm_g,
               fox_f_bias=m_fox_f_bias, fox_onorm_g=m_fox_onorm_g, gdn_conv_w=m_gdn_conv_w, gdn_A_log=m_gdn_A_log,
               gdn_dt_bias=m_gdn_dt_bias, gdn_onorm_g=m_gdn_onorm_g, w_out=m_w_out, norm_xattn_g=m_norm_xattn_g,
               mem_norm_g=m_mem_norm_g, w_cq=m_w_cq, w_ckv=m_w_ckv, xattn_qnorm_g=m_xattn_qnorm_g,
               xattn_knorm_g=m_xattn_knorm_g, w_co=m_w_co, norm_mlp_g=m_norm_mlp_g, w_mlp1=m_w_mlp1, w_mlp2=m_w_mlp2)
    var = dict(norm_mix_g=v_norm_mix_g, w_in=v_w_in, fox_qnorm_g=v_fox_qnorm_g, fox_knorm_g=v_fox_knorm_g,
               fox_f_bias=v_fox_f_bias, fox_onorm_g=v_fox_onorm_g, gdn_conv_w=v_gdn_conv_w, gdn_A_log=v_gdn_A_log,
               gdn_dt_bias=v_gdn_dt_bias, gdn_onorm_g=v_gdn_onorm_g, w_out=v_w_out, norm_xattn_g=v_norm_xattn_g,
               mem_norm_g=v_mem_norm_g, w_cq=v_w_cq, w_ckv=v_w_ckv, xattn_qnorm_g=v_xattn_qnorm_g,
               xattn_knorm_g=v_xattn_knorm_g, w_co=v_w_co, norm_mlp_g=v_norm_mlp_g, w_mlp1=v_w_mlp1, w_mlp2=v_w_mlp2)
    B, S, D = x.shape
    T = B * S
    big_names = [n for n, _, _ in BIG_SHARDS]
    chip = 2 * lax.axis_index("x") + lax.axis_index("y")
    core = lax.axis_index("c").astype(jnp.int32).reshape(1)

    shards = {n: wts[n][0].astype(MXU_DTYPE) for n in big_names[1:]}
    in_t = lambda p: jnp.swapaxes(p[0], 0, 1)
    shards["w_in"] = jnp.pad(in_t(w_in).astype(MXU_DTYPE), ((0, IN_SHARD_PAD - IN_SHARD), (0, 0)))
    w_in_all, conv_all = gather_weights([shards["w_in"]], gdn_conv_w[0])
    late = big_names[1:]
    send_sems, recv_sems, late_src, late_zones, token = gather_weights_start([shards[n] for n in late], conv_all)
    own = lambda g, s: lax.dynamic_update_slice(g, s[None], (chip,) + (0,) * s.ndim)
    full = {"w_in": own(w_in_all, shards["w_in"])}
    conv_full = own(conv_all, gdn_conv_w[0]).transpose(1, 0, 2).reshape(CONV_WIDTH, 3 * GDN_WIDTH)
    rows = lambda g: g.reshape(N_CHIPS * g.shape[1], g.shape[2])

    def late_weights(after):
        zones = gather_weights_wait(send_sems, recv_sems, late_src, late_zones, after)
        got = {n: own(z, shards[n]) for n, z in zip(late, zones)}
        return dict(w_out=rows(got["w_out"]), w_cq=rows(got["w_cq"]), w_ckv=rows(got["w_ckv"]),
                    w_co=got["w_co"].transpose(1, 0, 2).reshape(XATTN_WIDTH, D_MODEL),
                    w_mlp1=got["w_mlp1"], w_mlp2=rows(got["w_mlp2"]))

    in_flight = []

    def grads_ready(ready):
        names = list(ready)
        *started, tok = reduce_grads_start([ready[n] for n in names], name="reduce_grads_start_%d" % len(in_flight))
        in_flight.append((names, *started))
        return tok

    w_in_t = full["w_in"][:, :IN_SHARD].reshape(IN_DIM, D_MODEL)
    w = dict(wa_t=align_w_in_t(w_in_t), conv_w=conv_full, late=late_weights, grads_ready=grads_ready)
    sp = {n: wts[n] for n, _ in SMALL_DIMS}
    sp["norm_mix_g"] = sp["norm_mix_g"] + token[0, 0]

    loss_part, grad_x, g_big, g_small = local_step(x.reshape(T, D), mem.reshape(-1, D), loss_target.reshape(T, D), w, sp, B=B)

    small_pieces = [g_small[n] for n, _ in SMALL_DIMS] + [g_small["gdn_conv_w"], loss_part]
    small_sizes = [d for _, d in SMALL_DIMS] + [CONV_WIDTH * 3 * GDN_WIDTH, LANES]
    red_small = _unpack_rows(all_reduce_small(_pack_rows(small_pieces, SMALL_ROWS)), small_sizes)
    grads = {n: p.reshape(1, d) for (n, d), p in zip(SMALL_DIMS, red_small)}
    conv_grad = lax.dynamic_slice(red_small[-2].reshape(CONV_WIDTH, 3 * GDN_WIDTH), (0, chip * CONV_SHARD[1]), CONV_SHARD)
    grads["gdn_conv_w"] = conv_grad.reshape((1,) + CONV_SHARD)
    loss = red_small[-1][0]

    parts, zones = {}, {}

    def wait_group(k, after):
        names, send_sems, recv_sems, thru, land = in_flight[k]
        thru, land = reduce_grads_wait(send_sems, recv_sems, thru, land, after, name="reduce_grads_wait_%d" % k)
        parts.update(zip(names, thru))
        zones.update(zip(names, land))

    wait_group(0, grad_x)
    wait_group(1, grad_x)
    dev = 2 * chip + core[0]
    where = jnp.stack([chip, core[0]] + [dev ^ r for r in range(1, N_DEV)]).astype(jnp.int32)
    mine = [sum_partials(parts[n], zones[n], where, name="sum_partials_" + n) for n in late]
    theirs = swap_reduced_halves(mine, name="swap_reduced_halves")

    delta, new_m, new_v = {}, {}, {}
    for n, a, b in zip(late, mine, theirs):
        g, d, nm, nv = adamw_halves(wts[n][0], a, b, mom[n][0], var[n][0], core, name="adamw_" + n)
        grads[n], delta[n], new_m[n], new_v[n] = g[None], d[None], nm[None], nv[None]
    wait_group(2, new_v[late[-1]])
    mine_in = sum_partials(parts["w_in"], zones["w_in"], where, name="sum_partials_w_in")
    (theirs_in,) = swap_reduced_halves([mine_in], name="swap_reduced_halves_w_in")
    south = core[0] == 0
    g_in_t = jnp.concatenate([jnp.where(south, mine_in, theirs_in), jnp.where(south, theirs_in, mine_in)])[:IN_SHARD]
    back = lambda t: jnp.swapaxes(t, 0, 1)[None]
    d, nm, nv = adamw(in_t(w_in), g_in_t, in_t(m_w_in), in_t(v_w_in), name="adamw_w_in", tc=256)
    grads["w_in"], delta["w_in"], new_m["w_in"], new_v["w_in"] = back(g_in_t), back(d), back(nm), back(nv)
    small_names = [n for n, _ in SMALL_DIMS] + ["gdn_conv_w"]
    small_sz = [d for _, d in SMALL_DIMS] + [CONV_SHARD[0] * CONV_SHARD[1]]
    packed4 = [_pack_rows([src[n] for n in small_names], SMALL_ADAM_ROWS) for src in (wts, grads, mom, var)]
    outs = adamw(*packed4, name="adamw_small", tr=SMALL_ADAM_ROWS)
    for dst, buf in zip((delta, new_m, new_v), outs):
        for n, p in zip(small_names, _unpack_rows(buf, small_sz)):
            dst[n] = p.reshape(wts[n].shape)

    return (loss, grad_x.reshape(B, S, D), *[grads[n] for n in WEIGHT_ORDER], *[delta[n] for n in WEIGHT_ORDER],
            *[new_m[n] for n in WEIGHT_ORDER], *[new_v[n] for n in WEIGHT_ORDER])
```

```python
import functools

import jax
import jax.numpy as jnp
import numpy as np
from jax import lax
from jax.experimental import pallas as pl
from jax.experimental.pallas import tpu as pltpu

f32 = jnp.float32
bf16 = jnp.bfloat16
MXU_DTYPE = jnp.bfloat16
WIRE_DTYPE = jnp.bfloat16
INV_PRECISION = None

D_MODEL = 1024
FOX_HEADS = 8
FOX_HEAD_DIM = 64
FOX_WIDTH = 512
GDN_HEADS = 4
GDN_HEAD_DIM = 128
GDN_WIDTH = 512
CONV_WIDTH = 4
GDN_CHUNK = 64
XATTN_HEADS = 4
XATTN_HEAD_DIM = 128
XATTN_WIDTH = 512
D_FF = 4096
IN_DIM = 3600
EPS = 1e-6
NEG_INF = -1e30
LANES = 128
ADAM_LR = 0.001
ADAM_B1 = 0.9
ADAM_B2 = 0.999
ADAM_EPS = 1e-08
ADAM_WD = 0.01
ADAM_STEP = 10
VMEM_LIMIT = 48 * 1024 * 1024

COL_FOX = 0
COL_GDN = 1536
COL_Z = 3072
COL_SMALL = 3584
IN_ALIGNED = 3840
IN_TILE = 768
SM_F = 0
SM_B = 8
SM_A = 12


def _cparams(*sem):
    return pltpu.CompilerParams(dimension_semantics=sem, vmem_limit_bytes=VMEM_LIMIT)


def _mx(v):
    return v.astype(MXU_DTYPE)


def _dot(a, b, dims, precision=None):
    return lax.dot_general(a, b, (dims, ((), ())), preferred_element_type=f32, precision=precision)


def _dotm(a, b, dims):
    return _dot(_mx(a), _mx(b), dims)


NN = ((1,), (0,))
NT = ((1,), (1,))
TN = ((0,), (0,))


def matmul(a, b, *, name, ta=False, tb=False, b_stacked=False, out_stacked=False, residual=None, relu2_out=False,
           relu2_bwd_aux=None, out_dtype=f32, tm=1024, tn=1024, tk=1024):
    M, K = (a.shape[1], a.shape[0]) if ta else a.shape
    if b_stacked:
        b_cols = b.shape[2]
        N, tk = (b.shape[1], min(tk, b_cols)) if tb else (N_CHIPS * b_cols, tk)
        tn = tn if tb else min(tn, b_cols)
        assert K == (N_CHIPS * b_cols if tb else b.shape[1]), (name, a.shape, b.shape)
    else:
        N = b.shape[0] if tb else b.shape[1]
    if out_stacked:
        tn = min(tn, N // N_CHIPS)
    tm, tn, tk = min(tm, M), min(tn, N), min(tk, K)
    assert M % tm == 0 and N % tn == 0 and K % tk == 0, (name, M, N, K)
    nk = K // tk
    has_res = residual is not None
    has_aux = relu2_bwd_aux is not None

    def body(*refs):
        a_ref, b_ref = refs[0], refs[1]
        pos = 2
        res_ref = aux_ref = None
        if has_res:
            res_ref = refs[pos]
            pos += 1
        if has_aux:
            aux_ref = refs[pos]
            pos += 1
        o_ref = refs[pos]
        k = pl.program_id(2)
        dims = ((0,) if ta else (1,), (1,) if tb else (0,))
        part = _dot(_mx(a_ref[...]), _mx(b_ref[...]), dims)

        def finish(r):
            if has_res:
                r = r + res_ref[...]
            if has_aux:
                r = r * (2.0 * jnp.sqrt(aux_ref[...].astype(f32)))
            if relu2_out:
                o_ref[...] = jnp.square(jnp.maximum(r, 0.0)).astype(o_ref.dtype)
            else:
                o_ref[...] = r.astype(o_ref.dtype)

        if nk == 1:
            finish(part)
            return
        acc_ref = refs[pos + 1]

        @pl.when(k == 0)
        def _():
            acc_ref[...] = part

        @pl.when((k > 0) & (k < nk - 1))
        def _():
            acc_ref[...] += part

        @pl.when(k == nk - 1)
        def _():
            finish(acc_ref[...] + part)

    a_spec = pl.BlockSpec((tk, tm), lambda i, j, k: (k, i)) if ta else pl.BlockSpec((tm, tk), lambda i, j, k: (i, k))
    if b_stacked and tb:
        per = b_cols // tk
        b_spec = pl.BlockSpec((None, tn, tk), lambda i, j, k: (k // per, j, k % per))
    elif b_stacked:
        per = b_cols // tn
        b_spec = pl.BlockSpec((None, tk, tn), lambda i, j, k: (j // per, k, j % per))
    else:
        b_spec = pl.BlockSpec((tn, tk), lambda i, j, k: (j, k)) if tb else pl.BlockSpec((tk, tn), lambda i, j, k: (k, j))
    if out_stacked:
        assert not (has_res or has_aux or relu2_out), name
        per_o = N // N_CHIPS // tn
        o_spec = pl.BlockSpec((None, tm, tn), lambda i, j, k: (j // per_o, i, j % per_o))
        out_full = (N_CHIPS, M, N // N_CHIPS)
    else:
        o_spec = pl.BlockSpec((tm, tn), lambda i, j, k: (i, j))
        out_full = (M, N)
    in_specs, args = [a_spec, b_spec], [a, b]
    if has_res:
        in_specs.append(o_spec)
        args.append(residual)
    if has_aux:
        in_specs.append(o_spec)
        args.append(relu2_bwd_aux)
    out_shape = [jax.ShapeDtypeStruct(out_full, out_dtype)]
    out_specs = [o_spec]
    res = pl.pallas_call(
        body, name=name, grid=(M // tm, N // tn, nk), in_specs=in_specs, out_specs=out_specs, out_shape=out_shape,
        scratch_shapes=[pltpu.VMEM((tm, tn), f32)] if nk > 1 else [],
        compiler_params=_cparams("parallel", "parallel", "arbitrary"),
    )(*args)
    return res[0]


def matmul_rows(a, b, extras, *, name, mode, tb=False, b_stacked=False, tm=1024, tk=1024):
    M, K = a.shape
    N = D_MODEL
    if b_stacked:
        assert tb, name
        tk = min(tk, b.shape[2])
        per = b.shape[2] // tk
        b_spec = pl.BlockSpec((None, N, tk), lambda i, k: (k // per, 0, k % per))
    elif tb:
        tk = min(tk, K)
        b_spec = pl.BlockSpec((N, tk), lambda i, k: (0, k))
    else:
        tk = min(tk, K)
        b_spec = pl.BlockSpec((tk, N), lambda i, k: (k, 0))
    tm = min(tm, M)
    assert M % tm == 0 and K % tk == 0, (name, M, K)
    nk = K // tk
    extras = [e for e in extras if e is not None]
    n_ex = len(extras)

    def body(*refs):
        a_ref, b_ref = refs[0], refs[1]
        ex = refs[2:2 + n_ex]
        o_ref = refs[2 + n_ex]
        n_out = 3 if mode == "loss" else 2
        s_ref = refs[1 + n_ex + n_out]
        i, k = pl.program_id(0), pl.program_id(1)
        part = _dot(_mx(a_ref[...]), _mx(b_ref[...]), ((1,), (1,) if tb else (0,)))

        def finish(y):
            if mode == "rms_fwd":
                y = y + ex[0][...]
                o_ref[...] = y
                s_ref[...] = (y * lax.rsqrt(jnp.mean(y * y, axis=-1, keepdims=True) + EPS) * ex[1][...]).astype(MXU_DTYPE)
                return

            @pl.when(i == 0)
            def _():
                s_ref[...] = jnp.zeros_like(s_ref)

            if mode == "rms_bwd":
                xv, gv = ex[0][...], ex[1][...]
                rstd = lax.rsqrt(jnp.mean(xv * xv, axis=-1, keepdims=True) + EPS)
                xhat = xv * rstd
                gd = y * gv
                dx = rstd * (gd - xhat * jnp.mean(gd * xhat, axis=-1, keepdims=True))
                o_ref[...] = dx + ex[2][...] if n_ex == 3 else dx
                s_ref[...] += jnp.sum(y * xhat, axis=0, keepdims=True)
            else:
                e = y + ex[0][...] - ex[1][...]
                o_ref[...] = e * (1.0 / N)
                refs[3 + n_ex][...] = (e * (1.0 / N)).astype(MXU_DTYPE)
                tot = 0.5 * jnp.sum(jnp.mean(e * e, axis=-1, keepdims=True), axis=0, keepdims=True)
                s_ref[...] += jnp.broadcast_to(tot, s_ref.shape)

        if nk == 1:
            finish(part)
            return
        acc_ref = refs[2 + n_ex + n_out]

        @pl.when(k == 0)
        def _():
            acc_ref[...] = part

        @pl.when((k > 0) & (k < nk - 1))
        def _():
            acc_ref[...] += part

        @pl.when(k == nk - 1)
        def _():
            finish(acc_ref[...] + part)

    row = pl.BlockSpec((tm, N), lambda i, k: (i, 0))
    vec = pl.BlockSpec((1, N), lambda i, k: (0, 0))
    if mode == "rms_bwd":
        ex_specs = [row, vec] + ([row] if n_ex == 3 else [])
        s_shape, s_spec = jax.ShapeDtypeStruct((1, N), f32), vec
    elif mode == "rms_fwd":
        ex_specs = [row, vec]
        s_shape, s_spec = jax.ShapeDtypeStruct((M, N), MXU_DTYPE), row
    else:
        ex_specs = [row, row]
        s_shape, s_spec = jax.ShapeDtypeStruct((1, LANES), f32), pl.BlockSpec((1, LANES), lambda i, k: (0, 0))
    return pl.pallas_call(
        body, name=name, grid=(M // tm, nk),
        in_specs=[pl.BlockSpec((tm, tk), lambda i, k: (i, k)), b_spec] + ex_specs,
        out_specs=[row] * (2 if mode == "loss" else 1) + [s_spec],
        out_shape=[jax.ShapeDtypeStruct((M, N), f32)] + ([jax.ShapeDtypeStruct((M, N), MXU_DTYPE)] if mode == "loss" else [])
        + [s_shape],
        scratch_shapes=[pltpu.VMEM((tm, N), f32)] if nk > 1 else [],
        compiler_params=_cparams("arbitrary", "arbitrary"),
    )(a, b, *extras)


def rms_fwd(x, g, *, name, tr=1024):
    R, D = x.shape
    tr = min(tr, R)

    def body(x_ref, g_ref, o_ref):
        xv = x_ref[...]
        y = xv * lax.rsqrt(jnp.mean(xv * xv, axis=-1, keepdims=True) + EPS)
        o_ref[...] = (y * g_ref[...]).astype(o_ref.dtype)

    return pl.pallas_call(
        body, name=name, grid=(R // tr,),
        in_specs=[pl.BlockSpec((tr, D), lambda i: (i, 0)), pl.BlockSpec((1, D), lambda i: (0, 0))],
        out_specs=pl.BlockSpec((tr, D), lambda i: (i, 0)),
        out_shape=jax.ShapeDtypeStruct((R, D), MXU_DTYPE),
        compiler_params=_cparams("parallel"),
    )(x, g)


def rms_bwd(x, g, dh, residual, *, name, tr=512):
    R, D = x.shape
    tr = min(tr, R)
    has_res = residual is not None

    def body(*refs):
        if has_res:
            x_ref, g_ref, dh_ref, res_ref, dx_ref, dg_ref = refs
        else:
            x_ref, g_ref, dh_ref, dx_ref, dg_ref = refs
        xv = x_ref[...]
        rstd = lax.rsqrt(jnp.mean(xv * xv, axis=-1, keepdims=True) + EPS)
        xhat = xv * rstd
        dh = dh_ref[...].astype(f32)
        gd = dh * g_ref[...]
        dx = rstd * (gd - xhat * jnp.mean(gd * xhat, axis=-1, keepdims=True))
        if has_res:
            dx = dx + res_ref[...]
        dx_ref[...] = dx

        @pl.when(pl.program_id(0) == 0)
        def _():
            dg_ref[...] = jnp.zeros_like(dg_ref)

        dg_ref[...] += jnp.sum(dh * xhat, axis=0, keepdims=True)

    row = pl.BlockSpec((tr, D), lambda i: (i, 0))
    vec = pl.BlockSpec((1, D), lambda i: (0, 0))
    in_specs = [row, vec, row] + ([row] if has_res else [])
    args = [x, g, dh] + ([residual] if has_res else [])
    return pl.pallas_call(
        body, name=name, grid=(R // tr,), in_specs=in_specs, out_specs=[row, vec],
        out_shape=[jax.ShapeDtypeStruct((R, D), f32), jax.ShapeDtypeStruct((1, D), f32)],
        compiler_params=_cparams("arbitrary"),
    )(*args)


def loss_head(y, target, *, tr=512):
    R, D = y.shape
    tr = min(tr, R)

    def body(y_ref, t_ref, dy_ref, loss_ref):
        e = y_ref[...] - t_ref[...]
        dy_ref[...] = e * (1.0 / D)

        @pl.when(pl.program_id(0) == 0)
        def _():
            loss_ref[...] = jnp.zeros_like(loss_ref)

        part = 0.5 * jnp.sum(jnp.mean(e * e, axis=-1, keepdims=True), axis=0, keepdims=True)
        loss_ref[...] += jnp.broadcast_to(part, loss_ref.shape)

    row = pl.BlockSpec((tr, D), lambda i: (i, 0))
    return pl.pallas_call(
        body, name="loss_head", grid=(R // tr,), in_specs=[row, row],
        out_specs=[row, pl.BlockSpec((1, LANES), lambda i: (0, 0))],
        out_shape=[jax.ShapeDtypeStruct((R, D), f32), jax.ShapeDtypeStruct((1, LANES), f32)],
        compiler_params=_cparams("arbitrary"),
    )(y, target)


def _head_rms(v, g):
    r = lax.rsqrt(jnp.mean(v * v, axis=-1, keepdims=True) + EPS)
    return v * r * g, r


def _head_rms_bwd(v, r, g, dn):
    vhat = v * r
    gd = dn * g
    dv = r * (gd - vhat * jnp.mean(gd * vhat, axis=-1, keepdims=True))
    return dv, jnp.sum(dn * vhat, axis=0, keepdims=True)


def _softmax_rows(s):
    m = jnp.max(s, axis=-1, keepdims=True)
    e = jnp.exp(s - m)
    return e / jnp.sum(e, axis=-1, keepdims=True)


def xattn_fwd(cq, ckv, gq, gk, *, B, tq=1024):
    T = cq.shape[0]
    S = T // B
    M = ckv.shape[0] // B
    tq = min(tq, S)
    nq = S // tq
    hd, W = XATTN_HEAD_DIM, XATTN_WIDTH
    scale = hd ** -0.5

    def body(q_ref, k_ref, v_ref, gq_ref, gk_ref, o_ref):
        for h in range(XATTN_HEADS):
            sl = slice(h * hd, (h + 1) * hd)
            qn, _ = _head_rms(q_ref[:, sl], gq_ref[...])
            kn, _ = _head_rms(k_ref[:, sl], gk_ref[...])
            p = _softmax_rows(_dot(_mx(qn), _mx(kn), NT) * scale)
            o_ref[:, sl] = _dot(_mx(p), _mx(v_ref[:, sl]), NN).astype(o_ref.dtype)

    vec = pl.BlockSpec((1, hd), lambda b, i: (0, 0))
    qspec = pl.BlockSpec((tq, W), lambda b, i: (b * nq + i, 0))
    return pl.pallas_call(
        body, name="xattn_fwd", grid=(B, nq),
        in_specs=[qspec, pl.BlockSpec((M, W), lambda b, i: (b, 0)), pl.BlockSpec((M, W), lambda b, i: (b, 1)), vec, vec],
        out_specs=qspec, out_shape=jax.ShapeDtypeStruct((T, W), MXU_DTYPE),
        compiler_params=_cparams("parallel", "parallel"),
    )(cq, ckv, ckv, gq, gk)


def xattn_bwd(cq, ckv, gq, gk, dco, *, B, tq=1024):
    T = cq.shape[0]
    S = T // B
    M = ckv.shape[0] // B
    tq = min(tq, S)
    nq = S // tq
    hd, W = XATTN_HEAD_DIM, XATTN_WIDTH
    scale = hd ** -0.5

    def body(q_ref, k_ref, v_ref, gq_ref, gk_ref, do_ref, dq_ref, dkv_ref, dgq_ref, dgk_ref, dkn_acc, dv_acc):
        b, i = pl.program_id(0), pl.program_id(1)

        @pl.when((b == 0) & (i == 0))
        def _():
            dgq_ref[...] = jnp.zeros_like(dgq_ref)
            dgk_ref[...] = jnp.zeros_like(dgk_ref)

        @pl.when(i == 0)
        def _():
            dkn_acc[...] = jnp.zeros_like(dkn_acc)
            dv_acc[...] = jnp.zeros_like(dv_acc)

        gqv, gkv = gq_ref[...], gk_ref[...]
        for h in range(XATTN_HEADS):
            sl = slice(h * hd, (h + 1) * hd)
            q, k, v = q_ref[:, sl], k_ref[:, sl], v_ref[:, sl]
            qn, rq = _head_rms(q, gqv)
            kn, _ = _head_rms(k, gkv)
            p = _softmax_rows(_dot(_mx(qn), _mx(kn), NT) * scale)
            do = do_ref[:, sl]
            dv_acc[:, sl] += _dot(_mx(p), _mx(do), TN)
            dp = _dot(_mx(do), _mx(v), NT)
            ds = p * (dp - jnp.sum(dp * p, axis=-1, keepdims=True)) * scale
            dqn = _dot(_mx(ds), _mx(kn), NN)
            dkn_acc[:, sl] += _dot(_mx(ds), _mx(qn), TN)
            dq, dgq = _head_rms_bwd(q, rq, gqv, dqn)
            dq_ref[:, sl] = dq.astype(dq_ref.dtype)
            dgq_ref[...] += dgq

        @pl.when(i == nq - 1)
        def _():
            for h in range(XATTN_HEADS):
                sl = slice(h * hd, (h + 1) * hd)
                k = k_ref[:, sl]
                rk = lax.rsqrt(jnp.mean(k * k, axis=-1, keepdims=True) + EPS)
                dk, dgk = _head_rms_bwd(k, rk, gkv, dkn_acc[:, sl])
                dkv_ref[:, sl] = dk.astype(dkv_ref.dtype)
                dkv_ref[:, slice(W + h * hd, W + (h + 1) * hd)] = dv_acc[:, sl].astype(dkv_ref.dtype)
                dgk_ref[...] += dgk

    vec = pl.BlockSpec((1, hd), lambda b, i: (0, 0))
    qspec = pl.BlockSpec((tq, W), lambda b, i: (b * nq + i, 0))
    return pl.pallas_call(
        body, name="xattn_bwd", grid=(B, nq),
        in_specs=[qspec, pl.BlockSpec((M, W), lambda b, i: (b, 0)), pl.BlockSpec((M, W), lambda b, i: (b, 1)), vec, vec, qspec],
        out_specs=[qspec, pl.BlockSpec((M, 2 * W), lambda b, i: (b, 0)), vec, vec],
        out_shape=[jax.ShapeDtypeStruct((T, W), MXU_DTYPE), jax.ShapeDtypeStruct((B * M, 2 * W), MXU_DTYPE),
                   jax.ShapeDtypeStruct((1, hd), f32), jax.ShapeDtypeStruct((1, hd), f32)],
        scratch_shapes=[pltpu.VMEM((M, W), f32), pltpu.VMEM((M, W), f32)],
        compiler_params=_cparams("arbitrary", "arbitrary"),
    )(cq, ckv, ckv, gq, gk, dco)


FOX_PAIRS = FOX_HEADS // 2


def _fox_scores(qn, kn, ccol, crow, q0, tq, S, scale):
    s = _dot(_mx(qn), _mx(kn), NT) * scale + ccol - crow
    qpos = q0 + lax.broadcasted_iota(jnp.int32, (tq, S), 0)
    kpos = lax.broadcasted_iota(jnp.int32, (tq, S), 1)
    return jnp.where(kpos <= qpos, s, NEG_INF)


def fox_fwd(P, ccol, crow, gq, gk, go, *, B, tq=256):
    T = P.shape[0]
    S = T // B
    tq = min(tq, S)
    nq = S // tq
    hd = FOX_HEAD_DIM
    scale = hd ** -0.5

    def body(q_ref, k_ref, v_ref, ccol_ref, crow_ref, gq_ref, gk_ref, go_ref, o_ref, oa_ref):
        q0 = pl.program_id(2) * tq
        for e in range(2):
            sl = slice(e * hd, (e + 1) * hd)
            qn, _ = _head_rms(q_ref[:, sl], gq_ref[:, sl])
            kn, _ = _head_rms(k_ref[:, sl], gk_ref[:, sl])
            p = _softmax_rows(_fox_scores(qn, kn, ccol_ref[0, e], crow_ref[0, e], q0, tq, S, scale))
            o = _dot(_mx(p), _mx(v_ref[:, sl]), NN)
            o_ref[:, sl] = o
            oa_ref[:, sl] = _head_rms(o, go_ref[:, sl])[0].astype(oa_ref.dtype)

    W = 2 * hd
    vec = pl.BlockSpec((1, W), lambda b, h, i: (0, 0))
    ospec = pl.BlockSpec((tq, W), lambda b, h, i: (b * nq + i, h))
    return pl.pallas_call(
        body, name="fox_fwd", grid=(B, FOX_PAIRS, nq),
        in_specs=[pl.BlockSpec((tq, W), lambda b, h, i: (b * nq + i, h)),
                  pl.BlockSpec((S, W), lambda b, h, i: (b, FOX_PAIRS + h)),
                  pl.BlockSpec((S, W), lambda b, h, i: (b, 2 * FOX_PAIRS + h)),
                  pl.BlockSpec((1, 2, tq, 1), lambda b, h, i: (b, h, i, 0)),
                  pl.BlockSpec((1, 2, 1, S), lambda b, h, i: (b, h, 0, 0)), vec, vec, vec],
        out_specs=[ospec, ospec],
        out_shape=[jax.ShapeDtypeStruct((T, FOX_WIDTH), f32), jax.ShapeDtypeStruct((T, FOX_WIDTH), MXU_DTYPE)],
        compiler_params=_cparams("parallel", "parallel", "parallel"),
    )(P, P, P, ccol, crow, gq, gk, go)


def fox_bwd(P, ccol, crow, gq, gk, go, o_raw, d_oab, *, B, tq=256):
    T = P.shape[0]
    S = T // B
    tq = min(tq, S)
    nq = S // tq
    hd = FOX_HEAD_DIM
    scale = hd ** -0.5

    def body(q_ref, k_ref, v_ref, ccol_ref, crow_ref, gq_ref, gk_ref, go_ref, o_ref, doa_ref,
             dq_ref, dk_ref, dv_ref, dccol_ref, dcrow_ref, dgq_ref, dgk_ref, dgo_ref, dkn_acc, dv_acc, dcrow_acc):
        b, h, i = pl.program_id(0), pl.program_id(1), pl.program_id(2)
        q0 = i * tq

        @pl.when((b == 0) & (h == 0) & (i == 0))
        def _():
            dgq_ref[...] = jnp.zeros_like(dgq_ref)
            dgk_ref[...] = jnp.zeros_like(dgk_ref)
            dgo_ref[...] = jnp.zeros_like(dgo_ref)

        @pl.when(i == 0)
        def _():
            dkn_acc[...] = jnp.zeros_like(dkn_acc)
            dv_acc[...] = jnp.zeros_like(dv_acc)
            dcrow_acc[...] = jnp.zeros_like(dcrow_acc)

        for e in range(2):
            sl = slice(e * hd, (e + 1) * hd)
            q, k, v = q_ref[:, sl], k_ref[:, sl], v_ref[:, sl]
            gqv, gkv, gov = gq_ref[:, sl], gk_ref[:, sl], go_ref[:, sl]
            qn, rq = _head_rms(q, gqv)
            kn, rk = _head_rms(k, gkv)
            p = _softmax_rows(_fox_scores(qn, kn, ccol_ref[0, e], crow_ref[0, e], q0, tq, S, scale))
            o = o_ref[:, sl]
            ro = lax.rsqrt(jnp.mean(o * o, axis=-1, keepdims=True) + EPS)
            do, dgo = _head_rms_bwd(o, ro, gov, doa_ref[:, sl])
            dgo_ref[:, sl] += dgo
            dv_acc[e] += _dot(_mx(p), _mx(do), TN)
            dp = _dot(_mx(do), _mx(v), NT)
            ds = p * (dp - jnp.sum(do * o, axis=-1, keepdims=True))
            dccol_ref[0, e] = jnp.sum(ds, axis=1, keepdims=True)
            dcrow_acc[e] -= jnp.sum(ds, axis=0, keepdims=True)
            dqn = _dot(_mx(ds), _mx(kn), NN) * scale
            dkn_acc[e] += _dot(_mx(ds), _mx(qn), TN) * scale
            dq, dgq = _head_rms_bwd(q, rq, gqv, dqn)
            dq_ref[:, sl] = dq.astype(dq_ref.dtype)
            dgq_ref[:, sl] += dgq

        @pl.when(i == nq - 1)
        def _():
            for e in range(2):
                sl = slice(e * hd, (e + 1) * hd)
                k = k_ref[:, sl]
                gkv = gk_ref[:, sl]
                rk = lax.rsqrt(jnp.mean(k * k, axis=-1, keepdims=True) + EPS)
                dk, dgk = _head_rms_bwd(k, rk, gkv, dkn_acc[e])
                dk_ref[:, sl] = dk.astype(dk_ref.dtype)
                dv_ref[:, sl] = dv_acc[e].astype(dv_ref.dtype)
                dgk_ref[:, sl] += dgk
                dcrow_ref[0, e] = dcrow_acc[e]

    W = 2 * hd
    vec = pl.BlockSpec((1, W), lambda b, h, i: (0, 0))
    qspec = pl.BlockSpec((tq, W), lambda b, h, i: (b * nq + i, h))
    kvout = pl.BlockSpec((S, W), lambda b, h, i: (b, h))
    colspec = pl.BlockSpec((1, 2, tq, 1), lambda b, h, i: (b, h, i, 0))
    rowspec = pl.BlockSpec((1, 2, 1, S), lambda b, h, i: (b, h, 0, 0))
    return pl.pallas_call(
        body, name="fox_bwd", grid=(B, FOX_PAIRS, nq),
        in_specs=[qspec,
                  pl.BlockSpec((S, W), lambda b, h, i: (b, FOX_PAIRS + h)),
                  pl.BlockSpec((S, W), lambda b, h, i: (b, 2 * FOX_PAIRS + h)),
                  colspec, rowspec, vec, vec, vec, qspec, qspec],
        out_specs=[qspec, kvout, kvout, colspec, rowspec, vec, vec, vec],
        out_shape=[jax.ShapeDtypeStruct((T, FOX_WIDTH), MXU_DTYPE), jax.ShapeDtypeStruct((T, FOX_WIDTH), MXU_DTYPE),
                   jax.ShapeDtypeStruct((T, FOX_WIDTH), MXU_DTYPE),
                   jax.ShapeDtypeStruct((B, FOX_HEADS, S, 1), f32), jax.ShapeDtypeStruct((B, FOX_HEADS, 1, S), f32),
                   jax.ShapeDtypeStruct((1, W), f32), jax.ShapeDtypeStruct((1, W), f32), jax.ShapeDtypeStruct((1, W), f32)],
        scratch_shapes=[pltpu.VMEM((2, S, hd), f32), pltpu.VMEM((2, S, hd), f32), pltpu.VMEM((2, 1, S), f32)],
        compiler_params=_cparams("arbitrary", "arbitrary", "arbitrary"),
    )(P, P, P, ccol, crow, gq, gk, go, o_raw, d_oab)


FOX_TQ = 512
FOX_TK = FOX_TQ
GROUP_PRECISION = lax.Precision.HIGH


def _head_mean(v):
    n = v.shape[1]
    r = lax.broadcasted_iota(jnp.int32, (n, n), 0) // FOX_HEAD_DIM
    c = lax.broadcasted_iota(jnp.int32, (n, n), 1) // FOX_HEAD_DIM
    ones = (r == c).astype(bf16)
    hi = v.astype(bf16)
    lo = (v - hi.astype(f32)).astype(bf16)
    return (_dot(hi, ones, NN) + _dot(lo, ones, NN)) * (1.0 / FOX_HEAD_DIM)


def fox_prep_fwd(P, gq, gk, *, tr=1024):
    T = P.shape[0]
    tr = min(tr, T)
    scale = FOX_HEAD_DIM ** -0.5

    def body(q_ref, k_ref, v_ref, gq_ref, gk_ref, qn_ref, kn_ref, vb_ref):
        q, k = q_ref[...], k_ref[...]
        qn_ref[...] = (q * lax.rsqrt(_head_mean(q * q) + EPS) * (gq_ref[...] * scale)).astype(qn_ref.dtype)
        kn_ref[...] = (k * lax.rsqrt(_head_mean(k * k) + EPS) * gk_ref[...]).astype(kn_ref.dtype)
        vb_ref[...] = v_ref[...].astype(vb_ref.dtype)

    W = FOX_WIDTH
    col = lambda j: pl.BlockSpec((tr, W), lambda i: (i, j))
    vec = pl.BlockSpec((1, W), lambda i: (0, 0))
    out = jax.ShapeDtypeStruct((T, W), MXU_DTYPE)
    return pl.pallas_call(
        body, name="fox_prep_fwd", grid=(T // tr,), in_specs=[col(0), col(1), col(2), vec, vec],
        out_specs=[col(0)] * 3, out_shape=[out] * 3, compiler_params=_cparams("parallel"),
    )(P, P, P, gq, gk)


def fox_prep_bwd(P, gq, gk, dqn, dkn, *, tr=1024):
    T = P.shape[0]
    tr = min(tr, T)
    scale = FOX_HEAD_DIM ** -0.5

    def body(q_ref, k_ref, gq_ref, gk_ref, dqn_ref, dkn_ref, dq_ref, dk_ref, dgq_ref, dgk_ref):
        @pl.when(pl.program_id(0) == 0)
        def _():
            dgq_ref[...] = jnp.zeros_like(dgq_ref)
            dgk_ref[...] = jnp.zeros_like(dgk_ref)

        def one(x, g, dn, dx_ref, dg_ref):
            r = lax.rsqrt(_head_mean(x * x) + EPS)
            xhat = x * r
            gd = dn * g
            dx_ref[...] = (r * (gd - xhat * _head_mean(gd * xhat))).astype(dx_ref.dtype)
            return jnp.sum(dn * xhat, axis=0, keepdims=True)

        dgq_ref[...] += scale * one(q_ref[...], gq_ref[...] * scale, dqn_ref[...], dq_ref, dgq_ref)
        dgk_ref[...] += one(k_ref[...], gk_ref[...], dkn_ref[...], dk_ref, dgk_ref)

    W = FOX_WIDTH
    col = lambda j: pl.BlockSpec((tr, W), lambda i: (i, j))
    vec = pl.BlockSpec((1, W), lambda i: (0, 0))
    return pl.pallas_call(
        body, name="fox_prep_bwd", grid=(T // tr,), in_specs=[col(0), col(1), vec, vec, col(0), col(0)],
        out_specs=[col(0), col(0), vec, vec],
        out_shape=[jax.ShapeDtypeStruct((T, W), MXU_DTYPE), jax.ShapeDtypeStruct((T, W), MXU_DTYPE),
                   jax.ShapeDtypeStruct((1, W), f32), jax.ShapeDtypeStruct((1, W), f32)],
        compiler_params=_cparams("arbitrary"),
    )(P, P, gq, gk, dqn, dkn)


def _fox_tile_scores(q, k_ref, ccol_ref, cq, e, j, sl, mask_off):
    tq, tk = FOX_TQ, FOX_TK
    rows = pl.ds(pl.multiple_of(j * tk, tk), tk)
    k = k_ref[rows, sl]
    s = _dot(k, q, NT) + cq - ccol_ref[0, e, rows, :]
    if mask_off is not None:
        key = lax.broadcasted_iota(jnp.int32, (tk, tq), 0) + mask_off
        query = lax.broadcasted_iota(jnp.int32, (tk, tq), 1)
        s = jnp.where(key <= query, s, NEG_INF)
    return s, k, rows


def _fox_sweep(i, update, carry):
    nd = FOX_TQ // FOX_TK
    carry = lax.fori_loop(0, i * nd, lambda j, cr: update(cr, j, None), carry)
    for d in range(nd):
        carry = update(carry, i * nd + d, d * FOX_TK)
    return carry


def fox_core_fwd(qn, kn, vb, ccol, crow, go, *, B):
    T = qn.shape[0]
    S = T // B
    tq = FOX_TQ
    nq = S // tq
    hd = FOX_HEAD_DIM

    def body(q_ref, k_ref, v_ref, ccol_ref, crow_ref, go_ref, o_ref, oa_ref, lse_ref):
        i = pl.program_id(2)
        for e in range(2):
            sl = slice(e * hd, (e + 1) * hd)
            q = q_ref[:, sl]
            cq = crow_ref[0, e, i]

            def update(carry, j, mask_off):
                m, l, acc = carry
                s, _, rows = _fox_tile_scores(q, k_ref, ccol_ref, cq, e, j, sl, mask_off)
                m2 = jnp.maximum(m, jnp.max(s, axis=0, keepdims=True))
                a = jnp.exp(m - m2)
                p = jnp.exp(s - m2)
                return m2, a * l + jnp.sum(p, axis=0, keepdims=True), a * acc + _dot(v_ref[rows, sl], _mx(p), TN)

            def diagonal(carry):
                m, l, acc = carry
                hq = tq // 2
                base = pl.multiple_of(i * tq, tq)
                key = lax.broadcasted_iota(jnp.int32, (hq, tq), 0)
                query = lax.broadcasted_iota(jnp.int32, (hq, tq), 1)
                rows = pl.ds(base, hq)
                s = jnp.where(key <= query, _dot(k_ref[rows, sl], q, NT) + cq - ccol_ref[0, e, rows, :], NEG_INF)
                m2 = jnp.maximum(m, jnp.max(s, axis=0, keepdims=True))
                a, p = jnp.exp(m - m2), jnp.exp(s - m2)
                m, l, acc = m2, a * l + jnp.sum(p, axis=0, keepdims=True), a * acc + _dot(v_ref[rows, sl], _mx(p), TN)
                rows = pl.ds(pl.multiple_of(base + hq, hq), hq)
                s = _dot(k_ref[rows, sl], q[hq:, :], NT) + cq[:, hq:] - ccol_ref[0, e, rows, :]
                s = jnp.where(key[:, :hq] <= query[:, :hq], s, NEG_INF)
                m2 = jnp.maximum(m[:, hq:], jnp.max(s, axis=0, keepdims=True))
                a, p = jnp.exp(m[:, hq:] - m2), jnp.exp(s - m2)
                l2 = a * l[:, hq:] + jnp.sum(p, axis=0, keepdims=True)
                acc2 = a * acc[:, hq:] + _dot(v_ref[rows, sl], _mx(p), TN)
                join = lambda lo, hi: jnp.concatenate([lo[:, :hq], hi], axis=1)
                return join(m, m2), join(l, l2), join(acc, acc2)

            carry = (jnp.full((1, tq), NEG_INF, f32), jnp.zeros((1, tq), f32), jnp.zeros((hd, tq), f32))
            m, l, acc = diagonal(lax.fori_loop(0, i, lambda j, cr: update(cr, j, None), carry))
            o = (acc / l).T
            o_ref[:, sl] = o
            oa_ref[:, sl] = _head_rms(o, go_ref[:, sl])[0].astype(oa_ref.dtype)
            lse_ref[0, e, 0] = m + jnp.log(l)

    W = 2 * hd
    qspec = pl.BlockSpec((tq, W), lambda b, h, i: (b * nq + i, h))
    kspec = pl.BlockSpec((S, W), lambda b, h, i: (b, h))
    return pl.pallas_call(
        body, name="fox_core_fwd", grid=(B, FOX_PAIRS, nq),
        in_specs=[qspec, kspec, kspec, pl.BlockSpec((1, 2, S, 1), lambda b, h, i: (b, h, 0, 0)),
                  pl.BlockSpec((1, 2, nq, 1, tq), lambda b, h, i: (b, h, 0, 0, 0)),
                  pl.BlockSpec((1, W), lambda b, h, i: (0, 0))],
        out_specs=[qspec, qspec, pl.BlockSpec((1, 2, 1, 1, tq), lambda b, h, i: (b, h, i, 0, 0))],
        out_shape=[jax.ShapeDtypeStruct((T, FOX_WIDTH), f32), jax.ShapeDtypeStruct((T, FOX_WIDTH), MXU_DTYPE),
                   jax.ShapeDtypeStruct((B, FOX_HEADS, nq, 1, tq), f32)],
        compiler_params=_cparams("parallel", "parallel", "parallel"),
    )(qn, kn, vb, ccol, crow, go)


def fox_core_bwd(qn, kn, vb, ccol, crow, go, o_raw, lse, d_oab, *, B):
    T = qn.shape[0]
    S = T // B
    tq = FOX_TQ
    nq = S // tq
    hd = FOX_HEAD_DIM

    def body(q_ref, k_ref, v_ref, ccol_ref, crow_ref, go_ref, o_ref, lse_ref, doa_ref,
             dq_ref, dk_ref, dv_ref, dckey_ref, dcrow_ref, dgo_ref, dk_acc, dv_acc, dck_acc):
        b, h, i = pl.program_id(0), pl.program_id(1), pl.program_id(2)

        @pl.when((b == 0) & (h == 0) & (i == 0))
        def _():
            dgo_ref[...] = jnp.zeros_like(dgo_ref)

        @pl.when(i == 0)
        def _():
            dk_acc[...] = jnp.zeros_like(dk_acc)
            dv_acc[...] = jnp.zeros_like(dv_acc)
            dck_acc[...] = jnp.zeros_like(dck_acc)

        for e in range(2):
            sl = slice(e * hd, (e + 1) * hd)
            q = q_ref[:, sl]
            cq = crow_ref[0, e, i]
            lse_e = lse_ref[0, e, 0]
            o = o_ref[:, sl]
            ro = lax.rsqrt(jnp.mean(o * o, axis=-1, keepdims=True) + EPS)
            do, dgo = _head_rms_bwd(o, ro, go_ref[:, sl], doa_ref[:, sl])
            dgo_ref[:, sl] += dgo
            delta = jnp.sum((do * o).T, axis=0, keepdims=True)
            do_b = _mx(do)

            def update(carry, j, mask_off):
                dq, dcq = carry
                s, k, rows = _fox_tile_scores(q, k_ref, ccol_ref, cq, e, j, sl, mask_off)
                p = jnp.exp(s - lse_e)
                dv_acc[e, rows, :] += _dot(_mx(p), do_b, NN)
                ds = p * (_dot(v_ref[rows, sl], do_b, NT) - delta)
                dck_acc[e, rows, :] -= jnp.sum(ds, axis=1, keepdims=True)
                ds_b = _mx(ds)
                dk_acc[e, rows, :] += _dot(ds_b, q, NN)
                return dq + _dot(k, ds_b, TN), dcq + jnp.sum(ds, axis=0, keepdims=True)

            dq, dcq = _fox_sweep(i, update, (jnp.zeros((hd, tq), f32), jnp.zeros((1, tq), f32)))
            dq_ref[:, sl] = dq.T
            dcrow_ref[0, e, 0] = dcq

        @pl.when(i == nq - 1)
        def _():
            for e in range(2):
                sl = slice(e * hd, (e + 1) * hd)
                dk_ref[:, sl] = dk_acc[e]
                dv_ref[:, sl] = dv_acc[e].astype(dv_ref.dtype)
                dckey_ref[0, e] = jnp.transpose(jnp.broadcast_to(dck_acc[e], (S, LANES)))[0:1, :]

    W = 2 * hd
    qspec = pl.BlockSpec((tq, W), lambda b, h, i: (b * nq + i, h))
    kspec = pl.BlockSpec((S, W), lambda b, h, i: (b, h))
    colspec = pl.BlockSpec((1, 2, S, 1), lambda b, h, i: (b, h, 0, 0))
    rowspec = pl.BlockSpec((1, 2, nq, 1, tq), lambda b, h, i: (b, h, 0, 0, 0))
    tilespec = pl.BlockSpec((1, 2, 1, 1, tq), lambda b, h, i: (b, h, i, 0, 0))
    vec = pl.BlockSpec((1, W), lambda b, h, i: (0, 0))
    return pl.pallas_call(
        body, name="fox_core_bwd", grid=(B, FOX_PAIRS, nq),
        in_specs=[qspec, kspec, kspec, colspec, rowspec, vec, qspec, tilespec, qspec],
        out_specs=[qspec, kspec, kspec, pl.BlockSpec((1, 2, 1, S), lambda b, h, i: (b, h, 0, 0)), tilespec, vec],
        out_shape=[jax.ShapeDtypeStruct((T, FOX_WIDTH), f32), jax.ShapeDtypeStruct((T, FOX_WIDTH), f32),
                   jax.ShapeDtypeStruct((T, FOX_WIDTH), MXU_DTYPE),
                   jax.ShapeDtypeStruct((B, FOX_HEADS, 1, S), f32), jax.ShapeDtypeStruct((B, FOX_HEADS, nq, 1, tq), f32),
                   jax.ShapeDtypeStruct((1, W), f32)],
        scratch_shapes=[pltpu.VMEM((2, S, hd), f32), pltpu.VMEM((2, S, hd), f32), pltpu.VMEM((2, S, 1), f32)],
        compiler_params=_cparams("arbitrary", "arbitrary", "arbitrary"),
    )(qn, kn, vb, ccol, crow, go, o_raw, lse, d_oab)


def _lane_mask(lo, hi, shape):
    lane = lax.broadcasted_iota(jnp.int32, shape, 1)
    return (lane >= lo) & (lane < hi)


def _cumsum_rows(v, period, reverse=False):
    n = v.shape[0]
    pos = lax.broadcasted_iota(jnp.int32, v.shape, 0) % period
    sh = 1
    while sh < period:
        if reverse:
            v = v + jnp.where(pos + sh < period, pltpu.roll(v, n - sh, 0), 0.0)
        else:
            v = v + jnp.where(pos >= sh, pltpu.roll(v, sh, 0), 0.0)
        sh *= 2
    return v


def _gate_values(z, bias, alog):
    zb = z + bias
    ls = jax.nn.log_sigmoid(zb)
    beta = jax.nn.sigmoid(z)
    g = -jnp.exp(alog) * jax.nn.softplus(zb)
    return zb, ls, beta, g


def gates_fwd(P, bias, alog, *, B):
    T = P.shape[0]
    S = T // B

    def body(z_ref, bias_ref, alog_ref, o_ref):
        z = z_ref[...]
        _, ls, beta, g = _gate_values(z, bias_ref[...], alog_ref[...])
        c = _cumsum_rows(ls, S)
        gc = _cumsum_rows(g, GDN_CHUNK)
        o = jnp.where(_lane_mask(SM_F, SM_F + FOX_HEADS, z.shape), c, 0.0)
        o = jnp.where(_lane_mask(SM_B, SM_B + GDN_HEADS, z.shape), beta, o)
        o = jnp.where(_lane_mask(SM_A, SM_A + GDN_HEADS, z.shape), gc, o)
        o_ref[...] = o

    vec = pl.BlockSpec((1, LANES), lambda b: (0, 0))
    return pl.pallas_call(
        body, name="gates_fwd", grid=(B,),
        in_specs=[pl.BlockSpec((S, LANES), lambda b: (b, COL_SMALL // LANES)), vec, vec],
        out_specs=pl.BlockSpec((S, LANES), lambda b: (b, 0)),
        out_shape=jax.ShapeDtypeStruct((T, LANES), f32),
        compiler_params=_cparams("parallel"),
    )(P, bias, alog)


def gates_bwd(P, bias, alog, dgates, *, B):
    T = P.shape[0]
    S = T // B

    def body(z_ref, bias_ref, alog_ref, dg_ref, dz_ref, par_ref):
        z = z_ref[...]
        zb, ls, beta, g = _gate_values(z, bias_ref[...], alog_ref[...])
        d = dg_ref[...]
        dls = _cumsum_rows(d, S, reverse=True)
        dgr = _cumsum_rows(d, GDN_CHUNK, reverse=True)
        sig = jax.nn.sigmoid(zb)
        dz_f = dls * (1.0 - sig)
        dz_b = d * beta * (1.0 - beta)
        dz_a = dgr * (-jnp.exp(alog_ref[...])) * sig
        dz = jnp.where(_lane_mask(SM_F, SM_F + FOX_HEADS, z.shape), dz_f, 0.0)
        dz = jnp.where(_lane_mask(SM_B, SM_B + GDN_HEADS, z.shape), dz_b, dz)
        dz = jnp.where(_lane_mask(SM_A, SM_A + GDN_HEADS, z.shape), dz_a, dz)
        dz_ref[...] = dz.astype(dz_ref.dtype)

        @pl.when(pl.program_id(0) == 0)
        def _():
            par_ref[...] = jnp.zeros_like(par_ref)

        dalog = jnp.where(_lane_mask(SM_A, SM_A + GDN_HEADS, z.shape), dgr * g, 0.0)
        par_ref[0:1, :] += jnp.sum(dz, axis=0, keepdims=True)
        par_ref[1:2, :] += jnp.sum(dalog, axis=0, keepdims=True)

    vec = pl.BlockSpec((1, LANES), lambda b: (0, 0))
    return pl.pallas_call(
        body, name="gates_bwd", grid=(B,),
        in_specs=[pl.BlockSpec((S, LANES), lambda b: (b, COL_SMALL // LANES)), vec, vec,
                  pl.BlockSpec((S, LANES), lambda b: (b, 0))],
        out_specs=[pl.BlockSpec((S, LANES), lambda b: (b, 0)), pl.BlockSpec((8, LANES), lambda b: (0, 0))],
        out_shape=[jax.ShapeDtypeStruct((T, LANES), MXU_DTYPE), jax.ShapeDtypeStruct((8, LANES), f32)],
        compiler_params=_cparams("arbitrary"),
    )(P, bias, alog, dgates)


GDN_BLOCKS = 3 * GDN_HEADS


def _shift_rows(v, d, reverse=False):
    if d == 0:
        return v
    n = v.shape[0]
    row = lax.broadcasted_iota(jnp.int32, v.shape, 0)
    if reverse:
        return jnp.where(row + d < n, pltpu.roll(v, n - d, 0), 0.0)
    return jnp.where(row >= d, pltpu.roll(v, d, 0), 0.0)


def _conv_silu(x, w):
    pre = sum(w[j:j + 1, :] * _shift_rows(x, CONV_WIDTH - 1 - j) for j in range(CONV_WIDTH))
    return pre, pre * jax.nn.sigmoid(pre)


def gdn_prep_fwd(P, conv_w, *, B):
    T = P.shape[0]
    S = T // B

    def body(x_ref, w_ref, o_ref):
        _, y = _conv_silu(x_ref[...], w_ref[...])
        yn = y * lax.rsqrt(jnp.sum(y * y, axis=-1, keepdims=True) + EPS)
        o_ref[...] = jnp.where(pl.program_id(1) < 2 * GDN_HEADS, yn, y)

    return pl.pallas_call(
        body, name="gdn_prep_fwd", grid=(B, GDN_BLOCKS),
        in_specs=[pl.BlockSpec((S, LANES), lambda b, j: (b, COL_GDN // LANES + j)),
                  pl.BlockSpec((CONV_WIDTH, LANES), lambda b, j: (0, j))],
        out_specs=pl.BlockSpec((S, LANES), lambda b, j: (b, j)),
        out_shape=jax.ShapeDtypeStruct((T, 3 * GDN_WIDTH), f32),
        compiler_params=_cparams("parallel", "parallel"),
    )(P, conv_w)


def gdn_prep_bwd(P, conv_w, dGq, dGk, dGv, *, B):
    T = P.shape[0]
    S = T // B
    H = GDN_HEADS

    def body(x_ref, w_ref, dq_ref, dk_ref, dv_ref, dx_ref, dw_ref):
        x, w = x_ref[...], w_ref[...]
        pre, y = _conv_silu(x, w)
        jb = pl.program_id(0)
        dn = jnp.where(jb < H, dq_ref[...], jnp.where(jb < 2 * H, dk_ref[...], dv_ref[...]))
        r = lax.rsqrt(jnp.sum(y * y, axis=-1, keepdims=True) + EPS)
        n = y * r
        dy_norm = r * (dn - n * jnp.sum(dn * n, axis=-1, keepdims=True))
        dy = jnp.where(pl.program_id(0) < 2 * GDN_HEADS, dy_norm, dn)
        sg = jax.nn.sigmoid(pre)
        dpre = dy * (sg * (1.0 + pre * (1.0 - sg)))
        dx = sum(w[j:j + 1, :] * _shift_rows(dpre, CONV_WIDTH - 1 - j, reverse=True) for j in range(CONV_WIDTH))
        dx_ref[...] = dx.astype(dx_ref.dtype)

        @pl.when(pl.program_id(1) == 0)
        def _():
            dw_ref[...] = jnp.zeros_like(dw_ref)

        for j in range(CONV_WIDTH):
            dw_ref[j:j + 1, :] += jnp.sum(dpre * _shift_rows(x, CONV_WIDTH - 1 - j), axis=0, keepdims=True)

    return pl.pallas_call(
        body, name="gdn_prep_bwd", grid=(GDN_BLOCKS, B),
        in_specs=[pl.BlockSpec((S, LANES), lambda j, b: (b, COL_GDN // LANES + j)),
                  pl.BlockSpec((CONV_WIDTH, LANES), lambda j, b: (0, j))]
        + [pl.BlockSpec((S, LANES), lambda j, b, t=t: (jnp.where(j // H == t, b, 0), jnp.clip(j - t * H, 0, H - 1)))
           for t in range(3)],
        out_specs=[pl.BlockSpec((S, LANES), lambda j, b: (b, j)),
                   pl.BlockSpec((CONV_WIDTH, LANES), lambda j, b: (0, j))],
        out_shape=[jax.ShapeDtypeStruct((T, 3 * GDN_WIDTH), MXU_DTYPE),
                   jax.ShapeDtypeStruct((CONV_WIDTH, 3 * GDN_WIDTH), f32)],
        compiler_params=_cparams("arbitrary", "arbitrary"),
    )(P, conv_w, dGq, dGk, dGv)


GDN_GROUP = 16
GDN_GROUP_FWD = 16
B_NN = (((2,), (1,)), ((0,), (0,)))
B_NT = (((2,), (2,)), ((0,), (0,)))
B_TN = (((1,), (1,)), ((0,), (0,)))


def _bmm(a, b, dims, precision=None):
    if precision is None:
        a, b = _mx(a), _mx(b)
    return lax.dot_general(a, b, dims, preferred_element_type=f32, precision=precision)


def _tri_inverse(A):
    C = A.shape[-1]
    row = lax.broadcasted_iota(jnp.int32, A.shape, 1)
    col = lax.broadcasted_iota(jnp.int32, A.shape, 2)
    eye = (row == col).astype(f32)
    X = jnp.where((row // 4) == (col // 4), -A, 0.0)
    X2 = _bmm(X, X, B_NN, INV_PRECISION)
    Tm = eye + X + X2 + _bmm(X, X2, B_NN, INV_PRECISION)
    b = 4
    while b < C:
        off = ((row // (2 * b)) == (col // (2 * b))) & ((row // b) != (col // b))
        Tm = Tm - _bmm(_bmm(Tm, jnp.where(off, A, 0.0), B_NN, INV_PRECISION), Tm, B_NN, INV_PRECISION)
        b *= 2
    return Tm


def _pick_lane(block, lane_idx):
    lane = lax.broadcasted_iota(jnp.int32, block.shape, 1)
    return jnp.sum(jnp.where(lane == lane_idx, block, 0.0), axis=1, keepdims=True)


def _gdn_local(q, k, v, beta, gc, Tm=None, uwm=None):
    C = GDN_CHUNK
    n = q.shape[0] // C
    q = q.reshape(n, C, -1) * (GDN_HEAD_DIM ** -0.5)
    k = k.reshape(n, C, -1)
    v = v.reshape(n, C, -1)
    beta = beta.reshape(n, C, 1)
    gc = gc.reshape(n, C, 1)
    row = lax.broadcasted_iota(jnp.int32, (n, C, C), 1)
    col = lax.broadcasted_iota(jnp.int32, (n, C, C), 2)
    gcT = jnp.swapaxes(jnp.broadcast_to(gc, (n, C, C)), 1, 2)
    D = jnp.exp(jnp.where(row >= col, gc - gcT, NEG_INF))
    kb = k * beta
    vb = v * beta
    A = jnp.where(row > col, _bmm(kb, k, B_NT) * D, 0.0)
    Gam = jnp.exp(gc)
    kg = kb * Gam
    gl = gc[:, C - 1:C, :]
    kdec = jnp.exp(gl - gc)
    loc = dict(q=q, k=k, v=v, beta=beta, gc=gc, D=D, kb=kb, vb=vb, A=A, Gam=Gam, kg=kg,
               kdec=kdec, kd=k * kdec, qg=q * Gam, gam=jnp.exp(gl), row=row, col=col)
    uwm = Tm is None if uwm is None else uwm
    Tm = _tri_inverse(A) if Tm is None else Tm.reshape(n, C, C)
    if uwm:
        loc.update(u=_bmm(Tm, vb, B_NN), w=_bmm(Tm, kg, B_NN), M=_bmm(q, k, B_NT) * D)
    loc["Tm"] = Tm
    return loc


def _gdn_store_local(loc, r0, u_s, w_s, qg_s, kd_s, M_s, gam_s, c0):
    n = loc["u"].shape[0]
    R = n * GDN_CHUNK
    u_s[pl.ds(r0, R), :] = loc["u"].reshape(R, -1)
    w_s[pl.ds(r0, R), :] = loc["w"].reshape(R, -1)
    qg_s[pl.ds(r0, R), :] = loc["qg"].reshape(R, -1)
    kd_s[pl.ds(r0, R), :] = loc["kd"].reshape(R, -1)
    M_s[pl.ds(r0, R), :] = loc["M"].reshape(R, -1)
    gam_s[pl.ds(c0, n)] = jnp.broadcast_to(loc["gam"], (n, 1, LANES))


def _gdn_specs(S):
    blk = lambda off: pl.BlockSpec((S, LANES), lambda b, h: (b, off + h))
    return blk


def gdn_fwd(G, gates, P, g_on, *, B):
    T = G.shape[0]
    S = T // B
    C = GDN_CHUNK
    N = S // C
    grp = min(GDN_GROUP_FWD, N)
    R = grp * C
    hd = GDN_HEAD_DIM

    def body(q_ref, k_ref, v_ref, gt_ref, z_ref, gon_ref, o_ref, ob_ref, st_ref, tm_ref, A_s, B_s, Q_s, O_s, gam_s):
        h = pl.program_id(1)

        def local(gi, carry):
            r0 = pl.multiple_of(gi * R, R)
            gt = gt_ref[pl.ds(r0, R), :]
            loc = _gdn_local(q_ref[pl.ds(r0, R), :], k_ref[pl.ds(r0, R), :], v_ref[pl.ds(r0, R), :],
                             _pick_lane(gt, SM_B + h), _pick_lane(gt, SM_A + h))
            chunks = pl.ds(gi * grp, grp)
            tm_ref[0, 0, pl.ds(r0, R), :] = loc["Tm"].reshape(R, C)
            A_s[chunks] = -_bmm(loc["kd"], loc["w"], B_TN)
            B_s[chunks] = _bmm(loc["kd"], loc["u"], B_TN)
            Q_s[pl.ds(r0, R), :] = (loc["qg"] - _bmm(loc["M"], loc["w"], B_NN)).reshape(R, hd)
            O_s[pl.ds(r0, R), :] = _bmm(loc["M"], loc["u"], B_NN).reshape(R, hd)
            gam_s[chunks] = jnp.broadcast_to(loc["gam"], (grp, 1, LANES))
            return carry

        lax.fori_loop(0, N // grp, local, 0)

        def step(n, state):
            st_ref[0, 0, n] = state
            return state * gam_s[n] + _dotm(A_s[n], state, NN) + B_s[n]

        lax.fori_loop(0, N, step, jnp.zeros((hd, hd), f32))

        def outputs(gi, carry):
            r0 = pl.multiple_of(gi * R, R)
            Q = Q_s[pl.ds(r0, R), :].reshape(grp, C, hd)
            o = _bmm(Q, st_ref[0, 0, pl.ds(gi * grp, grp)], B_NN).reshape(R, hd) + O_s[pl.ds(r0, R), :]
            o_ref[pl.ds(r0, R), :] = o
            return carry

        lax.fori_loop(0, N // grp, outputs, 0)
        o = o_ref[...]
        z = z_ref[...]
        ob_ref[...] = (_head_rms(o, gon_ref[...])[0] * (z * jax.nn.sigmoid(z))).astype(ob_ref.dtype)

    blk = lambda off: pl.BlockSpec((S, LANES), lambda b, h: (b, off + h))
    rows = lambda: pltpu.VMEM((S, hd), f32)
    return pl.pallas_call(
        body, name="gdn_fwd", grid=(B, GDN_HEADS),
        in_specs=[blk(0), blk(GDN_HEADS), blk(2 * GDN_HEADS), pl.BlockSpec((S, LANES), lambda b, h: (b, 0)),
                  blk(COL_Z // LANES), pl.BlockSpec((1, hd), lambda b, h: (0, 0))],
        out_specs=[blk(0), blk(0), pl.BlockSpec((1, 1, N, hd, hd), lambda b, h: (b, h, 0, 0, 0)),
                   pl.BlockSpec((1, 1, S, C), lambda b, h: (b, h, 0, 0))],
        out_shape=[jax.ShapeDtypeStruct((T, GDN_WIDTH), f32), jax.ShapeDtypeStruct((T, GDN_WIDTH), MXU_DTYPE),
                   jax.ShapeDtypeStruct((B, GDN_HEADS, N, hd, hd), f32), jax.ShapeDtypeStruct((B, GDN_HEADS, S, C), f32)],
        scratch_shapes=[pltpu.VMEM((N, hd, hd), f32), pltpu.VMEM((N, hd, hd), f32), rows(), rows(),
                        pltpu.VMEM((N, 1, LANES), f32)],
        compiler_params=_cparams("parallel", "parallel"),
    )(G, G, G, gates, P, g_on)


def gdn_bwd(G, gates, P, g_on, o_raw, states, tm, d_oab, *, B):
    T = G.shape[0]
    S = T // B
    C = GDN_CHUNK
    N = S // C
    grp = min(GDN_GROUP, N)
    R = grp * C
    hd = GDN_HEAD_DIM

    def body(q_ref, k_ref, v_ref, gt_ref, z_ref, gon_ref, o_ref, st_ref, tm_ref, dob_ref,
             dq_ref, dk_ref, dv_ref, dgt_ref, dz_ref, dgon_ref,
             u_s, w_s, M_s, gam_s, do_s, A_s, C_s, dst_s):
        b, h = pl.program_id(0), pl.program_id(1)

        @pl.when((b == 0) & (h == 0))
        def _():
            dgon_ref[...] = jnp.zeros_like(dgon_ref)

        @pl.when(h == 0)
        def _():
            dgt_ref[...] = jnp.zeros_like(dgt_ref)

        def group_inputs(gi, uwm):
            r0 = pl.multiple_of(gi * R, R)
            gt = gt_ref[pl.ds(r0, R), :]
            return r0, _gdn_local(q_ref[pl.ds(r0, R), :], k_ref[pl.ds(r0, R), :], v_ref[pl.ds(r0, R), :],
                                  _pick_lane(gt, SM_B + h), _pick_lane(gt, SM_A + h), tm_ref[0, 0, pl.ds(r0, R), :], uwm)

        def local(gi, carry):
            r0, loc = group_inputs(gi, True)
            rows, chunks = pl.ds(r0, R), pl.ds(gi * grp, grp)
            u_s[rows, :] = loc["u"].reshape(R, hd)
            w_s[rows, :] = loc["w"].reshape(R, hd)
            M_s[rows, :] = loc["M"].reshape(R, C)
            gam_s[chunks] = jnp.broadcast_to(loc["gam"], (grp, 1, LANES))
            o, z, gon = o_ref[rows, :], z_ref[rows, :], gon_ref[...]
            dob = dob_ref[rows, :]
            on, ro = _head_rms(o, gon)
            sz = jax.nn.sigmoid(z)
            dz_ref[rows, :] = (dob * on * (sz * (1.0 + z * (1.0 - sz)))).astype(dz_ref.dtype)
            do, dgon = _head_rms_bwd(o, ro, gon, dob * (z * sz))
            do_s[rows, :] = do
            dgon_ref[...] += dgon
            A_s[chunks] = -_bmm(loc["kd"], loc["w"], B_TN)
            C_s[chunks] = _bmm(loc["qg"] - _bmm(loc["M"], loc["w"], B_NN), do.reshape(grp, C, hd), B_TN)
            return carry

        lax.fori_loop(0, N // grp, local, 0)

        def step(t, dS):
            n = N - 1 - t
            dst_s[n] = dS
            return dS * gam_s[n] + _dotm(A_s[n], dS, TN) + C_s[n]

        lax.fori_loop(0, N, step, jnp.zeros((hd, hd), f32))

        def finish(gi, carry):
            r0, L = group_inputs(gi, False)
            n = grp
            rows, chunks = pl.ds(r0, R), pl.ds(gi * grp, grp)
            g3 = lambda ref: ref[rows, :].reshape(n, C, -1)
            u, w, do = g3(u_s), g3(w_s), g3(do_s)
            L["M"] = g3(M_s)
            state, dS = st_ref[0, 0, chunks], dst_s[chunks]
            v_new = u - _bmm(w, state, B_NN)
            du = _bmm(L["M"], do, B_TN) + _bmm(L["kd"], dS, B_NN)
            dw = -_bmm(du, state, B_NT)
            dqg = _bmm(do, state, B_NT)
            dM = _bmm(do, v_new, B_NT)
            dkd = _bmm(v_new, dS, B_NT)
            dgl_state = jnp.sum(jnp.sum(dS * state, axis=2, keepdims=True), axis=1, keepdims=True) * L["gam"]
            TmT = jnp.swapaxes(L["Tm"], 1, 2)
            dTm = _bmm(du, L["vb"], B_NT) + _bmm(dw, L["kg"], B_NT)
            dvb = _bmm(TmT, du, B_NN)
            dkg = _bmm(TmT, dw, B_NN)
            dA = jnp.where(L["row"] > L["col"], -_bmm(_bmm(TmT, dTm, B_NN), TmT, B_NN), 0.0)
            dKK = dA * L["D"]
            dQK = dM * L["D"]
            dkb = _bmm(dKK, L["k"], B_NN) + dkg * L["Gam"]
            dk = (_bmm(dKK, L["kb"], B_TN) + _bmm(dQK, L["q"], B_TN) + dkd * L["kdec"] + L["beta"] * dkb)
            dq = (_bmm(dQK, L["k"], B_NN) + dqg * L["Gam"]) * (GDN_HEAD_DIM ** -0.5)
            E = dA * L["A"] + dM * L["M"]
            r = jnp.sum(dkd * L["kd"], axis=-1, keepdims=True)
            dgc = (jnp.sum(E, axis=2, keepdims=True) - jnp.sum(jnp.swapaxes(E, 1, 2), axis=2, keepdims=True)
                   + jnp.sum(dkg * L["kg"], axis=-1, keepdims=True) + jnp.sum(dqg * L["qg"], axis=-1, keepdims=True) - r)
            dgl = jnp.sum(r, axis=1, keepdims=True) + dgl_state
            rowc = lax.broadcasted_iota(jnp.int32, (n, C, 1), 1)
            dgc = dgc + jnp.where(rowc == C - 1, dgl, 0.0)
            dbeta = jnp.sum(dkb * L["k"], axis=-1, keepdims=True) + jnp.sum(dvb * L["v"], axis=-1, keepdims=True)
            dq_ref[rows, :] = dq.reshape(R, hd)
            dk_ref[rows, :] = dk.reshape(R, hd)
            dv_ref[rows, :] = (L["beta"] * dvb).reshape(R, hd)
            lane = lax.broadcasted_iota(jnp.int32, (R, LANES), 1)
            dgt_ref[rows, :] += (jnp.where(lane == SM_B + h, dbeta.reshape(R, 1), 0.0)
                                 + jnp.where(lane == SM_A + h, dgc.reshape(R, 1), 0.0))
            return carry

        lax.fori_loop(0, N // grp, finish, 0)

    blk = lambda off: pl.BlockSpec((S, LANES), lambda b, h: (b, off + h))
    rows = lambda: pltpu.VMEM((S, hd), f32)
    return pl.pallas_call(
        body, name="gdn_bwd", grid=(B, GDN_HEADS),
        in_specs=[blk(0), blk(GDN_HEADS), blk(2 * GDN_HEADS), pl.BlockSpec((S, LANES), lambda b, h: (b, 0)),
                  blk(COL_Z // LANES), pl.BlockSpec((1, hd), lambda b, h: (0, 0)), blk(0),
                  pl.BlockSpec((1, 1, N, hd, hd), lambda b, h: (b, h, 0, 0, 0)),
                  pl.BlockSpec((1, 1, S, C), lambda b, h: (b, h, 0, 0)), blk(GDN_HEADS)],
        out_specs=[blk(0), blk(0), blk(0), pl.BlockSpec((S, LANES), lambda b, h: (b, 0)), blk(0),
                   pl.BlockSpec((1, hd), lambda b, h: (0, 0))],
        out_shape=[jax.ShapeDtypeStruct((T, GDN_WIDTH), f32), jax.ShapeDtypeStruct((T, GDN_WIDTH), f32),
                   jax.ShapeDtypeStruct((T, GDN_WIDTH), f32), jax.ShapeDtypeStruct((T, LANES), f32),
                   jax.ShapeDtypeStruct((T, GDN_WIDTH), MXU_DTYPE), jax.ShapeDtypeStruct((1, hd), f32)],
        scratch_shapes=[rows(), rows(), pltpu.VMEM((S, C), f32), pltpu.VMEM((N, 1, LANES), f32), rows(),
                        pltpu.VMEM((N, hd, hd), f32), pltpu.VMEM((N, hd, hd), f32), pltpu.VMEM((N, hd, hd), f32)],
        compiler_params=_cparams("arbitrary", "arbitrary"),
    )(G, G, G, gates, P, g_on, o_raw, states, tm, d_oab)


IN_SPLIT = (0, 1536, 1544, 3080, 3088, 3600)


IN_SHARD = IN_DIM // 4
IN_SHARD_PAD = 928


def align_w_in_t(wt):
    s = IN_SPLIT
    pad = jnp.zeros((IN_ALIGNED - IN_DIM, wt.shape[1]), wt.dtype)
    return jnp.concatenate([wt[s[0]:s[1]], wt[s[2]:s[3]], wt[s[4]:s[5]], wt[s[1]:s[2]], wt[s[3]:s[4]], pad], axis=0)


def unalign_w_in_t(wa):
    return jnp.concatenate([wa[0:1536], wa[COL_SMALL:COL_SMALL + 8], wa[1536:3072],
                            wa[COL_SMALL + 8:COL_SMALL + 16], wa[3072:3584]], axis=0)


IN_SEGMENTS = (((0, 1536), 0), ((1536, 1544), COL_SMALL), ((1544, 3080), 1536), ((3080, 3088), COL_SMALL + 8),
               ((3088, 3600), 3072))


def align_w_in_slots(slots):
    pieces = []
    for (lo, hi), _ in sorted(IN_SEGMENTS, key=lambda seg: seg[1]):
        for k in range(N_CHIPS):
            a, b = max(lo, k * IN_SHARD), min(hi, (k + 1) * IN_SHARD)
            if a < b:
                pieces.append(slots[k, a - k * IN_SHARD:b - k * IN_SHARD])
    pieces.append(jnp.zeros((IN_ALIGNED - IN_DIM, slots.shape[2]), slots.dtype))
    return jnp.concatenate(pieces, axis=0)


def unalign_to_slots(wa):
    slots = []
    for k in range(N_CHIPS):
        lo, hi = k * IN_SHARD, (k + 1) * IN_SHARD
        pieces = []
        for (a, b), first in IN_SEGMENTS:
            x, y = max(a, lo), min(b, hi)
            if x < y:
                pieces.append(wa[first + x - a:first + y - a])
        pieces.append(jnp.zeros((IN_SHARD_PAD - IN_SHARD, wa.shape[1]), wa.dtype))
        slots.append(jnp.concatenate(pieces, axis=0))
    return jnp.stack(slots)


def _lanes_vec(pieces):
    v = jnp.zeros((1, LANES), f32)
    for off, a in pieces:
        v = lax.dynamic_update_slice(v, a.astype(f32), (0, off))
    return v


def local_step(x, mem, target, w, sp, *, B):
    T = x.shape[0]
    S = T // B
    gq8, gk8 = jnp.tile(sp["fox_qnorm_g"], (1, FOX_HEADS)), jnp.tile(sp["fox_knorm_g"], (1, FOX_HEADS))
    go2 = jnp.tile(sp["fox_onorm_g"], (1, 2))
    bias = _lanes_vec([(SM_F, sp["fox_f_bias"]), (SM_A, sp["gdn_dt_bias"])])
    alog = _lanes_vec([(SM_A, sp["gdn_A_log"])])

    h1 = rms_fwd(x, sp["norm_mix_g"], name="rms_mix")
    P = matmul(h1, w["wa_t"], tb=True, name="mm_in", tn=IN_TILE)
    gates = gates_fwd(P, bias, alog, B=B)
    c = gates[:, SM_F:SM_F + FOX_HEADS].reshape(B, S, FOX_HEADS).transpose(0, 2, 1)
    ccol, crow = c[..., None], c.reshape(B, FOX_HEADS, S // FOX_TQ, 1, FOX_TQ)
    qn, kn, vb = fox_prep_fwd(P, gq8, gk8)
    o_raw, o_a, lse = fox_core_fwd(qn, kn, vb, ccol, crow, go2, B=B)
    G = gdn_prep_fwd(P, w["conv_w"], B=B)
    ob_raw, o_b, states, gdn_tm = gdn_fwd(G, gates, P, sp["gdn_onorm_g"], B=B)
    oab = jnp.concatenate([o_a, o_b], axis=1)
    if "late" in w:
        w = {**w, **w["late"](oab)}
    x2, hq = matmul_rows(oab, w["w_out"], (x, sp["norm_xattn_g"]), mode="rms_fwd", name="mm_out_rms")
    hm = rms_fwd(mem, sp["mem_norm_g"], name="rms_mem")
    cq = matmul(hq, w["w_cq"], name="mm_cq")
    ckv = matmul(hm, w["w_ckv"], name="mm_ckv")
    co = xattn_fwd(cq, ckv, sp["xattn_qnorm_g"], sp["xattn_knorm_g"], B=B)
    x3, hf = matmul_rows(co, w["w_co"], (x2, sp["norm_mlp_g"]), mode="rms_fwd", name="mm_co_rms")
    act = matmul(hf, w["w_mlp1"], b_stacked=True, relu2_out=True, out_dtype=MXU_DTYPE, name="mm_mlp1")
    dy, dy_op, loss = matmul_rows(act, w["w_mlp2"], (x3, target), mode="loss", name="mm_mlp2_loss")

    da = matmul(dy_op, w["w_mlp2"], tb=True, relu2_bwd_aux=act, out_dtype=MXU_DTYPE, name="mm_d_act")
    g_mlp2 = matmul(act, dy_op, ta=True, out_dtype=WIRE_DTYPE, name="mm_g_mlp2")
    g_mlp1 = matmul(hf, da, ta=True, out_stacked=True, out_dtype=WIRE_DTYPE, name="mm_g_mlp1")
    by_rows = lambda g: g.reshape(N_CHIPS, g.shape[0] // N_CHIPS, g.shape[1])
    early = w.get("grads_ready", lambda grads: jnp.zeros((1, 1), f32))
    tok = early(dict(w_mlp1=g_mlp1, w_mlp2=by_rows(g_mlp2)))[0, 0]
    dx3, g_norm_mlp = matmul_rows(da, w["w_mlp1"], (x3, sp["norm_mlp_g"] + tok, dy), mode="rms_bwd", tb=True,
                                  b_stacked=True, name="mm_d_hf_rms")
    dco = matmul(dx3, w["w_co"], tb=True, name="mm_d_co")
    g_co = matmul(co, dx3, ta=True, out_dtype=WIRE_DTYPE, name="mm_g_co")
    g_co = g_co.reshape(XATTN_WIDTH, N_CHIPS, D_MODEL // N_CHIPS).transpose(1, 0, 2)
    dcq, dckv, g_xq, g_xk = xattn_bwd(cq, ckv, sp["xattn_qnorm_g"], sp["xattn_knorm_g"], dco, B=B)
    g_cq = matmul(hq, dcq, ta=True, out_dtype=WIRE_DTYPE, name="mm_g_cq")
    g_ckv = matmul(hm, dckv, ta=True, out_dtype=WIRE_DTYPE, name="mm_g_ckv")
    _, g_mem_norm = matmul_rows(dckv, w["w_ckv"], (mem, sp["mem_norm_g"], None), mode="rms_bwd", tb=True, name="mm_d_hm_rms")
    dx2, g_norm_xattn = matmul_rows(dcq, w["w_cq"], (x2, sp["norm_xattn_g"], dx3), mode="rms_bwd", tb=True, name="mm_d_hq_rms")
    doab = matmul(dx2, w["w_out"], tb=True, name="mm_d_oab")
    g_out = matmul(oab, dx2, ta=True, out_dtype=WIRE_DTYPE, name="mm_g_out")
    tok = early(dict(w_co=g_co, w_cq=by_rows(g_cq), w_ckv=by_rows(g_ckv), w_out=by_rows(g_out)))[0, 0]
    dqn, dkn, dv_f, dckey, dcrow, dgo2 = fox_core_bwd(qn, kn, vb, ccol, crow, go2 + tok, o_raw, lse, doab, B=B)
    dq_f, dk_f, dgq8, dgk8 = fox_prep_bwd(P, gq8, gk8, dqn, dkn)
    dGq, dGk, dGv, dgt, dz, g_gdn_on = gdn_bwd(G, gates, P, sp["gdn_onorm_g"], ob_raw, states, gdn_tm, doab, B=B)
    dPg, g_conv = gdn_prep_bwd(P, w["conv_w"], dGq, dGk, dGv, B=B)
    dc = (dckey[:, :, 0, :] + dcrow.reshape(B, FOX_HEADS, S)).transpose(0, 2, 1).reshape(T, FOX_HEADS)
    dgates = dgt + jnp.pad(dc, ((0, 0), (SM_F, LANES - SM_F - FOX_HEADS)))
    dsmall, par = gates_bwd(P, bias, alog, dgates, B=B)
    dP = jnp.concatenate([dq_f, dk_f, dv_f, dPg, dz, dsmall, jnp.zeros((T, IN_ALIGNED - COL_SMALL - LANES), MXU_DTYPE)], axis=1)
    g_wa = matmul(dP, h1, ta=True, out_dtype=WIRE_DTYPE, name="mm_g_in", tm=IN_TILE)
    g_in = unalign_to_slots(g_wa)
    tok = early(dict(w_in=g_in))[0, 0]
    dx, g_norm_mix = matmul_rows(dP, w["wa_t"], (x, sp["norm_mix_g"] + tok, dx2), mode="rms_bwd", tk=IN_TILE,
                                 name="mm_d_h1_rms")

    fold = lambda g: jnp.sum(g.reshape(-1, FOX_HEAD_DIM), axis=0, keepdims=True)
    big = dict(w_in=g_in, w_out=by_rows(g_out), w_cq=by_rows(g_cq), w_ckv=by_rows(g_ckv), w_co=g_co, w_mlp1=g_mlp1,
               w_mlp2=by_rows(g_mlp2))
    small = dict(norm_mix_g=g_norm_mix, fox_qnorm_g=fold(dgq8), fox_knorm_g=fold(dgk8),
                 fox_f_bias=par[0:1, SM_F:SM_F + FOX_HEADS], fox_onorm_g=fold(dgo2), gdn_conv_w=g_conv,
                 gdn_A_log=par[1:2, SM_A:SM_A + GDN_HEADS], gdn_dt_bias=par[0:1, SM_A:SM_A + GDN_HEADS],
                 gdn_onorm_g=g_gdn_on, norm_xattn_g=g_norm_xattn, mem_norm_g=g_mem_norm,
                 xattn_qnorm_g=g_xq, xattn_knorm_g=g_xk, norm_mlp_g=g_norm_mlp)
    return loss, dx, big, small


MESH_IDS = pl.DeviceIdType.MESH
N_CHIPS = 4
HBM_SPEC = pl.BlockSpec(memory_space=pltpu.HBM)
PACK_ROWS = 30720
PACK_HALF = PACK_ROWS // 2
PACK_BLOCK = 3072


def _place():
    return lax.axis_index("x"), lax.axis_index("y"), lax.axis_index("c")


def _other_chips(x, y):
    return [(1 - x, y), (x, 1 - y), (1 - x, 1 - y)]


def _remote(src, dst, send_sem, recv_sem, to):
    return pltpu.make_async_remote_copy(src_ref=src, dst_ref=dst, send_sem=send_sem, recv_sem=recv_sem,
                                        device_id=to, device_id_type=MESH_IDS)


def all_gather_shards(packed):
    half = PACK_HALF

    def body(src_ref, out_ref, send_sems, recv_sems):
        x, y, c = _place()
        me_chip = 2 * x + y
        sibling = (x, y, 1 - c)
        chips = _other_chips(x, y)

        def rows(chip, core):
            return out_ref.at[chip, pl.ds(core * half, half), :]

        sends = [_remote(src_ref.at[pl.ds(c * half, half), :], rows(me_chip, c), send_sems.at[j], recv_sems.at[j], (px, py, c))
                 for j, (px, py) in enumerate(chips)]
        for cp in sends:
            cp.start()
        passed = []
        for j, (px, py) in enumerate(chips):
            theirs = rows(2 * px + py, c)
            _remote(theirs, theirs, send_sems.at[j], recv_sems.at[j], (px, py, c)).wait_recv()
            cp = _remote(theirs, theirs, send_sems.at[3 + j], recv_sems.at[3 + j], sibling)
            cp.start()
            passed.append(cp)
        for j, (px, py) in enumerate(chips):
            theirs = rows(2 * px + py, 1 - c)
            _remote(theirs, theirs, send_sems.at[3 + j], recv_sems.at[3 + j], sibling).wait_recv()
        for cp in sends + passed:
            cp.wait_send()

    return pl.pallas_call(
        body, name="all_gather_shards", in_specs=[HBM_SPEC], out_specs=HBM_SPEC,
        out_shape=jax.ShapeDtypeStruct((N_CHIPS,) + packed.shape, packed.dtype),
        scratch_shapes=[pltpu.SemaphoreType.DMA((6,)), pltpu.SemaphoreType.DMA((6,))],
    )(packed)


def exchange_core_halves(G):
    half = PACK_HALF

    def body(g_ref, land_ref, send_sem, recv_sem):
        x, y, c = _place()
        cp = _remote(g_ref.at[:, pl.ds((1 - c) * half, half), :], land_ref, send_sem, recv_sem, (x, y, 1 - c))
        cp.start()
        cp.wait()

    return pl.pallas_call(
        body, name="exchange_core_halves", in_specs=[HBM_SPEC], out_specs=HBM_SPEC,
        out_shape=jax.ShapeDtypeStruct((N_CHIPS, half, LANES), G.dtype),
        scratch_shapes=[pltpu.SemaphoreType.DMA(()), pltpu.SemaphoreType.DMA(())],
    )(G)


def add_core_halves(G, land, core):
    nb = PACK_HALF // PACK_BLOCK

    def body(c_ref, g_ref, l_ref, o_ref):
        o_ref[...] = (g_ref[...].astype(f32) + l_ref[...].astype(f32)).astype(o_ref.dtype)

    blk = (1, PACK_BLOCK, LANES)
    return pl.pallas_call(
        body, name="add_core_halves",
        grid_spec=pltpu.PrefetchScalarGridSpec(
            num_scalar_prefetch=1, grid=(N_CHIPS, nb),
            in_specs=[pl.BlockSpec(blk, lambda k, i, c_ref: (k, c_ref[0] * nb + i, 0)),
                      pl.BlockSpec(blk, lambda k, i, c_ref: (k, i, 0))],
            out_specs=pl.BlockSpec(blk, lambda k, i, c_ref: (k, i, 0))),
        out_shape=jax.ShapeDtypeStruct(land.shape, land.dtype),
        compiler_params=_cparams("parallel", "parallel"),
    )(core, G, land)


def scatter_to_chips(part):
    def body(p_ref, land_ref, send_sems, recv_sems):
        x, y, c = _place()
        me_chip = 2 * x + y
        chips = _other_chips(x, y)
        sends = [_remote(p_ref.at[2 * px + py], land_ref.at[me_chip], send_sems.at[j], recv_sems.at[j], (px, py, c))
                 for j, (px, py) in enumerate(chips)]
        for cp in sends:
            cp.start()
        for j, (px, py) in enumerate(chips):
            slot = land_ref.at[2 * px + py]
            _remote(slot, slot, send_sems.at[j], recv_sems.at[j], (px, py, c)).wait_recv()
        for cp in sends:
            cp.wait_send()

    return pl.pallas_call(
        body, name="scatter_to_chips", in_specs=[HBM_SPEC], out_specs=HBM_SPEC,
        out_shape=jax.ShapeDtypeStruct(part.shape, part.dtype),
        scratch_shapes=[pltpu.SemaphoreType.DMA((3,)), pltpu.SemaphoreType.DMA((3,))],
    )(part)


def sum_chips(part, land, order):
    nb = PACK_HALF // PACK_BLOCK

    def body(order_ref, p_ref, l1_ref, l2_ref, l3_ref, o_ref):
        o_ref[...] = ((p_ref[0].astype(f32) + l1_ref[0].astype(f32)) + l2_ref[0].astype(f32)) + l3_ref[0].astype(f32)

    slot = lambda j: pl.BlockSpec((1, PACK_BLOCK, LANES), lambda i, order_ref: (order_ref[j], i, 0))
    return pl.pallas_call(
        body, name="sum_chips",
        grid_spec=pltpu.PrefetchScalarGridSpec(
            num_scalar_prefetch=1, grid=(nb,), in_specs=[slot(0), slot(1), slot(2), slot(3)],
            out_specs=pl.BlockSpec((PACK_BLOCK, LANES), lambda i, order_ref: (i, 0))),
        out_shape=jax.ShapeDtypeStruct((PACK_HALF, LANES), f32),
        compiler_params=_cparams("parallel"),
    )(order, part, land, land, land)


def swap_core_halves(red):
    def body(r_ref, out_ref, send_sem, recv_sem):
        x, y, c = _place()
        cp = _remote(r_ref, out_ref, send_sem, recv_sem, (x, y, 1 - c))
        cp.start()
        cp.wait()

    return pl.pallas_call(
        body, name="swap_core_halves", in_specs=[HBM_SPEC], out_specs=HBM_SPEC,
        out_shape=jax.ShapeDtypeStruct(red.shape, red.dtype),
        scratch_shapes=[pltpu.SemaphoreType.DMA(()), pltpu.SemaphoreType.DMA(())],
    )(red)


def _half(ref, core):
    rows = ref.shape[-2] // 2
    return ref.at[(slice(None),) * (len(ref.shape) - 2) + (pl.ds(core * rows, rows), slice(None))]


def gather_weights(shards, conv):
    n = len(shards)

    def body(*refs):
        src, conv_src = refs[:n], refs[n]
        out, conv_out = refs[n + 1:2 * n + 1], refs[2 * n + 1]
        send_sems, recv_sems = refs[2 * n + 2], refs[2 * n + 3]
        x, y, c = _place()
        me_chip = 2 * x + y
        sibling = (x, y, 1 - c)
        chips = _other_chips(x, y)
        sends = []
        for a in range(n):
            for j, (px, py) in enumerate(chips):
                sends.append(_remote(_half(src[a], c), _half(out[a].at[me_chip], c),
                                     send_sems.at[6 * a + j], recv_sems.at[6 * a + j], (px, py, c)))
        for j, (px, py) in enumerate(chips):
            sends.append(_remote(conv_src, conv_out.at[me_chip], send_sems.at[6 * n + j], recv_sems.at[6 * n + j], (px, py, c)))
        for cp in sends:
            cp.start()
        passed = []
        for a in range(n):
            for j, (px, py) in enumerate(chips):
                theirs = _half(out[a].at[2 * px + py], c)
                _remote(theirs, theirs, send_sems.at[6 * a + j], recv_sems.at[6 * a + j], (px, py, c)).wait_recv()
                cp = _remote(theirs, theirs, send_sems.at[6 * a + 3 + j], recv_sems.at[6 * a + 3 + j], sibling)
                cp.start()
                passed.append(cp)
        for j, (px, py) in enumerate(chips):
            theirs = conv_out.at[2 * px + py]
            _remote(theirs, theirs, send_sems.at[6 * n + j], recv_sems.at[6 * n + j], (px, py, c)).wait_recv()
        for a in range(n):
            for j, (px, py) in enumerate(chips):
                theirs = _half(out[a].at[2 * px + py], 1 - c)
                _remote(theirs, theirs, send_sems.at[6 * a + 3 + j], recv_sems.at[6 * a + 3 + j], sibling).wait_recv()
        for cp in sends + passed:
            cp.wait_send()

    return pl.pallas_call(
        body, name="gather_weights", in_specs=[HBM_SPEC] * (n + 1), out_specs=[HBM_SPEC] * (n + 1),
        out_shape=[jax.ShapeDtypeStruct((N_CHIPS,) + s.shape, s.dtype) for s in list(shards) + [conv]],
        scratch_shapes=[pltpu.SemaphoreType.DMA((6 * n + 3,)), pltpu.SemaphoreType.DMA((6 * n + 3,))],
    )(*shards, conv)


SEM_SPEC = pl.BlockSpec(memory_space=pltpu.SEMAPHORE)
SPLIT_EFFECT = pltpu.SideEffectType.DATAFLOW_SIDE_EFFECTING


def _gather_async_copies(src, land, send_sems, recv_sems, x, y, c):
    me_chip = 2 * x + y
    sends, arrivals = [], []
    for a in range(len(src)):
        for j, (px, py) in enumerate(_other_chips(x, y)):
            for core in range(2):
                sends.append(_remote(_half(src[a], c), _half(land[a].at[me_chip], c), send_sems.at[6 * a + 2 * j + core],
                                     recv_sems.at[6 * a + 2 * j + c], (px, py, core)))
                theirs = _half(land[a].at[2 * px + py], core)
                arrivals.append(_remote(theirs, theirs, send_sems.at[6 * a + 2 * j + core],
                                        recv_sems.at[6 * a + 2 * j + core], (px, py, core)))
    return sends, arrivals


def gather_weights_start(shards, after):
    n = len(shards)

    def body(*refs):
        src, land = refs[:n], refs[n:2 * n]
        send_sems, recv_sems, token = refs[2 * n + 1], refs[2 * n + 2], refs[4 * n + 3]
        x, y, c = _place()
        for cp in _gather_async_copies(src, land, send_sems, recv_sems, x, y, c)[0]:
            cp.start()
        token[...] = jnp.zeros_like(token)

    zones = [pltpu.with_memory_space_constraint(lax.empty((N_CHIPS,) + s.shape, s.dtype), pltpu.HBM) for s in shards]
    srcs = [pltpu.with_memory_space_constraint(s, pltpu.HBM) for s in shards]
    out = pl.pallas_call(
        body, name="gather_weights_start",
        out_shape=[pltpu.SemaphoreType.DMA((6 * n,)), pltpu.SemaphoreType.DMA((6 * n,))]
        + [pltpu.HBM(s.shape, s.dtype) for s in shards] + [pltpu.HBM(z.shape, z.dtype) for z in zones]
        + [jax.ShapeDtypeStruct((8, LANES), f32)],
        in_specs=[HBM_SPEC] * (2 * n) + [pl.BlockSpec(memory_space=pl.ANY)],
        out_specs=[SEM_SPEC, SEM_SPEC] + [HBM_SPEC] * (2 * n) + [pl.BlockSpec(memory_space=pltpu.VMEM)],
        input_output_aliases={i: 2 + i for i in range(2 * n)},
        compiler_params=pltpu.CompilerParams(has_side_effects=SPLIT_EFFECT),
    )(*srcs, *zones, after)
    return out[0], out[1], out[2:2 + n], out[2 + n:2 + 2 * n], out[-1]


def gather_weights_wait(send_sems, recv_sems, shards, zones, after):
    n = len(shards)

    def body(*refs):
        src, land = refs[:n], refs[n:2 * n]
        send_sems, recv_sems = refs[2 * n], refs[2 * n + 1]
        x, y, c = _place()
        sends, arrivals = _gather_async_copies(src, land, send_sems, recv_sems, x, y, c)
        for cp in sends:
            cp.wait_send()
        for cp in arrivals:
            cp.wait_recv()

    out = pl.pallas_call(
        body, name="gather_weights_wait",
        out_shape=[pltpu.HBM(s.shape, s.dtype) for s in shards] + [pltpu.HBM(z.shape, z.dtype) for z in zones],
        in_specs=[HBM_SPEC] * (2 * n) + [SEM_SPEC, SEM_SPEC, pl.BlockSpec(memory_space=pl.ANY)],
        out_specs=[HBM_SPEC] * (2 * n),
        input_output_aliases={i: i for i in range(2 * n)},
        compiler_params=pltpu.CompilerParams(has_side_effects=SPLIT_EFFECT),
    )(*shards, *zones, send_sems, recv_sems, after)
    return out[n:]


def swap_grad_halves(grads, *, name):
    n = len(grads)

    def body(*refs):
        g, land, send_sems, recv_sems = refs[:n], refs[n:2 * n], refs[2 * n], refs[2 * n + 1]
        x, y, c = _place()
        copies = [_remote(_half(g[a], 1 - c), land[a], send_sems.at[a], recv_sems.at[a], (x, y, 1 - c)) for a in range(n)]
        for cp in copies:
            cp.start()
        for cp in copies:
            cp.wait()

    return pl.pallas_call(
        body, name=name, in_specs=[HBM_SPEC] * n, out_specs=[HBM_SPEC] * n,
        out_shape=[jax.ShapeDtypeStruct((N_CHIPS, g.shape[1] // 2, g.shape[2]), g.dtype) for g in grads],
        scratch_shapes=[pltpu.SemaphoreType.DMA((n,)), pltpu.SemaphoreType.DMA((n,))],
    )(*grads)


GRAD_ROWS = 512


def add_grad_halves(g, land, core, *, name):
    _, half, cols = land.shape
    tr = GRAD_ROWS if half % GRAD_ROWS == 0 else half
    nb = half // tr

    def body(c_ref, g_ref, l_ref, o_ref):
        o_ref[...] = (g_ref[...].astype(f32) + l_ref[...].astype(f32)).astype(o_ref.dtype)

    blk = (1, tr, cols)
    return pl.pallas_call(
        body, name=name,
        grid_spec=pltpu.PrefetchScalarGridSpec(
            num_scalar_prefetch=1, grid=(N_CHIPS, nb),
            in_specs=[pl.BlockSpec(blk, lambda k, i, c_ref: (k, c_ref[0] * nb + i, 0)),
                      pl.BlockSpec(blk, lambda k, i, c_ref: (k, i, 0))],
            out_specs=pl.BlockSpec(blk, lambda k, i, c_ref: (k, i, 0))),
        out_shape=jax.ShapeDtypeStruct(land.shape, land.dtype),
        compiler_params=_cparams("parallel", "parallel"),
    )(core, g, land)


def scatter_grads(parts):
    n = len(parts)

    def body(*refs):
        p, land, send_sems, recv_sems = refs[:n], refs[n:2 * n], refs[2 * n], refs[2 * n + 1]
        x, y, c = _place()
        me_chip = 2 * x + y
        chips = _other_chips(x, y)
        sends = [_remote(p[a].at[2 * px + py], land[a].at[me_chip], send_sems.at[3 * a + j], recv_sems.at[3 * a + j], (px, py, c))
                 for a in range(n) for j, (px, py) in enumerate(chips)]
        for cp in sends:
            cp.start()
        for a in range(n):
            for j, (px, py) in enumerate(chips):
                slot = land[a].at[2 * px + py]
                _remote(slot, slot, send_sems.at[3 * a + j], recv_sems.at[3 * a + j], (px, py, c)).wait_recv()
        for cp in sends:
            cp.wait_send()

    return pl.pallas_call(
        body, name="scatter_grads", in_specs=[HBM_SPEC] * n, out_specs=[HBM_SPEC] * n,
        out_shape=[jax.ShapeDtypeStruct(p.shape, p.dtype) for p in parts],
        scratch_shapes=[pltpu.SemaphoreType.DMA((3 * n,)), pltpu.SemaphoreType.DMA((3 * n,))],
    )(*parts)


def _scatter_async_copies(parts, land, send_sems, recv_sems, x, y, c):
    me_chip = 2 * x + y
    sends, arrivals = [], []
    for a in range(len(parts)):
        for j, (px, py) in enumerate(_other_chips(x, y)):
            sems = (send_sems.at[3 * a + j], recv_sems.at[3 * a + j], (px, py, c))
            sends.append(_remote(parts[a].at[2 * px + py], land[a].at[me_chip], *sems))
            slot = land[a].at[2 * px + py]
            arrivals.append(_remote(slot, slot, *sems))
    return sends, arrivals


def scatter_grads_start(parts, *, name):
    n = len(parts)

    def body(*refs):
        p, land = refs[:n], refs[n:2 * n]
        send_sems, recv_sems, token = refs[2 * n], refs[2 * n + 1], refs[4 * n + 2]
        x, y, c = _place()
        for cp in _scatter_async_copies(p, land, send_sems, recv_sems, x, y, c)[0]:
            cp.start()
        token[...] = jnp.zeros_like(token)

    zones = [pltpu.with_memory_space_constraint(lax.empty(p.shape, p.dtype), pltpu.HBM) for p in parts]
    srcs = [pltpu.with_memory_space_constraint(p, pltpu.HBM) for p in parts]
    hbm = [pltpu.HBM(p.shape, p.dtype) for p in parts]
    out = pl.pallas_call(
        body, name=name,
        out_shape=[pltpu.SemaphoreType.DMA((3 * n,)), pltpu.SemaphoreType.DMA((3 * n,))] + hbm + hbm
        + [jax.ShapeDtypeStruct((8, LANES), f32)],
        in_specs=[HBM_SPEC] * (2 * n),
        out_specs=[SEM_SPEC, SEM_SPEC] + [HBM_SPEC] * (2 * n) + [pl.BlockSpec(memory_space=pltpu.VMEM)],
        input_output_aliases={i: 2 + i for i in range(2 * n)},
        compiler_params=pltpu.CompilerParams(has_side_effects=SPLIT_EFFECT),
    )(*srcs, *zones)
    return out[0], out[1], out[2:2 + n], out[2 + n:2 + 2 * n], out[-1]


def scatter_grads_wait(send_sems, recv_sems, parts, zones, after, *, name):
    n = len(parts)

    def body(*refs):
        p, land = refs[:n], refs[n:2 * n]
        x, y, c = _place()
        sends, arrivals = _scatter_async_copies(p, land, refs[2 * n], refs[2 * n + 1], x, y, c)
        for cp in sends:
            cp.wait_send()
        for cp in arrivals:
            cp.wait_recv()

    hbm = [pltpu.HBM(p.shape, p.dtype) for p in parts]
    out = pl.pallas_call(
        body, name=name, out_shape=hbm + hbm,
        in_specs=[HBM_SPEC] * (2 * n) + [SEM_SPEC, SEM_SPEC, pl.BlockSpec(memory_space=pl.ANY)],
        out_specs=[HBM_SPEC] * (2 * n),
        input_output_aliases={i: i for i in range(2 * n)},
        compiler_params=pltpu.CompilerParams(has_side_effects=SPLIT_EFFECT),
    )(*parts, *zones, send_sems, recv_sems, after)
    return out[:n], out[n:]


def sum_grads(part, land, order, *, name):
    _, half, cols = part.shape
    tr = GRAD_ROWS if half % GRAD_ROWS == 0 else half

    def body(order_ref, p_ref, l1_ref, l2_ref, l3_ref, o_ref):
        o_ref[...] = ((p_ref[0].astype(f32) + l1_ref[0].astype(f32)) + l2_ref[0].astype(f32)) + l3_ref[0].astype(f32)

    slot = lambda j: pl.BlockSpec((1, tr, cols), lambda i, order_ref: (order_ref[j], i, 0))
    return pl.pallas_call(
        body, name=name,
        grid_spec=pltpu.PrefetchScalarGridSpec(
            num_scalar_prefetch=1, grid=(half // tr,), in_specs=[slot(0), slot(1), slot(2), slot(3)],
            out_specs=pl.BlockSpec((tr, cols), lambda i, order_ref: (i, 0))),
        out_shape=jax.ShapeDtypeStruct((half, cols), f32),
        compiler_params=_cparams("parallel"),
    )(order, part, land, land, land)


def _peer(x, y, c, r):
    return ((1 - x) if r & 4 else x, (1 - y) if r & 2 else y, (1 - c) if r & 1 else c)


def _reduce_async_copies(grads, land, send_sems, recv_sems, x, y, c):
    me = 4 * x + 2 * y + c
    sends, arrivals = [], []
    for a in range(len(grads)):
        for r in range(1, N_DEV):
            px, py, pc = _peer(x, y, c, r)
            sems = (send_sems.at[7 * a + r - 1], recv_sems.at[7 * a + r - 1], (px, py, pc))
            sends.append(_remote(_half(grads[a].at[2 * px + py], pc), land[a].at[me], *sems))
            slot = land[a].at[4 * px + 2 * py + pc]
            arrivals.append(_remote(slot, slot, *sems))
    return sends, arrivals


def reduce_grads_start(grads, *, name):
    n = len(grads)

    def body(*refs):
        g, land = refs[:n], refs[n:2 * n]
        send_sems, recv_sems, token = refs[2 * n], refs[2 * n + 1], refs[4 * n + 2]
        x, y, c = _place()
        for cp in _reduce_async_copies(g, land, send_sems, recv_sems, x, y, c)[0]:
            cp.start()
        token[...] = jnp.zeros_like(token)

    zones = [pltpu.with_memory_space_constraint(lax.empty((N_DEV, g.shape[1] // 2, g.shape[2]), g.dtype), pltpu.HBM)
             for g in grads]
    srcs = [pltpu.with_memory_space_constraint(g, pltpu.HBM) for g in grads]
    out = pl.pallas_call(
        body, name=name,
        out_shape=[pltpu.SemaphoreType.DMA((7 * n,)), pltpu.SemaphoreType.DMA((7 * n,))]
        + [pltpu.HBM(g.shape, g.dtype) for g in grads] + [pltpu.HBM(z.shape, z.dtype) for z in zones]
        + [jax.ShapeDtypeStruct((8, LANES), f32)],
        in_specs=[HBM_SPEC] * (2 * n),
        out_specs=[SEM_SPEC, SEM_SPEC] + [HBM_SPEC] * (2 * n) + [pl.BlockSpec(memory_space=pltpu.VMEM)],
        input_output_aliases={i: 2 + i for i in range(2 * n)},
        compiler_params=pltpu.CompilerParams(has_side_effects=SPLIT_EFFECT),
    )(*srcs, *zones)
    return out[0], out[1], out[2:2 + n], out[2 + n:2 + 2 * n], out[-1]


def reduce_grads_wait(send_sems, recv_sems, grads, zones, after, *, name):
    n = len(grads)

    def body(*refs):
        g, land = refs[:n], refs[n:2 * n]
        x, y, c = _place()
        sends, arrivals = _reduce_async_copies(g, land, refs[2 * n], refs[2 * n + 1], x, y, c)
        for cp in sends:
            cp.wait_send()
        for cp in arrivals:
            cp.wait_recv()

    hbm = [pltpu.HBM(a.shape, a.dtype) for a in list(grads) + list(zones)]
    out = pl.pallas_call(
        body, name=name, out_shape=hbm,
        in_specs=[HBM_SPEC] * (2 * n) + [SEM_SPEC, SEM_SPEC, pl.BlockSpec(memory_space=pl.ANY)],
        out_specs=[HBM_SPEC] * (2 * n),
        input_output_aliases={i: i for i in range(2 * n)},
        compiler_params=pltpu.CompilerParams(has_side_effects=SPLIT_EFFECT),
    )(*grads, *zones, send_sems, recv_sems, after)
    return out[:n], out[n:]


def sum_partials(g, land, where, *, name):
    _, half, cols = land.shape
    tr = GRAD_ROWS if half % GRAD_ROWS == 0 else half
    nb = half // tr

    def body(where_ref, g_ref, *rest):
        o_ref = rest[-1]
        acc = g_ref[0].astype(f32)
        for l_ref in rest[:-1]:
            acc = acc + l_ref[0].astype(f32)
        o_ref[...] = acc

    blk = (1, tr, cols)
    slot = lambda j: pl.BlockSpec(blk, lambda i, where_ref: (where_ref[2 + j], i, 0))
    return pl.pallas_call(
        body, name=name,
        grid_spec=pltpu.PrefetchScalarGridSpec(
            num_scalar_prefetch=1, grid=(nb,),
            in_specs=[pl.BlockSpec(blk, lambda i, where_ref: (where_ref[0], where_ref[1] * nb + i, 0))]
            + [slot(j) for j in range(N_DEV - 1)],
            out_specs=pl.BlockSpec((tr, cols), lambda i, where_ref: (i, 0))),
        out_shape=jax.ShapeDtypeStruct((half, cols), f32),
        compiler_params=_cparams("parallel"),
    )(where, g, *([land] * (N_DEV - 1)))


def swap_reduced_halves(mine, *, name):
    n = len(mine)

    def body(*refs):
        r, out, send_sems, recv_sems = refs[:n], refs[n:2 * n], refs[2 * n], refs[2 * n + 1]
        x, y, c = _place()
        copies = [_remote(r[a], out[a], send_sems.at[a], recv_sems.at[a], (x, y, 1 - c)) for a in range(n)]
        for cp in copies:
            cp.start()
        for cp in copies:
            cp.wait()

    return pl.pallas_call(
        body, name=name, in_specs=[HBM_SPEC] * n, out_specs=[HBM_SPEC] * n,
        out_shape=[jax.ShapeDtypeStruct(r.shape, r.dtype) for r in mine],
        scratch_shapes=[pltpu.SemaphoreType.DMA((n,)), pltpu.SemaphoreType.DMA((n,))],
    )(*mine)


def adamw_halves(w, mine, theirs, m, v, core, *, name):
    R, C = w.shape
    tr = min(GRAD_ROWS, R // 2)
    half_nb = R // 2 // tr

    def body(c_ref, w_ref, a_ref, b_ref, m_ref, v_ref, g_ref, d_ref, nm_ref, nv_ref):
        low = pl.program_id(0) < half_nb
        gv = jnp.where(low == (c_ref[0] == 0), a_ref[...], b_ref[...])
        nm = ADAM_B1 * m_ref[...] + (1.0 - ADAM_B1) * gv
        nv = ADAM_B2 * v_ref[...] + (1.0 - ADAM_B2) * jnp.square(gv)
        m_hat = nm / (1.0 - ADAM_B1 ** ADAM_STEP)
        v_hat = nv / (1.0 - ADAM_B2 ** ADAM_STEP)
        g_ref[...] = gv
        d_ref[...] = -ADAM_LR * (m_hat / (jnp.sqrt(v_hat) + ADAM_EPS) + ADAM_WD * w_ref[...])
        nm_ref[...] = nm
        nv_ref[...] = nv

    full = pl.BlockSpec((tr, C), lambda i, c_ref: (i, 0))
    part = pl.BlockSpec((tr, C), lambda i, c_ref: (i % half_nb, 0))
    out = jax.ShapeDtypeStruct((R, C), f32)
    return pl.pallas_call(
        body, name=name,
        grid_spec=pltpu.PrefetchScalarGridSpec(
            num_scalar_prefetch=1, grid=(2 * half_nb,), in_specs=[full, part, part, full, full], out_specs=[full] * 4),
        out_shape=[out] * 4, compiler_params=_cparams("parallel"),
    )(core, w, mine, theirs, m, v)


N_DEV = 8


def all_reduce_small(v):
    def body(src_ref, out_ref, land_ref, send_sems, recv_sems):
        x, y, c = _place()
        me = 4 * x + 2 * y + c
        copies = []
        for r in range(1, N_DEV):
            peer = ((1 - x) if r & 4 else x, (1 - y) if r & 2 else y, (1 - c) if r & 1 else c)
            copies.append(_remote(src_ref, land_ref.at[r], send_sems.at[r - 1], recv_sems.at[r - 1], peer))
        for cp in copies:
            cp.start()
        land_ref[0] = src_ref[...]
        for cp in copies:
            cp.wait()
        acc = land_ref[me]
        for d in range(1, N_DEV):
            acc = acc + land_ref[jnp.bitwise_xor(me, d)]
        out_ref[...] = acc

    vm = pl.BlockSpec(memory_space=pltpu.VMEM)
    return pl.pallas_call(
        body, name="all_reduce_small", in_specs=[vm], out_specs=vm,
        out_shape=jax.ShapeDtypeStruct(v.shape, v.dtype),
        scratch_shapes=[pltpu.VMEM((N_DEV,) + v.shape, v.dtype),
                        pltpu.SemaphoreType.DMA((N_DEV - 1,)), pltpu.SemaphoreType.DMA((N_DEV - 1,))],
    )(v)


def adamw(w, g, m, v, *, name, tr=None, tc=None):
    R, C = w.shape
    if tc is None:
        tr, tc = min(tr, R), C
        blk = pl.BlockSpec((tr, C), lambda i: (i, 0))
    else:
        tr = R
        blk = pl.BlockSpec((R, tc), lambda i: (0, i))

    def body(w_ref, g_ref, m_ref, v_ref, d_ref, nm_ref, nv_ref):
        gv = g_ref[...]
        nm = ADAM_B1 * m_ref[...] + (1.0 - ADAM_B1) * gv
        nv = ADAM_B2 * v_ref[...] + (1.0 - ADAM_B2) * jnp.square(gv)
        m_hat = nm / (1.0 - ADAM_B1 ** ADAM_STEP)
        v_hat = nv / (1.0 - ADAM_B2 ** ADAM_STEP)
        d_ref[...] = -ADAM_LR * (m_hat / (jnp.sqrt(v_hat) + ADAM_EPS) + ADAM_WD * w_ref[...])
        nm_ref[...] = nm
        nv_ref[...] = nv

    out = jax.ShapeDtypeStruct((R, C), f32)
    return pl.pallas_call(
        body, name=name, grid=((R // tr) * (C // tc),), in_specs=[blk] * 4, out_specs=[blk] * 3, out_shape=[out] * 3,
        compiler_params=_cparams("parallel"),
    )(w, g, m, v)


BIG_SHARDS = (("w_in", (1024, 900), True), ("w_out", (256, 1024), False), ("w_cq", (256, 512), False),
              ("w_ckv", (256, 1024), False), ("w_co", (512, 256), True), ("w_mlp1", (1024, 1024), True),
              ("w_mlp2", (1024, 1024), False))
CONV_SHARD = (CONV_WIDTH, 3 * GDN_WIDTH // N_CHIPS)
SMALL_DIMS = (("norm_mix_g", 1024), ("fox_qnorm_g", 64), ("fox_knorm_g", 64), ("fox_f_bias", 8), ("fox_onorm_g", 64),
              ("gdn_A_log", 4), ("gdn_dt_bias", 4), ("gdn_onorm_g", 128), ("norm_xattn_g", 1024), ("mem_norm_g", 1024),
              ("xattn_qnorm_g", 128), ("xattn_knorm_g", 128), ("norm_mlp_g", 1024))
WEIGHT_ORDER = ("norm_mix_g", "w_in", "fox_qnorm_g", "fox_knorm_g", "fox_f_bias", "fox_onorm_g", "gdn_conv_w", "gdn_A_log",
                "gdn_dt_bias", "gdn_onorm_g", "w_out", "norm_xattn_g", "mem_norm_g", "w_cq", "w_ckv", "xattn_qnorm_g",
                "xattn_knorm_g", "w_co", "norm_mlp_g", "w_mlp1", "w_mlp2")


def _pack_rows(pieces, rows, lead=()):
    cat = jnp.concatenate([p.reshape(lead + (-1,)) for p in pieces], axis=-1)
    cat = jnp.pad(cat, [(0, 0)] * len(lead) + [(0, rows * LANES - cat.shape[-1])])
    return cat.reshape(lead + (rows, LANES))


def _unpack_rows(buf, sizes, lead=()):
    flat = buf.reshape(lead + (-1,))
    out, off = [], 0
    for n in sizes:
        out.append(flat[..., off:off + n])
        off += n
    return out


def _conv_to_wire(conv):
    return lax.bitcast_convert_type(conv, bf16)


def _conv_from_wire(wire):
    return lax.bitcast_convert_type(wire, f32)


SMALL_ROWS = 96
SMALL_ADAM_ROWS = 56


def kernel(x, mem, norm_mix_g, w_in, fox_qnorm_g, fox_knorm_g, fox_f_bias, fox_onorm_g, gdn_conv_w, gdn_A_log, gdn_dt_bias, gdn_onorm_g, w_out, norm_xattn_g, mem_norm_g, w_cq, w_ckv, xattn_qnorm_g, xattn_knorm_g, w_co, norm_mlp_g, w_mlp1, w_mlp2, loss_target, m_norm_mix_g, m_w_in, m_fox_qnorm_g, m_fox_knorm_g, m_fox_f_bias, m_fox_onorm_g, m_gdn_conv_w, m_gdn_A_log, m_gdn_dt_bias, m_gdn_onorm_g, m_w_out, m_norm_xattn_g, m_mem_norm_g, m_w_cq, m_w_ckv, m_xattn_qnorm_g, m_xattn_knorm_g, m_w_co, m_norm_mlp_g, m_w_mlp1, m_w_mlp2, v_norm_mix_g, v_w_in, v_fox_qnorm_g, v_fox_knorm_g, v_fox_f_bias, v_fox_onorm_g, v_gdn_conv_w, v_gdn_A_log, v_gdn_dt_bias, v_gdn_onorm_g, v_w_out, v_norm_xattn_g, v_mem_norm_g, v_w_cq, v_w_ckv, v_xattn_qnorm_g, v_xattn_knorm_g, v_w_co, v_norm_mlp_g, v_w_mlp1, v_w_mlp2):
    wts = dict(norm_mix_g=norm_mix_g, w_in=w_in, fox_qnorm_g=fox_qnorm_g, fox_knorm_g=fox_knorm_g, fox_f_bias=fox_f_bias,
               fox_onorm_g=fox_onorm_g, gdn_conv_w=gdn_conv_w, gdn_A_log=gdn_A_log, gdn_dt_bias=gdn_dt_bias,
               gdn_onorm_g=gdn_onorm_g, w_out=w_out, norm_xattn_g=norm_xattn_g, mem_norm_g=mem_norm_g, w_cq=w_cq, w_ckv=w_ckv,
               xattn_qnorm_g=xattn_qnorm_g, xattn_knorm_g=xattn_knorm_g, w_co=w_co, norm_mlp_g=norm_mlp_g, w_mlp1=w_mlp1,
               w_mlp2=w_mlp2)
    mom = dict(norm_mix_g=m_norm_mix_g, w_in=m_w_in, fox_qnorm_g=m_fox_qnorm_g, fox_knorm_g=m_fox_knorm_g,
               fox_f_bias=m_fox_f_bias, fox_onorm_g=m_fox_onorm_g, gdn_conv_w=m_gdn_conv_w, gdn_A_log=m_gdn_A_log,
               gdn_dt_bias=m_gdn_dt_bias, gdn_onorm_g=m_gdn_onorm_g, w_out=m_w_out, norm_xattn_g=m_norm_xattn_g,
               mem_norm_g=m_mem_norm_g, w_cq=m_w_cq, w_ckv=m_w_ckv, xattn_qnorm_g=m_xattn_qnorm_g,
               xattn_knorm_g=m_xattn_knorm_g, w_co=m_w_co, norm_mlp_g=m_norm_mlp_g, w_mlp1=m_w_mlp1, w_mlp2=m_w_mlp2)
    var = dict(norm_mix_g=v_norm_mix_g, w_in=v_w_in, fox_qnorm_g=v_fox_qnorm_g, fox_knorm_g=v_fox_knorm_g,
               fox_f_bias=v_fox_f_bias, fox_onorm_g=v_fox_onorm_g, gdn_conv_w=v_gdn_conv_w, gdn_A_log=v_gdn_A_log,
               gdn_dt_bias=v_gdn_dt_bias, gdn_onorm_g=v_gdn_onorm_g, w_out=v_w_out, norm_xattn_g=v_norm_xattn_g,
               mem_norm_g=v_mem_norm_g, w_cq=v_w_cq, w_ckv=v_w_ckv, xattn_qnorm_g=v_xattn_qnorm_g,
               xattn_knorm_g=v_xattn_knorm_g, w_co=v_w_co, norm_mlp_g=v_norm_mlp_g, w_mlp1=v_w_mlp1, w_mlp2=v_w_mlp2)
    B, S, D = x.shape
    T = B * S
    big_names = [n for n, _, _ in BIG_SHARDS]
    chip = 2 * lax.axis_index("x") + lax.axis_index("y")
    core = lax.axis_index("c").astype(jnp.int32).reshape(1)

    shards = {n: wts[n][0].astype(MXU_DTYPE) for n in big_names[1:]}
    in_t = lambda p: jnp.swapaxes(p[0], 0, 1)
    shards["w_in"] = jnp.pad(in_t(w_in).astype(MXU_DTYPE), ((0, IN_SHARD_PAD - IN_SHARD), (0, 0)))
    w_in_all, conv_all = gather_weights([shards["w_in"]], gdn_conv_w[0])
    late = big_names[1:]
    send_sems, recv_sems, late_src, late_zones, token = gather_weights_start([shards[n] for n in late], conv_all)
    own = lambda g, s: lax.dynamic_update_slice(g, s[None], (chip,) + (0,) * s.ndim)
    full = {"w_in": own(w_in_all, shards["w_in"])}
    conv_full = own(conv_all, gdn_conv_w[0]).transpose(1, 0, 2).reshape(CONV_WIDTH, 3 * GDN_WIDTH)
    rows = lambda g: g.reshape(N_CHIPS * g.shape[1], g.shape[2])

    def late_weights(after):
        zones = gather_weights_wait(send_sems, recv_sems, late_src, late_zones, after)
        got = {n: own(z, shards[n]) for n, z in zip(late, zones)}
        return dict(w_out=rows(got["w_out"]), w_cq=rows(got["w_cq"]), w_ckv=rows(got["w_ckv"]),
                    w_co=got["w_co"].transpose(1, 0, 2).reshape(XATTN_WIDTH, D_MODEL),
                    w_mlp1=got["w_mlp1"], w_mlp2=rows(got["w_mlp2"]))

    in_flight = []

    def grads_ready(ready):
        names = list(ready)
        *started, tok = reduce_grads_start([ready[n] for n in names], name="reduce_grads_start_%d" % len(in_flight))
        in_flight.append((names, *started))
        return tok

    w_in_t = full["w_in"][:, :IN_SHARD].reshape(IN_DIM, D_MODEL)
    w = dict(wa_t=align_w_in_t(w_in_t), conv_w=conv_full, late=late_weights, grads_ready=grads_ready)
    sp = {n: wts[n] for n, _ in SMALL_DIMS}
    sp["norm_mix_g"] = sp["norm_mix_g"] + token[0, 0]

    loss_part, grad_x, g_big, g_small = local_step(x.reshape(T, D), mem.reshape(-1, D), loss_target.reshape(T, D), w, sp, B=B)

    small_pieces = [g_small[n] for n, _ in SMALL_DIMS] + [g_small["gdn_conv_w"], loss_part]
    small_sizes = [d for _, d in SMALL_DIMS] + [CONV_WIDTH * 3 * GDN_WIDTH, LANES]
    red_small = _unpack_rows(all_reduce_small(_pack_rows(small_pieces, SMALL_ROWS)), small_sizes)
    grads = {n: p.reshape(1, d) for (n, d), p in zip(SMALL_DIMS, red_small)}
    conv_grad = lax.dynamic_slice(red_small[-2].reshape(CONV_WIDTH, 3 * GDN_WIDTH), (0, chip * CONV_SHARD[1]), CONV_SHARD)
    grads["gdn_conv_w"] = conv_grad.reshape((1,) + CONV_SHARD)
    loss = red_small[-1][0]

    parts, zones = {}, {}

    def wait_group(k, after):
        names, send_sems, recv_sems, thru, land = in_flight[k]
        thru, land = reduce_grads_wait(send_sems, recv_sems, thru, land, after, name="reduce_grads_wait_%d" % k)
        parts.update(zip(names, thru))
        zones.update(zip(names, land))

    wait_group(0, grad_x)
    wait_group(1, grad_x)
    dev = 2 * chip + core[0]
    where = jnp.stack([chip, core[0]] + [dev ^ r for r in range(1, N_DEV)]).astype(jnp.int32)
    mine = [sum_partials(parts[n], zones[n], where, name="sum_partials_" + n) for n in late]
    theirs = swap_reduced_halves(mine, name="swap_reduced_halves")

    delta, new_m, new_v = {}, {}, {}
    for n, a, b in zip(late, mine, theirs):
        g, d, nm, nv = adamw_halves(wts[n][0], a, b, mom[n][0], var[n][0], core, name="adamw_" + n)
        grads[n], delta[n], new_m[n], new_v[n] = g[None], d[None], nm[None], nv[None]
    wait_group(2, new_v[late[-1]])
    mine_in = sum_partials(parts["w_in"], zones["w_in"], where, name="sum_partials_w_in")
    (theirs_in,) = swap_reduced_halves([mine_in], name="swap_reduced_halves_w_in")
    south = core[0] == 0
    g_in_t = jnp.concatenate([jnp.where(south, mine_in, theirs_in), jnp.where(south, theirs_in, mine_in)])[:IN_SHARD]
    back = lambda t: jnp.swapaxes(t, 0, 1)[None]
    d, nm, nv = adamw(in_t(w_in), g_in_t, in_t(m_w_in), in_t(v_w_in), name="adamw_w_in", tc=256)
    grads["w_in"], delta["w_in"], new_m["w_in"], new_v["w_in"] = back(g_in_t), back(d), back(nm), back(nv)
    small_names = [n for n, _ in SMALL_DIMS] + ["gdn_conv_w"]
    small_sz = [d for _, d in SMALL_DIMS] + [CONV_SHARD[0] * CONV_SHARD[1]]
    packed4 = [_pack_rows([src[n] for n in small_names], SMALL_ADAM_ROWS) for src in (wts, grads, mom, var)]
    outs = adamw(*packed4, name="adamw_small", tr=SMALL_ADAM_ROWS)
    for dst, buf in zip((delta, new_m, new_v), outs):
        for n, p in zip(small_names, _unpack_rows(buf, small_sz)):
            dst[n] = p.reshape(wts[n].shape)

    return (loss, grad_x.reshape(B, S, D), *[grads[n] for n in WEIGHT_ORDER], *[delta[n] for n in WEIGHT_ORDER],
            *[new_m[n] for n in WEIGHT_ORDER], *[new_v[n] for n in WEIGHT_ORDER])
```
